```python
import jax, jax.numpy as jnp
from jax import lax
import numpy as np

D_MODEL = 1024
BATCH = 8
SEQ = 4096
DEPTH = 2

N_BRANCH = 3
POOL_WINDOWS = (2, 4, 8, 16)
POOL_WIDTH = D_MODEL // 2
POOL_GROUP = POOL_WIDTH // len(POOL_WINDOWS)
CONV_WIDTH = D_MODEL // 2
CONV_K = 3
HEAD_DIM = 64
ATTN_GROUPS = ((128, 1), (512, 4), (2048, 16))
HEADS_PER_GROUP = 4
N_HEADS = HEADS_PER_GROUP * len(ATTN_GROUPS)
ATTN_WIDTH = N_HEADS * HEAD_DIM
ATTN_OUT = HEADS_PER_GROUP * HEAD_DIM
ATTN_BLOCK = 128
D_FF = 4 * D_MODEL
EPS = 1e-6
MASK_VALUE = -1e30

OFF_POOL = 0
OFF_CONV_B = OFF_POOL + POOL_WIDTH
OFF_CONV_C = OFF_CONV_B + CONV_WIDTH
OFF_CONV_X = OFF_CONV_C + CONV_WIDTH
OFF_Q = OFF_CONV_X + CONV_WIDTH
OFF_K = OFF_Q + ATTN_WIDTH
OFF_V = OFF_K + ATTN_WIDTH
OFF_GATE = OFF_V + ATTN_WIDTH
IN_COLS = OFF_GATE + N_BRANCH * D_MODEL

kernel_name = "hybrid_pool_conv_dilated_attn_block"


def rms_norm(x, gain):
    xf = x.astype(jnp.float32)
    y = xf * lax.rsqrt(jnp.mean(xf * xf, axis=-1, keepdims=True) + EPS)
    return (y * gain.astype(jnp.float32)).astype(x.dtype)


def multiscale_pool(u, w_mix, scale):
    B, S, _ = u.shape
    cs = jnp.cumsum(u.astype(jnp.float32), axis=1)
    pos = jnp.arange(S)
    diffs = []
    for g, w in enumerate(POOL_WINDOWS):
        c = cs[..., g * POOL_GROUP:(g + 1) * POOL_GROUP]
        lag = jnp.pad(c, ((0, 0), (w, 0), (0, 0)))[:, :S]
        count = jnp.minimum(pos + 1, w).astype(jnp.float32)[None, :, None]
        diffs.append((c - lag) / count - u[..., g * POOL_GROUP:(g + 1) * POOL_GROUP].astype(jnp.float32))
    d = jnp.stack(diffs, axis=2).astype(u.dtype)
    y = jnp.einsum('bsgc,gcd->bsgd', d, w_mix).reshape(B, S, POOL_WIDTH)
    return y * scale


def short_gated_conv(b_gate, c_gate, xin, conv_w):
    S = xin.shape[1]
    u = c_gate * xin
    up = jnp.pad(u, ((0, 0), (CONV_K - 1, 0), (0, 0)))
    y = conv_w[CONV_K - 1] * up[:, CONV_K - 1:CONV_K - 1 + S]
    for j in range(CONV_K - 1):
        y = y + conv_w[j] * up[:, j:j + S]
    return b_gate * y


def dilated_window_attention(q, k, v, window, dilation):
    B, S, H, Dh = q.shape
    span = window // dilation
    assert span <= ATTN_BLOCK
    L = S // dilation
    nb = -(-L // ATTN_BLOCK)
    Lp = nb * ATTN_BLOCK

    def fold(t):
        t = t.reshape(B, L, dilation, H, Dh).transpose(0, 2, 3, 1, 4)
        t = jnp.pad(t, ((0, 0), (0, 0), (0, 0), (0, Lp - L), (0, 0)))
        return t.reshape(B, dilation, H, nb, ATTN_BLOCK, Dh)

    def with_prev(t):
        prev = jnp.pad(t, ((0, 0), (0, 0), (0, 0), (1, 0), (0, 0), (0, 0)))[:, :, :, :nb]
        return jnp.concatenate([prev, t], axis=4)

    qb = fold(q).astype(jnp.float32)
    kc = with_prev(fold(k)).astype(jnp.float32)
    vc = with_prev(fold(v)).astype(jnp.float32)
    s = jnp.einsum('bdhnqc,bdhnkc->bdhnqk', qb, kc) * (Dh ** -0.5)
    qi = jnp.arange(ATTN_BLOCK)[:, None]
    ki = jnp.arange(2 * ATTN_BLOCK)[None, :] - ATTN_BLOCK
    rel = qi - ki
    band = (rel >= 0) & (rel <= span)
    has_prev = (jnp.arange(nb) > 0)[:, None, None] | (ki >= 0)[None]
    mask = band[None] & has_prev
    s = jnp.where(mask, s, MASK_VALUE)
    m = jnp.max(s, axis=-1, keepdims=True)
    p = jnp.exp(s - m)
    den = jnp.sum(p, axis=-1, keepdims=True)
    o = jnp.einsum('bdhnqk,bdhnkc->bdhnqc', p, vc) / den
    lse = (m + jnp.log(den))[..., 0]
    o = o.reshape(B, dilation, H, Lp, Dh)[:, :, :, :L].transpose(0, 3, 1, 2, 4).reshape(B, S, H, Dh)
    lse = lse.reshape(B, dilation, H, Lp)[..., :L].transpose(0, 3, 1, 2).reshape(B, S, H)
    return o.astype(q.dtype), lse


def dilated_mixture_attention(q, k, v):
    B, S = q.shape[:2]
    outs, lses = [], []
    for g, (window, dilation) in enumerate(ATTN_GROUPS):
        hs = slice(g * HEADS_PER_GROUP, (g + 1) * HEADS_PER_GROUP)
        o, lse = dilated_window_attention(q[:, :, hs], k[:, :, hs], v[:, :, hs], window, dilation)
        outs.append(o)
        lses.append(lse)
    wts = jax.nn.softmax(jnp.stack(lses, axis=0), axis=0)
    o = jnp.sum(wts[..., None] * jnp.stack(outs, axis=0).astype(jnp.float32), axis=0)
    return o.reshape(B, S, ATTN_OUT).astype(q.dtype)


def hybrid_layer(x, norm_mix, w_in, b_gate, pool_mix, pool_scale, conv_w, q_gain, k_gain,
                 w_pool_up, w_conv_out, w_attn_up, w_o, norm_mlp, w_ff1, w_ff2):
    B, S, D = x.shape
    h = rms_norm(x, norm_mix)
    z = jnp.einsum('bsd,dc->bsc', h, w_in)
    y_pool = multiscale_pool(z[..., OFF_POOL:OFF_CONV_B], pool_mix, pool_scale)
    y_conv = short_gated_conv(z[..., OFF_CONV_B:OFF_CONV_C], z[..., OFF_CONV_C:OFF_CONV_X],
                              z[..., OFF_CONV_X:OFF_Q], conv_w)
    q = rms_norm(z[..., OFF_Q:OFF_K].reshape(B, S, N_HEADS, HEAD_DIM), q_gain)
    k = rms_norm(z[..., OFF_K:OFF_V].reshape(B, S, N_HEADS, HEAD_DIM), k_gain)
    v = z[..., OFF_V:OFF_GATE].reshape(B, S, N_HEADS, HEAD_DIM)
    y_attn = dilated_mixture_attention(q, k, v)
    gates = jax.nn.sigmoid((z[..., OFF_GATE:] + b_gate).astype(jnp.float32)).astype(x.dtype)
    gates = gates.reshape(B, S, N_BRANCH, D)
    merged = (gates[:, :, 0] * (y_pool @ w_pool_up)
              + gates[:, :, 1] * (y_conv @ w_conv_out)
              + gates[:, :, 2] * (y_attn @ w_attn_up))
    x = x + merged @ w_o
    h2 = rms_norm(x, norm_mlp)
    x = x + jnp.square(jax.nn.relu(h2 @ w_ff1)) @ w_ff2
    return x


def _fwd_setup_inputs(seed: int = 0) -> dict:
    key = jax.random.key(seed)
    ks = jax.random.split(key, 17)
    L, D = DEPTH, D_MODEL

    def nrm(k, shape, fan_in):
        return jax.random.normal(k, shape, jnp.float32) * (fan_in ** -0.5)

    def gain(k, shape):
        return 1.0 + 0.02 * jax.random.normal(k, shape, jnp.float32)

    return {
        "x": jax.random.normal(ks[0], (BATCH, SEQ, D), jnp.float32),
        "norm_mix": gain(ks[1], (L, D)),
        "w_in": nrm(ks[2], (L, D, IN_COLS), D),
        "b_gate": 0.01 * jax.random.normal(ks[3], (L, N_BRANCH * D), jnp.float32),
        "pool_mix": nrm(ks[4], (L, len(POOL_WINDOWS), POOL_GROUP, POOL_GROUP), POOL_GROUP),
        "pool_scale": gain(ks[5], (L, POOL_WIDTH)),
        "conv_w": nrm(ks[6], (L, CONV_K, CONV_WIDTH), CONV_K),
        "q_gain": gain(ks[7], (L, HEAD_DIM)),
        "k_gain": gain(ks[8], (L, HEAD_DIM)),
        "w_pool_up": nrm(ks[9], (L, POOL_WIDTH, D), POOL_WIDTH),
        "w_conv_out": nrm(ks[10], (L, CONV_WIDTH, D), CONV_WIDTH),
        "w_attn_up": nrm(ks[11], (L, ATTN_OUT, D), ATTN_OUT),
        "w_o": nrm(ks[12], (L, D, D), D),
        "norm_mlp": gain(ks[13], (L, D)),
        "w_ff1": nrm(ks[14], (L, D, D_FF), D),
        "w_ff2": nrm(ks[15], (L, D_FF, D), D_FF),
    }


def _fwd_reference(x, norm_mix, w_in, b_gate, pool_mix, pool_scale, conv_w, q_gain, k_gain,
              w_pool_up, w_conv_out, w_attn_up, w_o, norm_mlp, w_ff1, w_ff2):
    for l in range(DEPTH):
        x = hybrid_layer(x, norm_mix[l], w_in[l], b_gate[l], pool_mix[l], pool_scale[l], conv_w[l],
                         q_gain[l], k_gain[l], w_pool_up[l], w_conv_out[l], w_attn_up[l], w_o[l],
                         norm_mlp[l], w_ff1[l], w_ff2[l])
    return x


import jax as _jax
import jax.numpy as _jnp

TWIN_FORMAT = 'train_step'
FWD_PARAMS = ['x', 'norm_mix', 'w_in', 'b_gate', 'pool_mix', 'pool_scale', 'conv_w', 'q_gain', 'k_gain', 'w_pool_up', 'w_conv_out', 'w_attn_up', 'w_o', 'norm_mlp', 'w_ff1', 'w_ff2']
TWIN_WEIGHTS = ['norm_mix', 'w_in', 'b_gate', 'pool_mix', 'pool_scale', 'conv_w', 'q_gain', 'k_gain', 'w_pool_up', 'w_conv_out', 'w_attn_up', 'w_o', 'norm_mlp', 'w_ff1', 'w_ff2']
TWIN_DIFF_INPUT = 'x'
TWIN_INPUTS = ['x', 'norm_mix', 'w_in', 'b_gate', 'pool_mix', 'pool_scale', 'conv_w', 'q_gain', 'k_gain', 'w_pool_up', 'w_conv_out', 'w_attn_up', 'w_o', 'norm_mlp', 'w_ff1', 'w_ff2', 'loss_target', 'm_norm_mix', 'm_w_in', 'm_b_gate', 'm_pool_mix', 'm_pool_scale', 'm_conv_w', 'm_q_gain', 'm_k_gain', 'm_w_pool_up', 'm_w_conv_out', 'm_w_attn_up', 'm_w_o', 'm_norm_mlp', 'm_w_ff1', 'm_w_ff2', 'v_norm_mix', 'v_w_in', 'v_b_gate', 'v_pool_mix', 'v_pool_scale', 'v_conv_w', 'v_q_gain', 'v_k_gain', 'v_w_pool_up', 'v_w_conv_out', 'v_w_attn_up', 'v_w_o', 'v_norm_mlp', 'v_w_ff1', 'v_w_ff2']
TWIN_OUTPUTS = ['loss', 'grad_x', 'grad_norm_mix', 'grad_w_in', 'grad_b_gate', 'grad_pool_mix', 'grad_pool_scale', 'grad_conv_w', 'grad_q_gain', 'grad_k_gain', 'grad_w_pool_up', 'grad_w_conv_out', 'grad_w_attn_up', 'grad_w_o', 'grad_norm_mlp', 'grad_w_ff1', 'grad_w_ff2', 'delta_norm_mix', 'delta_w_in', 'delta_b_gate', 'delta_pool_mix', 'delta_pool_scale', 'delta_conv_w', 'delta_q_gain', 'delta_k_gain', 'delta_w_pool_up', 'delta_w_conv_out', 'delta_w_attn_up', 'delta_w_o', 'delta_norm_mlp', 'delta_w_ff1', 'delta_w_ff2', 'new_m_norm_mix', 'new_m_w_in', 'new_m_b_gate', 'new_m_pool_mix', 'new_m_pool_scale', 'new_m_conv_w', 'new_m_q_gain', 'new_m_k_gain', 'new_m_w_pool_up', 'new_m_w_conv_out', 'new_m_w_attn_up', 'new_m_w_o', 'new_m_norm_mlp', 'new_m_w_ff1', 'new_m_w_ff2', 'new_v_norm_mix', 'new_v_w_in', 'new_v_b_gate', 'new_v_pool_mix', 'new_v_pool_scale', 'new_v_conv_w', 'new_v_q_gain', 'new_v_k_gain', 'new_v_w_pool_up', 'new_v_w_conv_out', 'new_v_w_attn_up', 'new_v_w_o', 'new_v_norm_mlp', 'new_v_w_ff1', 'new_v_w_ff2']
TWIN_LEAF_KINDS = {'loss': 'loss', 'grad_x': 'grad_x', 'grad_norm_mix': 'grad_w', 'grad_w_in': 'grad_w', 'grad_b_gate': 'grad_w', 'grad_pool_mix': 'grad_w', 'grad_pool_scale': 'grad_w', 'grad_conv_w': 'grad_w', 'grad_q_gain': 'grad_w', 'grad_k_gain': 'grad_w', 'grad_w_pool_up': 'grad_w', 'grad_w_conv_out': 'grad_w', 'grad_w_attn_up': 'grad_w', 'grad_w_o': 'grad_w', 'grad_norm_mlp': 'grad_w', 'grad_w_ff1': 'grad_w', 'grad_w_ff2': 'grad_w', 'delta_norm_mix': 'delta_w', 'delta_w_in': 'delta_w', 'delta_b_gate': 'delta_w', 'delta_pool_mix': 'delta_w', 'delta_pool_scale': 'delta_w', 'delta_conv_w': 'delta_w', 'delta_q_gain': 'delta_w', 'delta_k_gain': 'delta_w', 'delta_w_pool_up': 'delta_w', 'delta_w_conv_out': 'delta_w', 'delta_w_attn_up': 'delta_w', 'delta_w_o': 'delta_w', 'delta_norm_mlp': 'delta_w', 'delta_w_ff1': 'delta_w', 'delta_w_ff2': 'delta_w', 'new_m_norm_mix': 'new_m', 'new_m_w_in': 'new_m', 'new_m_b_gate': 'new_m', 'new_m_pool_mix': 'new_m', 'new_m_pool_scale': 'new_m', 'new_m_conv_w': 'new_m', 'new_m_q_gain': 'new_m', 'new_m_k_gain': 'new_m', 'new_m_w_pool_up': 'new_m', 'new_m_w_conv_out': 'new_m', 'new_m_w_attn_up': 'new_m', 'new_m_w_o': 'new_m', 'new_m_norm_mlp': 'new_m', 'new_m_w_ff1': 'new_m', 'new_m_w_ff2': 'new_m', 'new_v_norm_mix': 'new_v', 'new_v_w_in': 'new_v', 'new_v_b_gate': 'new_v', 'new_v_pool_mix': 'new_v', 'new_v_pool_scale': 'new_v', 'new_v_conv_w': 'new_v', 'new_v_q_gain': 'new_v', 'new_v_k_gain': 'new_v', 'new_v_w_pool_up': 'new_v', 'new_v_w_conv_out': 'new_v', 'new_v_w_attn_up': 'new_v', 'new_v_w_o': 'new_v', 'new_v_norm_mlp': 'new_v', 'new_v_w_ff1': 'new_v', 'new_v_w_ff2': 'new_v'}


def _forward(args):
    return _fwd_reference(*[args[k] for k in FWD_PARAMS])


def _output_shape():
    def fwd():
        inp = _fwd_setup_inputs(0)
        return _fwd_reference(*[inp[k] for k in FWD_PARAMS])
    out = _jax.eval_shape(fwd)
    return out.shape, out.dtype

N_MICROBATCH = 1
ADAM_LR = 0.001
ADAM_B1 = 0.9
ADAM_B2 = 0.999
ADAM_EPS = 1e-08
ADAM_WD = 0.01
ADAM_STEP = 10
PER_EXAMPLE_BATCH_AXIS = {'x': 0, 'loss_target': 0}
SHARED_INPUTS = []
_WEIGHT_DTYPES = {'norm_mix': _jnp.float32, 'w_in': _jnp.float32, 'b_gate': _jnp.float32, 'pool_mix': _jnp.float32, 'pool_scale': _jnp.float32, 'conv_w': _jnp.float32, 'q_gain': _jnp.float32, 'k_gain': _jnp.float32, 'w_pool_up': _jnp.float32, 'w_conv_out': _jnp.float32, 'w_attn_up': _jnp.float32, 'w_o': _jnp.float32, 'norm_mlp': _jnp.float32, 'w_ff1': _jnp.float32, 'w_ff2': _jnp.float32}
MOMENT_SCALE = {'norm_mix': 3.635983e+01, 'w_in': 1.034967e+00, 'b_gate': 2.380428e+00, 'pool_mix': 1.936951e+00, 'pool_scale': 1.427453e+01, 'conv_w': 1.063779e+01, 'q_gain': 9.574550e-01, 'k_gain': 9.553939e-01, 'w_pool_up': 1.197847e+00, 'w_conv_out': 9.508064e-01, 'w_attn_up': 1.785147e+00, 'w_o': 2.321286e+00, 'norm_mlp': 9.623547e+01, 'w_ff1': 3.535561e+00, 'w_ff2': 1.413468e+01}


def _to_microbatches(a, axis):
    t = _jnp.moveaxis(a, axis, 0)
    t = t.reshape((N_MICROBATCH, t.shape[0] // N_MICROBATCH) + t.shape[1:])
    return _jnp.moveaxis(t, 1, axis + 1)


def setup_inputs(seed: int = 0) -> dict:
    inp = _fwd_setup_inputs(seed)
    key = _jax.random.fold_in(_jax.random.key(seed), 7919)
    shape, _ = _output_shape()
    out = dict(inp)
    out["loss_target"] = _jax.random.normal(_jax.random.fold_in(key, 0), shape, _jnp.float32)
    for i, name in enumerate(TWIN_WEIGHTS):
        w = inp[name].astype(_jnp.float32)
        if MOMENT_SCALE is None:
            s = _jnp.sqrt(_jnp.mean(_jnp.square(w)) + 1e-30)
        else:
            s = MOMENT_SCALE[name]
        km, kv = _jax.random.split(_jax.random.fold_in(key, i + 1))
        out[name] = w
        out["m_" + name] = s * _jax.random.normal(km, w.shape, _jnp.float32)
        out["v_" + name] = (s * s) * _jax.random.uniform(kv, w.shape, _jnp.float32, 0.5, 1.5)
    if N_MICROBATCH > 1:
        for name, axis in PER_EXAMPLE_BATCH_AXIS.items():
            out[name] = _to_microbatches(out[name], axis)
    return {'x': out['x'], 'norm_mix': out['norm_mix'], 'w_in': out['w_in'], 'b_gate': out['b_gate'], 'pool_mix': out['pool_mix'], 'pool_scale': out['pool_scale'], 'conv_w': out['conv_w'], 'q_gain': out['q_gain'], 'k_gain': out['k_gain'], 'w_pool_up': out['w_pool_up'], 'w_conv_out': out['w_conv_out'], 'w_attn_up': out['w_attn_up'], 'w_o': out['w_o'], 'norm_mlp': out['norm_mlp'], 'w_ff1': out['w_ff1'], 'w_ff2': out['w_ff2'], 'loss_target': out['loss_target'], 'm_norm_mix': out['m_norm_mix'], 'm_w_in': out['m_w_in'], 'm_b_gate': out['m_b_gate'], 'm_pool_mix': out['m_pool_mix'], 'm_pool_scale': out['m_pool_scale'], 'm_conv_w': out['m_conv_w'], 'm_q_gain': out['m_q_gain'], 'm_k_gain': out['m_k_gain'], 'm_w_pool_up': out['m_w_pool_up'], 'm_w_conv_out': out['m_w_conv_out'], 'm_w_attn_up': out['m_w_attn_up'], 'm_w_o': out['m_w_o'], 'm_norm_mlp': out['m_norm_mlp'], 'm_w_ff1': out['m_w_ff1'], 'm_w_ff2': out['m_w_ff2'], 'v_norm_mix': out['v_norm_mix'], 'v_w_in': out['v_w_in'], 'v_b_gate': out['v_b_gate'], 'v_pool_mix': out['v_pool_mix'], 'v_pool_scale': out['v_pool_scale'], 'v_conv_w': out['v_conv_w'], 'v_q_gain': out['v_q_gain'], 'v_k_gain': out['v_k_gain'], 'v_w_pool_up': out['v_w_pool_up'], 'v_w_conv_out': out['v_w_conv_out'], 'v_w_attn_up': out['v_w_attn_up'], 'v_w_o': out['v_w_o'], 'v_norm_mlp': out['v_norm_mlp'], 'v_w_ff1': out['v_w_ff1'], 'v_w_ff2': out['v_w_ff2']}


def _loss(weights, diff, rest, loss_target):
    with _jax.named_scope("forward"):
        args = {**rest, TWIN_DIFF_INPUT: diff, **{k: w.astype(_WEIGHT_DTYPES[k]) for k, w in weights.items()}}
        y = _forward(args)
    with _jax.named_scope("loss_head"):
        err = _jnp.square(y.astype(_jnp.float32) - loss_target)
        return 0.5 * _jnp.sum(_jnp.mean(err, axis=-1)) if err.ndim else 0.5 * err


def _adamw(w, g, m, v):
    m = ADAM_B1 * m + (1.0 - ADAM_B1) * g
    v = ADAM_B2 * v + (1.0 - ADAM_B2) * _jnp.square(g)
    m_hat = m / (1.0 - ADAM_B1 ** ADAM_STEP)
    v_hat = v / (1.0 - ADAM_B2 ** ADAM_STEP)
    delta = -ADAM_LR * (m_hat / (_jnp.sqrt(v_hat) + ADAM_EPS) + ADAM_WD * w)
    return delta, m, v


def reference(x, norm_mix, w_in, b_gate, pool_mix, pool_scale, conv_w, q_gain, k_gain, w_pool_up, w_conv_out, w_attn_up, w_o, norm_mlp, w_ff1, w_ff2, loss_target, m_norm_mix, m_w_in, m_b_gate, m_pool_mix, m_pool_scale, m_conv_w, m_q_gain, m_k_gain, m_w_pool_up, m_w_conv_out, m_w_attn_up, m_w_o, m_norm_mlp, m_w_ff1, m_w_ff2, v_norm_mix, v_w_in, v_b_gate, v_pool_mix, v_pool_scale, v_conv_w, v_q_gain, v_k_gain, v_w_pool_up, v_w_conv_out, v_w_attn_up, v_w_o, v_norm_mlp, v_w_ff1, v_w_ff2):
    given = dict(x=x, norm_mix=norm_mix, w_in=w_in, b_gate=b_gate, pool_mix=pool_mix, pool_scale=pool_scale, conv_w=conv_w, q_gain=q_gain, k_gain=k_gain, w_pool_up=w_pool_up, w_conv_out=w_conv_out, w_attn_up=w_attn_up, w_o=w_o, norm_mlp=norm_mlp, w_ff1=w_ff1, w_ff2=w_ff2, loss_target=loss_target, m_norm_mix=m_norm_mix, m_w_in=m_w_in, m_b_gate=m_b_gate, m_pool_mix=m_pool_mix, m_pool_scale=m_pool_scale, m_conv_w=m_conv_w, m_q_gain=m_q_gain, m_k_gain=m_k_gain, m_w_pool_up=m_w_pool_up, m_w_conv_out=m_w_conv_out, m_w_attn_up=m_w_attn_up, m_w_o=m_w_o, m_norm_mlp=m_norm_mlp, m_w_ff1=m_w_ff1, m_w_ff2=m_w_ff2, v_norm_mix=v_norm_mix, v_w_in=v_w_in, v_b_gate=v_b_gate, v_pool_mix=v_pool_mix, v_pool_scale=v_pool_scale, v_conv_w=v_conv_w, v_q_gain=v_q_gain, v_k_gain=v_k_gain, v_w_pool_up=v_w_pool_up, v_w_conv_out=v_w_conv_out, v_w_attn_up=v_w_attn_up, v_w_o=v_w_o, v_norm_mlp=v_norm_mlp, v_w_ff1=v_w_ff1, v_w_ff2=v_w_ff2)
    weights = {n: given[n] for n in TWIN_WEIGHTS}
    shared = {n: given[n] for n in SHARED_INPUTS}
    per_example = {n: given[n] for n in ['x']}
    grad_fn = _jax.value_and_grad(_loss, argnums=(0, 1))

    def one_microbatch(ex, loss_target):
        ex = dict(ex)
        diff = ex.pop(TWIN_DIFF_INPUT)
        return grad_fn(weights, diff, {**shared, **ex}, loss_target)

    if N_MICROBATCH == 1:
        loss, (grad_w, grad_x) = one_microbatch(per_example, given["loss_target"])
    else:
        def body(carry, xs):
            loss_sum, grad_sum = carry
            l_k, (gw_k, gx_k) = one_microbatch(xs[0], xs[1])
            with _jax.named_scope("update"):
                return (loss_sum + l_k, _jax.tree.map(_jnp.add, grad_sum, gw_k)), gx_k

        init = (_jnp.zeros((), _jnp.float32), _jax.tree.map(_jnp.zeros_like, weights))
        (loss, grad_w), grad_x = _jax.lax.scan(body, init, (per_example, given["loss_target"]))
    with _jax.named_scope("update"):
        delta_w, new_m, new_v = {}, {}, {}
        for n in TWIN_WEIGHTS:
            delta_w[n], new_m[n], new_v[n] = _adamw(weights[n], grad_w[n], given["m_" + n], given["v_" + n])
    return (loss, grad_x, *[grad_w[n] for n in TWIN_WEIGHTS], *[delta_w[n] for n in TWIN_WEIGHTS],
            *[new_m[n] for n in TWIN_WEIGHTS], *[new_v[n] for n in TWIN_WEIGHTS])
```

```python
import functools

import jax
import jax.numpy as jnp
from jax import lax
from jax.experimental import pallas as pl
from jax.experimental.pallas import tpu as pltpu

F32 = jnp.float32
BF = jnp.bfloat16
MESH_ID = pl.DeviceIdType.MESH
ANY = pl.BlockSpec(memory_space=pl.ANY)

EPS = 1e-6
MASK_VALUE = -1e30
POOL_WINDOWS = (2, 4, 8, 16)
ATTN_DILATIONS = (1, 4, 16)
ATTN_BLOCK = 128
HEAD_DIM = 64
OFF_Q, OFF_K, OFF_V, OFF_GATE = 2048, 2816, 3584, 4352
N_CHIPS = 4
ADAM_LR, ADAM_B1, ADAM_B2, ADAM_EPS, ADAM_WD, ADAM_STEP = 0.001, 0.9, 0.999, 1e-08, 0.01, 10

VMEM_LIMIT = 48 * 1024 * 1024
LANES = 128

_DIMS = {"nn": (((1,), (0,)), ((), ())), "nt": (((1,), (1,)), ((), ())), "tn": (((0,), (0,)), ((), ()))}


def _params(sem):
    return pltpu.CompilerParams(dimension_semantics=sem, vmem_limit_bytes=VMEM_LIMIT)


def _dot(a, b, mode="nn"):
    return lax.dot_general(a, b, _DIMS[mode], preferred_element_type=F32)


def _mm(a, b, mode, name, *, tm, tn, tk, out_dtype=F32, res=None, aux=None, epi=None, n_outer=False,
        b_shards=False, out_shards=False):
    if mode == "tn":
        K, M = a.shape
    else:
        M, K = a.shape
    if b_shards:
        if mode == "nn":
            assert b.shape[1] == K
            N = b.shape[2] * N_CHIPS
        else:
            assert mode == "nt"
            N = b.shape[1]
            assert b.shape[2] * N_CHIPS == K
    else:
        N = b.shape[0] if mode == "nt" else b.shape[1]
    tm, tn, tk = min(tm, M), min(tn, N), min(tk, K)
    assert M % tm == 0 and N % tn == 0 and K % tk == 0
    nk = K // tk
    if n_outer:
        grid = (N // tn, M // tm, nk)
        ij = lambda p, q_: (q_, p)
    else:
        grid = (M // tm, N // tn, nk)
        ij = lambda p, q_: (p, q_)

    def amap(p, q_, k):
        i, j = ij(p, q_)
        return (k, i) if mode == "tn" else (i, k)

    a_spec = pl.BlockSpec((tk, tm) if mode == "tn" else (tm, tk), amap)
    if b_shards:
        if mode == "nn":
            per = (N // N_CHIPS) // tn
            assert per >= 1 and (N // N_CHIPS) % tn == 0

            def bmap(p, q_, k):
                i, j = ij(p, q_)
                return (j // per, k, j % per)

            b_spec = pl.BlockSpec((None, tk, tn), bmap)
        else:
            per = (K // N_CHIPS) // tk
            assert per >= 1 and (K // N_CHIPS) % tk == 0

            def bmap(p, q_, k):
                i, j = ij(p, q_)
                return (k // per, j, k % per)

            b_spec = pl.BlockSpec((None, tn, tk), bmap)
    else:
        def bmap(p, q_, k):
            i, j = ij(p, q_)
            return (j, k) if mode == "nt" else (k, j)

        b_spec = pl.BlockSpec((tn, tk) if mode == "nt" else (tk, tn), bmap)

    def omap(p, q_, k):
        return ij(p, q_)

    o_spec = pl.BlockSpec((tm, tn), omap)
    if out_shards:
        per_o = (N // N_CHIPS) // tn
        assert per_o >= 1 and (N // N_CHIPS) % tn == 0

        def osmap(p, q_, k):
            i, j = ij(p, q_)
            return (j // per_o, i, j % per_o)

        out_spec0 = pl.BlockSpec((None, tm, tn), osmap)
        out_shape0 = jax.ShapeDtypeStruct((N_CHIPS, M, N // N_CHIPS), out_dtype)
    else:
        out_spec0 = o_spec
        out_shape0 = jax.ShapeDtypeStruct((M, N), out_dtype)

    in_specs = [a_spec, b_spec]
    args = [a, b]
    if res is not None:
        in_specs.append(o_spec)
        args.append(res)
    if aux is not None:
        in_specs.append(o_spec)
        args.append(aux)
    out_specs = [out_spec0]
    out_shape = [out_shape0]
    if epi == "relu2":
        out_specs.append(o_spec)
        out_shape.append(jax.ShapeDtypeStruct((M, N), BF))
    n_out = len(out_shape)
    has_res, has_aux = res is not None, aux is not None

    def body(*refs):
        a_ref, b_ref = refs[0], refs[1]
        pos = 2
        res_ref = aux_ref = None
        if has_res:
            res_ref = refs[pos]
            pos += 1
        if has_aux:
            aux_ref = refs[pos]
            pos += 1
        outs = refs[pos:pos + n_out]
        part = _dot(a_ref[...].astype(BF), b_ref[...].astype(BF), mode)

        def finish(acc):
            if res_ref is not None:
                acc = res_ref[...] + acc
            if epi == "relu2":
                outs[0][...] = acc
                r = jnp.maximum(acc, 0.0)
                outs[1][...] = (r * r).astype(BF)
            elif epi == "drelu2":
                outs[0][...] = (acc * (2.0 * jnp.maximum(aux_ref[...], 0.0))).astype(out_dtype)
            else:
                outs[0][...] = acc.astype(out_dtype)

        if nk == 1:
            finish(part)
        else:
            acc_ref = refs[pos + n_out]
            k = pl.program_id(2)

            @pl.when(k == 0)
            def _():
                acc_ref[...] = part

            @pl.when(k > 0)
            def _():
                acc_ref[...] += part

            @pl.when(k == nk - 1)
            def _():
                finish(acc_ref[...])

    scratch = [pltpu.VMEM((tm, tn), F32)] if nk > 1 else []
    out = pl.pallas_call(
        body, name=name, grid=grid, in_specs=in_specs, out_specs=out_specs, out_shape=out_shape,
        scratch_shapes=scratch, compiler_params=_params(("parallel", "parallel", "arbitrary")),
    )(*args)
    return out if n_out > 1 else out[0]


def _rms_fwd(x, gain, name):
    T, D = x.shape
    tm = min(512, T)

    def body(x_ref, g_ref, o_ref):
        xv = x_ref[...]
        r = lax.rsqrt(jnp.mean(xv * xv, axis=-1, keepdims=True) + EPS)
        o_ref[...] = ((xv * r) * g_ref[...]).astype(BF)

    return pl.pallas_call(
        body, name=name, grid=(T // tm,),
        in_specs=[pl.BlockSpec((tm, D), lambda i: (i, 0)), pl.BlockSpec((1, D), lambda i: (0, 0))],
        out_specs=pl.BlockSpec((tm, D), lambda i: (i, 0)), out_shape=jax.ShapeDtypeStruct((T, D), BF),
        compiler_params=_params(("parallel",)),
    )(x, gain)


def _rms_bwd(dh, x, gain, dres, name):
    T, D = x.shape
    tm = min(512, T)

    def body(dh_ref, x_ref, g_ref, dres_ref, dx_ref, dg_ref):
        xv = x_ref[...]
        r = lax.rsqrt(jnp.mean(xv * xv, axis=-1, keepdims=True) + EPS)
        xhat = xv * r
        dhv = dh_ref[...]
        dy = dhv * g_ref[...]
        dx_ref[...] = dres_ref[...] + r * (dy - xhat * jnp.mean(dy * xhat, axis=-1, keepdims=True))

        @pl.when(pl.program_id(0) == 0)
        def _():
            dg_ref[...] = jnp.zeros_like(dg_ref)

        dg_ref[...] += jnp.sum(dhv * xhat, axis=0, keepdims=True)

    row = pl.BlockSpec((tm, D), lambda i: (i, 0))
    vec = pl.BlockSpec((1, D), lambda i: (0, 0))
    return pl.pallas_call(
        body, name=name, grid=(T // tm,), in_specs=[row, row, vec, row], out_specs=[row, vec],
        out_shape=[jax.ShapeDtypeStruct((T, D), F32), jax.ShapeDtypeStruct((1, D), F32)],
        compiler_params=_params(("arbitrary",)),
    )(dh, x, gain, dres)


def _loss_grad(y, target, name):
    T, D = y.shape
    tm = min(512, T)

    def body(y_ref, t_ref, dy_ref, l_ref):
        e = y_ref[...] - t_ref[...]
        dy_ref[...] = e / float(D)

        @pl.when(pl.program_id(0) == 0)
        def _():
            l_ref[...] = jnp.zeros_like(l_ref)

        l_ref[...] += 0.5 * jnp.sum(jnp.mean(e * e, axis=-1, keepdims=True))

    row = pl.BlockSpec((tm, D), lambda i: (i, 0))
    return pl.pallas_call(
        body, name=name, grid=(T // tm,), in_specs=[row, row],
        out_specs=[row, pl.BlockSpec((1, LANES), lambda i: (0, 0))],
        out_shape=[jax.ShapeDtypeStruct((T, D), F32), jax.ShapeDtypeStruct((1, LANES), F32)],
        compiler_params=_params(("arbitrary",)),
    )(y, target)


POOL_HALO = 16
CONV_HALO = 8


def _causal_window_sum(v, w):
    s, sh = v, 1
    while sh < w:
        s = s + pltpu.roll(s, sh, 0)
        sh *= 2
    return s


def _anticausal_window_sum(v, w):
    n = v.shape[0]
    s, sh = v, 1
    while sh < w:
        s = s + pltpu.roll(s, n - sh, 0)
        sh *= 2
    return s


def _poolconv_fwd(z, pmix_b, pscale, convw, name):
    T = z.shape[0]
    R = min(512, T)
    PH, CH = R // POOL_HALO, R // CONV_HALO

    def body(u_ref, uh_ref, b_ref, c_ref, ch_ref, x_ref, xh_ref, mix_ref, sc_ref, cw_ref, yp_ref, yc_ref):
        i = pl.program_id(0)
        keep = (i > 0).astype(F32)
        row = i * R + lax.broadcasted_iota(jnp.int32, (R, 1), 0)
        w_all = jnp.concatenate([uh_ref[...] * keep, u_ref[...]], axis=0)
        for g, w in enumerate(POOL_WINDOWS):
            cols = slice(128 * g, 128 * (g + 1))
            wg = w_all[:, cols]
            s = _causal_window_sum(wg, w)[POOL_HALO:]
            cnt = jnp.minimum(row + 1, w).astype(F32)
            dgrp = s / cnt - wg[POOL_HALO:]
            y = _dot(dgrp.astype(BF), mix_ref[g]) * sc_ref[:, cols]
            yp_ref[:, cols] = y.astype(BF)
        uc = jnp.concatenate([ch_ref[...] * xh_ref[...] * keep, c_ref[...] * x_ref[...]], axis=0)
        yc = cw_ref[2:3, :] * uc + cw_ref[0:1, :] * pltpu.roll(uc, 2, 0) + cw_ref[1:2, :] * pltpu.roll(uc, 1, 0)
        yc_ref[...] = (b_ref[...] * yc[CONV_HALO:]).astype(BF)

    def main(cb):
        return pl.BlockSpec((R, 512), lambda i: (i, cb))

    def prev(cb, halo, per):
        return pl.BlockSpec((halo, 512), lambda i: (jnp.maximum(i * per - 1, 0), cb))

    full = lambda a: pl.BlockSpec(a.shape, lambda i: (0,) * a.ndim)
    return pl.pallas_call(
        body, name=name, grid=(T // R,),
        in_specs=[main(0), prev(0, POOL_HALO, PH), main(1), main(2), prev(2, CONV_HALO, CH), main(3),
                  prev(3, CONV_HALO, CH), full(pmix_b), full(pscale), full(convw)],
        out_specs=[pl.BlockSpec((R, 512), lambda i: (i, 0))] * 2,
        out_shape=[jax.ShapeDtypeStruct((T, 512), BF)] * 2,
        compiler_params=_params(("parallel",)),
    )(z, z, z, z, z, z, z, pmix_b, pscale, convw)


def _poolconv_bwd(z, dyp, dyc, pmix_b, pscale, convw, name):
    T = z.shape[0]
    R = min(512, T)
    PH, CH = R // POOL_HALO, R // CONV_HALO
    nsteps = T // R

    def body(u_ref, uh_ref, b_ref, bn_ref, c_ref, ch_ref, x_ref, xh_ref, dyp_ref, dypn_ref, dyc_ref, dycn_ref,
             mix_ref, sc_ref, cw_ref, dz_ref, dmix_ref, dsc_ref, dcw_ref):
        i = pl.program_id(0)
        keep_prev = (i > 0).astype(F32)
        keep_next = (i < nsteps - 1).astype(F32)

        @pl.when(i == 0)
        def _():
            dmix_ref[...] = jnp.zeros_like(dmix_ref)
            dsc_ref[...] = jnp.zeros_like(dsc_ref)
            dcw_ref[...] = jnp.zeros_like(dcw_ref)

        row = i * R + lax.broadcasted_iota(jnp.int32, (R, 1), 0)
        row_ext = i * R + lax.broadcasted_iota(jnp.int32, (R + POOL_HALO, 1), 0)
        w_all = jnp.concatenate([uh_ref[...] * keep_prev, u_ref[...]], axis=0)
        dyp_ext = jnp.concatenate([dyp_ref[...], dypn_ref[...] * keep_next], axis=0)
        for g, w in enumerate(POOL_WINDOWS):
            cols = slice(128 * g, 128 * (g + 1))
            wg = w_all[:, cols]
            s = _causal_window_sum(wg, w)[POOL_HALO:]
            cnt = jnp.minimum(row + 1, w).astype(F32)
            dgrp = (s / cnt - wg[POOL_HALO:]).astype(BF)
            y_pre = _dot(dgrp, mix_ref[g])
            dsc_ref[:, cols] += jnp.sum(dyp_ref[:, cols] * y_pre, axis=0, keepdims=True)
            dyb = (dyp_ext[:, cols] * sc_ref[:, cols]).astype(BF)
            dmix_ref[cols, :] += _dot(dgrp, dyb[:R], "tn")
            dd = _dot(dyb, mix_ref[g], "nt")
            cnt_ext = jnp.minimum(row_ext + 1, w).astype(F32)
            e = _anticausal_window_sum(dd / cnt_ext, w)
            dz_ref[:, cols] = (e[:R] - dd[:R]).astype(BF)
        cw0, cw1, cw2 = cw_ref[0:1, :], cw_ref[1:2, :], cw_ref[2:3, :]
        uc = jnp.concatenate([ch_ref[...] * xh_ref[...] * keep_prev, c_ref[...] * x_ref[...]], axis=0)
        uc1 = pltpu.roll(uc, 1, 0)[CONV_HALO:]
        uc2 = pltpu.roll(uc, 2, 0)[CONV_HALO:]
        uc0 = uc[CONV_HALO:]
        yc = cw2 * uc0 + cw0 * uc2 + cw1 * uc1
        dycv = dyc_ref[...]
        dz_ref[:, 512:1024] = (dycv * yc).astype(BF)
        dv_ext = jnp.concatenate([dycv * b_ref[...], dycn_ref[...] * bn_ref[...] * keep_next], axis=0)
        n_ext = R + CONV_HALO
        duc = (cw2 * dv_ext + cw1 * pltpu.roll(dv_ext, n_ext - 1, 0) + cw0 * pltpu.roll(dv_ext, n_ext - 2, 0))[:R]
        dv = dv_ext[:R]
        dcw_ref[0:1, :] += jnp.sum(dv * uc2, axis=0, keepdims=True)
        dcw_ref[1:2, :] += jnp.sum(dv * uc1, axis=0, keepdims=True)
        dcw_ref[2:3, :] += jnp.sum(dv * uc0, axis=0, keepdims=True)
        dz_ref[:, 1024:1536] = (duc * x_ref[...]).astype(BF)
        dz_ref[:, 1536:2048] = (duc * c_ref[...]).astype(BF)

    def main(cb):
        return pl.BlockSpec((R, 512), lambda i: (i, cb))

    def prev(cb, halo, per):
        return pl.BlockSpec((halo, 512), lambda i: (jnp.maximum(i * per - 1, 0), cb))

    def nxt(cb, halo, per):
        return pl.BlockSpec((halo, 512), lambda i: (jnp.minimum((i + 1) * per, T // halo - 1), cb))

    full = lambda a: pl.BlockSpec(a.shape, lambda i: (0,) * a.ndim)
    return pl.pallas_call(
        body, name=name, grid=(nsteps,),
        in_specs=[main(0), prev(0, POOL_HALO, PH), main(1), nxt(1, CONV_HALO, CH), main(2), prev(2, CONV_HALO, CH),
                  main(3), prev(3, CONV_HALO, CH), main(0), nxt(0, POOL_HALO, PH), main(0), nxt(0, CONV_HALO, CH),
                  full(pmix_b), full(pscale), full(convw)],
        out_specs=[pl.BlockSpec((R, 2048), lambda i: (i, 0)), pl.BlockSpec((512, 128), lambda i: (0, 0)),
                   pl.BlockSpec((1, 512), lambda i: (0, 0)), pl.BlockSpec((8, 512), lambda i: (0, 0))],
        out_shape=[jax.ShapeDtypeStruct((T, 2048), BF), jax.ShapeDtypeStruct((512, 128), F32),
                   jax.ShapeDtypeStruct((1, 512), F32), jax.ShapeDtypeStruct((8, 512), F32)],
        compiler_params=_params(("arbitrary",)),
    )(z, z, z, z, z, z, z, z, dyp, dyp, dyc, dyc, pmix_b, pscale, convw)


def _head_norm(x, g2, ma):
    sq = x * x
    sa = jnp.sum(jnp.where(ma, sq, 0.0), axis=-1, keepdims=True)
    sb = jnp.sum(jnp.where(ma, 0.0, sq), axis=-1, keepdims=True)
    r = jnp.where(ma, lax.rsqrt(sa / HEAD_DIM + EPS), lax.rsqrt(sb / HEAD_DIM + EPS))
    return x * r, r


def _head_norm_bwd(dy, xhat, r, g2, ma):
    dxh = dy * g2
    pr = dxh * xhat
    sa = jnp.sum(jnp.where(ma, pr, 0.0), axis=-1, keepdims=True)
    sb = jnp.sum(jnp.where(ma, 0.0, pr), axis=-1, keepdims=True)
    mh = jnp.where(ma, sa, sb) / HEAD_DIM
    return r * (dxh - xhat * mh)


def _head_col(tile, hm):
    return jnp.max(jnp.where(hm, tile, -jnp.inf), axis=-1, keepdims=True)


def _attn_masks(other_block_exists):
    lane = lax.broadcasted_iota(jnp.int32, (ATTN_BLOCK, ATTN_BLOCK), 1)
    qi = lax.broadcasted_iota(jnp.int32, (ATTN_BLOCK, ATTN_BLOCK), 0)
    never = (1 - other_block_exists.astype(jnp.int32)) * (2 * ATTN_BLOCK)
    return lane < HEAD_DIM, lane <= qi, lane >= qi + never


def _attn_fwd(qf, kf, vf, gq2, gk2, d, name):
    L = qf.shape[0]
    nb = L // ATTN_BLOCK
    scale = HEAD_DIM ** -0.5

    def body(q_ref, kc_ref, kp_ref, vc_ref, vp_ref, gq_ref, gk_ref, o_ref, lse_ref):
        j = pl.program_id(1)
        ma, mask_c, mask_p = _attn_masks(j > 0)
        for t in range(2):
            sl = slice(LANES * t, LANES * (t + 1))
            qn = _head_norm(q_ref[:, sl], gq_ref[...], ma)[0] * gq_ref[...]
            kcb = (_head_norm(kc_ref[:, sl], gk_ref[...], ma)[0] * gk_ref[...]).astype(BF)
            kpb = (_head_norm(kp_ref[:, sl], gk_ref[...], ma)[0] * gk_ref[...]).astype(BF)
            vcb = vc_ref[:, sl].astype(BF)
            vpb = vp_ref[:, sl].astype(BF)
            o_t = lse_t = None
            for hm in (ma, jnp.logical_not(ma)):
                qh = jnp.where(hm, qn, 0.0).astype(BF)
                s_c = jnp.where(mask_c, _dot(qh, kcb, "nt") * scale, MASK_VALUE)
                s_p = jnp.where(mask_p, _dot(qh, kpb, "nt") * scale, MASK_VALUE)
                m = jnp.maximum(jnp.max(s_c, axis=-1, keepdims=True), jnp.max(s_p, axis=-1, keepdims=True))
                p_c = jnp.exp(s_c - m)
                p_p = jnp.exp(s_p - m)
                den = jnp.sum(p_c, axis=-1, keepdims=True) + jnp.sum(p_p, axis=-1, keepdims=True)
                o = (_dot(p_c.astype(BF), vcb) + _dot(p_p.astype(BF), vpb)) / den
                lse = jnp.broadcast_to(m + jnp.log(den), o.shape)
                o_t = o if o_t is None else jnp.where(ma, o_t, o)
                lse_t = lse if lse_t is None else jnp.where(ma, lse_t, lse)
            o_ref[:, sl] = o_t
            lse_ref[:, sl] = lse_t

    cur = pl.BlockSpec((ATTN_BLOCK, 256), lambda r, j: (j, r))
    prv = pl.BlockSpec((ATTN_BLOCK, 256), lambda r, j: (jnp.maximum(j - 1, 0), r))
    vec = pl.BlockSpec((1, LANES), lambda r, j: (0, 0))
    return pl.pallas_call(
        body, name=name, grid=(d, nb), in_specs=[cur, cur, prv, cur, prv, vec, vec], out_specs=[cur, cur],
        out_shape=[jax.ShapeDtypeStruct(qf.shape, F32)] * 2,
        compiler_params=_params(("parallel", "parallel")),
    )(qf, kf, kf, vf, vf, gq2, gk2)


def _attn_bwd(qf, kf, vf, dof, cf, lsef, gq2, gk2, d, name):
    L = qf.shape[0]
    nb = L // ATTN_BLOCK
    scale = HEAD_DIM ** -0.5

    def body(q_ref, qn_ref, k_ref, kp_ref, v_ref, vp_ref, do_ref, don_ref, c_ref, cn_ref, lse_ref, lsen_ref,
             gq_ref, gk_ref, dq_ref, dk_ref, dv_ref, dgq_ref, dgk_ref):
        r_id, j = pl.program_id(0), pl.program_id(1)
        ma, mask_c, mask_p = _attn_masks(j > 0)
        mask_n = _attn_masks(j < nb - 1)[2]

        @pl.when((r_id == 0) & (j == 0))
        def _():
            dgq_ref[...] = jnp.zeros_like(dgq_ref)
            dgk_ref[...] = jnp.zeros_like(dgk_ref)

        gq, gk = gq_ref[...], gk_ref[...]
        for t in range(2):
            sl = slice(LANES * t, LANES * (t + 1))
            qhat, rq = _head_norm(q_ref[:, sl], gq, ma)
            qn = qhat * gq
            qn_next = _head_norm(qn_ref[:, sl], gq, ma)[0] * gq
            khat, rk = _head_norm(k_ref[:, sl], gk, ma)
            kcb = (khat * gk).astype(BF)
            kpb = (_head_norm(kp_ref[:, sl], gk, ma)[0] * gk).astype(BF)
            vcb = v_ref[:, sl].astype(BF)
            vpb = vp_ref[:, sl].astype(BF)
            do_t, don_t = do_ref[:, sl], don_ref[:, sl]
            c_t, cn_t = c_ref[:, sl], cn_ref[:, sl]
            lse_t, lsen_t = lse_ref[:, sl], lsen_ref[:, sl]
            dq_t = None
            dk_t = jnp.zeros((ATTN_BLOCK, LANES), F32)
            dv_t = jnp.zeros((ATTN_BLOCK, LANES), F32)
            for hm in (ma, jnp.logical_not(ma)):
                qh = jnp.where(hm, qn, 0.0).astype(BF)
                doh = jnp.where(hm, do_t, 0.0).astype(BF)
                lse_h = _head_col(lse_t, hm)
                c_h = _head_col(c_t, hm)
                s_c = jnp.where(mask_c, _dot(qh, kcb, "nt") * scale, MASK_VALUE)
                s_p = jnp.where(mask_p, _dot(qh, kpb, "nt") * scale, MASK_VALUE)
                p_c = jnp.exp(s_c - lse_h)
                p_p = jnp.exp(s_p - lse_h)
                ds_c = ((p_c * (_dot(doh, vcb, "nt") + c_h)) * scale).astype(BF)
                ds_p = ((p_p * (_dot(doh, vpb, "nt") + c_h)) * scale).astype(BF)
                dq_h = _dot(ds_c, kcb) + _dot(ds_p, kpb)
                dq_t = dq_h if dq_t is None else jnp.where(ma, dq_t, dq_h)
                qh_n = jnp.where(hm, qn_next, 0.0).astype(BF)
                doh_n = jnp.where(hm, don_t, 0.0).astype(BF)
                s_n = jnp.where(mask_n, _dot(qh_n, kcb, "nt") * scale, MASK_VALUE)
                p_n = jnp.exp(s_n - _head_col(lsen_t, hm))
                ds_n = ((p_n * (_dot(doh_n, vcb, "nt") + _head_col(cn_t, hm))) * scale).astype(BF)
                dv_t = dv_t + _dot(p_c.astype(BF), doh, "tn") + _dot(p_n.astype(BF), doh_n, "tn")
                dk_t = dk_t + _dot(ds_c, qh, "tn") + _dot(ds_n, qh_n, "tn")
            dq_ref[:, sl] = _head_norm_bwd(dq_t, qhat, rq, gq, ma).astype(BF)
            dk_ref[:, sl] = _head_norm_bwd(dk_t, khat, rk, gk, ma).astype(BF)
            dv_ref[:, sl] = dv_t.astype(BF)
            dgq_ref[...] += jnp.sum(dq_t * qhat, axis=0, keepdims=True)
            dgk_ref[...] += jnp.sum(dk_t * khat, axis=0, keepdims=True)

    cur = pl.BlockSpec((ATTN_BLOCK, 256), lambda r, j: (j, r))
    prv = pl.BlockSpec((ATTN_BLOCK, 256), lambda r, j: (jnp.maximum(j - 1, 0), r))
    nxt = pl.BlockSpec((ATTN_BLOCK, 256), lambda r, j: (jnp.minimum(j + 1, nb - 1), r))
    vec = pl.BlockSpec((1, LANES), lambda r, j: (0, 0))
    return pl.pallas_call(
        body, name=name, grid=(d, nb),
        in_specs=[cur, nxt, cur, prv, cur, prv, cur, nxt, cur, nxt, cur, nxt, vec, vec],
        out_specs=[cur, cur, cur, vec, vec],
        out_shape=[jax.ShapeDtypeStruct(qf.shape, BF)] * 3 + [jax.ShapeDtypeStruct((1, LANES), F32)] * 2,
        compiler_params=_params(("arbitrary", "arbitrary")),
    )(qf, qf, kf, kf, vf, vf, dof, dof, cf, cf, lsef, lsef, gq2, gk2)


MERGE_ROWS = 256
GATE_TILE = 256


def _group_mix(o_refs, lse_refs):
    lses = [r[...] for r in lse_refs]
    m = jnp.maximum(jnp.maximum(lses[0], lses[1]), lses[2])
    es = [jnp.exp(l - m) for l in lses]
    den = es[0] + es[1] + es[2]
    ws = [e / den for e in es]
    y = ws[0] * o_refs[0][...] + ws[1] * o_refs[1][...] + ws[2] * o_refs[2][...]
    return ws, y


def _sigmoid(v):
    return 1.0 / (1.0 + jnp.exp(-v))


def _merge_specs(T, z, bgate, gpu, gco, gau):
    tm = min(MERGE_ROWS, T)
    row = lambda w: pl.BlockSpec((tm, w), lambda i: (i, 0))
    gate0 = OFF_GATE // GATE_TILE
    gates = [pl.BlockSpec((tm, GATE_TILE), functools.partial(lambda i, cb: (i, cb), cb=gate0 + n))
             for n in range(3 * N_CHIPS)]
    full = lambda a: pl.BlockSpec(a.shape, lambda i: (0,) * a.ndim)
    specs = [row(512), row(512)] + [row(256)] * 6 + gates + [full(bgate), full(gpu), full(gco), full(gau)]
    return tm, row, specs


def _merge_fwd(yp, yc, o3, lse3, z, bgate, gpu, gco, gau, name):
    T = yp.shape[0]
    tm, row, specs = _merge_specs(T, z, bgate, gpu, gco, gau)

    def body(*refs):
        yp_ref, yc_ref = refs[0], refs[1]
        o_refs, lse_refs = refs[2:5], refs[5:8]
        zg = refs[8:20]
        b_ref, gpu_ref, gco_ref, gau_ref, out_ref = refs[20:25]
        yab = _group_mix(o_refs, lse_refs)[1].astype(BF)
        ys = (yp_ref[...], yc_ref[...], yab)
        ups = (gpu_ref, gco_ref, gau_ref)
        for n in range(N_CHIPS):
            acc = None
            for b in range(3):
                gcol = slice(1024 * b + GATE_TILE * n, 1024 * b + GATE_TILE * (n + 1))
                gate = _sigmoid(zg[N_CHIPS * b + n][...] + b_ref[:, gcol])
                term = gate * _dot(ys[b], ups[b][n])
                acc = term if acc is None else acc + term
            out_ref[:, GATE_TILE * n:GATE_TILE * (n + 1)] = acc.astype(BF)

    return pl.pallas_call(
        body, name=name, grid=(T // tm,), in_specs=specs, out_specs=row(1024),
        out_shape=jax.ShapeDtypeStruct((T, 1024), BF), compiler_params=_params(("parallel",)),
    )(yp, yc, *o3, *lse3, *([z] * 12), bgate, gpu, gco, gau)


def _merge_bwd(dm, yp, yc, o3, lse3, z, bgate, gpu, gco, gau, name):
    T = yp.shape[0]
    tm, row, specs = _merge_specs(T, z, bgate, gpu, gco, gau)
    nsteps = T // tm

    def body(*refs):
        dm_ref, yp_ref, yc_ref = refs[0:3]
        o_refs, lse_refs = refs[3:6], refs[6:9]
        zg = refs[9:21]
        b_ref, gpu_ref, gco_ref, gau_ref = refs[21:25]
        dzg_ref, dyp_ref, dyc_ref = refs[25:28]
        do_refs, c_refs = refs[28:31], refs[31:34]
        dgpu_ref, dgco_ref, dgau_ref, dbg_ref = refs[34:38]
        accs = refs[38:41]
        i = pl.program_id(0)

        @pl.when(i == 0)
        def _():
            for a in accs:
                a[...] = jnp.zeros_like(a)
            dbg_ref[...] = jnp.zeros_like(dbg_ref)

        ws, y = _group_mix(o_refs, lse_refs)
        ys = (yp_ref[...], yc_ref[...], y.astype(BF))
        ups = (gpu_ref, gco_ref, gau_ref)
        dys = [None, None, None]
        for n in range(N_CHIPS):
            dmn = dm_ref[:, GATE_TILE * n:GATE_TILE * (n + 1)]
            for b in range(3):
                gcol = slice(1024 * b + GATE_TILE * n, 1024 * b + GATE_TILE * (n + 1))
                gate = _sigmoid(zg[N_CHIPS * b + n][...] + b_ref[:, gcol])
                up = _dot(ys[b], ups[b][n])
                dzg = (dmn * up) * (gate * (1.0 - gate))
                dzg_ref[:, gcol] = dzg.astype(BF)
                dbg_ref[:, gcol] += jnp.sum(dzg, axis=0, keepdims=True)
                dup = (dmn * gate).astype(BF)
                accs[b][n] += _dot(ys[b], dup, "tn")
                dyb = _dot(dup, ups[b][n], "nt")
                dys[b] = dyb if dys[b] is None else dys[b] + dyb
        dyp_ref[...] = dys[0]
        dyc_ref[...] = dys[1]
        dya = dys[2]
        lane = lax.broadcasted_iota(jnp.int32, dya.shape, 1) // HEAD_DIM
        pr = dya * y
        rho = jnp.zeros_like(pr)
        for h in range(256 // HEAD_DIM):
            hm = lane == h
            rho = jnp.where(hm, jnp.sum(jnp.where(hm, pr, 0.0), axis=-1, keepdims=True), rho)
        for g in range(3):
            do_refs[g][...] = ws[g] * dya
            c_refs[g][...] = -(ws[g] * rho)

        @pl.when(i == nsteps - 1)
        def _():
            dgpu_ref[...] = accs[0][...].astype(BF)
            dgco_ref[...] = accs[1][...].astype(BF)
            dgau_ref[...] = accs[2][...].astype(BF)

    full = lambda a: pl.BlockSpec(a.shape, lambda i: (0,) * a.ndim)
    out_specs = ([row(3072), row(512), row(512)] + [row(256)] * 6 + [full(gpu), full(gco), full(gau)]
                 + [pl.BlockSpec((1, 3072), lambda i: (0, 0))])
    out_shape = ([jax.ShapeDtypeStruct((T, 3072), BF)] + [jax.ShapeDtypeStruct((T, 512), F32)] * 2
                 + [jax.ShapeDtypeStruct((T, 256), F32)] * 6
                 + [jax.ShapeDtypeStruct(g.shape, BF) for g in (gpu, gco, gau)]
                 + [jax.ShapeDtypeStruct((1, 3072), F32)])
    return pl.pallas_call(
        body, name=name, grid=(nsteps,), in_specs=[row(1024)] + specs, out_specs=out_specs, out_shape=out_shape,
        scratch_shapes=[pltpu.VMEM(g.shape, F32) for g in (gpu, gco, gau)],
        compiler_params=_params(("arbitrary",)),
    )(dm, yp, yc, *o3, *lse3, *([z] * 12), bgate, gpu, gco, gau)


def _fold(a, d):
    return a.reshape(a.shape[0] // d, d * a.shape[1])


def _unfold(a, d):
    return a.reshape(a.shape[0] * d, a.shape[1] // d)


def _layer_fwd(x, w, tag):
    hb = _rms_fwd(x, w["norm_mix"], f"rms_mix_{tag}")
    z = _mm(hb, w["w_in"], "nn", f"in_proj_{tag}", tm=512, tn=3712, tk=1024, n_outer=True)
    yp, yc = _poolconv_fwd(z, w["pool_mix"], w["pool_scale"], w["conv_w"], f"poolconv_{tag}")
    folded, o3, lse3 = [], [], []
    for g, d in enumerate(ATTN_DILATIONS):
        qf, kf, vf = (_fold(z[:, off + 256 * g:off + 256 * (g + 1)], d) for off in (OFF_Q, OFF_K, OFF_V))
        o, lse = _attn_fwd(qf, kf, vf, w["q_gain"], w["k_gain"], d, f"attn{g}_{tag}")
        folded.append((qf, kf, vf))
        o3.append(_unfold(o, d))
        lse3.append(_unfold(lse, d))
    merged = _merge_fwd(yp, yc, o3, lse3, z, w["b_gate"], w["w_pool_up"], w["w_conv_out"], w["w_attn_up"],
                        f"merge_{tag}")
    x1 = _mm(merged, w["w_o"], "nn", f"out_proj_{tag}", tm=1024, tn=1024, tk=1024, res=x)
    h2b = _rms_fwd(x1, w["norm_mlp"], f"rms_mlp_{tag}")
    a, rb = _mm(h2b, w["w_ff1"], "nn", f"ff1_{tag}", tm=1024, tn=1024, tk=1024, epi="relu2", n_outer=True,
                b_shards=True)
    x2 = _mm(rb, w["w_ff2"], "nn", f"ff2_{tag}", tm=1024, tn=1024, tk=1024, res=x1)
    saved = dict(x=x, hb=hb, z=z, yp=yp, yc=yc, folded=folded, o3=o3, lse3=lse3, merged=merged, x1=x1, h2b=h2b,
                 a=a, rb=rb)
    return x2, saved


def _layer_bwd(dx2, w, s, tag):
    g = {}
    dab = _mm(dx2, w["w_ff2"], "nt", f"d_ff2_act_{tag}", tm=1024, tn=1024, tk=1024, out_dtype=BF, aux=s["a"],
              epi="drelu2")
    g["w_ff2"] = _mm(s["rb"], dx2, "tn", f"d_ff2_w_{tag}", tm=1024, tn=1024, tk=1024, out_dtype=BF)
    g["w_ff1"] = _mm(s["h2b"], dab, "tn", f"d_ff1_w_{tag}", tm=1024, tn=1024, tk=1024, out_dtype=BF, out_shards=True)
    dh2 = _mm(dab, w["w_ff1"], "nt", f"d_ff1_act_{tag}", tm=1024, tn=1024, tk=1024, b_shards=True)
    dx1, g["norm_mlp"] = _rms_bwd(dh2, s["x1"], w["norm_mlp"], dx2, f"d_rms_mlp_{tag}")
    dm = _mm(dx1, w["w_o"], "nt", f"d_out_act_{tag}", tm=1024, tn=1024, tk=1024)
    g["w_o"] = _mm(s["merged"], dx1, "tn", f"d_out_w_{tag}", tm=1024, tn=1024, tk=1024, out_dtype=BF)
    (dzg, dyp, dyc, do0, do1, do2, c0, c1, c2, g["w_pool_up"], g["w_conv_out"], g["w_attn_up"],
     g["b_gate"]) = _merge_bwd(dm, s["yp"], s["yc"], s["o3"], s["lse3"], s["z"], w["b_gate"], w["w_pool_up"],
                               w["w_conv_out"], w["w_attn_up"], f"d_merge_{tag}")
    dq, dk, dv = [], [], []
    dgq = dgk = None
    for gi, d in enumerate(ATTN_DILATIONS):
        qf, kf, vf = s["folded"][gi]
        dzq, dzk, dzv, pq, pk = _attn_bwd(qf, kf, vf, _fold((do0, do1, do2)[gi], d), _fold((c0, c1, c2)[gi], d),
                                          _fold(s["lse3"][gi], d), w["q_gain"], w["k_gain"], d, f"d_attn{gi}_{tag}")
        dq.append(_unfold(dzq, d))
        dk.append(_unfold(dzk, d))
        dv.append(_unfold(dzv, d))
        dgq = pq if dgq is None else dgq + pq
        dgk = pk if dgk is None else dgk + pk
    g["q_gain"] = dgq[:, :HEAD_DIM] + dgq[:, HEAD_DIM:]
    g["k_gain"] = dgk[:, :HEAD_DIM] + dgk[:, HEAD_DIM:]
    dzpc, g["pool_mix"], g["pool_scale"], g["conv_w"] = _poolconv_bwd(
        s["z"], dyp, dyc, w["pool_mix"], w["pool_scale"], w["conv_w"], f"d_poolconv_{tag}")
    dz = jnp.concatenate([dzpc] + dq + dk + dv + [dzg], axis=1)
    g["w_in"] = _mm(s["hb"], dz, "tn", f"d_in_w_{tag}", tm=512, tn=3712, tk=512, out_dtype=BF)
    dh = _mm(dz, w["w_in"], "nt", f"d_in_act_{tag}", tm=512, tn=1024, tk=3712)
    dx, g["norm_mix"] = _rms_bwd(dh, s["x"], w["norm_mix"], dx1, f"d_rms_mix_{tag}")
    return dx, g


def _position():
    x, y, c = lax.axis_index("x"), lax.axis_index("y"), lax.axis_index("c")
    chips = [(1 - x, y), (x, 1 - y), (1 - x, 1 - y)]
    return x, y, c, 2 * x + y, chips, [2 * cx + cy for cx, cy in chips]


def _remote(src, dst, ssem, rsem, dev):
    return pltpu.make_async_remote_copy(src_ref=src, dst_ref=dst, send_sem=ssem, recv_sem=rsem, device_id=dev,
                                        device_id_type=MESH_ID)


def _gather_weights(shards):
    n = len(shards)

    def body(*refs):
        ins, outs = refs[:n], refs[n:3 * n]
        lsem, ssem, rsem, fssem, frsem = refs[3 * n:]
        x, y, c, q, chips, qs = _position()
        local = []
        for k in range(n):
            for l in (0, 1):
                cp = pltpu.make_async_copy(ins[k].at[l], outs[2 * k + l].at[q], lsem.at[k, l])
                cp.start()
                local.append(cp)
        for l in (0, 1):
            @pl.when(c == l)
            def _():
                sib = (x, y, 1 - c)
                sent = []
                for k in range(n):
                    for j, chip in enumerate(chips):
                        cp = _remote(ins[k].at[l], outs[2 * k + l].at[q], ssem.at[k, j], rsem.at[k, j],
                                     (chip[0], chip[1], c))
                        cp.start()
                        sent.append(cp)
                for k in range(n):
                    for j, chip in enumerate(chips):
                        slot = outs[2 * k + l].at[qs[j]]
                        _remote(ins[k].at[l], slot, ssem.at[k, j], rsem.at[k, j], (chip[0], chip[1], c)).wait_recv()
                        cp = _remote(slot, slot, fssem.at[k, j], frsem.at[k, j], sib)
                        cp.start()
                        sent.append(cp)
                for k in range(n):
                    for j in range(3):
                        slot = outs[2 * k + (1 - l)].at[qs[j]]
                        _remote(slot, slot, fssem.at[k, j], frsem.at[k, j], sib).wait_recv()
                for cp in sent:
                    cp.wait_send()
        for cp in local:
            cp.wait()

    out_shape = []
    for s in shards:
        out_shape += [jax.ShapeDtypeStruct((N_CHIPS,) + s.shape[1:], s.dtype)] * 2
    outs = pl.pallas_call(
        body, name="gather_weights", in_specs=[ANY] * n, out_specs=[ANY] * (2 * n), out_shape=out_shape,
        scratch_shapes=[pltpu.SemaphoreType.DMA((n, 2))] + [pltpu.SemaphoreType.DMA((n, 3))] * 4,
    )(*shards)
    return [(outs[2 * k], outs[2 * k + 1]) for k in range(n)]


def _sibling_swap_layers(g0s, g1s):
    n = len(g0s)

    def body(*refs):
        ins0, ins1, outs = refs[:n], refs[n:2 * n], refs[2 * n:3 * n]
        ssem, rsem = refs[3 * n:]
        x, y, c, _, _, _ = _position()
        for l in (0, 1):
            @pl.when(c == l)
            def _():
                srcs = ins1 if l == 0 else ins0
                cps = [_remote(srcs[k], outs[k], ssem.at[k], rsem.at[k], (x, y, 1 - c)) for k in range(n)]
                for cp in cps:
                    cp.start()
                for cp in cps:
                    cp.wait()

    return pl.pallas_call(
        body, name="grad_pair_swap", in_specs=[ANY] * (2 * n), out_specs=[ANY] * n,
        out_shape=[jax.ShapeDtypeStruct(a.shape, a.dtype) for a in g0s],
        scratch_shapes=[pltpu.SemaphoreType.DMA((n,))] * 2,
    )(*g0s, *g1s)


def _chip_exchange(parts):
    n = len(parts)

    def body(*refs):
        ins, outs = refs[:n], refs[n:2 * n]
        lsem, ssem, rsem = refs[2 * n:]
        x, y, c, q, chips, qs = _position()
        cps = []
        for k in range(n):
            cp = pltpu.make_async_copy(ins[k].at[q], outs[k].at[q], lsem.at[k])
            cp.start()
            cps.append(cp)
            for j, chip in enumerate(chips):
                cp = _remote(ins[k].at[qs[j]], outs[k].at[q], ssem.at[k, j], rsem.at[k, j], (chip[0], chip[1], c))
                cp.start()
                cps.append(cp)
        for k in range(n):
            for j, chip in enumerate(chips):
                _remote(ins[k].at[qs[j]], outs[k].at[qs[j]], ssem.at[k, j], rsem.at[k, j],
                        (chip[0], chip[1], c)).wait_recv()
        for k in range(n):
            cps[4 * k].wait()
            for j in range(3):
                cps[4 * k + 1 + j].wait_send()

    return pl.pallas_call(
        body, name="grad_chip_exchange", in_specs=[ANY] * n, out_specs=[ANY] * n,
        out_shape=[jax.ShapeDtypeStruct(a.shape, a.dtype) for a in parts],
        scratch_shapes=[pltpu.SemaphoreType.DMA((n,))] + [pltpu.SemaphoreType.DMA((n, 3))] * 2,
    )(*parts)


def _sibling_join_layers(halves):
    n = len(halves)

    def body(*refs):
        ins, outs = refs[:n], refs[n:2 * n]
        lsem, ssem, rsem = refs[2 * n:]
        x, y, c, _, _, _ = _position()
        loc, rem = [], []
        for k in range(n):
            cp = pltpu.make_async_copy(ins[k], outs[k].at[c], lsem.at[k])
            cp.start()
            loc.append(cp)
            cp = _remote(ins[k], outs[k].at[c], ssem.at[k], rsem.at[k], (x, y, 1 - c))
            cp.start()
            rem.append(cp)
        for k in range(n):
            _remote(ins[k], outs[k].at[1 - c], ssem.at[k], rsem.at[k], (x, y, 1 - c)).wait_recv()
        for k in range(n):
            rem[k].wait_send()
            loc[k].wait()

    return pl.pallas_call(
        body, name="grad_pair_join", in_specs=[ANY] * n, out_specs=[ANY] * n,
        out_shape=[jax.ShapeDtypeStruct((2,) + a.shape, a.dtype) for a in halves],
        scratch_shapes=[pltpu.SemaphoreType.DMA((n,))] * 3,
    )(*halves)


def _all_to_all_small(part):
    P = part.shape[0]

    def body(in_ref, out_ref, lsem, ssem, rsem):
        x, y, c = lax.axis_index("x"), lax.axis_index("y"), lax.axis_index("c")
        me = 4 * x + 2 * y + c
        flips = [(fx, fy, fc) for fx in (0, 1) for fy in (0, 1) for fc in (0, 1)][1:]
        peers = [((x + fx) % 2, (y + fy) % 2, (c + fc) % 2) for fx, fy, fc in flips]
        loc = pltpu.make_async_copy(in_ref, out_ref.at[me], lsem)
        loc.start()
        cps = [_remote(in_ref, out_ref.at[me], ssem.at[j], rsem.at[j], peer) for j, peer in enumerate(peers)]
        for cp in cps:
            cp.start()
        for j, (px, py, pc) in enumerate(peers):
            _remote(in_ref, out_ref.at[4 * px + 2 * py + pc], ssem.at[j], rsem.at[j], peers[j]).wait_recv()
        for cp in cps:
            cp.wait_send()
        loc.wait()

    return pl.pallas_call(
        body, name="small_grad_exchange", in_specs=[ANY], out_specs=ANY,
        out_shape=jax.ShapeDtypeStruct((8, P, LANES), F32),
        scratch_shapes=[pltpu.SemaphoreType.DMA(())] + [pltpu.SemaphoreType.DMA((7,))] * 2,
    )(part)


def _row_tile(rows, width, n_arrays):
    t = rows
    while t % 2 == 0 and t > 8 and 2 * n_arrays * t * width * 4 > VMEM_LIMIT // 2:
        t //= 2
    return t


def _pair_sum(g0, g1, recv, name):
    shape = g0.shape
    rows, width = shape[0] * shape[1], shape[2]
    tr = _row_tile(rows, width, 4)

    def body(g0_ref, g1_ref, r_ref, o_ref):
        c = lax.axis_index("c")

        @pl.when(c == 0)
        def _():
            o_ref[...] = (g0_ref[...].astype(F32) + r_ref[...].astype(F32)).astype(BF)

        @pl.when(c == 1)
        def _():
            o_ref[...] = (g1_ref[...].astype(F32) + r_ref[...].astype(F32)).astype(BF)

    blk = pl.BlockSpec((tr, width), lambda i: (i, 0))
    out = pl.pallas_call(
        body, name=name, grid=(rows // tr,), in_specs=[blk] * 3, out_specs=blk,
        out_shape=jax.ShapeDtypeStruct((rows, width), BF), compiler_params=_params(("parallel",)),
    )(g0.reshape(rows, width), g1.reshape(rows, width), recv.reshape(rows, width))
    return out.reshape(shape)


def _sum_slices(a, name):
    n, rows, width = a.shape
    tr = _row_tile(rows, width, n + 1)

    def body(a_ref, o_ref):
        acc = a_ref[0].astype(F32)
        for i in range(1, n):
            acc = acc + a_ref[i].astype(F32)
        o_ref[...] = acc

    return pl.pallas_call(
        body, name=name, grid=(rows // tr,), in_specs=[pl.BlockSpec((n, tr, width), lambda i: (0, i, 0))],
        out_specs=pl.BlockSpec((tr, width), lambda i: (i, 0)), out_shape=jax.ShapeDtypeStruct((rows, width), F32),
        compiler_params=_params(("parallel",)),
    )(a)


def _adamw(w, g, m, v, name):
    rows, width = w.shape
    tr = _row_tile(rows, width, 7)
    c1 = 1.0 - ADAM_B1 ** ADAM_STEP
    c2 = 1.0 - ADAM_B2 ** ADAM_STEP

    def body(w_ref, g_ref, m_ref, v_ref, d_ref, nm_ref, nv_ref):
        gv = g_ref[...]
        nm = ADAM_B1 * m_ref[...] + (1.0 - ADAM_B1) * gv
        nv = ADAM_B2 * v_ref[...] + (1.0 - ADAM_B2) * (gv * gv)
        nm_ref[...] = nm
        nv_ref[...] = nv
        d_ref[...] = -ADAM_LR * ((nm / c1) / (jnp.sqrt(nv / c2) + ADAM_EPS) + ADAM_WD * w_ref[...])

    blk = pl.BlockSpec((tr, width), lambda i: (i, 0))
    return pl.pallas_call(
        body, name=name, grid=(rows // tr,), in_specs=[blk] * 4, out_specs=[blk] * 3,
        out_shape=[jax.ShapeDtypeStruct((rows, width), F32)] * 3, compiler_params=_params(("parallel",)),
    )(w, g, m, v)


BIG = ("w_in", "w_pool_up", "w_conv_out", "w_attn_up", "w_o", "w_ff1", "w_ff2")
SMALL = ("norm_mix", "b_gate", "pool_mix", "pool_scale", "conv_w", "q_gain", "k_gain", "norm_mlp")
ORDER = ("norm_mix", "w_in", "b_gate", "pool_mix", "pool_scale", "conv_w", "q_gain", "k_gain", "w_pool_up",
         "w_conv_out", "w_attn_up", "w_o", "norm_mlp", "w_ff1", "w_ff2")
COLUMN_SHARDED = ("w_in", "w_pool_up", "w_conv_out", "w_attn_up", "w_ff1")


def _layer_weights(l, gathered, small, q):
    w = {}
    for name in BIG:
        g4 = gathered[name][l]
        if name == "w_in":
            w[name] = jnp.transpose(g4, (1, 0, 2)).reshape(g4.shape[1], N_CHIPS * g4.shape[2])
        elif name in COLUMN_SHARDED:
            w[name] = g4
        else:
            w[name] = g4.reshape(N_CHIPS * g4.shape[1], g4.shape[2])
    w["norm_mix"] = small["norm_mix"][l][None]
    w["norm_mlp"] = small["norm_mlp"][l][None]
    w["b_gate"] = small["b_gate"][l][None]
    w["pool_mix"] = small["pool_mix"][l].astype(BF)
    w["pool_scale"] = small["pool_scale"][l][None]
    w["conv_w"] = jnp.pad(small["conv_w_full"][l], ((0, 5), (0, 0)))
    w["q_gain"] = jnp.tile(small["q_gain"][l], 2)[None]
    w["k_gain"] = jnp.tile(small["k_gain"][l], 2)[None]
    return w


def _to_chip_major(name, g):
    if name == "w_in":
        return jnp.transpose(g.reshape(g.shape[0], N_CHIPS, g.shape[1] // N_CHIPS), (1, 0, 2))
    if name in COLUMN_SHARDED:
        return g
    return g.reshape(N_CHIPS, g.shape[0] // N_CHIPS, g.shape[1])


def _pad8(a):
    a = a.reshape(-1, LANES)
    return jnp.pad(a, ((0, (-a.shape[0]) % 8), (0, 0)))


def kernel(x, norm_mix, w_in, b_gate, pool_mix, pool_scale, conv_w, q_gain, k_gain, w_pool_up, w_conv_out, w_attn_up, w_o, norm_mlp, w_ff1, w_ff2, loss_target, m_norm_mix, m_w_in, m_b_gate, m_pool_mix, m_pool_scale, m_conv_w, m_q_gain, m_k_gain, m_w_pool_up, m_w_conv_out, m_w_attn_up, m_w_o, m_norm_mlp, m_w_ff1, m_w_ff2, v_norm_mix, v_w_in, v_b_gate, v_pool_mix, v_pool_scale, v_conv_w, v_q_gain, v_k_gain, v_w_pool_up, v_w_conv_out, v_w_attn_up, v_w_o, v_norm_mlp, v_w_ff1, v_w_ff2):
    weights = dict(norm_mix=norm_mix, w_in=w_in, b_gate=b_gate, pool_mix=pool_mix, pool_scale=pool_scale, conv_w=conv_w,
                   q_gain=q_gain, k_gain=k_gain, w_pool_up=w_pool_up, w_conv_out=w_conv_out, w_attn_up=w_attn_up,
                   w_o=w_o, norm_mlp=norm_mlp, w_ff1=w_ff1, w_ff2=w_ff2)
    moms = dict(norm_mix=m_norm_mix, w_in=m_w_in, b_gate=m_b_gate, pool_mix=m_pool_mix, pool_scale=m_pool_scale,
                conv_w=m_conv_w, q_gain=m_q_gain, k_gain=m_k_gain, w_pool_up=m_w_pool_up, w_conv_out=m_w_conv_out,
                w_attn_up=m_w_attn_up, w_o=m_w_o, norm_mlp=m_norm_mlp, w_ff1=m_w_ff1, w_ff2=m_w_ff2)
    vels = dict(norm_mix=v_norm_mix, w_in=v_w_in, b_gate=v_b_gate, pool_mix=v_pool_mix, pool_scale=v_pool_scale,
                conv_w=v_conv_w, q_gain=v_q_gain, k_gain=v_k_gain, w_pool_up=v_w_pool_up, w_conv_out=v_w_conv_out,
                w_attn_up=v_w_attn_up, w_o=v_w_o, norm_mlp=v_norm_mlp, w_ff1=v_w_ff1, w_ff2=v_w_ff2)
    depth = norm_mix.shape[0]
    assert depth == 2, "a core is in charge of one of the two layers' gradients"
    q = 2 * lax.axis_index("x") + lax.axis_index("y")

    gathered = dict(zip(BIG, _gather_weights([weights[n].astype(BF) for n in BIG])))
    cw_all = _all_to_all_small(_pad8(jnp.pad(conv_w.reshape(-1), (0, (-conv_w.size) % LANES))))
    conv_w_full = jnp.concatenate(
        [cw_all[2 * p].reshape(-1)[:conv_w.size].reshape(conv_w.shape) for p in range(N_CHIPS)], axis=-1)
    small = dict(weights)
    small["conv_w_full"] = conv_w_full

    h = x[0]
    saved, wl = [], []
    for l in range(depth):
        wl.append(_layer_weights(l, gathered, small, q))
        h, s = _layer_fwd(h, wl[l], f"l{l}")
        saved.append(s)
    dh, loss_row = _loss_grad(h, loss_target[0], "loss")
    grads = [None] * depth
    for l in reversed(range(depth)):
        dh, grads[l] = _layer_bwd(dh, wl[l], saved[l], f"l{l}")
    loss = lax.psum(loss_row[0, 0], ("x", "y", "c"))

    g0s = [_to_chip_major(n, grads[0][n]) for n in BIG]
    g1s = [_to_chip_major(n, grads[1][n]) for n in BIG]
    from_sibling = _sibling_swap_layers(g0s, g1s)
    chip_part = [_pair_sum(g0s[k], g1s[k], from_sibling[k], f"pair_sum_{n}") for k, n in enumerate(BIG)]
    from_chips = _chip_exchange(chip_part)
    halves = [_sum_slices(from_chips[k], f"chip_sum_{n}") for k, n in enumerate(BIG)]
    full = dict(zip(BIG, _sibling_join_layers(halves)))

    pieces = []
    for n in SMALL:
        per_layer = [grads[l][n] for l in range(depth)]
        if n == "conv_w":
            per_layer = [p[:3] for p in per_layer]
        pieces.append(_pad8(jnp.stack(per_layer).reshape(-1)))
    packed = jnp.concatenate(pieces, axis=0)
    summed = _sum_slices(_all_to_all_small(packed), "small_sum")
    row = 0
    for n, piece in zip(SMALL, pieces):
        size = weights[n].size if n != "conv_w" else depth * 3 * 512
        flat = summed[row:row + piece.shape[0]].reshape(-1)[:size]
        row += piece.shape[0]
        if n == "conv_w":
            full[n] = lax.dynamic_slice_in_dim(flat.reshape(depth, 3, 512), q * conv_w.shape[2], conv_w.shape[2], axis=2)
        else:
            full[n] = flat.reshape(weights[n].shape)

    deltas, new_m, new_v = {}, {}, {}
    for n in ORDER:
        shape = weights[n].shape
        two_d = (-1, shape[-1]) if n not in ("conv_w", "q_gain", "k_gain") else (1, -1)
        d2, m2, v2 = _adamw(weights[n].reshape(two_d), full[n].reshape(two_d), moms[n].reshape(two_d),
                            vels[n].reshape(two_d), f"adamw_{n}")
        deltas[n], new_m[n], new_v[n] = d2.reshape(shape), m2.reshape(shape), v2.reshape(shape)
        full[n] = full[n].reshape(shape)
    return (loss, dh[None], *[full[n] for n in ORDER], *[deltas[n] for n in ORDER], *[new_m[n] for n in ORDER],
            *[new_v[n] for n in ORDER])
```

```python
import functools

import jax
import jax.numpy as jnp
from jax import lax
from jax.experimental import pallas as pl
from jax.experimental.pallas import tpu as pltpu

F32 = jnp.float32
BF = jnp.bfloat16
MESH_ID = pl.DeviceIdType.MESH
ANY = pl.BlockSpec(memory_space=pl.ANY)

EPS = 1e-6
MASK_VALUE = -1e30
POOL_WINDOWS = (2, 4, 8, 16)
ATTN_DILATIONS = (1, 4, 16)
ATTN_BLOCK = 128
HEAD_DIM = 64
OFF_Q, OFF_K, OFF_V, OFF_GATE = 2048, 2816, 3584, 4352
N_CHIPS = 4
ADAM_LR, ADAM_B1, ADAM_B2, ADAM_EPS, ADAM_WD, ADAM_STEP = 0.001, 0.9, 0.999, 1e-08, 0.01, 10

VMEM_LIMIT = 48 * 1024 * 1024
LANES = 128

_DIMS = {"nn": (((1,), (0,)), ((), ())), "nt": (((1,), (1,)), ((), ())), "tn": (((0,), (0,)), ((), ()))}


def _params(sem):
    return pltpu.CompilerParams(dimension_semantics=sem, vmem_limit_bytes=VMEM_LIMIT)


def _dot(a, b, mode="nn"):
    return lax.dot_general(a, b, _DIMS[mode], preferred_element_type=F32)


def _mm(a, b, mode, name, *, tm, tn, tk, out_dtype=F32, res=None, aux=None, epi=None, n_outer=False,
        b_shards=False, out_shards=False):
    if mode == "tn":
        K, M = a.shape
    else:
        M, K = a.shape
    if b_shards:
        if mode == "nn":
            assert b.shape[1] == K
            N = b.shape[2] * N_CHIPS
        else:
            assert mode == "nt"
            N = b.shape[1]
            assert b.shape[2] * N_CHIPS == K
    else:
        N = b.shape[0] if mode == "nt" else b.shape[1]
    tm, tn, tk = min(tm, M), min(tn, N), min(tk, K)
    assert M % tm == 0 and N % tn == 0 and K % tk == 0
    nk = K // tk
    if n_outer:
        grid = (N // tn, M // tm, nk)
        ij = lambda p, q_: (q_, p)
    else:
        grid = (M // tm, N // tn, nk)
        ij = lambda p, q_: (p, q_)

    def amap(p, q_, k):
        i, j = ij(p, q_)
        return (k, i) if mode == "tn" else (i, k)

    a_spec = pl.BlockSpec((tk, tm) if mode == "tn" else (tm, tk), amap)
    if b_shards:
        if mode == "nn":
            per = (N // N_CHIPS) // tn
            assert per >= 1 and (N // N_CHIPS) % tn == 0

            def bmap(p, q_, k):
                i, j = ij(p, q_)
                return (j // per, k, j % per)

            b_spec = pl.BlockSpec((None, tk, tn), bmap)
        else:
            per = (K // N_CHIPS) // tk
            assert per >= 1 and (K // N_CHIPS) % tk == 0

            def bmap(p, q_, k):
                i, j = ij(p, q_)
                return (k // per, j, k % per)

            b_spec = pl.BlockSpec((None, tn, tk), bmap)
    else:
        def bmap(p, q_, k):
            i, j = ij(p, q_)
            return (j, k) if mode == "nt" else (k, j)

        b_spec = pl.BlockSpec((tn, tk) if mode == "nt" else (tk, tn), bmap)

    def omap(p, q_, k):
        return ij(p, q_)

    o_spec = pl.BlockSpec((tm, tn), omap)
    if out_shards:
        per_o = (N // N_CHIPS) // tn
        assert per_o >= 1 and (N // N_CHIPS) % tn == 0

        def osmap(p, q_, k):
            i, j = ij(p, q_)
            return (j // per_o, i, j % per_o)

        out_spec0 = pl.BlockSpec((None, tm, tn), osmap)
        out_shape0 = jax.ShapeDtypeStruct((N_CHIPS, M, N // N_CHIPS), out_dtype)
    else:
        out_spec0 = o_spec
        out_shape0 = jax.ShapeDtypeStruct((M, N), out_dtype)

    in_specs = [a_spec, b_spec]
    args = [a, b]
    if res is not None:
        in_specs.append(o_spec)
        args.append(res)
    if aux is not None:
        in_specs.append(o_spec)
        args.append(aux)
    out_specs = [out_spec0]
    out_shape = [out_shape0]
    if epi == "relu2":
        out_specs.append(o_spec)
        out_shape.append(jax.ShapeDtypeStruct((M, N), BF))
    n_out = len(out_shape)
    has_res, has_aux = res is not None, aux is not None

    def body(*refs):
        a_ref, b_ref = refs[0], refs[1]
        pos = 2
        res_ref = aux_ref = None
        if has_res:
            res_ref = refs[pos]
            pos += 1
        if has_aux:
            aux_ref = refs[pos]
            pos += 1
        outs = refs[pos:pos + n_out]
        part = _dot(a_ref[...].astype(BF), b_ref[...].astype(BF), mode)

        def finish(acc):
            if res_ref is not None:
                acc = res_ref[...] + acc
            if epi == "relu2":
                outs[0][...] = acc
                r = jnp.maximum(acc, 0.0)
                outs[1][...] = (r * r).astype(BF)
            elif epi == "drelu2":
                outs[0][...] = (acc * (2.0 * jnp.maximum(aux_ref[...], 0.0))).astype(out_dtype)
            else:
                outs[0][...] = acc.astype(out_dtype)

        if nk == 1:
            finish(part)
        else:
            acc_ref = refs[pos + n_out]
            k = pl.program_id(2)

            @pl.when(k == 0)
            def _():
                acc_ref[...] = part

            @pl.when(k > 0)
            def _():
                acc_ref[...] += part

            @pl.when(k == nk - 1)
            def _():
                finish(acc_ref[...])

    scratch = [pltpu.VMEM((tm, tn), F32)] if nk > 1 else []
    out = pl.pallas_call(
        body, name=name, grid=grid, in_specs=in_specs, out_specs=out_specs, out_shape=out_shape,
        scratch_shapes=scratch, compiler_params=_params(("parallel", "parallel", "arbitrary")),
    )(*args)
    return out if n_out > 1 else out[0]


def _rms_fwd(x, gain, name):
    T, D = x.shape
    tm = min(512, T)

    def body(x_ref, g_ref, o_ref):
        xv = x_ref[...]
        r = lax.rsqrt(jnp.mean(xv * xv, axis=-1, keepdims=True) + EPS)
        o_ref[...] = ((xv * r) * g_ref[...]).astype(BF)

    return pl.pallas_call(
        body, name=name, grid=(T // tm,),
        in_specs=[pl.BlockSpec((tm, D), lambda i: (i, 0)), pl.BlockSpec((1, D), lambda i: (0, 0))],
        out_specs=pl.BlockSpec((tm, D), lambda i: (i, 0)), out_shape=jax.ShapeDtypeStruct((T, D), BF),
        compiler_params=_params(("parallel",)),
    )(x, gain)


def _rms_bwd(dh, x, gain, dres, name):
    T, D = x.shape
    tm = min(512, T)

    def body(dh_ref, x_ref, g_ref, dres_ref, dx_ref, dg_ref):
        xv = x_ref[...]
        r = lax.rsqrt(jnp.mean(xv * xv, axis=-1, keepdims=True) + EPS)
        xhat = xv * r
        dhv = dh_ref[...]
        dy = dhv * g_ref[...]
        dx_ref[...] = dres_ref[...] + r * (dy - xhat * jnp.mean(dy * xhat, axis=-1, keepdims=True))

        @pl.when(pl.program_id(0) == 0)
        def _():
            dg_ref[...] = jnp.zeros_like(dg_ref)

        dg_ref[...] += jnp.sum(dhv * xhat, axis=0, keepdims=True)

    row = pl.BlockSpec((tm, D), lambda i: (i, 0))
    vec = pl.BlockSpec((1, D), lambda i: (0, 0))
    return pl.pallas_call(
        body, name=name, grid=(T // tm,), in_specs=[row, row, vec, row], out_specs=[row, vec],
        out_shape=[jax.ShapeDtypeStruct((T, D), F32), jax.ShapeDtypeStruct((1, D), F32)],
        compiler_params=_params(("arbitrary",)),
    )(dh, x, gain, dres)


def _loss_grad(y, target, name):
    T, D = y.shape
    tm = min(512, T)

    def body(y_ref, t_ref, dy_ref, l_ref):
        e = y_ref[...] - t_ref[...]
        dy_ref[...] = e / float(D)

        @pl.when(pl.program_id(0) == 0)
        def _():
            l_ref[...] = jnp.zeros_like(l_ref)

        l_ref[...] += 0.5 * jnp.sum(jnp.mean(e * e, axis=-1, keepdims=True))

    row = pl.BlockSpec((tm, D), lambda i: (i, 0))
    return pl.pallas_call(
        body, name=name, grid=(T // tm,), in_specs=[row, row],
        out_specs=[row, pl.BlockSpec((1, LANES), lambda i: (0, 0))],
        out_shape=[jax.ShapeDtypeStruct((T, D), F32), jax.ShapeDtypeStruct((1, LANES), F32)],
        compiler_params=_params(("arbitrary",)),
    )(y, target)


POOL_HALO = 16
CONV_HALO = 8


def _causal_window_sum(v, w):
    s, sh = v, 1
    while sh < w:
        s = s + pltpu.roll(s, sh, 0)
        sh *= 2
    return s


def _anticausal_window_sum(v, w):
    n = v.shape[0]
    s, sh = v, 1
    while sh < w:
        s = s + pltpu.roll(s, n - sh, 0)
        sh *= 2
    return s


def _poolconv_fwd(z, pmix_b, pscale, convw, name):
    T = z.shape[0]
    R = min(512, T)
    PH, CH = R // POOL_HALO, R // CONV_HALO

    def body(u_ref, uh_ref, b_ref, c_ref, ch_ref, x_ref, xh_ref, mix_ref, sc_ref, cw_ref, yp_ref, yc_ref):
        i = pl.program_id(0)
        keep = (i > 0).astype(F32)
        row = i * R + lax.broadcasted_iota(jnp.int32, (R, 1), 0)
        w_all = jnp.concatenate([uh_ref[...] * keep, u_ref[...]], axis=0)
        for g, w in enumerate(POOL_WINDOWS):
            cols = slice(128 * g, 128 * (g + 1))
            wg = w_all[:, cols]
            s = _causal_window_sum(wg, w)[POOL_HALO:]
            cnt = jnp.minimum(row + 1, w).astype(F32)
            dgrp = s / cnt - wg[POOL_HALO:]
            y = _dot(dgrp.astype(BF), mix_ref[g]) * sc_ref[:, cols]
            yp_ref[:, cols] = y.astype(BF)
        uc = jnp.concatenate([ch_ref[...] * xh_ref[...] * keep, c_ref[...] * x_ref[...]], axis=0)
        yc = cw_ref[2:3, :] * uc + cw_ref[0:1, :] * pltpu.roll(uc, 2, 0) + cw_ref[1:2, :] * pltpu.roll(uc, 1, 0)
        yc_ref[...] = (b_ref[...] * yc[CONV_HALO:]).astype(BF)

    def main(cb):
        return pl.BlockSpec((R, 512), lambda i: (i, cb))

    def prev(cb, halo, per):
        return pl.BlockSpec((halo, 512), lambda i: (jnp.maximum(i * per - 1, 0), cb))

    full = lambda a: pl.BlockSpec(a.shape, lambda i: (0,) * a.ndim)
    return pl.pallas_call(
        body, name=name, grid=(T // R,),
        in_specs=[main(0), prev(0, POOL_HALO, PH), main(1), main(2), prev(2, CONV_HALO, CH), main(3),
                  prev(3, CONV_HALO, CH), full(pmix_b), full(pscale), full(convw)],
        out_specs=[pl.BlockSpec((R, 512), lambda i: (i, 0))] * 2,
        out_shape=[jax.ShapeDtypeStruct((T, 512), BF)] * 2,
        compiler_params=_params(("parallel",)),
    )(z, z, z, z, z, z, z, pmix_b, pscale, convw)


def _poolconv_bwd(z, dyp, dyc, pmix_b, pscale, convw, name):
    T = z.shape[0]
    R = min(512, T)
    PH, CH = R // POOL_HALO, R // CONV_HALO
    nsteps = T // R

    def body(u_ref, uh_ref, b_ref, bn_ref, c_ref, ch_ref, x_ref, xh_ref, dyp_ref, dypn_ref, dyc_ref, dycn_ref,
             mix_ref, sc_ref, cw_ref, dz_ref, dmix_ref, dsc_ref, dcw_ref):
        i = pl.program_id(0)
        keep_prev = (i > 0).astype(F32)
        keep_next = (i < nsteps - 1).astype(F32)

        @pl.when(i == 0)
        def _():
            dmix_ref[...] = jnp.zeros_like(dmix_ref)
            dsc_ref[...] = jnp.zeros_like(dsc_ref)
            dcw_ref[...] = jnp.zeros_like(dcw_ref)

        row = i * R + lax.broadcasted_iota(jnp.int32, (R, 1), 0)
        row_ext = i * R + lax.broadcasted_iota(jnp.int32, (R + POOL_HALO, 1), 0)
        w_all = jnp.concatenate([uh_ref[...] * keep_prev, u_ref[...]], axis=0)
        dyp_ext = jnp.concatenate([dyp_ref[...], dypn_ref[...] * keep_next], axis=0)
        for g, w in enumerate(POOL_WINDOWS):
            cols = slice(128 * g, 128 * (g + 1))
            wg = w_all[:, cols]
            s = _causal_window_sum(wg, w)[POOL_HALO:]
            cnt = jnp.minimum(row + 1, w).astype(F32)
            dgrp = (s / cnt - wg[POOL_HALO:]).astype(BF)
            y_pre = _dot(dgrp, mix_ref[g])
            dsc_ref[:, cols] += jnp.sum(dyp_ref[:, cols] * y_pre, axis=0, keepdims=True)
            dyb = (dyp_ext[:, cols] * sc_ref[:, cols]).astype(BF)
            dmix_ref[cols, :] += _dot(dgrp, dyb[:R], "tn")
            dd = _dot(dyb, mix_ref[g], "nt")
            cnt_ext = jnp.minimum(row_ext + 1, w).astype(F32)
            e = _anticausal_window_sum(dd / cnt_ext, w)
            dz_ref[:, cols] = (e[:R] - dd[:R]).astype(BF)
        cw0, cw1, cw2 = cw_ref[0:1, :], cw_ref[1:2, :], cw_ref[2:3, :]
        uc = jnp.concatenate([ch_ref[...] * xh_ref[...] * keep_prev, c_ref[...] * x_ref[...]], axis=0)
        uc1 = pltpu.roll(uc, 1, 0)[CONV_HALO:]
        uc2 = pltpu.roll(uc, 2, 0)[CONV_HALO:]
        uc0 = uc[CONV_HALO:]
        yc = cw2 * uc0 + cw0 * uc2 + cw1 * uc1
        dycv = dyc_ref[...]
        dz_ref[:, 512:1024] = (dycv * yc).astype(BF)
        dv_ext = jnp.concatenate([dycv * b_ref[...], dycn_ref[...] * bn_ref[...] * keep_next], axis=0)
        n_ext = R + CONV_HALO
        duc = (cw2 * dv_ext + cw1 * pltpu.roll(dv_ext, n_ext - 1, 0) + cw0 * pltpu.roll(dv_ext, n_ext - 2, 0))[:R]
        dv = dv_ext[:R]
        dcw_ref[0:1, :] += jnp.sum(dv * uc2, axis=0, keepdims=True)
        dcw_ref[1:2, :] += jnp.sum(dv * uc1, axis=0, keepdims=True)
        dcw_ref[2:3, :] += jnp.sum(dv * uc0, axis=0, keepdims=True)
        dz_ref[:, 1024:1536] = (duc * x_ref[...]).astype(BF)
        dz_ref[:, 1536:2048] = (duc * c_ref[...]).astype(BF)

    def main(cb):
        return pl.BlockSpec((R, 512), lambda i: (i, cb))

    def prev(cb, halo, per):
        return pl.BlockSpec((halo, 512), lambda i: (jnp.maximum(i * per - 1, 0), cb))

    def nxt(cb, halo, per):
        return pl.BlockSpec((halo, 512), lambda i: (jnp.minimum((i + 1) * per, T // halo - 1), cb))

    full = lambda a: pl.BlockSpec(a.shape, lambda i: (0,) * a.ndim)
    return pl.pallas_call(
        body, name=name, grid=(nsteps,),
        in_specs=[main(0), prev(0, POOL_HALO, PH), main(1), nxt(1, CONV_HALO, CH), main(2), prev(2, CONV_HALO, CH),
                  main(3), prev(3, CONV_HALO, CH), main(0), nxt(0, POOL_HALO, PH), main(0), nxt(0, CONV_HALO, CH),
                  full(pmix_b), full(pscale), full(convw)],
        out_specs=[pl.BlockSpec((R, 2048), lambda i: (i, 0)), pl.BlockSpec((512, 128), lambda i: (0, 0)),
                   pl.BlockSpec((1, 512), lambda i: (0, 0)), pl.BlockSpec((8, 512), lambda i: (0, 0))],
        out_shape=[jax.ShapeDtypeStruct((T, 2048), BF), jax.ShapeDtypeStruct((512, 128), F32),
                   jax.ShapeDtypeStruct((1, 512), F32), jax.ShapeDtypeStruct((8, 512), F32)],
        compiler_params=_params(("arbitrary",)),
    )(z, z, z, z, z, z, z, z, dyp, dyp, dyc, dyc, pmix_b, pscale, convw)


def _head_norm(x, g2, ma):
    sq = x * x
    sa = jnp.sum(jnp.where(ma, sq, 0.0), axis=-1, keepdims=True)
    sb = jnp.sum(jnp.where(ma, 0.0, sq), axis=-1, keepdims=True)
    r = jnp.where(ma, lax.rsqrt(sa / HEAD_DIM + EPS), lax.rsqrt(sb / HEAD_DIM + EPS))
    return x * r, r


def _head_norm_bwd(dy, xhat, r, g2, ma):
    dxh = dy * g2
    pr = dxh * xhat
    sa = jnp.sum(jnp.where(ma, pr, 0.0), axis=-1, keepdims=True)
    sb = jnp.sum(jnp.where(ma, 0.0, pr), axis=-1, keepdims=True)
    mh = jnp.where(ma, sa, sb) / HEAD_DIM
    return r * (dxh - xhat * mh)


def _head_col(tile, hm):
    return jnp.max(jnp.where(hm, tile, -jnp.inf), axis=-1, keepdims=True)


def _attn_masks(other_block_exists):
    lane = lax.broadcasted_iota(jnp.int32, (ATTN_BLOCK, ATTN_BLOCK), 1)
    qi = lax.broadcasted_iota(jnp.int32, (ATTN_BLOCK, ATTN_BLOCK), 0)
    never = (1 - other_block_exists.astype(jnp.int32)) * (2 * ATTN_BLOCK)
    return lane < HEAD_DIM, lane <= qi, lane >= qi + never


def _attn_fwd(qf, kf, vf, gq2, gk2, d, name):
    L = qf.shape[0]
    nb = L // ATTN_BLOCK
    scale = HEAD_DIM ** -0.5

    def body(q_ref, kc_ref, kp_ref, vc_ref, vp_ref, gq_ref, gk_ref, o_ref, lse_ref):
        j = pl.program_id(1)
        ma, mask_c, mask_p = _attn_masks(j > 0)
        for t in range(2):
            sl = slice(LANES * t, LANES * (t + 1))
            qn = _head_norm(q_ref[:, sl], gq_ref[...], ma)[0] * gq_ref[...]
            kcb = (_head_norm(kc_ref[:, sl], gk_ref[...], ma)[0] * gk_ref[...]).astype(BF)
            kpb = (_head_norm(kp_ref[:, sl], gk_ref[...], ma)[0] * gk_ref[...]).astype(BF)
            vcb = vc_ref[:, sl].astype(BF)
            vpb = vp_ref[:, sl].astype(BF)
            o_t = lse_t = None
            for hm in (ma, jnp.logical_not(ma)):
                qh = jnp.where(hm, qn, 0.0).astype(BF)
                s_c = jnp.where(mask_c, _dot(qh, kcb, "nt") * scale, MASK_VALUE)
                s_p = jnp.where(mask_p, _dot(qh, kpb, "nt") * scale, MASK_VALUE)
                m = jnp.maximum(jnp.max(s_c, axis=-1, keepdims=True), jnp.max(s_p, axis=-1, keepdims=True))
                p_c = jnp.exp(s_c - m)
                p_p = jnp.exp(s_p - m)
                den = jnp.sum(p_c, axis=-1, keepdims=True) + jnp.sum(p_p, axis=-1, keepdims=True)
                o = (_dot(p_c.astype(BF), vcb) + _dot(p_p.astype(BF), vpb)) / den
                lse = jnp.broadcast_to(m + jnp.log(den), o.shape)
                o_t = o if o_t is None else jnp.where(ma, o_t, o)
                lse_t = lse if lse_t is None else jnp.where(ma, lse_t, lse)
            o_ref[:, sl] = o_t
            lse_ref[:, sl] = lse_t

    cur = pl.BlockSpec((ATTN_BLOCK, 256), lambda r, j: (j, r))
    prv = pl.BlockSpec((ATTN_BLOCK, 256), lambda r, j: (jnp.maximum(j - 1, 0), r))
    vec = pl.BlockSpec((1, LANES), lambda r, j: (0, 0))
    return pl.pallas_call(
        body, name=name, grid=(d, nb), in_specs=[cur, cur, prv, cur, prv, vec, vec], out_specs=[cur, cur],
        out_shape=[jax.ShapeDtypeStruct(qf.shape, F32)] * 2,
        compiler_params=_params(("parallel", "parallel")),
    )(qf, kf, kf, vf, vf, gq2, gk2)


def _attn_bwd(qf, kf, vf, dof, cf, lsef, gq2, gk2, d, name):
    L = qf.shape[0]
    nb = L // ATTN_BLOCK
    scale = HEAD_DIM ** -0.5

    def body(q_ref, qn_ref, k_ref, kp_ref, v_ref, vp_ref, do_ref, don_ref, c_ref, cn_ref, lse_ref, lsen_ref,
             gq_ref, gk_ref, dq_ref, dk_ref, dv_ref, dgq_ref, dgk_ref):
        r_id, j = pl.program_id(0), pl.program_id(1)
        ma, mask_c, mask_p = _attn_masks(j > 0)
        mask_n = _attn_masks(j < nb - 1)[2]

        @pl.when((r_id == 0) & (j == 0))
        def _():
            dgq_ref[...] = jnp.zeros_like(dgq_ref)
            dgk_ref[...] = jnp.zeros_like(dgk_ref)

        gq, gk = gq_ref[...], gk_ref[...]
        for t in range(2):
            sl = slice(LANES * t, LANES * (t + 1))
            qhat, rq = _head_norm(q_ref[:, sl], gq, ma)
            qn = qhat * gq
            qn_next = _head_norm(qn_ref[:, sl], gq, ma)[0] * gq
            khat, rk = _head_norm(k_ref[:, sl], gk, ma)
            kcb = (khat * gk).astype(BF)
            kpb = (_head_norm(kp_ref[:, sl], gk, ma)[0] * gk).astype(BF)
            vcb = v_ref[:, sl].astype(BF)
            vpb = vp_ref[:, sl].astype(BF)
            do_t, don_t = do_ref[:, sl], don_ref[:, sl]
            c_t, cn_t = c_ref[:, sl], cn_ref[:, sl]
            lse_t, lsen_t = lse_ref[:, sl], lsen_ref[:, sl]
            dq_t = None
            dk_t = jnp.zeros((ATTN_BLOCK, LANES), F32)
            dv_t = jnp.zeros((ATTN_BLOCK, LANES), F32)
            for hm in (ma, jnp.logical_not(ma)):
                qh = jnp.where(hm, qn, 0.0).astype(BF)
                doh = jnp.where(hm, do_t, 0.0).astype(BF)
                lse_h = _head_col(lse_t, hm)
                c_h = _head_col(c_t, hm)
                s_c = jnp.where(mask_c, _dot(qh, kcb, "nt") * scale, MASK_VALUE)
                s_p = jnp.where(mask_p, _dot(qh, kpb, "nt") * scale, MASK_VALUE)
                p_c = jnp.exp(s_c - lse_h)
                p_p = jnp.exp(s_p - lse_h)
                ds_c = ((p_c * (_dot(doh, vcb, "nt") + c_h)) * scale).astype(BF)
                ds_p = ((p_p * (_dot(doh, vpb, "nt") + c_h)) * scale).astype(BF)
                dq_h = _dot(ds_c, kcb) + _dot(ds_p, kpb)
                dq_t = dq_h if dq_t is None else jnp.where(ma, dq_t, dq_h)
                qh_n = jnp.where(hm, qn_next, 0.0).astype(BF)
                doh_n = jnp.where(hm, don_t, 0.0).astype(BF)
                s_n = jnp.where(mask_n, _dot(qh_n, kcb, "nt") * scale, MASK_VALUE)
                p_n = jnp.exp(s_n - _head_col(lsen_t, hm))
                ds_n = ((p_n * (_dot(doh_n, vcb, "nt") + _head_col(cn_t, hm))) * scale).astype(BF)
                dv_t = dv_t + _dot(p_c.astype(BF), doh, "tn") + _dot(p_n.astype(BF), doh_n, "tn")
                dk_t = dk_t + _dot(ds_c, qh, "tn") + _dot(ds_n, qh_n, "tn")
            dq_ref[:, sl] = _head_norm_bwd(dq_t, qhat, rq, gq, ma).astype(BF)
            dk_ref[:, sl] = _head_norm_bwd(dk_t, khat, rk, gk, ma).astype(BF)
            dv_ref[:, sl] = dv_t.astype(BF)
            dgq_ref[...] += jnp.sum(dq_t * qhat, axis=0, keepdims=True)
            dgk_ref[...] += jnp.sum(dk_t * khat, axis=0, keepdims=True)

    cur = pl.BlockSpec((ATTN_BLOCK, 256), lambda r, j: (j, r))
    prv = pl.BlockSpec((ATTN_BLOCK, 256), lambda r, j: (jnp.maximum(j - 1, 0), r))
    nxt = pl.BlockSpec((ATTN_BLOCK, 256), lambda r, j: (jnp.minimum(j + 1, nb - 1), r))
    vec = pl.BlockSpec((1, LANES), lambda r, j: (0, 0))
    return pl.pallas_call(
        body, name=name, grid=(d, nb),
        in_specs=[cur, nxt, cur, prv, cur, prv, cur, nxt, cur, nxt, cur, nxt, vec, vec],
        out_specs=[cur, cur, cur, vec, vec],
        out_shape=[jax.ShapeDtypeStruct(qf.shape, BF)] * 3 + [jax.ShapeDtypeStruct((1, LANES), F32)] * 2,
        compiler_params=_params(("arbitrary", "arbitrary")),
    )(qf, qf, kf, kf, vf, vf, dof, dof, cf, cf, lsef, lsef, gq2, gk2)


MERGE_ROWS = 256
GATE_TILE = 256


def _group_mix(o_refs, lse_refs):
    lses = [r[...] for r in lse_refs]
    m = jnp.maximum(jnp.maximum(lses[0], lses[1]), lses[2])
    es = [jnp.exp(l - m) for l in lses]
    den = es[0] + es[1] + es[2]
    ws = [e / den for e in es]
    y = ws[0] * o_refs[0][...] + ws[1] * o_refs[1][...] + ws[2] * o_refs[2][...]
    return ws, y


def _sigmoid(v):
    return 1.0 / (1.0 + jnp.exp(-v))


def _merge_specs(T, z, bgate, gpu, gco, gau):
    tm = min(MERGE_ROWS, T)
    row = lambda w: pl.BlockSpec((tm, w), lambda i: (i, 0))
    gate0 = OFF_GATE // GATE_TILE
    gates = [pl.BlockSpec((tm, GATE_TILE), functools.partial(lambda i, cb: (i, cb), cb=gate0 + n))
             for n in range(3 * N_CHIPS)]
    full = lambda a: pl.BlockSpec(a.shape, lambda i: (0,) * a.ndim)
    specs = [row(512), row(512)] + [row(256)] * 6 + gates + [full(bgate), full(gpu), full(gco), full(gau)]
    return tm, row, specs


def _merge_fwd(yp, yc, o3, lse3, z, bgate, gpu, gco, gau, name):
    T = yp.shape[0]
    tm, row, specs = _merge_specs(T, z, bgate, gpu, gco, gau)

    def body(*refs):
        yp_ref, yc_ref = refs[0], refs[1]
        o_refs, lse_refs = refs[2:5], refs[5:8]
        zg = refs[8:20]
        b_ref, gpu_ref, gco_ref, gau_ref, out_ref = refs[20:25]
        yab = _group_mix(o_refs, lse_refs)[1].astype(BF)
        ys = (yp_ref[...], yc_ref[...], yab)
        ups = (gpu_ref, gco_ref, gau_ref)
        for n in range(N_CHIPS):
            acc = None
            for b in range(3):
                gcol = slice(1024 * b + GATE_TILE * n, 1024 * b + GATE_TILE * (n + 1))
                gate = _sigmoid(zg[N_CHIPS * b + n][...] + b_ref[:, gcol])
                term = gate * _dot(ys[b], ups[b][n])
                acc = term if acc is None else acc + term
            out_ref[:, GATE_TILE * n:GATE_TILE * (n + 1)] = acc.astype(BF)

    return pl.pallas_call(
        body, name=name, grid=(T // tm,), in_specs=specs, out_specs=row(1024),
        out_shape=jax.ShapeDtypeStruct((T, 1024), BF), compiler_params=_params(("parallel",)),
    )(yp, yc, *o3, *lse3, *([z] * 12), bgate, gpu, gco, gau)


def _merge_bwd(dm, yp, yc, o3, lse3, z, bgate, gpu, gco, gau, name):
    T = yp.shape[0]
    tm, row, specs = _merge_specs(T, z, bgate, gpu, gco, gau)
    nsteps = T // tm

    def body(*refs):
        dm_ref, yp_ref, yc_ref = refs[0:3]
        o_refs, lse_refs = refs[3:6], refs[6:9]
        zg = refs[9:21]
        b_ref, gpu_ref, gco_ref, gau_ref = refs[21:25]
        dzg_ref, dyp_ref, dyc_ref = refs[25:28]
        do_refs, c_refs = refs[28:31], refs[31:34]
        dgpu_ref, dgco_ref, dgau_ref, dbg_ref = refs[34:38]
        accs = refs[38:41]
        i = pl.program_id(0)

        @pl.when(i == 0)
        def _():
            for a in accs:
                a[...] = jnp.zeros_like(a)
            dbg_ref[...] = jnp.zeros_like(dbg_ref)

        ws, y = _group_mix(o_refs, lse_refs)
        ys = (yp_ref[...], yc_ref[...], y.astype(BF))
        ups = (gpu_ref, gco_ref, gau_ref)
        dys = [None, None, None]
        for n in range(N_CHIPS):
            dmn = dm_ref[:, GATE_TILE * n:GATE_TILE * (n + 1)]
            for b in range(3):
                gcol = slice(1024 * b + GATE_TILE * n, 1024 * b + GATE_TILE * (n + 1))
                gate = _sigmoid(zg[N_CHIPS * b + n][...] + b_ref[:, gcol])
                up = _dot(ys[b], ups[b][n])
                dzg = (dmn * up) * (gate * (1.0 - gate))
                dzg_ref[:, gcol] = dzg.astype(BF)
                dbg_ref[:, gcol] += jnp.sum(dzg, axis=0, keepdims=True)
                dup = (dmn * gate).astype(BF)
                accs[b][n] += _dot(ys[b], dup, "tn")
                dyb = _dot(dup, ups[b][n], "nt")
                dys[b] = dyb if dys[b] is None else dys[b] + dyb
        dyp_ref[...] = dys[0]
        dyc_ref[...] = dys[1]
        dya = dys[2]
        lane = lax.broadcasted_iota(jnp.int32, dya.shape, 1) // HEAD_DIM
        pr = dya * y
        rho = jnp.zeros_like(pr)
        for h in range(256 // HEAD_DIM):
            hm = lane == h
            rho = jnp.where(hm, jnp.sum(jnp.where(hm, pr, 0.0), axis=-1, keepdims=True), rho)
        for g in range(3):
            do_refs[g][...] = ws[g] * dya
            c_refs[g][...] = -(ws[g] * rho)

        @pl.when(i == nsteps - 1)
        def _():
            dgpu_ref[...] = accs[0][...].astype(BF)
            dgco_ref[...] = accs[1][...].astype(BF)
            dgau_ref[...] = accs[2][...].astype(BF)

    full = lambda a: pl.BlockSpec(a.shape, lambda i: (0,) * a.ndim)
    out_specs = ([row(3072), row(512), row(512)] + [row(256)] * 6 + [full(gpu), full(gco), full(gau)]
                 + [pl.BlockSpec((1, 3072), lambda i: (0, 0))])
    out_shape = ([jax.ShapeDtypeStruct((T, 3072), BF)] + [jax.ShapeDtypeStruct((T, 512), F32)] * 2
                 + [jax.ShapeDtypeStruct((T, 256), F32)] * 6
                 + [jax.ShapeDtypeStruct(g.shape, BF) for g in (gpu, gco, gau)]
                 + [jax.ShapeDtypeStruct((1, 3072), F32)])
    return pl.pallas_call(
        body, name=name, grid=(nsteps,), in_specs=[row(1024)] + specs, out_specs=out_specs, out_shape=out_shape,
        scratch_shapes=[pltpu.VMEM(g.shape, F32) for g in (gpu, gco, gau)],
        compiler_params=_params(("arbitrary",)),
    )(dm, yp, yc, *o3, *lse3, *([z] * 12), bgate, gpu, gco, gau)


def _fold(a, d):
    return a.reshape(a.shape[0] // d, d * a.shape[1])


def _unfold(a, d):
    return a.reshape(a.shape[0] * d, a.shape[1] // d)


def _layer_fwd(x, w, tag):
    hb = _rms_fwd(x, w["norm_mix"], f"rms_mix_{tag}")
    z = _mm(hb, w["w_in"], "nn", f"in_proj_{tag}", tm=512, tn=3712, tk=1024, n_outer=True)
    yp, yc = _poolconv_fwd(z, w["pool_mix"], w["pool_scale"], w["conv_w"], f"poolconv_{tag}")
    folded, o3, lse3 = [], [], []
    for g, d in enumerate(ATTN_DILATIONS):
        qf, kf, vf = (_fold(z[:, off + 256 * g:off + 256 * (g + 1)], d) for off in (OFF_Q, OFF_K, OFF_V))
        o, lse = _attn_fwd(qf, kf, vf, w["q_gain"], w["k_gain"], d, f"attn{g}_{tag}")
        folded.append((qf, kf, vf))
        o3.append(_unfold(o, d))
        lse3.append(_unfold(lse, d))
    merged = _merge_fwd(yp, yc, o3, lse3, z, w["b_gate"], w["w_pool_up"], w["w_conv_out"], w["w_attn_up"],
                        f"merge_{tag}")
    x1 = _mm(merged, w["w_o"], "nn", f"out_proj_{tag}", tm=1024, tn=1024, tk=1024, res=x)
    h2b = _rms_fwd(x1, w["norm_mlp"], f"rms_mlp_{tag}")
    a, rb = _mm(h2b, w["w_ff1"], "nn", f"ff1_{tag}", tm=1024, tn=1024, tk=1024, epi="relu2", n_outer=True,
                b_shards=True)
    x2 = _mm(rb, w["w_ff2"], "nn", f"ff2_{tag}", tm=1024, tn=1024, tk=1024, res=x1)
    saved = dict(x=x, hb=hb, z=z, yp=yp, yc=yc, folded=folded, o3=o3, lse3=lse3, merged=merged, x1=x1, h2b=h2b,
                 a=a, rb=rb)
    return x2, saved


def _layer_bwd(dx2, w, s, tag):
    g = {}
    dab = _mm(dx2, w["w_ff2"], "nt", f"d_ff2_act_{tag}", tm=1024, tn=1024, tk=1024, out_dtype=BF, aux=s["a"],
              epi="drelu2")
    g["w_ff2"] = _mm(s["rb"], dx2, "tn", f"d_ff2_w_{tag}", tm=1024, tn=1024, tk=1024, out_dtype=BF)
    g["w_ff1"] = _mm(s["h2b"], dab, "tn", f"d_ff1_w_{tag}", tm=1024, tn=1024, tk=1024, out_dtype=BF, out_shards=True)
    dh2 = _mm(dab, w["w_ff1"], "nt", f"d_ff1_act_{tag}", tm=1024, tn=1024, tk=1024, b_shards=True)
    dx1, g["norm_mlp"] = _rms_bwd(dh2, s["x1"], w["norm_mlp"], dx2, f"d_rms_mlp_{tag}")
    dm = _mm(dx1, w["w_o"], "nt", f"d_out_act_{tag}", tm=1024, tn=1024, tk=1024)
    g["w_o"] = _mm(s["merged"], dx1, "tn", f"d_out_w_{tag}", tm=1024, tn=1024, tk=1024, out_dtype=BF)
    (dzg, dyp, dyc, do0, do1, do2, c0, c1, c2, g["w_pool_up"], g["w_conv_out"], g["w_attn_up"],
     g["b_gate"]) = _merge_bwd(dm, s["yp"], s["yc"], s["o3"], s["lse3"], s["z"], w["b_gate"], w["w_pool_up"],
                               w["w_conv_out"], w["w_attn_up"], f"d_merge_{tag}")
    dq, dk, dv = [], [], []
    dgq = dgk = None
    for gi, d in enumerate(ATTN_DILATIONS):
        qf, kf, vf = s["folded"][gi]
        dzq, dzk, dzv, pq, pk = _attn_bwd(qf, kf, vf, _fold((do0, do1, do2)[gi], d), _fold((c0, c1, c2)[gi], d),
                                          _fold(s["lse3"][gi], d), w["q_gain"], w["k_gain"], d, f"d_attn{gi}_{tag}")
        dq.append(_unfold(dzq, d))
        dk.append(_unfold(dzk, d))
        dv.append(_unfold(dzv, d))
        dgq = pq if dgq is None else dgq + pq
        dgk = pk if dgk is None else dgk + pk
    g["q_gain"] = dgq[:, :HEAD_DIM] + dgq[:, HEAD_DIM:]
    g["k_gain"] = dgk[:, :HEAD_DIM] + dgk[:, HEAD_DIM:]
    dzpc, g["pool_mix"], g["pool_scale"], g["conv_w"] = _poolconv_bwd(
        s["z"], dyp, dyc, w["pool_mix"], w["pool_scale"], w["conv_w"], f"d_poolconv_{tag}")
    dz = jnp.concatenate([dzpc] + dq + dk + dv + [dzg], axis=1)
    g["w_in"] = _mm(s["hb"], dz, "tn", f"d_in_w_{tag}", tm=512, tn=3712, tk=512, out_dtype=BF)
    dh = _mm(dz, w["w_in"], "nt", f"d_in_act_{tag}", tm=512, tn=1024, tk=3712)
    dx, g["norm_mix"] = _rms_bwd(dh, s["x"], w["norm_mix"], dx1, f"d_rms_mix_{tag}")
    return dx, g


def _position():
    x, y, c = lax.axis_index("x"), lax.axis_index("y"), lax.axis_index("c")
    chips = [(1 - x, y), (x, 1 - y), (1 - x, 1 - y)]
    return x, y, c, 2 * x + y, chips, [2 * cx + cy for cx, cy in chips]


def _remote(src, dst, ssem, rsem, dev):
    return pltpu.make_async_remote_copy(src_ref=src, dst_ref=dst, send_sem=ssem, recv_sem=rsem, device_id=dev,
                                        device_id_type=MESH_ID)


def _position_operand():
    x, y, c = lax.axis_index("x"), lax.axis_index("y"), lax.axis_index("c")
    return jnp.stack([2 * x + y, c]).astype(jnp.int32)


def _halves(a):
    return a.reshape(a.shape[0], 2, a.shape[1] // 2, a.shape[2])


def _gather(bufs, name):
    n = len(bufs)
    views = [_halves(b) for b in bufs]

    def body(*refs):
        outs = refs[n:2 * n]
        ssem, rsem, fssem, frsem = refs[2 * n:]
        x, y, c, q, chips, qs = _position()
        sib = (x, y, 1 - c)
        sent = []
        for k in range(n):
            mine = outs[k].at[q, c]
            for j, chip in enumerate(chips):
                cp = _remote(mine, mine, ssem.at[k, j], rsem.at[k, j], (chip[0], chip[1], c))
                cp.start()
                sent.append(cp)
        for k in range(n):
            for j, chip in enumerate(chips):
                slot = outs[k].at[qs[j], c]
                _remote(slot, slot, ssem.at[k, j], rsem.at[k, j], (chip[0], chip[1], c)).wait_recv()
                cp = _remote(slot, slot, fssem.at[k, j], frsem.at[k, j], sib)
                cp.start()
                sent.append(cp)
        for k in range(n):
            for j in range(3):
                slot = outs[k].at[qs[j], 1 - c]
                _remote(slot, slot, fssem.at[k, j], frsem.at[k, j], sib).wait_recv()
        for cp in sent:
            cp.wait_send()

    outs = pl.pallas_call(
        body, name=name, in_specs=[ANY] * n, out_specs=[ANY] * n,
        out_shape=[jax.ShapeDtypeStruct(v.shape, v.dtype) for v in views],
        input_output_aliases={k: k for k in range(n)},
        scratch_shapes=[pltpu.SemaphoreType.DMA((n, 3))] * 4,
    )(*views)
    return [o.reshape(b.shape) for o, b in zip(outs, bufs)]


def _pair_swap(views, name):
    n = len(views)

    def body(*refs):
        ins, outs = refs[:n], refs[n:2 * n]
        ssem, rsem = refs[2 * n:]
        x, y, c, _, _, _ = _position()
        cps = [_remote(ins[k].at[pl.ds(0, N_CHIPS), 1 - c], outs[k], ssem.at[k], rsem.at[k], (x, y, 1 - c))
               for k in range(n)]
        for cp in cps:
            cp.start()
        for cp in cps:
            cp.wait()

    return pl.pallas_call(
        body, name=name, in_specs=[ANY] * n, out_specs=[ANY] * n,
        out_shape=[jax.ShapeDtypeStruct((v.shape[0],) + v.shape[2:], v.dtype) for v in views],
        scratch_shapes=[pltpu.SemaphoreType.DMA((n,))] * 2,
    )(*views)


def _chip_exchange(parts, name):
    n = len(parts)

    def body(*refs):
        ins, outs = refs[:n], refs[n:2 * n]
        ssem, rsem = refs[2 * n:]
        x, y, c, q, chips, qs = _position()
        cps = []
        for k in range(n):
            for j, chip in enumerate(chips):
                cp = _remote(ins[k].at[qs[j]], outs[k].at[j], ssem.at[k, j], rsem.at[k, j], (chip[0], chip[1], c))
                cp.start()
                cps.append(cp)
        for cp in cps:
            cp.wait_recv()
        for cp in cps:
            cp.wait_send()

    return pl.pallas_call(
        body, name=name, in_specs=[ANY] * n, out_specs=[ANY] * n,
        out_shape=[jax.ShapeDtypeStruct((3,) + a.shape[1:], a.dtype) for a in parts],
        scratch_shapes=[pltpu.SemaphoreType.DMA((n, 3))] * 2,
    )(*parts)


def _pair_send(arrays, name):
    n = len(arrays)

    def body(*refs):
        ins, outs = refs[:n], refs[n:2 * n]
        ssem, rsem = refs[2 * n:]
        x, y, c, _, _, _ = _position()
        cps = [_remote(ins[k], outs[k], ssem.at[k], rsem.at[k], (x, y, 1 - c)) for k in range(n)]
        for cp in cps:
            cp.start()
        for cp in cps:
            cp.wait()

    return pl.pallas_call(
        body, name=name, in_specs=[ANY] * n, out_specs=[ANY] * n,
        out_shape=[jax.ShapeDtypeStruct(a.shape, a.dtype) for a in arrays],
        scratch_shapes=[pltpu.SemaphoreType.DMA((n,))] * 2,
    )(*arrays)


def _all_to_all_small(part):
    P = part.shape[0]

    def body(in_ref, out_ref, lsem, ssem, rsem):
        x, y, c = lax.axis_index("x"), lax.axis_index("y"), lax.axis_index("c")
        me = 4 * x + 2 * y + c
        flips = [(fx, fy, fc) for fx in (0, 1) for fy in (0, 1) for fc in (0, 1)][1:]
        peers = [((x + fx) % 2, (y + fy) % 2, (c + fc) % 2) for fx, fy, fc in flips]
        loc = pltpu.make_async_copy(in_ref, out_ref.at[me], lsem)
        loc.start()
        cps = [_remote(in_ref, out_ref.at[me], ssem.at[j], rsem.at[j], peer) for j, peer in enumerate(peers)]
        for cp in cps:
            cp.start()
        for j, (px, py, pc) in enumerate(peers):
            _remote(in_ref, out_ref.at[4 * px + 2 * py + pc], ssem.at[j], rsem.at[j], peers[j]).wait_recv()
        for cp in cps:
            cp.wait_send()
        loc.wait()

    return pl.pallas_call(
        body, name="small_grad_exchange", in_specs=[ANY], out_specs=ANY,
        out_shape=jax.ShapeDtypeStruct((8, P, LANES), F32),
        scratch_shapes=[pltpu.SemaphoreType.DMA(())] + [pltpu.SemaphoreType.DMA((7,))] * 2,
    )(part)


def _row_tile(rows, width, n_arrays):
    t = rows
    while t % 2 == 0 and t > 8 and 2 * n_arrays * t * width * 4 > VMEM_LIMIT // 2:
        t //= 2
    return t


def _scalar_grid(grid, in_specs, out_specs):
    return pltpu.PrefetchScalarGridSpec(num_scalar_prefetch=1, grid=grid, in_specs=in_specs, out_specs=out_specs)


def _cast_place(w3, layer, pos, name):
    _, r, c = w3.shape
    tr = _row_tile(r, c, 2)

    def body(pos_ref, w_ref, o_ref):
        o_ref[...] = w_ref[...].astype(BF)

    return pl.pallas_call(
        body, name=name,
        grid_spec=_scalar_grid((r // tr,), [pl.BlockSpec((None, tr, c), lambda i, pos: (layer, i, 0))],
                               pl.BlockSpec((None, tr, c), lambda i, pos: (pos[0], i, 0))),
        out_shape=jax.ShapeDtypeStruct((N_CHIPS, r, c), BF), compiler_params=_params(("parallel",)),
    )(pos, w3)


def _pair_sum(view, recv, pos, name):
    _, _, hr, c = view.shape
    tr = _row_tile(hr, c, 3)

    def body(pos_ref, g_ref, r_ref, o_ref):
        o_ref[...] = (g_ref[...].astype(F32) + r_ref[...].astype(F32)).astype(BF)

    blk = pl.BlockSpec((None, tr, c), lambda p, i, pos: (p, i, 0))
    return pl.pallas_call(
        body, name=name,
        grid_spec=_scalar_grid((N_CHIPS, hr // tr),
                               [pl.BlockSpec((None, None, tr, c), lambda p, i, pos: (p, pos[1], i, 0)), blk], blk),
        out_shape=jax.ShapeDtypeStruct(recv.shape, BF), compiler_params=_params(("parallel", "parallel")),
    )(pos, view, recv)


def _chip_sum(parts, recv, pos, name):
    _, hr, c = parts.shape
    tr = _row_tile(hr, c, 6)

    def body(pos_ref, p_ref, r_ref, o_ref):
        acc = p_ref[...].astype(F32)
        for j in range(3):
            acc = acc + r_ref[j].astype(F32)
        o_ref[...] = acc

    return pl.pallas_call(
        body, name=name,
        grid_spec=_scalar_grid((hr // tr,),
                               [pl.BlockSpec((None, tr, c), lambda i, pos: (pos[0], i, 0)),
                                pl.BlockSpec((3, tr, c), lambda i, pos: (0, i, 0))],
                               pl.BlockSpec((tr, c), lambda i, pos: (i, 0))),
        out_shape=jax.ShapeDtypeStruct((hr, c), F32), compiler_params=_params(("parallel",)),
    )(pos, parts, recv)


def _sum_slices(a, name):
    n, rows, width = a.shape
    tr = _row_tile(rows, width, n + 1)

    def body(a_ref, o_ref):
        acc = a_ref[0].astype(F32)
        for i in range(1, n):
            acc = acc + a_ref[i].astype(F32)
        o_ref[...] = acc

    return pl.pallas_call(
        body, name=name, grid=(rows // tr,), in_specs=[pl.BlockSpec((n, tr, width), lambda i: (0, i, 0))],
        out_specs=pl.BlockSpec((tr, width), lambda i: (i, 0)), out_shape=jax.ShapeDtypeStruct((rows, width), F32),
        compiler_params=_params(("parallel",)),
    )(a)


def _adamw_update(w, g, m, v):
    nm = ADAM_B1 * m + (1.0 - ADAM_B1) * g
    nv = ADAM_B2 * v + (1.0 - ADAM_B2) * (g * g)
    m_hat = nm / (1.0 - ADAM_B1 ** ADAM_STEP)
    v_hat = nv / (1.0 - ADAM_B2 ** ADAM_STEP)
    return -ADAM_LR * (m_hat / (jnp.sqrt(v_hat) + ADAM_EPS) + ADAM_WD * w), nm, nv


def _adamw(w, g, m, v, name):
    rows, width = w.shape
    tr = _row_tile(rows, width, 7)

    def body(w_ref, g_ref, m_ref, v_ref, d_ref, nm_ref, nv_ref):
        d_ref[...], nm_ref[...], nv_ref[...] = _adamw_update(w_ref[...], g_ref[...], m_ref[...], v_ref[...])

    blk = pl.BlockSpec((tr, width), lambda i: (i, 0))
    return pl.pallas_call(
        body, name=name, grid=(rows // tr,), in_specs=[blk] * 4, out_specs=[blk] * 3,
        out_shape=[jax.ShapeDtypeStruct((rows, width), F32)] * 3, compiler_params=_params(("parallel",)),
    )(w, g, m, v)


def _adamw_halves(w3, m3, v3, mine, other, pos, name):
    depth, r, c = w3.shape
    assert depth == 2
    hr = r // 2
    tr = _row_tile(hr, c, 11)
    sources = ((0, True, mine[0]), (0, False, other[0]), (1, True, mine[1]), (1, False, other[1]))

    def active(l, h, core, layer, own):
        mine_half = h == core
        return (l == layer) & (mine_half if own else jnp.logical_not(mine_half))

    def body(pos_ref, w_ref, m_ref, v_ref, *rest):
        g_refs, (go_ref, d_ref, nm_ref, nv_ref) = rest[:4], rest[4:]
        l, h = pl.program_id(0), pl.program_id(1)
        for (layer, own, _), g_ref in zip(sources, g_refs):
            @pl.when(active(l, h, pos_ref[1], layer, own))
            def _():
                gv = g_ref[...]
                go_ref[...] = gv
                d_ref[...], nm_ref[...], nv_ref[...] = _adamw_update(w_ref[...], gv, m_ref[...], v_ref[...])

    def gspec(layer, own):
        return pl.BlockSpec((tr, c), lambda l, h, i, pos: (jnp.where(active(l, h, pos[1], layer, own), i, 0), 0))

    blk = pl.BlockSpec((None, None, tr, c), lambda l, h, i, pos: (l, h, i, 0))
    view = lambda a: a.reshape(depth, 2, hr, c)
    outs = pl.pallas_call(
        body, name=name,
        grid_spec=_scalar_grid((depth, 2, hr // tr), [blk] * 3 + [gspec(layer, own) for layer, own, _ in sources],
                               [blk] * 4),
        out_shape=[jax.ShapeDtypeStruct((depth, 2, hr, c), F32)] * 4,
        compiler_params=_params(("parallel", "parallel", "parallel")),
    )(pos, view(w3), view(m3), view(v3), *[s[2] for s in sources])
    return [o.reshape(w3.shape) for o in outs]


BIG = ("w_in", "w_pool_up", "w_conv_out", "w_attn_up", "w_o", "w_ff1", "w_ff2")
SMALL = ("norm_mix", "b_gate", "pool_mix", "pool_scale", "conv_w", "q_gain", "k_gain", "norm_mlp")
ORDER = ("norm_mix", "w_in", "b_gate", "pool_mix", "pool_scale", "conv_w", "q_gain", "k_gain", "w_pool_up",
         "w_conv_out", "w_attn_up", "w_o", "norm_mlp", "w_ff1", "w_ff2")
COLUMN_SHARDED = ("w_in", "w_pool_up", "w_conv_out", "w_attn_up", "w_ff1")


def _layer_weights(l, gathered, small, q):
    w = {}
    for name in BIG:
        g4 = gathered[l][name]
        if name == "w_in":
            w[name] = jnp.transpose(g4, (1, 0, 2)).reshape(g4.shape[1], N_CHIPS * g4.shape[2])
        elif name in COLUMN_SHARDED:
            w[name] = g4
        else:
            w[name] = g4.reshape(N_CHIPS * g4.shape[1], g4.shape[2])
    w["norm_mix"] = small["norm_mix"][l][None]
    w["norm_mlp"] = small["norm_mlp"][l][None]
    w["b_gate"] = small["b_gate"][l][None]
    w["pool_mix"] = small["pool_mix"][l].astype(BF)
    w["pool_scale"] = small["pool_scale"][l][None]
    w["conv_w"] = jnp.pad(small["conv_w_full"][l], ((0, 5), (0, 0)))
    w["q_gain"] = jnp.tile(small["q_gain"][l], 2)[None]
    w["k_gain"] = jnp.tile(small["k_gain"][l], 2)[None]
    return w


def _to_chip_major(name, g):
    if name == "w_in":
        return jnp.transpose(g.reshape(g.shape[0], N_CHIPS, g.shape[1] // N_CHIPS), (1, 0, 2))
    if name in COLUMN_SHARDED:
        return g
    return g.reshape(N_CHIPS, g.shape[0] // N_CHIPS, g.shape[1])


def _pad8(a):
    a = a.reshape(-1, LANES)
    return jnp.pad(a, ((0, (-a.shape[0]) % 8), (0, 0)))


def kernel(x, norm_mix, w_in, b_gate, pool_mix, pool_scale, conv_w, q_gain, k_gain, w_pool_up, w_conv_out, w_attn_up, w_o, norm_mlp, w_ff1, w_ff2, loss_target, m_norm_mix, m_w_in, m_b_gate, m_pool_mix, m_pool_scale, m_conv_w, m_q_gain, m_k_gain, m_w_pool_up, m_w_conv_out, m_w_attn_up, m_w_o, m_norm_mlp, m_w_ff1, m_w_ff2, v_norm_mix, v_w_in, v_b_gate, v_pool_mix, v_pool_scale, v_conv_w, v_q_gain, v_k_gain, v_w_pool_up, v_w_conv_out, v_w_attn_up, v_w_o, v_norm_mlp, v_w_ff1, v_w_ff2):
    weights = dict(norm_mix=norm_mix, w_in=w_in, b_gate=b_gate, pool_mix=pool_mix, pool_scale=pool_scale, conv_w=conv_w,
                   q_gain=q_gain, k_gain=k_gain, w_pool_up=w_pool_up, w_conv_out=w_conv_out, w_attn_up=w_attn_up,
                   w_o=w_o, norm_mlp=norm_mlp, w_ff1=w_ff1, w_ff2=w_ff2)
    moms = dict(norm_mix=m_norm_mix, w_in=m_w_in, b_gate=m_b_gate, pool_mix=m_pool_mix, pool_scale=m_pool_scale,
                conv_w=m_conv_w, q_gain=m_q_gain, k_gain=m_k_gain, w_pool_up=m_w_pool_up, w_conv_out=m_w_conv_out,
                w_attn_up=m_w_attn_up, w_o=m_w_o, norm_mlp=m_norm_mlp, w_ff1=m_w_ff1, w_ff2=m_w_ff2)
    vels = dict(norm_mix=v_norm_mix, w_in=v_w_in, b_gate=v_b_gate, pool_mix=v_pool_mix, pool_scale=v_pool_scale,
                conv_w=v_conv_w, q_gain=v_q_gain, k_gain=v_k_gain, w_pool_up=v_w_pool_up, w_conv_out=v_w_conv_out,
                w_attn_up=v_w_attn_up, w_o=v_w_o, norm_mlp=v_norm_mlp, w_ff1=v_w_ff1, w_ff2=v_w_ff2)
    depth = norm_mix.shape[0]
    q = 2 * lax.axis_index("x") + lax.axis_index("y")
    pos = _position_operand()

    gathered = []
    for l in range(depth):
        bufs = [_cast_place(weights[n], l, pos, f"cast_{n}_l{l}") for n in BIG]
        gathered.append(dict(zip(BIG, _gather(bufs, f"gather_l{l}"))))
    cw_all = _all_to_all_small(_pad8(jnp.pad(conv_w.reshape(-1), (0, (-conv_w.size) % LANES))))
    conv_w_full = jnp.concatenate(
        [cw_all[2 * p].reshape(-1)[:conv_w.size].reshape(conv_w.shape) for p in range(N_CHIPS)], axis=-1)
    small = dict(weights)
    small["conv_w_full"] = conv_w_full

    h = x[0]
    saved, wl = [], []
    for l in range(depth):
        wl.append(_layer_weights(l, gathered, small, q))
        h, s = _layer_fwd(h, wl[l], f"l{l}")
        saved.append(s)
    dh, loss_row = _loss_grad(h, loss_target[0], "loss")
    grads, mine, other = [None] * depth, [None] * depth, [None] * depth
    for l in reversed(range(depth)):
        dh, grads[l] = _layer_bwd(dh, wl[l], saved[l], f"l{l}")
        views = [_halves(_to_chip_major(n, grads[l][n])) for n in BIG]
        from_sibling = _pair_swap(views, f"grad_pair_swap_l{l}")
        parts = [_pair_sum(views[k], from_sibling[k], pos, f"pair_sum_{n}_l{l}") for k, n in enumerate(BIG)]
        from_chips = _chip_exchange(parts, f"grad_chip_exchange_l{l}")
        mine[l] = [_chip_sum(parts[k], from_chips[k], pos, f"chip_sum_{n}_l{l}") for k, n in enumerate(BIG)]
        other[l] = _pair_send(mine[l], f"grad_pair_send_l{l}")
    loss = lax.psum(loss_row[0, 0], ("x", "y", "c"))
    full = {}

    pieces = []
    for n in SMALL:
        per_layer = [grads[l][n] for l in range(depth)]
        if n == "conv_w":
            per_layer = [p[:3] for p in per_layer]
        pieces.append(_pad8(jnp.stack(per_layer).reshape(-1)))
    packed = jnp.concatenate(pieces, axis=0)
    summed = _sum_slices(_all_to_all_small(packed), "small_sum")
    row = 0
    for n, piece in zip(SMALL, pieces):
        size = weights[n].size if n != "conv_w" else depth * 3 * 512
        flat = summed[row:row + piece.shape[0]].reshape(-1)[:size]
        row += piece.shape[0]
        if n == "conv_w":
            full[n] = lax.dynamic_slice_in_dim(flat.reshape(depth, 3, 512), q * conv_w.shape[2], conv_w.shape[2], axis=2)
        else:
            full[n] = flat.reshape(weights[n].shape)

    deltas, new_m, new_v = {}, {}, {}
    for k, n in enumerate(BIG):
        full[n], deltas[n], new_m[n], new_v[n] = _adamw_halves(
            weights[n], moms[n], vels[n], [mine[l][k] for l in range(depth)], [other[l][k] for l in range(depth)], pos,
            f"adamw_{n}")
    for n in SMALL:
        shape = weights[n].shape
        two_d = (-1, shape[-1]) if n not in ("conv_w", "q_gain", "k_gain") else (1, -1)
        d2, m2, v2 = _adamw(weights[n].reshape(two_d), full[n].reshape(two_d), moms[n].reshape(two_d),
                            vels[n].reshape(two_d), f"adamw_{n}")
        deltas[n], new_m[n], new_v[n] = d2.reshape(shape), m2.reshape(shape), v2.reshape(shape)
        full[n] = full[n].reshape(shape)
    return (loss, dh[None], *[full[n] for n in ORDER], *[deltas[n] for n in ORDER], *[new_m[n] for n in ORDER],
            *[new_v[n] for n in ORDER])
```

```python
import functools

import jax
import jax.numpy as jnp
from jax import lax
from jax.experimental import pallas as pl
from jax.experimental.pallas import tpu as pltpu

F32 = jnp.float32
BF = jnp.bfloat16
MESH_ID = pl.DeviceIdType.MESH
ANY = pl.BlockSpec(memory_space=pl.ANY)

EPS = 1e-6
MASK_VALUE = -1e30
POOL_WINDOWS = (2, 4, 8, 16)
ATTN_DILATIONS = (1, 4, 16)
ATTN_BLOCK = 128
HEAD_DIM = 64
OFF_Q, OFF_K, OFF_V, OFF_GATE = 2048, 2816, 3584, 4352
N_CHIPS = 4
ADAM_LR, ADAM_B1, ADAM_B2, ADAM_EPS, ADAM_WD, ADAM_STEP = 0.001, 0.9, 0.999, 1e-08, 0.01, 10

VMEM_LIMIT = 48 * 1024 * 1024
LANES = 128

_DIMS = {"nn": (((1,), (0,)), ((), ())), "nt": (((1,), (1,)), ((), ())), "tn": (((0,), (0,)), ((), ()))}


def _params(sem):
    return pltpu.CompilerParams(dimension_semantics=sem, vmem_limit_bytes=VMEM_LIMIT)


def _dot(a, b, mode="nn"):
    return lax.dot_general(a, b, _DIMS[mode], preferred_element_type=F32)


def _mm(a, b, mode, name, *, tm, tn, tk, out_dtype=F32, res=None, aux=None, epi=None, n_outer=False,
        b_shards=False, out_shards=False, after=None):
    if mode == "tn":
        K, M = a.shape
    else:
        M, K = a.shape
    if b_shards:
        if mode == "nn":
            assert b.shape[1] == K
            N = b.shape[2] * N_CHIPS
        else:
            assert mode == "nt"
            N = b.shape[1]
            assert b.shape[2] * N_CHIPS == K
    else:
        N = b.shape[0] if mode == "nt" else b.shape[1]
    tm, tn, tk = min(tm, M), min(tn, N), min(tk, K)
    assert M % tm == 0 and N % tn == 0 and K % tk == 0
    nk = K // tk
    if n_outer:
        grid = (N // tn, M // tm, nk)
        ij = lambda p, q_: (q_, p)
    else:
        grid = (M // tm, N // tn, nk)
        ij = lambda p, q_: (p, q_)

    def amap(p, q_, k):
        i, j = ij(p, q_)
        return (k, i) if mode == "tn" else (i, k)

    a_spec = pl.BlockSpec((tk, tm) if mode == "tn" else (tm, tk), amap)
    if b_shards:
        if mode == "nn":
            per = (N // N_CHIPS) // tn
            assert per >= 1 and (N // N_CHIPS) % tn == 0

            def bmap(p, q_, k):
                i, j = ij(p, q_)
                return (j // per, k, j % per)

            b_spec = pl.BlockSpec((None, tk, tn), bmap)
        else:
            per = (K // N_CHIPS) // tk
            assert per >= 1 and (K // N_CHIPS) % tk == 0

            def bmap(p, q_, k):
                i, j = ij(p, q_)
                return (k // per, j, k % per)

            b_spec = pl.BlockSpec((None, tn, tk), bmap)
    else:
        def bmap(p, q_, k):
            i, j = ij(p, q_)
            return (j, k) if mode == "nt" else (k, j)

        b_spec = pl.BlockSpec((tn, tk) if mode == "nt" else (tk, tn), bmap)

    def omap(p, q_, k):
        return ij(p, q_)

    o_spec = pl.BlockSpec((tm, tn), omap)
    if out_shards:
        per_o = (N // N_CHIPS) // tn
        assert per_o >= 1 and (N // N_CHIPS) % tn == 0

        def osmap(p, q_, k):
            i, j = ij(p, q_)
            return (j // per_o, i, j % per_o)

        out_spec0 = pl.BlockSpec((None, tm, tn), osmap)
        out_shape0 = jax.ShapeDtypeStruct((N_CHIPS, M, N // N_CHIPS), out_dtype)
    else:
        out_spec0 = o_spec
        out_shape0 = jax.ShapeDtypeStruct((M, N), out_dtype)

    in_specs = [a_spec, b_spec]
    args = [a, b]
    if res is not None:
        in_specs.append(o_spec)
        args.append(res)
    if aux is not None:
        in_specs.append(o_spec)
        args.append(aux)
    if after is not None:
        in_specs.append(ANY)
        args.append(after)
    out_specs = [out_spec0]
    out_shape = [out_shape0]
    if epi == "relu2":
        out_specs.append(o_spec)
        out_shape.append(jax.ShapeDtypeStruct((M, N), BF))
    n_out = len(out_shape)
    has_res, has_aux, has_after = res is not None, aux is not None, after is not None

    def body(*refs):
        a_ref, b_ref = refs[0], refs[1]
        pos = 2
        res_ref = aux_ref = None
        if has_res:
            res_ref = refs[pos]
            pos += 1
        if has_aux:
            aux_ref = refs[pos]
            pos += 1
        if has_after:
            pos += 1
        outs = refs[pos:pos + n_out]
        part = _dot(a_ref[...].astype(BF), b_ref[...].astype(BF), mode)

        def finish(acc):
            if res_ref is not None:
                acc = res_ref[...] + acc
            if epi == "relu2":
                outs[0][...] = acc
                r = jnp.maximum(acc, 0.0)
                outs[1][...] = (r * r).astype(BF)
            elif epi == "drelu2":
                outs[0][...] = (acc * (2.0 * jnp.maximum(aux_ref[...], 0.0))).astype(out_dtype)
            else:
                outs[0][...] = acc.astype(out_dtype)

        if nk == 1:
            finish(part)
        else:
            acc_ref = refs[pos + n_out]
            k = pl.program_id(2)

            @pl.when(k == 0)
            def _():
                acc_ref[...] = part

            @pl.when(k > 0)
            def _():
                acc_ref[...] += part

            @pl.when(k == nk - 1)
            def _():
                finish(acc_ref[...])

    scratch = [pltpu.VMEM((tm, tn), F32)] if nk > 1 else []
    out = pl.pallas_call(
        body, name=name, grid=grid, in_specs=in_specs, out_specs=out_specs, out_shape=out_shape,
        scratch_shapes=scratch, compiler_params=_params(("parallel", "parallel", "arbitrary")),
    )(*args)
    return out if n_out > 1 else out[0]


def _rms_fwd(x, gain, name, after=None):
    T, D = x.shape
    tm = min(512, T)

    def body(x_ref, g_ref, *rest):
        o_ref = rest[-1]
        xv = x_ref[...]
        r = lax.rsqrt(jnp.mean(xv * xv, axis=-1, keepdims=True) + EPS)
        o_ref[...] = ((xv * r) * g_ref[...]).astype(BF)

    extra = [] if after is None else [after]
    return pl.pallas_call(
        body, name=name, grid=(T // tm,),
        in_specs=[pl.BlockSpec((tm, D), lambda i: (i, 0)), pl.BlockSpec((1, D), lambda i: (0, 0))] + [ANY] * len(extra),
        out_specs=pl.BlockSpec((tm, D), lambda i: (i, 0)), out_shape=jax.ShapeDtypeStruct((T, D), BF),
        compiler_params=_params(("parallel",)),
    )(x, gain, *extra)


def _rms_bwd(dh, x, gain, dres, name):
    T, D = x.shape
    tm = min(512, T)

    def body(dh_ref, x_ref, g_ref, dres_ref, dx_ref, dg_ref):
        xv = x_ref[...]
        r = lax.rsqrt(jnp.mean(xv * xv, axis=-1, keepdims=True) + EPS)
        xhat = xv * r
        dhv = dh_ref[...]
        dy = dhv * g_ref[...]
        dx_ref[...] = dres_ref[...] + r * (dy - xhat * jnp.mean(dy * xhat, axis=-1, keepdims=True))

        @pl.when(pl.program_id(0) == 0)
        def _():
            dg_ref[...] = jnp.zeros_like(dg_ref)

        dg_ref[...] += jnp.sum(dhv * xhat, axis=0, keepdims=True)

    row = pl.BlockSpec((tm, D), lambda i: (i, 0))
    vec = pl.BlockSpec((1, D), lambda i: (0, 0))
    return pl.pallas_call(
        body, name=name, grid=(T // tm,), in_specs=[row, row, vec, row], out_specs=[row, vec],
        out_shape=[jax.ShapeDtypeStruct((T, D), F32), jax.ShapeDtypeStruct((1, D), F32)],
        compiler_params=_params(("arbitrary",)),
    )(dh, x, gain, dres)


def _loss_grad(y, target, name):
    T, D = y.shape
    tm = min(512, T)

    def body(y_ref, t_ref, dy_ref, l_ref):
        e = y_ref[...] - t_ref[...]
        dy_ref[...] = e / float(D)

        @pl.when(pl.program_id(0) == 0)
        def _():
            l_ref[...] = jnp.zeros_like(l_ref)

        l_ref[...] += 0.5 * jnp.sum(jnp.mean(e * e, axis=-1, keepdims=True))

    row = pl.BlockSpec((tm, D), lambda i: (i, 0))
    return pl.pallas_call(
        body, name=name, grid=(T // tm,), in_specs=[row, row],
        out_specs=[row, pl.BlockSpec((1, LANES), lambda i: (0, 0))],
        out_shape=[jax.ShapeDtypeStruct((T, D), F32), jax.ShapeDtypeStruct((1, LANES), F32)],
        compiler_params=_params(("arbitrary",)),
    )(y, target)


POOL_HALO = 16
CONV_HALO = 8


def _causal_window_sum(v, w):
    s, sh = v, 1
    while sh < w:
        s = s + pltpu.roll(s, sh, 0)
        sh *= 2
    return s


def _anticausal_window_sum(v, w):
    n = v.shape[0]
    s, sh = v, 1
    while sh < w:
        s = s + pltpu.roll(s, n - sh, 0)
        sh *= 2
    return s


def _poolconv_fwd(z, pmix_b, pscale, convw, name):
    T = z.shape[0]
    R = min(512, T)
    PH, CH = R // POOL_HALO, R // CONV_HALO

    def body(u_ref, uh_ref, b_ref, c_ref, ch_ref, x_ref, xh_ref, mix_ref, sc_ref, cw_ref, yp_ref, yc_ref):
        i = pl.program_id(0)
        keep = (i > 0).astype(F32)
        row = i * R + lax.broadcasted_iota(jnp.int32, (R, 1), 0)
        w_all = jnp.concatenate([uh_ref[...] * keep, u_ref[...]], axis=0)
        for g, w in enumerate(POOL_WINDOWS):
            cols = slice(128 * g, 128 * (g + 1))
            wg = w_all[:, cols]
            s = _causal_window_sum(wg, w)[POOL_HALO:]
            cnt = jnp.minimum(row + 1, w).astype(F32)
            dgrp = s / cnt - wg[POOL_HALO:]
            y = _dot(dgrp.astype(BF), mix_ref[g]) * sc_ref[:, cols]
            yp_ref[:, cols] = y.astype(BF)
        uc = jnp.concatenate([ch_ref[...] * xh_ref[...] * keep, c_ref[...] * x_ref[...]], axis=0)
        yc = cw_ref[2:3, :] * uc + cw_ref[0:1, :] * pltpu.roll(uc, 2, 0) + cw_ref[1:2, :] * pltpu.roll(uc, 1, 0)
        yc_ref[...] = (b_ref[...] * yc[CONV_HALO:]).astype(BF)

    def main(cb):
        return pl.BlockSpec((R, 512), lambda i: (i, cb))

    def prev(cb, halo, per):
        return pl.BlockSpec((halo, 512), lambda i: (jnp.maximum(i * per - 1, 0), cb))

    full = lambda a: pl.BlockSpec(a.shape, lambda i: (0,) * a.ndim)
    return pl.pallas_call(
        body, name=name, grid=(T // R,),
        in_specs=[main(0), prev(0, POOL_HALO, PH), main(1), main(2), prev(2, CONV_HALO, CH), main(3),
                  prev(3, CONV_HALO, CH), full(pmix_b), full(pscale), full(convw)],
        out_specs=[pl.BlockSpec((R, 512), lambda i: (i, 0))] * 2,
        out_shape=[jax.ShapeDtypeStruct((T, 512), BF)] * 2,
        compiler_params=_params(("parallel",)),
    )(z, z, z, z, z, z, z, pmix_b, pscale, convw)


def _poolconv_bwd(z, dyp, dyc, pmix_b, pscale, convw, name):
    T = z.shape[0]
    R = min(512, T)
    PH, CH = R // POOL_HALO, R // CONV_HALO
    nsteps = T // R

    def body(u_ref, uh_ref, b_ref, bn_ref, c_ref, ch_ref, x_ref, xh_ref, dyp_ref, dypn_ref, dyc_ref, dycn_ref,
             mix_ref, sc_ref, cw_ref, dz_ref, dmix_ref, dsc_ref, dcw_ref):
        i = pl.program_id(0)
        keep_prev = (i > 0).astype(F32)
        keep_next = (i < nsteps - 1).astype(F32)

        @pl.when(i == 0)
        def _():
            dmix_ref[...] = jnp.zeros_like(dmix_ref)
            dsc_ref[...] = jnp.zeros_like(dsc_ref)
            dcw_ref[...] = jnp.zeros_like(dcw_ref)

        row = i * R + lax.broadcasted_iota(jnp.int32, (R, 1), 0)
        row_ext = i * R + lax.broadcasted_iota(jnp.int32, (R + POOL_HALO, 1), 0)
        w_all = jnp.concatenate([uh_ref[...] * keep_prev, u_ref[...]], axis=0)
        dyp_ext = jnp.concatenate([dyp_ref[...], dypn_ref[...] * keep_next], axis=0)
        for g, w in enumerate(POOL_WINDOWS):
            cols = slice(128 * g, 128 * (g + 1))
            wg = w_all[:, cols]
            s = _causal_window_sum(wg, w)[POOL_HALO:]
            cnt = jnp.minimum(row + 1, w).astype(F32)
            dgrp = (s / cnt - wg[POOL_HALO:]).astype(BF)
            y_pre = _dot(dgrp, mix_ref[g])
            dsc_ref[:, cols] += jnp.sum(dyp_ref[:, cols] * y_pre, axis=0, keepdims=True)
            dyb = (dyp_ext[:, cols] * sc_ref[:, cols]).astype(BF)
            dmix_ref[cols, :] += _dot(dgrp, dyb[:R], "tn")
            dd = _dot(dyb, mix_ref[g], "nt")
            cnt_ext = jnp.minimum(row_ext + 1, w).astype(F32)
            e = _anticausal_window_sum(dd / cnt_ext, w)
            dz_ref[:, cols] = (e[:R] - dd[:R]).astype(BF)
        cw0, cw1, cw2 = cw_ref[0:1, :], cw_ref[1:2, :], cw_ref[2:3, :]
        uc = jnp.concatenate([ch_ref[...] * xh_ref[...] * keep_prev, c_ref[...] * x_ref[...]], axis=0)
        uc1 = pltpu.roll(uc, 1, 0)[CONV_HALO:]
        uc2 = pltpu.roll(uc, 2, 0)[CONV_HALO:]
        uc0 = uc[CONV_HALO:]
        yc = cw2 * uc0 + cw0 * uc2 + cw1 * uc1
        dycv = dyc_ref[...]
        dz_ref[:, 512:1024] = (dycv * yc).astype(BF)
        dv_ext = jnp.concatenate([dycv * b_ref[...], dycn_ref[...] * bn_ref[...] * keep_next], axis=0)
        n_ext = R + CONV_HALO
        duc = (cw2 * dv_ext + cw1 * pltpu.roll(dv_ext, n_ext - 1, 0) + cw0 * pltpu.roll(dv_ext, n_ext - 2, 0))[:R]
        dv = dv_ext[:R]
        dcw_ref[0:1, :] += jnp.sum(dv * uc2, axis=0, keepdims=True)
        dcw_ref[1:2, :] += jnp.sum(dv * uc1, axis=0, keepdims=True)
        dcw_ref[2:3, :] += jnp.sum(dv * uc0, axis=0, keepdims=True)
        dz_ref[:, 1024:1536] = (duc * x_ref[...]).astype(BF)
        dz_ref[:, 1536:2048] = (duc * c_ref[...]).astype(BF)

    def main(cb):
        return pl.BlockSpec((R, 512), lambda i: (i, cb))

    def prev(cb, halo, per):
        return pl.BlockSpec((halo, 512), lambda i: (jnp.maximum(i * per - 1, 0), cb))

    def nxt(cb, halo, per):
        return pl.BlockSpec((halo, 512), lambda i: (jnp.minimum((i + 1) * per, T // halo - 1), cb))

    full = lambda a: pl.BlockSpec(a.shape, lambda i: (0,) * a.ndim)
    return pl.pallas_call(
        body, name=name, grid=(nsteps,),
        in_specs=[main(0), prev(0, POOL_HALO, PH), main(1), nxt(1, CONV_HALO, CH), main(2), prev(2, CONV_HALO, CH),
                  main(3), prev(3, CONV_HALO, CH), main(0), nxt(0, POOL_HALO, PH), main(0), nxt(0, CONV_HALO, CH),
                  full(pmix_b), full(pscale), full(convw)],
        out_specs=[pl.BlockSpec((R, 2048), lambda i: (i, 0)), pl.BlockSpec((512, 128), lambda i: (0, 0)),
                   pl.BlockSpec((1, 512), lambda i: (0, 0)), pl.BlockSpec((8, 512), lambda i: (0, 0))],
        out_shape=[jax.ShapeDtypeStruct((T, 2048), BF), jax.ShapeDtypeStruct((512, 128), F32),
                   jax.ShapeDtypeStruct((1, 512), F32), jax.ShapeDtypeStruct((8, 512), F32)],
        compiler_params=_params(("arbitrary",)),
    )(z, z, z, z, z, z, z, z, dyp, dyp, dyc, dyc, pmix_b, pscale, convw)


def _head_norm(x, g2, ma):
    sq = x * x
    sa = jnp.sum(jnp.where(ma, sq, 0.0), axis=-1, keepdims=True)
    sb = jnp.sum(jnp.where(ma, 0.0, sq), axis=-1, keepdims=True)
    r = jnp.where(ma, lax.rsqrt(sa / HEAD_DIM + EPS), lax.rsqrt(sb / HEAD_DIM + EPS))
    return x * r, r


def _head_norm_bwd(dy, xhat, r, g2, ma):
    dxh = dy * g2
    pr = dxh * xhat
    sa = jnp.sum(jnp.where(ma, pr, 0.0), axis=-1, keepdims=True)
    sb = jnp.sum(jnp.where(ma, 0.0, pr), axis=-1, keepdims=True)
    mh = jnp.where(ma, sa, sb) / HEAD_DIM
    return r * (dxh - xhat * mh)


def _head_col(tile, hm):
    return jnp.max(jnp.where(hm, tile, -jnp.inf), axis=-1, keepdims=True)


def _attn_masks(other_block_exists):
    lane = lax.broadcasted_iota(jnp.int32, (ATTN_BLOCK, ATTN_BLOCK), 1)
    qi = lax.broadcasted_iota(jnp.int32, (ATTN_BLOCK, ATTN_BLOCK), 0)
    never = (1 - other_block_exists.astype(jnp.int32)) * (2 * ATTN_BLOCK)
    return lane < HEAD_DIM, lane <= qi, lane >= qi + never


def _attn_fwd(qf, kf, vf, gq2, gk2, d, name):
    L = qf.shape[0]
    nb = L // ATTN_BLOCK
    scale = HEAD_DIM ** -0.5

    def body(q_ref, kc_ref, kp_ref, vc_ref, vp_ref, gq_ref, gk_ref, o_ref, lse_ref):
        j = pl.program_id(1)
        ma, mask_c, mask_p = _attn_masks(j > 0)
        for t in range(2):
            sl = slice(LANES * t, LANES * (t + 1))
            qn = _head_norm(q_ref[:, sl], gq_ref[...], ma)[0] * gq_ref[...]
            kcb = (_head_norm(kc_ref[:, sl], gk_ref[...], ma)[0] * gk_ref[...]).astype(BF)
            kpb = (_head_norm(kp_ref[:, sl], gk_ref[...], ma)[0] * gk_ref[...]).astype(BF)
            vcb = vc_ref[:, sl].astype(BF)
            vpb = vp_ref[:, sl].astype(BF)
            o_t = lse_t = None
            for hm in (ma, jnp.logical_not(ma)):
                qh = jnp.where(hm, qn, 0.0).astype(BF)
                s_c = jnp.where(mask_c, _dot(qh, kcb, "nt") * scale, MASK_VALUE)
                s_p = jnp.where(mask_p, _dot(qh, kpb, "nt") * scale, MASK_VALUE)
                m = jnp.maximum(jnp.max(s_c, axis=-1, keepdims=True), jnp.max(s_p, axis=-1, keepdims=True))
                p_c = jnp.exp(s_c - m)
                p_p = jnp.exp(s_p - m)
                den = jnp.sum(p_c, axis=-1, keepdims=True) + jnp.sum(p_p, axis=-1, keepdims=True)
                o = (_dot(p_c.astype(BF), vcb) + _dot(p_p.astype(BF), vpb)) / den
                lse = jnp.broadcast_to(m + jnp.log(den), o.shape)
                o_t = o if o_t is None else jnp.where(ma, o_t, o)
                lse_t = lse if lse_t is None else jnp.where(ma, lse_t, lse)
            o_ref[:, sl] = o_t
            lse_ref[:, sl] = lse_t

    cur = pl.BlockSpec((ATTN_BLOCK, 256), lambda r, j: (j, r))
    prv = pl.BlockSpec((ATTN_BLOCK, 256), lambda r, j: (jnp.maximum(j - 1, 0), r))
    vec = pl.BlockSpec((1, LANES), lambda r, j: (0, 0))
    return pl.pallas_call(
        body, name=name, grid=(d, nb), in_specs=[cur, cur, prv, cur, prv, vec, vec], out_specs=[cur, cur],
        out_shape=[jax.ShapeDtypeStruct(qf.shape, F32)] * 2,
        compiler_params=_params(("parallel", "parallel")),
    )(qf, kf, kf, vf, vf, gq2, gk2)


def _attn_bwd(qf, kf, vf, dof, cf, lsef, gq2, gk2, d, name):
    L = qf.shape[0]
    nb = L // ATTN_BLOCK
    scale = HEAD_DIM ** -0.5

    def body(q_ref, qn_ref, k_ref, kp_ref, v_ref, vp_ref, do_ref, don_ref, c_ref, cn_ref, lse_ref, lsen_ref,
             gq_ref, gk_ref, dq_ref, dk_ref, dv_ref, dgq_ref, dgk_ref):
        r_id, j = pl.program_id(0), pl.program_id(1)
        ma, mask_c, mask_p = _attn_masks(j > 0)
        mask_n = _attn_masks(j < nb - 1)[2]

        @pl.when((r_id == 0) & (j == 0))
        def _():
            dgq_ref[...] = jnp.zeros_like(dgq_ref)
            dgk_ref[...] = jnp.zeros_like(dgk_ref)

        gq, gk = gq_ref[...], gk_ref[...]
        for t in range(2):
            sl = slice(LANES * t, LANES * (t + 1))
            qhat, rq = _head_norm(q_ref[:, sl], gq, ma)
            qn = qhat * gq
            qn_next = _head_norm(qn_ref[:, sl], gq, ma)[0] * gq
            khat, rk = _head_norm(k_ref[:, sl], gk, ma)
            kcb = (khat * gk).astype(BF)
            kpb = (_head_norm(kp_ref[:, sl], gk, ma)[0] * gk).astype(BF)
            vcb = v_ref[:, sl].astype(BF)
            vpb = vp_ref[:, sl].astype(BF)
            do_t, don_t = do_ref[:, sl], don_ref[:, sl]
            c_t, cn_t = c_ref[:, sl], cn_ref[:, sl]
            lse_t, lsen_t = lse_ref[:, sl], lsen_ref[:, sl]
            dq_t = None
            dk_t = jnp.zeros((ATTN_BLOCK, LANES), F32)
            dv_t = jnp.zeros((ATTN_BLOCK, LANES), F32)
            for hm in (ma, jnp.logical_not(ma)):
                qh = jnp.where(hm, qn, 0.0).astype(BF)
                doh = jnp.where(hm, do_t, 0.0).astype(BF)
                lse_h = _head_col(lse_t, hm)
                c_h = _head_col(c_t, hm)
                s_c = jnp.where(mask_c, _dot(qh, kcb, "nt") * scale, MASK_VALUE)
                s_p = jnp.where(mask_p, _dot(qh, kpb, "nt") * scale, MASK_VALUE)
                p_c = jnp.exp(s_c - lse_h)
                p_p = jnp.exp(s_p - lse_h)
                ds_c = ((p_c * (_dot(doh, vcb, "nt") + c_h)) * scale).astype(BF)
                ds_p = ((p_p * (_dot(doh, vpb, "nt") + c_h)) * scale).astype(BF)
                dq_h = _dot(ds_c, kcb) + _dot(ds_p, kpb)
                dq_t = dq_h if dq_t is None else jnp.where(ma, dq_t, dq_h)
                qh_n = jnp.where(hm, qn_next, 0.0).astype(BF)
                doh_n = jnp.where(hm, don_t, 0.0).astype(BF)
                s_n = jnp.where(mask_n, _dot(qh_n, kcb, "nt") * scale, MASK_VALUE)
                p_n = jnp.exp(s_n - _head_col(lsen_t, hm))
                ds_n = ((p_n * (_dot(doh_n, vcb, "nt") + _head_col(cn_t, hm))) * scale).astype(BF)
                dv_t = dv_t + _dot(p_c.astype(BF), doh, "tn") + _dot(p_n.astype(BF), doh_n, "tn")
                dk_t = dk_t + _dot(ds_c, qh, "tn") + _dot(ds_n, qh_n, "tn")
            dq_ref[:, sl] = _head_norm_bwd(dq_t, qhat, rq, gq, ma).astype(BF)
            dk_ref[:, sl] = _head_norm_bwd(dk_t, khat, rk, gk, ma).astype(BF)
            dv_ref[:, sl] = dv_t.astype(BF)
            dgq_ref[...] += jnp.sum(dq_t * qhat, axis=0, keepdims=True)
            dgk_ref[...] += jnp.sum(dk_t * khat, axis=0, keepdims=True)

    cur = pl.BlockSpec((ATTN_BLOCK, 256), lambda r, j: (j, r))
    prv = pl.BlockSpec((ATTN_BLOCK, 256), lambda r, j: (jnp.maximum(j - 1, 0), r))
    nxt = pl.BlockSpec((ATTN_BLOCK, 256), lambda r, j: (jnp.minimum(j + 1, nb - 1), r))
    vec = pl.BlockSpec((1, LANES), lambda r, j: (0, 0))
    return pl.pallas_call(
        body, name=name, grid=(d, nb),
        in_specs=[cur, nxt, cur, prv, cur, prv, cur, nxt, cur, nxt, cur, nxt, vec, vec],
        out_specs=[cur, cur, cur, vec, vec],
        out_shape=[jax.ShapeDtypeStruct(qf.shape, BF)] * 3 + [jax.ShapeDtypeStruct((1, LANES), F32)] * 2,
        compiler_params=_params(("arbitrary", "arbitrary")),
    )(qf, qf, kf, kf, vf, vf, dof, dof, cf, cf, lsef, lsef, gq2, gk2)


MERGE_ROWS = 256
GATE_TILE = 256


def _group_mix(o_refs, lse_refs):
    lses = [r[...] for r in lse_refs]
    m = jnp.maximum(jnp.maximum(lses[0], lses[1]), lses[2])
    es = [jnp.exp(l - m) for l in lses]
    den = es[0] + es[1] + es[2]
    ws = [e / den for e in es]
    y = ws[0] * o_refs[0][...] + ws[1] * o_refs[1][...] + ws[2] * o_refs[2][...]
    return ws, y


def _sigmoid(v):
    return 1.0 / (1.0 + jnp.exp(-v))


def _merge_specs(T, z, bgate, gpu, gco, gau):
    tm = min(MERGE_ROWS, T)
    row = lambda w: pl.BlockSpec((tm, w), lambda i: (i, 0))
    gate0 = OFF_GATE // GATE_TILE
    gates = [pl.BlockSpec((tm, GATE_TILE), functools.partial(lambda i, cb: (i, cb), cb=gate0 + n))
             for n in range(3 * N_CHIPS)]
    full = lambda a: pl.BlockSpec(a.shape, lambda i: (0,) * a.ndim)
    specs = [row(512), row(512)] + [row(256)] * 6 + gates + [full(bgate), full(gpu), full(gco), full(gau)]
    return tm, row, specs


def _merge_fwd(yp, yc, o3, lse3, z, bgate, gpu, gco, gau, name):
    T = yp.shape[0]
    tm, row, specs = _merge_specs(T, z, bgate, gpu, gco, gau)

    def body(*refs):
        yp_ref, yc_ref = refs[0], refs[1]
        o_refs, lse_refs = refs[2:5], refs[5:8]
        zg = refs[8:20]
        b_ref, gpu_ref, gco_ref, gau_ref, out_ref = refs[20:25]
        yab = _group_mix(o_refs, lse_refs)[1].astype(BF)
        ys = (yp_ref[...], yc_ref[...], yab)
        ups = (gpu_ref, gco_ref, gau_ref)
        for n in range(N_CHIPS):
            acc = None
            for b in range(3):
                gcol = slice(1024 * b + GATE_TILE * n, 1024 * b + GATE_TILE * (n + 1))
                gate = _sigmoid(zg[N_CHIPS * b + n][...] + b_ref[:, gcol])
                term = gate * _dot(ys[b], ups[b][n])
                acc = term if acc is None else acc + term
            out_ref[:, GATE_TILE * n:GATE_TILE * (n + 1)] = acc.astype(BF)

    return pl.pallas_call(
        body, name=name, grid=(T // tm,), in_specs=specs, out_specs=row(1024),
        out_shape=jax.ShapeDtypeStruct((T, 1024), BF), compiler_params=_params(("parallel",)),
    )(yp, yc, *o3, *lse3, *([z] * 12), bgate, gpu, gco, gau)


def _merge_bwd(dm, yp, yc, o3, lse3, z, bgate, gpu, gco, gau, name):
    T = yp.shape[0]
    tm, row, specs = _merge_specs(T, z, bgate, gpu, gco, gau)
    nsteps = T // tm

    def body(*refs):
        dm_ref, yp_ref, yc_ref = refs[0:3]
        o_refs, lse_refs = refs[3:6], refs[6:9]
        zg = refs[9:21]
        b_ref, gpu_ref, gco_ref, gau_ref = refs[21:25]
        dzg_ref, dyp_ref, dyc_ref = refs[25:28]
        do_refs, c_refs = refs[28:31], refs[31:34]
        dgpu_ref, dgco_ref, dgau_ref, dbg_ref = refs[34:38]
        accs = refs[38:41]
        i = pl.program_id(0)

        @pl.when(i == 0)
        def _():
            for a in accs:
                a[...] = jnp.zeros_like(a)
            dbg_ref[...] = jnp.zeros_like(dbg_ref)

        ws, y = _group_mix(o_refs, lse_refs)
        ys = (yp_ref[...], yc_ref[...], y.astype(BF))
        ups = (gpu_ref, gco_ref, gau_ref)
        dys = [None, None, None]
        for n in range(N_CHIPS):
            dmn = dm_ref[:, GATE_TILE * n:GATE_TILE * (n + 1)]
            for b in range(3):
                gcol = slice(1024 * b + GATE_TILE * n, 1024 * b + GATE_TILE * (n + 1))
                gate = _sigmoid(zg[N_CHIPS * b + n][...] + b_ref[:, gcol])
                up = _dot(ys[b], ups[b][n])
                dzg = (dmn * up) * (gate * (1.0 - gate))
                dzg_ref[:, gcol] = dzg.astype(BF)
                dbg_ref[:, gcol] += jnp.sum(dzg, axis=0, keepdims=True)
                dup = (dmn * gate).astype(BF)
                accs[b][n] += _dot(ys[b], dup, "tn")
                dyb = _dot(dup, ups[b][n], "nt")
                dys[b] = dyb if dys[b] is None else dys[b] + dyb
        dyp_ref[...] = dys[0]
        dyc_ref[...] = dys[1]
        dya = dys[2]
        lane = lax.broadcasted_iota(jnp.int32, dya.shape, 1) // HEAD_DIM
        pr = dya * y
        rho = jnp.zeros_like(pr)
        for h in range(256 // HEAD_DIM):
            hm = lane == h
            rho = jnp.where(hm, jnp.sum(jnp.where(hm, pr, 0.0), axis=-1, keepdims=True), rho)
        for g in range(3):
            do_refs[g][...] = ws[g] * dya
            c_refs[g][...] = -(ws[g] * rho)

        @pl.when(i == nsteps - 1)
        def _():
            dgpu_ref[...] = accs[0][...].astype(BF)
            dgco_ref[...] = accs[1][...].astype(BF)
            dgau_ref[...] = accs[2][...].astype(BF)

    full = lambda a: pl.BlockSpec(a.shape, lambda i: (0,) * a.ndim)
    out_specs = ([row(3072), row(512), row(512)] + [row(256)] * 6 + [full(gpu), full(gco), full(gau)]
                 + [pl.BlockSpec((1, 3072), lambda i: (0, 0))])
    out_shape = ([jax.ShapeDtypeStruct((T, 3072), BF)] + [jax.ShapeDtypeStruct((T, 512), F32)] * 2
                 + [jax.ShapeDtypeStruct((T, 256), F32)] * 6
                 + [jax.ShapeDtypeStruct(g.shape, BF) for g in (gpu, gco, gau)]
                 + [jax.ShapeDtypeStruct((1, 3072), F32)])
    return pl.pallas_call(
        body, name=name, grid=(nsteps,), in_specs=[row(1024)] + specs, out_specs=out_specs, out_shape=out_shape,
        scratch_shapes=[pltpu.VMEM(g.shape, F32) for g in (gpu, gco, gau)],
        compiler_params=_params(("arbitrary",)),
    )(dm, yp, yc, *o3, *lse3, *([z] * 12), bgate, gpu, gco, gau)


def _fold(a, d):
    return a.reshape(a.shape[0] // d, d * a.shape[1])


def _unfold(a, d):
    return a.reshape(a.shape[0] * d, a.shape[1] // d)


def _layer_fwd(x, w, tag, after=None):
    hb = _rms_fwd(x, w["norm_mix"], f"rms_mix_{tag}", after=after)
    z = _mm(hb, w["w_in"], "nn", f"in_proj_{tag}", tm=512, tn=3712, tk=1024, n_outer=True)
    yp, yc = _poolconv_fwd(z, w["pool_mix"], w["pool_scale"], w["conv_w"], f"poolconv_{tag}")
    folded, o3, lse3 = [], [], []
    for g, d in enumerate(ATTN_DILATIONS):
        qf, kf, vf = (_fold(z[:, off + 256 * g:off + 256 * (g + 1)], d) for off in (OFF_Q, OFF_K, OFF_V))
        o, lse = _attn_fwd(qf, kf, vf, w["q_gain"], w["k_gain"], d, f"attn{g}_{tag}")
        folded.append((qf, kf, vf))
        o3.append(_unfold(o, d))
        lse3.append(_unfold(lse, d))
    merged = _merge_fwd(yp, yc, o3, lse3, z, w["b_gate"], w["w_pool_up"], w["w_conv_out"], w["w_attn_up"],
                        f"merge_{tag}")
    x1 = _mm(merged, w["w_o"], "nn", f"out_proj_{tag}", tm=1024, tn=1024, tk=1024, res=x)
    h2b = _rms_fwd(x1, w["norm_mlp"], f"rms_mlp_{tag}")
    a, rb = _mm(h2b, w["w_ff1"], "nn", f"ff1_{tag}", tm=1024, tn=1024, tk=1024, epi="relu2", n_outer=True,
                b_shards=True)
    x2 = _mm(rb, w["w_ff2"], "nn", f"ff2_{tag}", tm=1024, tn=1024, tk=1024, res=x1)
    saved = dict(x=x, hb=hb, z=z, yp=yp, yc=yc, folded=folded, o3=o3, lse3=lse3, merged=merged, x1=x1, h2b=h2b,
                 a=a, rb=rb)
    return x2, saved


def _layer_bwd(dx2, w, s, tag, after=None):
    g = {}
    dab = _mm(dx2, w["w_ff2"], "nt", f"d_ff2_act_{tag}", tm=1024, tn=1024, tk=1024, out_dtype=BF, aux=s["a"],
              epi="drelu2", after=after)
    g["w_ff2"] = _mm(s["rb"], dx2, "tn", f"d_ff2_w_{tag}", tm=1024, tn=1024, tk=1024, out_dtype=BF)
    g["w_ff1"] = _mm(s["h2b"], dab, "tn", f"d_ff1_w_{tag}", tm=1024, tn=1024, tk=1024, out_dtype=BF, out_shards=True)
    dh2 = _mm(dab, w["w_ff1"], "nt", f"d_ff1_act_{tag}", tm=1024, tn=1024, tk=1024, b_shards=True)
    dx1, g["norm_mlp"] = _rms_bwd(dh2, s["x1"], w["norm_mlp"], dx2, f"d_rms_mlp_{tag}")
    dm = _mm(dx1, w["w_o"], "nt", f"d_out_act_{tag}", tm=1024, tn=1024, tk=1024)
    g["w_o"] = _mm(s["merged"], dx1, "tn", f"d_out_w_{tag}", tm=1024, tn=1024, tk=1024, out_dtype=BF)
    (dzg, dyp, dyc, do0, do1, do2, c0, c1, c2, g["w_pool_up"], g["w_conv_out"], g["w_attn_up"],
     g["b_gate"]) = _merge_bwd(dm, s["yp"], s["yc"], s["o3"], s["lse3"], s["z"], w["b_gate"], w["w_pool_up"],
                               w["w_conv_out"], w["w_attn_up"], f"d_merge_{tag}")
    dq, dk, dv = [], [], []
    dgq = dgk = None
    for gi, d in enumerate(ATTN_DILATIONS):
        qf, kf, vf = s["folded"][gi]
        dzq, dzk, dzv, pq, pk = _attn_bwd(qf, kf, vf, _fold((do0, do1, do2)[gi], d), _fold((c0, c1, c2)[gi], d),
                                          _fold(s["lse3"][gi], d), w["q_gain"], w["k_gain"], d, f"d_attn{gi}_{tag}")
        dq.append(_unfold(dzq, d))
        dk.append(_unfold(dzk, d))
        dv.append(_unfold(dzv, d))
        dgq = pq if dgq is None else dgq + pq
        dgk = pk if dgk is None else dgk + pk
    g["q_gain"] = dgq[:, :HEAD_DIM] + dgq[:, HEAD_DIM:]
    g["k_gain"] = dgk[:, :HEAD_DIM] + dgk[:, HEAD_DIM:]
    dzpc, g["pool_mix"], g["pool_scale"], g["conv_w"] = _poolconv_bwd(
        s["z"], dyp, dyc, w["pool_mix"], w["pool_scale"], w["conv_w"], f"d_poolconv_{tag}")
    dz = jnp.concatenate([dzpc] + dq + dk + dv + [dzg], axis=1)
    g["w_in"] = _mm(s["hb"], dz, "tn", f"d_in_w_{tag}", tm=512, tn=3712, tk=512, out_dtype=BF)
    dh = _mm(dz, w["w_in"], "nt", f"d_in_act_{tag}", tm=512, tn=1024, tk=3712)
    dx, g["norm_mix"] = _rms_bwd(dh, s["x"], w["norm_mix"], dx1, f"d_rms_mix_{tag}")
    return dx, g


def _position():
    x, y, c = lax.axis_index("x"), lax.axis_index("y"), lax.axis_index("c")
    chips = [(1 - x, y), (x, 1 - y), (1 - x, 1 - y)]
    return x, y, c, 2 * x + y, chips, [2 * cx + cy for cx, cy in chips]


def _remote(src, dst, ssem, rsem, dev):
    return pltpu.make_async_remote_copy(src_ref=src, dst_ref=dst, send_sem=ssem, recv_sem=rsem, device_id=dev,
                                        device_id_type=MESH_ID)


def _position_operand():
    x, y, c = lax.axis_index("x"), lax.axis_index("y"), lax.axis_index("c")
    return jnp.stack([2 * x + y, c]).astype(jnp.int32)


def _halves(a):
    return a.reshape(a.shape[0], 2, a.shape[1] // 2, a.shape[2])


def _gather(bufs, name):
    n = len(bufs)
    views = [_halves(b) for b in bufs]

    def body(*refs):
        outs = refs[n:2 * n]
        ssem, rsem, fssem, frsem = refs[2 * n:]
        x, y, c, q, chips, qs = _position()
        sib = (x, y, 1 - c)
        sent = []
        for k in range(n):
            mine = outs[k].at[q, c]
            for j, chip in enumerate(chips):
                cp = _remote(mine, mine, ssem.at[k, j], rsem.at[k, j], (chip[0], chip[1], c))
                cp.start()
                sent.append(cp)
        for k in range(n):
            for j, chip in enumerate(chips):
                slot = outs[k].at[qs[j], c]
                _remote(slot, slot, ssem.at[k, j], rsem.at[k, j], (chip[0], chip[1], c)).wait_recv()
                cp = _remote(slot, slot, fssem.at[k, j], frsem.at[k, j], sib)
                cp.start()
                sent.append(cp)
        for k in range(n):
            for j in range(3):
                slot = outs[k].at[qs[j], 1 - c]
                _remote(slot, slot, fssem.at[k, j], frsem.at[k, j], sib).wait_recv()
        for cp in sent:
            cp.wait_send()

    outs = pl.pallas_call(
        body, name=name, in_specs=[ANY] * n, out_specs=[ANY] * n,
        out_shape=[jax.ShapeDtypeStruct(v.shape, v.dtype) for v in views],
        input_output_aliases={k: k for k in range(n)},
        scratch_shapes=[pltpu.SemaphoreType.DMA((n, 3))] * 4,
    )(*views)
    return [o.reshape(b.shape) for o, b in zip(outs, bufs)]


SEM = pl.BlockSpec(memory_space=pltpu.SEMAPHORE)
TOKEN = jax.ShapeDtypeStruct((8, LANES), F32)
TOKEN_SPEC = pl.BlockSpec(memory_space=pltpu.VMEM)


def _split_params():
    return pltpu.CompilerParams(has_side_effects=pltpu.SideEffectType.DATAFLOW_SIDE_EFFECTING)


def _gather_start(bufs, name):
    n = len(bufs)
    views = [_halves(b) for b in bufs]

    def body(*refs):
        ssem, rsem = refs[n:n + ns], refs[n + ns:n + 2 * ns]
        outs, token = refs[n + 2 * ns:2 * n + 2 * ns], refs[2 * n + 2 * ns]
        x, y, c, q, chips, qs = _position()
        for k in range(n):
            mine = outs[k].at[q, c]
            for j, chip in enumerate(chips):
                _remote(mine, mine, ssem[3 * k + j], rsem[3 * k + j], (chip[0], chip[1], c)).start()
        token[...] = jnp.zeros_like(token)

    ns = 3 * n
    outs = pl.pallas_call(
        body, name=name, in_specs=[ANY] * n, out_specs=[SEM] * (2 * ns) + [ANY] * n + [TOKEN_SPEC],
        out_shape=[pltpu.SemaphoreType.DMA(())] * (2 * ns) + [jax.ShapeDtypeStruct(v.shape, v.dtype) for v in views]
        + [TOKEN],
        input_output_aliases={k: k + 2 * ns for k in range(n)}, compiler_params=_split_params(),
    )(*views)
    return list(outs[:ns]), list(outs[ns:2 * ns]), list(outs[2 * ns:2 * ns + n]), outs[2 * ns + n]


def _gather_finish(ssem, rsem, views, after, name_wait, name_forward, shapes):
    n = len(views)
    ns = len(ssem)

    def wait_body(*refs):
        ssem_ref, rsem_ref = refs[n:n + ns], refs[n + ns:n + 2 * ns]
        outs = refs[n + 2 * ns + 1:]
        x, y, c, q, chips, qs = _position()
        for k in range(n):
            for j, chip in enumerate(chips):
                cp = _remote(outs[k].at[q, c], outs[k].at[qs[j], c], ssem_ref[3 * k + j], rsem_ref[3 * k + j],
                             (chip[0], chip[1], c))
                cp.wait_send()
                cp.wait_recv()

    landed = pl.pallas_call(
        wait_body, name=name_wait, in_specs=[ANY] * n + [SEM] * (2 * ns) + [ANY], out_specs=[ANY] * n,
        out_shape=[jax.ShapeDtypeStruct(v.shape, v.dtype) for v in views],
        input_output_aliases={k: k for k in range(n)}, compiler_params=_split_params(),
    )(*views, *ssem, *rsem, after)

    def forward_body(*refs):
        outs = refs[n:2 * n]
        fssem, frsem = refs[2 * n:]
        x, y, c, q, chips, qs = _position()
        sib = (x, y, 1 - c)
        sent = []
        for k in range(n):
            for j in range(3):
                slot = outs[k].at[qs[j], c]
                cp = _remote(slot, slot, fssem.at[k, j], frsem.at[k, j], sib)
                cp.start()
                sent.append(cp)
        for k in range(n):
            for j in range(3):
                slot = outs[k].at[qs[j], 1 - c]
                _remote(slot, slot, fssem.at[k, j], frsem.at[k, j], sib).wait_recv()
        for cp in sent:
            cp.wait_send()

    outs = pl.pallas_call(
        forward_body, name=name_forward, in_specs=[ANY] * n, out_specs=[ANY] * n,
        out_shape=[jax.ShapeDtypeStruct(v.shape, v.dtype) for v in views],
        input_output_aliases={k: k for k in range(n)}, scratch_shapes=[pltpu.SemaphoreType.DMA((n, 3))] * 2,
    )(*landed)
    return [o.reshape(s) for o, s in zip(outs, shapes)]


def _chip_exchange_start(parts, name):
    n = len(parts)

    def body(*refs):
        ssem, rsem = refs[n:n + ns], refs[n + ns:n + 2 * ns]
        base = n + 2 * ns
        srcs, outs, token = refs[base:base + n], refs[base + n:base + 2 * n], refs[base + 2 * n]
        x, y, c, q, chips, qs = _position()
        for k in range(n):
            for j, chip in enumerate(chips):
                _remote(srcs[k].at[qs[j]], outs[k].at[j], ssem[3 * k + j], rsem[3 * k + j],
                        (chip[0], chip[1], c)).start()
        token[...] = jnp.zeros_like(token)

    ns = 3 * n
    outs = pl.pallas_call(
        body, name=name, in_specs=[ANY] * n, out_specs=[SEM] * (2 * ns) + [ANY] * (2 * n) + [TOKEN_SPEC],
        out_shape=[pltpu.SemaphoreType.DMA(())] * (2 * ns) + [jax.ShapeDtypeStruct(a.shape, a.dtype) for a in parts]
        + [jax.ShapeDtypeStruct((3,) + a.shape[1:], a.dtype) for a in parts] + [TOKEN],
        input_output_aliases={k: k + 2 * ns for k in range(n)}, compiler_params=_split_params(),
    )(*parts)
    b = 2 * ns
    return list(outs[:ns]), list(outs[ns:b]), list(outs[b:b + n]), list(outs[b + n:b + 2 * n]), outs[b + 2 * n]


def _chip_exchange_wait(ssem, rsem, parts, landing, after, name):
    n = len(parts)
    ns = len(ssem)

    def body(*refs):
        ssem_ref, rsem_ref = refs[2 * n:2 * n + ns], refs[2 * n + ns:2 * n + 2 * ns]
        base = 2 * n + 2 * ns + 1
        srcs, outs = refs[base:base + n], refs[base + n:]
        x, y, c, q, chips, qs = _position()
        for k in range(n):
            for j, chip in enumerate(chips):
                cp = _remote(srcs[k].at[qs[j]], outs[k].at[j], ssem_ref[3 * k + j], rsem_ref[3 * k + j],
                             (chip[0], chip[1], c))
                cp.wait_send()
                cp.wait_recv()

    outs = pl.pallas_call(
        body, name=name, in_specs=[ANY] * (2 * n) + [SEM] * (2 * ns) + [ANY], out_specs=[ANY] * (2 * n),
        out_shape=[jax.ShapeDtypeStruct(a.shape, a.dtype) for a in list(parts) + list(landing)],
        input_output_aliases={k: k for k in range(2 * n)}, compiler_params=_split_params(),
    )(*parts, *landing, *ssem, *rsem, after)
    return list(outs[:n]), list(outs[n:])


def _pair_swap(views, name):
    n = len(views)

    def body(*refs):
        ins, outs = refs[:n], refs[n:2 * n]
        ssem, rsem = refs[2 * n:]
        x, y, c, _, _, _ = _position()
        cps = [_remote(ins[k].at[pl.ds(0, N_CHIPS), 1 - c], outs[k], ssem.at[k], rsem.at[k], (x, y, 1 - c))
               for k in range(n)]
        for cp in cps:
            cp.start()
        for cp in cps:
            cp.wait()

    return pl.pallas_call(
        body, name=name, in_specs=[ANY] * n, out_specs=[ANY] * n,
        out_shape=[jax.ShapeDtypeStruct((v.shape[0],) + v.shape[2:], v.dtype) for v in views],
        scratch_shapes=[pltpu.SemaphoreType.DMA((n,))] * 2,
    )(*views)


def _chip_exchange(parts, name):
    n = len(parts)

    def body(*refs):
        ins, outs = refs[:n], refs[n:2 * n]
        ssem, rsem = refs[2 * n:]
        x, y, c, q, chips, qs = _position()
        cps = []
        for k in range(n):
            for j, chip in enumerate(chips):
                cp = _remote(ins[k].at[qs[j]], outs[k].at[j], ssem.at[k, j], rsem.at[k, j], (chip[0], chip[1], c))
                cp.start()
                cps.append(cp)
        for cp in cps:
            cp.wait_recv()
        for cp in cps:
            cp.wait_send()

    return pl.pallas_call(
        body, name=name, in_specs=[ANY] * n, out_specs=[ANY] * n,
        out_shape=[jax.ShapeDtypeStruct((3,) + a.shape[1:], a.dtype) for a in parts],
        scratch_shapes=[pltpu.SemaphoreType.DMA((n, 3))] * 2,
    )(*parts)


def _pair_send(arrays, name):
    n = len(arrays)

    def body(*refs):
        ins, outs = refs[:n], refs[n:2 * n]
        ssem, rsem = refs[2 * n:]
        x, y, c, _, _, _ = _position()
        cps = [_remote(ins[k], outs[k], ssem.at[k], rsem.at[k], (x, y, 1 - c)) for k in range(n)]
        for cp in cps:
            cp.start()
        for cp in cps:
            cp.wait()

    return pl.pallas_call(
        body, name=name, in_specs=[ANY] * n, out_specs=[ANY] * n,
        out_shape=[jax.ShapeDtypeStruct(a.shape, a.dtype) for a in arrays],
        scratch_shapes=[pltpu.SemaphoreType.DMA((n,))] * 2,
    )(*arrays)


def _all_to_all_small(part):
    P = part.shape[0]

    def body(in_ref, out_ref, lsem, ssem, rsem):
        x, y, c = lax.axis_index("x"), lax.axis_index("y"), lax.axis_index("c")
        me = 4 * x + 2 * y + c
        flips = [(fx, fy, fc) for fx in (0, 1) for fy in (0, 1) for fc in (0, 1)][1:]
        peers = [((x + fx) % 2, (y + fy) % 2, (c + fc) % 2) for fx, fy, fc in flips]
        loc = pltpu.make_async_copy(in_ref, out_ref.at[me], lsem)
        loc.start()
        cps = [_remote(in_ref, out_ref.at[me], ssem.at[j], rsem.at[j], peer) for j, peer in enumerate(peers)]
        for cp in cps:
            cp.start()
        for j, (px, py, pc) in enumerate(peers):
            _remote(in_ref, out_ref.at[4 * px + 2 * py + pc], ssem.at[j], rsem.at[j], peers[j]).wait_recv()
        for cp in cps:
            cp.wait_send()
        loc.wait()

    return pl.pallas_call(
        body, name="small_grad_exchange", in_specs=[ANY], out_specs=ANY,
        out_shape=jax.ShapeDtypeStruct((8, P, LANES), F32),
        scratch_shapes=[pltpu.SemaphoreType.DMA(())] + [pltpu.SemaphoreType.DMA((7,))] * 2,
    )(part)


def _row_tile(rows, width, n_arrays):
    t = rows
    while t % 2 == 0 and t > 8 and 2 * n_arrays * t * width * 4 > VMEM_LIMIT // 2:
        t //= 2
    return t


def _scalar_grid(grid, in_specs, out_specs):
    return pltpu.PrefetchScalarGridSpec(num_scalar_prefetch=1, grid=grid, in_specs=in_specs, out_specs=out_specs)


def _cast_place(w3, layer, pos, name):
    _, r, c = w3.shape
    tr = _row_tile(r, c, 2)

    def body(pos_ref, w_ref, o_ref):
        o_ref[...] = w_ref[...].astype(BF)

    return pl.pallas_call(
        body, name=name,
        grid_spec=_scalar_grid((r // tr,), [pl.BlockSpec((None, tr, c), lambda i, pos: (layer, i, 0))],
                               pl.BlockSpec((None, tr, c), lambda i, pos: (pos[0], i, 0))),
        out_shape=jax.ShapeDtypeStruct((N_CHIPS, r, c), BF), compiler_params=_params(("parallel",)),
    )(pos, w3)


def _pair_sum(view, recv, pos, name):
    _, _, hr, c = view.shape
    tr = _row_tile(hr, c, 3)

    def body(pos_ref, g_ref, r_ref, o_ref):
        o_ref[...] = (g_ref[...].astype(F32) + r_ref[...].astype(F32)).astype(BF)

    blk = pl.BlockSpec((None, tr, c), lambda p, i, pos: (p, i, 0))
    return pl.pallas_call(
        body, name=name,
        grid_spec=_scalar_grid((N_CHIPS, hr // tr),
                               [pl.BlockSpec((None, None, tr, c), lambda p, i, pos: (p, pos[1], i, 0)), blk], blk),
        out_shape=jax.ShapeDtypeStruct(recv.shape, BF), compiler_params=_params(("parallel", "parallel")),
    )(pos, view, recv)


def _chip_sum(parts, recv, pos, name):
    _, hr, c = parts.shape
    tr = _row_tile(hr, c, 6)

    def body(pos_ref, p_ref, r_ref, o_ref):
        acc = p_ref[...].astype(F32)
        for j in range(3):
            acc = acc + r_ref[j].astype(F32)
        o_ref[...] = acc

    return pl.pallas_call(
        body, name=name,
        grid_spec=_scalar_grid((hr // tr,),
                               [pl.BlockSpec((None, tr, c), lambda i, pos: (pos[0], i, 0)),
                                pl.BlockSpec((3, tr, c), lambda i, pos: (0, i, 0))],
                               pl.BlockSpec((tr, c), lambda i, pos: (i, 0))),
        out_shape=jax.ShapeDtypeStruct((hr, c), F32), compiler_params=_params(("parallel",)),
    )(pos, parts, recv)


def _sum_slices(a, name):
    n, rows, width = a.shape
    tr = _row_tile(rows, width, n + 1)

    def body(a_ref, o_ref):
        acc = a_ref[0].astype(F32)
        for i in range(1, n):
            acc = acc + a_ref[i].astype(F32)
        o_ref[...] = acc

    return pl.pallas_call(
        body, name=name, grid=(rows // tr,), in_specs=[pl.BlockSpec((n, tr, width), lambda i: (0, i, 0))],
        out_specs=pl.BlockSpec((tr, width), lambda i: (i, 0)), out_shape=jax.ShapeDtypeStruct((rows, width), F32),
        compiler_params=_params(("parallel",)),
    )(a)


def _adamw_update(w, g, m, v):
    nm = ADAM_B1 * m + (1.0 - ADAM_B1) * g
    nv = ADAM_B2 * v + (1.0 - ADAM_B2) * (g * g)
    m_hat = nm / (1.0 - ADAM_B1 ** ADAM_STEP)
    v_hat = nv / (1.0 - ADAM_B2 ** ADAM_STEP)
    return -ADAM_LR * (m_hat / (jnp.sqrt(v_hat) + ADAM_EPS) + ADAM_WD * w), nm, nv


def _adamw(w, g, m, v, name):
    rows, width = w.shape
    tr = _row_tile(rows, width, 7)

    def body(w_ref, g_ref, m_ref, v_ref, d_ref, nm_ref, nv_ref):
        d_ref[...], nm_ref[...], nv_ref[...] = _adamw_update(w_ref[...], g_ref[...], m_ref[...], v_ref[...])

    blk = pl.BlockSpec((tr, width), lambda i: (i, 0))
    return pl.pallas_call(
        body, name=name, grid=(rows // tr,), in_specs=[blk] * 4, out_specs=[blk] * 3,
        out_shape=[jax.ShapeDtypeStruct((rows, width), F32)] * 3, compiler_params=_params(("parallel",)),
    )(w, g, m, v)


def _adamw_halves(w3, m3, v3, mine, other, pos, name):
    depth, r, c = w3.shape
    assert depth == 2
    hr = r // 2
    tr = _row_tile(hr, c, 11)
    sources = ((0, True, mine[0]), (0, False, other[0]), (1, True, mine[1]), (1, False, other[1]))

    def active(l, h, core, layer, own):
        mine_half = h == core
        return (l == layer) & (mine_half if own else jnp.logical_not(mine_half))

    def body(pos_ref, w_ref, m_ref, v_ref, *rest):
        g_refs, (go_ref, d_ref, nm_ref, nv_ref) = rest[:4], rest[4:]
        l, h = pl.program_id(0), pl.program_id(1)
        for (layer, own, _), g_ref in zip(sources, g_refs):
            @pl.when(active(l, h, pos_ref[1], layer, own))
            def _():
                gv = g_ref[...]
                go_ref[...] = gv
                d_ref[...], nm_ref[...], nv_ref[...] = _adamw_update(w_ref[...], gv, m_ref[...], v_ref[...])

    def gspec(layer, own):
        return pl.BlockSpec((tr, c), lambda l, h, i, pos: (jnp.where(active(l, h, pos[1], layer, own), i, 0), 0))

    blk = pl.BlockSpec((None, None, tr, c), lambda l, h, i, pos: (l, h, i, 0))
    view = lambda a: a.reshape(depth, 2, hr, c)
    outs = pl.pallas_call(
        body, name=name,
        grid_spec=_scalar_grid((depth, 2, hr // tr), [blk] * 3 + [gspec(layer, own) for layer, own, _ in sources],
                               [blk] * 4),
        out_shape=[jax.ShapeDtypeStruct((depth, 2, hr, c), F32)] * 4,
        compiler_params=_params(("parallel", "parallel", "parallel")),
    )(pos, view(w3), view(m3), view(v3), *[s[2] for s in sources])
    return [o.reshape(w3.shape) for o in outs]


BIG = ("w_in", "w_pool_up", "w_conv_out", "w_attn_up", "w_o", "w_ff1", "w_ff2")
SMALL = ("norm_mix", "b_gate", "pool_mix", "pool_scale", "conv_w", "q_gain", "k_gain", "norm_mlp")
ORDER = ("norm_mix", "w_in", "b_gate", "pool_mix", "pool_scale", "conv_w", "q_gain", "k_gain", "w_pool_up",
         "w_conv_out", "w_attn_up", "w_o", "norm_mlp", "w_ff1", "w_ff2")
COLUMN_SHARDED = ("w_in", "w_pool_up", "w_conv_out", "w_attn_up", "w_ff1")


def _layer_weights(l, gathered, small, q):
    w = {}
    for name in BIG:
        g4 = gathered[l][name]
        if name == "w_in":
            w[name] = jnp.transpose(g4, (1, 0, 2)).reshape(g4.shape[1], N_CHIPS * g4.shape[2])
        elif name in COLUMN_SHARDED:
            w[name] = g4
        else:
            w[name] = g4.reshape(N_CHIPS * g4.shape[1], g4.shape[2])
    w["norm_mix"] = small["norm_mix"][l][None]
    w["norm_mlp"] = small["norm_mlp"][l][None]
    w["b_gate"] = small["b_gate"][l][None]
    w["pool_mix"] = small["pool_mix"][l].astype(BF)
    w["pool_scale"] = small["pool_scale"][l][None]
    w["conv_w"] = jnp.pad(small["conv_w_full"][l], ((0, 5), (0, 0)))
    w["q_gain"] = jnp.tile(small["q_gain"][l], 2)[None]
    w["k_gain"] = jnp.tile(small["k_gain"][l], 2)[None]
    return w


def _to_chip_major(name, g):
    if name == "w_in":
        return jnp.transpose(g.reshape(g.shape[0], N_CHIPS, g.shape[1] // N_CHIPS), (1, 0, 2))
    if name in COLUMN_SHARDED:
        return g
    return g.reshape(N_CHIPS, g.shape[0] // N_CHIPS, g.shape[1])


def _pad8(a):
    a = a.reshape(-1, LANES)
    return jnp.pad(a, ((0, (-a.shape[0]) % 8), (0, 0)))


def kernel(x, norm_mix, w_in, b_gate, pool_mix, pool_scale, conv_w, q_gain, k_gain, w_pool_up, w_conv_out, w_attn_up, w_o, norm_mlp, w_ff1, w_ff2, loss_target, m_norm_mix, m_w_in, m_b_gate, m_pool_mix, m_pool_scale, m_conv_w, m_q_gain, m_k_gain, m_w_pool_up, m_w_conv_out, m_w_attn_up, m_w_o, m_norm_mlp, m_w_ff1, m_w_ff2, v_norm_mix, v_w_in, v_b_gate, v_pool_mix, v_pool_scale, v_conv_w, v_q_gain, v_k_gain, v_w_pool_up, v_w_conv_out, v_w_attn_up, v_w_o, v_norm_mlp, v_w_ff1, v_w_ff2):
    weights = dict(norm_mix=norm_mix, w_in=w_in, b_gate=b_gate, pool_mix=pool_mix, pool_scale=pool_scale, conv_w=conv_w,
                   q_gain=q_gain, k_gain=k_gain, w_pool_up=w_pool_up, w_conv_out=w_conv_out, w_attn_up=w_attn_up,
                   w_o=w_o, norm_mlp=norm_mlp, w_ff1=w_ff1, w_ff2=w_ff2)
    moms = dict(norm_mix=m_norm_mix, w_in=m_w_in, b_gate=m_b_gate, pool_mix=m_pool_mix, pool_scale=m_pool_scale,
                conv_w=m_conv_w, q_gain=m_q_gain, k_gain=m_k_gain, w_pool_up=m_w_pool_up, w_conv_out=m_w_conv_out,
                w_attn_up=m_w_attn_up, w_o=m_w_o, norm_mlp=m_norm_mlp, w_ff1=m_w_ff1, w_ff2=m_w_ff2)
    vels = dict(norm_mix=v_norm_mix, w_in=v_w_in, b_gate=v_b_gate, pool_mix=v_pool_mix, pool_scale=v_pool_scale,
                conv_w=v_conv_w, q_gain=v_q_gain, k_gain=v_k_gain, w_pool_up=v_w_pool_up, w_conv_out=v_w_conv_out,
                w_attn_up=v_w_attn_up, w_o=v_w_o, norm_mlp=v_norm_mlp, w_ff1=v_w_ff1, w_ff2=v_w_ff2)
    depth = norm_mix.shape[0]
    q = 2 * lax.axis_index("x") + lax.axis_index("y")
    pos = _position_operand()

    assert depth == 2, "the second layer's gather hides behind the first layer's forward, and likewise backward"
    bufs = [[_cast_place(weights[n], l, pos, f"cast_{n}_l{l}") for n in BIG] for l in range(depth)]
    gathered = [dict(zip(BIG, _gather(bufs[0], "gather_l0"))), None]
    g_ssem, g_rsem, g_views, g_token = _gather_start(bufs[1], "gather_start_l1")
    cw_all = _all_to_all_small(_pad8(jnp.pad(conv_w.reshape(-1), (0, (-conv_w.size) % LANES))))
    conv_w_full = jnp.concatenate(
        [cw_all[2 * p].reshape(-1)[:conv_w.size].reshape(conv_w.shape) for p in range(N_CHIPS)], axis=-1)
    small = dict(weights)
    small["conv_w_full"] = conv_w_full

    h = x[0]
    saved, wl = [], []
    for l in range(depth):
        if l == 1:
            gathered[1] = dict(zip(BIG, _gather_finish(g_ssem, g_rsem, g_views, h, "gather_wait_l1", "gather_forward_l1",
                                                       [b.shape for b in bufs[1]])))
        wl.append(_layer_weights(l, gathered, small, q))
        h, s = _layer_fwd(h, wl[l], f"l{l}", after=g_token if l == 0 else None)
        saved.append(s)
    dh, loss_row = _loss_grad(h, loss_target[0], "loss")

    def pair_stage(l):
        views = [_halves(_to_chip_major(n, grads[l][n])) for n in BIG]
        from_sibling = _pair_swap(views, f"grad_pair_swap_l{l}")
        return [_pair_sum(views[k], from_sibling[k], pos, f"pair_sum_{n}_l{l}") for k, n in enumerate(BIG)]

    def sum_stage(l, parts, from_chips):
        mine[l] = [_chip_sum(parts[k], from_chips[k], pos, f"chip_sum_{n}_l{l}") for k, n in enumerate(BIG)]
        other[l] = _pair_send(mine[l], f"grad_pair_send_l{l}")

    grads, mine, other = [None] * depth, [None] * depth, [None] * depth
    dh, grads[1] = _layer_bwd(dh, wl[1], saved[1], "l1")
    e_ssem, e_rsem, parts1, landing1, e_token = _chip_exchange_start(pair_stage(1), "grad_chip_exchange_start_l1")
    dh, grads[0] = _layer_bwd(dh, wl[0], saved[0], "l0", after=e_token)
    parts1, from_chips1 = _chip_exchange_wait(e_ssem, e_rsem, parts1, landing1, dh, "grad_chip_exchange_wait_l1")
    sum_stage(1, parts1, from_chips1)
    parts0 = pair_stage(0)
    sum_stage(0, parts0, _chip_exchange(parts0, "grad_chip_exchange_l0"))
    loss = lax.psum(loss_row[0, 0], ("x", "y", "c"))
    full = {}

    pieces = []
    for n in SMALL:
        per_layer = [grads[l][n] for l in range(depth)]
        if n == "conv_w":
            per_layer = [p[:3] for p in per_layer]
        pieces.append(_pad8(jnp.stack(per_layer).reshape(-1)))
    packed = jnp.concatenate(pieces, axis=0)
    summed = _sum_slices(_all_to_all_small(packed), "small_sum")
    row = 0
    for n, piece in zip(SMALL, pieces):
        size = weights[n].size if n != "conv_w" else depth * 3 * 512
        flat = summed[row:row + piece.shape[0]].reshape(-1)[:size]
        row += piece.shape[0]
        if n == "conv_w":
            full[n] = lax.dynamic_slice_in_dim(flat.reshape(depth, 3, 512), q * conv_w.shape[2], conv_w.shape[2], axis=2)
        else:
            full[n] = flat.reshape(weights[n].shape)

    deltas, new_m, new_v = {}, {}, {}
    for k, n in enumerate(BIG):
        full[n], deltas[n], new_m[n], new_v[n] = _adamw_halves(
            weights[n], moms[n], vels[n], [mine[l][k] for l in range(depth)], [other[l][k] for l in range(depth)], pos,
            f"adamw_{n}")
    for n in SMALL:
        shape = weights[n].shape
        two_d = (-1, shape[-1]) if n not in ("conv_w", "q_gain", "k_gain") else (1, -1)
        d2, m2, v2 = _adamw(weights[n].reshape(two_d), full[n].reshape(two_d), moms[n].reshape(two_d),
                            vels[n].reshape(two_d), f"adamw_{n}")
        deltas[n], new_m[n], new_v[n] = d2.reshape(shape), m2.reshape(shape), v2.reshape(shape)
        full[n] = full[n].reshape(shape)
    return (loss, dh[None], *[full[n] for n in ORDER], *[deltas[n] for n in ORDER], *[new_m[n] for n in ORDER],
            *[new_v[n] for n in ORDER])
```

```python
import functools

import jax
import jax.numpy as jnp
from jax import lax
from jax.experimental import pallas as pl
from jax.experimental.pallas import tpu as pltpu

F32 = jnp.float32
BF = jnp.bfloat16
MESH_ID = pl.DeviceIdType.MESH
ANY = pl.BlockSpec(memory_space=pl.ANY)

EPS = 1e-6
MASK_VALUE = -1e30
POOL_WINDOWS = (2, 4, 8, 16)
ATTN_DILATIONS = (1, 4, 16)
ATTN_BLOCK = 128
HEAD_DIM = 64
OFF_Q, OFF_K, OFF_V, OFF_GATE = 2048, 2816, 3584, 4352
N_CHIPS = 4
ADAM_LR, ADAM_B1, ADAM_B2, ADAM_EPS, ADAM_WD, ADAM_STEP = 0.001, 0.9, 0.999, 1e-08, 0.01, 10

VMEM_LIMIT = 48 * 1024 * 1024
LANES = 128

_DIMS = {"nn": (((1,), (0,)), ((), ())), "nt": (((1,), (1,)), ((), ())), "tn": (((0,), (0,)), ((), ()))}


def _params(sem):
    return pltpu.CompilerParams(dimension_semantics=sem, vmem_limit_bytes=VMEM_LIMIT)


def _dot(a, b, mode="nn"):
    return lax.dot_general(a, b, _DIMS[mode], preferred_element_type=F32)


def _mm(a, b, mode, name, *, tm, tn, tk, out_dtype=F32, res=None, aux=None, epi=None, n_outer=False,
        b_shards=False, out_shards=False, after=None):
    if mode == "tn":
        K, M = a.shape
    else:
        M, K = a.shape
    if b_shards:
        if mode == "nn":
            assert b.shape[1] == K
            N = b.shape[2] * N_CHIPS
        else:
            assert mode == "nt"
            N = b.shape[1]
            assert b.shape[2] * N_CHIPS == K
    else:
        N = b.shape[0] if mode == "nt" else b.shape[1]
    tm, tn, tk = min(tm, M), min(tn, N), min(tk, K)
    assert M % tm == 0 and N % tn == 0 and K % tk == 0
    nk = K // tk
    if n_outer:
        grid = (N // tn, M // tm, nk)
        ij = lambda p, q_: (q_, p)
    else:
        grid = (M // tm, N // tn, nk)
        ij = lambda p, q_: (p, q_)

    def amap(p, q_, k):
        i, j = ij(p, q_)
        return (k, i) if mode == "tn" else (i, k)

    a_spec = pl.BlockSpec((tk, tm) if mode == "tn" else (tm, tk), amap)
    if b_shards:
        if mode == "nn":
            per = (N // N_CHIPS) // tn
            assert per >= 1 and (N // N_CHIPS) % tn == 0

            def bmap(p, q_, k):
                i, j = ij(p, q_)
                return (j // per, k, j % per)

            b_spec = pl.BlockSpec((None, tk, tn), bmap)
        else:
            per = (K // N_CHIPS) // tk
            assert per >= 1 and (K // N_CHIPS) % tk == 0

            def bmap(p, q_, k):
                i, j = ij(p, q_)
                return (k // per, j, k % per)

            b_spec = pl.BlockSpec((None, tn, tk), bmap)
    else:
        def bmap(p, q_, k):
            i, j = ij(p, q_)
            return (j, k) if mode == "nt" else (k, j)

        b_spec = pl.BlockSpec((tn, tk) if mode == "nt" else (tk, tn), bmap)

    def omap(p, q_, k):
        return ij(p, q_)

    o_spec = pl.BlockSpec((tm, tn), omap)
    if out_shards:
        per_o = (N // N_CHIPS) // tn
        assert per_o >= 1 and (N // N_CHIPS) % tn == 0

        def osmap(p, q_, k):
            i, j = ij(p, q_)
            return (j // per_o, i, j % per_o)

        out_spec0 = pl.BlockSpec((None, tm, tn), osmap)
        out_shape0 = jax.ShapeDtypeStruct((N_CHIPS, M, N // N_CHIPS), out_dtype)
    else:
        out_spec0 = o_spec
        out_shape0 = jax.ShapeDtypeStruct((M, N), out_dtype)

    in_specs = [a_spec, b_spec]
    args = [a, b]
    if res is not None:
        in_specs.append(o_spec)
        args.append(res)
    if aux is not None:
        in_specs.append(o_spec)
        args.append(aux)
    if after is not None:
        in_specs.append(ANY)
        args.append(after)
    out_specs = [out_spec0]
    out_shape = [out_shape0]
    n_out = len(out_shape)
    has_res, has_aux, has_after = res is not None, aux is not None, after is not None

    def body(*refs):
        a_ref, b_ref = refs[0], refs[1]
        pos = 2
        res_ref = aux_ref = None
        if has_res:
            res_ref = refs[pos]
            pos += 1
        if has_aux:
            aux_ref = refs[pos]
            pos += 1
        if has_after:
            pos += 1
        outs = refs[pos:pos + n_out]
        part = _dot(a_ref[...].astype(BF), b_ref[...].astype(BF), mode)

        def finish(acc):
            if res_ref is not None:
                acc = res_ref[...] + acc
            if epi == "relu2":
                r = jnp.maximum(acc, 0.0)
                outs[0][...] = (r * r).astype(out_dtype)
            elif epi == "drelu2":
                outs[0][...] = (acc * (2.0 * jnp.sqrt(aux_ref[...].astype(F32)))).astype(out_dtype)
            else:
                outs[0][...] = acc.astype(out_dtype)

        if nk == 1:
            finish(part)
        else:
            acc_ref = refs[pos + n_out]
            k = pl.program_id(2)

            @pl.when(k == 0)
            def _():
                acc_ref[...] = part

            @pl.when(k > 0)
            def _():
                acc_ref[...] += part

            @pl.when(k == nk - 1)
            def _():
                finish(acc_ref[...])

    scratch = [pltpu.VMEM((tm, tn), F32)] if nk > 1 else []
    out = pl.pallas_call(
        body, name=name, grid=grid, in_specs=in_specs, out_specs=out_specs, out_shape=out_shape,
        scratch_shapes=scratch, compiler_params=_params(("parallel", "parallel", "arbitrary")),
    )(*args)
    return out if n_out > 1 else out[0]


def _rms_fwd(x, gain, name, after=None):
    T, D = x.shape
    tm = min(512, T)

    def body(x_ref, g_ref, *rest):
        o_ref = rest[-1]
        xv = x_ref[...]
        r = lax.rsqrt(jnp.mean(xv * xv, axis=-1, keepdims=True) + EPS)
        o_ref[...] = ((xv * r) * g_ref[...]).astype(BF)

    extra = [] if after is None else list(after) if isinstance(after, (list, tuple)) else [after]
    return pl.pallas_call(
        body, name=name, grid=(T // tm,),
        in_specs=[pl.BlockSpec((tm, D), lambda i: (i, 0)), pl.BlockSpec((1, D), lambda i: (0, 0))] + [ANY] * len(extra),
        out_specs=pl.BlockSpec((tm, D), lambda i: (i, 0)), out_shape=jax.ShapeDtypeStruct((T, D), BF),
        compiler_params=_params(("parallel",)),
    )(x, gain, *extra)


def _rms_bwd(dh, x, gain, dres, name):
    T, D = x.shape
    tm = min(512, T)

    def body(dh_ref, x_ref, g_ref, dres_ref, dx_ref, dg_ref):
        xv = x_ref[...]
        r = lax.rsqrt(jnp.mean(xv * xv, axis=-1, keepdims=True) + EPS)
        xhat = xv * r
        dhv = dh_ref[...]
        dy = dhv * g_ref[...]
        dx_ref[...] = dres_ref[...] + r * (dy - xhat * jnp.mean(dy * xhat, axis=-1, keepdims=True))

        @pl.when(pl.program_id(0) == 0)
        def _():
            dg_ref[...] = jnp.zeros_like(dg_ref)

        dg_ref[...] += jnp.sum(dhv * xhat, axis=0, keepdims=True)

    row = pl.BlockSpec((tm, D), lambda i: (i, 0))
    vec = pl.BlockSpec((1, D), lambda i: (0, 0))
    return pl.pallas_call(
        body, name=name, grid=(T // tm,), in_specs=[row, row, vec, row], out_specs=[row, vec],
        out_shape=[jax.ShapeDtypeStruct((T, D), F32), jax.ShapeDtypeStruct((1, D), F32)],
        compiler_params=_params(("arbitrary",)),
    )(dh, x, gain, dres)


def _loss_grad(y, target, name):
    T, D = y.shape
    tm = min(512, T)

    def body(y_ref, t_ref, dy_ref, l_ref):
        e = y_ref[...] - t_ref[...]
        dy_ref[...] = e / float(D)

        @pl.when(pl.program_id(0) == 0)
        def _():
            l_ref[...] = jnp.zeros_like(l_ref)

        l_ref[...] += 0.5 * jnp.sum(jnp.mean(e * e, axis=-1, keepdims=True))

    row = pl.BlockSpec((tm, D), lambda i: (i, 0))
    return pl.pallas_call(
        body, name=name, grid=(T // tm,), in_specs=[row, row],
        out_specs=[row, pl.BlockSpec((1, LANES), lambda i: (0, 0))],
        out_shape=[jax.ShapeDtypeStruct((T, D), F32), jax.ShapeDtypeStruct((1, LANES), F32)],
        compiler_params=_params(("arbitrary",)),
    )(y, target)


POOL_HALO = 16
CONV_HALO = 8


def _causal_window_sum(v, w):
    s, sh = v, 1
    while sh < w:
        s = s + pltpu.roll(s, sh, 0)
        sh *= 2
    return s


def _anticausal_window_sum(v, w):
    n = v.shape[0]
    s, sh = v, 1
    while sh < w:
        s = s + pltpu.roll(s, n - sh, 0)
        sh *= 2
    return s


def _poolconv_fwd(z, pmix_b, pscale, convw, name):
    T = z.shape[0]
    R = min(512, T)
    PH, CH = R // POOL_HALO, R // CONV_HALO

    def body(u_ref, uh_ref, b_ref, c_ref, ch_ref, x_ref, xh_ref, mix_ref, sc_ref, cw_ref, yp_ref, yc_ref):
        i = pl.program_id(0)
        keep = (i > 0).astype(F32)
        row = i * R + lax.broadcasted_iota(jnp.int32, (R, 1), 0)
        w_all = jnp.concatenate([uh_ref[...] * keep, u_ref[...]], axis=0)
        for g, w in enumerate(POOL_WINDOWS):
            cols = slice(128 * g, 128 * (g + 1))
            wg = w_all[:, cols]
            s = _causal_window_sum(wg, w)[POOL_HALO:]
            cnt = jnp.minimum(row + 1, w).astype(F32)
            dgrp = s / cnt - wg[POOL_HALO:]
            y = _dot(dgrp.astype(BF), mix_ref[g]) * sc_ref[:, cols]
            yp_ref[:, cols] = y.astype(BF)
        uc = jnp.concatenate([ch_ref[...] * xh_ref[...] * keep, c_ref[...] * x_ref[...]], axis=0)
        yc = cw_ref[2:3, :] * uc + cw_ref[0:1, :] * pltpu.roll(uc, 2, 0) + cw_ref[1:2, :] * pltpu.roll(uc, 1, 0)
        yc_ref[...] = (b_ref[...] * yc[CONV_HALO:]).astype(BF)

    def main(cb):
        return pl.BlockSpec((R, 512), lambda i: (i, cb))

    def prev(cb, halo, per):
        return pl.BlockSpec((halo, 512), lambda i: (jnp.maximum(i * per - 1, 0), cb))

    full = lambda a: pl.BlockSpec(a.shape, lambda i: (0,) * a.ndim)
    return pl.pallas_call(
        body, name=name, grid=(T // R,),
        in_specs=[main(0), prev(0, POOL_HALO, PH), main(1), main(2), prev(2, CONV_HALO, CH), main(3),
                  prev(3, CONV_HALO, CH), full(pmix_b), full(pscale), full(convw)],
        out_specs=[pl.BlockSpec((R, 512), lambda i: (i, 0))] * 2,
        out_shape=[jax.ShapeDtypeStruct((T, 512), BF)] * 2,
        compiler_params=_params(("parallel",)),
    )(z, z, z, z, z, z, z, pmix_b, pscale, convw)


def _poolconv_bwd(z, dyp, dyc, pmix_b, pscale, convw, name):
    T = z.shape[0]
    R = min(512, T)
    PH, CH = R // POOL_HALO, R // CONV_HALO
    nsteps = T // R

    def body(u_ref, uh_ref, b_ref, bn_ref, c_ref, ch_ref, x_ref, xh_ref, dyp_ref, dypn_ref, dyc_ref, dycn_ref,
             mix_ref, sc_ref, cw_ref, dz_ref, dmix_ref, dsc_ref, dcw_ref):
        i = pl.program_id(0)
        keep_prev = (i > 0).astype(F32)
        keep_next = (i < nsteps - 1).astype(F32)

        @pl.when(i == 0)
        def _():
            dmix_ref[...] = jnp.zeros_like(dmix_ref)
            dsc_ref[...] = jnp.zeros_like(dsc_ref)
            dcw_ref[...] = jnp.zeros_like(dcw_ref)

        row = i * R + lax.broadcasted_iota(jnp.int32, (R, 1), 0)
        row_ext = i * R + lax.broadcasted_iota(jnp.int32, (R + POOL_HALO, 1), 0)
        w_all = jnp.concatenate([uh_ref[...] * keep_prev, u_ref[...]], axis=0)
        dyp_ext = jnp.concatenate([dyp_ref[...], dypn_ref[...] * keep_next], axis=0)
        for g, w in enumerate(POOL_WINDOWS):
            cols = slice(128 * g, 128 * (g + 1))
            wg = w_all[:, cols]
            s = _causal_window_sum(wg, w)[POOL_HALO:]
            cnt = jnp.minimum(row + 1, w).astype(F32)
            dgrp = (s / cnt - wg[POOL_HALO:]).astype(BF)
            y_pre = _dot(dgrp, mix_ref[g])
            dsc_ref[:, cols] += jnp.sum(dyp_ref[:, cols] * y_pre, axis=0, keepdims=True)
            dyb = (dyp_ext[:, cols] * sc_ref[:, cols]).astype(BF)
            dmix_ref[cols, :] += _dot(dgrp, dyb[:R], "tn")
            dd = _dot(dyb, mix_ref[g], "nt")
            cnt_ext = jnp.minimum(row_ext + 1, w).astype(F32)
            e = _anticausal_window_sum(dd / cnt_ext, w)
            dz_ref[:, cols] = (e[:R] - dd[:R]).astype(BF)
        cw0, cw1, cw2 = cw_ref[0:1, :], cw_ref[1:2, :], cw_ref[2:3, :]
        uc = jnp.concatenate([ch_ref[...] * xh_ref[...] * keep_prev, c_ref[...] * x_ref[...]], axis=0)
        uc1 = pltpu.roll(uc, 1, 0)[CONV_HALO:]
        uc2 = pltpu.roll(uc, 2, 0)[CONV_HALO:]
        uc0 = uc[CONV_HALO:]
        yc = cw2 * uc0 + cw0 * uc2 + cw1 * uc1
        dycv = dyc_ref[...]
        dz_ref[:, 512:1024] = (dycv * yc).astype(BF)
        dv_ext = jnp.concatenate([dycv * b_ref[...], dycn_ref[...] * bn_ref[...] * keep_next], axis=0)
        n_ext = R + CONV_HALO
        duc = (cw2 * dv_ext + cw1 * pltpu.roll(dv_ext, n_ext - 1, 0) + cw0 * pltpu.roll(dv_ext, n_ext - 2, 0))[:R]
        dv = dv_ext[:R]
        dcw_ref[0:1, :] += jnp.sum(dv * uc2, axis=0, keepdims=True)
        dcw_ref[1:2, :] += jnp.sum(dv * uc1, axis=0, keepdims=True)
        dcw_ref[2:3, :] += jnp.sum(dv * uc0, axis=0, keepdims=True)
        dz_ref[:, 1024:1536] = (duc * x_ref[...]).astype(BF)
        dz_ref[:, 1536:2048] = (duc * c_ref[...]).astype(BF)

    def main(cb):
        return pl.BlockSpec((R, 512), lambda i: (i, cb))

    def prev(cb, halo, per):
        return pl.BlockSpec((halo, 512), lambda i: (jnp.maximum(i * per - 1, 0), cb))

    def nxt(cb, halo, per):
        return pl.BlockSpec((halo, 512), lambda i: (jnp.minimum((i + 1) * per, T // halo - 1), cb))

    full = lambda a: pl.BlockSpec(a.shape, lambda i: (0,) * a.ndim)
    return pl.pallas_call(
        body, name=name, grid=(nsteps,),
        in_specs=[main(0), prev(0, POOL_HALO, PH), main(1), nxt(1, CONV_HALO, CH), main(2), prev(2, CONV_HALO, CH),
                  main(3), prev(3, CONV_HALO, CH), main(0), nxt(0, POOL_HALO, PH), main(0), nxt(0, CONV_HALO, CH),
                  full(pmix_b), full(pscale), full(convw)],
        out_specs=[pl.BlockSpec((R, 2048), lambda i: (i, 0)), pl.BlockSpec((512, 128), lambda i: (0, 0)),
                   pl.BlockSpec((1, 512), lambda i: (0, 0)), pl.BlockSpec((8, 512), lambda i: (0, 0))],
        out_shape=[jax.ShapeDtypeStruct((T, 2048), BF), jax.ShapeDtypeStruct((512, 128), F32),
                   jax.ShapeDtypeStruct((1, 512), F32), jax.ShapeDtypeStruct((8, 512), F32)],
        compiler_params=_params(("arbitrary",)),
    )(z, z, z, z, z, z, z, z, dyp, dyp, dyc, dyc, pmix_b, pscale, convw)


def _head_norm(x, g2, ma):
    sq = x * x
    sa = jnp.sum(jnp.where(ma, sq, 0.0), axis=-1, keepdims=True)
    sb = jnp.sum(jnp.where(ma, 0.0, sq), axis=-1, keepdims=True)
    r = jnp.where(ma, lax.rsqrt(sa / HEAD_DIM + EPS), lax.rsqrt(sb / HEAD_DIM + EPS))
    return x * r, r


def _head_norm_bwd(dy, xhat, r, g2, ma):
    dxh = dy * g2
    pr = dxh * xhat
    sa = jnp.sum(jnp.where(ma, pr, 0.0), axis=-1, keepdims=True)
    sb = jnp.sum(jnp.where(ma, 0.0, pr), axis=-1, keepdims=True)
    mh = jnp.where(ma, sa, sb) / HEAD_DIM
    return r * (dxh - xhat * mh)


def _head_col(tile, hm):
    return jnp.max(jnp.where(hm, tile, -jnp.inf), axis=-1, keepdims=True)


def _attn_masks(other_block_exists):
    lane = lax.broadcasted_iota(jnp.int32, (ATTN_BLOCK, ATTN_BLOCK), 1)
    qi = lax.broadcasted_iota(jnp.int32, (ATTN_BLOCK, ATTN_BLOCK), 0)
    never = (1 - other_block_exists.astype(jnp.int32)) * (2 * ATTN_BLOCK)
    return lane < HEAD_DIM, lane <= qi, lane >= qi + never


def _attn_fwd(qf, kf, vf, gq2, gk2, d, name):
    L = qf.shape[0]
    nb = L // ATTN_BLOCK
    scale = HEAD_DIM ** -0.5

    def body(q_ref, kc_ref, kp_ref, vc_ref, vp_ref, gq_ref, gk_ref, o_ref, lse_ref):
        j = pl.program_id(1)
        ma, mask_c, mask_p = _attn_masks(j > 0)
        for t in range(2):
            sl = slice(LANES * t, LANES * (t + 1))
            qn = _head_norm(q_ref[:, sl], gq_ref[...], ma)[0] * gq_ref[...]
            kcb = (_head_norm(kc_ref[:, sl], gk_ref[...], ma)[0] * gk_ref[...]).astype(BF)
            kpb = (_head_norm(kp_ref[:, sl], gk_ref[...], ma)[0] * gk_ref[...]).astype(BF)
            vcb = vc_ref[:, sl].astype(BF)
            vpb = vp_ref[:, sl].astype(BF)
            o_t = lse_t = None
            for hm in (ma, jnp.logical_not(ma)):
                qh = jnp.where(hm, qn, 0.0).astype(BF)
                s_c = jnp.where(mask_c, _dot(qh, kcb, "nt") * scale, MASK_VALUE)
                s_p = jnp.where(mask_p, _dot(qh, kpb, "nt") * scale, MASK_VALUE)
                m = jnp.maximum(jnp.max(s_c, axis=-1, keepdims=True), jnp.max(s_p, axis=-1, keepdims=True))
                p_c = jnp.exp(s_c - m)
                p_p = jnp.exp(s_p - m)
                den = jnp.sum(p_c, axis=-1, keepdims=True) + jnp.sum(p_p, axis=-1, keepdims=True)
                o = (_dot(p_c.astype(BF), vcb) + _dot(p_p.astype(BF), vpb)) / den
                lse = jnp.broadcast_to(m + jnp.log(den), o.shape)
                o_t = o if o_t is None else jnp.where(ma, o_t, o)
                lse_t = lse if lse_t is None else jnp.where(ma, lse_t, lse)
            o_ref[:, sl] = o_t
            lse_ref[:, sl] = lse_t

    cur = pl.BlockSpec((ATTN_BLOCK, 256), lambda r, j: (j, r))
    prv = pl.BlockSpec((ATTN_BLOCK, 256), lambda r, j: (jnp.maximum(j - 1, 0), r))
    vec = pl.BlockSpec((1, LANES), lambda r, j: (0, 0))
    return pl.pallas_call(
        body, name=name, grid=(d, nb), in_specs=[cur, cur, prv, cur, prv, vec, vec], out_specs=[cur, cur],
        out_shape=[jax.ShapeDtypeStruct(qf.shape, F32)] * 2,
        compiler_params=_params(("parallel", "parallel")),
    )(qf, kf, kf, vf, vf, gq2, gk2)


def _attn_bwd(qf, kf, vf, dof, cf, lsef, gq2, gk2, d, name, after=None):
    L = qf.shape[0]
    nb = L // ATTN_BLOCK
    scale = HEAD_DIM ** -0.5
    extra = [] if after is None else [after]

    def body(q_ref, qn_ref, k_ref, kp_ref, v_ref, vp_ref, do_ref, don_ref, c_ref, cn_ref, lse_ref, lsen_ref,
             gq_ref, gk_ref, *rest):
        dq_ref, dk_ref, dv_ref, dgq_ref, dgk_ref = rest[len(extra):]
        r_id, j = pl.program_id(0), pl.program_id(1)
        ma, mask_c, mask_p = _attn_masks(j > 0)
        mask_n = _attn_masks(j < nb - 1)[2]

        @pl.when((r_id == 0) & (j == 0))
        def _():
            dgq_ref[...] = jnp.zeros_like(dgq_ref)
            dgk_ref[...] = jnp.zeros_like(dgk_ref)

        gq, gk = gq_ref[...], gk_ref[...]
        for t in range(2):
            sl = slice(LANES * t, LANES * (t + 1))
            qhat, rq = _head_norm(q_ref[:, sl], gq, ma)
            qn = qhat * gq
            qn_next = _head_norm(qn_ref[:, sl], gq, ma)[0] * gq
            khat, rk = _head_norm(k_ref[:, sl], gk, ma)
            kcb = (khat * gk).astype(BF)
            kpb = (_head_norm(kp_ref[:, sl], gk, ma)[0] * gk).astype(BF)
            vcb = v_ref[:, sl].astype(BF)
            vpb = vp_ref[:, sl].astype(BF)
            do_t, don_t = do_ref[:, sl], don_ref[:, sl]
            c_t, cn_t = c_ref[:, sl], cn_ref[:, sl]
            lse_t, lsen_t = lse_ref[:, sl], lsen_ref[:, sl]
            dq_t = None
            dk_t = jnp.zeros((ATTN_BLOCK, LANES), F32)
            dv_t = jnp.zeros((ATTN_BLOCK, LANES), F32)
            for hm in (ma, jnp.logical_not(ma)):
                qh = jnp.where(hm, qn, 0.0).astype(BF)
                doh = jnp.where(hm, do_t, 0.0).astype(BF)
                lse_h = _head_col(lse_t, hm)
                c_h = _head_col(c_t, hm)
                s_c = jnp.where(mask_c, _dot(qh, kcb, "nt") * scale, MASK_VALUE)
                s_p = jnp.where(mask_p, _dot(qh, kpb, "nt") * scale, MASK_VALUE)
                p_c = jnp.exp(s_c - lse_h)
                p_p = jnp.exp(s_p - lse_h)
                ds_c = ((p_c * (_dot(doh, vcb, "nt") + c_h)) * scale).astype(BF)
                ds_p = ((p_p * (_dot(doh, vpb, "nt") + c_h)) * scale).astype(BF)
                dq_h = _dot(ds_c, kcb) + _dot(ds_p, kpb)
                dq_t = dq_h if dq_t is None else jnp.where(ma, dq_t, dq_h)
                qh_n = jnp.where(hm, qn_next, 0.0).astype(BF)
                doh_n = jnp.where(hm, don_t, 0.0).astype(BF)
                s_n = jnp.where(mask_n, _dot(qh_n, kcb, "nt") * scale, MASK_VALUE)
                p_n = jnp.exp(s_n - _head_col(lsen_t, hm))
                ds_n = ((p_n * (_dot(doh_n, vcb, "nt") + _head_col(cn_t, hm))) * scale).astype(BF)
                dv_t = dv_t + _dot(p_c.astype(BF), doh, "tn") + _dot(p_n.astype(BF), doh_n, "tn")
                dk_t = dk_t + _dot(ds_c, qh, "tn") + _dot(ds_n, qh_n, "tn")
            dq_ref[:, sl] = _head_norm_bwd(dq_t, qhat, rq, gq, ma).astype(BF)
            dk_ref[:, sl] = _head_norm_bwd(dk_t, khat, rk, gk, ma).astype(BF)
            dv_ref[:, sl] = dv_t.astype(BF)
            dgq_ref[...] += jnp.sum(dq_t * qhat, axis=0, keepdims=True)
            dgk_ref[...] += jnp.sum(dk_t * khat, axis=0, keepdims=True)

    cur = pl.BlockSpec((ATTN_BLOCK, 256), lambda r, j: (j, r))
    prv = pl.BlockSpec((ATTN_BLOCK, 256), lambda r, j: (jnp.maximum(j - 1, 0), r))
    nxt = pl.BlockSpec((ATTN_BLOCK, 256), lambda r, j: (jnp.minimum(j + 1, nb - 1), r))
    vec = pl.BlockSpec((1, LANES), lambda r, j: (0, 0))
    return pl.pallas_call(
        body, name=name, grid=(d, nb),
        in_specs=[cur, nxt, cur, prv, cur, prv, cur, nxt, cur, nxt, cur, nxt, vec, vec] + [ANY] * len(extra),
        out_specs=[cur, cur, cur, vec, vec],
        out_shape=[jax.ShapeDtypeStruct(qf.shape, BF)] * 3 + [jax.ShapeDtypeStruct((1, LANES), F32)] * 2,
        compiler_params=_params(("arbitrary", "arbitrary")),
    )(qf, qf, kf, kf, vf, vf, dof, dof, cf, cf, lsef, lsef, gq2, gk2, *extra)


MERGE_ROWS = 256
GATE_TILE = 256


def _group_mix(o_refs, lse_refs):
    lses = [r[...] for r in lse_refs]
    m = jnp.maximum(jnp.maximum(lses[0], lses[1]), lses[2])
    es = [jnp.exp(l - m) for l in lses]
    den = es[0] + es[1] + es[2]
    ws = [e / den for e in es]
    y = ws[0] * o_refs[0][...] + ws[1] * o_refs[1][...] + ws[2] * o_refs[2][...]
    return ws, y


def _sigmoid(v):
    return 1.0 / (1.0 + jnp.exp(-v))


def _merge_specs(T, z, bgate, gpu, gco, gau):
    tm = min(MERGE_ROWS, T)
    row = lambda w: pl.BlockSpec((tm, w), lambda i: (i, 0))
    gate0 = OFF_GATE // GATE_TILE
    gates = [pl.BlockSpec((tm, GATE_TILE), functools.partial(lambda i, cb: (i, cb), cb=gate0 + n))
             for n in range(3 * N_CHIPS)]
    full = lambda a: pl.BlockSpec(a.shape, lambda i: (0,) * a.ndim)
    specs = [row(512), row(512)] + [row(256)] * 6 + gates + [full(bgate), full(gpu), full(gco), full(gau)]
    return tm, row, specs


def _merge_fwd(yp, yc, o3, lse3, z, bgate, gpu, gco, gau, name):
    T = yp.shape[0]
    tm, row, specs = _merge_specs(T, z, bgate, gpu, gco, gau)

    def body(*refs):
        yp_ref, yc_ref = refs[0], refs[1]
        o_refs, lse_refs = refs[2:5], refs[5:8]
        zg = refs[8:20]
        b_ref, gpu_ref, gco_ref, gau_ref, out_ref = refs[20:25]
        yab = _group_mix(o_refs, lse_refs)[1].astype(BF)
        ys = (yp_ref[...], yc_ref[...], yab)
        ups = (gpu_ref, gco_ref, gau_ref)
        for n in range(N_CHIPS):
            acc = None
            for b in range(3):
                gcol = slice(1024 * b + GATE_TILE * n, 1024 * b + GATE_TILE * (n + 1))
                gate = _sigmoid(zg[N_CHIPS * b + n][...] + b_ref[:, gcol])
                term = gate * _dot(ys[b], ups[b][n])
                acc = term if acc is None else acc + term
            out_ref[:, GATE_TILE * n:GATE_TILE * (n + 1)] = acc.astype(BF)

    return pl.pallas_call(
        body, name=name, grid=(T // tm,), in_specs=specs, out_specs=row(1024),
        out_shape=jax.ShapeDtypeStruct((T, 1024), BF), compiler_params=_params(("parallel",)),
    )(yp, yc, *o3, *lse3, *([z] * 12), bgate, gpu, gco, gau)


def _merge_bwd(dm, yp, yc, o3, lse3, z, bgate, gpu, gco, gau, name):
    T = yp.shape[0]
    tm, row, specs = _merge_specs(T, z, bgate, gpu, gco, gau)
    nsteps = T // tm

    def body(*refs):
        dm_ref, yp_ref, yc_ref = refs[0:3]
        o_refs, lse_refs = refs[3:6], refs[6:9]
        zg = refs[9:21]
        b_ref, gpu_ref, gco_ref, gau_ref = refs[21:25]
        dzg_ref, dyp_ref, dyc_ref = refs[25:28]
        do_refs, c_refs = refs[28:31], refs[31:34]
        dgpu_ref, dgco_ref, dgau_ref, dbg_ref = refs[34:38]
        accs = refs[38:41]
        i = pl.program_id(0)

        @pl.when(i == 0)
        def _():
            for a in accs:
                a[...] = jnp.zeros_like(a)
            dbg_ref[...] = jnp.zeros_like(dbg_ref)

        ws, y = _group_mix(o_refs, lse_refs)
        ys = (yp_ref[...], yc_ref[...], y.astype(BF))
        ups = (gpu_ref, gco_ref, gau_ref)
        dys = [None, None, None]
        for n in range(N_CHIPS):
            dmn = dm_ref[:, GATE_TILE * n:GATE_TILE * (n + 1)]
            for b in range(3):
                gcol = slice(1024 * b + GATE_TILE * n, 1024 * b + GATE_TILE * (n + 1))
                gate = _sigmoid(zg[N_CHIPS * b + n][...] + b_ref[:, gcol])
                up = _dot(ys[b], ups[b][n])
                dzg = (dmn * up) * (gate * (1.0 - gate))
                dzg_ref[:, gcol] = dzg.astype(BF)
                dbg_ref[:, gcol] += jnp.sum(dzg, axis=0, keepdims=True)
                dup = (dmn * gate).astype(BF)
                accs[b][n] += _dot(ys[b], dup, "tn")
                dyb = _dot(dup, ups[b][n], "nt")
                dys[b] = dyb if dys[b] is None else dys[b] + dyb
        dyp_ref[...] = dys[0]
        dyc_ref[...] = dys[1]
        dya = dys[2]
        lane = lax.broadcasted_iota(jnp.int32, dya.shape, 1) // HEAD_DIM
        pr = dya * y
        rho = jnp.zeros_like(pr)
        for h in range(256 // HEAD_DIM):
            hm = lane == h
            rho = jnp.where(hm, jnp.sum(jnp.where(hm, pr, 0.0), axis=-1, keepdims=True), rho)
        for g in range(3):
            do_refs[g][...] = ws[g] * dya
            c_refs[g][...] = -(ws[g] * rho)

        @pl.when(i == nsteps - 1)
        def _():
            dgpu_ref[...] = accs[0][...].astype(BF)
            dgco_ref[...] = accs[1][...].astype(BF)
            dgau_ref[...] = accs[2][...].astype(BF)

    full = lambda a: pl.BlockSpec(a.shape, lambda i: (0,) * a.ndim)
    out_specs = ([row(3072), row(512), row(512)] + [row(256)] * 6 + [full(gpu), full(gco), full(gau)]
                 + [pl.BlockSpec((1, 3072), lambda i: (0, 0))])
    out_shape = ([jax.ShapeDtypeStruct((T, 3072), BF)] + [jax.ShapeDtypeStruct((T, 512), F32)] * 2
                 + [jax.ShapeDtypeStruct((T, 256), F32)] * 6
                 + [jax.ShapeDtypeStruct(g.shape, BF) for g in (gpu, gco, gau)]
                 + [jax.ShapeDtypeStruct((1, 3072), F32)])
    return pl.pallas_call(
        body, name=name, grid=(nsteps,), in_specs=[row(1024)] + specs, out_specs=out_specs, out_shape=out_shape,
        scratch_shapes=[pltpu.VMEM(g.shape, F32) for g in (gpu, gco, gau)],
        compiler_params=_params(("arbitrary",)),
    )(dm, yp, yc, *o3, *lse3, *([z] * 12), bgate, gpu, gco, gau)


def _fold(a, d):
    return a.reshape(a.shape[0] // d, d * a.shape[1])


def _unfold(a, d):
    return a.reshape(a.shape[0] * d, a.shape[1] // d)


def _layer_fwd(x, w, tag, after=None, late=None):
    hb = _rms_fwd(x, w["norm_mix"], f"rms_mix_{tag}", after=after)
    z = _mm(hb, w["w_in"], "nn", f"in_proj_{tag}", tm=512, tn=3712, tk=1024, n_outer=True)
    yp, yc = _poolconv_fwd(z, w["pool_mix"], w["pool_scale"], w["conv_w"], f"poolconv_{tag}")
    folded, o3, lse3 = [], [], []
    for g, d in enumerate(ATTN_DILATIONS):
        qf, kf, vf = (_fold(z[:, off + 256 * g:off + 256 * (g + 1)], d) for off in (OFF_Q, OFF_K, OFF_V))
        o, lse = _attn_fwd(qf, kf, vf, w["q_gain"], w["k_gain"], d, f"attn{g}_{tag}")
        folded.append((qf, kf, vf))
        o3.append(_unfold(o, d))
        lse3.append(_unfold(lse, d))
    if late is not None:
        w = dict(w, **late(lse3[-1]))
    merged = _merge_fwd(yp, yc, o3, lse3, z, w["b_gate"], w["w_pool_up"], w["w_conv_out"], w["w_attn_up"],
                        f"merge_{tag}")
    x1 = _mm(merged, w["w_o"], "nn", f"out_proj_{tag}", tm=1024, tn=1024, tk=1024, res=x)
    h2b = _rms_fwd(x1, w["norm_mlp"], f"rms_mlp_{tag}")
    rb = _mm(h2b, w["w_ff1"], "nn", f"ff1_{tag}", tm=1024, tn=1024, tk=1024, out_dtype=BF, epi="relu2", n_outer=True,
             b_shards=True)
    x2 = _mm(rb, w["w_ff2"], "nn", f"ff2_{tag}", tm=1024, tn=1024, tk=1024, res=x1)
    saved = dict(x=x, hb=hb, z=z, yp=yp, yc=yc, folded=folded, o3=o3, lse3=lse3, merged=merged, x1=x1, h2b=h2b, rb=rb)
    return x2, saved, w


def _layer_bwd(dx2, w, s, tag, after=None, mid=None):
    g = {}
    dab = _mm(dx2, w["w_ff2"], "nt", f"d_ff2_act_{tag}", tm=1024, tn=1024, tk=1024, out_dtype=BF, aux=s["rb"],
              epi="drelu2", after=after)
    g["w_ff2"] = _mm(s["rb"], dx2, "tn", f"d_ff2_w_{tag}", tm=1024, tn=1024, tk=1024, out_dtype=BF)
    g["w_ff1"] = _mm(s["h2b"], dab, "tn", f"d_ff1_w_{tag}", tm=1024, tn=1024, tk=1024, out_dtype=BF, out_shards=True)
    dh2 = _mm(dab, w["w_ff1"], "nt", f"d_ff1_act_{tag}", tm=1024, tn=1024, tk=1024, b_shards=True)
    dx1, g["norm_mlp"] = _rms_bwd(dh2, s["x1"], w["norm_mlp"], dx2, f"d_rms_mlp_{tag}")
    dm = _mm(dx1, w["w_o"], "nt", f"d_out_act_{tag}", tm=1024, tn=1024, tk=1024)
    g["w_o"] = _mm(s["merged"], dx1, "tn", f"d_out_w_{tag}", tm=1024, tn=1024, tk=1024, out_dtype=BF)
    (dzg, dyp, dyc, do0, do1, do2, c0, c1, c2, g["w_pool_up"], g["w_conv_out"], g["w_attn_up"],
     g["b_gate"]) = _merge_bwd(dm, s["yp"], s["yc"], s["o3"], s["lse3"], s["z"], w["b_gate"], w["w_pool_up"],
                               w["w_conv_out"], w["w_attn_up"], f"d_merge_{tag}")
    behind = mid(g) if mid is not None else None
    dq, dk, dv = [], [], []
    dgq = dgk = None
    for gi, d in enumerate(ATTN_DILATIONS):
        qf, kf, vf = s["folded"][gi]
        dzq, dzk, dzv, pq, pk = _attn_bwd(qf, kf, vf, _fold((do0, do1, do2)[gi], d), _fold((c0, c1, c2)[gi], d),
                                          _fold(s["lse3"][gi], d), w["q_gain"], w["k_gain"], d, f"d_attn{gi}_{tag}",
                                          after=behind)
        dq.append(_unfold(dzq, d))
        dk.append(_unfold(dzk, d))
        dv.append(_unfold(dzv, d))
        dgq = pq if dgq is None else dgq + pq
        dgk = pk if dgk is None else dgk + pk
    g["q_gain"] = dgq[:, :HEAD_DIM] + dgq[:, HEAD_DIM:]
    g["k_gain"] = dgk[:, :HEAD_DIM] + dgk[:, HEAD_DIM:]
    dzpc, g["pool_mix"], g["pool_scale"], g["conv_w"] = _poolconv_bwd(
        s["z"], dyp, dyc, w["pool_mix"], w["pool_scale"], w["conv_w"], f"d_poolconv_{tag}")
    dz = jnp.concatenate([dzpc] + dq + dk + dv + [dzg], axis=1)
    g["w_in"] = _mm(s["hb"], dz, "tn", f"d_in_w_{tag}", tm=512, tn=3712, tk=512, out_dtype=BF)
    dh = _mm(dz, w["w_in"], "nt", f"d_in_act_{tag}", tm=512, tn=1024, tk=3712)
    dx, g["norm_mix"] = _rms_bwd(dh, s["x"], w["norm_mix"], dx1, f"d_rms_mix_{tag}")
    return dx, g


def _position():
    x, y, c = lax.axis_index("x"), lax.axis_index("y"), lax.axis_index("c")
    chips = [(1 - x, y), (x, 1 - y), (1 - x, 1 - y)]
    return x, y, c, 2 * x + y, chips, [2 * cx + cy for cx, cy in chips]


def _remote(src, dst, ssem, rsem, dev):
    return pltpu.make_async_remote_copy(src_ref=src, dst_ref=dst, send_sem=ssem, recv_sem=rsem, device_id=dev,
                                        device_id_type=MESH_ID)


def _position_operand():
    x, y, c = lax.axis_index("x"), lax.axis_index("y"), lax.axis_index("c")
    return jnp.stack([2 * x + y, c]).astype(jnp.int32)


def _halves(a):
    return a.reshape(a.shape[0], 2, a.shape[1] // 2, a.shape[2])


def _gather(bufs, name):
    n = len(bufs)
    views = [_halves(b) for b in bufs]

    def body(*refs):
        outs = refs[n:2 * n]
        ssem, rsem, fssem, frsem = refs[2 * n:]
        x, y, c, q, chips, qs = _position()
        sib = (x, y, 1 - c)
        sent = []
        for k in range(n):
            mine = outs[k].at[q, c]
            for j, chip in enumerate(chips):
                cp = _remote(mine, mine, ssem.at[k, j], rsem.at[k, j], (chip[0], chip[1], c))
                cp.start()
                sent.append(cp)
        for k in range(n):
            for j, chip in enumerate(chips):
                slot = outs[k].at[qs[j], c]
                _remote(slot, slot, ssem.at[k, j], rsem.at[k, j], (chip[0], chip[1], c)).wait_recv()
                cp = _remote(slot, slot, fssem.at[k, j], frsem.at[k, j], sib)
                cp.start()
                sent.append(cp)
        for k in range(n):
            for j in range(3):
                slot = outs[k].at[qs[j], 1 - c]
                _remote(slot, slot, fssem.at[k, j], frsem.at[k, j], sib).wait_recv()
        for cp in sent:
            cp.wait_send()

    outs = pl.pallas_call(
        body, name=name, in_specs=[ANY] * n, out_specs=[ANY] * n,
        out_shape=[jax.ShapeDtypeStruct(v.shape, v.dtype) for v in views],
        input_output_aliases={k: k for k in range(n)},
        scratch_shapes=[pltpu.SemaphoreType.DMA((n, 3))] * 4,
    )(*views)
    return [o.reshape(b.shape) for o, b in zip(outs, bufs)]


SEM = pl.BlockSpec(memory_space=pltpu.SEMAPHORE)
TOKEN = jax.ShapeDtypeStruct((8, LANES), F32)
TOKEN_SPEC = pl.BlockSpec(memory_space=pltpu.VMEM)


def _split_params():
    return pltpu.CompilerParams(has_side_effects=pltpu.SideEffectType.DATAFLOW_SIDE_EFFECTING)


def _gather_start(bufs, name):
    n = len(bufs)
    views = [_halves(b) for b in bufs]

    def body(*refs):
        ssem, rsem = refs[n:n + ns], refs[n + ns:n + 2 * ns]
        outs, token = refs[n + 2 * ns:2 * n + 2 * ns], refs[2 * n + 2 * ns]
        x, y, c, q, chips, qs = _position()
        for k in range(n):
            mine = outs[k].at[q, c]
            for j, chip in enumerate(chips):
                _remote(mine, mine, ssem[3 * k + j], rsem[3 * k + j], (chip[0], chip[1], c)).start()
        token[...] = jnp.zeros_like(token)

    ns = 3 * n
    outs = pl.pallas_call(
        body, name=name, in_specs=[ANY] * n, out_specs=[SEM] * (2 * ns) + [ANY] * n + [TOKEN_SPEC],
        out_shape=[pltpu.SemaphoreType.DMA(())] * (2 * ns) + [jax.ShapeDtypeStruct(v.shape, v.dtype) for v in views]
        + [TOKEN],
        input_output_aliases={k: k + 2 * ns for k in range(n)}, compiler_params=_split_params(),
    )(*views)
    return list(outs[:ns]), list(outs[ns:2 * ns]), list(outs[2 * ns:2 * ns + n]), outs[2 * ns + n]


def _gather_finish(ssem, rsem, views, after, name_wait, name_forward, shapes):
    n = len(views)
    ns = len(ssem)

    def wait_body(*refs):
        ssem_ref, rsem_ref = refs[n:n + ns], refs[n + ns:n + 2 * ns]
        outs = refs[n + 2 * ns + 1:]
        x, y, c, q, chips, qs = _position()
        for k in range(n):
            for j, chip in enumerate(chips):
                cp = _remote(outs[k].at[q, c], outs[k].at[qs[j], c], ssem_ref[3 * k + j], rsem_ref[3 * k + j],
                             (chip[0], chip[1], c))
                cp.wait_send()
                cp.wait_recv()

    landed = pl.pallas_call(
        wait_body, name=name_wait, in_specs=[ANY] * n + [SEM] * (2 * ns) + [ANY], out_specs=[ANY] * n,
        out_shape=[jax.ShapeDtypeStruct(v.shape, v.dtype) for v in views],
        input_output_aliases={k: k for k in range(n)}, compiler_params=_split_params(),
    )(*views, *ssem, *rsem, after)

    def forward_body(*refs):
        outs = refs[n:2 * n]
        fssem, frsem = refs[2 * n:]
        x, y, c, q, chips, qs = _position()
        sib = (x, y, 1 - c)
        sent = []
        for k in range(n):
            for j in range(3):
                slot = outs[k].at[qs[j], c]
                cp = _remote(slot, slot, fssem.at[k, j], frsem.at[k, j], sib)
                cp.start()
                sent.append(cp)
        for k in range(n):
            for j in range(3):
                slot = outs[k].at[qs[j], 1 - c]
                _remote(slot, slot, fssem.at[k, j], frsem.at[k, j], sib).wait_recv()
        for cp in sent:
            cp.wait_send()

    outs = pl.pallas_call(
        forward_body, name=name_forward, in_specs=[ANY] * n, out_specs=[ANY] * n,
        out_shape=[jax.ShapeDtypeStruct(v.shape, v.dtype) for v in views],
        input_output_aliases={k: k for k in range(n)}, scratch_shapes=[pltpu.SemaphoreType.DMA((n, 3))] * 2,
    )(*landed)
    return [o.reshape(s) for o, s in zip(outs, shapes)]


def _chip_exchange_start(parts, name):
    n = len(parts)

    def body(*refs):
        ssem, rsem = refs[n:n + ns], refs[n + ns:n + 2 * ns]
        base = n + 2 * ns
        srcs, outs, token = refs[base:base + n], refs[base + n:base + 2 * n], refs[base + 2 * n]
        x, y, c, q, chips, qs = _position()
        for k in range(n):
            for j, chip in enumerate(chips):
                _remote(srcs[k].at[qs[j]], outs[k].at[j], ssem[3 * k + j], rsem[3 * k + j],
                        (chip[0], chip[1], c)).start()
        token[...] = jnp.zeros_like(token)

    ns = 3 * n
    outs = pl.pallas_call(
        body, name=name, in_specs=[ANY] * n, out_specs=[SEM] * (2 * ns) + [ANY] * (2 * n) + [TOKEN_SPEC],
        out_shape=[pltpu.SemaphoreType.DMA(())] * (2 * ns) + [jax.ShapeDtypeStruct(a.shape, a.dtype) for a in parts]
        + [jax.ShapeDtypeStruct((3,) + a.shape[1:], a.dtype) for a in parts] + [TOKEN],
        input_output_aliases={k: k + 2 * ns for k in range(n)}, compiler_params=_split_params(),
    )(*parts)
    b = 2 * ns
    return list(outs[:ns]), list(outs[ns:b]), list(outs[b:b + n]), list(outs[b + n:b + 2 * n]), outs[b + 2 * n]


def _chip_exchange_wait(ssem, rsem, parts, landing, after, name):
    n = len(parts)
    ns = len(ssem)

    def body(*refs):
        ssem_ref, rsem_ref = refs[2 * n:2 * n + ns], refs[2 * n + ns:2 * n + 2 * ns]
        base = 2 * n + 2 * ns + 1
        srcs, outs = refs[base:base + n], refs[base + n:]
        x, y, c, q, chips, qs = _position()
        for k in range(n):
            for j, chip in enumerate(chips):
                cp = _remote(srcs[k].at[qs[j]], outs[k].at[j], ssem_ref[3 * k + j], rsem_ref[3 * k + j],
                             (chip[0], chip[1], c))
                cp.wait_send()
                cp.wait_recv()

    outs = pl.pallas_call(
        body, name=name, in_specs=[ANY] * (2 * n) + [SEM] * (2 * ns) + [ANY], out_specs=[ANY] * (2 * n),
        out_shape=[jax.ShapeDtypeStruct(a.shape, a.dtype) for a in list(parts) + list(landing)],
        input_output_aliases={k: k for k in range(2 * n)}, compiler_params=_split_params(),
    )(*parts, *landing, *ssem, *rsem, after)
    return list(outs[:n]), list(outs[n:])


def _pair_swap(views, name):
    n = len(views)

    def body(*refs):
        ins, outs = refs[:n], refs[n:2 * n]
        ssem, rsem = refs[2 * n:]
        x, y, c, _, _, _ = _position()
        cps = [_remote(ins[k].at[pl.ds(0, N_CHIPS), 1 - c], outs[k], ssem.at[k], rsem.at[k], (x, y, 1 - c))
               for k in range(n)]
        for cp in cps:
            cp.start()
        for cp in cps:
            cp.wait()

    return pl.pallas_call(
        body, name=name, in_specs=[ANY] * n, out_specs=[ANY] * n,
        out_shape=[jax.ShapeDtypeStruct((v.shape[0],) + v.shape[2:], v.dtype) for v in views],
        scratch_shapes=[pltpu.SemaphoreType.DMA((n,))] * 2,
    )(*views)


def _chip_exchange(parts, name):
    n = len(parts)

    def body(*refs):
        ins, outs = refs[:n], refs[n:2 * n]
        ssem, rsem = refs[2 * n:]
        x, y, c, q, chips, qs = _position()
        cps = []
        for k in range(n):
            for j, chip in enumerate(chips):
                cp = _remote(ins[k].at[qs[j]], outs[k].at[j], ssem.at[k, j], rsem.at[k, j], (chip[0], chip[1], c))
                cp.start()
                cps.append(cp)
        for cp in cps:
            cp.wait_recv()
        for cp in cps:
            cp.wait_send()

    return pl.pallas_call(
        body, name=name, in_specs=[ANY] * n, out_specs=[ANY] * n,
        out_shape=[jax.ShapeDtypeStruct((3,) + a.shape[1:], a.dtype) for a in parts],
        scratch_shapes=[pltpu.SemaphoreType.DMA((n, 3))] * 2,
    )(*parts)


def _pair_send(arrays, name):
    n = len(arrays)

    def body(*refs):
        ins, outs = refs[:n], refs[n:2 * n]
        ssem, rsem = refs[2 * n:]
        x, y, c, _, _, _ = _position()
        cps = [_remote(ins[k], outs[k], ssem.at[k], rsem.at[k], (x, y, 1 - c)) for k in range(n)]
        for cp in cps:
            cp.start()
        for cp in cps:
            cp.wait()

    return pl.pallas_call(
        body, name=name, in_specs=[ANY] * n, out_specs=[ANY] * n,
        out_shape=[jax.ShapeDtypeStruct(a.shape, a.dtype) for a in arrays],
        scratch_shapes=[pltpu.SemaphoreType.DMA((n,))] * 2,
    )(*arrays)


def _all_to_all_small(part):
    P = part.shape[0]

    def body(in_ref, out_ref, lsem, ssem, rsem):
        x, y, c = lax.axis_index("x"), lax.axis_index("y"), lax.axis_index("c")
        me = 4 * x + 2 * y + c
        flips = [(fx, fy, fc) for fx in (0, 1) for fy in (0, 1) for fc in (0, 1)][1:]
        peers = [((x + fx) % 2, (y + fy) % 2, (c + fc) % 2) for fx, fy, fc in flips]
        loc = pltpu.make_async_copy(in_ref, out_ref.at[me], lsem)
        loc.start()
        cps = [_remote(in_ref, out_ref.at[me], ssem.at[j], rsem.at[j], peer) for j, peer in enumerate(peers)]
        for cp in cps:
            cp.start()
        for j, (px, py, pc) in enumerate(peers):
            _remote(in_ref, out_ref.at[4 * px + 2 * py + pc], ssem.at[j], rsem.at[j], peers[j]).wait_recv()
        for cp in cps:
            cp.wait_send()
        loc.wait()

    return pl.pallas_call(
        body, name="small_grad_exchange", in_specs=[ANY], out_specs=ANY,
        out_shape=jax.ShapeDtypeStruct((8, P, LANES), F32),
        scratch_shapes=[pltpu.SemaphoreType.DMA(())] + [pltpu.SemaphoreType.DMA((7,))] * 2,
    )(part)


def _row_tile(rows, width, n_arrays):
    t = rows
    while t % 2 == 0 and t > 8 and 2 * n_arrays * t * width * 4 > VMEM_LIMIT // 2:
        t //= 2
    return t


def _scalar_grid(grid, in_specs, out_specs):
    return pltpu.PrefetchScalarGridSpec(num_scalar_prefetch=1, grid=grid, in_specs=in_specs, out_specs=out_specs)


def _cast_place(w3, layer, pos, name):
    _, r, c = w3.shape
    tr = _row_tile(r, c, 2)

    def body(pos_ref, w_ref, o_ref):
        o_ref[...] = w_ref[...].astype(BF)

    return pl.pallas_call(
        body, name=name,
        grid_spec=_scalar_grid((r // tr,), [pl.BlockSpec((None, tr, c), lambda i, pos: (layer, i, 0))],
                               pl.BlockSpec((None, tr, c), lambda i, pos: (pos[0], i, 0))),
        out_shape=jax.ShapeDtypeStruct((N_CHIPS, r, c), BF), compiler_params=_params(("parallel",)),
    )(pos, w3)


def _pair_sum(view, recv, pos, name):
    _, _, hr, c = view.shape
    tr = _row_tile(hr, c, 3)

    def body(pos_ref, g_ref, r_ref, o_ref):
        o_ref[...] = (g_ref[...].astype(F32) + r_ref[...].astype(F32)).astype(BF)

    blk = pl.BlockSpec((None, tr, c), lambda p, i, pos: (p, i, 0))
    return pl.pallas_call(
        body, name=name,
        grid_spec=_scalar_grid((N_CHIPS, hr // tr),
                               [pl.BlockSpec((None, None, tr, c), lambda p, i, pos: (p, pos[1], i, 0)), blk], blk),
        out_shape=jax.ShapeDtypeStruct(recv.shape, BF), compiler_params=_params(("parallel", "parallel")),
    )(pos, view, recv)


def _chip_sum(parts, recv, pos, name):
    _, hr, c = parts.shape
    tr = _row_tile(hr, c, 6)

    def body(pos_ref, p_ref, r_ref, o_ref):
        acc = p_ref[...].astype(F32)
        for j in range(3):
            acc = acc + r_ref[j].astype(F32)
        o_ref[...] = acc

    return pl.pallas_call(
        body, name=name,
        grid_spec=_scalar_grid((hr // tr,),
                               [pl.BlockSpec((None, tr, c), lambda i, pos: (pos[0], i, 0)),
                                pl.BlockSpec((3, tr, c), lambda i, pos: (0, i, 0))],
                               pl.BlockSpec((tr, c), lambda i, pos: (i, 0))),
        out_shape=jax.ShapeDtypeStruct((hr, c), F32), compiler_params=_params(("parallel",)),
    )(pos, parts, recv)


def _sum_slices(a, name):
    n, rows, width = a.shape
    tr = _row_tile(rows, width, n + 1)

    def body(a_ref, o_ref):
        acc = a_ref[0].astype(F32)
        for i in range(1, n):
            acc = acc + a_ref[i].astype(F32)
        o_ref[...] = acc

    return pl.pallas_call(
        body, name=name, grid=(rows // tr,), in_specs=[pl.BlockSpec((n, tr, width), lambda i: (0, i, 0))],
        out_specs=pl.BlockSpec((tr, width), lambda i: (i, 0)), out_shape=jax.ShapeDtypeStruct((rows, width), F32),
        compiler_params=_params(("parallel",)),
    )(a)


def _adamw_update(w, g, m, v):
    nm = ADAM_B1 * m + (1.0 - ADAM_B1) * g
    nv = ADAM_B2 * v + (1.0 - ADAM_B2) * (g * g)
    m_hat = nm / (1.0 - ADAM_B1 ** ADAM_STEP)
    v_hat = nv / (1.0 - ADAM_B2 ** ADAM_STEP)
    return -ADAM_LR * (m_hat / (jnp.sqrt(v_hat) + ADAM_EPS) + ADAM_WD * w), nm, nv


def _adamw(w, g, m, v, name):
    rows, width = w.shape
    tr = _row_tile(rows, width, 7)

    def body(w_ref, g_ref, m_ref, v_ref, d_ref, nm_ref, nv_ref):
        d_ref[...], nm_ref[...], nv_ref[...] = _adamw_update(w_ref[...], g_ref[...], m_ref[...], v_ref[...])

    blk = pl.BlockSpec((tr, width), lambda i: (i, 0))
    return pl.pallas_call(
        body, name=name, grid=(rows // tr,), in_specs=[blk] * 4, out_specs=[blk] * 3,
        out_shape=[jax.ShapeDtypeStruct((rows, width), F32)] * 3, compiler_params=_params(("parallel",)),
    )(w, g, m, v)


def _adamw_halves(w3, m3, v3, mine, other, pos, name):
    depth, r, c = w3.shape
    assert depth == 2
    hr = r // 2
    tr = _row_tile(hr, c, 11)
    sources = ((0, True, mine[0]), (0, False, other[0]), (1, True, mine[1]), (1, False, other[1]))

    def active(l, h, core, layer, own):
        mine_half = h == core
        return (l == layer) & (mine_half if own else jnp.logical_not(mine_half))

    def body(pos_ref, w_ref, m_ref, v_ref, *rest):
        g_refs, (go_ref, d_ref, nm_ref, nv_ref) = rest[:4], rest[4:]
        l, h = pl.program_id(0), pl.program_id(1)
        for (layer, own, _), g_ref in zip(sources, g_refs):
            @pl.when(active(l, h, pos_ref[1], layer, own))
            def _():
                gv = g_ref[...]
                go_ref[...] = gv
                d_ref[...], nm_ref[...], nv_ref[...] = _adamw_update(w_ref[...], gv, m_ref[...], v_ref[...])

    def gspec(layer, own):
        return pl.BlockSpec((tr, c), lambda l, h, i, pos: (jnp.where(active(l, h, pos[1], layer, own), i, 0), 0))

    blk = pl.BlockSpec((None, None, tr, c), lambda l, h, i, pos: (l, h, i, 0))
    view = lambda a: a.reshape(depth, 2, hr, c)
    outs = pl.pallas_call(
        body, name=name,
        grid_spec=_scalar_grid((depth, 2, hr // tr), [blk] * 3 + [gspec(layer, own) for layer, own, _ in sources],
                               [blk] * 4),
        out_shape=[jax.ShapeDtypeStruct((depth, 2, hr, c), F32)] * 4,
        compiler_params=_params(("parallel", "parallel", "parallel")),
    )(pos, view(w3), view(m3), view(v3), *[s[2] for s in sources])
    return [o.reshape(w3.shape) for o in outs]


BIG = ("w_in", "w_pool_up", "w_conv_out", "w_attn_up", "w_o", "w_ff1", "w_ff2")
SMALL = ("norm_mix", "b_gate", "pool_mix", "pool_scale", "conv_w", "q_gain", "k_gain", "norm_mlp")
ORDER = ("norm_mix", "w_in", "b_gate", "pool_mix", "pool_scale", "conv_w", "q_gain", "k_gain", "w_pool_up",
         "w_conv_out", "w_attn_up", "w_o", "norm_mlp", "w_ff1", "w_ff2")
COLUMN_SHARDED = ("w_in", "w_pool_up", "w_conv_out", "w_attn_up", "w_ff1")


def _matrix_weights(gathered):
    w = {}
    for name, g4 in gathered.items():
        if name == "w_in":
            w[name] = jnp.transpose(g4, (1, 0, 2)).reshape(g4.shape[1], N_CHIPS * g4.shape[2])
        elif name in COLUMN_SHARDED:
            w[name] = g4
        else:
            w[name] = g4.reshape(N_CHIPS * g4.shape[1], g4.shape[2])
    return w


def _small_weights(l, small):
    w = {}
    w["norm_mix"] = small["norm_mix"][l][None]
    w["norm_mlp"] = small["norm_mlp"][l][None]
    w["b_gate"] = small["b_gate"][l][None]
    w["pool_mix"] = small["pool_mix"][l].astype(BF)
    w["pool_scale"] = small["pool_scale"][l][None]
    w["conv_w"] = jnp.pad(small["conv_w_full"][l], ((0, 5), (0, 0)))
    w["q_gain"] = jnp.tile(small["q_gain"][l], 2)[None]
    w["k_gain"] = jnp.tile(small["k_gain"][l], 2)[None]
    return w


def _to_chip_major(name, g):
    if name == "w_in":
        return jnp.transpose(g.reshape(g.shape[0], N_CHIPS, g.shape[1] // N_CHIPS), (1, 0, 2))
    if name in COLUMN_SHARDED:
        return g
    return g.reshape(N_CHIPS, g.shape[0] // N_CHIPS, g.shape[1])


def _pad8(a):
    a = a.reshape(-1, LANES)
    return jnp.pad(a, ((0, (-a.shape[0]) % 8), (0, 0)))


def kernel(x, norm_mix, w_in, b_gate, pool_mix, pool_scale, conv_w, q_gain, k_gain, w_pool_up, w_conv_out, w_attn_up, w_o, norm_mlp, w_ff1, w_ff2, loss_target, m_norm_mix, m_w_in, m_b_gate, m_pool_mix, m_pool_scale, m_conv_w, m_q_gain, m_k_gain, m_w_pool_up, m_w_conv_out, m_w_attn_up, m_w_o, m_norm_mlp, m_w_ff1, m_w_ff2, v_norm_mix, v_w_in, v_b_gate, v_pool_mix, v_pool_scale, v_conv_w, v_q_gain, v_k_gain, v_w_pool_up, v_w_conv_out, v_w_attn_up, v_w_o, v_norm_mlp, v_w_ff1, v_w_ff2):
    weights = dict(norm_mix=norm_mix, w_in=w_in, b_gate=b_gate, pool_mix=pool_mix, pool_scale=pool_scale, conv_w=conv_w,
                   q_gain=q_gain, k_gain=k_gain, w_pool_up=w_pool_up, w_conv_out=w_conv_out, w_attn_up=w_attn_up,
                   w_o=w_o, norm_mlp=norm_mlp, w_ff1=w_ff1, w_ff2=w_ff2)
    moms = dict(norm_mix=m_norm_mix, w_in=m_w_in, b_gate=m_b_gate, pool_mix=m_pool_mix, pool_scale=m_pool_scale,
                conv_w=m_conv_w, q_gain=m_q_gain, k_gain=m_k_gain, w_pool_up=m_w_pool_up, w_conv_out=m_w_conv_out,
                w_attn_up=m_w_attn_up, w_o=m_w_o, norm_mlp=m_norm_mlp, w_ff1=m_w_ff1, w_ff2=m_w_ff2)
    vels = dict(norm_mix=v_norm_mix, w_in=v_w_in, b_gate=v_b_gate, pool_mix=v_pool_mix, pool_scale=v_pool_scale,
                conv_w=v_conv_w, q_gain=v_q_gain, k_gain=v_k_gain, w_pool_up=v_w_pool_up, w_conv_out=v_w_conv_out,
                w_attn_up=v_w_attn_up, w_o=v_w_o, norm_mlp=v_norm_mlp, w_ff1=v_w_ff1, w_ff2=v_w_ff2)
    depth = norm_mix.shape[0]
    q = 2 * lax.axis_index("x") + lax.axis_index("y")
    pos = _position_operand()

    assert depth == 2, "the second layer's gather hides behind the first layer's forward, and likewise backward"
    first, rest = BIG[:1], BIG[1:]
    bufs = [{n: _cast_place(weights[n], l, pos, f"cast_{n}_l{l}") for n in BIG} for l in range(depth)]
    w_first = _matrix_weights(dict(zip(first, _gather([bufs[0][n] for n in first], "gather_l0_in"))))
    b_ssem, b_rsem, b_views, b_token = _gather_start([bufs[0][n] for n in rest], "gather_start_l0_rest")
    g_ssem, g_rsem, g_views, g_token = _gather_start([bufs[1][n] for n in BIG], "gather_start_l1")
    cw_all = _all_to_all_small(_pad8(jnp.pad(conv_w.reshape(-1), (0, (-conv_w.size) % LANES))))
    conv_w_full = jnp.concatenate(
        [cw_all[2 * p].reshape(-1)[:conv_w.size].reshape(conv_w.shape) for p in range(N_CHIPS)], axis=-1)
    small = dict(weights)
    small["conv_w_full"] = conv_w_full

    def late_weights(t):
        got = _gather_finish(b_ssem, b_rsem, b_views, t, "gather_wait_l0_rest", "gather_forward_l0_rest",
                             [bufs[0][n].shape for n in rest])
        return _matrix_weights(dict(zip(rest, got)))

    wl, saved = [None] * depth, [None] * depth
    h, saved[0], wl[0] = _layer_fwd(x[0], dict(_small_weights(0, small), **w_first), "l0", after=[b_token, g_token],
                                    late=late_weights)
    got = _gather_finish(g_ssem, g_rsem, g_views, h, "gather_wait_l1", "gather_forward_l1",
                         [bufs[1][n].shape for n in BIG])
    h, saved[1], wl[1] = _layer_fwd(h, dict(_small_weights(1, small), **_matrix_weights(dict(zip(BIG, got)))), "l1")
    dh, loss_row = _loss_grad(h, loss_target[0], "loss")

    def pair_stage(names, g, tag):
        views = [_halves(_to_chip_major(n, g[n])) for n in names]
        from_sibling = _pair_swap(views, f"grad_pair_swap_{tag}")
        return [_pair_sum(views[k], from_sibling[k], pos, f"pair_sum_{n}_{tag}") for k, n in enumerate(names)]

    mine, other = [{}, {}], [{}, {}]

    def finish(names, l, started, after, tag):
        ssem, rsem, parts, landing, _ = started
        parts, arrived = _chip_exchange_wait(ssem, rsem, parts, landing, after, f"grad_chip_exchange_wait_{tag}")
        got = [_chip_sum(parts[k], arrived[k], pos, f"chip_sum_{n}_{tag}") for k, n in enumerate(names)]
        mine[l].update(zip(names, got))
        other[l].update(zip(names, _pair_send(got, f"grad_pair_send_{tag}")))

    grads, early = [None] * depth, {}
    dh, grads[1] = _layer_bwd(dh, wl[1], saved[1], "l1")
    second = _chip_exchange_start(pair_stage(BIG, grads[1], "l1"), "grad_chip_exchange_start_l1")

    def start_rest(g):
        early["rest"] = _chip_exchange_start(pair_stage(rest, g, "l0_rest"), "grad_chip_exchange_start_l0_rest")
        return early["rest"][4]

    dh, grads[0] = _layer_bwd(dh, wl[0], saved[0], "l0", after=second[4], mid=start_rest)
    last = _chip_exchange_start(pair_stage(first, grads[0], "l0_in"), "grad_chip_exchange_start_l0_in")
    finish(BIG, 1, second, last[4], "l1")
    finish(rest, 0, early["rest"], last[4], "l0_rest")
    loss = lax.psum(loss_row[0, 0], ("x", "y", "c"))
    full = {}

    pieces = []
    for n in SMALL:
        per_layer = [grads[l][n] for l in range(depth)]
        if n == "conv_w":
            per_layer = [p[:3] for p in per_layer]
        pieces.append(_pad8(jnp.stack(per_layer).reshape(-1)))
    packed = jnp.concatenate(pieces, axis=0)
    summed = _sum_slices(_all_to_all_small(packed), "small_sum")
    row = 0
    for n, piece in zip(SMALL, pieces):
        size = weights[n].size if n != "conv_w" else depth * 3 * 512
        flat = summed[row:row + piece.shape[0]].reshape(-1)[:size]
        row += piece.shape[0]
        if n == "conv_w":
            full[n] = lax.dynamic_slice_in_dim(flat.reshape(depth, 3, 512), q * conv_w.shape[2], conv_w.shape[2], axis=2)
        else:
            full[n] = flat.reshape(weights[n].shape)

    deltas, new_m, new_v = {}, {}, {}

    def update_matrix(n):
        full[n], deltas[n], new_m[n], new_v[n] = _adamw_halves(
            weights[n], moms[n], vels[n], [mine[l][n] for l in range(depth)], [other[l][n] for l in range(depth)], pos,
            f"adamw_{n}")

    for n in rest:
        update_matrix(n)
    finish(first, 0, last, deltas[rest[-1]], "l0_in")
    for n in first:
        update_matrix(n)
    for n in SMALL:
        shape = weights[n].shape
        two_d = (-1, shape[-1]) if n not in ("conv_w", "q_gain", "k_gain") else (1, -1)
        d2, m2, v2 = _adamw(weights[n].reshape(two_d), full[n].reshape(two_d), moms[n].reshape(two_d),
                            vels[n].reshape(two_d), f"adamw_{n}")
        deltas[n], new_m[n], new_v[n] = d2.reshape(shape), m2.reshape(shape), v2.reshape(shape)
        full[n] = full[n].reshape(shape)
    return (loss, dh[None], *[full[n] for n in ORDER], *[deltas[n] for n in ORDER], *[new_m[n] for n in ORDER],
            *[new_v[n] for n in ORDER])
```

```python
import functools

import jax
import jax.numpy as jnp
from jax import lax
from jax.experimental import pallas as pl
from jax.experimental.pallas import tpu as pltpu

F32 = jnp.float32
BF = jnp.bfloat16
MESH_ID = pl.DeviceIdType.MESH
ANY = pl.BlockSpec(memory_space=pl.ANY)

EPS = 1e-6
MASK_VALUE = -1e30
POOL_WINDOWS = (2, 4, 8, 16)
ATTN_DILATIONS = (1, 4, 16)
ATTN_BLOCK = 128
HEAD_DIM = 64
OFF_Q, OFF_K, OFF_V, OFF_GATE = 2048, 2816, 3584, 4352
N_CHIPS = 4
ADAM_LR, ADAM_B1, ADAM_B2, ADAM_EPS, ADAM_WD, ADAM_STEP = 0.001, 0.9, 0.999, 1e-08, 0.01, 10

VMEM_LIMIT = 48 * 1024 * 1024
LANES = 128

_DIMS = {"nn": (((1,), (0,)), ((), ())), "nt": (((1,), (1,)), ((), ())), "tn": (((0,), (0,)), ((), ()))}


def _params(sem):
    return pltpu.CompilerParams(dimension_semantics=sem, vmem_limit_bytes=VMEM_LIMIT)


def _dot(a, b, mode="nn"):
    return lax.dot_general(a, b, _DIMS[mode], preferred_element_type=F32)


def _mm(a, b, mode, name, *, tm, tn, tk, out_dtype=F32, res=None, aux=None, epi=None, n_outer=False,
        b_shards=False, out_shards=False, after=None):
    if mode == "tn":
        K, M = a.shape
    else:
        M, K = a.shape
    if b_shards:
        if mode == "nn":
            assert b.shape[1] == K
            N = b.shape[2] * N_CHIPS
        else:
            assert mode == "nt"
            N = b.shape[1]
            assert b.shape[2] * N_CHIPS == K
    else:
        N = b.shape[0] if mode == "nt" else b.shape[1]
    tm, tn, tk = min(tm, M), min(tn, N), min(tk, K)
    assert M % tm == 0 and N % tn == 0 and K % tk == 0
    nk = K // tk
    if n_outer:
        grid = (N // tn, M // tm, nk)
        ij = lambda p, q_: (q_, p)
    else:
        grid = (M // tm, N // tn, nk)
        ij = lambda p, q_: (p, q_)

    def amap(p, q_, k):
        i, j = ij(p, q_)
        return (k, i) if mode == "tn" else (i, k)

    a_spec = pl.BlockSpec((tk, tm) if mode == "tn" else (tm, tk), amap)
    if b_shards:
        if mode == "nn":
            per = (N // N_CHIPS) // tn
            assert per >= 1 and (N // N_CHIPS) % tn == 0

            def bmap(p, q_, k):
                i, j = ij(p, q_)
                return (j // per, k, j % per)

            b_spec = pl.BlockSpec((None, tk, tn), bmap)
        else:
            per = (K // N_CHIPS) // tk
            assert per >= 1 and (K // N_CHIPS) % tk == 0

            def bmap(p, q_, k):
                i, j = ij(p, q_)
                return (k // per, j, k % per)

            b_spec = pl.BlockSpec((None, tn, tk), bmap)
    else:
        def bmap(p, q_, k):
            i, j = ij(p, q_)
            return (j, k) if mode == "nt" else (k, j)

        b_spec = pl.BlockSpec((tn, tk) if mode == "nt" else (tk, tn), bmap)

    def omap(p, q_, k):
        return ij(p, q_)

    o_spec = pl.BlockSpec((tm, tn), omap)
    if out_shards:
        per_o = (N // N_CHIPS) // tn
        assert per_o >= 1 and (N // N_CHIPS) % tn == 0

        def osmap(p, q_, k):
            i, j = ij(p, q_)
            return (j // per_o, i, j % per_o)

        out_spec0 = pl.BlockSpec((None, tm, tn), osmap)
        out_shape0 = jax.ShapeDtypeStruct((N_CHIPS, M, N // N_CHIPS), out_dtype)
    else:
        out_spec0 = o_spec
        out_shape0 = jax.ShapeDtypeStruct((M, N), out_dtype)

    in_specs = [a_spec, b_spec]
    args = [a, b]
    if res is not None:
        in_specs.append(o_spec)
        args.append(res)
    if aux is not None:
        in_specs.append(o_spec)
        args.append(aux)
    if after is not None:
        in_specs.append(ANY)
        args.append(after)
    out_specs = [out_spec0]
    out_shape = [out_shape0]
    n_out = len(out_shape)
    has_res, has_aux, has_after = res is not None, aux is not None, after is not None

    def body(*refs):
        a_ref, b_ref = refs[0], refs[1]
        pos = 2
        res_ref = aux_ref = None
        if has_res:
            res_ref = refs[pos]
            pos += 1
        if has_aux:
            aux_ref = refs[pos]
            pos += 1
        if has_after:
            pos += 1
        outs = refs[pos:pos + n_out]
        part = _dot(a_ref[...].astype(BF), b_ref[...].astype(BF), mode)

        def finish(acc):
            if res_ref is not None:
                acc = res_ref[...] + acc
            if epi == "relu2":
                r = jnp.maximum(acc, 0.0)
                outs[0][...] = (r * r).astype(out_dtype)
            elif epi == "drelu2":
                outs[0][...] = (acc * (2.0 * jnp.sqrt(aux_ref[...].astype(F32)))).astype(out_dtype)
            else:
                outs[0][...] = acc.astype(out_dtype)

        if nk == 1:
            finish(part)
        else:
            acc_ref = refs[pos + n_out]
            k = pl.program_id(2)

            @pl.when(k == 0)
            def _():
                acc_ref[...] = part

            @pl.when(k > 0)
            def _():
                acc_ref[...] += part

            @pl.when(k == nk - 1)
            def _():
                finish(acc_ref[...])

    scratch = [pltpu.VMEM((tm, tn), F32)] if nk > 1 else []
    out = pl.pallas_call(
        body, name=name, grid=grid, in_specs=in_specs, out_specs=out_specs, out_shape=out_shape,
        scratch_shapes=scratch, compiler_params=_params(("parallel", "parallel", "arbitrary")),
    )(*args)
    return out if n_out > 1 else out[0]


def _rms_fwd(x, gain, name, after=None):
    T, D = x.shape
    tm = min(512, T)

    def body(x_ref, g_ref, *rest):
        o_ref = rest[-1]
        xv = x_ref[...]
        r = lax.rsqrt(jnp.mean(xv * xv, axis=-1, keepdims=True) + EPS)
        o_ref[...] = ((xv * r) * g_ref[...]).astype(BF)

    extra = [] if after is None else list(after) if isinstance(after, (list, tuple)) else [after]
    return pl.pallas_call(
        body, name=name, grid=(T // tm,),
        in_specs=[pl.BlockSpec((tm, D), lambda i: (i, 0)), pl.BlockSpec((1, D), lambda i: (0, 0))] + [ANY] * len(extra),
        out_specs=pl.BlockSpec((tm, D), lambda i: (i, 0)), out_shape=jax.ShapeDtypeStruct((T, D), BF),
        compiler_params=_params(("parallel",)),
    )(x, gain, *extra)


def _rms_bwd(dh, x, gain, dres, name):
    T, D = x.shape
    tm = min(512, T)

    def body(dh_ref, x_ref, g_ref, dres_ref, dx_ref, dg_ref):
        xv = x_ref[...]
        r = lax.rsqrt(jnp.mean(xv * xv, axis=-1, keepdims=True) + EPS)
        xhat = xv * r
        dhv = dh_ref[...]
        dy = dhv * g_ref[...]
        dx_ref[...] = dres_ref[...] + r * (dy - xhat * jnp.mean(dy * xhat, axis=-1, keepdims=True))

        @pl.when(pl.program_id(0) == 0)
        def _():
            dg_ref[...] = jnp.zeros_like(dg_ref)

        dg_ref[...] += jnp.sum(dhv * xhat, axis=0, keepdims=True)

    row = pl.BlockSpec((tm, D), lambda i: (i, 0))
    vec = pl.BlockSpec((1, D), lambda i: (0, 0))
    return pl.pallas_call(
        body, name=name, grid=(T // tm,), in_specs=[row, row, vec, row], out_specs=[row, vec],
        out_shape=[jax.ShapeDtypeStruct((T, D), F32), jax.ShapeDtypeStruct((1, D), F32)],
        compiler_params=_params(("arbitrary",)),
    )(dh, x, gain, dres)


def _loss_grad(y, target, name):
    T, D = y.shape
    tm = min(512, T)

    def body(y_ref, t_ref, dy_ref, l_ref):
        e = y_ref[...] - t_ref[...]
        dy_ref[...] = e / float(D)

        @pl.when(pl.program_id(0) == 0)
        def _():
            l_ref[...] = jnp.zeros_like(l_ref)

        l_ref[...] += 0.5 * jnp.sum(jnp.mean(e * e, axis=-1, keepdims=True))

    row = pl.BlockSpec((tm, D), lambda i: (i, 0))
    return pl.pallas_call(
        body, name=name, grid=(T // tm,), in_specs=[row, row],
        out_specs=[row, pl.BlockSpec((1, LANES), lambda i: (0, 0))],
        out_shape=[jax.ShapeDtypeStruct((T, D), F32), jax.ShapeDtypeStruct((1, LANES), F32)],
        compiler_params=_params(("arbitrary",)),
    )(y, target)


POOL_HALO = 16
CONV_HALO = 8


def _causal_window_sum(v, w):
    s, sh = v, 1
    while sh < w:
        s = s + pltpu.roll(s, sh, 0)
        sh *= 2
    return s


def _anticausal_window_sum(v, w):
    n = v.shape[0]
    s, sh = v, 1
    while sh < w:
        s = s + pltpu.roll(s, n - sh, 0)
        sh *= 2
    return s


def _poolconv_fwd(z, pmix_b, pscale, convw, name):
    T = z.shape[0]
    R = min(512, T)
    PH, CH = R // POOL_HALO, R // CONV_HALO

    def body(u_ref, uh_ref, b_ref, c_ref, ch_ref, x_ref, xh_ref, mix_ref, sc_ref, cw_ref, yp_ref, yc_ref):
        i = pl.program_id(0)
        keep = (i > 0).astype(F32)
        row = i * R + lax.broadcasted_iota(jnp.int32, (R, 1), 0)
        w_all = jnp.concatenate([uh_ref[...] * keep, u_ref[...]], axis=0)
        for g, w in enumerate(POOL_WINDOWS):
            cols = slice(128 * g, 128 * (g + 1))
            wg = w_all[:, cols]
            s = _causal_window_sum(wg, w)[POOL_HALO:]
            cnt = jnp.minimum(row + 1, w).astype(F32)
            dgrp = s / cnt - wg[POOL_HALO:]
            y = _dot(dgrp.astype(BF), mix_ref[g]) * sc_ref[:, cols]
            yp_ref[:, cols] = y.astype(BF)
        uc = jnp.concatenate([ch_ref[...] * xh_ref[...] * keep, c_ref[...] * x_ref[...]], axis=0)
        yc = cw_ref[2:3, :] * uc + cw_ref[0:1, :] * pltpu.roll(uc, 2, 0) + cw_ref[1:2, :] * pltpu.roll(uc, 1, 0)
        yc_ref[...] = (b_ref[...] * yc[CONV_HALO:]).astype(BF)

    def main(cb):
        return pl.BlockSpec((R, 512), lambda i: (i, cb))

    def prev(cb, halo, per):
        return pl.BlockSpec((halo, 512), lambda i: (jnp.maximum(i * per - 1, 0), cb))

    full = lambda a: pl.BlockSpec(a.shape, lambda i: (0,) * a.ndim)
    return pl.pallas_call(
        body, name=name, grid=(T // R,),
        in_specs=[main(0), prev(0, POOL_HALO, PH), main(1), main(2), prev(2, CONV_HALO, CH), main(3),
                  prev(3, CONV_HALO, CH), full(pmix_b), full(pscale), full(convw)],
        out_specs=[pl.BlockSpec((R, 512), lambda i: (i, 0))] * 2,
        out_shape=[jax.ShapeDtypeStruct((T, 512), BF)] * 2,
        compiler_params=_params(("parallel",)),
    )(z, z, z, z, z, z, z, pmix_b, pscale, convw)


def _poolconv_bwd(z, dyp, dyc, pmix_b, pscale, convw, name):
    T = z.shape[0]
    R = min(512, T)
    PH, CH = R // POOL_HALO, R // CONV_HALO
    nsteps = T // R

    def body(u_ref, uh_ref, b_ref, bn_ref, c_ref, ch_ref, x_ref, xh_ref, dyp_ref, dypn_ref, dyc_ref, dycn_ref,
             mix_ref, sc_ref, cw_ref, dz_ref, dmix_ref, dsc_ref, dcw_ref):
        i = pl.program_id(0)
        keep_prev = (i > 0).astype(F32)
        keep_next = (i < nsteps - 1).astype(F32)

        @pl.when(i == 0)
        def _():
            dmix_ref[...] = jnp.zeros_like(dmix_ref)
            dsc_ref[...] = jnp.zeros_like(dsc_ref)
            dcw_ref[...] = jnp.zeros_like(dcw_ref)

        row = i * R + lax.broadcasted_iota(jnp.int32, (R, 1), 0)
        row_ext = i * R + lax.broadcasted_iota(jnp.int32, (R + POOL_HALO, 1), 0)
        w_all = jnp.concatenate([uh_ref[...] * keep_prev, u_ref[...]], axis=0)
        dyp_ext = jnp.concatenate([dyp_ref[...], dypn_ref[...] * keep_next], axis=0)
        for g, w in enumerate(POOL_WINDOWS):
            cols = slice(128 * g, 128 * (g + 1))
            wg = w_all[:, cols]
            s = _causal_window_sum(wg, w)[POOL_HALO:]
            cnt = jnp.minimum(row + 1, w).astype(F32)
            dgrp = (s / cnt - wg[POOL_HALO:]).astype(BF)
            y_pre = _dot(dgrp, mix_ref[g])
            dsc_ref[:, cols] += jnp.sum(dyp_ref[:, cols] * y_pre, axis=0, keepdims=True)
            dyb = (dyp_ext[:, cols] * sc_ref[:, cols]).astype(BF)
            dmix_ref[cols, :] += _dot(dgrp, dyb[:R], "tn")
            dd = _dot(dyb, mix_ref[g], "nt")
            cnt_ext = jnp.minimum(row_ext + 1, w).astype(F32)
            e = _anticausal_window_sum(dd / cnt_ext, w)
            dz_ref[:, cols] = (e[:R] - dd[:R]).astype(BF)
        cw0, cw1, cw2 = cw_ref[0:1, :], cw_ref[1:2, :], cw_ref[2:3, :]
        uc = jnp.concatenate([ch_ref[...] * xh_ref[...] * keep_prev, c_ref[...] * x_ref[...]], axis=0)
        uc1 = pltpu.roll(uc, 1, 0)[CONV_HALO:]
        uc2 = pltpu.roll(uc, 2, 0)[CONV_HALO:]
        uc0 = uc[CONV_HALO:]
        yc = cw2 * uc0 + cw0 * uc2 + cw1 * uc1
        dycv = dyc_ref[...]
        dz_ref[:, 512:1024] = (dycv * yc).astype(BF)
        dv_ext = jnp.concatenate([dycv * b_ref[...], dycn_ref[...] * bn_ref[...] * keep_next], axis=0)
        n_ext = R + CONV_HALO
        duc = (cw2 * dv_ext + cw1 * pltpu.roll(dv_ext, n_ext - 1, 0) + cw0 * pltpu.roll(dv_ext, n_ext - 2, 0))[:R]
        dv = dv_ext[:R]
        dcw_ref[0:1, :] += jnp.sum(dv * uc2, axis=0, keepdims=True)
        dcw_ref[1:2, :] += jnp.sum(dv * uc1, axis=0, keepdims=True)
        dcw_ref[2:3, :] += jnp.sum(dv * uc0, axis=0, keepdims=True)
        dz_ref[:, 1024:1536] = (duc * x_ref[...]).astype(BF)
        dz_ref[:, 1536:2048] = (duc * c_ref[...]).astype(BF)

    def main(cb):
        return pl.BlockSpec((R, 512), lambda i: (i, cb))

    def prev(cb, halo, per):
        return pl.BlockSpec((halo, 512), lambda i: (jnp.maximum(i * per - 1, 0), cb))

    def nxt(cb, halo, per):
        return pl.BlockSpec((halo, 512), lambda i: (jnp.minimum((i + 1) * per, T // halo - 1), cb))

    full = lambda a: pl.BlockSpec(a.shape, lambda i: (0,) * a.ndim)
    return pl.pallas_call(
        body, name=name, grid=(nsteps,),
        in_specs=[main(0), prev(0, POOL_HALO, PH), main(1), nxt(1, CONV_HALO, CH), main(2), prev(2, CONV_HALO, CH),
                  main(3), prev(3, CONV_HALO, CH), main(0), nxt(0, POOL_HALO, PH), main(0), nxt(0, CONV_HALO, CH),
                  full(pmix_b), full(pscale), full(convw)],
        out_specs=[pl.BlockSpec((R, 2048), lambda i: (i, 0)), pl.BlockSpec((512, 128), lambda i: (0, 0)),
                   pl.BlockSpec((1, 512), lambda i: (0, 0)), pl.BlockSpec((8, 512), lambda i: (0, 0))],
        out_shape=[jax.ShapeDtypeStruct((T, 2048), BF), jax.ShapeDtypeStruct((512, 128), F32),
                   jax.ShapeDtypeStruct((1, 512), F32), jax.ShapeDtypeStruct((8, 512), F32)],
        compiler_params=_params(("arbitrary",)),
    )(z, z, z, z, z, z, z, z, dyp, dyp, dyc, dyc, pmix_b, pscale, convw)


def _head_norm(x, g2, ma):
    sq = x * x
    sa = jnp.sum(jnp.where(ma, sq, 0.0), axis=-1, keepdims=True)
    sb = jnp.sum(jnp.where(ma, 0.0, sq), axis=-1, keepdims=True)
    r = jnp.where(ma, lax.rsqrt(sa / HEAD_DIM + EPS), lax.rsqrt(sb / HEAD_DIM + EPS))
    return x * r, r


def _head_norm_bwd(dy, xhat, r, g2, ma):
    dxh = dy * g2
    pr = dxh * xhat
    sa = jnp.sum(jnp.where(ma, pr, 0.0), axis=-1, keepdims=True)
    sb = jnp.sum(jnp.where(ma, 0.0, pr), axis=-1, keepdims=True)
    mh = jnp.where(ma, sa, sb) / HEAD_DIM
    return r * (dxh - xhat * mh)


def _head_col(tile, hm):
    return jnp.max(jnp.where(hm, tile, -jnp.inf), axis=-1, keepdims=True)


def _attn_masks(other_block_exists):
    lane = lax.broadcasted_iota(jnp.int32, (ATTN_BLOCK, ATTN_BLOCK), 1)
    qi = lax.broadcasted_iota(jnp.int32, (ATTN_BLOCK, ATTN_BLOCK), 0)
    never = (1 - other_block_exists.astype(jnp.int32)) * (2 * ATTN_BLOCK)
    return lane < HEAD_DIM, lane <= qi, lane >= qi + never


def _qk_norm(z, gains, name):
    T = z.shape[0]
    tm = min(512, T)
    per_kind = (OFF_K - OFF_Q) // 256

    def body(x_ref, g_ref, o_ref):
        ma = lax.broadcasted_iota(jnp.int32, (tm, LANES), 1) < HEAD_DIM
        is_q = jnp.full((1, LANES), pl.program_id(1)) < per_kind
        g = jnp.where(is_q, g_ref[0:1, :], g_ref[1:2, :])
        for t in range(2):
            sl = slice(LANES * t, LANES * (t + 1))
            o_ref[:, sl] = _head_norm(x_ref[:, sl], g, ma)[0] * g

    return pl.pallas_call(
        body, name=name, grid=(T // tm, 2 * per_kind),
        in_specs=[pl.BlockSpec((tm, 256), lambda i, n: (i, OFF_Q // 256 + n)), pl.BlockSpec((8, LANES), lambda i, n: (0, 0))],
        out_specs=pl.BlockSpec((tm, 256), lambda i, n: (i, n)),
        out_shape=jax.ShapeDtypeStruct((T, 2 * (OFF_K - OFF_Q)), F32), compiler_params=_params(("parallel", "parallel")),
    )(z, gains)


ATTN_STEP_ROWS = 1024


def _attn_geometry(T, d):
    sub = ATTN_BLOCK * d
    nb = T // sub
    m = max(1, min(nb, ATTN_STEP_ROWS // sub))
    assert T % sub == 0 and nb % m == 0
    return sub, nb, m


def _attn_rows(jj, r, sub, d):
    start = jj * sub + r
    if d == 1:
        return pl.ds(pl.multiple_of(start, ATTN_BLOCK), ATTN_BLOCK)
    return pl.ds(start, ATTN_BLOCK, stride=d)


def _pick(flag, a, b):
    return jnp.where(jnp.full(a.shape, flag.astype(jnp.int32)) > 0, a, b)


def _attn_fwd(z, qkn, g, d, name):
    T = z.shape[0]
    sub, nb, m = _attn_geometry(T, d)
    scale = HEAD_DIM ** -0.5

    def body(q_ref, kc_ref, kp_ref, vc_ref, vp_ref, o_ref, lse_ref):
        jb = pl.program_id(0)

        def step(s, carry):
            jj, r = s // d, s % d
            here, before = _attn_rows(jj, r, sub, d), _attn_rows(jnp.maximum(jj - 1, 0), r, sub, d)
            edge = _attn_rows(0, r, sub, d)
            first = jj == 0
            ma, mask_c, mask_p = _attn_masks(jb * m + jj > 0)
            qn = q_ref[here, :]
            kcb = kc_ref[here, :].astype(BF)
            kpb = _pick(first, kp_ref[edge, :], kc_ref[before, :]).astype(BF)
            vcb = vc_ref[here, :].astype(BF)
            vpb = _pick(first, vp_ref[edge, :], vc_ref[before, :]).astype(BF)
            o_t = lse_t = None
            for hm in (ma, jnp.logical_not(ma)):
                qh = jnp.where(hm, qn, 0.0).astype(BF)
                s_c = jnp.where(mask_c, _dot(qh, kcb, "nt") * scale, MASK_VALUE)
                s_p = jnp.where(mask_p, _dot(qh, kpb, "nt") * scale, MASK_VALUE)
                mx = jnp.maximum(jnp.max(s_c, axis=-1, keepdims=True), jnp.max(s_p, axis=-1, keepdims=True))
                p_c = jnp.exp(s_c - mx)
                p_p = jnp.exp(s_p - mx)
                den = jnp.sum(p_c, axis=-1, keepdims=True) + jnp.sum(p_p, axis=-1, keepdims=True)
                o = (_dot(p_c.astype(BF), vcb) + _dot(p_p.astype(BF), vpb)) / den
                lse = jnp.broadcast_to(mx + jnp.log(den), o.shape)
                o_t = o if o_t is None else jnp.where(ma, o_t, o)
                lse_t = lse if lse_t is None else jnp.where(ma, lse_t, lse)
            o_ref[here, :] = o_t
            lse_ref[here, :] = lse_t
            return carry

        lax.fori_loop(0, m * d, step, 0)

    def cur(col0):
        return pl.BlockSpec((m * sub, LANES), lambda j, t: (j, col0 + 2 * g + t))

    def prv(col0):
        return pl.BlockSpec((sub, LANES), lambda j, t: (jnp.maximum(j * m - 1, 0), col0 + 2 * g + t))

    k0, v0 = (OFF_K - OFF_Q) // LANES, OFF_V // LANES
    out = pl.BlockSpec((m * sub, LANES), lambda j, t: (j, t))
    return pl.pallas_call(
        body, name=name, grid=(nb // m, 2), in_specs=[cur(0), cur(k0), prv(k0), cur(v0), prv(v0)],
        out_specs=[out, out], out_shape=[jax.ShapeDtypeStruct((T, 256), F32)] * 2,
        compiler_params=_params(("parallel", "parallel")),
    )(qkn, qkn, qkn, z, z)


def _attn_bwd(z, qkn, do, c, lse, gains, g, d, name, after=None):
    T = z.shape[0]
    sub, nb, m = _attn_geometry(T, d)
    scale = HEAD_DIM ** -0.5
    extra = [] if after is None else [after]

    def body(qr_ref, kr_ref, vc_ref, vp_ref, qn_ref, qnn_ref, kn_ref, knp_ref, do_ref, don_ref, c_ref, cn_ref,
             lse_ref, lsen_ref, g_ref, *rest):
        dq_ref, dk_ref, dv_ref, dgq_ref, dgk_ref, sq_ref, sk_ref, sv_ref = rest[len(extra):]
        jb = pl.program_id(0)

        @pl.when((jb == 0) & (pl.program_id(1) == 0))
        def _():
            dgq_ref[...] = jnp.zeros_like(dgq_ref)
            dgk_ref[...] = jnp.zeros_like(dgk_ref)

        gq, gk = g_ref[0:1, :], g_ref[1:2, :]

        def step(s, carry):
            jj, r = s // d, s % d
            here, edge = _attn_rows(jj, r, sub, d), _attn_rows(0, r, sub, d)
            before = _attn_rows(jnp.maximum(jj - 1, 0), r, sub, d)
            behind = _attn_rows(jnp.minimum(jj + 1, m - 1), r, sub, d)
            first, last = jj == 0, jj == m - 1
            block = jb * m + jj
            ma, mask_c, mask_p = _attn_masks(block > 0)
            mask_n = _attn_masks(block < nb - 1)[2]
            qhat, rq = _head_norm(qr_ref[here, :], gq, ma)
            qn = qhat * gq
            qn_next = _pick(last, qnn_ref[edge, :], qn_ref[behind, :])
            khat, rk = _head_norm(kr_ref[here, :], gk, ma)
            kcb = (khat * gk).astype(BF)
            kpb = _pick(first, knp_ref[edge, :], kn_ref[before, :]).astype(BF)
            vcb = vc_ref[here, :].astype(BF)
            vpb = _pick(first, vp_ref[edge, :], vc_ref[before, :]).astype(BF)
            do_t, don_t = do_ref[here, :], _pick(last, don_ref[edge, :], do_ref[behind, :])
            c_t, cn_t = c_ref[here, :], _pick(last, cn_ref[edge, :], c_ref[behind, :])
            lse_t, lsen_t = lse_ref[here, :], _pick(last, lsen_ref[edge, :], lse_ref[behind, :])
            dq_t = None
            dk_t = jnp.zeros((ATTN_BLOCK, LANES), F32)
            dv_t = jnp.zeros((ATTN_BLOCK, LANES), F32)
            for hm in (ma, jnp.logical_not(ma)):
                qh = jnp.where(hm, qn, 0.0).astype(BF)
                doh = jnp.where(hm, do_t, 0.0).astype(BF)
                lse_h = _head_col(lse_t, hm)
                c_h = _head_col(c_t, hm)
                s_c = jnp.where(mask_c, _dot(qh, kcb, "nt") * scale, MASK_VALUE)
                s_p = jnp.where(mask_p, _dot(qh, kpb, "nt") * scale, MASK_VALUE)
                p_c = jnp.exp(s_c - lse_h)
                p_p = jnp.exp(s_p - lse_h)
                ds_c = ((p_c * (_dot(doh, vcb, "nt") + c_h)) * scale).astype(BF)
                ds_p = ((p_p * (_dot(doh, vpb, "nt") + c_h)) * scale).astype(BF)
                dq_h = _dot(ds_c, kcb) + _dot(ds_p, kpb)
                dq_t = dq_h if dq_t is None else jnp.where(ma, dq_t, dq_h)
                qh_n = jnp.where(hm, qn_next, 0.0).astype(BF)
                doh_n = jnp.where(hm, don_t, 0.0).astype(BF)
                s_n = jnp.where(mask_n, _dot(qh_n, kcb, "nt") * scale, MASK_VALUE)
                p_n = jnp.exp(s_n - _head_col(lsen_t, hm))
                ds_n = ((p_n * (_dot(doh_n, vcb, "nt") + _head_col(cn_t, hm))) * scale).astype(BF)
                dv_t = dv_t + _dot(p_c.astype(BF), doh, "tn") + _dot(p_n.astype(BF), doh_n, "tn")
                dk_t = dk_t + _dot(ds_c, qh, "tn") + _dot(ds_n, qh_n, "tn")
            sq_ref[here, :] = _head_norm_bwd(dq_t, qhat, rq, gq, ma)
            sk_ref[here, :] = _head_norm_bwd(dk_t, khat, rk, gk, ma)
            sv_ref[here, :] = dv_t
            dgq_ref[...] += jnp.sum(dq_t * qhat, axis=0, keepdims=True)
            dgk_ref[...] += jnp.sum(dk_t * khat, axis=0, keepdims=True)
            return carry

        lax.fori_loop(0, m * d, step, 0)
        dq_ref[...] = sq_ref[...].astype(BF)
        dk_ref[...] = sk_ref[...].astype(BF)
        dv_ref[...] = sv_ref[...].astype(BF)

    def cur(col0):
        return pl.BlockSpec((m * sub, LANES), lambda j, t: (j, col0 + 2 * g + t))

    def prv(col0):
        return pl.BlockSpec((sub, LANES), lambda j, t: (jnp.maximum(j * m - 1, 0), col0 + 2 * g + t))

    def nxt(col0):
        return pl.BlockSpec((sub, LANES), lambda j, t: (jnp.minimum((j + 1) * m, nb - 1), col0 + 2 * g + t))

    own = pl.BlockSpec((m * sub, LANES), lambda j, t: (j, t))
    own_next = pl.BlockSpec((sub, LANES), lambda j, t: (jnp.minimum((j + 1) * m, nb - 1), t))
    vec = pl.BlockSpec((1, LANES), lambda j, t: (0, 0))
    zq, zk, zv, k0 = OFF_Q // LANES, OFF_K // LANES, OFF_V // LANES, (OFF_K - OFF_Q) // LANES
    return pl.pallas_call(
        body, name=name, grid=(nb // m, 2),
        in_specs=[cur(zq), cur(zk), cur(zv), prv(zv), cur(0), nxt(0), cur(k0), prv(k0), own, own_next, own, own_next,
                  own, own_next, pl.BlockSpec((8, LANES), lambda j, t: (0, 0))] + [ANY] * len(extra),
        out_specs=[own, own, own, vec, vec],
        out_shape=[jax.ShapeDtypeStruct((T, 256), BF)] * 3 + [jax.ShapeDtypeStruct((1, LANES), F32)] * 2,
        scratch_shapes=[pltpu.VMEM((m * sub, LANES), F32)] * 3,
        compiler_params=_params(("arbitrary", "arbitrary")),
    )(z, z, z, z, qkn, qkn, qkn, qkn, do, do, c, c, lse, lse, gains, *extra)


MERGE_ROWS = 256
GATE_TILE = 256


def _group_mix(o_refs, lse_refs):
    lses = [r[...] for r in lse_refs]
    m = jnp.maximum(jnp.maximum(lses[0], lses[1]), lses[2])
    es = [jnp.exp(l - m) for l in lses]
    den = es[0] + es[1] + es[2]
    ws = [e / den for e in es]
    y = ws[0] * o_refs[0][...] + ws[1] * o_refs[1][...] + ws[2] * o_refs[2][...]
    return ws, y


def _sigmoid(v):
    return 1.0 / (1.0 + jnp.exp(-v))


def _merge_specs(T, z, bgate, gpu, gco, gau):
    tm = min(MERGE_ROWS, T)
    row = lambda w: pl.BlockSpec((tm, w), lambda i: (i, 0))
    gate0 = OFF_GATE // GATE_TILE
    gates = [pl.BlockSpec((tm, GATE_TILE), functools.partial(lambda i, cb: (i, cb), cb=gate0 + n))
             for n in range(3 * N_CHIPS)]
    full = lambda a: pl.BlockSpec(a.shape, lambda i: (0,) * a.ndim)
    specs = [row(512), row(512)] + [row(256)] * 6 + gates + [full(bgate), full(gpu), full(gco), full(gau)]
    return tm, row, specs


def _merge_fwd(yp, yc, o3, lse3, z, bgate, gpu, gco, gau, name):
    T = yp.shape[0]
    tm, row, specs = _merge_specs(T, z, bgate, gpu, gco, gau)

    def body(*refs):
        yp_ref, yc_ref = refs[0], refs[1]
        o_refs, lse_refs = refs[2:5], refs[5:8]
        zg = refs[8:20]
        b_ref, gpu_ref, gco_ref, gau_ref, out_ref = refs[20:25]
        yab = _group_mix(o_refs, lse_refs)[1].astype(BF)
        ys = (yp_ref[...], yc_ref[...], yab)
        ups = (gpu_ref, gco_ref, gau_ref)
        for n in range(N_CHIPS):
            acc = None
            for b in range(3):
                gcol = slice(1024 * b + GATE_TILE * n, 1024 * b + GATE_TILE * (n + 1))
                gate = _sigmoid(zg[N_CHIPS * b + n][...] + b_ref[:, gcol])
                term = gate * _dot(ys[b], ups[b][n])
                acc = term if acc is None else acc + term
            out_ref[:, GATE_TILE * n:GATE_TILE * (n + 1)] = acc.astype(BF)

    return pl.pallas_call(
        body, name=name, grid=(T // tm,), in_specs=specs, out_specs=row(1024),
        out_shape=jax.ShapeDtypeStruct((T, 1024), BF), compiler_params=_params(("parallel",)),
    )(yp, yc, *o3, *lse3, *([z] * 12), bgate, gpu, gco, gau)


def _merge_bwd(dm, yp, yc, o3, lse3, z, bgate, gpu, gco, gau, name):
    T = yp.shape[0]
    tm, row, specs = _merge_specs(T, z, bgate, gpu, gco, gau)
    nsteps = T // tm

    def body(*refs):
        dm_ref, yp_ref, yc_ref = refs[0:3]
        o_refs, lse_refs = refs[3:6], refs[6:9]
        zg = refs[9:21]
        b_ref, gpu_ref, gco_ref, gau_ref = refs[21:25]
        dzg_ref, dyp_ref, dyc_ref = refs[25:28]
        do_refs, c_refs = refs[28:31], refs[31:34]
        dgpu_ref, dgco_ref, dgau_ref, dbg_ref = refs[34:38]
        accs = refs[38:41]
        i = pl.program_id(0)

        @pl.when(i == 0)
        def _():
            for a in accs:
                a[...] = jnp.zeros_like(a)
            dbg_ref[...] = jnp.zeros_like(dbg_ref)

        ws, y = _group_mix(o_refs, lse_refs)
        ys = (yp_ref[...], yc_ref[...], y.astype(BF))
        ups = (gpu_ref, gco_ref, gau_ref)
        dys = [None, None, None]
        for n in range(N_CHIPS):
            dmn = dm_ref[:, GATE_TILE * n:GATE_TILE * (n + 1)]
            for b in range(3):
                gcol = slice(1024 * b + GATE_TILE * n, 1024 * b + GATE_TILE * (n + 1))
                gate = _sigmoid(zg[N_CHIPS * b + n][...] + b_ref[:, gcol])
                up = _dot(ys[b], ups[b][n])
                dzg = (dmn * up) * (gate * (1.0 - gate))
                dzg_ref[:, gcol] = dzg.astype(BF)
                dbg_ref[:, gcol] += jnp.sum(dzg, axis=0, keepdims=True)
                dup = (dmn * gate).astype(BF)
                accs[b][n] += _dot(ys[b], dup, "tn")
                dyb = _dot(dup, ups[b][n], "nt")
                dys[b] = dyb if dys[b] is None else dys[b] + dyb
        dyp_ref[...] = dys[0]
        dyc_ref[...] = dys[1]
        dya = dys[2]
        lane = lax.broadcasted_iota(jnp.int32, dya.shape, 1) // HEAD_DIM
        pr = dya * y
        rho = jnp.zeros_like(pr)
        for h in range(256 // HEAD_DIM):
            hm = lane == h
            rho = jnp.where(hm, jnp.sum(jnp.where(hm, pr, 0.0), axis=-1, keepdims=True), rho)
        for g in range(3):
            do_refs[g][...] = ws[g] * dya
            c_refs[g][...] = -(ws[g] * rho)

        @pl.when(i == nsteps - 1)
        def _():
            dgpu_ref[...] = accs[0][...].astype(BF)
            dgco_ref[...] = accs[1][...].astype(BF)
            dgau_ref[...] = accs[2][...].astype(BF)

    full = lambda a: pl.BlockSpec(a.shape, lambda i: (0,) * a.ndim)
    out_specs = ([row(3072), row(512), row(512)] + [row(256)] * 6 + [full(gpu), full(gco), full(gau)]
                 + [pl.BlockSpec((1, 3072), lambda i: (0, 0))])
    out_shape = ([jax.ShapeDtypeStruct((T, 3072), BF)] + [jax.ShapeDtypeStruct((T, 512), F32)] * 2
                 + [jax.ShapeDtypeStruct((T, 256), F32)] * 6
                 + [jax.ShapeDtypeStruct(g.shape, BF) for g in (gpu, gco, gau)]
                 + [jax.ShapeDtypeStruct((1, 3072), F32)])
    return pl.pallas_call(
        body, name=name, grid=(nsteps,), in_specs=[row(1024)] + specs, out_specs=out_specs, out_shape=out_shape,
        scratch_shapes=[pltpu.VMEM(g.shape, F32) for g in (gpu, gco, gau)],
        compiler_params=_params(("arbitrary",)),
    )(dm, yp, yc, *o3, *lse3, *([z] * 12), bgate, gpu, gco, gau)


def _layer_fwd(x, w, tag, after=None, late=None):
    hb = _rms_fwd(x, w["norm_mix"], f"rms_mix_{tag}", after=after)
    z = _mm(hb, w["w_in"], "nn", f"in_proj_{tag}", tm=512, tn=3712, tk=1024, n_outer=True)
    yp, yc = _poolconv_fwd(z, w["pool_mix"], w["pool_scale"], w["conv_w"], f"poolconv_{tag}")
    qkn = _qk_norm(z, w["qk_gain"], f"qk_norm_{tag}")
    o3, lse3 = [], []
    for g, d in enumerate(ATTN_DILATIONS):
        o, lse = _attn_fwd(z, qkn, g, d, f"attn{g}_{tag}")
        o3.append(o)
        lse3.append(lse)
    if late is not None:
        w = dict(w, **late(lse3[-1]))
    merged = _merge_fwd(yp, yc, o3, lse3, z, w["b_gate"], w["w_pool_up"], w["w_conv_out"], w["w_attn_up"],
                        f"merge_{tag}")
    x1 = _mm(merged, w["w_o"], "nn", f"out_proj_{tag}", tm=1024, tn=1024, tk=1024, res=x)
    h2b = _rms_fwd(x1, w["norm_mlp"], f"rms_mlp_{tag}")
    rb = _mm(h2b, w["w_ff1"], "nn", f"ff1_{tag}", tm=1024, tn=1024, tk=1024, out_dtype=BF, epi="relu2", n_outer=True,
             b_shards=True)
    x2 = _mm(rb, w["w_ff2"], "nn", f"ff2_{tag}", tm=1024, tn=1024, tk=1024, res=x1)
    saved = dict(x=x, hb=hb, z=z, yp=yp, yc=yc, qkn=qkn, o3=o3, lse3=lse3, merged=merged, x1=x1, h2b=h2b, rb=rb)
    return x2, saved, w


def _layer_bwd(dx2, w, s, tag, after=None, mid=None):
    g = {}
    dab = _mm(dx2, w["w_ff2"], "nt", f"d_ff2_act_{tag}", tm=1024, tn=1024, tk=1024, out_dtype=BF, aux=s["rb"],
              epi="drelu2", after=after)
    g["w_ff2"] = _mm(s["rb"], dx2, "tn", f"d_ff2_w_{tag}", tm=1024, tn=1024, tk=1024, out_dtype=BF)
    g["w_ff1"] = _mm(s["h2b"], dab, "tn", f"d_ff1_w_{tag}", tm=1024, tn=1024, tk=1024, out_dtype=BF, out_shards=True)
    dh2 = _mm(dab, w["w_ff1"], "nt", f"d_ff1_act_{tag}", tm=1024, tn=1024, tk=1024, b_shards=True)
    dx1, g["norm_mlp"] = _rms_bwd(dh2, s["x1"], w["norm_mlp"], dx2, f"d_rms_mlp_{tag}")
    dm = _mm(dx1, w["w_o"], "nt", f"d_out_act_{tag}", tm=1024, tn=1024, tk=1024)
    g["w_o"] = _mm(s["merged"], dx1, "tn", f"d_out_w_{tag}", tm=1024, tn=1024, tk=1024, out_dtype=BF)
    (dzg, dyp, dyc, do0, do1, do2, c0, c1, c2, g["w_pool_up"], g["w_conv_out"], g["w_attn_up"],
     g["b_gate"]) = _merge_bwd(dm, s["yp"], s["yc"], s["o3"], s["lse3"], s["z"], w["b_gate"], w["w_pool_up"],
                               w["w_conv_out"], w["w_attn_up"], f"d_merge_{tag}")
    behind = mid(g) if mid is not None else None
    dq, dk, dv = [], [], []
    dgq = dgk = None
    for gi, d in enumerate(ATTN_DILATIONS):
        dzq, dzk, dzv, pq, pk = _attn_bwd(s["z"], s["qkn"], (do0, do1, do2)[gi], (c0, c1, c2)[gi], s["lse3"][gi],
                                          w["qk_gain"], gi, d, f"d_attn{gi}_{tag}", after=behind)
        dq.append(dzq)
        dk.append(dzk)
        dv.append(dzv)
        dgq = pq if dgq is None else dgq + pq
        dgk = pk if dgk is None else dgk + pk
    g["q_gain"] = dgq[:, :HEAD_DIM] + dgq[:, HEAD_DIM:]
    g["k_gain"] = dgk[:, :HEAD_DIM] + dgk[:, HEAD_DIM:]
    dzpc, g["pool_mix"], g["pool_scale"], g["conv_w"] = _poolconv_bwd(
        s["z"], dyp, dyc, w["pool_mix"], w["pool_scale"], w["conv_w"], f"d_poolconv_{tag}")
    dz = jnp.concatenate([dzpc] + dq + dk + dv + [dzg], axis=1)
    g["w_in"] = _mm(s["hb"], dz, "tn", f"d_in_w_{tag}", tm=512, tn=3712, tk=512, out_dtype=BF)
    dh = _mm(dz, w["w_in"], "nt", f"d_in_act_{tag}", tm=512, tn=1024, tk=3712)
    dx, g["norm_mix"] = _rms_bwd(dh, s["x"], w["norm_mix"], dx1, f"d_rms_mix_{tag}")
    return dx, g


def _position():
    x, y, c = lax.axis_index("x"), lax.axis_index("y"), lax.axis_index("c")
    chips = [(1 - x, y), (x, 1 - y), (1 - x, 1 - y)]
    return x, y, c, 2 * x + y, chips, [2 * cx + cy for cx, cy in chips]


def _remote(src, dst, ssem, rsem, dev):
    return pltpu.make_async_remote_copy(src_ref=src, dst_ref=dst, send_sem=ssem, recv_sem=rsem, device_id=dev,
                                        device_id_type=MESH_ID)


def _position_operand():
    x, y, c = lax.axis_index("x"), lax.axis_index("y"), lax.axis_index("c")
    return jnp.stack([2 * x + y, c]).astype(jnp.int32)


def _halves(a):
    return a.reshape(a.shape[0], 2, a.shape[1] // 2, a.shape[2])


def _gather(bufs, name):
    n = len(bufs)
    views = [_halves(b) for b in bufs]

    def body(*refs):
        outs = refs[n:2 * n]
        ssem, rsem, fssem, frsem = refs[2 * n:]
        x, y, c, q, chips, qs = _position()
        sib = (x, y, 1 - c)
        sent = []
        for k in range(n):
            mine = outs[k].at[q, c]
            for j, chip in enumerate(chips):
                cp = _remote(mine, mine, ssem.at[k, j], rsem.at[k, j], (chip[0], chip[1], c))
                cp.start()
                sent.append(cp)
        for k in range(n):
            for j, chip in enumerate(chips):
                slot = outs[k].at[qs[j], c]
                _remote(slot, slot, ssem.at[k, j], rsem.at[k, j], (chip[0], chip[1], c)).wait_recv()
                cp = _remote(slot, slot, fssem.at[k, j], frsem.at[k, j], sib)
                cp.start()
                sent.append(cp)
        for k in range(n):
            for j in range(3):
                slot = outs[k].at[qs[j], 1 - c]
                _remote(slot, slot, fssem.at[k, j], frsem.at[k, j], sib).wait_recv()
        for cp in sent:
            cp.wait_send()

    outs = pl.pallas_call(
        body, name=name, in_specs=[ANY] * n, out_specs=[ANY] * n,
        out_shape=[jax.ShapeDtypeStruct(v.shape, v.dtype) for v in views],
        input_output_aliases={k: k for k in range(n)},
        scratch_shapes=[pltpu.SemaphoreType.DMA((n, 3))] * 4,
    )(*views)
    return [o.reshape(b.shape) for o, b in zip(outs, bufs)]


SEM = pl.BlockSpec(memory_space=pltpu.SEMAPHORE)
TOKEN = jax.ShapeDtypeStruct((8, LANES), F32)
TOKEN_SPEC = pl.BlockSpec(memory_space=pltpu.VMEM)


def _split_params():
    return pltpu.CompilerParams(has_side_effects=pltpu.SideEffectType.DATAFLOW_SIDE_EFFECTING)


def _gather_start(bufs, name):
    n = len(bufs)
    views = [_halves(b) for b in bufs]

    def body(*refs):
        ssem, rsem = refs[n:n + ns], refs[n + ns:n + 2 * ns]
        outs, token = refs[n + 2 * ns:2 * n + 2 * ns], refs[2 * n + 2 * ns]
        x, y, c, q, chips, qs = _position()
        for k in range(n):
            mine = outs[k].at[q, c]
            for j, chip in enumerate(chips):
                _remote(mine, mine, ssem[3 * k + j], rsem[3 * k + j], (chip[0], chip[1], c)).start()
        token[...] = jnp.zeros_like(token)

    ns = 3 * n
    outs = pl.pallas_call(
        body, name=name, in_specs=[ANY] * n, out_specs=[SEM] * (2 * ns) + [ANY] * n + [TOKEN_SPEC],
        out_shape=[pltpu.SemaphoreType.DMA(())] * (2 * ns) + [jax.ShapeDtypeStruct(v.shape, v.dtype) for v in views]
        + [TOKEN],
        input_output_aliases={k: k + 2 * ns for k in range(n)}, compiler_params=_split_params(),
    )(*views)
    return list(outs[:ns]), list(outs[ns:2 * ns]), list(outs[2 * ns:2 * ns + n]), outs[2 * ns + n]


def _gather_finish(ssem, rsem, views, after, name_wait, name_forward, shapes):
    n = len(views)
    ns = len(ssem)

    def wait_body(*refs):
        ssem_ref, rsem_ref = refs[n:n + ns], refs[n + ns:n + 2 * ns]
        outs = refs[n + 2 * ns + 1:]
        x, y, c, q, chips, qs = _position()
        for k in range(n):
            for j, chip in enumerate(chips):
                cp = _remote(outs[k].at[q, c], outs[k].at[qs[j], c], ssem_ref[3 * k + j], rsem_ref[3 * k + j],
                             (chip[0], chip[1], c))
                cp.wait_send()
                cp.wait_recv()

    landed = pl.pallas_call(
        wait_body, name=name_wait, in_specs=[ANY] * n + [SEM] * (2 * ns) + [ANY], out_specs=[ANY] * n,
        out_shape=[jax.ShapeDtypeStruct(v.shape, v.dtype) for v in views],
        input_output_aliases={k: k for k in range(n)}, compiler_params=_split_params(),
    )(*views, *ssem, *rsem, after)

    def forward_body(*refs):
        outs = refs[n:2 * n]
        fssem, frsem = refs[2 * n:]
        x, y, c, q, chips, qs = _position()
        sib = (x, y, 1 - c)
        sent = []
        for k in range(n):
            for j in range(3):
                slot = outs[k].at[qs[j], c]
                cp = _remote(slot, slot, fssem.at[k, j], frsem.at[k, j], sib)
                cp.start()
                sent.append(cp)
        for k in range(n):
            for j in range(3):
                slot = outs[k].at[qs[j], 1 - c]
                _remote(slot, slot, fssem.at[k, j], frsem.at[k, j], sib).wait_recv()
        for cp in sent:
            cp.wait_send()

    outs = pl.pallas_call(
        forward_body, name=name_forward, in_specs=[ANY] * n, out_specs=[ANY] * n,
        out_shape=[jax.ShapeDtypeStruct(v.shape, v.dtype) for v in views],
        input_output_aliases={k: k for k in range(n)}, scratch_shapes=[pltpu.SemaphoreType.DMA((n, 3))] * 2,
    )(*landed)
    return [o.reshape(s) for o, s in zip(outs, shapes)]


def _chip_exchange_start(parts, name):
    n = len(parts)

    def body(*refs):
        ssem, rsem = refs[n:n + ns], refs[n + ns:n + 2 * ns]
        base = n + 2 * ns
        srcs, outs, token = refs[base:base + n], refs[base + n:base + 2 * n], refs[base + 2 * n]
        x, y, c, q, chips, qs = _position()
        for k in range(n):
            for j, chip in enumerate(chips):
                _remote(srcs[k].at[qs[j]], outs[k].at[j], ssem[3 * k + j], rsem[3 * k + j],
                        (chip[0], chip[1], c)).start()
        token[...] = jnp.zeros_like(token)

    ns = 3 * n
    outs = pl.pallas_call(
        body, name=name, in_specs=[ANY] * n, out_specs=[SEM] * (2 * ns) + [ANY] * (2 * n) + [TOKEN_SPEC],
        out_shape=[pltpu.SemaphoreType.DMA(())] * (2 * ns) + [jax.ShapeDtypeStruct(a.shape, a.dtype) for a in parts]
        + [jax.ShapeDtypeStruct((3,) + a.shape[1:], a.dtype) for a in parts] + [TOKEN],
        input_output_aliases={k: k + 2 * ns for k in range(n)}, compiler_params=_split_params(),
    )(*parts)
    b = 2 * ns
    return list(outs[:ns]), list(outs[ns:b]), list(outs[b:b + n]), list(outs[b + n:b + 2 * n]), outs[b + 2 * n]


def _chip_exchange_wait(ssem, rsem, parts, landing, after, name):
    n = len(parts)
    ns = len(ssem)

    def body(*refs):
        ssem_ref, rsem_ref = refs[2 * n:2 * n + ns], refs[2 * n + ns:2 * n + 2 * ns]
        base = 2 * n + 2 * ns + 1
        srcs, outs = refs[base:base + n], refs[base + n:]
        x, y, c, q, chips, qs = _position()
        for k in range(n):
            for j, chip in enumerate(chips):
                cp = _remote(srcs[k].at[qs[j]], outs[k].at[j], ssem_ref[3 * k + j], rsem_ref[3 * k + j],
                             (chip[0], chip[1], c))
                cp.wait_send()
                cp.wait_recv()

    outs = pl.pallas_call(
        body, name=name, in_specs=[ANY] * (2 * n) + [SEM] * (2 * ns) + [ANY], out_specs=[ANY] * (2 * n),
        out_shape=[jax.ShapeDtypeStruct(a.shape, a.dtype) for a in list(parts) + list(landing)],
        input_output_aliases={k: k for k in range(2 * n)}, compiler_params=_split_params(),
    )(*parts, *landing, *ssem, *rsem, after)
    return list(outs[:n]), list(outs[n:])


def _pair_swap(views, name):
    n = len(views)

    def body(*refs):
        ins, outs = refs[:n], refs[n:2 * n]
        ssem, rsem = refs[2 * n:]
        x, y, c, _, _, _ = _position()
        cps = [_remote(ins[k].at[pl.ds(0, N_CHIPS), 1 - c], outs[k], ssem.at[k], rsem.at[k], (x, y, 1 - c))
               for k in range(n)]
        for cp in cps:
            cp.start()
        for cp in cps:
            cp.wait()

    return pl.pallas_call(
        body, name=name, in_specs=[ANY] * n, out_specs=[ANY] * n,
        out_shape=[jax.ShapeDtypeStruct((v.shape[0],) + v.shape[2:], v.dtype) for v in views],
        scratch_shapes=[pltpu.SemaphoreType.DMA((n,))] * 2,
    )(*views)


def _chip_exchange(parts, name):
    n = len(parts)

    def body(*refs):
        ins, outs = refs[:n], refs[n:2 * n]
        ssem, rsem = refs[2 * n:]
        x, y, c, q, chips, qs = _position()
        cps = []
        for k in range(n):
            for j, chip in enumerate(chips):
                cp = _remote(ins[k].at[qs[j]], outs[k].at[j], ssem.at[k, j], rsem.at[k, j], (chip[0], chip[1], c))
                cp.start()
                cps.append(cp)
        for cp in cps:
            cp.wait_recv()
        for cp in cps:
            cp.wait_send()

    return pl.pallas_call(
        body, name=name, in_specs=[ANY] * n, out_specs=[ANY] * n,
        out_shape=[jax.ShapeDtypeStruct((3,) + a.shape[1:], a.dtype) for a in parts],
        scratch_shapes=[pltpu.SemaphoreType.DMA((n, 3))] * 2,
    )(*parts)


def _pair_send(arrays, name):
    n = len(arrays)

    def body(*refs):
        ins, outs = refs[:n], refs[n:2 * n]
        ssem, rsem = refs[2 * n:]
        x, y, c, _, _, _ = _position()
        cps = [_remote(ins[k], outs[k], ssem.at[k], rsem.at[k], (x, y, 1 - c)) for k in range(n)]
        for cp in cps:
            cp.start()
        for cp in cps:
            cp.wait()

    return pl.pallas_call(
        body, name=name, in_specs=[ANY] * n, out_specs=[ANY] * n,
        out_shape=[jax.ShapeDtypeStruct(a.shape, a.dtype) for a in arrays],
        scratch_shapes=[pltpu.SemaphoreType.DMA((n,))] * 2,
    )(*arrays)


def _all_to_all_small(part):
    P = part.shape[0]

    def body(in_ref, out_ref, lsem, ssem, rsem):
        x, y, c = lax.axis_index("x"), lax.axis_index("y"), lax.axis_index("c")
        me = 4 * x + 2 * y + c
        flips = [(fx, fy, fc) for fx in (0, 1) for fy in (0, 1) for fc in (0, 1)][1:]
        peers = [((x + fx) % 2, (y + fy) % 2, (c + fc) % 2) for fx, fy, fc in flips]
        loc = pltpu.make_async_copy(in_ref, out_ref.at[me], lsem)
        loc.start()
        cps = [_remote(in_ref, out_ref.at[me], ssem.at[j], rsem.at[j], peer) for j, peer in enumerate(peers)]
        for cp in cps:
            cp.start()
        for j, (px, py, pc) in enumerate(peers):
            _remote(in_ref, out_ref.at[4 * px + 2 * py + pc], ssem.at[j], rsem.at[j], peers[j]).wait_recv()
        for cp in cps:
            cp.wait_send()
        loc.wait()

    return pl.pallas_call(
        body, name="small_grad_exchange", in_specs=[ANY], out_specs=ANY,
        out_shape=jax.ShapeDtypeStruct((8, P, LANES), F32),
        scratch_shapes=[pltpu.SemaphoreType.DMA(())] + [pltpu.SemaphoreType.DMA((7,))] * 2,
    )(part)


def _row_tile(rows, width, n_arrays):
    t = rows
    while t % 2 == 0 and t > 8 and 2 * n_arrays * t * width * 4 > VMEM_LIMIT // 2:
        t //= 2
    return t


def _scalar_grid(grid, in_specs, out_specs):
    return pltpu.PrefetchScalarGridSpec(num_scalar_prefetch=1, grid=grid, in_specs=in_specs, out_specs=out_specs)


def _cast_place(w3, layer, pos, name):
    _, r, c = w3.shape
    tr = _row_tile(r, c, 2)

    def body(pos_ref, w_ref, o_ref):
        o_ref[...] = w_ref[...].astype(BF)

    return pl.pallas_call(
        body, name=name,
        grid_spec=_scalar_grid((r // tr,), [pl.BlockSpec((None, tr, c), lambda i, pos: (layer, i, 0))],
                               pl.BlockSpec((None, tr, c), lambda i, pos: (pos[0], i, 0))),
        out_shape=jax.ShapeDtypeStruct((N_CHIPS, r, c), BF), compiler_params=_params(("parallel",)),
    )(pos, w3)


def _pair_sum(view, recv, pos, name):
    _, _, hr, c = view.shape
    tr = _row_tile(hr, c, 3)

    def body(pos_ref, g_ref, r_ref, o_ref):
        o_ref[...] = (g_ref[...].astype(F32) + r_ref[...].astype(F32)).astype(BF)

    blk = pl.BlockSpec((None, tr, c), lambda p, i, pos: (p, i, 0))
    return pl.pallas_call(
        body, name=name,
        grid_spec=_scalar_grid((N_CHIPS, hr // tr),
                               [pl.BlockSpec((None, None, tr, c), lambda p, i, pos: (p, pos[1], i, 0)), blk], blk),
        out_shape=jax.ShapeDtypeStruct(recv.shape, BF), compiler_params=_params(("parallel", "parallel")),
    )(pos, view, recv)


def _chip_sum(parts, recv, pos, name):
    _, hr, c = parts.shape
    tr = _row_tile(hr, c, 6)

    def body(pos_ref, p_ref, r_ref, o_ref):
        acc = p_ref[...].astype(F32)
        for j in range(3):
            acc = acc + r_ref[j].astype(F32)
        o_ref[...] = acc

    return pl.pallas_call(
        body, name=name,
        grid_spec=_scalar_grid((hr // tr,),
                               [pl.BlockSpec((None, tr, c), lambda i, pos: (pos[0], i, 0)),
                                pl.BlockSpec((3, tr, c), lambda i, pos: (0, i, 0))],
                               pl.BlockSpec((tr, c), lambda i, pos: (i, 0))),
        out_shape=jax.ShapeDtypeStruct((hr, c), F32), compiler_params=_params(("parallel",)),
    )(pos, parts, recv)


def _sum_slices(a, name):
    n, rows, width = a.shape
    tr = _row_tile(rows, width, n + 1)

    def body(a_ref, o_ref):
        acc = a_ref[0].astype(F32)
        for i in range(1, n):
            acc = acc + a_ref[i].astype(F32)
        o_ref[...] = acc

    return pl.pallas_call(
        body, name=name, grid=(rows // tr,), in_specs=[pl.BlockSpec((n, tr, width), lambda i: (0, i, 0))],
        out_specs=pl.BlockSpec((tr, width), lambda i: (i, 0)), out_shape=jax.ShapeDtypeStruct((rows, width), F32),
        compiler_params=_params(("parallel",)),
    )(a)


def _adamw_update(w, g, m, v):
    nm = ADAM_B1 * m + (1.0 - ADAM_B1) * g
    nv = ADAM_B2 * v + (1.0 - ADAM_B2) * (g * g)
    m_hat = nm / (1.0 - ADAM_B1 ** ADAM_STEP)
    v_hat = nv / (1.0 - ADAM_B2 ** ADAM_STEP)
    return -ADAM_LR * (m_hat / (jnp.sqrt(v_hat) + ADAM_EPS) + ADAM_WD * w), nm, nv


def _adamw(w, g, m, v, name):
    rows, width = w.shape
    tr = _row_tile(rows, width, 7)

    def body(w_ref, g_ref, m_ref, v_ref, d_ref, nm_ref, nv_ref):
        d_ref[...], nm_ref[...], nv_ref[...] = _adamw_update(w_ref[...], g_ref[...], m_ref[...], v_ref[...])

    blk = pl.BlockSpec((tr, width), lambda i: (i, 0))
    return pl.pallas_call(
        body, name=name, grid=(rows // tr,), in_specs=[blk] * 4, out_specs=[blk] * 3,
        out_shape=[jax.ShapeDtypeStruct((rows, width), F32)] * 3, compiler_params=_params(("parallel",)),
    )(w, g, m, v)


def _adamw_halves(w3, m3, v3, mine, other, pos, name):
    depth, r, c = w3.shape
    assert depth == 2
    hr = r // 2
    tr = _row_tile(hr, c, 11)
    sources = ((0, True, mine[0]), (0, False, other[0]), (1, True, mine[1]), (1, False, other[1]))

    def active(l, h, core, layer, own):
        mine_half = h == core
        return (l == layer) & (mine_half if own else jnp.logical_not(mine_half))

    def body(pos_ref, w_ref, m_ref, v_ref, *rest):
        g_refs, (go_ref, d_ref, nm_ref, nv_ref) = rest[:4], rest[4:]
        l, h = pl.program_id(0), pl.program_id(1)
        for (layer, own, _), g_ref in zip(sources, g_refs):
            @pl.when(active(l, h, pos_ref[1], layer, own))
            def _():
                gv = g_ref[...]
                go_ref[...] = gv
                d_ref[...], nm_ref[...], nv_ref[...] = _adamw_update(w_ref[...], gv, m_ref[...], v_ref[...])

    def gspec(layer, own):
        return pl.BlockSpec((tr, c), lambda l, h, i, pos: (jnp.where(active(l, h, pos[1], layer, own), i, 0), 0))

    blk = pl.BlockSpec((None, None, tr, c), lambda l, h, i, pos: (l, h, i, 0))
    view = lambda a: a.reshape(depth, 2, hr, c)
    outs = pl.pallas_call(
        body, name=name,
        grid_spec=_scalar_grid((depth, 2, hr // tr), [blk] * 3 + [gspec(layer, own) for layer, own, _ in sources],
                               [blk] * 4),
        out_shape=[jax.ShapeDtypeStruct((depth, 2, hr, c), F32)] * 4,
        compiler_params=_params(("parallel", "parallel", "parallel")),
    )(pos, view(w3), view(m3), view(v3), *[s[2] for s in sources])
    return [o.reshape(w3.shape) for o in outs]


BIG = ("w_in", "w_pool_up", "w_conv_out", "w_attn_up", "w_o", "w_ff1", "w_ff2")
SMALL = ("norm_mix", "b_gate", "pool_mix", "pool_scale", "conv_w", "q_gain", "k_gain", "norm_mlp")
ORDER = ("norm_mix", "w_in", "b_gate", "pool_mix", "pool_scale", "conv_w", "q_gain", "k_gain", "w_pool_up",
         "w_conv_out", "w_attn_up", "w_o", "norm_mlp", "w_ff1", "w_ff2")
COLUMN_SHARDED = ("w_in", "w_pool_up", "w_conv_out", "w_attn_up", "w_ff1")


def _matrix_weights(gathered):
    w = {}
    for name, g4 in gathered.items():
        if name == "w_in":
            w[name] = jnp.transpose(g4, (1, 0, 2)).reshape(g4.shape[1], N_CHIPS * g4.shape[2])
        elif name in COLUMN_SHARDED:
            w[name] = g4
        else:
            w[name] = g4.reshape(N_CHIPS * g4.shape[1], g4.shape[2])
    return w


def _small_weights(l, small):
    w = {}
    w["norm_mix"] = small["norm_mix"][l][None]
    w["norm_mlp"] = small["norm_mlp"][l][None]
    w["b_gate"] = small["b_gate"][l][None]
    w["pool_mix"] = small["pool_mix"][l].astype(BF)
    w["pool_scale"] = small["pool_scale"][l][None]
    w["conv_w"] = jnp.pad(small["conv_w_full"][l], ((0, 5), (0, 0)))
    w["qk_gain"] = jnp.pad(jnp.stack([jnp.tile(small["q_gain"][l], 2), jnp.tile(small["k_gain"][l], 2)]), ((0, 6), (0, 0)))
    return w


def _to_chip_major(name, g):
    if name == "w_in":
        return jnp.transpose(g.reshape(g.shape[0], N_CHIPS, g.shape[1] // N_CHIPS), (1, 0, 2))
    if name in COLUMN_SHARDED:
        return g
    return g.reshape(N_CHIPS, g.shape[0] // N_CHIPS, g.shape[1])


def _pad8(a):
    a = a.reshape(-1, LANES)
    return jnp.pad(a, ((0, (-a.shape[0]) % 8), (0, 0)))


def kernel(x, norm_mix, w_in, b_gate, pool_mix, pool_scale, conv_w, q_gain, k_gain, w_pool_up, w_conv_out, w_attn_up, w_o, norm_mlp, w_ff1, w_ff2, loss_target, m_norm_mix, m_w_in, m_b_gate, m_pool_mix, m_pool_scale, m_conv_w, m_q_gain, m_k_gain, m_w_pool_up, m_w_conv_out, m_w_attn_up, m_w_o, m_norm_mlp, m_w_ff1, m_w_ff2, v_norm_mix, v_w_in, v_b_gate, v_pool_mix, v_pool_scale, v_conv_w, v_q_gain, v_k_gain, v_w_pool_up, v_w_conv_out, v_w_attn_up, v_w_o, v_norm_mlp, v_w_ff1, v_w_ff2):
    weights = dict(norm_mix=norm_mix, w_in=w_in, b_gate=b_gate, pool_mix=pool_mix, pool_scale=pool_scale, conv_w=conv_w,
                   q_gain=q_gain, k_gain=k_gain, w_pool_up=w_pool_up, w_conv_out=w_conv_out, w_attn_up=w_attn_up,
                   w_o=w_o, norm_mlp=norm_mlp, w_ff1=w_ff1, w_ff2=w_ff2)
    moms = dict(norm_mix=m_norm_mix, w_in=m_w_in, b_gate=m_b_gate, pool_mix=m_pool_mix, pool_scale=m_pool_scale,
                conv_w=m_conv_w, q_gain=m_q_gain, k_gain=m_k_gain, w_pool_up=m_w_pool_up, w_conv_out=m_w_conv_out,
                w_attn_up=m_w_attn_up, w_o=m_w_o, norm_mlp=m_norm_mlp, w_ff1=m_w_ff1, w_ff2=m_w_ff2)
    vels = dict(norm_mix=v_norm_mix, w_in=v_w_in, b_gate=v_b_gate, pool_mix=v_pool_mix, pool_scale=v_pool_scale,
                conv_w=v_conv_w, q_gain=v_q_gain, k_gain=v_k_gain, w_pool_up=v_w_pool_up, w_conv_out=v_w_conv_out,
                w_attn_up=v_w_attn_up, w_o=v_w_o, norm_mlp=v_norm_mlp, w_ff1=v_w_ff1, w_ff2=v_w_ff2)
    depth = norm_mix.shape[0]
    q = 2 * lax.axis_index("x") + lax.axis_index("y")
    pos = _position_operand()

    assert depth == 2, "the second layer's gather hides behind the first layer's forward, and likewise backward"
    first, rest = BIG[:1], BIG[1:]
    bufs = [{n: _cast_place(weights[n], l, pos, f"cast_{n}_l{l}") for n in BIG} for l in range(depth)]
    w_first = _matrix_weights(dict(zip(first, _gather([bufs[0][n] for n in first], "gather_l0_in"))))
    b_ssem, b_rsem, b_views, b_token = _gather_start([bufs[0][n] for n in rest], "gather_start_l0_rest")
    g_ssem, g_rsem, g_views, g_token = _gather_start([bufs[1][n] for n in BIG], "gather_start_l1")
    cw_all = _all_to_all_small(_pad8(jnp.pad(conv_w.reshape(-1), (0, (-conv_w.size) % LANES))))
    conv_w_full = jnp.concatenate(
        [cw_all[2 * p].reshape(-1)[:conv_w.size].reshape(conv_w.shape) for p in range(N_CHIPS)], axis=-1)
    small = dict(weights)
    small["conv_w_full"] = conv_w_full

    def late_weights(t):
        got = _gather_finish(b_ssem, b_rsem, b_views, t, "gather_wait_l0_rest", "gather_forward_l0_rest",
                             [bufs[0][n].shape for n in rest])
        return _matrix_weights(dict(zip(rest, got)))

    wl, saved = [None] * depth, [None] * depth
    h, saved[0], wl[0] = _layer_fwd(x[0], dict(_small_weights(0, small), **w_first), "l0", after=[b_token, g_token],
                                    late=late_weights)
    got = _gather_finish(g_ssem, g_rsem, g_views, h, "gather_wait_l1", "gather_forward_l1",
                         [bufs[1][n].shape for n in BIG])
    h, saved[1], wl[1] = _layer_fwd(h, dict(_small_weights(1, small), **_matrix_weights(dict(zip(BIG, got)))), "l1")
    dh, loss_row = _loss_grad(h, loss_target[0], "loss")

    def pair_stage(names, g, tag):
        views = [_halves(_to_chip_major(n, g[n])) for n in names]
        from_sibling = _pair_swap(views, f"grad_pair_swap_{tag}")
        return [_pair_sum(views[k], from_sibling[k], pos, f"pair_sum_{n}_{tag}") for k, n in enumerate(names)]

    mine, other = [{}, {}], [{}, {}]

    def finish(names, l, started, after, tag):
        ssem, rsem, parts, landing, _ = started
        parts, arrived = _chip_exchange_wait(ssem, rsem, parts, landing, after, f"grad_chip_exchange_wait_{tag}")
        got = [_chip_sum(parts[k], arrived[k], pos, f"chip_sum_{n}_{tag}") for k, n in enumerate(names)]
        mine[l].update(zip(names, got))
        other[l].update(zip(names, _pair_send(got, f"grad_pair_send_{tag}")))

    grads, early = [None] * depth, {}
    dh, grads[1] = _layer_bwd(dh, wl[1], saved[1], "l1")
    second = _chip_exchange_start(pair_stage(BIG, grads[1], "l1"), "grad_chip_exchange_start_l1")

    def start_rest(g):
        early["rest"] = _chip_exchange_start(pair_stage(rest, g, "l0_rest"), "grad_chip_exchange_start_l0_rest")
        return early["rest"][4]

    dh, grads[0] = _layer_bwd(dh, wl[0], saved[0], "l0", after=second[4], mid=start_rest)
    last = _chip_exchange_start(pair_stage(first, grads[0], "l0_in"), "grad_chip_exchange_start_l0_in")
    finish(BIG, 1, second, last[4], "l1")
    finish(rest, 0, early["rest"], last[4], "l0_rest")
    loss = lax.psum(loss_row[0, 0], ("x", "y", "c"))
    full = {}

    pieces = []
    for n in SMALL:
        per_layer = [grads[l][n] for l in range(depth)]
        if n == "conv_w":
            per_layer = [p[:3] for p in per_layer]
        pieces.append(_pad8(jnp.stack(per_layer).reshape(-1)))
    packed = jnp.concatenate(pieces, axis=0)
    summed = _sum_slices(_all_to_all_small(packed), "small_sum")
    row = 0
    for n, piece in zip(SMALL, pieces):
        size = weights[n].size if n != "conv_w" else depth * 3 * 512
        flat = summed[row:row + piece.shape[0]].reshape(-1)[:size]
        row += piece.shape[0]
        if n == "conv_w":
            full[n] = lax.dynamic_slice_in_dim(flat.reshape(depth, 3, 512), q * conv_w.shape[2], conv_w.shape[2], axis=2)
        else:
            full[n] = flat.reshape(weights[n].shape)

    deltas, new_m, new_v = {}, {}, {}

    def update_matrix(n):
        full[n], deltas[n], new_m[n], new_v[n] = _adamw_halves(
            weights[n], moms[n], vels[n], [mine[l][n] for l in range(depth)], [other[l][n] for l in range(depth)], pos,
            f"adamw_{n}")

    for n in rest:
        update_matrix(n)
    finish(first, 0, last, deltas[rest[-1]], "l0_in")
    for n in first:
        update_matrix(n)
    for n in SMALL:
        shape = weights[n].shape
        two_d = (-1, shape[-1]) if n not in ("conv_w", "q_gain", "k_gain") else (1, -1)
        d2, m2, v2 = _adamw(weights[n].reshape(two_d), full[n].reshape(two_d), moms[n].reshape(two_d),
                            vels[n].reshape(two_d), f"adamw_{n}")
        deltas[n], new_m[n], new_v[n] = d2.reshape(shape), m2.reshape(shape), v2.reshape(shape)
        full[n] = full[n].reshape(shape)
    return (loss, dh[None], *[full[n] for n in ORDER], *[deltas[n] for n in ORDER], *[new_m[n] for n in ORDER],
            *[new_v[n] for n in ORDER])
```

```python
import functools

import jax
import jax.numpy as jnp
from jax import lax
from jax.experimental import pallas as pl
from jax.experimental.pallas import tpu as pltpu

F32 = jnp.float32
BF = jnp.bfloat16
MESH_ID = pl.DeviceIdType.MESH
ANY = pl.BlockSpec(memory_space=pl.ANY)

EPS = 1e-6
MASK_VALUE = -1e30
POOL_WINDOWS = (2, 4, 8, 16)
ATTN_DILATIONS = (1, 4, 16)
ATTN_BLOCK = 128
HEAD_DIM = 64
OFF_Q, OFF_K, OFF_V, OFF_GATE = 2048, 2816, 3584, 4352
N_CHIPS = 4
ADAM_LR, ADAM_B1, ADAM_B2, ADAM_EPS, ADAM_WD, ADAM_STEP = 0.001, 0.9, 0.999, 1e-08, 0.01, 10

VMEM_LIMIT = 48 * 1024 * 1024
LANES = 128

_DIMS = {"nn": (((1,), (0,)), ((), ())), "nt": (((1,), (1,)), ((), ())), "tn": (((0,), (0,)), ((), ()))}


def _params(sem):
    return pltpu.CompilerParams(dimension_semantics=sem, vmem_limit_bytes=VMEM_LIMIT)


def _dot(a, b, mode="nn"):
    return lax.dot_general(a, b, _DIMS[mode], preferred_element_type=F32)


def _mm(a, b, mode, name, *, tm, tn, tk, out_dtype=F32, res=None, aux=None, epi=None, n_outer=False,
        b_shards=False, out_shards=False, after=None):
    if mode == "tn":
        K, M = a.shape
    else:
        M, K = a.shape
    if b_shards:
        if mode == "nn":
            assert b.shape[1] == K
            N = b.shape[2] * N_CHIPS
        else:
            assert mode == "nt"
            N = b.shape[1]
            assert b.shape[2] * N_CHIPS == K
    else:
        N = b.shape[0] if mode == "nt" else b.shape[1]
    tm, tn, tk = min(tm, M), min(tn, N), min(tk, K)
    assert M % tm == 0 and N % tn == 0 and K % tk == 0
    nk = K // tk
    if n_outer:
        grid = (N // tn, M // tm, nk)
        ij = lambda p, q_: (q_, p)
    else:
        grid = (M // tm, N // tn, nk)
        ij = lambda p, q_: (p, q_)

    def amap(p, q_, k):
        i, j = ij(p, q_)
        return (k, i) if mode == "tn" else (i, k)

    a_spec = pl.BlockSpec((tk, tm) if mode == "tn" else (tm, tk), amap)
    if b_shards:
        if mode == "nn":
            per = (N // N_CHIPS) // tn
            assert per >= 1 and (N // N_CHIPS) % tn == 0

            def bmap(p, q_, k):
                i, j = ij(p, q_)
                return (j // per, k, j % per)

            b_spec = pl.BlockSpec((None, tk, tn), bmap)
        else:
            per = (K // N_CHIPS) // tk
            assert per >= 1 and (K // N_CHIPS) % tk == 0

            def bmap(p, q_, k):
                i, j = ij(p, q_)
                return (k // per, j, k % per)

            b_spec = pl.BlockSpec((None, tn, tk), bmap)
    else:
        def bmap(p, q_, k):
            i, j = ij(p, q_)
            return (j, k) if mode == "nt" else (k, j)

        b_spec = pl.BlockSpec((tn, tk) if mode == "nt" else (tk, tn), bmap)

    def omap(p, q_, k):
        return ij(p, q_)

    o_spec = pl.BlockSpec((tm, tn), omap)
    if out_shards:
        per_o = (N // N_CHIPS) // tn
        assert per_o >= 1 and (N // N_CHIPS) % tn == 0

        def osmap(p, q_, k):
            i, j = ij(p, q_)
            return (j // per_o, i, j % per_o)

        out_spec0 = pl.BlockSpec((None, tm, tn), osmap)
        out_shape0 = jax.ShapeDtypeStruct((N_CHIPS, M, N // N_CHIPS), out_dtype)
    else:
        out_spec0 = o_spec
        out_shape0 = jax.ShapeDtypeStruct((M, N), out_dtype)

    in_specs = [a_spec, b_spec]
    args = [a, b]
    if res is not None:
        in_specs.append(o_spec)
        args.append(res)
    if aux is not None:
        in_specs.append(o_spec)
        args.append(aux)
    if after is not None:
        in_specs.append(ANY)
        args.append(after)
    out_specs = [out_spec0]
    out_shape = [out_shape0]
    n_out = len(out_shape)
    has_res, has_aux, has_after = res is not None, aux is not None, after is not None

    def body(*refs):
        a_ref, b_ref = refs[0], refs[1]
        pos = 2
        res_ref = aux_ref = None
        if has_res:
            res_ref = refs[pos]
            pos += 1
        if has_aux:
            aux_ref = refs[pos]
            pos += 1
        if has_after:
            pos += 1
        outs = refs[pos:pos + n_out]
        part = _dot(a_ref[...].astype(BF), b_ref[...].astype(BF), mode)

        def finish(acc):
            if res_ref is not None:
                acc = res_ref[...] + acc
            if epi == "relu2":
                r = jnp.maximum(acc, 0.0)
                outs[0][...] = (r * r).astype(out_dtype)
            elif epi == "drelu2":
                outs[0][...] = (acc * (2.0 * jnp.sqrt(aux_ref[...].astype(F32)))).astype(out_dtype)
            else:
                outs[0][...] = acc.astype(out_dtype)

        if nk == 1:
            finish(part)
        else:
            acc_ref = refs[pos + n_out]
            k = pl.program_id(2)

            @pl.when(k == 0)
            def _():
                acc_ref[...] = part

            @pl.when(k > 0)
            def _():
                acc_ref[...] += part

            @pl.when(k == nk - 1)
            def _():
                finish(acc_ref[...])

    scratch = [pltpu.VMEM((tm, tn), F32)] if nk > 1 else []
    out = pl.pallas_call(
        body, name=name, grid=grid, in_specs=in_specs, out_specs=out_specs, out_shape=out_shape,
        scratch_shapes=scratch, compiler_params=_params(("parallel", "parallel", "arbitrary")),
    )(*args)
    return out if n_out > 1 else out[0]


def _rms_fwd(x, gain, name, after=None):
    T, D = x.shape
    tm = min(512, T)

    def body(x_ref, g_ref, *rest):
        o_ref = rest[-1]
        xv = x_ref[...]
        r = lax.rsqrt(jnp.mean(xv * xv, axis=-1, keepdims=True) + EPS)
        o_ref[...] = ((xv * r) * g_ref[...]).astype(BF)

    extra = [] if after is None else list(after) if isinstance(after, (list, tuple)) else [after]
    return pl.pallas_call(
        body, name=name, grid=(T // tm,),
        in_specs=[pl.BlockSpec((tm, D), lambda i: (i, 0)), pl.BlockSpec((1, D), lambda i: (0, 0))] + [ANY] * len(extra),
        out_specs=pl.BlockSpec((tm, D), lambda i: (i, 0)), out_shape=jax.ShapeDtypeStruct((T, D), BF),
        compiler_params=_params(("parallel",)),
    )(x, gain, *extra)


def _rms_bwd(dh, x, gain, dres, name):
    T, D = x.shape
    tm = min(512, T)

    def body(dh_ref, x_ref, g_ref, dres_ref, dx_ref, dg_ref):
        xv = x_ref[...]
        r = lax.rsqrt(jnp.mean(xv * xv, axis=-1, keepdims=True) + EPS)
        xhat = xv * r
        dhv = dh_ref[...]
        dy = dhv * g_ref[...]
        dx_ref[...] = dres_ref[...] + r * (dy - xhat * jnp.mean(dy * xhat, axis=-1, keepdims=True))

        @pl.when(pl.program_id(0) == 0)
        def _():
            dg_ref[...] = jnp.zeros_like(dg_ref)

        dg_ref[...] += jnp.sum(dhv * xhat, axis=0, keepdims=True)

    row = pl.BlockSpec((tm, D), lambda i: (i, 0))
    vec = pl.BlockSpec((1, D), lambda i: (0, 0))
    return pl.pallas_call(
        body, name=name, grid=(T // tm,), in_specs=[row, row, vec, row], out_specs=[row, vec],
        out_shape=[jax.ShapeDtypeStruct((T, D), F32), jax.ShapeDtypeStruct((1, D), F32)],
        compiler_params=_params(("arbitrary",)),
    )(dh, x, gain, dres)


def _loss_grad(y, target, name):
    T, D = y.shape
    tm = min(512, T)

    def body(y_ref, t_ref, dy_ref, l_ref):
        e = y_ref[...] - t_ref[...]
        dy_ref[...] = e / float(D)

        @pl.when(pl.program_id(0) == 0)
        def _():
            l_ref[...] = jnp.zeros_like(l_ref)

        l_ref[...] += 0.5 * jnp.sum(jnp.mean(e * e, axis=-1, keepdims=True))

    row = pl.BlockSpec((tm, D), lambda i: (i, 0))
    return pl.pallas_call(
        body, name=name, grid=(T // tm,), in_specs=[row, row],
        out_specs=[row, pl.BlockSpec((1, LANES), lambda i: (0, 0))],
        out_shape=[jax.ShapeDtypeStruct((T, D), F32), jax.ShapeDtypeStruct((1, LANES), F32)],
        compiler_params=_params(("arbitrary",)),
    )(y, target)


POOL_HALO = 16
CONV_HALO = 8


def _causal_window_sum(v, w):
    s, sh = v, 1
    while sh < w:
        s = s + pltpu.roll(s, sh, 0)
        sh *= 2
    return s


def _anticausal_window_sum(v, w):
    n = v.shape[0]
    s, sh = v, 1
    while sh < w:
        s = s + pltpu.roll(s, n - sh, 0)
        sh *= 2
    return s


def _poolconv_fwd(z, pmix_b, pscale, convw, name):
    T = z.shape[0]
    R = min(512, T)
    PH, CH = R // POOL_HALO, R // CONV_HALO

    def body(u_ref, uh_ref, b_ref, c_ref, ch_ref, x_ref, xh_ref, mix_ref, sc_ref, cw_ref, yp_ref, yc_ref):
        i = pl.program_id(0)
        keep = (i > 0).astype(F32)
        row = i * R + lax.broadcasted_iota(jnp.int32, (R, 1), 0)
        w_all = jnp.concatenate([uh_ref[...] * keep, u_ref[...]], axis=0)
        for g, w in enumerate(POOL_WINDOWS):
            cols = slice(128 * g, 128 * (g + 1))
            wg = w_all[:, cols]
            s = _causal_window_sum(wg, w)[POOL_HALO:]
            cnt = jnp.minimum(row + 1, w).astype(F32)
            dgrp = s / cnt - wg[POOL_HALO:]
            y = _dot(dgrp.astype(BF), mix_ref[g]) * sc_ref[:, cols]
            yp_ref[:, cols] = y.astype(BF)
        uc = jnp.concatenate([ch_ref[...] * xh_ref[...] * keep, c_ref[...] * x_ref[...]], axis=0)
        yc = cw_ref[2:3, :] * uc + cw_ref[0:1, :] * pltpu.roll(uc, 2, 0) + cw_ref[1:2, :] * pltpu.roll(uc, 1, 0)
        yc_ref[...] = (b_ref[...] * yc[CONV_HALO:]).astype(BF)

    def main(cb):
        return pl.BlockSpec((R, 512), lambda i: (i, cb))

    def prev(cb, halo, per):
        return pl.BlockSpec((halo, 512), lambda i: (jnp.maximum(i * per - 1, 0), cb))

    full = lambda a: pl.BlockSpec(a.shape, lambda i: (0,) * a.ndim)
    return pl.pallas_call(
        body, name=name, grid=(T // R,),
        in_specs=[main(0), prev(0, POOL_HALO, PH), main(1), main(2), prev(2, CONV_HALO, CH), main(3),
                  prev(3, CONV_HALO, CH), full(pmix_b), full(pscale), full(convw)],
        out_specs=[pl.BlockSpec((R, 512), lambda i: (i, 0))] * 2,
        out_shape=[jax.ShapeDtypeStruct((T, 512), BF)] * 2,
        compiler_params=_params(("parallel",)),
    )(z, z, z, z, z, z, z, pmix_b, pscale, convw)


def _poolconv_bwd(z, dyp, dyc, pmix_b, pscale, convw, name):
    T = z.shape[0]
    R = min(512, T)
    PH, CH = R // POOL_HALO, R // CONV_HALO
    nsteps = T // R

    def body(u_ref, uh_ref, b_ref, bn_ref, c_ref, ch_ref, x_ref, xh_ref, dyp_ref, dypn_ref, dyc_ref, dycn_ref,
             mix_ref, sc_ref, cw_ref, dz_ref, dmix_ref, dsc_ref, dcw_ref):
        i = pl.program_id(0)
        keep_prev = (i > 0).astype(F32)
        keep_next = (i < nsteps - 1).astype(F32)

        @pl.when(i == 0)
        def _():
            dmix_ref[...] = jnp.zeros_like(dmix_ref)
            dsc_ref[...] = jnp.zeros_like(dsc_ref)
            dcw_ref[...] = jnp.zeros_like(dcw_ref)

        row = i * R + lax.broadcasted_iota(jnp.int32, (R, 1), 0)
        row_ext = i * R + lax.broadcasted_iota(jnp.int32, (R + POOL_HALO, 1), 0)
        w_all = jnp.concatenate([uh_ref[...] * keep_prev, u_ref[...]], axis=0)
        dyp_ext = jnp.concatenate([dyp_ref[...], dypn_ref[...] * keep_next], axis=0)
        for g, w in enumerate(POOL_WINDOWS):
            cols = slice(128 * g, 128 * (g + 1))
            wg = w_all[:, cols]
            s = _causal_window_sum(wg, w)[POOL_HALO:]
            cnt = jnp.minimum(row + 1, w).astype(F32)
            dgrp = (s / cnt - wg[POOL_HALO:]).astype(BF)
            y_pre = _dot(dgrp, mix_ref[g])
            dsc_ref[:, cols] += jnp.sum(dyp_ref[:, cols] * y_pre, axis=0, keepdims=True)
            dyb = (dyp_ext[:, cols] * sc_ref[:, cols]).astype(BF)
            dmix_ref[cols, :] += _dot(dgrp, dyb[:R], "tn")
            dd = _dot(dyb, mix_ref[g], "nt")
            cnt_ext = jnp.minimum(row_ext + 1, w).astype(F32)
            e = _anticausal_window_sum(dd / cnt_ext, w)
            dz_ref[:, cols] = (e[:R] - dd[:R]).astype(BF)
        cw0, cw1, cw2 = cw_ref[0:1, :], cw_ref[1:2, :], cw_ref[2:3, :]
        uc = jnp.concatenate([ch_ref[...] * xh_ref[...] * keep_prev, c_ref[...] * x_ref[...]], axis=0)
        uc1 = pltpu.roll(uc, 1, 0)[CONV_HALO:]
        uc2 = pltpu.roll(uc, 2, 0)[CONV_HALO:]
        uc0 = uc[CONV_HALO:]
        yc = cw2 * uc0 + cw0 * uc2 + cw1 * uc1
        dycv = dyc_ref[...]
        dz_ref[:, 512:1024] = (dycv * yc).astype(BF)
        dv_ext = jnp.concatenate([dycv * b_ref[...], dycn_ref[...] * bn_ref[...] * keep_next], axis=0)
        n_ext = R + CONV_HALO
        duc = (cw2 * dv_ext + cw1 * pltpu.roll(dv_ext, n_ext - 1, 0) + cw0 * pltpu.roll(dv_ext, n_ext - 2, 0))[:R]
        dv = dv_ext[:R]
        dcw_ref[0:1, :] += jnp.sum(dv * uc2, axis=0, keepdims=True)
        dcw_ref[1:2, :] += jnp.sum(dv * uc1, axis=0, keepdims=True)
        dcw_ref[2:3, :] += jnp.sum(dv * uc0, axis=0, keepdims=True)
        dz_ref[:, 1024:1536] = (duc * x_ref[...]).astype(BF)
        dz_ref[:, 1536:2048] = (duc * c_ref[...]).astype(BF)

    def main(cb):
        return pl.BlockSpec((R, 512), lambda i: (i, cb))

    def prev(cb, halo, per):
        return pl.BlockSpec((halo, 512), lambda i: (jnp.maximum(i * per - 1, 0), cb))

    def nxt(cb, halo, per):
        return pl.BlockSpec((halo, 512), lambda i: (jnp.minimum((i + 1) * per, T // halo - 1), cb))

    full = lambda a: pl.BlockSpec(a.shape, lambda i: (0,) * a.ndim)
    return pl.pallas_call(
        body, name=name, grid=(nsteps,),
        in_specs=[main(0), prev(0, POOL_HALO, PH), main(1), nxt(1, CONV_HALO, CH), main(2), prev(2, CONV_HALO, CH),
                  main(3), prev(3, CONV_HALO, CH), main(0), nxt(0, POOL_HALO, PH), main(0), nxt(0, CONV_HALO, CH),
                  full(pmix_b), full(pscale), full(convw)],
        out_specs=[pl.BlockSpec((R, 2048), lambda i: (i, 0)), pl.BlockSpec((512, 128), lambda i: (0, 0)),
                   pl.BlockSpec((1, 512), lambda i: (0, 0)), pl.BlockSpec((8, 512), lambda i: (0, 0))],
        out_shape=[jax.ShapeDtypeStruct((T, 2048), BF), jax.ShapeDtypeStruct((512, 128), F32),
                   jax.ShapeDtypeStruct((1, 512), F32), jax.ShapeDtypeStruct((8, 512), F32)],
        compiler_params=_params(("arbitrary",)),
    )(z, z, z, z, z, z, z, z, dyp, dyp, dyc, dyc, pmix_b, pscale, convw)


def _head_sums(v):
    row = lax.broadcasted_iota(jnp.int32, (LANES, LANES), 0) < HEAD_DIM
    col = lax.broadcasted_iota(jnp.int32, (LANES, LANES), 1) < HEAD_DIM
    same_head = jnp.where(jnp.logical_xor(row, col), 0.0, 1.0).astype(BF)
    hi = v.astype(BF)
    lo = (v - hi.astype(F32)).astype(BF)
    return _dot(hi, same_head) + _dot(lo, same_head)


def _head_norm(x, g2, ma):
    r = lax.rsqrt(_head_sums(x * x) / HEAD_DIM + EPS)
    return x * r, r


def _head_norm_bwd(dy, xhat, r, g2, ma):
    dxh = dy * g2
    return r * (dxh - xhat * (_head_sums(dxh * xhat) / HEAD_DIM))


def _head_col(tile, hm):
    return jnp.max(jnp.where(hm, tile, -jnp.inf), axis=-1, keepdims=True)


def _attn_masks(other_block_exists):
    lane = lax.broadcasted_iota(jnp.int32, (2 * ATTN_BLOCK, ATTN_BLOCK), 1)
    qi = lax.broadcasted_iota(jnp.int32, (2 * ATTN_BLOCK, ATTN_BLOCK), 0) & (ATTN_BLOCK - 1)
    never = (1 - other_block_exists.astype(jnp.int32)) * (2 * ATTN_BLOCK)
    return lane[:ATTN_BLOCK] < HEAD_DIM, lane <= qi, lane >= qi + never


def _stack_heads(x, ma):
    return jnp.concatenate([jnp.where(ma, x, 0.0), jnp.where(ma, 0.0, x)], axis=0)


def _unstack_heads(y, ma):
    return jnp.where(ma, y[:ATTN_BLOCK], y[ATTN_BLOCK:])


def _stack_cols(tile, ma):
    return jnp.concatenate([_head_col(tile, ma), _head_col(tile, jnp.logical_not(ma))], axis=0)


def _qk_norm(z, gains, name):
    T = z.shape[0]
    tm = min(512, T)
    per_kind = (OFF_K - OFF_Q) // 256

    def body(x_ref, g_ref, o_ref):
        ma = lax.broadcasted_iota(jnp.int32, (tm, LANES), 1) < HEAD_DIM
        is_q = jnp.full((1, LANES), pl.program_id(1)) < per_kind
        g = jnp.where(is_q, g_ref[0:1, :], g_ref[1:2, :])
        for t in range(2):
            sl = slice(LANES * t, LANES * (t + 1))
            o_ref[:, sl] = _head_norm(x_ref[:, sl], g, ma)[0] * g

    return pl.pallas_call(
        body, name=name, grid=(T // tm, 2 * per_kind),
        in_specs=[pl.BlockSpec((tm, 256), lambda i, n: (i, OFF_Q // 256 + n)), pl.BlockSpec((8, LANES), lambda i, n: (0, 0))],
        out_specs=pl.BlockSpec((tm, 256), lambda i, n: (i, n)),
        out_shape=jax.ShapeDtypeStruct((T, 2 * (OFF_K - OFF_Q)), F32), compiler_params=_params(("parallel", "parallel")),
    )(z, gains)


ATTN_STEP_ROWS = 1024
ATTN_UNROLL = 2


def _attn_geometry(T, d):
    sub = ATTN_BLOCK * d
    nb = T // sub
    m = max(1, min(nb, ATTN_STEP_ROWS // sub))
    assert T % sub == 0 and nb % m == 0
    return sub, nb, m


def _attn_rows(jj, r, sub, d):
    start = jj * sub + r
    if d == 1:
        return pl.ds(pl.multiple_of(start, ATTN_BLOCK), ATTN_BLOCK)
    return pl.ds(start, ATTN_BLOCK, stride=d)


def _pick(flag, a, b):
    return jnp.where(jnp.full(a.shape, flag.astype(jnp.int32)) > 0, a, b)


def _attn_fwd(z, qkn, g, d, name):
    T = z.shape[0]
    sub, nb, m = _attn_geometry(T, d)
    scale = HEAD_DIM ** -0.5

    def body(q_ref, kc_ref, kp_ref, vc_ref, vp_ref, o_ref, lse_ref):
        jb = pl.program_id(0)

        def step(s, carry):
            jj, r = s // d, s % d
            here, before = _attn_rows(jj, r, sub, d), _attn_rows(jnp.maximum(jj - 1, 0), r, sub, d)
            edge = _attn_rows(0, r, sub, d)
            first = jj == 0
            ma, mask_c, mask_p = _attn_masks(jb * m + jj > 0)
            qs = _stack_heads(q_ref[here, :], ma).astype(BF)
            kcb = kc_ref[here, :].astype(BF)
            kpb = _pick(first, kp_ref[edge, :], kc_ref[before, :]).astype(BF)
            vcb = vc_ref[here, :].astype(BF)
            vpb = _pick(first, vp_ref[edge, :], vc_ref[before, :]).astype(BF)
            s_c = jnp.where(mask_c, _dot(qs, kcb, "nt") * scale, MASK_VALUE)
            s_p = jnp.where(mask_p, _dot(qs, kpb, "nt") * scale, MASK_VALUE)
            mx = jnp.maximum(jnp.max(s_c, axis=-1, keepdims=True), jnp.max(s_p, axis=-1, keepdims=True))
            p_c = jnp.exp(s_c - mx)
            p_p = jnp.exp(s_p - mx)
            den = jnp.sum(p_c, axis=-1, keepdims=True) + jnp.sum(p_p, axis=-1, keepdims=True)
            o = (_dot(p_c.astype(BF), vcb) + _dot(p_p.astype(BF), vpb)) / den
            o_ref[here, :] = _unstack_heads(o, ma)
            lse_ref[here, :] = _unstack_heads(jnp.broadcast_to(mx + jnp.log(den), o.shape), ma)
            return carry

        lax.fori_loop(0, m * d, step, 0, unroll=ATTN_UNROLL)

    def cur(col0):
        return pl.BlockSpec((m * sub, LANES), lambda j, t: (j, col0 + 2 * g + t))

    def prv(col0):
        return pl.BlockSpec((sub, LANES), lambda j, t: (jnp.maximum(j * m - 1, 0), col0 + 2 * g + t))

    k0, v0 = (OFF_K - OFF_Q) // LANES, OFF_V // LANES
    out = pl.BlockSpec((m * sub, LANES), lambda j, t: (j, t))
    return pl.pallas_call(
        body, name=name, grid=(nb // m, 2), in_specs=[cur(0), cur(k0), prv(k0), cur(v0), prv(v0)],
        out_specs=[out, out], out_shape=[jax.ShapeDtypeStruct((T, 256), F32)] * 2,
        compiler_params=_params(("parallel", "parallel")),
    )(qkn, qkn, qkn, z, z)


def _attn_bwd(z, qkn, do, c, lse, gains, g, d, name, after=None):
    T = z.shape[0]
    sub, nb, m = _attn_geometry(T, d)
    scale = HEAD_DIM ** -0.5
    extra = [] if after is None else [after]

    def body(qr_ref, kr_ref, vc_ref, vp_ref, qn_ref, qnn_ref, kn_ref, knp_ref, do_ref, don_ref, c_ref, cn_ref,
             lse_ref, lsen_ref, g_ref, *rest):
        dq_ref, dk_ref, dv_ref, dgq_ref, dgk_ref, sq_ref, sk_ref, sv_ref = rest[len(extra):]
        jb = pl.program_id(0)

        @pl.when((jb == 0) & (pl.program_id(1) == 0))
        def _():
            dgq_ref[...] = jnp.zeros_like(dgq_ref)
            dgk_ref[...] = jnp.zeros_like(dgk_ref)

        gq, gk = g_ref[0:1, :], g_ref[1:2, :]

        def step(s, carry):
            jj, r = s // d, s % d
            here, edge = _attn_rows(jj, r, sub, d), _attn_rows(0, r, sub, d)
            before = _attn_rows(jnp.maximum(jj - 1, 0), r, sub, d)
            behind = _attn_rows(jnp.minimum(jj + 1, m - 1), r, sub, d)
            first, last = jj == 0, jj == m - 1
            block = jb * m + jj
            ma, mask_c, mask_p = _attn_masks(block > 0)
            mask_n = _attn_masks(block < nb - 1)[2]
            qhat, rq = _head_norm(qr_ref[here, :], gq, ma)
            qn = qhat * gq
            qn_next = _pick(last, qnn_ref[edge, :], qn_ref[behind, :])
            khat, rk = _head_norm(kr_ref[here, :], gk, ma)
            kcb = (khat * gk).astype(BF)
            kpb = _pick(first, knp_ref[edge, :], kn_ref[before, :]).astype(BF)
            vcb = vc_ref[here, :].astype(BF)
            vpb = _pick(first, vp_ref[edge, :], vc_ref[before, :]).astype(BF)
            do_t, don_t = do_ref[here, :], _pick(last, don_ref[edge, :], do_ref[behind, :])
            c_t, cn_t = c_ref[here, :], _pick(last, cn_ref[edge, :], c_ref[behind, :])
            lse_t, lsen_t = lse_ref[here, :], _pick(last, lsen_ref[edge, :], lse_ref[behind, :])
            qs, dos = _stack_heads(qn, ma).astype(BF), _stack_heads(do_t, ma).astype(BF)
            lse_s, c_s = _stack_cols(lse_t, ma), _stack_cols(c_t, ma)
            s_c = jnp.where(mask_c, _dot(qs, kcb, "nt") * scale, MASK_VALUE)
            s_p = jnp.where(mask_p, _dot(qs, kpb, "nt") * scale, MASK_VALUE)
            p_c = jnp.exp(s_c - lse_s)
            p_p = jnp.exp(s_p - lse_s)
            ds_c = ((p_c * (_dot(dos, vcb, "nt") + c_s)) * scale).astype(BF)
            ds_p = ((p_p * (_dot(dos, vpb, "nt") + c_s)) * scale).astype(BF)
            dq_t = _unstack_heads(_dot(ds_c, kcb) + _dot(ds_p, kpb), ma)
            qs_n, dos_n = _stack_heads(qn_next, ma).astype(BF), _stack_heads(don_t, ma).astype(BF)
            s_n = jnp.where(mask_n, _dot(qs_n, kcb, "nt") * scale, MASK_VALUE)
            p_n = jnp.exp(s_n - _stack_cols(lsen_t, ma))
            ds_n = ((p_n * (_dot(dos_n, vcb, "nt") + _stack_cols(cn_t, ma))) * scale).astype(BF)
            dv_t = _dot(p_c.astype(BF), dos, "tn") + _dot(p_n.astype(BF), dos_n, "tn")
            dk_t = _dot(ds_c, qs, "tn") + _dot(ds_n, qs_n, "tn")
            sq_ref[here, :] = _head_norm_bwd(dq_t, qhat, rq, gq, ma)
            sk_ref[here, :] = _head_norm_bwd(dk_t, khat, rk, gk, ma)
            sv_ref[here, :] = dv_t
            dgq_ref[...] += jnp.sum(dq_t * qhat, axis=0, keepdims=True)
            dgk_ref[...] += jnp.sum(dk_t * khat, axis=0, keepdims=True)
            return carry

        lax.fori_loop(0, m * d, step, 0, unroll=ATTN_UNROLL)
        dq_ref[...] = sq_ref[...].astype(BF)
        dk_ref[...] = sk_ref[...].astype(BF)
        dv_ref[...] = sv_ref[...].astype(BF)

    def cur(col0):
        return pl.BlockSpec((m * sub, LANES), lambda j, t: (j, col0 + 2 * g + t))

    def prv(col0):
        return pl.BlockSpec((sub, LANES), lambda j, t: (jnp.maximum(j * m - 1, 0), col0 + 2 * g + t))

    def nxt(col0):
        return pl.BlockSpec((sub, LANES), lambda j, t: (jnp.minimum((j + 1) * m, nb - 1), col0 + 2 * g + t))

    own = pl.BlockSpec((m * sub, LANES), lambda j, t: (j, t))
    own_next = pl.BlockSpec((sub, LANES), lambda j, t: (jnp.minimum((j + 1) * m, nb - 1), t))
    vec = pl.BlockSpec((1, LANES), lambda j, t: (0, 0))
    zq, zk, zv, k0 = OFF_Q // LANES, OFF_K // LANES, OFF_V // LANES, (OFF_K - OFF_Q) // LANES
    return pl.pallas_call(
        body, name=name, grid=(nb // m, 2),
        in_specs=[cur(zq), cur(zk), cur(zv), prv(zv), cur(0), nxt(0), cur(k0), prv(k0), own, own_next, own, own_next,
                  own, own_next, pl.BlockSpec((8, LANES), lambda j, t: (0, 0))] + [ANY] * len(extra),
        out_specs=[own, own, own, vec, vec],
        out_shape=[jax.ShapeDtypeStruct((T, 256), BF)] * 3 + [jax.ShapeDtypeStruct((1, LANES), F32)] * 2,
        scratch_shapes=[pltpu.VMEM((m * sub, LANES), F32)] * 3,
        compiler_params=_params(("arbitrary", "arbitrary")),
    )(z, z, z, z, qkn, qkn, qkn, qkn, do, do, c, c, lse, lse, gains, *extra)


MERGE_ROWS = 256
GATE_TILE = 256


def _group_mix(o_refs, lse_refs):
    lses = [r[...] for r in lse_refs]
    m = jnp.maximum(jnp.maximum(lses[0], lses[1]), lses[2])
    es = [jnp.exp(l - m) for l in lses]
    den = es[0] + es[1] + es[2]
    ws = [e / den for e in es]
    y = ws[0] * o_refs[0][...] + ws[1] * o_refs[1][...] + ws[2] * o_refs[2][...]
    return ws, y


def _sigmoid(v):
    return 1.0 / (1.0 + jnp.exp(-v))


def _merge_specs(T, z, bgate, gpu, gco, gau):
    tm = min(MERGE_ROWS, T)
    row = lambda w: pl.BlockSpec((tm, w), lambda i: (i, 0))
    gate0 = OFF_GATE // GATE_TILE
    gates = [pl.BlockSpec((tm, GATE_TILE), functools.partial(lambda i, cb: (i, cb), cb=gate0 + n))
             for n in range(3 * N_CHIPS)]
    full = lambda a: pl.BlockSpec(a.shape, lambda i: (0,) * a.ndim)
    specs = [row(512), row(512)] + [row(256)] * 6 + gates + [full(bgate), full(gpu), full(gco), full(gau)]
    return tm, row, specs


def _merge_fwd(yp, yc, o3, lse3, z, bgate, gpu, gco, gau, name):
    T = yp.shape[0]
    tm, row, specs = _merge_specs(T, z, bgate, gpu, gco, gau)

    def body(*refs):
        yp_ref, yc_ref = refs[0], refs[1]
        o_refs, lse_refs = refs[2:5], refs[5:8]
        zg = refs[8:20]
        b_ref, gpu_ref, gco_ref, gau_ref, out_ref = refs[20:25]
        yab = _group_mix(o_refs, lse_refs)[1].astype(BF)
        ys = (yp_ref[...], yc_ref[...], yab)
        ups = (gpu_ref, gco_ref, gau_ref)
        for n in range(N_CHIPS):
            acc = None
            for b in range(3):
                gcol = slice(1024 * b + GATE_TILE * n, 1024 * b + GATE_TILE * (n + 1))
                gate = _sigmoid(zg[N_CHIPS * b + n][...] + b_ref[:, gcol])
                term = gate * _dot(ys[b], ups[b][n])
                acc = term if acc is None else acc + term
            out_ref[:, GATE_TILE * n:GATE_TILE * (n + 1)] = acc.astype(BF)

    return pl.pallas_call(
        body, name=name, grid=(T // tm,), in_specs=specs, out_specs=row(1024),
        out_shape=jax.ShapeDtypeStruct((T, 1024), BF), compiler_params=_params(("parallel",)),
    )(yp, yc, *o3, *lse3, *([z] * 12), bgate, gpu, gco, gau)


def _merge_bwd(dm, yp, yc, o3, lse3, z, bgate, gpu, gco, gau, name):
    T = yp.shape[0]
    tm, row, specs = _merge_specs(T, z, bgate, gpu, gco, gau)
    nsteps = T // tm

    def body(*refs):
        dm_ref, yp_ref, yc_ref = refs[0:3]
        o_refs, lse_refs = refs[3:6], refs[6:9]
        zg = refs[9:21]
        b_ref, gpu_ref, gco_ref, gau_ref = refs[21:25]
        dzg_ref, dyp_ref, dyc_ref = refs[25:28]
        do_refs, c_refs = refs[28:31], refs[31:34]
        dgpu_ref, dgco_ref, dgau_ref, dbg_ref = refs[34:38]
        accs = refs[38:41]
        i = pl.program_id(0)

        @pl.when(i == 0)
        def _():
            for a in accs:
                a[...] = jnp.zeros_like(a)
            dbg_ref[...] = jnp.zeros_like(dbg_ref)

        ws, y = _group_mix(o_refs, lse_refs)
        ys = (yp_ref[...], yc_ref[...], y.astype(BF))
        ups = (gpu_ref, gco_ref, gau_ref)
        dys = [None, None, None]
        for n in range(N_CHIPS):
            dmn = dm_ref[:, GATE_TILE * n:GATE_TILE * (n + 1)]
            for b in range(3):
                gcol = slice(1024 * b + GATE_TILE * n, 1024 * b + GATE_TILE * (n + 1))
                gate = _sigmoid(zg[N_CHIPS * b + n][...] + b_ref[:, gcol])
                up = _dot(ys[b], ups[b][n])
                dzg = (dmn * up) * (gate * (1.0 - gate))
                dzg_ref[:, gcol] = dzg.astype(BF)
                dbg_ref[:, gcol] += jnp.sum(dzg, axis=0, keepdims=True)
                dup = (dmn * gate).astype(BF)
                accs[b][n] += _dot(ys[b], dup, "tn")
                dyb = _dot(dup, ups[b][n], "nt")
                dys[b] = dyb if dys[b] is None else dys[b] + dyb
        dyp_ref[...] = dys[0]
        dyc_ref[...] = dys[1]
        dya = dys[2]
        lane = lax.broadcasted_iota(jnp.int32, dya.shape, 1) // HEAD_DIM
        pr = dya * y
        rho = jnp.zeros_like(pr)
        for h in range(256 // HEAD_DIM):
            hm = lane == h
            rho = jnp.where(hm, jnp.sum(jnp.where(hm, pr, 0.0), axis=-1, keepdims=True), rho)
        for g in range(3):
            do_refs[g][...] = ws[g] * dya
            c_refs[g][...] = -(ws[g] * rho)

        @pl.when(i == nsteps - 1)
        def _():
            dgpu_ref[...] = accs[0][...].astype(BF)
            dgco_ref[...] = accs[1][...].astype(BF)
            dgau_ref[...] = accs[2][...].astype(BF)

    full = lambda a: pl.BlockSpec(a.shape, lambda i: (0,) * a.ndim)
    out_specs = ([row(3072), row(512), row(512)] + [row(256)] * 6 + [full(gpu), full(gco), full(gau)]
                 + [pl.BlockSpec((1, 3072), lambda i: (0, 0))])
    out_shape = ([jax.ShapeDtypeStruct((T, 3072), BF)] + [jax.ShapeDtypeStruct((T, 512), F32)] * 2
                 + [jax.ShapeDtypeStruct((T, 256), F32)] * 6
                 + [jax.ShapeDtypeStruct(g.shape, BF) for g in (gpu, gco, gau)]
                 + [jax.ShapeDtypeStruct((1, 3072), F32)])
    return pl.pallas_call(
        body, name=name, grid=(nsteps,), in_specs=[row(1024)] + specs, out_specs=out_specs, out_shape=out_shape,
        scratch_shapes=[pltpu.VMEM(g.shape, F32) for g in (gpu, gco, gau)],
        compiler_params=_params(("arbitrary",)),
    )(dm, yp, yc, *o3, *lse3, *([z] * 12), bgate, gpu, gco, gau)


def _layer_fwd(x, w, tag, after=None, late=None):
    hb = _rms_fwd(x, w["norm_mix"], f"rms_mix_{tag}", after=after)
    z = _mm(hb, w["w_in"], "nn", f"in_proj_{tag}", tm=512, tn=3712, tk=1024, n_outer=True)
    yp, yc = _poolconv_fwd(z, w["pool_mix"], w["pool_scale"], w["conv_w"], f"poolconv_{tag}")
    qkn = _qk_norm(z, w["qk_gain"], f"qk_norm_{tag}")
    o3, lse3 = [], []
    for g, d in enumerate(ATTN_DILATIONS):
        o, lse = _attn_fwd(z, qkn, g, d, f"attn{g}_{tag}")
        o3.append(o)
        lse3.append(lse)
    if late is not None:
        w = dict(w, **late(lse3[-1]))
    merged = _merge_fwd(yp, yc, o3, lse3, z, w["b_gate"], w["w_pool_up"], w["w_conv_out"], w["w_attn_up"],
                        f"merge_{tag}")
    x1 = _mm(merged, w["w_o"], "nn", f"out_proj_{tag}", tm=1024, tn=1024, tk=1024, res=x)
    h2b = _rms_fwd(x1, w["norm_mlp"], f"rms_mlp_{tag}")
    rb = _mm(h2b, w["w_ff1"], "nn", f"ff1_{tag}", tm=1024, tn=1024, tk=1024, out_dtype=BF, epi="relu2", n_outer=True,
             b_shards=True)
    x2 = _mm(rb, w["w_ff2"], "nn", f"ff2_{tag}", tm=1024, tn=1024, tk=1024, res=x1)
    saved = dict(x=x, hb=hb, z=z, yp=yp, yc=yc, qkn=qkn, o3=o3, lse3=lse3, merged=merged, x1=x1, h2b=h2b, rb=rb)
    return x2, saved, w


def _layer_bwd(dx2, w, s, tag, after=None, mid=None):
    g = {}
    dab = _mm(dx2, w["w_ff2"], "nt", f"d_ff2_act_{tag}", tm=1024, tn=1024, tk=1024, out_dtype=BF, aux=s["rb"],
              epi="drelu2", after=after)
    g["w_ff2"] = _mm(s["rb"], dx2, "tn", f"d_ff2_w_{tag}", tm=1024, tn=1024, tk=1024, out_dtype=BF)
    g["w_ff1"] = _mm(s["h2b"], dab, "tn", f"d_ff1_w_{tag}", tm=1024, tn=1024, tk=1024, out_dtype=BF, out_shards=True)
    dh2 = _mm(dab, w["w_ff1"], "nt", f"d_ff1_act_{tag}", tm=1024, tn=1024, tk=1024, b_shards=True)
    dx1, g["norm_mlp"] = _rms_bwd(dh2, s["x1"], w["norm_mlp"], dx2, f"d_rms_mlp_{tag}")
    dm = _mm(dx1, w["w_o"], "nt", f"d_out_act_{tag}", tm=1024, tn=1024, tk=1024)
    g["w_o"] = _mm(s["merged"], dx1, "tn", f"d_out_w_{tag}", tm=1024, tn=1024, tk=1024, out_dtype=BF)
    (dzg, dyp, dyc, do0, do1, do2, c0, c1, c2, g["w_pool_up"], g["w_conv_out"], g["w_attn_up"],
     g["b_gate"]) = _merge_bwd(dm, s["yp"], s["yc"], s["o3"], s["lse3"], s["z"], w["b_gate"], w["w_pool_up"],
                               w["w_conv_out"], w["w_attn_up"], f"d_merge_{tag}")
    behind = mid(g) if mid is not None else None
    dq, dk, dv = [], [], []
    dgq = dgk = None
    for gi, d in enumerate(ATTN_DILATIONS):
        dzq, dzk, dzv, pq, pk = _attn_bwd(s["z"], s["qkn"], (do0, do1, do2)[gi], (c0, c1, c2)[gi], s["lse3"][gi],
                                          w["qk_gain"], gi, d, f"d_attn{gi}_{tag}", after=behind)
        dq.append(dzq)
        dk.append(dzk)
        dv.append(dzv)
        dgq = pq if dgq is None else dgq + pq
        dgk = pk if dgk is None else dgk + pk
    g["q_gain"] = dgq[:, :HEAD_DIM] + dgq[:, HEAD_DIM:]
    g["k_gain"] = dgk[:, :HEAD_DIM] + dgk[:, HEAD_DIM:]
    dzpc, g["pool_mix"], g["pool_scale"], g["conv_w"] = _poolconv_bwd(
        s["z"], dyp, dyc, w["pool_mix"], w["pool_scale"], w["conv_w"], f"d_poolconv_{tag}")
    dz = jnp.concatenate([dzpc] + dq + dk + dv + [dzg], axis=1)
    g["w_in"] = _mm(s["hb"], dz, "tn", f"d_in_w_{tag}", tm=512, tn=3712, tk=512, out_dtype=BF)
    dh = _mm(dz, w["w_in"], "nt", f"d_in_act_{tag}", tm=512, tn=1024, tk=3712)
    dx, g["norm_mix"] = _rms_bwd(dh, s["x"], w["norm_mix"], dx1, f"d_rms_mix_{tag}")
    return dx, g


def _position():
    x, y, c = lax.axis_index("x"), lax.axis_index("y"), lax.axis_index("c")
    chips = [(1 - x, y), (x, 1 - y), (1 - x, 1 - y)]
    return x, y, c, 2 * x + y, chips, [2 * cx + cy for cx, cy in chips]


def _remote(src, dst, ssem, rsem, dev):
    return pltpu.make_async_remote_copy(src_ref=src, dst_ref=dst, send_sem=ssem, recv_sem=rsem, device_id=dev,
                                        device_id_type=MESH_ID)


def _position_operand():
    x, y, c = lax.axis_index("x"), lax.axis_index("y"), lax.axis_index("c")
    return jnp.stack([2 * x + y, c]).astype(jnp.int32)


def _halves(a):
    return a.reshape(a.shape[0], 2, a.shape[1] // 2, a.shape[2])


def _gather(bufs, name):
    n = len(bufs)
    views = [_halves(b) for b in bufs]

    def body(*refs):
        outs = refs[n:2 * n]
        ssem, rsem, fssem, frsem = refs[2 * n:]
        x, y, c, q, chips, qs = _position()
        sib = (x, y, 1 - c)
        sent = []
        for k in range(n):
            mine = outs[k].at[q, c]
            for j, chip in enumerate(chips):
                cp = _remote(mine, mine, ssem.at[k, j], rsem.at[k, j], (chip[0], chip[1], c))
                cp.start()
                sent.append(cp)
        for k in range(n):
            for j, chip in enumerate(chips):
                slot = outs[k].at[qs[j], c]
                _remote(slot, slot, ssem.at[k, j], rsem.at[k, j], (chip[0], chip[1], c)).wait_recv()
                cp = _remote(slot, slot, fssem.at[k, j], frsem.at[k, j], sib)
                cp.start()
                sent.append(cp)
        for k in range(n):
            for j in range(3):
                slot = outs[k].at[qs[j], 1 - c]
                _remote(slot, slot, fssem.at[k, j], frsem.at[k, j], sib).wait_recv()
        for cp in sent:
            cp.wait_send()

    outs = pl.pallas_call(
        body, name=name, in_specs=[ANY] * n, out_specs=[ANY] * n,
        out_shape=[jax.ShapeDtypeStruct(v.shape, v.dtype) for v in views],
        input_output_aliases={k: k for k in range(n)},
        scratch_shapes=[pltpu.SemaphoreType.DMA((n, 3))] * 4,
    )(*views)
    return [o.reshape(b.shape) for o, b in zip(outs, bufs)]


SEM = pl.BlockSpec(memory_space=pltpu.SEMAPHORE)
TOKEN = jax.ShapeDtypeStruct((8, LANES), F32)
TOKEN_SPEC = pl.BlockSpec(memory_space=pltpu.VMEM)


def _split_params():
    return pltpu.CompilerParams(has_side_effects=pltpu.SideEffectType.DATAFLOW_SIDE_EFFECTING)


def _gather_start(bufs, name):
    n = len(bufs)
    views = [_halves(b) for b in bufs]

    def body(*refs):
        ssem, rsem = refs[n:n + ns], refs[n + ns:n + 2 * ns]
        outs, token = refs[n + 2 * ns:2 * n + 2 * ns], refs[2 * n + 2 * ns]
        x, y, c, q, chips, qs = _position()
        for k in range(n):
            mine = outs[k].at[q, c]
            for j, chip in enumerate(chips):
                _remote(mine, mine, ssem[3 * k + j], rsem[3 * k + j], (chip[0], chip[1], c)).start()
        token[...] = jnp.zeros_like(token)

    ns = 3 * n
    outs = pl.pallas_call(
        body, name=name, in_specs=[ANY] * n, out_specs=[SEM] * (2 * ns) + [ANY] * n + [TOKEN_SPEC],
        out_shape=[pltpu.SemaphoreType.DMA(())] * (2 * ns) + [jax.ShapeDtypeStruct(v.shape, v.dtype) for v in views]
        + [TOKEN],
        input_output_aliases={k: k + 2 * ns for k in range(n)}, compiler_params=_split_params(),
    )(*views)
    return list(outs[:ns]), list(outs[ns:2 * ns]), list(outs[2 * ns:2 * ns + n]), outs[2 * ns + n]


def _gather_finish(ssem, rsem, views, after, name_wait, name_forward, shapes):
    n = len(views)
    ns = len(ssem)

    def wait_body(*refs):
        ssem_ref, rsem_ref = refs[n:n + ns], refs[n + ns:n + 2 * ns]
        outs = refs[n + 2 * ns + 1:]
        x, y, c, q, chips, qs = _position()
        for k in range(n):
            for j, chip in enumerate(chips):
                cp = _remote(outs[k].at[q, c], outs[k].at[qs[j], c], ssem_ref[3 * k + j], rsem_ref[3 * k + j],
                             (chip[0], chip[1], c))
                cp.wait_send()
                cp.wait_recv()

    landed = pl.pallas_call(
        wait_body, name=name_wait, in_specs=[ANY] * n + [SEM] * (2 * ns) + [ANY], out_specs=[ANY] * n,
        out_shape=[jax.ShapeDtypeStruct(v.shape, v.dtype) for v in views],
        input_output_aliases={k: k for k in range(n)}, compiler_params=_split_params(),
    )(*views, *ssem, *rsem, after)

    def forward_body(*refs):
        outs = refs[n:2 * n]
        fssem, frsem = refs[2 * n:]
        x, y, c, q, chips, qs = _position()
        sib = (x, y, 1 - c)
        sent = []
        for k in range(n):
            for j in range(3):
                slot = outs[k].at[qs[j], c]
                cp = _remote(slot, slot, fssem.at[k, j], frsem.at[k, j], sib)
                cp.start()
                sent.append(cp)
        for k in range(n):
            for j in range(3):
                slot = outs[k].at[qs[j], 1 - c]
                _remote(slot, slot, fssem.at[k, j], frsem.at[k, j], sib).wait_recv()
        for cp in sent:
            cp.wait_send()

    outs = pl.pallas_call(
        forward_body, name=name_forward, in_specs=[ANY] * n, out_specs=[ANY] * n,
        out_shape=[jax.ShapeDtypeStruct(v.shape, v.dtype) for v in views],
        input_output_aliases={k: k for k in range(n)}, scratch_shapes=[pltpu.SemaphoreType.DMA((n, 3))] * 2,
    )(*landed)
    return [o.reshape(s) for o, s in zip(outs, shapes)]


def _chip_exchange_start(parts, name):
    n = len(parts)

    def body(*refs):
        ssem, rsem = refs[n:n + ns], refs[n + ns:n + 2 * ns]
        base = n + 2 * ns
        srcs, outs, token = refs[base:base + n], refs[base + n:base + 2 * n], refs[base + 2 * n]
        x, y, c, q, chips, qs = _position()
        for k in range(n):
            for j, chip in enumerate(chips):
                _remote(srcs[k].at[qs[j]], outs[k].at[j], ssem[3 * k + j], rsem[3 * k + j],
                        (chip[0], chip[1], c)).start()
        token[...] = jnp.zeros_like(token)

    ns = 3 * n
    outs = pl.pallas_call(
        body, name=name, in_specs=[ANY] * n, out_specs=[SEM] * (2 * ns) + [ANY] * (2 * n) + [TOKEN_SPEC],
        out_shape=[pltpu.SemaphoreType.DMA(())] * (2 * ns) + [jax.ShapeDtypeStruct(a.shape, a.dtype) for a in parts]
        + [jax.ShapeDtypeStruct((3,) + a.shape[1:], a.dtype) for a in parts] + [TOKEN],
        input_output_aliases={k: k + 2 * ns for k in range(n)}, compiler_params=_split_params(),
    )(*parts)
    b = 2 * ns
    return list(outs[:ns]), list(outs[ns:b]), list(outs[b:b + n]), list(outs[b + n:b + 2 * n]), outs[b + 2 * n]


def _chip_exchange_wait(ssem, rsem, parts, landing, after, name):
    n = len(parts)
    ns = len(ssem)

    def body(*refs):
        ssem_ref, rsem_ref = refs[2 * n:2 * n + ns], refs[2 * n + ns:2 * n + 2 * ns]
        base = 2 * n + 2 * ns + 1
        srcs, outs = refs[base:base + n], refs[base + n:]
        x, y, c, q, chips, qs = _position()
        for k in range(n):
            for j, chip in enumerate(chips):
                cp = _remote(srcs[k].at[qs[j]], outs[k].at[j], ssem_ref[3 * k + j], rsem_ref[3 * k + j],
                             (chip[0], chip[1], c))
                cp.wait_send()
                cp.wait_recv()

    outs = pl.pallas_call(
        body, name=name, in_specs=[ANY] * (2 * n) + [SEM] * (2 * ns) + [ANY], out_specs=[ANY] * (2 * n),
        out_shape=[jax.ShapeDtypeStruct(a.shape, a.dtype) for a in list(parts) + list(landing)],
        input_output_aliases={k: k for k in range(2 * n)}, compiler_params=_split_params(),
    )(*parts, *landing, *ssem, *rsem, after)
    return list(outs[:n]), list(outs[n:])


def _pair_swap(views, name):
    n = len(views)

    def body(*refs):
        ins, outs = refs[:n], refs[n:2 * n]
        ssem, rsem = refs[2 * n:]
        x, y, c, _, _, _ = _position()
        cps = [_remote(ins[k].at[pl.ds(0, N_CHIPS), 1 - c], outs[k], ssem.at[k], rsem.at[k], (x, y, 1 - c))
               for k in range(n)]
        for cp in cps:
            cp.start()
        for cp in cps:
            cp.wait()

    return pl.pallas_call(
        body, name=name, in_specs=[ANY] * n, out_specs=[ANY] * n,
        out_shape=[jax.ShapeDtypeStruct((v.shape[0],) + v.shape[2:], v.dtype) for v in views],
        scratch_shapes=[pltpu.SemaphoreType.DMA((n,))] * 2,
    )(*views)


def _chip_exchange(parts, name):
    n = len(parts)

    def body(*refs):
        ins, outs = refs[:n], refs[n:2 * n]
        ssem, rsem = refs[2 * n:]
        x, y, c, q, chips, qs = _position()
        cps = []
        for k in range(n):
            for j, chip in enumerate(chips):
                cp = _remote(ins[k].at[qs[j]], outs[k].at[j], ssem.at[k, j], rsem.at[k, j], (chip[0], chip[1], c))
                cp.start()
                cps.append(cp)
        for cp in cps:
            cp.wait_recv()
        for cp in cps:
            cp.wait_send()

    return pl.pallas_call(
        body, name=name, in_specs=[ANY] * n, out_specs=[ANY] * n,
        out_shape=[jax.ShapeDtypeStruct((3,) + a.shape[1:], a.dtype) for a in parts],
        scratch_shapes=[pltpu.SemaphoreType.DMA((n, 3))] * 2,
    )(*parts)


def _pair_send(arrays, name):
    n = len(arrays)

    def body(*refs):
        ins, outs = refs[:n], refs[n:2 * n]
        ssem, rsem = refs[2 * n:]
        x, y, c, _, _, _ = _position()
        cps = [_remote(ins[k], outs[k], ssem.at[k], rsem.at[k], (x, y, 1 - c)) for k in range(n)]
        for cp in cps:
            cp.start()
        for cp in cps:
            cp.wait()

    return pl.pallas_call(
        body, name=name, in_specs=[ANY] * n, out_specs=[ANY] * n,
        out_shape=[jax.ShapeDtypeStruct(a.shape, a.dtype) for a in arrays],
        scratch_shapes=[pltpu.SemaphoreType.DMA((n,))] * 2,
    )(*arrays)


def _all_to_all_small(part):
    P = part.shape[0]

    def body(in_ref, out_ref, lsem, ssem, rsem):
        x, y, c = lax.axis_index("x"), lax.axis_index("y"), lax.axis_index("c")
        me = 4 * x + 2 * y + c
        flips = [(fx, fy, fc) for fx in (0, 1) for fy in (0, 1) for fc in (0, 1)][1:]
        peers = [((x + fx) % 2, (y + fy) % 2, (c + fc) % 2) for fx, fy, fc in flips]
        loc = pltpu.make_async_copy(in_ref, out_ref.at[me], lsem)
        loc.start()
        cps = [_remote(in_ref, out_ref.at[me], ssem.at[j], rsem.at[j], peer) for j, peer in enumerate(peers)]
        for cp in cps:
            cp.start()
        for j, (px, py, pc) in enumerate(peers):
            _remote(in_ref, out_ref.at[4 * px + 2 * py + pc], ssem.at[j], rsem.at[j], peers[j]).wait_recv()
        for cp in cps:
            cp.wait_send()
        loc.wait()

    return pl.pallas_call(
        body, name="small_grad_exchange", in_specs=[ANY], out_specs=ANY,
        out_shape=jax.ShapeDtypeStruct((8, P, LANES), F32),
        scratch_shapes=[pltpu.SemaphoreType.DMA(())] + [pltpu.SemaphoreType.DMA((7,))] * 2,
    )(part)


def _row_tile(rows, width, n_arrays):
    t = rows
    while t % 2 == 0 and t > 8 and 2 * n_arrays * t * width * 4 > VMEM_LIMIT // 2:
        t //= 2
    return t


def _scalar_grid(grid, in_specs, out_specs):
    return pltpu.PrefetchScalarGridSpec(num_scalar_prefetch=1, grid=grid, in_specs=in_specs, out_specs=out_specs)


def _cast_place(w3, layer, pos, name):
    _, r, c = w3.shape
    tr = _row_tile(r, c, 2)

    def body(pos_ref, w_ref, o_ref):
        o_ref[...] = w_ref[...].astype(BF)

    return pl.pallas_call(
        body, name=name,
        grid_spec=_scalar_grid((r // tr,), [pl.BlockSpec((None, tr, c), lambda i, pos: (layer, i, 0))],
                               pl.BlockSpec((None, tr, c), lambda i, pos: (pos[0], i, 0))),
        out_shape=jax.ShapeDtypeStruct((N_CHIPS, r, c), BF), compiler_params=_params(("parallel",)),
    )(pos, w3)


def _pair_sum(view, recv, pos, name):
    _, _, hr, c = view.shape
    tr = _row_tile(hr, c, 3)

    def body(pos_ref, g_ref, r_ref, o_ref):
        o_ref[...] = (g_ref[...].astype(F32) + r_ref[...].astype(F32)).astype(BF)

    blk = pl.BlockSpec((None, tr, c), lambda p, i, pos: (p, i, 0))
    return pl.pallas_call(
        body, name=name,
        grid_spec=_scalar_grid((N_CHIPS, hr // tr),
                               [pl.BlockSpec((None, None, tr, c), lambda p, i, pos: (p, pos[1], i, 0)), blk], blk),
        out_shape=jax.ShapeDtypeStruct(recv.shape, BF), compiler_params=_params(("parallel", "parallel")),
    )(pos, view, recv)


def _chip_sum(parts, recv, pos, name):
    _, hr, c = parts.shape
    tr = _row_tile(hr, c, 6)

    def body(pos_ref, p_ref, r_ref, o_ref):
        acc = p_ref[...].astype(F32)
        for j in range(3):
            acc = acc + r_ref[j].astype(F32)
        o_ref[...] = acc

    return pl.pallas_call(
        body, name=name,
        grid_spec=_scalar_grid((hr // tr,),
                               [pl.BlockSpec((None, tr, c), lambda i, pos: (pos[0], i, 0)),
                                pl.BlockSpec((3, tr, c), lambda i, pos: (0, i, 0))],
                               pl.BlockSpec((tr, c), lambda i, pos: (i, 0))),
        out_shape=jax.ShapeDtypeStruct((hr, c), F32), compiler_params=_params(("parallel",)),
    )(pos, parts, recv)


def _sum_slices(a, name):
    n, rows, width = a.shape
    tr = _row_tile(rows, width, n + 1)

    def body(a_ref, o_ref):
        acc = a_ref[0].astype(F32)
        for i in range(1, n):
            acc = acc + a_ref[i].astype(F32)
        o_ref[...] = acc

    return pl.pallas_call(
        body, name=name, grid=(rows // tr,), in_specs=[pl.BlockSpec((n, tr, width), lambda i: (0, i, 0))],
        out_specs=pl.BlockSpec((tr, width), lambda i: (i, 0)), out_shape=jax.ShapeDtypeStruct((rows, width), F32),
        compiler_params=_params(("parallel",)),
    )(a)


def _adamw_update(w, g, m, v):
    nm = ADAM_B1 * m + (1.0 - ADAM_B1) * g
    nv = ADAM_B2 * v + (1.0 - ADAM_B2) * (g * g)
    m_hat = nm / (1.0 - ADAM_B1 ** ADAM_STEP)
    v_hat = nv / (1.0 - ADAM_B2 ** ADAM_STEP)
    return -ADAM_LR * (m_hat / (jnp.sqrt(v_hat) + ADAM_EPS) + ADAM_WD * w), nm, nv


def _adamw(w, g, m, v, name):
    rows, width = w.shape
    tr = _row_tile(rows, width, 7)

    def body(w_ref, g_ref, m_ref, v_ref, d_ref, nm_ref, nv_ref):
        d_ref[...], nm_ref[...], nv_ref[...] = _adamw_update(w_ref[...], g_ref[...], m_ref[...], v_ref[...])

    blk = pl.BlockSpec((tr, width), lambda i: (i, 0))
    return pl.pallas_call(
        body, name=name, grid=(rows // tr,), in_specs=[blk] * 4, out_specs=[blk] * 3,
        out_shape=[jax.ShapeDtypeStruct((rows, width), F32)] * 3, compiler_params=_params(("parallel",)),
    )(w, g, m, v)


def _adamw_halves(w3, m3, v3, mine, other, pos, name):
    depth, r, c = w3.shape
    assert depth == 2
    hr = r // 2
    tr = _row_tile(hr, c, 11)
    sources = ((0, True, mine[0]), (0, False, other[0]), (1, True, mine[1]), (1, False, other[1]))

    def active(l, h, core, layer, own):
        mine_half = h == core
        return (l == layer) & (mine_half if own else jnp.logical_not(mine_half))

    def body(pos_ref, w_ref, m_ref, v_ref, *rest):
        g_refs, (go_ref, d_ref, nm_ref, nv_ref) = rest[:4], rest[4:]
        l, h = pl.program_id(0), pl.program_id(1)
        for (layer, own, _), g_ref in zip(sources, g_refs):
            @pl.when(active(l, h, pos_ref[1], layer, own))
            def _():
                gv = g_ref[...]
                go_ref[...] = gv
                d_ref[...], nm_ref[...], nv_ref[...] = _adamw_update(w_ref[...], gv, m_ref[...], v_ref[...])

    def gspec(layer, own):
        return pl.BlockSpec((tr, c), lambda l, h, i, pos: (jnp.where(active(l, h, pos[1], layer, own), i, 0), 0))

    blk = pl.BlockSpec((None, None, tr, c), lambda l, h, i, pos: (l, h, i, 0))
    view = lambda a: a.reshape(depth, 2, hr, c)
    outs = pl.pallas_call(
        body, name=name,
        grid_spec=_scalar_grid((depth, 2, hr // tr), [blk] * 3 + [gspec(layer, own) for layer, own, _ in sources],
                               [blk] * 4),
        out_shape=[jax.ShapeDtypeStruct((depth, 2, hr, c), F32)] * 4,
        compiler_params=_params(("parallel", "parallel", "parallel")),
    )(pos, view(w3), view(m3), view(v3), *[s[2] for s in sources])
    return [o.reshape(w3.shape) for o in outs]


BIG = ("w_in", "w_pool_up", "w_conv_out", "w_attn_up", "w_o", "w_ff1", "w_ff2")
SMALL = ("norm_mix", "b_gate", "pool_mix", "pool_scale", "conv_w", "q_gain", "k_gain", "norm_mlp")
ORDER = ("norm_mix", "w_in", "b_gate", "pool_mix", "pool_scale", "conv_w", "q_gain", "k_gain", "w_pool_up",
         "w_conv_out", "w_attn_up", "w_o", "norm_mlp", "w_ff1", "w_ff2")
COLUMN_SHARDED = ("w_in", "w_pool_up", "w_conv_out", "w_attn_up", "w_ff1")


def _matrix_weights(gathered):
    w = {}
    for name, g4 in gathered.items():
        if name == "w_in":
            w[name] = jnp.transpose(g4, (1, 0, 2)).reshape(g4.shape[1], N_CHIPS * g4.shape[2])
        elif name in COLUMN_SHARDED:
            w[name] = g4
        else:
            w[name] = g4.reshape(N_CHIPS * g4.shape[1], g4.shape[2])
    return w


def _small_weights(l, small):
    w = {}
    w["norm_mix"] = small["norm_mix"][l][None]
    w["norm_mlp"] = small["norm_mlp"][l][None]
    w["b_gate"] = small["b_gate"][l][None]
    w["pool_mix"] = small["pool_mix"][l].astype(BF)
    w["pool_scale"] = small["pool_scale"][l][None]
    w["conv_w"] = jnp.pad(small["conv_w_full"][l], ((0, 5), (0, 0)))
    w["qk_gain"] = jnp.pad(jnp.stack([jnp.tile(small["q_gain"][l], 2), jnp.tile(small["k_gain"][l], 2)]), ((0, 6), (0, 0)))
    return w


def _to_chip_major(name, g):
    if name == "w_in":
        return jnp.transpose(g.reshape(g.shape[0], N_CHIPS, g.shape[1] // N_CHIPS), (1, 0, 2))
    if name in COLUMN_SHARDED:
        return g
    return g.reshape(N_CHIPS, g.shape[0] // N_CHIPS, g.shape[1])


def _pad8(a):
    a = a.reshape(-1, LANES)
    return jnp.pad(a, ((0, (-a.shape[0]) % 8), (0, 0)))


def kernel(x, norm_mix, w_in, b_gate, pool_mix, pool_scale, conv_w, q_gain, k_gain, w_pool_up, w_conv_out, w_attn_up, w_o, norm_mlp, w_ff1, w_ff2, loss_target, m_norm_mix, m_w_in, m_b_gate, m_pool_mix, m_pool_scale, m_conv_w, m_q_gain, m_k_gain, m_w_pool_up, m_w_conv_out, m_w_attn_up, m_w_o, m_norm_mlp, m_w_ff1, m_w_ff2, v_norm_mix, v_w_in, v_b_gate, v_pool_mix, v_pool_scale, v_conv_w, v_q_gain, v_k_gain, v_w_pool_up, v_w_conv_out, v_w_attn_up, v_w_o, v_norm_mlp, v_w_ff1, v_w_ff2):
    weights = dict(norm_mix=norm_mix, w_in=w_in, b_gate=b_gate, pool_mix=pool_mix, pool_scale=pool_scale, conv_w=conv_w,
                   q_gain=q_gain, k_gain=k_gain, w_pool_up=w_pool_up, w_conv_out=w_conv_out, w_attn_up=w_attn_up,
                   w_o=w_o, norm_mlp=norm_mlp, w_ff1=w_ff1, w_ff2=w_ff2)
    moms = dict(norm_mix=m_norm_mix, w_in=m_w_in, b_gate=m_b_gate, pool_mix=m_pool_mix, pool_scale=m_pool_scale,
                conv_w=m_conv_w, q_gain=m_q_gain, k_gain=m_k_gain, w_pool_up=m_w_pool_up, w_conv_out=m_w_conv_out,
                w_attn_up=m_w_attn_up, w_o=m_w_o, norm_mlp=m_norm_mlp, w_ff1=m_w_ff1, w_ff2=m_w_ff2)
    vels = dict(norm_mix=v_norm_mix, w_in=v_w_in, b_gate=v_b_gate, pool_mix=v_pool_mix, pool_scale=v_pool_scale,
                conv_w=v_conv_w, q_gain=v_q_gain, k_gain=v_k_gain, w_pool_up=v_w_pool_up, w_conv_out=v_w_conv_out,
                w_attn_up=v_w_attn_up, w_o=v_w_o, norm_mlp=v_norm_mlp, w_ff1=v_w_ff1, w_ff2=v_w_ff2)
    depth = norm_mix.shape[0]
    q = 2 * lax.axis_index("x") + lax.axis_index("y")
    pos = _position_operand()

    assert depth == 2, "the second layer's gather hides behind the first layer's forward, and likewise backward"
    first, rest = BIG[:1], BIG[1:]
    bufs = [{n: _cast_place(weights[n], l, pos, f"cast_{n}_l{l}") for n in BIG} for l in range(depth)]
    w_first = _matrix_weights(dict(zip(first, _gather([bufs[0][n] for n in first], "gather_l0_in"))))
    b_ssem, b_rsem, b_views, b_token = _gather_start([bufs[0][n] for n in rest], "gather_start_l0_rest")
    g_ssem, g_rsem, g_views, g_token = _gather_start([bufs[1][n] for n in BIG], "gather_start_l1")
    cw_all = _all_to_all_small(_pad8(jnp.pad(conv_w.reshape(-1), (0, (-conv_w.size) % LANES))))
    conv_w_full = jnp.concatenate(
        [cw_all[2 * p].reshape(-1)[:conv_w.size].reshape(conv_w.shape) for p in range(N_CHIPS)], axis=-1)
    small = dict(weights)
    small["conv_w_full"] = conv_w_full

    def late_weights(t):
        got = _gather_finish(b_ssem, b_rsem, b_views, t, "gather_wait_l0_rest", "gather_forward_l0_rest",
                             [bufs[0][n].shape for n in rest])
        return _matrix_weights(dict(zip(rest, got)))

    wl, saved = [None] * depth, [None] * depth
    h, saved[0], wl[0] = _layer_fwd(x[0], dict(_small_weights(0, small), **w_first), "l0", after=[b_token, g_token],
                                    late=late_weights)
    got = _gather_finish(g_ssem, g_rsem, g_views, h, "gather_wait_l1", "gather_forward_l1",
                         [bufs[1][n].shape for n in BIG])
    h, saved[1], wl[1] = _layer_fwd(h, dict(_small_weights(1, small), **_matrix_weights(dict(zip(BIG, got)))), "l1")
    dh, loss_row = _loss_grad(h, loss_target[0], "loss")

    def pair_stage(names, g, tag):
        views = [_halves(_to_chip_major(n, g[n])) for n in names]
        from_sibling = _pair_swap(views, f"grad_pair_swap_{tag}")
        return [_pair_sum(views[k], from_sibling[k], pos, f"pair_sum_{n}_{tag}") for k, n in enumerate(names)]

    mine, other = [{}, {}], [{}, {}]

    def finish(names, l, started, after, tag):
        ssem, rsem, parts, landing, _ = started
        parts, arrived = _chip_exchange_wait(ssem, rsem, parts, landing, after, f"grad_chip_exchange_wait_{tag}")
        got = [_chip_sum(parts[k], arrived[k], pos, f"chip_sum_{n}_{tag}") for k, n in enumerate(names)]
        mine[l].update(zip(names, got))
        other[l].update(zip(names, _pair_send(got, f"grad_pair_send_{tag}")))

    grads, early = [None] * depth, {}
    dh, grads[1] = _layer_bwd(dh, wl[1], saved[1], "l1")
    second = _chip_exchange_start(pair_stage(BIG, grads[1], "l1"), "grad_chip_exchange_start_l1")

    def start_rest(g):
        early["rest"] = _chip_exchange_start(pair_stage(rest, g, "l0_rest"), "grad_chip_exchange_start_l0_rest")
        return early["rest"][4]

    dh, grads[0] = _layer_bwd(dh, wl[0], saved[0], "l0", after=second[4], mid=start_rest)
    last = _chip_exchange_start(pair_stage(first, grads[0], "l0_in"), "grad_chip_exchange_start_l0_in")
    finish(BIG, 1, second, last[4], "l1")
    finish(rest, 0, early["rest"], last[4], "l0_rest")
    loss = lax.psum(loss_row[0, 0], ("x", "y", "c"))
    full = {}

    pieces = []
    for n in SMALL:
        per_layer = [grads[l][n] for l in range(depth)]
        if n == "conv_w":
            per_layer = [p[:3] for p in per_layer]
        pieces.append(_pad8(jnp.stack(per_layer).reshape(-1)))
    packed = jnp.concatenate(pieces, axis=0)
    summed = _sum_slices(_all_to_all_small(packed), "small_sum")
    row = 0
    for n, piece in zip(SMALL, pieces):
        size = weights[n].size if n != "conv_w" else depth * 3 * 512
        flat = summed[row:row + piece.shape[0]].reshape(-1)[:size]
        row += piece.shape[0]
        if n == "conv_w":
            full[n] = lax.dynamic_slice_in_dim(flat.reshape(depth, 3, 512), q * conv_w.shape[2], conv_w.shape[2], axis=2)
        else:
            full[n] = flat.reshape(weights[n].shape)

    deltas, new_m, new_v = {}, {}, {}

    def update_matrix(n):
        full[n], deltas[n], new_m[n], new_v[n] = _adamw_halves(
            weights[n], moms[n], vels[n], [mine[l][n] for l in range(depth)], [other[l][n] for l in range(depth)], pos,
            f"adamw_{n}")

    for n in rest:
        update_matrix(n)
    finish(first, 0, last, deltas[rest[-1]], "l0_in")
    for n in first:
        update_matrix(n)
    for n in SMALL:
        shape = weights[n].shape
        two_d = (-1, shape[-1]) if n not in ("conv_w", "q_gain", "k_gain") else (1, -1)
        d2, m2, v2 = _adamw(weights[n].reshape(two_d), full[n].reshape(two_d), moms[n].reshape(two_d),
                            vels[n].reshape(two_d), f"adamw_{n}")
        deltas[n], new_m[n], new_v[n] = d2.reshape(shape), m2.reshape(shape), v2.reshape(shape)
        full[n] = full[n].reshape(shape)
    return (loss, dh[None], *[full[n] for n in ORDER], *[deltas[n] for n in ORDER], *[new_m[n] for n in ORDER],
            *[new_v[n] for n in ORDER])
```

```python
import functools

import jax
import jax.numpy as jnp
from jax import lax
from jax.experimental import pallas as pl
from jax.experimental.pallas import tpu as pltpu

F32 = jnp.float32
BF = jnp.bfloat16
MESH_ID = pl.DeviceIdType.MESH
ANY = pl.BlockSpec(memory_space=pl.ANY)

EPS = 1e-6
MASK_VALUE = -1e30
POOL_WINDOWS = (2, 4, 8, 16)
ATTN_DILATIONS = (1, 4, 16)
ATTN_BLOCK = 128
HEAD_DIM = 64
OFF_Q, OFF_K, OFF_V, OFF_GATE = 2048, 2816, 3584, 4352
N_CHIPS = 4
ADAM_LR, ADAM_B1, ADAM_B2, ADAM_EPS, ADAM_WD, ADAM_STEP = 0.001, 0.9, 0.999, 1e-08, 0.01, 10

VMEM_LIMIT = 48 * 1024 * 1024
LANES = 128

_DIMS = {"nn": (((1,), (0,)), ((), ())), "nt": (((1,), (1,)), ((), ())), "tn": (((0,), (0,)), ((), ()))}


def _params(sem):
    return pltpu.CompilerParams(dimension_semantics=sem, vmem_limit_bytes=VMEM_LIMIT)


def _dot(a, b, mode="nn"):
    return lax.dot_general(a, b, _DIMS[mode], preferred_element_type=F32)


def _mm(a, b, mode, name, *, tm, tn, tk, out_dtype=F32, res=None, aux=None, epi=None, n_outer=False,
        b_shards=False, out_shards=False, after=None):
    if mode == "tn":
        K, M = a.shape
    else:
        M, K = a.shape
    if b_shards:
        if mode == "nn":
            assert b.shape[1] == K
            N = b.shape[2] * N_CHIPS
        else:
            assert mode == "nt"
            N = b.shape[1]
            assert b.shape[2] * N_CHIPS == K
    else:
        N = b.shape[0] if mode == "nt" else b.shape[1]
    tm, tn, tk = min(tm, M), min(tn, N), min(tk, K)
    assert M % tm == 0 and N % tn == 0 and K % tk == 0
    nk = K // tk
    if n_outer:
        grid = (N // tn, M // tm, nk)
        ij = lambda p, q_: (q_, p)
    else:
        grid = (M // tm, N // tn, nk)
        ij = lambda p, q_: (p, q_)

    def amap(p, q_, k):
        i, j = ij(p, q_)
        return (k, i) if mode == "tn" else (i, k)

    a_spec = pl.BlockSpec((tk, tm) if mode == "tn" else (tm, tk), amap)
    if b_shards:
        if mode == "nn":
            per = (N // N_CHIPS) // tn
            assert per >= 1 and (N // N_CHIPS) % tn == 0

            def bmap(p, q_, k):
                i, j = ij(p, q_)
                return (j // per, k, j % per)

            b_spec = pl.BlockSpec((None, tk, tn), bmap)
        else:
            per = (K // N_CHIPS) // tk
            assert per >= 1 and (K // N_CHIPS) % tk == 0

            def bmap(p, q_, k):
                i, j = ij(p, q_)
                return (k // per, j, k % per)

            b_spec = pl.BlockSpec((None, tn, tk), bmap)
    else:
        def bmap(p, q_, k):
            i, j = ij(p, q_)
            return (j, k) if mode == "nt" else (k, j)

        b_spec = pl.BlockSpec((tn, tk) if mode == "nt" else (tk, tn), bmap)

    def omap(p, q_, k):
        return ij(p, q_)

    o_spec = pl.BlockSpec((tm, tn), omap)
    if out_shards:
        per_o = (N // N_CHIPS) // tn
        assert per_o >= 1 and (N // N_CHIPS) % tn == 0

        def osmap(p, q_, k):
            i, j = ij(p, q_)
            return (j // per_o, i, j % per_o)

        out_spec0 = pl.BlockSpec((None, tm, tn), osmap)
        out_shape0 = jax.ShapeDtypeStruct((N_CHIPS, M, N // N_CHIPS), out_dtype)
    else:
        out_spec0 = o_spec
        out_shape0 = jax.ShapeDtypeStruct((M, N), out_dtype)

    in_specs = [a_spec, b_spec]
    args = [a, b]
    if res is not None:
        in_specs.append(o_spec)
        args.append(res)
    if aux is not None:
        in_specs.append(o_spec)
        args.append(aux)
    if after is not None:
        in_specs.append(ANY)
        args.append(after)
    out_specs = [out_spec0]
    out_shape = [out_shape0]
    n_out = len(out_shape)
    has_res, has_aux, has_after = res is not None, aux is not None, after is not None

    def body(*refs):
        a_ref, b_ref = refs[0], refs[1]
        pos = 2
        res_ref = aux_ref = None
        if has_res:
            res_ref = refs[pos]
            pos += 1
        if has_aux:
            aux_ref = refs[pos]
            pos += 1
        if has_after:
            pos += 1
        outs = refs[pos:pos + n_out]
        part = _dot(a_ref[...].astype(BF), b_ref[...].astype(BF), mode)

        def finish(acc):
            if res_ref is not None:
                acc = res_ref[...] + acc
            if epi == "relu2":
                r = jnp.maximum(acc, 0.0)
                outs[0][...] = (r * r).astype(out_dtype)
            elif epi == "drelu2":
                outs[0][...] = (acc * (2.0 * jnp.sqrt(aux_ref[...].astype(F32)))).astype(out_dtype)
            else:
                outs[0][...] = acc.astype(out_dtype)

        if nk == 1:
            finish(part)
        else:
            acc_ref = refs[pos + n_out]
            k = pl.program_id(2)

            @pl.when(k == 0)
            def _():
                acc_ref[...] = part

            @pl.when(k > 0)
            def _():
                acc_ref[...] += part

            @pl.when(k == nk - 1)
            def _():
                finish(acc_ref[...])

    scratch = [pltpu.VMEM((tm, tn), F32)] if nk > 1 else []
    out = pl.pallas_call(
        body, name=name, grid=grid, in_specs=in_specs, out_specs=out_specs, out_shape=out_shape,
        scratch_shapes=scratch, compiler_params=_params(("parallel", "parallel", "arbitrary")),
    )(*args)
    return out if n_out > 1 else out[0]


def _rms_fwd(x, gain, name, after=None):
    T, D = x.shape
    tm = min(512, T)

    def body(x_ref, g_ref, *rest):
        o_ref = rest[-1]
        xv = x_ref[...]
        r = lax.rsqrt(jnp.mean(xv * xv, axis=-1, keepdims=True) + EPS)
        o_ref[...] = ((xv * r) * g_ref[...]).astype(BF)

    extra = [] if after is None else list(after) if isinstance(after, (list, tuple)) else [after]
    return pl.pallas_call(
        body, name=name, grid=(T // tm,),
        in_specs=[pl.BlockSpec((tm, D), lambda i: (i, 0)), pl.BlockSpec((1, D), lambda i: (0, 0))] + [ANY] * len(extra),
        out_specs=pl.BlockSpec((tm, D), lambda i: (i, 0)), out_shape=jax.ShapeDtypeStruct((T, D), BF),
        compiler_params=_params(("parallel",)),
    )(x, gain, *extra)


def _rms_bwd(dh, x, gain, dres, name):
    T, D = x.shape
    tm = min(512, T)

    def body(dh_ref, x_ref, g_ref, dres_ref, dx_ref, dg_ref):
        xv = x_ref[...]
        r = lax.rsqrt(jnp.mean(xv * xv, axis=-1, keepdims=True) + EPS)
        xhat = xv * r
        dhv = dh_ref[...]
        dy = dhv * g_ref[...]
        dx_ref[...] = dres_ref[...] + r * (dy - xhat * jnp.mean(dy * xhat, axis=-1, keepdims=True))

        @pl.when(pl.program_id(0) == 0)
        def _():
            dg_ref[...] = jnp.zeros_like(dg_ref)

        dg_ref[...] += jnp.sum(dhv * xhat, axis=0, keepdims=True)

    row = pl.BlockSpec((tm, D), lambda i: (i, 0))
    vec = pl.BlockSpec((1, D), lambda i: (0, 0))
    return pl.pallas_call(
        body, name=name, grid=(T // tm,), in_specs=[row, row, vec, row], out_specs=[row, vec],
        out_shape=[jax.ShapeDtypeStruct((T, D), F32), jax.ShapeDtypeStruct((1, D), F32)],
        compiler_params=_params(("arbitrary",)),
    )(dh, x, gain, dres)


def _loss_grad(y, target, name):
    T, D = y.shape
    tm = min(512, T)

    def body(y_ref, t_ref, dy_ref, l_ref):
        e = y_ref[...] - t_ref[...]
        dy_ref[...] = e / float(D)

        @pl.when(pl.program_id(0) == 0)
        def _():
            l_ref[...] = jnp.zeros_like(l_ref)

        l_ref[...] += 0.5 * jnp.sum(jnp.mean(e * e, axis=-1, keepdims=True))

    row = pl.BlockSpec((tm, D), lambda i: (i, 0))
    return pl.pallas_call(
        body, name=name, grid=(T // tm,), in_specs=[row, row],
        out_specs=[row, pl.BlockSpec((1, LANES), lambda i: (0, 0))],
        out_shape=[jax.ShapeDtypeStruct((T, D), F32), jax.ShapeDtypeStruct((1, LANES), F32)],
        compiler_params=_params(("arbitrary",)),
    )(y, target)


POOL_HALO = 16
CONV_HALO = 8


def _causal_window_sum(v, w):
    s, sh = v, 1
    while sh < w:
        s = s + pltpu.roll(s, sh, 0)
        sh *= 2
    return s


def _anticausal_window_sum(v, w):
    n = v.shape[0]
    s, sh = v, 1
    while sh < w:
        s = s + pltpu.roll(s, n - sh, 0)
        sh *= 2
    return s


def _poolconv_fwd(z, pmix_b, pscale, convw, name):
    T = z.shape[0]
    R = min(512, T)
    PH, CH = R // POOL_HALO, R // CONV_HALO

    def body(u_ref, uh_ref, b_ref, c_ref, ch_ref, x_ref, xh_ref, mix_ref, sc_ref, cw_ref, yp_ref, yc_ref):
        i = pl.program_id(0)
        keep = (i > 0).astype(F32)
        row = i * R + lax.broadcasted_iota(jnp.int32, (R, 1), 0)
        w_all = jnp.concatenate([uh_ref[...] * keep, u_ref[...]], axis=0)
        for g, w in enumerate(POOL_WINDOWS):
            cols = slice(128 * g, 128 * (g + 1))
            wg = w_all[:, cols]
            s = _causal_window_sum(wg, w)[POOL_HALO:]
            cnt = jnp.minimum(row + 1, w).astype(F32)
            dgrp = s / cnt - wg[POOL_HALO:]
            y = _dot(dgrp.astype(BF), mix_ref[g]) * sc_ref[:, cols]
            yp_ref[:, cols] = y.astype(BF)
        uc = jnp.concatenate([ch_ref[...] * xh_ref[...] * keep, c_ref[...] * x_ref[...]], axis=0)
        yc = cw_ref[2:3, :] * uc + cw_ref[0:1, :] * pltpu.roll(uc, 2, 0) + cw_ref[1:2, :] * pltpu.roll(uc, 1, 0)
        yc_ref[...] = (b_ref[...] * yc[CONV_HALO:]).astype(BF)

    def main(cb):
        return pl.BlockSpec((R, 512), lambda i: (i, cb))

    def prev(cb, halo, per):
        return pl.BlockSpec((halo, 512), lambda i: (jnp.maximum(i * per - 1, 0), cb))

    full = lambda a: pl.BlockSpec(a.shape, lambda i: (0,) * a.ndim)
    return pl.pallas_call(
        body, name=name, grid=(T // R,),
        in_specs=[main(0), prev(0, POOL_HALO, PH), main(1), main(2), prev(2, CONV_HALO, CH), main(3),
                  prev(3, CONV_HALO, CH), full(pmix_b), full(pscale), full(convw)],
        out_specs=[pl.BlockSpec((R, 512), lambda i: (i, 0))] * 2,
        out_shape=[jax.ShapeDtypeStruct((T, 512), BF)] * 2,
        compiler_params=_params(("parallel",)),
    )(z, z, z, z, z, z, z, pmix_b, pscale, convw)


def _poolconv_bwd(z, dyp, dyc, pmix_b, pscale, convw, dz, name):
    T = z.shape[0]
    R = min(512, T)
    PH, CH = R // POOL_HALO, R // CONV_HALO
    nsteps = T // R

    def body(u_ref, uh_ref, b_ref, bn_ref, c_ref, ch_ref, x_ref, xh_ref, dyp_ref, dypn_ref, dyc_ref, dycn_ref,
             mix_ref, sc_ref, cw_ref, dz_in_ref, dz_ref, dmix_ref, dsc_ref, dcw_ref):
        i = pl.program_id(0)
        keep_prev = (i > 0).astype(F32)
        keep_next = (i < nsteps - 1).astype(F32)

        @pl.when(i == 0)
        def _():
            dmix_ref[...] = jnp.zeros_like(dmix_ref)
            dsc_ref[...] = jnp.zeros_like(dsc_ref)
            dcw_ref[...] = jnp.zeros_like(dcw_ref)

        row = i * R + lax.broadcasted_iota(jnp.int32, (R, 1), 0)
        row_ext = i * R + lax.broadcasted_iota(jnp.int32, (R + POOL_HALO, 1), 0)
        w_all = jnp.concatenate([uh_ref[...] * keep_prev, u_ref[...]], axis=0)
        dyp_ext = jnp.concatenate([dyp_ref[...], dypn_ref[...] * keep_next], axis=0)
        for g, w in enumerate(POOL_WINDOWS):
            cols = slice(128 * g, 128 * (g + 1))
            wg = w_all[:, cols]
            s = _causal_window_sum(wg, w)[POOL_HALO:]
            cnt = jnp.minimum(row + 1, w).astype(F32)
            dgrp = (s / cnt - wg[POOL_HALO:]).astype(BF)
            y_pre = _dot(dgrp, mix_ref[g])
            dsc_ref[:, cols] += jnp.sum(dyp_ref[:, cols] * y_pre, axis=0, keepdims=True)
            dyb = (dyp_ext[:, cols] * sc_ref[:, cols]).astype(BF)
            dmix_ref[cols, :] += _dot(dgrp, dyb[:R], "tn")
            dd = _dot(dyb, mix_ref[g], "nt")
            cnt_ext = jnp.minimum(row_ext + 1, w).astype(F32)
            e = _anticausal_window_sum(dd / cnt_ext, w)
            dz_ref[:, cols] = (e[:R] - dd[:R]).astype(BF)
        cw0, cw1, cw2 = cw_ref[0:1, :], cw_ref[1:2, :], cw_ref[2:3, :]
        uc = jnp.concatenate([ch_ref[...] * xh_ref[...] * keep_prev, c_ref[...] * x_ref[...]], axis=0)
        uc1 = pltpu.roll(uc, 1, 0)[CONV_HALO:]
        uc2 = pltpu.roll(uc, 2, 0)[CONV_HALO:]
        uc0 = uc[CONV_HALO:]
        yc = cw2 * uc0 + cw0 * uc2 + cw1 * uc1
        dycv = dyc_ref[...]
        dz_ref[:, 512:1024] = (dycv * yc).astype(BF)
        dv_ext = jnp.concatenate([dycv * b_ref[...], dycn_ref[...] * bn_ref[...] * keep_next], axis=0)
        n_ext = R + CONV_HALO
        duc = (cw2 * dv_ext + cw1 * pltpu.roll(dv_ext, n_ext - 1, 0) + cw0 * pltpu.roll(dv_ext, n_ext - 2, 0))[:R]
        dv = dv_ext[:R]
        dcw_ref[0:1, :] += jnp.sum(dv * uc2, axis=0, keepdims=True)
        dcw_ref[1:2, :] += jnp.sum(dv * uc1, axis=0, keepdims=True)
        dcw_ref[2:3, :] += jnp.sum(dv * uc0, axis=0, keepdims=True)
        dz_ref[:, 1024:1536] = (duc * x_ref[...]).astype(BF)
        dz_ref[:, 1536:2048] = (duc * c_ref[...]).astype(BF)

    def main(cb):
        return pl.BlockSpec((R, 512), lambda i: (i, cb))

    def prev(cb, halo, per):
        return pl.BlockSpec((halo, 512), lambda i: (jnp.maximum(i * per - 1, 0), cb))

    def nxt(cb, halo, per):
        return pl.BlockSpec((halo, 512), lambda i: (jnp.minimum((i + 1) * per, T // halo - 1), cb))

    full = lambda a: pl.BlockSpec(a.shape, lambda i: (0,) * a.ndim)
    return pl.pallas_call(
        body, name=name, grid=(nsteps,),
        in_specs=[main(0), prev(0, POOL_HALO, PH), main(1), nxt(1, CONV_HALO, CH), main(2), prev(2, CONV_HALO, CH),
                  main(3), prev(3, CONV_HALO, CH), main(0), nxt(0, POOL_HALO, PH), main(0), nxt(0, CONV_HALO, CH),
                  full(pmix_b), full(pscale), full(convw), ANY],
        out_specs=[pl.BlockSpec((R, 2048), lambda i: (i, 0)), pl.BlockSpec((512, 128), lambda i: (0, 0)),
                   pl.BlockSpec((1, 512), lambda i: (0, 0)), pl.BlockSpec((8, 512), lambda i: (0, 0))],
        out_shape=[jax.ShapeDtypeStruct(dz.shape, BF), jax.ShapeDtypeStruct((512, 128), F32),
                   jax.ShapeDtypeStruct((1, 512), F32), jax.ShapeDtypeStruct((8, 512), F32)],
        input_output_aliases={15: 0}, compiler_params=_params(("arbitrary",)),
    )(z, z, z, z, z, z, z, z, dyp, dyp, dyc, dyc, pmix_b, pscale, convw, dz)


def _head_sums(v):
    row = lax.broadcasted_iota(jnp.int32, (LANES, LANES), 0) < HEAD_DIM
    col = lax.broadcasted_iota(jnp.int32, (LANES, LANES), 1) < HEAD_DIM
    same_head = jnp.where(jnp.logical_xor(row, col), 0.0, 1.0).astype(BF)
    hi = v.astype(BF)
    lo = (v - hi.astype(F32)).astype(BF)
    return _dot(hi, same_head) + _dot(lo, same_head)


def _head_norm(x, g2, ma):
    r = lax.rsqrt(_head_sums(x * x) / HEAD_DIM + EPS)
    return x * r, r


def _head_norm_bwd(dy, xhat, r, g2, ma):
    dxh = dy * g2
    return r * (dxh - xhat * (_head_sums(dxh * xhat) / HEAD_DIM))


def _head_col(tile, hm):
    return jnp.max(jnp.where(hm, tile, -jnp.inf), axis=-1, keepdims=True)


def _attn_masks(other_block_exists):
    lane = lax.broadcasted_iota(jnp.int32, (2 * ATTN_BLOCK, ATTN_BLOCK), 1)
    qi = lax.broadcasted_iota(jnp.int32, (2 * ATTN_BLOCK, ATTN_BLOCK), 0) & (ATTN_BLOCK - 1)
    never = (1 - other_block_exists.astype(jnp.int32)) * (2 * ATTN_BLOCK)
    return lane[:ATTN_BLOCK] < HEAD_DIM, lane <= qi, lane >= qi + never


def _stack_heads(x, ma):
    return jnp.concatenate([jnp.where(ma, x, 0.0), jnp.where(ma, 0.0, x)], axis=0)


def _unstack_heads(y, ma):
    return jnp.where(ma, y[:ATTN_BLOCK], y[ATTN_BLOCK:])


def _stack_cols(tile, ma):
    return jnp.concatenate([_head_col(tile, ma), _head_col(tile, jnp.logical_not(ma))], axis=0)


def _qk_norm(z, gains, name):
    T = z.shape[0]
    tm = min(512, T)
    per_kind = (OFF_K - OFF_Q) // 256

    def body(x_ref, g_ref, o_ref):
        ma = lax.broadcasted_iota(jnp.int32, (tm, LANES), 1) < HEAD_DIM
        is_q = jnp.full((1, LANES), pl.program_id(1)) < per_kind
        g = jnp.where(is_q, g_ref[0:1, :], g_ref[1:2, :])
        for t in range(2):
            sl = slice(LANES * t, LANES * (t + 1))
            o_ref[:, sl] = _head_norm(x_ref[:, sl], g, ma)[0] * g

    return pl.pallas_call(
        body, name=name, grid=(T // tm, 2 * per_kind),
        in_specs=[pl.BlockSpec((tm, 256), lambda i, n: (i, OFF_Q // 256 + n)), pl.BlockSpec((8, LANES), lambda i, n: (0, 0))],
        out_specs=pl.BlockSpec((tm, 256), lambda i, n: (i, n)),
        out_shape=jax.ShapeDtypeStruct((T, 2 * (OFF_K - OFF_Q)), F32), compiler_params=_params(("parallel", "parallel")),
    )(z, gains)


ATTN_STEP_ROWS = 1024
ATTN_UNROLL = 2


def _attn_geometry(T, d):
    sub = ATTN_BLOCK * d
    nb = T // sub
    m = max(1, min(nb, ATTN_STEP_ROWS // sub))
    assert T % sub == 0 and nb % m == 0
    return sub, nb, m


def _attn_rows(jj, r, sub, d):
    start = jj * sub + r
    if d == 1:
        return pl.ds(pl.multiple_of(start, ATTN_BLOCK), ATTN_BLOCK)
    return pl.ds(start, ATTN_BLOCK, stride=d)


def _pick(flag, a, b):
    return jnp.where(jnp.full(a.shape, flag.astype(jnp.int32)) > 0, a, b)


def _attn_fwd(z, qkn, g, d, name):
    T = z.shape[0]
    sub, nb, m = _attn_geometry(T, d)
    scale = HEAD_DIM ** -0.5

    def body(q_ref, kc_ref, kp_ref, vc_ref, vp_ref, o_ref, lse_ref):
        jb = pl.program_id(0)

        def step(s, carry):
            jj, r = s // d, s % d
            here, before = _attn_rows(jj, r, sub, d), _attn_rows(jnp.maximum(jj - 1, 0), r, sub, d)
            edge = _attn_rows(0, r, sub, d)
            first = jj == 0
            ma, mask_c, mask_p = _attn_masks(jb * m + jj > 0)
            qs = _stack_heads(q_ref[here, :], ma).astype(BF)
            kcb = kc_ref[here, :].astype(BF)
            kpb = _pick(first, kp_ref[edge, :], kc_ref[before, :]).astype(BF)
            vcb = vc_ref[here, :].astype(BF)
            vpb = _pick(first, vp_ref[edge, :], vc_ref[before, :]).astype(BF)
            s_c = jnp.where(mask_c, _dot(qs, kcb, "nt") * scale, MASK_VALUE)
            s_p = jnp.where(mask_p, _dot(qs, kpb, "nt") * scale, MASK_VALUE)
            mx = jnp.maximum(jnp.max(s_c, axis=-1, keepdims=True), jnp.max(s_p, axis=-1, keepdims=True))
            p_c = jnp.exp(s_c - mx)
            p_p = jnp.exp(s_p - mx)
            den = jnp.sum(p_c, axis=-1, keepdims=True) + jnp.sum(p_p, axis=-1, keepdims=True)
            o = (_dot(p_c.astype(BF), vcb) + _dot(p_p.astype(BF), vpb)) / den
            o_ref[here, :] = _unstack_heads(o, ma)
            lse_ref[here, :] = _unstack_heads(jnp.broadcast_to(mx + jnp.log(den), o.shape), ma)
            return carry

        lax.fori_loop(0, m * d, step, 0, unroll=ATTN_UNROLL)

    def cur(col0):
        return pl.BlockSpec((m * sub, LANES), lambda j, t: (j, col0 + 2 * g + t))

    def prv(col0):
        return pl.BlockSpec((sub, LANES), lambda j, t: (jnp.maximum(j * m - 1, 0), col0 + 2 * g + t))

    k0, v0 = (OFF_K - OFF_Q) // LANES, OFF_V // LANES
    out = pl.BlockSpec((m * sub, LANES), lambda j, t: (j, t))
    return pl.pallas_call(
        body, name=name, grid=(nb // m, 2), in_specs=[cur(0), cur(k0), prv(k0), cur(v0), prv(v0)],
        out_specs=[out, out], out_shape=[jax.ShapeDtypeStruct((T, 256), F32)] * 2,
        compiler_params=_params(("parallel", "parallel")),
    )(qkn, qkn, qkn, z, z)


def _attn_bwd(z, qkn, do, c, lse, gains, g, d, name, after=None):
    T = z.shape[0]
    sub, nb, m = _attn_geometry(T, d)
    scale = HEAD_DIM ** -0.5
    extra = [] if after is None else [after]

    def body(qr_ref, kr_ref, vc_ref, vp_ref, qn_ref, qnn_ref, kn_ref, knp_ref, do_ref, don_ref, c_ref, cn_ref,
             lse_ref, lsen_ref, g_ref, *rest):
        dq_ref, dk_ref, dv_ref, dgq_ref, dgk_ref, sq_ref, sk_ref, sv_ref = rest[len(extra):]
        jb = pl.program_id(0)

        @pl.when((jb == 0) & (pl.program_id(1) == 0))
        def _():
            dgq_ref[...] = jnp.zeros_like(dgq_ref)
            dgk_ref[...] = jnp.zeros_like(dgk_ref)

        gq, gk = g_ref[0:1, :], g_ref[1:2, :]

        def step(s, carry):
            jj, r = s // d, s % d
            here, edge = _attn_rows(jj, r, sub, d), _attn_rows(0, r, sub, d)
            before = _attn_rows(jnp.maximum(jj - 1, 0), r, sub, d)
            behind = _attn_rows(jnp.minimum(jj + 1, m - 1), r, sub, d)
            first, last = jj == 0, jj == m - 1
            block = jb * m + jj
            ma, mask_c, mask_p = _attn_masks(block > 0)
            mask_n = _attn_masks(block < nb - 1)[2]
            qhat, rq = _head_norm(qr_ref[here, :], gq, ma)
            qn = qhat * gq
            qn_next = _pick(last, qnn_ref[edge, :], qn_ref[behind, :])
            khat, rk = _head_norm(kr_ref[here, :], gk, ma)
            kcb = (khat * gk).astype(BF)
            kpb = _pick(first, knp_ref[edge, :], kn_ref[before, :]).astype(BF)
            vcb = vc_ref[here, :].astype(BF)
            vpb = _pick(first, vp_ref[edge, :], vc_ref[before, :]).astype(BF)
            do_t, don_t = do_ref[here, :], _pick(last, don_ref[edge, :], do_ref[behind, :])
            c_t, cn_t = c_ref[here, :], _pick(last, cn_ref[edge, :], c_ref[behind, :])
            lse_t, lsen_t = lse_ref[here, :], _pick(last, lsen_ref[edge, :], lse_ref[behind, :])
            qs, dos = _stack_heads(qn, ma).astype(BF), _stack_heads(do_t, ma).astype(BF)
            lse_s, c_s = _stack_cols(lse_t, ma), _stack_cols(c_t, ma)
            s_c = jnp.where(mask_c, _dot(qs, kcb, "nt") * scale, MASK_VALUE)
            s_p = jnp.where(mask_p, _dot(qs, kpb, "nt") * scale, MASK_VALUE)
            p_c = jnp.exp(s_c - lse_s)
            p_p = jnp.exp(s_p - lse_s)
            ds_c = ((p_c * (_dot(dos, vcb, "nt") + c_s)) * scale).astype(BF)
            ds_p = ((p_p * (_dot(dos, vpb, "nt") + c_s)) * scale).astype(BF)
            dq_t = _unstack_heads(_dot(ds_c, kcb) + _dot(ds_p, kpb), ma)
            qs_n, dos_n = _stack_heads(qn_next, ma).astype(BF), _stack_heads(don_t, ma).astype(BF)
            s_n = jnp.where(mask_n, _dot(qs_n, kcb, "nt") * scale, MASK_VALUE)
            p_n = jnp.exp(s_n - _stack_cols(lsen_t, ma))
            ds_n = ((p_n * (_dot(dos_n, vcb, "nt") + _stack_cols(cn_t, ma))) * scale).astype(BF)
            dv_t = _dot(p_c.astype(BF), dos, "tn") + _dot(p_n.astype(BF), dos_n, "tn")
            dk_t = _dot(ds_c, qs, "tn") + _dot(ds_n, qs_n, "tn")
            sq_ref[here, :] = _head_norm_bwd(dq_t, qhat, rq, gq, ma)
            sk_ref[here, :] = _head_norm_bwd(dk_t, khat, rk, gk, ma)
            sv_ref[here, :] = dv_t
            dgq_ref[...] += jnp.sum(dq_t * qhat, axis=0, keepdims=True)
            dgk_ref[...] += jnp.sum(dk_t * khat, axis=0, keepdims=True)
            return carry

        lax.fori_loop(0, m * d, step, 0, unroll=ATTN_UNROLL)
        dq_ref[...] = sq_ref[...].astype(BF)
        dk_ref[...] = sk_ref[...].astype(BF)
        dv_ref[...] = sv_ref[...].astype(BF)

    def cur(col0):
        return pl.BlockSpec((m * sub, LANES), lambda j, t: (j, col0 + 2 * g + t))

    def prv(col0):
        return pl.BlockSpec((sub, LANES), lambda j, t: (jnp.maximum(j * m - 1, 0), col0 + 2 * g + t))

    def nxt(col0):
        return pl.BlockSpec((sub, LANES), lambda j, t: (jnp.minimum((j + 1) * m, nb - 1), col0 + 2 * g + t))

    own = pl.BlockSpec((m * sub, LANES), lambda j, t: (j, t))
    own_next = pl.BlockSpec((sub, LANES), lambda j, t: (jnp.minimum((j + 1) * m, nb - 1), t))
    vec = pl.BlockSpec((1, LANES), lambda j, t: (0, 0))
    zq, zk, zv, k0 = OFF_Q // LANES, OFF_K // LANES, OFF_V // LANES, (OFF_K - OFF_Q) // LANES
    return pl.pallas_call(
        body, name=name, grid=(nb // m, 2),
        in_specs=[cur(zq), cur(zk), cur(zv), prv(zv), cur(0), nxt(0), cur(k0), prv(k0), own, own_next, own, own_next,
                  own, own_next, pl.BlockSpec((8, LANES), lambda j, t: (0, 0))] + [ANY] * len(extra),
        out_specs=[own, own, own, vec, vec],
        out_shape=[jax.ShapeDtypeStruct((T, 256), BF)] * 3 + [jax.ShapeDtypeStruct((1, LANES), F32)] * 2,
        scratch_shapes=[pltpu.VMEM((m * sub, LANES), F32)] * 3,
        compiler_params=_params(("arbitrary", "arbitrary")),
    )(z, z, z, z, qkn, qkn, qkn, qkn, do, do, c, c, lse, lse, gains, *extra)


MERGE_ROWS = 256
GATE_TILE = 256


def _group_mix(o_refs, lse_refs):
    lses = [r[...] for r in lse_refs]
    m = jnp.maximum(jnp.maximum(lses[0], lses[1]), lses[2])
    es = [jnp.exp(l - m) for l in lses]
    den = es[0] + es[1] + es[2]
    ws = [e / den for e in es]
    y = ws[0] * o_refs[0][...] + ws[1] * o_refs[1][...] + ws[2] * o_refs[2][...]
    return ws, y


def _sigmoid(v):
    return 1.0 / (1.0 + jnp.exp(-v))


def _merge_specs(T, z, bgate, gpu, gco, gau):
    tm = min(MERGE_ROWS, T)
    row = lambda w: pl.BlockSpec((tm, w), lambda i: (i, 0))
    gate0 = OFF_GATE // GATE_TILE
    gates = [pl.BlockSpec((tm, GATE_TILE), functools.partial(lambda i, cb: (i, cb), cb=gate0 + n))
             for n in range(3 * N_CHIPS)]
    full = lambda a: pl.BlockSpec(a.shape, lambda i: (0,) * a.ndim)
    specs = [row(512), row(512)] + [row(256)] * 6 + gates + [full(bgate), full(gpu), full(gco), full(gau)]
    return tm, row, specs


def _merge_fwd(yp, yc, o3, lse3, z, bgate, gpu, gco, gau, name):
    T = yp.shape[0]
    tm, row, specs = _merge_specs(T, z, bgate, gpu, gco, gau)

    def body(*refs):
        yp_ref, yc_ref = refs[0], refs[1]
        o_refs, lse_refs = refs[2:5], refs[5:8]
        zg = refs[8:20]
        b_ref, gpu_ref, gco_ref, gau_ref, out_ref = refs[20:25]
        yab = _group_mix(o_refs, lse_refs)[1].astype(BF)
        ys = (yp_ref[...], yc_ref[...], yab)
        ups = (gpu_ref, gco_ref, gau_ref)
        for n in range(N_CHIPS):
            acc = None
            for b in range(3):
                gcol = slice(1024 * b + GATE_TILE * n, 1024 * b + GATE_TILE * (n + 1))
                gate = _sigmoid(zg[N_CHIPS * b + n][...] + b_ref[:, gcol])
                term = gate * _dot(ys[b], ups[b][n])
                acc = term if acc is None else acc + term
            out_ref[:, GATE_TILE * n:GATE_TILE * (n + 1)] = acc.astype(BF)

    return pl.pallas_call(
        body, name=name, grid=(T // tm,), in_specs=specs, out_specs=row(1024),
        out_shape=jax.ShapeDtypeStruct((T, 1024), BF), compiler_params=_params(("parallel",)),
    )(yp, yc, *o3, *lse3, *([z] * 12), bgate, gpu, gco, gau)


def _merge_bwd(dm, yp, yc, o3, lse3, z, bgate, gpu, gco, gau, name):
    T = yp.shape[0]
    tm, row, specs = _merge_specs(T, z, bgate, gpu, gco, gau)
    nsteps = T // tm

    def body(*refs):
        dm_ref, yp_ref, yc_ref = refs[0:3]
        o_refs, lse_refs = refs[3:6], refs[6:9]
        zg = refs[9:21]
        b_ref, gpu_ref, gco_ref, gau_ref = refs[21:25]
        dzg_ref, dyp_ref, dyc_ref = refs[25:28]
        do_refs, c_refs = refs[28:31], refs[31:34]
        dgpu_ref, dgco_ref, dgau_ref, dbg_ref = refs[34:38]
        accs = refs[38:41]
        i = pl.program_id(0)

        @pl.when(i == 0)
        def _():
            for a in accs:
                a[...] = jnp.zeros_like(a)
            dbg_ref[...] = jnp.zeros_like(dbg_ref)

        ws, y = _group_mix(o_refs, lse_refs)
        ys = (yp_ref[...], yc_ref[...], y.astype(BF))
        ups = (gpu_ref, gco_ref, gau_ref)
        dys = [None, None, None]
        for n in range(N_CHIPS):
            dmn = dm_ref[:, GATE_TILE * n:GATE_TILE * (n + 1)]
            for b in range(3):
                gcol = slice(1024 * b + GATE_TILE * n, 1024 * b + GATE_TILE * (n + 1))
                gate = _sigmoid(zg[N_CHIPS * b + n][...] + b_ref[:, gcol])
                up = _dot(ys[b], ups[b][n])
                dzg = (dmn * up) * (gate * (1.0 - gate))
                dzg_ref[:, gcol] = dzg.astype(BF)
                dbg_ref[:, gcol] += jnp.sum(dzg, axis=0, keepdims=True)
                dup = (dmn * gate).astype(BF)
                accs[b][n] += _dot(ys[b], dup, "tn")
                dyb = _dot(dup, ups[b][n], "nt")
                dys[b] = dyb if dys[b] is None else dys[b] + dyb
        dyp_ref[...] = dys[0]
        dyc_ref[...] = dys[1]
        dya = dys[2]
        lane = lax.broadcasted_iota(jnp.int32, dya.shape, 1) // HEAD_DIM
        pr = dya * y
        rho = jnp.zeros_like(pr)
        for h in range(256 // HEAD_DIM):
            hm = lane == h
            rho = jnp.where(hm, jnp.sum(jnp.where(hm, pr, 0.0), axis=-1, keepdims=True), rho)
        for g in range(3):
            do_refs[g][...] = ws[g] * dya
            c_refs[g][...] = -(ws[g] * rho)

        @pl.when(i == nsteps - 1)
        def _():
            dgpu_ref[...] = accs[0][...].astype(BF)
            dgco_ref[...] = accs[1][...].astype(BF)
            dgau_ref[...] = accs[2][...].astype(BF)

    full = lambda a: pl.BlockSpec(a.shape, lambda i: (0,) * a.ndim)
    dz_gate = pl.BlockSpec((pl.Element(tm), pl.Element(3072)), lambda i: (i * tm, OFF_GATE))
    out_specs = ([dz_gate, row(512), row(512)] + [row(256)] * 6 + [full(gpu), full(gco), full(gau)]
                 + [pl.BlockSpec((1, 3072), lambda i: (0, 0))])
    out_shape = ([jax.ShapeDtypeStruct(z.shape, BF)] + [jax.ShapeDtypeStruct((T, 512), F32)] * 2
                 + [jax.ShapeDtypeStruct((T, 256), F32)] * 6
                 + [jax.ShapeDtypeStruct(g.shape, BF) for g in (gpu, gco, gau)]
                 + [jax.ShapeDtypeStruct((1, 3072), F32)])
    return pl.pallas_call(
        body, name=name, grid=(nsteps,), in_specs=[row(1024)] + specs, out_specs=out_specs, out_shape=out_shape,
        scratch_shapes=[pltpu.VMEM(g.shape, F32) for g in (gpu, gco, gau)],
        compiler_params=_params(("arbitrary",)),
    )(dm, yp, yc, *o3, *lse3, *([z] * 12), bgate, gpu, gco, gau)


def _layer_fwd(x, w, tag, after=None, late=None):
    hb = _rms_fwd(x, w["norm_mix"], f"rms_mix_{tag}", after=after)
    z = _mm(hb, w["w_in"], "nn", f"in_proj_{tag}", tm=512, tn=3712, tk=1024, n_outer=True)
    yp, yc = _poolconv_fwd(z, w["pool_mix"], w["pool_scale"], w["conv_w"], f"poolconv_{tag}")
    qkn = _qk_norm(z, w["qk_gain"], f"qk_norm_{tag}")
    o3, lse3 = [], []
    for g, d in enumerate(ATTN_DILATIONS):
        o, lse = _attn_fwd(z, qkn, g, d, f"attn{g}_{tag}")
        o3.append(o)
        lse3.append(lse)
    if late is not None:
        w = dict(w, **late(lse3[-1]))
    merged = _merge_fwd(yp, yc, o3, lse3, z, w["b_gate"], w["w_pool_up"], w["w_conv_out"], w["w_attn_up"],
                        f"merge_{tag}")
    x1 = _mm(merged, w["w_o"], "nn", f"out_proj_{tag}", tm=1024, tn=1024, tk=1024, res=x)
    h2b = _rms_fwd(x1, w["norm_mlp"], f"rms_mlp_{tag}")
    rb = _mm(h2b, w["w_ff1"], "nn", f"ff1_{tag}", tm=1024, tn=1024, tk=1024, out_dtype=BF, epi="relu2", n_outer=True,
             b_shards=True)
    x2 = _mm(rb, w["w_ff2"], "nn", f"ff2_{tag}", tm=512, tn=1024, tk=4096, res=x1)
    saved = dict(x=x, hb=hb, z=z, yp=yp, yc=yc, qkn=qkn, o3=o3, lse3=lse3, merged=merged, x1=x1, h2b=h2b, rb=rb)
    return x2, saved, w


def _layer_bwd(dx2, w, s, tag, after=None, mid=None):
    g = {}
    dab = _mm(dx2, w["w_ff2"], "nt", f"d_ff2_act_{tag}", tm=1024, tn=1024, tk=1024, out_dtype=BF, aux=s["rb"],
              epi="drelu2", after=after)
    g["w_ff2"] = _mm(s["rb"], dx2, "tn", f"d_ff2_w_{tag}", tm=1024, tn=1024, tk=2048, out_dtype=BF)
    g["w_ff1"] = _mm(s["h2b"], dab, "tn", f"d_ff1_w_{tag}", tm=1024, tn=1024, tk=2048, out_dtype=BF, out_shards=True)
    dh2 = _mm(dab, w["w_ff1"], "nt", f"d_ff1_act_{tag}", tm=1024, tn=1024, tk=1024, b_shards=True)
    dx1, g["norm_mlp"] = _rms_bwd(dh2, s["x1"], w["norm_mlp"], dx2, f"d_rms_mlp_{tag}")
    dm = _mm(dx1, w["w_o"], "nt", f"d_out_act_{tag}", tm=1024, tn=1024, tk=1024)
    g["w_o"] = _mm(s["merged"], dx1, "tn", f"d_out_w_{tag}", tm=1024, tn=1024, tk=1024, out_dtype=BF)
    (dz, dyp, dyc, do0, do1, do2, c0, c1, c2, g["w_pool_up"], g["w_conv_out"], g["w_attn_up"],
     g["b_gate"]) = _merge_bwd(dm, s["yp"], s["yc"], s["o3"], s["lse3"], s["z"], w["b_gate"], w["w_pool_up"],
                               w["w_conv_out"], w["w_attn_up"], f"d_merge_{tag}")
    behind = mid(g) if mid is not None else None
    dq, dk, dv = [], [], []
    dgq = dgk = None
    for gi, d in enumerate(ATTN_DILATIONS):
        dzq, dzk, dzv, pq, pk = _attn_bwd(s["z"], s["qkn"], (do0, do1, do2)[gi], (c0, c1, c2)[gi], s["lse3"][gi],
                                          w["qk_gain"], gi, d, f"d_attn{gi}_{tag}", after=behind)
        dq.append(dzq)
        dk.append(dzk)
        dv.append(dzv)
        dgq = pq if dgq is None else dgq + pq
        dgk = pk if dgk is None else dgk + pk
    g["q_gain"] = dgq[:, :HEAD_DIM] + dgq[:, HEAD_DIM:]
    g["k_gain"] = dgk[:, :HEAD_DIM] + dgk[:, HEAD_DIM:]
    for off, pieces in ((OFF_Q, dq), (OFF_K, dk), (OFF_V, dv)):
        for gi, piece in enumerate(pieces):
            dz = lax.dynamic_update_slice(dz, piece, (0, off + 256 * gi))
    dz, g["pool_mix"], g["pool_scale"], g["conv_w"] = _poolconv_bwd(
        s["z"], dyp, dyc, w["pool_mix"], w["pool_scale"], w["conv_w"], dz, f"d_poolconv_{tag}")
    g["w_in"] = _mm(s["hb"], dz, "tn", f"d_in_w_{tag}", tm=512, tn=3712, tk=1024, out_dtype=BF)
    dh = _mm(dz, w["w_in"], "nt", f"d_in_act_{tag}", tm=1024, tn=1024, tk=3712)
    dx, g["norm_mix"] = _rms_bwd(dh, s["x"], w["norm_mix"], dx1, f"d_rms_mix_{tag}")
    return dx, g


def _position():
    x, y, c = lax.axis_index("x"), lax.axis_index("y"), lax.axis_index("c")
    chips = [(1 - x, y), (x, 1 - y), (1 - x, 1 - y)]
    return x, y, c, 2 * x + y, chips, [2 * cx + cy for cx, cy in chips]


def _remote(src, dst, ssem, rsem, dev):
    return pltpu.make_async_remote_copy(src_ref=src, dst_ref=dst, send_sem=ssem, recv_sem=rsem, device_id=dev,
                                        device_id_type=MESH_ID)


def _position_operand():
    x, y, c = lax.axis_index("x"), lax.axis_index("y"), lax.axis_index("c")
    return jnp.stack([2 * x + y, c]).astype(jnp.int32)


def _halves(a):
    return a.reshape(a.shape[0], 2, a.shape[1] // 2, a.shape[2])


def _gather(bufs, name):
    n = len(bufs)
    views = [_halves(b) for b in bufs]

    def body(*refs):
        outs = refs[n:2 * n]
        ssem, rsem, fssem, frsem = refs[2 * n:]
        x, y, c, q, chips, qs = _position()
        sib = (x, y, 1 - c)
        sent = []
        for k in range(n):
            mine = outs[k].at[q, c]
            for j, chip in enumerate(chips):
                cp = _remote(mine, mine, ssem.at[k, j], rsem.at[k, j], (chip[0], chip[1], c))
                cp.start()
                sent.append(cp)
        for k in range(n):
            for j, chip in enumerate(chips):
                slot = outs[k].at[qs[j], c]
                _remote(slot, slot, ssem.at[k, j], rsem.at[k, j], (chip[0], chip[1], c)).wait_recv()
                cp = _remote(slot, slot, fssem.at[k, j], frsem.at[k, j], sib)
                cp.start()
                sent.append(cp)
        for k in range(n):
            for j in range(3):
                slot = outs[k].at[qs[j], 1 - c]
                _remote(slot, slot, fssem.at[k, j], frsem.at[k, j], sib).wait_recv()
        for cp in sent:
            cp.wait_send()

    outs = pl.pallas_call(
        body, name=name, in_specs=[ANY] * n, out_specs=[ANY] * n,
        out_shape=[jax.ShapeDtypeStruct(v.shape, v.dtype) for v in views],
        input_output_aliases={k: k for k in range(n)},
        scratch_shapes=[pltpu.SemaphoreType.DMA((n, 3))] * 4,
    )(*views)
    return [o.reshape(b.shape) for o, b in zip(outs, bufs)]


SEM = pl.BlockSpec(memory_space=pltpu.SEMAPHORE)
TOKEN = jax.ShapeDtypeStruct((8, LANES), F32)
TOKEN_SPEC = pl.BlockSpec(memory_space=pltpu.VMEM)


def _split_params():
    return pltpu.CompilerParams(has_side_effects=pltpu.SideEffectType.DATAFLOW_SIDE_EFFECTING)


def _gather_start(bufs, name):
    n = len(bufs)
    views = [_halves(b) for b in bufs]

    def body(*refs):
        ssem, rsem = refs[n:n + ns], refs[n + ns:n + 2 * ns]
        outs, token = refs[n + 2 * ns:2 * n + 2 * ns], refs[2 * n + 2 * ns]
        x, y, c, q, chips, qs = _position()
        for k in range(n):
            mine = outs[k].at[q, c]
            for j, chip in enumerate(chips):
                _remote(mine, mine, ssem[3 * k + j], rsem[3 * k + j], (chip[0], chip[1], c)).start()
        token[...] = jnp.zeros_like(token)

    ns = 3 * n
    outs = pl.pallas_call(
        body, name=name, in_specs=[ANY] * n, out_specs=[SEM] * (2 * ns) + [ANY] * n + [TOKEN_SPEC],
        out_shape=[pltpu.SemaphoreType.DMA(())] * (2 * ns) + [jax.ShapeDtypeStruct(v.shape, v.dtype) for v in views]
        + [TOKEN],
        input_output_aliases={k: k + 2 * ns for k in range(n)}, compiler_params=_split_params(),
    )(*views)
    return list(outs[:ns]), list(outs[ns:2 * ns]), list(outs[2 * ns:2 * ns + n]), outs[2 * ns + n]


def _gather_finish(ssem, rsem, views, after, name_wait, name_forward, shapes):
    n = len(views)
    ns = len(ssem)

    def wait_body(*refs):
        ssem_ref, rsem_ref = refs[n:n + ns], refs[n + ns:n + 2 * ns]
        outs = refs[n + 2 * ns + 1:]
        x, y, c, q, chips, qs = _position()
        for k in range(n):
            for j, chip in enumerate(chips):
                cp = _remote(outs[k].at[q, c], outs[k].at[qs[j], c], ssem_ref[3 * k + j], rsem_ref[3 * k + j],
                             (chip[0], chip[1], c))
                cp.wait_send()
                cp.wait_recv()

    landed = pl.pallas_call(
        wait_body, name=name_wait, in_specs=[ANY] * n + [SEM] * (2 * ns) + [ANY], out_specs=[ANY] * n,
        out_shape=[jax.ShapeDtypeStruct(v.shape, v.dtype) for v in views],
        input_output_aliases={k: k for k in range(n)}, compiler_params=_split_params(),
    )(*views, *ssem, *rsem, after)

    def forward_body(*refs):
        outs = refs[n:2 * n]
        fssem, frsem = refs[2 * n:]
        x, y, c, q, chips, qs = _position()
        sib = (x, y, 1 - c)
        sent = []
        for k in range(n):
            for j in range(3):
                slot = outs[k].at[qs[j], c]
                cp = _remote(slot, slot, fssem.at[k, j], frsem.at[k, j], sib)
                cp.start()
                sent.append(cp)
        for k in range(n):
            for j in range(3):
                slot = outs[k].at[qs[j], 1 - c]
                _remote(slot, slot, fssem.at[k, j], frsem.at[k, j], sib).wait_recv()
        for cp in sent:
            cp.wait_send()

    outs = pl.pallas_call(
        forward_body, name=name_forward, in_specs=[ANY] * n, out_specs=[ANY] * n,
        out_shape=[jax.ShapeDtypeStruct(v.shape, v.dtype) for v in views],
        input_output_aliases={k: k for k in range(n)}, scratch_shapes=[pltpu.SemaphoreType.DMA((n, 3))] * 2,
    )(*landed)
    return [o.reshape(s) for o, s in zip(outs, shapes)]


def _chip_exchange_start(parts, name):
    n = len(parts)

    def body(*refs):
        ssem, rsem = refs[n:n + ns], refs[n + ns:n + 2 * ns]
        base = n + 2 * ns
        srcs, outs, token = refs[base:base + n], refs[base + n:base + 2 * n], refs[base + 2 * n]
        x, y, c, q, chips, qs = _position()
        for k in range(n):
            for j, chip in enumerate(chips):
                _remote(srcs[k].at[qs[j]], outs[k].at[j], ssem[3 * k + j], rsem[3 * k + j],
                        (chip[0], chip[1], c)).start()
        token[...] = jnp.zeros_like(token)

    ns = 3 * n
    outs = pl.pallas_call(
        body, name=name, in_specs=[ANY] * n, out_specs=[SEM] * (2 * ns) + [ANY] * (2 * n) + [TOKEN_SPEC],
        out_shape=[pltpu.SemaphoreType.DMA(())] * (2 * ns) + [jax.ShapeDtypeStruct(a.shape, a.dtype) for a in parts]
        + [jax.ShapeDtypeStruct((3,) + a.shape[1:], a.dtype) for a in parts] + [TOKEN],
        input_output_aliases={k: k + 2 * ns for k in range(n)}, compiler_params=_split_params(),
    )(*parts)
    b = 2 * ns
    return list(outs[:ns]), list(outs[ns:b]), list(outs[b:b + n]), list(outs[b + n:b + 2 * n]), outs[b + 2 * n]


def _chip_exchange_wait(ssem, rsem, parts, landing, after, name):
    n = len(parts)
    ns = len(ssem)

    def body(*refs):
        ssem_ref, rsem_ref = refs[2 * n:2 * n + ns], refs[2 * n + ns:2 * n + 2 * ns]
        base = 2 * n + 2 * ns + 1
        srcs, outs = refs[base:base + n], refs[base + n:]
        x, y, c, q, chips, qs = _position()
        for k in range(n):
            for j, chip in enumerate(chips):
                cp = _remote(srcs[k].at[qs[j]], outs[k].at[j], ssem_ref[3 * k + j], rsem_ref[3 * k + j],
                             (chip[0], chip[1], c))
                cp.wait_send()
                cp.wait_recv()

    outs = pl.pallas_call(
        body, name=name, in_specs=[ANY] * (2 * n) + [SEM] * (2 * ns) + [ANY], out_specs=[ANY] * (2 * n),
        out_shape=[jax.ShapeDtypeStruct(a.shape, a.dtype) for a in list(parts) + list(landing)],
        input_output_aliases={k: k for k in range(2 * n)}, compiler_params=_split_params(),
    )(*parts, *landing, *ssem, *rsem, after)
    return list(outs[:n]), list(outs[n:])


def _pair_swap(views, name):
    n = len(views)

    def body(*refs):
        ins, outs = refs[:n], refs[n:2 * n]
        ssem, rsem = refs[2 * n:]
        x, y, c, _, _, _ = _position()
        cps = [_remote(ins[k].at[pl.ds(0, N_CHIPS), 1 - c], outs[k], ssem.at[k], rsem.at[k], (x, y, 1 - c))
               for k in range(n)]
        for cp in cps:
            cp.start()
        for cp in cps:
            cp.wait()

    return pl.pallas_call(
        body, name=name, in_specs=[ANY] * n, out_specs=[ANY] * n,
        out_shape=[jax.ShapeDtypeStruct((v.shape[0],) + v.shape[2:], v.dtype) for v in views],
        scratch_shapes=[pltpu.SemaphoreType.DMA((n,))] * 2,
    )(*views)


def _chip_exchange(parts, name):
    n = len(parts)

    def body(*refs):
        ins, outs = refs[:n], refs[n:2 * n]
        ssem, rsem = refs[2 * n:]
        x, y, c, q, chips, qs = _position()
        cps = []
        for k in range(n):
            for j, chip in enumerate(chips):
                cp = _remote(ins[k].at[qs[j]], outs[k].at[j], ssem.at[k, j], rsem.at[k, j], (chip[0], chip[1], c))
                cp.start()
                cps.append(cp)
        for cp in cps:
            cp.wait_recv()
        for cp in cps:
            cp.wait_send()

    return pl.pallas_call(
        body, name=name, in_specs=[ANY] * n, out_specs=[ANY] * n,
        out_shape=[jax.ShapeDtypeStruct((3,) + a.shape[1:], a.dtype) for a in parts],
        scratch_shapes=[pltpu.SemaphoreType.DMA((n, 3))] * 2,
    )(*parts)


def _pair_send(arrays, name):
    n = len(arrays)

    def body(*refs):
        ins, outs = refs[:n], refs[n:2 * n]
        ssem, rsem = refs[2 * n:]
        x, y, c, _, _, _ = _position()
        cps = [_remote(ins[k], outs[k], ssem.at[k], rsem.at[k], (x, y, 1 - c)) for k in range(n)]
        for cp in cps:
            cp.start()
        for cp in cps:
            cp.wait()

    return pl.pallas_call(
        body, name=name, in_specs=[ANY] * n, out_specs=[ANY] * n,
        out_shape=[jax.ShapeDtypeStruct(a.shape, a.dtype) for a in arrays],
        scratch_shapes=[pltpu.SemaphoreType.DMA((n,))] * 2,
    )(*arrays)


def _all_to_all_small(part):
    P = part.shape[0]

    def body(in_ref, out_ref, lsem, ssem, rsem):
        x, y, c = lax.axis_index("x"), lax.axis_index("y"), lax.axis_index("c")
        me = 4 * x + 2 * y + c
        flips = [(fx, fy, fc) for fx in (0, 1) for fy in (0, 1) for fc in (0, 1)][1:]
        peers = [((x + fx) % 2, (y + fy) % 2, (c + fc) % 2) for fx, fy, fc in flips]
        loc = pltpu.make_async_copy(in_ref, out_ref.at[me], lsem)
        loc.start()
        cps = [_remote(in_ref, out_ref.at[me], ssem.at[j], rsem.at[j], peer) for j, peer in enumerate(peers)]
        for cp in cps:
            cp.start()
        for j, (px, py, pc) in enumerate(peers):
            _remote(in_ref, out_ref.at[4 * px + 2 * py + pc], ssem.at[j], rsem.at[j], peers[j]).wait_recv()
        for cp in cps:
            cp.wait_send()
        loc.wait()

    return pl.pallas_call(
        body, name="small_exchange", in_specs=[ANY], out_specs=ANY,
        out_shape=jax.ShapeDtypeStruct((8, P, LANES), F32),
        scratch_shapes=[pltpu.SemaphoreType.DMA(())] + [pltpu.SemaphoreType.DMA((7,))] * 2,
    )(part)


def _small_peers():
    x, y, c = lax.axis_index("x"), lax.axis_index("y"), lax.axis_index("c")
    flips = [(fx, fy, fc) for fx in (0, 1) for fy in (0, 1) for fc in (0, 1)][1:]
    peers = [((x + fx) % 2, (y + fy) % 2, (c + fc) % 2) for fx, fy, fc in flips]
    return 4 * x + 2 * y + c, peers


def _all_to_all_small_start(part, name):
    P = part.shape[0]
    me = 4 * lax.axis_index("x") + 2 * lax.axis_index("y") + lax.axis_index("c")
    landing = lax.dynamic_update_slice(jnp.zeros((8, P, LANES), F32), part[None], (me, 0, 0))

    def body(*refs):
        sems, src, land, token = refs[2:16], refs[16], refs[17], refs[18]
        me_, peers = _small_peers()
        for j, peer in enumerate(peers):
            _remote(src, land.at[me_], sems[j], sems[7 + j], peer).start()
        token[...] = jnp.zeros_like(token)

    outs = pl.pallas_call(
        body, name=name, in_specs=[ANY, ANY], out_specs=[SEM] * 14 + [ANY, ANY, TOKEN_SPEC],
        out_shape=[pltpu.SemaphoreType.DMA(())] * 14 + [jax.ShapeDtypeStruct(part.shape, F32),
                                                       jax.ShapeDtypeStruct((8, P, LANES), F32), TOKEN],
        input_output_aliases={0: 14, 1: 15}, compiler_params=_split_params(),
    )(part, landing)
    return list(outs[:7]), list(outs[7:14]), outs[14], outs[15], outs[16]


def _all_to_all_small_wait(ssem, rsem, part, landing, after, name):
    def body(*refs):
        sems, src, land = refs[2:16], refs[17], refs[18]
        _, peers = _small_peers()
        for j, (px, py, pc) in enumerate(peers):
            cp = _remote(src, land.at[4 * px + 2 * py + pc], sems[j], sems[7 + j], peers[j])
            cp.wait_send()
            cp.wait_recv()

    return pl.pallas_call(
        body, name=name, in_specs=[ANY, ANY] + [SEM] * 14 + [ANY], out_specs=[ANY, ANY],
        out_shape=[jax.ShapeDtypeStruct(part.shape, F32), jax.ShapeDtypeStruct(landing.shape, F32)],
        input_output_aliases={0: 0, 1: 1}, compiler_params=_split_params(),
    )(part, landing, *ssem, *rsem, after)[1]


def _row_tile(rows, width, n_arrays):
    t = rows
    while t % 2 == 0 and t > 8 and 2 * n_arrays * t * width * 4 > VMEM_LIMIT // 2:
        t //= 2
    return t


def _scalar_grid(grid, in_specs, out_specs):
    return pltpu.PrefetchScalarGridSpec(num_scalar_prefetch=1, grid=grid, in_specs=in_specs, out_specs=out_specs)


def _cast_place(w3, layer, pos, name):
    _, r, c = w3.shape
    tr = _row_tile(r, c, 2)

    def body(pos_ref, w_ref, o_ref):
        o_ref[...] = w_ref[...].astype(BF)

    return pl.pallas_call(
        body, name=name,
        grid_spec=_scalar_grid((r // tr,), [pl.BlockSpec((None, tr, c), lambda i, pos: (layer, i, 0))],
                               pl.BlockSpec((None, tr, c), lambda i, pos: (pos[0], i, 0))),
        out_shape=jax.ShapeDtypeStruct((N_CHIPS, r, c), BF), compiler_params=_params(("parallel",)),
    )(pos, w3)


def _pair_sum(view, recv, pos, name):
    _, _, hr, c = view.shape
    tr = _row_tile(hr, c, 3)

    def body(pos_ref, g_ref, r_ref, o_ref):
        o_ref[...] = (g_ref[...].astype(F32) + r_ref[...].astype(F32)).astype(BF)

    blk = pl.BlockSpec((None, tr, c), lambda p, i, pos: (p, i, 0))
    return pl.pallas_call(
        body, name=name,
        grid_spec=_scalar_grid((N_CHIPS, hr // tr),
                               [pl.BlockSpec((None, None, tr, c), lambda p, i, pos: (p, pos[1], i, 0)), blk], blk),
        out_shape=jax.ShapeDtypeStruct(recv.shape, BF), compiler_params=_params(("parallel", "parallel")),
    )(pos, view, recv)


def _chip_sum(parts, recv, pos, name):
    _, hr, c = parts.shape
    tr = _row_tile(hr, c, 6)

    def body(pos_ref, p_ref, r_ref, o_ref):
        acc = p_ref[...].astype(F32)
        for j in range(3):
            acc = acc + r_ref[j].astype(F32)
        o_ref[...] = acc

    return pl.pallas_call(
        body, name=name,
        grid_spec=_scalar_grid((hr // tr,),
                               [pl.BlockSpec((None, tr, c), lambda i, pos: (pos[0], i, 0)),
                                pl.BlockSpec((3, tr, c), lambda i, pos: (0, i, 0))],
                               pl.BlockSpec((tr, c), lambda i, pos: (i, 0))),
        out_shape=jax.ShapeDtypeStruct((hr, c), F32), compiler_params=_params(("parallel",)),
    )(pos, parts, recv)


def _sum_slices(a, name):
    n, rows, width = a.shape
    tr = _row_tile(rows, width, n + 1)

    def body(a_ref, o_ref):
        acc = a_ref[0].astype(F32)
        for i in range(1, n):
            acc = acc + a_ref[i].astype(F32)
        o_ref[...] = acc

    return pl.pallas_call(
        body, name=name, grid=(rows // tr,), in_specs=[pl.BlockSpec((n, tr, width), lambda i: (0, i, 0))],
        out_specs=pl.BlockSpec((tr, width), lambda i: (i, 0)), out_shape=jax.ShapeDtypeStruct((rows, width), F32),
        compiler_params=_params(("parallel",)),
    )(a)


def _adamw_update(w, g, m, v):
    nm = ADAM_B1 * m + (1.0 - ADAM_B1) * g
    nv = ADAM_B2 * v + (1.0 - ADAM_B2) * (g * g)
    m_hat = nm / (1.0 - ADAM_B1 ** ADAM_STEP)
    v_hat = nv / (1.0 - ADAM_B2 ** ADAM_STEP)
    return -ADAM_LR * (m_hat / (jnp.sqrt(v_hat) + ADAM_EPS) + ADAM_WD * w), nm, nv


def _adamw(w, g, m, v, name):
    rows, width = w.shape
    tr = _row_tile(rows, width, 7)

    def body(w_ref, g_ref, m_ref, v_ref, d_ref, nm_ref, nv_ref):
        d_ref[...], nm_ref[...], nv_ref[...] = _adamw_update(w_ref[...], g_ref[...], m_ref[...], v_ref[...])

    blk = pl.BlockSpec((tr, width), lambda i: (i, 0))
    return pl.pallas_call(
        body, name=name, grid=(rows // tr,), in_specs=[blk] * 4, out_specs=[blk] * 3,
        out_shape=[jax.ShapeDtypeStruct((rows, width), F32)] * 3, compiler_params=_params(("parallel",)),
    )(w, g, m, v)


def _adamw_halves(w3, m3, v3, mine, other, pos, name):
    depth, r, c = w3.shape
    assert depth == 2
    hr = r // 2
    tr = _row_tile(hr, c, 11)
    sources = ((0, True, mine[0]), (0, False, other[0]), (1, True, mine[1]), (1, False, other[1]))

    def active(l, h, core, layer, own):
        mine_half = h == core
        return (l == layer) & (mine_half if own else jnp.logical_not(mine_half))

    def body(pos_ref, w_ref, m_ref, v_ref, *rest):
        g_refs, (go_ref, d_ref, nm_ref, nv_ref) = rest[:4], rest[4:]
        l, h = pl.program_id(0), pl.program_id(1)
        for (layer, own, _), g_ref in zip(sources, g_refs):
            @pl.when(active(l, h, pos_ref[1], layer, own))
            def _():
                gv = g_ref[...]
                go_ref[...] = gv
                d_ref[...], nm_ref[...], nv_ref[...] = _adamw_update(w_ref[...], gv, m_ref[...], v_ref[...])

    def gspec(layer, own):
        return pl.BlockSpec((tr, c), lambda l, h, i, pos: (jnp.where(active(l, h, pos[1], layer, own), i, 0), 0))

    blk = pl.BlockSpec((None, None, tr, c), lambda l, h, i, pos: (l, h, i, 0))
    view = lambda a: a.reshape(depth, 2, hr, c)
    outs = pl.pallas_call(
        body, name=name,
        grid_spec=_scalar_grid((depth, 2, hr // tr), [blk] * 3 + [gspec(layer, own) for layer, own, _ in sources],
                               [blk] * 4),
        out_shape=[jax.ShapeDtypeStruct((depth, 2, hr, c), F32)] * 4,
        compiler_params=_params(("parallel", "parallel", "parallel")),
    )(pos, view(w3), view(m3), view(v3), *[s[2] for s in sources])
    return [o.reshape(w3.shape) for o in outs]


BIG = ("w_in", "w_pool_up", "w_conv_out", "w_attn_up", "w_o", "w_ff1", "w_ff2")
SMALL = ("norm_mix", "b_gate", "pool_mix", "pool_scale", "conv_w", "q_gain", "k_gain", "norm_mlp")
ORDER = ("norm_mix", "w_in", "b_gate", "pool_mix", "pool_scale", "conv_w", "q_gain", "k_gain", "w_pool_up",
         "w_conv_out", "w_attn_up", "w_o", "norm_mlp", "w_ff1", "w_ff2")
COLUMN_SHARDED = ("w_in", "w_pool_up", "w_conv_out", "w_attn_up", "w_ff1")


def _matrix_weights(gathered):
    w = {}
    for name, g4 in gathered.items():
        if name == "w_in":
            w[name] = jnp.transpose(g4, (1, 0, 2)).reshape(g4.shape[1], N_CHIPS * g4.shape[2])
        elif name in COLUMN_SHARDED:
            w[name] = g4
        else:
            w[name] = g4.reshape(N_CHIPS * g4.shape[1], g4.shape[2])
    return w


def _small_weights(l, small):
    w = {}
    w["norm_mix"] = small["norm_mix"][l][None]
    w["norm_mlp"] = small["norm_mlp"][l][None]
    w["b_gate"] = small["b_gate"][l][None]
    w["pool_mix"] = small["pool_mix"][l].astype(BF)
    w["pool_scale"] = small["pool_scale"][l][None]
    w["conv_w"] = jnp.pad(small["conv_w_full"][l], ((0, 5), (0, 0)))
    w["qk_gain"] = jnp.pad(jnp.stack([jnp.tile(small["q_gain"][l], 2), jnp.tile(small["k_gain"][l], 2)]), ((0, 6), (0, 0)))
    return w


def _to_chip_major(name, g):
    if name == "w_in":
        return jnp.transpose(g.reshape(g.shape[0], N_CHIPS, g.shape[1] // N_CHIPS), (1, 0, 2))
    if name in COLUMN_SHARDED:
        return g
    return g.reshape(N_CHIPS, g.shape[0] // N_CHIPS, g.shape[1])


def _pad8(a):
    a = a.reshape(-1, LANES)
    return jnp.pad(a, ((0, (-a.shape[0]) % 8), (0, 0)))


def kernel(x, norm_mix, w_in, b_gate, pool_mix, pool_scale, conv_w, q_gain, k_gain, w_pool_up, w_conv_out, w_attn_up, w_o, norm_mlp, w_ff1, w_ff2, loss_target, m_norm_mix, m_w_in, m_b_gate, m_pool_mix, m_pool_scale, m_conv_w, m_q_gain, m_k_gain, m_w_pool_up, m_w_conv_out, m_w_attn_up, m_w_o, m_norm_mlp, m_w_ff1, m_w_ff2, v_norm_mix, v_w_in, v_b_gate, v_pool_mix, v_pool_scale, v_conv_w, v_q_gain, v_k_gain, v_w_pool_up, v_w_conv_out, v_w_attn_up, v_w_o, v_norm_mlp, v_w_ff1, v_w_ff2):
    weights = dict(norm_mix=norm_mix, w_in=w_in, b_gate=b_gate, pool_mix=pool_mix, pool_scale=pool_scale, conv_w=conv_w,
                   q_gain=q_gain, k_gain=k_gain, w_pool_up=w_pool_up, w_conv_out=w_conv_out, w_attn_up=w_attn_up,
                   w_o=w_o, norm_mlp=norm_mlp, w_ff1=w_ff1, w_ff2=w_ff2)
    moms = dict(norm_mix=m_norm_mix, w_in=m_w_in, b_gate=m_b_gate, pool_mix=m_pool_mix, pool_scale=m_pool_scale,
                conv_w=m_conv_w, q_gain=m_q_gain, k_gain=m_k_gain, w_pool_up=m_w_pool_up, w_conv_out=m_w_conv_out,
                w_attn_up=m_w_attn_up, w_o=m_w_o, norm_mlp=m_norm_mlp, w_ff1=m_w_ff1, w_ff2=m_w_ff2)
    vels = dict(norm_mix=v_norm_mix, w_in=v_w_in, b_gate=v_b_gate, pool_mix=v_pool_mix, pool_scale=v_pool_scale,
                conv_w=v_conv_w, q_gain=v_q_gain, k_gain=v_k_gain, w_pool_up=v_w_pool_up, w_conv_out=v_w_conv_out,
                w_attn_up=v_w_attn_up, w_o=v_w_o, norm_mlp=v_norm_mlp, w_ff1=v_w_ff1, w_ff2=v_w_ff2)
    depth = norm_mix.shape[0]
    q = 2 * lax.axis_index("x") + lax.axis_index("y")
    pos = _position_operand()

    assert depth == 2, "the second layer's gather hides behind the first layer's forward, and likewise backward"
    first, rest = BIG[:1], BIG[1:]
    bufs = [{n: _cast_place(weights[n], l, pos, f"cast_{n}_l{l}") for n in BIG} for l in range(depth)]
    w_first = _matrix_weights(dict(zip(first, _gather([bufs[0][n] for n in first], "gather_l0_in"))))
    b_ssem, b_rsem, b_views, b_token = _gather_start([bufs[0][n] for n in rest], "gather_start_l0_rest")
    g_ssem, g_rsem, g_views, g_token = _gather_start([bufs[1][n] for n in BIG], "gather_start_l1")
    cw_all = _all_to_all_small(_pad8(jnp.pad(conv_w.reshape(-1), (0, (-conv_w.size) % LANES))))
    conv_w_full = jnp.concatenate(
        [cw_all[2 * p].reshape(-1)[:conv_w.size].reshape(conv_w.shape) for p in range(N_CHIPS)], axis=-1)
    small = dict(weights)
    small["conv_w_full"] = conv_w_full

    def late_weights(t):
        got = _gather_finish(b_ssem, b_rsem, b_views, t, "gather_wait_l0_rest", "gather_forward_l0_rest",
                             [bufs[0][n].shape for n in rest])
        return _matrix_weights(dict(zip(rest, got)))

    wl, saved = [None] * depth, [None] * depth
    h, saved[0], wl[0] = _layer_fwd(x[0], dict(_small_weights(0, small), **w_first), "l0", after=[b_token, g_token],
                                    late=late_weights)
    got = _gather_finish(g_ssem, g_rsem, g_views, h, "gather_wait_l1", "gather_forward_l1",
                         [bufs[1][n].shape for n in BIG])
    h, saved[1], wl[1] = _layer_fwd(h, dict(_small_weights(1, small), **_matrix_weights(dict(zip(BIG, got)))), "l1")
    dh, loss_row = _loss_grad(h, loss_target[0], "loss")

    def pair_stage(names, g, tag):
        views = [_halves(_to_chip_major(n, g[n])) for n in names]
        from_sibling = _pair_swap(views, f"grad_pair_swap_{tag}")
        return [_pair_sum(views[k], from_sibling[k], pos, f"pair_sum_{n}_{tag}") for k, n in enumerate(names)]

    mine, other = [{}, {}], [{}, {}]

    def finish(names, l, started, after, tag):
        ssem, rsem, parts, landing, _ = started
        parts, arrived = _chip_exchange_wait(ssem, rsem, parts, landing, after, f"grad_chip_exchange_wait_{tag}")
        got = [_chip_sum(parts[k], arrived[k], pos, f"chip_sum_{n}_{tag}") for k, n in enumerate(names)]
        mine[l].update(zip(names, got))
        other[l].update(zip(names, _pair_send(got, f"grad_pair_send_{tag}")))

    grads, early = [None] * depth, {}
    dh, grads[1] = _layer_bwd(dh, wl[1], saved[1], "l1")
    second = _chip_exchange_start(pair_stage(BIG, grads[1], "l1"), "grad_chip_exchange_start_l1")

    def start_rest(g):
        early["rest"] = _chip_exchange_start(pair_stage(rest, g, "l0_rest"), "grad_chip_exchange_start_l0_rest")
        return early["rest"][4]

    dh, grads[0] = _layer_bwd(dh, wl[0], saved[0], "l0", after=second[4], mid=start_rest)
    last = _chip_exchange_start(pair_stage(first, grads[0], "l0_in"), "grad_chip_exchange_start_l0_in")
    finish(BIG, 1, second, last[4], "l1")
    finish(rest, 0, early["rest"], last[4], "l0_rest")
    loss = lax.psum(loss_row[0, 0], ("x", "y", "c"))
    full = {}

    pieces = []
    for n in SMALL:
        per_layer = [grads[l][n] for l in range(depth)]
        if n == "conv_w":
            per_layer = [p[:3] for p in per_layer]
        pieces.append(_pad8(jnp.stack(per_layer).reshape(-1)))
    s_ssem, s_rsem, s_part, s_landing, _ = _all_to_all_small_start(jnp.concatenate(pieces, axis=0),
                                                                     "small_grad_exchange_start")

    deltas, new_m, new_v = {}, {}, {}

    def update_matrix(n):
        full[n], deltas[n], new_m[n], new_v[n] = _adamw_halves(
            weights[n], moms[n], vels[n], [mine[l][n] for l in range(depth)], [other[l][n] for l in range(depth)], pos,
            f"adamw_{n}")

    for n in rest:
        update_matrix(n)
    finish(first, 0, last, deltas[rest[-1]], "l0_in")
    for n in first:
        update_matrix(n)
    summed = _sum_slices(_all_to_all_small_wait(s_ssem, s_rsem, s_part, s_landing, deltas[first[-1]],
                                                "small_grad_exchange_wait"), "small_sum")
    row = 0
    for n, piece in zip(SMALL, pieces):
        size = weights[n].size if n != "conv_w" else depth * 3 * 512
        flat = summed[row:row + piece.shape[0]].reshape(-1)[:size]
        row += piece.shape[0]
        if n == "conv_w":
            full[n] = lax.dynamic_slice_in_dim(flat.reshape(depth, 3, 512), q * conv_w.shape[2], conv_w.shape[2], axis=2)
        else:
            full[n] = flat.reshape(weights[n].shape)
    for n in SMALL:
        shape = weights[n].shape
        two_d = (-1, shape[-1]) if n not in ("conv_w", "q_gain", "k_gain") else (1, -1)
        d2, m2, v2 = _adamw(weights[n].reshape(two_d), full[n].reshape(two_d), moms[n].reshape(two_d),
                            vels[n].reshape(two_d), f"adamw_{n}")
        deltas[n], new_m[n], new_v[n] = d2.reshape(shape), m2.reshape(shape), v2.reshape(shape)
        full[n] = full[n].reshape(shape)
    return (loss, dh[None], *[full[n] for n in ORDER], *[deltas[n] for n in ORDER], *[new_m[n] for n in ORDER],
            *[new_v[n] for n in ORDER])
```

```python
import functools

import jax
import jax.numpy as jnp
from jax import lax
from jax.experimental import pallas as pl
from jax.experimental.pallas import tpu as pltpu

F32 = jnp.float32
BF = jnp.bfloat16
MESH_ID = pl.DeviceIdType.MESH
ANY = pl.BlockSpec(memory_space=pl.ANY)

EPS = 1e-6
MASK_VALUE = -1e30
POOL_WINDOWS = (2, 4, 8, 16)
ATTN_DILATIONS = (1, 4, 16)
ATTN_BLOCK = 128
HEAD_DIM = 64
OFF_Q, OFF_K, OFF_V, OFF_GATE = 2048, 2816, 3584, 4352
N_CHIPS = 4
ADAM_LR, ADAM_B1, ADAM_B2, ADAM_EPS, ADAM_WD, ADAM_STEP = 0.001, 0.9, 0.999, 1e-08, 0.01, 10

VMEM_LIMIT = 48 * 1024 * 1024
LANES = 128

_DIMS = {"nn": (((1,), (0,)), ((), ())), "nt": (((1,), (1,)), ((), ())), "tn": (((0,), (0,)), ((), ()))}


def _params(sem):
    return pltpu.CompilerParams(dimension_semantics=sem, vmem_limit_bytes=VMEM_LIMIT)


def _dot(a, b, mode="nn"):
    return lax.dot_general(a, b, _DIMS[mode], preferred_element_type=F32)


def _mm(a, b, mode, name, *, tm, tn, tk, out_dtype=F32, res=None, aux=None, epi=None, n_outer=False,
        b_shards=False, out_shards=False, after=None):
    if mode == "tn":
        K, M = a.shape
    else:
        M, K = a.shape
    if b_shards:
        if mode == "nn":
            assert b.shape[1] == K
            N = b.shape[2] * N_CHIPS
        else:
            assert mode == "nt"
            N = b.shape[1]
            assert b.shape[2] * N_CHIPS == K
    else:
        N = b.shape[0] if mode == "nt" else b.shape[1]
    tm, tn, tk = min(tm, M), min(tn, N), min(tk, K)
    assert M % tm == 0 and N % tn == 0 and K % tk == 0
    nk = K // tk
    if n_outer:
        grid = (N // tn, M // tm, nk)
        ij = lambda p, q_: (q_, p)
    else:
        grid = (M // tm, N // tn, nk)
        ij = lambda p, q_: (p, q_)

    def amap(p, q_, k):
        i, j = ij(p, q_)
        return (k, i) if mode == "tn" else (i, k)

    a_spec = pl.BlockSpec((tk, tm) if mode == "tn" else (tm, tk), amap)
    if b_shards:
        if mode == "nn":
            per = (N // N_CHIPS) // tn
            assert per >= 1 and (N // N_CHIPS) % tn == 0

            def bmap(p, q_, k):
                i, j = ij(p, q_)
                return (j // per, k, j % per)

            b_spec = pl.BlockSpec((None, tk, tn), bmap)
        else:
            per = (K // N_CHIPS) // tk
            assert per >= 1 and (K // N_CHIPS) % tk == 0

            def bmap(p, q_, k):
                i, j = ij(p, q_)
                return (k // per, j, k % per)

            b_spec = pl.BlockSpec((None, tn, tk), bmap)
    else:
        def bmap(p, q_, k):
            i, j = ij(p, q_)
            return (j, k) if mode == "nt" else (k, j)

        b_spec = pl.BlockSpec((tn, tk) if mode == "nt" else (tk, tn), bmap)

    def omap(p, q_, k):
        return ij(p, q_)

    o_spec = pl.BlockSpec((tm, tn), omap)
    if out_shards:
        per_o = (N // N_CHIPS) // tn
        assert per_o >= 1 and (N // N_CHIPS) % tn == 0

        def osmap(p, q_, k):
            i, j = ij(p, q_)
            return (j // per_o, i, j % per_o)

        out_spec0 = pl.BlockSpec((None, tm, tn), osmap)
        out_shape0 = jax.ShapeDtypeStruct((N_CHIPS, M, N // N_CHIPS), out_dtype)
    else:
        out_spec0 = o_spec
        out_shape0 = jax.ShapeDtypeStruct((M, N), out_dtype)

    in_specs = [a_spec, b_spec]
    args = [a, b]
    if res is not None:
        in_specs.append(o_spec)
        args.append(res)
    if aux is not None:
        in_specs.append(o_spec)
        args.append(aux)
    after = [] if after is None else list(after) if isinstance(after, (list, tuple)) else [after]
    in_specs += [ANY] * len(after)
    args += after
    out_specs = [out_spec0]
    out_shape = [out_shape0]
    n_out = len(out_shape)
    has_res, has_aux, n_after = res is not None, aux is not None, len(after)

    def body(*refs):
        a_ref, b_ref = refs[0], refs[1]
        pos = 2
        res_ref = aux_ref = None
        if has_res:
            res_ref = refs[pos]
            pos += 1
        if has_aux:
            aux_ref = refs[pos]
            pos += 1
        pos += n_after
        outs = refs[pos:pos + n_out]
        part = _dot(a_ref[...].astype(BF), b_ref[...].astype(BF), mode)

        def finish(acc):
            if res_ref is not None:
                acc = res_ref[...] + acc
            if epi == "relu2":
                r = jnp.maximum(acc, 0.0)
                outs[0][...] = (r * r).astype(out_dtype)
            elif epi == "drelu2":
                outs[0][...] = (acc * (2.0 * jnp.sqrt(aux_ref[...].astype(F32)))).astype(out_dtype)
            else:
                outs[0][...] = acc.astype(out_dtype)

        if nk == 1:
            finish(part)
        else:
            acc_ref = refs[pos + n_out]
            k = pl.program_id(2)

            @pl.when(k == 0)
            def _():
                acc_ref[...] = part

            @pl.when(k > 0)
            def _():
                acc_ref[...] += part

            @pl.when(k == nk - 1)
            def _():
                finish(acc_ref[...])

    scratch = [pltpu.VMEM((tm, tn), F32)] if nk > 1 else []
    out = pl.pallas_call(
        body, name=name, grid=grid, in_specs=in_specs, out_specs=out_specs, out_shape=out_shape,
        scratch_shapes=scratch, compiler_params=_params(("parallel", "parallel", "arbitrary")),
    )(*args)
    return out if n_out > 1 else out[0]


def _rms_fwd(x, gain, name, after=None):
    T, D = x.shape
    tm = min(512, T)

    def body(x_ref, g_ref, *rest):
        o_ref = rest[-1]
        xv = x_ref[...]
        r = lax.rsqrt(jnp.mean(xv * xv, axis=-1, keepdims=True) + EPS)
        o_ref[...] = ((xv * r) * g_ref[...]).astype(BF)

    extra = [] if after is None else list(after) if isinstance(after, (list, tuple)) else [after]
    return pl.pallas_call(
        body, name=name, grid=(T // tm,),
        in_specs=[pl.BlockSpec((tm, D), lambda i: (i, 0)), pl.BlockSpec((1, D), lambda i: (0, 0))] + [ANY] * len(extra),
        out_specs=pl.BlockSpec((tm, D), lambda i: (i, 0)), out_shape=jax.ShapeDtypeStruct((T, D), BF),
        compiler_params=_params(("parallel",)),
    )(x, gain, *extra)


def _rms_bwd(dh, x, gain, dres, name):
    T, D = x.shape
    tm = min(512, T)

    def body(dh_ref, x_ref, g_ref, dres_ref, dx_ref, dg_ref):
        xv = x_ref[...]
        r = lax.rsqrt(jnp.mean(xv * xv, axis=-1, keepdims=True) + EPS)
        xhat = xv * r
        dhv = dh_ref[...]
        dy = dhv * g_ref[...]
        dx_ref[...] = dres_ref[...] + r * (dy - xhat * jnp.mean(dy * xhat, axis=-1, keepdims=True))

        @pl.when(pl.program_id(0) == 0)
        def _():
            dg_ref[...] = jnp.zeros_like(dg_ref)

        dg_ref[...] += jnp.sum(dhv * xhat, axis=0, keepdims=True)

    row = pl.BlockSpec((tm, D), lambda i: (i, 0))
    vec = pl.BlockSpec((1, D), lambda i: (0, 0))
    return pl.pallas_call(
        body, name=name, grid=(T // tm,), in_specs=[row, row, vec, row], out_specs=[row, vec],
        out_shape=[jax.ShapeDtypeStruct((T, D), F32), jax.ShapeDtypeStruct((1, D), F32)],
        compiler_params=_params(("arbitrary",)),
    )(dh, x, gain, dres)


def _loss_grad(y, target, name):
    T, D = y.shape
    tm = min(512, T)

    def body(y_ref, t_ref, dy_ref, l_ref):
        e = y_ref[...] - t_ref[...]
        dy_ref[...] = e / float(D)

        @pl.when(pl.program_id(0) == 0)
        def _():
            l_ref[...] = jnp.zeros_like(l_ref)

        l_ref[...] += 0.5 * jnp.sum(jnp.mean(e * e, axis=-1, keepdims=True))

    row = pl.BlockSpec((tm, D), lambda i: (i, 0))
    return pl.pallas_call(
        body, name=name, grid=(T // tm,), in_specs=[row, row],
        out_specs=[row, pl.BlockSpec((1, LANES), lambda i: (0, 0))],
        out_shape=[jax.ShapeDtypeStruct((T, D), F32), jax.ShapeDtypeStruct((1, LANES), F32)],
        compiler_params=_params(("arbitrary",)),
    )(y, target)


POOL_HALO = 16
CONV_HALO = 8


def _causal_window_sum(v, w):
    s, sh = v, 1
    while sh < w:
        s = s + pltpu.roll(s, sh, 0)
        sh *= 2
    return s


def _anticausal_window_sum(v, w):
    n = v.shape[0]
    s, sh = v, 1
    while sh < w:
        s = s + pltpu.roll(s, n - sh, 0)
        sh *= 2
    return s


def _poolconv_fwd(z, pmix_b, pscale, convw, name):
    T = z.shape[0]
    R = min(512, T)
    PH, CH = R // POOL_HALO, R // CONV_HALO

    def body(u_ref, uh_ref, b_ref, c_ref, ch_ref, x_ref, xh_ref, mix_ref, sc_ref, cw_ref, yp_ref, yc_ref):
        i = pl.program_id(0)
        keep = (i > 0).astype(F32)
        row = i * R + lax.broadcasted_iota(jnp.int32, (R, 1), 0)
        w_all = jnp.concatenate([uh_ref[...] * keep, u_ref[...]], axis=0)
        for g, w in enumerate(POOL_WINDOWS):
            cols = slice(128 * g, 128 * (g + 1))
            wg = w_all[:, cols]
            s = _causal_window_sum(wg, w)[POOL_HALO:]
            cnt = jnp.minimum(row + 1, w).astype(F32)
            dgrp = s / cnt - wg[POOL_HALO:]
            y = _dot(dgrp.astype(BF), mix_ref[g]) * sc_ref[:, cols]
            yp_ref[:, cols] = y.astype(BF)
        uc = jnp.concatenate([ch_ref[...] * xh_ref[...] * keep, c_ref[...] * x_ref[...]], axis=0)
        yc = cw_ref[2:3, :] * uc + cw_ref[0:1, :] * pltpu.roll(uc, 2, 0) + cw_ref[1:2, :] * pltpu.roll(uc, 1, 0)
        yc_ref[...] = (b_ref[...] * yc[CONV_HALO:]).astype(BF)

    def main(cb):
        return pl.BlockSpec((R, 512), lambda i: (i, cb))

    def prev(cb, halo, per):
        return pl.BlockSpec((halo, 512), lambda i: (jnp.maximum(i * per - 1, 0), cb))

    full = lambda a: pl.BlockSpec(a.shape, lambda i: (0,) * a.ndim)
    return pl.pallas_call(
        body, name=name, grid=(T // R,),
        in_specs=[main(0), prev(0, POOL_HALO, PH), main(1), main(2), prev(2, CONV_HALO, CH), main(3),
                  prev(3, CONV_HALO, CH), full(pmix_b), full(pscale), full(convw)],
        out_specs=[pl.BlockSpec((R, 512), lambda i: (i, 0))] * 2,
        out_shape=[jax.ShapeDtypeStruct((T, 512), BF)] * 2,
        compiler_params=_params(("parallel",)),
    )(z, z, z, z, z, z, z, pmix_b, pscale, convw)


def _poolconv_bwd(z, dyp, dyc, pmix_b, pscale, convw, dz, name):
    T = z.shape[0]
    R = min(512, T)
    PH, CH = R // POOL_HALO, R // CONV_HALO
    nsteps = T // R

    def body(u_ref, uh_ref, b_ref, bn_ref, c_ref, ch_ref, x_ref, xh_ref, dyp_ref, dypn_ref, dyc_ref, dycn_ref,
             mix_ref, sc_ref, cw_ref, dz_in_ref, dz_ref, dmix_ref, dsc_ref, dcw_ref):
        i = pl.program_id(0)
        keep_prev = (i > 0).astype(F32)
        keep_next = (i < nsteps - 1).astype(F32)

        @pl.when(i == 0)
        def _():
            dmix_ref[...] = jnp.zeros_like(dmix_ref)
            dsc_ref[...] = jnp.zeros_like(dsc_ref)
            dcw_ref[...] = jnp.zeros_like(dcw_ref)

        row = i * R + lax.broadcasted_iota(jnp.int32, (R, 1), 0)
        row_ext = i * R + lax.broadcasted_iota(jnp.int32, (R + POOL_HALO, 1), 0)
        w_all = jnp.concatenate([uh_ref[...] * keep_prev, u_ref[...]], axis=0)
        dyp_ext = jnp.concatenate([dyp_ref[...], dypn_ref[...] * keep_next], axis=0)
        for g, w in enumerate(POOL_WINDOWS):
            cols = slice(128 * g, 128 * (g + 1))
            wg = w_all[:, cols]
            s = _causal_window_sum(wg, w)[POOL_HALO:]
            cnt = jnp.minimum(row + 1, w).astype(F32)
            dgrp = (s / cnt - wg[POOL_HALO:]).astype(BF)
            y_pre = _dot(dgrp, mix_ref[g])
            dsc_ref[:, cols] += jnp.sum(dyp_ref[:, cols] * y_pre, axis=0, keepdims=True)
            dyb = (dyp_ext[:, cols] * sc_ref[:, cols]).astype(BF)
            dmix_ref[cols, :] += _dot(dgrp, dyb[:R], "tn")
            dd = _dot(dyb, mix_ref[g], "nt")
            cnt_ext = jnp.minimum(row_ext + 1, w).astype(F32)
            e = _anticausal_window_sum(dd / cnt_ext, w)
            dz_ref[:, cols] = (e[:R] - dd[:R]).astype(BF)
        cw0, cw1, cw2 = cw_ref[0:1, :], cw_ref[1:2, :], cw_ref[2:3, :]
        uc = jnp.concatenate([ch_ref[...] * xh_ref[...] * keep_prev, c_ref[...] * x_ref[...]], axis=0)
        uc1 = pltpu.roll(uc, 1, 0)[CONV_HALO:]
        uc2 = pltpu.roll(uc, 2, 0)[CONV_HALO:]
        uc0 = uc[CONV_HALO:]
        yc = cw2 * uc0 + cw0 * uc2 + cw1 * uc1
        dycv = dyc_ref[...]
        dz_ref[:, 512:1024] = (dycv * yc).astype(BF)
        dv_ext = jnp.concatenate([dycv * b_ref[...], dycn_ref[...] * bn_ref[...] * keep_next], axis=0)
        n_ext = R + CONV_HALO
        duc = (cw2 * dv_ext + cw1 * pltpu.roll(dv_ext, n_ext - 1, 0) + cw0 * pltpu.roll(dv_ext, n_ext - 2, 0))[:R]
        dv = dv_ext[:R]
        dcw_ref[0:1, :] += jnp.sum(dv * uc2, axis=0, keepdims=True)
        dcw_ref[1:2, :] += jnp.sum(dv * uc1, axis=0, keepdims=True)
        dcw_ref[2:3, :] += jnp.sum(dv * uc0, axis=0, keepdims=True)
        dz_ref[:, 1024:1536] = (duc * x_ref[...]).astype(BF)
        dz_ref[:, 1536:2048] = (duc * c_ref[...]).astype(BF)

    def main(cb):
        return pl.BlockSpec((R, 512), lambda i: (i, cb))

    def prev(cb, halo, per):
        return pl.BlockSpec((halo, 512), lambda i: (jnp.maximum(i * per - 1, 0), cb))

    def nxt(cb, halo, per):
        return pl.BlockSpec((halo, 512), lambda i: (jnp.minimum((i + 1) * per, T // halo - 1), cb))

    full = lambda a: pl.BlockSpec(a.shape, lambda i: (0,) * a.ndim)
    return pl.pallas_call(
        body, name=name, grid=(nsteps,),
        in_specs=[main(0), prev(0, POOL_HALO, PH), main(1), nxt(1, CONV_HALO, CH), main(2), prev(2, CONV_HALO, CH),
                  main(3), prev(3, CONV_HALO, CH), main(0), nxt(0, POOL_HALO, PH), main(0), nxt(0, CONV_HALO, CH),
                  full(pmix_b), full(pscale), full(convw), ANY],
        out_specs=[pl.BlockSpec((R, 2048), lambda i: (i, 0)), pl.BlockSpec((512, 128), lambda i: (0, 0)),
                   pl.BlockSpec((1, 512), lambda i: (0, 0)), pl.BlockSpec((8, 512), lambda i: (0, 0))],
        out_shape=[jax.ShapeDtypeStruct(dz.shape, BF), jax.ShapeDtypeStruct((512, 128), F32),
                   jax.ShapeDtypeStruct((1, 512), F32), jax.ShapeDtypeStruct((8, 512), F32)],
        input_output_aliases={15: 0}, compiler_params=_params(("arbitrary",)),
    )(z, z, z, z, z, z, z, z, dyp, dyp, dyc, dyc, pmix_b, pscale, convw, dz)


def _head_sums(v):
    row = lax.broadcasted_iota(jnp.int32, (LANES, LANES), 0) < HEAD_DIM
    col = lax.broadcasted_iota(jnp.int32, (LANES, LANES), 1) < HEAD_DIM
    same_head = jnp.where(jnp.logical_xor(row, col), 0.0, 1.0).astype(BF)
    hi = v.astype(BF)
    lo = (v - hi.astype(F32)).astype(BF)
    return _dot(hi, same_head) + _dot(lo, same_head)


def _head_norm(x, g2, ma):
    r = lax.rsqrt(_head_sums(x * x) / HEAD_DIM + EPS)
    return x * r, r


def _head_norm_bwd(dy, xhat, r, g2, ma):
    dxh = dy * g2
    return r * (dxh - xhat * (_head_sums(dxh * xhat) / HEAD_DIM))


def _head_col(tile, hm):
    return jnp.max(jnp.where(hm, tile, -jnp.inf), axis=-1, keepdims=True)


def _attn_masks(other_block_exists):
    lane = lax.broadcasted_iota(jnp.int32, (2 * ATTN_BLOCK, ATTN_BLOCK), 1)
    qi = lax.broadcasted_iota(jnp.int32, (2 * ATTN_BLOCK, ATTN_BLOCK), 0) & (ATTN_BLOCK - 1)
    never = (1 - other_block_exists.astype(jnp.int32)) * (2 * ATTN_BLOCK)
    return lane[:ATTN_BLOCK] < HEAD_DIM, lane <= qi, lane >= qi + never


def _stack_heads(x, ma):
    return jnp.concatenate([jnp.where(ma, x, 0.0), jnp.where(ma, 0.0, x)], axis=0)


def _unstack_heads(y, ma):
    return jnp.where(ma, y[:ATTN_BLOCK], y[ATTN_BLOCK:])


def _stack_cols(tile, ma):
    return jnp.concatenate([_head_col(tile, ma), _head_col(tile, jnp.logical_not(ma))], axis=0)


def _qk_norm(z, gains, name):
    T = z.shape[0]
    tm = min(512, T)
    per_kind = (OFF_K - OFF_Q) // 256

    def body(x_ref, g_ref, o_ref):
        ma = lax.broadcasted_iota(jnp.int32, (tm, LANES), 1) < HEAD_DIM
        is_q = jnp.full((1, LANES), pl.program_id(1)) < per_kind
        g = jnp.where(is_q, g_ref[0:1, :], g_ref[1:2, :])
        for t in range(2):
            sl = slice(LANES * t, LANES * (t + 1))
            o_ref[:, sl] = _head_norm(x_ref[:, sl], g, ma)[0] * g

    return pl.pallas_call(
        body, name=name, grid=(T // tm, 2 * per_kind),
        in_specs=[pl.BlockSpec((tm, 256), lambda i, n: (i, OFF_Q // 256 + n)), pl.BlockSpec((8, LANES), lambda i, n: (0, 0))],
        out_specs=pl.BlockSpec((tm, 256), lambda i, n: (i, n)),
        out_shape=jax.ShapeDtypeStruct((T, 2 * (OFF_K - OFF_Q)), F32), compiler_params=_params(("parallel", "parallel")),
    )(z, gains)


ATTN_STEP_ROWS = 1024
ATTN_UNROLL = 2


def _attn_geometry(T, d):
    sub = ATTN_BLOCK * d
    nb = T // sub
    m = max(1, min(nb, ATTN_STEP_ROWS // sub))
    assert T % sub == 0 and nb % m == 0
    return sub, nb, m


def _attn_rows(jj, r, sub, d):
    start = jj * sub + r
    if d == 1:
        return pl.ds(pl.multiple_of(start, ATTN_BLOCK), ATTN_BLOCK)
    return pl.ds(start, ATTN_BLOCK, stride=d)


def _pick(flag, a, b):
    return jnp.where(jnp.full(a.shape, flag.astype(jnp.int32)) > 0, a, b)


def _attn_fwd(z, qkn, g, d, name):
    T = z.shape[0]
    sub, nb, m = _attn_geometry(T, d)
    scale = HEAD_DIM ** -0.5

    def body(q_ref, kc_ref, kp_ref, vc_ref, vp_ref, o_ref, lse_ref):
        jb = pl.program_id(0)

        def step(s, carry):
            jj, r = s // d, s % d
            here, before = _attn_rows(jj, r, sub, d), _attn_rows(jnp.maximum(jj - 1, 0), r, sub, d)
            edge = _attn_rows(0, r, sub, d)
            first = jj == 0
            ma, mask_c, mask_p = _attn_masks(jb * m + jj > 0)
            qs = _stack_heads(q_ref[here, :], ma).astype(BF)
            kcb = kc_ref[here, :].astype(BF)
            kpb = _pick(first, kp_ref[edge, :], kc_ref[before, :]).astype(BF)
            vcb = vc_ref[here, :].astype(BF)
            vpb = _pick(first, vp_ref[edge, :], vc_ref[before, :]).astype(BF)
            s_c = jnp.where(mask_c, _dot(qs, kcb, "nt") * scale, MASK_VALUE)
            s_p = jnp.where(mask_p, _dot(qs, kpb, "nt") * scale, MASK_VALUE)
            mx = jnp.maximum(jnp.max(s_c, axis=-1, keepdims=True), jnp.max(s_p, axis=-1, keepdims=True))
            p_c = jnp.exp(s_c - mx)
            p_p = jnp.exp(s_p - mx)
            den = jnp.sum(p_c, axis=-1, keepdims=True) + jnp.sum(p_p, axis=-1, keepdims=True)
            o = (_dot(p_c.astype(BF), vcb) + _dot(p_p.astype(BF), vpb)) / den
            o_ref[here, :] = _unstack_heads(o, ma)
            lse_ref[here, :] = _unstack_heads(jnp.broadcast_to(mx + jnp.log(den), o.shape), ma)
            return carry

        lax.fori_loop(0, m * d, step, 0, unroll=ATTN_UNROLL)

    def cur(col0):
        return pl.BlockSpec((m * sub, LANES), lambda j, t: (j, col0 + 2 * g + t))

    def prv(col0):
        return pl.BlockSpec((sub, LANES), lambda j, t: (jnp.maximum(j * m - 1, 0), col0 + 2 * g + t))

    k0, v0 = (OFF_K - OFF_Q) // LANES, OFF_V // LANES
    out = pl.BlockSpec((m * sub, LANES), lambda j, t: (j, t))
    return pl.pallas_call(
        body, name=name, grid=(nb // m, 2), in_specs=[cur(0), cur(k0), prv(k0), cur(v0), prv(v0)],
        out_specs=[out, out], out_shape=[jax.ShapeDtypeStruct((T, 256), F32)] * 2,
        compiler_params=_params(("parallel", "parallel")),
    )(qkn, qkn, qkn, z, z)


def _attn_bwd(z, qkn, do, c, lse, gains, g, d, name, after=None):
    T = z.shape[0]
    sub, nb, m = _attn_geometry(T, d)
    scale = HEAD_DIM ** -0.5
    extra = [] if after is None else [after]

    def body(qr_ref, kr_ref, vc_ref, vp_ref, qn_ref, qnn_ref, kn_ref, knp_ref, do_ref, don_ref, c_ref, cn_ref,
             lse_ref, lsen_ref, g_ref, *rest):
        dq_ref, dk_ref, dv_ref, dgq_ref, dgk_ref, sq_ref, sk_ref, sv_ref = rest[len(extra):]
        jb = pl.program_id(0)

        @pl.when((jb == 0) & (pl.program_id(1) == 0))
        def _():
            dgq_ref[...] = jnp.zeros_like(dgq_ref)
            dgk_ref[...] = jnp.zeros_like(dgk_ref)

        gq, gk = g_ref[0:1, :], g_ref[1:2, :]

        def step(s, carry):
            jj, r = s // d, s % d
            here, edge = _attn_rows(jj, r, sub, d), _attn_rows(0, r, sub, d)
            before = _attn_rows(jnp.maximum(jj - 1, 0), r, sub, d)
            behind = _attn_rows(jnp.minimum(jj + 1, m - 1), r, sub, d)
            first, last = jj == 0, jj == m - 1
            block = jb * m + jj
            ma, mask_c, mask_p = _attn_masks(block > 0)
            mask_n = _attn_masks(block < nb - 1)[2]
            qhat, rq = _head_norm(qr_ref[here, :], gq, ma)
            qn = qhat * gq
            qn_next = _pick(last, qnn_ref[edge, :], qn_ref[behind, :])
            khat, rk = _head_norm(kr_ref[here, :], gk, ma)
            kcb = (khat * gk).astype(BF)
            kpb = _pick(first, knp_ref[edge, :], kn_ref[before, :]).astype(BF)
            vcb = vc_ref[here, :].astype(BF)
            vpb = _pick(first, vp_ref[edge, :], vc_ref[before, :]).astype(BF)
            do_t, don_t = do_ref[here, :], _pick(last, don_ref[edge, :], do_ref[behind, :])
            c_t, cn_t = c_ref[here, :], _pick(last, cn_ref[edge, :], c_ref[behind, :])
            lse_t, lsen_t = lse_ref[here, :], _pick(last, lsen_ref[edge, :], lse_ref[behind, :])
            qs, dos = _stack_heads(qn, ma).astype(BF), _stack_heads(do_t, ma).astype(BF)
            lse_s, c_s = _stack_cols(lse_t, ma), _stack_cols(c_t, ma)
            s_c = jnp.where(mask_c, _dot(qs, kcb, "nt") * scale, MASK_VALUE)
            s_p = jnp.where(mask_p, _dot(qs, kpb, "nt") * scale, MASK_VALUE)
            p_c = jnp.exp(s_c - lse_s)
            p_p = jnp.exp(s_p - lse_s)
            ds_c = ((p_c * (_dot(dos, vcb, "nt") + c_s)) * scale).astype(BF)
            ds_p = ((p_p * (_dot(dos, vpb, "nt") + c_s)) * scale).astype(BF)
            dq_t = _unstack_heads(_dot(ds_c, kcb) + _dot(ds_p, kpb), ma)
            qs_n, dos_n = _stack_heads(qn_next, ma).astype(BF), _stack_heads(don_t, ma).astype(BF)
            s_n = jnp.where(mask_n, _dot(qs_n, kcb, "nt") * scale, MASK_VALUE)
            p_n = jnp.exp(s_n - _stack_cols(lsen_t, ma))
            ds_n = ((p_n * (_dot(dos_n, vcb, "nt") + _stack_cols(cn_t, ma))) * scale).astype(BF)
            dv_t = _dot(p_c.astype(BF), dos, "tn") + _dot(p_n.astype(BF), dos_n, "tn")
            dk_t = _dot(ds_c, qs, "tn") + _dot(ds_n, qs_n, "tn")
            sq_ref[here, :] = _head_norm_bwd(dq_t, qhat, rq, gq, ma)
            sk_ref[here, :] = _head_norm_bwd(dk_t, khat, rk, gk, ma)
            sv_ref[here, :] = dv_t
            dgq_ref[...] += jnp.sum(dq_t * qhat, axis=0, keepdims=True)
            dgk_ref[...] += jnp.sum(dk_t * khat, axis=0, keepdims=True)
            return carry

        lax.fori_loop(0, m * d, step, 0, unroll=ATTN_UNROLL)
        dq_ref[...] = sq_ref[...].astype(BF)
        dk_ref[...] = sk_ref[...].astype(BF)
        dv_ref[...] = sv_ref[...].astype(BF)

    def cur(col0):
        return pl.BlockSpec((m * sub, LANES), lambda j, t: (j, col0 + 2 * g + t))

    def prv(col0):
        return pl.BlockSpec((sub, LANES), lambda j, t: (jnp.maximum(j * m - 1, 0), col0 + 2 * g + t))

    def nxt(col0):
        return pl.BlockSpec((sub, LANES), lambda j, t: (jnp.minimum((j + 1) * m, nb - 1), col0 + 2 * g + t))

    own = pl.BlockSpec((m * sub, LANES), lambda j, t: (j, t))
    own_next = pl.BlockSpec((sub, LANES), lambda j, t: (jnp.minimum((j + 1) * m, nb - 1), t))
    vec = pl.BlockSpec((1, LANES), lambda j, t: (0, 0))
    zq, zk, zv, k0 = OFF_Q // LANES, OFF_K // LANES, OFF_V // LANES, (OFF_K - OFF_Q) // LANES
    return pl.pallas_call(
        body, name=name, grid=(nb // m, 2),
        in_specs=[cur(zq), cur(zk), cur(zv), prv(zv), cur(0), nxt(0), cur(k0), prv(k0), own, own_next, own, own_next,
                  own, own_next, pl.BlockSpec((8, LANES), lambda j, t: (0, 0))] + [ANY] * len(extra),
        out_specs=[own, own, own, vec, vec],
        out_shape=[jax.ShapeDtypeStruct((T, 256), BF)] * 3 + [jax.ShapeDtypeStruct((1, LANES), F32)] * 2,
        scratch_shapes=[pltpu.VMEM((m * sub, LANES), F32)] * 3,
        compiler_params=_params(("arbitrary", "arbitrary")),
    )(z, z, z, z, qkn, qkn, qkn, qkn, do, do, c, c, lse, lse, gains, *extra)


MERGE_ROWS = 256
GATE_TILE = 256


def _group_mix(o_refs, lse_refs):
    lses = [r[...] for r in lse_refs]
    m = jnp.maximum(jnp.maximum(lses[0], lses[1]), lses[2])
    es = [jnp.exp(l - m) for l in lses]
    den = es[0] + es[1] + es[2]
    ws = [e / den for e in es]
    y = ws[0] * o_refs[0][...] + ws[1] * o_refs[1][...] + ws[2] * o_refs[2][...]
    return ws, y


def _sigmoid(v):
    return 1.0 / (1.0 + jnp.exp(-v))


def _merge_specs(T, z, bgate, gpu, gco, gau):
    tm = min(MERGE_ROWS, T)
    row = lambda w: pl.BlockSpec((tm, w), lambda i: (i, 0))
    gate0 = OFF_GATE // GATE_TILE
    gates = [pl.BlockSpec((tm, GATE_TILE), functools.partial(lambda i, cb: (i, cb), cb=gate0 + n))
             for n in range(3 * N_CHIPS)]
    full = lambda a: pl.BlockSpec(a.shape, lambda i: (0,) * a.ndim)
    specs = [row(512), row(512)] + [row(256)] * 6 + gates + [full(bgate), full(gpu), full(gco), full(gau)]
    return tm, row, specs


def _merge_fwd(yp, yc, o3, lse3, z, bgate, gpu, gco, gau, name):
    T = yp.shape[0]
    tm, row, specs = _merge_specs(T, z, bgate, gpu, gco, gau)

    def body(*refs):
        yp_ref, yc_ref = refs[0], refs[1]
        o_refs, lse_refs = refs[2:5], refs[5:8]
        zg = refs[8:20]
        b_ref, gpu_ref, gco_ref, gau_ref, out_ref = refs[20:25]
        yab = _group_mix(o_refs, lse_refs)[1].astype(BF)
        ys = (yp_ref[...], yc_ref[...], yab)
        ups = (gpu_ref, gco_ref, gau_ref)
        for n in range(N_CHIPS):
            acc = None
            for b in range(3):
                gcol = slice(1024 * b + GATE_TILE * n, 1024 * b + GATE_TILE * (n + 1))
                gate = _sigmoid(zg[N_CHIPS * b + n][...] + b_ref[:, gcol])
                term = gate * _dot(ys[b], ups[b][n])
                acc = term if acc is None else acc + term
            out_ref[:, GATE_TILE * n:GATE_TILE * (n + 1)] = acc.astype(BF)

    return pl.pallas_call(
        body, name=name, grid=(T // tm,), in_specs=specs, out_specs=row(1024),
        out_shape=jax.ShapeDtypeStruct((T, 1024), BF), compiler_params=_params(("parallel",)),
    )(yp, yc, *o3, *lse3, *([z] * 12), bgate, gpu, gco, gau)


def _merge_bwd(dm, yp, yc, o3, lse3, z, bgate, gpu, gco, gau, name):
    T = yp.shape[0]
    tm, row, specs = _merge_specs(T, z, bgate, gpu, gco, gau)
    nsteps = T // tm

    def body(*refs):
        dm_ref, yp_ref, yc_ref = refs[0:3]
        o_refs, lse_refs = refs[3:6], refs[6:9]
        zg = refs[9:21]
        b_ref, gpu_ref, gco_ref, gau_ref = refs[21:25]
        dzg_ref, dyp_ref, dyc_ref = refs[25:28]
        do_refs, c_refs = refs[28:31], refs[31:34]
        dgpu_ref, dgco_ref, dgau_ref, dbg_ref = refs[34:38]
        accs = refs[38:41]
        i = pl.program_id(0)

        @pl.when(i == 0)
        def _():
            for a in accs:
                a[...] = jnp.zeros_like(a)
            dbg_ref[...] = jnp.zeros_like(dbg_ref)

        ws, y = _group_mix(o_refs, lse_refs)
        ys = (yp_ref[...], yc_ref[...], y.astype(BF))
        ups = (gpu_ref, gco_ref, gau_ref)
        dys = [None, None, None]
        for n in range(N_CHIPS):
            dmn = dm_ref[:, GATE_TILE * n:GATE_TILE * (n + 1)]
            for b in range(3):
                gcol = slice(1024 * b + GATE_TILE * n, 1024 * b + GATE_TILE * (n + 1))
                gate = _sigmoid(zg[N_CHIPS * b + n][...] + b_ref[:, gcol])
                up = _dot(ys[b], ups[b][n])
                dzg = (dmn * up) * (gate * (1.0 - gate))
                dzg_ref[:, gcol] = dzg.astype(BF)
                dbg_ref[:, gcol] += jnp.sum(dzg, axis=0, keepdims=True)
                dup = (dmn * gate).astype(BF)
                accs[b][n] += _dot(ys[b], dup, "tn")
                dyb = _dot(dup, ups[b][n], "nt")
                dys[b] = dyb if dys[b] is None else dys[b] + dyb
        dyp_ref[...] = dys[0]
        dyc_ref[...] = dys[1]
        dya = dys[2]
        lane = lax.broadcasted_iota(jnp.int32, dya.shape, 1) // HEAD_DIM
        pr = dya * y
        rho = jnp.zeros_like(pr)
        for h in range(256 // HEAD_DIM):
            hm = lane == h
            rho = jnp.where(hm, jnp.sum(jnp.where(hm, pr, 0.0), axis=-1, keepdims=True), rho)
        for g in range(3):
            do_refs[g][...] = ws[g] * dya
            c_refs[g][...] = -(ws[g] * rho)

        @pl.when(i == nsteps - 1)
        def _():
            dgpu_ref[...] = accs[0][...].astype(BF)
            dgco_ref[...] = accs[1][...].astype(BF)
            dgau_ref[...] = accs[2][...].astype(BF)

    full = lambda a: pl.BlockSpec(a.shape, lambda i: (0,) * a.ndim)
    dz_gate = pl.BlockSpec((pl.Element(tm), pl.Element(3072)), lambda i: (i * tm, OFF_GATE))
    out_specs = ([dz_gate, row(512), row(512)] + [row(256)] * 6 + [full(gpu), full(gco), full(gau)]
                 + [pl.BlockSpec((1, 3072), lambda i: (0, 0))])
    out_shape = ([jax.ShapeDtypeStruct(z.shape, BF)] + [jax.ShapeDtypeStruct((T, 512), F32)] * 2
                 + [jax.ShapeDtypeStruct((T, 256), F32)] * 6
                 + [jax.ShapeDtypeStruct(g.shape, BF) for g in (gpu, gco, gau)]
                 + [jax.ShapeDtypeStruct((1, 3072), F32)])
    return pl.pallas_call(
        body, name=name, grid=(nsteps,), in_specs=[row(1024)] + specs, out_specs=out_specs, out_shape=out_shape,
        scratch_shapes=[pltpu.VMEM(g.shape, F32) for g in (gpu, gco, gau)],
        compiler_params=_params(("arbitrary",)),
    )(dm, yp, yc, *o3, *lse3, *([z] * 12), bgate, gpu, gco, gau)


def _layer_fwd(x, w, tag, after=None, late=None):
    hb = _rms_fwd(x, w["norm_mix"], f"rms_mix_{tag}", after=after)
    z = _mm(hb, w["w_in"], "nt", f"in_proj_{tag}", tm=512, tn=3712, tk=1024, n_outer=True)
    yp, yc = _poolconv_fwd(z, w["pool_mix"], w["pool_scale"], w["conv_w"], f"poolconv_{tag}")
    qkn = _qk_norm(z, w["qk_gain"], f"qk_norm_{tag}")
    o3, lse3 = [], []
    for g, d in enumerate(ATTN_DILATIONS):
        o, lse = _attn_fwd(z, qkn, g, d, f"attn{g}_{tag}")
        o3.append(o)
        lse3.append(lse)
    if late is not None:
        w = dict(w, **late(lse3[-1]))
    merged = _merge_fwd(yp, yc, o3, lse3, z, w["b_gate"], w["w_pool_up"], w["w_conv_out"], w["w_attn_up"],
                        f"merge_{tag}")
    x1 = _mm(merged, w["w_o"], "nn", f"out_proj_{tag}", tm=1024, tn=1024, tk=1024, res=x)
    h2b = _rms_fwd(x1, w["norm_mlp"], f"rms_mlp_{tag}")
    rb = _mm(h2b, w["w_ff1"], "nn", f"ff1_{tag}", tm=1024, tn=1024, tk=1024, out_dtype=BF, epi="relu2", n_outer=True,
             b_shards=True)
    x2 = _mm(rb, w["w_ff2"], "nn", f"ff2_{tag}", tm=512, tn=1024, tk=4096, res=x1)
    saved = dict(x=x, hb=hb, z=z, yp=yp, yc=yc, qkn=qkn, o3=o3, lse3=lse3, merged=merged, x1=x1, h2b=h2b, rb=rb)
    return x2, saved, w


def _layer_bwd(dx2, w, s, tag, after=None, mid=None, tail=None):
    g = {}
    dab = _mm(dx2, w["w_ff2"], "nt", f"d_ff2_act_{tag}", tm=1024, tn=1024, tk=1024, out_dtype=BF, aux=s["rb"],
              epi="drelu2", after=after)
    g["w_ff2"] = _mm(s["rb"], dx2, "tn", f"d_ff2_w_{tag}", tm=1024, tn=1024, tk=2048, out_dtype=BF)
    g["w_ff1"] = _mm(s["h2b"], dab, "tn", f"d_ff1_w_{tag}", tm=1024, tn=1024, tk=2048, out_dtype=BF, out_shards=True)
    dh2 = _mm(dab, w["w_ff1"], "nt", f"d_ff1_act_{tag}", tm=1024, tn=1024, tk=1024, b_shards=True)
    dx1, g["norm_mlp"] = _rms_bwd(dh2, s["x1"], w["norm_mlp"], dx2, f"d_rms_mlp_{tag}")
    dm = _mm(dx1, w["w_o"], "nt", f"d_out_act_{tag}", tm=1024, tn=1024, tk=1024)
    g["w_o"] = _mm(s["merged"], dx1, "tn", f"d_out_w_{tag}", tm=1024, tn=1024, tk=1024, out_dtype=BF)
    (dz, dyp, dyc, do0, do1, do2, c0, c1, c2, g["w_pool_up"], g["w_conv_out"], g["w_attn_up"],
     g["b_gate"]) = _merge_bwd(dm, s["yp"], s["yc"], s["o3"], s["lse3"], s["z"], w["b_gate"], w["w_pool_up"],
                               w["w_conv_out"], w["w_attn_up"], f"d_merge_{tag}")
    behind = mid(g) if mid is not None else None
    dq, dk, dv = [], [], []
    dgq = dgk = None
    for gi, d in enumerate(ATTN_DILATIONS):
        dzq, dzk, dzv, pq, pk = _attn_bwd(s["z"], s["qkn"], (do0, do1, do2)[gi], (c0, c1, c2)[gi], s["lse3"][gi],
                                          w["qk_gain"], gi, d, f"d_attn{gi}_{tag}", after=behind)
        dq.append(dzq)
        dk.append(dzk)
        dv.append(dzv)
        dgq = pq if dgq is None else dgq + pq
        dgk = pk if dgk is None else dgk + pk
    g["q_gain"] = dgq[:, :HEAD_DIM] + dgq[:, HEAD_DIM:]
    g["k_gain"] = dgk[:, :HEAD_DIM] + dgk[:, HEAD_DIM:]
    for off, pieces in ((OFF_Q, dq), (OFF_K, dk), (OFF_V, dv)):
        for gi, piece in enumerate(pieces):
            dz = lax.dynamic_update_slice(dz, piece, (0, off + 256 * gi))
    dz, g["pool_mix"], g["pool_scale"], g["conv_w"] = _poolconv_bwd(
        s["z"], dyp, dyc, w["pool_mix"], w["pool_scale"], w["conv_w"], dz, f"d_poolconv_{tag}")
    g["w_in"] = _mm(s["hb"], dz, "tn", f"d_in_w_{tag}", tm=512, tn=3712, tk=1024, out_dtype=BF)
    dh = _mm(dz, w["w_in"], "nn", f"d_in_act_{tag}", tm=1024, tn=1024, tk=3712,
             after=tail(g) if tail is not None else None)
    dx, g["norm_mix"] = _rms_bwd(dh, s["x"], w["norm_mix"], dx1, f"d_rms_mix_{tag}")
    return dx, g


def _position():
    x, y, c = lax.axis_index("x"), lax.axis_index("y"), lax.axis_index("c")
    chips = [(1 - x, y), (x, 1 - y), (1 - x, 1 - y)]
    return x, y, c, 2 * x + y, chips, [2 * cx + cy for cx, cy in chips]


def _remote(src, dst, ssem, rsem, dev):
    return pltpu.make_async_remote_copy(src_ref=src, dst_ref=dst, send_sem=ssem, recv_sem=rsem, device_id=dev,
                                        device_id_type=MESH_ID)


def _position_operand():
    x, y, c = lax.axis_index("x"), lax.axis_index("y"), lax.axis_index("c")
    return jnp.stack([2 * x + y, c]).astype(jnp.int32)


def _halves(a):
    return a.reshape(a.shape[0], 2, a.shape[1] // 2, a.shape[2])


def _gather(bufs, name):
    n = len(bufs)
    views = [_halves(b) for b in bufs]

    def body(*refs):
        outs = refs[n:2 * n]
        ssem, rsem, fssem, frsem = refs[2 * n:]
        x, y, c, q, chips, qs = _position()
        sib = (x, y, 1 - c)
        sent = []
        for k in range(n):
            mine = outs[k].at[q, c]
            for j, chip in enumerate(chips):
                cp = _remote(mine, mine, ssem.at[k, j], rsem.at[k, j], (chip[0], chip[1], c))
                cp.start()
                sent.append(cp)
        for k in range(n):
            for j, chip in enumerate(chips):
                slot = outs[k].at[qs[j], c]
                _remote(slot, slot, ssem.at[k, j], rsem.at[k, j], (chip[0], chip[1], c)).wait_recv()
                cp = _remote(slot, slot, fssem.at[k, j], frsem.at[k, j], sib)
                cp.start()
                sent.append(cp)
        for k in range(n):
            for j in range(3):
                slot = outs[k].at[qs[j], 1 - c]
                _remote(slot, slot, fssem.at[k, j], frsem.at[k, j], sib).wait_recv()
        for cp in sent:
            cp.wait_send()

    outs = pl.pallas_call(
        body, name=name, in_specs=[ANY] * n, out_specs=[ANY] * n,
        out_shape=[jax.ShapeDtypeStruct(v.shape, v.dtype) for v in views],
        input_output_aliases={k: k for k in range(n)},
        scratch_shapes=[pltpu.SemaphoreType.DMA((n, 3))] * 4,
    )(*views)
    return [o.reshape(b.shape) for o, b in zip(outs, bufs)]


SEM = pl.BlockSpec(memory_space=pltpu.SEMAPHORE)
TOKEN = jax.ShapeDtypeStruct((8, LANES), F32)
TOKEN_SPEC = pl.BlockSpec(memory_space=pltpu.VMEM)


def _split_params():
    return pltpu.CompilerParams(has_side_effects=pltpu.SideEffectType.DATAFLOW_SIDE_EFFECTING)


def _gather_start(bufs, name):
    n = len(bufs)
    views = [_halves(b) for b in bufs]

    def body(*refs):
        ssem, rsem = refs[n:n + ns], refs[n + ns:n + 2 * ns]
        outs, token = refs[n + 2 * ns:2 * n + 2 * ns], refs[2 * n + 2 * ns]
        x, y, c, q, chips, qs = _position()
        for k in range(n):
            mine = outs[k].at[q, c]
            for j, chip in enumerate(chips):
                _remote(mine, mine, ssem[3 * k + j], rsem[3 * k + j], (chip[0], chip[1], c)).start()
        token[...] = jnp.zeros_like(token)

    ns = 3 * n
    outs = pl.pallas_call(
        body, name=name, in_specs=[ANY] * n, out_specs=[SEM] * (2 * ns) + [ANY] * n + [TOKEN_SPEC],
        out_shape=[pltpu.SemaphoreType.DMA(())] * (2 * ns) + [jax.ShapeDtypeStruct(v.shape, v.dtype) for v in views]
        + [TOKEN],
        input_output_aliases={k: k + 2 * ns for k in range(n)}, compiler_params=_split_params(),
    )(*views)
    return list(outs[:ns]), list(outs[ns:2 * ns]), list(outs[2 * ns:2 * ns + n]), outs[2 * ns + n]


def _gather_finish(ssem, rsem, views, after, name_wait, name_forward, shapes):
    n = len(views)
    ns = len(ssem)

    def wait_body(*refs):
        ssem_ref, rsem_ref = refs[n:n + ns], refs[n + ns:n + 2 * ns]
        outs = refs[n + 2 * ns + 1:]
        x, y, c, q, chips, qs = _position()
        for k in range(n):
            for j, chip in enumerate(chips):
                cp = _remote(outs[k].at[q, c], outs[k].at[qs[j], c], ssem_ref[3 * k + j], rsem_ref[3 * k + j],
                             (chip[0], chip[1], c))
                cp.wait_send()
                cp.wait_recv()

    landed = pl.pallas_call(
        wait_body, name=name_wait, in_specs=[ANY] * n + [SEM] * (2 * ns) + [ANY], out_specs=[ANY] * n,
        out_shape=[jax.ShapeDtypeStruct(v.shape, v.dtype) for v in views],
        input_output_aliases={k: k for k in range(n)}, compiler_params=_split_params(),
    )(*views, *ssem, *rsem, after)

    def forward_body(*refs):
        outs = refs[n:2 * n]
        fssem, frsem = refs[2 * n:]
        x, y, c, q, chips, qs = _position()
        sib = (x, y, 1 - c)
        sent = []
        for k in range(n):
            for j in range(3):
                slot = outs[k].at[qs[j], c]
                cp = _remote(slot, slot, fssem.at[k, j], frsem.at[k, j], sib)
                cp.start()
                sent.append(cp)
        for k in range(n):
            for j in range(3):
                slot = outs[k].at[qs[j], 1 - c]
                _remote(slot, slot, fssem.at[k, j], frsem.at[k, j], sib).wait_recv()
        for cp in sent:
            cp.wait_send()

    outs = pl.pallas_call(
        forward_body, name=name_forward, in_specs=[ANY] * n, out_specs=[ANY] * n,
        out_shape=[jax.ShapeDtypeStruct(v.shape, v.dtype) for v in views],
        input_output_aliases={k: k for k in range(n)}, scratch_shapes=[pltpu.SemaphoreType.DMA((n, 3))] * 2,
    )(*landed)
    return [o.reshape(s) for o, s in zip(outs, shapes)]


def _chip_exchange_start(parts, name):
    n = len(parts)

    def body(*refs):
        ssem, rsem = refs[n:n + ns], refs[n + ns:n + 2 * ns]
        base = n + 2 * ns
        srcs, outs, token = refs[base:base + n], refs[base + n:base + 2 * n], refs[base + 2 * n]
        x, y, c, q, chips, qs = _position()
        for k in range(n):
            for j, chip in enumerate(chips):
                _remote(srcs[k].at[qs[j]], outs[k].at[j], ssem[3 * k + j], rsem[3 * k + j],
                        (chip[0], chip[1], c)).start()
        token[...] = jnp.zeros_like(token)

    ns = 3 * n
    outs = pl.pallas_call(
        body, name=name, in_specs=[ANY] * n, out_specs=[SEM] * (2 * ns) + [ANY] * (2 * n) + [TOKEN_SPEC],
        out_shape=[pltpu.SemaphoreType.DMA(())] * (2 * ns) + [jax.ShapeDtypeStruct(a.shape, a.dtype) for a in parts]
        + [jax.ShapeDtypeStruct((3,) + a.shape[1:], a.dtype) for a in parts] + [TOKEN],
        input_output_aliases={k: k + 2 * ns for k in range(n)}, compiler_params=_split_params(),
    )(*parts)
    b = 2 * ns
    return list(outs[:ns]), list(outs[ns:b]), list(outs[b:b + n]), list(outs[b + n:b + 2 * n]), outs[b + 2 * n]


def _chip_exchange_wait(ssem, rsem, parts, landing, after, name):
    n = len(parts)
    ns = len(ssem)

    def body(*refs):
        ssem_ref, rsem_ref = refs[2 * n:2 * n + ns], refs[2 * n + ns:2 * n + 2 * ns]
        base = 2 * n + 2 * ns + 1
        srcs, outs = refs[base:base + n], refs[base + n:]
        x, y, c, q, chips, qs = _position()
        for k in range(n):
            for j, chip in enumerate(chips):
                cp = _remote(srcs[k].at[qs[j]], outs[k].at[j], ssem_ref[3 * k + j], rsem_ref[3 * k + j],
                             (chip[0], chip[1], c))
                cp.wait_send()
                cp.wait_recv()

    outs = pl.pallas_call(
        body, name=name, in_specs=[ANY] * (2 * n) + [SEM] * (2 * ns) + [ANY], out_specs=[ANY] * (2 * n),
        out_shape=[jax.ShapeDtypeStruct(a.shape, a.dtype) for a in list(parts) + list(landing)],
        input_output_aliases={k: k for k in range(2 * n)}, compiler_params=_split_params(),
    )(*parts, *landing, *ssem, *rsem, after)
    return list(outs[:n]), list(outs[n:])


def _pair_swap(views, name):
    n = len(views)

    def body(*refs):
        ins, outs = refs[:n], refs[n:2 * n]
        ssem, rsem = refs[2 * n:]
        x, y, c, _, _, _ = _position()
        cps = [_remote(ins[k].at[pl.ds(0, N_CHIPS), 1 - c], outs[k], ssem.at[k], rsem.at[k], (x, y, 1 - c))
               for k in range(n)]
        for cp in cps:
            cp.start()
        for cp in cps:
            cp.wait()

    return pl.pallas_call(
        body, name=name, in_specs=[ANY] * n, out_specs=[ANY] * n,
        out_shape=[jax.ShapeDtypeStruct((v.shape[0],) + v.shape[2:], v.dtype) for v in views],
        scratch_shapes=[pltpu.SemaphoreType.DMA((n,))] * 2,
    )(*views)


def _chip_exchange(parts, name):
    n = len(parts)

    def body(*refs):
        ins, outs = refs[:n], refs[n:2 * n]
        ssem, rsem = refs[2 * n:]
        x, y, c, q, chips, qs = _position()
        cps = []
        for k in range(n):
            for j, chip in enumerate(chips):
                cp = _remote(ins[k].at[qs[j]], outs[k].at[j], ssem.at[k, j], rsem.at[k, j], (chip[0], chip[1], c))
                cp.start()
                cps.append(cp)
        for cp in cps:
            cp.wait_recv()
        for cp in cps:
            cp.wait_send()

    return pl.pallas_call(
        body, name=name, in_specs=[ANY] * n, out_specs=[ANY] * n,
        out_shape=[jax.ShapeDtypeStruct((3,) + a.shape[1:], a.dtype) for a in parts],
        scratch_shapes=[pltpu.SemaphoreType.DMA((n, 3))] * 2,
    )(*parts)


def _pair_send(arrays, name):
    n = len(arrays)

    def body(*refs):
        ins, outs = refs[:n], refs[n:2 * n]
        ssem, rsem = refs[2 * n:]
        x, y, c, _, _, _ = _position()
        cps = [_remote(ins[k], outs[k], ssem.at[k], rsem.at[k], (x, y, 1 - c)) for k in range(n)]
        for cp in cps:
            cp.start()
        for cp in cps:
            cp.wait()

    return pl.pallas_call(
        body, name=name, in_specs=[ANY] * n, out_specs=[ANY] * n,
        out_shape=[jax.ShapeDtypeStruct(a.shape, a.dtype) for a in arrays],
        scratch_shapes=[pltpu.SemaphoreType.DMA((n,))] * 2,
    )(*arrays)


def _all_to_all_small(part):
    P = part.shape[0]

    def body(in_ref, out_ref, lsem, ssem, rsem):
        x, y, c = lax.axis_index("x"), lax.axis_index("y"), lax.axis_index("c")
        me = 4 * x + 2 * y + c
        flips = [(fx, fy, fc) for fx in (0, 1) for fy in (0, 1) for fc in (0, 1)][1:]
        peers = [((x + fx) % 2, (y + fy) % 2, (c + fc) % 2) for fx, fy, fc in flips]
        loc = pltpu.make_async_copy(in_ref, out_ref.at[me], lsem)
        loc.start()
        cps = [_remote(in_ref, out_ref.at[me], ssem.at[j], rsem.at[j], peer) for j, peer in enumerate(peers)]
        for cp in cps:
            cp.start()
        for j, (px, py, pc) in enumerate(peers):
            _remote(in_ref, out_ref.at[4 * px + 2 * py + pc], ssem.at[j], rsem.at[j], peers[j]).wait_recv()
        for cp in cps:
            cp.wait_send()
        loc.wait()

    return pl.pallas_call(
        body, name="small_exchange", in_specs=[ANY], out_specs=ANY,
        out_shape=jax.ShapeDtypeStruct((8, P, LANES), F32),
        scratch_shapes=[pltpu.SemaphoreType.DMA(())] + [pltpu.SemaphoreType.DMA((7,))] * 2,
    )(part)


def _small_peers():
    x, y, c = lax.axis_index("x"), lax.axis_index("y"), lax.axis_index("c")
    flips = [(fx, fy, fc) for fx in (0, 1) for fy in (0, 1) for fc in (0, 1)][1:]
    peers = [((x + fx) % 2, (y + fy) % 2, (c + fc) % 2) for fx, fy, fc in flips]
    return 4 * x + 2 * y + c, peers


def _all_to_all_small_start(part, name):
    P = part.shape[0]
    me = 4 * lax.axis_index("x") + 2 * lax.axis_index("y") + lax.axis_index("c")
    landing = lax.dynamic_update_slice(jnp.zeros((8, P, LANES), F32), part[None], (me, 0, 0))

    def body(*refs):
        sems, src, land, token = refs[2:16], refs[16], refs[17], refs[18]
        me_, peers = _small_peers()
        for j, peer in enumerate(peers):
            _remote(src, land.at[me_], sems[j], sems[7 + j], peer).start()
        token[...] = jnp.zeros_like(token)

    outs = pl.pallas_call(
        body, name=name, in_specs=[ANY, ANY], out_specs=[SEM] * 14 + [ANY, ANY, TOKEN_SPEC],
        out_shape=[pltpu.SemaphoreType.DMA(())] * 14 + [jax.ShapeDtypeStruct(part.shape, F32),
                                                       jax.ShapeDtypeStruct((8, P, LANES), F32), TOKEN],
        input_output_aliases={0: 14, 1: 15}, compiler_params=_split_params(),
    )(part, landing)
    return list(outs[:7]), list(outs[7:14]), outs[14], outs[15], outs[16]


def _all_to_all_small_wait(ssem, rsem, part, landing, after, name):
    def body(*refs):
        sems, src, land = refs[2:16], refs[17], refs[18]
        _, peers = _small_peers()
        for j, (px, py, pc) in enumerate(peers):
            cp = _remote(src, land.at[4 * px + 2 * py + pc], sems[j], sems[7 + j], peers[j])
            cp.wait_send()
            cp.wait_recv()

    return pl.pallas_call(
        body, name=name, in_specs=[ANY, ANY] + [SEM] * 14 + [ANY], out_specs=[ANY, ANY],
        out_shape=[jax.ShapeDtypeStruct(part.shape, F32), jax.ShapeDtypeStruct(landing.shape, F32)],
        input_output_aliases={0: 0, 1: 1}, compiler_params=_split_params(),
    )(part, landing, *ssem, *rsem, after)[1]


def _row_tile(rows, width, n_arrays):
    t = rows
    while t % 2 == 0 and t > 8 and 2 * n_arrays * t * width * 4 > VMEM_LIMIT // 2:
        t //= 2
    return t


def _scalar_grid(grid, in_specs, out_specs):
    return pltpu.PrefetchScalarGridSpec(num_scalar_prefetch=1, grid=grid, in_specs=in_specs, out_specs=out_specs)


def _cast_place(w3, layer, pos, name):
    _, r, c = w3.shape
    tr = _row_tile(r, c, 2)

    def body(pos_ref, w_ref, o_ref):
        o_ref[...] = w_ref[...].astype(BF)

    return pl.pallas_call(
        body, name=name,
        grid_spec=_scalar_grid((r // tr,), [pl.BlockSpec((None, tr, c), lambda i, pos: (layer, i, 0))],
                               pl.BlockSpec((None, tr, c), lambda i, pos: (pos[0], i, 0))),
        out_shape=jax.ShapeDtypeStruct((N_CHIPS, r, c), BF), compiler_params=_params(("parallel",)),
    )(pos, w3)


def _pair_sum(view, recv, pos, name):
    _, _, hr, c = view.shape
    tr = _row_tile(hr, c, 3)

    def body(pos_ref, g_ref, r_ref, o_ref):
        o_ref[...] = (g_ref[...].astype(F32) + r_ref[...].astype(F32)).astype(BF)

    blk = pl.BlockSpec((None, tr, c), lambda p, i, pos: (p, i, 0))
    return pl.pallas_call(
        body, name=name,
        grid_spec=_scalar_grid((N_CHIPS, hr // tr),
                               [pl.BlockSpec((None, None, tr, c), lambda p, i, pos: (p, pos[1], i, 0)), blk], blk),
        out_shape=jax.ShapeDtypeStruct(recv.shape, BF), compiler_params=_params(("parallel", "parallel")),
    )(pos, view, recv)


def _chip_sum(parts, recv, pos, name):
    _, hr, c = parts.shape
    tr = _row_tile(hr, c, 6)

    def body(pos_ref, p_ref, r_ref, o_ref):
        acc = p_ref[...].astype(F32)
        for j in range(3):
            acc = acc + r_ref[j].astype(F32)
        o_ref[...] = acc

    return pl.pallas_call(
        body, name=name,
        grid_spec=_scalar_grid((hr // tr,),
                               [pl.BlockSpec((None, tr, c), lambda i, pos: (pos[0], i, 0)),
                                pl.BlockSpec((3, tr, c), lambda i, pos: (0, i, 0))],
                               pl.BlockSpec((tr, c), lambda i, pos: (i, 0))),
        out_shape=jax.ShapeDtypeStruct((hr, c), F32), compiler_params=_params(("parallel",)),
    )(pos, parts, recv)


def _sum_slices(a, name):
    n, rows, width = a.shape
    tr = _row_tile(rows, width, n + 1)

    def body(a_ref, o_ref):
        acc = a_ref[0].astype(F32)
        for i in range(1, n):
            acc = acc + a_ref[i].astype(F32)
        o_ref[...] = acc

    return pl.pallas_call(
        body, name=name, grid=(rows // tr,), in_specs=[pl.BlockSpec((n, tr, width), lambda i: (0, i, 0))],
        out_specs=pl.BlockSpec((tr, width), lambda i: (i, 0)), out_shape=jax.ShapeDtypeStruct((rows, width), F32),
        compiler_params=_params(("parallel",)),
    )(a)


def _adamw_update(w, g, m, v):
    nm = ADAM_B1 * m + (1.0 - ADAM_B1) * g
    nv = ADAM_B2 * v + (1.0 - ADAM_B2) * (g * g)
    m_hat = nm / (1.0 - ADAM_B1 ** ADAM_STEP)
    v_hat = nv / (1.0 - ADAM_B2 ** ADAM_STEP)
    return -ADAM_LR * (m_hat / (jnp.sqrt(v_hat) + ADAM_EPS) + ADAM_WD * w), nm, nv


def _adamw(w, g, m, v, name):
    rows, width = w.shape
    tr = _row_tile(rows, width, 7)

    def body(w_ref, g_ref, m_ref, v_ref, d_ref, nm_ref, nv_ref):
        d_ref[...], nm_ref[...], nv_ref[...] = _adamw_update(w_ref[...], g_ref[...], m_ref[...], v_ref[...])

    blk = pl.BlockSpec((tr, width), lambda i: (i, 0))
    return pl.pallas_call(
        body, name=name, grid=(rows // tr,), in_specs=[blk] * 4, out_specs=[blk] * 3,
        out_shape=[jax.ShapeDtypeStruct((rows, width), F32)] * 3, compiler_params=_params(("parallel",)),
    )(w, g, m, v)


def _adamw_halves(w3, m3, v3, mine, other, pos, name):
    depth, r, c = w3.shape
    assert depth == 2
    hr = r // 2
    tr = _row_tile(hr, c, 11)
    sources = ((0, True, mine[0]), (0, False, other[0]), (1, True, mine[1]), (1, False, other[1]))

    def active(l, h, core, layer, own):
        mine_half = h == core
        return (l == layer) & (mine_half if own else jnp.logical_not(mine_half))

    def body(pos_ref, w_ref, m_ref, v_ref, *rest):
        g_refs, (go_ref, d_ref, nm_ref, nv_ref) = rest[:4], rest[4:]
        l, h = pl.program_id(0), pl.program_id(1)
        for (layer, own, _), g_ref in zip(sources, g_refs):
            @pl.when(active(l, h, pos_ref[1], layer, own))
            def _():
                gv = g_ref[...]
                go_ref[...] = gv
                d_ref[...], nm_ref[...], nv_ref[...] = _adamw_update(w_ref[...], gv, m_ref[...], v_ref[...])

    def gspec(layer, own):
        return pl.BlockSpec((tr, c), lambda l, h, i, pos: (jnp.where(active(l, h, pos[1], layer, own), i, 0), 0))

    blk = pl.BlockSpec((None, None, tr, c), lambda l, h, i, pos: (l, h, i, 0))
    view = lambda a: a.reshape(depth, 2, hr, c)
    outs = pl.pallas_call(
        body, name=name,
        grid_spec=_scalar_grid((depth, 2, hr // tr), [blk] * 3 + [gspec(layer, own) for layer, own, _ in sources],
                               [blk] * 4),
        out_shape=[jax.ShapeDtypeStruct((depth, 2, hr, c), F32)] * 4,
        compiler_params=_params(("parallel", "parallel", "parallel")),
    )(pos, view(w3), view(m3), view(v3), *[s[2] for s in sources])
    return [o.reshape(w3.shape) for o in outs]


BIG = ("w_in", "w_pool_up", "w_conv_out", "w_attn_up", "w_o", "w_ff1", "w_ff2")
SMALL = ("norm_mix", "b_gate", "pool_mix", "pool_scale", "conv_w", "q_gain", "k_gain", "norm_mlp")
ORDER = ("norm_mix", "w_in", "b_gate", "pool_mix", "pool_scale", "conv_w", "q_gain", "k_gain", "w_pool_up",
         "w_conv_out", "w_attn_up", "w_o", "norm_mlp", "w_ff1", "w_ff2")
COLUMN_SHARDED = ("w_pool_up", "w_conv_out", "w_attn_up", "w_ff1")


def _matrix_weights(gathered):
    w = {}
    for name, g4 in gathered.items():
        if name in COLUMN_SHARDED:
            w[name] = g4
        else:
            w[name] = g4.reshape(N_CHIPS * g4.shape[1], g4.shape[2])
    return w


def _small_weights(l, small):
    w = {}
    w["norm_mix"] = small["norm_mix"][l][None]
    w["norm_mlp"] = small["norm_mlp"][l][None]
    w["b_gate"] = small["b_gate"][l][None]
    w["pool_mix"] = small["pool_mix"][l].astype(BF)
    w["pool_scale"] = small["pool_scale"][l][None]
    w["conv_w"] = jnp.pad(small["conv_w_full"][l], ((0, 5), (0, 0)))
    w["qk_gain"] = jnp.pad(jnp.stack([jnp.tile(small["q_gain"][l], 2), jnp.tile(small["k_gain"][l], 2)]), ((0, 6), (0, 0)))
    return w


def _to_chip_major(name, g):
    if name == "w_in":
        return g.T.reshape(N_CHIPS, g.shape[1] // N_CHIPS, g.shape[0])
    if name in COLUMN_SHARDED:
        return g
    return g.reshape(N_CHIPS, g.shape[0] // N_CHIPS, g.shape[1])


def _pad8(a):
    a = a.reshape(-1)
    return jnp.pad(a, (0, (-a.size) % (8 * LANES))).reshape(-1, LANES)


def kernel(x, norm_mix, w_in, b_gate, pool_mix, pool_scale, conv_w, q_gain, k_gain, w_pool_up, w_conv_out, w_attn_up, w_o, norm_mlp, w_ff1, w_ff2, loss_target, m_norm_mix, m_w_in, m_b_gate, m_pool_mix, m_pool_scale, m_conv_w, m_q_gain, m_k_gain, m_w_pool_up, m_w_conv_out, m_w_attn_up, m_w_o, m_norm_mlp, m_w_ff1, m_w_ff2, v_norm_mix, v_w_in, v_b_gate, v_pool_mix, v_pool_scale, v_conv_w, v_q_gain, v_k_gain, v_w_pool_up, v_w_conv_out, v_w_attn_up, v_w_o, v_norm_mlp, v_w_ff1, v_w_ff2):
    weights = dict(norm_mix=norm_mix, w_in=w_in, b_gate=b_gate, pool_mix=pool_mix, pool_scale=pool_scale, conv_w=conv_w,
                   q_gain=q_gain, k_gain=k_gain, w_pool_up=w_pool_up, w_conv_out=w_conv_out, w_attn_up=w_attn_up,
                   w_o=w_o, norm_mlp=norm_mlp, w_ff1=w_ff1, w_ff2=w_ff2)
    moms = dict(norm_mix=m_norm_mix, w_in=m_w_in, b_gate=m_b_gate, pool_mix=m_pool_mix, pool_scale=m_pool_scale,
                conv_w=m_conv_w, q_gain=m_q_gain, k_gain=m_k_gain, w_pool_up=m_w_pool_up, w_conv_out=m_w_conv_out,
                w_attn_up=m_w_attn_up, w_o=m_w_o, norm_mlp=m_norm_mlp, w_ff1=m_w_ff1, w_ff2=m_w_ff2)
    vels = dict(norm_mix=v_norm_mix, w_in=v_w_in, b_gate=v_b_gate, pool_mix=v_pool_mix, pool_scale=v_pool_scale,
                conv_w=v_conv_w, q_gain=v_q_gain, k_gain=v_k_gain, w_pool_up=v_w_pool_up, w_conv_out=v_w_conv_out,
                w_attn_up=v_w_attn_up, w_o=v_w_o, norm_mlp=v_norm_mlp, w_ff1=v_w_ff1, w_ff2=v_w_ff2)
    depth = norm_mix.shape[0]
    q = 2 * lax.axis_index("x") + lax.axis_index("y")
    pos = _position_operand()
    for group in (weights, moms, vels):
        group["w_in"] = jnp.swapaxes(group["w_in"], 1, 2)

    assert depth == 2, "the second layer's gather hides behind the first layer's forward, and likewise backward"
    first, rest = BIG[:1], BIG[1:]
    bufs = [{n: _cast_place(weights[n], l, pos, f"cast_{n}_l{l}") for n in BIG} for l in range(depth)]
    w_first = _matrix_weights(dict(zip(first, _gather([bufs[0][n] for n in first], "gather_l0_in"))))
    b_ssem, b_rsem, b_views, b_token = _gather_start([bufs[0][n] for n in rest], "gather_start_l0_rest")
    g_ssem, g_rsem, g_views, g_token = _gather_start([bufs[1][n] for n in BIG], "gather_start_l1")
    cw_all = _all_to_all_small(_pad8(jnp.pad(conv_w.reshape(-1), (0, (-conv_w.size) % LANES))))
    conv_w_full = jnp.concatenate(
        [cw_all[2 * p].reshape(-1)[:conv_w.size].reshape(conv_w.shape) for p in range(N_CHIPS)], axis=-1)
    small = dict(weights)
    small["conv_w_full"] = conv_w_full

    def late_weights(t):
        got = _gather_finish(b_ssem, b_rsem, b_views, t, "gather_wait_l0_rest", "gather_forward_l0_rest",
                             [bufs[0][n].shape for n in rest])
        return _matrix_weights(dict(zip(rest, got)))

    wl, saved = [None] * depth, [None] * depth
    h, saved[0], wl[0] = _layer_fwd(x[0], dict(_small_weights(0, small), **w_first), "l0", after=[b_token, g_token],
                                    late=late_weights)
    got = _gather_finish(g_ssem, g_rsem, g_views, h, "gather_wait_l1", "gather_forward_l1",
                         [bufs[1][n].shape for n in BIG])
    h, saved[1], wl[1] = _layer_fwd(h, dict(_small_weights(1, small), **_matrix_weights(dict(zip(BIG, got)))), "l1")
    dh, loss_row = _loss_grad(h, loss_target[0], "loss")

    def pair_stage(names, g, tag):
        views = [_halves(_to_chip_major(n, g[n])) for n in names]
        from_sibling = _pair_swap(views, f"grad_pair_swap_{tag}")
        return [_pair_sum(views[k], from_sibling[k], pos, f"pair_sum_{n}_{tag}") for k, n in enumerate(names)]

    mine, other = [{}, {}], [{}, {}]

    def finish(names, l, started, after, tag):
        ssem, rsem, parts, landing, _ = started
        parts, arrived = _chip_exchange_wait(ssem, rsem, parts, landing, after, f"grad_chip_exchange_wait_{tag}")
        got = [_chip_sum(parts[k], arrived[k], pos, f"chip_sum_{n}_{tag}") for k, n in enumerate(names)]
        mine[l].update(zip(names, got))
        other[l].update(zip(names, _pair_send(got, f"grad_pair_send_{tag}")))

    def small_pieces(g):
        return [_pad8(g[n][:3] if n == "conv_w" else g[n]) for n in SMALL]

    def start_small(l):
        return _all_to_all_small_start(jnp.concatenate(small_pieces(grads[l]), axis=0), f"small_grad_exchange_start_l{l}")

    grads, early, small = [None] * depth, {}, [None] * depth
    dh, grads[1] = _layer_bwd(dh, wl[1], saved[1], "l1")
    second = _chip_exchange_start(pair_stage(BIG, grads[1], "l1"), "grad_chip_exchange_start_l1")
    small[1] = start_small(1)

    def start_rest(g):
        early["rest"] = _chip_exchange_start(pair_stage(rest, g, "l0_rest"), "grad_chip_exchange_start_l0_rest")
        return early["rest"][4]

    def start_last(g):
        early["in"] = _chip_exchange_start(pair_stage(first, g, "l0_in"), "grad_chip_exchange_start_l0_in")
        return early["in"][4]

    dh, grads[0] = _layer_bwd(dh, wl[0], saved[0], "l0", after=[second[4], small[1][4]], mid=start_rest,
                              tail=start_last)
    small[0] = start_small(0)
    finish(BIG, 1, second, dh, "l1")
    finish(rest, 0, early["rest"], dh, "l0_rest")
    loss = lax.psum(loss_row[0, 0], ("x", "y", "c"))
    full = {}

    deltas, new_m, new_v = {}, {}, {}

    def update_matrix(n):
        full[n], deltas[n], new_m[n], new_v[n] = _adamw_halves(
            weights[n], moms[n], vels[n], [mine[l][n] for l in range(depth)], [other[l][n] for l in range(depth)], pos,
            f"adamw_{n}")

    for n in rest:
        update_matrix(n)
    finish(first, 0, early["in"], deltas[rest[-1]], "l0_in")
    for n in first:
        update_matrix(n)
    summed = []
    for l in range(depth):
        ssem, rsem, part, landing, _ = small[l]
        summed.append(_sum_slices(_all_to_all_small_wait(ssem, rsem, part, landing, deltas[first[-1]],
                                                         f"small_grad_exchange_wait_l{l}"), f"small_sum_l{l}"))
    row = 0
    for n, piece in zip(SMALL, small_pieces(grads[0])):
        size = (weights[n].size if n != "conv_w" else depth * 3 * 512) // depth
        flat = jnp.stack([s[row:row + piece.shape[0]].reshape(-1)[:size] for s in summed])
        row += piece.shape[0]
        if n == "conv_w":
            full[n] = lax.dynamic_slice_in_dim(flat.reshape(depth, 3, 512), q * conv_w.shape[2], conv_w.shape[2], axis=2)
        else:
            full[n] = flat.reshape(weights[n].shape)
    for n in SMALL:
        shape = weights[n].shape
        two_d = (-1, shape[-1]) if n not in ("conv_w", "q_gain", "k_gain") else (1, -1)
        d2, m2, v2 = _adamw(weights[n].reshape(two_d), full[n].reshape(two_d), moms[n].reshape(two_d),
                            vels[n].reshape(two_d), f"adamw_{n}")
        deltas[n], new_m[n], new_v[n] = d2.reshape(shape), m2.reshape(shape), v2.reshape(shape)
        full[n] = full[n].reshape(shape)
    for group in (full, deltas, new_m, new_v):
        group["w_in"] = jnp.swapaxes(group["w_in"], 1, 2)
    return (loss, dh[None], *[full[n] for n in ORDER], *[deltas[n] for n in ORDER], *[new_m[n] for n in ORDER],
            *[new_v[n] for n in ORDER])
```

```python
import functools

import jax
import jax.numpy as jnp
from jax import lax
from jax.experimental import pallas as pl
from jax.experimental.pallas import tpu as pltpu

F32 = jnp.float32
BF = jnp.bfloat16
MESH_ID = pl.DeviceIdType.MESH
ANY = pl.BlockSpec(memory_space=pl.ANY)

EPS = 1e-6
MASK_VALUE = -1e30
POOL_WINDOWS = (2, 4, 8, 16)
ATTN_DILATIONS = (1, 4, 16)
ATTN_BLOCK = 128
HEAD_DIM = 64
OFF_Q, OFF_K, OFF_V, OFF_GATE = 2048, 2816, 3584, 4352
N_CHIPS = 4
ADAM_LR, ADAM_B1, ADAM_B2, ADAM_EPS, ADAM_WD, ADAM_STEP = 0.001, 0.9, 0.999, 1e-08, 0.01, 10

VMEM_LIMIT = 48 * 1024 * 1024
LANES = 128

_DIMS = {"nn": (((1,), (0,)), ((), ())), "nt": (((1,), (1,)), ((), ())), "tn": (((0,), (0,)), ((), ()))}


def _params(sem):
    return pltpu.CompilerParams(dimension_semantics=sem, vmem_limit_bytes=VMEM_LIMIT)


def _pallas_call(body, **kw):
    call = pl.pallas_call(body, **kw)

    def run(*args):
        pinned = [pltpu.with_memory_space_constraint(a, pltpu.HBM)
                  if hasattr(a, "dtype") and jnp.issubdtype(a.dtype, jnp.floating) else a for a in args]
        return call(*pinned)

    return run


def _dot(a, b, mode="nn"):
    return lax.dot_general(a, b, _DIMS[mode], preferred_element_type=F32)


def _mm(a, b, mode, name, *, tm, tn, tk, out_dtype=F32, res=None, aux=None, epi=None, n_outer=False,
        b_shards=False, out_shards=False, after=None):
    if mode == "tn":
        K, M = a.shape
    else:
        M, K = a.shape
    if b_shards:
        if mode == "nn":
            assert b.shape[1] == K
            N = b.shape[2] * N_CHIPS
        else:
            assert mode == "nt"
            N = b.shape[1]
            assert b.shape[2] * N_CHIPS == K
    else:
        N = b.shape[0] if mode == "nt" else b.shape[1]
    tm, tn, tk = min(tm, M), min(tn, N), min(tk, K)
    assert M % tm == 0 and N % tn == 0 and K % tk == 0
    nk = K // tk
    if n_outer:
        grid = (N // tn, M // tm, nk)
        ij = lambda p, q_: (q_, p)
    else:
        grid = (M // tm, N // tn, nk)
        ij = lambda p, q_: (p, q_)

    def amap(p, q_, k):
        i, j = ij(p, q_)
        return (k, i) if mode == "tn" else (i, k)

    a_spec = pl.BlockSpec((tk, tm) if mode == "tn" else (tm, tk), amap)
    if b_shards:
        if mode == "nn":
            per = (N // N_CHIPS) // tn
            assert per >= 1 and (N // N_CHIPS) % tn == 0

            def bmap(p, q_, k):
                i, j = ij(p, q_)
                return (j // per, k, j % per)

            b_spec = pl.BlockSpec((None, tk, tn), bmap)
        else:
            per = (K // N_CHIPS) // tk
            assert per >= 1 and (K // N_CHIPS) % tk == 0

            def bmap(p, q_, k):
                i, j = ij(p, q_)
                return (k // per, j, k % per)

            b_spec = pl.BlockSpec((None, tn, tk), bmap)
    else:
        def bmap(p, q_, k):
            i, j = ij(p, q_)
            return (j, k) if mode == "nt" else (k, j)

        b_spec = pl.BlockSpec((tn, tk) if mode == "nt" else (tk, tn), bmap)

    def omap(p, q_, k):
        return ij(p, q_)

    o_spec = pl.BlockSpec((tm, tn), omap)
    if out_shards:
        per_o = (N // N_CHIPS) // tn
        assert per_o >= 1 and (N // N_CHIPS) % tn == 0

        def osmap(p, q_, k):
            i, j = ij(p, q_)
            return (j // per_o, i, j % per_o)

        out_spec0 = pl.BlockSpec((None, tm, tn), osmap)
        out_shape0 = jax.ShapeDtypeStruct((N_CHIPS, M, N // N_CHIPS), out_dtype)
    else:
        out_spec0 = o_spec
        out_shape0 = jax.ShapeDtypeStruct((M, N), out_dtype)

    in_specs = [a_spec, b_spec]
    args = [a, b]
    if res is not None:
        in_specs.append(o_spec)
        args.append(res)
    if aux is not None:
        in_specs.append(o_spec)
        args.append(aux)
    after = [] if after is None else list(after) if isinstance(after, (list, tuple)) else [after]
    in_specs += [ANY] * len(after)
    args += after
    out_specs = [out_spec0]
    out_shape = [out_shape0]
    n_out = len(out_shape)
    has_res, has_aux, n_after = res is not None, aux is not None, len(after)

    def body(*refs):
        a_ref, b_ref = refs[0], refs[1]
        pos = 2
        res_ref = aux_ref = None
        if has_res:
            res_ref = refs[pos]
            pos += 1
        if has_aux:
            aux_ref = refs[pos]
            pos += 1
        pos += n_after
        outs = refs[pos:pos + n_out]
        part = _dot(a_ref[...].astype(BF), b_ref[...].astype(BF), mode)

        def finish(acc):
            if res_ref is not None:
                acc = res_ref[...] + acc
            if epi == "relu2":
                r = jnp.maximum(acc, 0.0)
                outs[0][...] = (r * r).astype(out_dtype)
            elif epi == "drelu2":
                outs[0][...] = (acc * (2.0 * jnp.sqrt(aux_ref[...].astype(F32)))).astype(out_dtype)
            else:
                outs[0][...] = acc.astype(out_dtype)

        if nk == 1:
            finish(part)
        else:
            acc_ref = refs[pos + n_out]
            k = pl.program_id(2)

            @pl.when(k == 0)
            def _():
                acc_ref[...] = part

            @pl.when(k > 0)
            def _():
                acc_ref[...] += part

            @pl.when(k == nk - 1)
            def _():
                finish(acc_ref[...])

    scratch = [pltpu.VMEM((tm, tn), F32)] if nk > 1 else []
    out = _pallas_call(
        body, name=name, grid=grid, in_specs=in_specs, out_specs=out_specs, out_shape=out_shape,
        scratch_shapes=scratch, compiler_params=_params(("parallel", "parallel", "arbitrary")),
    )(*args)
    return out if n_out > 1 else out[0]


def _rms_fwd(x, gain, name, after=None):
    T, D = x.shape
    tm = min(512, T)

    def body(x_ref, g_ref, *rest):
        o_ref = rest[-1]
        xv = x_ref[...]
        r = lax.rsqrt(jnp.mean(xv * xv, axis=-1, keepdims=True) + EPS)
        o_ref[...] = ((xv * r) * g_ref[...]).astype(BF)

    extra = [] if after is None else list(after) if isinstance(after, (list, tuple)) else [after]
    return _pallas_call(
        body, name=name, grid=(T // tm,),
        in_specs=[pl.BlockSpec((tm, D), lambda i: (i, 0)), pl.BlockSpec((1, D), lambda i: (0, 0))] + [ANY] * len(extra),
        out_specs=pl.BlockSpec((tm, D), lambda i: (i, 0)), out_shape=jax.ShapeDtypeStruct((T, D), BF),
        compiler_params=_params(("parallel",)),
    )(x, gain, *extra)


def _rms_bwd(dh, x, gain, dres, name):
    T, D = x.shape
    tm = min(512, T)

    def body(dh_ref, x_ref, g_ref, dres_ref, dx_ref, dg_ref):
        xv = x_ref[...]
        r = lax.rsqrt(jnp.mean(xv * xv, axis=-1, keepdims=True) + EPS)
        xhat = xv * r
        dhv = dh_ref[...]
        dy = dhv * g_ref[...]
        dx_ref[...] = dres_ref[...] + r * (dy - xhat * jnp.mean(dy * xhat, axis=-1, keepdims=True))

        @pl.when(pl.program_id(0) == 0)
        def _():
            dg_ref[...] = jnp.zeros_like(dg_ref)

        dg_ref[...] += jnp.sum(dhv * xhat, axis=0, keepdims=True)

    row = pl.BlockSpec((tm, D), lambda i: (i, 0))
    vec = pl.BlockSpec((1, D), lambda i: (0, 0))
    return _pallas_call(
        body, name=name, grid=(T // tm,), in_specs=[row, row, vec, row], out_specs=[row, vec],
        out_shape=[jax.ShapeDtypeStruct((T, D), F32), jax.ShapeDtypeStruct((1, D), F32)],
        compiler_params=_params(("arbitrary",)),
    )(dh, x, gain, dres)


def _loss_grad(y, target, name):
    T, D = y.shape
    tm = min(512, T)

    def body(y_ref, t_ref, dy_ref, l_ref):
        e = y_ref[...] - t_ref[...]
        dy_ref[...] = e / float(D)

        @pl.when(pl.program_id(0) == 0)
        def _():
            l_ref[...] = jnp.zeros_like(l_ref)

        l_ref[...] += 0.5 * jnp.sum(jnp.mean(e * e, axis=-1, keepdims=True))

    row = pl.BlockSpec((tm, D), lambda i: (i, 0))
    return _pallas_call(
        body, name=name, grid=(T // tm,), in_specs=[row, row],
        out_specs=[row, pl.BlockSpec((1, LANES), lambda i: (0, 0))],
        out_shape=[jax.ShapeDtypeStruct((T, D), F32), jax.ShapeDtypeStruct((1, LANES), F32)],
        compiler_params=_params(("arbitrary",)),
    )(y, target)


POOL_HALO = 16
CONV_HALO = 8


def _causal_window_sum(v, w):
    s, sh = v, 1
    while sh < w:
        s = s + pltpu.roll(s, sh, 0)
        sh *= 2
    return s


def _anticausal_window_sum(v, w):
    n = v.shape[0]
    s, sh = v, 1
    while sh < w:
        s = s + pltpu.roll(s, n - sh, 0)
        sh *= 2
    return s


def _poolconv_fwd(z, pmix_b, pscale, convw, name):
    T = z.shape[0]
    R = min(512, T)
    PH, CH = R // POOL_HALO, R // CONV_HALO

    def body(u_ref, uh_ref, b_ref, c_ref, ch_ref, x_ref, xh_ref, mix_ref, sc_ref, cw_ref, yp_ref, yc_ref):
        i = pl.program_id(0)
        keep = (i > 0).astype(F32)
        row = i * R + lax.broadcasted_iota(jnp.int32, (R, 1), 0)
        w_all = jnp.concatenate([uh_ref[...] * keep, u_ref[...]], axis=0)
        for g, w in enumerate(POOL_WINDOWS):
            cols = slice(128 * g, 128 * (g + 1))
            wg = w_all[:, cols]
            s = _causal_window_sum(wg, w)[POOL_HALO:]
            cnt = jnp.minimum(row + 1, w).astype(F32)
            dgrp = s / cnt - wg[POOL_HALO:]
            y = _dot(dgrp.astype(BF), mix_ref[g]) * sc_ref[:, cols]
            yp_ref[:, cols] = y.astype(BF)
        uc = jnp.concatenate([ch_ref[...] * xh_ref[...] * keep, c_ref[...] * x_ref[...]], axis=0)
        yc = cw_ref[2:3, :] * uc + cw_ref[0:1, :] * pltpu.roll(uc, 2, 0) + cw_ref[1:2, :] * pltpu.roll(uc, 1, 0)
        yc_ref[...] = (b_ref[...] * yc[CONV_HALO:]).astype(BF)

    def main(cb):
        return pl.BlockSpec((R, 512), lambda i: (i, cb))

    def prev(cb, halo, per):
        return pl.BlockSpec((halo, 512), lambda i: (jnp.maximum(i * per - 1, 0), cb))

    full = lambda a: pl.BlockSpec(a.shape, lambda i: (0,) * a.ndim)
    return _pallas_call(
        body, name=name, grid=(T // R,),
        in_specs=[main(0), prev(0, POOL_HALO, PH), main(1), main(2), prev(2, CONV_HALO, CH), main(3),
                  prev(3, CONV_HALO, CH), full(pmix_b), full(pscale), full(convw)],
        out_specs=[pl.BlockSpec((R, 512), lambda i: (i, 0))] * 2,
        out_shape=[jax.ShapeDtypeStruct((T, 512), BF)] * 2,
        compiler_params=_params(("parallel",)),
    )(z, z, z, z, z, z, z, pmix_b, pscale, convw)


def _poolconv_bwd(z, dyp, dyc, pmix_b, pscale, convw, dz, name):
    T = z.shape[0]
    R = min(512, T)
    PH, CH = R // POOL_HALO, R // CONV_HALO
    nsteps = T // R

    def body(u_ref, uh_ref, b_ref, bn_ref, c_ref, ch_ref, x_ref, xh_ref, dyp_ref, dypn_ref, dyc_ref, dycn_ref,
             mix_ref, sc_ref, cw_ref, dz_in_ref, dz_ref, dmix_ref, dsc_ref, dcw_ref):
        i = pl.program_id(0)
        keep_prev = (i > 0).astype(F32)
        keep_next = (i < nsteps - 1).astype(F32)

        @pl.when(i == 0)
        def _():
            dmix_ref[...] = jnp.zeros_like(dmix_ref)
            dsc_ref[...] = jnp.zeros_like(dsc_ref)
            dcw_ref[...] = jnp.zeros_like(dcw_ref)

        row = i * R + lax.broadcasted_iota(jnp.int32, (R, 1), 0)
        row_ext = i * R + lax.broadcasted_iota(jnp.int32, (R + POOL_HALO, 1), 0)
        w_all = jnp.concatenate([uh_ref[...] * keep_prev, u_ref[...]], axis=0)
        dyp_ext = jnp.concatenate([dyp_ref[...], dypn_ref[...] * keep_next], axis=0)
        for g, w in enumerate(POOL_WINDOWS):
            cols = slice(128 * g, 128 * (g + 1))
            wg = w_all[:, cols]
            s = _causal_window_sum(wg, w)[POOL_HALO:]
            cnt = jnp.minimum(row + 1, w).astype(F32)
            dgrp = (s / cnt - wg[POOL_HALO:]).astype(BF)
            y_pre = _dot(dgrp, mix_ref[g])
            dsc_ref[:, cols] += jnp.sum(dyp_ref[:, cols] * y_pre, axis=0, keepdims=True)
            dyb = (dyp_ext[:, cols] * sc_ref[:, cols]).astype(BF)
            dmix_ref[cols, :] += _dot(dgrp, dyb[:R], "tn")
            dd = _dot(dyb, mix_ref[g], "nt")
            cnt_ext = jnp.minimum(row_ext + 1, w).astype(F32)
            e = _anticausal_window_sum(dd / cnt_ext, w)
            dz_ref[:, cols] = (e[:R] - dd[:R]).astype(BF)
        cw0, cw1, cw2 = cw_ref[0:1, :], cw_ref[1:2, :], cw_ref[2:3, :]
        uc = jnp.concatenate([ch_ref[...] * xh_ref[...] * keep_prev, c_ref[...] * x_ref[...]], axis=0)
        uc1 = pltpu.roll(uc, 1, 0)[CONV_HALO:]
        uc2 = pltpu.roll(uc, 2, 0)[CONV_HALO:]
        uc0 = uc[CONV_HALO:]
        yc = cw2 * uc0 + cw0 * uc2 + cw1 * uc1
        dycv = dyc_ref[...]
        dz_ref[:, 512:1024] = (dycv * yc).astype(BF)
        dv_ext = jnp.concatenate([dycv * b_ref[...], dycn_ref[...] * bn_ref[...] * keep_next], axis=0)
        n_ext = R + CONV_HALO
        duc = (cw2 * dv_ext + cw1 * pltpu.roll(dv_ext, n_ext - 1, 0) + cw0 * pltpu.roll(dv_ext, n_ext - 2, 0))[:R]
        dv = dv_ext[:R]
        dcw_ref[0:1, :] += jnp.sum(dv * uc2, axis=0, keepdims=True)
        dcw_ref[1:2, :] += jnp.sum(dv * uc1, axis=0, keepdims=True)
        dcw_ref[2:3, :] += jnp.sum(dv * uc0, axis=0, keepdims=True)
        dz_ref[:, 1024:1536] = (duc * x_ref[...]).astype(BF)
        dz_ref[:, 1536:2048] = (duc * c_ref[...]).astype(BF)

    def main(cb):
        return pl.BlockSpec((R, 512), lambda i: (i, cb))

    def prev(cb, halo, per):
        return pl.BlockSpec((halo, 512), lambda i: (jnp.maximum(i * per - 1, 0), cb))

    def nxt(cb, halo, per):
        return pl.BlockSpec((halo, 512), lambda i: (jnp.minimum((i + 1) * per, T // halo - 1), cb))

    full = lambda a: pl.BlockSpec(a.shape, lambda i: (0,) * a.ndim)
    return _pallas_call(
        body, name=name, grid=(nsteps,),
        in_specs=[main(0), prev(0, POOL_HALO, PH), main(1), nxt(1, CONV_HALO, CH), main(2), prev(2, CONV_HALO, CH),
                  main(3), prev(3, CONV_HALO, CH), main(0), nxt(0, POOL_HALO, PH), main(0), nxt(0, CONV_HALO, CH),
                  full(pmix_b), full(pscale), full(convw), ANY],
        out_specs=[pl.BlockSpec((R, 2048), lambda i: (i, 0)), pl.BlockSpec((512, 128), lambda i: (0, 0)),
                   pl.BlockSpec((1, 512), lambda i: (0, 0)), pl.BlockSpec((8, 512), lambda i: (0, 0))],
        out_shape=[jax.ShapeDtypeStruct(dz.shape, BF), jax.ShapeDtypeStruct((512, 128), F32),
                   jax.ShapeDtypeStruct((1, 512), F32), jax.ShapeDtypeStruct((8, 512), F32)],
        input_output_aliases={15: 0}, compiler_params=_params(("arbitrary",)),
    )(z, z, z, z, z, z, z, z, dyp, dyp, dyc, dyc, pmix_b, pscale, convw, dz)


def _head_sums(v):
    row = lax.broadcasted_iota(jnp.int32, (LANES, LANES), 0) < HEAD_DIM
    col = lax.broadcasted_iota(jnp.int32, (LANES, LANES), 1) < HEAD_DIM
    same_head = jnp.where(jnp.logical_xor(row, col), 0.0, 1.0).astype(BF)
    hi = v.astype(BF)
    lo = (v - hi.astype(F32)).astype(BF)
    return _dot(hi, same_head) + _dot(lo, same_head)


def _head_norm(x, g2, ma):
    r = lax.rsqrt(_head_sums(x * x) / HEAD_DIM + EPS)
    return x * r, r


def _head_norm_bwd(dy, xhat, r, g2, ma):
    dxh = dy * g2
    return r * (dxh - xhat * (_head_sums(dxh * xhat) / HEAD_DIM))


def _head_col(tile, hm):
    return jnp.max(jnp.where(hm, tile, -jnp.inf), axis=-1, keepdims=True)


def _attn_masks(other_block_exists):
    lane = lax.broadcasted_iota(jnp.int32, (2 * ATTN_BLOCK, ATTN_BLOCK), 1)
    qi = lax.broadcasted_iota(jnp.int32, (2 * ATTN_BLOCK, ATTN_BLOCK), 0) & (ATTN_BLOCK - 1)
    never = (1 - other_block_exists.astype(jnp.int32)) * (2 * ATTN_BLOCK)
    return lane[:ATTN_BLOCK] < HEAD_DIM, lane <= qi, lane >= qi + never


def _stack_heads(x, ma):
    return jnp.concatenate([jnp.where(ma, x, 0.0), jnp.where(ma, 0.0, x)], axis=0)


def _unstack_heads(y, ma):
    return jnp.where(ma, y[:ATTN_BLOCK], y[ATTN_BLOCK:])


def _stack_cols(tile, ma):
    return jnp.concatenate([_head_col(tile, ma), _head_col(tile, jnp.logical_not(ma))], axis=0)


def _qk_norm(z, gains, name):
    T = z.shape[0]
    tm = min(512, T)
    per_kind = (OFF_K - OFF_Q) // 256

    def body(x_ref, g_ref, o_ref):
        ma = lax.broadcasted_iota(jnp.int32, (tm, LANES), 1) < HEAD_DIM
        is_q = jnp.full((1, LANES), pl.program_id(1)) < per_kind
        g = jnp.where(is_q, g_ref[0:1, :], g_ref[1:2, :])
        for t in range(2):
            sl = slice(LANES * t, LANES * (t + 1))
            o_ref[:, sl] = _head_norm(x_ref[:, sl], g, ma)[0] * g

    return _pallas_call(
        body, name=name, grid=(T // tm, 2 * per_kind),
        in_specs=[pl.BlockSpec((tm, 256), lambda i, n: (i, OFF_Q // 256 + n)), pl.BlockSpec((8, LANES), lambda i, n: (0, 0))],
        out_specs=pl.BlockSpec((tm, 256), lambda i, n: (i, n)),
        out_shape=jax.ShapeDtypeStruct((T, 2 * (OFF_K - OFF_Q)), F32), compiler_params=_params(("parallel", "parallel")),
    )(z, gains)


ATTN_STEP_ROWS = 1024
ATTN_UNROLL = 2


def _attn_geometry(T, d):
    sub = ATTN_BLOCK * d
    nb = T // sub
    m = max(1, min(nb, ATTN_STEP_ROWS // sub))
    assert T % sub == 0 and nb % m == 0
    return sub, nb, m


def _attn_rows(jj, r, sub, d):
    start = jj * sub + r
    if d == 1:
        return pl.ds(pl.multiple_of(start, ATTN_BLOCK), ATTN_BLOCK)
    return pl.ds(start, ATTN_BLOCK, stride=d)


def _pick(flag, a, b):
    return jnp.where(jnp.full(a.shape, flag.astype(jnp.int32)) > 0, a, b)


def _attn_fwd(z, qkn, g, d, name):
    T = z.shape[0]
    sub, nb, m = _attn_geometry(T, d)
    scale = HEAD_DIM ** -0.5

    def body(q_ref, kc_ref, kp_ref, vc_ref, vp_ref, o_ref, lse_ref):
        jb = pl.program_id(0)

        def step(s, carry):
            jj, r = s // d, s % d
            here, before = _attn_rows(jj, r, sub, d), _attn_rows(jnp.maximum(jj - 1, 0), r, sub, d)
            edge = _attn_rows(0, r, sub, d)
            first = jj == 0
            ma, mask_c, mask_p = _attn_masks(jb * m + jj > 0)
            qs = _stack_heads(q_ref[here, :], ma).astype(BF)
            kcb = kc_ref[here, :].astype(BF)
            kpb = _pick(first, kp_ref[edge, :], kc_ref[before, :]).astype(BF)
            vcb = vc_ref[here, :].astype(BF)
            vpb = _pick(first, vp_ref[edge, :], vc_ref[before, :]).astype(BF)
            s_c = jnp.where(mask_c, _dot(qs, kcb, "nt") * scale, MASK_VALUE)
            s_p = jnp.where(mask_p, _dot(qs, kpb, "nt") * scale, MASK_VALUE)
            mx = jnp.maximum(jnp.max(s_c, axis=-1, keepdims=True), jnp.max(s_p, axis=-1, keepdims=True))
            p_c = jnp.exp(s_c - mx)
            p_p = jnp.exp(s_p - mx)
            den = jnp.sum(p_c, axis=-1, keepdims=True) + jnp.sum(p_p, axis=-1, keepdims=True)
            o = (_dot(p_c.astype(BF), vcb) + _dot(p_p.astype(BF), vpb)) / den
            o_ref[here, :] = _unstack_heads(o, ma)
            lse_ref[here, :] = _unstack_heads(jnp.broadcast_to(mx + jnp.log(den), o.shape), ma)
            return carry

        lax.fori_loop(0, m * d, step, 0, unroll=ATTN_UNROLL)

    def cur(col0):
        return pl.BlockSpec((m * sub, LANES), lambda j, t: (j, col0 + 2 * g + t))

    def prv(col0):
        return pl.BlockSpec((sub, LANES), lambda j, t: (jnp.maximum(j * m - 1, 0), col0 + 2 * g + t))

    k0, v0 = (OFF_K - OFF_Q) // LANES, OFF_V // LANES
    out = pl.BlockSpec((m * sub, LANES), lambda j, t: (j, t))
    return _pallas_call(
        body, name=name, grid=(nb // m, 2), in_specs=[cur(0), cur(k0), prv(k0), cur(v0), prv(v0)],
        out_specs=[out, out], out_shape=[jax.ShapeDtypeStruct((T, 256), F32)] * 2,
        compiler_params=_params(("parallel", "parallel")),
    )(qkn, qkn, qkn, z, z)


def _attn_bwd(z, qkn, do, c, lse, gains, g, d, name, after=None):
    T = z.shape[0]
    sub, nb, m = _attn_geometry(T, d)
    scale = HEAD_DIM ** -0.5
    extra = [] if after is None else [after]

    def body(qr_ref, kr_ref, vc_ref, vp_ref, qn_ref, qnn_ref, kn_ref, knp_ref, do_ref, don_ref, c_ref, cn_ref,
             lse_ref, lsen_ref, g_ref, *rest):
        dq_ref, dk_ref, dv_ref, dgq_ref, dgk_ref, sq_ref, sk_ref, sv_ref = rest[len(extra):]
        jb = pl.program_id(0)

        @pl.when((jb == 0) & (pl.program_id(1) == 0))
        def _():
            dgq_ref[...] = jnp.zeros_like(dgq_ref)
            dgk_ref[...] = jnp.zeros_like(dgk_ref)

        gq, gk = g_ref[0:1, :], g_ref[1:2, :]

        def step(s, carry):
            jj, r = s // d, s % d
            here, edge = _attn_rows(jj, r, sub, d), _attn_rows(0, r, sub, d)
            before = _attn_rows(jnp.maximum(jj - 1, 0), r, sub, d)
            behind = _attn_rows(jnp.minimum(jj + 1, m - 1), r, sub, d)
            first, last = jj == 0, jj == m - 1
            block = jb * m + jj
            ma, mask_c, mask_p = _attn_masks(block > 0)
            mask_n = _attn_masks(block < nb - 1)[2]
            qhat, rq = _head_norm(qr_ref[here, :], gq, ma)
            qn = qhat * gq
            qn_next = _pick(last, qnn_ref[edge, :], qn_ref[behind, :])
            khat, rk = _head_norm(kr_ref[here, :], gk, ma)
            kcb = (khat * gk).astype(BF)
            kpb = _pick(first, knp_ref[edge, :], kn_ref[before, :]).astype(BF)
            vcb = vc_ref[here, :].astype(BF)
            vpb = _pick(first, vp_ref[edge, :], vc_ref[before, :]).astype(BF)
            do_t, don_t = do_ref[here, :], _pick(last, don_ref[edge, :], do_ref[behind, :])
            c_t, cn_t = c_ref[here, :], _pick(last, cn_ref[edge, :], c_ref[behind, :])
            lse_t, lsen_t = lse_ref[here, :], _pick(last, lsen_ref[edge, :], lse_ref[behind, :])
            qs, dos = _stack_heads(qn, ma).astype(BF), _stack_heads(do_t, ma).astype(BF)
            lse_s, c_s = _stack_cols(lse_t, ma), _stack_cols(c_t, ma)
            s_c = jnp.where(mask_c, _dot(qs, kcb, "nt") * scale, MASK_VALUE)
            s_p = jnp.where(mask_p, _dot(qs, kpb, "nt") * scale, MASK_VALUE)
            p_c = jnp.exp(s_c - lse_s)
            p_p = jnp.exp(s_p - lse_s)
            ds_c = ((p_c * (_dot(dos, vcb, "nt") + c_s)) * scale).astype(BF)
            ds_p = ((p_p * (_dot(dos, vpb, "nt") + c_s)) * scale).astype(BF)
            dq_t = _unstack_heads(_dot(ds_c, kcb) + _dot(ds_p, kpb), ma)
            qs_n, dos_n = _stack_heads(qn_next, ma).astype(BF), _stack_heads(don_t, ma).astype(BF)
            s_n = jnp.where(mask_n, _dot(qs_n, kcb, "nt") * scale, MASK_VALUE)
            p_n = jnp.exp(s_n - _stack_cols(lsen_t, ma))
            ds_n = ((p_n * (_dot(dos_n, vcb, "nt") + _stack_cols(cn_t, ma))) * scale).astype(BF)
            dv_t = _dot(p_c.astype(BF), dos, "tn") + _dot(p_n.astype(BF), dos_n, "tn")
            dk_t = _dot(ds_c, qs, "tn") + _dot(ds_n, qs_n, "tn")
            sq_ref[here, :] = _head_norm_bwd(dq_t, qhat, rq, gq, ma)
            sk_ref[here, :] = _head_norm_bwd(dk_t, khat, rk, gk, ma)
            sv_ref[here, :] = dv_t
            dgq_ref[...] += jnp.sum(dq_t * qhat, axis=0, keepdims=True)
            dgk_ref[...] += jnp.sum(dk_t * khat, axis=0, keepdims=True)
            return carry

        lax.fori_loop(0, m * d, step, 0, unroll=ATTN_UNROLL)
        dq_ref[...] = sq_ref[...].astype(BF)
        dk_ref[...] = sk_ref[...].astype(BF)
        dv_ref[...] = sv_ref[...].astype(BF)

    def cur(col0):
        return pl.BlockSpec((m * sub, LANES), lambda j, t: (j, col0 + 2 * g + t))

    def prv(col0):
        return pl.BlockSpec((sub, LANES), lambda j, t: (jnp.maximum(j * m - 1, 0), col0 + 2 * g + t))

    def nxt(col0):
        return pl.BlockSpec((sub, LANES), lambda j, t: (jnp.minimum((j + 1) * m, nb - 1), col0 + 2 * g + t))

    own = pl.BlockSpec((m * sub, LANES), lambda j, t: (j, t))
    own_next = pl.BlockSpec((sub, LANES), lambda j, t: (jnp.minimum((j + 1) * m, nb - 1), t))
    vec = pl.BlockSpec((1, LANES), lambda j, t: (0, 0))
    zq, zk, zv, k0 = OFF_Q // LANES, OFF_K // LANES, OFF_V // LANES, (OFF_K - OFF_Q) // LANES
    return _pallas_call(
        body, name=name, grid=(nb // m, 2),
        in_specs=[cur(zq), cur(zk), cur(zv), prv(zv), cur(0), nxt(0), cur(k0), prv(k0), own, own_next, own, own_next,
                  own, own_next, pl.BlockSpec((8, LANES), lambda j, t: (0, 0))] + [ANY] * len(extra),
        out_specs=[own, own, own, vec, vec],
        out_shape=[jax.ShapeDtypeStruct((T, 256), BF)] * 3 + [jax.ShapeDtypeStruct((1, LANES), F32)] * 2,
        scratch_shapes=[pltpu.VMEM((m * sub, LANES), F32)] * 3,
        compiler_params=_params(("arbitrary", "arbitrary")),
    )(z, z, z, z, qkn, qkn, qkn, qkn, do, do, c, c, lse, lse, gains, *extra)


MERGE_ROWS = 256
GATE_TILE = 256


def _group_mix(o_refs, lse_refs):
    lses = [r[...] for r in lse_refs]
    m = jnp.maximum(jnp.maximum(lses[0], lses[1]), lses[2])
    es = [jnp.exp(l - m) for l in lses]
    den = es[0] + es[1] + es[2]
    ws = [e / den for e in es]
    y = ws[0] * o_refs[0][...] + ws[1] * o_refs[1][...] + ws[2] * o_refs[2][...]
    return ws, y


def _sigmoid(v):
    return 1.0 / (1.0 + jnp.exp(-v))


def _merge_specs(T, z, bgate, gpu, gco, gau):
    tm = min(MERGE_ROWS, T)
    row = lambda w: pl.BlockSpec((tm, w), lambda i: (i, 0))
    gate0 = OFF_GATE // GATE_TILE
    gates = [pl.BlockSpec((tm, GATE_TILE), functools.partial(lambda i, cb: (i, cb), cb=gate0 + n))
             for n in range(3 * N_CHIPS)]
    full = lambda a: pl.BlockSpec(a.shape, lambda i: (0,) * a.ndim)
    specs = [row(512), row(512)] + [row(256)] * 6 + gates + [full(bgate), full(gpu), full(gco), full(gau)]
    return tm, row, specs


def _merge_fwd(yp, yc, o3, lse3, z, bgate, gpu, gco, gau, name):
    T = yp.shape[0]
    tm, row, specs = _merge_specs(T, z, bgate, gpu, gco, gau)

    def body(*refs):
        yp_ref, yc_ref = refs[0], refs[1]
        o_refs, lse_refs = refs[2:5], refs[5:8]
        zg = refs[8:20]
        b_ref, gpu_ref, gco_ref, gau_ref, out_ref = refs[20:25]
        yab = _group_mix(o_refs, lse_refs)[1].astype(BF)
        ys = (yp_ref[...], yc_ref[...], yab)
        ups = (gpu_ref, gco_ref, gau_ref)
        for n in range(N_CHIPS):
            acc = None
            for b in range(3):
                gcol = slice(1024 * b + GATE_TILE * n, 1024 * b + GATE_TILE * (n + 1))
                gate = _sigmoid(zg[N_CHIPS * b + n][...] + b_ref[:, gcol])
                term = gate * _dot(ys[b], ups[b][n])
                acc = term if acc is None else acc + term
            out_ref[:, GATE_TILE * n:GATE_TILE * (n + 1)] = acc.astype(BF)

    return _pallas_call(
        body, name=name, grid=(T // tm,), in_specs=specs, out_specs=row(1024),
        out_shape=jax.ShapeDtypeStruct((T, 1024), BF), compiler_params=_params(("parallel",)),
    )(yp, yc, *o3, *lse3, *([z] * 12), bgate, gpu, gco, gau)


def _merge_bwd(dm, yp, yc, o3, lse3, z, bgate, gpu, gco, gau, name):
    T = yp.shape[0]
    tm, row, specs = _merge_specs(T, z, bgate, gpu, gco, gau)
    nsteps = T // tm

    def body(*refs):
        dm_ref, yp_ref, yc_ref = refs[0:3]
        o_refs, lse_refs = refs[3:6], refs[6:9]
        zg = refs[9:21]
        b_ref, gpu_ref, gco_ref, gau_ref = refs[21:25]
        dzg_ref, dyp_ref, dyc_ref = refs[25:28]
        do_refs, c_refs = refs[28:31], refs[31:34]
        dgpu_ref, dgco_ref, dgau_ref, dbg_ref = refs[34:38]
        accs = refs[38:41]
        i = pl.program_id(0)

        @pl.when(i == 0)
        def _():
            for a in accs:
                a[...] = jnp.zeros_like(a)
            dbg_ref[...] = jnp.zeros_like(dbg_ref)

        ws, y = _group_mix(o_refs, lse_refs)
        ys = (yp_ref[...], yc_ref[...], y.astype(BF))
        ups = (gpu_ref, gco_ref, gau_ref)
        dys = [None, None, None]
        for n in range(N_CHIPS):
            dmn = dm_ref[:, GATE_TILE * n:GATE_TILE * (n + 1)]
            for b in range(3):
                gcol = slice(1024 * b + GATE_TILE * n, 1024 * b + GATE_TILE * (n + 1))
                gate = _sigmoid(zg[N_CHIPS * b + n][...] + b_ref[:, gcol])
                up = _dot(ys[b], ups[b][n])
                dzg = (dmn * up) * (gate * (1.0 - gate))
                dzg_ref[:, gcol] = dzg.astype(BF)
                dbg_ref[:, gcol] += jnp.sum(dzg, axis=0, keepdims=True)
                dup = (dmn * gate).astype(BF)
                accs[b][n] += _dot(ys[b], dup, "tn")
                dyb = _dot(dup, ups[b][n], "nt")
                dys[b] = dyb if dys[b] is None else dys[b] + dyb
        dyp_ref[...] = dys[0]
        dyc_ref[...] = dys[1]
        dya = dys[2]
        lane = lax.broadcasted_iota(jnp.int32, dya.shape, 1) // HEAD_DIM
        pr = dya * y
        rho = jnp.zeros_like(pr)
        for h in range(256 // HEAD_DIM):
            hm = lane == h
            rho = jnp.where(hm, jnp.sum(jnp.where(hm, pr, 0.0), axis=-1, keepdims=True), rho)
        for g in range(3):
            do_refs[g][...] = ws[g] * dya
            c_refs[g][...] = -(ws[g] * rho)

        @pl.when(i == nsteps - 1)
        def _():
            dgpu_ref[...] = accs[0][...].astype(BF)
            dgco_ref[...] = accs[1][...].astype(BF)
            dgau_ref[...] = accs[2][...].astype(BF)

    full = lambda a: pl.BlockSpec(a.shape, lambda i: (0,) * a.ndim)
    dz_gate = pl.BlockSpec((pl.Element(tm), pl.Element(3072)), lambda i: (i * tm, OFF_GATE))
    out_specs = ([dz_gate, row(512), row(512)] + [row(256)] * 6 + [full(gpu), full(gco), full(gau)]
                 + [pl.BlockSpec((1, 3072), lambda i: (0, 0))])
    out_shape = ([jax.ShapeDtypeStruct(z.shape, BF)] + [jax.ShapeDtypeStruct((T, 512), F32)] * 2
                 + [jax.ShapeDtypeStruct((T, 256), F32)] * 6
                 + [jax.ShapeDtypeStruct(g.shape, BF) for g in (gpu, gco, gau)]
                 + [jax.ShapeDtypeStruct((1, 3072), F32)])
    return _pallas_call(
        body, name=name, grid=(nsteps,), in_specs=[row(1024)] + specs, out_specs=out_specs, out_shape=out_shape,
        scratch_shapes=[pltpu.VMEM(g.shape, F32) for g in (gpu, gco, gau)],
        compiler_params=_params(("arbitrary",)),
    )(dm, yp, yc, *o3, *lse3, *([z] * 12), bgate, gpu, gco, gau)


def _layer_fwd(x, w, tag, after=None, late=None):
    hb = _rms_fwd(x, w["norm_mix"], f"rms_mix_{tag}", after=after)
    z = _mm(hb, w["w_in"], "nt", f"in_proj_{tag}", tm=512, tn=3712, tk=1024, n_outer=True)
    yp, yc = _poolconv_fwd(z, w["pool_mix"], w["pool_scale"], w["conv_w"], f"poolconv_{tag}")
    qkn = _qk_norm(z, w["qk_gain"], f"qk_norm_{tag}")
    o3, lse3 = [], []
    for g, d in enumerate(ATTN_DILATIONS):
        o, lse = _attn_fwd(z, qkn, g, d, f"attn{g}_{tag}")
        o3.append(o)
        lse3.append(lse)
    if late is not None:
        w = dict(w, **late(lse3[-1]))
    merged = _merge_fwd(yp, yc, o3, lse3, z, w["b_gate"], w["w_pool_up"], w["w_conv_out"], w["w_attn_up"],
                        f"merge_{tag}")
    x1 = _mm(merged, w["w_o"], "nn", f"out_proj_{tag}", tm=1024, tn=1024, tk=1024, res=x)
    h2b = _rms_fwd(x1, w["norm_mlp"], f"rms_mlp_{tag}")
    rb = _mm(h2b, w["w_ff1"], "nn", f"ff1_{tag}", tm=1024, tn=1024, tk=1024, out_dtype=BF, epi="relu2", n_outer=True,
             b_shards=True)
    x2 = _mm(rb, w["w_ff2"], "nn", f"ff2_{tag}", tm=512, tn=1024, tk=4096, res=x1)
    saved = dict(x=x, hb=hb, z=z, yp=yp, yc=yc, qkn=qkn, o3=o3, lse3=lse3, merged=merged, x1=x1, h2b=h2b, rb=rb)
    return x2, saved, w


def _layer_bwd(dx2, w, s, tag, after=None, mid=None, tail=None):
    g = {}
    dab = _mm(dx2, w["w_ff2"], "nt", f"d_ff2_act_{tag}", tm=1024, tn=1024, tk=1024, out_dtype=BF, aux=s["rb"],
              epi="drelu2", after=after)
    g["w_ff2"] = _mm(s["rb"], dx2, "tn", f"d_ff2_w_{tag}", tm=1024, tn=1024, tk=2048, out_dtype=BF)
    g["w_ff1"] = _mm(s["h2b"], dab, "tn", f"d_ff1_w_{tag}", tm=1024, tn=1024, tk=2048, out_dtype=BF, out_shards=True)
    dh2 = _mm(dab, w["w_ff1"], "nt", f"d_ff1_act_{tag}", tm=1024, tn=1024, tk=1024, b_shards=True)
    dx1, g["norm_mlp"] = _rms_bwd(dh2, s["x1"], w["norm_mlp"], dx2, f"d_rms_mlp_{tag}")
    dm = _mm(dx1, w["w_o"], "nt", f"d_out_act_{tag}", tm=1024, tn=1024, tk=1024)
    g["w_o"] = _mm(s["merged"], dx1, "tn", f"d_out_w_{tag}", tm=1024, tn=1024, tk=1024, out_dtype=BF)
    (dz, dyp, dyc, do0, do1, do2, c0, c1, c2, g["w_pool_up"], g["w_conv_out"], g["w_attn_up"],
     g["b_gate"]) = _merge_bwd(dm, s["yp"], s["yc"], s["o3"], s["lse3"], s["z"], w["b_gate"], w["w_pool_up"],
                               w["w_conv_out"], w["w_attn_up"], f"d_merge_{tag}")
    behind = mid(g) if mid is not None else None
    dq, dk, dv = [], [], []
    dgq = dgk = None
    for gi, d in enumerate(ATTN_DILATIONS):
        dzq, dzk, dzv, pq, pk = _attn_bwd(s["z"], s["qkn"], (do0, do1, do2)[gi], (c0, c1, c2)[gi], s["lse3"][gi],
                                          w["qk_gain"], gi, d, f"d_attn{gi}_{tag}", after=behind)
        dq.append(dzq)
        dk.append(dzk)
        dv.append(dzv)
        dgq = pq if dgq is None else dgq + pq
        dgk = pk if dgk is None else dgk + pk
    g["q_gain"] = dgq[:, :HEAD_DIM] + dgq[:, HEAD_DIM:]
    g["k_gain"] = dgk[:, :HEAD_DIM] + dgk[:, HEAD_DIM:]
    for off, pieces in ((OFF_Q, dq), (OFF_K, dk), (OFF_V, dv)):
        for gi, piece in enumerate(pieces):
            dz = lax.dynamic_update_slice(dz, piece, (0, off + 256 * gi))
    dz, g["pool_mix"], g["pool_scale"], g["conv_w"] = _poolconv_bwd(
        s["z"], dyp, dyc, w["pool_mix"], w["pool_scale"], w["conv_w"], dz, f"d_poolconv_{tag}")
    g["w_in"] = _mm(s["hb"], dz, "tn", f"d_in_w_{tag}", tm=512, tn=3712, tk=1024, out_dtype=BF)
    dh = _mm(dz, w["w_in"], "nn", f"d_in_act_{tag}", tm=1024, tn=1024, tk=3712,
             after=tail(g) if tail is not None else None)
    dx, g["norm_mix"] = _rms_bwd(dh, s["x"], w["norm_mix"], dx1, f"d_rms_mix_{tag}")
    return dx, g


def _position():
    x, y, c = lax.axis_index("x"), lax.axis_index("y"), lax.axis_index("c")
    chips = [(1 - x, y), (x, 1 - y), (1 - x, 1 - y)]
    return x, y, c, 2 * x + y, chips, [2 * cx + cy for cx, cy in chips]


def _remote(src, dst, ssem, rsem, dev):
    return pltpu.make_async_remote_copy(src_ref=src, dst_ref=dst, send_sem=ssem, recv_sem=rsem, device_id=dev,
                                        device_id_type=MESH_ID)


def _position_operand():
    x, y, c = lax.axis_index("x"), lax.axis_index("y"), lax.axis_index("c")
    return jnp.stack([2 * x + y, c]).astype(jnp.int32)


def _halves(a):
    return a.reshape(a.shape[0], 2, a.shape[1] // 2, a.shape[2])


def _gather(bufs, name, after):
    n = len(bufs)
    views = [_halves(b) for b in bufs]

    def body(*refs):
        outs = refs[n + 1:2 * n + 1]
        ssem, rsem, fssem, frsem = refs[2 * n + 1:]
        x, y, c, q, chips, qs = _position()
        sib = (x, y, 1 - c)
        sent = []
        for k in range(n):
            mine = outs[k].at[q, c]
            for j, chip in enumerate(chips):
                cp = _remote(mine, mine, ssem.at[k, j], rsem.at[k, j], (chip[0], chip[1], c))
                cp.start()
                sent.append(cp)
        for k in range(n):
            for j, chip in enumerate(chips):
                slot = outs[k].at[qs[j], c]
                _remote(slot, slot, ssem.at[k, j], rsem.at[k, j], (chip[0], chip[1], c)).wait_recv()
                cp = _remote(slot, slot, fssem.at[k, j], frsem.at[k, j], sib)
                cp.start()
                sent.append(cp)
        for k in range(n):
            for j in range(3):
                slot = outs[k].at[qs[j], 1 - c]
                _remote(slot, slot, fssem.at[k, j], frsem.at[k, j], sib).wait_recv()
        for cp in sent:
            cp.wait_send()

    outs = _pallas_call(
        body, name=name, in_specs=[ANY] * (n + 1), out_specs=[ANY] * n,
        out_shape=[jax.ShapeDtypeStruct(v.shape, v.dtype) for v in views],
        input_output_aliases={k: k for k in range(n)},
        scratch_shapes=[pltpu.SemaphoreType.DMA((n, 3))] * 4,
    )(*views, after)
    return [o.reshape(b.shape) for o, b in zip(outs, bufs)]


SEM = pl.BlockSpec(memory_space=pltpu.SEMAPHORE)
TOKEN = jax.ShapeDtypeStruct((8, LANES), F32)
TOKEN_SPEC = pl.BlockSpec(memory_space=pltpu.VMEM)


def _split_params():
    return pltpu.CompilerParams(has_side_effects=pltpu.SideEffectType.DATAFLOW_SIDE_EFFECTING)


def _gather_start(bufs, name):
    n = len(bufs)
    views = [_halves(b) for b in bufs]

    def body(*refs):
        ssem, rsem = refs[n:n + ns], refs[n + ns:n + 2 * ns]
        outs, token = refs[n + 2 * ns:2 * n + 2 * ns], refs[2 * n + 2 * ns]
        x, y, c, q, chips, qs = _position()
        for k in range(n):
            mine = outs[k].at[q, c]
            for j, chip in enumerate(chips):
                _remote(mine, mine, ssem[3 * k + j], rsem[3 * k + j], (chip[0], chip[1], c)).start()
        token[...] = jnp.zeros_like(token)

    ns = 3 * n
    outs = _pallas_call(
        body, name=name, in_specs=[ANY] * n, out_specs=[SEM] * (2 * ns) + [ANY] * n + [TOKEN_SPEC],
        out_shape=[pltpu.SemaphoreType.DMA(())] * (2 * ns) + [jax.ShapeDtypeStruct(v.shape, v.dtype) for v in views]
        + [TOKEN],
        input_output_aliases={k: k + 2 * ns for k in range(n)}, compiler_params=_split_params(),
    )(*views)
    return list(outs[:ns]), list(outs[ns:2 * ns]), list(outs[2 * ns:2 * ns + n]), outs[2 * ns + n]


def _gather_finish(ssem, rsem, views, after, name_wait, name_forward, shapes):
    n = len(views)
    ns = len(ssem)

    def wait_body(*refs):
        ssem_ref, rsem_ref = refs[n:n + ns], refs[n + ns:n + 2 * ns]
        outs = refs[n + 2 * ns + 1:]
        x, y, c, q, chips, qs = _position()
        for k in range(n):
            for j, chip in enumerate(chips):
                cp = _remote(outs[k].at[q, c], outs[k].at[qs[j], c], ssem_ref[3 * k + j], rsem_ref[3 * k + j],
                             (chip[0], chip[1], c))
                cp.wait_send()
                cp.wait_recv()

    landed = _pallas_call(
        wait_body, name=name_wait, in_specs=[ANY] * n + [SEM] * (2 * ns) + [ANY], out_specs=[ANY] * n,
        out_shape=[jax.ShapeDtypeStruct(v.shape, v.dtype) for v in views],
        input_output_aliases={k: k for k in range(n)}, compiler_params=_split_params(),
    )(*views, *ssem, *rsem, after)

    def forward_body(*refs):
        outs = refs[n:2 * n]
        fssem, frsem = refs[2 * n:]
        x, y, c, q, chips, qs = _position()
        sib = (x, y, 1 - c)
        sent = []
        for k in range(n):
            for j in range(3):
                slot = outs[k].at[qs[j], c]
                cp = _remote(slot, slot, fssem.at[k, j], frsem.at[k, j], sib)
                cp.start()
                sent.append(cp)
        for k in range(n):
            for j in range(3):
                slot = outs[k].at[qs[j], 1 - c]
                _remote(slot, slot, fssem.at[k, j], frsem.at[k, j], sib).wait_recv()
        for cp in sent:
            cp.wait_send()

    outs = _pallas_call(
        forward_body, name=name_forward, in_specs=[ANY] * n, out_specs=[ANY] * n,
        out_shape=[jax.ShapeDtypeStruct(v.shape, v.dtype) for v in views],
        input_output_aliases={k: k for k in range(n)}, scratch_shapes=[pltpu.SemaphoreType.DMA((n, 3))] * 2,
    )(*landed)
    return [o.reshape(s) for o, s in zip(outs, shapes)]


def _chip_exchange_start(parts, name):
    n = len(parts)

    def body(*refs):
        ssem, rsem = refs[n:n + ns], refs[n + ns:n + 2 * ns]
        base = n + 2 * ns
        srcs, outs, token = refs[base:base + n], refs[base + n:base + 2 * n], refs[base + 2 * n]
        x, y, c, q, chips, qs = _position()
        for k in range(n):
            for j, chip in enumerate(chips):
                _remote(srcs[k].at[qs[j]], outs[k].at[j], ssem[3 * k + j], rsem[3 * k + j],
                        (chip[0], chip[1], c)).start()
        token[...] = jnp.zeros_like(token)

    ns = 3 * n
    outs = _pallas_call(
        body, name=name, in_specs=[ANY] * n, out_specs=[SEM] * (2 * ns) + [ANY] * (2 * n) + [TOKEN_SPEC],
        out_shape=[pltpu.SemaphoreType.DMA(())] * (2 * ns) + [jax.ShapeDtypeStruct(a.shape, a.dtype) for a in parts]
        + [jax.ShapeDtypeStruct((3,) + a.shape[1:], a.dtype) for a in parts] + [TOKEN],
        input_output_aliases={k: k + 2 * ns for k in range(n)}, compiler_params=_split_params(),
    )(*parts)
    b = 2 * ns
    return list(outs[:ns]), list(outs[ns:b]), list(outs[b:b + n]), list(outs[b + n:b + 2 * n]), outs[b + 2 * n]


def _chip_exchange_wait(ssem, rsem, parts, landing, after, name):
    n = len(parts)
    ns = len(ssem)

    def body(*refs):
        ssem_ref, rsem_ref = refs[2 * n:2 * n + ns], refs[2 * n + ns:2 * n + 2 * ns]
        base = 2 * n + 2 * ns + 1
        srcs, outs = refs[base:base + n], refs[base + n:]
        x, y, c, q, chips, qs = _position()
        for k in range(n):
            for j, chip in enumerate(chips):
                cp = _remote(srcs[k].at[qs[j]], outs[k].at[j], ssem_ref[3 * k + j], rsem_ref[3 * k + j],
                             (chip[0], chip[1], c))
                cp.wait_send()
                cp.wait_recv()

    outs = _pallas_call(
        body, name=name, in_specs=[ANY] * (2 * n) + [SEM] * (2 * ns) + [ANY], out_specs=[ANY] * (2 * n),
        out_shape=[jax.ShapeDtypeStruct(a.shape, a.dtype) for a in list(parts) + list(landing)],
        input_output_aliases={k: k for k in range(2 * n)}, compiler_params=_split_params(),
    )(*parts, *landing, *ssem, *rsem, after)
    return list(outs[:n]), list(outs[n:])


def _pair_swap(views, name):
    n = len(views)

    def body(*refs):
        ins, outs = refs[:n], refs[n:2 * n]
        ssem, rsem = refs[2 * n:]
        x, y, c, _, _, _ = _position()
        cps = [_remote(ins[k].at[pl.ds(0, N_CHIPS), 1 - c], outs[k], ssem.at[k], rsem.at[k], (x, y, 1 - c))
               for k in range(n)]
        for cp in cps:
            cp.start()
        for cp in cps:
            cp.wait()

    return _pallas_call(
        body, name=name, in_specs=[ANY] * n, out_specs=[ANY] * n,
        out_shape=[jax.ShapeDtypeStruct((v.shape[0],) + v.shape[2:], v.dtype) for v in views],
        scratch_shapes=[pltpu.SemaphoreType.DMA((n,))] * 2,
    )(*views)


def _chip_exchange(parts, name):
    n = len(parts)

    def body(*refs):
        ins, outs = refs[:n], refs[n:2 * n]
        ssem, rsem = refs[2 * n:]
        x, y, c, q, chips, qs = _position()
        cps = []
        for k in range(n):
            for j, chip in enumerate(chips):
                cp = _remote(ins[k].at[qs[j]], outs[k].at[j], ssem.at[k, j], rsem.at[k, j], (chip[0], chip[1], c))
                cp.start()
                cps.append(cp)
        for cp in cps:
            cp.wait_recv()
        for cp in cps:
            cp.wait_send()

    return _pallas_call(
        body, name=name, in_specs=[ANY] * n, out_specs=[ANY] * n,
        out_shape=[jax.ShapeDtypeStruct((3,) + a.shape[1:], a.dtype) for a in parts],
        scratch_shapes=[pltpu.SemaphoreType.DMA((n, 3))] * 2,
    )(*parts)


def _pair_send(arrays, name):
    n = len(arrays)

    def body(*refs):
        ins, outs = refs[:n], refs[n:2 * n]
        ssem, rsem = refs[2 * n:]
        x, y, c, _, _, _ = _position()
        cps = [_remote(ins[k], outs[k], ssem.at[k], rsem.at[k], (x, y, 1 - c)) for k in range(n)]
        for cp in cps:
            cp.start()
        for cp in cps:
            cp.wait()

    return _pallas_call(
        body, name=name, in_specs=[ANY] * n, out_specs=[ANY] * n,
        out_shape=[jax.ShapeDtypeStruct(a.shape, a.dtype) for a in arrays],
        scratch_shapes=[pltpu.SemaphoreType.DMA((n,))] * 2,
    )(*arrays)


def _all_to_all_small(part):
    P = part.shape[0]

    def body(in_ref, out_ref, lsem, ssem, rsem):
        x, y, c = lax.axis_index("x"), lax.axis_index("y"), lax.axis_index("c")
        me = 4 * x + 2 * y + c
        flips = [(fx, fy, fc) for fx in (0, 1) for fy in (0, 1) for fc in (0, 1)][1:]
        peers = [((x + fx) % 2, (y + fy) % 2, (c + fc) % 2) for fx, fy, fc in flips]
        loc = pltpu.make_async_copy(in_ref, out_ref.at[me], lsem)
        loc.start()
        cps = [_remote(in_ref, out_ref.at[me], ssem.at[j], rsem.at[j], peer) for j, peer in enumerate(peers)]
        for cp in cps:
            cp.start()
        for j, (px, py, pc) in enumerate(peers):
            _remote(in_ref, out_ref.at[4 * px + 2 * py + pc], ssem.at[j], rsem.at[j], peers[j]).wait_recv()
        for cp in cps:
            cp.wait_send()
        loc.wait()

    return _pallas_call(
        body, name="small_exchange", in_specs=[ANY], out_specs=ANY,
        out_shape=jax.ShapeDtypeStruct((8, P, LANES), F32),
        scratch_shapes=[pltpu.SemaphoreType.DMA(())] + [pltpu.SemaphoreType.DMA((7,))] * 2,
    )(part)


def _small_peers():
    x, y, c = lax.axis_index("x"), lax.axis_index("y"), lax.axis_index("c")
    flips = [(fx, fy, fc) for fx in (0, 1) for fy in (0, 1) for fc in (0, 1)][1:]
    peers = [((x + fx) % 2, (y + fy) % 2, (c + fc) % 2) for fx, fy, fc in flips]
    return 4 * x + 2 * y + c, peers


def _all_to_all_small_start(part, name):
    P = part.shape[0]
    me = 4 * lax.axis_index("x") + 2 * lax.axis_index("y") + lax.axis_index("c")
    landing = lax.dynamic_update_slice(jnp.zeros((8, P, LANES), F32), part[None], (me, 0, 0))

    def body(*refs):
        sems, src, land, token = refs[2:16], refs[16], refs[17], refs[18]
        me_, peers = _small_peers()
        for j, peer in enumerate(peers):
            _remote(src, land.at[me_], sems[j], sems[7 + j], peer).start()
        token[...] = jnp.zeros_like(token)

    outs = _pallas_call(
        body, name=name, in_specs=[ANY, ANY], out_specs=[SEM] * 14 + [ANY, ANY, TOKEN_SPEC],
        out_shape=[pltpu.SemaphoreType.DMA(())] * 14 + [jax.ShapeDtypeStruct(part.shape, F32),
                                                       jax.ShapeDtypeStruct((8, P, LANES), F32), TOKEN],
        input_output_aliases={0: 14, 1: 15}, compiler_params=_split_params(),
    )(part, landing)
    return list(outs[:7]), list(outs[7:14]), outs[14], outs[15], outs[16]


def _all_to_all_small_wait(ssem, rsem, part, landing, after, name):
    def body(*refs):
        sems, src, land = refs[2:16], refs[17], refs[18]
        _, peers = _small_peers()
        for j, (px, py, pc) in enumerate(peers):
            cp = _remote(src, land.at[4 * px + 2 * py + pc], sems[j], sems[7 + j], peers[j])
            cp.wait_send()
            cp.wait_recv()

    return _pallas_call(
        body, name=name, in_specs=[ANY, ANY] + [SEM] * 14 + [ANY], out_specs=[ANY, ANY],
        out_shape=[jax.ShapeDtypeStruct(part.shape, F32), jax.ShapeDtypeStruct(landing.shape, F32)],
        input_output_aliases={0: 0, 1: 1}, compiler_params=_split_params(),
    )(part, landing, *ssem, *rsem, after)[1]


def _row_tile(rows, width, n_arrays):
    t = rows
    while t % 2 == 0 and t > 8 and 2 * n_arrays * t * width * 4 > VMEM_LIMIT // 2:
        t //= 2
    return t


def _scalar_grid(grid, in_specs, out_specs):
    return pltpu.PrefetchScalarGridSpec(num_scalar_prefetch=1, grid=grid, in_specs=in_specs, out_specs=out_specs)


def _cast_place(w3, layer, pos, name):
    _, r, c = w3.shape
    tr = _row_tile(r, c, 2)

    def body(pos_ref, w_ref, o_ref):
        o_ref[...] = w_ref[...].astype(BF)

    return _pallas_call(
        body, name=name,
        grid_spec=_scalar_grid((r // tr,), [pl.BlockSpec((None, tr, c), lambda i, pos: (layer, i, 0))],
                               pl.BlockSpec((None, tr, c), lambda i, pos: (pos[0], i, 0))),
        out_shape=jax.ShapeDtypeStruct((N_CHIPS, r, c), BF), compiler_params=_params(("parallel",)),
    )(pos, w3)


def _pair_sum(view, recv, pos, name):
    _, _, hr, c = view.shape
    tr = _row_tile(hr, c, 3)

    def body(pos_ref, g_ref, r_ref, o_ref):
        o_ref[...] = (g_ref[...].astype(F32) + r_ref[...].astype(F32)).astype(BF)

    blk = pl.BlockSpec((None, tr, c), lambda p, i, pos: (p, i, 0))
    return _pallas_call(
        body, name=name,
        grid_spec=_scalar_grid((N_CHIPS, hr // tr),
                               [pl.BlockSpec((None, None, tr, c), lambda p, i, pos: (p, pos[1], i, 0)), blk], blk),
        out_shape=jax.ShapeDtypeStruct(recv.shape, BF), compiler_params=_params(("parallel", "parallel")),
    )(pos, view, recv)


def _chip_sum(parts, recv, pos, name):
    _, hr, c = parts.shape
    tr = _row_tile(hr, c, 6)

    def body(pos_ref, p_ref, r_ref, o_ref):
        acc = p_ref[...].astype(F32)
        for j in range(3):
            acc = acc + r_ref[j].astype(F32)
        o_ref[...] = acc

    return _pallas_call(
        body, name=name,
        grid_spec=_scalar_grid((hr // tr,),
                               [pl.BlockSpec((None, tr, c), lambda i, pos: (pos[0], i, 0)),
                                pl.BlockSpec((3, tr, c), lambda i, pos: (0, i, 0))],
                               pl.BlockSpec((tr, c), lambda i, pos: (i, 0))),
        out_shape=jax.ShapeDtypeStruct((hr, c), F32), compiler_params=_params(("parallel",)),
    )(pos, parts, recv)


def _sum_slices(a, name):
    n, rows, width = a.shape
    tr = _row_tile(rows, width, n + 1)

    def body(a_ref, o_ref):
        acc = a_ref[0].astype(F32)
        for i in range(1, n):
            acc = acc + a_ref[i].astype(F32)
        o_ref[...] = acc

    return _pallas_call(
        body, name=name, grid=(rows // tr,), in_specs=[pl.BlockSpec((n, tr, width), lambda i: (0, i, 0))],
        out_specs=pl.BlockSpec((tr, width), lambda i: (i, 0)), out_shape=jax.ShapeDtypeStruct((rows, width), F32),
        compiler_params=_params(("parallel",)),
    )(a)


def _adamw_update(w, g, m, v):
    nm = ADAM_B1 * m + (1.0 - ADAM_B1) * g
    nv = ADAM_B2 * v + (1.0 - ADAM_B2) * (g * g)
    m_hat = nm / (1.0 - ADAM_B1 ** ADAM_STEP)
    v_hat = nv / (1.0 - ADAM_B2 ** ADAM_STEP)
    return -ADAM_LR * (m_hat / (jnp.sqrt(v_hat) + ADAM_EPS) + ADAM_WD * w), nm, nv


def _adamw(w, g, m, v, name):
    rows, width = w.shape
    tr = _row_tile(rows, width, 7)

    def body(w_ref, g_ref, m_ref, v_ref, d_ref, nm_ref, nv_ref):
        d_ref[...], nm_ref[...], nv_ref[...] = _adamw_update(w_ref[...], g_ref[...], m_ref[...], v_ref[...])

    blk = pl.BlockSpec((tr, width), lambda i: (i, 0))
    return _pallas_call(
        body, name=name, grid=(rows // tr,), in_specs=[blk] * 4, out_specs=[blk] * 3,
        out_shape=[jax.ShapeDtypeStruct((rows, width), F32)] * 3, compiler_params=_params(("parallel",)),
    )(w, g, m, v)


def _adamw_halves(w3, m3, v3, mine, other, pos, name):
    depth, r, c = w3.shape
    assert depth == 2
    hr = r // 2
    tr = _row_tile(hr, c, 11)
    sources = ((0, True, mine[0]), (0, False, other[0]), (1, True, mine[1]), (1, False, other[1]))

    def active(l, h, core, layer, own):
        mine_half = h == core
        return (l == layer) & (mine_half if own else jnp.logical_not(mine_half))

    def body(pos_ref, w_ref, m_ref, v_ref, *rest):
        g_refs, (go_ref, d_ref, nm_ref, nv_ref) = rest[:4], rest[4:]
        l, h = pl.program_id(0), pl.program_id(1)
        for (layer, own, _), g_ref in zip(sources, g_refs):
            @pl.when(active(l, h, pos_ref[1], layer, own))
            def _():
                gv = g_ref[...]
                go_ref[...] = gv
                d_ref[...], nm_ref[...], nv_ref[...] = _adamw_update(w_ref[...], gv, m_ref[...], v_ref[...])

    def gspec(layer, own):
        return pl.BlockSpec((tr, c), lambda l, h, i, pos: (jnp.where(active(l, h, pos[1], layer, own), i, 0), 0))

    blk = pl.BlockSpec((None, None, tr, c), lambda l, h, i, pos: (l, h, i, 0))
    view = lambda a: a.reshape(depth, 2, hr, c)
    outs = _pallas_call(
        body, name=name,
        grid_spec=_scalar_grid((depth, 2, hr // tr), [blk] * 3 + [gspec(layer, own) for layer, own, _ in sources],
                               [blk] * 4),
        out_shape=[jax.ShapeDtypeStruct((depth, 2, hr, c), F32)] * 4,
        compiler_params=_params(("parallel", "parallel", "parallel")),
    )(pos, view(w3), view(m3), view(v3), *[s[2] for s in sources])
    return [o.reshape(w3.shape) for o in outs]


BIG = ("w_in", "w_pool_up", "w_conv_out", "w_attn_up", "w_o", "w_ff1", "w_ff2")
SMALL = ("norm_mix", "b_gate", "pool_mix", "pool_scale", "conv_w", "q_gain", "k_gain", "norm_mlp")
ORDER = ("norm_mix", "w_in", "b_gate", "pool_mix", "pool_scale", "conv_w", "q_gain", "k_gain", "w_pool_up",
         "w_conv_out", "w_attn_up", "w_o", "norm_mlp", "w_ff1", "w_ff2")
COLUMN_SHARDED = ("w_pool_up", "w_conv_out", "w_attn_up", "w_ff1")


def _matrix_weights(gathered):
    w = {}
    for name, g4 in gathered.items():
        if name in COLUMN_SHARDED:
            w[name] = g4
        else:
            w[name] = g4.reshape(N_CHIPS * g4.shape[1], g4.shape[2])
    return w


def _small_weights(l, small):
    w = {}
    w["norm_mix"] = small["norm_mix"][l][None]
    w["norm_mlp"] = small["norm_mlp"][l][None]
    w["b_gate"] = small["b_gate"][l][None]
    w["pool_mix"] = small["pool_mix"][l].astype(BF)
    w["pool_scale"] = small["pool_scale"][l][None]
    w["conv_w"] = jnp.pad(small["conv_w_full"][l], ((0, 5), (0, 0)))
    w["qk_gain"] = jnp.pad(jnp.stack([jnp.tile(small["q_gain"][l], 2), jnp.tile(small["k_gain"][l], 2)]), ((0, 6), (0, 0)))
    return w


def _to_chip_major(name, g):
    if name == "w_in":
        return g.T.reshape(N_CHIPS, g.shape[1] // N_CHIPS, g.shape[0])
    if name in COLUMN_SHARDED:
        return g
    return g.reshape(N_CHIPS, g.shape[0] // N_CHIPS, g.shape[1])


def _pad8(a):
    a = a.reshape(-1)
    return jnp.pad(a, (0, (-a.size) % (8 * LANES))).reshape(-1, LANES)


def kernel(x, norm_mix, w_in, b_gate, pool_mix, pool_scale, conv_w, q_gain, k_gain, w_pool_up, w_conv_out, w_attn_up, w_o, norm_mlp, w_ff1, w_ff2, loss_target, m_norm_mix, m_w_in, m_b_gate, m_pool_mix, m_pool_scale, m_conv_w, m_q_gain, m_k_gain, m_w_pool_up, m_w_conv_out, m_w_attn_up, m_w_o, m_norm_mlp, m_w_ff1, m_w_ff2, v_norm_mix, v_w_in, v_b_gate, v_pool_mix, v_pool_scale, v_conv_w, v_q_gain, v_k_gain, v_w_pool_up, v_w_conv_out, v_w_attn_up, v_w_o, v_norm_mlp, v_w_ff1, v_w_ff2):
    weights = dict(norm_mix=norm_mix, w_in=w_in, b_gate=b_gate, pool_mix=pool_mix, pool_scale=pool_scale, conv_w=conv_w,
                   q_gain=q_gain, k_gain=k_gain, w_pool_up=w_pool_up, w_conv_out=w_conv_out, w_attn_up=w_attn_up,
                   w_o=w_o, norm_mlp=norm_mlp, w_ff1=w_ff1, w_ff2=w_ff2)
    moms = dict(norm_mix=m_norm_mix, w_in=m_w_in, b_gate=m_b_gate, pool_mix=m_pool_mix, pool_scale=m_pool_scale,
                conv_w=m_conv_w, q_gain=m_q_gain, k_gain=m_k_gain, w_pool_up=m_w_pool_up, w_conv_out=m_w_conv_out,
                w_attn_up=m_w_attn_up, w_o=m_w_o, norm_mlp=m_norm_mlp, w_ff1=m_w_ff1, w_ff2=m_w_ff2)
    vels = dict(norm_mix=v_norm_mix, w_in=v_w_in, b_gate=v_b_gate, pool_mix=v_pool_mix, pool_scale=v_pool_scale,
                conv_w=v_conv_w, q_gain=v_q_gain, k_gain=v_k_gain, w_pool_up=v_w_pool_up, w_conv_out=v_w_conv_out,
                w_attn_up=v_w_attn_up, w_o=v_w_o, norm_mlp=v_norm_mlp, w_ff1=v_w_ff1, w_ff2=v_w_ff2)
    depth = norm_mix.shape[0]
    q = 2 * lax.axis_index("x") + lax.axis_index("y")
    pos = _position_operand()
    for group in (weights, moms, vels):
        group["w_in"] = jnp.swapaxes(group["w_in"], 1, 2)

    assert depth == 2, "the second layer's gather hides behind the first layer's forward, and likewise backward"
    first, rest = BIG[:1], BIG[1:]
    cw_all = _all_to_all_small(_pad8(conv_w))
    bufs = [{n: _cast_place(weights[n], l, pos, f"cast_{n}_l{l}") for n in BIG} for l in range(depth)]
    w_first = _matrix_weights(dict(zip(first, _gather([bufs[0][n] for n in first], "gather_l0_in", after=cw_all))))
    b_ssem, b_rsem, b_views, b_token = _gather_start([bufs[0][n] for n in rest], "gather_start_l0_rest")
    g_ssem, g_rsem, g_views, g_token = _gather_start([bufs[1][n] for n in BIG], "gather_start_l1")
    conv_w_full = jnp.concatenate(
        [cw_all[2 * p].reshape(-1)[:conv_w.size].reshape(conv_w.shape) for p in range(N_CHIPS)], axis=-1)
    small = dict(weights)
    small["conv_w_full"] = conv_w_full

    def late_weights(t):
        got = _gather_finish(b_ssem, b_rsem, b_views, t, "gather_wait_l0_rest", "gather_forward_l0_rest",
                             [bufs[0][n].shape for n in rest])
        return _matrix_weights(dict(zip(rest, got)))

    wl, saved = [None] * depth, [None] * depth
    h, saved[0], wl[0] = _layer_fwd(x[0], dict(_small_weights(0, small), **w_first), "l0", after=[b_token, g_token],
                                    late=late_weights)
    got = _gather_finish(g_ssem, g_rsem, g_views, h, "gather_wait_l1", "gather_forward_l1",
                         [bufs[1][n].shape for n in BIG])
    h, saved[1], wl[1] = _layer_fwd(h, dict(_small_weights(1, small), **_matrix_weights(dict(zip(BIG, got)))), "l1")
    dh, loss_row = _loss_grad(h, loss_target[0], "loss")

    def pair_stage(names, g, tag):
        views = [_halves(_to_chip_major(n, g[n])) for n in names]
        from_sibling = _pair_swap(views, f"grad_pair_swap_{tag}")
        return [_pair_sum(views[k], from_sibling[k], pos, f"pair_sum_{n}_{tag}") for k, n in enumerate(names)]

    mine, other = [{}, {}], [{}, {}]

    def finish(names, l, started, after, tag):
        ssem, rsem, parts, landing, _ = started
        parts, arrived = _chip_exchange_wait(ssem, rsem, parts, landing, after, f"grad_chip_exchange_wait_{tag}")
        got = [_chip_sum(parts[k], arrived[k], pos, f"chip_sum_{n}_{tag}") for k, n in enumerate(names)]
        mine[l].update(zip(names, got))
        other[l].update(zip(names, _pair_send(got, f"grad_pair_send_{tag}")))

    def small_pieces(g):
        return [_pad8(g[n][:3] if n == "conv_w" else g[n]) for n in SMALL]

    def start_small(l):
        return _all_to_all_small_start(jnp.concatenate(small_pieces(grads[l]), axis=0), f"small_grad_exchange_start_l{l}")

    grads, early, small = [None] * depth, {}, [None] * depth
    dh, grads[1] = _layer_bwd(dh, wl[1], saved[1], "l1")
    second = _chip_exchange_start(pair_stage(BIG, grads[1], "l1"), "grad_chip_exchange_start_l1")
    small[1] = start_small(1)

    def start_rest(g):
        early["rest"] = _chip_exchange_start(pair_stage(rest, g, "l0_rest"), "grad_chip_exchange_start_l0_rest")
        return early["rest"][4]

    def start_last(g):
        early["in"] = _chip_exchange_start(pair_stage(first, g, "l0_in"), "grad_chip_exchange_start_l0_in")
        return early["in"][4]

    dh, grads[0] = _layer_bwd(dh, wl[0], saved[0], "l0", after=[second[4], small[1][4]], mid=start_rest,
                              tail=start_last)
    small[0] = start_small(0)
    finish(BIG, 1, second, dh, "l1")
    finish(rest, 0, early["rest"], dh, "l0_rest")
    loss = lax.psum(loss_row[0, 0], ("x", "y", "c"))
    full = {}

    deltas, new_m, new_v = {}, {}, {}

    def update_matrix(n):
        full[n], deltas[n], new_m[n], new_v[n] = _adamw_halves(
            weights[n], moms[n], vels[n], [mine[l][n] for l in range(depth)], [other[l][n] for l in range(depth)], pos,
            f"adamw_{n}")

    for n in rest:
        update_matrix(n)
    finish(first, 0, early["in"], deltas[rest[-1]], "l0_in")
    for n in first:
        update_matrix(n)
    summed = []
    for l in range(depth):
        ssem, rsem, part, landing, _ = small[l]
        summed.append(_sum_slices(_all_to_all_small_wait(ssem, rsem, part, landing, deltas[first[-1]],
                                                         f"small_grad_exchange_wait_l{l}"), f"small_sum_l{l}"))
    row = 0
    for n, piece in zip(SMALL, small_pieces(grads[0])):
        size = (weights[n].size if n != "conv_w" else depth * 3 * 512) // depth
        flat = jnp.stack([s[row:row + piece.shape[0]].reshape(-1)[:size] for s in summed])
        row += piece.shape[0]
        if n == "conv_w":
            full[n] = lax.dynamic_slice_in_dim(flat.reshape(depth, 3, 512), q * conv_w.shape[2], conv_w.shape[2], axis=2)
        else:
            full[n] = flat.reshape(weights[n].shape)
    for n in SMALL:
        shape = weights[n].shape
        two_d = (-1, shape[-1]) if n not in ("conv_w", "q_gain", "k_gain") else (1, -1)
        d2, m2, v2 = _adamw(weights[n].reshape(two_d), full[n].reshape(two_d), moms[n].reshape(two_d),
                            vels[n].reshape(two_d), f"adamw_{n}")
        deltas[n], new_m[n], new_v[n] = d2.reshape(shape), m2.reshape(shape), v2.reshape(shape)
        full[n] = full[n].reshape(shape)
    for group in (full, deltas, new_m, new_v):
        group["w_in"] = jnp.swapaxes(group["w_in"], 1, 2)
    return (loss, dh[None], *[full[n] for n in ORDER], *[deltas[n] for n in ORDER], *[new_m[n] for n in ORDER],
            *[new_v[n] for n in ORDER])
```

```python
import functools

import jax
import jax.numpy as jnp
from jax import lax
from jax.experimental import pallas as pl
from jax.experimental.pallas import tpu as pltpu

F32 = jnp.float32
BF = jnp.bfloat16
MESH_ID = pl.DeviceIdType.MESH
ANY = pl.BlockSpec(memory_space=pl.ANY)

EPS = 1e-6
MASK_VALUE = -1e30
POOL_WINDOWS = (2, 4, 8, 16)
ATTN_DILATIONS = (1, 4, 16)
ATTN_BLOCK = 128
HEAD_DIM = 64
OFF_Q, OFF_K, OFF_V, OFF_GATE = 2048, 2816, 3584, 4352
N_CHIPS = 4
ADAM_LR, ADAM_B1, ADAM_B2, ADAM_EPS, ADAM_WD, ADAM_STEP = 0.001, 0.9, 0.999, 1e-08, 0.01, 10

VMEM_LIMIT = 48 * 1024 * 1024
LANES = 128

_DIMS = {"nn": (((1,), (0,)), ((), ())), "nt": (((1,), (1,)), ((), ())), "tn": (((0,), (0,)), ((), ()))}


def _params(sem):
    return pltpu.CompilerParams(dimension_semantics=sem, vmem_limit_bytes=VMEM_LIMIT)


def _pallas_call(body, **kw):
    call = pl.pallas_call(body, **kw)

    def run(*args):
        pinned = [pltpu.with_memory_space_constraint(a, pltpu.HBM)
                  if hasattr(a, "dtype") and jnp.issubdtype(a.dtype, jnp.floating) else a for a in args]
        return call(*pinned)

    return run


def _dot(a, b, mode="nn"):
    return lax.dot_general(a, b, _DIMS[mode], preferred_element_type=F32)


def _mm(a, b, mode, name, *, tm, tn, tk, out_dtype=F32, res=None, aux=None, epi=None, n_outer=False,
        b_shards=False, out_shards=False, after=None):
    if mode == "tn":
        K, M = a.shape
    else:
        M, K = a.shape
    if b_shards:
        if mode == "nn":
            assert b.shape[1] == K
            N = b.shape[2] * N_CHIPS
        else:
            assert mode == "nt"
            N = b.shape[1]
            assert b.shape[2] * N_CHIPS == K
    else:
        N = b.shape[0] if mode == "nt" else b.shape[1]
    tm, tn, tk = min(tm, M), min(tn, N), min(tk, K)
    assert M % tm == 0 and N % tn == 0 and K % tk == 0
    nk = K // tk
    if n_outer:
        grid = (N // tn, M // tm, nk)
        ij = lambda p, q_: (q_, p)
    else:
        grid = (M // tm, N // tn, nk)
        ij = lambda p, q_: (p, q_)

    def amap(p, q_, k):
        i, j = ij(p, q_)
        return (k, i) if mode == "tn" else (i, k)

    a_spec = pl.BlockSpec((tk, tm) if mode == "tn" else (tm, tk), amap)
    if b_shards:
        if mode == "nn":
            per = (N // N_CHIPS) // tn
            assert per >= 1 and (N // N_CHIPS) % tn == 0

            def bmap(p, q_, k):
                i, j = ij(p, q_)
                return (j // per, k, j % per)

            b_spec = pl.BlockSpec((None, tk, tn), bmap)
        else:
            per = (K // N_CHIPS) // tk
            assert per >= 1 and (K // N_CHIPS) % tk == 0

            def bmap(p, q_, k):
                i, j = ij(p, q_)
                return (k // per, j, k % per)

            b_spec = pl.BlockSpec((None, tn, tk), bmap)
    else:
        def bmap(p, q_, k):
            i, j = ij(p, q_)
            return (j, k) if mode == "nt" else (k, j)

        b_spec = pl.BlockSpec((tn, tk) if mode == "nt" else (tk, tn), bmap)

    def omap(p, q_, k):
        return ij(p, q_)

    o_spec = pl.BlockSpec((tm, tn), omap)
    if out_shards:
        per_o = (N // N_CHIPS) // tn
        assert per_o >= 1 and (N // N_CHIPS) % tn == 0

        def osmap(p, q_, k):
            i, j = ij(p, q_)
            return (j // per_o, i, j % per_o)

        out_spec0 = pl.BlockSpec((None, tm, tn), osmap)
        out_shape0 = jax.ShapeDtypeStruct((N_CHIPS, M, N // N_CHIPS), out_dtype)
    else:
        out_spec0 = o_spec
        out_shape0 = jax.ShapeDtypeStruct((M, N), out_dtype)

    in_specs = [a_spec, b_spec]
    args = [a, b]
    if res is not None:
        in_specs.append(o_spec)
        args.append(res)
    if aux is not None:
        in_specs.append(o_spec)
        args.append(aux)
    after = [] if after is None else list(after) if isinstance(after, (list, tuple)) else [after]
    in_specs += [ANY] * len(after)
    args += after
    out_specs = [out_spec0]
    out_shape = [out_shape0]
    n_out = len(out_shape)
    has_res, has_aux, n_after = res is not None, aux is not None, len(after)

    def body(*refs):
        a_ref, b_ref = refs[0], refs[1]
        pos = 2
        res_ref = aux_ref = None
        if has_res:
            res_ref = refs[pos]
            pos += 1
        if has_aux:
            aux_ref = refs[pos]
            pos += 1
        pos += n_after
        outs = refs[pos:pos + n_out]
        part = _dot(a_ref[...].astype(BF), b_ref[...].astype(BF), mode)

        def finish(acc):
            if res_ref is not None:
                acc = res_ref[...] + acc
            if epi == "relu2":
                r = jnp.maximum(acc, 0.0)
                outs[0][...] = (r * r).astype(out_dtype)
            elif epi == "drelu2":
                outs[0][...] = (acc * (2.0 * jnp.sqrt(aux_ref[...].astype(F32)))).astype(out_dtype)
            else:
                outs[0][...] = acc.astype(out_dtype)

        if nk == 1:
            finish(part)
        else:
            acc_ref = refs[pos + n_out]
            k = pl.program_id(2)

            @pl.when(k == 0)
            def _():
                acc_ref[...] = part

            @pl.when(k > 0)
            def _():
                acc_ref[...] += part

            @pl.when(k == nk - 1)
            def _():
                finish(acc_ref[...])

    scratch = [pltpu.VMEM((tm, tn), F32)] if nk > 1 else []
    out = _pallas_call(
        body, name=name, grid=grid, in_specs=in_specs, out_specs=out_specs, out_shape=out_shape,
        scratch_shapes=scratch, compiler_params=_params(("parallel", "parallel", "arbitrary")),
    )(*args)
    return out if n_out > 1 else out[0]


def _rms_fwd(x, gain, name, after=None):
    T, D = x.shape
    tm = min(512, T)

    def body(x_ref, g_ref, *rest):
        o_ref = rest[-1]
        xv = x_ref[...]
        r = lax.rsqrt(jnp.mean(xv * xv, axis=-1, keepdims=True) + EPS)
        o_ref[...] = ((xv * r) * g_ref[...]).astype(BF)

    extra = [] if after is None else list(after) if isinstance(after, (list, tuple)) else [after]
    return _pallas_call(
        body, name=name, grid=(T // tm,),
        in_specs=[pl.BlockSpec((tm, D), lambda i: (i, 0)), pl.BlockSpec((1, D), lambda i: (0, 0))] + [ANY] * len(extra),
        out_specs=pl.BlockSpec((tm, D), lambda i: (i, 0)), out_shape=jax.ShapeDtypeStruct((T, D), BF),
        compiler_params=_params(("parallel",)),
    )(x, gain, *extra)


def _rms_bwd(dh, x, gain, dres, name):
    T, D = x.shape
    tm = min(512, T)

    def body(dh_ref, x_ref, g_ref, dres_ref, dx_ref, dg_ref):
        xv = x_ref[...]
        r = lax.rsqrt(jnp.mean(xv * xv, axis=-1, keepdims=True) + EPS)
        xhat = xv * r
        dhv = dh_ref[...]
        dy = dhv * g_ref[...]
        dx_ref[...] = dres_ref[...] + r * (dy - xhat * jnp.mean(dy * xhat, axis=-1, keepdims=True))

        @pl.when(pl.program_id(0) == 0)
        def _():
            dg_ref[...] = jnp.zeros_like(dg_ref)

        dg_ref[...] += jnp.sum(dhv * xhat, axis=0, keepdims=True)

    row = pl.BlockSpec((tm, D), lambda i: (i, 0))
    vec = pl.BlockSpec((1, D), lambda i: (0, 0))
    return _pallas_call(
        body, name=name, grid=(T // tm,), in_specs=[row, row, vec, row], out_specs=[row, vec],
        out_shape=[jax.ShapeDtypeStruct((T, D), F32), jax.ShapeDtypeStruct((1, D), F32)],
        compiler_params=_params(("arbitrary",)),
    )(dh, x, gain, dres)


def _loss_grad(y, target, name):
    T, D = y.shape
    tm = min(512, T)

    def body(y_ref, t_ref, dy_ref, l_ref):
        e = y_ref[...] - t_ref[...]
        dy_ref[...] = e / float(D)

        @pl.when(pl.program_id(0) == 0)
        def _():
            l_ref[...] = jnp.zeros_like(l_ref)

        l_ref[...] += 0.5 * jnp.sum(jnp.mean(e * e, axis=-1, keepdims=True))

    row = pl.BlockSpec((tm, D), lambda i: (i, 0))
    return _pallas_call(
        body, name=name, grid=(T // tm,), in_specs=[row, row],
        out_specs=[row, pl.BlockSpec((1, LANES), lambda i: (0, 0))],
        out_shape=[jax.ShapeDtypeStruct((T, D), F32), jax.ShapeDtypeStruct((1, LANES), F32)],
        compiler_params=_params(("arbitrary",)),
    )(y, target)


POOL_HALO = 16
CONV_HALO = 8


def _causal_window_sum(v, w):
    s, sh = v, 1
    while sh < w:
        s = s + pltpu.roll(s, sh, 0)
        sh *= 2
    return s


def _anticausal_window_sum(v, w):
    n = v.shape[0]
    s, sh = v, 1
    while sh < w:
        s = s + pltpu.roll(s, n - sh, 0)
        sh *= 2
    return s


def _poolconv_fwd(z, pmix_b, pscale, convw, name):
    T = z.shape[0]
    R = min(512, T)
    PH, CH = R // POOL_HALO, R // CONV_HALO

    def body(u_ref, uh_ref, b_ref, c_ref, ch_ref, x_ref, xh_ref, mix_ref, sc_ref, cw_ref, yp_ref, yc_ref):
        i = pl.program_id(0)
        keep = (i > 0).astype(F32)
        row = i * R + lax.broadcasted_iota(jnp.int32, (R, 1), 0)
        w_all = jnp.concatenate([uh_ref[...] * keep, u_ref[...]], axis=0)
        for g, w in enumerate(POOL_WINDOWS):
            cols = slice(128 * g, 128 * (g + 1))
            wg = w_all[:, cols]
            s = _causal_window_sum(wg, w)[POOL_HALO:]
            cnt = jnp.minimum(row + 1, w).astype(F32)
            dgrp = s / cnt - wg[POOL_HALO:]
            y = _dot(dgrp.astype(BF), mix_ref[g]) * sc_ref[:, cols]
            yp_ref[:, cols] = y.astype(BF)
        uc = jnp.concatenate([ch_ref[...] * xh_ref[...] * keep, c_ref[...] * x_ref[...]], axis=0)
        yc = cw_ref[2:3, :] * uc + cw_ref[0:1, :] * pltpu.roll(uc, 2, 0) + cw_ref[1:2, :] * pltpu.roll(uc, 1, 0)
        yc_ref[...] = (b_ref[...] * yc[CONV_HALO:]).astype(BF)

    def main(cb):
        return pl.BlockSpec((R, 512), lambda i: (i, cb))

    def prev(cb, halo, per):
        return pl.BlockSpec((halo, 512), lambda i: (jnp.maximum(i * per - 1, 0), cb))

    full = lambda a: pl.BlockSpec(a.shape, lambda i: (0,) * a.ndim)
    return _pallas_call(
        body, name=name, grid=(T // R,),
        in_specs=[main(0), prev(0, POOL_HALO, PH), main(1), main(2), prev(2, CONV_HALO, CH), main(3),
                  prev(3, CONV_HALO, CH), full(pmix_b), full(pscale), full(convw)],
        out_specs=[pl.BlockSpec((R, 512), lambda i: (i, 0))] * 2,
        out_shape=[jax.ShapeDtypeStruct((T, 512), BF)] * 2,
        compiler_params=_params(("parallel",)),
    )(z, z, z, z, z, z, z, pmix_b, pscale, convw)


def _poolconv_bwd(z, dyp, dyc, pmix_b, pscale, convw, dz, name):
    T = z.shape[0]
    R = min(512, T)
    PH, CH = R // POOL_HALO, R // CONV_HALO
    nsteps = T // R

    def body(u_ref, uh_ref, b_ref, bn_ref, c_ref, ch_ref, x_ref, xh_ref, dyp_ref, dypn_ref, dyc_ref, dycn_ref,
             mix_ref, sc_ref, cw_ref, dz_in_ref, dz_ref, dmix_ref, dsc_ref, dcw_ref):
        i = pl.program_id(0)
        keep_prev = (i > 0).astype(F32)
        keep_next = (i < nsteps - 1).astype(F32)

        @pl.when(i == 0)
        def _():
            dmix_ref[...] = jnp.zeros_like(dmix_ref)
            dsc_ref[...] = jnp.zeros_like(dsc_ref)
            dcw_ref[...] = jnp.zeros_like(dcw_ref)

        row = i * R + lax.broadcasted_iota(jnp.int32, (R, 1), 0)
        row_ext = i * R + lax.broadcasted_iota(jnp.int32, (R + POOL_HALO, 1), 0)
        w_all = jnp.concatenate([uh_ref[...] * keep_prev, u_ref[...]], axis=0)
        dyp_ext = jnp.concatenate([dyp_ref[...], dypn_ref[...] * keep_next], axis=0)
        for g, w in enumerate(POOL_WINDOWS):
            cols = slice(128 * g, 128 * (g + 1))
            wg = w_all[:, cols]
            s = _causal_window_sum(wg, w)[POOL_HALO:]
            cnt = jnp.minimum(row + 1, w).astype(F32)
            dgrp = (s / cnt - wg[POOL_HALO:]).astype(BF)
            y_pre = _dot(dgrp, mix_ref[g])
            dsc_ref[:, cols] += jnp.sum(dyp_ref[:, cols] * y_pre, axis=0, keepdims=True)
            dyb = (dyp_ext[:, cols] * sc_ref[:, cols]).astype(BF)
            dmix_ref[cols, :] += _dot(dgrp, dyb[:R], "tn")
            dd = _dot(dyb, mix_ref[g], "nt")
            cnt_ext = jnp.minimum(row_ext + 1, w).astype(F32)
            e = _anticausal_window_sum(dd / cnt_ext, w)
            dz_ref[:, cols] = (e[:R] - dd[:R]).astype(BF)
        cw0, cw1, cw2 = cw_ref[0:1, :], cw_ref[1:2, :], cw_ref[2:3, :]
        uc = jnp.concatenate([ch_ref[...] * xh_ref[...] * keep_prev, c_ref[...] * x_ref[...]], axis=0)
        uc1 = pltpu.roll(uc, 1, 0)[CONV_HALO:]
        uc2 = pltpu.roll(uc, 2, 0)[CONV_HALO:]
        uc0 = uc[CONV_HALO:]
        yc = cw2 * uc0 + cw0 * uc2 + cw1 * uc1
        dycv = dyc_ref[...]
        dz_ref[:, 512:1024] = (dycv * yc).astype(BF)
        dv_ext = jnp.concatenate([dycv * b_ref[...], dycn_ref[...] * bn_ref[...] * keep_next], axis=0)
        n_ext = R + CONV_HALO
        duc = (cw2 * dv_ext + cw1 * pltpu.roll(dv_ext, n_ext - 1, 0) + cw0 * pltpu.roll(dv_ext, n_ext - 2, 0))[:R]
        dv = dv_ext[:R]
        dcw_ref[0:1, :] += jnp.sum(dv * uc2, axis=0, keepdims=True)
        dcw_ref[1:2, :] += jnp.sum(dv * uc1, axis=0, keepdims=True)
        dcw_ref[2:3, :] += jnp.sum(dv * uc0, axis=0, keepdims=True)
        dz_ref[:, 1024:1536] = (duc * x_ref[...]).astype(BF)
        dz_ref[:, 1536:2048] = (duc * c_ref[...]).astype(BF)

    def main(cb):
        return pl.BlockSpec((R, 512), lambda i: (i, cb))

    def prev(cb, halo, per):
        return pl.BlockSpec((halo, 512), lambda i: (jnp.maximum(i * per - 1, 0), cb))

    def nxt(cb, halo, per):
        return pl.BlockSpec((halo, 512), lambda i: (jnp.minimum((i + 1) * per, T // halo - 1), cb))

    full = lambda a: pl.BlockSpec(a.shape, lambda i: (0,) * a.ndim)
    return _pallas_call(
        body, name=name, grid=(nsteps,),
        in_specs=[main(0), prev(0, POOL_HALO, PH), main(1), nxt(1, CONV_HALO, CH), main(2), prev(2, CONV_HALO, CH),
                  main(3), prev(3, CONV_HALO, CH), main(0), nxt(0, POOL_HALO, PH), main(0), nxt(0, CONV_HALO, CH),
                  full(pmix_b), full(pscale), full(convw), ANY],
        out_specs=[pl.BlockSpec((R, 2048), lambda i: (i, 0)), pl.BlockSpec((512, 128), lambda i: (0, 0)),
                   pl.BlockSpec((1, 512), lambda i: (0, 0)), pl.BlockSpec((8, 512), lambda i: (0, 0))],
        out_shape=[jax.ShapeDtypeStruct(dz.shape, BF), jax.ShapeDtypeStruct((512, 128), F32),
                   jax.ShapeDtypeStruct((1, 512), F32), jax.ShapeDtypeStruct((8, 512), F32)],
        input_output_aliases={15: 0}, compiler_params=_params(("arbitrary",)),
    )(z, z, z, z, z, z, z, z, dyp, dyp, dyc, dyc, pmix_b, pscale, convw, dz)


def _head_sums(v):
    row = lax.broadcasted_iota(jnp.int32, (LANES, LANES), 0) < HEAD_DIM
    col = lax.broadcasted_iota(jnp.int32, (LANES, LANES), 1) < HEAD_DIM
    same_head = jnp.where(jnp.logical_xor(row, col), 0.0, 1.0).astype(BF)
    hi = v.astype(BF)
    lo = (v - hi.astype(F32)).astype(BF)
    return _dot(hi, same_head) + _dot(lo, same_head)


def _head_norm(x, g2, ma):
    r = lax.rsqrt(_head_sums(x * x) / HEAD_DIM + EPS)
    return x * r, r


def _head_norm_bwd(dy, xhat, r, g2, ma):
    dxh = dy * g2
    return r * (dxh - xhat * (_head_sums(dxh * xhat) / HEAD_DIM))


def _head_col(tile, hm):
    return jnp.max(jnp.where(hm, tile, -jnp.inf), axis=-1, keepdims=True)


def _attn_masks(other_block_exists):
    lane = lax.broadcasted_iota(jnp.int32, (2 * ATTN_BLOCK, ATTN_BLOCK), 1)
    qi = lax.broadcasted_iota(jnp.int32, (2 * ATTN_BLOCK, ATTN_BLOCK), 0) & (ATTN_BLOCK - 1)
    never = (1 - other_block_exists.astype(jnp.int32)) * (2 * ATTN_BLOCK)
    return lane[:ATTN_BLOCK] < HEAD_DIM, lane <= qi, lane >= qi + never


def _stack_heads(x, ma):
    return jnp.concatenate([jnp.where(ma, x, 0.0), jnp.where(ma, 0.0, x)], axis=0)


def _unstack_heads(y, ma):
    return jnp.where(ma, y[:ATTN_BLOCK], y[ATTN_BLOCK:])


def _stack_cols(tile, ma):
    return jnp.concatenate([_head_col(tile, ma), _head_col(tile, jnp.logical_not(ma))], axis=0)


QKV_TILES = (OFF_GATE - OFF_Q) // LANES
KIND_TILES = QKV_TILES // 3


def _qk_norm(z, gains, name):
    T = z.shape[0]
    tm = min(512, T)

    def body(x_ref, g_ref, o_ref):
        ma = lax.broadcasted_iota(jnp.int32, (tm, LANES), 1) < HEAD_DIM
        for tile in range(QKV_TILES):
            v = x_ref[:, LANES * tile:LANES * (tile + 1)]
            if tile < 2 * KIND_TILES:
                g = g_ref[0:1, :] if tile < KIND_TILES else g_ref[1:2, :]
                v = _head_norm(v, g, ma)[0] * g
            o_ref[tile] = v

    return _pallas_call(
        body, name=name, grid=(T // tm,),
        in_specs=[pl.BlockSpec((pl.Element(tm), pl.Element(OFF_GATE - OFF_Q)), lambda i: (i * tm, OFF_Q)),
                  pl.BlockSpec((8, LANES), lambda i: (0, 0))],
        out_specs=pl.BlockSpec((QKV_TILES, tm, LANES), lambda i: (0, i, 0)),
        out_shape=jax.ShapeDtypeStruct((QKV_TILES, T, LANES), F32), compiler_params=_params(("parallel",)),
    )(z, gains)


ATTN_STEP_ROWS = 1024
ATTN_UNROLL = 2


def _attn_geometry(T, d):
    sub = ATTN_BLOCK * d
    nb = T // sub
    m = max(1, min(nb, ATTN_STEP_ROWS // sub))
    assert T % sub == 0 and nb % m == 0
    return sub, nb, m


def _attn_rows(jj, r, sub, d):
    start = jj * sub + r
    if d == 1:
        return pl.ds(pl.multiple_of(start, ATTN_BLOCK), ATTN_BLOCK)
    return pl.ds(start, ATTN_BLOCK, stride=d)


def _pick(flag, a, b):
    return jnp.where(jnp.full(a.shape, flag.astype(jnp.int32)) > 0, a, b)


def _attn_fwd(qkv, g, d, name):
    T = qkv.shape[1]
    sub, nb, m = _attn_geometry(T, d)
    scale = HEAD_DIM ** -0.5

    def body(q_ref, kc_ref, kp_ref, vc_ref, vp_ref, o_ref, lse_ref):
        jb = pl.program_id(0)

        def step(s, carry):
            jj, r = s // d, s % d
            here, before = _attn_rows(jj, r, sub, d), _attn_rows(jnp.maximum(jj - 1, 0), r, sub, d)
            edge = _attn_rows(0, r, sub, d)
            first = jj == 0
            ma, mask_c, mask_p = _attn_masks(jb * m + jj > 0)
            qs = _stack_heads(q_ref[here, :], ma).astype(BF)
            kcb = kc_ref[here, :].astype(BF)
            kpb = _pick(first, kp_ref[edge, :], kc_ref[before, :]).astype(BF)
            vcb = vc_ref[here, :].astype(BF)
            vpb = _pick(first, vp_ref[edge, :], vc_ref[before, :]).astype(BF)
            s_c = jnp.where(mask_c, _dot(qs, kcb, "nt") * scale, MASK_VALUE)
            s_p = jnp.where(mask_p, _dot(qs, kpb, "nt") * scale, MASK_VALUE)
            mx = jnp.maximum(jnp.max(s_c, axis=-1, keepdims=True), jnp.max(s_p, axis=-1, keepdims=True))
            p_c = jnp.exp(s_c - mx)
            p_p = jnp.exp(s_p - mx)
            den = jnp.sum(p_c, axis=-1, keepdims=True) + jnp.sum(p_p, axis=-1, keepdims=True)
            o = (_dot(p_c.astype(BF), vcb) + _dot(p_p.astype(BF), vpb)) / den
            o_ref[here, :] = _unstack_heads(o, ma)
            lse_ref[here, :] = _unstack_heads(jnp.broadcast_to(mx + jnp.log(den), o.shape), ma)
            return carry

        lax.fori_loop(0, m * d, step, 0, unroll=ATTN_UNROLL)

    def cur(kind):
        return pl.BlockSpec((None, m * sub, LANES), lambda j, t: (KIND_TILES * kind + 2 * g + t, j, 0))

    def prv(kind):
        return pl.BlockSpec((None, sub, LANES), lambda j, t: (KIND_TILES * kind + 2 * g + t, jnp.maximum(j * m - 1, 0), 0))

    out = pl.BlockSpec((m * sub, LANES), lambda j, t: (j, t))
    return _pallas_call(
        body, name=name, grid=(nb // m, 2), in_specs=[cur(0), cur(1), prv(1), cur(2), prv(2)],
        out_specs=[out, out], out_shape=[jax.ShapeDtypeStruct((T, 256), F32)] * 2,
        compiler_params=_params(("parallel", "parallel")),
    )(qkv, qkv, qkv, qkv, qkv)


def _attn_bwd(z, qkv, do, c, lse, gains, g, d, name, after=None):
    T = z.shape[0]
    sub, nb, m = _attn_geometry(T, d)
    scale = HEAD_DIM ** -0.5
    extra = [] if after is None else [after]

    def body(qr_ref, kr_ref, vc_ref, vp_ref, qn_ref, qnn_ref, kn_ref, knp_ref, do_ref, don_ref, c_ref, cn_ref,
             lse_ref, lsen_ref, g_ref, *rest):
        dq_ref, dk_ref, dv_ref, dgq_ref, dgk_ref, sq_ref, sk_ref, sv_ref = rest[len(extra):]
        jb = pl.program_id(0)

        @pl.when((jb == 0) & (pl.program_id(1) == 0))
        def _():
            dgq_ref[...] = jnp.zeros_like(dgq_ref)
            dgk_ref[...] = jnp.zeros_like(dgk_ref)

        gq, gk = g_ref[0:1, :], g_ref[1:2, :]

        def step(s, carry):
            jj, r = s // d, s % d
            here, edge = _attn_rows(jj, r, sub, d), _attn_rows(0, r, sub, d)
            before = _attn_rows(jnp.maximum(jj - 1, 0), r, sub, d)
            behind = _attn_rows(jnp.minimum(jj + 1, m - 1), r, sub, d)
            first, last = jj == 0, jj == m - 1
            block = jb * m + jj
            ma, mask_c, mask_p = _attn_masks(block > 0)
            mask_n = _attn_masks(block < nb - 1)[2]
            qhat, rq = _head_norm(qr_ref[here, :], gq, ma)
            qn = qhat * gq
            qn_next = _pick(last, qnn_ref[edge, :], qn_ref[behind, :])
            khat, rk = _head_norm(kr_ref[here, :], gk, ma)
            kcb = (khat * gk).astype(BF)
            kpb = _pick(first, knp_ref[edge, :], kn_ref[before, :]).astype(BF)
            vcb = vc_ref[here, :].astype(BF)
            vpb = _pick(first, vp_ref[edge, :], vc_ref[before, :]).astype(BF)
            do_t, don_t = do_ref[here, :], _pick(last, don_ref[edge, :], do_ref[behind, :])
            c_t, cn_t = c_ref[here, :], _pick(last, cn_ref[edge, :], c_ref[behind, :])
            lse_t, lsen_t = lse_ref[here, :], _pick(last, lsen_ref[edge, :], lse_ref[behind, :])
            qs, dos = _stack_heads(qn, ma).astype(BF), _stack_heads(do_t, ma).astype(BF)
            lse_s, c_s = _stack_cols(lse_t, ma), _stack_cols(c_t, ma)
            s_c = jnp.where(mask_c, _dot(qs, kcb, "nt") * scale, MASK_VALUE)
            s_p = jnp.where(mask_p, _dot(qs, kpb, "nt") * scale, MASK_VALUE)
            p_c = jnp.exp(s_c - lse_s)
            p_p = jnp.exp(s_p - lse_s)
            ds_c = ((p_c * (_dot(dos, vcb, "nt") + c_s)) * scale).astype(BF)
            ds_p = ((p_p * (_dot(dos, vpb, "nt") + c_s)) * scale).astype(BF)
            dq_t = _unstack_heads(_dot(ds_c, kcb) + _dot(ds_p, kpb), ma)
            qs_n, dos_n = _stack_heads(qn_next, ma).astype(BF), _stack_heads(don_t, ma).astype(BF)
            s_n = jnp.where(mask_n, _dot(qs_n, kcb, "nt") * scale, MASK_VALUE)
            p_n = jnp.exp(s_n - _stack_cols(lsen_t, ma))
            ds_n = ((p_n * (_dot(dos_n, vcb, "nt") + _stack_cols(cn_t, ma))) * scale).astype(BF)
            dv_t = _dot(p_c.astype(BF), dos, "tn") + _dot(p_n.astype(BF), dos_n, "tn")
            dk_t = _dot(ds_c, qs, "tn") + _dot(ds_n, qs_n, "tn")
            sq_ref[here, :] = _head_norm_bwd(dq_t, qhat, rq, gq, ma)
            sk_ref[here, :] = _head_norm_bwd(dk_t, khat, rk, gk, ma)
            sv_ref[here, :] = dv_t
            dgq_ref[...] += jnp.sum(dq_t * qhat, axis=0, keepdims=True)
            dgk_ref[...] += jnp.sum(dk_t * khat, axis=0, keepdims=True)
            return carry

        lax.fori_loop(0, m * d, step, 0, unroll=ATTN_UNROLL)
        dq_ref[...] = sq_ref[...].astype(BF)
        dk_ref[...] = sk_ref[...].astype(BF)
        dv_ref[...] = sv_ref[...].astype(BF)

    def raw(col0):
        return pl.BlockSpec((m * sub, LANES), lambda j, t: (j, col0 + 2 * g + t))

    def cur(kind):
        return pl.BlockSpec((None, m * sub, LANES), lambda j, t: (KIND_TILES * kind + 2 * g + t, j, 0))

    def prv(kind):
        return pl.BlockSpec((None, sub, LANES), lambda j, t: (KIND_TILES * kind + 2 * g + t, jnp.maximum(j * m - 1, 0), 0))

    def nxt(kind):
        return pl.BlockSpec((None, sub, LANES),
                            lambda j, t: (KIND_TILES * kind + 2 * g + t, jnp.minimum((j + 1) * m, nb - 1), 0))

    own = pl.BlockSpec((m * sub, LANES), lambda j, t: (j, t))
    own_next = pl.BlockSpec((sub, LANES), lambda j, t: (jnp.minimum((j + 1) * m, nb - 1), t))
    vec = pl.BlockSpec((1, LANES), lambda j, t: (0, 0))
    return _pallas_call(
        body, name=name, grid=(nb // m, 2),
        in_specs=[raw(OFF_Q // LANES), raw(OFF_K // LANES), cur(2), prv(2), cur(0), nxt(0), cur(1), prv(1), own, own_next,
                  own, own_next,
                  own, own_next, pl.BlockSpec((8, LANES), lambda j, t: (0, 0))] + [ANY] * len(extra),
        out_specs=[own, own, own, vec, vec],
        out_shape=[jax.ShapeDtypeStruct((T, 256), BF)] * 3 + [jax.ShapeDtypeStruct((1, LANES), F32)] * 2,
        scratch_shapes=[pltpu.VMEM((m * sub, LANES), F32)] * 3,
        compiler_params=_params(("arbitrary", "arbitrary")),
    )(z, z, qkv, qkv, qkv, qkv, qkv, qkv, do, do, c, c, lse, lse, gains, *extra)


MERGE_ROWS = 256
GATE_TILE = 256


def _group_mix(o_refs, lse_refs):
    lses = [r[...] for r in lse_refs]
    m = jnp.maximum(jnp.maximum(lses[0], lses[1]), lses[2])
    es = [jnp.exp(l - m) for l in lses]
    den = es[0] + es[1] + es[2]
    ws = [e / den for e in es]
    y = ws[0] * o_refs[0][...] + ws[1] * o_refs[1][...] + ws[2] * o_refs[2][...]
    return ws, y


def _sigmoid(v):
    return 1.0 / (1.0 + jnp.exp(-v))


def _merge_specs(T, z, bgate, gpu, gco, gau):
    tm = min(MERGE_ROWS, T)
    row = lambda w: pl.BlockSpec((tm, w), lambda i: (i, 0))
    gate0 = OFF_GATE // GATE_TILE
    gates = [pl.BlockSpec((tm, GATE_TILE), functools.partial(lambda i, cb: (i, cb), cb=gate0 + n))
             for n in range(3 * N_CHIPS)]
    full = lambda a: pl.BlockSpec(a.shape, lambda i: (0,) * a.ndim)
    specs = [row(512), row(512)] + [row(256)] * 6 + gates + [full(bgate), full(gpu), full(gco), full(gau)]
    return tm, row, specs


def _merge_fwd(yp, yc, o3, lse3, z, bgate, gpu, gco, gau, name):
    T = yp.shape[0]
    tm, row, specs = _merge_specs(T, z, bgate, gpu, gco, gau)

    def body(*refs):
        yp_ref, yc_ref = refs[0], refs[1]
        o_refs, lse_refs = refs[2:5], refs[5:8]
        zg = refs[8:20]
        b_ref, gpu_ref, gco_ref, gau_ref, out_ref = refs[20:25]
        yab = _group_mix(o_refs, lse_refs)[1].astype(BF)
        ys = (yp_ref[...], yc_ref[...], yab)
        ups = (gpu_ref, gco_ref, gau_ref)
        for n in range(N_CHIPS):
            acc = None
            for b in range(3):
                gcol = slice(1024 * b + GATE_TILE * n, 1024 * b + GATE_TILE * (n + 1))
                gate = _sigmoid(zg[N_CHIPS * b + n][...] + b_ref[:, gcol])
                term = gate * _dot(ys[b], ups[b][n])
                acc = term if acc is None else acc + term
            out_ref[:, GATE_TILE * n:GATE_TILE * (n + 1)] = acc.astype(BF)

    return _pallas_call(
        body, name=name, grid=(T // tm,), in_specs=specs, out_specs=row(1024),
        out_shape=jax.ShapeDtypeStruct((T, 1024), BF), compiler_params=_params(("parallel",)),
    )(yp, yc, *o3, *lse3, *([z] * 12), bgate, gpu, gco, gau)


def _merge_bwd(dm, yp, yc, o3, lse3, z, bgate, gpu, gco, gau, name):
    T = yp.shape[0]
    tm, row, specs = _merge_specs(T, z, bgate, gpu, gco, gau)
    nsteps = T // tm

    def body(*refs):
        dm_ref, yp_ref, yc_ref = refs[0:3]
        o_refs, lse_refs = refs[3:6], refs[6:9]
        zg = refs[9:21]
        b_ref, gpu_ref, gco_ref, gau_ref = refs[21:25]
        dzg_ref, dyp_ref, dyc_ref = refs[25:28]
        do_refs, c_refs = refs[28:31], refs[31:34]
        dgpu_ref, dgco_ref, dgau_ref, dbg_ref = refs[34:38]
        accs = refs[38:41]
        i = pl.program_id(0)

        @pl.when(i == 0)
        def _():
            for a in accs:
                a[...] = jnp.zeros_like(a)
            dbg_ref[...] = jnp.zeros_like(dbg_ref)

        ws, y = _group_mix(o_refs, lse_refs)
        ys = (yp_ref[...], yc_ref[...], y.astype(BF))
        ups = (gpu_ref, gco_ref, gau_ref)
        dys = [None, None, None]
        for n in range(N_CHIPS):
            dmn = dm_ref[:, GATE_TILE * n:GATE_TILE * (n + 1)]
            for b in range(3):
                gcol = slice(1024 * b + GATE_TILE * n, 1024 * b + GATE_TILE * (n + 1))
                gate = _sigmoid(zg[N_CHIPS * b + n][...] + b_ref[:, gcol])
                up = _dot(ys[b], ups[b][n])
                dzg = (dmn * up) * (gate * (1.0 - gate))
                dzg_ref[:, gcol] = dzg.astype(BF)
                dbg_ref[:, gcol] += jnp.sum(dzg, axis=0, keepdims=True)
                dup = (dmn * gate).astype(BF)
                accs[b][n] += _dot(ys[b], dup, "tn")
                dyb = _dot(dup, ups[b][n], "nt")
                dys[b] = dyb if dys[b] is None else dys[b] + dyb
        dyp_ref[...] = dys[0]
        dyc_ref[...] = dys[1]
        dya = dys[2]
        lane = lax.broadcasted_iota(jnp.int32, dya.shape, 1) // HEAD_DIM
        pr = dya * y
        rho = jnp.zeros_like(pr)
        for h in range(256 // HEAD_DIM):
            hm = lane == h
            rho = jnp.where(hm, jnp.sum(jnp.where(hm, pr, 0.0), axis=-1, keepdims=True), rho)
        for g in range(3):
            do_refs[g][...] = ws[g] * dya
            c_refs[g][...] = -(ws[g] * rho)

        @pl.when(i == nsteps - 1)
        def _():
            dgpu_ref[...] = accs[0][...].astype(BF)
            dgco_ref[...] = accs[1][...].astype(BF)
            dgau_ref[...] = accs[2][...].astype(BF)

    full = lambda a: pl.BlockSpec(a.shape, lambda i: (0,) * a.ndim)
    dz_gate = pl.BlockSpec((pl.Element(tm), pl.Element(3072)), lambda i: (i * tm, OFF_GATE))
    out_specs = ([dz_gate, row(512), row(512)] + [row(256)] * 6 + [full(gpu), full(gco), full(gau)]
                 + [pl.BlockSpec((1, 3072), lambda i: (0, 0))])
    out_shape = ([jax.ShapeDtypeStruct(z.shape, BF)] + [jax.ShapeDtypeStruct((T, 512), F32)] * 2
                 + [jax.ShapeDtypeStruct((T, 256), F32)] * 6
                 + [jax.ShapeDtypeStruct(g.shape, BF) for g in (gpu, gco, gau)]
                 + [jax.ShapeDtypeStruct((1, 3072), F32)])
    return _pallas_call(
        body, name=name, grid=(nsteps,), in_specs=[row(1024)] + specs, out_specs=out_specs, out_shape=out_shape,
        scratch_shapes=[pltpu.VMEM(g.shape, F32) for g in (gpu, gco, gau)],
        compiler_params=_params(("arbitrary",)),
    )(dm, yp, yc, *o3, *lse3, *([z] * 12), bgate, gpu, gco, gau)


def _layer_fwd(x, w, tag, after=None, soon=None, late=None):
    hb = _rms_fwd(x, w["norm_mix"], f"rms_mix_{tag}", after=after)
    if soon is not None:
        w = dict(w, **soon(hb))
    z = _mm(hb, w["w_in"], "nt", f"in_proj_{tag}", tm=512, tn=3712, tk=1024, n_outer=True)
    yp, yc = _poolconv_fwd(z, w["pool_mix"], w["pool_scale"], w["conv_w"], f"poolconv_{tag}")
    qkv = _qk_norm(z, w["qk_gain"], f"qk_norm_{tag}")
    o3, lse3 = [], []
    for g, d in enumerate(ATTN_DILATIONS):
        o, lse = _attn_fwd(qkv, g, d, f"attn{g}_{tag}")
        o3.append(o)
        lse3.append(lse)
    if late is not None:
        w = dict(w, **late(lse3[-1]))
    merged = _merge_fwd(yp, yc, o3, lse3, z, w["b_gate"], w["w_pool_up"], w["w_conv_out"], w["w_attn_up"],
                        f"merge_{tag}")
    x1 = _mm(merged, w["w_o"], "nn", f"out_proj_{tag}", tm=1024, tn=1024, tk=1024, res=x)
    h2b = _rms_fwd(x1, w["norm_mlp"], f"rms_mlp_{tag}")
    rb = _mm(h2b, w["w_ff1"], "nn", f"ff1_{tag}", tm=1024, tn=1024, tk=1024, out_dtype=BF, epi="relu2", n_outer=True,
             b_shards=True)
    x2 = _mm(rb, w["w_ff2"], "nn", f"ff2_{tag}", tm=512, tn=1024, tk=4096, res=x1)
    saved = dict(x=x, hb=hb, z=z, yp=yp, yc=yc, qkv=qkv, o3=o3, lse3=lse3, merged=merged, x1=x1, h2b=h2b, rb=rb)
    return x2, saved, w


def _layer_bwd(dx2, w, s, tag, after=None, mid=None, tail=None):
    g = {}
    dab = _mm(dx2, w["w_ff2"], "nt", f"d_ff2_act_{tag}", tm=1024, tn=1024, tk=1024, out_dtype=BF, aux=s["rb"],
              epi="drelu2", after=after)
    g["w_ff2"] = _mm(s["rb"], dx2, "tn", f"d_ff2_w_{tag}", tm=1024, tn=1024, tk=2048, out_dtype=BF)
    g["w_ff1"] = _mm(s["h2b"], dab, "tn", f"d_ff1_w_{tag}", tm=1024, tn=1024, tk=2048, out_dtype=BF, out_shards=True)
    dh2 = _mm(dab, w["w_ff1"], "nt", f"d_ff1_act_{tag}", tm=1024, tn=1024, tk=1024, b_shards=True)
    dx1, g["norm_mlp"] = _rms_bwd(dh2, s["x1"], w["norm_mlp"], dx2, f"d_rms_mlp_{tag}")
    dm = _mm(dx1, w["w_o"], "nt", f"d_out_act_{tag}", tm=1024, tn=1024, tk=1024)
    g["w_o"] = _mm(s["merged"], dx1, "tn", f"d_out_w_{tag}", tm=1024, tn=1024, tk=1024, out_dtype=BF)
    (dz, dyp, dyc, do0, do1, do2, c0, c1, c2, g["w_pool_up"], g["w_conv_out"], g["w_attn_up"],
     g["b_gate"]) = _merge_bwd(dm, s["yp"], s["yc"], s["o3"], s["lse3"], s["z"], w["b_gate"], w["w_pool_up"],
                               w["w_conv_out"], w["w_attn_up"], f"d_merge_{tag}")
    behind = mid(g) if mid is not None else None
    dq, dk, dv = [], [], []
    dgq = dgk = None
    for gi, d in enumerate(ATTN_DILATIONS):
        dzq, dzk, dzv, pq, pk = _attn_bwd(s["z"], s["qkv"], (do0, do1, do2)[gi], (c0, c1, c2)[gi], s["lse3"][gi],
                                          w["qk_gain"], gi, d, f"d_attn{gi}_{tag}", after=behind)
        dq.append(dzq)
        dk.append(dzk)
        dv.append(dzv)
        dgq = pq if dgq is None else dgq + pq
        dgk = pk if dgk is None else dgk + pk
    g["q_gain"] = dgq[:, :HEAD_DIM] + dgq[:, HEAD_DIM:]
    g["k_gain"] = dgk[:, :HEAD_DIM] + dgk[:, HEAD_DIM:]
    for off, pieces in ((OFF_Q, dq), (OFF_K, dk), (OFF_V, dv)):
        for gi, piece in enumerate(pieces):
            dz = lax.dynamic_update_slice(dz, piece, (0, off + 256 * gi))
    dz, g["pool_mix"], g["pool_scale"], g["conv_w"] = _poolconv_bwd(
        s["z"], dyp, dyc, w["pool_mix"], w["pool_scale"], w["conv_w"], dz, f"d_poolconv_{tag}")
    g["w_in"] = _mm(s["hb"], dz, "tn", f"d_in_w_{tag}", tm=512, tn=3712, tk=1024, out_dtype=BF)
    dh = _mm(dz, w["w_in"], "nn", f"d_in_act_{tag}", tm=1024, tn=1024, tk=3712,
             after=tail(g) if tail is not None else None)
    dx, g["norm_mix"] = _rms_bwd(dh, s["x"], w["norm_mix"], dx1, f"d_rms_mix_{tag}")
    return dx, g


def _position():
    x, y, c = lax.axis_index("x"), lax.axis_index("y"), lax.axis_index("c")
    chips = [(1 - x, y), (x, 1 - y), (1 - x, 1 - y)]
    return x, y, c, 2 * x + y, chips, [2 * cx + cy for cx, cy in chips]


def _remote(src, dst, ssem, rsem, dev):
    return pltpu.make_async_remote_copy(src_ref=src, dst_ref=dst, send_sem=ssem, recv_sem=rsem, device_id=dev,
                                        device_id_type=MESH_ID)


def _position_operand():
    x, y, c = lax.axis_index("x"), lax.axis_index("y"), lax.axis_index("c")
    return jnp.stack([2 * x + y, c]).astype(jnp.int32)


def _halves(a):
    return a.reshape(a.shape[0], 2, a.shape[1] // 2, a.shape[2])


SEM = pl.BlockSpec(memory_space=pltpu.SEMAPHORE)
TOKEN = jax.ShapeDtypeStruct((8, LANES), F32)
TOKEN_SPEC = pl.BlockSpec(memory_space=pltpu.VMEM)


def _split_params():
    return pltpu.CompilerParams(has_side_effects=pltpu.SideEffectType.DATAFLOW_SIDE_EFFECTING)


def _gather_start(bufs, name, after):
    n = len(bufs)
    views = [_halves(b) for b in bufs]

    def body(*refs):
        first_sem = n + 1
        ssem, rsem = refs[first_sem:first_sem + ns], refs[first_sem + ns:first_sem + 2 * ns]
        outs, token = refs[first_sem + 2 * ns:first_sem + 2 * ns + n], refs[first_sem + 2 * ns + n]
        x, y, c, q, chips, qs = _position()
        for k in range(n):
            mine = outs[k].at[q, c]
            for j, chip in enumerate(chips):
                _remote(mine, mine, ssem[3 * k + j], rsem[3 * k + j], (chip[0], chip[1], c)).start()
        token[...] = jnp.zeros_like(token)

    ns = 3 * n
    outs = _pallas_call(
        body, name=name, in_specs=[ANY] * (n + 1), out_specs=[SEM] * (2 * ns) + [ANY] * n + [TOKEN_SPEC],
        out_shape=[pltpu.SemaphoreType.DMA(())] * (2 * ns) + [jax.ShapeDtypeStruct(v.shape, v.dtype) for v in views]
        + [TOKEN],
        input_output_aliases={k: k + 2 * ns for k in range(n)}, compiler_params=_split_params(),
    )(*views, after)
    return list(outs[:ns]), list(outs[ns:2 * ns]), list(outs[2 * ns:2 * ns + n]), outs[2 * ns + n]


def _gather_finish(ssem, rsem, views, after, name_wait, name_forward, shapes):
    n = len(views)
    ns = len(ssem)

    def wait_body(*refs):
        ssem_ref, rsem_ref = refs[n:n + ns], refs[n + ns:n + 2 * ns]
        outs = refs[n + 2 * ns + 1:]
        x, y, c, q, chips, qs = _position()
        for k in range(n):
            for j, chip in enumerate(chips):
                cp = _remote(outs[k].at[q, c], outs[k].at[qs[j], c], ssem_ref[3 * k + j], rsem_ref[3 * k + j],
                             (chip[0], chip[1], c))
                cp.wait_send()
                cp.wait_recv()

    landed = _pallas_call(
        wait_body, name=name_wait, in_specs=[ANY] * n + [SEM] * (2 * ns) + [ANY], out_specs=[ANY] * n,
        out_shape=[jax.ShapeDtypeStruct(v.shape, v.dtype) for v in views],
        input_output_aliases={k: k for k in range(n)}, compiler_params=_split_params(),
    )(*views, *ssem, *rsem, after)

    def forward_body(*refs):
        outs = refs[n:2 * n]
        fssem, frsem = refs[2 * n:]
        x, y, c, q, chips, qs = _position()
        sib = (x, y, 1 - c)
        sent = []
        for k in range(n):
            for j in range(3):
                slot = outs[k].at[qs[j], c]
                cp = _remote(slot, slot, fssem.at[k, j], frsem.at[k, j], sib)
                cp.start()
                sent.append(cp)
        for k in range(n):
            for j in range(3):
                slot = outs[k].at[qs[j], 1 - c]
                _remote(slot, slot, fssem.at[k, j], frsem.at[k, j], sib).wait_recv()
        for cp in sent:
            cp.wait_send()

    outs = _pallas_call(
        forward_body, name=name_forward, in_specs=[ANY] * n, out_specs=[ANY] * n,
        out_shape=[jax.ShapeDtypeStruct(v.shape, v.dtype) for v in views],
        input_output_aliases={k: k for k in range(n)}, scratch_shapes=[pltpu.SemaphoreType.DMA((n, 3))] * 2,
    )(*landed)
    return [o.reshape(s) for o, s in zip(outs, shapes)]


def _chip_exchange_start(parts, name):
    n = len(parts)

    def body(*refs):
        ssem, rsem = refs[n:n + ns], refs[n + ns:n + 2 * ns]
        base = n + 2 * ns
        srcs, outs, token = refs[base:base + n], refs[base + n:base + 2 * n], refs[base + 2 * n]
        x, y, c, q, chips, qs = _position()
        for k in range(n):
            for j, chip in enumerate(chips):
                _remote(srcs[k].at[qs[j]], outs[k].at[j], ssem[3 * k + j], rsem[3 * k + j],
                        (chip[0], chip[1], c)).start()
        token[...] = jnp.zeros_like(token)

    ns = 3 * n
    outs = _pallas_call(
        body, name=name, in_specs=[ANY] * n, out_specs=[SEM] * (2 * ns) + [ANY] * (2 * n) + [TOKEN_SPEC],
        out_shape=[pltpu.SemaphoreType.DMA(())] * (2 * ns) + [jax.ShapeDtypeStruct(a.shape, a.dtype) for a in parts]
        + [jax.ShapeDtypeStruct((3,) + a.shape[1:], a.dtype) for a in parts] + [TOKEN],
        input_output_aliases={k: k + 2 * ns for k in range(n)}, compiler_params=_split_params(),
    )(*parts)
    b = 2 * ns
    return list(outs[:ns]), list(outs[ns:b]), list(outs[b:b + n]), list(outs[b + n:b + 2 * n]), outs[b + 2 * n]


def _chip_exchange_wait(ssem, rsem, parts, landing, after, name):
    n = len(parts)
    ns = len(ssem)

    def body(*refs):
        ssem_ref, rsem_ref = refs[2 * n:2 * n + ns], refs[2 * n + ns:2 * n + 2 * ns]
        base = 2 * n + 2 * ns + 1
        srcs, outs = refs[base:base + n], refs[base + n:]
        x, y, c, q, chips, qs = _position()
        for k in range(n):
            for j, chip in enumerate(chips):
                cp = _remote(srcs[k].at[qs[j]], outs[k].at[j], ssem_ref[3 * k + j], rsem_ref[3 * k + j],
                             (chip[0], chip[1], c))
                cp.wait_send()
                cp.wait_recv()

    outs = _pallas_call(
        body, name=name, in_specs=[ANY] * (2 * n) + [SEM] * (2 * ns) + [ANY], out_specs=[ANY] * (2 * n),
        out_shape=[jax.ShapeDtypeStruct(a.shape, a.dtype) for a in list(parts) + list(landing)],
        input_output_aliases={k: k for k in range(2 * n)}, compiler_params=_split_params(),
    )(*parts, *landing, *ssem, *rsem, after)
    return list(outs[:n]), list(outs[n:])


def _pair_swap(views, name):
    n = len(views)

    def body(*refs):
        ins, outs = refs[:n], refs[n:2 * n]
        ssem, rsem = refs[2 * n:]
        x, y, c, _, _, _ = _position()
        cps = [_remote(ins[k].at[pl.ds(0, N_CHIPS), 1 - c], outs[k], ssem.at[k], rsem.at[k], (x, y, 1 - c))
               for k in range(n)]
        for cp in cps:
            cp.start()
        for cp in cps:
            cp.wait()

    return _pallas_call(
        body, name=name, in_specs=[ANY] * n, out_specs=[ANY] * n,
        out_shape=[jax.ShapeDtypeStruct((v.shape[0],) + v.shape[2:], v.dtype) for v in views],
        scratch_shapes=[pltpu.SemaphoreType.DMA((n,))] * 2,
    )(*views)


def _chip_exchange(parts, name):
    n = len(parts)

    def body(*refs):
        ins, outs = refs[:n], refs[n:2 * n]
        ssem, rsem = refs[2 * n:]
        x, y, c, q, chips, qs = _position()
        cps = []
        for k in range(n):
            for j, chip in enumerate(chips):
                cp = _remote(ins[k].at[qs[j]], outs[k].at[j], ssem.at[k, j], rsem.at[k, j], (chip[0], chip[1], c))
                cp.start()
                cps.append(cp)
        for cp in cps:
            cp.wait_recv()
        for cp in cps:
            cp.wait_send()

    return _pallas_call(
        body, name=name, in_specs=[ANY] * n, out_specs=[ANY] * n,
        out_shape=[jax.ShapeDtypeStruct((3,) + a.shape[1:], a.dtype) for a in parts],
        scratch_shapes=[pltpu.SemaphoreType.DMA((n, 3))] * 2,
    )(*parts)


def _pair_send(arrays, name):
    n = len(arrays)

    def body(*refs):
        ins, outs = refs[:n], refs[n:2 * n]
        ssem, rsem = refs[2 * n:]
        x, y, c, _, _, _ = _position()
        cps = [_remote(ins[k], outs[k], ssem.at[k], rsem.at[k], (x, y, 1 - c)) for k in range(n)]
        for cp in cps:
            cp.start()
        for cp in cps:
            cp.wait()

    return _pallas_call(
        body, name=name, in_specs=[ANY] * n, out_specs=[ANY] * n,
        out_shape=[jax.ShapeDtypeStruct(a.shape, a.dtype) for a in arrays],
        scratch_shapes=[pltpu.SemaphoreType.DMA((n,))] * 2,
    )(*arrays)


def _all_to_all_small(part):
    P = part.shape[0]

    def body(in_ref, out_ref, lsem, ssem, rsem):
        x, y, c = lax.axis_index("x"), lax.axis_index("y"), lax.axis_index("c")
        me = 4 * x + 2 * y + c
        flips = [(fx, fy, fc) for fx in (0, 1) for fy in (0, 1) for fc in (0, 1)][1:]
        peers = [((x + fx) % 2, (y + fy) % 2, (c + fc) % 2) for fx, fy, fc in flips]
        loc = pltpu.make_async_copy(in_ref, out_ref.at[me], lsem)
        loc.start()
        cps = [_remote(in_ref, out_ref.at[me], ssem.at[j], rsem.at[j], peer) for j, peer in enumerate(peers)]
        for cp in cps:
            cp.start()
        for j, (px, py, pc) in enumerate(peers):
            _remote(in_ref, out_ref.at[4 * px + 2 * py + pc], ssem.at[j], rsem.at[j], peers[j]).wait_recv()
        for cp in cps:
            cp.wait_send()
        loc.wait()

    return _pallas_call(
        body, name="small_exchange", in_specs=[ANY], out_specs=ANY,
        out_shape=jax.ShapeDtypeStruct((8, P, LANES), F32),
        scratch_shapes=[pltpu.SemaphoreType.DMA(())] + [pltpu.SemaphoreType.DMA((7,))] * 2,
    )(part)


def _small_peers():
    x, y, c = lax.axis_index("x"), lax.axis_index("y"), lax.axis_index("c")
    flips = [(fx, fy, fc) for fx in (0, 1) for fy in (0, 1) for fc in (0, 1)][1:]
    peers = [((x + fx) % 2, (y + fy) % 2, (c + fc) % 2) for fx, fy, fc in flips]
    return 4 * x + 2 * y + c, peers


def _all_to_all_small_start(part, name):
    P = part.shape[0]
    me = 4 * lax.axis_index("x") + 2 * lax.axis_index("y") + lax.axis_index("c")
    landing = lax.dynamic_update_slice(jnp.zeros((8, P, LANES), F32), part[None], (me, 0, 0))

    def body(*refs):
        sems, src, land, token = refs[2:16], refs[16], refs[17], refs[18]
        me_, peers = _small_peers()
        for j, peer in enumerate(peers):
            _remote(src, land.at[me_], sems[j], sems[7 + j], peer).start()
        token[...] = jnp.zeros_like(token)

    outs = _pallas_call(
        body, name=name, in_specs=[ANY, ANY], out_specs=[SEM] * 14 + [ANY, ANY, TOKEN_SPEC],
        out_shape=[pltpu.SemaphoreType.DMA(())] * 14 + [jax.ShapeDtypeStruct(part.shape, F32),
                                                       jax.ShapeDtypeStruct((8, P, LANES), F32), TOKEN],
        input_output_aliases={0: 14, 1: 15}, compiler_params=_split_params(),
    )(part, landing)
    return list(outs[:7]), list(outs[7:14]), outs[14], outs[15], outs[16]


def _all_to_all_small_wait(ssem, rsem, part, landing, after, name):
    def body(*refs):
        sems, src, land = refs[2:16], refs[17], refs[18]
        _, peers = _small_peers()
        for j, (px, py, pc) in enumerate(peers):
            cp = _remote(src, land.at[4 * px + 2 * py + pc], sems[j], sems[7 + j], peers[j])
            cp.wait_send()
            cp.wait_recv()

    return _pallas_call(
        body, name=name, in_specs=[ANY, ANY] + [SEM] * 14 + [ANY], out_specs=[ANY, ANY],
        out_shape=[jax.ShapeDtypeStruct(part.shape, F32), jax.ShapeDtypeStruct(landing.shape, F32)],
        input_output_aliases={0: 0, 1: 1}, compiler_params=_split_params(),
    )(part, landing, *ssem, *rsem, after)[1]


def _row_tile(rows, width, n_arrays):
    t = rows
    while t % 2 == 0 and t > 8 and 2 * n_arrays * t * width * 4 > VMEM_LIMIT // 2:
        t //= 2
    return t


def _scalar_grid(grid, in_specs, out_specs):
    return pltpu.PrefetchScalarGridSpec(num_scalar_prefetch=1, grid=grid, in_specs=in_specs, out_specs=out_specs)


def _cast_place(w3, layer, pos, name):
    _, r, c = w3.shape
    tr = _row_tile(r, c, 2)

    def body(pos_ref, w_ref, o_ref):
        o_ref[...] = w_ref[...].astype(BF)

    return _pallas_call(
        body, name=name,
        grid_spec=_scalar_grid((r // tr,), [pl.BlockSpec((None, tr, c), lambda i, pos: (layer, i, 0))],
                               pl.BlockSpec((None, tr, c), lambda i, pos: (pos[0], i, 0))),
        out_shape=jax.ShapeDtypeStruct((N_CHIPS, r, c), BF), compiler_params=_params(("parallel",)),
    )(pos, w3)


def _pair_sum(view, recv, pos, name):
    _, _, hr, c = view.shape
    tr = _row_tile(hr, c, 3)

    def body(pos_ref, g_ref, r_ref, o_ref):
        o_ref[...] = (g_ref[...].astype(F32) + r_ref[...].astype(F32)).astype(BF)

    blk = pl.BlockSpec((None, tr, c), lambda p, i, pos: (p, i, 0))
    return _pallas_call(
        body, name=name,
        grid_spec=_scalar_grid((N_CHIPS, hr // tr),
                               [pl.BlockSpec((None, None, tr, c), lambda p, i, pos: (p, pos[1], i, 0)), blk], blk),
        out_shape=jax.ShapeDtypeStruct(recv.shape, BF), compiler_params=_params(("parallel", "parallel")),
    )(pos, view, recv)


def _chip_sum(parts, recv, pos, name):
    _, hr, c = parts.shape
    tr = _row_tile(hr, c, 6)

    def body(pos_ref, p_ref, r_ref, o_ref):
        acc = p_ref[...].astype(F32)
        for j in range(3):
            acc = acc + r_ref[j].astype(F32)
        o_ref[...] = acc

    return _pallas_call(
        body, name=name,
        grid_spec=_scalar_grid((hr // tr,),
                               [pl.BlockSpec((None, tr, c), lambda i, pos: (pos[0], i, 0)),
                                pl.BlockSpec((3, tr, c), lambda i, pos: (0, i, 0))],
                               pl.BlockSpec((tr, c), lambda i, pos: (i, 0))),
        out_shape=jax.ShapeDtypeStruct((hr, c), F32), compiler_params=_params(("parallel",)),
    )(pos, parts, recv)


def _sum_slices(a, name):
    n, rows, width = a.shape
    tr = _row_tile(rows, width, n + 1)

    def body(a_ref, o_ref):
        acc = a_ref[0].astype(F32)
        for i in range(1, n):
            acc = acc + a_ref[i].astype(F32)
        o_ref[...] = acc

    return _pallas_call(
        body, name=name, grid=(rows // tr,), in_specs=[pl.BlockSpec((n, tr, width), lambda i: (0, i, 0))],
        out_specs=pl.BlockSpec((tr, width), lambda i: (i, 0)), out_shape=jax.ShapeDtypeStruct((rows, width), F32),
        compiler_params=_params(("parallel",)),
    )(a)


def _adamw_update(w, g, m, v):
    nm = ADAM_B1 * m + (1.0 - ADAM_B1) * g
    nv = ADAM_B2 * v + (1.0 - ADAM_B2) * (g * g)
    m_hat = nm / (1.0 - ADAM_B1 ** ADAM_STEP)
    v_hat = nv / (1.0 - ADAM_B2 ** ADAM_STEP)
    return -ADAM_LR * (m_hat / (jnp.sqrt(v_hat) + ADAM_EPS) + ADAM_WD * w), nm, nv


def _adamw(w, g, m, v, name):
    rows, width = w.shape
    tr = _row_tile(rows, width, 7)

    def body(w_ref, g_ref, m_ref, v_ref, d_ref, nm_ref, nv_ref):
        d_ref[...], nm_ref[...], nv_ref[...] = _adamw_update(w_ref[...], g_ref[...], m_ref[...], v_ref[...])

    blk = pl.BlockSpec((tr, width), lambda i: (i, 0))
    return _pallas_call(
        body, name=name, grid=(rows // tr,), in_specs=[blk] * 4, out_specs=[blk] * 3,
        out_shape=[jax.ShapeDtypeStruct((rows, width), F32)] * 3, compiler_params=_params(("parallel",)),
    )(w, g, m, v)


def _adamw_halves(w3, m3, v3, mine, other, pos, name):
    depth, r, c = w3.shape
    assert depth == 2
    hr = r // 2
    tr = _row_tile(hr, c, 11)
    sources = ((0, True, mine[0]), (0, False, other[0]), (1, True, mine[1]), (1, False, other[1]))

    def active(l, h, core, layer, own):
        mine_half = h == core
        return (l == layer) & (mine_half if own else jnp.logical_not(mine_half))

    def body(pos_ref, w_ref, m_ref, v_ref, *rest):
        g_refs, (go_ref, d_ref, nm_ref, nv_ref) = rest[:4], rest[4:]
        l, h = pl.program_id(0), pl.program_id(1)
        for (layer, own, _), g_ref in zip(sources, g_refs):
            @pl.when(active(l, h, pos_ref[1], layer, own))
            def _():
                gv = g_ref[...]
                go_ref[...] = gv
                d_ref[...], nm_ref[...], nv_ref[...] = _adamw_update(w_ref[...], gv, m_ref[...], v_ref[...])

    def gspec(layer, own):
        return pl.BlockSpec((tr, c), lambda l, h, i, pos: (jnp.where(active(l, h, pos[1], layer, own), i, 0), 0))

    blk = pl.BlockSpec((None, None, tr, c), lambda l, h, i, pos: (l, h, i, 0))
    view = lambda a: a.reshape(depth, 2, hr, c)
    outs = _pallas_call(
        body, name=name,
        grid_spec=_scalar_grid((depth, 2, hr // tr), [blk] * 3 + [gspec(layer, own) for layer, own, _ in sources],
                               [blk] * 4),
        out_shape=[jax.ShapeDtypeStruct((depth, 2, hr, c), F32)] * 4,
        compiler_params=_params(("parallel", "parallel", "parallel")),
    )(pos, view(w3), view(m3), view(v3), *[s[2] for s in sources])
    return [o.reshape(w3.shape) for o in outs]


BIG = ("w_in", "w_pool_up", "w_conv_out", "w_attn_up", "w_o", "w_ff1", "w_ff2")
SMALL = ("norm_mix", "b_gate", "pool_mix", "pool_scale", "conv_w", "q_gain", "k_gain", "norm_mlp")
ORDER = ("norm_mix", "w_in", "b_gate", "pool_mix", "pool_scale", "conv_w", "q_gain", "k_gain", "w_pool_up",
         "w_conv_out", "w_attn_up", "w_o", "norm_mlp", "w_ff1", "w_ff2")
COLUMN_SHARDED = ("w_pool_up", "w_conv_out", "w_attn_up", "w_ff1")


def _matrix_weights(gathered):
    w = {}
    for name, g4 in gathered.items():
        if name in COLUMN_SHARDED:
            w[name] = g4
        else:
            w[name] = g4.reshape(N_CHIPS * g4.shape[1], g4.shape[2])
    return w


def _small_weights(l, small):
    w = {}
    w["norm_mix"] = small["norm_mix"][l][None]
    w["norm_mlp"] = small["norm_mlp"][l][None]
    w["b_gate"] = small["b_gate"][l][None]
    w["pool_mix"] = small["pool_mix"][l].astype(BF)
    w["pool_scale"] = small["pool_scale"][l][None]
    w["conv_w"] = jnp.pad(small["conv_w_full"][l], ((0, 5), (0, 0)))
    w["qk_gain"] = jnp.pad(jnp.stack([jnp.tile(small["q_gain"][l], 2), jnp.tile(small["k_gain"][l], 2)]), ((0, 6), (0, 0)))
    return w


def _to_chip_major(name, g):
    if name == "w_in":
        return g.T.reshape(N_CHIPS, g.shape[1] // N_CHIPS, g.shape[0])
    if name in COLUMN_SHARDED:
        return g
    return g.reshape(N_CHIPS, g.shape[0] // N_CHIPS, g.shape[1])


def _pad8(a):
    a = a.reshape(-1)
    return jnp.pad(a, (0, (-a.size) % (8 * LANES))).reshape(-1, LANES)


def kernel(x, norm_mix, w_in, b_gate, pool_mix, pool_scale, conv_w, q_gain, k_gain, w_pool_up, w_conv_out, w_attn_up, w_o, norm_mlp, w_ff1, w_ff2, loss_target, m_norm_mix, m_w_in, m_b_gate, m_pool_mix, m_pool_scale, m_conv_w, m_q_gain, m_k_gain, m_w_pool_up, m_w_conv_out, m_w_attn_up, m_w_o, m_norm_mlp, m_w_ff1, m_w_ff2, v_norm_mix, v_w_in, v_b_gate, v_pool_mix, v_pool_scale, v_conv_w, v_q_gain, v_k_gain, v_w_pool_up, v_w_conv_out, v_w_attn_up, v_w_o, v_norm_mlp, v_w_ff1, v_w_ff2):
    weights = dict(norm_mix=norm_mix, w_in=w_in, b_gate=b_gate, pool_mix=pool_mix, pool_scale=pool_scale, conv_w=conv_w,
                   q_gain=q_gain, k_gain=k_gain, w_pool_up=w_pool_up, w_conv_out=w_conv_out, w_attn_up=w_attn_up,
                   w_o=w_o, norm_mlp=norm_mlp, w_ff1=w_ff1, w_ff2=w_ff2)
    moms = dict(norm_mix=m_norm_mix, w_in=m_w_in, b_gate=m_b_gate, pool_mix=m_pool_mix, pool_scale=m_pool_scale,
                conv_w=m_conv_w, q_gain=m_q_gain, k_gain=m_k_gain, w_pool_up=m_w_pool_up, w_conv_out=m_w_conv_out,
                w_attn_up=m_w_attn_up, w_o=m_w_o, norm_mlp=m_norm_mlp, w_ff1=m_w_ff1, w_ff2=m_w_ff2)
    vels = dict(norm_mix=v_norm_mix, w_in=v_w_in, b_gate=v_b_gate, pool_mix=v_pool_mix, pool_scale=v_pool_scale,
                conv_w=v_conv_w, q_gain=v_q_gain, k_gain=v_k_gain, w_pool_up=v_w_pool_up, w_conv_out=v_w_conv_out,
                w_attn_up=v_w_attn_up, w_o=v_w_o, norm_mlp=v_norm_mlp, w_ff1=v_w_ff1, w_ff2=v_w_ff2)
    depth = norm_mix.shape[0]
    q = 2 * lax.axis_index("x") + lax.axis_index("y")
    pos = _position_operand()
    for group in (weights, moms, vels):
        group["w_in"] = jnp.swapaxes(group["w_in"], 1, 2)

    assert depth == 2, "the second layer's gather hides behind the first layer's forward, and likewise backward"
    first, rest = BIG[:1], BIG[1:]
    cw_all = _all_to_all_small(_pad8(conv_w))
    bufs = [{n: _cast_place(weights[n], 0, pos, f"cast_{n}_l0") for n in first}]
    a_ssem, a_rsem, a_views, a_token = _gather_start([bufs[0][n] for n in first], "gather_start_l0_in", cw_all)
    bufs[0].update({n: _cast_place(weights[n], 0, pos, f"cast_{n}_l0") for n in rest})
    bufs += [{n: _cast_place(weights[n], l, pos, f"cast_{n}_l{l}") for n in BIG} for l in range(1, depth)]
    b_ssem, b_rsem, b_views, b_token = _gather_start([bufs[0][n] for n in rest], "gather_start_l0_rest", a_token)
    g_ssem, g_rsem, g_views, g_token = _gather_start([bufs[1][n] for n in BIG], "gather_start_l1", b_token)
    conv_w_full = jnp.concatenate(
        [cw_all[2 * p].reshape(-1)[:conv_w.size].reshape(conv_w.shape) for p in range(N_CHIPS)], axis=-1)
    small = dict(weights)
    small["conv_w_full"] = conv_w_full

    def soon_weights(t):
        got = _gather_finish(a_ssem, a_rsem, a_views, t, "gather_wait_l0_in", "gather_forward_l0_in",
                             [bufs[0][n].shape for n in first])
        return _matrix_weights(dict(zip(first, got)))

    def late_weights(t):
        got = _gather_finish(b_ssem, b_rsem, b_views, t, "gather_wait_l0_rest", "gather_forward_l0_rest",
                             [bufs[0][n].shape for n in rest])
        return _matrix_weights(dict(zip(rest, got)))

    wl, saved = [None] * depth, [None] * depth
    h, saved[0], wl[0] = _layer_fwd(x[0], _small_weights(0, small), "l0", after=g_token, soon=soon_weights,
                                    late=late_weights)
    got = _gather_finish(g_ssem, g_rsem, g_views, h, "gather_wait_l1", "gather_forward_l1",
                         [bufs[1][n].shape for n in BIG])
    h, saved[1], wl[1] = _layer_fwd(h, dict(_small_weights(1, small), **_matrix_weights(dict(zip(BIG, got)))), "l1")
    dh, loss_row = _loss_grad(h, loss_target[0], "loss")

    def pair_stage(names, g, tag):
        views = [_halves(_to_chip_major(n, g[n])) for n in names]
        from_sibling = _pair_swap(views, f"grad_pair_swap_{tag}")
        return [_pair_sum(views[k], from_sibling[k], pos, f"pair_sum_{n}_{tag}") for k, n in enumerate(names)]

    mine, other = [{}, {}], [{}, {}]

    def finish(names, l, started, after, tag):
        ssem, rsem, parts, landing, _ = started
        parts, arrived = _chip_exchange_wait(ssem, rsem, parts, landing, after, f"grad_chip_exchange_wait_{tag}")
        got = [_chip_sum(parts[k], arrived[k], pos, f"chip_sum_{n}_{tag}") for k, n in enumerate(names)]
        mine[l].update(zip(names, got))
        other[l].update(zip(names, _pair_send(got, f"grad_pair_send_{tag}")))

    def small_pieces(g):
        return [_pad8(g[n][:3] if n == "conv_w" else g[n]) for n in SMALL]

    def start_small(l):
        return _all_to_all_small_start(jnp.concatenate(small_pieces(grads[l]), axis=0), f"small_grad_exchange_start_l{l}")

    grads, early, small = [None] * depth, {}, [None] * depth
    dh, grads[1] = _layer_bwd(dh, wl[1], saved[1], "l1")
    second = _chip_exchange_start(pair_stage(BIG, grads[1], "l1"), "grad_chip_exchange_start_l1")
    small[1] = start_small(1)

    def start_rest(g):
        early["rest"] = _chip_exchange_start(pair_stage(rest, g, "l0_rest"), "grad_chip_exchange_start_l0_rest")
        return early["rest"][4]

    def start_last(g):
        early["in"] = _chip_exchange_start(pair_stage(first, g, "l0_in"), "grad_chip_exchange_start_l0_in")
        return early["in"][4]

    dh, grads[0] = _layer_bwd(dh, wl[0], saved[0], "l0", after=[second[4], small[1][4]], mid=start_rest,
                              tail=start_last)
    small[0] = start_small(0)
    finish(BIG, 1, second, dh, "l1")
    finish(rest, 0, early["rest"], dh, "l0_rest")
    loss = lax.psum(loss_row[0, 0], ("x", "y", "c"))
    full = {}

    deltas, new_m, new_v = {}, {}, {}

    def update_matrix(n):
        full[n], deltas[n], new_m[n], new_v[n] = _adamw_halves(
            weights[n], moms[n], vels[n], [mine[l][n] for l in range(depth)], [other[l][n] for l in range(depth)], pos,
            f"adamw_{n}")

    for n in rest:
        update_matrix(n)
    finish(first, 0, early["in"], deltas[rest[-1]], "l0_in")
    for n in first:
        update_matrix(n)
    summed = []
    for l in range(depth):
        ssem, rsem, part, landing, _ = small[l]
        summed.append(_sum_slices(_all_to_all_small_wait(ssem, rsem, part, landing, deltas[first[-1]],
                                                         f"small_grad_exchange_wait_l{l}"), f"small_sum_l{l}"))
    row = 0
    for n, piece in zip(SMALL, small_pieces(grads[0])):
        size = (weights[n].size if n != "conv_w" else depth * 3 * 512) // depth
        flat = jnp.stack([s[row:row + piece.shape[0]].reshape(-1)[:size] for s in summed])
        row += piece.shape[0]
        if n == "conv_w":
            full[n] = lax.dynamic_slice_in_dim(flat.reshape(depth, 3, 512), q * conv_w.shape[2], conv_w.shape[2], axis=2)
        else:
            full[n] = flat.reshape(weights[n].shape)
    for n in SMALL:
        shape = weights[n].shape
        two_d = (-1, shape[-1]) if n not in ("conv_w", "q_gain", "k_gain") else (1, -1)
        d2, m2, v2 = _adamw(weights[n].reshape(two_d), full[n].reshape(two_d), moms[n].reshape(two_d),
                            vels[n].reshape(two_d), f"adamw_{n}")
        deltas[n], new_m[n], new_v[n] = d2.reshape(shape), m2.reshape(shape), v2.reshape(shape)
        full[n] = full[n].reshape(shape)
    for group in (full, deltas, new_m, new_v):
        group["w_in"] = jnp.swapaxes(group["w_in"], 1, 2)
    return (loss, dh[None], *[full[n] for n in ORDER], *[deltas[n] for n in ORDER], *[new_m[n] for n in ORDER],
            *[new_v[n] for n in ORDER])
```

```python
import functools

import jax
import jax.numpy as jnp
from jax import lax
from jax.experimental import pallas as pl
from jax.experimental.pallas import tpu as pltpu

F32 = jnp.float32
BF = jnp.bfloat16
MESH_ID = pl.DeviceIdType.MESH
ANY = pl.BlockSpec(memory_space=pl.ANY)

EPS = 1e-6
MASK_VALUE = -1e30
POOL_WINDOWS = (2, 4, 8, 16)
ATTN_DILATIONS = (1, 4, 16)
ATTN_BLOCK = 128
HEAD_DIM = 64
OFF_Q, OFF_K, OFF_V, OFF_GATE = 2048, 2816, 3584, 4352
N_CHIPS = 4
ADAM_LR, ADAM_B1, ADAM_B2, ADAM_EPS, ADAM_WD, ADAM_STEP = 0.001, 0.9, 0.999, 1e-08, 0.01, 10

VMEM_LIMIT = 48 * 1024 * 1024
LANES = 128

_DIMS = {"nn": (((1,), (0,)), ((), ())), "nt": (((1,), (1,)), ((), ())), "tn": (((0,), (0,)), ((), ()))}


def _params(sem):
    return pltpu.CompilerParams(dimension_semantics=sem, vmem_limit_bytes=VMEM_LIMIT)


def _pallas_call(body, **kw):
    call = pl.pallas_call(body, **kw)

    def run(*args):
        pinned = [pltpu.with_memory_space_constraint(a, pltpu.HBM)
                  if hasattr(a, "dtype") and jnp.issubdtype(a.dtype, jnp.floating) else a for a in args]
        return call(*pinned)

    return run


def _dot(a, b, mode="nn"):
    return lax.dot_general(a, b, _DIMS[mode], preferred_element_type=F32)


def _mm(a, b, mode, name, *, tm, tn, tk, out_dtype=F32, res=None, aux=None, epi=None, n_outer=False,
        b_shards=False, out_shards=False, after=None):
    if mode == "tn":
        K, M = a.shape
    else:
        M, K = a.shape
    if b_shards:
        if mode == "nn":
            assert b.shape[1] == K
            N = b.shape[2] * N_CHIPS
        else:
            assert mode == "nt"
            N = b.shape[1]
            assert b.shape[2] * N_CHIPS == K
    else:
        N = b.shape[0] if mode == "nt" else b.shape[1]
    tm, tn, tk = min(tm, M), min(tn, N), min(tk, K)
    assert M % tm == 0 and N % tn == 0 and K % tk == 0
    nk = K // tk
    if n_outer:
        grid = (N // tn, M // tm, nk)
        ij = lambda p, q_: (q_, p)
    else:
        grid = (M // tm, N // tn, nk)
        ij = lambda p, q_: (p, q_)

    def amap(p, q_, k):
        i, j = ij(p, q_)
        return (k, i) if mode == "tn" else (i, k)

    a_spec = pl.BlockSpec((tk, tm) if mode == "tn" else (tm, tk), amap)
    if b_shards:
        if mode == "nn":
            per = (N // N_CHIPS) // tn
            assert per >= 1 and (N // N_CHIPS) % tn == 0

            def bmap(p, q_, k):
                i, j = ij(p, q_)
                return (j // per, k, j % per)

            b_spec = pl.BlockSpec((None, tk, tn), bmap)
        else:
            per = (K // N_CHIPS) // tk
            assert per >= 1 and (K // N_CHIPS) % tk == 0

            def bmap(p, q_, k):
                i, j = ij(p, q_)
                return (k // per, j, k % per)

            b_spec = pl.BlockSpec((None, tn, tk), bmap)
    else:
        def bmap(p, q_, k):
            i, j = ij(p, q_)
            return (j, k) if mode == "nt" else (k, j)

        b_spec = pl.BlockSpec((tn, tk) if mode == "nt" else (tk, tn), bmap)

    def omap(p, q_, k):
        return ij(p, q_)

    o_spec = pl.BlockSpec((tm, tn), omap)
    if out_shards:
        per_o = (N // N_CHIPS) // tn
        assert per_o >= 1 and (N // N_CHIPS) % tn == 0

        def osmap(p, q_, k):
            i, j = ij(p, q_)
            return (j // per_o, i, j % per_o)

        out_spec0 = pl.BlockSpec((None, tm, tn), osmap)
        out_shape0 = jax.ShapeDtypeStruct((N_CHIPS, M, N // N_CHIPS), out_dtype)
    else:
        out_spec0 = o_spec
        out_shape0 = jax.ShapeDtypeStruct((M, N), out_dtype)

    in_specs = [a_spec, b_spec]
    args = [a, b]
    if res is not None:
        in_specs.append(o_spec)
        args.append(res)
    if aux is not None:
        in_specs.append(o_spec)
        args.append(aux)
    after = [] if after is None else list(after) if isinstance(after, (list, tuple)) else [after]
    in_specs += [ANY] * len(after)
    args += after
    out_specs = [out_spec0]
    out_shape = [out_shape0]
    n_out = len(out_shape)
    has_res, has_aux, n_after = res is not None, aux is not None, len(after)

    def body(*refs):
        a_ref, b_ref = refs[0], refs[1]
        pos = 2
        res_ref = aux_ref = None
        if has_res:
            res_ref = refs[pos]
            pos += 1
        if has_aux:
            aux_ref = refs[pos]
            pos += 1
        pos += n_after
        outs = refs[pos:pos + n_out]
        part = _dot(a_ref[...].astype(BF), b_ref[...].astype(BF), mode)

        def finish(acc):
            if res_ref is not None:
                acc = res_ref[...] + acc
            if epi == "relu2":
                r = jnp.maximum(acc, 0.0)
                outs[0][...] = (r * r).astype(out_dtype)
            elif epi == "drelu2":
                outs[0][...] = (acc * (2.0 * jnp.sqrt(aux_ref[...].astype(F32)))).astype(out_dtype)
            else:
                outs[0][...] = acc.astype(out_dtype)

        if nk == 1:
            finish(part)
        else:
            acc_ref = refs[pos + n_out]
            k = pl.program_id(2)

            @pl.when(k == 0)
            def _():
                acc_ref[...] = part

            @pl.when(k > 0)
            def _():
                acc_ref[...] += part

            @pl.when(k == nk - 1)
            def _():
                finish(acc_ref[...])

    scratch = [pltpu.VMEM((tm, tn), F32)] if nk > 1 else []
    out = _pallas_call(
        body, name=name, grid=grid, in_specs=in_specs, out_specs=out_specs, out_shape=out_shape,
        scratch_shapes=scratch, compiler_params=_params(("parallel", "parallel", "arbitrary")),
    )(*args)
    return out if n_out > 1 else out[0]


def _rms_fwd(x, gain, name, after=None):
    T, D = x.shape
    tm = min(512, T)

    def body(x_ref, g_ref, *rest):
        o_ref = rest[-1]
        xv = x_ref[...]
        r = lax.rsqrt(jnp.mean(xv * xv, axis=-1, keepdims=True) + EPS)
        o_ref[...] = ((xv * r) * g_ref[...]).astype(BF)

    extra = [] if after is None else list(after) if isinstance(after, (list, tuple)) else [after]
    return _pallas_call(
        body, name=name, grid=(T // tm,),
        in_specs=[pl.BlockSpec((tm, D), lambda i: (i, 0)), pl.BlockSpec((1, D), lambda i: (0, 0))] + [ANY] * len(extra),
        out_specs=pl.BlockSpec((tm, D), lambda i: (i, 0)), out_shape=jax.ShapeDtypeStruct((T, D), BF),
        compiler_params=_params(("parallel",)),
    )(x, gain, *extra)


def _rms_bwd(dh, x, gain, dres, name):
    T, D = x.shape
    tm = min(512, T)

    def body(dh_ref, x_ref, g_ref, dres_ref, dx_ref, dg_ref):
        xv = x_ref[...]
        r = lax.rsqrt(jnp.mean(xv * xv, axis=-1, keepdims=True) + EPS)
        xhat = xv * r
        dhv = dh_ref[...]
        dy = dhv * g_ref[...]
        dx_ref[...] = dres_ref[...] + r * (dy - xhat * jnp.mean(dy * xhat, axis=-1, keepdims=True))

        @pl.when(pl.program_id(0) == 0)
        def _():
            dg_ref[...] = jnp.zeros_like(dg_ref)

        dg_ref[...] += jnp.sum(dhv * xhat, axis=0, keepdims=True)

    row = pl.BlockSpec((tm, D), lambda i: (i, 0))
    vec = pl.BlockSpec((1, D), lambda i: (0, 0))
    return _pallas_call(
        body, name=name, grid=(T // tm,), in_specs=[row, row, vec, row], out_specs=[row, vec],
        out_shape=[jax.ShapeDtypeStruct((T, D), F32), jax.ShapeDtypeStruct((1, D), F32)],
        compiler_params=_params(("arbitrary",)),
    )(dh, x, gain, dres)


def _loss_grad(y, target, name):
    T, D = y.shape
    tm = min(512, T)

    def body(y_ref, t_ref, dy_ref, l_ref):
        e = y_ref[...] - t_ref[...]
        dy_ref[...] = e / float(D)

        @pl.when(pl.program_id(0) == 0)
        def _():
            l_ref[...] = jnp.zeros_like(l_ref)

        l_ref[...] += 0.5 * jnp.sum(jnp.mean(e * e, axis=-1, keepdims=True))

    row = pl.BlockSpec((tm, D), lambda i: (i, 0))
    return _pallas_call(
        body, name=name, grid=(T // tm,), in_specs=[row, row],
        out_specs=[row, pl.BlockSpec((1, LANES), lambda i: (0, 0))],
        out_shape=[jax.ShapeDtypeStruct((T, D), F32), jax.ShapeDtypeStruct((1, LANES), F32)],
        compiler_params=_params(("arbitrary",)),
    )(y, target)


POOL_HALO = 16
CONV_HALO = 8


def _causal_window_sum(v, w):
    s, sh = v, 1
    while sh < w:
        s = s + pltpu.roll(s, sh, 0)
        sh *= 2
    return s


def _anticausal_window_sum(v, w):
    n = v.shape[0]
    s, sh = v, 1
    while sh < w:
        s = s + pltpu.roll(s, n - sh, 0)
        sh *= 2
    return s


def _poolconv_fwd(z, pmix_b, pscale, convw, name):
    T = z.shape[0]
    R = min(512, T)
    PH, CH = R // POOL_HALO, R // CONV_HALO

    def body(u_ref, uh_ref, b_ref, c_ref, ch_ref, x_ref, xh_ref, mix_ref, sc_ref, cw_ref, yp_ref, yc_ref):
        i = pl.program_id(0)
        keep = (i > 0).astype(F32)
        row = i * R + lax.broadcasted_iota(jnp.int32, (R, 1), 0)
        w_all = jnp.concatenate([uh_ref[...] * keep, u_ref[...]], axis=0)
        for g, w in enumerate(POOL_WINDOWS):
            cols = slice(128 * g, 128 * (g + 1))
            wg = w_all[:, cols]
            s = _causal_window_sum(wg, w)[POOL_HALO:]
            cnt = jnp.minimum(row + 1, w).astype(F32)
            dgrp = s / cnt - wg[POOL_HALO:]
            y = _dot(dgrp.astype(BF), mix_ref[g]) * sc_ref[:, cols]
            yp_ref[:, cols] = y.astype(BF)
        uc = jnp.concatenate([ch_ref[...] * xh_ref[...] * keep, c_ref[...] * x_ref[...]], axis=0)
        yc = cw_ref[2:3, :] * uc + cw_ref[0:1, :] * pltpu.roll(uc, 2, 0) + cw_ref[1:2, :] * pltpu.roll(uc, 1, 0)
        yc_ref[...] = (b_ref[...] * yc[CONV_HALO:]).astype(BF)

    def main(cb):
        return pl.BlockSpec((R, 512), lambda i: (i, cb))

    def prev(cb, halo, per):
        return pl.BlockSpec((halo, 512), lambda i: (jnp.maximum(i * per - 1, 0), cb))

    full = lambda a: pl.BlockSpec(a.shape, lambda i: (0,) * a.ndim)
    return _pallas_call(
        body, name=name, grid=(T // R,),
        in_specs=[main(0), prev(0, POOL_HALO, PH), main(1), main(2), prev(2, CONV_HALO, CH), main(3),
                  prev(3, CONV_HALO, CH), full(pmix_b), full(pscale), full(convw)],
        out_specs=[pl.BlockSpec((R, 512), lambda i: (i, 0))] * 2,
        out_shape=[jax.ShapeDtypeStruct((T, 512), BF)] * 2,
        compiler_params=_params(("parallel",)),
    )(z, z, z, z, z, z, z, pmix_b, pscale, convw)


def _poolconv_bwd(z, dyp, dyc, pmix_b, pscale, convw, dz, name):
    T = z.shape[0]
    R = min(512, T)
    PH, CH = R // POOL_HALO, R // CONV_HALO
    nsteps = T // R

    def body(u_ref, uh_ref, b_ref, bn_ref, c_ref, ch_ref, x_ref, xh_ref, dyp_ref, dypn_ref, dyc_ref, dycn_ref,
             mix_ref, sc_ref, cw_ref, dz_in_ref, dz_ref, dmix_ref, dsc_ref, dcw_ref):
        i = pl.program_id(0)
        keep_prev = (i > 0).astype(F32)
        keep_next = (i < nsteps - 1).astype(F32)

        @pl.when(i == 0)
        def _():
            dmix_ref[...] = jnp.zeros_like(dmix_ref)
            dsc_ref[...] = jnp.zeros_like(dsc_ref)
            dcw_ref[...] = jnp.zeros_like(dcw_ref)

        row = i * R + lax.broadcasted_iota(jnp.int32, (R, 1), 0)
        row_ext = i * R + lax.broadcasted_iota(jnp.int32, (R + POOL_HALO, 1), 0)
        w_all = jnp.concatenate([uh_ref[...] * keep_prev, u_ref[...]], axis=0)
        dyp_ext = jnp.concatenate([dyp_ref[...], dypn_ref[...] * keep_next], axis=0)
        for g, w in enumerate(POOL_WINDOWS):
            cols = slice(128 * g, 128 * (g + 1))
            wg = w_all[:, cols]
            s = _causal_window_sum(wg, w)[POOL_HALO:]
            cnt = jnp.minimum(row + 1, w).astype(F32)
            dgrp = (s / cnt - wg[POOL_HALO:]).astype(BF)
            y_pre = _dot(dgrp, mix_ref[g])
            dsc_ref[:, cols] += jnp.sum(dyp_ref[:, cols] * y_pre, axis=0, keepdims=True)
            dyb = (dyp_ext[:, cols] * sc_ref[:, cols]).astype(BF)
            dmix_ref[cols, :] += _dot(dgrp, dyb[:R], "tn")
            dd = _dot(dyb, mix_ref[g], "nt")
            cnt_ext = jnp.minimum(row_ext + 1, w).astype(F32)
            e = _anticausal_window_sum(dd / cnt_ext, w)
            dz_ref[:, cols] = (e[:R] - dd[:R]).astype(BF)
        cw0, cw1, cw2 = cw_ref[0:1, :], cw_ref[1:2, :], cw_ref[2:3, :]
        uc = jnp.concatenate([ch_ref[...] * xh_ref[...] * keep_prev, c_ref[...] * x_ref[...]], axis=0)
        uc1 = pltpu.roll(uc, 1, 0)[CONV_HALO:]
        uc2 = pltpu.roll(uc, 2, 0)[CONV_HALO:]
        uc0 = uc[CONV_HALO:]
        yc = cw2 * uc0 + cw0 * uc2 + cw1 * uc1
        dycv = dyc_ref[...]
        dz_ref[:, 512:1024] = (dycv * yc).astype(BF)
        dv_ext = jnp.concatenate([dycv * b_ref[...], dycn_ref[...] * bn_ref[...] * keep_next], axis=0)
        n_ext = R + CONV_HALO
        duc = (cw2 * dv_ext + cw1 * pltpu.roll(dv_ext, n_ext - 1, 0) + cw0 * pltpu.roll(dv_ext, n_ext - 2, 0))[:R]
        dv = dv_ext[:R]
        dcw_ref[0:1, :] += jnp.sum(dv * uc2, axis=0, keepdims=True)
        dcw_ref[1:2, :] += jnp.sum(dv * uc1, axis=0, keepdims=True)
        dcw_ref[2:3, :] += jnp.sum(dv * uc0, axis=0, keepdims=True)
        dz_ref[:, 1024:1536] = (duc * x_ref[...]).astype(BF)
        dz_ref[:, 1536:2048] = (duc * c_ref[...]).astype(BF)

    def main(cb):
        return pl.BlockSpec((R, 512), lambda i: (i, cb))

    def prev(cb, halo, per):
        return pl.BlockSpec((halo, 512), lambda i: (jnp.maximum(i * per - 1, 0), cb))

    def nxt(cb, halo, per):
        return pl.BlockSpec((halo, 512), lambda i: (jnp.minimum((i + 1) * per, T // halo - 1), cb))

    full = lambda a: pl.BlockSpec(a.shape, lambda i: (0,) * a.ndim)
    return _pallas_call(
        body, name=name, grid=(nsteps,),
        in_specs=[main(0), prev(0, POOL_HALO, PH), main(1), nxt(1, CONV_HALO, CH), main(2), prev(2, CONV_HALO, CH),
                  main(3), prev(3, CONV_HALO, CH), main(0), nxt(0, POOL_HALO, PH), main(0), nxt(0, CONV_HALO, CH),
                  full(pmix_b), full(pscale), full(convw), ANY],
        out_specs=[pl.BlockSpec((R, 2048), lambda i: (i, 0)), pl.BlockSpec((512, 128), lambda i: (0, 0)),
                   pl.BlockSpec((1, 512), lambda i: (0, 0)), pl.BlockSpec((8, 512), lambda i: (0, 0))],
        out_shape=[jax.ShapeDtypeStruct(dz.shape, BF), jax.ShapeDtypeStruct((512, 128), F32),
                   jax.ShapeDtypeStruct((1, 512), F32), jax.ShapeDtypeStruct((8, 512), F32)],
        input_output_aliases={15: 0}, compiler_params=_params(("arbitrary",)),
    )(z, z, z, z, z, z, z, z, dyp, dyp, dyc, dyc, pmix_b, pscale, convw, dz)


def _head_sums(v):
    row = lax.broadcasted_iota(jnp.int32, (LANES, LANES), 0) < HEAD_DIM
    col = lax.broadcasted_iota(jnp.int32, (LANES, LANES), 1) < HEAD_DIM
    same_head = jnp.where(jnp.logical_xor(row, col), 0.0, 1.0).astype(BF)
    hi = v.astype(BF)
    lo = (v - hi.astype(F32)).astype(BF)
    return _dot(hi, same_head) + _dot(lo, same_head)


def _head_norm(x, g2, ma):
    r = lax.rsqrt(_head_sums(x * x) / HEAD_DIM + EPS)
    return x * r, r


def _head_norm_bwd(dy, xhat, r, g2, ma):
    dxh = dy * g2
    return r * (dxh - xhat * (_head_sums(dxh * xhat) / HEAD_DIM))


def _head_col(tile, hm):
    return jnp.max(jnp.where(hm, tile, -jnp.inf), axis=-1, keepdims=True)


def _attn_masks(other_block_exists):
    lane = lax.broadcasted_iota(jnp.int32, (2 * ATTN_BLOCK, ATTN_BLOCK), 1)
    qi = lax.broadcasted_iota(jnp.int32, (2 * ATTN_BLOCK, ATTN_BLOCK), 0) & (ATTN_BLOCK - 1)
    never = (1 - other_block_exists.astype(jnp.int32)) * (2 * ATTN_BLOCK)
    return lane[:ATTN_BLOCK] < HEAD_DIM, lane <= qi, lane >= qi + never


def _stack_heads(x, ma):
    return jnp.concatenate([jnp.where(ma, x, 0.0), jnp.where(ma, 0.0, x)], axis=0)


def _unstack_heads(y, ma):
    return jnp.where(ma, y[:ATTN_BLOCK], y[ATTN_BLOCK:])


def _stack_cols(tile, ma):
    return jnp.concatenate([_head_col(tile, ma), _head_col(tile, jnp.logical_not(ma))], axis=0)


QKV_TILES = (OFF_GATE - OFF_Q) // LANES
KIND_TILES = QKV_TILES // 3


def _qk_norm(z, gains, name):
    T = z.shape[0]
    tm = min(512, T)

    def body(x_ref, g_ref, o_ref):
        ma = lax.broadcasted_iota(jnp.int32, (tm, LANES), 1) < HEAD_DIM
        for tile in range(QKV_TILES):
            v = x_ref[:, LANES * tile:LANES * (tile + 1)]
            if tile < 2 * KIND_TILES:
                g = g_ref[0:1, :] if tile < KIND_TILES else g_ref[1:2, :]
                v = _head_norm(v, g, ma)[0] * g
            o_ref[tile] = v

    return _pallas_call(
        body, name=name, grid=(T // tm,),
        in_specs=[pl.BlockSpec((pl.Element(tm), pl.Element(OFF_GATE - OFF_Q)), lambda i: (i * tm, OFF_Q)),
                  pl.BlockSpec((8, LANES), lambda i: (0, 0))],
        out_specs=pl.BlockSpec((QKV_TILES, tm, LANES), lambda i: (0, i, 0)),
        out_shape=jax.ShapeDtypeStruct((QKV_TILES, T, LANES), F32), compiler_params=_params(("parallel",)),
    )(z, gains)


ATTN_STEP_ROWS = 1024
ATTN_UNROLL = 2


def _attn_geometry(T, d):
    sub = ATTN_BLOCK * d
    nb = T // sub
    m = max(1, min(nb, ATTN_STEP_ROWS // sub))
    assert T % sub == 0 and nb % m == 0
    return sub, nb, m


def _attn_rows(jj, r, sub, d):
    start = jj * sub + r
    if d == 1:
        return pl.ds(pl.multiple_of(start, ATTN_BLOCK), ATTN_BLOCK)
    return pl.ds(start, ATTN_BLOCK, stride=d)


def _pick(flag, a, b):
    return jnp.where(jnp.full(a.shape, flag.astype(jnp.int32)) > 0, a, b)


def _attn_fwd(qkv, g, d, name):
    T = qkv.shape[1]
    sub, nb, m = _attn_geometry(T, d)
    scale = HEAD_DIM ** -0.5

    def body(q_ref, kc_ref, kp_ref, vc_ref, vp_ref, o_ref, lse_ref):
        jb = pl.program_id(0)

        def step(s, carry):
            jj, r = s // d, s % d
            here, before = _attn_rows(jj, r, sub, d), _attn_rows(jnp.maximum(jj - 1, 0), r, sub, d)
            edge = _attn_rows(0, r, sub, d)
            first = jj == 0
            ma, mask_c, mask_p = _attn_masks(jb * m + jj > 0)
            qs = _stack_heads(q_ref[here, :], ma).astype(BF)
            kcb = kc_ref[here, :].astype(BF)
            kpb = _pick(first, kp_ref[edge, :], kc_ref[before, :]).astype(BF)
            vcb = vc_ref[here, :].astype(BF)
            vpb = _pick(first, vp_ref[edge, :], vc_ref[before, :]).astype(BF)
            s_c = jnp.where(mask_c, _dot(qs, kcb, "nt") * scale, MASK_VALUE)
            s_p = jnp.where(mask_p, _dot(qs, kpb, "nt") * scale, MASK_VALUE)
            mx = jnp.maximum(jnp.max(s_c, axis=-1, keepdims=True), jnp.max(s_p, axis=-1, keepdims=True))
            p_c = jnp.exp(s_c - mx)
            p_p = jnp.exp(s_p - mx)
            den = jnp.sum(p_c, axis=-1, keepdims=True) + jnp.sum(p_p, axis=-1, keepdims=True)
            o = (_dot(p_c.astype(BF), vcb) + _dot(p_p.astype(BF), vpb)) / den
            o_ref[here, :] = _unstack_heads(o, ma)
            lse_ref[here, :] = _unstack_heads(jnp.broadcast_to(mx + jnp.log(den), o.shape), ma)
            return carry

        lax.fori_loop(0, m * d, step, 0, unroll=ATTN_UNROLL)

    def cur(kind):
        return pl.BlockSpec((None, m * sub, LANES), lambda j, t: (KIND_TILES * kind + 2 * g + t, j, 0))

    def prv(kind):
        return pl.BlockSpec((None, sub, LANES), lambda j, t: (KIND_TILES * kind + 2 * g + t, jnp.maximum(j * m - 1, 0), 0))

    out = pl.BlockSpec((m * sub, LANES), lambda j, t: (j, t))
    return _pallas_call(
        body, name=name, grid=(nb // m, 2), in_specs=[cur(0), cur(1), prv(1), cur(2), prv(2)],
        out_specs=[out, out], out_shape=[jax.ShapeDtypeStruct((T, 256), F32)] * 2,
        compiler_params=_params(("parallel", "parallel")),
    )(qkv, qkv, qkv, qkv, qkv)


def _attn_bwd(z, qkv, do, c, lse, gains, g, d, name, after=None):
    T = z.shape[0]
    sub, nb, m = _attn_geometry(T, d)
    scale = HEAD_DIM ** -0.5
    extra = [] if after is None else [after]

    def body(qr_ref, kr_ref, vc_ref, vp_ref, qn_ref, qnn_ref, kn_ref, knp_ref, do_ref, don_ref, c_ref, cn_ref,
             lse_ref, lsen_ref, g_ref, *rest):
        dq_ref, dk_ref, dv_ref, dgq_ref, dgk_ref, sq_ref, sk_ref, sv_ref = rest[len(extra):]
        jb = pl.program_id(0)

        @pl.when((jb == 0) & (pl.program_id(1) == 0))
        def _():
            dgq_ref[...] = jnp.zeros_like(dgq_ref)
            dgk_ref[...] = jnp.zeros_like(dgk_ref)

        gq, gk = g_ref[0:1, :], g_ref[1:2, :]

        def step(s, carry):
            jj, r = s // d, s % d
            here, edge = _attn_rows(jj, r, sub, d), _attn_rows(0, r, sub, d)
            before = _attn_rows(jnp.maximum(jj - 1, 0), r, sub, d)
            behind = _attn_rows(jnp.minimum(jj + 1, m - 1), r, sub, d)
            first, last = jj == 0, jj == m - 1
            block = jb * m + jj
            ma, mask_c, mask_p = _attn_masks(block > 0)
            mask_n = _attn_masks(block < nb - 1)[2]
            qhat, rq = _head_norm(qr_ref[here, :], gq, ma)
            qn = qhat * gq
            qn_next = _pick(last, qnn_ref[edge, :], qn_ref[behind, :])
            khat, rk = _head_norm(kr_ref[here, :], gk, ma)
            kcb = (khat * gk).astype(BF)
            kpb = _pick(first, knp_ref[edge, :], kn_ref[before, :]).astype(BF)
            vcb = vc_ref[here, :].astype(BF)
            vpb = _pick(first, vp_ref[edge, :], vc_ref[before, :]).astype(BF)
            do_t, don_t = do_ref[here, :], _pick(last, don_ref[edge, :], do_ref[behind, :])
            c_t, cn_t = c_ref[here, :], _pick(last, cn_ref[edge, :], c_ref[behind, :])
            lse_t, lsen_t = lse_ref[here, :], _pick(last, lsen_ref[edge, :], lse_ref[behind, :])
            qs, dos = _stack_heads(qn, ma).astype(BF), _stack_heads(do_t, ma).astype(BF)
            lse_s, c_s = _stack_cols(lse_t, ma), _stack_cols(c_t, ma)
            s_c = jnp.where(mask_c, _dot(qs, kcb, "nt") * scale, MASK_VALUE)
            s_p = jnp.where(mask_p, _dot(qs, kpb, "nt") * scale, MASK_VALUE)
            p_c = jnp.exp(s_c - lse_s)
            p_p = jnp.exp(s_p - lse_s)
            ds_c = ((p_c * (_dot(dos, vcb, "nt") + c_s)) * scale).astype(BF)
            ds_p = ((p_p * (_dot(dos, vpb, "nt") + c_s)) * scale).astype(BF)
            dq_t = _unstack_heads(_dot(ds_c, kcb) + _dot(ds_p, kpb), ma)
            qs_n, dos_n = _stack_heads(qn_next, ma).astype(BF), _stack_heads(don_t, ma).astype(BF)
            s_n = jnp.where(mask_n, _dot(qs_n, kcb, "nt") * scale, MASK_VALUE)
            p_n = jnp.exp(s_n - _stack_cols(lsen_t, ma))
            ds_n = ((p_n * (_dot(dos_n, vcb, "nt") + _stack_cols(cn_t, ma))) * scale).astype(BF)
            dv_t = _dot(p_c.astype(BF), dos, "tn") + _dot(p_n.astype(BF), dos_n, "tn")
            dk_t = _dot(ds_c, qs, "tn") + _dot(ds_n, qs_n, "tn")
            sq_ref[here, :] = _head_norm_bwd(dq_t, qhat, rq, gq, ma)
            sk_ref[here, :] = _head_norm_bwd(dk_t, khat, rk, gk, ma)
            sv_ref[here, :] = dv_t
            dgq_ref[...] += jnp.sum(dq_t * qhat, axis=0, keepdims=True)
            dgk_ref[...] += jnp.sum(dk_t * khat, axis=0, keepdims=True)
            return carry

        lax.fori_loop(0, m * d, step, 0, unroll=ATTN_UNROLL)
        dq_ref[...] = sq_ref[...].astype(BF)
        dk_ref[...] = sk_ref[...].astype(BF)
        dv_ref[...] = sv_ref[...].astype(BF)

    def raw(col0):
        return pl.BlockSpec((m * sub, LANES), lambda j, t: (j, col0 + 2 * g + t))

    def cur(kind):
        return pl.BlockSpec((None, m * sub, LANES), lambda j, t: (KIND_TILES * kind + 2 * g + t, j, 0))

    def prv(kind):
        return pl.BlockSpec((None, sub, LANES), lambda j, t: (KIND_TILES * kind + 2 * g + t, jnp.maximum(j * m - 1, 0), 0))

    def nxt(kind):
        return pl.BlockSpec((None, sub, LANES),
                            lambda j, t: (KIND_TILES * kind + 2 * g + t, jnp.minimum((j + 1) * m, nb - 1), 0))

    own = pl.BlockSpec((m * sub, LANES), lambda j, t: (j, t))
    own_next = pl.BlockSpec((sub, LANES), lambda j, t: (jnp.minimum((j + 1) * m, nb - 1), t))
    vec = pl.BlockSpec((1, LANES), lambda j, t: (0, 0))
    return _pallas_call(
        body, name=name, grid=(nb // m, 2),
        in_specs=[raw(OFF_Q // LANES), raw(OFF_K // LANES), cur(2), prv(2), cur(0), nxt(0), cur(1), prv(1), own, own_next,
                  own, own_next,
                  own, own_next, pl.BlockSpec((8, LANES), lambda j, t: (0, 0))] + [ANY] * len(extra),
        out_specs=[own, own, own, vec, vec],
        out_shape=[jax.ShapeDtypeStruct((T, 256), BF)] * 3 + [jax.ShapeDtypeStruct((1, LANES), F32)] * 2,
        scratch_shapes=[pltpu.VMEM((m * sub, LANES), F32)] * 3,
        compiler_params=_params(("arbitrary", "arbitrary")),
    )(z, z, qkv, qkv, qkv, qkv, qkv, qkv, do, do, c, c, lse, lse, gains, *extra)


MERGE_ROWS = 256
GATE_TILE = 256


def _group_mix(o_refs, lse_refs):
    lses = [r[...] for r in lse_refs]
    m = jnp.maximum(jnp.maximum(lses[0], lses[1]), lses[2])
    es = [jnp.exp(l - m) for l in lses]
    den = es[0] + es[1] + es[2]
    ws = [e / den for e in es]
    y = ws[0] * o_refs[0][...] + ws[1] * o_refs[1][...] + ws[2] * o_refs[2][...]
    return ws, y


def _sigmoid(v):
    return 1.0 / (1.0 + jnp.exp(-v))


def _merge_specs(T, z, bgate, gpu, gco, gau):
    tm = min(MERGE_ROWS, T)
    row = lambda w: pl.BlockSpec((tm, w), lambda i: (i, 0))
    gate0 = OFF_GATE // GATE_TILE
    gates = [pl.BlockSpec((tm, GATE_TILE), functools.partial(lambda i, cb: (i, cb), cb=gate0 + n))
             for n in range(3 * N_CHIPS)]
    full = lambda a: pl.BlockSpec(a.shape, lambda i: (0,) * a.ndim)
    specs = [row(512), row(512)] + [row(256)] * 6 + gates + [full(bgate), full(gpu), full(gco), full(gau)]
    return tm, row, specs


def _merge_fwd(yp, yc, o3, lse3, z, bgate, gpu, gco, gau, name):
    T = yp.shape[0]
    tm, row, specs = _merge_specs(T, z, bgate, gpu, gco, gau)

    def body(*refs):
        yp_ref, yc_ref = refs[0], refs[1]
        o_refs, lse_refs = refs[2:5], refs[5:8]
        zg = refs[8:20]
        b_ref, gpu_ref, gco_ref, gau_ref, out_ref = refs[20:25]
        yab = _group_mix(o_refs, lse_refs)[1].astype(BF)
        ys = (yp_ref[...], yc_ref[...], yab)
        ups = (gpu_ref, gco_ref, gau_ref)
        for n in range(N_CHIPS):
            acc = None
            for b in range(3):
                gcol = slice(1024 * b + GATE_TILE * n, 1024 * b + GATE_TILE * (n + 1))
                gate = _sigmoid(zg[N_CHIPS * b + n][...] + b_ref[:, gcol])
                term = gate * _dot(ys[b], ups[b][n])
                acc = term if acc is None else acc + term
            out_ref[:, GATE_TILE * n:GATE_TILE * (n + 1)] = acc.astype(BF)

    return _pallas_call(
        body, name=name, grid=(T // tm,), in_specs=specs, out_specs=row(1024),
        out_shape=jax.ShapeDtypeStruct((T, 1024), BF), compiler_params=_params(("parallel",)),
    )(yp, yc, *o3, *lse3, *([z] * 12), bgate, gpu, gco, gau)


def _merge_bwd(dm, yp, yc, o3, lse3, z, bgate, gpu, gco, gau, name):
    T = yp.shape[0]
    tm, row, specs = _merge_specs(T, z, bgate, gpu, gco, gau)
    nsteps = T // tm

    def body(*refs):
        dm_ref, yp_ref, yc_ref = refs[0:3]
        o_refs, lse_refs = refs[3:6], refs[6:9]
        zg = refs[9:21]
        b_ref, gpu_ref, gco_ref, gau_ref = refs[21:25]
        dzg_ref, dyp_ref, dyc_ref = refs[25:28]
        do_refs, c_refs = refs[28:31], refs[31:34]
        dgpu_ref, dgco_ref, dgau_ref, dbg_ref = refs[34:38]
        accs = refs[38:41]
        i = pl.program_id(0)

        @pl.when(i == 0)
        def _():
            for a in accs:
                a[...] = jnp.zeros_like(a)
            dbg_ref[...] = jnp.zeros_like(dbg_ref)

        ws, y = _group_mix(o_refs, lse_refs)
        ys = (yp_ref[...], yc_ref[...], y.astype(BF))
        ups = (gpu_ref, gco_ref, gau_ref)
        dys = [None, None, None]
        for n in range(N_CHIPS):
            dmn = dm_ref[:, GATE_TILE * n:GATE_TILE * (n + 1)]
            for b in range(3):
                gcol = slice(1024 * b + GATE_TILE * n, 1024 * b + GATE_TILE * (n + 1))
                gate = _sigmoid(zg[N_CHIPS * b + n][...] + b_ref[:, gcol])
                up = _dot(ys[b], ups[b][n])
                dzg = (dmn * up) * (gate * (1.0 - gate))
                dzg_ref[:, gcol] = dzg.astype(BF)
                dbg_ref[:, gcol] += jnp.sum(dzg, axis=0, keepdims=True)
                dup = (dmn * gate).astype(BF)
                accs[b][n] += _dot(ys[b], dup, "tn")
                dyb = _dot(dup, ups[b][n], "nt")
                dys[b] = dyb if dys[b] is None else dys[b] + dyb
        dyp_ref[...] = dys[0]
        dyc_ref[...] = dys[1]
        dya = dys[2]
        lane = lax.broadcasted_iota(jnp.int32, dya.shape, 1) // HEAD_DIM
        pr = dya * y
        rho = jnp.zeros_like(pr)
        for h in range(256 // HEAD_DIM):
            hm = lane == h
            rho = jnp.where(hm, jnp.sum(jnp.where(hm, pr, 0.0), axis=-1, keepdims=True), rho)
        for g in range(3):
            do_refs[g][...] = ws[g] * dya
            c_refs[g][...] = -(ws[g] * rho)

        @pl.when(i == nsteps - 1)
        def _():
            dgpu_ref[...] = accs[0][...].astype(BF)
            dgco_ref[...] = accs[1][...].astype(BF)
            dgau_ref[...] = accs[2][...].astype(BF)

    full = lambda a: pl.BlockSpec(a.shape, lambda i: (0,) * a.ndim)
    dz_gate = pl.BlockSpec((pl.Element(tm), pl.Element(3072)), lambda i: (i * tm, OFF_GATE))
    out_specs = ([dz_gate, row(512), row(512)] + [row(256)] * 6 + [full(gpu), full(gco), full(gau)]
                 + [pl.BlockSpec((1, 3072), lambda i: (0, 0))])
    out_shape = ([jax.ShapeDtypeStruct(z.shape, BF)] + [jax.ShapeDtypeStruct((T, 512), F32)] * 2
                 + [jax.ShapeDtypeStruct((T, 256), F32)] * 6
                 + [jax.ShapeDtypeStruct(g.shape, BF) for g in (gpu, gco, gau)]
                 + [jax.ShapeDtypeStruct((1, 3072), F32)])
    return _pallas_call(
        body, name=name, grid=(nsteps,), in_specs=[row(1024)] + specs, out_specs=out_specs, out_shape=out_shape,
        scratch_shapes=[pltpu.VMEM(g.shape, F32) for g in (gpu, gco, gau)],
        compiler_params=_params(("arbitrary",)),
    )(dm, yp, yc, *o3, *lse3, *([z] * 12), bgate, gpu, gco, gau)


def _layer_fwd(x, w, tag, after=None, soon=None, late=None):
    hb = _rms_fwd(x, w["norm_mix"], f"rms_mix_{tag}", after=after)
    if soon is not None:
        w = dict(w, **soon(hb))
    z = _mm(hb, w["w_in"], "nt", f"in_proj_{tag}", tm=512, tn=3712, tk=1024, n_outer=True)
    yp, yc = _poolconv_fwd(z, w["pool_mix"], w["pool_scale"], w["conv_w"], f"poolconv_{tag}")
    qkv = _qk_norm(z, w["qk_gain"], f"qk_norm_{tag}")
    o3, lse3 = [], []
    for g, d in enumerate(ATTN_DILATIONS):
        o, lse = _attn_fwd(qkv, g, d, f"attn{g}_{tag}")
        o3.append(o)
        lse3.append(lse)
    if late is not None:
        w = dict(w, **late(lse3[-1]))
    merged = _merge_fwd(yp, yc, o3, lse3, z, w["b_gate"], w["w_pool_up"], w["w_conv_out"], w["w_attn_up"],
                        f"merge_{tag}")
    x1 = _mm(merged, w["w_o"], "nn", f"out_proj_{tag}", tm=1024, tn=1024, tk=1024, res=x)
    h2b = _rms_fwd(x1, w["norm_mlp"], f"rms_mlp_{tag}")
    rb = _mm(h2b, w["w_ff1"], "nn", f"ff1_{tag}", tm=1024, tn=1024, tk=1024, out_dtype=BF, epi="relu2", n_outer=True,
             b_shards=True)
    x2 = _mm(rb, w["w_ff2"], "nn", f"ff2_{tag}", tm=512, tn=1024, tk=4096, res=x1)
    saved = dict(x=x, hb=hb, z=z, yp=yp, yc=yc, qkv=qkv, o3=o3, lse3=lse3, merged=merged, x1=x1, h2b=h2b, rb=rb)
    return x2, saved, w


def _layer_bwd(dx2, w, s, tag, after=None, mid=None, tail=None):
    g = {}
    dab = _mm(dx2, w["w_ff2"], "nt", f"d_ff2_act_{tag}", tm=1024, tn=1024, tk=1024, out_dtype=BF, aux=s["rb"],
              epi="drelu2", after=after)
    g["w_ff2"] = _mm(s["rb"], dx2, "tn", f"d_ff2_w_{tag}", tm=1024, tn=1024, tk=2048, out_dtype=BF)
    g["w_ff1"] = _mm(s["h2b"], dab, "tn", f"d_ff1_w_{tag}", tm=1024, tn=1024, tk=2048, out_dtype=BF, out_shards=True)
    dh2 = _mm(dab, w["w_ff1"], "nt", f"d_ff1_act_{tag}", tm=1024, tn=1024, tk=1024, b_shards=True)
    dx1, g["norm_mlp"] = _rms_bwd(dh2, s["x1"], w["norm_mlp"], dx2, f"d_rms_mlp_{tag}")
    dm = _mm(dx1, w["w_o"], "nt", f"d_out_act_{tag}", tm=1024, tn=1024, tk=1024)
    g["w_o"] = _mm(s["merged"], dx1, "tn", f"d_out_w_{tag}", tm=1024, tn=1024, tk=1024, out_dtype=BF)
    (dz, dyp, dyc, do0, do1, do2, c0, c1, c2, g["w_pool_up"], g["w_conv_out"], g["w_attn_up"],
     g["b_gate"]) = _merge_bwd(dm, s["yp"], s["yc"], s["o3"], s["lse3"], s["z"], w["b_gate"], w["w_pool_up"],
                               w["w_conv_out"], w["w_attn_up"], f"d_merge_{tag}")
    behind = mid(g) if mid is not None else None
    dq, dk, dv = [], [], []
    dgq = dgk = None
    for gi, d in enumerate(ATTN_DILATIONS):
        dzq, dzk, dzv, pq, pk = _attn_bwd(s["z"], s["qkv"], (do0, do1, do2)[gi], (c0, c1, c2)[gi], s["lse3"][gi],
                                          w["qk_gain"], gi, d, f"d_attn{gi}_{tag}", after=behind)
        dq.append(dzq)
        dk.append(dzk)
        dv.append(dzv)
        dgq = pq if dgq is None else dgq + pq
        dgk = pk if dgk is None else dgk + pk
    g["q_gain"] = dgq[:, :HEAD_DIM] + dgq[:, HEAD_DIM:]
    g["k_gain"] = dgk[:, :HEAD_DIM] + dgk[:, HEAD_DIM:]
    for off, pieces in ((OFF_Q, dq), (OFF_K, dk), (OFF_V, dv)):
        for gi, piece in enumerate(pieces):
            dz = lax.dynamic_update_slice(dz, piece, (0, off + 256 * gi))
    dz, g["pool_mix"], g["pool_scale"], g["conv_w"] = _poolconv_bwd(
        s["z"], dyp, dyc, w["pool_mix"], w["pool_scale"], w["conv_w"], dz, f"d_poolconv_{tag}")
    g["w_in"] = _mm(s["hb"], dz, "tn", f"d_in_w_{tag}", tm=512, tn=3712, tk=1024, out_dtype=BF)
    dh = _mm(dz, w["w_in"], "nn", f"d_in_act_{tag}", tm=1024, tn=1024, tk=3712,
             after=tail(g) if tail is not None else None)
    dx, g["norm_mix"] = _rms_bwd(dh, s["x"], w["norm_mix"], dx1, f"d_rms_mix_{tag}")
    return dx, g


def _position():
    x, y, c = lax.axis_index("x"), lax.axis_index("y"), lax.axis_index("c")
    chips = [(1 - x, y), (x, 1 - y), (1 - x, 1 - y)]
    return x, y, c, 2 * x + y, chips, [2 * cx + cy for cx, cy in chips]


def _remote(src, dst, ssem, rsem, dev):
    return pltpu.make_async_remote_copy(src_ref=src, dst_ref=dst, send_sem=ssem, recv_sem=rsem, device_id=dev,
                                        device_id_type=MESH_ID)


def _halves(a):
    return a.reshape(a.shape[0], 2, a.shape[1] // 2, a.shape[2])


SEM = pl.BlockSpec(memory_space=pltpu.SEMAPHORE)
TOKEN = jax.ShapeDtypeStruct((8, LANES), F32)
TOKEN_SPEC = pl.BlockSpec(memory_space=pltpu.VMEM)


def _split_params():
    return pltpu.CompilerParams(has_side_effects=pltpu.SideEffectType.DATAFLOW_SIDE_EFFECTING)


def _gather_start(bufs, name, after):
    n = len(bufs)
    views = [_halves(b) for b in bufs]

    def body(*refs):
        first_sem = n + 1
        ssem, rsem = refs[first_sem:first_sem + ns], refs[first_sem + ns:first_sem + 2 * ns]
        outs, token = refs[first_sem + 2 * ns:first_sem + 2 * ns + n], refs[first_sem + 2 * ns + n]
        x, y, c, q, chips, qs = _position()
        for k in range(n):
            mine = outs[k].at[q, c]
            for j, chip in enumerate(chips):
                _remote(mine, mine, ssem[3 * k + j], rsem[3 * k + j], (chip[0], chip[1], c)).start()
        token[...] = jnp.zeros_like(token)

    ns = 3 * n
    outs = _pallas_call(
        body, name=name, in_specs=[ANY] * (n + 1), out_specs=[SEM] * (2 * ns) + [ANY] * n + [TOKEN_SPEC],
        out_shape=[pltpu.SemaphoreType.DMA(())] * (2 * ns) + [jax.ShapeDtypeStruct(v.shape, v.dtype) for v in views]
        + [TOKEN],
        input_output_aliases={k: k + 2 * ns for k in range(n)}, compiler_params=_split_params(),
    )(*views, after)
    return list(outs[:ns]), list(outs[ns:2 * ns]), list(outs[2 * ns:2 * ns + n]), outs[2 * ns + n]


def _gather_finish(ssem, rsem, views, after, name_wait, name_forward, shapes):
    n = len(views)
    ns = len(ssem)

    def wait_body(*refs):
        ssem_ref, rsem_ref = refs[n:n + ns], refs[n + ns:n + 2 * ns]
        outs = refs[n + 2 * ns + 1:]
        x, y, c, q, chips, qs = _position()
        for k in range(n):
            for j, chip in enumerate(chips):
                cp = _remote(outs[k].at[q, c], outs[k].at[qs[j], c], ssem_ref[3 * k + j], rsem_ref[3 * k + j],
                             (chip[0], chip[1], c))
                cp.wait_send()
                cp.wait_recv()

    landed = _pallas_call(
        wait_body, name=name_wait, in_specs=[ANY] * n + [SEM] * (2 * ns) + [ANY], out_specs=[ANY] * n,
        out_shape=[jax.ShapeDtypeStruct(v.shape, v.dtype) for v in views],
        input_output_aliases={k: k for k in range(n)}, compiler_params=_split_params(),
    )(*views, *ssem, *rsem, after)

    def forward_body(*refs):
        outs = refs[n:2 * n]
        fssem, frsem = refs[2 * n:]
        x, y, c, q, chips, qs = _position()
        sib = (x, y, 1 - c)
        sent = []
        for k in range(n):
            for j in range(3):
                slot = outs[k].at[qs[j], c]
                cp = _remote(slot, slot, fssem.at[k, j], frsem.at[k, j], sib)
                cp.start()
                sent.append(cp)
        for k in range(n):
            for j in range(3):
                slot = outs[k].at[qs[j], 1 - c]
                _remote(slot, slot, fssem.at[k, j], frsem.at[k, j], sib).wait_recv()
        for cp in sent:
            cp.wait_send()

    outs = _pallas_call(
        forward_body, name=name_forward, in_specs=[ANY] * n, out_specs=[ANY] * n,
        out_shape=[jax.ShapeDtypeStruct(v.shape, v.dtype) for v in views],
        input_output_aliases={k: k for k in range(n)}, scratch_shapes=[pltpu.SemaphoreType.DMA((n, 3))] * 2,
    )(*landed)
    return [o.reshape(s) for o, s in zip(outs, shapes)]


def _chip_exchange_start(parts, name):
    n = len(parts)

    def body(*refs):
        ssem, rsem = refs[n:n + ns], refs[n + ns:n + 2 * ns]
        base = n + 2 * ns
        srcs, outs, token = refs[base:base + n], refs[base + n:base + 2 * n], refs[base + 2 * n]
        x, y, c, q, chips, qs = _position()
        for k in range(n):
            for j, chip in enumerate(chips):
                _remote(srcs[k].at[qs[j]], outs[k].at[j], ssem[3 * k + j], rsem[3 * k + j],
                        (chip[0], chip[1], c)).start()
        token[...] = jnp.zeros_like(token)

    ns = 3 * n
    outs = _pallas_call(
        body, name=name, in_specs=[ANY] * n, out_specs=[SEM] * (2 * ns) + [ANY] * (2 * n) + [TOKEN_SPEC],
        out_shape=[pltpu.SemaphoreType.DMA(())] * (2 * ns) + [jax.ShapeDtypeStruct(a.shape, a.dtype) for a in parts]
        + [jax.ShapeDtypeStruct((3,) + a.shape[1:], a.dtype) for a in parts] + [TOKEN],
        input_output_aliases={k: k + 2 * ns for k in range(n)}, compiler_params=_split_params(),
    )(*parts)
    b = 2 * ns
    return list(outs[:ns]), list(outs[ns:b]), list(outs[b:b + n]), list(outs[b + n:b + 2 * n]), outs[b + 2 * n]


def _chip_exchange_wait(ssem, rsem, parts, landing, after, name):
    n = len(parts)
    ns = len(ssem)

    def body(*refs):
        ssem_ref, rsem_ref = refs[2 * n:2 * n + ns], refs[2 * n + ns:2 * n + 2 * ns]
        base = 2 * n + 2 * ns + 1
        srcs, outs = refs[base:base + n], refs[base + n:]
        x, y, c, q, chips, qs = _position()
        for k in range(n):
            for j, chip in enumerate(chips):
                cp = _remote(srcs[k].at[qs[j]], outs[k].at[j], ssem_ref[3 * k + j], rsem_ref[3 * k + j],
                             (chip[0], chip[1], c))
                cp.wait_send()
                cp.wait_recv()

    outs = _pallas_call(
        body, name=name, in_specs=[ANY] * (2 * n) + [SEM] * (2 * ns) + [ANY], out_specs=[ANY] * (2 * n),
        out_shape=[jax.ShapeDtypeStruct(a.shape, a.dtype) for a in list(parts) + list(landing)],
        input_output_aliases={k: k for k in range(2 * n)}, compiler_params=_split_params(),
    )(*parts, *landing, *ssem, *rsem, after)
    return list(outs[:n]), list(outs[n:])


def _pair_swap(views, name):
    n = len(views)

    def body(*refs):
        ins, outs = refs[:n], refs[n:2 * n]
        ssem, rsem = refs[2 * n:]
        x, y, c, _, _, _ = _position()
        cps = [_remote(ins[k].at[pl.ds(0, N_CHIPS), 1 - c], outs[k], ssem.at[k], rsem.at[k], (x, y, 1 - c))
               for k in range(n)]
        for cp in cps:
            cp.start()
        for cp in cps:
            cp.wait()

    return _pallas_call(
        body, name=name, in_specs=[ANY] * n, out_specs=[ANY] * n,
        out_shape=[jax.ShapeDtypeStruct((v.shape[0],) + v.shape[2:], v.dtype) for v in views],
        scratch_shapes=[pltpu.SemaphoreType.DMA((n,))] * 2,
    )(*views)


def _chip_exchange(parts, name):
    n = len(parts)

    def body(*refs):
        ins, outs = refs[:n], refs[n:2 * n]
        ssem, rsem = refs[2 * n:]
        x, y, c, q, chips, qs = _position()
        cps = []
        for k in range(n):
            for j, chip in enumerate(chips):
                cp = _remote(ins[k].at[qs[j]], outs[k].at[j], ssem.at[k, j], rsem.at[k, j], (chip[0], chip[1], c))
                cp.start()
                cps.append(cp)
        for cp in cps:
            cp.wait_recv()
        for cp in cps:
            cp.wait_send()

    return _pallas_call(
        body, name=name, in_specs=[ANY] * n, out_specs=[ANY] * n,
        out_shape=[jax.ShapeDtypeStruct((3,) + a.shape[1:], a.dtype) for a in parts],
        scratch_shapes=[pltpu.SemaphoreType.DMA((n, 3))] * 2,
    )(*parts)


def _pair_send(arrays, name):
    n = len(arrays)

    def body(*refs):
        ins, outs = refs[:n], refs[n:2 * n]
        ssem, rsem = refs[2 * n:]
        x, y, c, _, _, _ = _position()
        cps = [_remote(ins[k], outs[k], ssem.at[k], rsem.at[k], (x, y, 1 - c)) for k in range(n)]
        for cp in cps:
            cp.start()
        for cp in cps:
            cp.wait()

    return _pallas_call(
        body, name=name, in_specs=[ANY] * n, out_specs=[ANY] * n,
        out_shape=[jax.ShapeDtypeStruct(a.shape, a.dtype) for a in arrays],
        scratch_shapes=[pltpu.SemaphoreType.DMA((n,))] * 2,
    )(*arrays)


def _all_to_all_small(part):
    P = part.shape[0]

    def body(in_ref, out_ref, lsem, ssem, rsem):
        x, y, c = lax.axis_index("x"), lax.axis_index("y"), lax.axis_index("c")
        me = 4 * x + 2 * y + c
        flips = [(fx, fy, fc) for fx in (0, 1) for fy in (0, 1) for fc in (0, 1)][1:]
        peers = [((x + fx) % 2, (y + fy) % 2, (c + fc) % 2) for fx, fy, fc in flips]
        loc = pltpu.make_async_copy(in_ref, out_ref.at[me], lsem)
        loc.start()
        cps = [_remote(in_ref, out_ref.at[me], ssem.at[j], rsem.at[j], peer) for j, peer in enumerate(peers)]
        for cp in cps:
            cp.start()
        for j, (px, py, pc) in enumerate(peers):
            _remote(in_ref, out_ref.at[4 * px + 2 * py + pc], ssem.at[j], rsem.at[j], peers[j]).wait_recv()
        for cp in cps:
            cp.wait_send()
        loc.wait()

    return _pallas_call(
        body, name="small_exchange", in_specs=[ANY], out_specs=ANY,
        out_shape=jax.ShapeDtypeStruct((8, P, LANES), F32),
        scratch_shapes=[pltpu.SemaphoreType.DMA(())] + [pltpu.SemaphoreType.DMA((7,))] * 2,
    )(part)


def _small_peers():
    x, y, c = lax.axis_index("x"), lax.axis_index("y"), lax.axis_index("c")
    flips = [(fx, fy, fc) for fx in (0, 1) for fy in (0, 1) for fc in (0, 1)][1:]
    peers = [((x + fx) % 2, (y + fy) % 2, (c + fc) % 2) for fx, fy, fc in flips]
    return 4 * x + 2 * y + c, peers


def _all_to_all_small_start(part, name):
    P = part.shape[0]
    me = 4 * lax.axis_index("x") + 2 * lax.axis_index("y") + lax.axis_index("c")
    landing = lax.dynamic_update_slice(jnp.zeros((8, P, LANES), F32), part[None], (me, 0, 0))

    def body(*refs):
        sems, src, land, token = refs[2:16], refs[16], refs[17], refs[18]
        me_, peers = _small_peers()
        for j, peer in enumerate(peers):
            _remote(src, land.at[me_], sems[j], sems[7 + j], peer).start()
        token[...] = jnp.zeros_like(token)

    outs = _pallas_call(
        body, name=name, in_specs=[ANY, ANY], out_specs=[SEM] * 14 + [ANY, ANY, TOKEN_SPEC],
        out_shape=[pltpu.SemaphoreType.DMA(())] * 14 + [jax.ShapeDtypeStruct(part.shape, F32),
                                                       jax.ShapeDtypeStruct((8, P, LANES), F32), TOKEN],
        input_output_aliases={0: 14, 1: 15}, compiler_params=_split_params(),
    )(part, landing)
    return list(outs[:7]), list(outs[7:14]), outs[14], outs[15], outs[16]


def _all_to_all_small_wait(ssem, rsem, part, landing, after, name):
    def body(*refs):
        sems, src, land = refs[2:16], refs[17], refs[18]
        _, peers = _small_peers()
        for j, (px, py, pc) in enumerate(peers):
            cp = _remote(src, land.at[4 * px + 2 * py + pc], sems[j], sems[7 + j], peers[j])
            cp.wait_send()
            cp.wait_recv()

    return _pallas_call(
        body, name=name, in_specs=[ANY, ANY] + [SEM] * 14 + [ANY], out_specs=[ANY, ANY],
        out_shape=[jax.ShapeDtypeStruct(part.shape, F32), jax.ShapeDtypeStruct(landing.shape, F32)],
        input_output_aliases={0: 0, 1: 1}, compiler_params=_split_params(),
    )(part, landing, *ssem, *rsem, after)[1]


def _row_tile(rows, width, n_arrays):
    t = rows
    while t % 2 == 0 and t > 8 and 2 * n_arrays * t * width * 4 > VMEM_LIMIT // 2:
        t //= 2
    return t


def _chip():
    return 2 * lax.axis_index("x") + lax.axis_index("y")


def _core():
    return lax.axis_index("c")


def _cast_place(w3, layer, name):
    _, r, c = w3.shape
    tr = _row_tile(r, c, 2)

    def body(w_ref, o_ref):
        o_ref[...] = w_ref[...].astype(BF)

    return _pallas_call(
        body, name=name, grid=(r // tr,), in_specs=[pl.BlockSpec((None, tr, c), lambda i: (layer, i, 0))],
        out_specs=pl.BlockSpec((None, tr, c), lambda i: (_chip(), i, 0)),
        out_shape=jax.ShapeDtypeStruct((N_CHIPS, r, c), BF), compiler_params=_params(("parallel",)),
    )(w3)


def _pair_sum(view, recv, name):
    _, _, hr, c = view.shape
    tr = _row_tile(hr, c, 3)

    def body(g_ref, r_ref, o_ref):
        o_ref[...] = (g_ref[...].astype(F32) + r_ref[...].astype(F32)).astype(BF)

    blk = pl.BlockSpec((None, tr, c), lambda p, i: (p, i, 0))
    return _pallas_call(
        body, name=name, grid=(N_CHIPS, hr // tr),
        in_specs=[pl.BlockSpec((None, None, tr, c), lambda p, i: (p, _core(), i, 0)), blk], out_specs=blk,
        out_shape=jax.ShapeDtypeStruct(recv.shape, BF), compiler_params=_params(("parallel", "parallel")),
    )(view, recv)


def _chip_sum(parts, recv, name):
    _, hr, c = parts.shape
    tr = _row_tile(hr, c, 6)

    def body(p_ref, r_ref, o_ref):
        acc = p_ref[...].astype(F32)
        for j in range(3):
            acc = acc + r_ref[j].astype(F32)
        o_ref[...] = acc

    return _pallas_call(
        body, name=name, grid=(hr // tr,),
        in_specs=[pl.BlockSpec((None, tr, c), lambda i: (_chip(), i, 0)), pl.BlockSpec((3, tr, c), lambda i: (0, i, 0))],
        out_specs=pl.BlockSpec((tr, c), lambda i: (i, 0)),
        out_shape=jax.ShapeDtypeStruct((hr, c), F32), compiler_params=_params(("parallel",)),
    )(parts, recv)


def _sum_slices(a, name):
    n, rows, width = a.shape
    tr = _row_tile(rows, width, n + 1)

    def body(a_ref, o_ref):
        acc = a_ref[0].astype(F32)
        for i in range(1, n):
            acc = acc + a_ref[i].astype(F32)
        o_ref[...] = acc

    return _pallas_call(
        body, name=name, grid=(rows // tr,), in_specs=[pl.BlockSpec((n, tr, width), lambda i: (0, i, 0))],
        out_specs=pl.BlockSpec((tr, width), lambda i: (i, 0)), out_shape=jax.ShapeDtypeStruct((rows, width), F32),
        compiler_params=_params(("parallel",)),
    )(a)


def _adamw_update(w, g, m, v):
    nm = ADAM_B1 * m + (1.0 - ADAM_B1) * g
    nv = ADAM_B2 * v + (1.0 - ADAM_B2) * (g * g)
    m_hat = nm / (1.0 - ADAM_B1 ** ADAM_STEP)
    v_hat = nv / (1.0 - ADAM_B2 ** ADAM_STEP)
    return -ADAM_LR * (m_hat / (jnp.sqrt(v_hat) + ADAM_EPS) + ADAM_WD * w), nm, nv


def _adamw(w, g, m, v, name):
    rows, width = w.shape
    tr = _row_tile(rows, width, 7)

    def body(w_ref, g_ref, m_ref, v_ref, d_ref, nm_ref, nv_ref):
        d_ref[...], nm_ref[...], nv_ref[...] = _adamw_update(w_ref[...], g_ref[...], m_ref[...], v_ref[...])

    blk = pl.BlockSpec((tr, width), lambda i: (i, 0))
    return _pallas_call(
        body, name=name, grid=(rows // tr,), in_specs=[blk] * 4, out_specs=[blk] * 3,
        out_shape=[jax.ShapeDtypeStruct((rows, width), F32)] * 3, compiler_params=_params(("parallel",)),
    )(w, g, m, v)


def _adamw_halves(w3, m3, v3, mine, other, name):
    depth, r, c = w3.shape
    assert depth == 2
    hr = r // 2
    tr = _row_tile(hr, c, 11)
    sources = ((0, True, mine[0]), (0, False, other[0]), (1, True, mine[1]), (1, False, other[1]))

    def active(l, h, layer, own):
        mine_half = h == _core()
        return (l == layer) & (mine_half if own else jnp.logical_not(mine_half))

    def body(w_ref, m_ref, v_ref, *rest):
        g_refs, (go_ref, d_ref, nm_ref, nv_ref) = rest[:4], rest[4:]
        l, h = pl.program_id(0), pl.program_id(1)
        for (layer, own, _), g_ref in zip(sources, g_refs):
            @pl.when(active(l, h, layer, own))
            def _():
                gv = g_ref[...]
                go_ref[...] = gv
                d_ref[...], nm_ref[...], nv_ref[...] = _adamw_update(w_ref[...], gv, m_ref[...], v_ref[...])

    def gspec(layer, own):
        return pl.BlockSpec((tr, c), lambda l, h, i: (jnp.where(active(l, h, layer, own), i, 0), 0))

    blk = pl.BlockSpec((None, None, tr, c), lambda l, h, i: (l, h, i, 0))
    view = lambda a: a.reshape(depth, 2, hr, c)
    outs = _pallas_call(
        body, name=name, grid=(depth, 2, hr // tr),
        in_specs=[blk] * 3 + [gspec(layer, own) for layer, own, _ in sources], out_specs=[blk] * 4,
        out_shape=[jax.ShapeDtypeStruct((depth, 2, hr, c), F32)] * 4,
        compiler_params=_params(("parallel", "parallel", "parallel")),
    )(view(w3), view(m3), view(v3), *[s[2] for s in sources])
    return [o.reshape(w3.shape) for o in outs]


BIG = ("w_in", "w_pool_up", "w_conv_out", "w_attn_up", "w_o", "w_ff1", "w_ff2")
SMALL = ("norm_mix", "b_gate", "pool_mix", "pool_scale", "conv_w", "q_gain", "k_gain", "norm_mlp")
ORDER = ("norm_mix", "w_in", "b_gate", "pool_mix", "pool_scale", "conv_w", "q_gain", "k_gain", "w_pool_up",
         "w_conv_out", "w_attn_up", "w_o", "norm_mlp", "w_ff1", "w_ff2")
COLUMN_SHARDED = ("w_pool_up", "w_conv_out", "w_attn_up", "w_ff1")


def _matrix_weights(gathered):
    w = {}
    for name, g4 in gathered.items():
        if name in COLUMN_SHARDED:
            w[name] = g4
        else:
            w[name] = g4.reshape(N_CHIPS * g4.shape[1], g4.shape[2])
    return w


def _small_weights(l, small):
    w = {}
    w["norm_mix"] = small["norm_mix"][l][None]
    w["norm_mlp"] = small["norm_mlp"][l][None]
    w["b_gate"] = small["b_gate"][l][None]
    w["pool_mix"] = small["pool_mix"][l].astype(BF)
    w["pool_scale"] = small["pool_scale"][l][None]
    w["conv_w"] = jnp.pad(small["conv_w_full"][l], ((0, 5), (0, 0)))
    w["qk_gain"] = jnp.pad(jnp.stack([jnp.tile(small["q_gain"][l], 2), jnp.tile(small["k_gain"][l], 2)]), ((0, 6), (0, 0)))
    return w


def _to_chip_major(name, g):
    if name == "w_in":
        return g.T.reshape(N_CHIPS, g.shape[1] // N_CHIPS, g.shape[0])
    if name in COLUMN_SHARDED:
        return g
    return g.reshape(N_CHIPS, g.shape[0] // N_CHIPS, g.shape[1])


def _pad8(a):
    a = a.reshape(-1)
    return jnp.pad(a, (0, (-a.size) % (8 * LANES))).reshape(-1, LANES)


def kernel(x, norm_mix, w_in, b_gate, pool_mix, pool_scale, conv_w, q_gain, k_gain, w_pool_up, w_conv_out, w_attn_up, w_o, norm_mlp, w_ff1, w_ff2, loss_target, m_norm_mix, m_w_in, m_b_gate, m_pool_mix, m_pool_scale, m_conv_w, m_q_gain, m_k_gain, m_w_pool_up, m_w_conv_out, m_w_attn_up, m_w_o, m_norm_mlp, m_w_ff1, m_w_ff2, v_norm_mix, v_w_in, v_b_gate, v_pool_mix, v_pool_scale, v_conv_w, v_q_gain, v_k_gain, v_w_pool_up, v_w_conv_out, v_w_attn_up, v_w_o, v_norm_mlp, v_w_ff1, v_w_ff2):
    weights = dict(norm_mix=norm_mix, w_in=w_in, b_gate=b_gate, pool_mix=pool_mix, pool_scale=pool_scale, conv_w=conv_w,
                   q_gain=q_gain, k_gain=k_gain, w_pool_up=w_pool_up, w_conv_out=w_conv_out, w_attn_up=w_attn_up,
                   w_o=w_o, norm_mlp=norm_mlp, w_ff1=w_ff1, w_ff2=w_ff2)
    moms = dict(norm_mix=m_norm_mix, w_in=m_w_in, b_gate=m_b_gate, pool_mix=m_pool_mix, pool_scale=m_pool_scale,
                conv_w=m_conv_w, q_gain=m_q_gain, k_gain=m_k_gain, w_pool_up=m_w_pool_up, w_conv_out=m_w_conv_out,
                w_attn_up=m_w_attn_up, w_o=m_w_o, norm_mlp=m_norm_mlp, w_ff1=m_w_ff1, w_ff2=m_w_ff2)
    vels = dict(norm_mix=v_norm_mix, w_in=v_w_in, b_gate=v_b_gate, pool_mix=v_pool_mix, pool_scale=v_pool_scale,
                conv_w=v_conv_w, q_gain=v_q_gain, k_gain=v_k_gain, w_pool_up=v_w_pool_up, w_conv_out=v_w_conv_out,
                w_attn_up=v_w_attn_up, w_o=v_w_o, norm_mlp=v_norm_mlp, w_ff1=v_w_ff1, w_ff2=v_w_ff2)
    depth = norm_mix.shape[0]
    q = 2 * lax.axis_index("x") + lax.axis_index("y")
    for group in (weights, moms, vels):
        group["w_in"] = jnp.swapaxes(group["w_in"], 1, 2)

    assert depth == 2, "the second layer's gather hides behind the first layer's forward, and likewise backward"
    first, rest = BIG[:1], BIG[1:]
    cw_all = _all_to_all_small(_pad8(conv_w))
    bufs = [{n: _cast_place(weights[n], 0, f"cast_{n}_l0") for n in first}]
    a_ssem, a_rsem, a_views, a_token = _gather_start([bufs[0][n] for n in first], "gather_start_l0_in", cw_all)
    bufs[0].update({n: _cast_place(weights[n], 0, f"cast_{n}_l0") for n in rest})
    bufs += [{n: _cast_place(weights[n], l, f"cast_{n}_l{l}") for n in BIG} for l in range(1, depth)]
    b_ssem, b_rsem, b_views, b_token = _gather_start([bufs[0][n] for n in rest], "gather_start_l0_rest", a_token)
    g_ssem, g_rsem, g_views, g_token = _gather_start([bufs[1][n] for n in BIG], "gather_start_l1", b_token)
    conv_w_full = jnp.concatenate(
        [cw_all[2 * p].reshape(-1)[:conv_w.size].reshape(conv_w.shape) for p in range(N_CHIPS)], axis=-1)
    small = dict(weights)
    small["conv_w_full"] = conv_w_full

    def soon_weights(t):
        got = _gather_finish(a_ssem, a_rsem, a_views, t, "gather_wait_l0_in", "gather_forward_l0_in",
                             [bufs[0][n].shape for n in first])
        return _matrix_weights(dict(zip(first, got)))

    def late_weights(t):
        got = _gather_finish(b_ssem, b_rsem, b_views, t, "gather_wait_l0_rest", "gather_forward_l0_rest",
                             [bufs[0][n].shape for n in rest])
        return _matrix_weights(dict(zip(rest, got)))

    wl, saved = [None] * depth, [None] * depth
    h, saved[0], wl[0] = _layer_fwd(x[0], _small_weights(0, small), "l0", after=g_token, soon=soon_weights,
                                    late=late_weights)
    got = _gather_finish(g_ssem, g_rsem, g_views, h, "gather_wait_l1", "gather_forward_l1",
                         [bufs[1][n].shape for n in BIG])
    h, saved[1], wl[1] = _layer_fwd(h, dict(_small_weights(1, small), **_matrix_weights(dict(zip(BIG, got)))), "l1")
    dh, loss_row = _loss_grad(h, loss_target[0], "loss")

    def pair_stage(names, g, tag):
        views = [_halves(_to_chip_major(n, g[n])) for n in names]
        from_sibling = _pair_swap(views, f"grad_pair_swap_{tag}")
        return [_pair_sum(views[k], from_sibling[k], f"pair_sum_{n}_{tag}") for k, n in enumerate(names)]

    mine, other = [{}, {}], [{}, {}]

    def finish(names, l, started, after, tag):
        ssem, rsem, parts, landing, _ = started
        parts, arrived = _chip_exchange_wait(ssem, rsem, parts, landing, after, f"grad_chip_exchange_wait_{tag}")
        got = [_chip_sum(parts[k], arrived[k], f"chip_sum_{n}_{tag}") for k, n in enumerate(names)]
        mine[l].update(zip(names, got))
        other[l].update(zip(names, _pair_send(got, f"grad_pair_send_{tag}")))

    def small_pieces(g):
        return [_pad8(g[n][:3] if n == "conv_w" else g[n]) for n in SMALL]

    def start_small(l):
        return _all_to_all_small_start(jnp.concatenate(small_pieces(grads[l]), axis=0), f"small_grad_exchange_start_l{l}")

    grads, early, small = [None] * depth, {}, [None] * depth
    dh, grads[1] = _layer_bwd(dh, wl[1], saved[1], "l1")
    second = _chip_exchange_start(pair_stage(BIG, grads[1], "l1"), "grad_chip_exchange_start_l1")
    small[1] = start_small(1)

    def start_rest(g):
        early["rest"] = _chip_exchange_start(pair_stage(rest, g, "l0_rest"), "grad_chip_exchange_start_l0_rest")
        return early["rest"][4]

    def start_last(g):
        early["in"] = _chip_exchange_start(pair_stage(first, g, "l0_in"), "grad_chip_exchange_start_l0_in")
        return early["in"][4]

    dh, grads[0] = _layer_bwd(dh, wl[0], saved[0], "l0", after=[second[4], small[1][4]], mid=start_rest,
                              tail=start_last)
    small[0] = start_small(0)
    finish(BIG, 1, second, dh, "l1")
    finish(rest, 0, early["rest"], dh, "l0_rest")
    loss = lax.psum(loss_row[0, 0], ("x", "y", "c"))
    full = {}

    deltas, new_m, new_v = {}, {}, {}

    def update_matrix(n):
        full[n], deltas[n], new_m[n], new_v[n] = _adamw_halves(
            weights[n], moms[n], vels[n], [mine[l][n] for l in range(depth)], [other[l][n] for l in range(depth)],
            f"adamw_{n}")

    for n in rest:
        update_matrix(n)
    finish(first, 0, early["in"], deltas[rest[-1]], "l0_in")
    for n in first:
        update_matrix(n)
    summed = []
    for l in range(depth):
        ssem, rsem, part, landing, _ = small[l]
        summed.append(_sum_slices(_all_to_all_small_wait(ssem, rsem, part, landing, deltas[first[-1]],
                                                         f"small_grad_exchange_wait_l{l}"), f"small_sum_l{l}"))
    row = 0
    for n, piece in zip(SMALL, small_pieces(grads[0])):
        size = (weights[n].size if n != "conv_w" else depth * 3 * 512) // depth
        flat = jnp.stack([s[row:row + piece.shape[0]].reshape(-1)[:size] for s in summed])
        row += piece.shape[0]
        if n == "conv_w":
            full[n] = lax.dynamic_slice_in_dim(flat.reshape(depth, 3, 512), q * conv_w.shape[2], conv_w.shape[2], axis=2)
        else:
            full[n] = flat.reshape(weights[n].shape)
    for n in SMALL:
        shape = weights[n].shape
        two_d = (-1, shape[-1]) if n not in ("conv_w", "q_gain", "k_gain") else (1, -1)
        d2, m2, v2 = _adamw(weights[n].reshape(two_d), full[n].reshape(two_d), moms[n].reshape(two_d),
                            vels[n].reshape(two_d), f"adamw_{n}")
        deltas[n], new_m[n], new_v[n] = d2.reshape(shape), m2.reshape(shape), v2.reshape(shape)
        full[n] = full[n].reshape(shape)
    for group in (full, deltas, new_m, new_v):
        group["w_in"] = jnp.swapaxes(group["w_in"], 1, 2)
    return (loss, dh[None], *[full[n] for n in ORDER], *[deltas[n] for n in ORDER], *[new_m[n] for n in ORDER],
            *[new_v[n] for n in ORDER])
```

```python
import functools

import jax
import jax.numpy as jnp
from jax import lax
from jax.experimental import pallas as pl
from jax.experimental.pallas import tpu as pltpu

F32 = jnp.float32
BF = jnp.bfloat16
MESH_ID = pl.DeviceIdType.MESH
ANY = pl.BlockSpec(memory_space=pl.ANY)

EPS = 1e-6
MASK_VALUE = -1e30
POOL_WINDOWS = (2, 4, 8, 16)
ATTN_DILATIONS = (1, 4, 16)
ATTN_BLOCK = 128
HEAD_DIM = 64
OFF_Q, OFF_K, OFF_V, OFF_GATE = 2048, 2816, 3584, 4352
N_CHIPS = 4
ADAM_LR, ADAM_B1, ADAM_B2, ADAM_EPS, ADAM_WD, ADAM_STEP = 0.001, 0.9, 0.999, 1e-08, 0.01, 10

VMEM_LIMIT = 48 * 1024 * 1024
LANES = 128

_DIMS = {"nn": (((1,), (0,)), ((), ())), "nt": (((1,), (1,)), ((), ())), "tn": (((0,), (0,)), ((), ()))}


def _params(sem):
    return pltpu.CompilerParams(dimension_semantics=sem, vmem_limit_bytes=VMEM_LIMIT)


def _pallas_call(body, **kw):
    def in_hbm(s):
        pin = isinstance(s, jax.ShapeDtypeStruct) and s is not TOKEN and jnp.issubdtype(s.dtype, jnp.floating)
        return pltpu.HBM(s.shape, s.dtype) if pin else s

    out_shape = kw.pop("out_shape")
    kw["out_shape"] = [in_hbm(s) for s in out_shape] if isinstance(out_shape, (list, tuple)) else in_hbm(out_shape)
    call = pl.pallas_call(body, **kw)

    def run(*args):
        pinned = [pltpu.with_memory_space_constraint(a, pltpu.HBM)
                  if hasattr(a, "dtype") and jnp.issubdtype(a.dtype, jnp.floating) else a for a in args]
        return call(*pinned)

    return run


def _dot(a, b, mode="nn"):
    return lax.dot_general(a, b, _DIMS[mode], preferred_element_type=F32)


def _mm(a, b, mode, name, *, tm, tn, tk, out_dtype=F32, res=None, aux=None, epi=None, n_outer=False,
        b_shards=False, out_shards=False, after=None):
    if mode == "tn":
        K, M = a.shape
    else:
        M, K = a.shape
    if b_shards:
        if mode == "nn":
            assert b.shape[1] == K
            N = b.shape[2] * N_CHIPS
        else:
            assert mode == "nt"
            N = b.shape[1]
            assert b.shape[2] * N_CHIPS == K
    else:
        N = b.shape[0] if mode == "nt" else b.shape[1]
    tm, tn, tk = min(tm, M), min(tn, N), min(tk, K)
    assert M % tm == 0 and N % tn == 0 and K % tk == 0
    nk = K // tk
    if n_outer:
        grid = (N // tn, M // tm, nk)
        ij = lambda p, q_: (q_, p)
    else:
        grid = (M // tm, N // tn, nk)
        ij = lambda p, q_: (p, q_)

    def amap(p, q_, k):
        i, j = ij(p, q_)
        return (k, i) if mode == "tn" else (i, k)

    a_spec = pl.BlockSpec((tk, tm) if mode == "tn" else (tm, tk), amap)
    if b_shards:
        if mode == "nn":
            per = (N // N_CHIPS) // tn
            assert per >= 1 and (N // N_CHIPS) % tn == 0

            def bmap(p, q_, k):
                i, j = ij(p, q_)
                return (j // per, k, j % per)

            b_spec = pl.BlockSpec((None, tk, tn), bmap)
        else:
            per = (K // N_CHIPS) // tk
            assert per >= 1 and (K // N_CHIPS) % tk == 0

            def bmap(p, q_, k):
                i, j = ij(p, q_)
                return (k // per, j, k % per)

            b_spec = pl.BlockSpec((None, tn, tk), bmap)
    else:
        def bmap(p, q_, k):
            i, j = ij(p, q_)
            return (j, k) if mode == "nt" else (k, j)

        b_spec = pl.BlockSpec((tn, tk) if mode == "nt" else (tk, tn), bmap)

    def omap(p, q_, k):
        return ij(p, q_)

    o_spec = pl.BlockSpec((tm, tn), omap)
    if out_shards:
        per_o = (N // N_CHIPS) // tn
        assert per_o >= 1 and (N // N_CHIPS) % tn == 0

        def osmap(p, q_, k):
            i, j = ij(p, q_)
            return (j // per_o, i, j % per_o)

        out_spec0 = pl.BlockSpec((None, tm, tn), osmap)
        out_shape0 = jax.ShapeDtypeStruct((N_CHIPS, M, N // N_CHIPS), out_dtype)
    else:
        out_spec0 = o_spec
        out_shape0 = jax.ShapeDtypeStruct((M, N), out_dtype)

    in_specs = [a_spec, b_spec]
    args = [a, b]
    if res is not None:
        in_specs.append(o_spec)
        args.append(res)
    if aux is not None:
        in_specs.append(o_spec)
        args.append(aux)
    after = [] if after is None else list(after) if isinstance(after, (list, tuple)) else [after]
    in_specs += [ANY] * len(after)
    args += after
    out_specs = [out_spec0]
    out_shape = [out_shape0]
    n_out = len(out_shape)
    has_res, has_aux, n_after = res is not None, aux is not None, len(after)

    def body(*refs):
        a_ref, b_ref = refs[0], refs[1]
        pos = 2
        res_ref = aux_ref = None
        if has_res:
            res_ref = refs[pos]
            pos += 1
        if has_aux:
            aux_ref = refs[pos]
            pos += 1
        pos += n_after
        outs = refs[pos:pos + n_out]
        part = _dot(a_ref[...].astype(BF), b_ref[...].astype(BF), mode)

        def finish(acc):
            if res_ref is not None:
                acc = res_ref[...] + acc
            if epi == "relu2":
                r = jnp.maximum(acc, 0.0)
                outs[0][...] = (r * r).astype(out_dtype)
            elif epi == "drelu2":
                outs[0][...] = (acc * (2.0 * jnp.sqrt(aux_ref[...].astype(F32)))).astype(out_dtype)
            else:
                outs[0][...] = acc.astype(out_dtype)

        if nk == 1:
            finish(part)
        else:
            acc_ref = refs[pos + n_out]
            k = pl.program_id(2)

            @pl.when(k == 0)
            def _():
                acc_ref[...] = part

            @pl.when(k > 0)
            def _():
                acc_ref[...] += part

            @pl.when(k == nk - 1)
            def _():
                finish(acc_ref[...])

    scratch = [pltpu.VMEM((tm, tn), F32)] if nk > 1 else []
    out = _pallas_call(
        body, name=name, grid=grid, in_specs=in_specs, out_specs=out_specs, out_shape=out_shape,
        scratch_shapes=scratch, compiler_params=_params(("parallel", "parallel", "arbitrary")),
    )(*args)
    return out if n_out > 1 else out[0]


def _rms_fwd(x, gain, name, after=None):
    T, D = x.shape
    tm = min(512, T)

    def body(x_ref, g_ref, *rest):
        o_ref = rest[-1]
        xv = x_ref[...]
        r = lax.rsqrt(jnp.mean(xv * xv, axis=-1, keepdims=True) + EPS)
        o_ref[...] = ((xv * r) * g_ref[...]).astype(BF)

    extra = [] if after is None else list(after) if isinstance(after, (list, tuple)) else [after]
    return _pallas_call(
        body, name=name, grid=(T // tm,),
        in_specs=[pl.BlockSpec((tm, D), lambda i: (i, 0)), pl.BlockSpec((1, D), lambda i: (0, 0))] + [ANY] * len(extra),
        out_specs=pl.BlockSpec((tm, D), lambda i: (i, 0)), out_shape=jax.ShapeDtypeStruct((T, D), BF),
        compiler_params=_params(("parallel",)),
    )(x, gain, *extra)


def _rms_bwd(dh, x, gain, dres, name):
    T, D = x.shape
    tm = min(512, T)

    def body(dh_ref, x_ref, g_ref, dres_ref, dx_ref, dg_ref):
        xv = x_ref[...]
        r = lax.rsqrt(jnp.mean(xv * xv, axis=-1, keepdims=True) + EPS)
        xhat = xv * r
        dhv = dh_ref[...]
        dy = dhv * g_ref[...]
        dx_ref[...] = dres_ref[...] + r * (dy - xhat * jnp.mean(dy * xhat, axis=-1, keepdims=True))

        @pl.when(pl.program_id(0) == 0)
        def _():
            dg_ref[...] = jnp.zeros_like(dg_ref)

        dg_ref[...] += jnp.sum(dhv * xhat, axis=0, keepdims=True)

    row = pl.BlockSpec((tm, D), lambda i: (i, 0))
    vec = pl.BlockSpec((1, D), lambda i: (0, 0))
    return _pallas_call(
        body, name=name, grid=(T // tm,), in_specs=[row, row, vec, row], out_specs=[row, vec],
        out_shape=[jax.ShapeDtypeStruct((T, D), F32), jax.ShapeDtypeStruct((1, D), F32)],
        compiler_params=_params(("arbitrary",)),
    )(dh, x, gain, dres)


def _loss_grad(y, target, name):
    T, D = y.shape
    tm = min(512, T)

    def body(y_ref, t_ref, dy_ref, l_ref):
        e = y_ref[...] - t_ref[...]
        dy_ref[...] = e / float(D)

        @pl.when(pl.program_id(0) == 0)
        def _():
            l_ref[...] = jnp.zeros_like(l_ref)

        l_ref[...] += 0.5 * jnp.sum(jnp.mean(e * e, axis=-1, keepdims=True))

    row = pl.BlockSpec((tm, D), lambda i: (i, 0))
    return _pallas_call(
        body, name=name, grid=(T // tm,), in_specs=[row, row],
        out_specs=[row, pl.BlockSpec((1, LANES), lambda i: (0, 0))],
        out_shape=[jax.ShapeDtypeStruct((T, D), F32), jax.ShapeDtypeStruct((1, LANES), F32)],
        compiler_params=_params(("arbitrary",)),
    )(y, target)


POOL_HALO = 16
CONV_HALO = 8


def _causal_window_sum(v, w):
    s, sh = v, 1
    while sh < w:
        s = s + pltpu.roll(s, sh, 0)
        sh *= 2
    return s


def _anticausal_window_sum(v, w):
    n = v.shape[0]
    s, sh = v, 1
    while sh < w:
        s = s + pltpu.roll(s, n - sh, 0)
        sh *= 2
    return s


def _poolconv_fwd(z, pmix_b, pscale, convw, name):
    T = z.shape[0]
    R = min(512, T)
    PH, CH = R // POOL_HALO, R // CONV_HALO

    def body(u_ref, uh_ref, b_ref, c_ref, ch_ref, x_ref, xh_ref, mix_ref, sc_ref, cw_ref, yp_ref, yc_ref):
        i = pl.program_id(0)
        keep = (i > 0).astype(F32)
        row = i * R + lax.broadcasted_iota(jnp.int32, (R, 1), 0)
        w_all = jnp.concatenate([uh_ref[...] * keep, u_ref[...]], axis=0)
        for g, w in enumerate(POOL_WINDOWS):
            cols = slice(128 * g, 128 * (g + 1))
            wg = w_all[:, cols]
            s = _causal_window_sum(wg, w)[POOL_HALO:]
            cnt = jnp.minimum(row + 1, w).astype(F32)
            dgrp = s / cnt - wg[POOL_HALO:]
            y = _dot(dgrp.astype(BF), mix_ref[g]) * sc_ref[:, cols]
            yp_ref[:, cols] = y.astype(BF)
        uc = jnp.concatenate([ch_ref[...] * xh_ref[...] * keep, c_ref[...] * x_ref[...]], axis=0)
        yc = cw_ref[2:3, :] * uc + cw_ref[0:1, :] * pltpu.roll(uc, 2, 0) + cw_ref[1:2, :] * pltpu.roll(uc, 1, 0)
        yc_ref[...] = (b_ref[...] * yc[CONV_HALO:]).astype(BF)

    def main(cb):
        return pl.BlockSpec((R, 512), lambda i: (i, cb))

    def prev(cb, halo, per):
        return pl.BlockSpec((halo, 512), lambda i: (jnp.maximum(i * per - 1, 0), cb))

    full = lambda a: pl.BlockSpec(a.shape, lambda i: (0,) * a.ndim)
    return _pallas_call(
        body, name=name, grid=(T // R,),
        in_specs=[main(0), prev(0, POOL_HALO, PH), main(1), main(2), prev(2, CONV_HALO, CH), main(3),
                  prev(3, CONV_HALO, CH), full(pmix_b), full(pscale), full(convw)],
        out_specs=[pl.BlockSpec((R, 512), lambda i: (i, 0))] * 2,
        out_shape=[jax.ShapeDtypeStruct((T, 512), BF)] * 2,
        compiler_params=_params(("parallel",)),
    )(z, z, z, z, z, z, z, pmix_b, pscale, convw)


def _poolconv_bwd(z, dyp, dyc, pmix_b, pscale, convw, dz, name):
    T = z.shape[0]
    R = min(512, T)
    PH, CH = R // POOL_HALO, R // CONV_HALO
    nsteps = T // R

    def body(u_ref, uh_ref, b_ref, bn_ref, c_ref, ch_ref, x_ref, xh_ref, dyp_ref, dypn_ref, dyc_ref, dycn_ref,
             mix_ref, sc_ref, cw_ref, dz_in_ref, dz_ref, dmix_ref, dsc_ref, dcw_ref):
        i = pl.program_id(0)
        keep_prev = (i > 0).astype(F32)
        keep_next = (i < nsteps - 1).astype(F32)

        @pl.when(i == 0)
        def _():
            dmix_ref[...] = jnp.zeros_like(dmix_ref)
            dsc_ref[...] = jnp.zeros_like(dsc_ref)
            dcw_ref[...] = jnp.zeros_like(dcw_ref)

        row = i * R + lax.broadcasted_iota(jnp.int32, (R, 1), 0)
        row_ext = i * R + lax.broadcasted_iota(jnp.int32, (R + POOL_HALO, 1), 0)
        w_all = jnp.concatenate([uh_ref[...] * keep_prev, u_ref[...]], axis=0)
        dyp_ext = jnp.concatenate([dyp_ref[...], dypn_ref[...] * keep_next], axis=0)
        for g, w in enumerate(POOL_WINDOWS):
            cols = slice(128 * g, 128 * (g + 1))
            wg = w_all[:, cols]
            s = _causal_window_sum(wg, w)[POOL_HALO:]
            cnt = jnp.minimum(row + 1, w).astype(F32)
            dgrp = (s / cnt - wg[POOL_HALO:]).astype(BF)
            y_pre = _dot(dgrp, mix_ref[g])
            dsc_ref[:, cols] += jnp.sum(dyp_ref[:, cols] * y_pre, axis=0, keepdims=True)
            dyb = (dyp_ext[:, cols] * sc_ref[:, cols]).astype(BF)
            dmix_ref[cols, :] += _dot(dgrp, dyb[:R], "tn")
            dd = _dot(dyb, mix_ref[g], "nt")
            cnt_ext = jnp.minimum(row_ext + 1, w).astype(F32)
            e = _anticausal_window_sum(dd / cnt_ext, w)
            dz_ref[:, cols] = (e[:R] - dd[:R]).astype(BF)
        cw0, cw1, cw2 = cw_ref[0:1, :], cw_ref[1:2, :], cw_ref[2:3, :]
        uc = jnp.concatenate([ch_ref[...] * xh_ref[...] * keep_prev, c_ref[...] * x_ref[...]], axis=0)
        uc1 = pltpu.roll(uc, 1, 0)[CONV_HALO:]
        uc2 = pltpu.roll(uc, 2, 0)[CONV_HALO:]
        uc0 = uc[CONV_HALO:]
        yc = cw2 * uc0 + cw0 * uc2 + cw1 * uc1
        dycv = dyc_ref[...]
        dz_ref[:, 512:1024] = (dycv * yc).astype(BF)
        dv_ext = jnp.concatenate([dycv * b_ref[...], dycn_ref[...] * bn_ref[...] * keep_next], axis=0)
        n_ext = R + CONV_HALO
        duc = (cw2 * dv_ext + cw1 * pltpu.roll(dv_ext, n_ext - 1, 0) + cw0 * pltpu.roll(dv_ext, n_ext - 2, 0))[:R]
        dv = dv_ext[:R]
        dcw_ref[0:1, :] += jnp.sum(dv * uc2, axis=0, keepdims=True)
        dcw_ref[1:2, :] += jnp.sum(dv * uc1, axis=0, keepdims=True)
        dcw_ref[2:3, :] += jnp.sum(dv * uc0, axis=0, keepdims=True)
        dz_ref[:, 1024:1536] = (duc * x_ref[...]).astype(BF)
        dz_ref[:, 1536:2048] = (duc * c_ref[...]).astype(BF)

    def main(cb):
        return pl.BlockSpec((R, 512), lambda i: (i, cb))

    def prev(cb, halo, per):
        return pl.BlockSpec((halo, 512), lambda i: (jnp.maximum(i * per - 1, 0), cb))

    def nxt(cb, halo, per):
        return pl.BlockSpec((halo, 512), lambda i: (jnp.minimum((i + 1) * per, T // halo - 1), cb))

    full = lambda a: pl.BlockSpec(a.shape, lambda i: (0,) * a.ndim)
    return _pallas_call(
        body, name=name, grid=(nsteps,),
        in_specs=[main(0), prev(0, POOL_HALO, PH), main(1), nxt(1, CONV_HALO, CH), main(2), prev(2, CONV_HALO, CH),
                  main(3), prev(3, CONV_HALO, CH), main(0), nxt(0, POOL_HALO, PH), main(0), nxt(0, CONV_HALO, CH),
                  full(pmix_b), full(pscale), full(convw), ANY],
        out_specs=[pl.BlockSpec((R, 2048), lambda i: (i, 0)), pl.BlockSpec((512, 128), lambda i: (0, 0)),
                   pl.BlockSpec((1, 512), lambda i: (0, 0)), pl.BlockSpec((8, 512), lambda i: (0, 0))],
        out_shape=[jax.ShapeDtypeStruct(dz.shape, BF), jax.ShapeDtypeStruct((512, 128), F32),
                   jax.ShapeDtypeStruct((1, 512), F32), jax.ShapeDtypeStruct((8, 512), F32)],
        input_output_aliases={15: 0}, compiler_params=_params(("arbitrary",)),
    )(z, z, z, z, z, z, z, z, dyp, dyp, dyc, dyc, pmix_b, pscale, convw, dz)


def _head_sums(v):
    row = lax.broadcasted_iota(jnp.int32, (LANES, LANES), 0) < HEAD_DIM
    col = lax.broadcasted_iota(jnp.int32, (LANES, LANES), 1) < HEAD_DIM
    same_head = jnp.where(jnp.logical_xor(row, col), 0.0, 1.0).astype(BF)
    hi = v.astype(BF)
    lo = (v - hi.astype(F32)).astype(BF)
    return _dot(hi, same_head) + _dot(lo, same_head)


def _head_norm(x, g2, ma):
    r = lax.rsqrt(_head_sums(x * x) / HEAD_DIM + EPS)
    return x * r, r


def _head_norm_bwd(dy, xhat, r, g2, ma):
    dxh = dy * g2
    return r * (dxh - xhat * (_head_sums(dxh * xhat) / HEAD_DIM))


def _head_col(tile, hm):
    return jnp.max(jnp.where(hm, tile, -jnp.inf), axis=-1, keepdims=True)


def _attn_masks(other_block_exists):
    lane = lax.broadcasted_iota(jnp.int32, (2 * ATTN_BLOCK, ATTN_BLOCK), 1)
    qi = lax.broadcasted_iota(jnp.int32, (2 * ATTN_BLOCK, ATTN_BLOCK), 0) & (ATTN_BLOCK - 1)
    never = (1 - other_block_exists.astype(jnp.int32)) * (2 * ATTN_BLOCK)
    return lane[:ATTN_BLOCK] < HEAD_DIM, lane <= qi, lane >= qi + never


def _stack_heads(x, ma):
    return jnp.concatenate([jnp.where(ma, x, 0.0), jnp.where(ma, 0.0, x)], axis=0)


def _unstack_heads(y, ma):
    return jnp.where(ma, y[:ATTN_BLOCK], y[ATTN_BLOCK:])


def _stack_cols(tile, ma):
    return jnp.concatenate([_head_col(tile, ma), _head_col(tile, jnp.logical_not(ma))], axis=0)


QKV_TILES = (OFF_GATE - OFF_Q) // LANES
KIND_TILES = QKV_TILES // 3


def _qk_norm(z, gains, name):
    T = z.shape[0]
    tm = min(512, T)

    def body(x_ref, g_ref, o_ref):
        ma = lax.broadcasted_iota(jnp.int32, (tm, LANES), 1) < HEAD_DIM
        for tile in range(QKV_TILES):
            v = x_ref[:, LANES * tile:LANES * (tile + 1)]
            if tile < 2 * KIND_TILES:
                g = g_ref[0:1, :] if tile < KIND_TILES else g_ref[1:2, :]
                v = _head_norm(v, g, ma)[0] * g
            o_ref[tile] = v

    return _pallas_call(
        body, name=name, grid=(T // tm,),
        in_specs=[pl.BlockSpec((pl.Element(tm), pl.Element(OFF_GATE - OFF_Q)), lambda i: (i * tm, OFF_Q)),
                  pl.BlockSpec((8, LANES), lambda i: (0, 0))],
        out_specs=pl.BlockSpec((QKV_TILES, tm, LANES), lambda i: (0, i, 0)),
        out_shape=jax.ShapeDtypeStruct((QKV_TILES, T, LANES), F32), compiler_params=_params(("parallel",)),
    )(z, gains)


ATTN_STEP_ROWS = 1024
ATTN_UNROLL = 2


def _attn_geometry(T, d):
    sub = ATTN_BLOCK * d
    nb = T // sub
    m = max(1, min(nb, ATTN_STEP_ROWS // sub))
    assert T % sub == 0 and nb % m == 0
    return sub, nb, m


def _attn_rows(jj, r, sub, d):
    start = jj * sub + r
    if d == 1:
        return pl.ds(pl.multiple_of(start, ATTN_BLOCK), ATTN_BLOCK)
    return pl.ds(start, ATTN_BLOCK, stride=d)


def _pick(flag, a, b):
    return jnp.where(jnp.full(a.shape, flag.astype(jnp.int32)) > 0, a, b)


def _attn_fwd(qkv, g, d, name):
    T = qkv.shape[1]
    sub, nb, m = _attn_geometry(T, d)
    scale = HEAD_DIM ** -0.5

    def body(q_ref, kc_ref, kp_ref, vc_ref, vp_ref, o_ref, lse_ref):
        jb = pl.program_id(0)

        def step(s, carry):
            jj, r = s // d, s % d
            here, before = _attn_rows(jj, r, sub, d), _attn_rows(jnp.maximum(jj - 1, 0), r, sub, d)
            edge = _attn_rows(0, r, sub, d)
            first = jj == 0
            ma, mask_c, mask_p = _attn_masks(jb * m + jj > 0)
            qs = _stack_heads(q_ref[here, :], ma).astype(BF)
            kcb = kc_ref[here, :].astype(BF)
            kpb = _pick(first, kp_ref[edge, :], kc_ref[before, :]).astype(BF)
            vcb = vc_ref[here, :].astype(BF)
            vpb = _pick(first, vp_ref[edge, :], vc_ref[before, :]).astype(BF)
            s_c = jnp.where(mask_c, _dot(qs, kcb, "nt") * scale, MASK_VALUE)
            s_p = jnp.where(mask_p, _dot(qs, kpb, "nt") * scale, MASK_VALUE)
            mx = jnp.maximum(jnp.max(s_c, axis=-1, keepdims=True), jnp.max(s_p, axis=-1, keepdims=True))
            p_c = jnp.exp(s_c - mx)
            p_p = jnp.exp(s_p - mx)
            den = jnp.sum(p_c, axis=-1, keepdims=True) + jnp.sum(p_p, axis=-1, keepdims=True)
            o = (_dot(p_c.astype(BF), vcb) + _dot(p_p.astype(BF), vpb)) / den
            o_ref[here, :] = _unstack_heads(o, ma)
            lse_ref[here, :] = _unstack_heads(jnp.broadcast_to(mx + jnp.log(den), o.shape), ma)
            return carry

        lax.fori_loop(0, m * d, step, 0, unroll=ATTN_UNROLL)

    def cur(kind):
        return pl.BlockSpec((None, m * sub, LANES), lambda j, t: (KIND_TILES * kind + 2 * g + t, j, 0))

    def prv(kind):
        return pl.BlockSpec((None, sub, LANES), lambda j, t: (KIND_TILES * kind + 2 * g + t, jnp.maximum(j * m - 1, 0), 0))

    out = pl.BlockSpec((m * sub, LANES), lambda j, t: (j, t))
    return _pallas_call(
        body, name=name, grid=(nb // m, 2), in_specs=[cur(0), cur(1), prv(1), cur(2), prv(2)],
        out_specs=[out, out], out_shape=[jax.ShapeDtypeStruct((T, 256), F32)] * 2,
        compiler_params=_params(("parallel", "parallel")),
    )(qkv, qkv, qkv, qkv, qkv)


def _attn_bwd(z, qkv, do, c, lse, gains, g, d, name, after=None):
    T = z.shape[0]
    sub, nb, m = _attn_geometry(T, d)
    scale = HEAD_DIM ** -0.5
    extra = [] if after is None else [after]

    def body(qr_ref, kr_ref, vc_ref, vp_ref, qn_ref, qnn_ref, kn_ref, knp_ref, do_ref, don_ref, c_ref, cn_ref,
             lse_ref, lsen_ref, g_ref, *rest):
        dq_ref, dk_ref, dv_ref, dgq_ref, dgk_ref, sq_ref, sk_ref, sv_ref = rest[len(extra):]
        jb = pl.program_id(0)

        @pl.when((jb == 0) & (pl.program_id(1) == 0))
        def _():
            dgq_ref[...] = jnp.zeros_like(dgq_ref)
            dgk_ref[...] = jnp.zeros_like(dgk_ref)

        gq, gk = g_ref[0:1, :], g_ref[1:2, :]

        def step(s, carry):
            jj, r = s // d, s % d
            here, edge = _attn_rows(jj, r, sub, d), _attn_rows(0, r, sub, d)
            before = _attn_rows(jnp.maximum(jj - 1, 0), r, sub, d)
            behind = _attn_rows(jnp.minimum(jj + 1, m - 1), r, sub, d)
            first, last = jj == 0, jj == m - 1
            block = jb * m + jj
            ma, mask_c, mask_p = _attn_masks(block > 0)
            mask_n = _attn_masks(block < nb - 1)[2]
            qhat, rq = _head_norm(qr_ref[here, :], gq, ma)
            qn = qhat * gq
            qn_next = _pick(last, qnn_ref[edge, :], qn_ref[behind, :])
            khat, rk = _head_norm(kr_ref[here, :], gk, ma)
            kcb = (khat * gk).astype(BF)
            kpb = _pick(first, knp_ref[edge, :], kn_ref[before, :]).astype(BF)
            vcb = vc_ref[here, :].astype(BF)
            vpb = _pick(first, vp_ref[edge, :], vc_ref[before, :]).astype(BF)
            do_t, don_t = do_ref[here, :], _pick(last, don_ref[edge, :], do_ref[behind, :])
            c_t, cn_t = c_ref[here, :], _pick(last, cn_ref[edge, :], c_ref[behind, :])
            lse_t, lsen_t = lse_ref[here, :], _pick(last, lsen_ref[edge, :], lse_ref[behind, :])
            qs, dos = _stack_heads(qn, ma).astype(BF), _stack_heads(do_t, ma).astype(BF)
            lse_s, c_s = _stack_cols(lse_t, ma), _stack_cols(c_t, ma)
            s_c = jnp.where(mask_c, _dot(qs, kcb, "nt") * scale, MASK_VALUE)
            s_p = jnp.where(mask_p, _dot(qs, kpb, "nt") * scale, MASK_VALUE)
            p_c = jnp.exp(s_c - lse_s)
            p_p = jnp.exp(s_p - lse_s)
            ds_c = ((p_c * (_dot(dos, vcb, "nt") + c_s)) * scale).astype(BF)
            ds_p = ((p_p * (_dot(dos, vpb, "nt") + c_s)) * scale).astype(BF)
            dq_t = _unstack_heads(_dot(ds_c, kcb) + _dot(ds_p, kpb), ma)
            qs_n, dos_n = _stack_heads(qn_next, ma).astype(BF), _stack_heads(don_t, ma).astype(BF)
            s_n = jnp.where(mask_n, _dot(qs_n, kcb, "nt") * scale, MASK_VALUE)
            p_n = jnp.exp(s_n - _stack_cols(lsen_t, ma))
            ds_n = ((p_n * (_dot(dos_n, vcb, "nt") + _stack_cols(cn_t, ma))) * scale).astype(BF)
            dv_t = _dot(p_c.astype(BF), dos, "tn") + _dot(p_n.astype(BF), dos_n, "tn")
            dk_t = _dot(ds_c, qs, "tn") + _dot(ds_n, qs_n, "tn")
            sq_ref[here, :] = _head_norm_bwd(dq_t, qhat, rq, gq, ma)
            sk_ref[here, :] = _head_norm_bwd(dk_t, khat, rk, gk, ma)
            sv_ref[here, :] = dv_t
            dgq_ref[...] += jnp.sum(dq_t * qhat, axis=0, keepdims=True)
            dgk_ref[...] += jnp.sum(dk_t * khat, axis=0, keepdims=True)
            return carry

        lax.fori_loop(0, m * d, step, 0, unroll=ATTN_UNROLL)
        dq_ref[...] = sq_ref[...].astype(BF)
        dk_ref[...] = sk_ref[...].astype(BF)
        dv_ref[...] = sv_ref[...].astype(BF)

    def raw(col0):
        return pl.BlockSpec((m * sub, LANES), lambda j, t: (j, col0 + 2 * g + t))

    def cur(kind):
        return pl.BlockSpec((None, m * sub, LANES), lambda j, t: (KIND_TILES * kind + 2 * g + t, j, 0))

    def prv(kind):
        return pl.BlockSpec((None, sub, LANES), lambda j, t: (KIND_TILES * kind + 2 * g + t, jnp.maximum(j * m - 1, 0), 0))

    def nxt(kind):
        return pl.BlockSpec((None, sub, LANES),
                            lambda j, t: (KIND_TILES * kind + 2 * g + t, jnp.minimum((j + 1) * m, nb - 1), 0))

    own = pl.BlockSpec((m * sub, LANES), lambda j, t: (j, t))
    own_next = pl.BlockSpec((sub, LANES), lambda j, t: (jnp.minimum((j + 1) * m, nb - 1), t))
    vec = pl.BlockSpec((1, LANES), lambda j, t: (0, 0))
    return _pallas_call(
        body, name=name, grid=(nb // m, 2),
        in_specs=[raw(OFF_Q // LANES), raw(OFF_K // LANES), cur(2), prv(2), cur(0), nxt(0), cur(1), prv(1), own, own_next,
                  own, own_next,
                  own, own_next, pl.BlockSpec((8, LANES), lambda j, t: (0, 0))] + [ANY] * len(extra),
        out_specs=[own, own, own, vec, vec],
        out_shape=[jax.ShapeDtypeStruct((T, 256), BF)] * 3 + [jax.ShapeDtypeStruct((1, LANES), F32)] * 2,
        scratch_shapes=[pltpu.VMEM((m * sub, LANES), F32)] * 3,
        compiler_params=_params(("arbitrary", "arbitrary")),
    )(z, z, qkv, qkv, qkv, qkv, qkv, qkv, do, do, c, c, lse, lse, gains, *extra)


MERGE_ROWS = 256
GATE_TILE = 256


def _group_mix(o_refs, lse_refs):
    lses = [r[...] for r in lse_refs]
    m = jnp.maximum(jnp.maximum(lses[0], lses[1]), lses[2])
    es = [jnp.exp(l - m) for l in lses]
    den = es[0] + es[1] + es[2]
    ws = [e / den for e in es]
    y = ws[0] * o_refs[0][...] + ws[1] * o_refs[1][...] + ws[2] * o_refs[2][...]
    return ws, y


def _sigmoid(v):
    return 1.0 / (1.0 + jnp.exp(-v))


def _merge_specs(T, z, bgate, gpu, gco, gau):
    tm = min(MERGE_ROWS, T)
    row = lambda w: pl.BlockSpec((tm, w), lambda i: (i, 0))
    gate0 = OFF_GATE // GATE_TILE
    gates = [pl.BlockSpec((tm, GATE_TILE), functools.partial(lambda i, cb: (i, cb), cb=gate0 + n))
             for n in range(3 * N_CHIPS)]
    full = lambda a: pl.BlockSpec(a.shape, lambda i: (0,) * a.ndim)
    specs = [row(512), row(512)] + [row(256)] * 6 + gates + [full(bgate), full(gpu), full(gco), full(gau)]
    return tm, row, specs


def _merge_fwd(yp, yc, o3, lse3, z, bgate, gpu, gco, gau, name):
    T = yp.shape[0]
    tm, row, specs = _merge_specs(T, z, bgate, gpu, gco, gau)

    def body(*refs):
        yp_ref, yc_ref = refs[0], refs[1]
        o_refs, lse_refs = refs[2:5], refs[5:8]
        zg = refs[8:20]
        b_ref, gpu_ref, gco_ref, gau_ref, out_ref = refs[20:25]
        yab = _group_mix(o_refs, lse_refs)[1].astype(BF)
        ys = (yp_ref[...], yc_ref[...], yab)
        ups = (gpu_ref, gco_ref, gau_ref)
        for n in range(N_CHIPS):
            acc = None
            for b in range(3):
                gcol = slice(1024 * b + GATE_TILE * n, 1024 * b + GATE_TILE * (n + 1))
                gate = _sigmoid(zg[N_CHIPS * b + n][...] + b_ref[:, gcol])
                term = gate * _dot(ys[b], ups[b][n])
                acc = term if acc is None else acc + term
            out_ref[:, GATE_TILE * n:GATE_TILE * (n + 1)] = acc.astype(BF)

    return _pallas_call(
        body, name=name, grid=(T // tm,), in_specs=specs, out_specs=row(1024),
        out_shape=jax.ShapeDtypeStruct((T, 1024), BF), compiler_params=_params(("parallel",)),
    )(yp, yc, *o3, *lse3, *([z] * 12), bgate, gpu, gco, gau)


def _merge_bwd(dm, yp, yc, o3, lse3, z, bgate, gpu, gco, gau, name):
    T = yp.shape[0]
    tm, row, specs = _merge_specs(T, z, bgate, gpu, gco, gau)
    nsteps = T // tm

    def body(*refs):
        dm_ref, yp_ref, yc_ref = refs[0:3]
        o_refs, lse_refs = refs[3:6], refs[6:9]
        zg = refs[9:21]
        b_ref, gpu_ref, gco_ref, gau_ref = refs[21:25]
        dzg_ref, dyp_ref, dyc_ref = refs[25:28]
        do_refs, c_refs = refs[28:31], refs[31:34]
        dgpu_ref, dgco_ref, dgau_ref, dbg_ref = refs[34:38]
        accs = refs[38:41]
        i = pl.program_id(0)

        @pl.when(i == 0)
        def _():
            for a in accs:
                a[...] = jnp.zeros_like(a)
            dbg_ref[...] = jnp.zeros_like(dbg_ref)

        ws, y = _group_mix(o_refs, lse_refs)
        ys = (yp_ref[...], yc_ref[...], y.astype(BF))
        ups = (gpu_ref, gco_ref, gau_ref)
        dys = [None, None, None]
        for n in range(N_CHIPS):
            dmn = dm_ref[:, GATE_TILE * n:GATE_TILE * (n + 1)]
            for b in range(3):
                gcol = slice(1024 * b + GATE_TILE * n, 1024 * b + GATE_TILE * (n + 1))
                gate = _sigmoid(zg[N_CHIPS * b + n][...] + b_ref[:, gcol])
                up = _dot(ys[b], ups[b][n])
                dzg = (dmn * up) * (gate * (1.0 - gate))
                dzg_ref[:, gcol] = dzg.astype(BF)
                dbg_ref[:, gcol] += jnp.sum(dzg, axis=0, keepdims=True)
                dup = (dmn * gate).astype(BF)
                accs[b][n] += _dot(ys[b], dup, "tn")
                dyb = _dot(dup, ups[b][n], "nt")
                dys[b] = dyb if dys[b] is None else dys[b] + dyb
        dyp_ref[...] = dys[0]
        dyc_ref[...] = dys[1]
        dya = dys[2]
        lane = lax.broadcasted_iota(jnp.int32, dya.shape, 1) // HEAD_DIM
        pr = dya * y
        rho = jnp.zeros_like(pr)
        for h in range(256 // HEAD_DIM):
            hm = lane == h
            rho = jnp.where(hm, jnp.sum(jnp.where(hm, pr, 0.0), axis=-1, keepdims=True), rho)
        for g in range(3):
            do_refs[g][...] = ws[g] * dya
            c_refs[g][...] = -(ws[g] * rho)

        @pl.when(i == nsteps - 1)
        def _():
            dgpu_ref[...] = accs[0][...].astype(BF)
            dgco_ref[...] = accs[1][...].astype(BF)
            dgau_ref[...] = accs[2][...].astype(BF)

    full = lambda a: pl.BlockSpec(a.shape, lambda i: (0,) * a.ndim)
    dz_gate = pl.BlockSpec((pl.Element(tm), pl.Element(3072)), lambda i: (i * tm, OFF_GATE))
    out_specs = ([dz_gate, row(512), row(512)] + [row(256)] * 6 + [full(gpu), full(gco), full(gau)]
                 + [pl.BlockSpec((1, 3072), lambda i: (0, 0))])
    out_shape = ([jax.ShapeDtypeStruct(z.shape, BF)] + [jax.ShapeDtypeStruct((T, 512), F32)] * 2
                 + [jax.ShapeDtypeStruct((T, 256), F32)] * 6
                 + [jax.ShapeDtypeStruct(g.shape, BF) for g in (gpu, gco, gau)]
                 + [jax.ShapeDtypeStruct((1, 3072), F32)])
    return _pallas_call(
        body, name=name, grid=(nsteps,), in_specs=[row(1024)] + specs, out_specs=out_specs, out_shape=out_shape,
        scratch_shapes=[pltpu.VMEM(g.shape, F32) for g in (gpu, gco, gau)],
        compiler_params=_params(("arbitrary",)),
    )(dm, yp, yc, *o3, *lse3, *([z] * 12), bgate, gpu, gco, gau)


def _layer_fwd(x, w, tag, after=None, soon=None, late=None):
    hb = _rms_fwd(x, w["norm_mix"], f"rms_mix_{tag}", after=after)
    if soon is not None:
        w = dict(w, **soon(hb))
    z = _mm(hb, w["w_in"], "nt", f"in_proj_{tag}", tm=512, tn=3712, tk=1024, n_outer=True)
    yp, yc = _poolconv_fwd(z, w["pool_mix"], w["pool_scale"], w["conv_w"], f"poolconv_{tag}")
    qkv = _qk_norm(z, w["qk_gain"], f"qk_norm_{tag}")
    o3, lse3 = [], []
    for g, d in enumerate(ATTN_DILATIONS):
        o, lse = _attn_fwd(qkv, g, d, f"attn{g}_{tag}")
        o3.append(o)
        lse3.append(lse)
    if late is not None:
        w = dict(w, **late(lse3[-1]))
    merged = _merge_fwd(yp, yc, o3, lse3, z, w["b_gate"], w["w_pool_up"], w["w_conv_out"], w["w_attn_up"],
                        f"merge_{tag}")
    x1 = _mm(merged, w["w_o"], "nn", f"out_proj_{tag}", tm=1024, tn=1024, tk=1024, res=x)
    h2b = _rms_fwd(x1, w["norm_mlp"], f"rms_mlp_{tag}")
    rb = _mm(h2b, w["w_ff1"], "nn", f"ff1_{tag}", tm=1024, tn=1024, tk=1024, out_dtype=BF, epi="relu2", n_outer=True,
             b_shards=True)
    x2 = _mm(rb, w["w_ff2"], "nn", f"ff2_{tag}", tm=512, tn=1024, tk=4096, res=x1)
    saved = dict(x=x, hb=hb, z=z, yp=yp, yc=yc, qkv=qkv, o3=o3, lse3=lse3, merged=merged, x1=x1, h2b=h2b, rb=rb)
    return x2, saved, w


def _layer_bwd(dx2, w, s, tag, after=None, mid=None, tail=None):
    g = {}
    dab = _mm(dx2, w["w_ff2"], "nt", f"d_ff2_act_{tag}", tm=1024, tn=1024, tk=1024, out_dtype=BF, aux=s["rb"],
              epi="drelu2", after=after)
    g["w_ff2"] = _mm(s["rb"], dx2, "tn", f"d_ff2_w_{tag}", tm=1024, tn=1024, tk=2048, out_dtype=BF)
    g["w_ff1"] = _mm(s["h2b"], dab, "tn", f"d_ff1_w_{tag}", tm=1024, tn=1024, tk=2048, out_dtype=BF, out_shards=True)
    dh2 = _mm(dab, w["w_ff1"], "nt", f"d_ff1_act_{tag}", tm=1024, tn=1024, tk=1024, b_shards=True)
    dx1, g["norm_mlp"] = _rms_bwd(dh2, s["x1"], w["norm_mlp"], dx2, f"d_rms_mlp_{tag}")
    dm = _mm(dx1, w["w_o"], "nt", f"d_out_act_{tag}", tm=1024, tn=1024, tk=1024)
    g["w_o"] = _mm(s["merged"], dx1, "tn", f"d_out_w_{tag}", tm=1024, tn=1024, tk=1024, out_dtype=BF)
    (dz, dyp, dyc, do0, do1, do2, c0, c1, c2, g["w_pool_up"], g["w_conv_out"], g["w_attn_up"],
     g["b_gate"]) = _merge_bwd(dm, s["yp"], s["yc"], s["o3"], s["lse3"], s["z"], w["b_gate"], w["w_pool_up"],
                               w["w_conv_out"], w["w_attn_up"], f"d_merge_{tag}")
    behind = mid(g) if mid is not None else None
    dq, dk, dv = [], [], []
    dgq = dgk = None
    for gi, d in enumerate(ATTN_DILATIONS):
        dzq, dzk, dzv, pq, pk = _attn_bwd(s["z"], s["qkv"], (do0, do1, do2)[gi], (c0, c1, c2)[gi], s["lse3"][gi],
                                          w["qk_gain"], gi, d, f"d_attn{gi}_{tag}", after=behind)
        dq.append(dzq)
        dk.append(dzk)
        dv.append(dzv)
        dgq = pq if dgq is None else dgq + pq
        dgk = pk if dgk is None else dgk + pk
    g["q_gain"] = dgq[:, :HEAD_DIM] + dgq[:, HEAD_DIM:]
    g["k_gain"] = dgk[:, :HEAD_DIM] + dgk[:, HEAD_DIM:]
    for off, pieces in ((OFF_Q, dq), (OFF_K, dk), (OFF_V, dv)):
        for gi, piece in enumerate(pieces):
            dz = lax.dynamic_update_slice(dz, piece, (0, off + 256 * gi))
    dz, g["pool_mix"], g["pool_scale"], g["conv_w"] = _poolconv_bwd(
        s["z"], dyp, dyc, w["pool_mix"], w["pool_scale"], w["conv_w"], dz, f"d_poolconv_{tag}")
    g["w_in"] = _mm(s["hb"], dz, "tn", f"d_in_w_{tag}", tm=512, tn=3712, tk=1024, out_dtype=BF)
    dh = _mm(dz, w["w_in"], "nn", f"d_in_act_{tag}", tm=1024, tn=1024, tk=3712,
             after=tail(g) if tail is not None else None)
    dx, g["norm_mix"] = _rms_bwd(dh, s["x"], w["norm_mix"], dx1, f"d_rms_mix_{tag}")
    return dx, g


def _position():
    x, y, c = lax.axis_index("x"), lax.axis_index("y"), lax.axis_index("c")
    chips = [(1 - x, y), (x, 1 - y), (1 - x, 1 - y)]
    return x, y, c, 2 * x + y, chips, [2 * cx + cy for cx, cy in chips]


def _remote(src, dst, ssem, rsem, dev):
    return pltpu.make_async_remote_copy(src_ref=src, dst_ref=dst, send_sem=ssem, recv_sem=rsem, device_id=dev,
                                        device_id_type=MESH_ID)


def _halves(a):
    return a.reshape(a.shape[0], 2, a.shape[1] // 2, a.shape[2])


SEM = pl.BlockSpec(memory_space=pltpu.SEMAPHORE)
TOKEN = jax.ShapeDtypeStruct((8, LANES), F32)
TOKEN_SPEC = pl.BlockSpec(memory_space=pltpu.VMEM)


def _split_params():
    return pltpu.CompilerParams(has_side_effects=pltpu.SideEffectType.DATAFLOW_SIDE_EFFECTING)


def _gather_start(bufs, name, after):
    n = len(bufs)
    views = [_halves(b) for b in bufs]

    def body(*refs):
        first_sem = n + 1
        ssem, rsem = refs[first_sem:first_sem + ns], refs[first_sem + ns:first_sem + 2 * ns]
        outs, token = refs[first_sem + 2 * ns:first_sem + 2 * ns + n], refs[first_sem + 2 * ns + n]
        x, y, c, q, chips, qs = _position()
        for k in range(n):
            mine = outs[k].at[q, c]
            for j, chip in enumerate(chips):
                _remote(mine, mine, ssem[3 * k + j], rsem[3 * k + j], (chip[0], chip[1], c)).start()
        token[...] = jnp.zeros_like(token)

    ns = 3 * n
    outs = _pallas_call(
        body, name=name, in_specs=[ANY] * (n + 1), out_specs=[SEM] * (2 * ns) + [ANY] * n + [TOKEN_SPEC],
        out_shape=[pltpu.SemaphoreType.DMA(())] * (2 * ns) + [jax.ShapeDtypeStruct(v.shape, v.dtype) for v in views]
        + [TOKEN],
        input_output_aliases={k: k + 2 * ns for k in range(n)}, compiler_params=_split_params(),
    )(*views, after)
    return list(outs[:ns]), list(outs[ns:2 * ns]), list(outs[2 * ns:2 * ns + n]), outs[2 * ns + n]


def _gather_finish(ssem, rsem, views, after, name_wait, name_forward, shapes):
    n = len(views)
    ns = len(ssem)

    def wait_body(*refs):
        ssem_ref, rsem_ref = refs[n:n + ns], refs[n + ns:n + 2 * ns]
        outs = refs[n + 2 * ns + 1:]
        x, y, c, q, chips, qs = _position()
        for k in range(n):
            for j, chip in enumerate(chips):
                cp = _remote(outs[k].at[q, c], outs[k].at[qs[j], c], ssem_ref[3 * k + j], rsem_ref[3 * k + j],
                             (chip[0], chip[1], c))
                cp.wait_send()
                cp.wait_recv()

    landed = _pallas_call(
        wait_body, name=name_wait, in_specs=[ANY] * n + [SEM] * (2 * ns) + [ANY], out_specs=[ANY] * n,
        out_shape=[jax.ShapeDtypeStruct(v.shape, v.dtype) for v in views],
        input_output_aliases={k: k for k in range(n)}, compiler_params=_split_params(),
    )(*views, *ssem, *rsem, after)

    def forward_body(*refs):
        outs = refs[n:2 * n]
        fssem, frsem = refs[2 * n:]
        x, y, c, q, chips, qs = _position()
        sib = (x, y, 1 - c)
        sent = []
        for k in range(n):
            for j in range(3):
                slot = outs[k].at[qs[j], c]
                cp = _remote(slot, slot, fssem.at[k, j], frsem.at[k, j], sib)
                cp.start()
                sent.append(cp)
        for k in range(n):
            for j in range(3):
                slot = outs[k].at[qs[j], 1 - c]
                _remote(slot, slot, fssem.at[k, j], frsem.at[k, j], sib).wait_recv()
        for cp in sent:
            cp.wait_send()

    outs = _pallas_call(
        forward_body, name=name_forward, in_specs=[ANY] * n, out_specs=[ANY] * n,
        out_shape=[jax.ShapeDtypeStruct(v.shape, v.dtype) for v in views],
        input_output_aliases={k: k for k in range(n)}, scratch_shapes=[pltpu.SemaphoreType.DMA((n, 3))] * 2,
    )(*landed)
    return [o.reshape(s) for o, s in zip(outs, shapes)]


def _chip_exchange_start(parts, name):
    n = len(parts)

    def body(*refs):
        ssem, rsem = refs[n:n + ns], refs[n + ns:n + 2 * ns]
        base = n + 2 * ns
        srcs, outs, token = refs[base:base + n], refs[base + n:base + 2 * n], refs[base + 2 * n]
        x, y, c, q, chips, qs = _position()
        for k in range(n):
            for j, chip in enumerate(chips):
                _remote(srcs[k].at[qs[j]], outs[k].at[j], ssem[3 * k + j], rsem[3 * k + j],
                        (chip[0], chip[1], c)).start()
        token[...] = jnp.zeros_like(token)

    ns = 3 * n
    outs = _pallas_call(
        body, name=name, in_specs=[ANY] * n, out_specs=[SEM] * (2 * ns) + [ANY] * (2 * n) + [TOKEN_SPEC],
        out_shape=[pltpu.SemaphoreType.DMA(())] * (2 * ns) + [jax.ShapeDtypeStruct(a.shape, a.dtype) for a in parts]
        + [jax.ShapeDtypeStruct((3,) + a.shape[1:], a.dtype) for a in parts] + [TOKEN],
        input_output_aliases={k: k + 2 * ns for k in range(n)}, compiler_params=_split_params(),
    )(*parts)
    b = 2 * ns
    return list(outs[:ns]), list(outs[ns:b]), list(outs[b:b + n]), list(outs[b + n:b + 2 * n]), outs[b + 2 * n]


def _chip_exchange_wait(ssem, rsem, parts, landing, after, name):
    n = len(parts)
    ns = len(ssem)

    def body(*refs):
        ssem_ref, rsem_ref = refs[2 * n:2 * n + ns], refs[2 * n + ns:2 * n + 2 * ns]
        base = 2 * n + 2 * ns + 1
        srcs, outs = refs[base:base + n], refs[base + n:]
        x, y, c, q, chips, qs = _position()
        for k in range(n):
            for j, chip in enumerate(chips):
                cp = _remote(srcs[k].at[qs[j]], outs[k].at[j], ssem_ref[3 * k + j], rsem_ref[3 * k + j],
                             (chip[0], chip[1], c))
                cp.wait_send()
                cp.wait_recv()

    outs = _pallas_call(
        body, name=name, in_specs=[ANY] * (2 * n) + [SEM] * (2 * ns) + [ANY], out_specs=[ANY] * (2 * n),
        out_shape=[jax.ShapeDtypeStruct(a.shape, a.dtype) for a in list(parts) + list(landing)],
        input_output_aliases={k: k for k in range(2 * n)}, compiler_params=_split_params(),
    )(*parts, *landing, *ssem, *rsem, after)
    return list(outs[:n]), list(outs[n:])


def _pair_swap(views, name):
    n = len(views)

    def body(*refs):
        ins, outs = refs[:n], refs[n:2 * n]
        ssem, rsem = refs[2 * n:]
        x, y, c, _, _, _ = _position()
        cps = [_remote(ins[k].at[pl.ds(0, N_CHIPS), 1 - c], outs[k], ssem.at[k], rsem.at[k], (x, y, 1 - c))
               for k in range(n)]
        for cp in cps:
            cp.start()
        for cp in cps:
            cp.wait()

    return _pallas_call(
        body, name=name, in_specs=[ANY] * n, out_specs=[ANY] * n,
        out_shape=[jax.ShapeDtypeStruct((v.shape[0],) + v.shape[2:], v.dtype) for v in views],
        scratch_shapes=[pltpu.SemaphoreType.DMA((n,))] * 2,
    )(*views)


def _chip_exchange(parts, name):
    n = len(parts)

    def body(*refs):
        ins, outs = refs[:n], refs[n:2 * n]
        ssem, rsem = refs[2 * n:]
        x, y, c, q, chips, qs = _position()
        cps = []
        for k in range(n):
            for j, chip in enumerate(chips):
                cp = _remote(ins[k].at[qs[j]], outs[k].at[j], ssem.at[k, j], rsem.at[k, j], (chip[0], chip[1], c))
                cp.start()
                cps.append(cp)
        for cp in cps:
            cp.wait_recv()
        for cp in cps:
            cp.wait_send()

    return _pallas_call(
        body, name=name, in_specs=[ANY] * n, out_specs=[ANY] * n,
        out_shape=[jax.ShapeDtypeStruct((3,) + a.shape[1:], a.dtype) for a in parts],
        scratch_shapes=[pltpu.SemaphoreType.DMA((n, 3))] * 2,
    )(*parts)


def _pair_send(arrays, name):
    n = len(arrays)

    def body(*refs):
        ins, outs = refs[:n], refs[n:2 * n]
        ssem, rsem = refs[2 * n:]
        x, y, c, _, _, _ = _position()
        cps = [_remote(ins[k], outs[k], ssem.at[k], rsem.at[k], (x, y, 1 - c)) for k in range(n)]
        for cp in cps:
            cp.start()
        for cp in cps:
            cp.wait()

    return _pallas_call(
        body, name=name, in_specs=[ANY] * n, out_specs=[ANY] * n,
        out_shape=[jax.ShapeDtypeStruct(a.shape, a.dtype) for a in arrays],
        scratch_shapes=[pltpu.SemaphoreType.DMA((n,))] * 2,
    )(*arrays)


def _all_to_all_small(part):
    P = part.shape[0]

    def body(in_ref, out_ref, lsem, ssem, rsem):
        x, y, c = lax.axis_index("x"), lax.axis_index("y"), lax.axis_index("c")
        me = 4 * x + 2 * y + c
        flips = [(fx, fy, fc) for fx in (0, 1) for fy in (0, 1) for fc in (0, 1)][1:]
        peers = [((x + fx) % 2, (y + fy) % 2, (c + fc) % 2) for fx, fy, fc in flips]
        loc = pltpu.make_async_copy(in_ref, out_ref.at[me], lsem)
        loc.start()
        cps = [_remote(in_ref, out_ref.at[me], ssem.at[j], rsem.at[j], peer) for j, peer in enumerate(peers)]
        for cp in cps:
            cp.start()
        for j, (px, py, pc) in enumerate(peers):
            _remote(in_ref, out_ref.at[4 * px + 2 * py + pc], ssem.at[j], rsem.at[j], peers[j]).wait_recv()
        for cp in cps:
            cp.wait_send()
        loc.wait()

    return _pallas_call(
        body, name="small_exchange", in_specs=[ANY], out_specs=ANY,
        out_shape=jax.ShapeDtypeStruct((8, P, LANES), F32),
        scratch_shapes=[pltpu.SemaphoreType.DMA(())] + [pltpu.SemaphoreType.DMA((7,))] * 2,
    )(part)


def _small_peers():
    x, y, c = lax.axis_index("x"), lax.axis_index("y"), lax.axis_index("c")
    flips = [(fx, fy, fc) for fx in (0, 1) for fy in (0, 1) for fc in (0, 1)][1:]
    peers = [((x + fx) % 2, (y + fy) % 2, (c + fc) % 2) for fx, fy, fc in flips]
    return 4 * x + 2 * y + c, peers


def _all_to_all_small_start(part, name):
    P = part.shape[0]
    me = 4 * lax.axis_index("x") + 2 * lax.axis_index("y") + lax.axis_index("c")
    landing = lax.dynamic_update_slice(jnp.zeros((8, P, LANES), F32), part[None], (me, 0, 0))

    def body(*refs):
        sems, src, land, token = refs[2:16], refs[16], refs[17], refs[18]
        me_, peers = _small_peers()
        for j, peer in enumerate(peers):
            _remote(src, land.at[me_], sems[j], sems[7 + j], peer).start()
        token[...] = jnp.zeros_like(token)

    outs = _pallas_call(
        body, name=name, in_specs=[ANY, ANY], out_specs=[SEM] * 14 + [ANY, ANY, TOKEN_SPEC],
        out_shape=[pltpu.SemaphoreType.DMA(())] * 14 + [jax.ShapeDtypeStruct(part.shape, F32),
                                                       jax.ShapeDtypeStruct((8, P, LANES), F32), TOKEN],
        input_output_aliases={0: 14, 1: 15}, compiler_params=_split_params(),
    )(part, landing)
    return list(outs[:7]), list(outs[7:14]), outs[14], outs[15], outs[16]


def _all_to_all_small_wait(ssem, rsem, part, landing, after, name):
    def body(*refs):
        sems, src, land = refs[2:16], refs[17], refs[18]
        _, peers = _small_peers()
        for j, (px, py, pc) in enumerate(peers):
            cp = _remote(src, land.at[4 * px + 2 * py + pc], sems[j], sems[7 + j], peers[j])
            cp.wait_send()
            cp.wait_recv()

    return _pallas_call(
        body, name=name, in_specs=[ANY, ANY] + [SEM] * 14 + [ANY], out_specs=[ANY, ANY],
        out_shape=[jax.ShapeDtypeStruct(part.shape, F32), jax.ShapeDtypeStruct(landing.shape, F32)],
        input_output_aliases={0: 0, 1: 1}, compiler_params=_split_params(),
    )(part, landing, *ssem, *rsem, after)[1]


def _row_tile(rows, width, n_arrays):
    t = rows
    while t % 2 == 0 and t > 8 and 2 * n_arrays * t * width * 4 > VMEM_LIMIT // 2:
        t //= 2
    return t


def _chip():
    return 2 * lax.axis_index("x") + lax.axis_index("y")


def _core():
    return lax.axis_index("c")


def _cast_place(w3, layer, name):
    _, r, c = w3.shape
    tr = _row_tile(r, c, 2)

    def body(w_ref, o_ref):
        o_ref[...] = w_ref[...].astype(BF)

    return _pallas_call(
        body, name=name, grid=(r // tr,), in_specs=[pl.BlockSpec((None, tr, c), lambda i: (layer, i, 0))],
        out_specs=pl.BlockSpec((None, tr, c), lambda i: (_chip(), i, 0)),
        out_shape=jax.ShapeDtypeStruct((N_CHIPS, r, c), BF), compiler_params=_params(("parallel",)),
    )(w3)


def _pair_sum(view, recv, name):
    _, _, hr, c = view.shape
    tr = _row_tile(hr, c, 3)

    def body(g_ref, r_ref, o_ref):
        o_ref[...] = (g_ref[...].astype(F32) + r_ref[...].astype(F32)).astype(BF)

    blk = pl.BlockSpec((None, tr, c), lambda p, i: (p, i, 0))
    return _pallas_call(
        body, name=name, grid=(N_CHIPS, hr // tr),
        in_specs=[pl.BlockSpec((None, None, tr, c), lambda p, i: (p, _core(), i, 0)), blk], out_specs=blk,
        out_shape=jax.ShapeDtypeStruct(recv.shape, BF), compiler_params=_params(("parallel", "parallel")),
    )(view, recv)


def _chip_sum(parts, recv, name):
    _, hr, c = parts.shape
    tr = _row_tile(hr, c, 6)

    def body(p_ref, r_ref, o_ref):
        acc = p_ref[...].astype(F32)
        for j in range(3):
            acc = acc + r_ref[j].astype(F32)
        o_ref[...] = acc

    return _pallas_call(
        body, name=name, grid=(hr // tr,),
        in_specs=[pl.BlockSpec((None, tr, c), lambda i: (_chip(), i, 0)), pl.BlockSpec((3, tr, c), lambda i: (0, i, 0))],
        out_specs=pl.BlockSpec((tr, c), lambda i: (i, 0)),
        out_shape=jax.ShapeDtypeStruct((hr, c), F32), compiler_params=_params(("parallel",)),
    )(parts, recv)


def _sum_slices(a, name):
    n, rows, width = a.shape
    tr = _row_tile(rows, width, n + 1)

    def body(a_ref, o_ref):
        acc = a_ref[0].astype(F32)
        for i in range(1, n):
            acc = acc + a_ref[i].astype(F32)
        o_ref[...] = acc

    return _pallas_call(
        body, name=name, grid=(rows // tr,), in_specs=[pl.BlockSpec((n, tr, width), lambda i: (0, i, 0))],
        out_specs=pl.BlockSpec((tr, width), lambda i: (i, 0)), out_shape=jax.ShapeDtypeStruct((rows, width), F32),
        compiler_params=_params(("parallel",)),
    )(a)


def _adamw_update(w, g, m, v):
    nm = ADAM_B1 * m + (1.0 - ADAM_B1) * g
    nv = ADAM_B2 * v + (1.0 - ADAM_B2) * (g * g)
    m_hat = nm / (1.0 - ADAM_B1 ** ADAM_STEP)
    v_hat = nv / (1.0 - ADAM_B2 ** ADAM_STEP)
    return -ADAM_LR * (m_hat / (jnp.sqrt(v_hat) + ADAM_EPS) + ADAM_WD * w), nm, nv


def _adamw(w, g, m, v, name):
    rows, width = w.shape
    tr = _row_tile(rows, width, 7)

    def body(w_ref, g_ref, m_ref, v_ref, d_ref, nm_ref, nv_ref):
        d_ref[...], nm_ref[...], nv_ref[...] = _adamw_update(w_ref[...], g_ref[...], m_ref[...], v_ref[...])

    blk = pl.BlockSpec((tr, width), lambda i: (i, 0))
    return _pallas_call(
        body, name=name, grid=(rows // tr,), in_specs=[blk] * 4, out_specs=[blk] * 3,
        out_shape=[jax.ShapeDtypeStruct((rows, width), F32)] * 3, compiler_params=_params(("parallel",)),
    )(w, g, m, v)


def _adamw_halves(w3, m3, v3, mine, other, name):
    depth, r, c = w3.shape
    assert depth == 2
    hr = r // 2
    tr = _row_tile(hr, c, 11)
    sources = ((0, True, mine[0]), (0, False, other[0]), (1, True, mine[1]), (1, False, other[1]))

    def active(l, h, layer, own):
        mine_half = h == _core()
        return (l == layer) & (mine_half if own else jnp.logical_not(mine_half))

    def body(w_ref, m_ref, v_ref, *rest):
        g_refs, (go_ref, d_ref, nm_ref, nv_ref) = rest[:4], rest[4:]
        l, h = pl.program_id(0), pl.program_id(1)
        for (layer, own, _), g_ref in zip(sources, g_refs):
            @pl.when(active(l, h, layer, own))
            def _():
                gv = g_ref[...]
                go_ref[...] = gv
                d_ref[...], nm_ref[...], nv_ref[...] = _adamw_update(w_ref[...], gv, m_ref[...], v_ref[...])

    def gspec(layer, own):
        return pl.BlockSpec((tr, c), lambda l, h, i: (jnp.where(active(l, h, layer, own), i, 0), 0))

    blk = pl.BlockSpec((None, None, tr, c), lambda l, h, i: (l, h, i, 0))
    view = lambda a: a.reshape(depth, 2, hr, c)
    outs = _pallas_call(
        body, name=name, grid=(depth, 2, hr // tr),
        in_specs=[blk] * 3 + [gspec(layer, own) for layer, own, _ in sources], out_specs=[blk] * 4,
        out_shape=[jax.ShapeDtypeStruct((depth, 2, hr, c), F32)] * 4,
        compiler_params=_params(("parallel", "parallel", "parallel")),
    )(view(w3), view(m3), view(v3), *[s[2] for s in sources])
    return [o.reshape(w3.shape) for o in outs]


BIG = ("w_in", "w_pool_up", "w_conv_out", "w_attn_up", "w_o", "w_ff1", "w_ff2")
SMALL = ("norm_mix", "b_gate", "pool_mix", "pool_scale", "conv_w", "q_gain", "k_gain", "norm_mlp")
ORDER = ("norm_mix", "w_in", "b_gate", "pool_mix", "pool_scale", "conv_w", "q_gain", "k_gain", "w_pool_up",
         "w_conv_out", "w_attn_up", "w_o", "norm_mlp", "w_ff1", "w_ff2")
COLUMN_SHARDED = ("w_pool_up", "w_conv_out", "w_attn_up", "w_ff1")


def _matrix_weights(gathered):
    w = {}
    for name, g4 in gathered.items():
        if name in COLUMN_SHARDED:
            w[name] = g4
        else:
            w[name] = g4.reshape(N_CHIPS * g4.shape[1], g4.shape[2])
    return w


def _small_weights(l, small):
    w = {}
    w["norm_mix"] = small["norm_mix"][l][None]
    w["norm_mlp"] = small["norm_mlp"][l][None]
    w["b_gate"] = small["b_gate"][l][None]
    w["pool_mix"] = small["pool_mix"][l].astype(BF)
    w["pool_scale"] = small["pool_scale"][l][None]
    w["conv_w"] = jnp.pad(small["conv_w_full"][l], ((0, 5), (0, 0)))
    w["qk_gain"] = jnp.pad(jnp.stack([jnp.tile(small["q_gain"][l], 2), jnp.tile(small["k_gain"][l], 2)]), ((0, 6), (0, 0)))
    return w


def _to_chip_major(name, g):
    if name == "w_in":
        return g.T.reshape(N_CHIPS, g.shape[1] // N_CHIPS, g.shape[0])
    if name in COLUMN_SHARDED:
        return g
    return g.reshape(N_CHIPS, g.shape[0] // N_CHIPS, g.shape[1])


def _pad8(a):
    a = a.reshape(-1)
    return jnp.pad(a, (0, (-a.size) % (8 * LANES))).reshape(-1, LANES)


def kernel(x, norm_mix, w_in, b_gate, pool_mix, pool_scale, conv_w, q_gain, k_gain, w_pool_up, w_conv_out, w_attn_up, w_o, norm_mlp, w_ff1, w_ff2, loss_target, m_norm_mix, m_w_in, m_b_gate, m_pool_mix, m_pool_scale, m_conv_w, m_q_gain, m_k_gain, m_w_pool_up, m_w_conv_out, m_w_attn_up, m_w_o, m_norm_mlp, m_w_ff1, m_w_ff2, v_norm_mix, v_w_in, v_b_gate, v_pool_mix, v_pool_scale, v_conv_w, v_q_gain, v_k_gain, v_w_pool_up, v_w_conv_out, v_w_attn_up, v_w_o, v_norm_mlp, v_w_ff1, v_w_ff2):
    weights = dict(norm_mix=norm_mix, w_in=w_in, b_gate=b_gate, pool_mix=pool_mix, pool_scale=pool_scale, conv_w=conv_w,
                   q_gain=q_gain, k_gain=k_gain, w_pool_up=w_pool_up, w_conv_out=w_conv_out, w_attn_up=w_attn_up,
                   w_o=w_o, norm_mlp=norm_mlp, w_ff1=w_ff1, w_ff2=w_ff2)
    moms = dict(norm_mix=m_norm_mix, w_in=m_w_in, b_gate=m_b_gate, pool_mix=m_pool_mix, pool_scale=m_pool_scale,
                conv_w=m_conv_w, q_gain=m_q_gain, k_gain=m_k_gain, w_pool_up=m_w_pool_up, w_conv_out=m_w_conv_out,
                w_attn_up=m_w_attn_up, w_o=m_w_o, norm_mlp=m_norm_mlp, w_ff1=m_w_ff1, w_ff2=m_w_ff2)
    vels = dict(norm_mix=v_norm_mix, w_in=v_w_in, b_gate=v_b_gate, pool_mix=v_pool_mix, pool_scale=v_pool_scale,
                conv_w=v_conv_w, q_gain=v_q_gain, k_gain=v_k_gain, w_pool_up=v_w_pool_up, w_conv_out=v_w_conv_out,
                w_attn_up=v_w_attn_up, w_o=v_w_o, norm_mlp=v_norm_mlp, w_ff1=v_w_ff1, w_ff2=v_w_ff2)
    depth = norm_mix.shape[0]
    q = 2 * lax.axis_index("x") + lax.axis_index("y")
    for group in (weights, moms, vels):
        group["w_in"] = jnp.swapaxes(group["w_in"], 1, 2)

    assert depth == 2, "the second layer's gather hides behind the first layer's forward, and likewise backward"
    first, rest = BIG[:1], BIG[1:]
    cw_all = _all_to_all_small(_pad8(conv_w))
    bufs = [{n: _cast_place(weights[n], 0, f"cast_{n}_l0") for n in first}]
    a_ssem, a_rsem, a_views, a_token = _gather_start([bufs[0][n] for n in first], "gather_start_l0_in", cw_all)
    bufs[0].update({n: _cast_place(weights[n], 0, f"cast_{n}_l0") for n in rest})
    bufs += [{n: _cast_place(weights[n], l, f"cast_{n}_l{l}") for n in BIG} for l in range(1, depth)]
    b_ssem, b_rsem, b_views, b_token = _gather_start([bufs[0][n] for n in rest], "gather_start_l0_rest", a_token)
    g_ssem, g_rsem, g_views, g_token = _gather_start([bufs[1][n] for n in BIG], "gather_start_l1", b_token)
    conv_w_full = jnp.concatenate(
        [cw_all[2 * p].reshape(-1)[:conv_w.size].reshape(conv_w.shape) for p in range(N_CHIPS)], axis=-1)
    small = dict(weights)
    small["conv_w_full"] = conv_w_full

    def soon_weights(t):
        got = _gather_finish(a_ssem, a_rsem, a_views, t, "gather_wait_l0_in", "gather_forward_l0_in",
                             [bufs[0][n].shape for n in first])
        return _matrix_weights(dict(zip(first, got)))

    def late_weights(t):
        got = _gather_finish(b_ssem, b_rsem, b_views, t, "gather_wait_l0_rest", "gather_forward_l0_rest",
                             [bufs[0][n].shape for n in rest])
        return _matrix_weights(dict(zip(rest, got)))

    wl, saved = [None] * depth, [None] * depth
    h, saved[0], wl[0] = _layer_fwd(x[0], _small_weights(0, small), "l0", after=g_token, soon=soon_weights,
                                    late=late_weights)
    got = _gather_finish(g_ssem, g_rsem, g_views, h, "gather_wait_l1", "gather_forward_l1",
                         [bufs[1][n].shape for n in BIG])
    h, saved[1], wl[1] = _layer_fwd(h, dict(_small_weights(1, small), **_matrix_weights(dict(zip(BIG, got)))), "l1")
    dh, loss_row = _loss_grad(h, loss_target[0], "loss")

    def pair_stage(names, g, tag):
        views = [_halves(_to_chip_major(n, g[n])) for n in names]
        from_sibling = _pair_swap(views, f"grad_pair_swap_{tag}")
        return [_pair_sum(views[k], from_sibling[k], f"pair_sum_{n}_{tag}") for k, n in enumerate(names)]

    mine, other = [{}, {}], [{}, {}]

    def finish(names, l, started, after, tag):
        ssem, rsem, parts, landing, _ = started
        parts, arrived = _chip_exchange_wait(ssem, rsem, parts, landing, after, f"grad_chip_exchange_wait_{tag}")
        got = [_chip_sum(parts[k], arrived[k], f"chip_sum_{n}_{tag}") for k, n in enumerate(names)]
        mine[l].update(zip(names, got))
        other[l].update(zip(names, _pair_send(got, f"grad_pair_send_{tag}")))

    def small_pieces(g):
        return [_pad8(g[n][:3] if n == "conv_w" else g[n]) for n in SMALL]

    def start_small(l):
        return _all_to_all_small_start(jnp.concatenate(small_pieces(grads[l]), axis=0), f"small_grad_exchange_start_l{l}")

    grads, early, small = [None] * depth, {}, [None] * depth
    dh, grads[1] = _layer_bwd(dh, wl[1], saved[1], "l1")
    second = _chip_exchange_start(pair_stage(BIG, grads[1], "l1"), "grad_chip_exchange_start_l1")
    small[1] = start_small(1)

    def start_rest(g):
        early["rest"] = _chip_exchange_start(pair_stage(rest, g, "l0_rest"), "grad_chip_exchange_start_l0_rest")
        return early["rest"][4]

    def start_last(g):
        early["in"] = _chip_exchange_start(pair_stage(first, g, "l0_in"), "grad_chip_exchange_start_l0_in")
        return early["in"][4]

    dh, grads[0] = _layer_bwd(dh, wl[0], saved[0], "l0", after=[second[4], small[1][4]], mid=start_rest,
                              tail=start_last)
    small[0] = start_small(0)
    finish(BIG, 1, second, dh, "l1")
    finish(rest, 0, early["rest"], dh, "l0_rest")
    loss = lax.psum(loss_row[0, 0], ("x", "y", "c"))
    full = {}

    deltas, new_m, new_v = {}, {}, {}

    def update_matrix(n):
        full[n], deltas[n], new_m[n], new_v[n] = _adamw_halves(
            weights[n], moms[n], vels[n], [mine[l][n] for l in range(depth)], [other[l][n] for l in range(depth)],
            f"adamw_{n}")

    for n in rest:
        update_matrix(n)
    finish(first, 0, early["in"], deltas[rest[-1]], "l0_in")
    for n in first:
        update_matrix(n)
    summed = []
    for l in range(depth):
        ssem, rsem, part, landing, _ = small[l]
        summed.append(_sum_slices(_all_to_all_small_wait(ssem, rsem, part, landing, deltas[first[-1]],
                                                         f"small_grad_exchange_wait_l{l}"), f"small_sum_l{l}"))
    row = 0
    for n, piece in zip(SMALL, small_pieces(grads[0])):
        size = (weights[n].size if n != "conv_w" else depth * 3 * 512) // depth
        flat = jnp.stack([s[row:row + piece.shape[0]].reshape(-1)[:size] for s in summed])
        row += piece.shape[0]
        if n == "conv_w":
            full[n] = lax.dynamic_slice_in_dim(flat.reshape(depth, 3, 512), q * conv_w.shape[2], conv_w.shape[2], axis=2)
        else:
            full[n] = flat.reshape(weights[n].shape)
    for n in SMALL:
        shape = weights[n].shape
        two_d = (-1, shape[-1]) if n not in ("conv_w", "q_gain", "k_gain") else (1, -1)
        d2, m2, v2 = _adamw(weights[n].reshape(two_d), full[n].reshape(two_d), moms[n].reshape(two_d),
                            vels[n].reshape(two_d), f"adamw_{n}")
        deltas[n], new_m[n], new_v[n] = d2.reshape(shape), m2.reshape(shape), v2.reshape(shape)
        full[n] = full[n].reshape(shape)
    for group in (full, deltas, new_m, new_v):
        group["w_in"] = jnp.swapaxes(group["w_in"], 1, 2)
    return (loss, dh[None], *[full[n] for n in ORDER], *[deltas[n] for n in ORDER], *[new_m[n] for n in ORDER],
            *[new_v[n] for n in ORDER])
```

```python
import functools

import jax
import jax.numpy as jnp
from jax import lax
from jax.experimental import pallas as pl
from jax.experimental.pallas import tpu as pltpu

F32 = jnp.float32
BF = jnp.bfloat16
MESH_ID = pl.DeviceIdType.MESH
ANY = pl.BlockSpec(memory_space=pl.ANY)

EPS = 1e-6
MASK_VALUE = -1e30
POOL_WINDOWS = (2, 4, 8, 16)
ATTN_DILATIONS = (1, 4, 16)
ATTN_BLOCK = 128
HEAD_DIM = 64
OFF_Q, OFF_K, OFF_V, OFF_GATE = 2048, 2816, 3584, 4352
N_CHIPS = 4
ADAM_LR, ADAM_B1, ADAM_B2, ADAM_EPS, ADAM_WD, ADAM_STEP = 0.001, 0.9, 0.999, 1e-08, 0.01, 10

VMEM_LIMIT = 48 * 1024 * 1024
LANES = 128

_DIMS = {"nn": (((1,), (0,)), ((), ())), "nt": (((1,), (1,)), ((), ())), "tn": (((0,), (0,)), ((), ()))}


def _params(sem):
    return pltpu.CompilerParams(dimension_semantics=sem, vmem_limit_bytes=VMEM_LIMIT)


def _pallas_call(body, **kw):
    def in_hbm(s):
        pin = isinstance(s, jax.ShapeDtypeStruct) and s is not TOKEN and jnp.issubdtype(s.dtype, jnp.floating)
        return pltpu.HBM(s.shape, s.dtype) if pin else s

    out_shape = kw.pop("out_shape")
    kw["out_shape"] = [in_hbm(s) for s in out_shape] if isinstance(out_shape, (list, tuple)) else in_hbm(out_shape)
    call = pl.pallas_call(body, **kw)

    def run(*args):
        pinned = [pltpu.with_memory_space_constraint(a, pltpu.HBM)
                  if hasattr(a, "dtype") and jnp.issubdtype(a.dtype, jnp.floating) else a for a in args]
        return call(*pinned)

    return run


def _dot(a, b, mode="nn"):
    return lax.dot_general(a, b, _DIMS[mode], preferred_element_type=F32)


def _mm(a, b, mode, name, *, tm, tn, tk, out_dtype=F32, res=None, aux=None, epi=None, n_outer=False,
        b_shards=False, out_shards=False, after=None):
    if mode == "tn":
        K, M = a.shape
    else:
        M, K = a.shape
    if b_shards:
        if mode == "nn":
            assert b.shape[1] == K
            N = b.shape[2] * N_CHIPS
        else:
            assert mode == "nt"
            N = b.shape[1]
            assert b.shape[2] * N_CHIPS == K
    else:
        N = b.shape[0] if mode == "nt" else b.shape[1]
    tm, tn, tk = min(tm, M), min(tn, N), min(tk, K)
    assert M % tm == 0 and N % tn == 0 and K % tk == 0
    nk = K // tk
    if n_outer:
        grid = (N // tn, M // tm, nk)
        ij = lambda p, q_: (q_, p)
    else:
        grid = (M // tm, N // tn, nk)
        ij = lambda p, q_: (p, q_)

    def amap(p, q_, k):
        i, j = ij(p, q_)
        return (k, i) if mode == "tn" else (i, k)

    a_spec = pl.BlockSpec((tk, tm) if mode == "tn" else (tm, tk), amap)
    if b_shards:
        if mode == "nn":
            per = (N // N_CHIPS) // tn
            assert per >= 1 and (N // N_CHIPS) % tn == 0

            def bmap(p, q_, k):
                i, j = ij(p, q_)
                return (j // per, k, j % per)

            b_spec = pl.BlockSpec((None, tk, tn), bmap)
        else:
            per = (K // N_CHIPS) // tk
            assert per >= 1 and (K // N_CHIPS) % tk == 0

            def bmap(p, q_, k):
                i, j = ij(p, q_)
                return (k // per, j, k % per)

            b_spec = pl.BlockSpec((None, tn, tk), bmap)
    else:
        def bmap(p, q_, k):
            i, j = ij(p, q_)
            return (j, k) if mode == "nt" else (k, j)

        b_spec = pl.BlockSpec((tn, tk) if mode == "nt" else (tk, tn), bmap)

    def omap(p, q_, k):
        return ij(p, q_)

    o_spec = pl.BlockSpec((tm, tn), omap)
    if out_shards:
        per_o = (N // N_CHIPS) // tn
        assert per_o >= 1 and (N // N_CHIPS) % tn == 0

        def osmap(p, q_, k):
            i, j = ij(p, q_)
            return (j // per_o, i, j % per_o)

        out_spec0 = pl.BlockSpec((None, tm, tn), osmap)
        out_shape0 = jax.ShapeDtypeStruct((N_CHIPS, M, N // N_CHIPS), out_dtype)
    else:
        out_spec0 = o_spec
        out_shape0 = jax.ShapeDtypeStruct((M, N), out_dtype)

    in_specs = [a_spec, b_spec]
    args = [a, b]
    if res is not None:
        in_specs.append(o_spec)
        args.append(res)
    if aux is not None:
        in_specs.append(o_spec)
        args.append(aux)
    after = [] if after is None else list(after) if isinstance(after, (list, tuple)) else [after]
    in_specs += [ANY] * len(after)
    args += after
    out_specs = [out_spec0]
    out_shape = [out_shape0]
    n_out = len(out_shape)
    has_res, has_aux, n_after = res is not None, aux is not None, len(after)

    def body(*refs):
        a_ref, b_ref = refs[0], refs[1]
        pos = 2
        res_ref = aux_ref = None
        if has_res:
            res_ref = refs[pos]
            pos += 1
        if has_aux:
            aux_ref = refs[pos]
            pos += 1
        pos += n_after
        outs = refs[pos:pos + n_out]
        part = _dot(a_ref[...].astype(BF), b_ref[...].astype(BF), mode)

        def finish(acc):
            if res_ref is not None:
                acc = res_ref[...] + acc
            if epi == "relu2":
                r = jnp.maximum(acc, 0.0)
                outs[0][...] = (r * r).astype(out_dtype)
            elif epi == "drelu2":
                outs[0][...] = (acc * (2.0 * jnp.sqrt(aux_ref[...].astype(F32)))).astype(out_dtype)
            else:
                outs[0][...] = acc.astype(out_dtype)

        if nk == 1:
            finish(part)
        else:
            acc_ref = refs[pos + n_out]
            k = pl.program_id(2)

            @pl.when(k == 0)
            def _():
                acc_ref[...] = part

            @pl.when(k > 0)
            def _():
                acc_ref[...] += part

            @pl.when(k == nk - 1)
            def _():
                finish(acc_ref[...])

    scratch = [pltpu.VMEM((tm, tn), F32)] if nk > 1 else []
    out = _pallas_call(
        body, name=name, grid=grid, in_specs=in_specs, out_specs=out_specs, out_shape=out_shape,
        scratch_shapes=scratch, compiler_params=_params(("parallel", "parallel", "arbitrary")),
    )(*args)
    return out if n_out > 1 else out[0]


def _rms_fwd(x, gain, name, after=None):
    T, D = x.shape
    tm = min(512, T)

    def body(x_ref, g_ref, *rest):
        o_ref = rest[-1]
        xv = x_ref[...]
        r = lax.rsqrt(jnp.mean(xv * xv, axis=-1, keepdims=True) + EPS)
        o_ref[...] = ((xv * r) * g_ref[...]).astype(BF)

    extra = [] if after is None else list(after) if isinstance(after, (list, tuple)) else [after]
    return _pallas_call(
        body, name=name, grid=(T // tm,),
        in_specs=[pl.BlockSpec((tm, D), lambda i: (i, 0)), pl.BlockSpec((1, D), lambda i: (0, 0))] + [ANY] * len(extra),
        out_specs=pl.BlockSpec((tm, D), lambda i: (i, 0)), out_shape=jax.ShapeDtypeStruct((T, D), BF),
        compiler_params=_params(("parallel",)),
    )(x, gain, *extra)


def _rms_bwd(dh, x, gain, dres, name):
    T, D = x.shape
    tm = min(512, T)

    def body(dh_ref, x_ref, g_ref, dres_ref, dx_ref, dg_ref):
        xv = x_ref[...]
        r = lax.rsqrt(jnp.mean(xv * xv, axis=-1, keepdims=True) + EPS)
        xhat = xv * r
        dhv = dh_ref[...]
        dy = dhv * g_ref[...]
        dx_ref[...] = dres_ref[...] + r * (dy - xhat * jnp.mean(dy * xhat, axis=-1, keepdims=True))

        @pl.when(pl.program_id(0) == 0)
        def _():
            dg_ref[...] = jnp.zeros_like(dg_ref)

        dg_ref[...] += jnp.sum(dhv * xhat, axis=0, keepdims=True)

    row = pl.BlockSpec((tm, D), lambda i: (i, 0))
    vec = pl.BlockSpec((1, D), lambda i: (0, 0))
    return _pallas_call(
        body, name=name, grid=(T // tm,), in_specs=[row, row, vec, row], out_specs=[row, vec],
        out_shape=[jax.ShapeDtypeStruct((T, D), F32), jax.ShapeDtypeStruct((1, D), F32)],
        compiler_params=_params(("arbitrary",)),
    )(dh, x, gain, dres)


def _loss_grad(y, target, name):
    T, D = y.shape
    tm = min(512, T)

    def body(y_ref, t_ref, dy_ref, l_ref):
        e = y_ref[...] - t_ref[...]
        dy_ref[...] = e / float(D)

        @pl.when(pl.program_id(0) == 0)
        def _():
            l_ref[...] = jnp.zeros_like(l_ref)

        l_ref[...] += 0.5 * jnp.sum(jnp.mean(e * e, axis=-1, keepdims=True))

    row = pl.BlockSpec((tm, D), lambda i: (i, 0))
    return _pallas_call(
        body, name=name, grid=(T // tm,), in_specs=[row, row],
        out_specs=[row, pl.BlockSpec((1, LANES), lambda i: (0, 0))],
        out_shape=[jax.ShapeDtypeStruct((T, D), F32), jax.ShapeDtypeStruct((1, LANES), F32)],
        compiler_params=_params(("arbitrary",)),
    )(y, target)


POOL_HALO = 16
CONV_HALO = 8


def _causal_window_sum(v, w):
    s, sh = v, 1
    while sh < w:
        s = s + pltpu.roll(s, sh, 0)
        sh *= 2
    return s


def _anticausal_window_sum(v, w):
    n = v.shape[0]
    s, sh = v, 1
    while sh < w:
        s = s + pltpu.roll(s, n - sh, 0)
        sh *= 2
    return s


def _poolconv_fwd(z, pmix_b, pscale, convw, name):
    T = z.shape[0]
    R = min(512, T)
    PH, CH = R // POOL_HALO, R // CONV_HALO

    def body(u_ref, uh_ref, b_ref, c_ref, ch_ref, x_ref, xh_ref, mix_ref, sc_ref, cw_ref, yp_ref, yc_ref):
        i = pl.program_id(0)
        keep = (i > 0).astype(F32)
        row = i * R + lax.broadcasted_iota(jnp.int32, (R, 1), 0)
        w_all = jnp.concatenate([uh_ref[...] * keep, u_ref[...]], axis=0)
        for g, w in enumerate(POOL_WINDOWS):
            cols = slice(128 * g, 128 * (g + 1))
            wg = w_all[:, cols]
            s = _causal_window_sum(wg, w)[POOL_HALO:]
            cnt = jnp.minimum(row + 1, w).astype(F32)
            dgrp = s / cnt - wg[POOL_HALO:]
            y = _dot(dgrp.astype(BF), mix_ref[g]) * sc_ref[:, cols]
            yp_ref[:, cols] = y.astype(BF)
        uc = jnp.concatenate([ch_ref[...] * xh_ref[...] * keep, c_ref[...] * x_ref[...]], axis=0)
        yc = cw_ref[2:3, :] * uc + cw_ref[0:1, :] * pltpu.roll(uc, 2, 0) + cw_ref[1:2, :] * pltpu.roll(uc, 1, 0)
        yc_ref[...] = (b_ref[...] * yc[CONV_HALO:]).astype(BF)

    def main(cb):
        return pl.BlockSpec((R, 512), lambda i: (i, cb))

    def prev(cb, halo, per):
        return pl.BlockSpec((halo, 512), lambda i: (jnp.maximum(i * per - 1, 0), cb))

    full = lambda a: pl.BlockSpec(a.shape, lambda i: (0,) * a.ndim)
    return _pallas_call(
        body, name=name, grid=(T // R,),
        in_specs=[main(0), prev(0, POOL_HALO, PH), main(1), main(2), prev(2, CONV_HALO, CH), main(3),
                  prev(3, CONV_HALO, CH), full(pmix_b), full(pscale), full(convw)],
        out_specs=[pl.BlockSpec((R, 512), lambda i: (i, 0))] * 2,
        out_shape=[jax.ShapeDtypeStruct((T, 512), BF)] * 2,
        compiler_params=_params(("parallel",)),
    )(z, z, z, z, z, z, z, pmix_b, pscale, convw)


def _poolconv_bwd(z, dyp, dyc, pmix_b, pscale, convw, dz, name):
    T = z.shape[0]
    R = min(512, T)
    PH, CH = R // POOL_HALO, R // CONV_HALO
    nsteps = T // R

    def body(u_ref, uh_ref, b_ref, bn_ref, c_ref, ch_ref, x_ref, xh_ref, dyp_ref, dypn_ref, dyc_ref, dycn_ref,
             mix_ref, sc_ref, cw_ref, dz_in_ref, dz_ref, dmix_ref, dsc_ref, dcw_ref):
        i = pl.program_id(0)
        keep_prev = (i > 0).astype(F32)
        keep_next = (i < nsteps - 1).astype(F32)

        @pl.when(i == 0)
        def _():
            dmix_ref[...] = jnp.zeros_like(dmix_ref)
            dsc_ref[...] = jnp.zeros_like(dsc_ref)
            dcw_ref[...] = jnp.zeros_like(dcw_ref)

        row = i * R + lax.broadcasted_iota(jnp.int32, (R, 1), 0)
        row_ext = i * R + lax.broadcasted_iota(jnp.int32, (R + POOL_HALO, 1), 0)
        w_all = jnp.concatenate([uh_ref[...] * keep_prev, u_ref[...]], axis=0)
        dyp_ext = jnp.concatenate([dyp_ref[...], dypn_ref[...] * keep_next], axis=0)
        for g, w in enumerate(POOL_WINDOWS):
            cols = slice(128 * g, 128 * (g + 1))
            wg = w_all[:, cols]
            s = _causal_window_sum(wg, w)[POOL_HALO:]
            cnt = jnp.minimum(row + 1, w).astype(F32)
            dgrp = (s / cnt - wg[POOL_HALO:]).astype(BF)
            y_pre = _dot(dgrp, mix_ref[g])
            dsc_ref[:, cols] += jnp.sum(dyp_ref[:, cols] * y_pre, axis=0, keepdims=True)
            dyb = (dyp_ext[:, cols] * sc_ref[:, cols]).astype(BF)
            dmix_ref[cols, :] += _dot(dgrp, dyb[:R], "tn")
            dd = _dot(dyb, mix_ref[g], "nt")
            cnt_ext = jnp.minimum(row_ext + 1, w).astype(F32)
            e = _anticausal_window_sum(dd / cnt_ext, w)
            dz_ref[:, cols] = (e[:R] - dd[:R]).astype(BF)
        cw0, cw1, cw2 = cw_ref[0:1, :], cw_ref[1:2, :], cw_ref[2:3, :]
        uc = jnp.concatenate([ch_ref[...] * xh_ref[...] * keep_prev, c_ref[...] * x_ref[...]], axis=0)
        uc1 = pltpu.roll(uc, 1, 0)[CONV_HALO:]
        uc2 = pltpu.roll(uc, 2, 0)[CONV_HALO:]
        uc0 = uc[CONV_HALO:]
        yc = cw2 * uc0 + cw0 * uc2 + cw1 * uc1
        dycv = dyc_ref[...]
        dz_ref[:, 512:1024] = (dycv * yc).astype(BF)
        dv_ext = jnp.concatenate([dycv * b_ref[...], dycn_ref[...] * bn_ref[...] * keep_next], axis=0)
        n_ext = R + CONV_HALO
        duc = (cw2 * dv_ext + cw1 * pltpu.roll(dv_ext, n_ext - 1, 0) + cw0 * pltpu.roll(dv_ext, n_ext - 2, 0))[:R]
        dv = dv_ext[:R]
        dcw_ref[0:1, :] += jnp.sum(dv * uc2, axis=0, keepdims=True)
        dcw_ref[1:2, :] += jnp.sum(dv * uc1, axis=0, keepdims=True)
        dcw_ref[2:3, :] += jnp.sum(dv * uc0, axis=0, keepdims=True)
        dz_ref[:, 1024:1536] = (duc * x_ref[...]).astype(BF)
        dz_ref[:, 1536:2048] = (duc * c_ref[...]).astype(BF)

    def main(cb):
        return pl.BlockSpec((R, 512), lambda i: (i, cb))

    def prev(cb, halo, per):
        return pl.BlockSpec((halo, 512), lambda i: (jnp.maximum(i * per - 1, 0), cb))

    def nxt(cb, halo, per):
        return pl.BlockSpec((halo, 512), lambda i: (jnp.minimum((i + 1) * per, T // halo - 1), cb))

    full = lambda a: pl.BlockSpec(a.shape, lambda i: (0,) * a.ndim)
    return _pallas_call(
        body, name=name, grid=(nsteps,),
        in_specs=[main(0), prev(0, POOL_HALO, PH), main(1), nxt(1, CONV_HALO, CH), main(2), prev(2, CONV_HALO, CH),
                  main(3), prev(3, CONV_HALO, CH), main(0), nxt(0, POOL_HALO, PH), main(0), nxt(0, CONV_HALO, CH),
                  full(pmix_b), full(pscale), full(convw), ANY],
        out_specs=[pl.BlockSpec((R, 2048), lambda i: (i, 0)), pl.BlockSpec((512, 128), lambda i: (0, 0)),
                   pl.BlockSpec((1, 512), lambda i: (0, 0)), pl.BlockSpec((8, 512), lambda i: (0, 0))],
        out_shape=[jax.ShapeDtypeStruct(dz.shape, BF), jax.ShapeDtypeStruct((512, 128), F32),
                   jax.ShapeDtypeStruct((1, 512), F32), jax.ShapeDtypeStruct((8, 512), F32)],
        input_output_aliases={15: 0}, compiler_params=_params(("arbitrary",)),
    )(z, z, z, z, z, z, z, z, dyp, dyp, dyc, dyc, pmix_b, pscale, convw, dz)


def _head_sums(v):
    row = lax.broadcasted_iota(jnp.int32, (LANES, LANES), 0) < HEAD_DIM
    col = lax.broadcasted_iota(jnp.int32, (LANES, LANES), 1) < HEAD_DIM
    same_head = jnp.where(jnp.logical_xor(row, col), 0.0, 1.0).astype(BF)
    hi = v.astype(BF)
    lo = (v - hi.astype(F32)).astype(BF)
    return _dot(hi, same_head) + _dot(lo, same_head)


def _head_norm(x, g2, ma):
    r = lax.rsqrt(_head_sums(x * x) / HEAD_DIM + EPS)
    return x * r, r


def _head_norm_bwd(dy, xhat, r, g2, ma):
    dxh = dy * g2
    return r * (dxh - xhat * (_head_sums(dxh * xhat) / HEAD_DIM))


def _head_col(tile, hm):
    return jnp.max(jnp.where(hm, tile, -jnp.inf), axis=-1, keepdims=True)


def _attn_masks(other_block_exists):
    lane = lax.broadcasted_iota(jnp.int32, (2 * ATTN_BLOCK, ATTN_BLOCK), 1)
    qi = lax.broadcasted_iota(jnp.int32, (2 * ATTN_BLOCK, ATTN_BLOCK), 0) & (ATTN_BLOCK - 1)
    never = (1 - other_block_exists.astype(jnp.int32)) * (2 * ATTN_BLOCK)
    return lane[:ATTN_BLOCK] < HEAD_DIM, lane <= qi, lane >= qi + never


def _stack_heads(x, ma):
    return jnp.concatenate([jnp.where(ma, x, 0.0), jnp.where(ma, 0.0, x)], axis=0)


def _unstack_heads(y, ma):
    return jnp.where(ma, y[:ATTN_BLOCK], y[ATTN_BLOCK:])


def _stack_cols(tile, ma):
    return jnp.concatenate([tile[:, 0:1], tile[:, HEAD_DIM:HEAD_DIM + 1]], axis=0)


QKV_TILES = (OFF_GATE - OFF_Q) // LANES
KIND_TILES = QKV_TILES // 3


def _qk_norm(z, gains, name):
    T = z.shape[0]
    tm = min(512, T)

    def body(x_ref, g_ref, o_ref):
        ma = lax.broadcasted_iota(jnp.int32, (tm, LANES), 1) < HEAD_DIM
        for tile in range(QKV_TILES):
            v = x_ref[:, LANES * tile:LANES * (tile + 1)]
            if tile < 2 * KIND_TILES:
                g = g_ref[0:1, :] if tile < KIND_TILES else g_ref[1:2, :]
                v = _head_norm(v, g, ma)[0] * g
            o_ref[tile] = v

    return _pallas_call(
        body, name=name, grid=(T // tm,),
        in_specs=[pl.BlockSpec((pl.Element(tm), pl.Element(OFF_GATE - OFF_Q)), lambda i: (i * tm, OFF_Q)),
                  pl.BlockSpec((8, LANES), lambda i: (0, 0))],
        out_specs=pl.BlockSpec((QKV_TILES, tm, LANES), lambda i: (0, i, 0)),
        out_shape=jax.ShapeDtypeStruct((QKV_TILES, T, LANES), F32), compiler_params=_params(("parallel",)),
    )(z, gains)


ATTN_STEP_ROWS = 1024
ATTN_UNROLL = 4


def _attn_geometry(T, d):
    sub = ATTN_BLOCK * d
    nb = T // sub
    m = max(1, min(nb, ATTN_STEP_ROWS // sub))
    assert T % sub == 0 and nb % m == 0
    return sub, nb, m


def _attn_rows(jj, r, sub, d):
    start = jj * sub + r
    if d == 1:
        return pl.ds(pl.multiple_of(start, ATTN_BLOCK), ATTN_BLOCK)
    return pl.ds(start, ATTN_BLOCK, stride=d)


def _pick(flag, a, b):
    return jnp.where(jnp.full(a.shape, flag.astype(jnp.int32)) > 0, a, b)


def _attn_fwd(qkv, g, d, name):
    T = qkv.shape[1]
    sub, nb, m = _attn_geometry(T, d)
    scale = HEAD_DIM ** -0.5

    def body(q_ref, kc_ref, kp_ref, vc_ref, vp_ref, o_ref, lse_ref):
        jb = pl.program_id(0)

        def step(s, carry):
            jj, r = s // d, s % d
            here, before = _attn_rows(jj, r, sub, d), _attn_rows(jnp.maximum(jj - 1, 0), r, sub, d)
            edge = _attn_rows(0, r, sub, d)
            first = jj == 0
            ma, mask_c, mask_p = _attn_masks(jb * m + jj > 0)
            qs = _stack_heads(q_ref[here, :], ma).astype(BF)
            kcb = kc_ref[here, :].astype(BF)
            kpb = _pick(first, kp_ref[edge, :], kc_ref[before, :]).astype(BF)
            vcb = vc_ref[here, :].astype(BF)
            vpb = _pick(first, vp_ref[edge, :], vc_ref[before, :]).astype(BF)
            s_c = jnp.where(mask_c, _dot(qs, kcb, "nt") * scale, MASK_VALUE)
            s_p = jnp.where(mask_p, _dot(qs, kpb, "nt") * scale, MASK_VALUE)
            mx = jnp.maximum(jnp.max(s_c, axis=-1, keepdims=True), jnp.max(s_p, axis=-1, keepdims=True))
            p_c = jnp.exp(s_c - mx)
            p_p = jnp.exp(s_p - mx)
            den = jnp.sum(p_c, axis=-1, keepdims=True) + jnp.sum(p_p, axis=-1, keepdims=True)
            o = (_dot(p_c.astype(BF), vcb) + _dot(p_p.astype(BF), vpb)) / den
            o_ref[here, :] = _unstack_heads(o, ma)
            lse_ref[here, :] = _unstack_heads(jnp.broadcast_to(mx + jnp.log(den), o.shape), ma)
            return carry

        lax.fori_loop(0, m * d, step, 0, unroll=ATTN_UNROLL)

    def cur(kind):
        return pl.BlockSpec((None, m * sub, LANES), lambda j, t: (KIND_TILES * kind + 2 * g + t, j, 0))

    def prv(kind):
        return pl.BlockSpec((None, sub, LANES), lambda j, t: (KIND_TILES * kind + 2 * g + t, jnp.maximum(j * m - 1, 0), 0))

    out = pl.BlockSpec((m * sub, LANES), lambda j, t: (j, t))
    return _pallas_call(
        body, name=name, grid=(nb // m, 2), in_specs=[cur(0), cur(1), prv(1), cur(2), prv(2)],
        out_specs=[out, out], out_shape=[jax.ShapeDtypeStruct((T, 256), F32)] * 2,
        compiler_params=_params(("parallel", "parallel")),
    )(qkv, qkv, qkv, qkv, qkv)


def _attn_bwd(z, qkv, do, c, lse, gains, g, d, name, after=None):
    T = z.shape[0]
    sub, nb, m = _attn_geometry(T, d)
    scale = HEAD_DIM ** -0.5
    extra = [] if after is None else [after]

    def body(qr_ref, kr_ref, vc_ref, vp_ref, qn_ref, qnn_ref, kn_ref, knp_ref, do_ref, don_ref, c_ref, cn_ref,
             lse_ref, lsen_ref, g_ref, *rest):
        dq_ref, dk_ref, dv_ref, dgq_ref, dgk_ref, sq_ref, sk_ref, sv_ref = rest[len(extra):]
        jb = pl.program_id(0)

        @pl.when((jb == 0) & (pl.program_id(1) == 0))
        def _():
            dgq_ref[...] = jnp.zeros_like(dgq_ref)
            dgk_ref[...] = jnp.zeros_like(dgk_ref)

        gq, gk = g_ref[0:1, :], g_ref[1:2, :]

        def step(s, carry):
            jj, r = s // d, s % d
            here, edge = _attn_rows(jj, r, sub, d), _attn_rows(0, r, sub, d)
            before = _attn_rows(jnp.maximum(jj - 1, 0), r, sub, d)
            behind = _attn_rows(jnp.minimum(jj + 1, m - 1), r, sub, d)
            first, last = jj == 0, jj == m - 1
            block = jb * m + jj
            ma, mask_c, mask_p = _attn_masks(block > 0)
            mask_n = _attn_masks(block < nb - 1)[2]
            qhat, rq = _head_norm(qr_ref[here, :], gq, ma)
            qn = qn_ref[here, :]
            qn_next = _pick(last, qnn_ref[edge, :], qn_ref[behind, :])
            khat, rk = _head_norm(kr_ref[here, :], gk, ma)
            kcb = kn_ref[here, :].astype(BF)
            kpb = _pick(first, knp_ref[edge, :], kn_ref[before, :]).astype(BF)
            vcb = vc_ref[here, :].astype(BF)
            vpb = _pick(first, vp_ref[edge, :], vc_ref[before, :]).astype(BF)
            do_t, don_t = do_ref[here, :], _pick(last, don_ref[edge, :], do_ref[behind, :])
            c_t, cn_t = c_ref[here, :], _pick(last, cn_ref[edge, :], c_ref[behind, :])
            lse_t, lsen_t = lse_ref[here, :], _pick(last, lsen_ref[edge, :], lse_ref[behind, :])
            qs, dos = _stack_heads(qn, ma).astype(BF), _stack_heads(do_t, ma).astype(BF)
            lse_s, c_s = _stack_cols(lse_t, ma), _stack_cols(c_t, ma)
            s_c = jnp.where(mask_c, _dot(qs, kcb, "nt") * scale, MASK_VALUE)
            s_p = jnp.where(mask_p, _dot(qs, kpb, "nt") * scale, MASK_VALUE)
            p_c = jnp.exp(s_c - lse_s)
            p_p = jnp.exp(s_p - lse_s)
            ds_c = ((p_c * (_dot(dos, vcb, "nt") + c_s)) * scale).astype(BF)
            ds_p = ((p_p * (_dot(dos, vpb, "nt") + c_s)) * scale).astype(BF)
            dq_t = _unstack_heads(_dot(ds_c, kcb) + _dot(ds_p, kpb), ma)
            qs_n, dos_n = _stack_heads(qn_next, ma).astype(BF), _stack_heads(don_t, ma).astype(BF)
            s_n = jnp.where(mask_n, _dot(qs_n, kcb, "nt") * scale, MASK_VALUE)
            p_n = jnp.exp(s_n - _stack_cols(lsen_t, ma))
            ds_n = ((p_n * (_dot(dos_n, vcb, "nt") + _stack_cols(cn_t, ma))) * scale).astype(BF)
            dv_t = _dot(p_c.astype(BF), dos, "tn") + _dot(p_n.astype(BF), dos_n, "tn")
            dk_t = _dot(ds_c, qs, "tn") + _dot(ds_n, qs_n, "tn")
            sq_ref[here, :] = _head_norm_bwd(dq_t, qhat, rq, gq, ma)
            sk_ref[here, :] = _head_norm_bwd(dk_t, khat, rk, gk, ma)
            sv_ref[here, :] = dv_t
            dgq_ref[...] += jnp.sum(dq_t * qhat, axis=0, keepdims=True)
            dgk_ref[...] += jnp.sum(dk_t * khat, axis=0, keepdims=True)
            return carry

        lax.fori_loop(0, m * d, step, 0, unroll=ATTN_UNROLL)
        dq_ref[...] = sq_ref[...].astype(BF)
        dk_ref[...] = sk_ref[...].astype(BF)
        dv_ref[...] = sv_ref[...].astype(BF)

    def raw(col0):
        return pl.BlockSpec((m * sub, LANES), lambda j, t: (j, col0 + 2 * g + t))

    def cur(kind):
        return pl.BlockSpec((None, m * sub, LANES), lambda j, t: (KIND_TILES * kind + 2 * g + t, j, 0))

    def prv(kind):
        return pl.BlockSpec((None, sub, LANES), lambda j, t: (KIND_TILES * kind + 2 * g + t, jnp.maximum(j * m - 1, 0), 0))

    def nxt(kind):
        return pl.BlockSpec((None, sub, LANES),
                            lambda j, t: (KIND_TILES * kind + 2 * g + t, jnp.minimum((j + 1) * m, nb - 1), 0))

    own = pl.BlockSpec((m * sub, LANES), lambda j, t: (j, t))
    own_next = pl.BlockSpec((sub, LANES), lambda j, t: (jnp.minimum((j + 1) * m, nb - 1), t))
    vec = pl.BlockSpec((1, LANES), lambda j, t: (0, 0))
    return _pallas_call(
        body, name=name, grid=(nb // m, 2),
        in_specs=[raw(OFF_Q // LANES), raw(OFF_K // LANES), cur(2), prv(2), cur(0), nxt(0), cur(1), prv(1), own, own_next,
                  own, own_next,
                  own, own_next, pl.BlockSpec((8, LANES), lambda j, t: (0, 0))] + [ANY] * len(extra),
        out_specs=[own, own, own, vec, vec],
        out_shape=[jax.ShapeDtypeStruct((T, 256), BF)] * 3 + [jax.ShapeDtypeStruct((1, LANES), F32)] * 2,
        scratch_shapes=[pltpu.VMEM((m * sub, LANES), F32)] * 3,
        compiler_params=_params(("arbitrary", "arbitrary")),
    )(z, z, qkv, qkv, qkv, qkv, qkv, qkv, do, do, c, c, lse, lse, gains, *extra)


MERGE_ROWS = 256
GATE_TILE = 256


def _group_mix(o_refs, lse_refs):
    lses = [r[...] for r in lse_refs]
    m = jnp.maximum(jnp.maximum(lses[0], lses[1]), lses[2])
    es = [jnp.exp(l - m) for l in lses]
    den = es[0] + es[1] + es[2]
    ws = [e / den for e in es]
    y = ws[0] * o_refs[0][...] + ws[1] * o_refs[1][...] + ws[2] * o_refs[2][...]
    return ws, y


def _sigmoid(v):
    return 1.0 / (1.0 + jnp.exp(-v))


def _merge_specs(T, z, bgate, gpu, gco, gau):
    tm = min(MERGE_ROWS, T)
    row = lambda w: pl.BlockSpec((tm, w), lambda i: (i, 0))
    gate0 = OFF_GATE // GATE_TILE
    gates = [pl.BlockSpec((tm, GATE_TILE), functools.partial(lambda i, cb: (i, cb), cb=gate0 + n))
             for n in range(3 * N_CHIPS)]
    full = lambda a: pl.BlockSpec(a.shape, lambda i: (0,) * a.ndim)
    specs = [row(512), row(512)] + [row(256)] * 6 + gates + [full(bgate), full(gpu), full(gco), full(gau)]
    return tm, row, specs


def _merge_fwd(yp, yc, o3, lse3, z, bgate, gpu, gco, gau, name):
    T = yp.shape[0]
    tm, row, specs = _merge_specs(T, z, bgate, gpu, gco, gau)

    def body(*refs):
        yp_ref, yc_ref = refs[0], refs[1]
        o_refs, lse_refs = refs[2:5], refs[5:8]
        zg = refs[8:20]
        b_ref, gpu_ref, gco_ref, gau_ref, out_ref = refs[20:25]
        yab = _group_mix(o_refs, lse_refs)[1].astype(BF)
        ys = (yp_ref[...], yc_ref[...], yab)
        ups = (gpu_ref, gco_ref, gau_ref)
        for n in range(N_CHIPS):
            acc = None
            for b in range(3):
                gcol = slice(1024 * b + GATE_TILE * n, 1024 * b + GATE_TILE * (n + 1))
                gate = _sigmoid(zg[N_CHIPS * b + n][...] + b_ref[:, gcol])
                term = gate * _dot(ys[b], ups[b][n])
                acc = term if acc is None else acc + term
            out_ref[:, GATE_TILE * n:GATE_TILE * (n + 1)] = acc.astype(BF)

    return _pallas_call(
        body, name=name, grid=(T // tm,), in_specs=specs, out_specs=row(1024),
        out_shape=jax.ShapeDtypeStruct((T, 1024), BF), compiler_params=_params(("parallel",)),
    )(yp, yc, *o3, *lse3, *([z] * 12), bgate, gpu, gco, gau)


def _merge_bwd(dm, yp, yc, o3, lse3, z, bgate, gpu, gco, gau, name):
    T = yp.shape[0]
    tm, row, specs = _merge_specs(T, z, bgate, gpu, gco, gau)
    nsteps = T // tm

    def body(*refs):
        dm_ref, yp_ref, yc_ref = refs[0:3]
        o_refs, lse_refs = refs[3:6], refs[6:9]
        zg = refs[9:21]
        b_ref, gpu_ref, gco_ref, gau_ref = refs[21:25]
        dzg_ref, dyp_ref, dyc_ref = refs[25:28]
        do_refs, c_refs = refs[28:31], refs[31:34]
        dgpu_ref, dgco_ref, dgau_ref, dbg_ref = refs[34:38]
        accs = refs[38:41]
        i = pl.program_id(0)

        @pl.when(i == 0)
        def _():
            for a in accs:
                a[...] = jnp.zeros_like(a)
            dbg_ref[...] = jnp.zeros_like(dbg_ref)

        ws, y = _group_mix(o_refs, lse_refs)
        ys = (yp_ref[...], yc_ref[...], y.astype(BF))
        ups = (gpu_ref, gco_ref, gau_ref)
        dys = [None, None, None]
        for n in range(N_CHIPS):
            dmn = dm_ref[:, GATE_TILE * n:GATE_TILE * (n + 1)]
            for b in range(3):
                gcol = slice(1024 * b + GATE_TILE * n, 1024 * b + GATE_TILE * (n + 1))
                gate = _sigmoid(zg[N_CHIPS * b + n][...] + b_ref[:, gcol])
                up = _dot(ys[b], ups[b][n])
                dzg = (dmn * up) * (gate * (1.0 - gate))
                dzg_ref[:, gcol] = dzg.astype(BF)
                dbg_ref[:, gcol] += jnp.sum(dzg, axis=0, keepdims=True)
                dup = (dmn * gate).astype(BF)
                accs[b][n] += _dot(ys[b], dup, "tn")
                dyb = _dot(dup, ups[b][n], "nt")
                dys[b] = dyb if dys[b] is None else dys[b] + dyb
        dyp_ref[...] = dys[0]
        dyc_ref[...] = dys[1]
        dya = dys[2]
        lane = lax.broadcasted_iota(jnp.int32, dya.shape, 1) // HEAD_DIM
        pr = dya * y
        rho = jnp.zeros_like(pr)
        for h in range(256 // HEAD_DIM):
            hm = lane == h
            rho = jnp.where(hm, jnp.sum(jnp.where(hm, pr, 0.0), axis=-1, keepdims=True), rho)
        for g in range(3):
            do_refs[g][...] = ws[g] * dya
            c_refs[g][...] = -(ws[g] * rho)

        @pl.when(i == nsteps - 1)
        def _():
            dgpu_ref[...] = accs[0][...].astype(BF)
            dgco_ref[...] = accs[1][...].astype(BF)
            dgau_ref[...] = accs[2][...].astype(BF)

    full = lambda a: pl.BlockSpec(a.shape, lambda i: (0,) * a.ndim)
    dz_gate = pl.BlockSpec((pl.Element(tm), pl.Element(3072)), lambda i: (i * tm, OFF_GATE))
    out_specs = ([dz_gate, row(512), row(512)] + [row(256)] * 6 + [full(gpu), full(gco), full(gau)]
                 + [pl.BlockSpec((1, 3072), lambda i: (0, 0))])
    out_shape = ([jax.ShapeDtypeStruct(z.shape, BF)] + [jax.ShapeDtypeStruct((T, 512), F32)] * 2
                 + [jax.ShapeDtypeStruct((T, 256), F32)] * 6
                 + [jax.ShapeDtypeStruct(g.shape, BF) for g in (gpu, gco, gau)]
                 + [jax.ShapeDtypeStruct((1, 3072), F32)])
    return _pallas_call(
        body, name=name, grid=(nsteps,), in_specs=[row(1024)] + specs, out_specs=out_specs, out_shape=out_shape,
        scratch_shapes=[pltpu.VMEM(g.shape, F32) for g in (gpu, gco, gau)],
        compiler_params=_params(("arbitrary",)),
    )(dm, yp, yc, *o3, *lse3, *([z] * 12), bgate, gpu, gco, gau)


def _layer_fwd(x, w, tag, after=None, soon=None, late=None):
    hb = _rms_fwd(x, w["norm_mix"], f"rms_mix_{tag}", after=after)
    if soon is not None:
        w = dict(w, **soon(hb))
    z = _mm(hb, w["w_in"], "nt", f"in_proj_{tag}", tm=512, tn=3712, tk=1024, n_outer=True)
    yp, yc = _poolconv_fwd(z, w["pool_mix"], w["pool_scale"], w["conv_w"], f"poolconv_{tag}")
    qkv = _qk_norm(z, w["qk_gain"], f"qk_norm_{tag}")
    o3, lse3 = [], []
    for g, d in enumerate(ATTN_DILATIONS):
        o, lse = _attn_fwd(qkv, g, d, f"attn{g}_{tag}")
        o3.append(o)
        lse3.append(lse)
    if late is not None:
        w = dict(w, **late(lse3[-1]))
    merged = _merge_fwd(yp, yc, o3, lse3, z, w["b_gate"], w["w_pool_up"], w["w_conv_out"], w["w_attn_up"],
                        f"merge_{tag}")
    x1 = _mm(merged, w["w_o"], "nn", f"out_proj_{tag}", tm=1024, tn=1024, tk=1024, res=x)
    h2b = _rms_fwd(x1, w["norm_mlp"], f"rms_mlp_{tag}")
    rb = _mm(h2b, w["w_ff1"], "nn", f"ff1_{tag}", tm=1024, tn=1024, tk=1024, out_dtype=BF, epi="relu2", n_outer=True,
             b_shards=True)
    x2 = _mm(rb, w["w_ff2"], "nn", f"ff2_{tag}", tm=512, tn=1024, tk=4096, res=x1)
    saved = dict(x=x, hb=hb, z=z, yp=yp, yc=yc, qkv=qkv, o3=o3, lse3=lse3, merged=merged, x1=x1, h2b=h2b, rb=rb)
    return x2, saved, w


def _layer_bwd(dx2, w, s, tag, after=None, mid=None, tail=None):
    g = {}
    dab = _mm(dx2, w["w_ff2"], "nt", f"d_ff2_act_{tag}", tm=1024, tn=1024, tk=1024, out_dtype=BF, aux=s["rb"],
              epi="drelu2", after=after)
    g["w_ff2"] = _mm(s["rb"], dx2, "tn", f"d_ff2_w_{tag}", tm=1024, tn=1024, tk=2048, out_dtype=BF)
    g["w_ff1"] = _mm(s["h2b"], dab, "tn", f"d_ff1_w_{tag}", tm=1024, tn=1024, tk=2048, out_dtype=BF, out_shards=True)
    dh2 = _mm(dab, w["w_ff1"], "nt", f"d_ff1_act_{tag}", tm=1024, tn=1024, tk=1024, b_shards=True)
    dx1, g["norm_mlp"] = _rms_bwd(dh2, s["x1"], w["norm_mlp"], dx2, f"d_rms_mlp_{tag}")
    dm = _mm(dx1, w["w_o"], "nt", f"d_out_act_{tag}", tm=1024, tn=1024, tk=1024)
    g["w_o"] = _mm(s["merged"], dx1, "tn", f"d_out_w_{tag}", tm=1024, tn=1024, tk=1024, out_dtype=BF)
    (dz, dyp, dyc, do0, do1, do2, c0, c1, c2, g["w_pool_up"], g["w_conv_out"], g["w_attn_up"],
     g["b_gate"]) = _merge_bwd(dm, s["yp"], s["yc"], s["o3"], s["lse3"], s["z"], w["b_gate"], w["w_pool_up"],
                               w["w_conv_out"], w["w_attn_up"], f"d_merge_{tag}")
    behind = mid(g) if mid is not None else None
    dq, dk, dv = [], [], []
    dgq = dgk = None
    for gi, d in enumerate(ATTN_DILATIONS):
        dzq, dzk, dzv, pq, pk = _attn_bwd(s["z"], s["qkv"], (do0, do1, do2)[gi], (c0, c1, c2)[gi], s["lse3"][gi],
                                          w["qk_gain"], gi, d, f"d_attn{gi}_{tag}", after=behind)
        dq.append(dzq)
        dk.append(dzk)
        dv.append(dzv)
        dgq = pq if dgq is None else dgq + pq
        dgk = pk if dgk is None else dgk + pk
    g["q_gain"] = dgq[:, :HEAD_DIM] + dgq[:, HEAD_DIM:]
    g["k_gain"] = dgk[:, :HEAD_DIM] + dgk[:, HEAD_DIM:]
    for off, pieces in ((OFF_Q, dq), (OFF_K, dk), (OFF_V, dv)):
        for gi, piece in enumerate(pieces):
            dz = lax.dynamic_update_slice(dz, piece, (0, off + 256 * gi))
    dz, g["pool_mix"], g["pool_scale"], g["conv_w"] = _poolconv_bwd(
        s["z"], dyp, dyc, w["pool_mix"], w["pool_scale"], w["conv_w"], dz, f"d_poolconv_{tag}")
    g["w_in"] = _mm(s["hb"], dz, "tn", f"d_in_w_{tag}", tm=512, tn=3712, tk=1024, out_dtype=BF)
    dh = _mm(dz, w["w_in"], "nn", f"d_in_act_{tag}", tm=1024, tn=1024, tk=3712,
             after=tail(g) if tail is not None else None)
    dx, g["norm_mix"] = _rms_bwd(dh, s["x"], w["norm_mix"], dx1, f"d_rms_mix_{tag}")
    return dx, g


def _position():
    x, y, c = lax.axis_index("x"), lax.axis_index("y"), lax.axis_index("c")
    chips = [(1 - x, y), (x, 1 - y), (1 - x, 1 - y)]
    return x, y, c, 2 * x + y, chips, [2 * cx + cy for cx, cy in chips]


def _remote(src, dst, ssem, rsem, dev):
    return pltpu.make_async_remote_copy(src_ref=src, dst_ref=dst, send_sem=ssem, recv_sem=rsem, device_id=dev,
                                        device_id_type=MESH_ID)


def _halves(a):
    return a.reshape(a.shape[0], 2, a.shape[1] // 2, a.shape[2])


SEM = pl.BlockSpec(memory_space=pltpu.SEMAPHORE)
TOKEN = jax.ShapeDtypeStruct((8, LANES), F32)
TOKEN_SPEC = pl.BlockSpec(memory_space=pltpu.VMEM)


def _split_params():
    return pltpu.CompilerParams(has_side_effects=pltpu.SideEffectType.DATAFLOW_SIDE_EFFECTING)


def _gather_start(bufs, name, after):
    n = len(bufs)
    views = [_halves(b) for b in bufs]

    def body(*refs):
        first_sem = n + 1
        ssem, rsem = refs[first_sem:first_sem + ns], refs[first_sem + ns:first_sem + 2 * ns]
        outs, token = refs[first_sem + 2 * ns:first_sem + 2 * ns + n], refs[first_sem + 2 * ns + n]
        x, y, c, q, chips, qs = _position()
        for k in range(n):
            mine = outs[k].at[q, c]
            for j, chip in enumerate(chips):
                _remote(mine, mine, ssem[3 * k + j], rsem[3 * k + j], (chip[0], chip[1], c)).start()
        token[...] = jnp.zeros_like(token)

    ns = 3 * n
    outs = _pallas_call(
        body, name=name, in_specs=[ANY] * (n + 1), out_specs=[SEM] * (2 * ns) + [ANY] * n + [TOKEN_SPEC],
        out_shape=[pltpu.SemaphoreType.DMA(())] * (2 * ns) + [jax.ShapeDtypeStruct(v.shape, v.dtype) for v in views]
        + [TOKEN],
        input_output_aliases={k: k + 2 * ns for k in range(n)}, compiler_params=_split_params(),
    )(*views, after)
    return list(outs[:ns]), list(outs[ns:2 * ns]), list(outs[2 * ns:2 * ns + n]), outs[2 * ns + n]


def _gather_finish(ssem, rsem, views, after, name_wait, name_forward, shapes):
    n = len(views)
    ns = len(ssem)

    def wait_body(*refs):
        ssem_ref, rsem_ref = refs[n:n + ns], refs[n + ns:n + 2 * ns]
        outs = refs[n + 2 * ns + 1:]
        x, y, c, q, chips, qs = _position()
        for k in range(n):
            for j, chip in enumerate(chips):
                cp = _remote(outs[k].at[q, c], outs[k].at[qs[j], c], ssem_ref[3 * k + j], rsem_ref[3 * k + j],
                             (chip[0], chip[1], c))
                cp.wait_send()
                cp.wait_recv()

    landed = _pallas_call(
        wait_body, name=name_wait, in_specs=[ANY] * n + [SEM] * (2 * ns) + [ANY], out_specs=[ANY] * n,
        out_shape=[jax.ShapeDtypeStruct(v.shape, v.dtype) for v in views],
        input_output_aliases={k: k for k in range(n)}, compiler_params=_split_params(),
    )(*views, *ssem, *rsem, after)

    def forward_body(*refs):
        outs = refs[n:2 * n]
        fssem, frsem = refs[2 * n:]
        x, y, c, q, chips, qs = _position()
        sib = (x, y, 1 - c)
        sent = []
        for k in range(n):
            for j in range(3):
                slot = outs[k].at[qs[j], c]
                cp = _remote(slot, slot, fssem.at[k, j], frsem.at[k, j], sib)
                cp.start()
                sent.append(cp)
        for k in range(n):
            for j in range(3):
                slot = outs[k].at[qs[j], 1 - c]
                _remote(slot, slot, fssem.at[k, j], frsem.at[k, j], sib).wait_recv()
        for cp in sent:
            cp.wait_send()

    outs = _pallas_call(
        forward_body, name=name_forward, in_specs=[ANY] * n, out_specs=[ANY] * n,
        out_shape=[jax.ShapeDtypeStruct(v.shape, v.dtype) for v in views],
        input_output_aliases={k: k for k in range(n)}, scratch_shapes=[pltpu.SemaphoreType.DMA((n, 3))] * 2,
    )(*landed)
    return [o.reshape(s) for o, s in zip(outs, shapes)]


def _chip_exchange_start(parts, name):
    n = len(parts)

    def body(*refs):
        ssem, rsem = refs[n:n + ns], refs[n + ns:n + 2 * ns]
        base = n + 2 * ns
        srcs, outs, token = refs[base:base + n], refs[base + n:base + 2 * n], refs[base + 2 * n]
        x, y, c, q, chips, qs = _position()
        for k in range(n):
            for j, chip in enumerate(chips):
                _remote(srcs[k].at[qs[j]], outs[k].at[j], ssem[3 * k + j], rsem[3 * k + j],
                        (chip[0], chip[1], c)).start()
        token[...] = jnp.zeros_like(token)

    ns = 3 * n
    outs = _pallas_call(
        body, name=name, in_specs=[ANY] * n, out_specs=[SEM] * (2 * ns) + [ANY] * (2 * n) + [TOKEN_SPEC],
        out_shape=[pltpu.SemaphoreType.DMA(())] * (2 * ns) + [jax.ShapeDtypeStruct(a.shape, a.dtype) for a in parts]
        + [jax.ShapeDtypeStruct((3,) + a.shape[1:], a.dtype) for a in parts] + [TOKEN],
        input_output_aliases={k: k + 2 * ns for k in range(n)}, compiler_params=_split_params(),
    )(*parts)
    b = 2 * ns
    return list(outs[:ns]), list(outs[ns:b]), list(outs[b:b + n]), list(outs[b + n:b + 2 * n]), outs[b + 2 * n]


def _chip_exchange_wait(ssem, rsem, parts, landing, after, name):
    n = len(parts)
    ns = len(ssem)

    def body(*refs):
        ssem_ref, rsem_ref = refs[2 * n:2 * n + ns], refs[2 * n + ns:2 * n + 2 * ns]
        base = 2 * n + 2 * ns + 1
        srcs, outs = refs[base:base + n], refs[base + n:]
        x, y, c, q, chips, qs = _position()
        for k in range(n):
            for j, chip in enumerate(chips):
                cp = _remote(srcs[k].at[qs[j]], outs[k].at[j], ssem_ref[3 * k + j], rsem_ref[3 * k + j],
                             (chip[0], chip[1], c))
                cp.wait_send()
                cp.wait_recv()

    outs = _pallas_call(
        body, name=name, in_specs=[ANY] * (2 * n) + [SEM] * (2 * ns) + [ANY], out_specs=[ANY] * (2 * n),
        out_shape=[jax.ShapeDtypeStruct(a.shape, a.dtype) for a in list(parts) + list(landing)],
        input_output_aliases={k: k for k in range(2 * n)}, compiler_params=_split_params(),
    )(*parts, *landing, *ssem, *rsem, after)
    return list(outs[:n]), list(outs[n:])


def _pair_swap(views, name):
    n = len(views)

    def body(*refs):
        ins, outs = refs[:n], refs[n:2 * n]
        ssem, rsem = refs[2 * n:]
        x, y, c, _, _, _ = _position()
        cps = [_remote(ins[k].at[pl.ds(0, N_CHIPS), 1 - c], outs[k], ssem.at[k], rsem.at[k], (x, y, 1 - c))
               for k in range(n)]
        for cp in cps:
            cp.start()
        for cp in cps:
            cp.wait()

    return _pallas_call(
        body, name=name, in_specs=[ANY] * n, out_specs=[ANY] * n,
        out_shape=[jax.ShapeDtypeStruct((v.shape[0],) + v.shape[2:], v.dtype) for v in views],
        scratch_shapes=[pltpu.SemaphoreType.DMA((n,))] * 2,
    )(*views)


def _chip_exchange(parts, name):
    n = len(parts)

    def body(*refs):
        ins, outs = refs[:n], refs[n:2 * n]
        ssem, rsem = refs[2 * n:]
        x, y, c, q, chips, qs = _position()
        cps = []
        for k in range(n):
            for j, chip in enumerate(chips):
                cp = _remote(ins[k].at[qs[j]], outs[k].at[j], ssem.at[k, j], rsem.at[k, j], (chip[0], chip[1], c))
                cp.start()
                cps.append(cp)
        for cp in cps:
            cp.wait_recv()
        for cp in cps:
            cp.wait_send()

    return _pallas_call(
        body, name=name, in_specs=[ANY] * n, out_specs=[ANY] * n,
        out_shape=[jax.ShapeDtypeStruct((3,) + a.shape[1:], a.dtype) for a in parts],
        scratch_shapes=[pltpu.SemaphoreType.DMA((n, 3))] * 2,
    )(*parts)


def _pair_send(arrays, name):
    n = len(arrays)

    def body(*refs):
        ins, outs = refs[:n], refs[n:2 * n]
        ssem, rsem = refs[2 * n:]
        x, y, c, _, _, _ = _position()
        cps = [_remote(ins[k], outs[k], ssem.at[k], rsem.at[k], (x, y, 1 - c)) for k in range(n)]
        for cp in cps:
            cp.start()
        for cp in cps:
            cp.wait()

    return _pallas_call(
        body, name=name, in_specs=[ANY] * n, out_specs=[ANY] * n,
        out_shape=[jax.ShapeDtypeStruct(a.shape, a.dtype) for a in arrays],
        scratch_shapes=[pltpu.SemaphoreType.DMA((n,))] * 2,
    )(*arrays)


def _all_to_all_small(part):
    P = part.shape[0]

    def body(in_ref, out_ref, lsem, ssem, rsem):
        x, y, c = lax.axis_index("x"), lax.axis_index("y"), lax.axis_index("c")
        me = 4 * x + 2 * y + c
        flips = [(fx, fy, fc) for fx in (0, 1) for fy in (0, 1) for fc in (0, 1)][1:]
        peers = [((x + fx) % 2, (y + fy) % 2, (c + fc) % 2) for fx, fy, fc in flips]
        loc = pltpu.make_async_copy(in_ref, out_ref.at[me], lsem)
        loc.start()
        cps = [_remote(in_ref, out_ref.at[me], ssem.at[j], rsem.at[j], peer) for j, peer in enumerate(peers)]
        for cp in cps:
            cp.start()
        for j, (px, py, pc) in enumerate(peers):
            _remote(in_ref, out_ref.at[4 * px + 2 * py + pc], ssem.at[j], rsem.at[j], peers[j]).wait_recv()
        for cp in cps:
            cp.wait_send()
        loc.wait()

    return _pallas_call(
        body, name="small_exchange", in_specs=[ANY], out_specs=ANY,
        out_shape=jax.ShapeDtypeStruct((8, P, LANES), F32),
        scratch_shapes=[pltpu.SemaphoreType.DMA(())] + [pltpu.SemaphoreType.DMA((7,))] * 2,
    )(part)


def _small_peers():
    x, y, c = lax.axis_index("x"), lax.axis_index("y"), lax.axis_index("c")
    flips = [(fx, fy, fc) for fx in (0, 1) for fy in (0, 1) for fc in (0, 1)][1:]
    peers = [((x + fx) % 2, (y + fy) % 2, (c + fc) % 2) for fx, fy, fc in flips]
    return 4 * x + 2 * y + c, peers


def _all_to_all_small_start(part, name):
    P = part.shape[0]
    me = 4 * lax.axis_index("x") + 2 * lax.axis_index("y") + lax.axis_index("c")
    landing = lax.dynamic_update_slice(jnp.zeros((8, P, LANES), F32), part[None], (me, 0, 0))

    def body(*refs):
        sems, src, land, token = refs[2:16], refs[16], refs[17], refs[18]
        me_, peers = _small_peers()
        for j, peer in enumerate(peers):
            _remote(src, land.at[me_], sems[j], sems[7 + j], peer).start()
        token[...] = jnp.zeros_like(token)

    outs = _pallas_call(
        body, name=name, in_specs=[ANY, ANY], out_specs=[SEM] * 14 + [ANY, ANY, TOKEN_SPEC],
        out_shape=[pltpu.SemaphoreType.DMA(())] * 14 + [jax.ShapeDtypeStruct(part.shape, F32),
                                                       jax.ShapeDtypeStruct((8, P, LANES), F32), TOKEN],
        input_output_aliases={0: 14, 1: 15}, compiler_params=_split_params(),
    )(part, landing)
    return list(outs[:7]), list(outs[7:14]), outs[14], outs[15], outs[16]


def _all_to_all_small_wait(ssem, rsem, part, landing, after, name):
    def body(*refs):
        sems, src, land = refs[2:16], refs[17], refs[18]
        _, peers = _small_peers()
        for j, (px, py, pc) in enumerate(peers):
            cp = _remote(src, land.at[4 * px + 2 * py + pc], sems[j], sems[7 + j], peers[j])
            cp.wait_send()
            cp.wait_recv()

    return _pallas_call(
        body, name=name, in_specs=[ANY, ANY] + [SEM] * 14 + [ANY], out_specs=[ANY, ANY],
        out_shape=[jax.ShapeDtypeStruct(part.shape, F32), jax.ShapeDtypeStruct(landing.shape, F32)],
        input_output_aliases={0: 0, 1: 1}, compiler_params=_split_params(),
    )(part, landing, *ssem, *rsem, after)[1]


def _row_tile(rows, width, n_arrays):
    t = rows
    while t % 2 == 0 and t > 8 and 2 * n_arrays * t * width * 4 > VMEM_LIMIT // 2:
        t //= 2
    return t


def _chip():
    return 2 * lax.axis_index("x") + lax.axis_index("y")


def _core():
    return lax.axis_index("c")


def _cast_place(w3, layer, name):
    _, r, c = w3.shape
    tr = _row_tile(r, c, 2)

    def body(w_ref, o_ref):
        o_ref[...] = w_ref[...].astype(BF)

    return _pallas_call(
        body, name=name, grid=(r // tr,), in_specs=[pl.BlockSpec((None, tr, c), lambda i: (layer, i, 0))],
        out_specs=pl.BlockSpec((None, tr, c), lambda i: (_chip(), i, 0)),
        out_shape=jax.ShapeDtypeStruct((N_CHIPS, r, c), BF), compiler_params=_params(("parallel",)),
    )(w3)


def _pair_sum(view, recv, name):
    _, _, hr, c = view.shape
    tr = _row_tile(hr, c, 3)

    def body(g_ref, r_ref, o_ref):
        o_ref[...] = (g_ref[...].astype(F32) + r_ref[...].astype(F32)).astype(BF)

    blk = pl.BlockSpec((None, tr, c), lambda p, i: (p, i, 0))
    return _pallas_call(
        body, name=name, grid=(N_CHIPS, hr // tr),
        in_specs=[pl.BlockSpec((None, None, tr, c), lambda p, i: (p, _core(), i, 0)), blk], out_specs=blk,
        out_shape=jax.ShapeDtypeStruct(recv.shape, BF), compiler_params=_params(("parallel", "parallel")),
    )(view, recv)


def _chip_sum(parts, recv, name):
    _, hr, c = parts.shape
    tr = _row_tile(hr, c, 6)

    def body(p_ref, r_ref, o_ref):
        acc = p_ref[...].astype(F32)
        for j in range(3):
            acc = acc + r_ref[j].astype(F32)
        o_ref[...] = acc

    return _pallas_call(
        body, name=name, grid=(hr // tr,),
        in_specs=[pl.BlockSpec((None, tr, c), lambda i: (_chip(), i, 0)), pl.BlockSpec((3, tr, c), lambda i: (0, i, 0))],
        out_specs=pl.BlockSpec((tr, c), lambda i: (i, 0)),
        out_shape=jax.ShapeDtypeStruct((hr, c), F32), compiler_params=_params(("parallel",)),
    )(parts, recv)


def _sum_slices(a, name):
    n, rows, width = a.shape
    tr = _row_tile(rows, width, n + 1)

    def body(a_ref, o_ref):
        acc = a_ref[0].astype(F32)
        for i in range(1, n):
            acc = acc + a_ref[i].astype(F32)
        o_ref[...] = acc

    return _pallas_call(
        body, name=name, grid=(rows // tr,), in_specs=[pl.BlockSpec((n, tr, width), lambda i: (0, i, 0))],
        out_specs=pl.BlockSpec((tr, width), lambda i: (i, 0)), out_shape=jax.ShapeDtypeStruct((rows, width), F32),
        compiler_params=_params(("parallel",)),
    )(a)


def _adamw_update(w, g, m, v):
    nm = ADAM_B1 * m + (1.0 - ADAM_B1) * g
    nv = ADAM_B2 * v + (1.0 - ADAM_B2) * (g * g)
    m_hat = nm / (1.0 - ADAM_B1 ** ADAM_STEP)
    v_hat = nv / (1.0 - ADAM_B2 ** ADAM_STEP)
    return -ADAM_LR * (m_hat / (jnp.sqrt(v_hat) + ADAM_EPS) + ADAM_WD * w), nm, nv


def _adamw(w, g, m, v, name):
    rows, width = w.shape
    tr = _row_tile(rows, width, 7)

    def body(w_ref, g_ref, m_ref, v_ref, d_ref, nm_ref, nv_ref):
        d_ref[...], nm_ref[...], nv_ref[...] = _adamw_update(w_ref[...], g_ref[...], m_ref[...], v_ref[...])

    blk = pl.BlockSpec((tr, width), lambda i: (i, 0))
    return _pallas_call(
        body, name=name, grid=(rows // tr,), in_specs=[blk] * 4, out_specs=[blk] * 3,
        out_shape=[jax.ShapeDtypeStruct((rows, width), F32)] * 3, compiler_params=_params(("parallel",)),
    )(w, g, m, v)


def _adamw_halves(w3, m3, v3, mine, other, name):
    depth, r, c = w3.shape
    assert depth == 2
    hr = r // 2
    tr = _row_tile(hr, c, 11)
    sources = ((0, True, mine[0]), (0, False, other[0]), (1, True, mine[1]), (1, False, other[1]))

    def active(l, h, layer, own):
        mine_half = h == _core()
        return (l == layer) & (mine_half if own else jnp.logical_not(mine_half))

    def body(w_ref, m_ref, v_ref, *rest):
        g_refs, (go_ref, d_ref, nm_ref, nv_ref) = rest[:4], rest[4:]
        l, h = pl.program_id(0), pl.program_id(1)
        for (layer, own, _), g_ref in zip(sources, g_refs):
            @pl.when(active(l, h, layer, own))
            def _():
                gv = g_ref[...]
                go_ref[...] = gv
                d_ref[...], nm_ref[...], nv_ref[...] = _adamw_update(w_ref[...], gv, m_ref[...], v_ref[...])

    def gspec(layer, own):
        return pl.BlockSpec((tr, c), lambda l, h, i: (jnp.where(active(l, h, layer, own), i, 0), 0))

    blk = pl.BlockSpec((None, None, tr, c), lambda l, h, i: (l, h, i, 0))
    view = lambda a: a.reshape(depth, 2, hr, c)
    outs = _pallas_call(
        body, name=name, grid=(depth, 2, hr // tr),
        in_specs=[blk] * 3 + [gspec(layer, own) for layer, own, _ in sources], out_specs=[blk] * 4,
        out_shape=[jax.ShapeDtypeStruct((depth, 2, hr, c), F32)] * 4,
        compiler_params=_params(("parallel", "parallel", "parallel")),
    )(view(w3), view(m3), view(v3), *[s[2] for s in sources])
    return [o.reshape(w3.shape) for o in outs]


BIG = ("w_in", "w_pool_up", "w_conv_out", "w_attn_up", "w_o", "w_ff1", "w_ff2")
SMALL = ("norm_mix", "b_gate", "pool_mix", "pool_scale", "conv_w", "q_gain", "k_gain", "norm_mlp")
ORDER = ("norm_mix", "w_in", "b_gate", "pool_mix", "pool_scale", "conv_w", "q_gain", "k_gain", "w_pool_up",
         "w_conv_out", "w_attn_up", "w_o", "norm_mlp", "w_ff1", "w_ff2")
COLUMN_SHARDED = ("w_pool_up", "w_conv_out", "w_attn_up", "w_ff1")


def _matrix_weights(gathered):
    w = {}
    for name, g4 in gathered.items():
        if name in COLUMN_SHARDED:
            w[name] = g4
        else:
            w[name] = g4.reshape(N_CHIPS * g4.shape[1], g4.shape[2])
    return w


def _small_weights(l, small):
    w = {}
    w["norm_mix"] = small["norm_mix"][l][None]
    w["norm_mlp"] = small["norm_mlp"][l][None]
    w["b_gate"] = small["b_gate"][l][None]
    w["pool_mix"] = small["pool_mix"][l].astype(BF)
    w["pool_scale"] = small["pool_scale"][l][None]
    w["conv_w"] = jnp.pad(small["conv_w_full"][l], ((0, 5), (0, 0)))
    w["qk_gain"] = jnp.pad(jnp.stack([jnp.tile(small["q_gain"][l], 2), jnp.tile(small["k_gain"][l], 2)]), ((0, 6), (0, 0)))
    return w


def _to_chip_major(name, g):
    if name == "w_in":
        return g.T.reshape(N_CHIPS, g.shape[1] // N_CHIPS, g.shape[0])
    if name in COLUMN_SHARDED:
        return g
    return g.reshape(N_CHIPS, g.shape[0] // N_CHIPS, g.shape[1])


def _pad8(a):
    a = a.reshape(-1)
    return jnp.pad(a, (0, (-a.size) % (8 * LANES))).reshape(-1, LANES)


def kernel(x, norm_mix, w_in, b_gate, pool_mix, pool_scale, conv_w, q_gain, k_gain, w_pool_up, w_conv_out, w_attn_up, w_o, norm_mlp, w_ff1, w_ff2, loss_target, m_norm_mix, m_w_in, m_b_gate, m_pool_mix, m_pool_scale, m_conv_w, m_q_gain, m_k_gain, m_w_pool_up, m_w_conv_out, m_w_attn_up, m_w_o, m_norm_mlp, m_w_ff1, m_w_ff2, v_norm_mix, v_w_in, v_b_gate, v_pool_mix, v_pool_scale, v_conv_w, v_q_gain, v_k_gain, v_w_pool_up, v_w_conv_out, v_w_attn_up, v_w_o, v_norm_mlp, v_w_ff1, v_w_ff2):
    weights = dict(norm_mix=norm_mix, w_in=w_in, b_gate=b_gate, pool_mix=pool_mix, pool_scale=pool_scale, conv_w=conv_w,
                   q_gain=q_gain, k_gain=k_gain, w_pool_up=w_pool_up, w_conv_out=w_conv_out, w_attn_up=w_attn_up,
                   w_o=w_o, norm_mlp=norm_mlp, w_ff1=w_ff1, w_ff2=w_ff2)
    moms = dict(norm_mix=m_norm_mix, w_in=m_w_in, b_gate=m_b_gate, pool_mix=m_pool_mix, pool_scale=m_pool_scale,
                conv_w=m_conv_w, q_gain=m_q_gain, k_gain=m_k_gain, w_pool_up=m_w_pool_up, w_conv_out=m_w_conv_out,
                w_attn_up=m_w_attn_up, w_o=m_w_o, norm_mlp=m_norm_mlp, w_ff1=m_w_ff1, w_ff2=m_w_ff2)
    vels = dict(norm_mix=v_norm_mix, w_in=v_w_in, b_gate=v_b_gate, pool_mix=v_pool_mix, pool_scale=v_pool_scale,
                conv_w=v_conv_w, q_gain=v_q_gain, k_gain=v_k_gain, w_pool_up=v_w_pool_up, w_conv_out=v_w_conv_out,
                w_attn_up=v_w_attn_up, w_o=v_w_o, norm_mlp=v_norm_mlp, w_ff1=v_w_ff1, w_ff2=v_w_ff2)
    depth = norm_mix.shape[0]
    q = 2 * lax.axis_index("x") + lax.axis_index("y")
    for group in (weights, moms, vels):
        group["w_in"] = jnp.swapaxes(group["w_in"], 1, 2)

    assert depth == 2, "the second layer's gather hides behind the first layer's forward, and likewise backward"
    first, rest = BIG[:1], BIG[1:]
    cw_all = _all_to_all_small(_pad8(conv_w))
    bufs = [{n: _cast_place(weights[n], 0, f"cast_{n}_l0") for n in first}]
    a_ssem, a_rsem, a_views, a_token = _gather_start([bufs[0][n] for n in first], "gather_start_l0_in", cw_all)
    bufs[0].update({n: _cast_place(weights[n], 0, f"cast_{n}_l0") for n in rest})
    bufs += [{n: _cast_place(weights[n], l, f"cast_{n}_l{l}") for n in BIG} for l in range(1, depth)]
    b_ssem, b_rsem, b_views, b_token = _gather_start([bufs[0][n] for n in rest], "gather_start_l0_rest", a_token)
    g_ssem, g_rsem, g_views, g_token = _gather_start([bufs[1][n] for n in BIG], "gather_start_l1", b_token)
    conv_w_full = jnp.concatenate(
        [cw_all[2 * p].reshape(-1)[:conv_w.size].reshape(conv_w.shape) for p in range(N_CHIPS)], axis=-1)
    small = dict(weights)
    small["conv_w_full"] = conv_w_full

    def soon_weights(t):
        got = _gather_finish(a_ssem, a_rsem, a_views, t, "gather_wait_l0_in", "gather_forward_l0_in",
                             [bufs[0][n].shape for n in first])
        return _matrix_weights(dict(zip(first, got)))

    def late_weights(t):
        got = _gather_finish(b_ssem, b_rsem, b_views, t, "gather_wait_l0_rest", "gather_forward_l0_rest",
                             [bufs[0][n].shape for n in rest])
        return _matrix_weights(dict(zip(rest, got)))

    wl, saved = [None] * depth, [None] * depth
    h, saved[0], wl[0] = _layer_fwd(x[0], _small_weights(0, small), "l0", after=g_token, soon=soon_weights,
                                    late=late_weights)
    got = _gather_finish(g_ssem, g_rsem, g_views, h, "gather_wait_l1", "gather_forward_l1",
                         [bufs[1][n].shape for n in BIG])
    h, saved[1], wl[1] = _layer_fwd(h, dict(_small_weights(1, small), **_matrix_weights(dict(zip(BIG, got)))), "l1")
    dh, loss_row = _loss_grad(h, loss_target[0], "loss")

    def pair_stage(names, g, tag):
        views = [_halves(_to_chip_major(n, g[n])) for n in names]
        from_sibling = _pair_swap(views, f"grad_pair_swap_{tag}")
        return [_pair_sum(views[k], from_sibling[k], f"pair_sum_{n}_{tag}") for k, n in enumerate(names)]

    mine, other = [{}, {}], [{}, {}]

    def finish(names, l, started, after, tag):
        ssem, rsem, parts, landing, _ = started
        parts, arrived = _chip_exchange_wait(ssem, rsem, parts, landing, after, f"grad_chip_exchange_wait_{tag}")
        got = [_chip_sum(parts[k], arrived[k], f"chip_sum_{n}_{tag}") for k, n in enumerate(names)]
        mine[l].update(zip(names, got))
        other[l].update(zip(names, _pair_send(got, f"grad_pair_send_{tag}")))

    def small_pieces(g):
        return [_pad8(g[n][:3] if n == "conv_w" else g[n]) for n in SMALL]

    def start_small(l):
        return _all_to_all_small_start(jnp.concatenate(small_pieces(grads[l]), axis=0), f"small_grad_exchange_start_l{l}")

    grads, early, small = [None] * depth, {}, [None] * depth
    dh, grads[1] = _layer_bwd(dh, wl[1], saved[1], "l1")
    second = _chip_exchange_start(pair_stage(BIG, grads[1], "l1"), "grad_chip_exchange_start_l1")
    small[1] = start_small(1)

    def start_rest(g):
        early["rest"] = _chip_exchange_start(pair_stage(rest, g, "l0_rest"), "grad_chip_exchange_start_l0_rest")
        return early["rest"][4]

    def start_last(g):
        early["in"] = _chip_exchange_start(pair_stage(first, g, "l0_in"), "grad_chip_exchange_start_l0_in")
        return early["in"][4]

    dh, grads[0] = _layer_bwd(dh, wl[0], saved[0], "l0", after=[second[4], small[1][4]], mid=start_rest,
                              tail=start_last)
    small[0] = start_small(0)
    finish(BIG, 1, second, dh, "l1")
    finish(rest, 0, early["rest"], dh, "l0_rest")
    loss = lax.psum(loss_row[0, 0], ("x", "y", "c"))
    full = {}

    deltas, new_m, new_v = {}, {}, {}

    def update_matrix(n):
        full[n], deltas[n], new_m[n], new_v[n] = _adamw_halves(
            weights[n], moms[n], vels[n], [mine[l][n] for l in range(depth)], [other[l][n] for l in range(depth)],
            f"adamw_{n}")

    for n in rest:
        update_matrix(n)
    finish(first, 0, early["in"], deltas[rest[-1]], "l0_in")
    for n in first:
        update_matrix(n)
    summed = []
    for l in range(depth):
        ssem, rsem, part, landing, _ = small[l]
        summed.append(_sum_slices(_all_to_all_small_wait(ssem, rsem, part, landing, deltas[first[-1]],
                                                         f"small_grad_exchange_wait_l{l}"), f"small_sum_l{l}"))
    row = 0
    for n, piece in zip(SMALL, small_pieces(grads[0])):
        size = (weights[n].size if n != "conv_w" else depth * 3 * 512) // depth
        flat = jnp.stack([s[row:row + piece.shape[0]].reshape(-1)[:size] for s in summed])
        row += piece.shape[0]
        if n == "conv_w":
            full[n] = lax.dynamic_slice_in_dim(flat.reshape(depth, 3, 512), q * conv_w.shape[2], conv_w.shape[2], axis=2)
        else:
            full[n] = flat.reshape(weights[n].shape)
    for n in SMALL:
        shape = weights[n].shape
        two_d = (-1, shape[-1]) if n not in ("conv_w", "q_gain", "k_gain") else (1, -1)
        d2, m2, v2 = _adamw(weights[n].reshape(two_d), full[n].reshape(two_d), moms[n].reshape(two_d),
                            vels[n].reshape(two_d), f"adamw_{n}")
        deltas[n], new_m[n], new_v[n] = d2.reshape(shape), m2.reshape(shape), v2.reshape(shape)
        full[n] = full[n].reshape(shape)
    for group in (full, deltas, new_m, new_v):
        group["w_in"] = jnp.swapaxes(group["w_in"], 1, 2)
    return (loss, dh[None], *[full[n] for n in ORDER], *[deltas[n] for n in ORDER], *[new_m[n] for n in ORDER],
            *[new_v[n] for n in ORDER])
```

```python
import functools

import jax
import jax.numpy as jnp
from jax import lax
from jax.experimental import pallas as pl
from jax.experimental.pallas import tpu as pltpu

F32 = jnp.float32
BF = jnp.bfloat16
MESH_ID = pl.DeviceIdType.MESH
ANY = pl.BlockSpec(memory_space=pl.ANY)

EPS = 1e-6
MASK_VALUE = -1e30
POOL_WINDOWS = (2, 4, 8, 16)
ATTN_DILATIONS = (1, 4, 16)
ATTN_BLOCK = 128
HEAD_DIM = 64
OFF_Q, OFF_K, OFF_V, OFF_GATE = 2048, 2816, 3584, 4352
N_CHIPS = 4
ADAM_LR, ADAM_B1, ADAM_B2, ADAM_EPS, ADAM_WD, ADAM_STEP = 0.001, 0.9, 0.999, 1e-08, 0.01, 10

VMEM_LIMIT = 48 * 1024 * 1024
LANES = 128

_DIMS = {"nn": (((1,), (0,)), ((), ())), "nt": (((1,), (1,)), ((), ())), "tn": (((0,), (0,)), ((), ()))}


def _params(sem):
    return pltpu.CompilerParams(dimension_semantics=sem, vmem_limit_bytes=VMEM_LIMIT)


def _pallas_call(body, **kw):
    def in_hbm(s):
        pin = isinstance(s, jax.ShapeDtypeStruct) and s is not TOKEN and jnp.issubdtype(s.dtype, jnp.floating)
        return pltpu.HBM(s.shape, s.dtype) if pin else s

    out_shape = kw.pop("out_shape")
    kw["out_shape"] = [in_hbm(s) for s in out_shape] if isinstance(out_shape, (list, tuple)) else in_hbm(out_shape)
    call = pl.pallas_call(body, **kw)

    def run(*args):
        pinned = [pltpu.with_memory_space_constraint(a, pltpu.HBM)
                  if hasattr(a, "dtype") and jnp.issubdtype(a.dtype, jnp.floating) else a for a in args]
        return call(*pinned)

    return run


def _dot(a, b, mode="nn"):
    return lax.dot_general(a, b, _DIMS[mode], preferred_element_type=F32)


def _mm(a, b, mode, name, *, tm, tn, tk, out_dtype=F32, res=None, aux=None, epi=None, n_outer=False,
        b_shards=False, out_shards=False, after=None, vec=None):
    if mode == "tn":
        K, M = a.shape
    else:
        M, K = a.shape
    if b_shards:
        if mode == "nn":
            assert b.shape[1] == K
            N = b.shape[2] * N_CHIPS
        else:
            assert mode == "nt"
            N = b.shape[1]
            assert b.shape[2] * N_CHIPS == K
    else:
        N = b.shape[0] if mode == "nt" else b.shape[1]
    tm, tn, tk = min(tm, M), min(tn, N), min(tk, K)
    assert M % tm == 0 and N % tn == 0 and K % tk == 0
    nk = K // tk
    if n_outer:
        grid = (N // tn, M // tm, nk)
        ij = lambda p, q_: (q_, p)
    else:
        grid = (M // tm, N // tn, nk)
        ij = lambda p, q_: (p, q_)

    def amap(p, q_, k):
        i, j = ij(p, q_)
        return (k, i) if mode == "tn" else (i, k)

    a_spec = pl.BlockSpec((tk, tm) if mode == "tn" else (tm, tk), amap)
    if b_shards:
        if mode == "nn":
            per = (N // N_CHIPS) // tn
            assert per >= 1 and (N // N_CHIPS) % tn == 0

            def bmap(p, q_, k):
                i, j = ij(p, q_)
                return (j // per, k, j % per)

            b_spec = pl.BlockSpec((None, tk, tn), bmap)
        else:
            per = (K // N_CHIPS) // tk
            assert per >= 1 and (K // N_CHIPS) % tk == 0

            def bmap(p, q_, k):
                i, j = ij(p, q_)
                return (k // per, j, k % per)

            b_spec = pl.BlockSpec((None, tn, tk), bmap)
    else:
        def bmap(p, q_, k):
            i, j = ij(p, q_)
            return (j, k) if mode == "nt" else (k, j)

        b_spec = pl.BlockSpec((tn, tk) if mode == "nt" else (tk, tn), bmap)

    def omap(p, q_, k):
        return ij(p, q_)

    o_spec = pl.BlockSpec((tm, tn), omap)
    if out_shards:
        per_o = (N // N_CHIPS) // tn
        assert per_o >= 1 and (N // N_CHIPS) % tn == 0

        def osmap(p, q_, k):
            i, j = ij(p, q_)
            return (j // per_o, i, j % per_o)

        out_spec0 = pl.BlockSpec((None, tm, tn), osmap)
        out_shape0 = jax.ShapeDtypeStruct((N_CHIPS, M, N // N_CHIPS), out_dtype)
    else:
        out_spec0 = o_spec
        out_shape0 = jax.ShapeDtypeStruct((M, N), out_dtype)

    in_specs = [a_spec, b_spec]
    args = [a, b]
    if res is not None:
        in_specs.append(o_spec)
        args.append(res)
    if aux is not None:
        in_specs.append(o_spec)
        args.append(aux)
    if vec is not None:
        in_specs.append(pl.BlockSpec((1, tn), lambda p, q_, k: (0, ij(p, q_)[1])))
        args.append(vec)
    after = [] if after is None else list(after) if isinstance(after, (list, tuple)) else [after]
    in_specs += [ANY] * len(after)
    args += after
    out_specs = [out_spec0]
    out_shape = [out_shape0]
    reduces = epi in ("loss", "rms_bwd")
    if reduces:
        assert tn == N and not n_outer and not out_shards
        width = LANES if epi == "loss" else N
        out_specs.append(pl.BlockSpec((1, width), lambda p, q_, k: (0, 0)))
        out_shape.append(jax.ShapeDtypeStruct((1, width), F32))
    n_out = len(out_shape)
    has_res, has_aux, has_vec, n_after = res is not None, aux is not None, vec is not None, len(after)

    def body(*refs):
        a_ref, b_ref = refs[0], refs[1]
        pos = 2
        res_ref = aux_ref = vec_ref = None
        if has_res:
            res_ref = refs[pos]
            pos += 1
        if has_aux:
            aux_ref = refs[pos]
            pos += 1
        if has_vec:
            vec_ref = refs[pos]
            pos += 1
        pos += n_after
        outs = refs[pos:pos + n_out]
        part = _dot(a_ref[...].astype(BF), b_ref[...].astype(BF), mode)

        first_row_tile = pl.program_id(0) == 0

        def add_to_sum(row):
            @pl.when(first_row_tile)
            def _():
                outs[1][...] = jnp.zeros_like(outs[1])

            outs[1][...] += row

        def finish(acc):
            if epi == "rms_bwd":
                xv = aux_ref[...]
                r = lax.rsqrt(jnp.mean(xv * xv, axis=-1, keepdims=True) + EPS)
                xhat = xv * r
                dy = acc * vec_ref[...]
                outs[0][...] = res_ref[...] + r * (dy - xhat * jnp.mean(dy * xhat, axis=-1, keepdims=True))
                add_to_sum(jnp.sum(acc * xhat, axis=0, keepdims=True))
                return
            if res_ref is not None:
                acc = res_ref[...] + acc
            if epi == "relu2":
                r = jnp.maximum(acc, 0.0)
                outs[0][...] = (r * r).astype(out_dtype)
            elif epi == "drelu2":
                outs[0][...] = (acc * (2.0 * jnp.sqrt(aux_ref[...].astype(F32)))).astype(out_dtype)
            elif epi == "loss":
                e = acc - aux_ref[...]
                outs[0][...] = e / float(N)
                add_to_sum(0.5 * jnp.sum(jnp.mean(e * e, axis=-1, keepdims=True)))
            else:
                outs[0][...] = acc.astype(out_dtype)

        if nk == 1:
            finish(part)
        else:
            acc_ref = refs[pos + n_out]
            k = pl.program_id(2)

            @pl.when(k == 0)
            def _():
                acc_ref[...] = part

            @pl.when(k > 0)
            def _():
                acc_ref[...] += part

            @pl.when(k == nk - 1)
            def _():
                finish(acc_ref[...])

    scratch = [pltpu.VMEM((tm, tn), F32)] if nk > 1 else []
    out = _pallas_call(
        body, name=name, grid=grid, in_specs=in_specs, out_specs=out_specs, out_shape=out_shape,
        scratch_shapes=scratch,
        compiler_params=_params(("arbitrary" if reduces else "parallel", "parallel", "arbitrary")),
    )(*args)
    return out if n_out > 1 else out[0]


def _rms_fwd(x, gain, name, after=None):
    T, D = x.shape
    tm = min(512, T)

    def body(x_ref, g_ref, *rest):
        o_ref = rest[-1]
        xv = x_ref[...]
        r = lax.rsqrt(jnp.mean(xv * xv, axis=-1, keepdims=True) + EPS)
        o_ref[...] = ((xv * r) * g_ref[...]).astype(BF)

    extra = [] if after is None else list(after) if isinstance(after, (list, tuple)) else [after]
    return _pallas_call(
        body, name=name, grid=(T // tm,),
        in_specs=[pl.BlockSpec((tm, D), lambda i: (i, 0)), pl.BlockSpec((1, D), lambda i: (0, 0))] + [ANY] * len(extra),
        out_specs=pl.BlockSpec((tm, D), lambda i: (i, 0)), out_shape=jax.ShapeDtypeStruct((T, D), BF),
        compiler_params=_params(("parallel",)),
    )(x, gain, *extra)


def _rms_bwd(dh, x, gain, dres, name):
    T, D = x.shape
    tm = min(512, T)

    def body(dh_ref, x_ref, g_ref, dres_ref, dx_ref, dg_ref):
        xv = x_ref[...]
        r = lax.rsqrt(jnp.mean(xv * xv, axis=-1, keepdims=True) + EPS)
        xhat = xv * r
        dhv = dh_ref[...]
        dy = dhv * g_ref[...]
        dx_ref[...] = dres_ref[...] + r * (dy - xhat * jnp.mean(dy * xhat, axis=-1, keepdims=True))

        @pl.when(pl.program_id(0) == 0)
        def _():
            dg_ref[...] = jnp.zeros_like(dg_ref)

        dg_ref[...] += jnp.sum(dhv * xhat, axis=0, keepdims=True)

    row = pl.BlockSpec((tm, D), lambda i: (i, 0))
    vec = pl.BlockSpec((1, D), lambda i: (0, 0))
    return _pallas_call(
        body, name=name, grid=(T // tm,), in_specs=[row, row, vec, row], out_specs=[row, vec],
        out_shape=[jax.ShapeDtypeStruct((T, D), F32), jax.ShapeDtypeStruct((1, D), F32)],
        compiler_params=_params(("arbitrary",)),
    )(dh, x, gain, dres)


POOL_HALO = 16
CONV_HALO = 8
POOLCONV_ROWS = 512


def _causal_window_sum(v, w):
    s, sh = v, 1
    while sh < w:
        s = s + pltpu.roll(s, sh, 0)
        sh *= 2
    return s


def _anticausal_window_sum(v, w):
    n = v.shape[0]
    s, sh = v, 1
    while sh < w:
        s = s + pltpu.roll(s, n - sh, 0)
        sh *= 2
    return s


def _poolconv_fwd(z, pmix_b, pscale, convw, name):
    T = z.shape[0]
    R = min(POOLCONV_ROWS, T)
    PH, CH = R // POOL_HALO, R // CONV_HALO

    def body(u_ref, uh_ref, b_ref, c_ref, ch_ref, x_ref, xh_ref, mix_ref, sc_ref, cw_ref, yp_ref, yc_ref):
        i = pl.program_id(0)
        keep = (i > 0).astype(F32)
        row = i * R + lax.broadcasted_iota(jnp.int32, (R, 1), 0)
        w_all = jnp.concatenate([uh_ref[...] * keep, u_ref[...]], axis=0)
        for g, w in enumerate(POOL_WINDOWS):
            cols = slice(128 * g, 128 * (g + 1))
            wg = w_all[:, cols]
            s = _causal_window_sum(wg, w)[POOL_HALO:]
            inv_cnt = 1.0 / jnp.minimum(row + 1, w).astype(F32)
            dgrp = s * inv_cnt - wg[POOL_HALO:]
            y = _dot(dgrp.astype(BF), mix_ref[g]) * sc_ref[:, cols]
            yp_ref[:, cols] = y.astype(BF)
        uc = jnp.concatenate([ch_ref[...] * xh_ref[...] * keep, c_ref[...] * x_ref[...]], axis=0)
        yc = cw_ref[2:3, :] * uc + cw_ref[0:1, :] * pltpu.roll(uc, 2, 0) + cw_ref[1:2, :] * pltpu.roll(uc, 1, 0)
        yc_ref[...] = (b_ref[...] * yc[CONV_HALO:]).astype(BF)

    def main(cb):
        return pl.BlockSpec((R, 512), lambda i: (i, cb))

    def prev(cb, halo, per):
        return pl.BlockSpec((halo, 512), lambda i: (jnp.maximum(i * per - 1, 0), cb))

    full = lambda a: pl.BlockSpec(a.shape, lambda i: (0,) * a.ndim)
    return _pallas_call(
        body, name=name, grid=(T // R,),
        in_specs=[main(0), prev(0, POOL_HALO, PH), main(1), main(2), prev(2, CONV_HALO, CH), main(3),
                  prev(3, CONV_HALO, CH), full(pmix_b), full(pscale), full(convw)],
        out_specs=[pl.BlockSpec((R, 512), lambda i: (i, 0))] * 2,
        out_shape=[jax.ShapeDtypeStruct((T, 512), BF)] * 2,
        compiler_params=_params(("parallel",)),
    )(z, z, z, z, z, z, z, pmix_b, pscale, convw)


def _poolconv_bwd(z, dyp, dyc, pmix_b, pscale, convw, dz, name):
    T = z.shape[0]
    R = min(POOLCONV_ROWS, T)
    PH, CH = R // POOL_HALO, R // CONV_HALO
    nsteps = T // R

    def body(u_ref, uh_ref, b_ref, bn_ref, c_ref, ch_ref, x_ref, xh_ref, dyp_ref, dypn_ref, dyc_ref, dycn_ref,
             mix_ref, sc_ref, cw_ref, dz_in_ref, dz_ref, dmix_ref, dsc_ref, dcw_ref):
        i = pl.program_id(0)
        keep_prev = (i > 0).astype(F32)
        keep_next = (i < nsteps - 1).astype(F32)

        @pl.when(i == 0)
        def _():
            dmix_ref[...] = jnp.zeros_like(dmix_ref)
            dsc_ref[...] = jnp.zeros_like(dsc_ref)
            dcw_ref[...] = jnp.zeros_like(dcw_ref)

        row = i * R + lax.broadcasted_iota(jnp.int32, (R, 1), 0)
        row_ext = i * R + lax.broadcasted_iota(jnp.int32, (R + POOL_HALO, 1), 0)
        w_all = jnp.concatenate([uh_ref[...] * keep_prev, u_ref[...]], axis=0)
        dyp_ext = jnp.concatenate([dyp_ref[...], dypn_ref[...] * keep_next], axis=0)
        for g, w in enumerate(POOL_WINDOWS):
            cols = slice(128 * g, 128 * (g + 1))
            wg = w_all[:, cols]
            s = _causal_window_sum(wg, w)[POOL_HALO:]
            inv_cnt = 1.0 / jnp.minimum(row + 1, w).astype(F32)
            dgrp = (s * inv_cnt - wg[POOL_HALO:]).astype(BF)
            y_pre = _dot(dgrp, mix_ref[g])
            dsc_ref[:, cols] += jnp.sum(dyp_ref[:, cols] * y_pre, axis=0, keepdims=True)
            dyb = (dyp_ext[:, cols] * sc_ref[:, cols]).astype(BF)
            dmix_ref[cols, :] += _dot(dgrp, dyb[:R], "tn")
            dd = _dot(dyb, mix_ref[g], "nt")
            inv_cnt_ext = 1.0 / jnp.minimum(row_ext + 1, w).astype(F32)
            e = _anticausal_window_sum(dd * inv_cnt_ext, w)
            dz_ref[:, cols] = (e[:R] - dd[:R]).astype(BF)
        cw0, cw1, cw2 = cw_ref[0:1, :], cw_ref[1:2, :], cw_ref[2:3, :]
        uc = jnp.concatenate([ch_ref[...] * xh_ref[...] * keep_prev, c_ref[...] * x_ref[...]], axis=0)
        uc1 = pltpu.roll(uc, 1, 0)[CONV_HALO:]
        uc2 = pltpu.roll(uc, 2, 0)[CONV_HALO:]
        uc0 = uc[CONV_HALO:]
        yc = cw2 * uc0 + cw0 * uc2 + cw1 * uc1
        dycv = dyc_ref[...]
        dz_ref[:, 512:1024] = (dycv * yc).astype(BF)
        dv_ext = jnp.concatenate([dycv * b_ref[...], dycn_ref[...] * bn_ref[...] * keep_next], axis=0)
        n_ext = R + CONV_HALO
        duc = (cw2 * dv_ext + cw1 * pltpu.roll(dv_ext, n_ext - 1, 0) + cw0 * pltpu.roll(dv_ext, n_ext - 2, 0))[:R]
        dv = dv_ext[:R]
        dcw_ref[0:1, :] += jnp.sum(dv * uc2, axis=0, keepdims=True)
        dcw_ref[1:2, :] += jnp.sum(dv * uc1, axis=0, keepdims=True)
        dcw_ref[2:3, :] += jnp.sum(dv * uc0, axis=0, keepdims=True)
        dz_ref[:, 1024:1536] = (duc * x_ref[...]).astype(BF)
        dz_ref[:, 1536:2048] = (duc * c_ref[...]).astype(BF)

    def main(cb):
        return pl.BlockSpec((R, 512), lambda i: (i, cb))

    def prev(cb, halo, per):
        return pl.BlockSpec((halo, 512), lambda i: (jnp.maximum(i * per - 1, 0), cb))

    def nxt(cb, halo, per):
        return pl.BlockSpec((halo, 512), lambda i: (jnp.minimum((i + 1) * per, T // halo - 1), cb))

    full = lambda a: pl.BlockSpec(a.shape, lambda i: (0,) * a.ndim)
    return _pallas_call(
        body, name=name, grid=(nsteps,),
        in_specs=[main(0), prev(0, POOL_HALO, PH), main(1), nxt(1, CONV_HALO, CH), main(2), prev(2, CONV_HALO, CH),
                  main(3), prev(3, CONV_HALO, CH), main(0), nxt(0, POOL_HALO, PH), main(0), nxt(0, CONV_HALO, CH),
                  full(pmix_b), full(pscale), full(convw), ANY],
        out_specs=[pl.BlockSpec((R, 2048), lambda i: (i, 0)), pl.BlockSpec((512, 128), lambda i: (0, 0)),
                   pl.BlockSpec((1, 512), lambda i: (0, 0)), pl.BlockSpec((8, 512), lambda i: (0, 0))],
        out_shape=[jax.ShapeDtypeStruct(dz.shape, BF), jax.ShapeDtypeStruct((512, 128), F32),
                   jax.ShapeDtypeStruct((1, 512), F32), jax.ShapeDtypeStruct((8, 512), F32)],
        input_output_aliases={15: 0}, compiler_params=_params(("arbitrary",)),
    )(z, z, z, z, z, z, z, z, dyp, dyp, dyc, dyc, pmix_b, pscale, convw, dz)


def _head_sums(v):
    row = lax.broadcasted_iota(jnp.int32, (LANES, LANES), 0) < HEAD_DIM
    col = lax.broadcasted_iota(jnp.int32, (LANES, LANES), 1) < HEAD_DIM
    same_head = jnp.where(jnp.logical_xor(row, col), 0.0, 1.0).astype(BF)
    hi = v.astype(BF)
    lo = (v - hi.astype(F32)).astype(BF)
    return _dot(hi, same_head) + _dot(lo, same_head)


def _head_norm(x, g2, ma):
    r = lax.rsqrt(_head_sums(x * x) / HEAD_DIM + EPS)
    return x * r, r


def _head_norm_bwd(dy, xhat, r, g2, ma):
    dxh = dy * g2
    return r * (dxh - xhat * (_head_sums(dxh * xhat) / HEAD_DIM))


def _head_col(tile, hm):
    return jnp.max(jnp.where(hm, tile, -jnp.inf), axis=-1, keepdims=True)


def _attn_masks(other_block_exists):
    lane = lax.broadcasted_iota(jnp.int32, (2 * ATTN_BLOCK, ATTN_BLOCK), 1)
    qi = lax.broadcasted_iota(jnp.int32, (2 * ATTN_BLOCK, ATTN_BLOCK), 0) & (ATTN_BLOCK - 1)
    never = (1 - other_block_exists.astype(jnp.int32)) * (2 * ATTN_BLOCK)
    return lane[:ATTN_BLOCK] < HEAD_DIM, lane <= qi, lane >= qi + never


def _stack_heads(x, ma):
    return jnp.concatenate([jnp.where(ma, x, 0.0), jnp.where(ma, 0.0, x)], axis=0)


def _unstack_heads(y, ma):
    return jnp.where(ma, y[:ATTN_BLOCK], y[ATTN_BLOCK:])


def _stack_cols(tile, ma):
    return jnp.concatenate([tile[:, 0:1], tile[:, HEAD_DIM:HEAD_DIM + 1]], axis=0)


QKV_TILES = (OFF_GATE - OFF_Q) // LANES
KIND_TILES = QKV_TILES // 3


def _qk_norm(z, gains, name):
    T = z.shape[0]
    tm = min(512, T)

    def body(x_ref, g_ref, o_ref):
        ma = lax.broadcasted_iota(jnp.int32, (tm, LANES), 1) < HEAD_DIM
        for tile in range(QKV_TILES):
            v = x_ref[:, LANES * tile:LANES * (tile + 1)]
            if tile < 2 * KIND_TILES:
                g = g_ref[0:1, :] if tile < KIND_TILES else g_ref[1:2, :]
                v = _head_norm(v, g, ma)[0] * g
            o_ref[tile] = v

    return _pallas_call(
        body, name=name, grid=(T // tm,),
        in_specs=[pl.BlockSpec((pl.Element(tm), pl.Element(OFF_GATE - OFF_Q)), lambda i: (i * tm, OFF_Q)),
                  pl.BlockSpec((8, LANES), lambda i: (0, 0))],
        out_specs=pl.BlockSpec((QKV_TILES, tm, LANES), lambda i: (0, i, 0)),
        out_shape=jax.ShapeDtypeStruct((QKV_TILES, T, LANES), F32), compiler_params=_params(("parallel",)),
    )(z, gains)


ATTN_STEP_ROWS = 1024
ATTN_UNROLL = 4


def _attn_geometry(T, d):
    sub = ATTN_BLOCK * d
    nb = T // sub
    m = max(1, min(nb, ATTN_STEP_ROWS // sub))
    assert T % sub == 0 and nb % m == 0
    return sub, nb, m


def _attn_rows(jj, r, sub, d):
    start = jj * sub + r
    if d == 1:
        return pl.ds(pl.multiple_of(start, ATTN_BLOCK), ATTN_BLOCK)
    return pl.ds(start, ATTN_BLOCK, stride=d)


def _pick(flag, a, b):
    return jnp.where(jnp.full(a.shape, flag.astype(jnp.int32)) > 0, a, b)


def _attn_fwd(qkv, g, d, name):
    T = qkv.shape[1]
    sub, nb, m = _attn_geometry(T, d)
    scale = HEAD_DIM ** -0.5

    def body(q_ref, kc_ref, kp_ref, vc_ref, vp_ref, o_ref, lse_ref):
        jb = pl.program_id(0)

        def step(s, carry):
            jj, r = s // d, s % d
            here, before = _attn_rows(jj, r, sub, d), _attn_rows(jnp.maximum(jj - 1, 0), r, sub, d)
            edge = _attn_rows(0, r, sub, d)
            first = jj == 0
            ma, mask_c, mask_p = _attn_masks(jb * m + jj > 0)
            qs = _stack_heads(q_ref[here, :], ma).astype(BF)
            kcb = kc_ref[here, :].astype(BF)
            kpb = _pick(first, kp_ref[edge, :], kc_ref[before, :]).astype(BF)
            vcb = vc_ref[here, :].astype(BF)
            vpb = _pick(first, vp_ref[edge, :], vc_ref[before, :]).astype(BF)
            s_c = jnp.where(mask_c, _dot(qs, kcb, "nt") * scale, MASK_VALUE)
            s_p = jnp.where(mask_p, _dot(qs, kpb, "nt") * scale, MASK_VALUE)
            mx = jnp.maximum(jnp.max(s_c, axis=-1, keepdims=True), jnp.max(s_p, axis=-1, keepdims=True))
            p_c = jnp.exp(s_c - mx)
            p_p = jnp.exp(s_p - mx)
            den = jnp.sum(p_c, axis=-1, keepdims=True) + jnp.sum(p_p, axis=-1, keepdims=True)
            o = (_dot(p_c.astype(BF), vcb) + _dot(p_p.astype(BF), vpb)) / den
            o_ref[here, :] = _unstack_heads(o, ma)
            lse_ref[here, :] = _unstack_heads(jnp.broadcast_to(mx + jnp.log(den), o.shape), ma)
            return carry

        lax.fori_loop(0, m * d, step, 0, unroll=ATTN_UNROLL)

    def cur(kind):
        return pl.BlockSpec((None, m * sub, LANES), lambda j, t: (KIND_TILES * kind + 2 * g + t, j, 0))

    def prv(kind):
        return pl.BlockSpec((None, sub, LANES), lambda j, t: (KIND_TILES * kind + 2 * g + t, jnp.maximum(j * m - 1, 0), 0))

    out = pl.BlockSpec((m * sub, LANES), lambda j, t: (j, t))
    return _pallas_call(
        body, name=name, grid=(nb // m, 2), in_specs=[cur(0), cur(1), prv(1), cur(2), prv(2)],
        out_specs=[out, out], out_shape=[jax.ShapeDtypeStruct((T, 256), F32)] * 2,
        compiler_params=_params(("parallel", "parallel")),
    )(qkv, qkv, qkv, qkv, qkv)


def _attn_bwd(z, qkv, do, c, lse, gains, g, d, name, after=None):
    T = z.shape[0]
    sub, nb, m = _attn_geometry(T, d)
    scale = HEAD_DIM ** -0.5
    extra = [] if after is None else [after]

    def body(qr_ref, kr_ref, vc_ref, vp_ref, qn_ref, qnn_ref, kn_ref, knp_ref, do_ref, don_ref, c_ref, cn_ref,
             lse_ref, lsen_ref, g_ref, *rest):
        dq_ref, dk_ref, dv_ref, dgq_ref, dgk_ref, sq_ref, sk_ref, sv_ref = rest[len(extra):]
        jb = pl.program_id(0)

        @pl.when((jb == 0) & (pl.program_id(1) == 0))
        def _():
            dgq_ref[...] = jnp.zeros_like(dgq_ref)
            dgk_ref[...] = jnp.zeros_like(dgk_ref)

        gq, gk = g_ref[0:1, :], g_ref[1:2, :]

        def step(s, carry):
            jj, r = s // d, s % d
            here, edge = _attn_rows(jj, r, sub, d), _attn_rows(0, r, sub, d)
            before = _attn_rows(jnp.maximum(jj - 1, 0), r, sub, d)
            behind = _attn_rows(jnp.minimum(jj + 1, m - 1), r, sub, d)
            first, last = jj == 0, jj == m - 1
            block = jb * m + jj
            ma, mask_c, mask_p = _attn_masks(block > 0)
            mask_n = _attn_masks(block < nb - 1)[2]
            qhat, rq = _head_norm(qr_ref[here, :], gq, ma)
            qn = qn_ref[here, :]
            qn_next = _pick(last, qnn_ref[edge, :], qn_ref[behind, :])
            khat, rk = _head_norm(kr_ref[here, :], gk, ma)
            kcb = kn_ref[here, :].astype(BF)
            kpb = _pick(first, knp_ref[edge, :], kn_ref[before, :]).astype(BF)
            vcb = vc_ref[here, :].astype(BF)
            vpb = _pick(first, vp_ref[edge, :], vc_ref[before, :]).astype(BF)
            do_t, don_t = do_ref[here, :], _pick(last, don_ref[edge, :], do_ref[behind, :])
            c_t, cn_t = c_ref[here, :], _pick(last, cn_ref[edge, :], c_ref[behind, :])
            lse_t, lsen_t = lse_ref[here, :], _pick(last, lsen_ref[edge, :], lse_ref[behind, :])
            qs, dos = _stack_heads(qn, ma).astype(BF), _stack_heads(do_t, ma).astype(BF)
            lse_s, c_s = _stack_cols(lse_t, ma), _stack_cols(c_t, ma)
            s_c = jnp.where(mask_c, _dot(qs, kcb, "nt") * scale, MASK_VALUE)
            s_p = jnp.where(mask_p, _dot(qs, kpb, "nt") * scale, MASK_VALUE)
            p_c = jnp.exp(s_c - lse_s)
            p_p = jnp.exp(s_p - lse_s)
            ds_c = ((p_c * (_dot(dos, vcb, "nt") + c_s)) * scale).astype(BF)
            ds_p = ((p_p * (_dot(dos, vpb, "nt") + c_s)) * scale).astype(BF)
            dq_t = _unstack_heads(_dot(ds_c, kcb) + _dot(ds_p, kpb), ma)
            qs_n, dos_n = _stack_heads(qn_next, ma).astype(BF), _stack_heads(don_t, ma).astype(BF)
            s_n = jnp.where(mask_n, _dot(qs_n, kcb, "nt") * scale, MASK_VALUE)
            p_n = jnp.exp(s_n - _stack_cols(lsen_t, ma))
            ds_n = ((p_n * (_dot(dos_n, vcb, "nt") + _stack_cols(cn_t, ma))) * scale).astype(BF)
            dv_t = _dot(p_c.astype(BF), dos, "tn") + _dot(p_n.astype(BF), dos_n, "tn")
            dk_t = _dot(ds_c, qs, "tn") + _dot(ds_n, qs_n, "tn")
            sq_ref[here, :] = _head_norm_bwd(dq_t, qhat, rq, gq, ma)
            sk_ref[here, :] = _head_norm_bwd(dk_t, khat, rk, gk, ma)
            sv_ref[here, :] = dv_t
            dgq_ref[...] += jnp.sum(dq_t * qhat, axis=0, keepdims=True)
            dgk_ref[...] += jnp.sum(dk_t * khat, axis=0, keepdims=True)
            return carry

        lax.fori_loop(0, m * d, step, 0, unroll=ATTN_UNROLL)
        dq_ref[...] = sq_ref[...].astype(BF)
        dk_ref[...] = sk_ref[...].astype(BF)
        dv_ref[...] = sv_ref[...].astype(BF)

    def raw(col0):
        return pl.BlockSpec((m * sub, LANES), lambda j, t: (j, col0 + 2 * g + t))

    def cur(kind):
        return pl.BlockSpec((None, m * sub, LANES), lambda j, t: (KIND_TILES * kind + 2 * g + t, j, 0))

    def prv(kind):
        return pl.BlockSpec((None, sub, LANES), lambda j, t: (KIND_TILES * kind + 2 * g + t, jnp.maximum(j * m - 1, 0), 0))

    def nxt(kind):
        return pl.BlockSpec((None, sub, LANES),
                            lambda j, t: (KIND_TILES * kind + 2 * g + t, jnp.minimum((j + 1) * m, nb - 1), 0))

    own = pl.BlockSpec((m * sub, LANES), lambda j, t: (j, t))
    own_next = pl.BlockSpec((sub, LANES), lambda j, t: (jnp.minimum((j + 1) * m, nb - 1), t))
    vec = pl.BlockSpec((1, LANES), lambda j, t: (0, 0))
    return _pallas_call(
        body, name=name, grid=(nb // m, 2),
        in_specs=[raw(OFF_Q // LANES), raw(OFF_K // LANES), cur(2), prv(2), cur(0), nxt(0), cur(1), prv(1), own, own_next,
                  own, own_next,
                  own, own_next, pl.BlockSpec((8, LANES), lambda j, t: (0, 0))] + [ANY] * len(extra),
        out_specs=[own, own, own, vec, vec],
        out_shape=[jax.ShapeDtypeStruct((T, 256), BF)] * 3 + [jax.ShapeDtypeStruct((1, LANES), F32)] * 2,
        scratch_shapes=[pltpu.VMEM((m * sub, LANES), F32)] * 3,
        compiler_params=_params(("arbitrary", "arbitrary")),
    )(z, z, qkv, qkv, qkv, qkv, qkv, qkv, do, do, c, c, lse, lse, gains, *extra)


MERGE_ROWS = 256
GATE_TILE = 256


def _group_mix(o_refs, lse_refs):
    lses = [r[...] for r in lse_refs]
    m = jnp.maximum(jnp.maximum(lses[0], lses[1]), lses[2])
    es = [jnp.exp(l - m) for l in lses]
    den = es[0] + es[1] + es[2]
    ws = [e / den for e in es]
    y = ws[0] * o_refs[0][...] + ws[1] * o_refs[1][...] + ws[2] * o_refs[2][...]
    return ws, y


def _sigmoid(v):
    return 1.0 / (1.0 + jnp.exp(-v))


def _merge_specs(T, z, bgate, gpu, gco, gau):
    tm = min(MERGE_ROWS, T)
    row = lambda w: pl.BlockSpec((tm, w), lambda i: (i, 0))
    gate0 = OFF_GATE // GATE_TILE
    gates = [pl.BlockSpec((tm, GATE_TILE), functools.partial(lambda i, cb: (i, cb), cb=gate0 + n))
             for n in range(3 * N_CHIPS)]
    full = lambda a: pl.BlockSpec(a.shape, lambda i: (0,) * a.ndim)
    specs = [row(512), row(512)] + [row(256)] * 6 + gates + [full(bgate), full(gpu), full(gco), full(gau)]
    return tm, row, specs


def _merge_fwd(yp, yc, o3, lse3, z, bgate, gpu, gco, gau, name):
    T = yp.shape[0]
    tm, row, specs = _merge_specs(T, z, bgate, gpu, gco, gau)

    def body(*refs):
        yp_ref, yc_ref = refs[0], refs[1]
        o_refs, lse_refs = refs[2:5], refs[5:8]
        zg = refs[8:20]
        b_ref, gpu_ref, gco_ref, gau_ref, out_ref = refs[20:25]
        yab = _group_mix(o_refs, lse_refs)[1].astype(BF)
        ys = (yp_ref[...], yc_ref[...], yab)
        ups = (gpu_ref, gco_ref, gau_ref)
        for n in range(N_CHIPS):
            acc = None
            for b in range(3):
                gcol = slice(1024 * b + GATE_TILE * n, 1024 * b + GATE_TILE * (n + 1))
                gate = _sigmoid(zg[N_CHIPS * b + n][...] + b_ref[:, gcol])
                term = gate * _dot(ys[b], ups[b][n])
                acc = term if acc is None else acc + term
            out_ref[:, GATE_TILE * n:GATE_TILE * (n + 1)] = acc.astype(BF)

    return _pallas_call(
        body, name=name, grid=(T // tm,), in_specs=specs, out_specs=row(1024),
        out_shape=jax.ShapeDtypeStruct((T, 1024), BF), compiler_params=_params(("parallel",)),
    )(yp, yc, *o3, *lse3, *([z] * 12), bgate, gpu, gco, gau)


def _merge_bwd(dm, yp, yc, o3, lse3, z, bgate, gpu, gco, gau, name):
    T = yp.shape[0]
    tm, row, specs = _merge_specs(T, z, bgate, gpu, gco, gau)
    nsteps = T // tm

    def body(*refs):
        dm_ref, yp_ref, yc_ref = refs[0:3]
        o_refs, lse_refs = refs[3:6], refs[6:9]
        zg = refs[9:21]
        b_ref, gpu_ref, gco_ref, gau_ref = refs[21:25]
        dzg_ref, dyp_ref, dyc_ref = refs[25:28]
        do_refs, c_refs = refs[28:31], refs[31:34]
        dgpu_ref, dgco_ref, dgau_ref, dbg_ref = refs[34:38]
        accs = refs[38:41]
        i = pl.program_id(0)

        @pl.when(i == 0)
        def _():
            for a in accs:
                a[...] = jnp.zeros_like(a)
            dbg_ref[...] = jnp.zeros_like(dbg_ref)

        ws, y = _group_mix(o_refs, lse_refs)
        ys = (yp_ref[...], yc_ref[...], y.astype(BF))
        ups = (gpu_ref, gco_ref, gau_ref)
        dys = [None, None, None]
        for n in range(N_CHIPS):
            dmn = dm_ref[:, GATE_TILE * n:GATE_TILE * (n + 1)]
            for b in range(3):
                gcol = slice(1024 * b + GATE_TILE * n, 1024 * b + GATE_TILE * (n + 1))
                gate = _sigmoid(zg[N_CHIPS * b + n][...] + b_ref[:, gcol])
                up = _dot(ys[b], ups[b][n])
                dzg = (dmn * up) * (gate * (1.0 - gate))
                dzg_ref[:, gcol] = dzg.astype(BF)
                dbg_ref[:, gcol] += jnp.sum(dzg, axis=0, keepdims=True)
                dup = (dmn * gate).astype(BF)
                accs[b][n] += _dot(ys[b], dup, "tn")
                dyb = _dot(dup, ups[b][n], "nt")
                dys[b] = dyb if dys[b] is None else dys[b] + dyb
        dyp_ref[...] = dys[0]
        dyc_ref[...] = dys[1]
        dya = dys[2]
        lane = lax.broadcasted_iota(jnp.int32, dya.shape, 1) // HEAD_DIM
        pr = dya * y
        rho = jnp.zeros_like(pr)
        for h in range(256 // HEAD_DIM):
            hm = lane == h
            rho = jnp.where(hm, jnp.sum(jnp.where(hm, pr, 0.0), axis=-1, keepdims=True), rho)
        for g in range(3):
            do_refs[g][...] = ws[g] * dya
            c_refs[g][...] = -(ws[g] * rho)

        @pl.when(i == nsteps - 1)
        def _():
            dgpu_ref[...] = accs[0][...].astype(BF)
            dgco_ref[...] = accs[1][...].astype(BF)
            dgau_ref[...] = accs[2][...].astype(BF)

    full = lambda a: pl.BlockSpec(a.shape, lambda i: (0,) * a.ndim)
    dz_gate = pl.BlockSpec((pl.Element(tm), pl.Element(3072)), lambda i: (i * tm, OFF_GATE))
    out_specs = ([dz_gate, row(512), row(512)] + [row(256)] * 6 + [full(gpu), full(gco), full(gau)]
                 + [pl.BlockSpec((1, 3072), lambda i: (0, 0))])
    out_shape = ([jax.ShapeDtypeStruct(z.shape, BF)] + [jax.ShapeDtypeStruct((T, 512), F32)] * 2
                 + [jax.ShapeDtypeStruct((T, 256), F32)] * 6
                 + [jax.ShapeDtypeStruct(g.shape, BF) for g in (gpu, gco, gau)]
                 + [jax.ShapeDtypeStruct((1, 3072), F32)])
    return _pallas_call(
        body, name=name, grid=(nsteps,), in_specs=[row(1024)] + specs, out_specs=out_specs, out_shape=out_shape,
        scratch_shapes=[pltpu.VMEM(g.shape, F32) for g in (gpu, gco, gau)],
        compiler_params=_params(("arbitrary",)),
    )(dm, yp, yc, *o3, *lse3, *([z] * 12), bgate, gpu, gco, gau)


def _layer_fwd(x, w, tag, after=None, soon=None, late=None, target=None):
    hb = _rms_fwd(x, w["norm_mix"], f"rms_mix_{tag}", after=after)
    if soon is not None:
        w = dict(w, **soon(hb))
    z = _mm(hb, w["w_in"], "nt", f"in_proj_{tag}", tm=512, tn=3712, tk=1024, n_outer=True)
    yp, yc = _poolconv_fwd(z, w["pool_mix"], w["pool_scale"], w["conv_w"], f"poolconv_{tag}")
    qkv = _qk_norm(z, w["qk_gain"], f"qk_norm_{tag}")
    o3, lse3 = [], []
    for g, d in enumerate(ATTN_DILATIONS):
        o, lse = _attn_fwd(qkv, g, d, f"attn{g}_{tag}")
        o3.append(o)
        lse3.append(lse)
    if late is not None:
        w = dict(w, **late(lse3[-1]))
    merged = _merge_fwd(yp, yc, o3, lse3, z, w["b_gate"], w["w_pool_up"], w["w_conv_out"], w["w_attn_up"],
                        f"merge_{tag}")
    x1 = _mm(merged, w["w_o"], "nn", f"out_proj_{tag}", tm=1024, tn=1024, tk=1024, res=x)
    h2b = _rms_fwd(x1, w["norm_mlp"], f"rms_mlp_{tag}")
    rb = _mm(h2b, w["w_ff1"], "nn", f"ff1_{tag}", tm=1024, tn=1024, tk=1024, out_dtype=BF, epi="relu2", n_outer=True,
             b_shards=True)
    if target is None:
        x2 = _mm(rb, w["w_ff2"], "nn", f"ff2_{tag}", tm=512, tn=1024, tk=4096, res=x1)
    else:
        x2 = _mm(rb, w["w_ff2"], "nn", f"ff2_{tag}", tm=512, tn=1024, tk=4096, res=x1, aux=target, epi="loss")
    saved = dict(x=x, hb=hb, z=z, yp=yp, yc=yc, qkv=qkv, o3=o3, lse3=lse3, merged=merged, x1=x1, h2b=h2b, rb=rb)
    return x2, saved, w


def _layer_bwd(dx2, w, s, tag, after=None, mid=None, tail=None):
    g = {}
    dab = _mm(dx2, w["w_ff2"], "nt", f"d_ff2_act_{tag}", tm=1024, tn=1024, tk=1024, out_dtype=BF, aux=s["rb"],
              epi="drelu2", after=after)
    g["w_ff2"] = _mm(s["rb"], dx2, "tn", f"d_ff2_w_{tag}", tm=1024, tn=1024, tk=2048, out_dtype=BF)
    g["w_ff1"] = _mm(s["h2b"], dab, "tn", f"d_ff1_w_{tag}", tm=1024, tn=1024, tk=2048, out_dtype=BF, out_shards=True)
    dx1, g["norm_mlp"] = _mm(dab, w["w_ff1"], "nt", f"d_ff1_act_{tag}", tm=1024, tn=1024, tk=1024, b_shards=True,
                             res=dx2, aux=s["x1"], vec=w["norm_mlp"], epi="rms_bwd")
    dm = _mm(dx1, w["w_o"], "nt", f"d_out_act_{tag}", tm=1024, tn=1024, tk=1024)
    g["w_o"] = _mm(s["merged"], dx1, "tn", f"d_out_w_{tag}", tm=1024, tn=1024, tk=1024, out_dtype=BF)
    (dz, dyp, dyc, do0, do1, do2, c0, c1, c2, g["w_pool_up"], g["w_conv_out"], g["w_attn_up"],
     g["b_gate"]) = _merge_bwd(dm, s["yp"], s["yc"], s["o3"], s["lse3"], s["z"], w["b_gate"], w["w_pool_up"],
                               w["w_conv_out"], w["w_attn_up"], f"d_merge_{tag}")
    behind = mid(g) if mid is not None else None
    dq, dk, dv = [], [], []
    dgq = dgk = None
    for gi, d in enumerate(ATTN_DILATIONS):
        dzq, dzk, dzv, pq, pk = _attn_bwd(s["z"], s["qkv"], (do0, do1, do2)[gi], (c0, c1, c2)[gi], s["lse3"][gi],
                                          w["qk_gain"], gi, d, f"d_attn{gi}_{tag}", after=behind)
        dq.append(dzq)
        dk.append(dzk)
        dv.append(dzv)
        dgq = pq if dgq is None else dgq + pq
        dgk = pk if dgk is None else dgk + pk
    g["q_gain"] = dgq[:, :HEAD_DIM] + dgq[:, HEAD_DIM:]
    g["k_gain"] = dgk[:, :HEAD_DIM] + dgk[:, HEAD_DIM:]
    for off, pieces in ((OFF_Q, dq), (OFF_K, dk), (OFF_V, dv)):
        for gi, piece in enumerate(pieces):
            dz = lax.dynamic_update_slice(dz, piece, (0, off + 256 * gi))
    dz, g["pool_mix"], g["pool_scale"], g["conv_w"] = _poolconv_bwd(
        s["z"], dyp, dyc, w["pool_mix"], w["pool_scale"], w["conv_w"], dz, f"d_poolconv_{tag}")
    g["w_in"] = _mm(s["hb"], dz, "tn", f"d_in_w_{tag}", tm=512, tn=3712, tk=1024, out_dtype=BF)
    dh = _mm(dz, w["w_in"], "nn", f"d_in_act_{tag}", tm=1024, tn=1024, tk=3712,
             after=tail(g) if tail is not None else None)
    dx, g["norm_mix"] = _rms_bwd(dh, s["x"], w["norm_mix"], dx1, f"d_rms_mix_{tag}")
    return dx, g


def _position():
    x, y, c = lax.axis_index("x"), lax.axis_index("y"), lax.axis_index("c")
    chips = [(1 - x, y), (x, 1 - y), (1 - x, 1 - y)]
    return x, y, c, 2 * x + y, chips, [2 * cx + cy for cx, cy in chips]


def _remote(src, dst, ssem, rsem, dev):
    return pltpu.make_async_remote_copy(src_ref=src, dst_ref=dst, send_sem=ssem, recv_sem=rsem, device_id=dev,
                                        device_id_type=MESH_ID)


def _halves(a):
    return a.reshape(a.shape[0], 2, a.shape[1] // 2, a.shape[2])


SEM = pl.BlockSpec(memory_space=pltpu.SEMAPHORE)
TOKEN = jax.ShapeDtypeStruct((8, LANES), F32)
TOKEN_SPEC = pl.BlockSpec(memory_space=pltpu.VMEM)


def _split_params():
    return pltpu.CompilerParams(has_side_effects=pltpu.SideEffectType.DATAFLOW_SIDE_EFFECTING)


def _gather_start(bufs, name, after):
    n = len(bufs)
    views = [_halves(b) for b in bufs]

    def body(*refs):
        first_sem = n + 1
        ssem, rsem = refs[first_sem:first_sem + ns], refs[first_sem + ns:first_sem + 2 * ns]
        outs, token = refs[first_sem + 2 * ns:first_sem + 2 * ns + n], refs[first_sem + 2 * ns + n]
        x, y, c, q, chips, qs = _position()
        for k in range(n):
            mine = outs[k].at[q, c]
            for j, chip in enumerate(chips):
                _remote(mine, mine, ssem[3 * k + j], rsem[3 * k + j], (chip[0], chip[1], c)).start()
        token[...] = jnp.zeros_like(token)

    ns = 3 * n
    outs = _pallas_call(
        body, name=name, in_specs=[ANY] * (n + 1), out_specs=[SEM] * (2 * ns) + [ANY] * n + [TOKEN_SPEC],
        out_shape=[pltpu.SemaphoreType.DMA(())] * (2 * ns) + [jax.ShapeDtypeStruct(v.shape, v.dtype) for v in views]
        + [TOKEN],
        input_output_aliases={k: k + 2 * ns for k in range(n)}, compiler_params=_split_params(),
    )(*views, after)
    return list(outs[:ns]), list(outs[ns:2 * ns]), list(outs[2 * ns:2 * ns + n]), outs[2 * ns + n]


def _gather_finish(ssem, rsem, views, after, name_wait, name_forward, shapes):
    n = len(views)
    ns = len(ssem)

    def wait_body(*refs):
        ssem_ref, rsem_ref = refs[n:n + ns], refs[n + ns:n + 2 * ns]
        outs = refs[n + 2 * ns + 1:]
        x, y, c, q, chips, qs = _position()
        for k in range(n):
            for j, chip in enumerate(chips):
                cp = _remote(outs[k].at[q, c], outs[k].at[qs[j], c], ssem_ref[3 * k + j], rsem_ref[3 * k + j],
                             (chip[0], chip[1], c))
                cp.wait_send()
                cp.wait_recv()

    landed = _pallas_call(
        wait_body, name=name_wait, in_specs=[ANY] * n + [SEM] * (2 * ns) + [ANY], out_specs=[ANY] * n,
        out_shape=[jax.ShapeDtypeStruct(v.shape, v.dtype) for v in views],
        input_output_aliases={k: k for k in range(n)}, compiler_params=_split_params(),
    )(*views, *ssem, *rsem, after)

    def forward_body(*refs):
        outs = refs[n:2 * n]
        fssem, frsem = refs[2 * n:]
        x, y, c, q, chips, qs = _position()
        sib = (x, y, 1 - c)
        sent = []
        for k in range(n):
            for j in range(3):
                slot = outs[k].at[qs[j], c]
                cp = _remote(slot, slot, fssem.at[k, j], frsem.at[k, j], sib)
                cp.start()
                sent.append(cp)
        for k in range(n):
            for j in range(3):
                slot = outs[k].at[qs[j], 1 - c]
                _remote(slot, slot, fssem.at[k, j], frsem.at[k, j], sib).wait_recv()
        for cp in sent:
            cp.wait_send()

    outs = _pallas_call(
        forward_body, name=name_forward, in_specs=[ANY] * n, out_specs=[ANY] * n,
        out_shape=[jax.ShapeDtypeStruct(v.shape, v.dtype) for v in views],
        input_output_aliases={k: k for k in range(n)}, scratch_shapes=[pltpu.SemaphoreType.DMA((n, 3))] * 2,
    )(*landed)
    return [o.reshape(s) for o, s in zip(outs, shapes)]


def _chip_exchange_start(parts, name):
    n = len(parts)

    def body(*refs):
        ssem, rsem = refs[n:n + ns], refs[n + ns:n + 2 * ns]
        base = n + 2 * ns
        srcs, outs, token = refs[base:base + n], refs[base + n:base + 2 * n], refs[base + 2 * n]
        x, y, c, q, chips, qs = _position()
        for k in range(n):
            for j, chip in enumerate(chips):
                _remote(srcs[k].at[qs[j]], outs[k].at[j], ssem[3 * k + j], rsem[3 * k + j],
                        (chip[0], chip[1], c)).start()
        token[...] = jnp.zeros_like(token)

    ns = 3 * n
    outs = _pallas_call(
        body, name=name, in_specs=[ANY] * n, out_specs=[SEM] * (2 * ns) + [ANY] * (2 * n) + [TOKEN_SPEC],
        out_shape=[pltpu.SemaphoreType.DMA(())] * (2 * ns) + [jax.ShapeDtypeStruct(a.shape, a.dtype) for a in parts]
        + [jax.ShapeDtypeStruct((3,) + a.shape[1:], a.dtype) for a in parts] + [TOKEN],
        input_output_aliases={k: k + 2 * ns for k in range(n)}, compiler_params=_split_params(),
    )(*parts)
    b = 2 * ns
    return list(outs[:ns]), list(outs[ns:b]), list(outs[b:b + n]), list(outs[b + n:b + 2 * n]), outs[b + 2 * n]


def _chip_exchange_wait(ssem, rsem, parts, landing, after, name):
    n = len(parts)
    ns = len(ssem)

    def body(*refs):
        ssem_ref, rsem_ref = refs[2 * n:2 * n + ns], refs[2 * n + ns:2 * n + 2 * ns]
        base = 2 * n + 2 * ns + 1
        srcs, outs = refs[base:base + n], refs[base + n:]
        x, y, c, q, chips, qs = _position()
        for k in range(n):
            for j, chip in enumerate(chips):
                cp = _remote(srcs[k].at[qs[j]], outs[k].at[j], ssem_ref[3 * k + j], rsem_ref[3 * k + j],
                             (chip[0], chip[1], c))
                cp.wait_send()
                cp.wait_recv()

    outs = _pallas_call(
        body, name=name, in_specs=[ANY] * (2 * n) + [SEM] * (2 * ns) + [ANY], out_specs=[ANY] * (2 * n),
        out_shape=[jax.ShapeDtypeStruct(a.shape, a.dtype) for a in list(parts) + list(landing)],
        input_output_aliases={k: k for k in range(2 * n)}, compiler_params=_split_params(),
    )(*parts, *landing, *ssem, *rsem, after)
    return list(outs[:n]), list(outs[n:])


def _pair_swap(views, name):
    n = len(views)

    def body(*refs):
        ins, outs = refs[:n], refs[n:2 * n]
        ssem, rsem = refs[2 * n:]
        x, y, c, _, _, _ = _position()
        cps = [_remote(ins[k].at[pl.ds(0, N_CHIPS), 1 - c], outs[k], ssem.at[k], rsem.at[k], (x, y, 1 - c))
               for k in range(n)]
        for cp in cps:
            cp.start()
        for cp in cps:
            cp.wait()

    return _pallas_call(
        body, name=name, in_specs=[ANY] * n, out_specs=[ANY] * n,
        out_shape=[jax.ShapeDtypeStruct((v.shape[0],) + v.shape[2:], v.dtype) for v in views],
        scratch_shapes=[pltpu.SemaphoreType.DMA((n,))] * 2,
    )(*views)


def _chip_exchange(parts, name):
    n = len(parts)

    def body(*refs):
        ins, outs = refs[:n], refs[n:2 * n]
        ssem, rsem = refs[2 * n:]
        x, y, c, q, chips, qs = _position()
        cps = []
        for k in range(n):
            for j, chip in enumerate(chips):
                cp = _remote(ins[k].at[qs[j]], outs[k].at[j], ssem.at[k, j], rsem.at[k, j], (chip[0], chip[1], c))
                cp.start()
                cps.append(cp)
        for cp in cps:
            cp.wait_recv()
        for cp in cps:
            cp.wait_send()

    return _pallas_call(
        body, name=name, in_specs=[ANY] * n, out_specs=[ANY] * n,
        out_shape=[jax.ShapeDtypeStruct((3,) + a.shape[1:], a.dtype) for a in parts],
        scratch_shapes=[pltpu.SemaphoreType.DMA((n, 3))] * 2,
    )(*parts)


def _pair_send(arrays, name):
    n = len(arrays)

    def body(*refs):
        ins, outs = refs[:n], refs[n:2 * n]
        ssem, rsem = refs[2 * n:]
        x, y, c, _, _, _ = _position()
        cps = [_remote(ins[k], outs[k], ssem.at[k], rsem.at[k], (x, y, 1 - c)) for k in range(n)]
        for cp in cps:
            cp.start()
        for cp in cps:
            cp.wait()

    return _pallas_call(
        body, name=name, in_specs=[ANY] * n, out_specs=[ANY] * n,
        out_shape=[jax.ShapeDtypeStruct(a.shape, a.dtype) for a in arrays],
        scratch_shapes=[pltpu.SemaphoreType.DMA((n,))] * 2,
    )(*arrays)


def _all_to_all_small(part):
    P = part.shape[0]

    def body(in_ref, out_ref, lsem, ssem, rsem):
        x, y, c = lax.axis_index("x"), lax.axis_index("y"), lax.axis_index("c")
        me = 4 * x + 2 * y + c
        flips = [(fx, fy, fc) for fx in (0, 1) for fy in (0, 1) for fc in (0, 1)][1:]
        peers = [((x + fx) % 2, (y + fy) % 2, (c + fc) % 2) for fx, fy, fc in flips]
        loc = pltpu.make_async_copy(in_ref, out_ref.at[me], lsem)
        loc.start()
        cps = [_remote(in_ref, out_ref.at[me], ssem.at[j], rsem.at[j], peer) for j, peer in enumerate(peers)]
        for cp in cps:
            cp.start()
        for j, (px, py, pc) in enumerate(peers):
            _remote(in_ref, out_ref.at[4 * px + 2 * py + pc], ssem.at[j], rsem.at[j], peers[j]).wait_recv()
        for cp in cps:
            cp.wait_send()
        loc.wait()

    return _pallas_call(
        body, name="small_exchange", in_specs=[ANY], out_specs=ANY,
        out_shape=jax.ShapeDtypeStruct((8, P, LANES), F32),
        scratch_shapes=[pltpu.SemaphoreType.DMA(())] + [pltpu.SemaphoreType.DMA((7,))] * 2,
    )(part)


def _small_peers():
    x, y, c = lax.axis_index("x"), lax.axis_index("y"), lax.axis_index("c")
    flips = [(fx, fy, fc) for fx in (0, 1) for fy in (0, 1) for fc in (0, 1)][1:]
    peers = [((x + fx) % 2, (y + fy) % 2, (c + fc) % 2) for fx, fy, fc in flips]
    return 4 * x + 2 * y + c, peers


def _all_to_all_small_start(part, name):
    P = part.shape[0]
    me = 4 * lax.axis_index("x") + 2 * lax.axis_index("y") + lax.axis_index("c")
    landing = lax.dynamic_update_slice(jnp.zeros((8, P, LANES), F32), part[None], (me, 0, 0))

    def body(*refs):
        sems, src, land, token = refs[2:16], refs[16], refs[17], refs[18]
        me_, peers = _small_peers()
        for j, peer in enumerate(peers):
            _remote(src, land.at[me_], sems[j], sems[7 + j], peer).start()
        token[...] = jnp.zeros_like(token)

    outs = _pallas_call(
        body, name=name, in_specs=[ANY, ANY], out_specs=[SEM] * 14 + [ANY, ANY, TOKEN_SPEC],
        out_shape=[pltpu.SemaphoreType.DMA(())] * 14 + [jax.ShapeDtypeStruct(part.shape, F32),
                                                       jax.ShapeDtypeStruct((8, P, LANES), F32), TOKEN],
        input_output_aliases={0: 14, 1: 15}, compiler_params=_split_params(),
    )(part, landing)
    return list(outs[:7]), list(outs[7:14]), outs[14], outs[15], outs[16]


def _all_to_all_small_wait(ssem, rsem, part, landing, after, name):
    def body(*refs):
        sems, src, land = refs[2:16], refs[17], refs[18]
        _, peers = _small_peers()
        for j, (px, py, pc) in enumerate(peers):
            cp = _remote(src, land.at[4 * px + 2 * py + pc], sems[j], sems[7 + j], peers[j])
            cp.wait_send()
            cp.wait_recv()

    return _pallas_call(
        body, name=name, in_specs=[ANY, ANY] + [SEM] * 14 + [ANY], out_specs=[ANY, ANY],
        out_shape=[jax.ShapeDtypeStruct(part.shape, F32), jax.ShapeDtypeStruct(landing.shape, F32)],
        input_output_aliases={0: 0, 1: 1}, compiler_params=_split_params(),
    )(part, landing, *ssem, *rsem, after)[1]


def _row_tile(rows, width, n_arrays):
    t = rows
    while t % 2 == 0 and t > 8 and 2 * n_arrays * t * width * 4 > VMEM_LIMIT // 2:
        t //= 2
    return t


def _chip():
    return 2 * lax.axis_index("x") + lax.axis_index("y")


def _core():
    return lax.axis_index("c")


def _cast_place(w3, layer, name):
    _, r, c = w3.shape
    tr = _row_tile(r, c, 2)

    def body(w_ref, o_ref):
        o_ref[...] = w_ref[...].astype(BF)

    return _pallas_call(
        body, name=name, grid=(r // tr,), in_specs=[pl.BlockSpec((None, tr, c), lambda i: (layer, i, 0))],
        out_specs=pl.BlockSpec((None, tr, c), lambda i: (_chip(), i, 0)),
        out_shape=jax.ShapeDtypeStruct((N_CHIPS, r, c), BF), compiler_params=_params(("parallel",)),
    )(w3)


def _pair_sum(view, recv, name):
    _, _, hr, c = view.shape
    tr = _row_tile(hr, c, 3)

    def body(g_ref, r_ref, o_ref):
        o_ref[...] = (g_ref[...].astype(F32) + r_ref[...].astype(F32)).astype(BF)

    blk = pl.BlockSpec((None, tr, c), lambda p, i: (p, i, 0))
    return _pallas_call(
        body, name=name, grid=(N_CHIPS, hr // tr),
        in_specs=[pl.BlockSpec((None, None, tr, c), lambda p, i: (p, _core(), i, 0)), blk], out_specs=blk,
        out_shape=jax.ShapeDtypeStruct(recv.shape, BF), compiler_params=_params(("parallel", "parallel")),
    )(view, recv)


def _chip_sum(parts, recv, name):
    _, hr, c = parts.shape
    tr = _row_tile(hr, c, 6)

    def body(p_ref, r_ref, o_ref):
        acc = p_ref[...].astype(F32)
        for j in range(3):
            acc = acc + r_ref[j].astype(F32)
        o_ref[...] = acc

    return _pallas_call(
        body, name=name, grid=(hr // tr,),
        in_specs=[pl.BlockSpec((None, tr, c), lambda i: (_chip(), i, 0)), pl.BlockSpec((3, tr, c), lambda i: (0, i, 0))],
        out_specs=pl.BlockSpec((tr, c), lambda i: (i, 0)),
        out_shape=jax.ShapeDtypeStruct((hr, c), F32), compiler_params=_params(("parallel",)),
    )(parts, recv)


def _sum_slices(a, name):
    n, rows, width = a.shape
    tr = _row_tile(rows, width, n + 1)

    def body(a_ref, o_ref):
        acc = a_ref[0].astype(F32)
        for i in range(1, n):
            acc = acc + a_ref[i].astype(F32)
        o_ref[...] = acc

    return _pallas_call(
        body, name=name, grid=(rows // tr,), in_specs=[pl.BlockSpec((n, tr, width), lambda i: (0, i, 0))],
        out_specs=pl.BlockSpec((tr, width), lambda i: (i, 0)), out_shape=jax.ShapeDtypeStruct((rows, width), F32),
        compiler_params=_params(("parallel",)),
    )(a)


def _adamw_update(w, g, m, v):
    nm = ADAM_B1 * m + (1.0 - ADAM_B1) * g
    nv = ADAM_B2 * v + (1.0 - ADAM_B2) * (g * g)
    m_hat = nm / (1.0 - ADAM_B1 ** ADAM_STEP)
    v_hat = nv / (1.0 - ADAM_B2 ** ADAM_STEP)
    return -ADAM_LR * (m_hat / (jnp.sqrt(v_hat) + ADAM_EPS) + ADAM_WD * w), nm, nv


def _adamw(w, g, m, v, name):
    rows, width = w.shape
    tr = _row_tile(rows, width, 7)

    def body(w_ref, g_ref, m_ref, v_ref, d_ref, nm_ref, nv_ref):
        d_ref[...], nm_ref[...], nv_ref[...] = _adamw_update(w_ref[...], g_ref[...], m_ref[...], v_ref[...])

    blk = pl.BlockSpec((tr, width), lambda i: (i, 0))
    return _pallas_call(
        body, name=name, grid=(rows // tr,), in_specs=[blk] * 4, out_specs=[blk] * 3,
        out_shape=[jax.ShapeDtypeStruct((rows, width), F32)] * 3, compiler_params=_params(("parallel",)),
    )(w, g, m, v)


def _adamw_halves(w3, m3, v3, mine, other, name):
    depth, r, c = w3.shape
    assert depth == 2
    hr = r // 2
    tr = _row_tile(hr, c, 11)
    sources = ((0, True, mine[0]), (0, False, other[0]), (1, True, mine[1]), (1, False, other[1]))

    def active(l, h, layer, own):
        mine_half = h == _core()
        return (l == layer) & (mine_half if own else jnp.logical_not(mine_half))

    def body(w_ref, m_ref, v_ref, *rest):
        g_refs, (go_ref, d_ref, nm_ref, nv_ref) = rest[:4], rest[4:]
        l, h = pl.program_id(0), pl.program_id(1)
        for (layer, own, _), g_ref in zip(sources, g_refs):
            @pl.when(active(l, h, layer, own))
            def _():
                gv = g_ref[...]
                go_ref[...] = gv
                d_ref[...], nm_ref[...], nv_ref[...] = _adamw_update(w_ref[...], gv, m_ref[...], v_ref[...])

    def gspec(layer, own):
        return pl.BlockSpec((tr, c), lambda l, h, i: (jnp.where(active(l, h, layer, own), i, 0), 0))

    blk = pl.BlockSpec((None, None, tr, c), lambda l, h, i: (l, h, i, 0))
    view = lambda a: a.reshape(depth, 2, hr, c)
    outs = _pallas_call(
        body, name=name, grid=(depth, 2, hr // tr),
        in_specs=[blk] * 3 + [gspec(layer, own) for layer, own, _ in sources], out_specs=[blk] * 4,
        out_shape=[jax.ShapeDtypeStruct((depth, 2, hr, c), F32)] * 4,
        compiler_params=_params(("parallel", "parallel", "parallel")),
    )(view(w3), view(m3), view(v3), *[s[2] for s in sources])
    return [o.reshape(w3.shape) for o in outs]


BIG = ("w_in", "w_pool_up", "w_conv_out", "w_attn_up", "w_o", "w_ff1", "w_ff2")
SMALL = ("norm_mix", "b_gate", "pool_mix", "pool_scale", "conv_w", "q_gain", "k_gain", "norm_mlp")
ORDER = ("norm_mix", "w_in", "b_gate", "pool_mix", "pool_scale", "conv_w", "q_gain", "k_gain", "w_pool_up",
         "w_conv_out", "w_attn_up", "w_o", "norm_mlp", "w_ff1", "w_ff2")
COLUMN_SHARDED = ("w_pool_up", "w_conv_out", "w_attn_up", "w_ff1")


def _matrix_weights(gathered):
    w = {}
    for name, g4 in gathered.items():
        if name in COLUMN_SHARDED:
            w[name] = g4
        else:
            w[name] = g4.reshape(N_CHIPS * g4.shape[1], g4.shape[2])
    return w


def _small_weights(l, small):
    w = {}
    w["norm_mix"] = small["norm_mix"][l][None]
    w["norm_mlp"] = small["norm_mlp"][l][None]
    w["b_gate"] = small["b_gate"][l][None]
    w["pool_mix"] = small["pool_mix"][l].astype(BF)
    w["pool_scale"] = small["pool_scale"][l][None]
    w["conv_w"] = jnp.pad(small["conv_w_full"][l], ((0, 5), (0, 0)))
    w["qk_gain"] = jnp.pad(jnp.stack([jnp.tile(small["q_gain"][l], 2), jnp.tile(small["k_gain"][l], 2)]), ((0, 6), (0, 0)))
    return w


def _to_chip_major(name, g):
    if name == "w_in":
        return g.T.reshape(N_CHIPS, g.shape[1] // N_CHIPS, g.shape[0])
    if name in COLUMN_SHARDED:
        return g
    return g.reshape(N_CHIPS, g.shape[0] // N_CHIPS, g.shape[1])


def _pad8(a):
    a = a.reshape(-1)
    return jnp.pad(a, (0, (-a.size) % (8 * LANES))).reshape(-1, LANES)


def kernel(x, norm_mix, w_in, b_gate, pool_mix, pool_scale, conv_w, q_gain, k_gain, w_pool_up, w_conv_out, w_attn_up, w_o, norm_mlp, w_ff1, w_ff2, loss_target, m_norm_mix, m_w_in, m_b_gate, m_pool_mix, m_pool_scale, m_conv_w, m_q_gain, m_k_gain, m_w_pool_up, m_w_conv_out, m_w_attn_up, m_w_o, m_norm_mlp, m_w_ff1, m_w_ff2, v_norm_mix, v_w_in, v_b_gate, v_pool_mix, v_pool_scale, v_conv_w, v_q_gain, v_k_gain, v_w_pool_up, v_w_conv_out, v_w_attn_up, v_w_o, v_norm_mlp, v_w_ff1, v_w_ff2):
    weights = dict(norm_mix=norm_mix, w_in=w_in, b_gate=b_gate, pool_mix=pool_mix, pool_scale=pool_scale, conv_w=conv_w,
                   q_gain=q_gain, k_gain=k_gain, w_pool_up=w_pool_up, w_conv_out=w_conv_out, w_attn_up=w_attn_up,
                   w_o=w_o, norm_mlp=norm_mlp, w_ff1=w_ff1, w_ff2=w_ff2)
    moms = dict(norm_mix=m_norm_mix, w_in=m_w_in, b_gate=m_b_gate, pool_mix=m_pool_mix, pool_scale=m_pool_scale,
                conv_w=m_conv_w, q_gain=m_q_gain, k_gain=m_k_gain, w_pool_up=m_w_pool_up, w_conv_out=m_w_conv_out,
                w_attn_up=m_w_attn_up, w_o=m_w_o, norm_mlp=m_norm_mlp, w_ff1=m_w_ff1, w_ff2=m_w_ff2)
    vels = dict(norm_mix=v_norm_mix, w_in=v_w_in, b_gate=v_b_gate, pool_mix=v_pool_mix, pool_scale=v_pool_scale,
                conv_w=v_conv_w, q_gain=v_q_gain, k_gain=v_k_gain, w_pool_up=v_w_pool_up, w_conv_out=v_w_conv_out,
                w_attn_up=v_w_attn_up, w_o=v_w_o, norm_mlp=v_norm_mlp, w_ff1=v_w_ff1, w_ff2=v_w_ff2)
    depth = norm_mix.shape[0]
    q = 2 * lax.axis_index("x") + lax.axis_index("y")
    for group in (weights, moms, vels):
        group["w_in"] = jnp.swapaxes(group["w_in"], 1, 2)

    assert depth == 2, "the second layer's gather hides behind the first layer's forward, and likewise backward"
    first, rest = BIG[:1], BIG[1:]
    cw_all = _all_to_all_small(_pad8(conv_w))
    bufs = [{n: _cast_place(weights[n], 0, f"cast_{n}_l0") for n in first}]
    a_ssem, a_rsem, a_views, a_token = _gather_start([bufs[0][n] for n in first], "gather_start_l0_in", cw_all)
    bufs[0].update({n: _cast_place(weights[n], 0, f"cast_{n}_l0") for n in rest})
    bufs += [{n: _cast_place(weights[n], l, f"cast_{n}_l{l}") for n in BIG} for l in range(1, depth)]
    b_ssem, b_rsem, b_views, b_token = _gather_start([bufs[0][n] for n in rest], "gather_start_l0_rest", a_token)
    g_ssem, g_rsem, g_views, g_token = _gather_start([bufs[1][n] for n in BIG], "gather_start_l1", b_token)
    conv_w_full = jnp.concatenate(
        [cw_all[2 * p].reshape(-1)[:conv_w.size].reshape(conv_w.shape) for p in range(N_CHIPS)], axis=-1)
    small = dict(weights)
    small["conv_w_full"] = conv_w_full

    def soon_weights(t):
        got = _gather_finish(a_ssem, a_rsem, a_views, t, "gather_wait_l0_in", "gather_forward_l0_in",
                             [bufs[0][n].shape for n in first])
        return _matrix_weights(dict(zip(first, got)))

    def late_weights(t):
        got = _gather_finish(b_ssem, b_rsem, b_views, t, "gather_wait_l0_rest", "gather_forward_l0_rest",
                             [bufs[0][n].shape for n in rest])
        return _matrix_weights(dict(zip(rest, got)))

    wl, saved = [None] * depth, [None] * depth
    h, saved[0], wl[0] = _layer_fwd(x[0], _small_weights(0, small), "l0", after=g_token, soon=soon_weights,
                                    late=late_weights)
    got = _gather_finish(g_ssem, g_rsem, g_views, h, "gather_wait_l1", "gather_forward_l1",
                         [bufs[1][n].shape for n in BIG])
    (dh, loss_row), saved[1], wl[1] = _layer_fwd(
        h, dict(_small_weights(1, small), **_matrix_weights(dict(zip(BIG, got)))), "l1", target=loss_target[0])

    def pair_stage(names, g, tag):
        views = [_halves(_to_chip_major(n, g[n])) for n in names]
        from_sibling = _pair_swap(views, f"grad_pair_swap_{tag}")
        return [_pair_sum(views[k], from_sibling[k], f"pair_sum_{n}_{tag}") for k, n in enumerate(names)]

    mine, other = [{}, {}], [{}, {}]

    def finish(names, l, started, after, tag):
        ssem, rsem, parts, landing, _ = started
        parts, arrived = _chip_exchange_wait(ssem, rsem, parts, landing, after, f"grad_chip_exchange_wait_{tag}")
        got = [_chip_sum(parts[k], arrived[k], f"chip_sum_{n}_{tag}") for k, n in enumerate(names)]
        mine[l].update(zip(names, got))
        other[l].update(zip(names, _pair_send(got, f"grad_pair_send_{tag}")))

    def small_pieces(g):
        return [_pad8(g[n][:3] if n == "conv_w" else g[n]) for n in SMALL]

    def start_small(l):
        return _all_to_all_small_start(jnp.concatenate(small_pieces(grads[l]), axis=0), f"small_grad_exchange_start_l{l}")

    grads, early, small = [None] * depth, {}, [None] * depth
    dh, grads[1] = _layer_bwd(dh, wl[1], saved[1], "l1")
    second = _chip_exchange_start(pair_stage(BIG, grads[1], "l1"), "grad_chip_exchange_start_l1")
    small[1] = start_small(1)

    def start_rest(g):
        early["rest"] = _chip_exchange_start(pair_stage(rest, g, "l0_rest"), "grad_chip_exchange_start_l0_rest")
        return early["rest"][4]

    def start_last(g):
        early["in"] = _chip_exchange_start(pair_stage(first, g, "l0_in"), "grad_chip_exchange_start_l0_in")
        return early["in"][4]

    dh, grads[0] = _layer_bwd(dh, wl[0], saved[0], "l0", after=[second[4], small[1][4]], mid=start_rest,
                              tail=start_last)
    small[0] = start_small(0)
    finish(BIG, 1, second, dh, "l1")
    finish(rest, 0, early["rest"], dh, "l0_rest")
    loss = lax.psum(loss_row[0, 0], ("x", "y", "c"))
    full = {}

    deltas, new_m, new_v = {}, {}, {}

    def update_matrix(n):
        full[n], deltas[n], new_m[n], new_v[n] = _adamw_halves(
            weights[n], moms[n], vels[n], [mine[l][n] for l in range(depth)], [other[l][n] for l in range(depth)],
            f"adamw_{n}")

    for n in rest:
        update_matrix(n)
    finish(first, 0, early["in"], deltas[rest[-1]], "l0_in")
    for n in first:
        update_matrix(n)
    summed = []
    for l in range(depth):
        ssem, rsem, part, landing, _ = small[l]
        summed.append(_sum_slices(_all_to_all_small_wait(ssem, rsem, part, landing, deltas[first[-1]],
                                                         f"small_grad_exchange_wait_l{l}"), f"small_sum_l{l}"))
    row = 0
    for n, piece in zip(SMALL, small_pieces(grads[0])):
        size = (weights[n].size if n != "conv_w" else depth * 3 * 512) // depth
        flat = jnp.stack([s[row:row + piece.shape[0]].reshape(-1)[:size] for s in summed])
        row += piece.shape[0]
        if n == "conv_w":
            full[n] = lax.dynamic_slice_in_dim(flat.reshape(depth, 3, 512), q * conv_w.shape[2], conv_w.shape[2], axis=2)
        else:
            full[n] = flat.reshape(weights[n].shape)
    for n in SMALL:
        shape = weights[n].shape
        two_d = (-1, shape[-1]) if n not in ("conv_w", "q_gain", "k_gain") else (1, -1)
        d2, m2, v2 = _adamw(weights[n].reshape(two_d), full[n].reshape(two_d), moms[n].reshape(two_d),
                            vels[n].reshape(two_d), f"adamw_{n}")
        deltas[n], new_m[n], new_v[n] = d2.reshape(shape), m2.reshape(shape), v2.reshape(shape)
        full[n] = full[n].reshape(shape)
    for group in (full, deltas, new_m, new_v):
        group["w_in"] = jnp.swapaxes(group["w_in"], 1, 2)
    return (loss, dh[None], *[full[n] for n in ORDER], *[deltas[n] for n in ORDER], *[new_m[n] for n in ORDER],
            *[new_v[n] for n in ORDER])
```

```python
import functools

import jax
import jax.numpy as jnp
from jax import lax
from jax.experimental import pallas as pl
from jax.experimental.pallas import tpu as pltpu

F32 = jnp.float32
BF = jnp.bfloat16
MESH_ID = pl.DeviceIdType.MESH
ANY = pl.BlockSpec(memory_space=pl.ANY)

EPS = 1e-6
MASK_VALUE = -1e30
POOL_WINDOWS = (2, 4, 8, 16)
ATTN_DILATIONS = (1, 4, 16)
ATTN_BLOCK = 128
HEAD_DIM = 64
OFF_Q, OFF_K, OFF_V, OFF_GATE = 2048, 2816, 3584, 4352
N_CHIPS = 4
ADAM_LR, ADAM_B1, ADAM_B2, ADAM_EPS, ADAM_WD, ADAM_STEP = 0.001, 0.9, 0.999, 1e-08, 0.01, 10

VMEM_LIMIT = 48 * 1024 * 1024
LANES = 128

_DIMS = {"nn": (((1,), (0,)), ((), ())), "nt": (((1,), (1,)), ((), ())), "tn": (((0,), (0,)), ((), ()))}


def _params(sem):
    return pltpu.CompilerParams(dimension_semantics=sem, vmem_limit_bytes=VMEM_LIMIT)


def _pallas_call(body, **kw):
    def in_hbm(s):
        pin = isinstance(s, jax.ShapeDtypeStruct) and s is not TOKEN and jnp.issubdtype(s.dtype, jnp.floating)
        return pltpu.HBM(s.shape, s.dtype) if pin else s

    out_shape = kw.pop("out_shape")
    kw["out_shape"] = [in_hbm(s) for s in out_shape] if isinstance(out_shape, (list, tuple)) else in_hbm(out_shape)
    call = pl.pallas_call(body, **kw)

    def run(*args):
        pinned = [pltpu.with_memory_space_constraint(a, pltpu.HBM)
                  if hasattr(a, "dtype") and jnp.issubdtype(a.dtype, jnp.floating) else a for a in args]
        return call(*pinned)

    return run


def _dot(a, b, mode="nn"):
    return lax.dot_general(a, b, _DIMS[mode], preferred_element_type=F32)


def _mm(a, b, mode, name, *, tm, tn, tk, out_dtype=F32, res=None, aux=None, epi=None, n_outer=False,
        b_shards=False, out_shards=False, after=None, vec=None):
    if mode == "tn":
        K, M = a.shape
    else:
        M, K = a.shape
    if b_shards:
        if mode == "nn":
            assert b.shape[1] == K
            N = b.shape[2] * N_CHIPS
        else:
            assert mode == "nt"
            N = b.shape[1]
            assert b.shape[2] * N_CHIPS == K
    else:
        N = b.shape[0] if mode == "nt" else b.shape[1]
    tm, tn, tk = min(tm, M), min(tn, N), min(tk, K)
    assert M % tm == 0 and N % tn == 0 and K % tk == 0
    nk = K // tk
    if n_outer:
        grid = (N // tn, M // tm, nk)
        ij = lambda p, q_: (q_, p)
    else:
        grid = (M // tm, N // tn, nk)
        ij = lambda p, q_: (p, q_)

    def amap(p, q_, k):
        i, j = ij(p, q_)
        return (k, i) if mode == "tn" else (i, k)

    a_spec = pl.BlockSpec((tk, tm) if mode == "tn" else (tm, tk), amap)
    if b_shards:
        if mode == "nn":
            per = (N // N_CHIPS) // tn
            assert per >= 1 and (N // N_CHIPS) % tn == 0

            def bmap(p, q_, k):
                i, j = ij(p, q_)
                return (j // per, k, j % per)

            b_spec = pl.BlockSpec((None, tk, tn), bmap)
        else:
            per = (K // N_CHIPS) // tk
            assert per >= 1 and (K // N_CHIPS) % tk == 0

            def bmap(p, q_, k):
                i, j = ij(p, q_)
                return (k // per, j, k % per)

            b_spec = pl.BlockSpec((None, tn, tk), bmap)
    else:
        def bmap(p, q_, k):
            i, j = ij(p, q_)
            return (j, k) if mode == "nt" else (k, j)

        b_spec = pl.BlockSpec((tn, tk) if mode == "nt" else (tk, tn), bmap)

    def omap(p, q_, k):
        return ij(p, q_)

    o_spec = pl.BlockSpec((tm, tn), omap)
    if out_shards:
        per_o = (N // N_CHIPS) // tn
        assert per_o >= 1 and (N // N_CHIPS) % tn == 0

        def osmap(p, q_, k):
            i, j = ij(p, q_)
            return (j // per_o, i, j % per_o)

        out_spec0 = pl.BlockSpec((None, tm, tn), osmap)
        out_shape0 = jax.ShapeDtypeStruct((N_CHIPS, M, N // N_CHIPS), out_dtype)
    else:
        out_spec0 = o_spec
        out_shape0 = jax.ShapeDtypeStruct((M, N), out_dtype)

    in_specs = [a_spec, b_spec]
    args = [a, b]
    if res is not None:
        in_specs.append(o_spec)
        args.append(res)
    if aux is not None:
        in_specs.append(o_spec)
        args.append(aux)
    if vec is not None:
        in_specs.append(pl.BlockSpec((1, tn), lambda p, q_, k: (0, ij(p, q_)[1])))
        args.append(vec)
    after = [] if after is None else list(after) if isinstance(after, (list, tuple)) else [after]
    in_specs += [ANY] * len(after)
    args += after
    out_specs = [out_spec0]
    out_shape = [out_shape0]
    reduces = epi in ("loss", "rms_bwd")
    if reduces:
        assert tn == N and not n_outer and not out_shards
        width = LANES if epi == "loss" else N
        out_specs.append(pl.BlockSpec((1, width), lambda p, q_, k: (0, 0)))
        out_shape.append(jax.ShapeDtypeStruct((1, width), F32))
    n_out = len(out_shape)
    has_res, has_aux, has_vec, n_after = res is not None, aux is not None, vec is not None, len(after)

    def body(*refs):
        a_ref, b_ref = refs[0], refs[1]
        pos = 2
        res_ref = aux_ref = vec_ref = None
        if has_res:
            res_ref = refs[pos]
            pos += 1
        if has_aux:
            aux_ref = refs[pos]
            pos += 1
        if has_vec:
            vec_ref = refs[pos]
            pos += 1
        pos += n_after
        outs = refs[pos:pos + n_out]
        part = _dot(a_ref[...].astype(BF), b_ref[...].astype(BF), mode)

        first_row_tile = pl.program_id(0) == 0

        def add_to_sum(row):
            @pl.when(first_row_tile)
            def _():
                outs[1][...] = jnp.zeros_like(outs[1])

            outs[1][...] += row

        def finish(acc):
            if epi == "rms_bwd":
                xv = aux_ref[...]
                r = lax.rsqrt(jnp.mean(xv * xv, axis=-1, keepdims=True) + EPS)
                xhat = xv * r
                dy = acc * vec_ref[...]
                outs[0][...] = res_ref[...] + r * (dy - xhat * jnp.mean(dy * xhat, axis=-1, keepdims=True))
                add_to_sum(jnp.sum(acc * xhat, axis=0, keepdims=True))
                return
            if res_ref is not None:
                acc = res_ref[...] + acc
            if epi == "relu2":
                r = jnp.maximum(acc, 0.0)
                outs[0][...] = (r * r).astype(out_dtype)
            elif epi == "drelu2":
                outs[0][...] = (acc.astype(BF) * (2.0 * jnp.sqrt(aux_ref[...]))).astype(out_dtype)
            elif epi == "loss":
                e = acc - aux_ref[...]
                outs[0][...] = e / float(N)
                add_to_sum(0.5 * jnp.sum(jnp.mean(e * e, axis=-1, keepdims=True)))
            else:
                outs[0][...] = acc.astype(out_dtype)

        if nk == 1:
            finish(part)
        else:
            acc_ref = refs[pos + n_out]
            k = pl.program_id(2)

            @pl.when(k == 0)
            def _():
                acc_ref[...] = part

            @pl.when(k > 0)
            def _():
                acc_ref[...] += part

            @pl.when(k == nk - 1)
            def _():
                finish(acc_ref[...])

    scratch = [pltpu.VMEM((tm, tn), F32)] if nk > 1 else []
    out = _pallas_call(
        body, name=name, grid=grid, in_specs=in_specs, out_specs=out_specs, out_shape=out_shape,
        scratch_shapes=scratch,
        compiler_params=_params(("arbitrary" if reduces else "parallel", "parallel", "arbitrary")),
    )(*args)
    return out if n_out > 1 else out[0]


def _rms_fwd(x, gain, name, after=None):
    T, D = x.shape
    tm = min(512, T)

    def body(x_ref, g_ref, *rest):
        o_ref = rest[-1]
        xv = x_ref[...]
        r = lax.rsqrt(jnp.mean(xv * xv, axis=-1, keepdims=True) + EPS)
        o_ref[...] = ((xv * r) * g_ref[...]).astype(BF)

    extra = [] if after is None else list(after) if isinstance(after, (list, tuple)) else [after]
    return _pallas_call(
        body, name=name, grid=(T // tm,),
        in_specs=[pl.BlockSpec((tm, D), lambda i: (i, 0)), pl.BlockSpec((1, D), lambda i: (0, 0))] + [ANY] * len(extra),
        out_specs=pl.BlockSpec((tm, D), lambda i: (i, 0)), out_shape=jax.ShapeDtypeStruct((T, D), BF),
        compiler_params=_params(("parallel",)),
    )(x, gain, *extra)


def _rms_bwd(dh, x, gain, dres, name):
    T, D = x.shape
    tm = min(512, T)

    def body(dh_ref, x_ref, g_ref, dres_ref, dx_ref, dg_ref):
        xv = x_ref[...]
        r = lax.rsqrt(jnp.mean(xv * xv, axis=-1, keepdims=True) + EPS)
        xhat = xv * r
        dhv = dh_ref[...]
        dy = dhv * g_ref[...]
        dx_ref[...] = dres_ref[...] + r * (dy - xhat * jnp.mean(dy * xhat, axis=-1, keepdims=True))

        @pl.when(pl.program_id(0) == 0)
        def _():
            dg_ref[...] = jnp.zeros_like(dg_ref)

        dg_ref[...] += jnp.sum(dhv * xhat, axis=0, keepdims=True)

    row = pl.BlockSpec((tm, D), lambda i: (i, 0))
    vec = pl.BlockSpec((1, D), lambda i: (0, 0))
    return _pallas_call(
        body, name=name, grid=(T // tm,), in_specs=[row, row, vec, row], out_specs=[row, vec],
        out_shape=[jax.ShapeDtypeStruct((T, D), F32), jax.ShapeDtypeStruct((1, D), F32)],
        compiler_params=_params(("arbitrary",)),
    )(dh, x, gain, dres)


POOL_HALO = 16
CONV_HALO = 8
POOLCONV_ROWS = 512


def _causal_window_sum(v, w):
    s, sh = v, 1
    while sh < w:
        s = s + pltpu.roll(s, sh, 0)
        sh *= 2
    return s


def _anticausal_window_sum(v, w):
    n = v.shape[0]
    s, sh = v, 1
    while sh < w:
        s = s + pltpu.roll(s, n - sh, 0)
        sh *= 2
    return s


def _poolconv_fwd(z, pmix_b, pscale, convw, name):
    T = z.shape[0]
    R = min(POOLCONV_ROWS, T)
    PH, CH = R // POOL_HALO, R // CONV_HALO

    def body(u_ref, uh_ref, b_ref, c_ref, ch_ref, x_ref, xh_ref, mix_ref, sc_ref, cw_ref, yp_ref, yc_ref):
        i = pl.program_id(0)
        keep = (i > 0).astype(F32)
        row = i * R + lax.broadcasted_iota(jnp.int32, (R, 1), 0)
        w_all = jnp.concatenate([uh_ref[...] * keep, u_ref[...]], axis=0)
        for g, w in enumerate(POOL_WINDOWS):
            cols = slice(128 * g, 128 * (g + 1))
            wg = w_all[:, cols]
            s = _causal_window_sum(wg, w)[POOL_HALO:]
            inv_cnt = 1.0 / jnp.minimum(row + 1, w).astype(F32)
            dgrp = s * inv_cnt - wg[POOL_HALO:]
            y = _dot(dgrp.astype(BF), mix_ref[g]) * sc_ref[:, cols]
            yp_ref[:, cols] = y.astype(BF)
        uc = jnp.concatenate([ch_ref[...] * xh_ref[...] * keep, c_ref[...] * x_ref[...]], axis=0)
        yc = cw_ref[2:3, :] * uc + cw_ref[0:1, :] * pltpu.roll(uc, 2, 0) + cw_ref[1:2, :] * pltpu.roll(uc, 1, 0)
        yc_ref[...] = (b_ref[...] * yc[CONV_HALO:]).astype(BF)

    def main(cb):
        return pl.BlockSpec((R, 512), lambda i: (i, cb))

    def prev(cb, halo, per):
        return pl.BlockSpec((halo, 512), lambda i: (jnp.maximum(i * per - 1, 0), cb))

    full = lambda a: pl.BlockSpec(a.shape, lambda i: (0,) * a.ndim)
    return _pallas_call(
        body, name=name, grid=(T // R,),
        in_specs=[main(0), prev(0, POOL_HALO, PH), main(1), main(2), prev(2, CONV_HALO, CH), main(3),
                  prev(3, CONV_HALO, CH), full(pmix_b), full(pscale), full(convw)],
        out_specs=[pl.BlockSpec((R, 512), lambda i: (i, 0))] * 2,
        out_shape=[jax.ShapeDtypeStruct((T, 512), BF)] * 2,
        compiler_params=_params(("parallel",)),
    )(z, z, z, z, z, z, z, pmix_b, pscale, convw)


def _poolconv_bwd(z, dyp, dyc, pmix_b, pscale, convw, dz, name):
    T = z.shape[0]
    R = min(POOLCONV_ROWS, T)
    PH, CH = R // POOL_HALO, R // CONV_HALO
    nsteps = T // R

    def body(u_ref, uh_ref, b_ref, bn_ref, c_ref, ch_ref, x_ref, xh_ref, dyp_ref, dypn_ref, dyc_ref, dycn_ref,
             mix_ref, sc_ref, cw_ref, dz_in_ref, dz_ref, dmix_ref, dsc_ref, dcw_ref):
        i = pl.program_id(0)
        keep_prev = (i > 0).astype(F32)
        keep_next = (i < nsteps - 1).astype(F32)

        @pl.when(i == 0)
        def _():
            dmix_ref[...] = jnp.zeros_like(dmix_ref)
            dsc_ref[...] = jnp.zeros_like(dsc_ref)
            dcw_ref[...] = jnp.zeros_like(dcw_ref)

        row = i * R + lax.broadcasted_iota(jnp.int32, (R, 1), 0)
        row_ext = i * R + lax.broadcasted_iota(jnp.int32, (R + POOL_HALO, 1), 0)
        w_all = jnp.concatenate([uh_ref[...] * keep_prev, u_ref[...]], axis=0)
        dyp_ext = jnp.concatenate([dyp_ref[...], dypn_ref[...] * keep_next], axis=0)
        for g, w in enumerate(POOL_WINDOWS):
            cols = slice(128 * g, 128 * (g + 1))
            wg = w_all[:, cols]
            s = _causal_window_sum(wg, w)[POOL_HALO:]
            inv_cnt = 1.0 / jnp.minimum(row + 1, w).astype(F32)
            dgrp = (s * inv_cnt - wg[POOL_HALO:]).astype(BF)
            y_pre = _dot(dgrp, mix_ref[g])
            dsc_ref[:, cols] += jnp.sum(dyp_ref[:, cols] * y_pre, axis=0, keepdims=True)
            dyb = (dyp_ext[:, cols] * sc_ref[:, cols]).astype(BF)
            dmix_ref[cols, :] += _dot(dgrp, dyb[:R], "tn")
            dd = _dot(dyb, mix_ref[g], "nt")
            inv_cnt_ext = 1.0 / jnp.minimum(row_ext + 1, w).astype(F32)
            e = _anticausal_window_sum(dd * inv_cnt_ext, w)
            dz_ref[:, cols] = (e[:R] - dd[:R]).astype(BF)
        cw0, cw1, cw2 = cw_ref[0:1, :], cw_ref[1:2, :], cw_ref[2:3, :]
        uc = jnp.concatenate([ch_ref[...] * xh_ref[...] * keep_prev, c_ref[...] * x_ref[...]], axis=0)
        uc1 = pltpu.roll(uc, 1, 0)[CONV_HALO:]
        uc2 = pltpu.roll(uc, 2, 0)[CONV_HALO:]
        uc0 = uc[CONV_HALO:]
        yc = cw2 * uc0 + cw0 * uc2 + cw1 * uc1
        dycv = dyc_ref[...]
        dz_ref[:, 512:1024] = (dycv * yc).astype(BF)
        dv_ext = jnp.concatenate([dycv * b_ref[...], dycn_ref[...] * bn_ref[...] * keep_next], axis=0)
        n_ext = R + CONV_HALO
        duc = (cw2 * dv_ext + cw1 * pltpu.roll(dv_ext, n_ext - 1, 0) + cw0 * pltpu.roll(dv_ext, n_ext - 2, 0))[:R]
        dv = dv_ext[:R]
        dcw_ref[0:1, :] += jnp.sum(dv * uc2, axis=0, keepdims=True)
        dcw_ref[1:2, :] += jnp.sum(dv * uc1, axis=0, keepdims=True)
        dcw_ref[2:3, :] += jnp.sum(dv * uc0, axis=0, keepdims=True)
        dz_ref[:, 1024:1536] = (duc * x_ref[...]).astype(BF)
        dz_ref[:, 1536:2048] = (duc * c_ref[...]).astype(BF)

    def main(cb):
        return pl.BlockSpec((R, 512), lambda i: (i, cb))

    def prev(cb, halo, per):
        return pl.BlockSpec((halo, 512), lambda i: (jnp.maximum(i * per - 1, 0), cb))

    def nxt(cb, halo, per):
        return pl.BlockSpec((halo, 512), lambda i: (jnp.minimum((i + 1) * per, T // halo - 1), cb))

    full = lambda a: pl.BlockSpec(a.shape, lambda i: (0,) * a.ndim)
    return _pallas_call(
        body, name=name, grid=(nsteps,),
        in_specs=[main(0), prev(0, POOL_HALO, PH), main(1), nxt(1, CONV_HALO, CH), main(2), prev(2, CONV_HALO, CH),
                  main(3), prev(3, CONV_HALO, CH), main(0), nxt(0, POOL_HALO, PH), main(0), nxt(0, CONV_HALO, CH),
                  full(pmix_b), full(pscale), full(convw), ANY],
        out_specs=[pl.BlockSpec((R, 2048), lambda i: (i, 0)), pl.BlockSpec((512, 128), lambda i: (0, 0)),
                   pl.BlockSpec((1, 512), lambda i: (0, 0)), pl.BlockSpec((8, 512), lambda i: (0, 0))],
        out_shape=[jax.ShapeDtypeStruct(dz.shape, BF), jax.ShapeDtypeStruct((512, 128), F32),
                   jax.ShapeDtypeStruct((1, 512), F32), jax.ShapeDtypeStruct((8, 512), F32)],
        input_output_aliases={15: 0}, compiler_params=_params(("arbitrary",)),
    )(z, z, z, z, z, z, z, z, dyp, dyp, dyc, dyc, pmix_b, pscale, convw, dz)


def _head_sums(v):
    row = lax.broadcasted_iota(jnp.int32, (LANES, LANES), 0) < HEAD_DIM
    col = lax.broadcasted_iota(jnp.int32, (LANES, LANES), 1) < HEAD_DIM
    same_head = jnp.where(jnp.logical_xor(row, col), 0.0, 1.0).astype(BF)
    hi = v.astype(BF)
    lo = (v - hi.astype(F32)).astype(BF)
    return _dot(hi, same_head) + _dot(lo, same_head)


def _head_norm(x, g2, ma):
    r = lax.rsqrt(_head_sums(x * x) / HEAD_DIM + EPS)
    return x * r, r


def _head_norm_bwd(dy, xhat, r, g2, ma):
    dxh = dy * g2
    return r * (dxh - xhat * (_head_sums(dxh * xhat) / HEAD_DIM))


def _attn_masks(other_block_exists):
    lane = lax.broadcasted_iota(jnp.int32, (2 * ATTN_BLOCK, ATTN_BLOCK), 1)
    qi = lax.broadcasted_iota(jnp.int32, (2 * ATTN_BLOCK, ATTN_BLOCK), 0) & (ATTN_BLOCK - 1)
    never = (1 - other_block_exists.astype(jnp.int32)) * (2 * ATTN_BLOCK)
    return lane[:ATTN_BLOCK] < HEAD_DIM, lane <= qi, lane >= qi + never


def _stack_heads(x, ma):
    return jnp.concatenate([jnp.where(ma, x, 0.0), jnp.where(ma, 0.0, x)], axis=0)


def _unstack_heads(y, ma):
    return jnp.where(ma, y[:ATTN_BLOCK], y[ATTN_BLOCK:])


def _stack_cols(tile, ma):
    return jnp.concatenate([tile[:, 0:1], tile[:, HEAD_DIM:HEAD_DIM + 1]], axis=0)


QKV_TILES = (OFF_GATE - OFF_Q) // LANES
KIND_TILES = QKV_TILES // 3


def _qk_norm(z, gains, name):
    T = z.shape[0]
    tm = min(512, T)

    def body(x_ref, g_ref, o_ref):
        ma = lax.broadcasted_iota(jnp.int32, (tm, LANES), 1) < HEAD_DIM
        for tile in range(QKV_TILES):
            v = x_ref[:, LANES * tile:LANES * (tile + 1)]
            if tile < 2 * KIND_TILES:
                g = g_ref[0:1, :] if tile < KIND_TILES else g_ref[1:2, :]
                v = _head_norm(v, g, ma)[0] * g
            o_ref[tile] = v

    return _pallas_call(
        body, name=name, grid=(T // tm,),
        in_specs=[pl.BlockSpec((pl.Element(tm), pl.Element(OFF_GATE - OFF_Q)), lambda i: (i * tm, OFF_Q)),
                  pl.BlockSpec((8, LANES), lambda i: (0, 0))],
        out_specs=pl.BlockSpec((QKV_TILES, tm, LANES), lambda i: (0, i, 0)),
        out_shape=jax.ShapeDtypeStruct((QKV_TILES, T, LANES), F32), compiler_params=_params(("parallel",)),
    )(z, gains)


ATTN_STEP_ROWS = 1024
ATTN_UNROLL = 4


def _attn_geometry(T, d):
    sub = ATTN_BLOCK * d
    nb = T // sub
    m = max(1, min(nb, ATTN_STEP_ROWS // sub))
    assert T % sub == 0 and nb % m == 0
    return sub, nb, m


def _attn_rows(jj, r, sub, d):
    start = jj * sub + r
    if d == 1:
        return pl.ds(pl.multiple_of(start, ATTN_BLOCK), ATTN_BLOCK)
    return pl.ds(start, ATTN_BLOCK, stride=d)


def _pick(flag, a, b):
    return jnp.where(jnp.full(a.shape, flag.astype(jnp.int32)) > 0, a, b)


def _attn_fwd(qkv, g, d, name):
    T = qkv.shape[1]
    sub, nb, m = _attn_geometry(T, d)
    scale = HEAD_DIM ** -0.5

    def body(q_ref, kc_ref, kp_ref, vc_ref, vp_ref, o_ref, lse_ref):
        jb = pl.program_id(0)

        def step(s, carry):
            jj, r = s // d, s % d
            here, before = _attn_rows(jj, r, sub, d), _attn_rows(jnp.maximum(jj - 1, 0), r, sub, d)
            edge = _attn_rows(0, r, sub, d)
            first = jj == 0
            ma, mask_c, mask_p = _attn_masks(jb * m + jj > 0)
            qs = _stack_heads(q_ref[here, :], ma).astype(BF)
            kcb = kc_ref[here, :].astype(BF)
            kpb = _pick(first, kp_ref[edge, :], kc_ref[before, :]).astype(BF)
            vcb = vc_ref[here, :].astype(BF)
            vpb = _pick(first, vp_ref[edge, :], vc_ref[before, :]).astype(BF)
            s_c = jnp.where(mask_c, _dot(qs, kcb, "nt") * scale, MASK_VALUE)
            s_p = jnp.where(mask_p, _dot(qs, kpb, "nt") * scale, MASK_VALUE)
            mx = jnp.maximum(jnp.max(s_c, axis=-1, keepdims=True), jnp.max(s_p, axis=-1, keepdims=True))
            p_c = jnp.exp(s_c - mx)
            p_p = jnp.exp(s_p - mx)
            den = jnp.sum(p_c, axis=-1, keepdims=True) + jnp.sum(p_p, axis=-1, keepdims=True)
            o = (_dot(p_c.astype(BF), vcb) + _dot(p_p.astype(BF), vpb)) / den
            o_ref[here, :] = _unstack_heads(o, ma)
            lse_ref[here, :] = _unstack_heads(jnp.broadcast_to(mx + jnp.log(den), o.shape), ma)
            return carry

        lax.fori_loop(0, m * d, step, 0, unroll=ATTN_UNROLL)

    def cur(kind):
        return pl.BlockSpec((None, m * sub, LANES), lambda j, t: (KIND_TILES * kind + 2 * g + t, j, 0))

    def prv(kind):
        return pl.BlockSpec((None, sub, LANES), lambda j, t: (KIND_TILES * kind + 2 * g + t, jnp.maximum(j * m - 1, 0), 0))

    out = pl.BlockSpec((m * sub, LANES), lambda j, t: (j, t))
    return _pallas_call(
        body, name=name, grid=(nb // m, 2), in_specs=[cur(0), cur(1), prv(1), cur(2), prv(2)],
        out_specs=[out, out], out_shape=[jax.ShapeDtypeStruct((T, 256), F32)] * 2,
        compiler_params=_params(("parallel", "parallel")),
    )(qkv, qkv, qkv, qkv, qkv)


def _attn_bwd(z, qkv, do, c, lse, gains, g, d, name, after=None):
    T = z.shape[0]
    sub, nb, m = _attn_geometry(T, d)
    scale = HEAD_DIM ** -0.5
    extra = [] if after is None else [after]

    def body(qr_ref, kr_ref, vc_ref, vp_ref, qn_ref, qnn_ref, kn_ref, knp_ref, do_ref, don_ref, c_ref, cn_ref,
             lse_ref, lsen_ref, g_ref, *rest):
        dq_ref, dk_ref, dv_ref, dgq_ref, dgk_ref, sq_ref, sk_ref, sv_ref = rest[len(extra):]
        jb = pl.program_id(0)

        @pl.when((jb == 0) & (pl.program_id(1) == 0))
        def _():
            dgq_ref[...] = jnp.zeros_like(dgq_ref)
            dgk_ref[...] = jnp.zeros_like(dgk_ref)

        gq, gk = g_ref[0:1, :], g_ref[1:2, :]

        def step(s, carry):
            jj, r = s // d, s % d
            here, edge = _attn_rows(jj, r, sub, d), _attn_rows(0, r, sub, d)
            before = _attn_rows(jnp.maximum(jj - 1, 0), r, sub, d)
            behind = _attn_rows(jnp.minimum(jj + 1, m - 1), r, sub, d)
            first, last = jj == 0, jj == m - 1
            block = jb * m + jj
            ma, mask_c, mask_p = _attn_masks(block > 0)
            mask_n = _attn_masks(block < nb - 1)[2]
            qhat, rq = _head_norm(qr_ref[here, :], gq, ma)
            qn = qn_ref[here, :]
            qn_next = _pick(last, qnn_ref[edge, :], qn_ref[behind, :])
            khat, rk = _head_norm(kr_ref[here, :], gk, ma)
            kcb = kn_ref[here, :].astype(BF)
            kpb = _pick(first, knp_ref[edge, :], kn_ref[before, :]).astype(BF)
            vcb = vc_ref[here, :].astype(BF)
            vpb = _pick(first, vp_ref[edge, :], vc_ref[before, :]).astype(BF)
            do_t, don_t = do_ref[here, :], _pick(last, don_ref[edge, :], do_ref[behind, :])
            c_t, cn_t = c_ref[here, :], _pick(last, cn_ref[edge, :], c_ref[behind, :])
            lse_t, lsen_t = lse_ref[here, :], _pick(last, lsen_ref[edge, :], lse_ref[behind, :])
            qs, dos = _stack_heads(qn, ma).astype(BF), _stack_heads(do_t, ma).astype(BF)
            lse_s, c_s = _stack_cols(lse_t, ma), _stack_cols(c_t, ma)
            s_c = jnp.where(mask_c, _dot(qs, kcb, "nt") * scale, MASK_VALUE)
            s_p = jnp.where(mask_p, _dot(qs, kpb, "nt") * scale, MASK_VALUE)
            p_c = jnp.exp(s_c - lse_s)
            p_p = jnp.exp(s_p - lse_s)
            ds_c = ((p_c * (_dot(dos, vcb, "nt") + c_s)) * scale).astype(BF)
            ds_p = ((p_p * (_dot(dos, vpb, "nt") + c_s)) * scale).astype(BF)
            dq_t = _unstack_heads(_dot(ds_c, kcb) + _dot(ds_p, kpb), ma)
            qs_n, dos_n = _stack_heads(qn_next, ma).astype(BF), _stack_heads(don_t, ma).astype(BF)
            s_n = jnp.where(mask_n, _dot(qs_n, kcb, "nt") * scale, MASK_VALUE)
            p_n = jnp.exp(s_n - _stack_cols(lsen_t, ma))
            ds_n = ((p_n * (_dot(dos_n, vcb, "nt") + _stack_cols(cn_t, ma))) * scale).astype(BF)
            dv_t = _dot(p_c.astype(BF), dos, "tn") + _dot(p_n.astype(BF), dos_n, "tn")
            dk_t = _dot(ds_c, qs, "tn") + _dot(ds_n, qs_n, "tn")
            sq_ref[here, :] = _head_norm_bwd(dq_t, qhat, rq, gq, ma)
            sk_ref[here, :] = _head_norm_bwd(dk_t, khat, rk, gk, ma)
            sv_ref[here, :] = dv_t
            dgq_ref[...] += jnp.sum(dq_t * qhat, axis=0, keepdims=True)
            dgk_ref[...] += jnp.sum(dk_t * khat, axis=0, keepdims=True)
            return carry

        lax.fori_loop(0, m * d, step, 0, unroll=ATTN_UNROLL)
        dq_ref[...] = sq_ref[...].astype(BF)
        dk_ref[...] = sk_ref[...].astype(BF)
        dv_ref[...] = sv_ref[...].astype(BF)

    def raw(col0):
        return pl.BlockSpec((m * sub, LANES), lambda j, t: (j, col0 + 2 * g + t))

    def cur(kind):
        return pl.BlockSpec((None, m * sub, LANES), lambda j, t: (KIND_TILES * kind + 2 * g + t, j, 0))

    def prv(kind):
        return pl.BlockSpec((None, sub, LANES), lambda j, t: (KIND_TILES * kind + 2 * g + t, jnp.maximum(j * m - 1, 0), 0))

    def nxt(kind):
        return pl.BlockSpec((None, sub, LANES),
                            lambda j, t: (KIND_TILES * kind + 2 * g + t, jnp.minimum((j + 1) * m, nb - 1), 0))

    own = pl.BlockSpec((m * sub, LANES), lambda j, t: (j, t))
    own_next = pl.BlockSpec((sub, LANES), lambda j, t: (jnp.minimum((j + 1) * m, nb - 1), t))
    vec = pl.BlockSpec((1, LANES), lambda j, t: (0, 0))
    return _pallas_call(
        body, name=name, grid=(nb // m, 2),
        in_specs=[raw(OFF_Q // LANES), raw(OFF_K // LANES), cur(2), prv(2), cur(0), nxt(0), cur(1), prv(1), own, own_next,
                  own, own_next,
                  own, own_next, pl.BlockSpec((8, LANES), lambda j, t: (0, 0))] + [ANY] * len(extra),
        out_specs=[own, own, own, vec, vec],
        out_shape=[jax.ShapeDtypeStruct((T, 256), BF)] * 3 + [jax.ShapeDtypeStruct((1, LANES), F32)] * 2,
        scratch_shapes=[pltpu.VMEM((m * sub, LANES), F32)] * 3,
        compiler_params=_params(("arbitrary", "arbitrary")),
    )(z, z, qkv, qkv, qkv, qkv, qkv, qkv, do, do, c, c, lse, lse, gains, *extra)


MERGE_ROWS = 256
GATE_TILE = 256


def _group_mix(o_refs, lse_refs):
    lses = [r[...] for r in lse_refs]
    m = jnp.maximum(jnp.maximum(lses[0], lses[1]), lses[2])
    es = [jnp.exp(l - m) for l in lses]
    den = es[0] + es[1] + es[2]
    ws = [e / den for e in es]
    y = ws[0] * o_refs[0][...] + ws[1] * o_refs[1][...] + ws[2] * o_refs[2][...]
    return ws, y


def _sigmoid(v):
    return 1.0 / (1.0 + jnp.exp(-v))


def _merge_specs(T, z, bgate, gpu, gco, gau):
    tm = min(MERGE_ROWS, T)
    row = lambda w: pl.BlockSpec((tm, w), lambda i: (i, 0))
    gate0 = OFF_GATE // GATE_TILE
    gates = [pl.BlockSpec((tm, GATE_TILE), functools.partial(lambda i, cb: (i, cb), cb=gate0 + n))
             for n in range(3 * N_CHIPS)]
    full = lambda a: pl.BlockSpec(a.shape, lambda i: (0,) * a.ndim)
    specs = [row(512), row(512)] + [row(256)] * 6 + gates + [full(bgate), full(gpu), full(gco), full(gau)]
    return tm, row, specs


def _merge_fwd(yp, yc, o3, lse3, z, bgate, gpu, gco, gau, name):
    T = yp.shape[0]
    tm, row, specs = _merge_specs(T, z, bgate, gpu, gco, gau)

    def body(*refs):
        yp_ref, yc_ref = refs[0], refs[1]
        o_refs, lse_refs = refs[2:5], refs[5:8]
        zg = refs[8:20]
        b_ref, gpu_ref, gco_ref, gau_ref, out_ref = refs[20:25]
        yab = _group_mix(o_refs, lse_refs)[1].astype(BF)
        ys = (yp_ref[...], yc_ref[...], yab)
        ups = (gpu_ref, gco_ref, gau_ref)
        for n in range(N_CHIPS):
            acc = None
            for b in range(3):
                gcol = slice(1024 * b + GATE_TILE * n, 1024 * b + GATE_TILE * (n + 1))
                gate = _sigmoid(zg[N_CHIPS * b + n][...] + b_ref[:, gcol])
                term = gate * _dot(ys[b], ups[b][n])
                acc = term if acc is None else acc + term
            out_ref[:, GATE_TILE * n:GATE_TILE * (n + 1)] = acc.astype(BF)

    return _pallas_call(
        body, name=name, grid=(T // tm,), in_specs=specs, out_specs=row(1024),
        out_shape=jax.ShapeDtypeStruct((T, 1024), BF), compiler_params=_params(("parallel",)),
    )(yp, yc, *o3, *lse3, *([z] * 12), bgate, gpu, gco, gau)


def _merge_bwd(dm, yp, yc, o3, lse3, z, bgate, gpu, gco, gau, name):
    T = yp.shape[0]
    tm, row, specs = _merge_specs(T, z, bgate, gpu, gco, gau)
    nsteps = T // tm

    def body(*refs):
        dm_ref, yp_ref, yc_ref = refs[0:3]
        o_refs, lse_refs = refs[3:6], refs[6:9]
        zg = refs[9:21]
        b_ref, gpu_ref, gco_ref, gau_ref = refs[21:25]
        dzg_ref, dyp_ref, dyc_ref = refs[25:28]
        do_refs, c_refs = refs[28:31], refs[31:34]
        dgpu_ref, dgco_ref, dgau_ref, dbg_ref = refs[34:38]
        accs = refs[38:41]
        i = pl.program_id(0)

        @pl.when(i == 0)
        def _():
            for a in accs:
                a[...] = jnp.zeros_like(a)
            dbg_ref[...] = jnp.zeros_like(dbg_ref)

        ws, y = _group_mix(o_refs, lse_refs)
        ys = (yp_ref[...], yc_ref[...], y.astype(BF))
        ups = (gpu_ref, gco_ref, gau_ref)
        dys = [None, None, None]
        for n in range(N_CHIPS):
            dmn = dm_ref[:, GATE_TILE * n:GATE_TILE * (n + 1)]
            for b in range(3):
                gcol = slice(1024 * b + GATE_TILE * n, 1024 * b + GATE_TILE * (n + 1))
                gate = _sigmoid(zg[N_CHIPS * b + n][...] + b_ref[:, gcol])
                up = _dot(ys[b], ups[b][n])
                dzg = (dmn * up) * (gate * (1.0 - gate))
                dzg_ref[:, gcol] = dzg.astype(BF)
                dbg_ref[:, gcol] += jnp.sum(dzg, axis=0, keepdims=True)
                dup = (dmn * gate).astype(BF)
                accs[b][n] += _dot(ys[b], dup, "tn")
                dyb = _dot(dup, ups[b][n], "nt")
                dys[b] = dyb if dys[b] is None else dys[b] + dyb
        dyp_ref[...] = dys[0]
        dyc_ref[...] = dys[1]
        dya = dys[2]
        lane = lax.broadcasted_iota(jnp.int32, dya.shape, 1) // HEAD_DIM
        pr = dya * y
        rho = jnp.zeros_like(pr)
        for h in range(256 // HEAD_DIM):
            hm = lane == h
            rho = jnp.where(hm, jnp.sum(jnp.where(hm, pr, 0.0), axis=-1, keepdims=True), rho)
        for g in range(3):
            do_refs[g][...] = ws[g] * dya
            c_refs[g][...] = -(ws[g] * rho)

        @pl.when(i == nsteps - 1)
        def _():
            dgpu_ref[...] = accs[0][...].astype(BF)
            dgco_ref[...] = accs[1][...].astype(BF)
            dgau_ref[...] = accs[2][...].astype(BF)

    full = lambda a: pl.BlockSpec(a.shape, lambda i: (0,) * a.ndim)
    dz_gate = pl.BlockSpec((pl.Element(tm), pl.Element(3072)), lambda i: (i * tm, OFF_GATE))
    out_specs = ([dz_gate, row(512), row(512)] + [row(256)] * 6 + [full(gpu), full(gco), full(gau)]
                 + [pl.BlockSpec((1, 3072), lambda i: (0, 0))])
    out_shape = ([jax.ShapeDtypeStruct(z.shape, BF)] + [jax.ShapeDtypeStruct((T, 512), F32)] * 2
                 + [jax.ShapeDtypeStruct((T, 256), F32)] * 6
                 + [jax.ShapeDtypeStruct(g.shape, BF) for g in (gpu, gco, gau)]
                 + [jax.ShapeDtypeStruct((1, 3072), F32)])
    return _pallas_call(
        body, name=name, grid=(nsteps,), in_specs=[row(1024)] + specs, out_specs=out_specs, out_shape=out_shape,
        scratch_shapes=[pltpu.VMEM(g.shape, F32) for g in (gpu, gco, gau)],
        compiler_params=_params(("arbitrary",)),
    )(dm, yp, yc, *o3, *lse3, *([z] * 12), bgate, gpu, gco, gau)


def _layer_fwd(x, w, tag, after=None, soon=None, late=None, target=None):
    hb = _rms_fwd(x, w["norm_mix"], f"rms_mix_{tag}", after=after)
    if soon is not None:
        w = dict(w, **soon(hb))
    z = _mm(hb, w["w_in"], "nt", f"in_proj_{tag}", tm=512, tn=3712, tk=1024, n_outer=True)
    yp, yc = _poolconv_fwd(z, w["pool_mix"], w["pool_scale"], w["conv_w"], f"poolconv_{tag}")
    qkv = _qk_norm(z, w["qk_gain"], f"qk_norm_{tag}")
    o3, lse3 = [], []
    for g, d in enumerate(ATTN_DILATIONS):
        o, lse = _attn_fwd(qkv, g, d, f"attn{g}_{tag}")
        o3.append(o)
        lse3.append(lse)
    if late is not None:
        w = dict(w, **late(lse3[-1]))
    merged = _merge_fwd(yp, yc, o3, lse3, z, w["b_gate"], w["w_pool_up"], w["w_conv_out"], w["w_attn_up"],
                        f"merge_{tag}")
    x1 = _mm(merged, w["w_o"], "nn", f"out_proj_{tag}", tm=1024, tn=1024, tk=1024, res=x)
    h2b = _rms_fwd(x1, w["norm_mlp"], f"rms_mlp_{tag}")
    rb = _mm(h2b, w["w_ff1"], "nn", f"ff1_{tag}", tm=1024, tn=1024, tk=1024, out_dtype=BF, epi="relu2", n_outer=True,
             b_shards=True)
    if target is None:
        x2 = _mm(rb, w["w_ff2"], "nn", f"ff2_{tag}", tm=512, tn=1024, tk=4096, res=x1)
    else:
        x2 = _mm(rb, w["w_ff2"], "nn", f"ff2_{tag}", tm=512, tn=1024, tk=4096, res=x1, aux=target, epi="loss")
    saved = dict(x=x, hb=hb, z=z, yp=yp, yc=yc, qkv=qkv, o3=o3, lse3=lse3, merged=merged, x1=x1, h2b=h2b, rb=rb)
    return x2, saved, w


def _layer_bwd(dx2, w, s, tag, after=None, mid=None, tail=None):
    g = {}
    dab = _mm(dx2, w["w_ff2"], "nt", f"d_ff2_act_{tag}", tm=1024, tn=1024, tk=1024, out_dtype=BF, aux=s["rb"],
              epi="drelu2", after=after)
    g["w_ff2"] = _mm(s["rb"], dx2, "tn", f"d_ff2_w_{tag}", tm=1024, tn=1024, tk=2048, out_dtype=BF)
    g["w_ff1"] = _mm(s["h2b"], dab, "tn", f"d_ff1_w_{tag}", tm=1024, tn=1024, tk=2048, out_dtype=BF, out_shards=True)
    dx1, g["norm_mlp"] = _mm(dab, w["w_ff1"], "nt", f"d_ff1_act_{tag}", tm=1024, tn=1024, tk=1024, b_shards=True,
                             res=dx2, aux=s["x1"], vec=w["norm_mlp"], epi="rms_bwd")
    dm = _mm(dx1, w["w_o"], "nt", f"d_out_act_{tag}", tm=1024, tn=1024, tk=1024)
    g["w_o"] = _mm(s["merged"], dx1, "tn", f"d_out_w_{tag}", tm=1024, tn=1024, tk=1024, out_dtype=BF)
    (dz, dyp, dyc, do0, do1, do2, c0, c1, c2, g["w_pool_up"], g["w_conv_out"], g["w_attn_up"],
     g["b_gate"]) = _merge_bwd(dm, s["yp"], s["yc"], s["o3"], s["lse3"], s["z"], w["b_gate"], w["w_pool_up"],
                               w["w_conv_out"], w["w_attn_up"], f"d_merge_{tag}")
    behind = mid(g) if mid is not None else None
    dq, dk, dv = [], [], []
    dgq = dgk = None
    for gi, d in enumerate(ATTN_DILATIONS):
        dzq, dzk, dzv, pq, pk = _attn_bwd(s["z"], s["qkv"], (do0, do1, do2)[gi], (c0, c1, c2)[gi], s["lse3"][gi],
                                          w["qk_gain"], gi, d, f"d_attn{gi}_{tag}", after=behind)
        dq.append(dzq)
        dk.append(dzk)
        dv.append(dzv)
        dgq = pq if dgq is None else dgq + pq
        dgk = pk if dgk is None else dgk + pk
    g["q_gain"] = dgq[:, :HEAD_DIM] + dgq[:, HEAD_DIM:]
    g["k_gain"] = dgk[:, :HEAD_DIM] + dgk[:, HEAD_DIM:]
    for off, pieces in ((OFF_Q, dq), (OFF_K, dk), (OFF_V, dv)):
        for gi, piece in enumerate(pieces):
            dz = lax.dynamic_update_slice(dz, piece, (0, off + 256 * gi))
    dz, g["pool_mix"], g["pool_scale"], g["conv_w"] = _poolconv_bwd(
        s["z"], dyp, dyc, w["pool_mix"], w["pool_scale"], w["conv_w"], dz, f"d_poolconv_{tag}")
    g["w_in"] = _mm(s["hb"], dz, "tn", f"d_in_w_{tag}", tm=512, tn=3712, tk=1024, out_dtype=BF)
    dh = _mm(dz, w["w_in"], "nn", f"d_in_act_{tag}", tm=1024, tn=1024, tk=3712,
             after=tail(g) if tail is not None else None)
    dx, g["norm_mix"] = _rms_bwd(dh, s["x"], w["norm_mix"], dx1, f"d_rms_mix_{tag}")
    return dx, g


def _position():
    x, y, c = lax.axis_index("x"), lax.axis_index("y"), lax.axis_index("c")
    chips = [(1 - x, y), (x, 1 - y), (1 - x, 1 - y)]
    return x, y, c, 2 * x + y, chips, [2 * cx + cy for cx, cy in chips]


def _remote(src, dst, ssem, rsem, dev):
    return pltpu.make_async_remote_copy(src_ref=src, dst_ref=dst, send_sem=ssem, recv_sem=rsem, device_id=dev,
                                        device_id_type=MESH_ID)


def _halves(a):
    return a.reshape(a.shape[0], 2, a.shape[1] // 2, a.shape[2])


SEM = pl.BlockSpec(memory_space=pltpu.SEMAPHORE)
TOKEN = jax.ShapeDtypeStruct((8, LANES), F32)
TOKEN_SPEC = pl.BlockSpec(memory_space=pltpu.VMEM)


def _split_params():
    return pltpu.CompilerParams(has_side_effects=pltpu.SideEffectType.DATAFLOW_SIDE_EFFECTING)


def _gather_start(bufs, name, after):
    n = len(bufs)
    views = [_halves(b) for b in bufs]

    def body(*refs):
        first_sem = n + 1
        ssem, rsem = refs[first_sem:first_sem + ns], refs[first_sem + ns:first_sem + 2 * ns]
        outs, token = refs[first_sem + 2 * ns:first_sem + 2 * ns + n], refs[first_sem + 2 * ns + n]
        x, y, c, q, chips, qs = _position()
        for k in range(n):
            mine = outs[k].at[q, c]
            for j, chip in enumerate(chips):
                _remote(mine, mine, ssem[3 * k + j], rsem[3 * k + j], (chip[0], chip[1], c)).start()
        token[...] = jnp.zeros_like(token)

    ns = 3 * n
    outs = _pallas_call(
        body, name=name, in_specs=[ANY] * (n + 1), out_specs=[SEM] * (2 * ns) + [ANY] * n + [TOKEN_SPEC],
        out_shape=[pltpu.SemaphoreType.DMA(())] * (2 * ns) + [jax.ShapeDtypeStruct(v.shape, v.dtype) for v in views]
        + [TOKEN],
        input_output_aliases={k: k + 2 * ns for k in range(n)}, compiler_params=_split_params(),
    )(*views, after)
    return list(outs[:ns]), list(outs[ns:2 * ns]), list(outs[2 * ns:2 * ns + n]), outs[2 * ns + n]


def _gather_finish(ssem, rsem, views, after, name_wait, name_forward, shapes):
    n = len(views)
    ns = len(ssem)

    def wait_body(*refs):
        ssem_ref, rsem_ref = refs[n:n + ns], refs[n + ns:n + 2 * ns]
        outs = refs[n + 2 * ns + 1:]
        x, y, c, q, chips, qs = _position()
        for k in range(n):
            for j, chip in enumerate(chips):
                cp = _remote(outs[k].at[q, c], outs[k].at[qs[j], c], ssem_ref[3 * k + j], rsem_ref[3 * k + j],
                             (chip[0], chip[1], c))
                cp.wait_send()
                cp.wait_recv()

    landed = _pallas_call(
        wait_body, name=name_wait, in_specs=[ANY] * n + [SEM] * (2 * ns) + [ANY], out_specs=[ANY] * n,
        out_shape=[jax.ShapeDtypeStruct(v.shape, v.dtype) for v in views],
        input_output_aliases={k: k for k in range(n)}, compiler_params=_split_params(),
    )(*views, *ssem, *rsem, after)

    def forward_body(*refs):
        outs = refs[n:2 * n]
        fssem, frsem = refs[2 * n:]
        x, y, c, q, chips, qs = _position()
        sib = (x, y, 1 - c)
        sent = []
        for k in range(n):
            for j in range(3):
                slot = outs[k].at[qs[j], c]
                cp = _remote(slot, slot, fssem.at[k, j], frsem.at[k, j], sib)
                cp.start()
                sent.append(cp)
        for k in range(n):
            for j in range(3):
                slot = outs[k].at[qs[j], 1 - c]
                _remote(slot, slot, fssem.at[k, j], frsem.at[k, j], sib).wait_recv()
        for cp in sent:
            cp.wait_send()

    outs = _pallas_call(
        forward_body, name=name_forward, in_specs=[ANY] * n, out_specs=[ANY] * n,
        out_shape=[jax.ShapeDtypeStruct(v.shape, v.dtype) for v in views],
        input_output_aliases={k: k for k in range(n)}, scratch_shapes=[pltpu.SemaphoreType.DMA((n, 3))] * 2,
    )(*landed)
    return [o.reshape(s) for o, s in zip(outs, shapes)]


def _chip_exchange_start(parts, name):
    n = len(parts)

    def body(*refs):
        ssem, rsem = refs[n:n + ns], refs[n + ns:n + 2 * ns]
        base = n + 2 * ns
        srcs, outs, token = refs[base:base + n], refs[base + n:base + 2 * n], refs[base + 2 * n]
        x, y, c, q, chips, qs = _position()
        for k in range(n):
            for j, chip in enumerate(chips):
                _remote(srcs[k].at[qs[j]], outs[k].at[j], ssem[3 * k + j], rsem[3 * k + j],
                        (chip[0], chip[1], c)).start()
        token[...] = jnp.zeros_like(token)

    ns = 3 * n
    outs = _pallas_call(
        body, name=name, in_specs=[ANY] * n, out_specs=[SEM] * (2 * ns) + [ANY] * (2 * n) + [TOKEN_SPEC],
        out_shape=[pltpu.SemaphoreType.DMA(())] * (2 * ns) + [jax.ShapeDtypeStruct(a.shape, a.dtype) for a in parts]
        + [jax.ShapeDtypeStruct((3,) + a.shape[1:], a.dtype) for a in parts] + [TOKEN],
        input_output_aliases={k: k + 2 * ns for k in range(n)}, compiler_params=_split_params(),
    )(*parts)
    b = 2 * ns
    return list(outs[:ns]), list(outs[ns:b]), list(outs[b:b + n]), list(outs[b + n:b + 2 * n]), outs[b + 2 * n]


def _chip_exchange_wait(ssem, rsem, parts, landing, after, name):
    n = len(parts)
    ns = len(ssem)

    def body(*refs):
        ssem_ref, rsem_ref = refs[2 * n:2 * n + ns], refs[2 * n + ns:2 * n + 2 * ns]
        base = 2 * n + 2 * ns + 1
        srcs, outs = refs[base:base + n], refs[base + n:]
        x, y, c, q, chips, qs = _position()
        for k in range(n):
            for j, chip in enumerate(chips):
                cp = _remote(srcs[k].at[qs[j]], outs[k].at[j], ssem_ref[3 * k + j], rsem_ref[3 * k + j],
                             (chip[0], chip[1], c))
                cp.wait_send()
                cp.wait_recv()

    outs = _pallas_call(
        body, name=name, in_specs=[ANY] * (2 * n) + [SEM] * (2 * ns) + [ANY], out_specs=[ANY] * (2 * n),
        out_shape=[jax.ShapeDtypeStruct(a.shape, a.dtype) for a in list(parts) + list(landing)],
        input_output_aliases={k: k for k in range(2 * n)}, compiler_params=_split_params(),
    )(*parts, *landing, *ssem, *rsem, after)
    return list(outs[:n]), list(outs[n:])


def _pair_swap(views, name):
    n = len(views)

    def body(*refs):
        ins, outs = refs[:n], refs[n:2 * n]
        ssem, rsem = refs[2 * n:]
        x, y, c, _, _, _ = _position()
        cps = [_remote(ins[k].at[pl.ds(0, N_CHIPS), 1 - c], outs[k], ssem.at[k], rsem.at[k], (x, y, 1 - c))
               for k in range(n)]
        for cp in cps:
            cp.start()
        for cp in cps:
            cp.wait()

    return _pallas_call(
        body, name=name, in_specs=[ANY] * n, out_specs=[ANY] * n,
        out_shape=[jax.ShapeDtypeStruct((v.shape[0],) + v.shape[2:], v.dtype) for v in views],
        scratch_shapes=[pltpu.SemaphoreType.DMA((n,))] * 2,
    )(*views)


def _pair_send(arrays, name):
    n = len(arrays)

    def body(*refs):
        ins, outs = refs[:n], refs[n:2 * n]
        ssem, rsem = refs[2 * n:]
        x, y, c, _, _, _ = _position()
        cps = [_remote(ins[k], outs[k], ssem.at[k], rsem.at[k], (x, y, 1 - c)) for k in range(n)]
        for cp in cps:
            cp.start()
        for cp in cps:
            cp.wait()

    return _pallas_call(
        body, name=name, in_specs=[ANY] * n, out_specs=[ANY] * n,
        out_shape=[jax.ShapeDtypeStruct(a.shape, a.dtype) for a in arrays],
        scratch_shapes=[pltpu.SemaphoreType.DMA((n,))] * 2,
    )(*arrays)


def _all_to_all_small(part):
    P = part.shape[0]

    def body(in_ref, out_ref, lsem, ssem, rsem):
        x, y, c = lax.axis_index("x"), lax.axis_index("y"), lax.axis_index("c")
        me = 4 * x + 2 * y + c
        flips = [(fx, fy, fc) for fx in (0, 1) for fy in (0, 1) for fc in (0, 1)][1:]
        peers = [((x + fx) % 2, (y + fy) % 2, (c + fc) % 2) for fx, fy, fc in flips]
        loc = pltpu.make_async_copy(in_ref, out_ref.at[me], lsem)
        loc.start()
        cps = [_remote(in_ref, out_ref.at[me], ssem.at[j], rsem.at[j], peer) for j, peer in enumerate(peers)]
        for cp in cps:
            cp.start()
        for j, (px, py, pc) in enumerate(peers):
            _remote(in_ref, out_ref.at[4 * px + 2 * py + pc], ssem.at[j], rsem.at[j], peers[j]).wait_recv()
        for cp in cps:
            cp.wait_send()
        loc.wait()

    return _pallas_call(
        body, name="small_exchange", in_specs=[ANY], out_specs=ANY,
        out_shape=jax.ShapeDtypeStruct((8, P, LANES), F32),
        scratch_shapes=[pltpu.SemaphoreType.DMA(())] + [pltpu.SemaphoreType.DMA((7,))] * 2,
    )(part)


def _small_peers():
    x, y, c = lax.axis_index("x"), lax.axis_index("y"), lax.axis_index("c")
    flips = [(fx, fy, fc) for fx in (0, 1) for fy in (0, 1) for fc in (0, 1)][1:]
    peers = [((x + fx) % 2, (y + fy) % 2, (c + fc) % 2) for fx, fy, fc in flips]
    return 4 * x + 2 * y + c, peers


def _all_to_all_small_start(part, name):
    P = part.shape[0]
    me = 4 * lax.axis_index("x") + 2 * lax.axis_index("y") + lax.axis_index("c")
    landing = lax.dynamic_update_slice(jnp.zeros((8, P, LANES), F32), part[None], (me, 0, 0))

    def body(*refs):
        sems, src, land, token = refs[2:16], refs[16], refs[17], refs[18]
        me_, peers = _small_peers()
        for j, peer in enumerate(peers):
            _remote(src, land.at[me_], sems[j], sems[7 + j], peer).start()
        token[...] = jnp.zeros_like(token)

    outs = _pallas_call(
        body, name=name, in_specs=[ANY, ANY], out_specs=[SEM] * 14 + [ANY, ANY, TOKEN_SPEC],
        out_shape=[pltpu.SemaphoreType.DMA(())] * 14 + [jax.ShapeDtypeStruct(part.shape, F32),
                                                       jax.ShapeDtypeStruct((8, P, LANES), F32), TOKEN],
        input_output_aliases={0: 14, 1: 15}, compiler_params=_split_params(),
    )(part, landing)
    return list(outs[:7]), list(outs[7:14]), outs[14], outs[15], outs[16]


def _all_to_all_small_wait(ssem, rsem, part, landing, after, name):
    def body(*refs):
        sems, src, land = refs[2:16], refs[17], refs[18]
        _, peers = _small_peers()
        for j, (px, py, pc) in enumerate(peers):
            cp = _remote(src, land.at[4 * px + 2 * py + pc], sems[j], sems[7 + j], peers[j])
            cp.wait_send()
            cp.wait_recv()

    return _pallas_call(
        body, name=name, in_specs=[ANY, ANY] + [SEM] * 14 + [ANY], out_specs=[ANY, ANY],
        out_shape=[jax.ShapeDtypeStruct(part.shape, F32), jax.ShapeDtypeStruct(landing.shape, F32)],
        input_output_aliases={0: 0, 1: 1}, compiler_params=_split_params(),
    )(part, landing, *ssem, *rsem, after)[1]


def _row_tile(rows, width, n_arrays):
    t = rows
    while t % 2 == 0 and t > 8 and 2 * n_arrays * t * width * 4 > VMEM_LIMIT // 2:
        t //= 2
    return t


def _chip():
    return 2 * lax.axis_index("x") + lax.axis_index("y")


def _core():
    return lax.axis_index("c")


def _cast_place(w3, layer, name):
    _, r, c = w3.shape
    tr = _row_tile(r, c, 2)

    def body(w_ref, o_ref):
        o_ref[...] = w_ref[...].astype(BF)

    return _pallas_call(
        body, name=name, grid=(r // tr,), in_specs=[pl.BlockSpec((None, tr, c), lambda i: (layer, i, 0))],
        out_specs=pl.BlockSpec((None, tr, c), lambda i: (_chip(), i, 0)),
        out_shape=jax.ShapeDtypeStruct((N_CHIPS, r, c), BF), compiler_params=_params(("parallel",)),
    )(w3)


def _pair_sum(view, recv, name):
    _, _, hr, c = view.shape
    tr = _row_tile(hr, c, 3)

    def body(g_ref, r_ref, o_ref):
        o_ref[...] = (g_ref[...].astype(F32) + r_ref[...].astype(F32)).astype(BF)

    blk = pl.BlockSpec((None, tr, c), lambda p, i: (p, i, 0))
    return _pallas_call(
        body, name=name, grid=(N_CHIPS, hr // tr),
        in_specs=[pl.BlockSpec((None, None, tr, c), lambda p, i: (p, _core(), i, 0)), blk], out_specs=blk,
        out_shape=jax.ShapeDtypeStruct(recv.shape, BF), compiler_params=_params(("parallel", "parallel")),
    )(view, recv)


def _chip_sum(parts, recv, name):
    _, hr, c = parts.shape
    tr = _row_tile(hr, c, 6)

    def body(p_ref, r_ref, o_ref):
        acc = p_ref[...].astype(F32)
        for j in range(3):
            acc = acc + r_ref[j].astype(F32)
        o_ref[...] = acc

    return _pallas_call(
        body, name=name, grid=(hr // tr,),
        in_specs=[pl.BlockSpec((None, tr, c), lambda i: (_chip(), i, 0)), pl.BlockSpec((3, tr, c), lambda i: (0, i, 0))],
        out_specs=pl.BlockSpec((tr, c), lambda i: (i, 0)),
        out_shape=jax.ShapeDtypeStruct((hr, c), F32), compiler_params=_params(("parallel",)),
    )(parts, recv)


def _sum_slices(a, name):
    n, rows, width = a.shape
    tr = _row_tile(rows, width, n + 1)

    def body(a_ref, o_ref):
        acc = a_ref[0].astype(F32)
        for i in range(1, n):
            acc = acc + a_ref[i].astype(F32)
        o_ref[...] = acc

    return _pallas_call(
        body, name=name, grid=(rows // tr,), in_specs=[pl.BlockSpec((n, tr, width), lambda i: (0, i, 0))],
        out_specs=pl.BlockSpec((tr, width), lambda i: (i, 0)), out_shape=jax.ShapeDtypeStruct((rows, width), F32),
        compiler_params=_params(("parallel",)),
    )(a)


def _adamw_update(w, g, m, v):
    nm = ADAM_B1 * m + (1.0 - ADAM_B1) * g
    nv = ADAM_B2 * v + (1.0 - ADAM_B2) * (g * g)
    m_hat = nm / (1.0 - ADAM_B1 ** ADAM_STEP)
    v_hat = nv / (1.0 - ADAM_B2 ** ADAM_STEP)
    return -ADAM_LR * (m_hat / (jnp.sqrt(v_hat) + ADAM_EPS) + ADAM_WD * w), nm, nv


def _adamw(w, g, m, v, name):
    rows, width = w.shape
    tr = _row_tile(rows, width, 7)

    def body(w_ref, g_ref, m_ref, v_ref, d_ref, nm_ref, nv_ref):
        d_ref[...], nm_ref[...], nv_ref[...] = _adamw_update(w_ref[...], g_ref[...], m_ref[...], v_ref[...])

    blk = pl.BlockSpec((tr, width), lambda i: (i, 0))
    return _pallas_call(
        body, name=name, grid=(rows // tr,), in_specs=[blk] * 4, out_specs=[blk] * 3,
        out_shape=[jax.ShapeDtypeStruct((rows, width), F32)] * 3, compiler_params=_params(("parallel",)),
    )(w, g, m, v)


def _adamw_halves(w3, m3, v3, mine, other, name):
    depth, r, c = w3.shape
    assert depth == 2
    hr = r // 2
    tr = _row_tile(hr, c, 11)
    sources = ((0, True, mine[0]), (0, False, other[0]), (1, True, mine[1]), (1, False, other[1]))

    def active(l, h, layer, own):
        mine_half = h == _core()
        return (l == layer) & (mine_half if own else jnp.logical_not(mine_half))

    def body(w_ref, m_ref, v_ref, *rest):
        g_refs, (go_ref, d_ref, nm_ref, nv_ref) = rest[:4], rest[4:]
        l, h = pl.program_id(0), pl.program_id(1)
        for (layer, own, _), g_ref in zip(sources, g_refs):
            @pl.when(active(l, h, layer, own))
            def _():
                gv = g_ref[...]
                go_ref[...] = gv
                d_ref[...], nm_ref[...], nv_ref[...] = _adamw_update(w_ref[...], gv, m_ref[...], v_ref[...])

    def gspec(layer, own):
        return pl.BlockSpec((tr, c), lambda l, h, i: (jnp.where(active(l, h, layer, own), i, 0), 0))

    blk = pl.BlockSpec((None, None, tr, c), lambda l, h, i: (l, h, i, 0))
    view = lambda a: a.reshape(depth, 2, hr, c)
    outs = _pallas_call(
        body, name=name, grid=(depth, 2, hr // tr),
        in_specs=[blk] * 3 + [gspec(layer, own) for layer, own, _ in sources], out_specs=[blk] * 4,
        out_shape=[jax.ShapeDtypeStruct((depth, 2, hr, c), F32)] * 4,
        compiler_params=_params(("parallel", "parallel", "parallel")),
    )(view(w3), view(m3), view(v3), *[s[2] for s in sources])
    return [o.reshape(w3.shape) for o in outs]


BIG = ("w_in", "w_pool_up", "w_conv_out", "w_attn_up", "w_o", "w_ff1", "w_ff2")
SMALL = ("norm_mix", "b_gate", "pool_mix", "pool_scale", "conv_w", "q_gain", "k_gain", "norm_mlp")
ORDER = ("norm_mix", "w_in", "b_gate", "pool_mix", "pool_scale", "conv_w", "q_gain", "k_gain", "w_pool_up",
         "w_conv_out", "w_attn_up", "w_o", "norm_mlp", "w_ff1", "w_ff2")
COLUMN_SHARDED = ("w_pool_up", "w_conv_out", "w_attn_up", "w_ff1")


def _matrix_weights(gathered):
    w = {}
    for name, g4 in gathered.items():
        if name in COLUMN_SHARDED:
            w[name] = g4
        else:
            w[name] = g4.reshape(N_CHIPS * g4.shape[1], g4.shape[2])
    return w


def _small_weights(l, small):
    w = {}
    w["norm_mix"] = small["norm_mix"][l][None]
    w["norm_mlp"] = small["norm_mlp"][l][None]
    w["b_gate"] = small["b_gate"][l][None]
    w["pool_mix"] = small["pool_mix"][l].astype(BF)
    w["pool_scale"] = small["pool_scale"][l][None]
    w["conv_w"] = jnp.pad(small["conv_w_full"][l], ((0, 5), (0, 0)))
    w["qk_gain"] = jnp.pad(jnp.stack([jnp.tile(small["q_gain"][l], 2), jnp.tile(small["k_gain"][l], 2)]), ((0, 6), (0, 0)))
    return w


def _to_chip_major(name, g):
    if name == "w_in":
        return g.T.reshape(N_CHIPS, g.shape[1] // N_CHIPS, g.shape[0])
    if name in COLUMN_SHARDED:
        return g
    return g.reshape(N_CHIPS, g.shape[0] // N_CHIPS, g.shape[1])


def _pad8(a):
    a = a.reshape(-1)
    return jnp.pad(a, (0, (-a.size) % (8 * LANES))).reshape(-1, LANES)


def kernel(x, norm_mix, w_in, b_gate, pool_mix, pool_scale, conv_w, q_gain, k_gain, w_pool_up, w_conv_out, w_attn_up, w_o, norm_mlp, w_ff1, w_ff2, loss_target, m_norm_mix, m_w_in, m_b_gate, m_pool_mix, m_pool_scale, m_conv_w, m_q_gain, m_k_gain, m_w_pool_up, m_w_conv_out, m_w_attn_up, m_w_o, m_norm_mlp, m_w_ff1, m_w_ff2, v_norm_mix, v_w_in, v_b_gate, v_pool_mix, v_pool_scale, v_conv_w, v_q_gain, v_k_gain, v_w_pool_up, v_w_conv_out, v_w_attn_up, v_w_o, v_norm_mlp, v_w_ff1, v_w_ff2):
    weights = dict(norm_mix=norm_mix, w_in=w_in, b_gate=b_gate, pool_mix=pool_mix, pool_scale=pool_scale, conv_w=conv_w,
                   q_gain=q_gain, k_gain=k_gain, w_pool_up=w_pool_up, w_conv_out=w_conv_out, w_attn_up=w_attn_up,
                   w_o=w_o, norm_mlp=norm_mlp, w_ff1=w_ff1, w_ff2=w_ff2)
    moms = dict(norm_mix=m_norm_mix, w_in=m_w_in, b_gate=m_b_gate, pool_mix=m_pool_mix, pool_scale=m_pool_scale,
                conv_w=m_conv_w, q_gain=m_q_gain, k_gain=m_k_gain, w_pool_up=m_w_pool_up, w_conv_out=m_w_conv_out,
                w_attn_up=m_w_attn_up, w_o=m_w_o, norm_mlp=m_norm_mlp, w_ff1=m_w_ff1, w_ff2=m_w_ff2)
    vels = dict(norm_mix=v_norm_mix, w_in=v_w_in, b_gate=v_b_gate, pool_mix=v_pool_mix, pool_scale=v_pool_scale,
                conv_w=v_conv_w, q_gain=v_q_gain, k_gain=v_k_gain, w_pool_up=v_w_pool_up, w_conv_out=v_w_conv_out,
                w_attn_up=v_w_attn_up, w_o=v_w_o, norm_mlp=v_norm_mlp, w_ff1=v_w_ff1, w_ff2=v_w_ff2)
    depth = norm_mix.shape[0]
    q = 2 * lax.axis_index("x") + lax.axis_index("y")
    for group in (weights, moms, vels):
        group["w_in"] = jnp.swapaxes(group["w_in"], 1, 2)

    assert depth == 2, "the second layer's gather hides behind the first layer's forward, and likewise backward"
    first, rest = BIG[:1], BIG[1:]
    cw_all = _all_to_all_small(_pad8(conv_w))
    bufs = [{n: _cast_place(weights[n], 0, f"cast_{n}_l0") for n in first}]
    a_ssem, a_rsem, a_views, a_token = _gather_start([bufs[0][n] for n in first], "gather_start_l0_in", cw_all)
    bufs[0].update({n: _cast_place(weights[n], 0, f"cast_{n}_l0") for n in rest})
    bufs += [{n: _cast_place(weights[n], l, f"cast_{n}_l{l}") for n in BIG} for l in range(1, depth)]
    b_ssem, b_rsem, b_views, b_token = _gather_start([bufs[0][n] for n in rest], "gather_start_l0_rest", a_token)
    g_ssem, g_rsem, g_views, g_token = _gather_start([bufs[1][n] for n in BIG], "gather_start_l1", b_token)
    conv_w_full = jnp.concatenate(
        [cw_all[2 * p].reshape(-1)[:conv_w.size].reshape(conv_w.shape) for p in range(N_CHIPS)], axis=-1)
    small = dict(weights)
    small["conv_w_full"] = conv_w_full

    def soon_weights(t):
        got = _gather_finish(a_ssem, a_rsem, a_views, t, "gather_wait_l0_in", "gather_forward_l0_in",
                             [bufs[0][n].shape for n in first])
        return _matrix_weights(dict(zip(first, got)))

    def late_weights(t):
        got = _gather_finish(b_ssem, b_rsem, b_views, t, "gather_wait_l0_rest", "gather_forward_l0_rest",
                             [bufs[0][n].shape for n in rest])
        return _matrix_weights(dict(zip(rest, got)))

    wl, saved = [None] * depth, [None] * depth
    h, saved[0], wl[0] = _layer_fwd(x[0], _small_weights(0, small), "l0", after=g_token, soon=soon_weights,
                                    late=late_weights)
    got = _gather_finish(g_ssem, g_rsem, g_views, h, "gather_wait_l1", "gather_forward_l1",
                         [bufs[1][n].shape for n in BIG])
    (dh, loss_row), saved[1], wl[1] = _layer_fwd(
        h, dict(_small_weights(1, small), **_matrix_weights(dict(zip(BIG, got)))), "l1", target=loss_target[0])

    def pair_stage(names, g, tag):
        views = [_halves(_to_chip_major(n, g[n])) for n in names]
        from_sibling = _pair_swap(views, f"grad_pair_swap_{tag}")
        return [_pair_sum(views[k], from_sibling[k], f"pair_sum_{n}_{tag}") for k, n in enumerate(names)]

    mine, other = [{}, {}], [{}, {}]

    def finish(names, l, started, after, tag):
        ssem, rsem, parts, landing, _ = started
        parts, arrived = _chip_exchange_wait(ssem, rsem, parts, landing, after, f"grad_chip_exchange_wait_{tag}")
        got = [_chip_sum(parts[k], arrived[k], f"chip_sum_{n}_{tag}") for k, n in enumerate(names)]
        mine[l].update(zip(names, got))
        other[l].update(zip(names, _pair_send(got, f"grad_pair_send_{tag}")))

    def small_pieces(g):
        return [_pad8(g[n][:3] if n == "conv_w" else g[n]) for n in SMALL]

    def start_small(l):
        return _all_to_all_small_start(jnp.concatenate(small_pieces(grads[l]), axis=0), f"small_grad_exchange_start_l{l}")

    grads, early, small = [None] * depth, {}, [None] * depth
    dh, grads[1] = _layer_bwd(dh, wl[1], saved[1], "l1")
    second = _chip_exchange_start(pair_stage(BIG, grads[1], "l1"), "grad_chip_exchange_start_l1")
    small[1] = start_small(1)

    def start_rest(g):
        early["rest"] = _chip_exchange_start(pair_stage(rest, g, "l0_rest"), "grad_chip_exchange_start_l0_rest")
        return early["rest"][4]

    def start_last(g):
        early["in"] = _chip_exchange_start(pair_stage(first, g, "l0_in"), "grad_chip_exchange_start_l0_in")
        return early["in"][4]

    dh, grads[0] = _layer_bwd(dh, wl[0], saved[0], "l0", after=[second[4], small[1][4]], mid=start_rest,
                              tail=start_last)
    small[0] = start_small(0)
    finish(BIG, 1, second, dh, "l1")
    finish(rest, 0, early["rest"], dh, "l0_rest")
    loss = lax.psum(loss_row[0, 0], ("x", "y", "c"))
    full = {}

    deltas, new_m, new_v = {}, {}, {}

    def update_matrix(n):
        full[n], deltas[n], new_m[n], new_v[n] = _adamw_halves(
            weights[n], moms[n], vels[n], [mine[l][n] for l in range(depth)], [other[l][n] for l in range(depth)],
            f"adamw_{n}")

    for n in rest:
        update_matrix(n)
    finish(first, 0, early["in"], deltas[rest[-1]], "l0_in")
    for n in first:
        update_matrix(n)
    summed = []
    for l in range(depth):
        ssem, rsem, part, landing, _ = small[l]
        summed.append(_sum_slices(_all_to_all_small_wait(ssem, rsem, part, landing, deltas[first[-1]],
                                                         f"small_grad_exchange_wait_l{l}"), f"small_sum_l{l}"))
    row = 0
    for n, piece in zip(SMALL, small_pieces(grads[0])):
        size = (weights[n].size if n != "conv_w" else depth * 3 * 512) // depth
        flat = jnp.stack([s[row:row + piece.shape[0]].reshape(-1)[:size] for s in summed])
        row += piece.shape[0]
        if n == "conv_w":
            full[n] = lax.dynamic_slice_in_dim(flat.reshape(depth, 3, 512), q * conv_w.shape[2], conv_w.shape[2], axis=2)
        else:
            full[n] = flat.reshape(weights[n].shape)
    for n in SMALL:
        shape = weights[n].shape
        two_d = (-1, shape[-1]) if n not in ("conv_w", "q_gain", "k_gain") else (1, -1)
        d2, m2, v2 = _adamw(weights[n].reshape(two_d), full[n].reshape(two_d), moms[n].reshape(two_d),
                            vels[n].reshape(two_d), f"adamw_{n}")
        deltas[n], new_m[n], new_v[n] = d2.reshape(shape), m2.reshape(shape), v2.reshape(shape)
        full[n] = full[n].reshape(shape)
    for group in (full, deltas, new_m, new_v):
        group["w_in"] = jnp.swapaxes(group["w_in"], 1, 2)
    return (loss, dh[None], *[full[n] for n in ORDER], *[deltas[n] for n in ORDER], *[new_m[n] for n in ORDER],
            *[new_v[n] for n in ORDER])
```

```python
import functools

import jax
import jax.numpy as jnp
from jax import lax
from jax.experimental import pallas as pl
from jax.experimental.pallas import tpu as pltpu

F32 = jnp.float32
BF = jnp.bfloat16
MESH_ID = pl.DeviceIdType.MESH
ANY = pl.BlockSpec(memory_space=pl.ANY)

EPS = 1e-6
MASK_VALUE = -1e30
POOL_WINDOWS = (2, 4, 8, 16)
ATTN_DILATIONS = (1, 4, 16)
ATTN_BLOCK = 128
HEAD_DIM = 64
OFF_Q, OFF_K, OFF_V, OFF_GATE = 2048, 2816, 3584, 4352
N_CHIPS = 4
ADAM_LR, ADAM_B1, ADAM_B2, ADAM_EPS, ADAM_WD, ADAM_STEP = 0.001, 0.9, 0.999, 1e-08, 0.01, 10

VMEM_LIMIT = 48 * 1024 * 1024
LANES = 128

_DIMS = {"nn": (((1,), (0,)), ((), ())), "nt": (((1,), (1,)), ((), ())), "tn": (((0,), (0,)), ((), ()))}


def _params(sem):
    return pltpu.CompilerParams(dimension_semantics=sem, vmem_limit_bytes=VMEM_LIMIT)


def _pallas_call(body, **kw):
    def in_hbm(s):
        pin = isinstance(s, jax.ShapeDtypeStruct) and s is not TOKEN and jnp.issubdtype(s.dtype, jnp.floating)
        return pltpu.HBM(s.shape, s.dtype) if pin else s

    out_shape = kw.pop("out_shape")
    kw["out_shape"] = [in_hbm(s) for s in out_shape] if isinstance(out_shape, (list, tuple)) else in_hbm(out_shape)
    call = pl.pallas_call(body, **kw)

    def run(*args):
        pinned = [pltpu.with_memory_space_constraint(a, pltpu.HBM)
                  if hasattr(a, "dtype") and jnp.issubdtype(a.dtype, jnp.floating) else a for a in args]
        return call(*pinned)

    return run


def _dot(a, b, mode="nn"):
    return lax.dot_general(a, b, _DIMS[mode], preferred_element_type=F32)


def _mm(a, b, mode, name, *, tm, tn, tk, out_dtype=F32, res=None, aux=None, epi=None, n_outer=False,
        b_shards=False, out_shards=False, after=None, vec=None):
    if mode == "tn":
        K, M = a.shape
    else:
        M, K = a.shape
    if b_shards:
        if mode == "nn":
            assert b.shape[1] == K
            N = b.shape[2] * N_CHIPS
        else:
            assert mode == "nt"
            N = b.shape[1]
            assert b.shape[2] * N_CHIPS == K
    else:
        N = b.shape[0] if mode == "nt" else b.shape[1]
    tm, tn, tk = min(tm, M), min(tn, N), min(tk, K)
    assert M % tm == 0 and N % tn == 0 and K % tk == 0
    nk = K // tk
    if n_outer:
        grid = (N // tn, M // tm, nk)
        ij = lambda p, q_: (q_, p)
    else:
        grid = (M // tm, N // tn, nk)
        ij = lambda p, q_: (p, q_)

    def amap(p, q_, k):
        i, j = ij(p, q_)
        return (k, i) if mode == "tn" else (i, k)

    a_spec = pl.BlockSpec((tk, tm) if mode == "tn" else (tm, tk), amap)
    if b_shards:
        if mode == "nn":
            per = (N // N_CHIPS) // tn
            assert per >= 1 and (N // N_CHIPS) % tn == 0

            def bmap(p, q_, k):
                i, j = ij(p, q_)
                return (j // per, k, j % per)

            b_spec = pl.BlockSpec((None, tk, tn), bmap)
        else:
            per = (K // N_CHIPS) // tk
            assert per >= 1 and (K // N_CHIPS) % tk == 0

            def bmap(p, q_, k):
                i, j = ij(p, q_)
                return (k // per, j, k % per)

            b_spec = pl.BlockSpec((None, tn, tk), bmap)
    else:
        def bmap(p, q_, k):
            i, j = ij(p, q_)
            return (j, k) if mode == "nt" else (k, j)

        b_spec = pl.BlockSpec((tn, tk) if mode == "nt" else (tk, tn), bmap)

    def omap(p, q_, k):
        return ij(p, q_)

    o_spec = pl.BlockSpec((tm, tn), omap)
    if out_shards:
        per_o = (N // N_CHIPS) // tn
        assert per_o >= 1 and (N // N_CHIPS) % tn == 0

        def osmap(p, q_, k):
            i, j = ij(p, q_)
            return (j // per_o, i, j % per_o)

        out_spec0 = pl.BlockSpec((None, tm, tn), osmap)
        out_shape0 = jax.ShapeDtypeStruct((N_CHIPS, M, N // N_CHIPS), out_dtype)
    else:
        out_spec0 = o_spec
        out_shape0 = jax.ShapeDtypeStruct((M, N), out_dtype)

    in_specs = [a_spec, b_spec]
    args = [a, b]
    if res is not None:
        in_specs.append(o_spec)
        args.append(res)
    if aux is not None:
        in_specs.append(o_spec)
        args.append(aux)
    if vec is not None:
        in_specs.append(pl.BlockSpec((1, tn), lambda p, q_, k: (0, ij(p, q_)[1])))
        args.append(vec)
    after = [] if after is None else list(after) if isinstance(after, (list, tuple)) else [after]
    in_specs += [ANY] * len(after)
    args += after
    out_specs = [out_spec0]
    out_shape = [out_shape0]
    reduces = epi in ("loss", "rms_bwd")
    if reduces:
        assert tn == N and not n_outer and not out_shards
        width = LANES if epi == "loss" else N
        out_specs.append(pl.BlockSpec((1, width), lambda p, q_, k: (0, 0)))
        out_shape.append(jax.ShapeDtypeStruct((1, width), F32))
    if epi == "rms_next":
        assert tn == N and not out_shards
        out_specs.append(o_spec)
        out_shape.append(jax.ShapeDtypeStruct((M, N), BF))
    n_out = len(out_shape)
    has_res, has_aux, has_vec, n_after = res is not None, aux is not None, vec is not None, len(after)

    def body(*refs):
        a_ref, b_ref = refs[0], refs[1]
        pos = 2
        res_ref = aux_ref = vec_ref = None
        if has_res:
            res_ref = refs[pos]
            pos += 1
        if has_aux:
            aux_ref = refs[pos]
            pos += 1
        if has_vec:
            vec_ref = refs[pos]
            pos += 1
        pos += n_after
        outs = refs[pos:pos + n_out]
        part = _dot(a_ref[...].astype(BF), b_ref[...].astype(BF), mode)

        first_row_tile = pl.program_id(0) == 0

        def add_to_sum(row):
            @pl.when(first_row_tile)
            def _():
                outs[1][...] = jnp.zeros_like(outs[1])

            outs[1][...] += row

        def finish(acc):
            if epi == "rms_bwd":
                xv = aux_ref[...]
                r = lax.rsqrt(jnp.mean(xv * xv, axis=-1, keepdims=True) + EPS)
                xhat = xv * r
                dy = acc * vec_ref[...]
                outs[0][...] = res_ref[...] + r * (dy - xhat * jnp.mean(dy * xhat, axis=-1, keepdims=True))
                add_to_sum(jnp.sum(acc * xhat, axis=0, keepdims=True))
                return
            if res_ref is not None:
                acc = res_ref[...] + acc
            if epi == "relu2":
                r = jnp.maximum(acc, 0.0)
                outs[0][...] = (r * r).astype(out_dtype)
            elif epi == "drelu2":
                outs[0][...] = (acc.astype(BF) * (2.0 * jnp.sqrt(aux_ref[...]))).astype(out_dtype)
            elif epi == "rms_next":
                outs[0][...] = acc
                r = lax.rsqrt(jnp.mean(acc * acc, axis=-1, keepdims=True) + EPS)
                outs[1][...] = ((acc * r) * vec_ref[...]).astype(BF)
            elif epi == "loss":
                e = acc - aux_ref[...]
                outs[0][...] = e / float(N)
                add_to_sum(0.5 * jnp.sum(jnp.mean(e * e, axis=-1, keepdims=True)))
            else:
                outs[0][...] = acc.astype(out_dtype)

        if nk == 1:
            finish(part)
        else:
            acc_ref = refs[pos + n_out]
            k = pl.program_id(2)

            @pl.when(k == 0)
            def _():
                acc_ref[...] = part

            @pl.when(k > 0)
            def _():
                acc_ref[...] += part

            @pl.when(k == nk - 1)
            def _():
                finish(acc_ref[...])

    scratch = [pltpu.VMEM((tm, tn), F32)] if nk > 1 else []
    out = _pallas_call(
        body, name=name, grid=grid, in_specs=in_specs, out_specs=out_specs, out_shape=out_shape,
        scratch_shapes=scratch,
        compiler_params=_params(("arbitrary" if reduces else "parallel", "parallel", "arbitrary")),
    )(*args)
    return out if n_out > 1 else out[0]


def _rms_fwd(x, gain, name, after=None):
    T, D = x.shape
    tm = min(512, T)

    def body(x_ref, g_ref, *rest):
        o_ref = rest[-1]
        xv = x_ref[...]
        r = lax.rsqrt(jnp.mean(xv * xv, axis=-1, keepdims=True) + EPS)
        o_ref[...] = ((xv * r) * g_ref[...]).astype(BF)

    extra = [] if after is None else list(after) if isinstance(after, (list, tuple)) else [after]
    return _pallas_call(
        body, name=name, grid=(T // tm,),
        in_specs=[pl.BlockSpec((tm, D), lambda i: (i, 0)), pl.BlockSpec((1, D), lambda i: (0, 0))] + [ANY] * len(extra),
        out_specs=pl.BlockSpec((tm, D), lambda i: (i, 0)), out_shape=jax.ShapeDtypeStruct((T, D), BF),
        compiler_params=_params(("parallel",)),
    )(x, gain, *extra)


def _rms_bwd(dh, x, gain, dres, name):
    T, D = x.shape
    tm = min(512, T)

    def body(dh_ref, x_ref, g_ref, dres_ref, dx_ref, dg_ref):
        xv = x_ref[...]
        r = lax.rsqrt(jnp.mean(xv * xv, axis=-1, keepdims=True) + EPS)
        xhat = xv * r
        dhv = dh_ref[...]
        dy = dhv * g_ref[...]
        dx_ref[...] = dres_ref[...] + r * (dy - xhat * jnp.mean(dy * xhat, axis=-1, keepdims=True))

        @pl.when(pl.program_id(0) == 0)
        def _():
            dg_ref[...] = jnp.zeros_like(dg_ref)

        dg_ref[...] += jnp.sum(dhv * xhat, axis=0, keepdims=True)

    row = pl.BlockSpec((tm, D), lambda i: (i, 0))
    vec = pl.BlockSpec((1, D), lambda i: (0, 0))
    return _pallas_call(
        body, name=name, grid=(T // tm,), in_specs=[row, row, vec, row], out_specs=[row, vec],
        out_shape=[jax.ShapeDtypeStruct((T, D), F32), jax.ShapeDtypeStruct((1, D), F32)],
        compiler_params=_params(("arbitrary",)),
    )(dh, x, gain, dres)


POOL_HALO = 16
CONV_HALO = 8
POOLCONV_ROWS = 512


def _causal_window_sum(v, w):
    s, sh = v, 1
    while sh < w:
        s = s + pltpu.roll(s, sh, 0)
        sh *= 2
    return s


def _anticausal_window_sum(v, w):
    n = v.shape[0]
    s, sh = v, 1
    while sh < w:
        s = s + pltpu.roll(s, n - sh, 0)
        sh *= 2
    return s


def _poolconv_fwd(z, pmix_b, pscale, convw, name):
    T = z.shape[0]
    R = min(POOLCONV_ROWS, T)
    PH, CH = R // POOL_HALO, R // CONV_HALO

    def body(u_ref, uh_ref, b_ref, c_ref, ch_ref, x_ref, xh_ref, mix_ref, sc_ref, cw_ref, yp_ref, yc_ref):
        i = pl.program_id(0)
        keep = (i > 0).astype(F32)
        row = i * R + lax.broadcasted_iota(jnp.int32, (R, 1), 0)
        w_all = jnp.concatenate([uh_ref[...] * keep, u_ref[...]], axis=0)
        for g, w in enumerate(POOL_WINDOWS):
            cols = slice(128 * g, 128 * (g + 1))
            wg = w_all[:, cols]
            s = _causal_window_sum(wg, w)[POOL_HALO:]
            inv_cnt = 1.0 / jnp.minimum(row + 1, w).astype(F32)
            dgrp = s * inv_cnt - wg[POOL_HALO:]
            y = _dot(dgrp.astype(BF), mix_ref[g]) * sc_ref[:, cols]
            yp_ref[:, cols] = y.astype(BF)
        uc = jnp.concatenate([ch_ref[...] * xh_ref[...] * keep, c_ref[...] * x_ref[...]], axis=0)
        yc = cw_ref[2:3, :] * uc + cw_ref[0:1, :] * pltpu.roll(uc, 2, 0) + cw_ref[1:2, :] * pltpu.roll(uc, 1, 0)
        yc_ref[...] = (b_ref[...] * yc[CONV_HALO:]).astype(BF)

    def main(cb):
        return pl.BlockSpec((R, 512), lambda i: (i, cb))

    def prev(cb, halo, per):
        return pl.BlockSpec((halo, 512), lambda i: (jnp.maximum(i * per - 1, 0), cb))

    full = lambda a: pl.BlockSpec(a.shape, lambda i: (0,) * a.ndim)
    return _pallas_call(
        body, name=name, grid=(T // R,),
        in_specs=[main(0), prev(0, POOL_HALO, PH), main(1), main(2), prev(2, CONV_HALO, CH), main(3),
                  prev(3, CONV_HALO, CH), full(pmix_b), full(pscale), full(convw)],
        out_specs=[pl.BlockSpec((R, 512), lambda i: (i, 0))] * 2,
        out_shape=[jax.ShapeDtypeStruct((T, 512), BF)] * 2,
        compiler_params=_params(("parallel",)),
    )(z, z, z, z, z, z, z, pmix_b, pscale, convw)


def _poolconv_bwd(z, dyp, dyc, pmix_b, pscale, convw, dz, name):
    T = z.shape[0]
    R = min(POOLCONV_ROWS, T)
    PH, CH = R // POOL_HALO, R // CONV_HALO
    nsteps = T // R

    def body(u_ref, uh_ref, b_ref, bn_ref, c_ref, ch_ref, x_ref, xh_ref, dyp_ref, dypn_ref, dyc_ref, dycn_ref,
             mix_ref, sc_ref, cw_ref, dz_in_ref, dz_ref, dmix_ref, dsc_ref, dcw_ref):
        i = pl.program_id(0)
        keep_prev = (i > 0).astype(F32)
        keep_next = (i < nsteps - 1).astype(F32)

        @pl.when(i == 0)
        def _():
            dmix_ref[...] = jnp.zeros_like(dmix_ref)
            dsc_ref[...] = jnp.zeros_like(dsc_ref)
            dcw_ref[...] = jnp.zeros_like(dcw_ref)

        row = i * R + lax.broadcasted_iota(jnp.int32, (R, 1), 0)
        row_ext = i * R + lax.broadcasted_iota(jnp.int32, (R + POOL_HALO, 1), 0)
        w_all = jnp.concatenate([uh_ref[...] * keep_prev, u_ref[...]], axis=0)
        dyp_ext = jnp.concatenate([dyp_ref[...], dypn_ref[...] * keep_next], axis=0)
        for g, w in enumerate(POOL_WINDOWS):
            cols = slice(128 * g, 128 * (g + 1))
            wg = w_all[:, cols]
            s = _causal_window_sum(wg, w)[POOL_HALO:]
            inv_cnt = 1.0 / jnp.minimum(row + 1, w).astype(F32)
            dgrp = (s * inv_cnt - wg[POOL_HALO:]).astype(BF)
            y_pre = _dot(dgrp, mix_ref[g])
            dsc_ref[:, cols] += jnp.sum(dyp_ref[:, cols] * y_pre, axis=0, keepdims=True)
            dyb = (dyp_ext[:, cols] * sc_ref[:, cols]).astype(BF)
            dmix_ref[cols, :] += _dot(dgrp, dyb[:R], "tn")
            dd = _dot(dyb, mix_ref[g], "nt")
            inv_cnt_ext = 1.0 / jnp.minimum(row_ext + 1, w).astype(F32)
            e = _anticausal_window_sum(dd * inv_cnt_ext, w)
            dz_ref[:, cols] = (e[:R] - dd[:R]).astype(BF)
        cw0, cw1, cw2 = cw_ref[0:1, :], cw_ref[1:2, :], cw_ref[2:3, :]
        uc = jnp.concatenate([ch_ref[...] * xh_ref[...] * keep_prev, c_ref[...] * x_ref[...]], axis=0)
        uc1 = pltpu.roll(uc, 1, 0)[CONV_HALO:]
        uc2 = pltpu.roll(uc, 2, 0)[CONV_HALO:]
        uc0 = uc[CONV_HALO:]
        yc = cw2 * uc0 + cw0 * uc2 + cw1 * uc1
        dycv = dyc_ref[...]
        dz_ref[:, 512:1024] = (dycv * yc).astype(BF)
        dv_ext = jnp.concatenate([dycv * b_ref[...], dycn_ref[...] * bn_ref[...] * keep_next], axis=0)
        n_ext = R + CONV_HALO
        duc = (cw2 * dv_ext + cw1 * pltpu.roll(dv_ext, n_ext - 1, 0) + cw0 * pltpu.roll(dv_ext, n_ext - 2, 0))[:R]
        dv = dv_ext[:R]
        dcw_ref[0:1, :] += jnp.sum(dv * uc2, axis=0, keepdims=True)
        dcw_ref[1:2, :] += jnp.sum(dv * uc1, axis=0, keepdims=True)
        dcw_ref[2:3, :] += jnp.sum(dv * uc0, axis=0, keepdims=True)
        dz_ref[:, 1024:1536] = (duc * x_ref[...]).astype(BF)
        dz_ref[:, 1536:2048] = (duc * c_ref[...]).astype(BF)

    def main(cb):
        return pl.BlockSpec((R, 512), lambda i: (i, cb))

    def prev(cb, halo, per):
        return pl.BlockSpec((halo, 512), lambda i: (jnp.maximum(i * per - 1, 0), cb))

    def nxt(cb, halo, per):
        return pl.BlockSpec((halo, 512), lambda i: (jnp.minimum((i + 1) * per, T // halo - 1), cb))

    full = lambda a: pl.BlockSpec(a.shape, lambda i: (0,) * a.ndim)
    return _pallas_call(
        body, name=name, grid=(nsteps,),
        in_specs=[main(0), prev(0, POOL_HALO, PH), main(1), nxt(1, CONV_HALO, CH), main(2), prev(2, CONV_HALO, CH),
                  main(3), prev(3, CONV_HALO, CH), main(0), nxt(0, POOL_HALO, PH), main(0), nxt(0, CONV_HALO, CH),
                  full(pmix_b), full(pscale), full(convw), ANY],
        out_specs=[pl.BlockSpec((R, 2048), lambda i: (i, 0)), pl.BlockSpec((512, 128), lambda i: (0, 0)),
                   pl.BlockSpec((1, 512), lambda i: (0, 0)), pl.BlockSpec((8, 512), lambda i: (0, 0))],
        out_shape=[jax.ShapeDtypeStruct(dz.shape, BF), jax.ShapeDtypeStruct((512, 128), F32),
                   jax.ShapeDtypeStruct((1, 512), F32), jax.ShapeDtypeStruct((8, 512), F32)],
        input_output_aliases={15: 0}, compiler_params=_params(("arbitrary",)),
    )(z, z, z, z, z, z, z, z, dyp, dyp, dyc, dyc, pmix_b, pscale, convw, dz)


def _head_sums(v):
    row = lax.broadcasted_iota(jnp.int32, (LANES, LANES), 0) < HEAD_DIM
    col = lax.broadcasted_iota(jnp.int32, (LANES, LANES), 1) < HEAD_DIM
    same_head = jnp.where(jnp.logical_xor(row, col), 0.0, 1.0).astype(BF)
    hi = v.astype(BF)
    lo = (v - hi.astype(F32)).astype(BF)
    return _dot(hi, same_head) + _dot(lo, same_head)


def _head_norm(x, g2, ma):
    r = lax.rsqrt(_head_sums(x * x) / HEAD_DIM + EPS)
    return x * r, r


def _head_norm_bwd(dy, xhat, r, g2, ma):
    dxh = dy * g2
    return r * (dxh - xhat * (_head_sums(dxh * xhat) / HEAD_DIM))


def _attn_masks(other_block_exists):
    lane = lax.broadcasted_iota(jnp.int32, (2 * ATTN_BLOCK, ATTN_BLOCK), 1)
    qi = lax.broadcasted_iota(jnp.int32, (2 * ATTN_BLOCK, ATTN_BLOCK), 0) & (ATTN_BLOCK - 1)
    never = (1 - other_block_exists.astype(jnp.int32)) * (2 * ATTN_BLOCK)
    return lane[:ATTN_BLOCK] < HEAD_DIM, lane <= qi, lane >= qi + never


def _stack_heads(x, ma):
    return jnp.concatenate([jnp.where(ma, x, 0.0), jnp.where(ma, 0.0, x)], axis=0)


def _unstack_heads(y, ma):
    return jnp.where(ma, y[:ATTN_BLOCK], y[ATTN_BLOCK:])


def _stack_cols(tile, ma):
    return jnp.concatenate([tile[:, 0:1], tile[:, HEAD_DIM:HEAD_DIM + 1]], axis=0)


QKV_TILES = (OFF_GATE - OFF_Q) // LANES
KIND_TILES = QKV_TILES // 3


def _qk_norm(z, gains, name):
    T = z.shape[0]
    tm = min(512, T)

    def body(x_ref, g_ref, o_ref):
        ma = lax.broadcasted_iota(jnp.int32, (tm, LANES), 1) < HEAD_DIM
        for tile in range(QKV_TILES):
            v = x_ref[:, LANES * tile:LANES * (tile + 1)]
            if tile < 2 * KIND_TILES:
                g = g_ref[0:1, :] if tile < KIND_TILES else g_ref[1:2, :]
                v = _head_norm(v, g, ma)[0] * g
            o_ref[tile] = v

    return _pallas_call(
        body, name=name, grid=(T // tm,),
        in_specs=[pl.BlockSpec((pl.Element(tm), pl.Element(OFF_GATE - OFF_Q)), lambda i: (i * tm, OFF_Q)),
                  pl.BlockSpec((8, LANES), lambda i: (0, 0))],
        out_specs=pl.BlockSpec((QKV_TILES, tm, LANES), lambda i: (0, i, 0)),
        out_shape=jax.ShapeDtypeStruct((QKV_TILES, T, LANES), F32), compiler_params=_params(("parallel",)),
    )(z, gains)


ATTN_STEP_ROWS = 2048
ATTN_UNROLL = 4


def _attn_geometry(T, d):
    sub = ATTN_BLOCK * d
    nb = T // sub
    m = max(1, min(nb, ATTN_STEP_ROWS // sub))
    assert T % sub == 0 and nb % m == 0
    return sub, nb, m


def _attn_rows(jj, r, sub, d):
    start = jj * sub + r
    if d == 1:
        return pl.ds(pl.multiple_of(start, ATTN_BLOCK), ATTN_BLOCK)
    return pl.ds(start, ATTN_BLOCK, stride=d)


def _pick(flag, a, b):
    return jnp.where(jnp.full(a.shape, flag.astype(jnp.int32)) > 0, a, b)


def _attn_fwd(qkv, g, d, name):
    T = qkv.shape[1]
    sub, nb, m = _attn_geometry(T, d)
    scale = HEAD_DIM ** -0.5

    def body(q_ref, kc_ref, kp_ref, vc_ref, vp_ref, o_ref, lse_ref):
        jb = pl.program_id(0)

        def step(s, carry):
            jj, r = s // d, s % d
            here, before = _attn_rows(jj, r, sub, d), _attn_rows(jnp.maximum(jj - 1, 0), r, sub, d)
            edge = _attn_rows(0, r, sub, d)
            first = jj == 0
            ma, mask_c, mask_p = _attn_masks(jb * m + jj > 0)
            qs = _stack_heads(q_ref[here, :], ma).astype(BF)
            kcb = kc_ref[here, :].astype(BF)
            kpb = _pick(first, kp_ref[edge, :], kc_ref[before, :]).astype(BF)
            vcb = vc_ref[here, :].astype(BF)
            vpb = _pick(first, vp_ref[edge, :], vc_ref[before, :]).astype(BF)
            s_c = jnp.where(mask_c, _dot(qs, kcb, "nt") * scale, MASK_VALUE)
            s_p = jnp.where(mask_p, _dot(qs, kpb, "nt") * scale, MASK_VALUE)
            mx = jnp.maximum(jnp.max(s_c, axis=-1, keepdims=True), jnp.max(s_p, axis=-1, keepdims=True))
            p_c = jnp.exp(s_c - mx)
            p_p = jnp.exp(s_p - mx)
            den = jnp.sum(p_c, axis=-1, keepdims=True) + jnp.sum(p_p, axis=-1, keepdims=True)
            o = (_dot(p_c.astype(BF), vcb) + _dot(p_p.astype(BF), vpb)) / den
            o_ref[here, :] = _unstack_heads(o, ma)
            lse_ref[here, :] = _unstack_heads(jnp.broadcast_to(mx + jnp.log(den), o.shape), ma)
            return carry

        lax.fori_loop(0, m * d, step, 0, unroll=ATTN_UNROLL)

    def cur(kind):
        return pl.BlockSpec((None, m * sub, LANES), lambda j, t: (KIND_TILES * kind + 2 * g + t, j, 0))

    def prv(kind):
        return pl.BlockSpec((None, sub, LANES), lambda j, t: (KIND_TILES * kind + 2 * g + t, jnp.maximum(j * m - 1, 0), 0))

    out = pl.BlockSpec((m * sub, LANES), lambda j, t: (j, t))
    return _pallas_call(
        body, name=name, grid=(nb // m, 2), in_specs=[cur(0), cur(1), prv(1), cur(2), prv(2)],
        out_specs=[out, out], out_shape=[jax.ShapeDtypeStruct((T, 256), F32)] * 2,
        compiler_params=_params(("parallel", "parallel")),
    )(qkv, qkv, qkv, qkv, qkv)


def _attn_bwd(z, qkv, do, c, lse, gains, g, d, name, after=None):
    T = z.shape[0]
    sub, nb, m = _attn_geometry(T, d)
    scale = HEAD_DIM ** -0.5
    extra = [] if after is None else [after]

    def body(qr_ref, kr_ref, vc_ref, vp_ref, qn_ref, qnn_ref, kn_ref, knp_ref, do_ref, don_ref, c_ref, cn_ref,
             lse_ref, lsen_ref, g_ref, *rest):
        dq_ref, dk_ref, dv_ref, dgq_ref, dgk_ref, sq_ref, sk_ref, sv_ref = rest[len(extra):]
        jb = pl.program_id(0)

        @pl.when((jb == 0) & (pl.program_id(1) == 0))
        def _():
            dgq_ref[...] = jnp.zeros_like(dgq_ref)
            dgk_ref[...] = jnp.zeros_like(dgk_ref)

        gq, gk = g_ref[0:1, :], g_ref[1:2, :]

        def step(s, carry):
            jj, r = s // d, s % d
            here, edge = _attn_rows(jj, r, sub, d), _attn_rows(0, r, sub, d)
            before = _attn_rows(jnp.maximum(jj - 1, 0), r, sub, d)
            behind = _attn_rows(jnp.minimum(jj + 1, m - 1), r, sub, d)
            first, last = jj == 0, jj == m - 1
            block = jb * m + jj
            ma, mask_c, mask_p = _attn_masks(block > 0)
            mask_n = _attn_masks(block < nb - 1)[2]
            qhat, rq = _head_norm(qr_ref[here, :], gq, ma)
            qn = qn_ref[here, :]
            qn_next = _pick(last, qnn_ref[edge, :], qn_ref[behind, :])
            khat, rk = _head_norm(kr_ref[here, :], gk, ma)
            kcb = kn_ref[here, :].astype(BF)
            kpb = _pick(first, knp_ref[edge, :], kn_ref[before, :]).astype(BF)
            vcb = vc_ref[here, :].astype(BF)
            vpb = _pick(first, vp_ref[edge, :], vc_ref[before, :]).astype(BF)
            do_t, don_t = do_ref[here, :], _pick(last, don_ref[edge, :], do_ref[behind, :])
            c_t, cn_t = c_ref[here, :], _pick(last, cn_ref[edge, :], c_ref[behind, :])
            lse_t, lsen_t = lse_ref[here, :], _pick(last, lsen_ref[edge, :], lse_ref[behind, :])
            qs, dos = _stack_heads(qn, ma).astype(BF), _stack_heads(do_t, ma).astype(BF)
            lse_s, c_s = _stack_cols(lse_t, ma), _stack_cols(c_t, ma)
            s_c = jnp.where(mask_c, _dot(qs, kcb, "nt") * scale, MASK_VALUE)
            s_p = jnp.where(mask_p, _dot(qs, kpb, "nt") * scale, MASK_VALUE)
            p_c = jnp.exp(s_c - lse_s)
            p_p = jnp.exp(s_p - lse_s)
            ds_c = ((p_c * (_dot(dos, vcb, "nt") + c_s)) * scale).astype(BF)
            ds_p = ((p_p * (_dot(dos, vpb, "nt") + c_s)) * scale).astype(BF)
            dq_t = _unstack_heads(_dot(ds_c, kcb) + _dot(ds_p, kpb), ma)
            qs_n, dos_n = _stack_heads(qn_next, ma).astype(BF), _stack_heads(don_t, ma).astype(BF)
            s_n = jnp.where(mask_n, _dot(qs_n, kcb, "nt") * scale, MASK_VALUE)
            p_n = jnp.exp(s_n - _stack_cols(lsen_t, ma))
            ds_n = ((p_n * (_dot(dos_n, vcb, "nt") + _stack_cols(cn_t, ma))) * scale).astype(BF)
            dv_t = _dot(p_c.astype(BF), dos, "tn") + _dot(p_n.astype(BF), dos_n, "tn")
            dk_t = _dot(ds_c, qs, "tn") + _dot(ds_n, qs_n, "tn")
            sq_ref[here, :] = _head_norm_bwd(dq_t, qhat, rq, gq, ma)
            sk_ref[here, :] = _head_norm_bwd(dk_t, khat, rk, gk, ma)
            sv_ref[here, :] = dv_t
            dgq_ref[...] += jnp.sum(dq_t * qhat, axis=0, keepdims=True)
            dgk_ref[...] += jnp.sum(dk_t * khat, axis=0, keepdims=True)
            return carry

        lax.fori_loop(0, m * d, step, 0, unroll=ATTN_UNROLL)
        dq_ref[...] = sq_ref[...].astype(BF)
        dk_ref[...] = sk_ref[...].astype(BF)
        dv_ref[...] = sv_ref[...].astype(BF)

    def raw(col0):
        return pl.BlockSpec((m * sub, LANES), lambda j, t: (j, col0 + 2 * g + t))

    def cur(kind):
        return pl.BlockSpec((None, m * sub, LANES), lambda j, t: (KIND_TILES * kind + 2 * g + t, j, 0))

    def prv(kind):
        return pl.BlockSpec((None, sub, LANES), lambda j, t: (KIND_TILES * kind + 2 * g + t, jnp.maximum(j * m - 1, 0), 0))

    def nxt(kind):
        return pl.BlockSpec((None, sub, LANES),
                            lambda j, t: (KIND_TILES * kind + 2 * g + t, jnp.minimum((j + 1) * m, nb - 1), 0))

    own = pl.BlockSpec((m * sub, LANES), lambda j, t: (j, t))
    own_next = pl.BlockSpec((sub, LANES), lambda j, t: (jnp.minimum((j + 1) * m, nb - 1), t))
    vec = pl.BlockSpec((1, LANES), lambda j, t: (0, 0))
    return _pallas_call(
        body, name=name, grid=(nb // m, 2),
        in_specs=[raw(OFF_Q // LANES), raw(OFF_K // LANES), cur(2), prv(2), cur(0), nxt(0), cur(1), prv(1), own, own_next,
                  own, own_next,
                  own, own_next, pl.BlockSpec((8, LANES), lambda j, t: (0, 0))] + [ANY] * len(extra),
        out_specs=[own, own, own, vec, vec],
        out_shape=[jax.ShapeDtypeStruct((T, 256), BF)] * 3 + [jax.ShapeDtypeStruct((1, LANES), F32)] * 2,
        scratch_shapes=[pltpu.VMEM((m * sub, LANES), F32)] * 3,
        compiler_params=_params(("arbitrary", "arbitrary")),
    )(z, z, qkv, qkv, qkv, qkv, qkv, qkv, do, do, c, c, lse, lse, gains, *extra)


MERGE_ROWS = 256
GATE_TILE = 256


def _group_mix(o_refs, lse_refs):
    lses = [r[...] for r in lse_refs]
    m = jnp.maximum(jnp.maximum(lses[0], lses[1]), lses[2])
    es = [jnp.exp(l - m) for l in lses]
    den = es[0] + es[1] + es[2]
    ws = [e / den for e in es]
    y = ws[0] * o_refs[0][...] + ws[1] * o_refs[1][...] + ws[2] * o_refs[2][...]
    return ws, y


def _sigmoid(v):
    return 1.0 / (1.0 + jnp.exp(-v))


def _merge_specs(T, z, bgate, gpu, gco, gau):
    tm = min(MERGE_ROWS, T)
    row = lambda w: pl.BlockSpec((tm, w), lambda i: (i, 0))
    gate0 = OFF_GATE // GATE_TILE
    gates = [pl.BlockSpec((tm, GATE_TILE), functools.partial(lambda i, cb: (i, cb), cb=gate0 + n))
             for n in range(3 * N_CHIPS)]
    full = lambda a: pl.BlockSpec(a.shape, lambda i: (0,) * a.ndim)
    specs = [row(512), row(512)] + [row(256)] * 6 + gates + [full(bgate), full(gpu), full(gco), full(gau)]
    return tm, row, specs


def _merge_fwd(yp, yc, o3, lse3, z, bgate, gpu, gco, gau, name):
    T = yp.shape[0]
    tm, row, specs = _merge_specs(T, z, bgate, gpu, gco, gau)

    def body(*refs):
        yp_ref, yc_ref = refs[0], refs[1]
        o_refs, lse_refs = refs[2:5], refs[5:8]
        zg = refs[8:20]
        b_ref, gpu_ref, gco_ref, gau_ref, out_ref = refs[20:25]
        yab = _group_mix(o_refs, lse_refs)[1].astype(BF)
        ys = (yp_ref[...], yc_ref[...], yab)
        ups = (gpu_ref, gco_ref, gau_ref)
        for n in range(N_CHIPS):
            acc = None
            for b in range(3):
                gcol = slice(1024 * b + GATE_TILE * n, 1024 * b + GATE_TILE * (n + 1))
                gate = _sigmoid(zg[N_CHIPS * b + n][...] + b_ref[:, gcol])
                term = gate * _dot(ys[b], ups[b][n])
                acc = term if acc is None else acc + term
            out_ref[:, GATE_TILE * n:GATE_TILE * (n + 1)] = acc.astype(BF)

    return _pallas_call(
        body, name=name, grid=(T // tm,), in_specs=specs, out_specs=row(1024),
        out_shape=jax.ShapeDtypeStruct((T, 1024), BF), compiler_params=_params(("parallel",)),
    )(yp, yc, *o3, *lse3, *([z] * 12), bgate, gpu, gco, gau)


def _merge_bwd(dm, yp, yc, o3, lse3, z, bgate, gpu, gco, gau, name):
    T = yp.shape[0]
    tm, row, specs = _merge_specs(T, z, bgate, gpu, gco, gau)
    nsteps = T // tm

    def body(*refs):
        dm_ref, yp_ref, yc_ref = refs[0:3]
        o_refs, lse_refs = refs[3:6], refs[6:9]
        zg = refs[9:21]
        b_ref, gpu_ref, gco_ref, gau_ref = refs[21:25]
        dzg_ref, dyp_ref, dyc_ref = refs[25:28]
        do_refs, c_refs = refs[28:31], refs[31:34]
        dgpu_ref, dgco_ref, dgau_ref, dbg_ref = refs[34:38]
        accs = refs[38:41]
        i = pl.program_id(0)

        @pl.when(i == 0)
        def _():
            for a in accs:
                a[...] = jnp.zeros_like(a)
            dbg_ref[...] = jnp.zeros_like(dbg_ref)

        ws, y = _group_mix(o_refs, lse_refs)
        ys = (yp_ref[...], yc_ref[...], y.astype(BF))
        ups = (gpu_ref, gco_ref, gau_ref)
        dys = [None, None, None]
        for n in range(N_CHIPS):
            dmn = dm_ref[:, GATE_TILE * n:GATE_TILE * (n + 1)]
            for b in range(3):
                gcol = slice(1024 * b + GATE_TILE * n, 1024 * b + GATE_TILE * (n + 1))
                gate = _sigmoid(zg[N_CHIPS * b + n][...] + b_ref[:, gcol])
                up = _dot(ys[b], ups[b][n])
                dzg = (dmn * up) * (gate * (1.0 - gate))
                dzg_ref[:, gcol] = dzg.astype(BF)
                dbg_ref[:, gcol] += jnp.sum(dzg, axis=0, keepdims=True)
                dup = (dmn * gate).astype(BF)
                accs[b][n] += _dot(ys[b], dup, "tn")
                dyb = _dot(dup, ups[b][n], "nt")
                dys[b] = dyb if dys[b] is None else dys[b] + dyb
        dyp_ref[...] = dys[0]
        dyc_ref[...] = dys[1]
        dya = dys[2]
        lane = lax.broadcasted_iota(jnp.int32, dya.shape, 1) // HEAD_DIM
        pr = dya * y
        rho = jnp.zeros_like(pr)
        for h in range(256 // HEAD_DIM):
            hm = lane == h
            rho = jnp.where(hm, jnp.sum(jnp.where(hm, pr, 0.0), axis=-1, keepdims=True), rho)
        for g in range(3):
            do_refs[g][...] = ws[g] * dya
            c_refs[g][...] = -(ws[g] * rho)

        @pl.when(i == nsteps - 1)
        def _():
            dgpu_ref[...] = accs[0][...].astype(BF)
            dgco_ref[...] = accs[1][...].astype(BF)
            dgau_ref[...] = accs[2][...].astype(BF)

    full = lambda a: pl.BlockSpec(a.shape, lambda i: (0,) * a.ndim)
    dz_gate = pl.BlockSpec((pl.Element(tm), pl.Element(3072)), lambda i: (i * tm, OFF_GATE))
    out_specs = ([dz_gate, row(512), row(512)] + [row(256)] * 6 + [full(gpu), full(gco), full(gau)]
                 + [pl.BlockSpec((1, 3072), lambda i: (0, 0))])
    out_shape = ([jax.ShapeDtypeStruct(z.shape, BF)] + [jax.ShapeDtypeStruct((T, 512), F32)] * 2
                 + [jax.ShapeDtypeStruct((T, 256), F32)] * 6
                 + [jax.ShapeDtypeStruct(g.shape, BF) for g in (gpu, gco, gau)]
                 + [jax.ShapeDtypeStruct((1, 3072), F32)])
    return _pallas_call(
        body, name=name, grid=(nsteps,), in_specs=[row(1024)] + specs, out_specs=out_specs, out_shape=out_shape,
        scratch_shapes=[pltpu.VMEM(g.shape, F32) for g in (gpu, gco, gau)],
        compiler_params=_params(("arbitrary",)),
    )(dm, yp, yc, *o3, *lse3, *([z] * 12), bgate, gpu, gco, gau)


def _layer_fwd(x, w, tag, after=None, soon=None, late=None, target=None, hb=None, next_gain=None):
    if hb is None:
        hb = _rms_fwd(x, w["norm_mix"], f"rms_mix_{tag}", after=after)
    if soon is not None:
        w = dict(w, **soon(hb))
    z = _mm(hb, w["w_in"], "nt", f"in_proj_{tag}", tm=512, tn=3712, tk=1024, n_outer=True)
    yp, yc = _poolconv_fwd(z, w["pool_mix"], w["pool_scale"], w["conv_w"], f"poolconv_{tag}")
    qkv = _qk_norm(z, w["qk_gain"], f"qk_norm_{tag}")
    o3, lse3 = [], []
    for g, d in enumerate(ATTN_DILATIONS):
        o, lse = _attn_fwd(qkv, g, d, f"attn{g}_{tag}")
        o3.append(o)
        lse3.append(lse)
    if late is not None:
        w = dict(w, **late(lse3[-1]))
    merged = _merge_fwd(yp, yc, o3, lse3, z, w["b_gate"], w["w_pool_up"], w["w_conv_out"], w["w_attn_up"],
                        f"merge_{tag}")
    x1, h2b = _mm(merged, w["w_o"], "nn", f"out_proj_{tag}", tm=1024, tn=1024, tk=1024, res=x, vec=w["norm_mlp"],
                  epi="rms_next")
    rb = _mm(h2b, w["w_ff1"], "nn", f"ff1_{tag}", tm=1024, tn=1024, tk=1024, out_dtype=BF, epi="relu2", n_outer=True,
             b_shards=True)
    if target is not None:
        x2 = _mm(rb, w["w_ff2"], "nn", f"ff2_{tag}", tm=512, tn=1024, tk=4096, res=x1, aux=target, epi="loss")
    elif next_gain is not None:
        x2 = _mm(rb, w["w_ff2"], "nn", f"ff2_{tag}", tm=512, tn=1024, tk=4096, res=x1, vec=next_gain, epi="rms_next")
    else:
        x2 = _mm(rb, w["w_ff2"], "nn", f"ff2_{tag}", tm=512, tn=1024, tk=4096, res=x1)
    saved = dict(x=x, hb=hb, z=z, yp=yp, yc=yc, qkv=qkv, o3=o3, lse3=lse3, merged=merged, x1=x1, h2b=h2b, rb=rb)
    return x2, saved, w


def _layer_bwd(dx2, w, s, tag, after=None, mid=None, tail=None):
    g = {}
    dab = _mm(dx2, w["w_ff2"], "nt", f"d_ff2_act_{tag}", tm=1024, tn=1024, tk=1024, out_dtype=BF, aux=s["rb"],
              epi="drelu2", after=after)
    g["w_ff2"] = _mm(s["rb"], dx2, "tn", f"d_ff2_w_{tag}", tm=1024, tn=1024, tk=2048, out_dtype=BF)
    g["w_ff1"] = _mm(s["h2b"], dab, "tn", f"d_ff1_w_{tag}", tm=1024, tn=1024, tk=2048, out_dtype=BF, out_shards=True)
    dx1, g["norm_mlp"] = _mm(dab, w["w_ff1"], "nt", f"d_ff1_act_{tag}", tm=1024, tn=1024, tk=1024, b_shards=True,
                             res=dx2, aux=s["x1"], vec=w["norm_mlp"], epi="rms_bwd")
    dm = _mm(dx1, w["w_o"], "nt", f"d_out_act_{tag}", tm=1024, tn=1024, tk=1024)
    g["w_o"] = _mm(s["merged"], dx1, "tn", f"d_out_w_{tag}", tm=1024, tn=1024, tk=1024, out_dtype=BF)
    (dz, dyp, dyc, do0, do1, do2, c0, c1, c2, g["w_pool_up"], g["w_conv_out"], g["w_attn_up"],
     g["b_gate"]) = _merge_bwd(dm, s["yp"], s["yc"], s["o3"], s["lse3"], s["z"], w["b_gate"], w["w_pool_up"],
                               w["w_conv_out"], w["w_attn_up"], f"d_merge_{tag}")
    behind = mid(g) if mid is not None else None
    dq, dk, dv = [], [], []
    dgq = dgk = None
    for gi, d in enumerate(ATTN_DILATIONS):
        dzq, dzk, dzv, pq, pk = _attn_bwd(s["z"], s["qkv"], (do0, do1, do2)[gi], (c0, c1, c2)[gi], s["lse3"][gi],
                                          w["qk_gain"], gi, d, f"d_attn{gi}_{tag}", after=behind)
        dq.append(dzq)
        dk.append(dzk)
        dv.append(dzv)
        dgq = pq if dgq is None else dgq + pq
        dgk = pk if dgk is None else dgk + pk
    g["q_gain"] = dgq[:, :HEAD_DIM] + dgq[:, HEAD_DIM:]
    g["k_gain"] = dgk[:, :HEAD_DIM] + dgk[:, HEAD_DIM:]
    for off, pieces in ((OFF_Q, dq), (OFF_K, dk), (OFF_V, dv)):
        for gi, piece in enumerate(pieces):
            dz = lax.dynamic_update_slice(dz, piece, (0, off + 256 * gi))
    dz, g["pool_mix"], g["pool_scale"], g["conv_w"] = _poolconv_bwd(
        s["z"], dyp, dyc, w["pool_mix"], w["pool_scale"], w["conv_w"], dz, f"d_poolconv_{tag}")
    g["w_in"] = _mm(s["hb"], dz, "tn", f"d_in_w_{tag}", tm=512, tn=3712, tk=1024, out_dtype=BF)
    dh = _mm(dz, w["w_in"], "nn", f"d_in_act_{tag}", tm=1024, tn=1024, tk=3712,
             after=tail(g) if tail is not None else None)
    dx, g["norm_mix"] = _rms_bwd(dh, s["x"], w["norm_mix"], dx1, f"d_rms_mix_{tag}")
    return dx, g


def _position():
    x, y, c = lax.axis_index("x"), lax.axis_index("y"), lax.axis_index("c")
    chips = [(1 - x, y), (x, 1 - y), (1 - x, 1 - y)]
    return x, y, c, 2 * x + y, chips, [2 * cx + cy for cx, cy in chips]


def _remote(src, dst, ssem, rsem, dev):
    return pltpu.make_async_remote_copy(src_ref=src, dst_ref=dst, send_sem=ssem, recv_sem=rsem, device_id=dev,
                                        device_id_type=MESH_ID)


def _halves(a):
    return a.reshape(a.shape[0], 2, a.shape[1] // 2, a.shape[2])


SEM = pl.BlockSpec(memory_space=pltpu.SEMAPHORE)
TOKEN = jax.ShapeDtypeStruct((8, LANES), F32)
TOKEN_SPEC = pl.BlockSpec(memory_space=pltpu.VMEM)


def _split_params():
    return pltpu.CompilerParams(has_side_effects=pltpu.SideEffectType.DATAFLOW_SIDE_EFFECTING)


def _gather_start(bufs, name, after):
    n = len(bufs)
    views = [_halves(b) for b in bufs]

    def body(*refs):
        first_sem = n + 1
        ssem, rsem = refs[first_sem:first_sem + ns], refs[first_sem + ns:first_sem + 2 * ns]
        outs, token = refs[first_sem + 2 * ns:first_sem + 2 * ns + n], refs[first_sem + 2 * ns + n]
        x, y, c, q, chips, qs = _position()
        for k in range(n):
            mine = outs[k].at[q, c]
            for j, chip in enumerate(chips):
                _remote(mine, mine, ssem[3 * k + j], rsem[3 * k + j], (chip[0], chip[1], c)).start()
        token[...] = jnp.zeros_like(token)

    ns = 3 * n
    outs = _pallas_call(
        body, name=name, in_specs=[ANY] * (n + 1), out_specs=[SEM] * (2 * ns) + [ANY] * n + [TOKEN_SPEC],
        out_shape=[pltpu.SemaphoreType.DMA(())] * (2 * ns) + [jax.ShapeDtypeStruct(v.shape, v.dtype) for v in views]
        + [TOKEN],
        input_output_aliases={k: k + 2 * ns for k in range(n)}, compiler_params=_split_params(),
    )(*views, after)
    return list(outs[:ns]), list(outs[ns:2 * ns]), list(outs[2 * ns:2 * ns + n]), outs[2 * ns + n]


def _gather_finish(ssem, rsem, views, after, name_wait, name_forward, shapes):
    n = len(views)
    ns = len(ssem)

    def wait_body(*refs):
        ssem_ref, rsem_ref = refs[n:n + ns], refs[n + ns:n + 2 * ns]
        outs = refs[n + 2 * ns + 1:]
        x, y, c, q, chips, qs = _position()
        for k in range(n):
            for j, chip in enumerate(chips):
                cp = _remote(outs[k].at[q, c], outs[k].at[qs[j], c], ssem_ref[3 * k + j], rsem_ref[3 * k + j],
                             (chip[0], chip[1], c))
                cp.wait_send()
                cp.wait_recv()

    landed = _pallas_call(
        wait_body, name=name_wait, in_specs=[ANY] * n + [SEM] * (2 * ns) + [ANY], out_specs=[ANY] * n,
        out_shape=[jax.ShapeDtypeStruct(v.shape, v.dtype) for v in views],
        input_output_aliases={k: k for k in range(n)}, compiler_params=_split_params(),
    )(*views, *ssem, *rsem, after)

    def forward_body(*refs):
        outs = refs[n:2 * n]
        fssem, frsem = refs[2 * n:]
        x, y, c, q, chips, qs = _position()
        sib = (x, y, 1 - c)
        sent = []
        for k in range(n):
            for j in range(3):
                slot = outs[k].at[qs[j], c]
                cp = _remote(slot, slot, fssem.at[k, j], frsem.at[k, j], sib)
                cp.start()
                sent.append(cp)
        for k in range(n):
            for j in range(3):
                slot = outs[k].at[qs[j], 1 - c]
                _remote(slot, slot, fssem.at[k, j], frsem.at[k, j], sib).wait_recv()
        for cp in sent:
            cp.wait_send()

    outs = _pallas_call(
        forward_body, name=name_forward, in_specs=[ANY] * n, out_specs=[ANY] * n,
        out_shape=[jax.ShapeDtypeStruct(v.shape, v.dtype) for v in views],
        input_output_aliases={k: k for k in range(n)}, scratch_shapes=[pltpu.SemaphoreType.DMA((n, 3))] * 2,
    )(*landed)
    return [o.reshape(s) for o, s in zip(outs, shapes)]


def _chip_exchange_start(parts, name):
    n = len(parts)

    def body(*refs):
        ssem, rsem = refs[n:n + ns], refs[n + ns:n + 2 * ns]
        base = n + 2 * ns
        srcs, outs, token = refs[base:base + n], refs[base + n:base + 2 * n], refs[base + 2 * n]
        x, y, c, q, chips, qs = _position()
        for k in range(n):
            for j, chip in enumerate(chips):
                _remote(srcs[k].at[qs[j]], outs[k].at[j], ssem[3 * k + j], rsem[3 * k + j],
                        (chip[0], chip[1], c)).start()
        token[...] = jnp.zeros_like(token)

    ns = 3 * n
    outs = _pallas_call(
        body, name=name, in_specs=[ANY] * n, out_specs=[SEM] * (2 * ns) + [ANY] * (2 * n) + [TOKEN_SPEC],
        out_shape=[pltpu.SemaphoreType.DMA(())] * (2 * ns) + [jax.ShapeDtypeStruct(a.shape, a.dtype) for a in parts]
        + [jax.ShapeDtypeStruct((3,) + a.shape[1:], a.dtype) for a in parts] + [TOKEN],
        input_output_aliases={k: k + 2 * ns for k in range(n)}, compiler_params=_split_params(),
    )(*parts)
    b = 2 * ns
    return list(outs[:ns]), list(outs[ns:b]), list(outs[b:b + n]), list(outs[b + n:b + 2 * n]), outs[b + 2 * n]


def _chip_exchange_wait(ssem, rsem, parts, landing, after, name):
    n = len(parts)
    ns = len(ssem)

    def body(*refs):
        ssem_ref, rsem_ref = refs[2 * n:2 * n + ns], refs[2 * n + ns:2 * n + 2 * ns]
        base = 2 * n + 2 * ns + 1
        srcs, outs = refs[base:base + n], refs[base + n:]
        x, y, c, q, chips, qs = _position()
        for k in range(n):
            for j, chip in enumerate(chips):
                cp = _remote(srcs[k].at[qs[j]], outs[k].at[j], ssem_ref[3 * k + j], rsem_ref[3 * k + j],
                             (chip[0], chip[1], c))
                cp.wait_send()
                cp.wait_recv()

    outs = _pallas_call(
        body, name=name, in_specs=[ANY] * (2 * n) + [SEM] * (2 * ns) + [ANY], out_specs=[ANY] * (2 * n),
        out_shape=[jax.ShapeDtypeStruct(a.shape, a.dtype) for a in list(parts) + list(landing)],
        input_output_aliases={k: k for k in range(2 * n)}, compiler_params=_split_params(),
    )(*parts, *landing, *ssem, *rsem, after)
    return list(outs[:n]), list(outs[n:])


def _pair_swap(views, name):
    n = len(views)

    def body(*refs):
        ins, outs = refs[:n], refs[n:2 * n]
        ssem, rsem = refs[2 * n:]
        x, y, c, _, _, _ = _position()
        cps = [_remote(ins[k].at[pl.ds(0, N_CHIPS), 1 - c], outs[k], ssem.at[k], rsem.at[k], (x, y, 1 - c))
               for k in range(n)]
        for cp in cps:
            cp.start()
        for cp in cps:
            cp.wait()

    return _pallas_call(
        body, name=name, in_specs=[ANY] * n, out_specs=[ANY] * n,
        out_shape=[jax.ShapeDtypeStruct((v.shape[0],) + v.shape[2:], v.dtype) for v in views],
        scratch_shapes=[pltpu.SemaphoreType.DMA((n,))] * 2,
    )(*views)


def _pair_send(arrays, name):
    n = len(arrays)

    def body(*refs):
        ins, outs = refs[:n], refs[n:2 * n]
        ssem, rsem = refs[2 * n:]
        x, y, c, _, _, _ = _position()
        cps = [_remote(ins[k], outs[k], ssem.at[k], rsem.at[k], (x, y, 1 - c)) for k in range(n)]
        for cp in cps:
            cp.start()
        for cp in cps:
            cp.wait()

    return _pallas_call(
        body, name=name, in_specs=[ANY] * n, out_specs=[ANY] * n,
        out_shape=[jax.ShapeDtypeStruct(a.shape, a.dtype) for a in arrays],
        scratch_shapes=[pltpu.SemaphoreType.DMA((n,))] * 2,
    )(*arrays)


def _all_to_all_small(part):
    P = part.shape[0]

    def body(in_ref, out_ref, lsem, ssem, rsem):
        x, y, c = lax.axis_index("x"), lax.axis_index("y"), lax.axis_index("c")
        me = 4 * x + 2 * y + c
        flips = [(fx, fy, fc) for fx in (0, 1) for fy in (0, 1) for fc in (0, 1)][1:]
        peers = [((x + fx) % 2, (y + fy) % 2, (c + fc) % 2) for fx, fy, fc in flips]
        loc = pltpu.make_async_copy(in_ref, out_ref.at[me], lsem)
        loc.start()
        cps = [_remote(in_ref, out_ref.at[me], ssem.at[j], rsem.at[j], peer) for j, peer in enumerate(peers)]
        for cp in cps:
            cp.start()
        for j, (px, py, pc) in enumerate(peers):
            _remote(in_ref, out_ref.at[4 * px + 2 * py + pc], ssem.at[j], rsem.at[j], peers[j]).wait_recv()
        for cp in cps:
            cp.wait_send()
        loc.wait()

    return _pallas_call(
        body, name="small_exchange", in_specs=[ANY], out_specs=ANY,
        out_shape=jax.ShapeDtypeStruct((8, P, LANES), F32),
        scratch_shapes=[pltpu.SemaphoreType.DMA(())] + [pltpu.SemaphoreType.DMA((7,))] * 2,
    )(part)


def _small_peers():
    x, y, c = lax.axis_index("x"), lax.axis_index("y"), lax.axis_index("c")
    flips = [(fx, fy, fc) for fx in (0, 1) for fy in (0, 1) for fc in (0, 1)][1:]
    peers = [((x + fx) % 2, (y + fy) % 2, (c + fc) % 2) for fx, fy, fc in flips]
    return 4 * x + 2 * y + c, peers


def _all_to_all_small_start(part, name):
    P = part.shape[0]
    me = 4 * lax.axis_index("x") + 2 * lax.axis_index("y") + lax.axis_index("c")
    landing = lax.dynamic_update_slice(jnp.zeros((8, P, LANES), F32), part[None], (me, 0, 0))

    def body(*refs):
        sems, src, land, token = refs[2:16], refs[16], refs[17], refs[18]
        me_, peers = _small_peers()
        for j, peer in enumerate(peers):
            _remote(src, land.at[me_], sems[j], sems[7 + j], peer).start()
        token[...] = jnp.zeros_like(token)

    outs = _pallas_call(
        body, name=name, in_specs=[ANY, ANY], out_specs=[SEM] * 14 + [ANY, ANY, TOKEN_SPEC],
        out_shape=[pltpu.SemaphoreType.DMA(())] * 14 + [jax.ShapeDtypeStruct(part.shape, F32),
                                                       jax.ShapeDtypeStruct((8, P, LANES), F32), TOKEN],
        input_output_aliases={0: 14, 1: 15}, compiler_params=_split_params(),
    )(part, landing)
    return list(outs[:7]), list(outs[7:14]), outs[14], outs[15], outs[16]


def _all_to_all_small_wait(ssem, rsem, part, landing, after, name):
    def body(*refs):
        sems, src, land = refs[2:16], refs[17], refs[18]
        _, peers = _small_peers()
        for j, (px, py, pc) in enumerate(peers):
            cp = _remote(src, land.at[4 * px + 2 * py + pc], sems[j], sems[7 + j], peers[j])
            cp.wait_send()
            cp.wait_recv()

    return _pallas_call(
        body, name=name, in_specs=[ANY, ANY] + [SEM] * 14 + [ANY], out_specs=[ANY, ANY],
        out_shape=[jax.ShapeDtypeStruct(part.shape, F32), jax.ShapeDtypeStruct(landing.shape, F32)],
        input_output_aliases={0: 0, 1: 1}, compiler_params=_split_params(),
    )(part, landing, *ssem, *rsem, after)[1]


def _row_tile(rows, width, n_arrays):
    t = rows
    while t % 2 == 0 and t > 8 and 2 * n_arrays * t * width * 4 > VMEM_LIMIT // 2:
        t //= 2
    return t


def _chip():
    return 2 * lax.axis_index("x") + lax.axis_index("y")


def _core():
    return lax.axis_index("c")


def _cast_place(w3, layer, name):
    _, r, c = w3.shape
    tr = _row_tile(r, c, 2)

    def body(w_ref, o_ref):
        o_ref[...] = w_ref[...].astype(BF)

    return _pallas_call(
        body, name=name, grid=(r // tr,), in_specs=[pl.BlockSpec((None, tr, c), lambda i: (layer, i, 0))],
        out_specs=pl.BlockSpec((None, tr, c), lambda i: (_chip(), i, 0)),
        out_shape=jax.ShapeDtypeStruct((N_CHIPS, r, c), BF), compiler_params=_params(("parallel",)),
    )(w3)


def _pair_sum(view, recv, name):
    _, _, hr, c = view.shape
    tr = _row_tile(hr, c, 3)

    def body(g_ref, r_ref, o_ref):
        o_ref[...] = (g_ref[...].astype(F32) + r_ref[...].astype(F32)).astype(BF)

    blk = pl.BlockSpec((None, tr, c), lambda p, i: (p, i, 0))
    return _pallas_call(
        body, name=name, grid=(N_CHIPS, hr // tr),
        in_specs=[pl.BlockSpec((None, None, tr, c), lambda p, i: (p, _core(), i, 0)), blk], out_specs=blk,
        out_shape=jax.ShapeDtypeStruct(recv.shape, BF), compiler_params=_params(("parallel", "parallel")),
    )(view, recv)


def _chip_sum(parts, recv, name):
    _, hr, c = parts.shape
    tr = _row_tile(hr, c, 6)

    def body(p_ref, r_ref, o_ref):
        acc = p_ref[...].astype(F32)
        for j in range(3):
            acc = acc + r_ref[j].astype(F32)
        o_ref[...] = acc

    return _pallas_call(
        body, name=name, grid=(hr // tr,),
        in_specs=[pl.BlockSpec((None, tr, c), lambda i: (_chip(), i, 0)), pl.BlockSpec((3, tr, c), lambda i: (0, i, 0))],
        out_specs=pl.BlockSpec((tr, c), lambda i: (i, 0)),
        out_shape=jax.ShapeDtypeStruct((hr, c), F32), compiler_params=_params(("parallel",)),
    )(parts, recv)


def _sum_slices(a, name):
    n, rows, width = a.shape
    tr = _row_tile(rows, width, n + 1)

    def body(a_ref, o_ref):
        acc = a_ref[0].astype(F32)
        for i in range(1, n):
            acc = acc + a_ref[i].astype(F32)
        o_ref[...] = acc

    return _pallas_call(
        body, name=name, grid=(rows // tr,), in_specs=[pl.BlockSpec((n, tr, width), lambda i: (0, i, 0))],
        out_specs=pl.BlockSpec((tr, width), lambda i: (i, 0)), out_shape=jax.ShapeDtypeStruct((rows, width), F32),
        compiler_params=_params(("parallel",)),
    )(a)


def _adamw_update(w, g, m, v):
    nm = ADAM_B1 * m + (1.0 - ADAM_B1) * g
    nv = ADAM_B2 * v + (1.0 - ADAM_B2) * (g * g)
    m_hat = nm / (1.0 - ADAM_B1 ** ADAM_STEP)
    v_hat = nv / (1.0 - ADAM_B2 ** ADAM_STEP)
    return -ADAM_LR * (m_hat / (jnp.sqrt(v_hat) + ADAM_EPS) + ADAM_WD * w), nm, nv


def _adamw(w, g, m, v, name):
    rows, width = w.shape
    tr = _row_tile(rows, width, 7)

    def body(w_ref, g_ref, m_ref, v_ref, d_ref, nm_ref, nv_ref):
        d_ref[...], nm_ref[...], nv_ref[...] = _adamw_update(w_ref[...], g_ref[...], m_ref[...], v_ref[...])

    blk = pl.BlockSpec((tr, width), lambda i: (i, 0))
    return _pallas_call(
        body, name=name, grid=(rows // tr,), in_specs=[blk] * 4, out_specs=[blk] * 3,
        out_shape=[jax.ShapeDtypeStruct((rows, width), F32)] * 3, compiler_params=_params(("parallel",)),
    )(w, g, m, v)


def _adamw_halves(w3, m3, v3, mine, other, name):
    depth, r, c = w3.shape
    assert depth == 2
    hr = r // 2
    tr = _row_tile(hr, c, 11)
    sources = ((0, True, mine[0]), (0, False, other[0]), (1, True, mine[1]), (1, False, other[1]))

    def active(l, h, layer, own):
        mine_half = h == _core()
        return (l == layer) & (mine_half if own else jnp.logical_not(mine_half))

    def body(w_ref, m_ref, v_ref, *rest):
        g_refs, (go_ref, d_ref, nm_ref, nv_ref) = rest[:4], rest[4:]
        l, h = pl.program_id(0), pl.program_id(1)
        for (layer, own, _), g_ref in zip(sources, g_refs):
            @pl.when(active(l, h, layer, own))
            def _():
                gv = g_ref[...]
                go_ref[...] = gv
                d_ref[...], nm_ref[...], nv_ref[...] = _adamw_update(w_ref[...], gv, m_ref[...], v_ref[...])

    def gspec(layer, own):
        return pl.BlockSpec((tr, c), lambda l, h, i: (jnp.where(active(l, h, layer, own), i, 0), 0))

    blk = pl.BlockSpec((None, None, tr, c), lambda l, h, i: (l, h, i, 0))
    view = lambda a: a.reshape(depth, 2, hr, c)
    outs = _pallas_call(
        body, name=name, grid=(depth, 2, hr // tr),
        in_specs=[blk] * 3 + [gspec(layer, own) for layer, own, _ in sources], out_specs=[blk] * 4,
        out_shape=[jax.ShapeDtypeStruct((depth, 2, hr, c), F32)] * 4,
        compiler_params=_params(("parallel", "parallel", "parallel")),
    )(view(w3), view(m3), view(v3), *[s[2] for s in sources])
    return [o.reshape(w3.shape) for o in outs]


BIG = ("w_in", "w_pool_up", "w_conv_out", "w_attn_up", "w_o", "w_ff1", "w_ff2")
SMALL = ("norm_mix", "b_gate", "pool_mix", "pool_scale", "conv_w", "q_gain", "k_gain", "norm_mlp")
ORDER = ("norm_mix", "w_in", "b_gate", "pool_mix", "pool_scale", "conv_w", "q_gain", "k_gain", "w_pool_up",
         "w_conv_out", "w_attn_up", "w_o", "norm_mlp", "w_ff1", "w_ff2")
COLUMN_SHARDED = ("w_pool_up", "w_conv_out", "w_attn_up", "w_ff1")


def _matrix_weights(gathered):
    w = {}
    for name, g4 in gathered.items():
        if name in COLUMN_SHARDED:
            w[name] = g4
        else:
            w[name] = g4.reshape(N_CHIPS * g4.shape[1], g4.shape[2])
    return w


def _small_weights(l, small):
    w = {}
    w["norm_mix"] = small["norm_mix"][l][None]
    w["norm_mlp"] = small["norm_mlp"][l][None]
    w["b_gate"] = small["b_gate"][l][None]
    w["pool_mix"] = small["pool_mix"][l].astype(BF)
    w["pool_scale"] = small["pool_scale"][l][None]
    w["conv_w"] = jnp.pad(small["conv_w_full"][l], ((0, 5), (0, 0)))
    w["qk_gain"] = jnp.pad(jnp.stack([jnp.tile(small["q_gain"][l], 2), jnp.tile(small["k_gain"][l], 2)]), ((0, 6), (0, 0)))
    return w


def _to_chip_major(name, g):
    if name == "w_in":
        return g.T.reshape(N_CHIPS, g.shape[1] // N_CHIPS, g.shape[0])
    if name in COLUMN_SHARDED:
        return g
    return g.reshape(N_CHIPS, g.shape[0] // N_CHIPS, g.shape[1])


def _pad8(a):
    a = a.reshape(-1)
    return jnp.pad(a, (0, (-a.size) % (8 * LANES))).reshape(-1, LANES)


def kernel(x, norm_mix, w_in, b_gate, pool_mix, pool_scale, conv_w, q_gain, k_gain, w_pool_up, w_conv_out, w_attn_up, w_o, norm_mlp, w_ff1, w_ff2, loss_target, m_norm_mix, m_w_in, m_b_gate, m_pool_mix, m_pool_scale, m_conv_w, m_q_gain, m_k_gain, m_w_pool_up, m_w_conv_out, m_w_attn_up, m_w_o, m_norm_mlp, m_w_ff1, m_w_ff2, v_norm_mix, v_w_in, v_b_gate, v_pool_mix, v_pool_scale, v_conv_w, v_q_gain, v_k_gain, v_w_pool_up, v_w_conv_out, v_w_attn_up, v_w_o, v_norm_mlp, v_w_ff1, v_w_ff2):
    weights = dict(norm_mix=norm_mix, w_in=w_in, b_gate=b_gate, pool_mix=pool_mix, pool_scale=pool_scale, conv_w=conv_w,
                   q_gain=q_gain, k_gain=k_gain, w_pool_up=w_pool_up, w_conv_out=w_conv_out, w_attn_up=w_attn_up,
                   w_o=w_o, norm_mlp=norm_mlp, w_ff1=w_ff1, w_ff2=w_ff2)
    moms = dict(norm_mix=m_norm_mix, w_in=m_w_in, b_gate=m_b_gate, pool_mix=m_pool_mix, pool_scale=m_pool_scale,
                conv_w=m_conv_w, q_gain=m_q_gain, k_gain=m_k_gain, w_pool_up=m_w_pool_up, w_conv_out=m_w_conv_out,
                w_attn_up=m_w_attn_up, w_o=m_w_o, norm_mlp=m_norm_mlp, w_ff1=m_w_ff1, w_ff2=m_w_ff2)
    vels = dict(norm_mix=v_norm_mix, w_in=v_w_in, b_gate=v_b_gate, pool_mix=v_pool_mix, pool_scale=v_pool_scale,
                conv_w=v_conv_w, q_gain=v_q_gain, k_gain=v_k_gain, w_pool_up=v_w_pool_up, w_conv_out=v_w_conv_out,
                w_attn_up=v_w_attn_up, w_o=v_w_o, norm_mlp=v_norm_mlp, w_ff1=v_w_ff1, w_ff2=v_w_ff2)
    depth = norm_mix.shape[0]
    q = 2 * lax.axis_index("x") + lax.axis_index("y")
    for group in (weights, moms, vels):
        group["w_in"] = jnp.swapaxes(group["w_in"], 1, 2)

    assert depth == 2, "the second layer's gather hides behind the first layer's forward, and likewise backward"
    first, rest = BIG[:1], BIG[1:]
    cw_all = _all_to_all_small(_pad8(conv_w))
    bufs = [{n: _cast_place(weights[n], 0, f"cast_{n}_l0") for n in first}]
    a_ssem, a_rsem, a_views, a_token = _gather_start([bufs[0][n] for n in first], "gather_start_l0_in", cw_all)
    bufs[0].update({n: _cast_place(weights[n], 0, f"cast_{n}_l0") for n in rest})
    bufs += [{n: _cast_place(weights[n], l, f"cast_{n}_l{l}") for n in BIG} for l in range(1, depth)]
    b_ssem, b_rsem, b_views, b_token = _gather_start([bufs[0][n] for n in rest], "gather_start_l0_rest", a_token)
    g_ssem, g_rsem, g_views, g_token = _gather_start([bufs[1][n] for n in BIG], "gather_start_l1", b_token)
    conv_w_full = jnp.concatenate(
        [cw_all[2 * p].reshape(-1)[:conv_w.size].reshape(conv_w.shape) for p in range(N_CHIPS)], axis=-1)
    small = dict(weights)
    small["conv_w_full"] = conv_w_full

    def soon_weights(t):
        got = _gather_finish(a_ssem, a_rsem, a_views, t, "gather_wait_l0_in", "gather_forward_l0_in",
                             [bufs[0][n].shape for n in first])
        return _matrix_weights(dict(zip(first, got)))

    def late_weights(t):
        got = _gather_finish(b_ssem, b_rsem, b_views, t, "gather_wait_l0_rest", "gather_forward_l0_rest",
                             [bufs[0][n].shape for n in rest])
        return _matrix_weights(dict(zip(rest, got)))

    wl, saved = [None] * depth, [None] * depth
    small_1 = _small_weights(1, small)
    (h, hb_1), saved[0], wl[0] = _layer_fwd(x[0], _small_weights(0, small), "l0", after=g_token, soon=soon_weights,
                                            late=late_weights, next_gain=small_1["norm_mix"])
    got = _gather_finish(g_ssem, g_rsem, g_views, h, "gather_wait_l1", "gather_forward_l1",
                         [bufs[1][n].shape for n in BIG])
    (dh, loss_row), saved[1], wl[1] = _layer_fwd(
        h, dict(small_1, **_matrix_weights(dict(zip(BIG, got)))), "l1", target=loss_target[0], hb=hb_1)

    def pair_stage(names, g, tag):
        views = [_halves(_to_chip_major(n, g[n])) for n in names]
        from_sibling = _pair_swap(views, f"grad_pair_swap_{tag}")
        return [_pair_sum(views[k], from_sibling[k], f"pair_sum_{n}_{tag}") for k, n in enumerate(names)]

    mine, other = [{}, {}], [{}, {}]

    def finish(names, l, started, after, tag):
        ssem, rsem, parts, landing, _ = started
        parts, arrived = _chip_exchange_wait(ssem, rsem, parts, landing, after, f"grad_chip_exchange_wait_{tag}")
        got = [_chip_sum(parts[k], arrived[k], f"chip_sum_{n}_{tag}") for k, n in enumerate(names)]
        mine[l].update(zip(names, got))
        other[l].update(zip(names, _pair_send(got, f"grad_pair_send_{tag}")))

    def small_pieces(g):
        return [_pad8(g[n][:3] if n == "conv_w" else g[n]) for n in SMALL]

    def start_small(l):
        return _all_to_all_small_start(jnp.concatenate(small_pieces(grads[l]), axis=0), f"small_grad_exchange_start_l{l}")

    grads, early, small = [None] * depth, {}, [None] * depth
    dh, grads[1] = _layer_bwd(dh, wl[1], saved[1], "l1")
    second = _chip_exchange_start(pair_stage(BIG, grads[1], "l1"), "grad_chip_exchange_start_l1")
    small[1] = start_small(1)

    def start_rest(g):
        early["rest"] = _chip_exchange_start(pair_stage(rest, g, "l0_rest"), "grad_chip_exchange_start_l0_rest")
        return early["rest"][4]

    def start_last(g):
        early["in"] = _chip_exchange_start(pair_stage(first, g, "l0_in"), "grad_chip_exchange_start_l0_in")
        return early["in"][4]

    dh, grads[0] = _layer_bwd(dh, wl[0], saved[0], "l0", after=[second[4], small[1][4]], mid=start_rest,
                              tail=start_last)
    small[0] = start_small(0)
    finish(BIG, 1, second, dh, "l1")
    finish(rest, 0, early["rest"], dh, "l0_rest")
    loss = lax.psum(loss_row[0, 0], ("x", "y", "c"))
    full = {}

    deltas, new_m, new_v = {}, {}, {}

    def update_matrix(n):
        full[n], deltas[n], new_m[n], new_v[n] = _adamw_halves(
            weights[n], moms[n], vels[n], [mine[l][n] for l in range(depth)], [other[l][n] for l in range(depth)],
            f"adamw_{n}")

    for n in rest:
        update_matrix(n)
    finish(first, 0, early["in"], deltas[rest[-1]], "l0_in")
    for n in first:
        update_matrix(n)
    summed = []
    for l in range(depth):
        ssem, rsem, part, landing, _ = small[l]
        summed.append(_sum_slices(_all_to_all_small_wait(ssem, rsem, part, landing, deltas[first[-1]],
                                                         f"small_grad_exchange_wait_l{l}"), f"small_sum_l{l}"))
    row = 0
    for n, piece in zip(SMALL, small_pieces(grads[0])):
        size = (weights[n].size if n != "conv_w" else depth * 3 * 512) // depth
        flat = jnp.stack([s[row:row + piece.shape[0]].reshape(-1)[:size] for s in summed])
        row += piece.shape[0]
        if n == "conv_w":
            full[n] = lax.dynamic_slice_in_dim(flat.reshape(depth, 3, 512), q * conv_w.shape[2], conv_w.shape[2], axis=2)
        else:
            full[n] = flat.reshape(weights[n].shape)
    for n in SMALL:
        shape = weights[n].shape
        two_d = (-1, shape[-1]) if n not in ("conv_w", "q_gain", "k_gain") else (1, -1)
        d2, m2, v2 = _adamw(weights[n].reshape(two_d), full[n].reshape(two_d), moms[n].reshape(two_d),
                            vels[n].reshape(two_d), f"adamw_{n}")
        deltas[n], new_m[n], new_v[n] = d2.reshape(shape), m2.reshape(shape), v2.reshape(shape)
        full[n] = full[n].reshape(shape)
    for group in (full, deltas, new_m, new_v):
        group["w_in"] = jnp.swapaxes(group["w_in"], 1, 2)
    return (loss, dh[None], *[full[n] for n in ORDER], *[deltas[n] for n in ORDER], *[new_m[n] for n in ORDER],
            *[new_v[n] for n in ORDER])
```

```python
import functools

import jax
import jax.numpy as jnp
from jax import lax
from jax.experimental import pallas as pl
from jax.experimental.pallas import tpu as pltpu

F32 = jnp.float32
BF = jnp.bfloat16
MESH_ID = pl.DeviceIdType.MESH
ANY = pl.BlockSpec(memory_space=pl.ANY)

EPS = 1e-6
MASK_VALUE = -1e30
POOL_WINDOWS = (2, 4, 8, 16)
ATTN_DILATIONS = (1, 4, 16)
ATTN_BLOCK = 128
HEAD_DIM = 64
OFF_Q, OFF_K, OFF_V, OFF_GATE = 2048, 2816, 3584, 4352
N_CHIPS = 4
ADAM_LR, ADAM_B1, ADAM_B2, ADAM_EPS, ADAM_WD, ADAM_STEP = 0.001, 0.9, 0.999, 1e-08, 0.01, 10

VMEM_LIMIT = 48 * 1024 * 1024
LANES = 128

_DIMS = {"nn": (((1,), (0,)), ((), ())), "nt": (((1,), (1,)), ((), ())), "tn": (((0,), (0,)), ((), ()))}


def _params(sem):
    return pltpu.CompilerParams(dimension_semantics=sem, vmem_limit_bytes=VMEM_LIMIT)


def _pallas_call(body, **kw):
    def in_hbm(s):
        pin = isinstance(s, jax.ShapeDtypeStruct) and s is not TOKEN and jnp.issubdtype(s.dtype, jnp.floating)
        return pltpu.HBM(s.shape, s.dtype) if pin else s

    out_shape = kw.pop("out_shape")
    kw["out_shape"] = [in_hbm(s) for s in out_shape] if isinstance(out_shape, (list, tuple)) else in_hbm(out_shape)
    call = pl.pallas_call(body, **kw)

    def run(*args):
        pinned = [pltpu.with_memory_space_constraint(a, pltpu.HBM)
                  if hasattr(a, "dtype") and jnp.issubdtype(a.dtype, jnp.floating) else a for a in args]
        return call(*pinned)

    return run


def _dot(a, b, mode="nn"):
    return lax.dot_general(a, b, _DIMS[mode], preferred_element_type=F32)


def _mm(a, b, mode, name, *, tm, tn, tk, out_dtype=F32, res=None, aux=None, epi=None, n_outer=False,
        b_shards=False, out_shards=False, after=None, vec=None):
    if mode == "tn":
        K, M = a.shape
    else:
        M, K = a.shape
    if b_shards:
        if mode == "nn":
            assert b.shape[1] == K
            N = b.shape[2] * N_CHIPS
        else:
            assert mode == "nt"
            N = b.shape[1]
            assert b.shape[2] * N_CHIPS == K
    else:
        N = b.shape[0] if mode == "nt" else b.shape[1]
    tm, tn, tk = min(tm, M), min(tn, N), min(tk, K)
    assert M % tm == 0 and N % tn == 0 and K % tk == 0
    nk = K // tk
    if n_outer:
        grid = (N // tn, M // tm, nk)
        ij = lambda p, q_: (q_, p)
    else:
        grid = (M // tm, N // tn, nk)
        ij = lambda p, q_: (p, q_)

    def amap(p, q_, k):
        i, j = ij(p, q_)
        return (k, i) if mode == "tn" else (i, k)

    a_spec = pl.BlockSpec((tk, tm) if mode == "tn" else (tm, tk), amap)
    if b_shards:
        if mode == "nn":
            per = (N // N_CHIPS) // tn
            assert per >= 1 and (N // N_CHIPS) % tn == 0

            def bmap(p, q_, k):
                i, j = ij(p, q_)
                return (j // per, k, j % per)

            b_spec = pl.BlockSpec((None, tk, tn), bmap)
        else:
            per = (K // N_CHIPS) // tk
            assert per >= 1 and (K // N_CHIPS) % tk == 0

            def bmap(p, q_, k):
                i, j = ij(p, q_)
                return (k // per, j, k % per)

            b_spec = pl.BlockSpec((None, tn, tk), bmap)
    else:
        def bmap(p, q_, k):
            i, j = ij(p, q_)
            return (j, k) if mode == "nt" else (k, j)

        b_spec = pl.BlockSpec((tn, tk) if mode == "nt" else (tk, tn), bmap)

    def omap(p, q_, k):
        return ij(p, q_)

    o_spec = pl.BlockSpec((tm, tn), omap)
    if out_shards:
        per_o = (N // N_CHIPS) // tn
        assert per_o >= 1 and (N // N_CHIPS) % tn == 0

        def osmap(p, q_, k):
            i, j = ij(p, q_)
            return (j // per_o, i, j % per_o)

        out_spec0 = pl.BlockSpec((None, tm, tn), osmap)
        out_shape0 = jax.ShapeDtypeStruct((N_CHIPS, M, N // N_CHIPS), out_dtype)
    else:
        out_spec0 = o_spec
        out_shape0 = jax.ShapeDtypeStruct((M, N), out_dtype)

    in_specs = [a_spec, b_spec]
    args = [a, b]
    if res is not None:
        in_specs.append(o_spec)
        args.append(res)
    if aux is not None:
        in_specs.append(o_spec)
        args.append(aux)
    if vec is not None:
        in_specs.append(pl.BlockSpec((1, tn), lambda p, q_, k: (0, ij(p, q_)[1])))
        args.append(vec)
    after = [] if after is None else list(after) if isinstance(after, (list, tuple)) else [after]
    in_specs += [ANY] * len(after)
    args += after
    out_specs = [out_spec0]
    out_shape = [out_shape0]
    reduces = epi in ("loss", "rms_bwd")
    if reduces:
        assert tn == N and not n_outer and not out_shards
        width = LANES if epi == "loss" else N
        out_specs.append(pl.BlockSpec((1, width), lambda p, q_, k: (0, 0)))
        out_shape.append(jax.ShapeDtypeStruct((1, width), F32))
    if epi == "rms_next":
        assert tn == N and not out_shards
        out_specs.append(o_spec)
        out_shape.append(jax.ShapeDtypeStruct((M, N), BF))
    n_out = len(out_shape)
    has_res, has_aux, has_vec, n_after = res is not None, aux is not None, vec is not None, len(after)

    def body(*refs):
        a_ref, b_ref = refs[0], refs[1]
        pos = 2
        res_ref = aux_ref = vec_ref = None
        if has_res:
            res_ref = refs[pos]
            pos += 1
        if has_aux:
            aux_ref = refs[pos]
            pos += 1
        if has_vec:
            vec_ref = refs[pos]
            pos += 1
        pos += n_after
        outs = refs[pos:pos + n_out]
        part = _dot(a_ref[...].astype(BF), b_ref[...].astype(BF), mode)

        first_row_tile = pl.program_id(0) == 0

        def add_to_sum(row):
            @pl.when(first_row_tile)
            def _():
                outs[1][...] = jnp.zeros_like(outs[1])

            outs[1][...] += row

        def finish(acc):
            if epi == "rms_bwd":
                xv = aux_ref[...]
                r = lax.rsqrt(jnp.mean(xv * xv, axis=-1, keepdims=True) + EPS)
                xhat = xv * r
                dy = acc * vec_ref[...]
                outs[0][...] = res_ref[...] + r * (dy - xhat * jnp.mean(dy * xhat, axis=-1, keepdims=True))
                add_to_sum(jnp.sum(acc * xhat, axis=0, keepdims=True))
                return
            if res_ref is not None:
                acc = res_ref[...] + acc
            if epi == "relu2":
                r = jnp.maximum(acc, 0.0)
                outs[0][...] = (r * r).astype(out_dtype)
            elif epi == "drelu2":
                outs[0][...] = (acc.astype(BF) * (2.0 * jnp.sqrt(aux_ref[...]))).astype(out_dtype)
            elif epi == "rms_next":
                outs[0][...] = acc
                r = lax.rsqrt(jnp.mean(acc * acc, axis=-1, keepdims=True) + EPS)
                outs[1][...] = ((acc * r) * vec_ref[...]).astype(BF)
            elif epi == "loss":
                e = acc - aux_ref[...]
                outs[0][...] = e / float(N)
                add_to_sum(0.5 * jnp.sum(jnp.mean(e * e, axis=-1, keepdims=True)))
            else:
                outs[0][...] = acc.astype(out_dtype)

        if nk == 1:
            finish(part)
        else:
            acc_ref = refs[pos + n_out]
            k = pl.program_id(2)

            @pl.when(k == 0)
            def _():
                acc_ref[...] = part

            @pl.when(k > 0)
            def _():
                acc_ref[...] += part

            @pl.when(k == nk - 1)
            def _():
                finish(acc_ref[...])

    scratch = [pltpu.VMEM((tm, tn), F32)] if nk > 1 else []
    out = _pallas_call(
        body, name=name, grid=grid, in_specs=in_specs, out_specs=out_specs, out_shape=out_shape,
        scratch_shapes=scratch,
        compiler_params=_params(("arbitrary" if reduces else "parallel", "parallel", "arbitrary")),
    )(*args)
    return out if n_out > 1 else out[0]


def _rms_fwd(x, gain, name, after=None):
    T, D = x.shape
    tm = min(512, T)

    def body(x_ref, g_ref, *rest):
        o_ref = rest[-1]
        xv = x_ref[...]
        r = lax.rsqrt(jnp.mean(xv * xv, axis=-1, keepdims=True) + EPS)
        o_ref[...] = ((xv * r) * g_ref[...]).astype(BF)

    extra = [] if after is None else list(after) if isinstance(after, (list, tuple)) else [after]
    return _pallas_call(
        body, name=name, grid=(T // tm,),
        in_specs=[pl.BlockSpec((tm, D), lambda i: (i, 0)), pl.BlockSpec((1, D), lambda i: (0, 0))] + [ANY] * len(extra),
        out_specs=pl.BlockSpec((tm, D), lambda i: (i, 0)), out_shape=jax.ShapeDtypeStruct((T, D), BF),
        compiler_params=_params(("parallel",)),
    )(x, gain, *extra)


def _rms_bwd(dh, x, gain, dres, name):
    T, D = x.shape
    tm = min(512, T)

    def body(dh_ref, x_ref, g_ref, dres_ref, dx_ref, dg_ref):
        xv = x_ref[...]
        r = lax.rsqrt(jnp.mean(xv * xv, axis=-1, keepdims=True) + EPS)
        xhat = xv * r
        dhv = dh_ref[...]
        dy = dhv * g_ref[...]
        dx_ref[...] = dres_ref[...] + r * (dy - xhat * jnp.mean(dy * xhat, axis=-1, keepdims=True))

        @pl.when(pl.program_id(0) == 0)
        def _():
            dg_ref[...] = jnp.zeros_like(dg_ref)

        dg_ref[...] += jnp.sum(dhv * xhat, axis=0, keepdims=True)

    row = pl.BlockSpec((tm, D), lambda i: (i, 0))
    vec = pl.BlockSpec((1, D), lambda i: (0, 0))
    return _pallas_call(
        body, name=name, grid=(T // tm,), in_specs=[row, row, vec, row], out_specs=[row, vec],
        out_shape=[jax.ShapeDtypeStruct((T, D), F32), jax.ShapeDtypeStruct((1, D), F32)],
        compiler_params=_params(("arbitrary",)),
    )(dh, x, gain, dres)


POOL_HALO = 16
CONV_HALO = 8
POOLCONV_ROWS = 512


def _causal_window_sum(v, w):
    s, sh = v, 1
    while sh < w:
        s = s + pltpu.roll(s, sh, 0)
        sh *= 2
    return s


def _anticausal_window_sum(v, w):
    n = v.shape[0]
    s, sh = v, 1
    while sh < w:
        s = s + pltpu.roll(s, n - sh, 0)
        sh *= 2
    return s


def _poolconv_fwd(z, pmix_b, pscale, convw, name):
    T = z.shape[0]
    R = min(POOLCONV_ROWS, T)
    PH, CH = R // POOL_HALO, R // CONV_HALO

    def body(u_ref, uh_ref, b_ref, c_ref, ch_ref, x_ref, xh_ref, mix_ref, sc_ref, cw_ref, yp_ref, yc_ref):
        i = pl.program_id(0)
        keep = (i > 0).astype(F32)
        row = i * R + lax.broadcasted_iota(jnp.int32, (R, 1), 0)
        w_all = jnp.concatenate([uh_ref[...] * keep, u_ref[...]], axis=0)
        for g, w in enumerate(POOL_WINDOWS):
            cols = slice(128 * g, 128 * (g + 1))
            wg = w_all[:, cols]
            s = _causal_window_sum(wg, w)[POOL_HALO:]
            inv_cnt = 1.0 / jnp.minimum(row + 1, w).astype(F32)
            dgrp = s * inv_cnt - wg[POOL_HALO:]
            y = _dot(dgrp.astype(BF), mix_ref[g]) * sc_ref[:, cols]
            yp_ref[:, cols] = y.astype(BF)
        uc = jnp.concatenate([ch_ref[...] * xh_ref[...] * keep, c_ref[...] * x_ref[...]], axis=0)
        yc = cw_ref[2:3, :] * uc + cw_ref[0:1, :] * pltpu.roll(uc, 2, 0) + cw_ref[1:2, :] * pltpu.roll(uc, 1, 0)
        yc_ref[...] = (b_ref[...] * yc[CONV_HALO:]).astype(BF)

    def main(cb):
        return pl.BlockSpec((R, 512), lambda i: (i, cb))

    def prev(cb, halo, per):
        return pl.BlockSpec((halo, 512), lambda i: (jnp.maximum(i * per - 1, 0), cb))

    full = lambda a: pl.BlockSpec(a.shape, lambda i: (0,) * a.ndim)
    return _pallas_call(
        body, name=name, grid=(T // R,),
        in_specs=[main(0), prev(0, POOL_HALO, PH), main(1), main(2), prev(2, CONV_HALO, CH), main(3),
                  prev(3, CONV_HALO, CH), full(pmix_b), full(pscale), full(convw)],
        out_specs=[pl.BlockSpec((R, 512), lambda i: (i, 0))] * 2,
        out_shape=[jax.ShapeDtypeStruct((T, 512), BF)] * 2,
        compiler_params=_params(("parallel",)),
    )(z, z, z, z, z, z, z, pmix_b, pscale, convw)


def _poolconv_bwd(z, dyp, dyc, pmix_b, pscale, convw, dz, name):
    T = z.shape[0]
    R = min(POOLCONV_ROWS, T)
    PH, CH = R // POOL_HALO, R // CONV_HALO
    nsteps = T // R

    def body(u_ref, uh_ref, b_ref, bn_ref, c_ref, ch_ref, x_ref, xh_ref, dyp_ref, dypn_ref, dyc_ref, dycn_ref,
             mix_ref, sc_ref, cw_ref, dz_in_ref, dz_ref, dmix_ref, dsc_ref, dcw_ref):
        i = pl.program_id(0)
        keep_prev = (i > 0).astype(F32)
        keep_next = (i < nsteps - 1).astype(F32)

        @pl.when(i == 0)
        def _():
            dmix_ref[...] = jnp.zeros_like(dmix_ref)
            dsc_ref[...] = jnp.zeros_like(dsc_ref)
            dcw_ref[...] = jnp.zeros_like(dcw_ref)

        row = i * R + lax.broadcasted_iota(jnp.int32, (R, 1), 0)
        row_ext = i * R + lax.broadcasted_iota(jnp.int32, (R + POOL_HALO, 1), 0)
        w_all = jnp.concatenate([uh_ref[...] * keep_prev, u_ref[...]], axis=0)
        dyp_ext = jnp.concatenate([dyp_ref[...], dypn_ref[...] * keep_next], axis=0)
        for g, w in enumerate(POOL_WINDOWS):
            cols = slice(128 * g, 128 * (g + 1))
            wg = w_all[:, cols]
            s = _causal_window_sum(wg, w)[POOL_HALO:]
            inv_cnt = 1.0 / jnp.minimum(row + 1, w).astype(F32)
            dgrp = (s * inv_cnt - wg[POOL_HALO:]).astype(BF)
            y_pre = _dot(dgrp, mix_ref[g])
            dsc_ref[:, cols] += jnp.sum(dyp_ref[:, cols] * y_pre, axis=0, keepdims=True)
            dyb = (dyp_ext[:, cols] * sc_ref[:, cols]).astype(BF)
            dmix_ref[cols, :] += _dot(dgrp, dyb[:R], "tn")
            dd = _dot(dyb, mix_ref[g], "nt")
            inv_cnt_ext = 1.0 / jnp.minimum(row_ext + 1, w).astype(F32)
            e = _anticausal_window_sum(dd * inv_cnt_ext, w)
            dz_ref[:, cols] = (e[:R] - dd[:R]).astype(BF)
        cw0, cw1, cw2 = cw_ref[0:1, :], cw_ref[1:2, :], cw_ref[2:3, :]
        uc = jnp.concatenate([ch_ref[...] * xh_ref[...] * keep_prev, c_ref[...] * x_ref[...]], axis=0)
        uc1 = pltpu.roll(uc, 1, 0)[CONV_HALO:]
        uc2 = pltpu.roll(uc, 2, 0)[CONV_HALO:]
        uc0 = uc[CONV_HALO:]
        yc = cw2 * uc0 + cw0 * uc2 + cw1 * uc1
        dycv = dyc_ref[...]
        dz_ref[:, 512:1024] = (dycv * yc).astype(BF)
        dv_ext = jnp.concatenate([dycv * b_ref[...], dycn_ref[...] * bn_ref[...] * keep_next], axis=0)
        n_ext = R + CONV_HALO
        duc = (cw2 * dv_ext + cw1 * pltpu.roll(dv_ext, n_ext - 1, 0) + cw0 * pltpu.roll(dv_ext, n_ext - 2, 0))[:R]
        dv = dv_ext[:R]
        dcw_ref[0:1, :] += jnp.sum(dv * uc2, axis=0, keepdims=True)
        dcw_ref[1:2, :] += jnp.sum(dv * uc1, axis=0, keepdims=True)
        dcw_ref[2:3, :] += jnp.sum(dv * uc0, axis=0, keepdims=True)
        dz_ref[:, 1024:1536] = (duc * x_ref[...]).astype(BF)
        dz_ref[:, 1536:2048] = (duc * c_ref[...]).astype(BF)

    def main(cb):
        return pl.BlockSpec((R, 512), lambda i: (i, cb))

    def prev(cb, halo, per):
        return pl.BlockSpec((halo, 512), lambda i: (jnp.maximum(i * per - 1, 0), cb))

    def nxt(cb, halo, per):
        return pl.BlockSpec((halo, 512), lambda i: (jnp.minimum((i + 1) * per, T // halo - 1), cb))

    full = lambda a: pl.BlockSpec(a.shape, lambda i: (0,) * a.ndim)
    return _pallas_call(
        body, name=name, grid=(nsteps,),
        in_specs=[main(0), prev(0, POOL_HALO, PH), main(1), nxt(1, CONV_HALO, CH), main(2), prev(2, CONV_HALO, CH),
                  main(3), prev(3, CONV_HALO, CH), main(0), nxt(0, POOL_HALO, PH), main(0), nxt(0, CONV_HALO, CH),
                  full(pmix_b), full(pscale), full(convw), ANY],
        out_specs=[pl.BlockSpec((R, 2048), lambda i: (i, 0)), pl.BlockSpec((512, 128), lambda i: (0, 0)),
                   pl.BlockSpec((1, 512), lambda i: (0, 0)), pl.BlockSpec((8, 512), lambda i: (0, 0))],
        out_shape=[jax.ShapeDtypeStruct(dz.shape, BF), jax.ShapeDtypeStruct((512, 128), F32),
                   jax.ShapeDtypeStruct((1, 512), F32), jax.ShapeDtypeStruct((8, 512), F32)],
        input_output_aliases={15: 0}, compiler_params=_params(("arbitrary",)),
    )(z, z, z, z, z, z, z, z, dyp, dyp, dyc, dyc, pmix_b, pscale, convw, dz)


def _head_sums(v):
    row = lax.broadcasted_iota(jnp.int32, (LANES, LANES), 0) < HEAD_DIM
    col = lax.broadcasted_iota(jnp.int32, (LANES, LANES), 1) < HEAD_DIM
    same_head = jnp.where(jnp.logical_xor(row, col), 0.0, 1.0).astype(BF)
    hi = v.astype(BF)
    lo = (v - hi.astype(F32)).astype(BF)
    return _dot(hi, same_head) + _dot(lo, same_head)


def _head_norm(x, g2, ma):
    r = lax.rsqrt(_head_sums(x * x) / HEAD_DIM + EPS)
    return x * r, r


def _head_norm_bwd(dy, xhat, r, g2, ma):
    dxh = dy * g2
    return r * (dxh - xhat * (_head_sums(dxh * xhat) / HEAD_DIM))


def _attn_masks(other_block_exists):
    lane = lax.broadcasted_iota(jnp.int32, (2 * ATTN_BLOCK, ATTN_BLOCK), 1)
    qi = lax.broadcasted_iota(jnp.int32, (2 * ATTN_BLOCK, ATTN_BLOCK), 0) & (ATTN_BLOCK - 1)
    never = (1 - other_block_exists.astype(jnp.int32)) * (2 * ATTN_BLOCK)
    return lane[:ATTN_BLOCK] < HEAD_DIM, lane <= qi, lane >= qi + never


def _stack_heads(x, ma):
    return jnp.concatenate([jnp.where(ma, x, 0.0), jnp.where(ma, 0.0, x)], axis=0)


def _unstack_heads(y, ma):
    return jnp.where(ma, y[:ATTN_BLOCK], y[ATTN_BLOCK:])


def _stack_cols(tile, ma):
    return jnp.concatenate([tile[:, 0:1], tile[:, HEAD_DIM:HEAD_DIM + 1]], axis=0)


QKV_TILES = (OFF_GATE - OFF_Q) // LANES
KIND_TILES = QKV_TILES // 3


def _qk_norm(z, gains, name):
    T = z.shape[0]
    tm = min(512, T)

    def body(x_ref, g_ref, o_ref):
        ma = lax.broadcasted_iota(jnp.int32, (tm, LANES), 1) < HEAD_DIM
        for tile in range(QKV_TILES):
            v = x_ref[:, LANES * tile:LANES * (tile + 1)]
            if tile < 2 * KIND_TILES:
                g = g_ref[0:1, :] if tile < KIND_TILES else g_ref[1:2, :]
                v = _head_norm(v, g, ma)[0] * g
            o_ref[tile] = v

    return _pallas_call(
        body, name=name, grid=(T // tm,),
        in_specs=[pl.BlockSpec((pl.Element(tm), pl.Element(OFF_GATE - OFF_Q)), lambda i: (i * tm, OFF_Q)),
                  pl.BlockSpec((8, LANES), lambda i: (0, 0))],
        out_specs=pl.BlockSpec((QKV_TILES, tm, LANES), lambda i: (0, i, 0)),
        out_shape=jax.ShapeDtypeStruct((QKV_TILES, T, LANES), F32), compiler_params=_params(("parallel",)),
    )(z, gains)


ATTN_STEP_ROWS = 1024
ATTN_UNROLL = 4


def _attn_geometry(T, d):
    sub = ATTN_BLOCK * d
    nb = T // sub
    m = max(1, min(nb, ATTN_STEP_ROWS // sub))
    assert T % sub == 0 and nb % m == 0
    return sub, nb, m


def _attn_rows(jj, r, sub, d):
    start = jj * sub + r
    if d == 1:
        return pl.ds(pl.multiple_of(start, ATTN_BLOCK), ATTN_BLOCK)
    return pl.ds(start, ATTN_BLOCK, stride=d)


def _pick(flag, a, b):
    return jnp.where(jnp.full(a.shape, flag.astype(jnp.int32)) > 0, a, b)


def _attn_fwd(qkv, g, d, name):
    T = qkv.shape[1]
    sub, nb, m = _attn_geometry(T, d)
    scale = HEAD_DIM ** -0.5

    def body(q_ref, kc_ref, kp_ref, vc_ref, vp_ref, o_ref, lse_ref):
        jb = pl.program_id(0)

        def step(s, carry):
            jj, r = s // d, s % d
            here, before = _attn_rows(jj, r, sub, d), _attn_rows(jnp.maximum(jj - 1, 0), r, sub, d)
            edge = _attn_rows(0, r, sub, d)
            first = jj == 0
            ma, mask_c, mask_p = _attn_masks(jb * m + jj > 0)
            qs = _stack_heads(q_ref[here, :], ma).astype(BF)
            kcb = kc_ref[here, :].astype(BF)
            kpb = _pick(first, kp_ref[edge, :], kc_ref[before, :]).astype(BF)
            vcb = vc_ref[here, :].astype(BF)
            vpb = _pick(first, vp_ref[edge, :], vc_ref[before, :]).astype(BF)
            s_c = jnp.where(mask_c, _dot(qs, kcb, "nt") * scale, MASK_VALUE)
            s_p = jnp.where(mask_p, _dot(qs, kpb, "nt") * scale, MASK_VALUE)
            mx = jnp.maximum(jnp.max(s_c, axis=-1, keepdims=True), jnp.max(s_p, axis=-1, keepdims=True))
            p_c = jnp.exp(s_c - mx)
            p_p = jnp.exp(s_p - mx)
            den = jnp.sum(p_c, axis=-1, keepdims=True) + jnp.sum(p_p, axis=-1, keepdims=True)
            o = (_dot(p_c.astype(BF), vcb) + _dot(p_p.astype(BF), vpb)) / den
            o_ref[here, :] = _unstack_heads(o, ma)
            lse_ref[here, :] = _unstack_heads(jnp.broadcast_to(mx + jnp.log(den), o.shape), ma)
            return carry

        lax.fori_loop(0, m * d, step, 0, unroll=ATTN_UNROLL)

    def cur(kind):
        return pl.BlockSpec((None, m * sub, LANES), lambda j, t: (KIND_TILES * kind + 2 * g + t, j, 0))

    def prv(kind):
        return pl.BlockSpec((None, sub, LANES), lambda j, t: (KIND_TILES * kind + 2 * g + t, jnp.maximum(j * m - 1, 0), 0))

    out = pl.BlockSpec((m * sub, LANES), lambda j, t: (j, t))
    return _pallas_call(
        body, name=name, grid=(nb // m, 2), in_specs=[cur(0), cur(1), prv(1), cur(2), prv(2)],
        out_specs=[out, out], out_shape=[jax.ShapeDtypeStruct((T, 256), F32)] * 2,
        compiler_params=_params(("parallel", "parallel")),
    )(qkv, qkv, qkv, qkv, qkv)


def _attn_bwd(z, qkv, do, c, lse, gains, g, d, name, after=None):
    T = z.shape[0]
    sub, nb, m = _attn_geometry(T, d)
    scale = HEAD_DIM ** -0.5
    extra = [] if after is None else [after]

    def body(qr_ref, kr_ref, vc_ref, vp_ref, qn_ref, qnn_ref, kn_ref, knp_ref, do_ref, don_ref, c_ref, cn_ref,
             lse_ref, lsen_ref, g_ref, *rest):
        dq_ref, dk_ref, dv_ref, dgq_ref, dgk_ref, sq_ref, sk_ref, sv_ref = rest[len(extra):]
        jb = pl.program_id(0)

        @pl.when((jb == 0) & (pl.program_id(1) == 0))
        def _():
            dgq_ref[...] = jnp.zeros_like(dgq_ref)
            dgk_ref[...] = jnp.zeros_like(dgk_ref)

        gq, gk = g_ref[0:1, :], g_ref[1:2, :]

        def step(s, carry):
            jj, r = s // d, s % d
            here, edge = _attn_rows(jj, r, sub, d), _attn_rows(0, r, sub, d)
            before = _attn_rows(jnp.maximum(jj - 1, 0), r, sub, d)
            behind = _attn_rows(jnp.minimum(jj + 1, m - 1), r, sub, d)
            first, last = jj == 0, jj == m - 1
            block = jb * m + jj
            ma, mask_c, mask_p = _attn_masks(block > 0)
            mask_n = _attn_masks(block < nb - 1)[2]
            qhat, rq = _head_norm(qr_ref[here, :], gq, ma)
            qn = qn_ref[here, :]
            qn_next = _pick(last, qnn_ref[edge, :], qn_ref[behind, :])
            khat, rk = _head_norm(kr_ref[here, :], gk, ma)
            kcb = kn_ref[here, :].astype(BF)
            kpb = _pick(first, knp_ref[edge, :], kn_ref[before, :]).astype(BF)
            vcb = vc_ref[here, :].astype(BF)
            vpb = _pick(first, vp_ref[edge, :], vc_ref[before, :]).astype(BF)
            do_t, don_t = do_ref[here, :], _pick(last, don_ref[edge, :], do_ref[behind, :])
            c_t, cn_t = c_ref[here, :], _pick(last, cn_ref[edge, :], c_ref[behind, :])
            lse_t, lsen_t = lse_ref[here, :], _pick(last, lsen_ref[edge, :], lse_ref[behind, :])
            qs, dos = _stack_heads(qn, ma).astype(BF), _stack_heads(do_t, ma).astype(BF)
            lse_s, c_s = _stack_cols(lse_t, ma), _stack_cols(c_t, ma)
            s_c = jnp.where(mask_c, _dot(qs, kcb, "nt") * scale, MASK_VALUE)
            s_p = jnp.where(mask_p, _dot(qs, kpb, "nt") * scale, MASK_VALUE)
            p_c = jnp.exp(s_c - lse_s)
            p_p = jnp.exp(s_p - lse_s)
            ds_c = ((p_c * (_dot(dos, vcb, "nt") + c_s)) * scale).astype(BF)
            ds_p = ((p_p * (_dot(dos, vpb, "nt") + c_s)) * scale).astype(BF)
            dq_t = _unstack_heads(_dot(ds_c, kcb) + _dot(ds_p, kpb), ma)
            qs_n, dos_n = _stack_heads(qn_next, ma).astype(BF), _stack_heads(don_t, ma).astype(BF)
            s_n = jnp.where(mask_n, _dot(qs_n, kcb, "nt") * scale, MASK_VALUE)
            p_n = jnp.exp(s_n - _stack_cols(lsen_t, ma))
            ds_n = ((p_n * (_dot(dos_n, vcb, "nt") + _stack_cols(cn_t, ma))) * scale).astype(BF)
            dv_t = _dot(p_c.astype(BF), dos, "tn") + _dot(p_n.astype(BF), dos_n, "tn")
            dk_t = _dot(ds_c, qs, "tn") + _dot(ds_n, qs_n, "tn")
            sq_ref[here, :] = _head_norm_bwd(dq_t, qhat, rq, gq, ma)
            sk_ref[here, :] = _head_norm_bwd(dk_t, khat, rk, gk, ma)
            sv_ref[here, :] = dv_t
            dgq_ref[...] += jnp.sum(dq_t * qhat, axis=0, keepdims=True)
            dgk_ref[...] += jnp.sum(dk_t * khat, axis=0, keepdims=True)
            return carry

        lax.fori_loop(0, m * d, step, 0, unroll=ATTN_UNROLL)
        dq_ref[...] = sq_ref[...].astype(BF)
        dk_ref[...] = sk_ref[...].astype(BF)
        dv_ref[...] = sv_ref[...].astype(BF)

    def raw(col0):
        return pl.BlockSpec((m * sub, LANES), lambda j, t: (j, col0 + 2 * g + t))

    def cur(kind):
        return pl.BlockSpec((None, m * sub, LANES), lambda j, t: (KIND_TILES * kind + 2 * g + t, j, 0))

    def prv(kind):
        return pl.BlockSpec((None, sub, LANES), lambda j, t: (KIND_TILES * kind + 2 * g + t, jnp.maximum(j * m - 1, 0), 0))

    def nxt(kind):
        return pl.BlockSpec((None, sub, LANES),
                            lambda j, t: (KIND_TILES * kind + 2 * g + t, jnp.minimum((j + 1) * m, nb - 1), 0))

    own = pl.BlockSpec((m * sub, LANES), lambda j, t: (j, t))
    own_next = pl.BlockSpec((sub, LANES), lambda j, t: (jnp.minimum((j + 1) * m, nb - 1), t))
    vec = pl.BlockSpec((1, LANES), lambda j, t: (0, 0))
    return _pallas_call(
        body, name=name, grid=(nb // m, 2),
        in_specs=[raw(OFF_Q // LANES), raw(OFF_K // LANES), cur(2), prv(2), cur(0), nxt(0), cur(1), prv(1), own, own_next,
                  own, own_next,
                  own, own_next, pl.BlockSpec((8, LANES), lambda j, t: (0, 0))] + [ANY] * len(extra),
        out_specs=[own, own, own, vec, vec],
        out_shape=[jax.ShapeDtypeStruct((T, 256), BF)] * 3 + [jax.ShapeDtypeStruct((1, LANES), F32)] * 2,
        scratch_shapes=[pltpu.VMEM((m * sub, LANES), F32)] * 3,
        compiler_params=_params(("arbitrary", "arbitrary")),
    )(z, z, qkv, qkv, qkv, qkv, qkv, qkv, do, do, c, c, lse, lse, gains, *extra)


MERGE_ROWS = 256
GATE_TILE = 256


def _group_mix(o_refs, lse_refs):
    lses = [r[...] for r in lse_refs]
    m = jnp.maximum(jnp.maximum(lses[0], lses[1]), lses[2])
    es = [jnp.exp(l - m) for l in lses]
    den = es[0] + es[1] + es[2]
    ws = [e / den for e in es]
    y = ws[0] * o_refs[0][...] + ws[1] * o_refs[1][...] + ws[2] * o_refs[2][...]
    return ws, y


def _sigmoid(v):
    return 1.0 / (1.0 + jnp.exp(-v))


def _merge_specs(T, z, bgate, gpu, gco, gau):
    tm = min(MERGE_ROWS, T)
    row = lambda w: pl.BlockSpec((tm, w), lambda i: (i, 0))
    gate0 = OFF_GATE // GATE_TILE
    gates = [pl.BlockSpec((tm, GATE_TILE), functools.partial(lambda i, cb: (i, cb), cb=gate0 + n))
             for n in range(3 * N_CHIPS)]
    full = lambda a: pl.BlockSpec(a.shape, lambda i: (0,) * a.ndim)
    specs = [row(512), row(512)] + [row(256)] * 6 + gates + [full(bgate), full(gpu), full(gco), full(gau)]
    return tm, row, specs


def _merge_fwd(yp, yc, o3, lse3, z, bgate, gpu, gco, gau, name):
    T = yp.shape[0]
    tm, row, specs = _merge_specs(T, z, bgate, gpu, gco, gau)

    def body(*refs):
        yp_ref, yc_ref = refs[0], refs[1]
        o_refs, lse_refs = refs[2:5], refs[5:8]
        zg = refs[8:20]
        b_ref, gpu_ref, gco_ref, gau_ref, out_ref = refs[20:25]
        yab = _group_mix(o_refs, lse_refs)[1].astype(BF)
        ys = (yp_ref[...], yc_ref[...], yab)
        ups = (gpu_ref, gco_ref, gau_ref)
        for n in range(N_CHIPS):
            acc = None
            for b in range(3):
                gcol = slice(1024 * b + GATE_TILE * n, 1024 * b + GATE_TILE * (n + 1))
                gate = _sigmoid(zg[N_CHIPS * b + n][...] + b_ref[:, gcol])
                term = gate * _dot(ys[b], ups[b][n])
                acc = term if acc is None else acc + term
            out_ref[:, GATE_TILE * n:GATE_TILE * (n + 1)] = acc.astype(BF)

    return _pallas_call(
        body, name=name, grid=(T // tm,), in_specs=specs, out_specs=row(1024),
        out_shape=jax.ShapeDtypeStruct((T, 1024), BF), compiler_params=_params(("parallel",)),
    )(yp, yc, *o3, *lse3, *([z] * 12), bgate, gpu, gco, gau)


def _merge_bwd(dm, yp, yc, o3, lse3, z, bgate, gpu, gco, gau, name):
    T = yp.shape[0]
    tm, row, specs = _merge_specs(T, z, bgate, gpu, gco, gau)
    nsteps = T // tm

    def body(*refs):
        dm_ref, yp_ref, yc_ref = refs[0:3]
        o_refs, lse_refs = refs[3:6], refs[6:9]
        zg = refs[9:21]
        b_ref, gpu_ref, gco_ref, gau_ref = refs[21:25]
        dzg_ref, dyp_ref, dyc_ref = refs[25:28]
        do_refs, c_refs = refs[28:31], refs[31:34]
        dgpu_ref, dgco_ref, dgau_ref, dbg_ref = refs[34:38]
        accs = refs[38:41]
        i = pl.program_id(0)

        @pl.when(i == 0)
        def _():
            for a in accs:
                a[...] = jnp.zeros_like(a)
            dbg_ref[...] = jnp.zeros_like(dbg_ref)

        ws, y = _group_mix(o_refs, lse_refs)
        ys = (yp_ref[...], yc_ref[...], y.astype(BF))
        ups = (gpu_ref, gco_ref, gau_ref)
        dys = [None, None, None]
        for n in range(N_CHIPS):
            dmn = dm_ref[:, GATE_TILE * n:GATE_TILE * (n + 1)]
            for b in range(3):
                gcol = slice(1024 * b + GATE_TILE * n, 1024 * b + GATE_TILE * (n + 1))
                gate = _sigmoid(zg[N_CHIPS * b + n][...] + b_ref[:, gcol])
                up = _dot(ys[b], ups[b][n])
                dzg = (dmn * up) * (gate * (1.0 - gate))
                dzg_ref[:, gcol] = dzg.astype(BF)
                dbg_ref[:, gcol] += jnp.sum(dzg, axis=0, keepdims=True)
                dup = (dmn * gate).astype(BF)
                accs[b][n] += _dot(ys[b], dup, "tn")
                dyb = _dot(dup, ups[b][n], "nt")
                dys[b] = dyb if dys[b] is None else dys[b] + dyb
        dyp_ref[...] = dys[0]
        dyc_ref[...] = dys[1]
        dya = dys[2]
        lane = lax.broadcasted_iota(jnp.int32, dya.shape, 1) // HEAD_DIM
        pr = dya * y
        rho = jnp.zeros_like(pr)
        for h in range(256 // HEAD_DIM):
            hm = lane == h
            rho = jnp.where(hm, jnp.sum(jnp.where(hm, pr, 0.0), axis=-1, keepdims=True), rho)
        for g in range(3):
            do_refs[g][...] = ws[g] * dya
            c_refs[g][...] = -(ws[g] * rho)

        @pl.when(i == nsteps - 1)
        def _():
            dgpu_ref[...] = accs[0][...].astype(BF)
            dgco_ref[...] = accs[1][...].astype(BF)
            dgau_ref[...] = accs[2][...].astype(BF)

    full = lambda a: pl.BlockSpec(a.shape, lambda i: (0,) * a.ndim)
    dz_gate = pl.BlockSpec((pl.Element(tm), pl.Element(3072)), lambda i: (i * tm, OFF_GATE))
    out_specs = ([dz_gate, row(512), row(512)] + [row(256)] * 6 + [full(gpu), full(gco), full(gau)]
                 + [pl.BlockSpec((1, 3072), lambda i: (0, 0))])
    out_shape = ([jax.ShapeDtypeStruct(z.shape, BF)] + [jax.ShapeDtypeStruct((T, 512), F32)] * 2
                 + [jax.ShapeDtypeStruct((T, 256), F32)] * 6
                 + [jax.ShapeDtypeStruct(g.shape, BF) for g in (gpu, gco, gau)]
                 + [jax.ShapeDtypeStruct((1, 3072), F32)])
    return _pallas_call(
        body, name=name, grid=(nsteps,), in_specs=[row(1024)] + specs, out_specs=out_specs, out_shape=out_shape,
        scratch_shapes=[pltpu.VMEM(g.shape, F32) for g in (gpu, gco, gau)],
        compiler_params=_params(("arbitrary",)),
    )(dm, yp, yc, *o3, *lse3, *([z] * 12), bgate, gpu, gco, gau)


def _layer_fwd(x, w, tag, after=None, soon=None, late=None, target=None, hb=None, next_gain=None):
    if hb is None:
        hb = _rms_fwd(x, w["norm_mix"], f"rms_mix_{tag}", after=after)
    if soon is not None:
        w = dict(w, **soon(hb))
    z = _mm(hb, w["w_in"], "nt", f"in_proj_{tag}", tm=512, tn=3712, tk=1024, n_outer=True)
    yp, yc = _poolconv_fwd(z, w["pool_mix"], w["pool_scale"], w["conv_w"], f"poolconv_{tag}")
    qkv = _qk_norm(z, w["qk_gain"], f"qk_norm_{tag}")
    o3, lse3 = [], []
    for g, d in enumerate(ATTN_DILATIONS):
        o, lse = _attn_fwd(qkv, g, d, f"attn{g}_{tag}")
        o3.append(o)
        lse3.append(lse)
    if late is not None:
        w = dict(w, **late(lse3[-1]))
    merged = _merge_fwd(yp, yc, o3, lse3, z, w["b_gate"], w["w_pool_up"], w["w_conv_out"], w["w_attn_up"],
                        f"merge_{tag}")
    x1, h2b = _mm(merged, w["w_o"], "nn", f"out_proj_{tag}", tm=1024, tn=1024, tk=1024, res=x, vec=w["norm_mlp"],
                  epi="rms_next")
    rb = _mm(h2b, w["w_ff1"], "nn", f"ff1_{tag}", tm=1024, tn=1024, tk=1024, out_dtype=BF, epi="relu2", n_outer=True,
             b_shards=True)
    if target is not None:
        x2 = _mm(rb, w["w_ff2"], "nn", f"ff2_{tag}", tm=512, tn=1024, tk=4096, res=x1, aux=target, epi="loss")
    elif next_gain is not None:
        x2 = _mm(rb, w["w_ff2"], "nn", f"ff2_{tag}", tm=512, tn=1024, tk=4096, res=x1, vec=next_gain, epi="rms_next")
    else:
        x2 = _mm(rb, w["w_ff2"], "nn", f"ff2_{tag}", tm=512, tn=1024, tk=4096, res=x1)
    saved = dict(x=x, hb=hb, z=z, yp=yp, yc=yc, qkv=qkv, o3=o3, lse3=lse3, merged=merged, x1=x1, h2b=h2b, rb=rb)
    return x2, saved, w


def _layer_bwd(dx2, w, s, tag, after=None, mid=None, tail=None):
    g = {}
    dab = _mm(dx2, w["w_ff2"], "nt", f"d_ff2_act_{tag}", tm=1024, tn=1024, tk=1024, out_dtype=BF, aux=s["rb"],
              epi="drelu2", after=after)
    g["w_ff2"] = _mm(s["rb"], dx2, "tn", f"d_ff2_w_{tag}", tm=1024, tn=1024, tk=2048, out_dtype=BF)
    g["w_ff1"] = _mm(s["h2b"], dab, "tn", f"d_ff1_w_{tag}", tm=1024, tn=1024, tk=2048, out_dtype=BF, out_shards=True)
    dx1, g["norm_mlp"] = _mm(dab, w["w_ff1"], "nt", f"d_ff1_act_{tag}", tm=1024, tn=1024, tk=1024, b_shards=True,
                             res=dx2, aux=s["x1"], vec=w["norm_mlp"], epi="rms_bwd")
    dm = _mm(dx1, w["w_o"], "nt", f"d_out_act_{tag}", tm=1024, tn=1024, tk=1024)
    g["w_o"] = _mm(s["merged"], dx1, "tn", f"d_out_w_{tag}", tm=1024, tn=1024, tk=1024, out_dtype=BF)
    (dz, dyp, dyc, do0, do1, do2, c0, c1, c2, g["w_pool_up"], g["w_conv_out"], g["w_attn_up"],
     g["b_gate"]) = _merge_bwd(dm, s["yp"], s["yc"], s["o3"], s["lse3"], s["z"], w["b_gate"], w["w_pool_up"],
                               w["w_conv_out"], w["w_attn_up"], f"d_merge_{tag}")
    behind = mid(g) if mid is not None else None
    dq, dk, dv = [], [], []
    dgq = dgk = None
    for gi, d in enumerate(ATTN_DILATIONS):
        dzq, dzk, dzv, pq, pk = _attn_bwd(s["z"], s["qkv"], (do0, do1, do2)[gi], (c0, c1, c2)[gi], s["lse3"][gi],
                                          w["qk_gain"], gi, d, f"d_attn{gi}_{tag}", after=behind)
        dq.append(dzq)
        dk.append(dzk)
        dv.append(dzv)
        dgq = pq if dgq is None else dgq + pq
        dgk = pk if dgk is None else dgk + pk
    g["q_gain"] = dgq[:, :HEAD_DIM] + dgq[:, HEAD_DIM:]
    g["k_gain"] = dgk[:, :HEAD_DIM] + dgk[:, HEAD_DIM:]
    for off, pieces in ((OFF_Q, dq), (OFF_K, dk), (OFF_V, dv)):
        for gi, piece in enumerate(pieces):
            dz = lax.dynamic_update_slice(dz, piece, (0, off + 256 * gi))
    dz, g["pool_mix"], g["pool_scale"], g["conv_w"] = _poolconv_bwd(
        s["z"], dyp, dyc, w["pool_mix"], w["pool_scale"], w["conv_w"], dz, f"d_poolconv_{tag}")
    g["w_in"] = _mm(s["hb"], dz, "tn", f"d_in_w_{tag}", tm=512, tn=3712, tk=1024, out_dtype=BF)
    dh = _mm(dz, w["w_in"], "nn", f"d_in_act_{tag}", tm=1024, tn=1024, tk=3712,
             after=tail(g) if tail is not None else None)
    dx, g["norm_mix"] = _rms_bwd(dh, s["x"], w["norm_mix"], dx1, f"d_rms_mix_{tag}")
    return dx, g


def _position():
    x, y, c = lax.axis_index("x"), lax.axis_index("y"), lax.axis_index("c")
    chips = [(1 - x, y), (x, 1 - y), (1 - x, 1 - y)]
    return x, y, c, 2 * x + y, chips, [2 * cx + cy for cx, cy in chips]


def _remote(src, dst, ssem, rsem, dev):
    return pltpu.make_async_remote_copy(src_ref=src, dst_ref=dst, send_sem=ssem, recv_sem=rsem, device_id=dev,
                                        device_id_type=MESH_ID)


def _halves(a):
    return a.reshape(a.shape[0], 2, a.shape[1] // 2, a.shape[2])


SEM = pl.BlockSpec(memory_space=pltpu.SEMAPHORE)
TOKEN = jax.ShapeDtypeStruct((8, LANES), F32)
TOKEN_SPEC = pl.BlockSpec(memory_space=pltpu.VMEM)


def _split_params():
    return pltpu.CompilerParams(has_side_effects=pltpu.SideEffectType.DATAFLOW_SIDE_EFFECTING)


def _gather_start(bufs, name, after):
    n = len(bufs)
    views = [_halves(b) for b in bufs]

    def body(*refs):
        first_sem = n + 1
        ssem, rsem = refs[first_sem:first_sem + ns], refs[first_sem + ns:first_sem + 2 * ns]
        outs, token = refs[first_sem + 2 * ns:first_sem + 2 * ns + n], refs[first_sem + 2 * ns + n]
        x, y, c, q, chips, qs = _position()
        for k in range(n):
            mine = outs[k].at[q, c]
            for j, chip in enumerate(chips):
                _remote(mine, mine, ssem[3 * k + j], rsem[3 * k + j], (chip[0], chip[1], c)).start()
        token[...] = jnp.zeros_like(token)

    ns = 3 * n
    outs = _pallas_call(
        body, name=name, in_specs=[ANY] * (n + 1), out_specs=[SEM] * (2 * ns) + [ANY] * n + [TOKEN_SPEC],
        out_shape=[pltpu.SemaphoreType.DMA(())] * (2 * ns) + [jax.ShapeDtypeStruct(v.shape, v.dtype) for v in views]
        + [TOKEN],
        input_output_aliases={k: k + 2 * ns for k in range(n)}, compiler_params=_split_params(),
    )(*views, after)
    return list(outs[:ns]), list(outs[ns:2 * ns]), list(outs[2 * ns:2 * ns + n]), outs[2 * ns + n]


def _gather_finish(ssem, rsem, views, after, name_wait, name_forward, shapes):
    n = len(views)
    ns = len(ssem)

    def wait_body(*refs):
        ssem_ref, rsem_ref = refs[n:n + ns], refs[n + ns:n + 2 * ns]
        outs = refs[n + 2 * ns + 1:]
        x, y, c, q, chips, qs = _position()
        for k in range(n):
            for j, chip in enumerate(chips):
                cp = _remote(outs[k].at[q, c], outs[k].at[qs[j], c], ssem_ref[3 * k + j], rsem_ref[3 * k + j],
                             (chip[0], chip[1], c))
                cp.wait_send()
                cp.wait_recv()

    landed = _pallas_call(
        wait_body, name=name_wait, in_specs=[ANY] * n + [SEM] * (2 * ns) + [ANY], out_specs=[ANY] * n,
        out_shape=[jax.ShapeDtypeStruct(v.shape, v.dtype) for v in views],
        input_output_aliases={k: k for k in range(n)}, compiler_params=_split_params(),
    )(*views, *ssem, *rsem, after)

    def forward_body(*refs):
        outs = refs[n:2 * n]
        fssem, frsem = refs[2 * n:]
        x, y, c, q, chips, qs = _position()
        sib = (x, y, 1 - c)
        sent = []
        for k in range(n):
            for j in range(3):
                slot = outs[k].at[qs[j], c]
                cp = _remote(slot, slot, fssem.at[k, j], frsem.at[k, j], sib)
                cp.start()
                sent.append(cp)
        for k in range(n):
            for j in range(3):
                slot = outs[k].at[qs[j], 1 - c]
                _remote(slot, slot, fssem.at[k, j], frsem.at[k, j], sib).wait_recv()
        for cp in sent:
            cp.wait_send()

    outs = _pallas_call(
        forward_body, name=name_forward, in_specs=[ANY] * n, out_specs=[ANY] * n,
        out_shape=[jax.ShapeDtypeStruct(v.shape, v.dtype) for v in views],
        input_output_aliases={k: k for k in range(n)}, scratch_shapes=[pltpu.SemaphoreType.DMA((n, 3))] * 2,
    )(*landed)
    return [o.reshape(s) for o, s in zip(outs, shapes)]


def _chip_exchange_start(parts, name):
    n = len(parts)

    def body(*refs):
        ssem, rsem = refs[n:n + ns], refs[n + ns:n + 2 * ns]
        base = n + 2 * ns
        srcs, outs, token = refs[base:base + n], refs[base + n:base + 2 * n], refs[base + 2 * n]
        x, y, c, q, chips, qs = _position()
        for k in range(n):
            for j, chip in enumerate(chips):
                _remote(srcs[k].at[qs[j]], outs[k].at[j], ssem[3 * k + j], rsem[3 * k + j],
                        (chip[0], chip[1], c)).start()
        token[...] = jnp.zeros_like(token)

    ns = 3 * n
    outs = _pallas_call(
        body, name=name, in_specs=[ANY] * n, out_specs=[SEM] * (2 * ns) + [ANY] * (2 * n) + [TOKEN_SPEC],
        out_shape=[pltpu.SemaphoreType.DMA(())] * (2 * ns) + [jax.ShapeDtypeStruct(a.shape, a.dtype) for a in parts]
        + [jax.ShapeDtypeStruct((3,) + a.shape[1:], a.dtype) for a in parts] + [TOKEN],
        input_output_aliases={k: k + 2 * ns for k in range(n)}, compiler_params=_split_params(),
    )(*parts)
    b = 2 * ns
    return list(outs[:ns]), list(outs[ns:b]), list(outs[b:b + n]), list(outs[b + n:b + 2 * n]), outs[b + 2 * n]


def _chip_exchange_wait(ssem, rsem, parts, landing, after, name):
    n = len(parts)
    ns = len(ssem)

    def body(*refs):
        ssem_ref, rsem_ref = refs[2 * n:2 * n + ns], refs[2 * n + ns:2 * n + 2 * ns]
        base = 2 * n + 2 * ns + 1
        srcs, outs = refs[base:base + n], refs[base + n:]
        x, y, c, q, chips, qs = _position()
        for k in range(n):
            for j, chip in enumerate(chips):
                cp = _remote(srcs[k].at[qs[j]], outs[k].at[j], ssem_ref[3 * k + j], rsem_ref[3 * k + j],
                             (chip[0], chip[1], c))
                cp.wait_send()
                cp.wait_recv()

    outs = _pallas_call(
        body, name=name, in_specs=[ANY] * (2 * n) + [SEM] * (2 * ns) + [ANY], out_specs=[ANY] * (2 * n),
        out_shape=[jax.ShapeDtypeStruct(a.shape, a.dtype) for a in list(parts) + list(landing)],
        input_output_aliases={k: k for k in range(2 * n)}, compiler_params=_split_params(),
    )(*parts, *landing, *ssem, *rsem, after)
    return list(outs[:n]), list(outs[n:])


def _pair_swap(views, name):
    n = len(views)

    def body(*refs):
        ins, outs = refs[:n], refs[n:2 * n]
        ssem, rsem = refs[2 * n:]
        x, y, c, _, _, _ = _position()
        cps = [_remote(ins[k].at[pl.ds(0, N_CHIPS), 1 - c], outs[k], ssem.at[k], rsem.at[k], (x, y, 1 - c))
               for k in range(n)]
        for cp in cps:
            cp.start()
        for cp in cps:
            cp.wait()

    return _pallas_call(
        body, name=name, in_specs=[ANY] * n, out_specs=[ANY] * n,
        out_shape=[jax.ShapeDtypeStruct((v.shape[0],) + v.shape[2:], v.dtype) for v in views],
        scratch_shapes=[pltpu.SemaphoreType.DMA((n,))] * 2,
    )(*views)


def _pair_send(arrays, name):
    n = len(arrays)

    def body(*refs):
        ins, outs = refs[:n], refs[n:2 * n]
        ssem, rsem = refs[2 * n:]
        x, y, c, _, _, _ = _position()
        cps = [_remote(ins[k], outs[k], ssem.at[k], rsem.at[k], (x, y, 1 - c)) for k in range(n)]
        for cp in cps:
            cp.start()
        for cp in cps:
            cp.wait()

    return _pallas_call(
        body, name=name, in_specs=[ANY] * n, out_specs=[ANY] * n,
        out_shape=[jax.ShapeDtypeStruct(a.shape, a.dtype) for a in arrays],
        scratch_shapes=[pltpu.SemaphoreType.DMA((n,))] * 2,
    )(*arrays)


def _all_to_all_small(part):
    P = part.shape[0]

    def body(in_ref, out_ref, lsem, ssem, rsem):
        x, y, c = lax.axis_index("x"), lax.axis_index("y"), lax.axis_index("c")
        me = 4 * x + 2 * y + c
        flips = [(fx, fy, fc) for fx in (0, 1) for fy in (0, 1) for fc in (0, 1)][1:]
        peers = [((x + fx) % 2, (y + fy) % 2, (c + fc) % 2) for fx, fy, fc in flips]
        loc = pltpu.make_async_copy(in_ref, out_ref.at[me], lsem)
        loc.start()
        cps = [_remote(in_ref, out_ref.at[me], ssem.at[j], rsem.at[j], peer) for j, peer in enumerate(peers)]
        for cp in cps:
            cp.start()
        for j, (px, py, pc) in enumerate(peers):
            _remote(in_ref, out_ref.at[4 * px + 2 * py + pc], ssem.at[j], rsem.at[j], peers[j]).wait_recv()
        for cp in cps:
            cp.wait_send()
        loc.wait()

    return _pallas_call(
        body, name="small_exchange", in_specs=[ANY], out_specs=ANY,
        out_shape=jax.ShapeDtypeStruct((8, P, LANES), F32),
        scratch_shapes=[pltpu.SemaphoreType.DMA(())] + [pltpu.SemaphoreType.DMA((7,))] * 2,
    )(part)


def _small_peers():
    x, y, c = lax.axis_index("x"), lax.axis_index("y"), lax.axis_index("c")
    flips = [(fx, fy, fc) for fx in (0, 1) for fy in (0, 1) for fc in (0, 1)][1:]
    peers = [((x + fx) % 2, (y + fy) % 2, (c + fc) % 2) for fx, fy, fc in flips]
    return 4 * x + 2 * y + c, peers


def _all_to_all_small_start(part, name):
    P = part.shape[0]
    me = 4 * lax.axis_index("x") + 2 * lax.axis_index("y") + lax.axis_index("c")
    landing = lax.dynamic_update_slice(jnp.zeros((8, P, LANES), F32), part[None], (me, 0, 0))

    def body(*refs):
        sems, src, land, token = refs[2:16], refs[16], refs[17], refs[18]
        me_, peers = _small_peers()
        for j, peer in enumerate(peers):
            _remote(src, land.at[me_], sems[j], sems[7 + j], peer).start()
        token[...] = jnp.zeros_like(token)

    outs = _pallas_call(
        body, name=name, in_specs=[ANY, ANY], out_specs=[SEM] * 14 + [ANY, ANY, TOKEN_SPEC],
        out_shape=[pltpu.SemaphoreType.DMA(())] * 14 + [jax.ShapeDtypeStruct(part.shape, F32),
                                                       jax.ShapeDtypeStruct((8, P, LANES), F32), TOKEN],
        input_output_aliases={0: 14, 1: 15}, compiler_params=_split_params(),
    )(part, landing)
    return list(outs[:7]), list(outs[7:14]), outs[14], outs[15], outs[16]


def _all_to_all_small_wait(ssem, rsem, part, landing, after, name):
    def body(*refs):
        sems, src, land = refs[2:16], refs[17], refs[18]
        _, peers = _small_peers()
        for j, (px, py, pc) in enumerate(peers):
            cp = _remote(src, land.at[4 * px + 2 * py + pc], sems[j], sems[7 + j], peers[j])
            cp.wait_send()
            cp.wait_recv()

    return _pallas_call(
        body, name=name, in_specs=[ANY, ANY] + [SEM] * 14 + [ANY], out_specs=[ANY, ANY],
        out_shape=[jax.ShapeDtypeStruct(part.shape, F32), jax.ShapeDtypeStruct(landing.shape, F32)],
        input_output_aliases={0: 0, 1: 1}, compiler_params=_split_params(),
    )(part, landing, *ssem, *rsem, after)[1]


def _row_tile(rows, width, n_arrays):
    t = rows
    while t % 2 == 0 and t > 8 and 2 * n_arrays * t * width * 4 > VMEM_LIMIT // 2:
        t //= 2
    return t


def _chip():
    return 2 * lax.axis_index("x") + lax.axis_index("y")


def _core():
    return lax.axis_index("c")


def _cast_place(w3, layer, name):
    _, r, c = w3.shape
    tr = _row_tile(r, c, 2)

    def body(w_ref, o_ref):
        o_ref[...] = w_ref[...].astype(BF)

    return _pallas_call(
        body, name=name, grid=(r // tr,), in_specs=[pl.BlockSpec((None, tr, c), lambda i: (layer, i, 0))],
        out_specs=pl.BlockSpec((None, tr, c), lambda i: (_chip(), i, 0)),
        out_shape=jax.ShapeDtypeStruct((N_CHIPS, r, c), BF), compiler_params=_params(("parallel",)),
    )(w3)


def _pair_sum(views, recvs, name):
    n = len(views)

    def body(*refs):
        for g_ref, r_ref, o_ref in zip(refs[:n], refs[n:2 * n], refs[2 * n:]):
            o_ref[...] = (g_ref[...].astype(F32) + r_ref[...].astype(F32)).astype(BF)

    own = [pl.BlockSpec((None, None) + v.shape[2:], lambda p: (p, _core(), 0, 0)) for v in views]
    blk = [pl.BlockSpec((None,) + r.shape[1:], lambda p: (p, 0, 0)) for r in recvs]
    return _pallas_call(
        body, name=name, grid=(N_CHIPS,), in_specs=own + blk, out_specs=blk,
        out_shape=[jax.ShapeDtypeStruct(r.shape, BF) for r in recvs], compiler_params=_params(("parallel",)),
    )(*views, *recvs)


CHIP_SUM_STEPS = 2


def _chip_sum(parts, recvs, name):
    n = len(parts)

    def body(*refs):
        for p_ref, r_ref, o_ref in zip(refs[:n], refs[n:2 * n], refs[2 * n:]):
            acc = p_ref[...].astype(F32)
            for j in range(3):
                acc = acc + r_ref[j].astype(F32)
            o_ref[...] = acc

    rows = [p.shape[1] // CHIP_SUM_STEPS for p in parts]
    return _pallas_call(
        body, name=name, grid=(CHIP_SUM_STEPS,),
        in_specs=[pl.BlockSpec((None, t, p.shape[2]), lambda i: (_chip(), i, 0)) for p, t in zip(parts, rows)]
        + [pl.BlockSpec((3, t, p.shape[2]), lambda i: (0, i, 0)) for p, t in zip(parts, rows)],
        out_specs=[pl.BlockSpec((t, p.shape[2]), lambda i: (i, 0)) for p, t in zip(parts, rows)],
        out_shape=[jax.ShapeDtypeStruct(p.shape[1:], F32) for p in parts], compiler_params=_params(("parallel",)),
    )(*parts, *recvs)


def _sum_slices(a, name):
    n, rows, width = a.shape
    tr = _row_tile(rows, width, n + 1)

    def body(a_ref, o_ref):
        acc = a_ref[0].astype(F32)
        for i in range(1, n):
            acc = acc + a_ref[i].astype(F32)
        o_ref[...] = acc

    return _pallas_call(
        body, name=name, grid=(rows // tr,), in_specs=[pl.BlockSpec((n, tr, width), lambda i: (0, i, 0))],
        out_specs=pl.BlockSpec((tr, width), lambda i: (i, 0)), out_shape=jax.ShapeDtypeStruct((rows, width), F32),
        compiler_params=_params(("parallel",)),
    )(a)


def _adamw_update(w, g, m, v):
    nm = ADAM_B1 * m + (1.0 - ADAM_B1) * g
    nv = ADAM_B2 * v + (1.0 - ADAM_B2) * (g * g)
    m_hat = nm / (1.0 - ADAM_B1 ** ADAM_STEP)
    v_hat = nv / (1.0 - ADAM_B2 ** ADAM_STEP)
    return -ADAM_LR * (m_hat / (jnp.sqrt(v_hat) + ADAM_EPS) + ADAM_WD * w), nm, nv


def _adamw(w, g, m, v, name):
    rows, width = w.shape
    tr = _row_tile(rows, width, 7)

    def body(w_ref, g_ref, m_ref, v_ref, d_ref, nm_ref, nv_ref):
        d_ref[...], nm_ref[...], nv_ref[...] = _adamw_update(w_ref[...], g_ref[...], m_ref[...], v_ref[...])

    blk = pl.BlockSpec((tr, width), lambda i: (i, 0))
    return _pallas_call(
        body, name=name, grid=(rows // tr,), in_specs=[blk] * 4, out_specs=[blk] * 3,
        out_shape=[jax.ShapeDtypeStruct((rows, width), F32)] * 3, compiler_params=_params(("parallel",)),
    )(w, g, m, v)


def _adamw_halves(w3, m3, v3, mine, other, name):
    depth, r, c = w3.shape
    assert depth == 2
    hr = r // 2
    tr = _row_tile(hr, c, 11)
    sources = ((0, True, mine[0]), (0, False, other[0]), (1, True, mine[1]), (1, False, other[1]))

    def active(l, h, layer, own):
        mine_half = h == _core()
        return (l == layer) & (mine_half if own else jnp.logical_not(mine_half))

    def body(w_ref, m_ref, v_ref, *rest):
        g_refs, (go_ref, d_ref, nm_ref, nv_ref) = rest[:4], rest[4:]
        l, h = pl.program_id(0), pl.program_id(1)
        for (layer, own, _), g_ref in zip(sources, g_refs):
            @pl.when(active(l, h, layer, own))
            def _():
                gv = g_ref[...]
                go_ref[...] = gv
                d_ref[...], nm_ref[...], nv_ref[...] = _adamw_update(w_ref[...], gv, m_ref[...], v_ref[...])

    def gspec(layer, own):
        return pl.BlockSpec((tr, c), lambda l, h, i: (jnp.where(active(l, h, layer, own), i, 0), 0))

    blk = pl.BlockSpec((None, None, tr, c), lambda l, h, i: (l, h, i, 0))
    view = lambda a: a.reshape(depth, 2, hr, c)
    outs = _pallas_call(
        body, name=name, grid=(depth, 2, hr // tr),
        in_specs=[blk] * 3 + [gspec(layer, own) for layer, own, _ in sources], out_specs=[blk] * 4,
        out_shape=[jax.ShapeDtypeStruct((depth, 2, hr, c), F32)] * 4,
        compiler_params=_params(("parallel", "parallel", "parallel")),
    )(view(w3), view(m3), view(v3), *[s[2] for s in sources])
    return [o.reshape(w3.shape) for o in outs]


BIG = ("w_in", "w_pool_up", "w_conv_out", "w_attn_up", "w_o", "w_ff1", "w_ff2")
SMALL = ("norm_mix", "b_gate", "pool_mix", "pool_scale", "conv_w", "q_gain", "k_gain", "norm_mlp")
ORDER = ("norm_mix", "w_in", "b_gate", "pool_mix", "pool_scale", "conv_w", "q_gain", "k_gain", "w_pool_up",
         "w_conv_out", "w_attn_up", "w_o", "norm_mlp", "w_ff1", "w_ff2")
COLUMN_SHARDED = ("w_pool_up", "w_conv_out", "w_attn_up", "w_ff1")


def _matrix_weights(gathered):
    w = {}
    for name, g4 in gathered.items():
        if name in COLUMN_SHARDED:
            w[name] = g4
        else:
            w[name] = g4.reshape(N_CHIPS * g4.shape[1], g4.shape[2])
    return w


def _small_weights(l, small):
    w = {}
    w["norm_mix"] = small["norm_mix"][l][None]
    w["norm_mlp"] = small["norm_mlp"][l][None]
    w["b_gate"] = small["b_gate"][l][None]
    w["pool_mix"] = small["pool_mix"][l].astype(BF)
    w["pool_scale"] = small["pool_scale"][l][None]
    w["conv_w"] = jnp.pad(small["conv_w_full"][l], ((0, 5), (0, 0)))
    w["qk_gain"] = jnp.pad(jnp.stack([jnp.tile(small["q_gain"][l], 2), jnp.tile(small["k_gain"][l], 2)]), ((0, 6), (0, 0)))
    return w


def _to_chip_major(name, g):
    if name == "w_in":
        return g.T.reshape(N_CHIPS, g.shape[1] // N_CHIPS, g.shape[0])
    if name in COLUMN_SHARDED:
        return g
    return g.reshape(N_CHIPS, g.shape[0] // N_CHIPS, g.shape[1])


def _pad8(a):
    a = a.reshape(-1)
    return jnp.pad(a, (0, (-a.size) % (8 * LANES))).reshape(-1, LANES)


def kernel(x, norm_mix, w_in, b_gate, pool_mix, pool_scale, conv_w, q_gain, k_gain, w_pool_up, w_conv_out, w_attn_up, w_o, norm_mlp, w_ff1, w_ff2, loss_target, m_norm_mix, m_w_in, m_b_gate, m_pool_mix, m_pool_scale, m_conv_w, m_q_gain, m_k_gain, m_w_pool_up, m_w_conv_out, m_w_attn_up, m_w_o, m_norm_mlp, m_w_ff1, m_w_ff2, v_norm_mix, v_w_in, v_b_gate, v_pool_mix, v_pool_scale, v_conv_w, v_q_gain, v_k_gain, v_w_pool_up, v_w_conv_out, v_w_attn_up, v_w_o, v_norm_mlp, v_w_ff1, v_w_ff2):
    weights = dict(norm_mix=norm_mix, w_in=w_in, b_gate=b_gate, pool_mix=pool_mix, pool_scale=pool_scale, conv_w=conv_w,
                   q_gain=q_gain, k_gain=k_gain, w_pool_up=w_pool_up, w_conv_out=w_conv_out, w_attn_up=w_attn_up,
                   w_o=w_o, norm_mlp=norm_mlp, w_ff1=w_ff1, w_ff2=w_ff2)
    moms = dict(norm_mix=m_norm_mix, w_in=m_w_in, b_gate=m_b_gate, pool_mix=m_pool_mix, pool_scale=m_pool_scale,
                conv_w=m_conv_w, q_gain=m_q_gain, k_gain=m_k_gain, w_pool_up=m_w_pool_up, w_conv_out=m_w_conv_out,
                w_attn_up=m_w_attn_up, w_o=m_w_o, norm_mlp=m_norm_mlp, w_ff1=m_w_ff1, w_ff2=m_w_ff2)
    vels = dict(norm_mix=v_norm_mix, w_in=v_w_in, b_gate=v_b_gate, pool_mix=v_pool_mix, pool_scale=v_pool_scale,
                conv_w=v_conv_w, q_gain=v_q_gain, k_gain=v_k_gain, w_pool_up=v_w_pool_up, w_conv_out=v_w_conv_out,
                w_attn_up=v_w_attn_up, w_o=v_w_o, norm_mlp=v_norm_mlp, w_ff1=v_w_ff1, w_ff2=v_w_ff2)
    depth = norm_mix.shape[0]
    q = 2 * lax.axis_index("x") + lax.axis_index("y")
    for group in (weights, moms, vels):
        group["w_in"] = jnp.swapaxes(group["w_in"], 1, 2)

    assert depth == 2, "the second layer's gather hides behind the first layer's forward, and likewise backward"
    first, rest = BIG[:1], BIG[1:]
    cw_all = _all_to_all_small(_pad8(conv_w))
    bufs = [{n: _cast_place(weights[n], 0, f"cast_{n}_l0") for n in first}]
    a_ssem, a_rsem, a_views, a_token = _gather_start([bufs[0][n] for n in first], "gather_start_l0_in", cw_all)
    bufs[0].update({n: _cast_place(weights[n], 0, f"cast_{n}_l0") for n in rest})
    bufs += [{n: _cast_place(weights[n], l, f"cast_{n}_l{l}") for n in BIG} for l in range(1, depth)]
    b_ssem, b_rsem, b_views, b_token = _gather_start([bufs[0][n] for n in rest], "gather_start_l0_rest", a_token)
    g_ssem, g_rsem, g_views, g_token = _gather_start([bufs[1][n] for n in BIG], "gather_start_l1", b_token)
    conv_w_full = jnp.concatenate(
        [cw_all[2 * p].reshape(-1)[:conv_w.size].reshape(conv_w.shape) for p in range(N_CHIPS)], axis=-1)
    small = dict(weights)
    small["conv_w_full"] = conv_w_full

    def soon_weights(t):
        got = _gather_finish(a_ssem, a_rsem, a_views, t, "gather_wait_l0_in", "gather_forward_l0_in",
                             [bufs[0][n].shape for n in first])
        return _matrix_weights(dict(zip(first, got)))

    def late_weights(t):
        got = _gather_finish(b_ssem, b_rsem, b_views, t, "gather_wait_l0_rest", "gather_forward_l0_rest",
                             [bufs[0][n].shape for n in rest])
        return _matrix_weights(dict(zip(rest, got)))

    wl, saved = [None] * depth, [None] * depth
    small_1 = _small_weights(1, small)
    (h, hb_1), saved[0], wl[0] = _layer_fwd(x[0], _small_weights(0, small), "l0", after=g_token, soon=soon_weights,
                                            late=late_weights, next_gain=small_1["norm_mix"])
    got = _gather_finish(g_ssem, g_rsem, g_views, h, "gather_wait_l1", "gather_forward_l1",
                         [bufs[1][n].shape for n in BIG])
    (dh, loss_row), saved[1], wl[1] = _layer_fwd(
        h, dict(small_1, **_matrix_weights(dict(zip(BIG, got)))), "l1", target=loss_target[0], hb=hb_1)

    def pair_stage(names, g, tag):
        views = [_halves(_to_chip_major(n, g[n])) for n in names]
        from_sibling = _pair_swap(views, f"grad_pair_swap_{tag}")
        return _pair_sum(views, from_sibling, f"pair_sum_{tag}")

    mine, other = [{}, {}], [{}, {}]

    def finish(names, l, started, after, tag):
        ssem, rsem, parts, landing, _ = started
        parts, arrived = _chip_exchange_wait(ssem, rsem, parts, landing, after, f"grad_chip_exchange_wait_{tag}")
        got = _chip_sum(parts, arrived, f"chip_sum_{tag}")
        mine[l].update(zip(names, got))
        other[l].update(zip(names, _pair_send(got, f"grad_pair_send_{tag}")))

    def small_pieces(g):
        return [_pad8(g[n][:3] if n == "conv_w" else g[n]) for n in SMALL]

    def start_small(l):
        return _all_to_all_small_start(jnp.concatenate(small_pieces(grads[l]), axis=0), f"small_grad_exchange_start_l{l}")

    grads, early, small = [None] * depth, {}, [None] * depth
    dh, grads[1] = _layer_bwd(dh, wl[1], saved[1], "l1")
    second = _chip_exchange_start(pair_stage(BIG, grads[1], "l1"), "grad_chip_exchange_start_l1")
    small[1] = start_small(1)

    def start_rest(g):
        early["rest"] = _chip_exchange_start(pair_stage(rest, g, "l0_rest"), "grad_chip_exchange_start_l0_rest")
        return early["rest"][4]

    def start_last(g):
        early["in"] = _chip_exchange_start(pair_stage(first, g, "l0_in"), "grad_chip_exchange_start_l0_in")
        return early["in"][4]

    dh, grads[0] = _layer_bwd(dh, wl[0], saved[0], "l0", after=[second[4], small[1][4]], mid=start_rest,
                              tail=start_last)
    small[0] = start_small(0)
    finish(BIG, 1, second, dh, "l1")
    finish(rest, 0, early["rest"], dh, "l0_rest")
    loss = lax.psum(loss_row[0, 0], ("x", "y", "c"))
    full = {}

    deltas, new_m, new_v = {}, {}, {}

    def update_matrix(n):
        full[n], deltas[n], new_m[n], new_v[n] = _adamw_halves(
            weights[n], moms[n], vels[n], [mine[l][n] for l in range(depth)], [other[l][n] for l in range(depth)],
            f"adamw_{n}")

    for n in rest:
        update_matrix(n)
    finish(first, 0, early["in"], deltas[rest[-1]], "l0_in")
    for n in first:
        update_matrix(n)
    summed = []
    for l in range(depth):
        ssem, rsem, part, landing, _ = small[l]
        summed.append(_sum_slices(_all_to_all_small_wait(ssem, rsem, part, landing, deltas[first[-1]],
                                                         f"small_grad_exchange_wait_l{l}"), f"small_sum_l{l}"))
    row = 0
    for n, piece in zip(SMALL, small_pieces(grads[0])):
        size = (weights[n].size if n != "conv_w" else depth * 3 * 512) // depth
        flat = jnp.stack([s[row:row + piece.shape[0]].reshape(-1)[:size] for s in summed])
        row += piece.shape[0]
        if n == "conv_w":
            full[n] = lax.dynamic_slice_in_dim(flat.reshape(depth, 3, 512), q * conv_w.shape[2], conv_w.shape[2], axis=2)
        else:
            full[n] = flat.reshape(weights[n].shape)
    for n in SMALL:
        shape = weights[n].shape
        two_d = (-1, shape[-1]) if n not in ("conv_w", "q_gain", "k_gain") else (1, -1)
        d2, m2, v2 = _adamw(weights[n].reshape(two_d), full[n].reshape(two_d), moms[n].reshape(two_d),
                            vels[n].reshape(two_d), f"adamw_{n}")
        deltas[n], new_m[n], new_v[n] = d2.reshape(shape), m2.reshape(shape), v2.reshape(shape)
        full[n] = full[n].reshape(shape)
    for group in (full, deltas, new_m, new_v):
        group["w_in"] = jnp.swapaxes(group["w_in"], 1, 2)
    return (loss, dh[None], *[full[n] for n in ORDER], *[deltas[n] for n in ORDER], *[new_m[n] for n in ORDER],
            *[new_v[n] for n in ORDER])
```

```python
import functools

import jax
import jax.numpy as jnp
from jax import lax
from jax.experimental import pallas as pl
from jax.experimental.pallas import tpu as pltpu

F32 = jnp.float32
BF = jnp.bfloat16
MESH_ID = pl.DeviceIdType.MESH
ANY = pl.BlockSpec(memory_space=pl.ANY)

EPS = 1e-6
MASK_VALUE = -1e30
POOL_WINDOWS = (2, 4, 8, 16)
ATTN_DILATIONS = (1, 4, 16)
ATTN_BLOCK = 128
HEAD_DIM = 64
OFF_Q, OFF_K, OFF_V, OFF_GATE = 2048, 2816, 3584, 4352
N_CHIPS = 4
ADAM_LR, ADAM_B1, ADAM_B2, ADAM_EPS, ADAM_WD, ADAM_STEP = 0.001, 0.9, 0.999, 1e-08, 0.01, 10

VMEM_LIMIT = 48 * 1024 * 1024
LANES = 128

_DIMS = {"nn": (((1,), (0,)), ((), ())), "nt": (((1,), (1,)), ((), ())), "tn": (((0,), (0,)), ((), ()))}


def _params(sem):
    return pltpu.CompilerParams(dimension_semantics=sem, vmem_limit_bytes=VMEM_LIMIT)


def _pallas_call(body, **kw):
    def in_hbm(s):
        pin = isinstance(s, jax.ShapeDtypeStruct) and s is not TOKEN and jnp.issubdtype(s.dtype, jnp.floating)
        return pltpu.HBM(s.shape, s.dtype) if pin else s

    out_shape = kw.pop("out_shape")
    kw["out_shape"] = [in_hbm(s) for s in out_shape] if isinstance(out_shape, (list, tuple)) else in_hbm(out_shape)
    call = pl.pallas_call(body, **kw)

    def run(*args):
        pinned = [pltpu.with_memory_space_constraint(a, pltpu.HBM)
                  if hasattr(a, "dtype") and jnp.issubdtype(a.dtype, jnp.floating) else a for a in args]
        return call(*pinned)

    return run


def _dot(a, b, mode="nn"):
    return lax.dot_general(a, b, _DIMS[mode], preferred_element_type=F32)


def _mm(a, b, mode, name, *, tm, tn, tk, out_dtype=F32, res=None, aux=None, epi=None, n_outer=False,
        b_shards=False, out_shards=False, after=None, vec=None):
    if mode == "tn":
        K, M = a.shape
    else:
        M, K = a.shape
    if b_shards:
        if mode == "nn":
            assert b.shape[1] == K
            N = b.shape[2] * N_CHIPS
        else:
            assert mode == "nt"
            N = b.shape[1]
            assert b.shape[2] * N_CHIPS == K
    else:
        N = b.shape[0] if mode == "nt" else b.shape[1]
    tm, tn, tk = min(tm, M), min(tn, N), min(tk, K)
    assert M % tm == 0 and N % tn == 0 and K % tk == 0
    nk = K // tk
    if n_outer:
        grid = (N // tn, M // tm, nk)
        ij = lambda p, q_: (q_, p)
    else:
        grid = (M // tm, N // tn, nk)
        ij = lambda p, q_: (p, q_)

    def amap(p, q_, k):
        i, j = ij(p, q_)
        return (k, i) if mode == "tn" else (i, k)

    a_spec = pl.BlockSpec((tk, tm) if mode == "tn" else (tm, tk), amap)
    if b_shards:
        if mode == "nn":
            per = (N // N_CHIPS) // tn
            assert per >= 1 and (N // N_CHIPS) % tn == 0

            def bmap(p, q_, k):
                i, j = ij(p, q_)
                return (j // per, k, j % per)

            b_spec = pl.BlockSpec((None, tk, tn), bmap)
        else:
            per = (K // N_CHIPS) // tk
            assert per >= 1 and (K // N_CHIPS) % tk == 0

            def bmap(p, q_, k):
                i, j = ij(p, q_)
                return (k // per, j, k % per)

            b_spec = pl.BlockSpec((None, tn, tk), bmap)
    else:
        def bmap(p, q_, k):
            i, j = ij(p, q_)
            return (j, k) if mode == "nt" else (k, j)

        b_spec = pl.BlockSpec((tn, tk) if mode == "nt" else (tk, tn), bmap)

    def omap(p, q_, k):
        return ij(p, q_)

    o_spec = pl.BlockSpec((tm, tn), omap)
    if out_shards:
        per_o = (N // N_CHIPS) // tn
        assert per_o >= 1 and (N // N_CHIPS) % tn == 0

        def osmap(p, q_, k):
            i, j = ij(p, q_)
            return (j // per_o, i, j % per_o)

        out_spec0 = pl.BlockSpec((None, tm, tn), osmap)
        out_shape0 = jax.ShapeDtypeStruct((N_CHIPS, M, N // N_CHIPS), out_dtype)
    else:
        out_spec0 = o_spec
        out_shape0 = jax.ShapeDtypeStruct((M, N), out_dtype)

    in_specs = [a_spec, b_spec]
    args = [a, b]
    if res is not None:
        in_specs.append(o_spec)
        args.append(res)
    if aux is not None:
        in_specs.append(o_spec)
        args.append(aux)
    if vec is not None:
        in_specs.append(pl.BlockSpec((1, tn), lambda p, q_, k: (0, ij(p, q_)[1])))
        args.append(vec)
    after = [] if after is None else list(after) if isinstance(after, (list, tuple)) else [after]
    in_specs += [ANY] * len(after)
    args += after
    out_specs = [out_spec0]
    out_shape = [out_shape0]
    reduces = epi in ("loss", "rms_bwd")
    if reduces:
        assert tn == N and not n_outer and not out_shards
        width = LANES if epi == "loss" else N
        out_specs.append(pl.BlockSpec((1, width), lambda p, q_, k: (0, 0)))
        out_shape.append(jax.ShapeDtypeStruct((1, width), F32))
    if epi == "rms_next":
        assert tn == N and not out_shards
        out_specs.append(o_spec)
        out_shape.append(jax.ShapeDtypeStruct((M, N), BF))
    n_out = len(out_shape)
    has_res, has_aux, has_vec, n_after = res is not None, aux is not None, vec is not None, len(after)

    def body(*refs):
        a_ref, b_ref = refs[0], refs[1]
        pos = 2
        res_ref = aux_ref = vec_ref = None
        if has_res:
            res_ref = refs[pos]
            pos += 1
        if has_aux:
            aux_ref = refs[pos]
            pos += 1
        if has_vec:
            vec_ref = refs[pos]
            pos += 1
        pos += n_after
        outs = refs[pos:pos + n_out]
        part = _dot(a_ref[...].astype(BF), b_ref[...].astype(BF), mode)

        first_row_tile = pl.program_id(0) == 0

        def add_to_sum(row):
            @pl.when(first_row_tile)
            def _():
                outs[1][...] = jnp.zeros_like(outs[1])

            outs[1][...] += row

        def finish(acc):
            if epi == "rms_bwd":
                xv = aux_ref[...]
                r = lax.rsqrt(jnp.mean(xv * xv, axis=-1, keepdims=True) + EPS)
                xhat = xv * r
                dy = acc * vec_ref[...]
                outs[0][...] = res_ref[...] + r * (dy - xhat * jnp.mean(dy * xhat, axis=-1, keepdims=True))
                add_to_sum(jnp.sum(acc * xhat, axis=0, keepdims=True))
                return
            if res_ref is not None:
                acc = res_ref[...] + acc
            if epi == "relu2":
                r = jnp.maximum(acc, 0.0)
                outs[0][...] = (r * r).astype(out_dtype)
            elif epi == "drelu2":
                outs[0][...] = (acc.astype(BF) * (2.0 * jnp.sqrt(aux_ref[...]))).astype(out_dtype)
            elif epi == "rms_next":
                outs[0][...] = acc
                r = lax.rsqrt(jnp.mean(acc * acc, axis=-1, keepdims=True) + EPS)
                outs[1][...] = ((acc * r) * vec_ref[...]).astype(BF)
            elif epi == "loss":
                e = acc - aux_ref[...]
                outs[0][...] = e / float(N)
                add_to_sum(0.5 * jnp.sum(jnp.mean(e * e, axis=-1, keepdims=True)))
            else:
                outs[0][...] = acc.astype(out_dtype)

        if nk == 1:
            finish(part)
        else:
            acc_ref = refs[pos + n_out]
            k = pl.program_id(2)

            @pl.when(k == 0)
            def _():
                acc_ref[...] = part

            @pl.when(k > 0)
            def _():
                acc_ref[...] += part

            @pl.when(k == nk - 1)
            def _():
                finish(acc_ref[...])

    scratch = [pltpu.VMEM((tm, tn), F32)] if nk > 1 else []
    out = _pallas_call(
        body, name=name, grid=grid, in_specs=in_specs, out_specs=out_specs, out_shape=out_shape,
        scratch_shapes=scratch,
        compiler_params=_params(("arbitrary" if reduces else "parallel", "parallel", "arbitrary")),
    )(*args)
    return out if n_out > 1 else out[0]


def _rms_fwd(x, gain, name, after=None):
    T, D = x.shape
    tm = min(512, T)

    def body(x_ref, g_ref, *rest):
        o_ref = rest[-1]
        xv = x_ref[...]
        r = lax.rsqrt(jnp.mean(xv * xv, axis=-1, keepdims=True) + EPS)
        o_ref[...] = ((xv * r) * g_ref[...]).astype(BF)

    extra = [] if after is None else list(after) if isinstance(after, (list, tuple)) else [after]
    return _pallas_call(
        body, name=name, grid=(T // tm,),
        in_specs=[pl.BlockSpec((tm, D), lambda i: (i, 0)), pl.BlockSpec((1, D), lambda i: (0, 0))] + [ANY] * len(extra),
        out_specs=pl.BlockSpec((tm, D), lambda i: (i, 0)), out_shape=jax.ShapeDtypeStruct((T, D), BF),
        compiler_params=_params(("parallel",)),
    )(x, gain, *extra)


def _rms_bwd(dh, x, gain, dres, name):
    T, D = x.shape
    tm = min(512, T)

    def body(dh_ref, x_ref, g_ref, dres_ref, dx_ref, dg_ref):
        xv = x_ref[...]
        r = lax.rsqrt(jnp.mean(xv * xv, axis=-1, keepdims=True) + EPS)
        xhat = xv * r
        dhv = dh_ref[...]
        dy = dhv * g_ref[...]
        dx_ref[...] = dres_ref[...] + r * (dy - xhat * jnp.mean(dy * xhat, axis=-1, keepdims=True))

        @pl.when(pl.program_id(0) == 0)
        def _():
            dg_ref[...] = jnp.zeros_like(dg_ref)

        dg_ref[...] += jnp.sum(dhv * xhat, axis=0, keepdims=True)

    row = pl.BlockSpec((tm, D), lambda i: (i, 0))
    vec = pl.BlockSpec((1, D), lambda i: (0, 0))
    return _pallas_call(
        body, name=name, grid=(T // tm,), in_specs=[row, row, vec, row], out_specs=[row, vec],
        out_shape=[jax.ShapeDtypeStruct((T, D), F32), jax.ShapeDtypeStruct((1, D), F32)],
        compiler_params=_params(("arbitrary",)),
    )(dh, x, gain, dres)


POOL_HALO = 16
CONV_HALO = 8
POOLCONV_ROWS = 512


def _causal_window_sum(v, w):
    s, sh = v, 1
    while sh < w:
        s = s + pltpu.roll(s, sh, 0)
        sh *= 2
    return s


def _anticausal_window_sum(v, w):
    n = v.shape[0]
    s, sh = v, 1
    while sh < w:
        s = s + pltpu.roll(s, n - sh, 0)
        sh *= 2
    return s


def _poolconv_fwd(z, pmix_b, pscale, convw, name):
    T = z.shape[0]
    R = min(POOLCONV_ROWS, T)
    PH, CH = R // POOL_HALO, R // CONV_HALO

    def body(u_ref, uh_ref, b_ref, c_ref, ch_ref, x_ref, xh_ref, mix_ref, sc_ref, cw_ref, yp_ref, yc_ref):
        i = pl.program_id(0)
        keep = (i > 0).astype(F32)
        row = i * R + lax.broadcasted_iota(jnp.int32, (R, 1), 0)
        w_all = jnp.concatenate([uh_ref[...] * keep, u_ref[...]], axis=0)
        for g, w in enumerate(POOL_WINDOWS):
            cols = slice(128 * g, 128 * (g + 1))
            wg = w_all[:, cols]
            s = _causal_window_sum(wg, w)[POOL_HALO:]
            inv_cnt = 1.0 / jnp.minimum(row + 1, w).astype(F32)
            dgrp = s * inv_cnt - wg[POOL_HALO:]
            y = _dot(dgrp.astype(BF), mix_ref[g]) * sc_ref[:, cols]
            yp_ref[:, cols] = y.astype(BF)
        uc = jnp.concatenate([ch_ref[...] * xh_ref[...] * keep, c_ref[...] * x_ref[...]], axis=0)
        yc = cw_ref[2:3, :] * uc + cw_ref[0:1, :] * pltpu.roll(uc, 2, 0) + cw_ref[1:2, :] * pltpu.roll(uc, 1, 0)
        yc_ref[...] = (b_ref[...] * yc[CONV_HALO:]).astype(BF)

    def main(cb):
        return pl.BlockSpec((R, 512), lambda i: (i, cb))

    def prev(cb, halo, per):
        return pl.BlockSpec((halo, 512), lambda i: (jnp.maximum(i * per - 1, 0), cb))

    full = lambda a: pl.BlockSpec(a.shape, lambda i: (0,) * a.ndim)
    return _pallas_call(
        body, name=name, grid=(T // R,),
        in_specs=[main(0), prev(0, POOL_HALO, PH), main(1), main(2), prev(2, CONV_HALO, CH), main(3),
                  prev(3, CONV_HALO, CH), full(pmix_b), full(pscale), full(convw)],
        out_specs=[pl.BlockSpec((R, 512), lambda i: (i, 0))] * 2,
        out_shape=[jax.ShapeDtypeStruct((T, 512), BF)] * 2,
        compiler_params=_params(("parallel",)),
    )(z, z, z, z, z, z, z, pmix_b, pscale, convw)


def _poolconv_bwd(z, dyp, dyc, pmix_b, pscale, convw, dz, name):
    T = z.shape[0]
    R = min(POOLCONV_ROWS, T)
    PH, CH = R // POOL_HALO, R // CONV_HALO
    nsteps = T // R

    def body(u_ref, uh_ref, b_ref, bn_ref, c_ref, ch_ref, x_ref, xh_ref, dyp_ref, dypn_ref, dyc_ref, dycn_ref,
             mix_ref, sc_ref, cw_ref, dz_in_ref, dz_ref, dmix_ref, dsc_ref, dcw_ref):
        i = pl.program_id(0)
        keep_prev = (i > 0).astype(F32)
        keep_next = (i < nsteps - 1).astype(F32)

        @pl.when(i == 0)
        def _():
            dmix_ref[...] = jnp.zeros_like(dmix_ref)
            dsc_ref[...] = jnp.zeros_like(dsc_ref)
            dcw_ref[...] = jnp.zeros_like(dcw_ref)

        row = i * R + lax.broadcasted_iota(jnp.int32, (R, 1), 0)
        row_ext = i * R + lax.broadcasted_iota(jnp.int32, (R + POOL_HALO, 1), 0)
        w_all = jnp.concatenate([uh_ref[...] * keep_prev, u_ref[...]], axis=0)
        dyp_ext = jnp.concatenate([dyp_ref[...], dypn_ref[...] * keep_next], axis=0)
        for g, w in enumerate(POOL_WINDOWS):
            cols = slice(128 * g, 128 * (g + 1))
            wg = w_all[:, cols]
            s = _causal_window_sum(wg, w)[POOL_HALO:]
            inv_cnt = 1.0 / jnp.minimum(row + 1, w).astype(F32)
            dgrp = (s * inv_cnt - wg[POOL_HALO:]).astype(BF)
            y_pre = _dot(dgrp, mix_ref[g])
            dsc_ref[:, cols] += jnp.sum(dyp_ref[:, cols] * y_pre, axis=0, keepdims=True)
            dyb = (dyp_ext[:, cols] * sc_ref[:, cols]).astype(BF)
            dmix_ref[cols, :] += _dot(dgrp, dyb[:R], "tn")
            dd = _dot(dyb, mix_ref[g], "nt")
            inv_cnt_ext = 1.0 / jnp.minimum(row_ext + 1, w).astype(F32)
            e = _anticausal_window_sum(dd * inv_cnt_ext, w)
            dz_ref[:, cols] = (e[:R] - dd[:R]).astype(BF)
        cw0, cw1, cw2 = cw_ref[0:1, :], cw_ref[1:2, :], cw_ref[2:3, :]
        uc = jnp.concatenate([ch_ref[...] * xh_ref[...] * keep_prev, c_ref[...] * x_ref[...]], axis=0)
        uc1 = pltpu.roll(uc, 1, 0)[CONV_HALO:]
        uc2 = pltpu.roll(uc, 2, 0)[CONV_HALO:]
        uc0 = uc[CONV_HALO:]
        yc = cw2 * uc0 + cw0 * uc2 + cw1 * uc1
        dycv = dyc_ref[...]
        dz_ref[:, 512:1024] = (dycv * yc).astype(BF)
        dv_ext = jnp.concatenate([dycv * b_ref[...], dycn_ref[...] * bn_ref[...] * keep_next], axis=0)
        n_ext = R + CONV_HALO
        duc = (cw2 * dv_ext + cw1 * pltpu.roll(dv_ext, n_ext - 1, 0) + cw0 * pltpu.roll(dv_ext, n_ext - 2, 0))[:R]
        dv = dv_ext[:R]
        dcw_ref[0:1, :] += jnp.sum(dv * uc2, axis=0, keepdims=True)
        dcw_ref[1:2, :] += jnp.sum(dv * uc1, axis=0, keepdims=True)
        dcw_ref[2:3, :] += jnp.sum(dv * uc0, axis=0, keepdims=True)
        dz_ref[:, 1024:1536] = (duc * x_ref[...]).astype(BF)
        dz_ref[:, 1536:2048] = (duc * c_ref[...]).astype(BF)

    def main(cb):
        return pl.BlockSpec((R, 512), lambda i: (i, cb))

    def prev(cb, halo, per):
        return pl.BlockSpec((halo, 512), lambda i: (jnp.maximum(i * per - 1, 0), cb))

    def nxt(cb, halo, per):
        return pl.BlockSpec((halo, 512), lambda i: (jnp.minimum((i + 1) * per, T // halo - 1), cb))

    full = lambda a: pl.BlockSpec(a.shape, lambda i: (0,) * a.ndim)
    return _pallas_call(
        body, name=name, grid=(nsteps,),
        in_specs=[main(0), prev(0, POOL_HALO, PH), main(1), nxt(1, CONV_HALO, CH), main(2), prev(2, CONV_HALO, CH),
                  main(3), prev(3, CONV_HALO, CH), main(0), nxt(0, POOL_HALO, PH), main(0), nxt(0, CONV_HALO, CH),
                  full(pmix_b), full(pscale), full(convw), ANY],
        out_specs=[pl.BlockSpec((R, 2048), lambda i: (i, 0)), pl.BlockSpec((512, 128), lambda i: (0, 0)),
                   pl.BlockSpec((1, 512), lambda i: (0, 0)), pl.BlockSpec((8, 512), lambda i: (0, 0))],
        out_shape=[jax.ShapeDtypeStruct(dz.shape, BF), jax.ShapeDtypeStruct((512, 128), F32),
                   jax.ShapeDtypeStruct((1, 512), F32), jax.ShapeDtypeStruct((8, 512), F32)],
        input_output_aliases={15: 0}, compiler_params=_params(("arbitrary",)),
    )(z, z, z, z, z, z, z, z, dyp, dyp, dyc, dyc, pmix_b, pscale, convw, dz)


def _head_sums(v):
    row = lax.broadcasted_iota(jnp.int32, (LANES, LANES), 0) < HEAD_DIM
    col = lax.broadcasted_iota(jnp.int32, (LANES, LANES), 1) < HEAD_DIM
    same_head = jnp.where(jnp.logical_xor(row, col), 0.0, 1.0).astype(BF)
    hi = v.astype(BF)
    lo = (v - hi.astype(F32)).astype(BF)
    return _dot(hi, same_head) + _dot(lo, same_head)


def _head_norm(x, g2, ma):
    r = lax.rsqrt(_head_sums(x * x) / HEAD_DIM + EPS)
    return x * r, r


def _head_norm_bwd(dy, xhat, r, g2, ma):
    dxh = dy * g2
    return r * (dxh - xhat * (_head_sums(dxh * xhat) / HEAD_DIM))


def _attn_masks(other_block_exists):
    lane = lax.broadcasted_iota(jnp.int32, (2 * ATTN_BLOCK, ATTN_BLOCK), 1)
    qi = lax.broadcasted_iota(jnp.int32, (2 * ATTN_BLOCK, ATTN_BLOCK), 0) & (ATTN_BLOCK - 1)
    never = (1 - other_block_exists.astype(jnp.int32)) * (2 * ATTN_BLOCK)
    return lane[:ATTN_BLOCK] < HEAD_DIM, lane <= qi, lane >= qi + never


def _stack_heads(x, ma):
    return jnp.concatenate([jnp.where(ma, x, 0.0), jnp.where(ma, 0.0, x)], axis=0)


def _unstack_heads(y, ma):
    return jnp.where(ma, y[:ATTN_BLOCK], y[ATTN_BLOCK:])


def _stack_cols(tile, ma):
    return jnp.concatenate([tile[:, 0:1], tile[:, HEAD_DIM:HEAD_DIM + 1]], axis=0)


QKV_TILES = (OFF_GATE - OFF_Q) // LANES
KIND_TILES = QKV_TILES // 3


def _qk_norm(z, gains, name):
    T = z.shape[0]
    tm = min(512, T)

    def body(x_ref, g_ref, o_ref):
        ma = lax.broadcasted_iota(jnp.int32, (tm, LANES), 1) < HEAD_DIM
        for tile in range(QKV_TILES):
            v = x_ref[:, LANES * tile:LANES * (tile + 1)]
            if tile < 2 * KIND_TILES:
                g = g_ref[0:1, :] if tile < KIND_TILES else g_ref[1:2, :]
                v = _head_norm(v, g, ma)[0] * g
            o_ref[tile] = v

    return _pallas_call(
        body, name=name, grid=(T // tm,),
        in_specs=[pl.BlockSpec((pl.Element(tm), pl.Element(OFF_GATE - OFF_Q)), lambda i: (i * tm, OFF_Q)),
                  pl.BlockSpec((8, LANES), lambda i: (0, 0))],
        out_specs=pl.BlockSpec((QKV_TILES, tm, LANES), lambda i: (0, i, 0)),
        out_shape=jax.ShapeDtypeStruct((QKV_TILES, T, LANES), F32), compiler_params=_params(("parallel",)),
    )(z, gains)


ATTN_STEP_ROWS = 1024
ATTN_UNROLL = 4


def _attn_geometry(T, d):
    sub = ATTN_BLOCK * d
    nb = T // sub
    m = max(1, min(nb, ATTN_STEP_ROWS // sub))
    assert T % sub == 0 and nb % m == 0
    return sub, nb, m


def _attn_rows(jj, r, sub, d):
    start = jj * sub + r
    if d == 1:
        return pl.ds(pl.multiple_of(start, ATTN_BLOCK), ATTN_BLOCK)
    return pl.ds(start, ATTN_BLOCK, stride=d)


def _pick(flag, a, b):
    return jnp.where(jnp.full(a.shape, flag.astype(jnp.int32)) > 0, a, b)


def _attn_fwd(qkv, g, d, name):
    T = qkv.shape[1]
    sub, nb, m = _attn_geometry(T, d)
    scale = HEAD_DIM ** -0.5

    def body(q_ref, kc_ref, kp_ref, vc_ref, vp_ref, o_ref, lse_ref):
        jb = pl.program_id(0)

        def step(s, carry):
            jj, r = s // d, s % d
            here, before = _attn_rows(jj, r, sub, d), _attn_rows(jnp.maximum(jj - 1, 0), r, sub, d)
            edge = _attn_rows(0, r, sub, d)
            first = jj == 0
            ma, mask_c, mask_p = _attn_masks(jb * m + jj > 0)
            qs = _stack_heads(q_ref[here, :], ma).astype(BF)
            kcb = kc_ref[here, :].astype(BF)
            kpb = _pick(first, kp_ref[edge, :], kc_ref[before, :]).astype(BF)
            vcb = vc_ref[here, :].astype(BF)
            vpb = _pick(first, vp_ref[edge, :], vc_ref[before, :]).astype(BF)
            s_c = jnp.where(mask_c, _dot(qs, kcb, "nt") * scale, MASK_VALUE)
            s_p = jnp.where(mask_p, _dot(qs, kpb, "nt") * scale, MASK_VALUE)
            mx = jnp.maximum(jnp.max(s_c, axis=-1, keepdims=True), jnp.max(s_p, axis=-1, keepdims=True))
            p_c = jnp.exp(s_c - mx)
            p_p = jnp.exp(s_p - mx)
            den = jnp.sum(p_c, axis=-1, keepdims=True) + jnp.sum(p_p, axis=-1, keepdims=True)
            o = (_dot(p_c.astype(BF), vcb) + _dot(p_p.astype(BF), vpb)) / den
            o_ref[here, :] = _unstack_heads(o, ma)
            lse_ref[here, :] = _unstack_heads(jnp.broadcast_to(mx + jnp.log(den), o.shape), ma)
            return carry

        lax.fori_loop(0, m * d, step, 0, unroll=ATTN_UNROLL)

    def cur(kind):
        return pl.BlockSpec((None, m * sub, LANES), lambda j, t: (KIND_TILES * kind + 2 * g + t, j, 0))

    def prv(kind):
        return pl.BlockSpec((None, sub, LANES), lambda j, t: (KIND_TILES * kind + 2 * g + t, jnp.maximum(j * m - 1, 0), 0))

    out = pl.BlockSpec((m * sub, LANES), lambda j, t: (j, t))
    return _pallas_call(
        body, name=name, grid=(nb // m, 2), in_specs=[cur(0), cur(1), prv(1), cur(2), prv(2)],
        out_specs=[out, out], out_shape=[jax.ShapeDtypeStruct((T, 256), F32)] * 2,
        compiler_params=_params(("parallel", "parallel")),
    )(qkv, qkv, qkv, qkv, qkv)


def _attn_bwd(z, qkv, do, c, lse, gains, g, d, name, after=None):
    T = z.shape[0]
    sub, nb, m = _attn_geometry(T, d)
    scale = HEAD_DIM ** -0.5
    extra = [] if after is None else [after]

    def body(qr_ref, kr_ref, vc_ref, vp_ref, qn_ref, qnn_ref, kn_ref, knp_ref, do_ref, don_ref, c_ref, cn_ref,
             lse_ref, lsen_ref, g_ref, *rest):
        dq_ref, dk_ref, dv_ref, dgq_ref, dgk_ref, sq_ref, sk_ref, sv_ref = rest[len(extra):]
        jb = pl.program_id(0)

        @pl.when((jb == 0) & (pl.program_id(1) == 0))
        def _():
            dgq_ref[...] = jnp.zeros_like(dgq_ref)
            dgk_ref[...] = jnp.zeros_like(dgk_ref)

        gq, gk = g_ref[0:1, :], g_ref[1:2, :]

        def step(s, carry):
            jj, r = s // d, s % d
            here, edge = _attn_rows(jj, r, sub, d), _attn_rows(0, r, sub, d)
            before = _attn_rows(jnp.maximum(jj - 1, 0), r, sub, d)
            behind = _attn_rows(jnp.minimum(jj + 1, m - 1), r, sub, d)
            first, last = jj == 0, jj == m - 1
            block = jb * m + jj
            ma, mask_c, mask_p = _attn_masks(block > 0)
            mask_n = _attn_masks(block < nb - 1)[2]
            qhat, rq = _head_norm(qr_ref[here, :], gq, ma)
            qn = qn_ref[here, :]
            qn_next = _pick(last, qnn_ref[edge, :], qn_ref[behind, :])
            khat, rk = _head_norm(kr_ref[here, :], gk, ma)
            kcb = kn_ref[here, :].astype(BF)
            kpb = _pick(first, knp_ref[edge, :], kn_ref[before, :]).astype(BF)
            vcb = vc_ref[here, :].astype(BF)
            vpb = _pick(first, vp_ref[edge, :], vc_ref[before, :]).astype(BF)
            do_t, don_t = do_ref[here, :], _pick(last, don_ref[edge, :], do_ref[behind, :])
            c_t, cn_t = c_ref[here, :], _pick(last, cn_ref[edge, :], c_ref[behind, :])
            lse_t, lsen_t = lse_ref[here, :], _pick(last, lsen_ref[edge, :], lse_ref[behind, :])
            qs, dos = _stack_heads(qn, ma).astype(BF), _stack_heads(do_t, ma).astype(BF)
            lse_s, c_s = _stack_cols(lse_t, ma), _stack_cols(c_t, ma)
            s_c = jnp.where(mask_c, _dot(qs, kcb, "nt") * scale, MASK_VALUE)
            s_p = jnp.where(mask_p, _dot(qs, kpb, "nt") * scale, MASK_VALUE)
            p_c = jnp.exp(s_c - lse_s)
            p_p = jnp.exp(s_p - lse_s)
            ds_c = ((p_c * (_dot(dos, vcb, "nt") + c_s)) * scale).astype(BF)
            ds_p = ((p_p * (_dot(dos, vpb, "nt") + c_s)) * scale).astype(BF)
            dq_t = _unstack_heads(_dot(ds_c, kcb) + _dot(ds_p, kpb), ma)
            qs_n, dos_n = _stack_heads(qn_next, ma).astype(BF), _stack_heads(don_t, ma).astype(BF)
            s_n = jnp.where(mask_n, _dot(qs_n, kcb, "nt") * scale, MASK_VALUE)
            p_n = jnp.exp(s_n - _stack_cols(lsen_t, ma))
            ds_n = ((p_n * (_dot(dos_n, vcb, "nt") + _stack_cols(cn_t, ma))) * scale).astype(BF)
            dv_t = _dot(p_c.astype(BF), dos, "tn") + _dot(p_n.astype(BF), dos_n, "tn")
            dk_t = _dot(ds_c, qs, "tn") + _dot(ds_n, qs_n, "tn")
            sq_ref[here, :] = _head_norm_bwd(dq_t, qhat, rq, gq, ma)
            sk_ref[here, :] = _head_norm_bwd(dk_t, khat, rk, gk, ma)
            sv_ref[here, :] = dv_t
            dgq_ref[...] += jnp.sum(dq_t * qhat, axis=0, keepdims=True)
            dgk_ref[...] += jnp.sum(dk_t * khat, axis=0, keepdims=True)
            return carry

        lax.fori_loop(0, m * d, step, 0, unroll=ATTN_UNROLL)
        dq_ref[...] = sq_ref[...].astype(BF)
        dk_ref[...] = sk_ref[...].astype(BF)
        dv_ref[...] = sv_ref[...].astype(BF)

    def raw(col0):
        return pl.BlockSpec((m * sub, LANES), lambda j, t: (j, col0 + 2 * g + t))

    def cur(kind):
        return pl.BlockSpec((None, m * sub, LANES), lambda j, t: (KIND_TILES * kind + 2 * g + t, j, 0))

    def prv(kind):
        return pl.BlockSpec((None, sub, LANES), lambda j, t: (KIND_TILES * kind + 2 * g + t, jnp.maximum(j * m - 1, 0), 0))

    def nxt(kind):
        return pl.BlockSpec((None, sub, LANES),
                            lambda j, t: (KIND_TILES * kind + 2 * g + t, jnp.minimum((j + 1) * m, nb - 1), 0))

    own = pl.BlockSpec((m * sub, LANES), lambda j, t: (j, t))
    own_next = pl.BlockSpec((sub, LANES), lambda j, t: (jnp.minimum((j + 1) * m, nb - 1), t))
    vec = pl.BlockSpec((1, LANES), lambda j, t: (0, 0))
    return _pallas_call(
        body, name=name, grid=(nb // m, 2),
        in_specs=[raw(OFF_Q // LANES), raw(OFF_K // LANES), cur(2), prv(2), cur(0), nxt(0), cur(1), prv(1), own, own_next,
                  own, own_next,
                  own, own_next, pl.BlockSpec((8, LANES), lambda j, t: (0, 0))] + [ANY] * len(extra),
        out_specs=[own, own, own, vec, vec],
        out_shape=[jax.ShapeDtypeStruct((T, 256), BF)] * 3 + [jax.ShapeDtypeStruct((1, LANES), F32)] * 2,
        scratch_shapes=[pltpu.VMEM((m * sub, LANES), F32)] * 3,
        compiler_params=_params(("arbitrary", "arbitrary")),
    )(z, z, qkv, qkv, qkv, qkv, qkv, qkv, do, do, c, c, lse, lse, gains, *extra)


MERGE_ROWS = 256
GATE_TILE = 256


def _group_mix(o_refs, lse_refs):
    lses = [r[...] for r in lse_refs]
    m = jnp.maximum(jnp.maximum(lses[0], lses[1]), lses[2])
    es = [jnp.exp(l - m) for l in lses]
    den = es[0] + es[1] + es[2]
    ws = [e / den for e in es]
    y = ws[0] * o_refs[0][...] + ws[1] * o_refs[1][...] + ws[2] * o_refs[2][...]
    return ws, y


def _sigmoid(v):
    return 1.0 / (1.0 + jnp.exp(-v))


def _merge_specs(T, z, bgate, gpu, gco, gau):
    tm = min(MERGE_ROWS, T)
    row = lambda w: pl.BlockSpec((tm, w), lambda i: (i, 0))
    gate0 = OFF_GATE // GATE_TILE
    gates = [pl.BlockSpec((tm, GATE_TILE), functools.partial(lambda i, cb: (i, cb), cb=gate0 + n))
             for n in range(3 * N_CHIPS)]
    full = lambda a: pl.BlockSpec(a.shape, lambda i: (0,) * a.ndim)
    specs = [row(512), row(512)] + [row(256)] * 6 + gates + [full(bgate), full(gpu), full(gco), full(gau)]
    return tm, row, specs


def _merge_fwd(yp, yc, o3, lse3, z, bgate, gpu, gco, gau, name):
    T = yp.shape[0]
    tm, row, specs = _merge_specs(T, z, bgate, gpu, gco, gau)

    def body(*refs):
        yp_ref, yc_ref = refs[0], refs[1]
        o_refs, lse_refs = refs[2:5], refs[5:8]
        zg = refs[8:20]
        b_ref, gpu_ref, gco_ref, gau_ref, out_ref = refs[20:25]
        yab = _group_mix(o_refs, lse_refs)[1].astype(BF)
        ys = (yp_ref[...], yc_ref[...], yab)
        ups = (gpu_ref, gco_ref, gau_ref)
        for n in range(N_CHIPS):
            acc = None
            for b in range(3):
                gcol = slice(1024 * b + GATE_TILE * n, 1024 * b + GATE_TILE * (n + 1))
                gate = _sigmoid(zg[N_CHIPS * b + n][...] + b_ref[:, gcol])
                term = gate * _dot(ys[b], ups[b][n])
                acc = term if acc is None else acc + term
            out_ref[:, GATE_TILE * n:GATE_TILE * (n + 1)] = acc.astype(BF)

    return _pallas_call(
        body, name=name, grid=(T // tm,), in_specs=specs, out_specs=row(1024),
        out_shape=jax.ShapeDtypeStruct((T, 1024), BF), compiler_params=_params(("parallel",)),
    )(yp, yc, *o3, *lse3, *([z] * 12), bgate, gpu, gco, gau)


def _merge_bwd(dm, yp, yc, o3, lse3, z, bgate, gpu, gco, gau, name):
    T = yp.shape[0]
    tm, row, specs = _merge_specs(T, z, bgate, gpu, gco, gau)
    nsteps = T // tm

    def body(*refs):
        dm_ref, yp_ref, yc_ref = refs[0:3]
        o_refs, lse_refs = refs[3:6], refs[6:9]
        zg = refs[9:21]
        b_ref, gpu_ref, gco_ref, gau_ref = refs[21:25]
        dzg_ref, dyp_ref, dyc_ref = refs[25:28]
        do_refs, c_refs = refs[28:31], refs[31:34]
        dgpu_ref, dgco_ref, dgau_ref, dbg_ref = refs[34:38]
        accs = refs[38:41]
        i = pl.program_id(0)

        @pl.when(i == 0)
        def _():
            for a in accs:
                a[...] = jnp.zeros_like(a)
            dbg_ref[...] = jnp.zeros_like(dbg_ref)

        ws, y = _group_mix(o_refs, lse_refs)
        ys = (yp_ref[...], yc_ref[...], y.astype(BF))
        ups = (gpu_ref, gco_ref, gau_ref)
        dys = [None, None, None]
        for n in range(N_CHIPS):
            dmn = dm_ref[:, GATE_TILE * n:GATE_TILE * (n + 1)]
            for b in range(3):
                gcol = slice(1024 * b + GATE_TILE * n, 1024 * b + GATE_TILE * (n + 1))
                gate = _sigmoid(zg[N_CHIPS * b + n][...] + b_ref[:, gcol])
                up = _dot(ys[b], ups[b][n])
                dzg = (dmn * up) * (gate * (1.0 - gate))
                dzg_ref[:, gcol] = dzg.astype(BF)
                dbg_ref[:, gcol] += jnp.sum(dzg, axis=0, keepdims=True)
                dup = (dmn * gate).astype(BF)
                accs[b][n] += _dot(ys[b], dup, "tn")
                dyb = _dot(dup, ups[b][n], "nt")
                dys[b] = dyb if dys[b] is None else dys[b] + dyb
        dyp_ref[...] = dys[0]
        dyc_ref[...] = dys[1]
        dya = dys[2]
        lane = lax.broadcasted_iota(jnp.int32, dya.shape, 1) // HEAD_DIM
        pr = dya * y
        rho = jnp.zeros_like(pr)
        for h in range(256 // HEAD_DIM):
            hm = lane == h
            rho = jnp.where(hm, jnp.sum(jnp.where(hm, pr, 0.0), axis=-1, keepdims=True), rho)
        for g in range(3):
            do_refs[g][...] = ws[g] * dya
            c_refs[g][...] = -(ws[g] * rho)

        @pl.when(i == nsteps - 1)
        def _():
            dgpu_ref[...] = accs[0][...].astype(BF)
            dgco_ref[...] = accs[1][...].astype(BF)
            dgau_ref[...] = accs[2][...].astype(BF)

    full = lambda a: pl.BlockSpec(a.shape, lambda i: (0,) * a.ndim)
    dz_gate = pl.BlockSpec((pl.Element(tm), pl.Element(3072)), lambda i: (i * tm, OFF_GATE))
    out_specs = ([dz_gate, row(512), row(512)] + [row(256)] * 6 + [full(gpu), full(gco), full(gau)]
                 + [pl.BlockSpec((1, 3072), lambda i: (0, 0))])
    out_shape = ([jax.ShapeDtypeStruct(z.shape, BF)] + [jax.ShapeDtypeStruct((T, 512), F32)] * 2
                 + [jax.ShapeDtypeStruct((T, 256), F32)] * 6
                 + [jax.ShapeDtypeStruct(g.shape, BF) for g in (gpu, gco, gau)]
                 + [jax.ShapeDtypeStruct((1, 3072), F32)])
    return _pallas_call(
        body, name=name, grid=(nsteps,), in_specs=[row(1024)] + specs, out_specs=out_specs, out_shape=out_shape,
        scratch_shapes=[pltpu.VMEM(g.shape, F32) for g in (gpu, gco, gau)],
        compiler_params=_params(("arbitrary",)),
    )(dm, yp, yc, *o3, *lse3, *([z] * 12), bgate, gpu, gco, gau)


def _layer_fwd(x, w, tag, after=None, soon=None, late=None, target=None, hb=None, next_gain=None):
    if hb is None:
        hb = _rms_fwd(x, w["norm_mix"], f"rms_mix_{tag}", after=after)
    if soon is not None:
        w = dict(w, **soon(hb))
    z = _mm(hb, w["w_in"], "nt", f"in_proj_{tag}", tm=512, tn=3712, tk=1024, n_outer=True)
    yp, yc = _poolconv_fwd(z, w["pool_mix"], w["pool_scale"], w["conv_w"], f"poolconv_{tag}")
    qkv = _qk_norm(z, w["qk_gain"], f"qk_norm_{tag}")
    o3, lse3 = [], []
    for g, d in enumerate(ATTN_DILATIONS):
        o, lse = _attn_fwd(qkv, g, d, f"attn{g}_{tag}")
        o3.append(o)
        lse3.append(lse)
    if late is not None:
        w = dict(w, **late(lse3[-1]))
    merged = _merge_fwd(yp, yc, o3, lse3, z, w["b_gate"], w["w_pool_up"], w["w_conv_out"], w["w_attn_up"],
                        f"merge_{tag}")
    x1, h2b = _mm(merged, w["w_o"], "nn", f"out_proj_{tag}", tm=1024, tn=1024, tk=1024, res=x, vec=w["norm_mlp"],
                  epi="rms_next")
    rb = _mm(h2b, w["w_ff1"], "nn", f"ff1_{tag}", tm=1024, tn=1024, tk=1024, out_dtype=BF, epi="relu2", n_outer=True,
             b_shards=True)
    if target is not None:
        x2 = _mm(rb, w["w_ff2"], "nn", f"ff2_{tag}", tm=512, tn=1024, tk=4096, res=x1, aux=target, epi="loss")
    elif next_gain is not None:
        x2 = _mm(rb, w["w_ff2"], "nn", f"ff2_{tag}", tm=512, tn=1024, tk=4096, res=x1, vec=next_gain, epi="rms_next")
    else:
        x2 = _mm(rb, w["w_ff2"], "nn", f"ff2_{tag}", tm=512, tn=1024, tk=4096, res=x1)
    saved = dict(x=x, hb=hb, z=z, yp=yp, yc=yc, qkv=qkv, o3=o3, lse3=lse3, merged=merged, x1=x1, h2b=h2b, rb=rb)
    return x2, saved, w


def _layer_bwd(dx2, w, s, tag, after=None, mid=None, tail=None):
    g = {}
    dab = _mm(dx2, w["w_ff2"], "nt", f"d_ff2_act_{tag}", tm=1024, tn=1024, tk=1024, out_dtype=BF, aux=s["rb"],
              epi="drelu2", after=after)
    g["w_ff2"] = _mm(s["rb"], dx2, "tn", f"d_ff2_w_{tag}", tm=1024, tn=1024, tk=2048, out_dtype=BF)
    g["w_ff1"] = _mm(s["h2b"], dab, "tn", f"d_ff1_w_{tag}", tm=1024, tn=1024, tk=2048, out_dtype=BF, out_shards=True)
    dx1, g["norm_mlp"] = _mm(dab, w["w_ff1"], "nt", f"d_ff1_act_{tag}", tm=1024, tn=1024, tk=1024, b_shards=True,
                             res=dx2, aux=s["x1"], vec=w["norm_mlp"], epi="rms_bwd")
    dm = _mm(dx1, w["w_o"], "nt", f"d_out_act_{tag}", tm=1024, tn=1024, tk=1024)
    g["w_o"] = _mm(s["merged"], dx1, "tn", f"d_out_w_{tag}", tm=1024, tn=1024, tk=1024, out_dtype=BF)
    (dz, dyp, dyc, do0, do1, do2, c0, c1, c2, g["w_pool_up"], g["w_conv_out"], g["w_attn_up"],
     g["b_gate"]) = _merge_bwd(dm, s["yp"], s["yc"], s["o3"], s["lse3"], s["z"], w["b_gate"], w["w_pool_up"],
                               w["w_conv_out"], w["w_attn_up"], f"d_merge_{tag}")
    behind = mid(g) if mid is not None else None
    dq, dk, dv = [], [], []
    dgq = dgk = None
    for gi, d in enumerate(ATTN_DILATIONS):
        dzq, dzk, dzv, pq, pk = _attn_bwd(s["z"], s["qkv"], (do0, do1, do2)[gi], (c0, c1, c2)[gi], s["lse3"][gi],
                                          w["qk_gain"], gi, d, f"d_attn{gi}_{tag}", after=behind)
        dq.append(dzq)
        dk.append(dzk)
        dv.append(dzv)
        dgq = pq if dgq is None else dgq + pq
        dgk = pk if dgk is None else dgk + pk
    g["q_gain"] = dgq[:, :HEAD_DIM] + dgq[:, HEAD_DIM:]
    g["k_gain"] = dgk[:, :HEAD_DIM] + dgk[:, HEAD_DIM:]
    for off, pieces in ((OFF_Q, dq), (OFF_K, dk), (OFF_V, dv)):
        for gi, piece in enumerate(pieces):
            dz = lax.dynamic_update_slice(dz, piece, (0, off + 256 * gi))
    dz, g["pool_mix"], g["pool_scale"], g["conv_w"] = _poolconv_bwd(
        s["z"], dyp, dyc, w["pool_mix"], w["pool_scale"], w["conv_w"], dz, f"d_poolconv_{tag}")
    g["w_in"] = _mm(s["hb"], dz, "tn", f"d_in_w_{tag}", tm=512, tn=3712, tk=1024, out_dtype=BF)
    dh = _mm(dz, w["w_in"], "nn", f"d_in_act_{tag}", tm=1024, tn=1024, tk=3712,
             after=tail(g) if tail is not None else None)
    dx, g["norm_mix"] = _rms_bwd(dh, s["x"], w["norm_mix"], dx1, f"d_rms_mix_{tag}")
    return dx, g


def _position():
    x, y, c = lax.axis_index("x"), lax.axis_index("y"), lax.axis_index("c")
    chips = [(1 - x, y), (x, 1 - y), (1 - x, 1 - y)]
    return x, y, c, 2 * x + y, chips, [2 * cx + cy for cx, cy in chips]


def _remote(src, dst, ssem, rsem, dev):
    return pltpu.make_async_remote_copy(src_ref=src, dst_ref=dst, send_sem=ssem, recv_sem=rsem, device_id=dev,
                                        device_id_type=MESH_ID)


def _halves(a):
    return a.reshape(a.shape[0], 2, a.shape[1] // 2, a.shape[2])


SEM = pl.BlockSpec(memory_space=pltpu.SEMAPHORE)
TOKEN = jax.ShapeDtypeStruct((8, LANES), F32)
TOKEN_SPEC = pl.BlockSpec(memory_space=pltpu.VMEM)


def _split_params():
    return pltpu.CompilerParams(has_side_effects=pltpu.SideEffectType.DATAFLOW_SIDE_EFFECTING)


def _gather_start(bufs, name, after):
    n = len(bufs)
    views = [_halves(b) for b in bufs]

    def body(*refs):
        first_sem = n + 1
        ssem, rsem = refs[first_sem:first_sem + ns], refs[first_sem + ns:first_sem + 2 * ns]
        outs, token = refs[first_sem + 2 * ns:first_sem + 2 * ns + n], refs[first_sem + 2 * ns + n]
        x, y, c, q, chips, qs = _position()
        for k in range(n):
            mine = outs[k].at[q, c]
            for j, chip in enumerate(chips):
                _remote(mine, mine, ssem[3 * k + j], rsem[3 * k + j], (chip[0], chip[1], c)).start()
        token[...] = jnp.zeros_like(token)

    ns = 3 * n
    outs = _pallas_call(
        body, name=name, in_specs=[ANY] * (n + 1), out_specs=[SEM] * (2 * ns) + [ANY] * n + [TOKEN_SPEC],
        out_shape=[pltpu.SemaphoreType.DMA(())] * (2 * ns) + [jax.ShapeDtypeStruct(v.shape, v.dtype) for v in views]
        + [TOKEN],
        input_output_aliases={k: k + 2 * ns for k in range(n)}, compiler_params=_split_params(),
    )(*views, after)
    return list(outs[:ns]), list(outs[ns:2 * ns]), list(outs[2 * ns:2 * ns + n]), outs[2 * ns + n]


def _gather_finish(ssem, rsem, views, after, name_wait, name_forward, shapes):
    n = len(views)
    ns = len(ssem)

    def wait_body(*refs):
        ssem_ref, rsem_ref = refs[n:n + ns], refs[n + ns:n + 2 * ns]
        outs = refs[n + 2 * ns + 1:]
        x, y, c, q, chips, qs = _position()
        for k in range(n):
            for j, chip in enumerate(chips):
                cp = _remote(outs[k].at[q, c], outs[k].at[qs[j], c], ssem_ref[3 * k + j], rsem_ref[3 * k + j],
                             (chip[0], chip[1], c))
                cp.wait_send()
                cp.wait_recv()

    landed = _pallas_call(
        wait_body, name=name_wait, in_specs=[ANY] * n + [SEM] * (2 * ns) + [ANY], out_specs=[ANY] * n,
        out_shape=[jax.ShapeDtypeStruct(v.shape, v.dtype) for v in views],
        input_output_aliases={k: k for k in range(n)}, compiler_params=_split_params(),
    )(*views, *ssem, *rsem, after)

    def forward_body(*refs):
        outs = refs[n:2 * n]
        fssem, frsem = refs[2 * n:]
        x, y, c, q, chips, qs = _position()
        sib = (x, y, 1 - c)
        sent = []
        for k in range(n):
            for j in range(3):
                slot = outs[k].at[qs[j], c]
                cp = _remote(slot, slot, fssem.at[k, j], frsem.at[k, j], sib)
                cp.start()
                sent.append(cp)
        for k in range(n):
            for j in range(3):
                slot = outs[k].at[qs[j], 1 - c]
                _remote(slot, slot, fssem.at[k, j], frsem.at[k, j], sib).wait_recv()
        for cp in sent:
            cp.wait_send()

    outs = _pallas_call(
        forward_body, name=name_forward, in_specs=[ANY] * n, out_specs=[ANY] * n,
        out_shape=[jax.ShapeDtypeStruct(v.shape, v.dtype) for v in views],
        input_output_aliases={k: k for k in range(n)}, scratch_shapes=[pltpu.SemaphoreType.DMA((n, 3))] * 2,
    )(*landed)
    return [o.reshape(s) for o, s in zip(outs, shapes)]


def _chip_exchange_start(parts, name):
    n = len(parts)

    def body(*refs):
        ssem, rsem = refs[n:n + ns], refs[n + ns:n + 2 * ns]
        base = n + 2 * ns
        srcs, outs, token = refs[base:base + n], refs[base + n:base + 2 * n], refs[base + 2 * n]
        x, y, c, q, chips, qs = _position()
        for k in range(n):
            for j, chip in enumerate(chips):
                _remote(srcs[k].at[qs[j]], outs[k].at[j], ssem[3 * k + j], rsem[3 * k + j],
                        (chip[0], chip[1], c)).start()
        token[...] = jnp.zeros_like(token)

    ns = 3 * n
    outs = _pallas_call(
        body, name=name, in_specs=[ANY] * n, out_specs=[SEM] * (2 * ns) + [ANY] * (2 * n) + [TOKEN_SPEC],
        out_shape=[pltpu.SemaphoreType.DMA(())] * (2 * ns) + [jax.ShapeDtypeStruct(a.shape, a.dtype) for a in parts]
        + [jax.ShapeDtypeStruct((3,) + a.shape[1:], a.dtype) for a in parts] + [TOKEN],
        input_output_aliases={k: k + 2 * ns for k in range(n)}, compiler_params=_split_params(),
    )(*parts)
    b = 2 * ns
    return list(outs[:ns]), list(outs[ns:b]), list(outs[b:b + n]), list(outs[b + n:b + 2 * n]), outs[b + 2 * n]


def _chip_exchange_wait(ssem, rsem, parts, landing, after, name):
    n = len(parts)
    ns = len(ssem)

    def body(*refs):
        ssem_ref, rsem_ref = refs[2 * n:2 * n + ns], refs[2 * n + ns:2 * n + 2 * ns]
        base = 2 * n + 2 * ns + 1
        srcs, outs = refs[base:base + n], refs[base + n:]
        x, y, c, q, chips, qs = _position()
        for k in range(n):
            for j, chip in enumerate(chips):
                cp = _remote(srcs[k].at[qs[j]], outs[k].at[j], ssem_ref[3 * k + j], rsem_ref[3 * k + j],
                             (chip[0], chip[1], c))
                cp.wait_send()
                cp.wait_recv()

    outs = _pallas_call(
        body, name=name, in_specs=[ANY] * (2 * n) + [SEM] * (2 * ns) + [ANY], out_specs=[ANY] * (2 * n),
        out_shape=[jax.ShapeDtypeStruct(a.shape, a.dtype) for a in list(parts) + list(landing)],
        input_output_aliases={k: k for k in range(2 * n)}, compiler_params=_split_params(),
    )(*parts, *landing, *ssem, *rsem, after)
    return list(outs[:n]), list(outs[n:])


def _pair_swap(views, name):
    n = len(views)

    def body(*refs):
        ins, outs = refs[:n], refs[n:2 * n]
        ssem, rsem = refs[2 * n:]
        x, y, c, _, _, _ = _position()
        cps = [_remote(ins[k].at[pl.ds(0, N_CHIPS), 1 - c], outs[k], ssem.at[k], rsem.at[k], (x, y, 1 - c))
               for k in range(n)]
        for cp in cps:
            cp.start()
        for cp in cps:
            cp.wait()

    return _pallas_call(
        body, name=name, in_specs=[ANY] * n, out_specs=[ANY] * n,
        out_shape=[jax.ShapeDtypeStruct((v.shape[0],) + v.shape[2:], v.dtype) for v in views],
        scratch_shapes=[pltpu.SemaphoreType.DMA((n,))] * 2,
    )(*views)


def _pair_send(arrays, name):
    n = len(arrays)

    def body(*refs):
        ins, outs = refs[:n], refs[n:2 * n]
        ssem, rsem = refs[2 * n:]
        x, y, c, _, _, _ = _position()
        cps = [_remote(ins[k], outs[k], ssem.at[k], rsem.at[k], (x, y, 1 - c)) for k in range(n)]
        for cp in cps:
            cp.start()
        for cp in cps:
            cp.wait()

    return _pallas_call(
        body, name=name, in_specs=[ANY] * n, out_specs=[ANY] * n,
        out_shape=[jax.ShapeDtypeStruct(a.shape, a.dtype) for a in arrays],
        scratch_shapes=[pltpu.SemaphoreType.DMA((n,))] * 2,
    )(*arrays)


def _all_to_all_small(part):
    P = part.shape[0]

    def body(in_ref, out_ref, lsem, ssem, rsem):
        x, y, c = lax.axis_index("x"), lax.axis_index("y"), lax.axis_index("c")
        me = 4 * x + 2 * y + c
        flips = [(fx, fy, fc) for fx in (0, 1) for fy in (0, 1) for fc in (0, 1)][1:]
        peers = [((x + fx) % 2, (y + fy) % 2, (c + fc) % 2) for fx, fy, fc in flips]
        loc = pltpu.make_async_copy(in_ref, out_ref.at[me], lsem)
        loc.start()
        cps = [_remote(in_ref, out_ref.at[me], ssem.at[j], rsem.at[j], peer) for j, peer in enumerate(peers)]
        for cp in cps:
            cp.start()
        for j, (px, py, pc) in enumerate(peers):
            _remote(in_ref, out_ref.at[4 * px + 2 * py + pc], ssem.at[j], rsem.at[j], peers[j]).wait_recv()
        for cp in cps:
            cp.wait_send()
        loc.wait()

    return _pallas_call(
        body, name="small_exchange", in_specs=[ANY], out_specs=ANY,
        out_shape=jax.ShapeDtypeStruct((8, P, LANES), F32),
        scratch_shapes=[pltpu.SemaphoreType.DMA(())] + [pltpu.SemaphoreType.DMA((7,))] * 2,
    )(part)


def _small_peers():
    x, y, c = lax.axis_index("x"), lax.axis_index("y"), lax.axis_index("c")
    flips = [(fx, fy, fc) for fx in (0, 1) for fy in (0, 1) for fc in (0, 1)][1:]
    peers = [((x + fx) % 2, (y + fy) % 2, (c + fc) % 2) for fx, fy, fc in flips]
    return 4 * x + 2 * y + c, peers


def _all_to_all_small_start(part, name):
    P = part.shape[0]
    me = 4 * lax.axis_index("x") + 2 * lax.axis_index("y") + lax.axis_index("c")
    landing = lax.dynamic_update_slice(jnp.zeros((8, P, LANES), F32), part[None], (me, 0, 0))

    def body(*refs):
        sems, src, land, token = refs[2:16], refs[16], refs[17], refs[18]
        me_, peers = _small_peers()
        for j, peer in enumerate(peers):
            _remote(src, land.at[me_], sems[j], sems[7 + j], peer).start()
        token[...] = jnp.zeros_like(token)

    outs = _pallas_call(
        body, name=name, in_specs=[ANY, ANY], out_specs=[SEM] * 14 + [ANY, ANY, TOKEN_SPEC],
        out_shape=[pltpu.SemaphoreType.DMA(())] * 14 + [jax.ShapeDtypeStruct(part.shape, F32),
                                                       jax.ShapeDtypeStruct((8, P, LANES), F32), TOKEN],
        input_output_aliases={0: 14, 1: 15}, compiler_params=_split_params(),
    )(part, landing)
    return list(outs[:7]), list(outs[7:14]), outs[14], outs[15], outs[16]


def _all_to_all_small_wait(ssem, rsem, part, landing, after, name):
    def body(*refs):
        sems, src, land = refs[2:16], refs[17], refs[18]
        _, peers = _small_peers()
        for j, (px, py, pc) in enumerate(peers):
            cp = _remote(src, land.at[4 * px + 2 * py + pc], sems[j], sems[7 + j], peers[j])
            cp.wait_send()
            cp.wait_recv()

    return _pallas_call(
        body, name=name, in_specs=[ANY, ANY] + [SEM] * 14 + [ANY], out_specs=[ANY, ANY],
        out_shape=[jax.ShapeDtypeStruct(part.shape, F32), jax.ShapeDtypeStruct(landing.shape, F32)],
        input_output_aliases={0: 0, 1: 1}, compiler_params=_split_params(),
    )(part, landing, *ssem, *rsem, after)[1]


def _row_tile(rows, width, n_arrays):
    t = rows
    while t % 2 == 0 and t > 8 and 2 * n_arrays * t * width * 4 > VMEM_LIMIT // 2:
        t //= 2
    return t


def _chip():
    return 2 * lax.axis_index("x") + lax.axis_index("y")


def _core():
    return lax.axis_index("c")


def _cast_place(w3, layer, name):
    _, r, c = w3.shape
    tr = _row_tile(r, c, 2)

    def body(w_ref, o_ref):
        o_ref[...] = w_ref[...].astype(BF)

    return _pallas_call(
        body, name=name, grid=(r // tr,), in_specs=[pl.BlockSpec((None, tr, c), lambda i: (layer, i, 0))],
        out_specs=pl.BlockSpec((None, tr, c), lambda i: (_chip(), i, 0)),
        out_shape=jax.ShapeDtypeStruct((N_CHIPS, r, c), BF), compiler_params=_params(("parallel",)),
    )(w3)


def _pair_sum(views, recvs, name):
    n = len(views)

    def body(*refs):
        for g_ref, r_ref, o_ref in zip(refs[:n], refs[n:2 * n], refs[2 * n:]):
            o_ref[...] = (g_ref[...].astype(F32) + r_ref[...].astype(F32)).astype(BF)

    own = [pl.BlockSpec((None, None) + v.shape[2:], lambda p: (p, _core(), 0, 0)) for v in views]
    blk = [pl.BlockSpec((None,) + r.shape[1:], lambda p: (p, 0, 0)) for r in recvs]
    return _pallas_call(
        body, name=name, grid=(N_CHIPS,), in_specs=own + blk, out_specs=blk,
        out_shape=[jax.ShapeDtypeStruct(r.shape, BF) for r in recvs], compiler_params=_params(("parallel",)),
    )(*views, *recvs)


CHIP_SUM_STEPS = 2


def _chip_sum(parts, recvs, name):
    n = len(parts)

    def body(*refs):
        for p_ref, r_ref, o_ref in zip(refs[:n], refs[n:2 * n], refs[2 * n:]):
            acc = p_ref[...].astype(F32)
            for j in range(3):
                acc = acc + r_ref[j].astype(F32)
            o_ref[...] = acc

    rows = [p.shape[1] // CHIP_SUM_STEPS for p in parts]
    return _pallas_call(
        body, name=name, grid=(CHIP_SUM_STEPS,),
        in_specs=[pl.BlockSpec((None, t, p.shape[2]), lambda i: (_chip(), i, 0)) for p, t in zip(parts, rows)]
        + [pl.BlockSpec((3, t, p.shape[2]), lambda i: (0, i, 0)) for p, t in zip(parts, rows)],
        out_specs=[pl.BlockSpec((t, p.shape[2]), lambda i: (i, 0)) for p, t in zip(parts, rows)],
        out_shape=[jax.ShapeDtypeStruct(p.shape[1:], F32) for p in parts], compiler_params=_params(("parallel",)),
    )(*parts, *recvs)


def _sum_slices(a, name):
    n, rows, width = a.shape
    tr = _row_tile(rows, width, n + 1)

    def body(a_ref, o_ref):
        acc = a_ref[0].astype(F32)
        for i in range(1, n):
            acc = acc + a_ref[i].astype(F32)
        o_ref[...] = acc

    return _pallas_call(
        body, name=name, grid=(rows // tr,), in_specs=[pl.BlockSpec((n, tr, width), lambda i: (0, i, 0))],
        out_specs=pl.BlockSpec((tr, width), lambda i: (i, 0)), out_shape=jax.ShapeDtypeStruct((rows, width), F32),
        compiler_params=_params(("parallel",)),
    )(a)


def _adamw_update(w, g, m, v):
    nm = ADAM_B1 * m + (1.0 - ADAM_B1) * g
    nv = ADAM_B2 * v + (1.0 - ADAM_B2) * (g * g)
    m_hat = nm / (1.0 - ADAM_B1 ** ADAM_STEP)
    v_hat = nv / (1.0 - ADAM_B2 ** ADAM_STEP)
    return -ADAM_LR * (m_hat / (jnp.sqrt(v_hat) + ADAM_EPS) + ADAM_WD * w), nm, nv


def _adamw(ws, gs, ms, vs, name):
    n = len(ws)

    def body(*refs):
        for k in range(n):
            w_ref, g_ref, m_ref, v_ref = (refs[s * n + k] for s in range(4))
            d_ref, nm_ref, nv_ref = (refs[(4 + s) * n + k] for s in range(3))
            d_ref[...], nm_ref[...], nv_ref[...] = _adamw_update(w_ref[...], g_ref[...], m_ref[...], v_ref[...])

    whole = [pl.BlockSpec(w.shape, lambda i: (0, 0)) for w in ws]
    outs = _pallas_call(
        body, name=name, grid=(1,), in_specs=whole * 4, out_specs=whole * 3,
        out_shape=[jax.ShapeDtypeStruct(w.shape, F32) for _ in range(3) for w in ws],
        compiler_params=_params(("arbitrary",)),
    )(*ws, *gs, *ms, *vs)
    return [[outs[s * n + k] for s in range(3)] for k in range(n)]


ADAMW_STEPS = 4


def _adamw_halves(ws, ms, vs, mine, other, name):
    n = len(ws)
    depth = ws[0].shape[0]
    assert depth == 2
    halves = [(w.shape[1] // 2, w.shape[2]) for w in ws]
    tiles = [hr // ADAMW_STEPS for hr, _ in halves]
    kinds = ((0, True), (0, False), (1, True), (1, False))

    def active(l, h, layer, own):
        mine_half = h == _core()
        return (l == layer) & (mine_half if own else jnp.logical_not(mine_half))

    def body(*refs):
        l, h = pl.program_id(0), pl.program_id(1)
        flags = [active(l, h, layer, own) for layer, own in kinds]
        for k in range(n):
            w_ref, m_ref, v_ref = refs[k], refs[n + k], refs[2 * n + k]
            g_refs = [refs[(3 + s) * n + k] for s in range(4)]
            go_ref, d_ref, nm_ref, nv_ref = (refs[(7 + s) * n + k] for s in range(4))
            for flag, g_ref in zip(flags, g_refs):
                @pl.when(flag)
                def _():
                    gv = g_ref[...]
                    go_ref[...] = gv
                    d_ref[...], nm_ref[...], nv_ref[...] = _adamw_update(w_ref[...], gv, m_ref[...], v_ref[...])

    def blk(k):
        return pl.BlockSpec((None, None, tiles[k], halves[k][1]), lambda l, h, i: (l, h, i, 0))

    def gspec(k, layer, own):
        return pl.BlockSpec((tiles[k], halves[k][1]), lambda l, h, i: (jnp.where(active(l, h, layer, own), i, 0), 0))

    def view(a, k):
        return a.reshape(depth, 2, halves[k][0], halves[k][1])

    blks = [blk(k) for k in range(n)]
    sources = [[(mine if own else other)[layer][k] for k in range(n)] for layer, own in kinds]
    outs = _pallas_call(
        body, name=name, grid=(depth, 2, ADAMW_STEPS),
        in_specs=blks * 3 + [gspec(k, layer, own) for layer, own in kinds for k in range(n)], out_specs=blks * 4,
        out_shape=[jax.ShapeDtypeStruct((depth, 2) + halves[k], F32) for _ in range(4) for k in range(n)],
        compiler_params=_params(("parallel", "parallel", "parallel")),
    )(*[view(a, k) for group in (ws, ms, vs) for k, a in enumerate(group)], *[g for src in sources for g in src])
    return [[outs[s * n + k].reshape(ws[k].shape) for s in range(4)] for k in range(n)]


BIG = ("w_in", "w_pool_up", "w_conv_out", "w_attn_up", "w_o", "w_ff1", "w_ff2")
SMALL = ("norm_mix", "b_gate", "pool_mix", "pool_scale", "conv_w", "q_gain", "k_gain", "norm_mlp")
ORDER = ("norm_mix", "w_in", "b_gate", "pool_mix", "pool_scale", "conv_w", "q_gain", "k_gain", "w_pool_up",
         "w_conv_out", "w_attn_up", "w_o", "norm_mlp", "w_ff1", "w_ff2")
COLUMN_SHARDED = ("w_pool_up", "w_conv_out", "w_attn_up", "w_ff1")


def _matrix_weights(gathered):
    w = {}
    for name, g4 in gathered.items():
        if name in COLUMN_SHARDED:
            w[name] = g4
        else:
            w[name] = g4.reshape(N_CHIPS * g4.shape[1], g4.shape[2])
    return w


def _small_weights(l, small):
    w = {}
    w["norm_mix"] = small["norm_mix"][l][None]
    w["norm_mlp"] = small["norm_mlp"][l][None]
    w["b_gate"] = small["b_gate"][l][None]
    w["pool_mix"] = small["pool_mix"][l].astype(BF)
    w["pool_scale"] = small["pool_scale"][l][None]
    w["conv_w"] = jnp.pad(small["conv_w_full"][l], ((0, 5), (0, 0)))
    w["qk_gain"] = jnp.pad(jnp.stack([jnp.tile(small["q_gain"][l], 2), jnp.tile(small["k_gain"][l], 2)]), ((0, 6), (0, 0)))
    return w


def _to_chip_major(name, g):
    if name == "w_in":
        return g.T.reshape(N_CHIPS, g.shape[1] // N_CHIPS, g.shape[0])
    if name in COLUMN_SHARDED:
        return g
    return g.reshape(N_CHIPS, g.shape[0] // N_CHIPS, g.shape[1])


def _pad8(a):
    a = a.reshape(-1)
    return jnp.pad(a, (0, (-a.size) % (8 * LANES))).reshape(-1, LANES)


def kernel(x, norm_mix, w_in, b_gate, pool_mix, pool_scale, conv_w, q_gain, k_gain, w_pool_up, w_conv_out, w_attn_up, w_o, norm_mlp, w_ff1, w_ff2, loss_target, m_norm_mix, m_w_in, m_b_gate, m_pool_mix, m_pool_scale, m_conv_w, m_q_gain, m_k_gain, m_w_pool_up, m_w_conv_out, m_w_attn_up, m_w_o, m_norm_mlp, m_w_ff1, m_w_ff2, v_norm_mix, v_w_in, v_b_gate, v_pool_mix, v_pool_scale, v_conv_w, v_q_gain, v_k_gain, v_w_pool_up, v_w_conv_out, v_w_attn_up, v_w_o, v_norm_mlp, v_w_ff1, v_w_ff2):
    weights = dict(norm_mix=norm_mix, w_in=w_in, b_gate=b_gate, pool_mix=pool_mix, pool_scale=pool_scale, conv_w=conv_w,
                   q_gain=q_gain, k_gain=k_gain, w_pool_up=w_pool_up, w_conv_out=w_conv_out, w_attn_up=w_attn_up,
                   w_o=w_o, norm_mlp=norm_mlp, w_ff1=w_ff1, w_ff2=w_ff2)
    moms = dict(norm_mix=m_norm_mix, w_in=m_w_in, b_gate=m_b_gate, pool_mix=m_pool_mix, pool_scale=m_pool_scale,
                conv_w=m_conv_w, q_gain=m_q_gain, k_gain=m_k_gain, w_pool_up=m_w_pool_up, w_conv_out=m_w_conv_out,
                w_attn_up=m_w_attn_up, w_o=m_w_o, norm_mlp=m_norm_mlp, w_ff1=m_w_ff1, w_ff2=m_w_ff2)
    vels = dict(norm_mix=v_norm_mix, w_in=v_w_in, b_gate=v_b_gate, pool_mix=v_pool_mix, pool_scale=v_pool_scale,
                conv_w=v_conv_w, q_gain=v_q_gain, k_gain=v_k_gain, w_pool_up=v_w_pool_up, w_conv_out=v_w_conv_out,
                w_attn_up=v_w_attn_up, w_o=v_w_o, norm_mlp=v_norm_mlp, w_ff1=v_w_ff1, w_ff2=v_w_ff2)
    depth = norm_mix.shape[0]
    q = 2 * lax.axis_index("x") + lax.axis_index("y")
    for group in (weights, moms, vels):
        group["w_in"] = jnp.swapaxes(group["w_in"], 1, 2)

    assert depth == 2, "the second layer's gather hides behind the first layer's forward, and likewise backward"
    first, rest = BIG[:1], BIG[1:]
    cw_all = _all_to_all_small(_pad8(conv_w))
    bufs = [{n: _cast_place(weights[n], 0, f"cast_{n}_l0") for n in first}]
    a_ssem, a_rsem, a_views, a_token = _gather_start([bufs[0][n] for n in first], "gather_start_l0_in", cw_all)
    bufs[0].update({n: _cast_place(weights[n], 0, f"cast_{n}_l0") for n in rest})
    bufs += [{n: _cast_place(weights[n], l, f"cast_{n}_l{l}") for n in BIG} for l in range(1, depth)]
    b_ssem, b_rsem, b_views, b_token = _gather_start([bufs[0][n] for n in rest], "gather_start_l0_rest", a_token)
    g_ssem, g_rsem, g_views, g_token = _gather_start([bufs[1][n] for n in BIG], "gather_start_l1", b_token)
    conv_w_full = jnp.concatenate(
        [cw_all[2 * p].reshape(-1)[:conv_w.size].reshape(conv_w.shape) for p in range(N_CHIPS)], axis=-1)
    small = dict(weights)
    small["conv_w_full"] = conv_w_full

    def soon_weights(t):
        got = _gather_finish(a_ssem, a_rsem, a_views, t, "gather_wait_l0_in", "gather_forward_l0_in",
                             [bufs[0][n].shape for n in first])
        return _matrix_weights(dict(zip(first, got)))

    def late_weights(t):
        got = _gather_finish(b_ssem, b_rsem, b_views, t, "gather_wait_l0_rest", "gather_forward_l0_rest",
                             [bufs[0][n].shape for n in rest])
        return _matrix_weights(dict(zip(rest, got)))

    wl, saved = [None] * depth, [None] * depth
    small_1 = _small_weights(1, small)
    (h, hb_1), saved[0], wl[0] = _layer_fwd(x[0], _small_weights(0, small), "l0", after=g_token, soon=soon_weights,
                                            late=late_weights, next_gain=small_1["norm_mix"])
    got = _gather_finish(g_ssem, g_rsem, g_views, h, "gather_wait_l1", "gather_forward_l1",
                         [bufs[1][n].shape for n in BIG])
    (dh, loss_row), saved[1], wl[1] = _layer_fwd(
        h, dict(small_1, **_matrix_weights(dict(zip(BIG, got)))), "l1", target=loss_target[0], hb=hb_1)

    def pair_stage(names, g, tag):
        views = [_halves(_to_chip_major(n, g[n])) for n in names]
        from_sibling = _pair_swap(views, f"grad_pair_swap_{tag}")
        return _pair_sum(views, from_sibling, f"pair_sum_{tag}")

    mine, other = [{}, {}], [{}, {}]

    def finish(names, l, started, after, tag):
        ssem, rsem, parts, landing, _ = started
        parts, arrived = _chip_exchange_wait(ssem, rsem, parts, landing, after, f"grad_chip_exchange_wait_{tag}")
        got = _chip_sum(parts, arrived, f"chip_sum_{tag}")
        mine[l].update(zip(names, got))
        other[l].update(zip(names, _pair_send(got, f"grad_pair_send_{tag}")))

    def small_pieces(g):
        return [_pad8(g[n][:3] if n == "conv_w" else g[n]) for n in SMALL]

    def start_small(l):
        return _all_to_all_small_start(jnp.concatenate(small_pieces(grads[l]), axis=0), f"small_grad_exchange_start_l{l}")

    grads, early, small = [None] * depth, {}, [None] * depth
    dh, grads[1] = _layer_bwd(dh, wl[1], saved[1], "l1")
    second = _chip_exchange_start(pair_stage(BIG, grads[1], "l1"), "grad_chip_exchange_start_l1")
    small[1] = start_small(1)

    def start_rest(g):
        early["rest"] = _chip_exchange_start(pair_stage(rest, g, "l0_rest"), "grad_chip_exchange_start_l0_rest")
        return early["rest"][4]

    def start_last(g):
        early["in"] = _chip_exchange_start(pair_stage(first, g, "l0_in"), "grad_chip_exchange_start_l0_in")
        return early["in"][4]

    dh, grads[0] = _layer_bwd(dh, wl[0], saved[0], "l0", after=[second[4], small[1][4]], mid=start_rest,
                              tail=start_last)
    small[0] = start_small(0)
    finish(BIG, 1, second, dh, "l1")
    finish(rest, 0, early["rest"], dh, "l0_rest")
    loss = lax.psum(loss_row[0, 0], ("x", "y", "c"))
    full = {}

    deltas, new_m, new_v = {}, {}, {}

    def update_matrices(names, tag):
        results = _adamw_halves(
            [weights[n] for n in names], [moms[n] for n in names], [vels[n] for n in names],
            [[mine[l][n] for n in names] for l in range(depth)], [[other[l][n] for n in names] for l in range(depth)],
            f"adamw_{tag}")
        for n, (g_, d_, m_, v_) in zip(names, results):
            full[n], deltas[n], new_m[n], new_v[n] = g_, d_, m_, v_

    update_matrices(rest, "rest")
    finish(first, 0, early["in"], deltas[rest[-1]], "l0_in")
    update_matrices(first, "in")
    summed = []
    for l in range(depth):
        ssem, rsem, part, landing, _ = small[l]
        summed.append(_sum_slices(_all_to_all_small_wait(ssem, rsem, part, landing, deltas[first[-1]],
                                                         f"small_grad_exchange_wait_l{l}"), f"small_sum_l{l}"))
    row = 0
    for n, piece in zip(SMALL, small_pieces(grads[0])):
        size = (weights[n].size if n != "conv_w" else depth * 3 * 512) // depth
        flat = jnp.stack([s[row:row + piece.shape[0]].reshape(-1)[:size] for s in summed])
        row += piece.shape[0]
        if n == "conv_w":
            full[n] = lax.dynamic_slice_in_dim(flat.reshape(depth, 3, 512), q * conv_w.shape[2], conv_w.shape[2], axis=2)
        else:
            full[n] = flat.reshape(weights[n].shape)
    two_d = {n: (-1, weights[n].shape[-1]) if n not in ("conv_w", "q_gain", "k_gain") else (1, -1) for n in SMALL}
    results = _adamw(*[[group[n].reshape(two_d[n]) for n in SMALL] for group in (weights, full, moms, vels)],
                     "adamw_small")
    for n, (d2, m2, v2) in zip(SMALL, results):
        shape = weights[n].shape
        deltas[n], new_m[n], new_v[n] = d2.reshape(shape), m2.reshape(shape), v2.reshape(shape)
        full[n] = full[n].reshape(shape)
    for group in (full, deltas, new_m, new_v):
        group["w_in"] = jnp.swapaxes(group["w_in"], 1, 2)
    return (loss, dh[None], *[full[n] for n in ORDER], *[deltas[n] for n in ORDER], *[new_m[n] for n in ORDER],
            *[new_v[n] for n in ORDER])
```

```python
import functools

import jax
import jax.numpy as jnp
from jax import lax
from jax.experimental import pallas as pl
from jax.experimental.pallas import tpu as pltpu

F32 = jnp.float32
BF = jnp.bfloat16
MESH_ID = pl.DeviceIdType.MESH
ANY = pl.BlockSpec(memory_space=pl.ANY)

EPS = 1e-6
MASK_VALUE = -1e30
POOL_WINDOWS = (2, 4, 8, 16)
ATTN_DILATIONS = (1, 4, 16)
ATTN_BLOCK = 128
HEAD_DIM = 64
OFF_Q, OFF_K, OFF_V, OFF_GATE = 2048, 2816, 3584, 4352
N_CHIPS = 4
ADAM_LR, ADAM_B1, ADAM_B2, ADAM_EPS, ADAM_WD, ADAM_STEP = 0.001, 0.9, 0.999, 1e-08, 0.01, 10

VMEM_LIMIT = 48 * 1024 * 1024
LANES = 128

_DIMS = {"nn": (((1,), (0,)), ((), ())), "nt": (((1,), (1,)), ((), ())), "tn": (((0,), (0,)), ((), ()))}


def _params(sem):
    return pltpu.CompilerParams(dimension_semantics=sem, vmem_limit_bytes=VMEM_LIMIT)


def _pallas_call(body, **kw):
    def in_hbm(s):
        pin = isinstance(s, jax.ShapeDtypeStruct) and s is not TOKEN and jnp.issubdtype(s.dtype, jnp.floating)
        return pltpu.HBM(s.shape, s.dtype) if pin else s

    out_shape = kw.pop("out_shape")
    kw["out_shape"] = [in_hbm(s) for s in out_shape] if isinstance(out_shape, (list, tuple)) else in_hbm(out_shape)
    call = pl.pallas_call(body, **kw)

    def run(*args):
        pinned = [pltpu.with_memory_space_constraint(a, pltpu.HBM)
                  if hasattr(a, "dtype") and jnp.issubdtype(a.dtype, jnp.floating) else a for a in args]
        return call(*pinned)

    return run


def _dot(a, b, mode="nn"):
    return lax.dot_general(a, b, _DIMS[mode], preferred_element_type=F32)


def _mm(a, b, mode, name, *, tm, tn, tk, out_dtype=F32, res=None, aux=None, epi=None, n_outer=False,
        b_shards=False, out_shards=False, after=None, vec=None):
    if mode == "tn":
        K, M = a.shape
    else:
        M, K = a.shape
    if b_shards:
        if mode == "nn":
            assert b.shape[1] == K
            N = b.shape[2] * N_CHIPS
        else:
            assert mode == "nt"
            N = b.shape[1]
            assert b.shape[2] * N_CHIPS == K
    else:
        N = b.shape[0] if mode == "nt" else b.shape[1]
    tm, tn, tk = min(tm, M), min(tn, N), min(tk, K)
    assert M % tm == 0 and N % tn == 0 and K % tk == 0
    nk = K // tk
    if n_outer:
        grid = (N // tn, M // tm, nk)
        ij = lambda p, q_: (q_, p)
    else:
        grid = (M // tm, N // tn, nk)
        ij = lambda p, q_: (p, q_)

    def amap(p, q_, k):
        i, j = ij(p, q_)
        return (k, i) if mode == "tn" else (i, k)

    a_spec = pl.BlockSpec((tk, tm) if mode == "tn" else (tm, tk), amap)
    if b_shards:
        if mode == "nn":
            per = (N // N_CHIPS) // tn
            assert per >= 1 and (N // N_CHIPS) % tn == 0

            def bmap(p, q_, k):
                i, j = ij(p, q_)
                return (j // per, k, j % per)

            b_spec = pl.BlockSpec((None, tk, tn), bmap)
        else:
            per = (K // N_CHIPS) // tk
            assert per >= 1 and (K // N_CHIPS) % tk == 0

            def bmap(p, q_, k):
                i, j = ij(p, q_)
                return (k // per, j, k % per)

            b_spec = pl.BlockSpec((None, tn, tk), bmap)
    else:
        def bmap(p, q_, k):
            i, j = ij(p, q_)
            return (j, k) if mode == "nt" else (k, j)

        b_spec = pl.BlockSpec((tn, tk) if mode == "nt" else (tk, tn), bmap)

    def omap(p, q_, k):
        return ij(p, q_)

    o_spec = pl.BlockSpec((tm, tn), omap)
    if out_shards:
        per_o = (N // N_CHIPS) // tn
        assert per_o >= 1 and (N // N_CHIPS) % tn == 0

        def osmap(p, q_, k):
            i, j = ij(p, q_)
            return (j // per_o, i, j % per_o)

        out_spec0 = pl.BlockSpec((None, tm, tn), osmap)
        out_shape0 = jax.ShapeDtypeStruct((N_CHIPS, M, N // N_CHIPS), out_dtype)
    else:
        out_spec0 = o_spec
        out_shape0 = jax.ShapeDtypeStruct((M, N), out_dtype)

    in_specs = [a_spec, b_spec]
    args = [a, b]
    if res is not None:
        in_specs.append(o_spec)
        args.append(res)
    if aux is not None:
        in_specs.append(o_spec)
        args.append(aux)
    if vec is not None:
        in_specs.append(pl.BlockSpec((1, tn), lambda p, q_, k: (0, ij(p, q_)[1])))
        args.append(vec)
    after = [] if after is None else list(after) if isinstance(after, (list, tuple)) else [after]
    in_specs += [ANY] * len(after)
    args += after
    out_specs = [out_spec0]
    out_shape = [out_shape0]
    reduces = epi in ("loss", "rms_bwd")
    if reduces:
        assert tn == N and not n_outer and not out_shards
        width = LANES if epi == "loss" else N
        out_specs.append(pl.BlockSpec((1, width), lambda p, q_, k: (0, 0)))
        out_shape.append(jax.ShapeDtypeStruct((1, width), F32))
    if epi == "rms_next":
        assert tn == N and not out_shards
        out_specs.append(o_spec)
        out_shape.append(jax.ShapeDtypeStruct((M, N), BF))
    n_out = len(out_shape)
    has_res, has_aux, has_vec, n_after = res is not None, aux is not None, vec is not None, len(after)

    def body(*refs):
        a_ref, b_ref = refs[0], refs[1]
        pos = 2
        res_ref = aux_ref = vec_ref = None
        if has_res:
            res_ref = refs[pos]
            pos += 1
        if has_aux:
            aux_ref = refs[pos]
            pos += 1
        if has_vec:
            vec_ref = refs[pos]
            pos += 1
        pos += n_after
        outs = refs[pos:pos + n_out]
        part = _dot(a_ref[...].astype(BF), b_ref[...].astype(BF), mode)

        first_row_tile = pl.program_id(0) == 0

        def add_to_sum(row):
            @pl.when(first_row_tile)
            def _():
                outs[1][...] = jnp.zeros_like(outs[1])

            outs[1][...] += row

        def finish(acc):
            if epi == "rms_bwd":
                xv = aux_ref[...]
                r = lax.rsqrt(jnp.mean(xv * xv, axis=-1, keepdims=True) + EPS)
                xhat = xv * r
                dy = acc * vec_ref[...]
                outs[0][...] = res_ref[...] + r * (dy - xhat * jnp.mean(dy * xhat, axis=-1, keepdims=True))
                add_to_sum(jnp.sum(acc * xhat, axis=0, keepdims=True))
                return
            if res_ref is not None:
                acc = res_ref[...] + acc
            if epi == "relu2":
                r = jnp.maximum(acc, 0.0)
                outs[0][...] = (r * r).astype(out_dtype)
            elif epi == "drelu2":
                outs[0][...] = (acc.astype(BF) * (2.0 * jnp.sqrt(aux_ref[...]))).astype(out_dtype)
            elif epi == "rms_next":
                outs[0][...] = acc
                r = lax.rsqrt(jnp.mean(acc * acc, axis=-1, keepdims=True) + EPS)
                outs[1][...] = ((acc * r) * vec_ref[...]).astype(BF)
            elif epi == "loss":
                e = acc - aux_ref[...]
                outs[0][...] = e / float(N)
                add_to_sum(0.5 * jnp.sum(jnp.mean(e * e, axis=-1, keepdims=True)))
            else:
                outs[0][...] = acc.astype(out_dtype)

        if nk == 1:
            finish(part)
        else:
            acc_ref = refs[pos + n_out]
            k = pl.program_id(2)

            @pl.when(k == 0)
            def _():
                acc_ref[...] = part

            @pl.when(k > 0)
            def _():
                acc_ref[...] += part

            @pl.when(k == nk - 1)
            def _():
                finish(acc_ref[...])

    scratch = [pltpu.VMEM((tm, tn), F32)] if nk > 1 else []
    out = _pallas_call(
        body, name=name, grid=grid, in_specs=in_specs, out_specs=out_specs, out_shape=out_shape,
        scratch_shapes=scratch,
        compiler_params=_params(("arbitrary" if reduces else "parallel", "parallel", "arbitrary")),
    )(*args)
    return out if n_out > 1 else out[0]


def _rms_fwd(x, gain, name, after=None):
    T, D = x.shape
    tm = min(512, T)

    def body(x_ref, g_ref, *rest):
        o_ref = rest[-1]
        xv = x_ref[...]
        r = lax.rsqrt(jnp.mean(xv * xv, axis=-1, keepdims=True) + EPS)
        o_ref[...] = ((xv * r) * g_ref[...]).astype(BF)

    extra = [] if after is None else list(after) if isinstance(after, (list, tuple)) else [after]
    return _pallas_call(
        body, name=name, grid=(T // tm,),
        in_specs=[pl.BlockSpec((tm, D), lambda i: (i, 0)), pl.BlockSpec((1, D), lambda i: (0, 0))] + [ANY] * len(extra),
        out_specs=pl.BlockSpec((tm, D), lambda i: (i, 0)), out_shape=jax.ShapeDtypeStruct((T, D), BF),
        compiler_params=_params(("parallel",)),
    )(x, gain, *extra)


def _rms_bwd(dh, x, gain, dres, name):
    T, D = x.shape
    tm = min(512, T)

    def body(dh_ref, x_ref, g_ref, dres_ref, dx_ref, dg_ref):
        xv = x_ref[...]
        r = lax.rsqrt(jnp.mean(xv * xv, axis=-1, keepdims=True) + EPS)
        xhat = xv * r
        dhv = dh_ref[...]
        dy = dhv * g_ref[...]
        dx_ref[...] = dres_ref[...] + r * (dy - xhat * jnp.mean(dy * xhat, axis=-1, keepdims=True))

        @pl.when(pl.program_id(0) == 0)
        def _():
            dg_ref[...] = jnp.zeros_like(dg_ref)

        dg_ref[...] += jnp.sum(dhv * xhat, axis=0, keepdims=True)

    row = pl.BlockSpec((tm, D), lambda i: (i, 0))
    vec = pl.BlockSpec((1, D), lambda i: (0, 0))
    return _pallas_call(
        body, name=name, grid=(T // tm,), in_specs=[row, row, vec, row], out_specs=[row, vec],
        out_shape=[jax.ShapeDtypeStruct((T, D), F32), jax.ShapeDtypeStruct((1, D), F32)],
        compiler_params=_params(("arbitrary",)),
    )(dh, x, gain, dres)


POOL_HALO = 16
CONV_HALO = 8
POOLCONV_ROWS = 512


def _causal_window_sum(v, w):
    s, sh = v, 1
    while sh < w:
        s = s + pltpu.roll(s, sh, 0)
        sh *= 2
    return s


def _anticausal_window_sum(v, w):
    n = v.shape[0]
    s, sh = v, 1
    while sh < w:
        s = s + pltpu.roll(s, n - sh, 0)
        sh *= 2
    return s


def _poolconv_fwd(z, pmix_b, pscale, convw, name):
    T = z.shape[0]
    R = min(POOLCONV_ROWS, T)
    PH, CH = R // POOL_HALO, R // CONV_HALO

    def body(u_ref, uh_ref, b_ref, c_ref, ch_ref, x_ref, xh_ref, mix_ref, sc_ref, cw_ref, yp_ref, yc_ref):
        i = pl.program_id(0)
        keep = (i > 0).astype(F32)
        row = i * R + lax.broadcasted_iota(jnp.int32, (R, 1), 0)
        w_all = jnp.concatenate([uh_ref[...] * keep, u_ref[...]], axis=0)
        for g, w in enumerate(POOL_WINDOWS):
            cols = slice(128 * g, 128 * (g + 1))
            wg = w_all[:, cols]
            s = _causal_window_sum(wg, w)[POOL_HALO:]
            inv_cnt = 1.0 / jnp.minimum(row + 1, w).astype(F32)
            dgrp = s * inv_cnt - wg[POOL_HALO:]
            y = _dot(dgrp.astype(BF), mix_ref[g]) * sc_ref[:, cols]
            yp_ref[:, cols] = y.astype(BF)
        uc = jnp.concatenate([ch_ref[...] * xh_ref[...] * keep, c_ref[...] * x_ref[...]], axis=0)
        yc = cw_ref[2:3, :] * uc + cw_ref[0:1, :] * pltpu.roll(uc, 2, 0) + cw_ref[1:2, :] * pltpu.roll(uc, 1, 0)
        yc_ref[...] = (b_ref[...] * yc[CONV_HALO:]).astype(BF)

    def main(cb):
        return pl.BlockSpec((R, 512), lambda i: (i, cb))

    def prev(cb, halo, per):
        return pl.BlockSpec((halo, 512), lambda i: (jnp.maximum(i * per - 1, 0), cb))

    full = lambda a: pl.BlockSpec(a.shape, lambda i: (0,) * a.ndim)
    return _pallas_call(
        body, name=name, grid=(T // R,),
        in_specs=[main(0), prev(0, POOL_HALO, PH), main(1), main(2), prev(2, CONV_HALO, CH), main(3),
                  prev(3, CONV_HALO, CH), full(pmix_b), full(pscale), full(convw)],
        out_specs=[pl.BlockSpec((R, 512), lambda i: (i, 0))] * 2,
        out_shape=[jax.ShapeDtypeStruct((T, 512), BF)] * 2,
        compiler_params=_params(("parallel",)),
    )(z, z, z, z, z, z, z, pmix_b, pscale, convw)


def _poolconv_bwd(z, dyp, dyc, pmix_b, pscale, convw, dz, name):
    T = z.shape[0]
    R = min(POOLCONV_ROWS, T)
    PH, CH = R // POOL_HALO, R // CONV_HALO
    nsteps = T // R

    def body(u_ref, uh_ref, b_ref, bn_ref, c_ref, ch_ref, x_ref, xh_ref, dyp_ref, dypn_ref, dyc_ref, dycn_ref,
             mix_ref, sc_ref, cw_ref, dz_in_ref, dz_ref, dmix_ref, dsc_ref, dcw_ref):
        i = pl.program_id(0)
        keep_prev = (i > 0).astype(F32)
        keep_next = (i < nsteps - 1).astype(F32)

        @pl.when(i == 0)
        def _():
            dmix_ref[...] = jnp.zeros_like(dmix_ref)
            dsc_ref[...] = jnp.zeros_like(dsc_ref)
            dcw_ref[...] = jnp.zeros_like(dcw_ref)

        row = i * R + lax.broadcasted_iota(jnp.int32, (R, 1), 0)
        row_ext = i * R + lax.broadcasted_iota(jnp.int32, (R + POOL_HALO, 1), 0)
        w_all = jnp.concatenate([uh_ref[...] * keep_prev, u_ref[...]], axis=0)
        dyp_ext = jnp.concatenate([dyp_ref[...], dypn_ref[...] * keep_next], axis=0)
        for g, w in enumerate(POOL_WINDOWS):
            cols = slice(128 * g, 128 * (g + 1))
            wg = w_all[:, cols]
            s = _causal_window_sum(wg, w)[POOL_HALO:]
            inv_cnt = 1.0 / jnp.minimum(row + 1, w).astype(F32)
            dgrp = (s * inv_cnt - wg[POOL_HALO:]).astype(BF)
            y_pre = _dot(dgrp, mix_ref[g])
            dsc_ref[:, cols] += jnp.sum(dyp_ref[:, cols] * y_pre, axis=0, keepdims=True)
            dyb = (dyp_ext[:, cols] * sc_ref[:, cols]).astype(BF)
            dmix_ref[cols, :] += _dot(dgrp, dyb[:R], "tn")
            dd = _dot(dyb, mix_ref[g], "nt")
            inv_cnt_ext = 1.0 / jnp.minimum(row_ext + 1, w).astype(F32)
            e = _anticausal_window_sum(dd * inv_cnt_ext, w)
            dz_ref[:, cols] = (e[:R] - dd[:R]).astype(BF)
        cw0, cw1, cw2 = cw_ref[0:1, :], cw_ref[1:2, :], cw_ref[2:3, :]
        uc = jnp.concatenate([ch_ref[...] * xh_ref[...] * keep_prev, c_ref[...] * x_ref[...]], axis=0)
        uc1 = pltpu.roll(uc, 1, 0)[CONV_HALO:]
        uc2 = pltpu.roll(uc, 2, 0)[CONV_HALO:]
        uc0 = uc[CONV_HALO:]
        yc = cw2 * uc0 + cw0 * uc2 + cw1 * uc1
        dycv = dyc_ref[...]
        dz_ref[:, 512:1024] = (dycv * yc).astype(BF)
        dv_ext = jnp.concatenate([dycv * b_ref[...], dycn_ref[...] * bn_ref[...] * keep_next], axis=0)
        n_ext = R + CONV_HALO
        duc = (cw2 * dv_ext + cw1 * pltpu.roll(dv_ext, n_ext - 1, 0) + cw0 * pltpu.roll(dv_ext, n_ext - 2, 0))[:R]
        dv = dv_ext[:R]
        dcw_ref[0:1, :] += jnp.sum(dv * uc2, axis=0, keepdims=True)
        dcw_ref[1:2, :] += jnp.sum(dv * uc1, axis=0, keepdims=True)
        dcw_ref[2:3, :] += jnp.sum(dv * uc0, axis=0, keepdims=True)
        dz_ref[:, 1024:1536] = (duc * x_ref[...]).astype(BF)
        dz_ref[:, 1536:2048] = (duc * c_ref[...]).astype(BF)

    def main(cb):
        return pl.BlockSpec((R, 512), lambda i: (i, cb))

    def prev(cb, halo, per):
        return pl.BlockSpec((halo, 512), lambda i: (jnp.maximum(i * per - 1, 0), cb))

    def nxt(cb, halo, per):
        return pl.BlockSpec((halo, 512), lambda i: (jnp.minimum((i + 1) * per, T // halo - 1), cb))

    full = lambda a: pl.BlockSpec(a.shape, lambda i: (0,) * a.ndim)
    return _pallas_call(
        body, name=name, grid=(nsteps,),
        in_specs=[main(0), prev(0, POOL_HALO, PH), main(1), nxt(1, CONV_HALO, CH), main(2), prev(2, CONV_HALO, CH),
                  main(3), prev(3, CONV_HALO, CH), main(0), nxt(0, POOL_HALO, PH), main(0), nxt(0, CONV_HALO, CH),
                  full(pmix_b), full(pscale), full(convw), ANY],
        out_specs=[pl.BlockSpec((R, 2048), lambda i: (i, 0)), pl.BlockSpec((512, 128), lambda i: (0, 0)),
                   pl.BlockSpec((1, 512), lambda i: (0, 0)), pl.BlockSpec((8, 512), lambda i: (0, 0))],
        out_shape=[jax.ShapeDtypeStruct(dz.shape, BF), jax.ShapeDtypeStruct((512, 128), F32),
                   jax.ShapeDtypeStruct((1, 512), F32), jax.ShapeDtypeStruct((8, 512), F32)],
        input_output_aliases={15: 0}, compiler_params=_params(("arbitrary",)),
    )(z, z, z, z, z, z, z, z, dyp, dyp, dyc, dyc, pmix_b, pscale, convw, dz)


def _head_sums(v):
    row = lax.broadcasted_iota(jnp.int32, (LANES, LANES), 0) < HEAD_DIM
    col = lax.broadcasted_iota(jnp.int32, (LANES, LANES), 1) < HEAD_DIM
    same_head = jnp.where(jnp.logical_xor(row, col), 0.0, 1.0).astype(BF)
    hi = v.astype(BF)
    lo = (v - hi.astype(F32)).astype(BF)
    return _dot(hi, same_head) + _dot(lo, same_head)


def _head_norm(x, g2, ma):
    r = lax.rsqrt(_head_sums(x * x) / HEAD_DIM + EPS)
    return x * r, r


def _head_norm_bwd(dy, xhat, r, g2, ma):
    dxh = dy * g2
    return r * (dxh - xhat * (_head_sums(dxh * xhat) / HEAD_DIM))


def _attn_masks(other_block_exists):
    lane = lax.broadcasted_iota(jnp.int32, (2 * ATTN_BLOCK, ATTN_BLOCK), 1)
    qi = lax.broadcasted_iota(jnp.int32, (2 * ATTN_BLOCK, ATTN_BLOCK), 0) & (ATTN_BLOCK - 1)
    never = (1 - other_block_exists.astype(jnp.int32)) * (2 * ATTN_BLOCK)
    return lane[:ATTN_BLOCK] < HEAD_DIM, lane <= qi, lane >= qi + never


def _stack_heads(x, ma):
    return jnp.concatenate([jnp.where(ma, x, 0.0), jnp.where(ma, 0.0, x)], axis=0)


def _unstack_heads(y, ma):
    return jnp.where(ma, y[:ATTN_BLOCK], y[ATTN_BLOCK:])


def _stack_cols(tile, ma):
    return jnp.concatenate([tile[:, 0:1], tile[:, HEAD_DIM:HEAD_DIM + 1]], axis=0)


QKV_TILES = (OFF_GATE - OFF_Q) // LANES
KIND_TILES = QKV_TILES // 3


def _qk_norm(z, gains, name):
    T = z.shape[0]
    tm = min(512, T)

    def body(x_ref, g_ref, o_ref):
        ma = lax.broadcasted_iota(jnp.int32, (tm, LANES), 1) < HEAD_DIM
        for tile in range(QKV_TILES):
            v = x_ref[:, LANES * tile:LANES * (tile + 1)]
            if tile < 2 * KIND_TILES:
                g = g_ref[0:1, :] if tile < KIND_TILES else g_ref[1:2, :]
                v = _head_norm(v, g, ma)[0] * g
            o_ref[tile] = v

    return _pallas_call(
        body, name=name, grid=(T // tm,),
        in_specs=[pl.BlockSpec((pl.Element(tm), pl.Element(OFF_GATE - OFF_Q)), lambda i: (i * tm, OFF_Q)),
                  pl.BlockSpec((8, LANES), lambda i: (0, 0))],
        out_specs=pl.BlockSpec((QKV_TILES, tm, LANES), lambda i: (0, i, 0)),
        out_shape=jax.ShapeDtypeStruct((QKV_TILES, T, LANES), F32), compiler_params=_params(("parallel",)),
    )(z, gains)


ATTN_STEP_ROWS = 2048
ATTN_UNROLL = 4


def _attn_steps(T):
    assert ATTN_STEP_ROWS == ATTN_BLOCK * max(ATTN_DILATIONS) and T % ATTN_STEP_ROWS == 0
    return T // ATTN_STEP_ROWS


def _attn_rows(jj, r, sub, d):
    start = jj * sub + r
    if d == 1:
        return pl.ds(pl.multiple_of(start, ATTN_BLOCK), ATTN_BLOCK)
    return pl.ds(start, ATTN_BLOCK, stride=d)


def _pick(flag, a, b):
    return jnp.where(jnp.full(a.shape, flag.astype(jnp.int32)) > 0, a, b)


def _attn_fwd(qkv, name):
    T = qkv.shape[1]
    nbig = _attn_steps(T)
    scale = HEAD_DIM ** -0.5

    def body(q_ref, kc_ref, kp_ref, vc_ref, vp_ref, o_ref, lse_ref):
        jb = pl.program_id(1)
        for gi, d in enumerate(ATTN_DILATIONS):
            pl.when(pl.program_id(0) == gi)(functools.partial(group, d, jb, q_ref, kc_ref, kp_ref, vc_ref, vp_ref,
                                                              o_ref, lse_ref))

    def group(d, jb, q_ref, kc_ref, kp_ref, vc_ref, vp_ref, o_ref, lse_ref):
        sub, m = ATTN_BLOCK * d, ATTN_STEP_ROWS // (ATTN_BLOCK * d)

        def step(s, carry):
            jj, r = s // d, s % d
            here, before = _attn_rows(jj, r, sub, d), _attn_rows(jnp.maximum(jj - 1, 0), r, sub, d)
            edge = _attn_rows(m - 1, r, sub, d)
            first = jj == 0
            ma, mask_c, mask_p = _attn_masks(jb * m + jj > 0)
            qs = _stack_heads(q_ref[here, :], ma).astype(BF)
            kcb = kc_ref[here, :].astype(BF)
            kpb = _pick(first, kp_ref[edge, :], kc_ref[before, :]).astype(BF)
            vcb = vc_ref[here, :].astype(BF)
            vpb = _pick(first, vp_ref[edge, :], vc_ref[before, :]).astype(BF)
            s_c = jnp.where(mask_c, _dot(qs, kcb, "nt") * scale, MASK_VALUE)
            s_p = jnp.where(mask_p, _dot(qs, kpb, "nt") * scale, MASK_VALUE)
            mx = jnp.maximum(jnp.max(s_c, axis=-1, keepdims=True), jnp.max(s_p, axis=-1, keepdims=True))
            p_c = jnp.exp(s_c - mx)
            p_p = jnp.exp(s_p - mx)
            den = jnp.sum(p_c, axis=-1, keepdims=True) + jnp.sum(p_p, axis=-1, keepdims=True)
            o = (_dot(p_c.astype(BF), vcb) + _dot(p_p.astype(BF), vpb)) / den
            o_ref[here, :] = _unstack_heads(o, ma)
            lse_ref[here, :] = _unstack_heads(jnp.broadcast_to(mx + jnp.log(den), o.shape), ma)
            return carry

        lax.fori_loop(0, m * d, step, 0, unroll=ATTN_UNROLL)

    def cur(kind):
        return pl.BlockSpec((None, ATTN_STEP_ROWS, LANES), lambda g, j, t: (KIND_TILES * kind + 2 * g + t, j, 0))

    def prv(kind):
        return pl.BlockSpec((None, ATTN_STEP_ROWS, LANES),
                            lambda g, j, t: (KIND_TILES * kind + 2 * g + t, jnp.maximum(j - 1, 0), 0))

    out = pl.BlockSpec((ATTN_STEP_ROWS, LANES), lambda g, j, t: (j, 2 * g + t))
    width = 2 * LANES * len(ATTN_DILATIONS)
    return _pallas_call(
        body, name=name, grid=(len(ATTN_DILATIONS), nbig, 2), in_specs=[cur(0), cur(1), prv(1), cur(2), prv(2)],
        out_specs=[out, out], out_shape=[jax.ShapeDtypeStruct((T, width), F32)] * 2,
        compiler_params=_params(("parallel", "parallel", "parallel")),
    )(qkv, qkv, qkv, qkv, qkv)


def _attn_bwd(z, qkv, do, c, lse, gains, name, after=None):
    T = z.shape[0]
    nbig = _attn_steps(T)
    scale = HEAD_DIM ** -0.5
    extra = [] if after is None else [after]

    def body(*refs):
        g, jb = pl.program_id(0), pl.program_id(1)
        dgq_ref, dgk_ref = refs[len(refs) - 5], refs[len(refs) - 4]

        @pl.when((g == 0) & (jb == 0) & (pl.program_id(2) == 0))
        def _():
            dgq_ref[...] = jnp.zeros_like(dgq_ref)
            dgk_ref[...] = jnp.zeros_like(dgk_ref)

        for gi, d in enumerate(ATTN_DILATIONS):
            pl.when(g == gi)(functools.partial(group, d, jb, *refs))

    def group(d, jb, qr_ref, kr_ref, vc_ref, vp_ref, qn_ref, qnn_ref, kn_ref, knp_ref, do_ref, don_ref, c_ref, cn_ref,
              lse_ref, lsen_ref, g_ref, *rest):
        dq_ref, dk_ref, dv_ref, dgq_ref, dgk_ref, sq_ref, sk_ref, sv_ref = rest[len(extra):]
        sub, m = ATTN_BLOCK * d, ATTN_STEP_ROWS // (ATTN_BLOCK * d)
        nb = T // sub
        gq, gk = g_ref[0:1, :], g_ref[1:2, :]

        def step(s, carry):
            jj, r = s // d, s % d
            here = _attn_rows(jj, r, sub, d)
            before = _attn_rows(jnp.maximum(jj - 1, 0), r, sub, d)
            behind = _attn_rows(jnp.minimum(jj + 1, m - 1), r, sub, d)
            edge_before, edge_behind = _attn_rows(m - 1, r, sub, d), _attn_rows(0, r, sub, d)
            first, last = jj == 0, jj == m - 1
            block = jb * m + jj
            ma, mask_c, mask_p = _attn_masks(block > 0)
            mask_n = _attn_masks(block < nb - 1)[2]
            qhat, rq = _head_norm(qr_ref[here, :], gq, ma)
            qn = qn_ref[here, :]
            qn_next = _pick(last, qnn_ref[edge_behind, :], qn_ref[behind, :])
            khat, rk = _head_norm(kr_ref[here, :], gk, ma)
            kcb = kn_ref[here, :].astype(BF)
            kpb = _pick(first, knp_ref[edge_before, :], kn_ref[before, :]).astype(BF)
            vcb = vc_ref[here, :].astype(BF)
            vpb = _pick(first, vp_ref[edge_before, :], vc_ref[before, :]).astype(BF)
            do_t, don_t = do_ref[here, :], _pick(last, don_ref[edge_behind, :], do_ref[behind, :])
            c_t, cn_t = c_ref[here, :], _pick(last, cn_ref[edge_behind, :], c_ref[behind, :])
            lse_t, lsen_t = lse_ref[here, :], _pick(last, lsen_ref[edge_behind, :], lse_ref[behind, :])
            qs, dos = _stack_heads(qn, ma).astype(BF), _stack_heads(do_t, ma).astype(BF)
            lse_s, c_s = _stack_cols(lse_t, ma), _stack_cols(c_t, ma)
            s_c = jnp.where(mask_c, _dot(qs, kcb, "nt") * scale, MASK_VALUE)
            s_p = jnp.where(mask_p, _dot(qs, kpb, "nt") * scale, MASK_VALUE)
            p_c = jnp.exp(s_c - lse_s)
            p_p = jnp.exp(s_p - lse_s)
            ds_c = ((p_c * (_dot(dos, vcb, "nt") + c_s)) * scale).astype(BF)
            ds_p = ((p_p * (_dot(dos, vpb, "nt") + c_s)) * scale).astype(BF)
            dq_t = _unstack_heads(_dot(ds_c, kcb) + _dot(ds_p, kpb), ma)
            qs_n, dos_n = _stack_heads(qn_next, ma).astype(BF), _stack_heads(don_t, ma).astype(BF)
            s_n = jnp.where(mask_n, _dot(qs_n, kcb, "nt") * scale, MASK_VALUE)
            p_n = jnp.exp(s_n - _stack_cols(lsen_t, ma))
            ds_n = ((p_n * (_dot(dos_n, vcb, "nt") + _stack_cols(cn_t, ma))) * scale).astype(BF)
            dv_t = _dot(p_c.astype(BF), dos, "tn") + _dot(p_n.astype(BF), dos_n, "tn")
            dk_t = _dot(ds_c, qs, "tn") + _dot(ds_n, qs_n, "tn")
            sq_ref[here, :] = _head_norm_bwd(dq_t, qhat, rq, gq, ma)
            sk_ref[here, :] = _head_norm_bwd(dk_t, khat, rk, gk, ma)
            sv_ref[here, :] = dv_t
            dgq_ref[...] += jnp.sum(dq_t * qhat, axis=0, keepdims=True)
            dgk_ref[...] += jnp.sum(dk_t * khat, axis=0, keepdims=True)
            return carry

        lax.fori_loop(0, m * d, step, 0, unroll=ATTN_UNROLL)
        dq_ref[...] = sq_ref[...].astype(BF)
        dk_ref[...] = sk_ref[...].astype(BF)
        dv_ref[...] = sv_ref[...].astype(BF)

    rows = ATTN_STEP_ROWS

    def raw(col0):
        return pl.BlockSpec((rows, LANES), lambda g, j, t: (j, col0 + 2 * g + t))

    def cur(kind):
        return pl.BlockSpec((None, rows, LANES), lambda g, j, t: (KIND_TILES * kind + 2 * g + t, j, 0))

    def prv(kind):
        return pl.BlockSpec((None, rows, LANES), lambda g, j, t: (KIND_TILES * kind + 2 * g + t, jnp.maximum(j - 1, 0), 0))

    def nxt(kind):
        return pl.BlockSpec((None, rows, LANES),
                            lambda g, j, t: (KIND_TILES * kind + 2 * g + t, jnp.minimum(j + 1, nbig - 1), 0))

    own = pl.BlockSpec((rows, LANES), lambda g, j, t: (j, 2 * g + t))
    own_next = pl.BlockSpec((rows, LANES), lambda g, j, t: (jnp.minimum(j + 1, nbig - 1), 2 * g + t))
    vec = pl.BlockSpec((1, LANES), lambda g, j, t: (0, 0))
    width = 2 * LANES * len(ATTN_DILATIONS)
    return _pallas_call(
        body, name=name, grid=(len(ATTN_DILATIONS), nbig, 2),
        in_specs=[raw(OFF_Q // LANES), raw(OFF_K // LANES), cur(2), prv(2), cur(0), nxt(0), cur(1), prv(1), own, own_next,
                  own, own_next, own, own_next, pl.BlockSpec((8, LANES), lambda g, j, t: (0, 0))] + [ANY] * len(extra),
        out_specs=[own, own, own, vec, vec],
        out_shape=[jax.ShapeDtypeStruct((T, width), BF)] * 3 + [jax.ShapeDtypeStruct((1, LANES), F32)] * 2,
        scratch_shapes=[pltpu.VMEM((rows, LANES), F32)] * 3,
        compiler_params=_params(("arbitrary", "arbitrary", "arbitrary")),
    )(z, z, qkv, qkv, qkv, qkv, qkv, qkv, do, do, c, c, lse, lse, gains, *extra)


MERGE_ROWS = 256
GATE_TILE = 256


def _group_mix(o_refs, lse_refs):
    lses = [r[...] for r in lse_refs]
    m = jnp.maximum(jnp.maximum(lses[0], lses[1]), lses[2])
    es = [jnp.exp(l - m) for l in lses]
    den = es[0] + es[1] + es[2]
    ws = [e / den for e in es]
    y = ws[0] * o_refs[0][...] + ws[1] * o_refs[1][...] + ws[2] * o_refs[2][...]
    return ws, y


def _sigmoid(v):
    return 1.0 / (1.0 + jnp.exp(-v))


def _merge_specs(T, z, bgate, gpu, gco, gau):
    tm = min(MERGE_ROWS, T)
    row = lambda w: pl.BlockSpec((tm, w), lambda i: (i, 0))
    gate0 = OFF_GATE // GATE_TILE
    gates = [pl.BlockSpec((tm, GATE_TILE), functools.partial(lambda i, cb: (i, cb), cb=gate0 + n))
             for n in range(3 * N_CHIPS)]
    full = lambda a: pl.BlockSpec(a.shape, lambda i: (0,) * a.ndim)
    by_group = [pl.BlockSpec((tm, 256), functools.partial(lambda i, g: (i, g), g=g)) for g in range(3)]
    specs = [row(512), row(512)] + by_group * 2 + gates + [full(bgate), full(gpu), full(gco), full(gau)]
    return tm, row, specs


def _merge_fwd(yp, yc, o3, lse3, z, bgate, gpu, gco, gau, name):
    T = yp.shape[0]
    tm, row, specs = _merge_specs(T, z, bgate, gpu, gco, gau)

    def body(*refs):
        yp_ref, yc_ref = refs[0], refs[1]
        o_refs, lse_refs = refs[2:5], refs[5:8]
        zg = refs[8:20]
        b_ref, gpu_ref, gco_ref, gau_ref, out_ref = refs[20:25]
        yab = _group_mix(o_refs, lse_refs)[1].astype(BF)
        ys = (yp_ref[...], yc_ref[...], yab)
        ups = (gpu_ref, gco_ref, gau_ref)
        for n in range(N_CHIPS):
            acc = None
            for b in range(3):
                gcol = slice(1024 * b + GATE_TILE * n, 1024 * b + GATE_TILE * (n + 1))
                gate = _sigmoid(zg[N_CHIPS * b + n][...] + b_ref[:, gcol])
                term = gate * _dot(ys[b], ups[b][n])
                acc = term if acc is None else acc + term
            out_ref[:, GATE_TILE * n:GATE_TILE * (n + 1)] = acc.astype(BF)

    return _pallas_call(
        body, name=name, grid=(T // tm,), in_specs=specs, out_specs=row(1024),
        out_shape=jax.ShapeDtypeStruct((T, 1024), BF), compiler_params=_params(("parallel",)),
    )(yp, yc, *([o3] * 3), *([lse3] * 3), *([z] * 12), bgate, gpu, gco, gau)


def _merge_bwd(dm, yp, yc, o3, lse3, z, bgate, gpu, gco, gau, name):
    T = yp.shape[0]
    tm, row, specs = _merge_specs(T, z, bgate, gpu, gco, gau)
    nsteps = T // tm

    def body(*refs):
        dm_ref, yp_ref, yc_ref = refs[0:3]
        o_refs, lse_refs = refs[3:6], refs[6:9]
        zg = refs[9:21]
        b_ref, gpu_ref, gco_ref, gau_ref = refs[21:25]
        dzg_ref, dyp_ref, dyc_ref = refs[25:28]
        do_ref, c_ref = refs[28:30]
        dgpu_ref, dgco_ref, dgau_ref, dbg_ref = refs[30:34]
        accs = refs[34:37]
        i = pl.program_id(0)

        @pl.when(i == 0)
        def _():
            for a in accs:
                a[...] = jnp.zeros_like(a)
            dbg_ref[...] = jnp.zeros_like(dbg_ref)

        ws, y = _group_mix(o_refs, lse_refs)
        ys = (yp_ref[...], yc_ref[...], y.astype(BF))
        ups = (gpu_ref, gco_ref, gau_ref)
        dys = [None, None, None]
        for n in range(N_CHIPS):
            dmn = dm_ref[:, GATE_TILE * n:GATE_TILE * (n + 1)]
            for b in range(3):
                gcol = slice(1024 * b + GATE_TILE * n, 1024 * b + GATE_TILE * (n + 1))
                gate = _sigmoid(zg[N_CHIPS * b + n][...] + b_ref[:, gcol])
                up = _dot(ys[b], ups[b][n])
                dzg = (dmn * up) * (gate * (1.0 - gate))
                dzg_ref[:, gcol] = dzg.astype(BF)
                dbg_ref[:, gcol] += jnp.sum(dzg, axis=0, keepdims=True)
                dup = (dmn * gate).astype(BF)
                accs[b][n] += _dot(ys[b], dup, "tn")
                dyb = _dot(dup, ups[b][n], "nt")
                dys[b] = dyb if dys[b] is None else dys[b] + dyb
        dyp_ref[...] = dys[0]
        dyc_ref[...] = dys[1]
        dya = dys[2]
        lane = lax.broadcasted_iota(jnp.int32, dya.shape, 1) // HEAD_DIM
        pr = dya * y
        rho = jnp.zeros_like(pr)
        for h in range(256 // HEAD_DIM):
            hm = lane == h
            rho = jnp.where(hm, jnp.sum(jnp.where(hm, pr, 0.0), axis=-1, keepdims=True), rho)
        for g in range(3):
            do_ref[:, 256 * g:256 * (g + 1)] = ws[g] * dya
            c_ref[:, 256 * g:256 * (g + 1)] = -(ws[g] * rho)

        @pl.when(i == nsteps - 1)
        def _():
            dgpu_ref[...] = accs[0][...].astype(BF)
            dgco_ref[...] = accs[1][...].astype(BF)
            dgau_ref[...] = accs[2][...].astype(BF)

    full = lambda a: pl.BlockSpec(a.shape, lambda i: (0,) * a.ndim)
    dz_gate = pl.BlockSpec((pl.Element(tm), pl.Element(3072)), lambda i: (i * tm, OFF_GATE))
    out_specs = ([dz_gate, row(512), row(512)] + [row(768)] * 2 + [full(gpu), full(gco), full(gau)]
                 + [pl.BlockSpec((1, 3072), lambda i: (0, 0))])
    out_shape = ([jax.ShapeDtypeStruct(z.shape, BF)] + [jax.ShapeDtypeStruct((T, 512), F32)] * 2
                 + [jax.ShapeDtypeStruct((T, 768), F32)] * 2
                 + [jax.ShapeDtypeStruct(g.shape, BF) for g in (gpu, gco, gau)]
                 + [jax.ShapeDtypeStruct((1, 3072), F32)])
    return _pallas_call(
        body, name=name, grid=(nsteps,), in_specs=[row(1024)] + specs, out_specs=out_specs, out_shape=out_shape,
        scratch_shapes=[pltpu.VMEM(g.shape, F32) for g in (gpu, gco, gau)],
        compiler_params=_params(("arbitrary",)),
    )(dm, yp, yc, *([o3] * 3), *([lse3] * 3), *([z] * 12), bgate, gpu, gco, gau)


def _layer_fwd(x, w, tag, after=None, soon=None, late=None, target=None, hb=None, next_gain=None):
    if hb is None:
        hb = _rms_fwd(x, w["norm_mix"], f"rms_mix_{tag}", after=after)
    if soon is not None:
        w = dict(w, **soon(hb))
    z = _mm(hb, w["w_in"], "nt", f"in_proj_{tag}", tm=512, tn=3712, tk=1024, n_outer=True)
    yp, yc = _poolconv_fwd(z, w["pool_mix"], w["pool_scale"], w["conv_w"], f"poolconv_{tag}")
    qkv = _qk_norm(z, w["qk_gain"], f"qk_norm_{tag}")
    o3, lse3 = _attn_fwd(qkv, f"attn_{tag}")
    if late is not None:
        w = dict(w, **late(lse3))
    merged = _merge_fwd(yp, yc, o3, lse3, z, w["b_gate"], w["w_pool_up"], w["w_conv_out"], w["w_attn_up"],
                        f"merge_{tag}")
    x1, h2b = _mm(merged, w["w_o"], "nn", f"out_proj_{tag}", tm=1024, tn=1024, tk=1024, res=x, vec=w["norm_mlp"],
                  epi="rms_next")
    rb = _mm(h2b, w["w_ff1"], "nn", f"ff1_{tag}", tm=1024, tn=1024, tk=1024, out_dtype=BF, epi="relu2", n_outer=True,
             b_shards=True)
    if target is not None:
        x2 = _mm(rb, w["w_ff2"], "nn", f"ff2_{tag}", tm=512, tn=1024, tk=4096, res=x1, aux=target, epi="loss")
    elif next_gain is not None:
        x2 = _mm(rb, w["w_ff2"], "nn", f"ff2_{tag}", tm=512, tn=1024, tk=4096, res=x1, vec=next_gain, epi="rms_next")
    else:
        x2 = _mm(rb, w["w_ff2"], "nn", f"ff2_{tag}", tm=512, tn=1024, tk=4096, res=x1)
    saved = dict(x=x, hb=hb, z=z, yp=yp, yc=yc, qkv=qkv, o3=o3, lse3=lse3, merged=merged, x1=x1, h2b=h2b, rb=rb)
    return x2, saved, w


def _layer_bwd(dx2, w, s, tag, after=None, mid=None, tail=None):
    g = {}
    dab = _mm(dx2, w["w_ff2"], "nt", f"d_ff2_act_{tag}", tm=1024, tn=1024, tk=1024, out_dtype=BF, aux=s["rb"],
              epi="drelu2", after=after)
    g["w_ff2"] = _mm(s["rb"], dx2, "tn", f"d_ff2_w_{tag}", tm=1024, tn=1024, tk=2048, out_dtype=BF)
    g["w_ff1"] = _mm(s["h2b"], dab, "tn", f"d_ff1_w_{tag}", tm=1024, tn=1024, tk=2048, out_dtype=BF, out_shards=True)
    dx1, g["norm_mlp"] = _mm(dab, w["w_ff1"], "nt", f"d_ff1_act_{tag}", tm=1024, tn=1024, tk=1024, b_shards=True,
                             res=dx2, aux=s["x1"], vec=w["norm_mlp"], epi="rms_bwd")
    dm = _mm(dx1, w["w_o"], "nt", f"d_out_act_{tag}", tm=1024, tn=1024, tk=1024)
    g["w_o"] = _mm(s["merged"], dx1, "tn", f"d_out_w_{tag}", tm=1024, tn=1024, tk=1024, out_dtype=BF)
    (dz, dyp, dyc, do3, c3, g["w_pool_up"], g["w_conv_out"], g["w_attn_up"],
     g["b_gate"]) = _merge_bwd(dm, s["yp"], s["yc"], s["o3"], s["lse3"], s["z"], w["b_gate"], w["w_pool_up"],
                               w["w_conv_out"], w["w_attn_up"], f"d_merge_{tag}")
    behind = mid(g) if mid is not None else None
    dzq, dzk, dzv, dgq, dgk = _attn_bwd(s["z"], s["qkv"], do3, c3, s["lse3"], w["qk_gain"], f"d_attn_{tag}",
                                        after=behind)
    g["q_gain"] = dgq[:, :HEAD_DIM] + dgq[:, HEAD_DIM:]
    g["k_gain"] = dgk[:, :HEAD_DIM] + dgk[:, HEAD_DIM:]
    for off, piece in ((OFF_Q, dzq), (OFF_K, dzk), (OFF_V, dzv)):
        dz = lax.dynamic_update_slice(dz, piece, (0, off))
    dz, g["pool_mix"], g["pool_scale"], g["conv_w"] = _poolconv_bwd(
        s["z"], dyp, dyc, w["pool_mix"], w["pool_scale"], w["conv_w"], dz, f"d_poolconv_{tag}")
    g["w_in"] = _mm(s["hb"], dz, "tn", f"d_in_w_{tag}", tm=512, tn=3712, tk=1024, out_dtype=BF)
    dh = _mm(dz, w["w_in"], "nn", f"d_in_act_{tag}", tm=1024, tn=1024, tk=3712,
             after=tail(g) if tail is not None else None)
    dx, g["norm_mix"] = _rms_bwd(dh, s["x"], w["norm_mix"], dx1, f"d_rms_mix_{tag}")
    return dx, g


def _position():
    x, y, c = lax.axis_index("x"), lax.axis_index("y"), lax.axis_index("c")
    chips = [(1 - x, y), (x, 1 - y), (1 - x, 1 - y)]
    return x, y, c, 2 * x + y, chips, [2 * cx + cy for cx, cy in chips]


def _remote(src, dst, ssem, rsem, dev):
    return pltpu.make_async_remote_copy(src_ref=src, dst_ref=dst, send_sem=ssem, recv_sem=rsem, device_id=dev,
                                        device_id_type=MESH_ID)


def _halves(a):
    return a.reshape(a.shape[0], 2, a.shape[1] // 2, a.shape[2])


SEM = pl.BlockSpec(memory_space=pltpu.SEMAPHORE)
TOKEN = jax.ShapeDtypeStruct((8, LANES), F32)
TOKEN_SPEC = pl.BlockSpec(memory_space=pltpu.VMEM)


def _split_params():
    return pltpu.CompilerParams(has_side_effects=pltpu.SideEffectType.DATAFLOW_SIDE_EFFECTING)


def _gather_start(bufs, name, after):
    n = len(bufs)
    views = [_halves(b) for b in bufs]

    def body(*refs):
        first_sem = n + 1
        ssem, rsem = refs[first_sem:first_sem + ns], refs[first_sem + ns:first_sem + 2 * ns]
        outs, token = refs[first_sem + 2 * ns:first_sem + 2 * ns + n], refs[first_sem + 2 * ns + n]
        x, y, c, q, chips, qs = _position()
        for k in range(n):
            mine = outs[k].at[q, c]
            for j, chip in enumerate(chips):
                _remote(mine, mine, ssem[3 * k + j], rsem[3 * k + j], (chip[0], chip[1], c)).start()
        token[...] = jnp.zeros_like(token)

    ns = 3 * n
    outs = _pallas_call(
        body, name=name, in_specs=[ANY] * (n + 1), out_specs=[SEM] * (2 * ns) + [ANY] * n + [TOKEN_SPEC],
        out_shape=[pltpu.SemaphoreType.DMA(())] * (2 * ns) + [jax.ShapeDtypeStruct(v.shape, v.dtype) for v in views]
        + [TOKEN],
        input_output_aliases={k: k + 2 * ns for k in range(n)}, compiler_params=_split_params(),
    )(*views, after)
    return list(outs[:ns]), list(outs[ns:2 * ns]), list(outs[2 * ns:2 * ns + n]), outs[2 * ns + n]


def _gather_finish(ssem, rsem, views, after, name_wait, name_forward, shapes):
    n = len(views)
    ns = len(ssem)

    def wait_body(*refs):
        ssem_ref, rsem_ref = refs[n:n + ns], refs[n + ns:n + 2 * ns]
        outs = refs[n + 2 * ns + 1:]
        x, y, c, q, chips, qs = _position()
        for k in range(n):
            for j, chip in enumerate(chips):
                cp = _remote(outs[k].at[q, c], outs[k].at[qs[j], c], ssem_ref[3 * k + j], rsem_ref[3 * k + j],
                             (chip[0], chip[1], c))
                cp.wait_send()
                cp.wait_recv()

    landed = _pallas_call(
        wait_body, name=name_wait, in_specs=[ANY] * n + [SEM] * (2 * ns) + [ANY], out_specs=[ANY] * n,
        out_shape=[jax.ShapeDtypeStruct(v.shape, v.dtype) for v in views],
        input_output_aliases={k: k for k in range(n)}, compiler_params=_split_params(),
    )(*views, *ssem, *rsem, after)

    def forward_body(*refs):
        outs = refs[n:2 * n]
        fssem, frsem = refs[2 * n:]
        x, y, c, q, chips, qs = _position()
        sib = (x, y, 1 - c)
        sent = []
        for k in range(n):
            for j in range(3):
                slot = outs[k].at[qs[j], c]
                cp = _remote(slot, slot, fssem.at[k, j], frsem.at[k, j], sib)
                cp.start()
                sent.append(cp)
        for k in range(n):
            for j in range(3):
                slot = outs[k].at[qs[j], 1 - c]
                _remote(slot, slot, fssem.at[k, j], frsem.at[k, j], sib).wait_recv()
        for cp in sent:
            cp.wait_send()

    outs = _pallas_call(
        forward_body, name=name_forward, in_specs=[ANY] * n, out_specs=[ANY] * n,
        out_shape=[jax.ShapeDtypeStruct(v.shape, v.dtype) for v in views],
        input_output_aliases={k: k for k in range(n)}, scratch_shapes=[pltpu.SemaphoreType.DMA((n, 3))] * 2,
    )(*landed)
    return [o.reshape(s) for o, s in zip(outs, shapes)]


def _chip_exchange_start(parts, name):
    n = len(parts)

    def body(*refs):
        ssem, rsem = refs[n:n + ns], refs[n + ns:n + 2 * ns]
        base = n + 2 * ns
        srcs, outs, token = refs[base:base + n], refs[base + n:base + 2 * n], refs[base + 2 * n]
        x, y, c, q, chips, qs = _position()
        for k in range(n):
            for j, chip in enumerate(chips):
                _remote(srcs[k].at[qs[j]], outs[k].at[j], ssem[3 * k + j], rsem[3 * k + j],
                        (chip[0], chip[1], c)).start()
        token[...] = jnp.zeros_like(token)

    ns = 3 * n
    outs = _pallas_call(
        body, name=name, in_specs=[ANY] * n, out_specs=[SEM] * (2 * ns) + [ANY] * (2 * n) + [TOKEN_SPEC],
        out_shape=[pltpu.SemaphoreType.DMA(())] * (2 * ns) + [jax.ShapeDtypeStruct(a.shape, a.dtype) for a in parts]
        + [jax.ShapeDtypeStruct((3,) + a.shape[1:], a.dtype) for a in parts] + [TOKEN],
        input_output_aliases={k: k + 2 * ns for k in range(n)}, compiler_params=_split_params(),
    )(*parts)
    b = 2 * ns
    return list(outs[:ns]), list(outs[ns:b]), list(outs[b:b + n]), list(outs[b + n:b + 2 * n]), outs[b + 2 * n]


def _chip_exchange_wait(ssem, rsem, parts, landing, after, name):
    n = len(parts)
    ns = len(ssem)

    def body(*refs):
        ssem_ref, rsem_ref = refs[2 * n:2 * n + ns], refs[2 * n + ns:2 * n + 2 * ns]
        base = 2 * n + 2 * ns + 1
        srcs, outs = refs[base:base + n], refs[base + n:]
        x, y, c, q, chips, qs = _position()
        for k in range(n):
            for j, chip in enumerate(chips):
                cp = _remote(srcs[k].at[qs[j]], outs[k].at[j], ssem_ref[3 * k + j], rsem_ref[3 * k + j],
                             (chip[0], chip[1], c))
                cp.wait_send()
                cp.wait_recv()

    outs = _pallas_call(
        body, name=name, in_specs=[ANY] * (2 * n) + [SEM] * (2 * ns) + [ANY], out_specs=[ANY] * (2 * n),
        out_shape=[jax.ShapeDtypeStruct(a.shape, a.dtype) for a in list(parts) + list(landing)],
        input_output_aliases={k: k for k in range(2 * n)}, compiler_params=_split_params(),
    )(*parts, *landing, *ssem, *rsem, after)
    return list(outs[:n]), list(outs[n:])


def _pair_swap(views, name):
    n = len(views)

    def body(*refs):
        ins, outs = refs[:n], refs[n:2 * n]
        ssem, rsem = refs[2 * n:]
        x, y, c, _, _, _ = _position()
        cps = [_remote(ins[k].at[pl.ds(0, N_CHIPS), 1 - c], outs[k], ssem.at[k], rsem.at[k], (x, y, 1 - c))
               for k in range(n)]
        for cp in cps:
            cp.start()
        for cp in cps:
            cp.wait()

    return _pallas_call(
        body, name=name, in_specs=[ANY] * n, out_specs=[ANY] * n,
        out_shape=[jax.ShapeDtypeStruct((v.shape[0],) + v.shape[2:], v.dtype) for v in views],
        scratch_shapes=[pltpu.SemaphoreType.DMA((n,))] * 2,
    )(*views)


def _pair_send(arrays, name):
    n = len(arrays)

    def body(*refs):
        ins, outs = refs[:n], refs[n:2 * n]
        ssem, rsem = refs[2 * n:]
        x, y, c, _, _, _ = _position()
        cps = [_remote(ins[k], outs[k], ssem.at[k], rsem.at[k], (x, y, 1 - c)) for k in range(n)]
        for cp in cps:
            cp.start()
        for cp in cps:
            cp.wait()

    return _pallas_call(
        body, name=name, in_specs=[ANY] * n, out_specs=[ANY] * n,
        out_shape=[jax.ShapeDtypeStruct(a.shape, a.dtype) for a in arrays],
        scratch_shapes=[pltpu.SemaphoreType.DMA((n,))] * 2,
    )(*arrays)


def _all_to_all_small(part):
    P = part.shape[0]

    def body(in_ref, out_ref, lsem, ssem, rsem):
        x, y, c = lax.axis_index("x"), lax.axis_index("y"), lax.axis_index("c")
        me = 4 * x + 2 * y + c
        flips = [(fx, fy, fc) for fx in (0, 1) for fy in (0, 1) for fc in (0, 1)][1:]
        peers = [((x + fx) % 2, (y + fy) % 2, (c + fc) % 2) for fx, fy, fc in flips]
        loc = pltpu.make_async_copy(in_ref, out_ref.at[me], lsem)
        loc.start()
        cps = [_remote(in_ref, out_ref.at[me], ssem.at[j], rsem.at[j], peer) for j, peer in enumerate(peers)]
        for cp in cps:
            cp.start()
        for j, (px, py, pc) in enumerate(peers):
            _remote(in_ref, out_ref.at[4 * px + 2 * py + pc], ssem.at[j], rsem.at[j], peers[j]).wait_recv()
        for cp in cps:
            cp.wait_send()
        loc.wait()

    return _pallas_call(
        body, name="small_exchange", in_specs=[ANY], out_specs=ANY,
        out_shape=jax.ShapeDtypeStruct((8, P, LANES), F32),
        scratch_shapes=[pltpu.SemaphoreType.DMA(())] + [pltpu.SemaphoreType.DMA((7,))] * 2,
    )(part)


def _small_peers():
    x, y, c = lax.axis_index("x"), lax.axis_index("y"), lax.axis_index("c")
    flips = [(fx, fy, fc) for fx in (0, 1) for fy in (0, 1) for fc in (0, 1)][1:]
    peers = [((x + fx) % 2, (y + fy) % 2, (c + fc) % 2) for fx, fy, fc in flips]
    return 4 * x + 2 * y + c, peers


def _all_to_all_small_start(part, name):
    P = part.shape[0]
    me = 4 * lax.axis_index("x") + 2 * lax.axis_index("y") + lax.axis_index("c")
    landing = lax.dynamic_update_slice(jnp.zeros((8, P, LANES), F32), part[None], (me, 0, 0))

    def body(*refs):
        sems, src, land, token = refs[2:16], refs[16], refs[17], refs[18]
        me_, peers = _small_peers()
        for j, peer in enumerate(peers):
            _remote(src, land.at[me_], sems[j], sems[7 + j], peer).start()
        token[...] = jnp.zeros_like(token)

    outs = _pallas_call(
        body, name=name, in_specs=[ANY, ANY], out_specs=[SEM] * 14 + [ANY, ANY, TOKEN_SPEC],
        out_shape=[pltpu.SemaphoreType.DMA(())] * 14 + [jax.ShapeDtypeStruct(part.shape, F32),
                                                       jax.ShapeDtypeStruct((8, P, LANES), F32), TOKEN],
        input_output_aliases={0: 14, 1: 15}, compiler_params=_split_params(),
    )(part, landing)
    return list(outs[:7]), list(outs[7:14]), outs[14], outs[15], outs[16]


def _all_to_all_small_wait(ssem, rsem, part, landing, after, name):
    def body(*refs):
        sems, src, land = refs[2:16], refs[17], refs[18]
        _, peers = _small_peers()
        for j, (px, py, pc) in enumerate(peers):
            cp = _remote(src, land.at[4 * px + 2 * py + pc], sems[j], sems[7 + j], peers[j])
            cp.wait_send()
            cp.wait_recv()

    return _pallas_call(
        body, name=name, in_specs=[ANY, ANY] + [SEM] * 14 + [ANY], out_specs=[ANY, ANY],
        out_shape=[jax.ShapeDtypeStruct(part.shape, F32), jax.ShapeDtypeStruct(landing.shape, F32)],
        input_output_aliases={0: 0, 1: 1}, compiler_params=_split_params(),
    )(part, landing, *ssem, *rsem, after)[1]


def _row_tile(rows, width, n_arrays):
    t = rows
    while t % 2 == 0 and t > 8 and 2 * n_arrays * t * width * 4 > VMEM_LIMIT // 2:
        t //= 2
    return t


def _chip():
    return 2 * lax.axis_index("x") + lax.axis_index("y")


def _core():
    return lax.axis_index("c")


def _cast_place(w3, layer, name):
    _, r, c = w3.shape
    tr = _row_tile(r, c, 2)

    def body(w_ref, o_ref):
        o_ref[...] = w_ref[...].astype(BF)

    return _pallas_call(
        body, name=name, grid=(r // tr,), in_specs=[pl.BlockSpec((None, tr, c), lambda i: (layer, i, 0))],
        out_specs=pl.BlockSpec((None, tr, c), lambda i: (_chip(), i, 0)),
        out_shape=jax.ShapeDtypeStruct((N_CHIPS, r, c), BF), compiler_params=_params(("parallel",)),
    )(w3)


def _pair_sum(views, recvs, name):
    n = len(views)

    def body(*refs):
        for g_ref, r_ref, o_ref in zip(refs[:n], refs[n:2 * n], refs[2 * n:]):
            o_ref[...] = (g_ref[...].astype(F32) + r_ref[...].astype(F32)).astype(BF)

    own = [pl.BlockSpec((None, None) + v.shape[2:], lambda p: (p, _core(), 0, 0)) for v in views]
    blk = [pl.BlockSpec((None,) + r.shape[1:], lambda p: (p, 0, 0)) for r in recvs]
    return _pallas_call(
        body, name=name, grid=(N_CHIPS,), in_specs=own + blk, out_specs=blk,
        out_shape=[jax.ShapeDtypeStruct(r.shape, BF) for r in recvs], compiler_params=_params(("parallel",)),
    )(*views, *recvs)


CHIP_SUM_STEPS = 2


def _chip_sum(parts, recvs, name):
    n = len(parts)

    def body(*refs):
        for p_ref, r_ref, o_ref in zip(refs[:n], refs[n:2 * n], refs[2 * n:]):
            acc = p_ref[...].astype(F32)
            for j in range(3):
                acc = acc + r_ref[j].astype(F32)
            o_ref[...] = acc

    rows = [p.shape[1] // CHIP_SUM_STEPS for p in parts]
    return _pallas_call(
        body, name=name, grid=(CHIP_SUM_STEPS,),
        in_specs=[pl.BlockSpec((None, t, p.shape[2]), lambda i: (_chip(), i, 0)) for p, t in zip(parts, rows)]
        + [pl.BlockSpec((3, t, p.shape[2]), lambda i: (0, i, 0)) for p, t in zip(parts, rows)],
        out_specs=[pl.BlockSpec((t, p.shape[2]), lambda i: (i, 0)) for p, t in zip(parts, rows)],
        out_shape=[jax.ShapeDtypeStruct(p.shape[1:], F32) for p in parts], compiler_params=_params(("parallel",)),
    )(*parts, *recvs)


def _sum_slices(a, name):
    n, rows, width = a.shape
    tr = _row_tile(rows, width, n + 1)

    def body(a_ref, o_ref):
        acc = a_ref[0].astype(F32)
        for i in range(1, n):
            acc = acc + a_ref[i].astype(F32)
        o_ref[...] = acc

    return _pallas_call(
        body, name=name, grid=(rows // tr,), in_specs=[pl.BlockSpec((n, tr, width), lambda i: (0, i, 0))],
        out_specs=pl.BlockSpec((tr, width), lambda i: (i, 0)), out_shape=jax.ShapeDtypeStruct((rows, width), F32),
        compiler_params=_params(("parallel",)),
    )(a)


def _adamw_update(w, g, m, v):
    nm = ADAM_B1 * m + (1.0 - ADAM_B1) * g
    nv = ADAM_B2 * v + (1.0 - ADAM_B2) * (g * g)
    m_hat = nm / (1.0 - ADAM_B1 ** ADAM_STEP)
    v_hat = nv / (1.0 - ADAM_B2 ** ADAM_STEP)
    return -ADAM_LR * (m_hat / (jnp.sqrt(v_hat) + ADAM_EPS) + ADAM_WD * w), nm, nv


def _adamw(ws, gs, ms, vs, name):
    n = len(ws)

    def body(*refs):
        for k in range(n):
            w_ref, g_ref, m_ref, v_ref = (refs[s * n + k] for s in range(4))
            d_ref, nm_ref, nv_ref = (refs[(4 + s) * n + k] for s in range(3))
            d_ref[...], nm_ref[...], nv_ref[...] = _adamw_update(w_ref[...], g_ref[...], m_ref[...], v_ref[...])

    whole = [pl.BlockSpec(w.shape, lambda i: (0, 0)) for w in ws]
    outs = _pallas_call(
        body, name=name, grid=(1,), in_specs=whole * 4, out_specs=whole * 3,
        out_shape=[jax.ShapeDtypeStruct(w.shape, F32) for _ in range(3) for w in ws],
        compiler_params=_params(("arbitrary",)),
    )(*ws, *gs, *ms, *vs)
    return [[outs[s * n + k] for s in range(3)] for k in range(n)]


ADAMW_STEPS = 4


def _adamw_halves(ws, ms, vs, mine, other, name):
    n = len(ws)
    depth = ws[0].shape[0]
    assert depth == 2
    halves = [(w.shape[1] // 2, w.shape[2]) for w in ws]
    tiles = [hr // ADAMW_STEPS for hr, _ in halves]
    kinds = ((0, True), (0, False), (1, True), (1, False))

    def active(l, h, layer, own):
        mine_half = h == _core()
        return (l == layer) & (mine_half if own else jnp.logical_not(mine_half))

    def body(*refs):
        l, h = pl.program_id(0), pl.program_id(1)
        flags = [active(l, h, layer, own) for layer, own in kinds]
        for k in range(n):
            w_ref, m_ref, v_ref = refs[k], refs[n + k], refs[2 * n + k]
            g_refs = [refs[(3 + s) * n + k] for s in range(4)]
            go_ref, d_ref, nm_ref, nv_ref = (refs[(7 + s) * n + k] for s in range(4))
            for flag, g_ref in zip(flags, g_refs):
                @pl.when(flag)
                def _():
                    gv = g_ref[...]
                    go_ref[...] = gv
                    d_ref[...], nm_ref[...], nv_ref[...] = _adamw_update(w_ref[...], gv, m_ref[...], v_ref[...])

    def blk(k):
        return pl.BlockSpec((None, None, tiles[k], halves[k][1]), lambda l, h, i: (l, h, i, 0))

    def gspec(k, layer, own):
        return pl.BlockSpec((tiles[k], halves[k][1]), lambda l, h, i: (jnp.where(active(l, h, layer, own), i, 0), 0))

    def view(a, k):
        return a.reshape(depth, 2, halves[k][0], halves[k][1])

    blks = [blk(k) for k in range(n)]
    sources = [[(mine if own else other)[layer][k] for k in range(n)] for layer, own in kinds]
    outs = _pallas_call(
        body, name=name, grid=(depth, 2, ADAMW_STEPS),
        in_specs=blks * 3 + [gspec(k, layer, own) for layer, own in kinds for k in range(n)], out_specs=blks * 4,
        out_shape=[jax.ShapeDtypeStruct((depth, 2) + halves[k], F32) for _ in range(4) for k in range(n)],
        compiler_params=_params(("parallel", "parallel", "parallel")),
    )(*[view(a, k) for group in (ws, ms, vs) for k, a in enumerate(group)], *[g for src in sources for g in src])
    return [[outs[s * n + k].reshape(ws[k].shape) for s in range(4)] for k in range(n)]


BIG = ("w_in", "w_pool_up", "w_conv_out", "w_attn_up", "w_o", "w_ff1", "w_ff2")
SMALL = ("norm_mix", "b_gate", "pool_mix", "pool_scale", "conv_w", "q_gain", "k_gain", "norm_mlp")
ORDER = ("norm_mix", "w_in", "b_gate", "pool_mix", "pool_scale", "conv_w", "q_gain", "k_gain", "w_pool_up",
         "w_conv_out", "w_attn_up", "w_o", "norm_mlp", "w_ff1", "w_ff2")
COLUMN_SHARDED = ("w_pool_up", "w_conv_out", "w_attn_up", "w_ff1")


def _matrix_weights(gathered):
    w = {}
    for name, g4 in gathered.items():
        if name in COLUMN_SHARDED:
            w[name] = g4
        else:
            w[name] = g4.reshape(N_CHIPS * g4.shape[1], g4.shape[2])
    return w


def _small_weights(l, small):
    w = {}
    w["norm_mix"] = small["norm_mix"][l][None]
    w["norm_mlp"] = small["norm_mlp"][l][None]
    w["b_gate"] = small["b_gate"][l][None]
    w["pool_mix"] = small["pool_mix"][l].astype(BF)
    w["pool_scale"] = small["pool_scale"][l][None]
    w["conv_w"] = jnp.pad(small["conv_w_full"][l], ((0, 5), (0, 0)))
    w["qk_gain"] = jnp.pad(jnp.stack([jnp.tile(small["q_gain"][l], 2), jnp.tile(small["k_gain"][l], 2)]), ((0, 6), (0, 0)))
    return w


def _to_chip_major(name, g):
    if name == "w_in":
        return g.T.reshape(N_CHIPS, g.shape[1] // N_CHIPS, g.shape[0])
    if name in COLUMN_SHARDED:
        return g
    return g.reshape(N_CHIPS, g.shape[0] // N_CHIPS, g.shape[1])


def _pad8(a):
    a = a.reshape(-1)
    return jnp.pad(a, (0, (-a.size) % (8 * LANES))).reshape(-1, LANES)


def kernel(x, norm_mix, w_in, b_gate, pool_mix, pool_scale, conv_w, q_gain, k_gain, w_pool_up, w_conv_out, w_attn_up, w_o, norm_mlp, w_ff1, w_ff2, loss_target, m_norm_mix, m_w_in, m_b_gate, m_pool_mix, m_pool_scale, m_conv_w, m_q_gain, m_k_gain, m_w_pool_up, m_w_conv_out, m_w_attn_up, m_w_o, m_norm_mlp, m_w_ff1, m_w_ff2, v_norm_mix, v_w_in, v_b_gate, v_pool_mix, v_pool_scale, v_conv_w, v_q_gain, v_k_gain, v_w_pool_up, v_w_conv_out, v_w_attn_up, v_w_o, v_norm_mlp, v_w_ff1, v_w_ff2):
    weights = dict(norm_mix=norm_mix, w_in=w_in, b_gate=b_gate, pool_mix=pool_mix, pool_scale=pool_scale, conv_w=conv_w,
                   q_gain=q_gain, k_gain=k_gain, w_pool_up=w_pool_up, w_conv_out=w_conv_out, w_attn_up=w_attn_up,
                   w_o=w_o, norm_mlp=norm_mlp, w_ff1=w_ff1, w_ff2=w_ff2)
    moms = dict(norm_mix=m_norm_mix, w_in=m_w_in, b_gate=m_b_gate, pool_mix=m_pool_mix, pool_scale=m_pool_scale,
                conv_w=m_conv_w, q_gain=m_q_gain, k_gain=m_k_gain, w_pool_up=m_w_pool_up, w_conv_out=m_w_conv_out,
                w_attn_up=m_w_attn_up, w_o=m_w_o, norm_mlp=m_norm_mlp, w_ff1=m_w_ff1, w_ff2=m_w_ff2)
    vels = dict(norm_mix=v_norm_mix, w_in=v_w_in, b_gate=v_b_gate, pool_mix=v_pool_mix, pool_scale=v_pool_scale,
                conv_w=v_conv_w, q_gain=v_q_gain, k_gain=v_k_gain, w_pool_up=v_w_pool_up, w_conv_out=v_w_conv_out,
                w_attn_up=v_w_attn_up, w_o=v_w_o, norm_mlp=v_norm_mlp, w_ff1=v_w_ff1, w_ff2=v_w_ff2)
    depth = norm_mix.shape[0]
    q = 2 * lax.axis_index("x") + lax.axis_index("y")
    for group in (weights, moms, vels):
        group["w_in"] = jnp.swapaxes(group["w_in"], 1, 2)

    assert depth == 2, "the second layer's gather hides behind the first layer's forward, and likewise backward"
    first, rest = BIG[:1], BIG[1:]
    cw_all = _all_to_all_small(_pad8(conv_w))
    bufs = [{n: _cast_place(weights[n], 0, f"cast_{n}_l0") for n in first}]
    a_ssem, a_rsem, a_views, a_token = _gather_start([bufs[0][n] for n in first], "gather_start_l0_in", cw_all)
    bufs[0].update({n: _cast_place(weights[n], 0, f"cast_{n}_l0") for n in rest})
    bufs += [{n: _cast_place(weights[n], l, f"cast_{n}_l{l}") for n in BIG} for l in range(1, depth)]
    b_ssem, b_rsem, b_views, b_token = _gather_start([bufs[0][n] for n in rest], "gather_start_l0_rest", a_token)
    g_ssem, g_rsem, g_views, g_token = _gather_start([bufs[1][n] for n in BIG], "gather_start_l1", b_token)
    conv_w_full = jnp.concatenate(
        [cw_all[2 * p].reshape(-1)[:conv_w.size].reshape(conv_w.shape) for p in range(N_CHIPS)], axis=-1)
    small = dict(weights)
    small["conv_w_full"] = conv_w_full

    def soon_weights(t):
        got = _gather_finish(a_ssem, a_rsem, a_views, t, "gather_wait_l0_in", "gather_forward_l0_in",
                             [bufs[0][n].shape for n in first])
        return _matrix_weights(dict(zip(first, got)))

    def late_weights(t):
        got = _gather_finish(b_ssem, b_rsem, b_views, t, "gather_wait_l0_rest", "gather_forward_l0_rest",
                             [bufs[0][n].shape for n in rest])
        return _matrix_weights(dict(zip(rest, got)))

    wl, saved = [None] * depth, [None] * depth
    small_1 = _small_weights(1, small)
    (h, hb_1), saved[0], wl[0] = _layer_fwd(x[0], _small_weights(0, small), "l0", after=g_token, soon=soon_weights,
                                            late=late_weights, next_gain=small_1["norm_mix"])
    got = _gather_finish(g_ssem, g_rsem, g_views, h, "gather_wait_l1", "gather_forward_l1",
                         [bufs[1][n].shape for n in BIG])
    (dh, loss_row), saved[1], wl[1] = _layer_fwd(
        h, dict(small_1, **_matrix_weights(dict(zip(BIG, got)))), "l1", target=loss_target[0], hb=hb_1)

    def pair_stage(names, g, tag):
        views = [_halves(_to_chip_major(n, g[n])) for n in names]
        from_sibling = _pair_swap(views, f"grad_pair_swap_{tag}")
        return _pair_sum(views, from_sibling, f"pair_sum_{tag}")

    mine, other = [{}, {}], [{}, {}]

    def finish(names, l, started, after, tag):
        ssem, rsem, parts, landing, _ = started
        parts, arrived = _chip_exchange_wait(ssem, rsem, parts, landing, after, f"grad_chip_exchange_wait_{tag}")
        got = _chip_sum(parts, arrived, f"chip_sum_{tag}")
        mine[l].update(zip(names, got))
        other[l].update(zip(names, _pair_send(got, f"grad_pair_send_{tag}")))

    def small_pieces(g):
        return [_pad8(g[n][:3] if n == "conv_w" else g[n]) for n in SMALL]

    def start_small(l):
        return _all_to_all_small_start(jnp.concatenate(small_pieces(grads[l]), axis=0), f"small_grad_exchange_start_l{l}")

    grads, early, small = [None] * depth, {}, [None] * depth
    dh, grads[1] = _layer_bwd(dh, wl[1], saved[1], "l1")
    second = _chip_exchange_start(pair_stage(BIG, grads[1], "l1"), "grad_chip_exchange_start_l1")
    small[1] = start_small(1)

    def start_rest(g):
        early["rest"] = _chip_exchange_start(pair_stage(rest, g, "l0_rest"), "grad_chip_exchange_start_l0_rest")
        return early["rest"][4]

    def start_last(g):
        early["in"] = _chip_exchange_start(pair_stage(first, g, "l0_in"), "grad_chip_exchange_start_l0_in")
        return early["in"][4]

    dh, grads[0] = _layer_bwd(dh, wl[0], saved[0], "l0", after=[second[4], small[1][4]], mid=start_rest,
                              tail=start_last)
    small[0] = start_small(0)
    finish(BIG, 1, second, dh, "l1")
    finish(rest, 0, early["rest"], dh, "l0_rest")
    loss = lax.psum(loss_row[0, 0], ("x", "y", "c"))
    full = {}

    deltas, new_m, new_v = {}, {}, {}

    def update_matrices(names, tag):
        results = _adamw_halves(
            [weights[n] for n in names], [moms[n] for n in names], [vels[n] for n in names],
            [[mine[l][n] for n in names] for l in range(depth)], [[other[l][n] for n in names] for l in range(depth)],
            f"adamw_{tag}")
        for n, (g_, d_, m_, v_) in zip(names, results):
            full[n], deltas[n], new_m[n], new_v[n] = g_, d_, m_, v_

    update_matrices(rest, "rest")
    finish(first, 0, early["in"], deltas[rest[-1]], "l0_in")
    update_matrices(first, "in")
    summed = []
    for l in range(depth):
        ssem, rsem, part, landing, _ = small[l]
        summed.append(_sum_slices(_all_to_all_small_wait(ssem, rsem, part, landing, deltas[first[-1]],
                                                         f"small_grad_exchange_wait_l{l}"), f"small_sum_l{l}"))
    row = 0
    for n, piece in zip(SMALL, small_pieces(grads[0])):
        size = (weights[n].size if n != "conv_w" else depth * 3 * 512) // depth
        flat = jnp.stack([s[row:row + piece.shape[0]].reshape(-1)[:size] for s in summed])
        row += piece.shape[0]
        if n == "conv_w":
            full[n] = lax.dynamic_slice_in_dim(flat.reshape(depth, 3, 512), q * conv_w.shape[2], conv_w.shape[2], axis=2)
        else:
            full[n] = flat.reshape(weights[n].shape)
    two_d = {n: (-1, weights[n].shape[-1]) if n not in ("conv_w", "q_gain", "k_gain") else (1, -1) for n in SMALL}
    results = _adamw(*[[group[n].reshape(two_d[n]) for n in SMALL] for group in (weights, full, moms, vels)],
                     "adamw_small")
    for n, (d2, m2, v2) in zip(SMALL, results):
        shape = weights[n].shape
        deltas[n], new_m[n], new_v[n] = d2.reshape(shape), m2.reshape(shape), v2.reshape(shape)
        full[n] = full[n].reshape(shape)
    for group in (full, deltas, new_m, new_v):
        group["w_in"] = jnp.swapaxes(group["w_in"], 1, 2)
    return (loss, dh[None], *[full[n] for n in ORDER], *[deltas[n] for n in ORDER], *[new_m[n] for n in ORDER],
            *[new_v[n] for n in ORDER])
```

```python
import functools

import jax
import jax.numpy as jnp
from jax import lax
from jax.experimental import pallas as pl
from jax.experimental.pallas import tpu as pltpu

F32 = jnp.float32
BF = jnp.bfloat16
MESH_ID = pl.DeviceIdType.MESH
ANY = pl.BlockSpec(memory_space=pl.ANY)

EPS = 1e-6
MASK_VALUE = -1e30
POOL_WINDOWS = (2, 4, 8, 16)
ATTN_DILATIONS = (1, 4, 16)
ATTN_BLOCK = 128
HEAD_DIM = 64
OFF_Q, OFF_K, OFF_V, OFF_GATE = 2048, 2816, 3584, 4352
N_CHIPS = 4
ADAM_LR, ADAM_B1, ADAM_B2, ADAM_EPS, ADAM_WD, ADAM_STEP = 0.001, 0.9, 0.999, 1e-08, 0.01, 10

VMEM_LIMIT = 48 * 1024 * 1024
LANES = 128

_DIMS = {"nn": (((1,), (0,)), ((), ())), "nt": (((1,), (1,)), ((), ())), "tn": (((0,), (0,)), ((), ()))}


def _params(sem):
    return pltpu.CompilerParams(dimension_semantics=sem, vmem_limit_bytes=VMEM_LIMIT)


def _pallas_call(body, **kw):
    def in_hbm(s):
        pin = isinstance(s, jax.ShapeDtypeStruct) and s is not TOKEN and jnp.issubdtype(s.dtype, jnp.floating)
        return pltpu.HBM(s.shape, s.dtype) if pin else s

    out_shape = kw.pop("out_shape")
    kw["out_shape"] = [in_hbm(s) for s in out_shape] if isinstance(out_shape, (list, tuple)) else in_hbm(out_shape)
    call = pl.pallas_call(body, **kw)

    def run(*args):
        pinned = [pltpu.with_memory_space_constraint(a, pltpu.HBM)
                  if hasattr(a, "dtype") and jnp.issubdtype(a.dtype, jnp.floating) else a for a in args]
        return call(*pinned)

    return run


def _dot(a, b, mode="nn"):
    return lax.dot_general(a, b, _DIMS[mode], preferred_element_type=F32)


def _mm(a, b, mode, name, *, tm, tn, tk, out_dtype=F32, res=None, aux=None, epi=None, n_outer=False,
        b_shards=False, out_shards=False, after=None, vec=None):
    if mode == "tn":
        K, M = a.shape
    else:
        M, K = a.shape
    if b_shards:
        if mode == "nn":
            assert b.shape[1] == K
            N = b.shape[2] * N_CHIPS
        else:
            assert mode == "nt"
            N = b.shape[1]
            assert b.shape[2] * N_CHIPS == K
    else:
        N = b.shape[0] if mode == "nt" else b.shape[1]
    tm, tn, tk = min(tm, M), min(tn, N), min(tk, K)
    assert M % tm == 0 and N % tn == 0 and K % tk == 0
    nk = K // tk
    if n_outer:
        grid = (N // tn, M // tm, nk)
        ij = lambda p, q_: (q_, p)
    else:
        grid = (M // tm, N // tn, nk)
        ij = lambda p, q_: (p, q_)

    def amap(p, q_, k):
        i, j = ij(p, q_)
        return (k, i) if mode == "tn" else (i, k)

    a_spec = pl.BlockSpec((tk, tm) if mode == "tn" else (tm, tk), amap)
    if b_shards:
        if mode == "nn":
            per = (N // N_CHIPS) // tn
            assert per >= 1 and (N // N_CHIPS) % tn == 0

            def bmap(p, q_, k):
                i, j = ij(p, q_)
                return (j // per, k, j % per)

            b_spec = pl.BlockSpec((None, tk, tn), bmap)
        else:
            per = (K // N_CHIPS) // tk
            assert per >= 1 and (K // N_CHIPS) % tk == 0

            def bmap(p, q_, k):
                i, j = ij(p, q_)
                return (k // per, j, k % per)

            b_spec = pl.BlockSpec((None, tn, tk), bmap)
    else:
        def bmap(p, q_, k):
            i, j = ij(p, q_)
            return (j, k) if mode == "nt" else (k, j)

        b_spec = pl.BlockSpec((tn, tk) if mode == "nt" else (tk, tn), bmap)

    def omap(p, q_, k):
        return ij(p, q_)

    o_spec = pl.BlockSpec((tm, tn), omap)
    if out_shards:
        per_o = (N // N_CHIPS) // tn
        assert per_o >= 1 and (N // N_CHIPS) % tn == 0

        def osmap(p, q_, k):
            i, j = ij(p, q_)
            return (j // per_o, i, j % per_o)

        out_spec0 = pl.BlockSpec((None, tm, tn), osmap)
        out_shape0 = jax.ShapeDtypeStruct((N_CHIPS, M, N // N_CHIPS), out_dtype)
    else:
        out_spec0 = o_spec
        out_shape0 = jax.ShapeDtypeStruct((M, N), out_dtype)

    in_specs = [a_spec, b_spec]
    args = [a, b]
    if res is not None:
        in_specs.append(o_spec)
        args.append(res)
    if aux is not None:
        in_specs.append(o_spec)
        args.append(aux)
    if vec is not None:
        in_specs.append(pl.BlockSpec((1, tn), lambda p, q_, k: (0, ij(p, q_)[1])))
        args.append(vec)
    after = [] if after is None else list(after) if isinstance(after, (list, tuple)) else [after]
    in_specs += [ANY] * len(after)
    args += after
    out_specs = [out_spec0]
    out_shape = [out_shape0]
    reduces = epi in ("loss", "rms_bwd")
    if reduces:
        assert tn == N and not n_outer and not out_shards
        width = LANES if epi == "loss" else N
        out_specs.append(pl.BlockSpec((1, width), lambda p, q_, k: (0, 0)))
        out_shape.append(jax.ShapeDtypeStruct((1, width), F32))
    if epi == "rms_next":
        assert tn == N and not out_shards
        out_specs.append(o_spec)
        out_shape.append(jax.ShapeDtypeStruct((M, N), BF))
    n_out = len(out_shape)
    has_res, has_aux, has_vec, n_after = res is not None, aux is not None, vec is not None, len(after)

    def body(*refs):
        a_ref, b_ref = refs[0], refs[1]
        pos = 2
        res_ref = aux_ref = vec_ref = None
        if has_res:
            res_ref = refs[pos]
            pos += 1
        if has_aux:
            aux_ref = refs[pos]
            pos += 1
        if has_vec:
            vec_ref = refs[pos]
            pos += 1
        pos += n_after
        outs = refs[pos:pos + n_out]
        part = _dot(a_ref[...].astype(BF), b_ref[...].astype(BF), mode)

        first_row_tile = pl.program_id(0) == 0

        def add_to_sum(row):
            @pl.when(first_row_tile)
            def _():
                outs[1][...] = jnp.zeros_like(outs[1])

            outs[1][...] += row

        def finish(acc):
            if epi == "rms_bwd":
                xv = aux_ref[...]
                r = lax.rsqrt(jnp.mean(xv * xv, axis=-1, keepdims=True) + EPS)
                xhat = xv * r
                dy = acc * vec_ref[...]
                outs[0][...] = res_ref[...] + r * (dy - xhat * jnp.mean(dy * xhat, axis=-1, keepdims=True))
                add_to_sum(jnp.sum(acc * xhat, axis=0, keepdims=True))
                return
            if res_ref is not None:
                acc = res_ref[...] + acc
            if epi == "relu2":
                r = jnp.maximum(acc, 0.0)
                outs[0][...] = (r * r).astype(out_dtype)
            elif epi == "drelu2":
                outs[0][...] = (acc.astype(BF) * (2.0 * jnp.sqrt(aux_ref[...]))).astype(out_dtype)
            elif epi == "rms_next":
                outs[0][...] = acc
                r = lax.rsqrt(jnp.mean(acc * acc, axis=-1, keepdims=True) + EPS)
                outs[1][...] = ((acc * r) * vec_ref[...]).astype(BF)
            elif epi == "loss":
                e = acc - aux_ref[...]
                outs[0][...] = e / float(N)
                add_to_sum(0.5 * jnp.sum(jnp.mean(e * e, axis=-1, keepdims=True)))
            else:
                outs[0][...] = acc.astype(out_dtype)

        if nk == 1:
            finish(part)
        else:
            acc_ref = refs[pos + n_out]
            k = pl.program_id(2)

            @pl.when(k == 0)
            def _():
                acc_ref[...] = part

            @pl.when(k > 0)
            def _():
                acc_ref[...] += part

            @pl.when(k == nk - 1)
            def _():
                finish(acc_ref[...])

    scratch = [pltpu.VMEM((tm, tn), F32)] if nk > 1 else []
    out = _pallas_call(
        body, name=name, grid=grid, in_specs=in_specs, out_specs=out_specs, out_shape=out_shape,
        scratch_shapes=scratch,
        compiler_params=_params(("arbitrary" if reduces else "parallel", "parallel", "arbitrary")),
    )(*args)
    return out if n_out > 1 else out[0]


def _rms_fwd(x, gain, name, after=None):
    T, D = x.shape
    tm = min(512, T)

    def body(x_ref, g_ref, *rest):
        o_ref = rest[-1]
        xv = x_ref[...]
        r = lax.rsqrt(jnp.mean(xv * xv, axis=-1, keepdims=True) + EPS)
        o_ref[...] = ((xv * r) * g_ref[...]).astype(BF)

    extra = [] if after is None else list(after) if isinstance(after, (list, tuple)) else [after]
    return _pallas_call(
        body, name=name, grid=(T // tm,),
        in_specs=[pl.BlockSpec((tm, D), lambda i: (i, 0)), pl.BlockSpec((1, D), lambda i: (0, 0))] + [ANY] * len(extra),
        out_specs=pl.BlockSpec((tm, D), lambda i: (i, 0)), out_shape=jax.ShapeDtypeStruct((T, D), BF),
        compiler_params=_params(("parallel",)),
    )(x, gain, *extra)


def _rms_bwd(dh, x, gain, dres, name):
    T, D = x.shape
    tm = min(512, T)

    def body(dh_ref, x_ref, g_ref, dres_ref, dx_ref, dg_ref):
        xv = x_ref[...]
        r = lax.rsqrt(jnp.mean(xv * xv, axis=-1, keepdims=True) + EPS)
        xhat = xv * r
        dhv = dh_ref[...]
        dy = dhv * g_ref[...]
        dx_ref[...] = dres_ref[...] + r * (dy - xhat * jnp.mean(dy * xhat, axis=-1, keepdims=True))

        @pl.when(pl.program_id(0) == 0)
        def _():
            dg_ref[...] = jnp.zeros_like(dg_ref)

        dg_ref[...] += jnp.sum(dhv * xhat, axis=0, keepdims=True)

    row = pl.BlockSpec((tm, D), lambda i: (i, 0))
    vec = pl.BlockSpec((1, D), lambda i: (0, 0))
    return _pallas_call(
        body, name=name, grid=(T // tm,), in_specs=[row, row, vec, row], out_specs=[row, vec],
        out_shape=[jax.ShapeDtypeStruct((T, D), F32), jax.ShapeDtypeStruct((1, D), F32)],
        compiler_params=_params(("arbitrary",)),
    )(dh, x, gain, dres)


POOL_HALO = 16
CONV_HALO = 8
POOLCONV_ROWS = 512


def _causal_window_sum(v, w):
    s, sh = v, 1
    while sh < w:
        s = s + pltpu.roll(s, sh, 0)
        sh *= 2
    return s


def _anticausal_window_sum(v, w):
    n = v.shape[0]
    s, sh = v, 1
    while sh < w:
        s = s + pltpu.roll(s, n - sh, 0)
        sh *= 2
    return s


def _poolconv_fwd(z, pmix_b, pscale, convw, name):
    T = z.shape[0]
    R = min(POOLCONV_ROWS, T)
    PH, CH = R // POOL_HALO, R // CONV_HALO

    def body(u_ref, uh_ref, b_ref, c_ref, ch_ref, x_ref, xh_ref, mix_ref, sc_ref, cw_ref, yp_ref, yc_ref):
        i = pl.program_id(0)
        keep = (i > 0).astype(F32)
        row = i * R + lax.broadcasted_iota(jnp.int32, (R, 1), 0)
        w_all = jnp.concatenate([uh_ref[...] * keep, u_ref[...]], axis=0)
        for g, w in enumerate(POOL_WINDOWS):
            cols = slice(128 * g, 128 * (g + 1))
            wg = w_all[:, cols]
            s = _causal_window_sum(wg, w)[POOL_HALO:]
            inv_cnt = 1.0 / jnp.minimum(row + 1, w).astype(F32)
            dgrp = s * inv_cnt - wg[POOL_HALO:]
            y = _dot(dgrp.astype(BF), mix_ref[g]) * sc_ref[:, cols]
            yp_ref[:, cols] = y.astype(BF)
        uc = jnp.concatenate([ch_ref[...] * xh_ref[...] * keep, c_ref[...] * x_ref[...]], axis=0)
        yc = cw_ref[2:3, :] * uc + cw_ref[0:1, :] * pltpu.roll(uc, 2, 0) + cw_ref[1:2, :] * pltpu.roll(uc, 1, 0)
        yc_ref[...] = (b_ref[...] * yc[CONV_HALO:]).astype(BF)

    def main(cb):
        return pl.BlockSpec((R, 512), lambda i: (i, cb))

    def prev(cb, halo, per):
        return pl.BlockSpec((halo, 512), lambda i: (jnp.maximum(i * per - 1, 0), cb))

    full = lambda a: pl.BlockSpec(a.shape, lambda i: (0,) * a.ndim)
    return _pallas_call(
        body, name=name, grid=(T // R,),
        in_specs=[main(0), prev(0, POOL_HALO, PH), main(1), main(2), prev(2, CONV_HALO, CH), main(3),
                  prev(3, CONV_HALO, CH), full(pmix_b), full(pscale), full(convw)],
        out_specs=[pl.BlockSpec((R, 512), lambda i: (i, 0))] * 2,
        out_shape=[jax.ShapeDtypeStruct((T, 512), BF)] * 2,
        compiler_params=_params(("parallel",)),
    )(z, z, z, z, z, z, z, pmix_b, pscale, convw)


def _poolconv_bwd(z, dyp, dyc, pmix_b, pscale, convw, dz, name):
    T = z.shape[0]
    R = min(POOLCONV_ROWS, T)
    PH, CH = R // POOL_HALO, R // CONV_HALO
    nsteps = T // R

    def body(u_ref, uh_ref, b_ref, bn_ref, c_ref, ch_ref, x_ref, xh_ref, dyp_ref, dypn_ref, dyc_ref, dycn_ref,
             mix_ref, sc_ref, cw_ref, dz_in_ref, dz_ref, dmix_ref, dsc_ref, dcw_ref):
        i = pl.program_id(0)
        keep_prev = (i > 0).astype(F32)
        keep_next = (i < nsteps - 1).astype(F32)

        @pl.when(i == 0)
        def _():
            dmix_ref[...] = jnp.zeros_like(dmix_ref)
            dsc_ref[...] = jnp.zeros_like(dsc_ref)
            dcw_ref[...] = jnp.zeros_like(dcw_ref)

        row = i * R + lax.broadcasted_iota(jnp.int32, (R, 1), 0)
        row_ext = i * R + lax.broadcasted_iota(jnp.int32, (R + POOL_HALO, 1), 0)
        w_all = jnp.concatenate([uh_ref[...] * keep_prev, u_ref[...]], axis=0)
        dyp_ext = jnp.concatenate([dyp_ref[...], dypn_ref[...] * keep_next], axis=0)
        for g, w in enumerate(POOL_WINDOWS):
            cols = slice(128 * g, 128 * (g + 1))
            wg = w_all[:, cols]
            s = _causal_window_sum(wg, w)[POOL_HALO:]
            inv_cnt = 1.0 / jnp.minimum(row + 1, w).astype(F32)
            dgrp = (s * inv_cnt - wg[POOL_HALO:]).astype(BF)
            y_pre = _dot(dgrp, mix_ref[g])
            dsc_ref[:, cols] += jnp.sum(dyp_ref[:, cols] * y_pre, axis=0, keepdims=True)
            dyb = (dyp_ext[:, cols] * sc_ref[:, cols]).astype(BF)
            dmix_ref[cols, :] += _dot(dgrp, dyb[:R], "tn")
            dd = _dot(dyb, mix_ref[g], "nt")
            inv_cnt_ext = 1.0 / jnp.minimum(row_ext + 1, w).astype(F32)
            e = _anticausal_window_sum(dd * inv_cnt_ext, w)
            dz_ref[:, cols] = (e[:R] - dd[:R]).astype(BF)
        cw0, cw1, cw2 = cw_ref[0:1, :], cw_ref[1:2, :], cw_ref[2:3, :]
        uc = jnp.concatenate([ch_ref[...] * xh_ref[...] * keep_prev, c_ref[...] * x_ref[...]], axis=0)
        uc1 = pltpu.roll(uc, 1, 0)[CONV_HALO:]
        uc2 = pltpu.roll(uc, 2, 0)[CONV_HALO:]
        uc0 = uc[CONV_HALO:]
        yc = cw2 * uc0 + cw0 * uc2 + cw1 * uc1
        dycv = dyc_ref[...]
        dz_ref[:, 512:1024] = (dycv * yc).astype(BF)
        dv_ext = jnp.concatenate([dycv * b_ref[...], dycn_ref[...] * bn_ref[...] * keep_next], axis=0)
        n_ext = R + CONV_HALO
        duc = (cw2 * dv_ext + cw1 * pltpu.roll(dv_ext, n_ext - 1, 0) + cw0 * pltpu.roll(dv_ext, n_ext - 2, 0))[:R]
        dv = dv_ext[:R]
        dcw_ref[0:1, :] += jnp.sum(dv * uc2, axis=0, keepdims=True)
        dcw_ref[1:2, :] += jnp.sum(dv * uc1, axis=0, keepdims=True)
        dcw_ref[2:3, :] += jnp.sum(dv * uc0, axis=0, keepdims=True)
        dz_ref[:, 1024:1536] = (duc * x_ref[...]).astype(BF)
        dz_ref[:, 1536:2048] = (duc * c_ref[...]).astype(BF)

    def main(cb):
        return pl.BlockSpec((R, 512), lambda i: (i, cb))

    def prev(cb, halo, per):
        return pl.BlockSpec((halo, 512), lambda i: (jnp.maximum(i * per - 1, 0), cb))

    def nxt(cb, halo, per):
        return pl.BlockSpec((halo, 512), lambda i: (jnp.minimum((i + 1) * per, T // halo - 1), cb))

    full = lambda a: pl.BlockSpec(a.shape, lambda i: (0,) * a.ndim)
    return _pallas_call(
        body, name=name, grid=(nsteps,),
        in_specs=[main(0), prev(0, POOL_HALO, PH), main(1), nxt(1, CONV_HALO, CH), main(2), prev(2, CONV_HALO, CH),
                  main(3), prev(3, CONV_HALO, CH), main(0), nxt(0, POOL_HALO, PH), main(0), nxt(0, CONV_HALO, CH),
                  full(pmix_b), full(pscale), full(convw), ANY],
        out_specs=[pl.BlockSpec((R, 2048), lambda i: (i, 0)), pl.BlockSpec((512, 128), lambda i: (0, 0)),
                   pl.BlockSpec((1, 512), lambda i: (0, 0)), pl.BlockSpec((8, 512), lambda i: (0, 0))],
        out_shape=[jax.ShapeDtypeStruct(dz.shape, BF), jax.ShapeDtypeStruct((512, 128), F32),
                   jax.ShapeDtypeStruct((1, 512), F32), jax.ShapeDtypeStruct((8, 512), F32)],
        input_output_aliases={15: 0}, compiler_params=_params(("arbitrary",)),
    )(z, z, z, z, z, z, z, z, dyp, dyp, dyc, dyc, pmix_b, pscale, convw, dz)


def _head_sums(v):
    row = lax.broadcasted_iota(jnp.int32, (LANES, LANES), 0) < HEAD_DIM
    col = lax.broadcasted_iota(jnp.int32, (LANES, LANES), 1) < HEAD_DIM
    same_head = jnp.where(jnp.logical_xor(row, col), 0.0, 1.0).astype(BF)
    hi = v.astype(BF)
    lo = (v - hi.astype(F32)).astype(BF)
    return _dot(hi, same_head) + _dot(lo, same_head)


def _head_norm(x, g2, ma):
    r = lax.rsqrt(_head_sums(x * x) / HEAD_DIM + EPS)
    return x * r, r


def _head_norm_bwd(dy, xhat, r, g2, ma):
    dxh = dy * g2
    return r * (dxh - xhat * (_head_sums(dxh * xhat) / HEAD_DIM))


def _attn_masks(other_block_exists):
    lane = lax.broadcasted_iota(jnp.int32, (2 * ATTN_BLOCK, ATTN_BLOCK), 1)
    qi = lax.broadcasted_iota(jnp.int32, (2 * ATTN_BLOCK, ATTN_BLOCK), 0) & (ATTN_BLOCK - 1)
    never = (1 - other_block_exists.astype(jnp.int32)) * (2 * ATTN_BLOCK)
    return lane[:ATTN_BLOCK] < HEAD_DIM, lane <= qi, lane >= qi + never


def _stack_heads(x, ma):
    return jnp.concatenate([jnp.where(ma, x, 0.0), jnp.where(ma, 0.0, x)], axis=0)


def _unstack_heads(y, ma):
    return jnp.where(ma, y[:ATTN_BLOCK], y[ATTN_BLOCK:])


def _stack_cols(tile, ma):
    return jnp.concatenate([tile[:, 0:1], tile[:, HEAD_DIM:HEAD_DIM + 1]], axis=0)


QKV_TILES = (OFF_GATE - OFF_Q) // LANES
KIND_TILES = QKV_TILES // 3


def _qk_norm(z, gains, name):
    T = z.shape[0]
    tm = min(512, T)

    def body(x_ref, g_ref, o_ref):
        ma = lax.broadcasted_iota(jnp.int32, (tm, LANES), 1) < HEAD_DIM
        for tile in range(QKV_TILES):
            v = x_ref[:, LANES * tile:LANES * (tile + 1)]
            if tile < 2 * KIND_TILES:
                g = g_ref[0:1, :] if tile < KIND_TILES else g_ref[1:2, :]
                v = _head_norm(v, g, ma)[0] * g
            o_ref[tile] = v

    return _pallas_call(
        body, name=name, grid=(T // tm,),
        in_specs=[pl.BlockSpec((pl.Element(tm), pl.Element(OFF_GATE - OFF_Q)), lambda i: (i * tm, OFF_Q)),
                  pl.BlockSpec((8, LANES), lambda i: (0, 0))],
        out_specs=pl.BlockSpec((QKV_TILES, tm, LANES), lambda i: (0, i, 0)),
        out_shape=jax.ShapeDtypeStruct((QKV_TILES, T, LANES), F32), compiler_params=_params(("parallel",)),
    )(z, gains)


ATTN_STEP_ROWS = 2048
ATTN_UNROLL = 4


def _attn_steps(T):
    assert ATTN_STEP_ROWS == ATTN_BLOCK * max(ATTN_DILATIONS) and T % ATTN_STEP_ROWS == 0
    return T // ATTN_STEP_ROWS


def _attn_rows(jj, r, sub, d):
    start = jj * sub + r
    if d == 1:
        return pl.ds(pl.multiple_of(start, ATTN_BLOCK), ATTN_BLOCK)
    return pl.ds(start, ATTN_BLOCK, stride=d)


def _pick(flag, a, b):
    return jnp.where(jnp.full(a.shape, flag.astype(jnp.int32)) > 0, a, b)


def _attn_fwd(qkv, name):
    T = qkv.shape[1]
    nbig = _attn_steps(T)
    scale = HEAD_DIM ** -0.5

    def body(q_ref, kc_ref, kp_ref, vc_ref, vp_ref, o_ref, lse_ref):
        jb = pl.program_id(1)
        for gi, d in enumerate(ATTN_DILATIONS):
            pl.when(pl.program_id(0) == gi)(functools.partial(group, d, jb, q_ref, kc_ref, kp_ref, vc_ref, vp_ref,
                                                              o_ref, lse_ref))

    def group(d, jb, q_ref, kc_ref, kp_ref, vc_ref, vp_ref, o_ref, lse_ref):
        sub, m = ATTN_BLOCK * d, ATTN_STEP_ROWS // (ATTN_BLOCK * d)

        def step(s, carry):
            jj, r = s // d, s % d
            here, before = _attn_rows(jj, r, sub, d), _attn_rows(jnp.maximum(jj - 1, 0), r, sub, d)
            edge = _attn_rows(m - 1, r, sub, d)
            first = jj == 0
            ma, mask_c, mask_p = _attn_masks(jb * m + jj > 0)
            qs = _stack_heads(q_ref[here, :], ma).astype(BF)
            kcb = kc_ref[here, :].astype(BF)
            kpb = _pick(first, kp_ref[edge, :], kc_ref[before, :]).astype(BF)
            vcb = vc_ref[here, :].astype(BF)
            vpb = _pick(first, vp_ref[edge, :], vc_ref[before, :]).astype(BF)
            s_c = jnp.where(mask_c, _dot(qs, kcb, "nt") * scale, MASK_VALUE)
            s_p = jnp.where(mask_p, _dot(qs, kpb, "nt") * scale, MASK_VALUE)
            mx = jnp.maximum(jnp.max(s_c, axis=-1, keepdims=True), jnp.max(s_p, axis=-1, keepdims=True))
            p_c = jnp.exp(s_c - mx)
            p_p = jnp.exp(s_p - mx)
            den = jnp.sum(p_c, axis=-1, keepdims=True) + jnp.sum(p_p, axis=-1, keepdims=True)
            o = (_dot(p_c.astype(BF), vcb) + _dot(p_p.astype(BF), vpb)) / den
            o_ref[here, :] = _unstack_heads(o, ma)
            lse_ref[here, :] = _unstack_heads(jnp.broadcast_to(mx + jnp.log(den), o.shape), ma)
            return carry

        lax.fori_loop(0, m * d, step, 0, unroll=ATTN_UNROLL)

    def cur(kind):
        return pl.BlockSpec((None, ATTN_STEP_ROWS, LANES), lambda g, j, t: (KIND_TILES * kind + 2 * g + t, j, 0))

    def prv(kind):
        return pl.BlockSpec((None, ATTN_STEP_ROWS, LANES),
                            lambda g, j, t: (KIND_TILES * kind + 2 * g + t, jnp.maximum(j - 1, 0), 0))

    out = pl.BlockSpec((ATTN_STEP_ROWS, LANES), lambda g, j, t: (j, 2 * g + t))
    width = 2 * LANES * len(ATTN_DILATIONS)
    return _pallas_call(
        body, name=name, grid=(len(ATTN_DILATIONS), nbig, 2), in_specs=[cur(0), cur(1), prv(1), cur(2), prv(2)],
        out_specs=[out, out], out_shape=[jax.ShapeDtypeStruct((T, width), F32)] * 2,
        compiler_params=_params(("parallel", "parallel", "parallel")),
    )(qkv, qkv, qkv, qkv, qkv)


def _attn_bwd(z, qkv, do, c, lse, gains, name, after=None):
    T = z.shape[0]
    nbig = _attn_steps(T)
    scale = HEAD_DIM ** -0.5
    extra = [] if after is None else [after]

    def body(*refs):
        g, jb = pl.program_id(0), pl.program_id(1)
        dgq_ref, dgk_ref = refs[len(refs) - 5], refs[len(refs) - 4]

        @pl.when((g == 0) & (jb == 0) & (pl.program_id(2) == 0))
        def _():
            dgq_ref[...] = jnp.zeros_like(dgq_ref)
            dgk_ref[...] = jnp.zeros_like(dgk_ref)

        for gi, d in enumerate(ATTN_DILATIONS):
            pl.when(g == gi)(functools.partial(group, d, jb, *refs))

    def group(d, jb, qr_ref, kr_ref, vc_ref, vp_ref, qn_ref, qnn_ref, kn_ref, knp_ref, do_ref, don_ref, c_ref, cn_ref,
              lse_ref, lsen_ref, g_ref, *rest):
        dq_ref, dk_ref, dv_ref, dgq_ref, dgk_ref, sq_ref, sk_ref, sv_ref = rest[len(extra):]
        sub, m = ATTN_BLOCK * d, ATTN_STEP_ROWS // (ATTN_BLOCK * d)
        nb = T // sub
        gq, gk = g_ref[0:1, :], g_ref[1:2, :]

        def step(s, carry):
            jj, r = s // d, s % d
            here = _attn_rows(jj, r, sub, d)
            before = _attn_rows(jnp.maximum(jj - 1, 0), r, sub, d)
            behind = _attn_rows(jnp.minimum(jj + 1, m - 1), r, sub, d)
            edge_before, edge_behind = _attn_rows(m - 1, r, sub, d), _attn_rows(0, r, sub, d)
            first, last = jj == 0, jj == m - 1
            block = jb * m + jj
            ma, mask_c, mask_p = _attn_masks(block > 0)
            mask_n = _attn_masks(block < nb - 1)[2]
            qhat, rq = _head_norm(qr_ref[here, :], gq, ma)
            qn = qn_ref[here, :]
            qn_next = _pick(last, qnn_ref[edge_behind, :], qn_ref[behind, :])
            khat, rk = _head_norm(kr_ref[here, :], gk, ma)
            kcb = kn_ref[here, :].astype(BF)
            kpb = _pick(first, knp_ref[edge_before, :], kn_ref[before, :]).astype(BF)
            vcb = vc_ref[here, :].astype(BF)
            vpb = _pick(first, vp_ref[edge_before, :], vc_ref[before, :]).astype(BF)
            do_t, don_t = do_ref[here, :], _pick(last, don_ref[edge_behind, :], do_ref[behind, :])
            c_t, cn_t = c_ref[here, :], _pick(last, cn_ref[edge_behind, :], c_ref[behind, :])
            lse_t, lsen_t = lse_ref[here, :], _pick(last, lsen_ref[edge_behind, :], lse_ref[behind, :])
            qs, dos = _stack_heads(qn, ma).astype(BF), _stack_heads(do_t, ma).astype(BF)
            lse_s, c_s = _stack_cols(lse_t, ma), _stack_cols(c_t, ma)
            s_c = jnp.where(mask_c, _dot(qs, kcb, "nt") * scale, MASK_VALUE)
            s_p = jnp.where(mask_p, _dot(qs, kpb, "nt") * scale, MASK_VALUE)
            p_c = jnp.exp(s_c - lse_s)
            p_p = jnp.exp(s_p - lse_s)
            ds_c = ((p_c * (_dot(dos, vcb, "nt") + c_s)) * scale).astype(BF)
            ds_p = ((p_p * (_dot(dos, vpb, "nt") + c_s)) * scale).astype(BF)
            dq_t = _unstack_heads(_dot(ds_c, kcb) + _dot(ds_p, kpb), ma)
            qs_n, dos_n = _stack_heads(qn_next, ma).astype(BF), _stack_heads(don_t, ma).astype(BF)
            s_n = jnp.where(mask_n, _dot(qs_n, kcb, "nt") * scale, MASK_VALUE)
            p_n = jnp.exp(s_n - _stack_cols(lsen_t, ma))
            ds_n = ((p_n * (_dot(dos_n, vcb, "nt") + _stack_cols(cn_t, ma))) * scale).astype(BF)
            dv_t = _dot(p_c.astype(BF), dos, "tn") + _dot(p_n.astype(BF), dos_n, "tn")
            dk_t = _dot(ds_c, qs, "tn") + _dot(ds_n, qs_n, "tn")
            sq_ref[here, :] = _head_norm_bwd(dq_t, qhat, rq, gq, ma)
            sk_ref[here, :] = _head_norm_bwd(dk_t, khat, rk, gk, ma)
            sv_ref[here, :] = dv_t
            dgq_ref[...] += jnp.sum(dq_t * qhat, axis=0, keepdims=True)
            dgk_ref[...] += jnp.sum(dk_t * khat, axis=0, keepdims=True)
            return carry

        lax.fori_loop(0, m * d, step, 0, unroll=ATTN_UNROLL)
        dq_ref[...] = sq_ref[...].astype(BF)
        dk_ref[...] = sk_ref[...].astype(BF)
        dv_ref[...] = sv_ref[...].astype(BF)

    rows = ATTN_STEP_ROWS

    def raw(col0):
        return pl.BlockSpec((rows, LANES), lambda g, j, t: (j, col0 + 2 * g + t))

    def cur(kind):
        return pl.BlockSpec((None, rows, LANES), lambda g, j, t: (KIND_TILES * kind + 2 * g + t, j, 0))

    def prv(kind):
        return pl.BlockSpec((None, rows, LANES), lambda g, j, t: (KIND_TILES * kind + 2 * g + t, jnp.maximum(j - 1, 0), 0))

    def nxt(kind):
        return pl.BlockSpec((None, rows, LANES),
                            lambda g, j, t: (KIND_TILES * kind + 2 * g + t, jnp.minimum(j + 1, nbig - 1), 0))

    own = pl.BlockSpec((rows, LANES), lambda g, j, t: (j, 2 * g + t))
    own_next = pl.BlockSpec((rows, LANES), lambda g, j, t: (jnp.minimum(j + 1, nbig - 1), 2 * g + t))
    vec = pl.BlockSpec((1, LANES), lambda g, j, t: (0, 0))
    width = 2 * LANES * len(ATTN_DILATIONS)
    return _pallas_call(
        body, name=name, grid=(len(ATTN_DILATIONS), nbig, 2),
        in_specs=[raw(OFF_Q // LANES), raw(OFF_K // LANES), cur(2), prv(2), cur(0), nxt(0), cur(1), prv(1), own, own_next,
                  own, own_next, own, own_next, pl.BlockSpec((8, LANES), lambda g, j, t: (0, 0))] + [ANY] * len(extra),
        out_specs=[own, own, own, vec, vec],
        out_shape=[jax.ShapeDtypeStruct((T, width), BF)] * 3 + [jax.ShapeDtypeStruct((1, LANES), F32)] * 2,
        scratch_shapes=[pltpu.VMEM((rows, LANES), F32)] * 3,
        compiler_params=_params(("arbitrary", "arbitrary", "arbitrary")),
    )(z, z, qkv, qkv, qkv, qkv, qkv, qkv, do, do, c, c, lse, lse, gains, *extra)


MERGE_ROWS = 256
GATE_TILE = 256


def _group_mix(o_refs, lse_refs):
    lses = [r[...] for r in lse_refs]
    m = jnp.maximum(jnp.maximum(lses[0], lses[1]), lses[2])
    es = [jnp.exp(l - m) for l in lses]
    den = es[0] + es[1] + es[2]
    ws = [e / den for e in es]
    y = ws[0] * o_refs[0][...] + ws[1] * o_refs[1][...] + ws[2] * o_refs[2][...]
    return ws, y


def _sigmoid(v):
    return 1.0 / (1.0 + jnp.exp(-v))


def _merge_specs(T, z, bgate, gpu, gco, gau):
    tm = min(MERGE_ROWS, T)
    row = lambda w: pl.BlockSpec((tm, w), lambda i: (i, 0))
    gate0 = OFF_GATE // GATE_TILE
    gates = [pl.BlockSpec((tm, GATE_TILE), functools.partial(lambda i, cb: (i, cb), cb=gate0 + n))
             for n in range(3 * N_CHIPS)]
    full = lambda a: pl.BlockSpec(a.shape, lambda i: (0,) * a.ndim)
    by_group = [pl.BlockSpec((tm, 256), functools.partial(lambda i, g: (i, g), g=g)) for g in range(3)]
    specs = [row(512), row(512)] + by_group * 2 + gates + [full(bgate), full(gpu), full(gco), full(gau)]
    return tm, row, specs


def _merge_fwd(yp, yc, o3, lse3, z, bgate, gpu, gco, gau, name):
    T = yp.shape[0]
    tm, row, specs = _merge_specs(T, z, bgate, gpu, gco, gau)

    def body(*refs):
        yp_ref, yc_ref = refs[0], refs[1]
        o_refs, lse_refs = refs[2:5], refs[5:8]
        zg = refs[8:20]
        b_ref, gpu_ref, gco_ref, gau_ref, out_ref = refs[20:25]
        yab = _group_mix(o_refs, lse_refs)[1].astype(BF)
        ys = (yp_ref[...], yc_ref[...], yab)
        ups = (gpu_ref, gco_ref, gau_ref)
        for n in range(N_CHIPS):
            acc = None
            for b in range(3):
                gcol = slice(1024 * b + GATE_TILE * n, 1024 * b + GATE_TILE * (n + 1))
                gate = _sigmoid(zg[N_CHIPS * b + n][...] + b_ref[:, gcol])
                term = gate * _dot(ys[b], ups[b][n])
                acc = term if acc is None else acc + term
            out_ref[:, GATE_TILE * n:GATE_TILE * (n + 1)] = acc.astype(BF)

    return _pallas_call(
        body, name=name, grid=(T // tm,), in_specs=specs, out_specs=row(1024),
        out_shape=jax.ShapeDtypeStruct((T, 1024), BF), compiler_params=_params(("parallel",)),
    )(yp, yc, *([o3] * 3), *([lse3] * 3), *([z] * 12), bgate, gpu, gco, gau)


def _merge_bwd(dm, yp, yc, o3, lse3, z, bgate, gpu, gco, gau, name):
    T = yp.shape[0]
    tm, row, specs = _merge_specs(T, z, bgate, gpu, gco, gau)
    nsteps = T // tm

    def body(*refs):
        dm_ref, yp_ref, yc_ref = refs[0:3]
        o_refs, lse_refs = refs[3:6], refs[6:9]
        zg = refs[9:21]
        b_ref, gpu_ref, gco_ref, gau_ref = refs[21:25]
        dzg_ref, dyp_ref, dyc_ref = refs[25:28]
        do_ref, c_ref = refs[28:30]
        dgpu_ref, dgco_ref, dgau_ref, dbg_ref = refs[30:34]
        accs = refs[34:37]
        i = pl.program_id(0)

        @pl.when(i == 0)
        def _():
            for a in accs:
                a[...] = jnp.zeros_like(a)
            dbg_ref[...] = jnp.zeros_like(dbg_ref)

        ws, y = _group_mix(o_refs, lse_refs)
        ys = (yp_ref[...], yc_ref[...], y.astype(BF))
        ups = (gpu_ref, gco_ref, gau_ref)
        dys = [None, None, None]
        for n in range(N_CHIPS):
            dmn = dm_ref[:, GATE_TILE * n:GATE_TILE * (n + 1)]
            for b in range(3):
                gcol = slice(1024 * b + GATE_TILE * n, 1024 * b + GATE_TILE * (n + 1))
                gate = _sigmoid(zg[N_CHIPS * b + n][...] + b_ref[:, gcol])
                up = _dot(ys[b], ups[b][n])
                dzg = (dmn * up) * (gate * (1.0 - gate))
                dzg_ref[:, gcol] = dzg.astype(BF)
                dbg_ref[:, gcol] += jnp.sum(dzg, axis=0, keepdims=True)
                dup = (dmn * gate).astype(BF)
                accs[b][n] += _dot(ys[b], dup, "tn")
                dyb = _dot(dup, ups[b][n], "nt")
                dys[b] = dyb if dys[b] is None else dys[b] + dyb
        dyp_ref[...] = dys[0]
        dyc_ref[...] = dys[1]
        dya = dys[2]
        lane = lax.broadcasted_iota(jnp.int32, dya.shape, 1) // HEAD_DIM
        pr = dya * y
        rho = jnp.zeros_like(pr)
        for h in range(256 // HEAD_DIM):
            hm = lane == h
            rho = jnp.where(hm, jnp.sum(jnp.where(hm, pr, 0.0), axis=-1, keepdims=True), rho)
        for g in range(3):
            do_ref[:, 256 * g:256 * (g + 1)] = ws[g] * dya
            c_ref[:, 256 * g:256 * (g + 1)] = -(ws[g] * rho)

        @pl.when(i == nsteps - 1)
        def _():
            dgpu_ref[...] = accs[0][...].astype(BF)
            dgco_ref[...] = accs[1][...].astype(BF)
            dgau_ref[...] = accs[2][...].astype(BF)

    full = lambda a: pl.BlockSpec(a.shape, lambda i: (0,) * a.ndim)
    dz_gate = pl.BlockSpec((pl.Element(tm), pl.Element(3072)), lambda i: (i * tm, OFF_GATE))
    out_specs = ([dz_gate, row(512), row(512)] + [row(768)] * 2 + [full(gpu), full(gco), full(gau)]
                 + [pl.BlockSpec((1, 3072), lambda i: (0, 0))])
    out_shape = ([jax.ShapeDtypeStruct(z.shape, BF)] + [jax.ShapeDtypeStruct((T, 512), F32)] * 2
                 + [jax.ShapeDtypeStruct((T, 768), F32)] * 2
                 + [jax.ShapeDtypeStruct(g.shape, BF) for g in (gpu, gco, gau)]
                 + [jax.ShapeDtypeStruct((1, 3072), F32)])
    return _pallas_call(
        body, name=name, grid=(nsteps,), in_specs=[row(1024)] + specs, out_specs=out_specs, out_shape=out_shape,
        scratch_shapes=[pltpu.VMEM(g.shape, F32) for g in (gpu, gco, gau)],
        compiler_params=_params(("arbitrary",)),
    )(dm, yp, yc, *([o3] * 3), *([lse3] * 3), *([z] * 12), bgate, gpu, gco, gau)


def _layer_fwd(x, w, tag, after=None, soon=None, late=None, target=None, hb=None, next_gain=None):
    if hb is None:
        hb = _rms_fwd(x, w["norm_mix"], f"rms_mix_{tag}", after=after)
    if soon is not None:
        w = dict(w, **soon(hb))
    z = _mm(hb, w["w_in"], "nt", f"in_proj_{tag}", tm=512, tn=3712, tk=1024, n_outer=True)
    yp, yc = _poolconv_fwd(z, w["pool_mix"], w["pool_scale"], w["conv_w"], f"poolconv_{tag}")
    qkv = _qk_norm(z, w["qk_gain"], f"qk_norm_{tag}")
    o3, lse3 = _attn_fwd(qkv, f"attn_{tag}")
    if late is not None:
        w = dict(w, **late(lse3))
    merged = _merge_fwd(yp, yc, o3, lse3, z, w["b_gate"], w["w_pool_up"], w["w_conv_out"], w["w_attn_up"],
                        f"merge_{tag}")
    x1, h2b = _mm(merged, w["w_o"], "nn", f"out_proj_{tag}", tm=1024, tn=1024, tk=1024, res=x, vec=w["norm_mlp"],
                  epi="rms_next")
    rb = _mm(h2b, w["w_ff1"], "nn", f"ff1_{tag}", tm=1024, tn=1024, tk=1024, out_dtype=BF, epi="relu2", n_outer=True,
             b_shards=True)
    if target is not None:
        x2 = _mm(rb, w["w_ff2"], "nn", f"ff2_{tag}", tm=512, tn=1024, tk=4096, res=x1, aux=target, epi="loss")
    elif next_gain is not None:
        x2 = _mm(rb, w["w_ff2"], "nn", f"ff2_{tag}", tm=512, tn=1024, tk=4096, res=x1, vec=next_gain, epi="rms_next")
    else:
        x2 = _mm(rb, w["w_ff2"], "nn", f"ff2_{tag}", tm=512, tn=1024, tk=4096, res=x1)
    saved = dict(x=x, hb=hb, z=z, yp=yp, yc=yc, qkv=qkv, o3=o3, lse3=lse3, merged=merged, x1=x1, h2b=h2b, rb=rb)
    return x2, saved, w


def _layer_bwd(dx2, w, s, tag, after=None, mid=None, tail=None):
    g = {}
    dab = _mm(dx2, w["w_ff2"], "nt", f"d_ff2_act_{tag}", tm=1024, tn=1024, tk=1024, out_dtype=BF, aux=s["rb"],
              epi="drelu2", after=after)
    g["w_ff2"] = _mm(s["rb"], dx2, "tn", f"d_ff2_w_{tag}", tm=1024, tn=1024, tk=2048, out_dtype=BF)
    g["w_ff1"] = _mm(s["h2b"], dab, "tn", f"d_ff1_w_{tag}", tm=1024, tn=1024, tk=2048, out_dtype=BF, out_shards=True)
    dx1, g["norm_mlp"] = _mm(dab, w["w_ff1"], "nt", f"d_ff1_act_{tag}", tm=1024, tn=1024, tk=1024, b_shards=True,
                             res=dx2, aux=s["x1"], vec=w["norm_mlp"], epi="rms_bwd")
    dm = _mm(dx1, w["w_o"], "nt", f"d_out_act_{tag}", tm=1024, tn=1024, tk=1024)
    g["w_o"] = _mm(s["merged"], dx1, "tn", f"d_out_w_{tag}", tm=1024, tn=1024, tk=1024, out_dtype=BF)
    (dz, dyp, dyc, do3, c3, g["w_pool_up"], g["w_conv_out"], g["w_attn_up"],
     g["b_gate"]) = _merge_bwd(dm, s["yp"], s["yc"], s["o3"], s["lse3"], s["z"], w["b_gate"], w["w_pool_up"],
                               w["w_conv_out"], w["w_attn_up"], f"d_merge_{tag}")
    behind = mid(g) if mid is not None else None
    dzq, dzk, dzv, dgq, dgk = _attn_bwd(s["z"], s["qkv"], do3, c3, s["lse3"], w["qk_gain"], f"d_attn_{tag}",
                                        after=behind)
    g["q_gain"] = dgq[:, :HEAD_DIM] + dgq[:, HEAD_DIM:]
    g["k_gain"] = dgk[:, :HEAD_DIM] + dgk[:, HEAD_DIM:]
    for off, piece in ((OFF_Q, dzq), (OFF_K, dzk), (OFF_V, dzv)):
        dz = lax.dynamic_update_slice(dz, piece, (0, off))
    dz, g["pool_mix"], g["pool_scale"], g["conv_w"] = _poolconv_bwd(
        s["z"], dyp, dyc, w["pool_mix"], w["pool_scale"], w["conv_w"], dz, f"d_poolconv_{tag}")
    g["w_in"] = _mm(s["hb"], dz, "tn", f"d_in_w_{tag}", tm=512, tn=3712, tk=1024, out_dtype=BF)
    dh = _mm(dz, w["w_in"], "nn", f"d_in_act_{tag}", tm=1024, tn=1024, tk=3712,
             after=tail(g) if tail is not None else None)
    dx, g["norm_mix"] = _rms_bwd(dh, s["x"], w["norm_mix"], dx1, f"d_rms_mix_{tag}")
    return dx, g


def _position():
    x, y, c = lax.axis_index("x"), lax.axis_index("y"), lax.axis_index("c")
    chips = [(1 - x, y), (x, 1 - y), (1 - x, 1 - y)]
    return x, y, c, 2 * x + y, chips, [2 * cx + cy for cx, cy in chips]


def _remote(src, dst, ssem, rsem, dev):
    return pltpu.make_async_remote_copy(src_ref=src, dst_ref=dst, send_sem=ssem, recv_sem=rsem, device_id=dev,
                                        device_id_type=MESH_ID)


def _halves(a):
    return a.reshape(a.shape[0], 2, a.shape[1] // 2, a.shape[2])


SEM = pl.BlockSpec(memory_space=pltpu.SEMAPHORE)
TOKEN = jax.ShapeDtypeStruct((8, LANES), F32)
TOKEN_SPEC = pl.BlockSpec(memory_space=pltpu.VMEM)


def _split_params():
    return pltpu.CompilerParams(has_side_effects=pltpu.SideEffectType.DATAFLOW_SIDE_EFFECTING)


def _gather_start(bufs, name, after):
    n = len(bufs)
    views = [_halves(b) for b in bufs]

    def body(*refs):
        first_sem = n + 1
        ssem, rsem = refs[first_sem:first_sem + ns], refs[first_sem + ns:first_sem + 2 * ns]
        outs, token = refs[first_sem + 2 * ns:first_sem + 2 * ns + n], refs[first_sem + 2 * ns + n]
        x, y, c, q, chips, qs = _position()
        for k in range(n):
            mine = outs[k].at[q, c]
            for j, chip in enumerate(chips):
                _remote(mine, mine, ssem[3 * k + j], rsem[3 * k + j], (chip[0], chip[1], c)).start()
        token[...] = jnp.zeros_like(token)

    ns = 3 * n
    outs = _pallas_call(
        body, name=name, in_specs=[ANY] * (n + 1), out_specs=[SEM] * (2 * ns) + [ANY] * n + [TOKEN_SPEC],
        out_shape=[pltpu.SemaphoreType.DMA(())] * (2 * ns) + [jax.ShapeDtypeStruct(v.shape, v.dtype) for v in views]
        + [TOKEN],
        input_output_aliases={k: k + 2 * ns for k in range(n)}, compiler_params=_split_params(),
    )(*views, after)
    return list(outs[:ns]), list(outs[ns:2 * ns]), list(outs[2 * ns:2 * ns + n]), outs[2 * ns + n]


def _gather_finish(ssem, rsem, views, after, name_wait, name_forward, shapes):
    n = len(views)
    ns = len(ssem)

    def wait_body(*refs):
        ssem_ref, rsem_ref = refs[n:n + ns], refs[n + ns:n + 2 * ns]
        outs = refs[n + 2 * ns + 1:]
        x, y, c, q, chips, qs = _position()
        for k in range(n):
            for j, chip in enumerate(chips):
                cp = _remote(outs[k].at[q, c], outs[k].at[qs[j], c], ssem_ref[3 * k + j], rsem_ref[3 * k + j],
                             (chip[0], chip[1], c))
                cp.wait_send()
                cp.wait_recv()

    landed = _pallas_call(
        wait_body, name=name_wait, in_specs=[ANY] * n + [SEM] * (2 * ns) + [ANY], out_specs=[ANY] * n,
        out_shape=[jax.ShapeDtypeStruct(v.shape, v.dtype) for v in views],
        input_output_aliases={k: k for k in range(n)}, compiler_params=_split_params(),
    )(*views, *ssem, *rsem, after)

    def forward_body(*refs):
        outs = refs[n:2 * n]
        fssem, frsem = refs[2 * n:]
        x, y, c, q, chips, qs = _position()
        sib = (x, y, 1 - c)
        sent = []
        for k in range(n):
            for j in range(3):
                slot = outs[k].at[qs[j], c]
                cp = _remote(slot, slot, fssem.at[k, j], frsem.at[k, j], sib)
                cp.start()
                sent.append(cp)
        for k in range(n):
            for j in range(3):
                slot = outs[k].at[qs[j], 1 - c]
                _remote(slot, slot, fssem.at[k, j], frsem.at[k, j], sib).wait_recv()
        for cp in sent:
            cp.wait_send()

    outs = _pallas_call(
        forward_body, name=name_forward, in_specs=[ANY] * n, out_specs=[ANY] * n,
        out_shape=[jax.ShapeDtypeStruct(v.shape, v.dtype) for v in views],
        input_output_aliases={k: k for k in range(n)}, scratch_shapes=[pltpu.SemaphoreType.DMA((n, 3))] * 2,
    )(*landed)
    return [o.reshape(s) for o, s in zip(outs, shapes)]


def _chip_exchange_start(parts, name):
    n = len(parts)

    def body(*refs):
        ssem, rsem = refs[n:n + ns], refs[n + ns:n + 2 * ns]
        base = n + 2 * ns
        srcs, outs, token = refs[base:base + n], refs[base + n:base + 2 * n], refs[base + 2 * n]
        x, y, c, q, chips, qs = _position()
        for k in range(n):
            for j, chip in enumerate(chips):
                _remote(srcs[k].at[qs[j]], outs[k].at[j], ssem[3 * k + j], rsem[3 * k + j],
                        (chip[0], chip[1], c)).start()
        token[...] = jnp.zeros_like(token)

    ns = 3 * n
    outs = _pallas_call(
        body, name=name, in_specs=[ANY] * n, out_specs=[SEM] * (2 * ns) + [ANY] * (2 * n) + [TOKEN_SPEC],
        out_shape=[pltpu.SemaphoreType.DMA(())] * (2 * ns) + [jax.ShapeDtypeStruct(a.shape, a.dtype) for a in parts]
        + [jax.ShapeDtypeStruct((3,) + a.shape[1:], a.dtype) for a in parts] + [TOKEN],
        input_output_aliases={k: k + 2 * ns for k in range(n)}, compiler_params=_split_params(),
    )(*parts)
    b = 2 * ns
    return list(outs[:ns]), list(outs[ns:b]), list(outs[b:b + n]), list(outs[b + n:b + 2 * n]), outs[b + 2 * n]


def _chip_exchange_wait(ssem, rsem, parts, landing, after, name):
    n = len(parts)
    ns = len(ssem)

    def body(*refs):
        ssem_ref, rsem_ref = refs[2 * n:2 * n + ns], refs[2 * n + ns:2 * n + 2 * ns]
        base = 2 * n + 2 * ns + 1
        srcs, outs = refs[base:base + n], refs[base + n:]
        x, y, c, q, chips, qs = _position()
        for k in range(n):
            for j, chip in enumerate(chips):
                cp = _remote(srcs[k].at[qs[j]], outs[k].at[j], ssem_ref[3 * k + j], rsem_ref[3 * k + j],
                             (chip[0], chip[1], c))
                cp.wait_send()
                cp.wait_recv()

    outs = _pallas_call(
        body, name=name, in_specs=[ANY] * (2 * n) + [SEM] * (2 * ns) + [ANY], out_specs=[ANY] * (2 * n),
        out_shape=[jax.ShapeDtypeStruct(a.shape, a.dtype) for a in list(parts) + list(landing)],
        input_output_aliases={k: k for k in range(2 * n)}, compiler_params=_split_params(),
    )(*parts, *landing, *ssem, *rsem, after)
    return list(outs[:n]), list(outs[n:])


def _pair_swap(views, name):
    n = len(views)

    def body(*refs):
        ins, outs = refs[:n], refs[n:2 * n]
        ssem, rsem = refs[2 * n:]
        x, y, c, _, _, _ = _position()
        cps = [_remote(ins[k].at[pl.ds(0, N_CHIPS), 1 - c], outs[k], ssem.at[k], rsem.at[k], (x, y, 1 - c))
               for k in range(n)]
        for cp in cps:
            cp.start()
        for cp in cps:
            cp.wait()

    return _pallas_call(
        body, name=name, in_specs=[ANY] * n, out_specs=[ANY] * n,
        out_shape=[jax.ShapeDtypeStruct((v.shape[0],) + v.shape[2:], v.dtype) for v in views],
        scratch_shapes=[pltpu.SemaphoreType.DMA((n,))] * 2,
    )(*views)


def _pair_send(arrays, name):
    n = len(arrays)

    def body(*refs):
        ins, outs = refs[:n], refs[n:2 * n]
        ssem, rsem = refs[2 * n:]
        x, y, c, _, _, _ = _position()
        cps = [_remote(ins[k], outs[k], ssem.at[k], rsem.at[k], (x, y, 1 - c)) for k in range(n)]
        for cp in cps:
            cp.start()
        for cp in cps:
            cp.wait()

    return _pallas_call(
        body, name=name, in_specs=[ANY] * n, out_specs=[ANY] * n,
        out_shape=[jax.ShapeDtypeStruct(a.shape, a.dtype) for a in arrays],
        scratch_shapes=[pltpu.SemaphoreType.DMA((n,))] * 2,
    )(*arrays)


def _all_to_all_small(part):
    P = part.shape[0]

    def body(in_ref, out_ref, lsem, ssem, rsem):
        x, y, c = lax.axis_index("x"), lax.axis_index("y"), lax.axis_index("c")
        me = 4 * x + 2 * y + c
        flips = [(fx, fy, fc) for fx in (0, 1) for fy in (0, 1) for fc in (0, 1)][1:]
        peers = [((x + fx) % 2, (y + fy) % 2, (c + fc) % 2) for fx, fy, fc in flips]
        loc = pltpu.make_async_copy(in_ref, out_ref.at[me], lsem)
        loc.start()
        cps = [_remote(in_ref, out_ref.at[me], ssem.at[j], rsem.at[j], peer) for j, peer in enumerate(peers)]
        for cp in cps:
            cp.start()
        for j, (px, py, pc) in enumerate(peers):
            _remote(in_ref, out_ref.at[4 * px + 2 * py + pc], ssem.at[j], rsem.at[j], peers[j]).wait_recv()
        for cp in cps:
            cp.wait_send()
        loc.wait()

    return _pallas_call(
        body, name="small_exchange", in_specs=[ANY], out_specs=ANY,
        out_shape=jax.ShapeDtypeStruct((8, P, LANES), F32),
        scratch_shapes=[pltpu.SemaphoreType.DMA(())] + [pltpu.SemaphoreType.DMA((7,))] * 2,
    )(part)


def _small_peers():
    x, y, c = lax.axis_index("x"), lax.axis_index("y"), lax.axis_index("c")
    flips = [(fx, fy, fc) for fx in (0, 1) for fy in (0, 1) for fc in (0, 1)][1:]
    peers = [((x + fx) % 2, (y + fy) % 2, (c + fc) % 2) for fx, fy, fc in flips]
    return 4 * x + 2 * y + c, peers


def _all_to_all_small_start(part, name):
    P = part.shape[0]
    me = 4 * lax.axis_index("x") + 2 * lax.axis_index("y") + lax.axis_index("c")
    landing = lax.dynamic_update_slice(jnp.zeros((8, P, LANES), F32), part[None], (me, 0, 0))

    def body(*refs):
        sems, src, land, token = refs[2:16], refs[16], refs[17], refs[18]
        me_, peers = _small_peers()
        for j, peer in enumerate(peers):
            _remote(src, land.at[me_], sems[j], sems[7 + j], peer).start()
        token[...] = jnp.zeros_like(token)

    outs = _pallas_call(
        body, name=name, in_specs=[ANY, ANY], out_specs=[SEM] * 14 + [ANY, ANY, TOKEN_SPEC],
        out_shape=[pltpu.SemaphoreType.DMA(())] * 14 + [jax.ShapeDtypeStruct(part.shape, F32),
                                                       jax.ShapeDtypeStruct((8, P, LANES), F32), TOKEN],
        input_output_aliases={0: 14, 1: 15}, compiler_params=_split_params(),
    )(part, landing)
    return list(outs[:7]), list(outs[7:14]), outs[14], outs[15], outs[16]


def _all_to_all_small_wait(ssem, rsem, part, landing, after, name):
    def body(*refs):
        sems, src, land = refs[2:16], refs[17], refs[18]
        _, peers = _small_peers()
        for j, (px, py, pc) in enumerate(peers):
            cp = _remote(src, land.at[4 * px + 2 * py + pc], sems[j], sems[7 + j], peers[j])
            cp.wait_send()
            cp.wait_recv()

    return _pallas_call(
        body, name=name, in_specs=[ANY, ANY] + [SEM] * 14 + [ANY], out_specs=[ANY, ANY],
        out_shape=[jax.ShapeDtypeStruct(part.shape, F32), jax.ShapeDtypeStruct(landing.shape, F32)],
        input_output_aliases={0: 0, 1: 1}, compiler_params=_split_params(),
    )(part, landing, *ssem, *rsem, after)[1]


def _row_tile(rows, width, n_arrays):
    t = rows
    while t % 2 == 0 and t > 8 and 2 * n_arrays * t * width * 4 > VMEM_LIMIT // 2:
        t //= 2
    return t


def _chip():
    return 2 * lax.axis_index("x") + lax.axis_index("y")


def _core():
    return lax.axis_index("c")


def _cast_place(w3, layer, name):
    _, r, c = w3.shape
    tr = _row_tile(r, c, 2)

    def body(w_ref, o_ref):
        o_ref[...] = w_ref[...].astype(BF)

    return _pallas_call(
        body, name=name, grid=(r // tr,), in_specs=[pl.BlockSpec((None, tr, c), lambda i: (layer, i, 0))],
        out_specs=pl.BlockSpec((None, tr, c), lambda i: (_chip(), i, 0)),
        out_shape=jax.ShapeDtypeStruct((N_CHIPS, r, c), BF), compiler_params=_params(("parallel",)),
    )(w3)


def _pair_sum(views, recvs, name):
    n = len(views)

    def body(*refs):
        for g_ref, r_ref, o_ref in zip(refs[:n], refs[n:2 * n], refs[2 * n:]):
            o_ref[...] = (g_ref[...].astype(F32) + r_ref[...].astype(F32)).astype(BF)

    own = [pl.BlockSpec((None, None) + v.shape[2:], lambda p: (p, _core(), 0, 0)) for v in views]
    blk = [pl.BlockSpec((None,) + r.shape[1:], lambda p: (p, 0, 0)) for r in recvs]
    return _pallas_call(
        body, name=name, grid=(N_CHIPS,), in_specs=own + blk, out_specs=blk,
        out_shape=[jax.ShapeDtypeStruct(r.shape, BF) for r in recvs], compiler_params=_params(("parallel",)),
    )(*views, *recvs)


CHIP_SUM_STEPS = 2


def _chip_sum(parts, recvs, name):
    n = len(parts)

    def body(*refs):
        for p_ref, r_ref, o_ref in zip(refs[:n], refs[n:2 * n], refs[2 * n:]):
            acc = p_ref[...].astype(F32)
            for j in range(3):
                acc = acc + r_ref[j].astype(F32)
            o_ref[...] = acc

    rows = [p.shape[1] // CHIP_SUM_STEPS for p in parts]
    return _pallas_call(
        body, name=name, grid=(CHIP_SUM_STEPS,),
        in_specs=[pl.BlockSpec((None, t, p.shape[2]), lambda i: (_chip(), i, 0)) for p, t in zip(parts, rows)]
        + [pl.BlockSpec((3, t, p.shape[2]), lambda i: (0, i, 0)) for p, t in zip(parts, rows)],
        out_specs=[pl.BlockSpec((t, p.shape[2]), lambda i: (i, 0)) for p, t in zip(parts, rows)],
        out_shape=[jax.ShapeDtypeStruct(p.shape[1:], F32) for p in parts], compiler_params=_params(("parallel",)),
    )(*parts, *recvs)


def _sum_slices(a, name):
    n, rows, width = a.shape
    tr = _row_tile(rows, width, n + 1)

    def body(a_ref, o_ref):
        acc = a_ref[0].astype(F32)
        for i in range(1, n):
            acc = acc + a_ref[i].astype(F32)
        o_ref[...] = acc

    return _pallas_call(
        body, name=name, grid=(rows // tr,), in_specs=[pl.BlockSpec((n, tr, width), lambda i: (0, i, 0))],
        out_specs=pl.BlockSpec((tr, width), lambda i: (i, 0)), out_shape=jax.ShapeDtypeStruct((rows, width), F32),
        compiler_params=_params(("parallel",)),
    )(a)


def _adamw_update(w, g, m, v):
    nm = ADAM_B1 * m + (1.0 - ADAM_B1) * g
    nv = ADAM_B2 * v + (1.0 - ADAM_B2) * (g * g)
    m_hat = nm / (1.0 - ADAM_B1 ** ADAM_STEP)
    v_hat = nv / (1.0 - ADAM_B2 ** ADAM_STEP)
    return -ADAM_LR * (m_hat / (jnp.sqrt(v_hat) + ADAM_EPS) + ADAM_WD * w), nm, nv


def _adamw(ws, gs, ms, vs, name):
    n = len(ws)

    def body(*refs):
        for k in range(n):
            w_ref, g_ref, m_ref, v_ref = (refs[s * n + k] for s in range(4))
            d_ref, nm_ref, nv_ref = (refs[(4 + s) * n + k] for s in range(3))
            d_ref[...], nm_ref[...], nv_ref[...] = _adamw_update(w_ref[...], g_ref[...], m_ref[...], v_ref[...])

    whole = [pl.BlockSpec(w.shape, lambda i: (0, 0)) for w in ws]
    outs = _pallas_call(
        body, name=name, grid=(1,), in_specs=whole * 4, out_specs=whole * 3,
        out_shape=[jax.ShapeDtypeStruct(w.shape, F32) for _ in range(3) for w in ws],
        compiler_params=_params(("arbitrary",)),
    )(*ws, *gs, *ms, *vs)
    return [[outs[s * n + k] for s in range(3)] for k in range(n)]


ADAMW_STEPS = 4


def _adamw_halves(ws, ms, vs, mine, other, name):
    n = len(ws)
    depth = ws[0].shape[0]
    assert depth == 2
    halves = [(w.shape[1] // 2, w.shape[2]) for w in ws]
    tiles = [hr // ADAMW_STEPS for hr, _ in halves]
    kinds = ((0, True), (0, False), (1, True), (1, False))

    def active(l, h, layer, own):
        mine_half = h == _core()
        return (l == layer) & (mine_half if own else jnp.logical_not(mine_half))

    def body(*refs):
        l, h = pl.program_id(0), pl.program_id(1)
        flags = [active(l, h, layer, own) for layer, own in kinds]
        for k in range(n):
            w_ref, m_ref, v_ref = refs[k], refs[n + k], refs[2 * n + k]
            g_refs = [refs[(3 + s) * n + k] for s in range(4)]
            go_ref, d_ref, nm_ref, nv_ref = (refs[(7 + s) * n + k] for s in range(4))
            for flag, g_ref in zip(flags, g_refs):
                @pl.when(flag)
                def _():
                    gv = g_ref[...]
                    go_ref[...] = gv
                    d_ref[...], nm_ref[...], nv_ref[...] = _adamw_update(w_ref[...], gv, m_ref[...], v_ref[...])

    def blk(k):
        return pl.BlockSpec((None, None, tiles[k], halves[k][1]), lambda l, h, i: (l, h, i, 0))

    def gspec(k, layer, own):
        return pl.BlockSpec((tiles[k], halves[k][1]), lambda l, h, i: (jnp.where(active(l, h, layer, own), i, 0), 0))

    def view(a, k):
        return a.reshape(depth, 2, halves[k][0], halves[k][1])

    blks = [blk(k) for k in range(n)]
    sources = [[(mine if own else other)[layer][k] for k in range(n)] for layer, own in kinds]
    outs = _pallas_call(
        body, name=name, grid=(depth, 2, ADAMW_STEPS),
        in_specs=blks * 3 + [gspec(k, layer, own) for layer, own in kinds for k in range(n)], out_specs=blks * 4,
        out_shape=[jax.ShapeDtypeStruct((depth, 2) + halves[k], F32) for _ in range(4) for k in range(n)],
        compiler_params=_params(("parallel", "parallel", "parallel")),
    )(*[view(a, k) for group in (ws, ms, vs) for k, a in enumerate(group)], *[g for src in sources for g in src])
    return [[outs[s * n + k].reshape(ws[k].shape) for s in range(4)] for k in range(n)]


BIG = ("w_in", "w_pool_up", "w_conv_out", "w_attn_up", "w_o", "w_ff1", "w_ff2")
SMALL = ("norm_mix", "b_gate", "pool_mix", "pool_scale", "conv_w", "q_gain", "k_gain", "norm_mlp")
ORDER = ("norm_mix", "w_in", "b_gate", "pool_mix", "pool_scale", "conv_w", "q_gain", "k_gain", "w_pool_up",
         "w_conv_out", "w_attn_up", "w_o", "norm_mlp", "w_ff1", "w_ff2")
COLUMN_SHARDED = ("w_pool_up", "w_conv_out", "w_attn_up", "w_ff1")


def _matrix_weights(gathered):
    w = {}
    for name, g4 in gathered.items():
        if name in COLUMN_SHARDED:
            w[name] = g4
        else:
            w[name] = g4.reshape(N_CHIPS * g4.shape[1], g4.shape[2])
    return w


def _small_weights(l, small):
    w = {}
    w["norm_mix"] = small["norm_mix"][l][None]
    w["norm_mlp"] = small["norm_mlp"][l][None]
    w["b_gate"] = small["b_gate"][l][None]
    w["pool_mix"] = small["pool_mix"][l].astype(BF)
    w["pool_scale"] = small["pool_scale"][l][None]
    w["conv_w"] = jnp.pad(small["conv_w_full"][l], ((0, 5), (0, 0)))
    w["qk_gain"] = jnp.pad(jnp.stack([jnp.tile(small["q_gain"][l], 2), jnp.tile(small["k_gain"][l], 2)]), ((0, 6), (0, 0)))
    return w


def _to_chip_major(name, g):
    if name == "w_in":
        return g.T.reshape(N_CHIPS, g.shape[1] // N_CHIPS, g.shape[0])
    if name in COLUMN_SHARDED:
        return g
    return g.reshape(N_CHIPS, g.shape[0] // N_CHIPS, g.shape[1])


def _pad8(a):
    a = a.reshape(-1)
    return jnp.pad(a, (0, (-a.size) % (8 * LANES))).reshape(-1, LANES)


def kernel(x, norm_mix, w_in, b_gate, pool_mix, pool_scale, conv_w, q_gain, k_gain, w_pool_up, w_conv_out, w_attn_up, w_o, norm_mlp, w_ff1, w_ff2, loss_target, m_norm_mix, m_w_in, m_b_gate, m_pool_mix, m_pool_scale, m_conv_w, m_q_gain, m_k_gain, m_w_pool_up, m_w_conv_out, m_w_attn_up, m_w_o, m_norm_mlp, m_w_ff1, m_w_ff2, v_norm_mix, v_w_in, v_b_gate, v_pool_mix, v_pool_scale, v_conv_w, v_q_gain, v_k_gain, v_w_pool_up, v_w_conv_out, v_w_attn_up, v_w_o, v_norm_mlp, v_w_ff1, v_w_ff2):
    weights = dict(norm_mix=norm_mix, w_in=w_in, b_gate=b_gate, pool_mix=pool_mix, pool_scale=pool_scale, conv_w=conv_w,
                   q_gain=q_gain, k_gain=k_gain, w_pool_up=w_pool_up, w_conv_out=w_conv_out, w_attn_up=w_attn_up,
                   w_o=w_o, norm_mlp=norm_mlp, w_ff1=w_ff1, w_ff2=w_ff2)
    moms = dict(norm_mix=m_norm_mix, w_in=m_w_in, b_gate=m_b_gate, pool_mix=m_pool_mix, pool_scale=m_pool_scale,
                conv_w=m_conv_w, q_gain=m_q_gain, k_gain=m_k_gain, w_pool_up=m_w_pool_up, w_conv_out=m_w_conv_out,
                w_attn_up=m_w_attn_up, w_o=m_w_o, norm_mlp=m_norm_mlp, w_ff1=m_w_ff1, w_ff2=m_w_ff2)
    vels = dict(norm_mix=v_norm_mix, w_in=v_w_in, b_gate=v_b_gate, pool_mix=v_pool_mix, pool_scale=v_pool_scale,
                conv_w=v_conv_w, q_gain=v_q_gain, k_gain=v_k_gain, w_pool_up=v_w_pool_up, w_conv_out=v_w_conv_out,
                w_attn_up=v_w_attn_up, w_o=v_w_o, norm_mlp=v_norm_mlp, w_ff1=v_w_ff1, w_ff2=v_w_ff2)
    depth = norm_mix.shape[0]
    q = 2 * lax.axis_index("x") + lax.axis_index("y")
    for group in (weights, moms, vels):
        group["w_in"] = jnp.swapaxes(group["w_in"], 1, 2)

    assert depth == 2, "the second layer's gather hides behind the first layer's forward, and likewise backward"
    first, rest = BIG[:1], BIG[1:]
    cw_all = _all_to_all_small(_pad8(conv_w))
    bufs = [{n: _cast_place(weights[n], 0, f"cast_{n}_l0") for n in first}]
    a_ssem, a_rsem, a_views, a_token = _gather_start([bufs[0][n] for n in first], "gather_start_l0_in", cw_all)
    bufs[0].update({n: _cast_place(weights[n], 0, f"cast_{n}_l0") for n in rest})
    bufs += [{n: _cast_place(weights[n], l, f"cast_{n}_l{l}") for n in BIG} for l in range(1, depth)]
    b_ssem, b_rsem, b_views, b_token = _gather_start([bufs[0][n] for n in rest], "gather_start_l0_rest", a_token)
    g_ssem, g_rsem, g_views, g_token = _gather_start([bufs[1][n] for n in BIG], "gather_start_l1", b_token)
    conv_w_full = jnp.concatenate(
        [cw_all[2 * p].reshape(-1)[:conv_w.size].reshape(conv_w.shape) for p in range(N_CHIPS)], axis=-1)
    small = dict(weights)
    small["conv_w_full"] = conv_w_full

    def soon_weights(t):
        got = _gather_finish(a_ssem, a_rsem, a_views, t, "gather_wait_l0_in", "gather_forward_l0_in",
                             [bufs[0][n].shape for n in first])
        return _matrix_weights(dict(zip(first, got)))

    def late_weights(t):
        got = _gather_finish(b_ssem, b_rsem, b_views, t, "gather_wait_l0_rest", "gather_forward_l0_rest",
                             [bufs[0][n].shape for n in rest])
        return _matrix_weights(dict(zip(rest, got)))

    wl, saved = [None] * depth, [None] * depth
    small_1 = _small_weights(1, small)
    (h, hb_1), saved[0], wl[0] = _layer_fwd(x[0], _small_weights(0, small), "l0", after=g_token, soon=soon_weights,
                                            late=late_weights, next_gain=small_1["norm_mix"])
    got = _gather_finish(g_ssem, g_rsem, g_views, h, "gather_wait_l1", "gather_forward_l1",
                         [bufs[1][n].shape for n in BIG])
    (dh, loss_row), saved[1], wl[1] = _layer_fwd(
        h, dict(small_1, **_matrix_weights(dict(zip(BIG, got)))), "l1", target=loss_target[0], hb=hb_1)

    def pair_stage(names, g, tag):
        views = [_halves(_to_chip_major(n, g[n])) for n in names]
        from_sibling = _pair_swap(views, f"grad_pair_swap_{tag}")
        return _pair_sum(views, from_sibling, f"pair_sum_{tag}")

    mine, other = [{}, {}], [{}, {}]

    def finish(names, l, started, after, tag):
        ssem, rsem, parts, landing, _ = started
        parts, arrived = _chip_exchange_wait(ssem, rsem, parts, landing, after, f"grad_chip_exchange_wait_{tag}")
        got = _chip_sum(parts, arrived, f"chip_sum_{tag}")
        mine[l].update(zip(names, got))
        other[l].update(zip(names, _pair_send(got, f"grad_pair_send_{tag}")))

    def small_pieces(g):
        return [_pad8(g[n][:3] if n == "conv_w" else g[n]) for n in SMALL]

    def start_small(l):
        pieces = small_pieces(grads[l]) + ([_pad8(loss_row)] if l == depth - 1 else [])
        return _all_to_all_small_start(jnp.concatenate(pieces, axis=0), f"small_grad_exchange_start_l{l}")

    grads, early, small = [None] * depth, {}, [None] * depth
    dh, grads[1] = _layer_bwd(dh, wl[1], saved[1], "l1")
    second = _chip_exchange_start(pair_stage(BIG, grads[1], "l1"), "grad_chip_exchange_start_l1")
    small[1] = start_small(1)

    def start_rest(g):
        early["rest"] = _chip_exchange_start(pair_stage(rest, g, "l0_rest"), "grad_chip_exchange_start_l0_rest")
        return early["rest"][4]

    def start_last(g):
        early["in"] = _chip_exchange_start(pair_stage(first, g, "l0_in"), "grad_chip_exchange_start_l0_in")
        return early["in"][4]

    dh, grads[0] = _layer_bwd(dh, wl[0], saved[0], "l0", after=[second[4], small[1][4]], mid=start_rest,
                              tail=start_last)
    small[0] = start_small(0)
    finish(BIG, 1, second, dh, "l1")
    finish(rest, 0, early["rest"], dh, "l0_rest")
    full = {}

    deltas, new_m, new_v = {}, {}, {}

    def update_matrices(names, tag):
        results = _adamw_halves(
            [weights[n] for n in names], [moms[n] for n in names], [vels[n] for n in names],
            [[mine[l][n] for n in names] for l in range(depth)], [[other[l][n] for n in names] for l in range(depth)],
            f"adamw_{tag}")
        for n, (g_, d_, m_, v_) in zip(names, results):
            full[n], deltas[n], new_m[n], new_v[n] = g_, d_, m_, v_

    update_matrices(rest, "rest")
    finish(first, 0, early["in"], deltas[rest[-1]], "l0_in")
    update_matrices(first, "in")
    summed = []
    for l in range(depth):
        ssem, rsem, part, landing, _ = small[l]
        summed.append(_sum_slices(_all_to_all_small_wait(ssem, rsem, part, landing, deltas[first[-1]],
                                                         f"small_grad_exchange_wait_l{l}"), f"small_sum_l{l}"))
    row = 0
    for n, piece in zip(SMALL, small_pieces(grads[0])):
        size = (weights[n].size if n != "conv_w" else depth * 3 * 512) // depth
        flat = jnp.stack([s[row:row + piece.shape[0]].reshape(-1)[:size] for s in summed])
        row += piece.shape[0]
        if n == "conv_w":
            full[n] = lax.dynamic_slice_in_dim(flat.reshape(depth, 3, 512), q * conv_w.shape[2], conv_w.shape[2], axis=2)
        else:
            full[n] = flat.reshape(weights[n].shape)
    loss = summed[depth - 1][row, 0]
    two_d = {n: (-1, weights[n].shape[-1]) if n not in ("conv_w", "q_gain", "k_gain") else (1, -1) for n in SMALL}
    results = _adamw(*[[group[n].reshape(two_d[n]) for n in SMALL] for group in (weights, full, moms, vels)],
                     "adamw_small")
    for n, (d2, m2, v2) in zip(SMALL, results):
        shape = weights[n].shape
        deltas[n], new_m[n], new_v[n] = d2.reshape(shape), m2.reshape(shape), v2.reshape(shape)
        full[n] = full[n].reshape(shape)
    for group in (full, deltas, new_m, new_v):
        group["w_in"] = jnp.swapaxes(group["w_in"], 1, 2)
    return (loss, dh[None], *[full[n] for n in ORDER], *[deltas[n] for n in ORDER], *[new_m[n] for n in ORDER],
            *[new_v[n] for n in ORDER])
```

```python
import functools

import jax
import jax.numpy as jnp
from jax import lax
from jax.experimental import pallas as pl
from jax.experimental.pallas import tpu as pltpu

F32 = jnp.float32
BF = jnp.bfloat16
MESH_ID = pl.DeviceIdType.MESH
ANY = pl.BlockSpec(memory_space=pl.ANY)

EPS = 1e-6
MASK_VALUE = -1e30
POOL_WINDOWS = (2, 4, 8, 16)
ATTN_DILATIONS = (1, 4, 16)
ATTN_BLOCK = 128
HEAD_DIM = 64
OFF_Q, OFF_K, OFF_V, OFF_GATE = 2048, 2816, 3584, 4352
N_CHIPS = 4
ADAM_LR, ADAM_B1, ADAM_B2, ADAM_EPS, ADAM_WD, ADAM_STEP = 0.001, 0.9, 0.999, 1e-08, 0.01, 10

VMEM_LIMIT = 48 * 1024 * 1024
LANES = 128

_DIMS = {"nn": (((1,), (0,)), ((), ())), "nt": (((1,), (1,)), ((), ())), "tn": (((0,), (0,)), ((), ()))}


def _params(sem):
    return pltpu.CompilerParams(dimension_semantics=sem, vmem_limit_bytes=VMEM_LIMIT)


def _pallas_call(body, **kw):
    def in_hbm(s):
        pin = isinstance(s, jax.ShapeDtypeStruct) and s is not TOKEN and jnp.issubdtype(s.dtype, jnp.floating)
        return pltpu.HBM(s.shape, s.dtype) if pin else s

    out_shape = kw.pop("out_shape")
    kw["out_shape"] = [in_hbm(s) for s in out_shape] if isinstance(out_shape, (list, tuple)) else in_hbm(out_shape)
    call = pl.pallas_call(body, **kw)

    def run(*args):
        pinned = [pltpu.with_memory_space_constraint(a, pltpu.HBM)
                  if hasattr(a, "dtype") and jnp.issubdtype(a.dtype, jnp.floating) else a for a in args]
        return call(*pinned)

    return run


def _dot(a, b, mode="nn"):
    return lax.dot_general(a, b, _DIMS[mode], preferred_element_type=F32)


def _mm(a, b, mode, name, *, tm, tn, tk, out_dtype=F32, res=None, aux=None, epi=None, n_outer=False,
        b_shards=False, out_shards=False, after=None, vec=None):
    if mode == "tn":
        K, M = a.shape
    else:
        M, K = a.shape
    if b_shards:
        if mode == "nn":
            assert b.shape[1] == K
            N = b.shape[2] * N_CHIPS
        else:
            assert mode == "nt"
            N = b.shape[1]
            assert b.shape[2] * N_CHIPS == K
    else:
        N = b.shape[0] if mode == "nt" else b.shape[1]
    tm, tn, tk = min(tm, M), min(tn, N), min(tk, K)
    assert M % tm == 0 and N % tn == 0 and K % tk == 0
    nk = K // tk
    if n_outer:
        grid = (N // tn, M // tm, nk)
        ij = lambda p, q_: (q_, p)
    else:
        grid = (M // tm, N // tn, nk)
        ij = lambda p, q_: (p, q_)

    def amap(p, q_, k):
        i, j = ij(p, q_)
        return (k, i) if mode == "tn" else (i, k)

    a_spec = pl.BlockSpec((tk, tm) if mode == "tn" else (tm, tk), amap)
    if b_shards:
        if mode == "nn":
            per = (N // N_CHIPS) // tn
            assert per >= 1 and (N // N_CHIPS) % tn == 0

            def bmap(p, q_, k):
                i, j = ij(p, q_)
                return (j // per, k, j % per)

            b_spec = pl.BlockSpec((None, tk, tn), bmap)
        else:
            per = (K // N_CHIPS) // tk
            assert per >= 1 and (K // N_CHIPS) % tk == 0

            def bmap(p, q_, k):
                i, j = ij(p, q_)
                return (k // per, j, k % per)

            b_spec = pl.BlockSpec((None, tn, tk), bmap)
    else:
        def bmap(p, q_, k):
            i, j = ij(p, q_)
            return (j, k) if mode == "nt" else (k, j)

        b_spec = pl.BlockSpec((tn, tk) if mode == "nt" else (tk, tn), bmap)

    def omap(p, q_, k):
        return ij(p, q_)

    o_spec = pl.BlockSpec((tm, tn), omap)
    if out_shards:
        per_o = (N // N_CHIPS) // tn
        assert per_o >= 1 and (N // N_CHIPS) % tn == 0

        def osmap(p, q_, k):
            i, j = ij(p, q_)
            return (j // per_o, i, j % per_o)

        out_spec0 = pl.BlockSpec((None, tm, tn), osmap)
        out_shape0 = jax.ShapeDtypeStruct((N_CHIPS, M, N // N_CHIPS), out_dtype)
    else:
        out_spec0 = o_spec
        out_shape0 = jax.ShapeDtypeStruct((M, N), out_dtype)

    in_specs = [a_spec, b_spec]
    args = [a, b]
    if res is not None:
        in_specs.append(o_spec)
        args.append(res)
    if aux is not None:
        in_specs.append(o_spec)
        args.append(aux)
    if vec is not None:
        in_specs.append(pl.BlockSpec((1, tn), lambda p, q_, k: (0, ij(p, q_)[1])))
        args.append(vec)
    after = [] if after is None else list(after) if isinstance(after, (list, tuple)) else [after]
    in_specs += [ANY] * len(after)
    args += after
    out_specs = [out_spec0]
    out_shape = [out_shape0]
    reduces = epi in ("loss", "rms_bwd")
    if reduces:
        assert tn == N and not n_outer and not out_shards
        width = LANES if epi == "loss" else N
        out_specs.append(pl.BlockSpec((1, width), lambda p, q_, k: (0, 0)))
        out_shape.append(jax.ShapeDtypeStruct((1, width), F32))
    if epi == "rms_next":
        assert tn == N and not out_shards
        out_specs.append(o_spec)
        out_shape.append(jax.ShapeDtypeStruct((M, N), BF))
    n_out = len(out_shape)
    has_res, has_aux, has_vec, n_after = res is not None, aux is not None, vec is not None, len(after)

    def body(*refs):
        a_ref, b_ref = refs[0], refs[1]
        pos = 2
        res_ref = aux_ref = vec_ref = None
        if has_res:
            res_ref = refs[pos]
            pos += 1
        if has_aux:
            aux_ref = refs[pos]
            pos += 1
        if has_vec:
            vec_ref = refs[pos]
            pos += 1
        pos += n_after
        outs = refs[pos:pos + n_out]
        part = _dot(a_ref[...].astype(BF), b_ref[...].astype(BF), mode)

        first_row_tile = pl.program_id(0) == 0

        def add_to_sum(row):
            @pl.when(first_row_tile)
            def _():
                outs[1][...] = jnp.zeros_like(outs[1])

            outs[1][...] += row

        def finish(acc):
            if epi == "rms_bwd":
                xv = aux_ref[...]
                r = lax.rsqrt(jnp.mean(xv * xv, axis=-1, keepdims=True) + EPS)
                xhat = xv * r
                dy = acc * vec_ref[...]
                outs[0][...] = res_ref[...] + r * (dy - xhat * jnp.mean(dy * xhat, axis=-1, keepdims=True))
                add_to_sum(jnp.sum(acc * xhat, axis=0, keepdims=True))
                return
            if res_ref is not None:
                acc = res_ref[...] + acc
            if epi == "relu2":
                r = jnp.maximum(acc, 0.0)
                outs[0][...] = (r * r).astype(out_dtype)
            elif epi == "drelu2":
                outs[0][...] = (acc.astype(BF) * (2.0 * jnp.sqrt(aux_ref[...]))).astype(out_dtype)
            elif epi == "rms_next":
                outs[0][...] = acc
                r = lax.rsqrt(jnp.mean(acc * acc, axis=-1, keepdims=True) + EPS)
                outs[1][...] = ((acc * r) * vec_ref[...]).astype(BF)
            elif epi == "loss":
                e = acc - aux_ref[...]
                outs[0][...] = e / float(N)
                add_to_sum(0.5 * jnp.sum(jnp.mean(e * e, axis=-1, keepdims=True)))
            else:
                outs[0][...] = acc.astype(out_dtype)

        if nk == 1:
            finish(part)
        else:
            acc_ref = refs[pos + n_out]
            k = pl.program_id(2)

            @pl.when(k == 0)
            def _():
                acc_ref[...] = part

            @pl.when(k > 0)
            def _():
                acc_ref[...] += part

            @pl.when(k == nk - 1)
            def _():
                finish(acc_ref[...])

    scratch = [pltpu.VMEM((tm, tn), F32)] if nk > 1 else []
    out = _pallas_call(
        body, name=name, grid=grid, in_specs=in_specs, out_specs=out_specs, out_shape=out_shape,
        scratch_shapes=scratch,
        compiler_params=_params(("arbitrary" if reduces else "parallel", "parallel", "arbitrary")),
    )(*args)
    return out if n_out > 1 else out[0]


def _rms_fwd(x, gain, name, after=None):
    T, D = x.shape
    tm = min(512, T)

    def body(x_ref, g_ref, *rest):
        o_ref = rest[-1]
        xv = x_ref[...]
        r = lax.rsqrt(jnp.mean(xv * xv, axis=-1, keepdims=True) + EPS)
        o_ref[...] = ((xv * r) * g_ref[...]).astype(BF)

    extra = [] if after is None else list(after) if isinstance(after, (list, tuple)) else [after]
    return _pallas_call(
        body, name=name, grid=(T // tm,),
        in_specs=[pl.BlockSpec((tm, D), lambda i: (i, 0)), pl.BlockSpec((1, D), lambda i: (0, 0))] + [ANY] * len(extra),
        out_specs=pl.BlockSpec((tm, D), lambda i: (i, 0)), out_shape=jax.ShapeDtypeStruct((T, D), BF),
        compiler_params=_params(("parallel",)),
    )(x, gain, *extra)


def _rms_bwd(dh, x, gain, dres, name):
    T, D = x.shape
    tm = min(512, T)

    def body(dh_ref, x_ref, g_ref, dres_ref, dx_ref, dg_ref):
        xv = x_ref[...]
        r = lax.rsqrt(jnp.mean(xv * xv, axis=-1, keepdims=True) + EPS)
        xhat = xv * r
        dhv = dh_ref[...]
        dy = dhv * g_ref[...]
        dx_ref[...] = dres_ref[...] + r * (dy - xhat * jnp.mean(dy * xhat, axis=-1, keepdims=True))

        @pl.when(pl.program_id(0) == 0)
        def _():
            dg_ref[...] = jnp.zeros_like(dg_ref)

        dg_ref[...] += jnp.sum(dhv * xhat, axis=0, keepdims=True)

    row = pl.BlockSpec((tm, D), lambda i: (i, 0))
    vec = pl.BlockSpec((1, D), lambda i: (0, 0))
    return _pallas_call(
        body, name=name, grid=(T // tm,), in_specs=[row, row, vec, row], out_specs=[row, vec],
        out_shape=[jax.ShapeDtypeStruct((T, D), F32), jax.ShapeDtypeStruct((1, D), F32)],
        compiler_params=_params(("arbitrary",)),
    )(dh, x, gain, dres)


POOL_HALO = 16
CONV_HALO = 8
POOLCONV_ROWS = 512


def _causal_window_sum(v, w):
    s, sh = v, 1
    while sh < w:
        s = s + pltpu.roll(s, sh, 0)
        sh *= 2
    return s


def _anticausal_window_sum(v, w):
    n = v.shape[0]
    s, sh = v, 1
    while sh < w:
        s = s + pltpu.roll(s, n - sh, 0)
        sh *= 2
    return s


def _poolconv_fwd(z, pmix_b, pscale, convw, name):
    T = z.shape[0]
    R = min(POOLCONV_ROWS, T)
    PH, CH = R // POOL_HALO, R // CONV_HALO

    def body(u_ref, uh_ref, b_ref, c_ref, ch_ref, x_ref, xh_ref, mix_ref, sc_ref, cw_ref, yp_ref, yc_ref):
        i = pl.program_id(0)
        keep = (i > 0).astype(F32)
        row = i * R + lax.broadcasted_iota(jnp.int32, (R, 1), 0)
        w_all = jnp.concatenate([uh_ref[...] * keep, u_ref[...]], axis=0)
        for g, w in enumerate(POOL_WINDOWS):
            cols = slice(128 * g, 128 * (g + 1))
            wg = w_all[:, cols]
            s = _causal_window_sum(wg, w)[POOL_HALO:]
            inv_cnt = 1.0 / jnp.minimum(row + 1, w).astype(F32)
            dgrp = s * inv_cnt - wg[POOL_HALO:]
            y = _dot(dgrp.astype(BF), mix_ref[g]) * sc_ref[:, cols]
            yp_ref[:, cols] = y.astype(BF)
        uc = jnp.concatenate([ch_ref[...] * xh_ref[...] * keep, c_ref[...] * x_ref[...]], axis=0)
        yc = cw_ref[2:3, :] * uc + cw_ref[0:1, :] * pltpu.roll(uc, 2, 0) + cw_ref[1:2, :] * pltpu.roll(uc, 1, 0)
        yc_ref[...] = (b_ref[...] * yc[CONV_HALO:]).astype(BF)

    def main(cb):
        return pl.BlockSpec((R, 512), lambda i: (i, cb))

    def prev(cb, halo, per):
        return pl.BlockSpec((halo, 512), lambda i: (jnp.maximum(i * per - 1, 0), cb))

    full = lambda a: pl.BlockSpec(a.shape, lambda i: (0,) * a.ndim)
    return _pallas_call(
        body, name=name, grid=(T // R,),
        in_specs=[main(0), prev(0, POOL_HALO, PH), main(1), main(2), prev(2, CONV_HALO, CH), main(3),
                  prev(3, CONV_HALO, CH), full(pmix_b), full(pscale), full(convw)],
        out_specs=[pl.BlockSpec((R, 512), lambda i: (i, 0))] * 2,
        out_shape=[jax.ShapeDtypeStruct((T, 512), BF)] * 2,
        compiler_params=_params(("parallel",)),
    )(z, z, z, z, z, z, z, pmix_b, pscale, convw)


def _poolconv_bwd(z, dyp, dyc, pmix_b, pscale, convw, dz, name):
    T = z.shape[0]
    R = min(POOLCONV_ROWS, T)
    PH, CH = R // POOL_HALO, R // CONV_HALO
    nsteps = T // R

    def body(u_ref, uh_ref, b_ref, bn_ref, c_ref, ch_ref, x_ref, xh_ref, dyp_ref, dypn_ref, dyc_ref, dycn_ref,
             mix_ref, sc_ref, cw_ref, dz_in_ref, dz_ref, dmix_ref, dsc_ref, dcw_ref):
        i = pl.program_id(0)
        keep_prev = (i > 0).astype(F32)
        keep_next = (i < nsteps - 1).astype(F32)

        @pl.when(i == 0)
        def _():
            dmix_ref[...] = jnp.zeros_like(dmix_ref)
            dsc_ref[...] = jnp.zeros_like(dsc_ref)
            dcw_ref[...] = jnp.zeros_like(dcw_ref)

        row = i * R + lax.broadcasted_iota(jnp.int32, (R, 1), 0)
        row_ext = i * R + lax.broadcasted_iota(jnp.int32, (R + POOL_HALO, 1), 0)
        w_all = jnp.concatenate([uh_ref[...] * keep_prev, u_ref[...]], axis=0)
        dyp_ext = jnp.concatenate([dyp_ref[...], dypn_ref[...] * keep_next], axis=0)
        for g, w in enumerate(POOL_WINDOWS):
            cols = slice(128 * g, 128 * (g + 1))
            wg = w_all[:, cols]
            s = _causal_window_sum(wg, w)[POOL_HALO:]
            inv_cnt = 1.0 / jnp.minimum(row + 1, w).astype(F32)
            dgrp = (s * inv_cnt - wg[POOL_HALO:]).astype(BF)
            y_pre = _dot(dgrp, mix_ref[g])
            dsc_ref[:, cols] += jnp.sum(dyp_ref[:, cols] * y_pre, axis=0, keepdims=True)
            dyb = (dyp_ext[:, cols] * sc_ref[:, cols]).astype(BF)
            dmix_ref[cols, :] += _dot(dgrp, dyb[:R], "tn")
            dd = _dot(dyb, mix_ref[g], "nt")
            inv_cnt_ext = 1.0 / jnp.minimum(row_ext + 1, w).astype(F32)
            e = _anticausal_window_sum(dd * inv_cnt_ext, w)
            dz_ref[:, cols] = (e[:R] - dd[:R]).astype(BF)
        cw0, cw1, cw2 = cw_ref[0:1, :], cw_ref[1:2, :], cw_ref[2:3, :]
        uc = jnp.concatenate([ch_ref[...] * xh_ref[...] * keep_prev, c_ref[...] * x_ref[...]], axis=0)
        uc1 = pltpu.roll(uc, 1, 0)[CONV_HALO:]
        uc2 = pltpu.roll(uc, 2, 0)[CONV_HALO:]
        uc0 = uc[CONV_HALO:]
        yc = cw2 * uc0 + cw0 * uc2 + cw1 * uc1
        dycv = dyc_ref[...]
        dz_ref[:, 512:1024] = (dycv * yc).astype(BF)
        dv_ext = jnp.concatenate([dycv * b_ref[...], dycn_ref[...] * bn_ref[...] * keep_next], axis=0)
        n_ext = R + CONV_HALO
        duc = (cw2 * dv_ext + cw1 * pltpu.roll(dv_ext, n_ext - 1, 0) + cw0 * pltpu.roll(dv_ext, n_ext - 2, 0))[:R]
        dv = dv_ext[:R]
        dcw_ref[0:1, :] += jnp.sum(dv * uc2, axis=0, keepdims=True)
        dcw_ref[1:2, :] += jnp.sum(dv * uc1, axis=0, keepdims=True)
        dcw_ref[2:3, :] += jnp.sum(dv * uc0, axis=0, keepdims=True)
        dz_ref[:, 1024:1536] = (duc * x_ref[...]).astype(BF)
        dz_ref[:, 1536:2048] = (duc * c_ref[...]).astype(BF)

    def main(cb):
        return pl.BlockSpec((R, 512), lambda i: (i, cb))

    def prev(cb, halo, per):
        return pl.BlockSpec((halo, 512), lambda i: (jnp.maximum(i * per - 1, 0), cb))

    def nxt(cb, halo, per):
        return pl.BlockSpec((halo, 512), lambda i: (jnp.minimum((i + 1) * per, T // halo - 1), cb))

    full = lambda a: pl.BlockSpec(a.shape, lambda i: (0,) * a.ndim)
    return _pallas_call(
        body, name=name, grid=(nsteps,),
        in_specs=[main(0), prev(0, POOL_HALO, PH), main(1), nxt(1, CONV_HALO, CH), main(2), prev(2, CONV_HALO, CH),
                  main(3), prev(3, CONV_HALO, CH), main(0), nxt(0, POOL_HALO, PH), main(0), nxt(0, CONV_HALO, CH),
                  full(pmix_b), full(pscale), full(convw), ANY],
        out_specs=[pl.BlockSpec((R, 2048), lambda i: (i, 0)), pl.BlockSpec((512, 128), lambda i: (0, 0)),
                   pl.BlockSpec((1, 512), lambda i: (0, 0)), pl.BlockSpec((8, 512), lambda i: (0, 0))],
        out_shape=[jax.ShapeDtypeStruct(dz.shape, BF), jax.ShapeDtypeStruct((512, 128), F32),
                   jax.ShapeDtypeStruct((1, 512), F32), jax.ShapeDtypeStruct((8, 512), F32)],
        input_output_aliases={15: 0}, compiler_params=_params(("arbitrary",)),
    )(z, z, z, z, z, z, z, z, dyp, dyp, dyc, dyc, pmix_b, pscale, convw, dz)


def _head_sums(v):
    row = lax.broadcasted_iota(jnp.int32, (LANES, LANES), 0) < HEAD_DIM
    col = lax.broadcasted_iota(jnp.int32, (LANES, LANES), 1) < HEAD_DIM
    same_head = jnp.where(jnp.logical_xor(row, col), 0.0, 1.0).astype(BF)
    hi = v.astype(BF)
    lo = (v - hi.astype(F32)).astype(BF)
    return _dot(hi, same_head) + _dot(lo, same_head)


def _head_norm(x, g2, ma):
    r = lax.rsqrt(_head_sums(x * x) / HEAD_DIM + EPS)
    return x * r, r


def _head_norm_bwd(dy, xhat, r, g2, ma):
    dxh = dy * g2
    return r * (dxh - xhat * (_head_sums(dxh * xhat) / HEAD_DIM))


def _attn_masks(other_block_exists):
    lane = lax.broadcasted_iota(jnp.int32, (2 * ATTN_BLOCK, ATTN_BLOCK), 1)
    qi = lax.broadcasted_iota(jnp.int32, (2 * ATTN_BLOCK, ATTN_BLOCK), 0) & (ATTN_BLOCK - 1)
    never = (1 - other_block_exists.astype(jnp.int32)) * (2 * ATTN_BLOCK)
    return lane[:ATTN_BLOCK] < HEAD_DIM, lane <= qi, lane >= qi + never


def _stack_heads(x, ma):
    return jnp.concatenate([jnp.where(ma, x, 0.0), jnp.where(ma, 0.0, x)], axis=0)


def _unstack_heads(y, ma):
    return jnp.where(ma, y[:ATTN_BLOCK], y[ATTN_BLOCK:])


def _stack_cols(tile, ma):
    return jnp.concatenate([tile[:, 0:1], tile[:, HEAD_DIM:HEAD_DIM + 1]], axis=0)


QKV_TILES = (OFF_GATE - OFF_Q) // LANES
KIND_TILES = QKV_TILES // 3


def _qk_norm(z, gains, name):
    T = z.shape[0]
    tm = min(512, T)

    def body(x_ref, g_ref, o_ref):
        ma = lax.broadcasted_iota(jnp.int32, (tm, LANES), 1) < HEAD_DIM
        for tile in range(QKV_TILES):
            v = x_ref[:, LANES * tile:LANES * (tile + 1)]
            if tile < 2 * KIND_TILES:
                g = g_ref[0:1, :] if tile < KIND_TILES else g_ref[1:2, :]
                v = _head_norm(v, g, ma)[0] * g
            o_ref[tile] = v

    return _pallas_call(
        body, name=name, grid=(T // tm,),
        in_specs=[pl.BlockSpec((pl.Element(tm), pl.Element(OFF_GATE - OFF_Q)), lambda i: (i * tm, OFF_Q)),
                  pl.BlockSpec((8, LANES), lambda i: (0, 0))],
        out_specs=pl.BlockSpec((QKV_TILES, tm, LANES), lambda i: (0, i, 0)),
        out_shape=jax.ShapeDtypeStruct((QKV_TILES, T, LANES), F32), compiler_params=_params(("parallel",)),
    )(z, gains)


ATTN_STEP_ROWS = 2048
ATTN_UNROLL = 4


def _attn_steps(T):
    assert ATTN_STEP_ROWS == ATTN_BLOCK * max(ATTN_DILATIONS) and T % ATTN_STEP_ROWS == 0
    return T // ATTN_STEP_ROWS


def _attn_rows(jj, r, sub, d):
    start = jj * sub + r
    if d == 1:
        return pl.ds(pl.multiple_of(start, ATTN_BLOCK), ATTN_BLOCK)
    return pl.ds(start, ATTN_BLOCK, stride=d)


def _pick(flag, a, b):
    return jnp.where(jnp.full(a.shape, flag.astype(jnp.int32)) > 0, a, b)


def _attn_fwd(qkv, name):
    T = qkv.shape[1]
    nbig = _attn_steps(T)
    scale = HEAD_DIM ** -0.5

    def body(q_ref, kc_ref, kp_ref, vc_ref, vp_ref, o_ref, lse_ref):
        jb = pl.program_id(1)
        for gi, d in enumerate(ATTN_DILATIONS):
            pl.when(pl.program_id(0) == gi)(functools.partial(group, d, jb, q_ref, kc_ref, kp_ref, vc_ref, vp_ref,
                                                              o_ref, lse_ref))

    def group(d, jb, q_ref, kc_ref, kp_ref, vc_ref, vp_ref, o_ref, lse_ref):
        sub, m = ATTN_BLOCK * d, ATTN_STEP_ROWS // (ATTN_BLOCK * d)

        def step(s, carry):
            jj, r = s // d, s % d
            here, before = _attn_rows(jj, r, sub, d), _attn_rows(jnp.maximum(jj - 1, 0), r, sub, d)
            edge = _attn_rows(m - 1, r, sub, d)
            first = jj == 0
            ma, mask_c, mask_p = _attn_masks(jb * m + jj > 0)
            qs = _stack_heads(q_ref[here, :], ma).astype(BF)
            kcb = kc_ref[here, :].astype(BF)
            kpb = _pick(first, kp_ref[edge, :], kc_ref[before, :]).astype(BF)
            vcb = vc_ref[here, :].astype(BF)
            vpb = _pick(first, vp_ref[edge, :], vc_ref[before, :]).astype(BF)
            s_c = jnp.where(mask_c, _dot(qs, kcb, "nt") * scale, MASK_VALUE)
            s_p = jnp.where(mask_p, _dot(qs, kpb, "nt") * scale, MASK_VALUE)
            mx = jnp.maximum(jnp.max(s_c, axis=-1, keepdims=True), jnp.max(s_p, axis=-1, keepdims=True))
            p_c = jnp.exp(s_c - mx)
            p_p = jnp.exp(s_p - mx)
            den = jnp.sum(p_c, axis=-1, keepdims=True) + jnp.sum(p_p, axis=-1, keepdims=True)
            o = (_dot(p_c.astype(BF), vcb) + _dot(p_p.astype(BF), vpb)) / den
            o_ref[here, :] = _unstack_heads(o, ma)
            lse_ref[here, :] = _unstack_heads(jnp.broadcast_to(mx + jnp.log(den), o.shape), ma)
            return carry

        lax.fori_loop(0, m * d, step, 0, unroll=ATTN_UNROLL)

    def cur(kind):
        return pl.BlockSpec((None, ATTN_STEP_ROWS, LANES), lambda g, j, t: (KIND_TILES * kind + 2 * g + t, j, 0))

    def prv(kind):
        return pl.BlockSpec((None, ATTN_STEP_ROWS, LANES),
                            lambda g, j, t: (KIND_TILES * kind + 2 * g + t, jnp.maximum(j - 1, 0), 0))

    out = pl.BlockSpec((ATTN_STEP_ROWS, LANES), lambda g, j, t: (j, 2 * g + t))
    width = 2 * LANES * len(ATTN_DILATIONS)
    return _pallas_call(
        body, name=name, grid=(len(ATTN_DILATIONS), nbig, 2), in_specs=[cur(0), cur(1), prv(1), cur(2), prv(2)],
        out_specs=[out, out], out_shape=[jax.ShapeDtypeStruct((T, width), F32)] * 2,
        compiler_params=_params(("parallel", "parallel", "parallel")),
    )(qkv, qkv, qkv, qkv, qkv)


def _attn_bwd(z, qkv, do, c, lse, gains, name, after=None):
    T = z.shape[0]
    nbig = _attn_steps(T)
    scale = HEAD_DIM ** -0.5
    extra = [] if after is None else [after]

    def body(*refs):
        g, jb = pl.program_id(0), pl.program_id(1)
        dgq_ref, dgk_ref = refs[len(refs) - 5], refs[len(refs) - 4]

        @pl.when((g == 0) & (jb == 0) & (pl.program_id(2) == 0))
        def _():
            dgq_ref[...] = jnp.zeros_like(dgq_ref)
            dgk_ref[...] = jnp.zeros_like(dgk_ref)

        for gi, d in enumerate(ATTN_DILATIONS):
            pl.when(g == gi)(functools.partial(group, d, jb, *refs))

    def group(d, jb, qr_ref, kr_ref, vc_ref, vp_ref, qn_ref, qnn_ref, kn_ref, knp_ref, do_ref, don_ref, c_ref, cn_ref,
              lse_ref, lsen_ref, g_ref, *rest):
        dq_ref, dk_ref, dv_ref, dgq_ref, dgk_ref, sq_ref, sk_ref, sv_ref = rest[len(extra):]
        sub, m = ATTN_BLOCK * d, ATTN_STEP_ROWS // (ATTN_BLOCK * d)
        nb = T // sub
        gq, gk = g_ref[0:1, :], g_ref[1:2, :]

        def step(s, carry):
            jj, r = s // d, s % d
            here = _attn_rows(jj, r, sub, d)
            before = _attn_rows(jnp.maximum(jj - 1, 0), r, sub, d)
            behind = _attn_rows(jnp.minimum(jj + 1, m - 1), r, sub, d)
            edge_before, edge_behind = _attn_rows(m - 1, r, sub, d), _attn_rows(0, r, sub, d)
            first, last = jj == 0, jj == m - 1
            block = jb * m + jj
            ma, mask_c, mask_p = _attn_masks(block > 0)
            mask_n = _attn_masks(block < nb - 1)[2]
            qhat, rq = _head_norm(qr_ref[here, :], gq, ma)
            qn = qn_ref[here, :]
            qn_next = _pick(last, qnn_ref[edge_behind, :], qn_ref[behind, :])
            khat, rk = _head_norm(kr_ref[here, :], gk, ma)
            kcb = kn_ref[here, :].astype(BF)
            kpb = _pick(first, knp_ref[edge_before, :], kn_ref[before, :]).astype(BF)
            vcb = vc_ref[here, :].astype(BF)
            vpb = _pick(first, vp_ref[edge_before, :], vc_ref[before, :]).astype(BF)
            do_t, don_t = do_ref[here, :], _pick(last, don_ref[edge_behind, :], do_ref[behind, :])
            c_t, cn_t = c_ref[here, :], _pick(last, cn_ref[edge_behind, :], c_ref[behind, :])
            lse_t, lsen_t = lse_ref[here, :], _pick(last, lsen_ref[edge_behind, :], lse_ref[behind, :])
            qs, dos = _stack_heads(qn, ma).astype(BF), _stack_heads(do_t, ma).astype(BF)
            lse_s, c_s = _stack_cols(lse_t, ma), _stack_cols(c_t, ma)
            s_c = jnp.where(mask_c, _dot(qs, kcb, "nt") * scale, MASK_VALUE)
            s_p = jnp.where(mask_p, _dot(qs, kpb, "nt") * scale, MASK_VALUE)
            p_c = jnp.exp(s_c - lse_s)
            p_p = jnp.exp(s_p - lse_s)
            ds_c = ((p_c * (_dot(dos, vcb, "nt") + c_s)) * scale).astype(BF)
            ds_p = ((p_p * (_dot(dos, vpb, "nt") + c_s)) * scale).astype(BF)
            dq_t = _unstack_heads(_dot(ds_c, kcb) + _dot(ds_p, kpb), ma)
            qs_n, dos_n = _stack_heads(qn_next, ma).astype(BF), _stack_heads(don_t, ma).astype(BF)
            s_n = jnp.where(mask_n, _dot(qs_n, kcb, "nt") * scale, MASK_VALUE)
            p_n = jnp.exp(s_n - _stack_cols(lsen_t, ma))
            ds_n = ((p_n * (_dot(dos_n, vcb, "nt") + _stack_cols(cn_t, ma))) * scale).astype(BF)
            dv_t = _dot(p_c.astype(BF), dos, "tn") + _dot(p_n.astype(BF), dos_n, "tn")
            dk_t = _dot(ds_c, qs, "tn") + _dot(ds_n, qs_n, "tn")
            sq_ref[here, :] = _head_norm_bwd(dq_t, qhat, rq, gq, ma)
            sk_ref[here, :] = _head_norm_bwd(dk_t, khat, rk, gk, ma)
            sv_ref[here, :] = dv_t
            dgq_ref[...] += jnp.sum(dq_t * qhat, axis=0, keepdims=True)
            dgk_ref[...] += jnp.sum(dk_t * khat, axis=0, keepdims=True)
            return carry

        lax.fori_loop(0, m * d, step, 0, unroll=ATTN_UNROLL)
        dq_ref[...] = sq_ref[...].astype(BF)
        dk_ref[...] = sk_ref[...].astype(BF)
        dv_ref[...] = sv_ref[...].astype(BF)

    rows = ATTN_STEP_ROWS

    def raw(col0):
        return pl.BlockSpec((rows, LANES), lambda g, j, t: (j, col0 + 2 * g + t))

    def cur(kind):
        return pl.BlockSpec((None, rows, LANES), lambda g, j, t: (KIND_TILES * kind + 2 * g + t, j, 0))

    def prv(kind):
        return pl.BlockSpec((None, rows, LANES), lambda g, j, t: (KIND_TILES * kind + 2 * g + t, jnp.maximum(j - 1, 0), 0))

    def nxt(kind):
        return pl.BlockSpec((None, rows, LANES),
                            lambda g, j, t: (KIND_TILES * kind + 2 * g + t, jnp.minimum(j + 1, nbig - 1), 0))

    own = pl.BlockSpec((rows, LANES), lambda g, j, t: (j, 2 * g + t))
    own_next = pl.BlockSpec((rows, LANES), lambda g, j, t: (jnp.minimum(j + 1, nbig - 1), 2 * g + t))
    vec = pl.BlockSpec((1, LANES), lambda g, j, t: (0, 0))
    width = 2 * LANES * len(ATTN_DILATIONS)
    return _pallas_call(
        body, name=name, grid=(len(ATTN_DILATIONS), nbig, 2),
        in_specs=[raw(OFF_Q // LANES), raw(OFF_K // LANES), cur(2), prv(2), cur(0), nxt(0), cur(1), prv(1), own, own_next,
                  own, own_next, own, own_next, pl.BlockSpec((8, LANES), lambda g, j, t: (0, 0))] + [ANY] * len(extra),
        out_specs=[own, own, own, vec, vec],
        out_shape=[jax.ShapeDtypeStruct((T, width), BF)] * 3 + [jax.ShapeDtypeStruct((1, LANES), F32)] * 2,
        scratch_shapes=[pltpu.VMEM((rows, LANES), F32)] * 3,
        compiler_params=_params(("arbitrary", "arbitrary", "arbitrary")),
    )(z, z, qkv, qkv, qkv, qkv, qkv, qkv, do, do, c, c, lse, lse, gains, *extra)


MERGE_ROWS = 256
GATE_TILE = 256


def _group_mix(o_refs, lse_refs):
    lses = [r[...] for r in lse_refs]
    m = jnp.maximum(jnp.maximum(lses[0], lses[1]), lses[2])
    es = [jnp.exp(l - m) for l in lses]
    den = es[0] + es[1] + es[2]
    ws = [e / den for e in es]
    y = ws[0] * o_refs[0][...] + ws[1] * o_refs[1][...] + ws[2] * o_refs[2][...]
    return ws, y


def _sigmoid(v):
    return 1.0 / (1.0 + jnp.exp(-v))


def _merge_specs(T, z, bgate, gpu, gco, gau):
    tm = min(MERGE_ROWS, T)
    row = lambda w: pl.BlockSpec((tm, w), lambda i: (i, 0))
    gate0 = OFF_GATE // GATE_TILE
    gates = [pl.BlockSpec((tm, GATE_TILE), functools.partial(lambda i, cb: (i, cb), cb=gate0 + n))
             for n in range(3 * N_CHIPS)]
    full = lambda a: pl.BlockSpec(a.shape, lambda i: (0,) * a.ndim)
    by_group = [pl.BlockSpec((tm, 256), functools.partial(lambda i, g: (i, g), g=g)) for g in range(3)]
    specs = [row(512), row(512)] + by_group * 2 + gates + [full(bgate), full(gpu), full(gco), full(gau)]
    return tm, row, specs


def _merge_fwd(yp, yc, o3, lse3, z, bgate, gpu, gco, gau, name):
    T = yp.shape[0]
    tm, row, specs = _merge_specs(T, z, bgate, gpu, gco, gau)

    def body(*refs):
        yp_ref, yc_ref = refs[0], refs[1]
        o_refs, lse_refs = refs[2:5], refs[5:8]
        zg = refs[8:20]
        b_ref, gpu_ref, gco_ref, gau_ref, out_ref = refs[20:25]
        yab = _group_mix(o_refs, lse_refs)[1].astype(BF)
        ys = (yp_ref[...], yc_ref[...], yab)
        ups = (gpu_ref, gco_ref, gau_ref)
        for n in range(N_CHIPS):
            acc = None
            for b in range(3):
                gcol = slice(1024 * b + GATE_TILE * n, 1024 * b + GATE_TILE * (n + 1))
                gate = _sigmoid(zg[N_CHIPS * b + n][...] + b_ref[:, gcol])
                term = gate * _dot(ys[b], ups[b][n])
                acc = term if acc is None else acc + term
            out_ref[:, GATE_TILE * n:GATE_TILE * (n + 1)] = acc.astype(BF)

    return _pallas_call(
        body, name=name, grid=(T // tm,), in_specs=specs, out_specs=row(1024),
        out_shape=jax.ShapeDtypeStruct((T, 1024), BF), compiler_params=_params(("parallel",)),
    )(yp, yc, *([o3] * 3), *([lse3] * 3), *([z] * 12), bgate, gpu, gco, gau)


def _merge_bwd(dm, yp, yc, o3, lse3, z, bgate, gpu, gco, gau, name):
    T = yp.shape[0]
    tm, row, specs = _merge_specs(T, z, bgate, gpu, gco, gau)
    nsteps = T // tm

    def body(*refs):
        dm_ref, yp_ref, yc_ref = refs[0:3]
        o_refs, lse_refs = refs[3:6], refs[6:9]
        zg = refs[9:21]
        b_ref, gpu_ref, gco_ref, gau_ref = refs[21:25]
        dzg_ref, dyp_ref, dyc_ref = refs[25:28]
        do_ref, c_ref = refs[28:30]
        dgpu_ref, dgco_ref, dgau_ref, dbg_ref = refs[30:34]
        accs = refs[34:37]
        i = pl.program_id(0)

        @pl.when(i == 0)
        def _():
            for a in accs:
                a[...] = jnp.zeros_like(a)
            dbg_ref[...] = jnp.zeros_like(dbg_ref)

        ws, y = _group_mix(o_refs, lse_refs)
        ys = (yp_ref[...], yc_ref[...], y.astype(BF))
        ups = (gpu_ref, gco_ref, gau_ref)
        dys = [None, None, None]
        for n in range(N_CHIPS):
            dmn = dm_ref[:, GATE_TILE * n:GATE_TILE * (n + 1)]
            for b in range(3):
                gcol = slice(1024 * b + GATE_TILE * n, 1024 * b + GATE_TILE * (n + 1))
                gate = _sigmoid(zg[N_CHIPS * b + n][...] + b_ref[:, gcol])
                up = _dot(ys[b], ups[b][n])
                dzg = (dmn * up) * (gate * (1.0 - gate))
                dzg_ref[:, gcol] = dzg.astype(BF)
                dbg_ref[:, gcol] += jnp.sum(dzg, axis=0, keepdims=True)
                dup = (dmn * gate).astype(BF)
                accs[b][n] += _dot(ys[b], dup, "tn")
                dyb = _dot(dup, ups[b][n], "nt")
                dys[b] = dyb if dys[b] is None else dys[b] + dyb
        dyp_ref[...] = dys[0]
        dyc_ref[...] = dys[1]
        dya = dys[2]
        lane = lax.broadcasted_iota(jnp.int32, dya.shape, 1) // HEAD_DIM
        pr = dya * y
        rho = jnp.zeros_like(pr)
        for h in range(256 // HEAD_DIM):
            hm = lane == h
            rho = jnp.where(hm, jnp.sum(jnp.where(hm, pr, 0.0), axis=-1, keepdims=True), rho)
        for g in range(3):
            do_ref[:, 256 * g:256 * (g + 1)] = ws[g] * dya
            c_ref[:, 256 * g:256 * (g + 1)] = -(ws[g] * rho)

        @pl.when(i == nsteps - 1)
        def _():
            dgpu_ref[...] = accs[0][...].astype(BF)
            dgco_ref[...] = accs[1][...].astype(BF)
            dgau_ref[...] = accs[2][...].astype(BF)

    full = lambda a: pl.BlockSpec(a.shape, lambda i: (0,) * a.ndim)
    dz_gate = pl.BlockSpec((pl.Element(tm), pl.Element(3072)), lambda i: (i * tm, OFF_GATE))
    out_specs = ([dz_gate, row(512), row(512)] + [row(768)] * 2 + [full(gpu), full(gco), full(gau)]
                 + [pl.BlockSpec((1, 3072), lambda i: (0, 0))])
    out_shape = ([jax.ShapeDtypeStruct(z.shape, BF)] + [jax.ShapeDtypeStruct((T, 512), F32)] * 2
                 + [jax.ShapeDtypeStruct((T, 768), F32)] * 2
                 + [jax.ShapeDtypeStruct(g.shape, BF) for g in (gpu, gco, gau)]
                 + [jax.ShapeDtypeStruct((1, 3072), F32)])
    return _pallas_call(
        body, name=name, grid=(nsteps,), in_specs=[row(1024)] + specs, out_specs=out_specs, out_shape=out_shape,
        scratch_shapes=[pltpu.VMEM(g.shape, F32) for g in (gpu, gco, gau)],
        compiler_params=_params(("arbitrary",)),
    )(dm, yp, yc, *([o3] * 3), *([lse3] * 3), *([z] * 12), bgate, gpu, gco, gau)


def _layer_fwd(x, w, tag, after=None, soon=None, late=None, target=None, hb=None, next_gain=None):
    if hb is None:
        hb = _rms_fwd(x, w["norm_mix"], f"rms_mix_{tag}", after=after)
    if soon is not None:
        w = dict(w, **soon(hb))
    z = _mm(hb, w["w_in"], "nt", f"in_proj_{tag}", tm=512, tn=3712, tk=1024, n_outer=True)
    yp, yc = _poolconv_fwd(z, w["pool_mix"], w["pool_scale"], w["conv_w"], f"poolconv_{tag}")
    qkv = _qk_norm(z, w["qk_gain"], f"qk_norm_{tag}")
    o3, lse3 = _attn_fwd(qkv, f"attn_{tag}")
    if late is not None:
        w = dict(w, **late(lse3))
    merged = _merge_fwd(yp, yc, o3, lse3, z, w["b_gate"], w["w_pool_up"], w["w_conv_out"], w["w_attn_up"],
                        f"merge_{tag}")
    x1, h2b = _mm(merged, w["w_o"], "nn", f"out_proj_{tag}", tm=1024, tn=1024, tk=1024, res=x, vec=w["norm_mlp"],
                  epi="rms_next")
    rb = _mm(h2b, w["w_ff1"], "nn", f"ff1_{tag}", tm=1024, tn=1024, tk=1024, out_dtype=BF, epi="relu2", n_outer=True,
             b_shards=True)
    if target is not None:
        x2 = _mm(rb, w["w_ff2"], "nn", f"ff2_{tag}", tm=512, tn=1024, tk=4096, res=x1, aux=target, epi="loss")
    elif next_gain is not None:
        x2 = _mm(rb, w["w_ff2"], "nn", f"ff2_{tag}", tm=512, tn=1024, tk=4096, res=x1, vec=next_gain, epi="rms_next")
    else:
        x2 = _mm(rb, w["w_ff2"], "nn", f"ff2_{tag}", tm=512, tn=1024, tk=4096, res=x1)
    saved = dict(x=x, hb=hb, z=z, yp=yp, yc=yc, qkv=qkv, o3=o3, lse3=lse3, merged=merged, x1=x1, h2b=h2b, rb=rb)
    return x2, saved, w


def _layer_bwd(dx2, w, s, tag, after=None, mid=None, tail=None):
    g = {}
    dab = _mm(dx2, w["w_ff2"], "nt", f"d_ff2_act_{tag}", tm=1024, tn=1024, tk=1024, out_dtype=BF, aux=s["rb"],
              epi="drelu2", after=after)
    g["w_ff2"] = _mm(s["rb"], dx2, "tn", f"d_ff2_w_{tag}", tm=1024, tn=1024, tk=2048, out_dtype=BF)
    g["w_ff1"] = _mm(s["h2b"], dab, "tn", f"d_ff1_w_{tag}", tm=1024, tn=1024, tk=2048, out_dtype=BF, out_shards=True)
    dx1, g["norm_mlp"] = _mm(dab, w["w_ff1"], "nt", f"d_ff1_act_{tag}", tm=1024, tn=1024, tk=1024, b_shards=True,
                             res=dx2, aux=s["x1"], vec=w["norm_mlp"], epi="rms_bwd")
    dm = _mm(dx1, w["w_o"], "nt", f"d_out_act_{tag}", tm=1024, tn=1024, tk=1024)
    g["w_o"] = _mm(s["merged"], dx1, "tn", f"d_out_w_{tag}", tm=1024, tn=1024, tk=1024, out_dtype=BF)
    (dz, dyp, dyc, do3, c3, g["w_pool_up"], g["w_conv_out"], g["w_attn_up"],
     g["b_gate"]) = _merge_bwd(dm, s["yp"], s["yc"], s["o3"], s["lse3"], s["z"], w["b_gate"], w["w_pool_up"],
                               w["w_conv_out"], w["w_attn_up"], f"d_merge_{tag}")
    behind = mid(g) if mid is not None else None
    dzq, dzk, dzv, dgq, dgk = _attn_bwd(s["z"], s["qkv"], do3, c3, s["lse3"], w["qk_gain"], f"d_attn_{tag}",
                                        after=behind)
    g["q_gain"] = dgq[:, :HEAD_DIM] + dgq[:, HEAD_DIM:]
    g["k_gain"] = dgk[:, :HEAD_DIM] + dgk[:, HEAD_DIM:]
    for off, piece in ((OFF_Q, dzq), (OFF_K, dzk), (OFF_V, dzv)):
        dz = lax.dynamic_update_slice(dz, piece, (0, off))
    dz, g["pool_mix"], g["pool_scale"], g["conv_w"] = _poolconv_bwd(
        s["z"], dyp, dyc, w["pool_mix"], w["pool_scale"], w["conv_w"], dz, f"d_poolconv_{tag}")
    g["w_in"] = _mm(s["hb"], dz, "tn", f"d_in_w_{tag}", tm=512, tn=3712, tk=1024, out_dtype=BF)
    dh = _mm(dz, w["w_in"], "nn", f"d_in_act_{tag}", tm=1024, tn=1024, tk=3712,
             after=tail(g) if tail is not None else None)
    dx, g["norm_mix"] = _rms_bwd(dh, s["x"], w["norm_mix"], dx1, f"d_rms_mix_{tag}")
    return dx, g


def _position():
    x, y, c = lax.axis_index("x"), lax.axis_index("y"), lax.axis_index("c")
    chips = [(1 - x, y), (x, 1 - y), (1 - x, 1 - y)]
    return x, y, c, 2 * x + y, chips, [2 * cx + cy for cx, cy in chips]


def _remote(src, dst, ssem, rsem, dev):
    return pltpu.make_async_remote_copy(src_ref=src, dst_ref=dst, send_sem=ssem, recv_sem=rsem, device_id=dev,
                                        device_id_type=MESH_ID)


def _halves(a):
    return a.reshape(a.shape[0], 2, a.shape[1] // 2, a.shape[2])


SEM = pl.BlockSpec(memory_space=pltpu.SEMAPHORE)
TOKEN = jax.ShapeDtypeStruct((8, LANES), F32)
TOKEN_SPEC = pl.BlockSpec(memory_space=pltpu.VMEM)


def _split_params():
    return pltpu.CompilerParams(has_side_effects=pltpu.SideEffectType.DATAFLOW_SIDE_EFFECTING)


def _gather_start(bufs, name, after):
    n = len(bufs)
    views = [_halves(b) for b in bufs]

    def body(*refs):
        first_sem = n + 1
        ssem, rsem = refs[first_sem:first_sem + ns], refs[first_sem + ns:first_sem + 2 * ns]
        outs, token = refs[first_sem + 2 * ns:first_sem + 2 * ns + n], refs[first_sem + 2 * ns + n]
        x, y, c, q, chips, qs = _position()
        for k in range(n):
            mine = outs[k].at[q, c]
            for j, chip in enumerate(chips):
                _remote(mine, mine, ssem[3 * k + j], rsem[3 * k + j], (chip[0], chip[1], c)).start()
        token[...] = jnp.zeros_like(token)

    ns = 3 * n
    outs = _pallas_call(
        body, name=name, in_specs=[ANY] * (n + 1), out_specs=[SEM] * (2 * ns) + [ANY] * n + [TOKEN_SPEC],
        out_shape=[pltpu.SemaphoreType.DMA(())] * (2 * ns) + [jax.ShapeDtypeStruct(v.shape, v.dtype) for v in views]
        + [TOKEN],
        input_output_aliases={k: k + 2 * ns for k in range(n)}, compiler_params=_split_params(),
    )(*views, after)
    return list(outs[:ns]), list(outs[ns:2 * ns]), list(outs[2 * ns:2 * ns + n]), outs[2 * ns + n]


def _gather_finish(ssem, rsem, views, after, name_wait, name_forward, shapes):
    n = len(views)
    ns = len(ssem)

    def wait_body(*refs):
        ssem_ref, rsem_ref = refs[n:n + ns], refs[n + ns:n + 2 * ns]
        outs = refs[n + 2 * ns + 1:]
        x, y, c, q, chips, qs = _position()
        for k in range(n):
            for j, chip in enumerate(chips):
                cp = _remote(outs[k].at[q, c], outs[k].at[qs[j], c], ssem_ref[3 * k + j], rsem_ref[3 * k + j],
                             (chip[0], chip[1], c))
                cp.wait_send()
                cp.wait_recv()

    landed = _pallas_call(
        wait_body, name=name_wait, in_specs=[ANY] * n + [SEM] * (2 * ns) + [ANY], out_specs=[ANY] * n,
        out_shape=[jax.ShapeDtypeStruct(v.shape, v.dtype) for v in views],
        input_output_aliases={k: k for k in range(n)}, compiler_params=_split_params(),
    )(*views, *ssem, *rsem, after)

    def forward_body(*refs):
        outs = refs[n:2 * n]
        fssem, frsem = refs[2 * n:]
        x, y, c, q, chips, qs = _position()
        sib = (x, y, 1 - c)
        sent = []
        for k in range(n):
            for j in range(3):
                slot = outs[k].at[qs[j], c]
                cp = _remote(slot, slot, fssem.at[k, j], frsem.at[k, j], sib)
                cp.start()
                sent.append(cp)
        for k in range(n):
            for j in range(3):
                slot = outs[k].at[qs[j], 1 - c]
                _remote(slot, slot, fssem.at[k, j], frsem.at[k, j], sib).wait_recv()
        for cp in sent:
            cp.wait_send()

    outs = _pallas_call(
        forward_body, name=name_forward, in_specs=[ANY] * n, out_specs=[ANY] * n,
        out_shape=[jax.ShapeDtypeStruct(v.shape, v.dtype) for v in views],
        input_output_aliases={k: k for k in range(n)}, scratch_shapes=[pltpu.SemaphoreType.DMA((n, 3))] * 2,
    )(*landed)
    return [o.reshape(s) for o, s in zip(outs, shapes)]


def _chip_exchange_start(parts, name):
    n = len(parts)

    def body(*refs):
        ssem, rsem = refs[n:n + ns], refs[n + ns:n + 2 * ns]
        base = n + 2 * ns
        srcs, outs, token = refs[base:base + n], refs[base + n:base + 2 * n], refs[base + 2 * n]
        x, y, c, q, chips, qs = _position()
        for k in range(n):
            for j, chip in enumerate(chips):
                _remote(srcs[k].at[qs[j]], outs[k].at[j], ssem[3 * k + j], rsem[3 * k + j],
                        (chip[0], chip[1], c)).start()
        token[...] = jnp.zeros_like(token)

    ns = 3 * n
    outs = _pallas_call(
        body, name=name, in_specs=[ANY] * n, out_specs=[SEM] * (2 * ns) + [ANY] * (2 * n) + [TOKEN_SPEC],
        out_shape=[pltpu.SemaphoreType.DMA(())] * (2 * ns) + [jax.ShapeDtypeStruct(a.shape, a.dtype) for a in parts]
        + [jax.ShapeDtypeStruct((3,) + a.shape[1:], a.dtype) for a in parts] + [TOKEN],
        input_output_aliases={k: k + 2 * ns for k in range(n)}, compiler_params=_split_params(),
    )(*parts)
    b = 2 * ns
    return list(outs[:ns]), list(outs[ns:b]), list(outs[b:b + n]), list(outs[b + n:b + 2 * n]), outs[b + 2 * n]


def _chip_exchange_wait(ssem, rsem, parts, landing, after, name):
    n = len(parts)
    ns = len(ssem)

    def body(*refs):
        ssem_ref, rsem_ref = refs[2 * n:2 * n + ns], refs[2 * n + ns:2 * n + 2 * ns]
        base = 2 * n + 2 * ns + 1
        srcs, outs = refs[base:base + n], refs[base + n:]
        x, y, c, q, chips, qs = _position()
        for k in range(n):
            for j, chip in enumerate(chips):
                cp = _remote(srcs[k].at[qs[j]], outs[k].at[j], ssem_ref[3 * k + j], rsem_ref[3 * k + j],
                             (chip[0], chip[1], c))
                cp.wait_send()
                cp.wait_recv()

    outs = _pallas_call(
        body, name=name, in_specs=[ANY] * (2 * n) + [SEM] * (2 * ns) + [ANY], out_specs=[ANY] * (2 * n),
        out_shape=[jax.ShapeDtypeStruct(a.shape, a.dtype) for a in list(parts) + list(landing)],
        input_output_aliases={k: k for k in range(2 * n)}, compiler_params=_split_params(),
    )(*parts, *landing, *ssem, *rsem, after)
    return list(outs[:n]), list(outs[n:])


def _pair_swap(views, name):
    n = len(views)

    def body(*refs):
        ins, outs = refs[:n], refs[n:2 * n]
        ssem, rsem = refs[2 * n:]
        x, y, c, _, _, _ = _position()
        cps = [_remote(ins[k].at[pl.ds(0, N_CHIPS), 1 - c], outs[k], ssem.at[k], rsem.at[k], (x, y, 1 - c))
               for k in range(n)]
        for cp in cps:
            cp.start()
        for cp in cps:
            cp.wait()

    return _pallas_call(
        body, name=name, in_specs=[ANY] * n, out_specs=[ANY] * n,
        out_shape=[jax.ShapeDtypeStruct((v.shape[0],) + v.shape[2:], v.dtype) for v in views],
        scratch_shapes=[pltpu.SemaphoreType.DMA((n,))] * 2,
    )(*views)


def _pair_send(arrays, name):
    n = len(arrays)

    def body(*refs):
        ins, outs = refs[:n], refs[n:2 * n]
        ssem, rsem = refs[2 * n:]
        x, y, c, _, _, _ = _position()
        cps = [_remote(ins[k], outs[k], ssem.at[k], rsem.at[k], (x, y, 1 - c)) for k in range(n)]
        for cp in cps:
            cp.start()
        for cp in cps:
            cp.wait()

    return _pallas_call(
        body, name=name, in_specs=[ANY] * n, out_specs=[ANY] * n,
        out_shape=[jax.ShapeDtypeStruct(a.shape, a.dtype) for a in arrays],
        scratch_shapes=[pltpu.SemaphoreType.DMA((n,))] * 2,
    )(*arrays)


def _all_to_all_small(part):
    P = part.shape[0]

    def body(in_ref, out_ref, lsem, ssem, rsem):
        x, y, c = lax.axis_index("x"), lax.axis_index("y"), lax.axis_index("c")
        me = 4 * x + 2 * y + c
        flips = [(fx, fy, fc) for fx in (0, 1) for fy in (0, 1) for fc in (0, 1)][1:]
        peers = [((x + fx) % 2, (y + fy) % 2, (c + fc) % 2) for fx, fy, fc in flips]
        loc = pltpu.make_async_copy(in_ref, out_ref.at[me], lsem)
        loc.start()
        cps = [_remote(in_ref, out_ref.at[me], ssem.at[j], rsem.at[j], peer) for j, peer in enumerate(peers)]
        for cp in cps:
            cp.start()
        for j, (px, py, pc) in enumerate(peers):
            _remote(in_ref, out_ref.at[4 * px + 2 * py + pc], ssem.at[j], rsem.at[j], peers[j]).wait_recv()
        for cp in cps:
            cp.wait_send()
        loc.wait()

    return _pallas_call(
        body, name="small_exchange", in_specs=[ANY], out_specs=ANY,
        out_shape=jax.ShapeDtypeStruct((8, P, LANES), F32),
        scratch_shapes=[pltpu.SemaphoreType.DMA(())] + [pltpu.SemaphoreType.DMA((7,))] * 2,
    )(part)


def _small_peers():
    x, y, c = lax.axis_index("x"), lax.axis_index("y"), lax.axis_index("c")
    flips = [(fx, fy, fc) for fx in (0, 1) for fy in (0, 1) for fc in (0, 1)][1:]
    peers = [((x + fx) % 2, (y + fy) % 2, (c + fc) % 2) for fx, fy, fc in flips]
    return 4 * x + 2 * y + c, peers


def _all_to_all_small_start(part, name):
    P = part.shape[0]
    me = 4 * lax.axis_index("x") + 2 * lax.axis_index("y") + lax.axis_index("c")
    landing = lax.dynamic_update_slice(jnp.zeros((8, P, LANES), F32), part[None], (me, 0, 0))

    def body(*refs):
        sems, src, land, token = refs[2:16], refs[16], refs[17], refs[18]
        me_, peers = _small_peers()
        for j, peer in enumerate(peers):
            _remote(src, land.at[me_], sems[j], sems[7 + j], peer).start()
        token[...] = jnp.zeros_like(token)

    outs = _pallas_call(
        body, name=name, in_specs=[ANY, ANY], out_specs=[SEM] * 14 + [ANY, ANY, TOKEN_SPEC],
        out_shape=[pltpu.SemaphoreType.DMA(())] * 14 + [jax.ShapeDtypeStruct(part.shape, F32),
                                                       jax.ShapeDtypeStruct((8, P, LANES), F32), TOKEN],
        input_output_aliases={0: 14, 1: 15}, compiler_params=_split_params(),
    )(part, landing)
    return list(outs[:7]), list(outs[7:14]), outs[14], outs[15], outs[16]


def _all_to_all_small_wait(ssem, rsem, part, landing, after, name):
    def body(*refs):
        sems, src, land = refs[2:16], refs[17], refs[18]
        _, peers = _small_peers()
        for j, (px, py, pc) in enumerate(peers):
            cp = _remote(src, land.at[4 * px + 2 * py + pc], sems[j], sems[7 + j], peers[j])
            cp.wait_send()
            cp.wait_recv()

    return _pallas_call(
        body, name=name, in_specs=[ANY, ANY] + [SEM] * 14 + [ANY], out_specs=[ANY, ANY],
        out_shape=[jax.ShapeDtypeStruct(part.shape, F32), jax.ShapeDtypeStruct(landing.shape, F32)],
        input_output_aliases={0: 0, 1: 1}, compiler_params=_split_params(),
    )(part, landing, *ssem, *rsem, after)[1]


def _row_tile(rows, width, n_arrays):
    t = rows
    while t % 2 == 0 and t > 8 and 2 * n_arrays * t * width * 4 > VMEM_LIMIT // 2:
        t //= 2
    return t


def _chip():
    return 2 * lax.axis_index("x") + lax.axis_index("y")


def _core():
    return lax.axis_index("c")


def _cast_place(w3, layer, name):
    _, r, c = w3.shape
    tr = _row_tile(r, c, 2)

    def body(w_ref, o_ref):
        o_ref[...] = w_ref[...].astype(BF)

    return _pallas_call(
        body, name=name, grid=(r // tr,), in_specs=[pl.BlockSpec((None, tr, c), lambda i: (layer, i, 0))],
        out_specs=pl.BlockSpec((None, tr, c), lambda i: (_chip(), i, 0)),
        out_shape=jax.ShapeDtypeStruct((N_CHIPS, r, c), BF), compiler_params=_params(("parallel",)),
    )(w3)


def _pair_sum(views, recvs, name):
    n = len(views)

    def body(*refs):
        for g_ref, r_ref, o_ref in zip(refs[:n], refs[n:2 * n], refs[2 * n:]):
            o_ref[...] = (g_ref[...].astype(F32) + r_ref[...].astype(F32)).astype(BF)

    own = [pl.BlockSpec((None, None) + v.shape[2:], lambda p: (p, _core(), 0, 0)) for v in views]
    blk = [pl.BlockSpec((None,) + r.shape[1:], lambda p: (p, 0, 0)) for r in recvs]
    return _pallas_call(
        body, name=name, grid=(N_CHIPS,), in_specs=own + blk, out_specs=blk,
        out_shape=[jax.ShapeDtypeStruct(r.shape, BF) for r in recvs], compiler_params=_params(("parallel",)),
    )(*views, *recvs)


CHIP_SUM_STEPS = 2


def _chip_sum(parts, recvs, name):
    n = len(parts)

    def body(*refs):
        for p_ref, r_ref, o_ref in zip(refs[:n], refs[n:2 * n], refs[2 * n:]):
            acc = p_ref[...].astype(F32)
            for j in range(3):
                acc = acc + r_ref[j].astype(F32)
            o_ref[...] = acc

    rows = [p.shape[1] // CHIP_SUM_STEPS for p in parts]
    return _pallas_call(
        body, name=name, grid=(CHIP_SUM_STEPS,),
        in_specs=[pl.BlockSpec((None, t, p.shape[2]), lambda i: (_chip(), i, 0)) for p, t in zip(parts, rows)]
        + [pl.BlockSpec((3, t, p.shape[2]), lambda i: (0, i, 0)) for p, t in zip(parts, rows)],
        out_specs=[pl.BlockSpec((t, p.shape[2]), lambda i: (i, 0)) for p, t in zip(parts, rows)],
        out_shape=[jax.ShapeDtypeStruct(p.shape[1:], F32) for p in parts], compiler_params=_params(("parallel",)),
    )(*parts, *recvs)


def _sum_slices(a, name):
    n, rows, width = a.shape
    tr = _row_tile(rows, width, n + 1)

    def body(a_ref, o_ref):
        acc = a_ref[0].astype(F32)
        for i in range(1, n):
            acc = acc + a_ref[i].astype(F32)
        o_ref[...] = acc

    return _pallas_call(
        body, name=name, grid=(rows // tr,), in_specs=[pl.BlockSpec((n, tr, width), lambda i: (0, i, 0))],
        out_specs=pl.BlockSpec((tr, width), lambda i: (i, 0)), out_shape=jax.ShapeDtypeStruct((rows, width), F32),
        compiler_params=_params(("parallel",)),
    )(a)


def _adamw_update(w, g, m, v):
    nm = ADAM_B1 * m + (1.0 - ADAM_B1) * g
    nv = ADAM_B2 * v + (1.0 - ADAM_B2) * (g * g)
    m_hat = nm / (1.0 - ADAM_B1 ** ADAM_STEP)
    v_hat = nv / (1.0 - ADAM_B2 ** ADAM_STEP)
    return -ADAM_LR * (m_hat / (jnp.sqrt(v_hat) + ADAM_EPS) + ADAM_WD * w), nm, nv


def _adamw(ws, gs, ms, vs, name):
    n = len(ws)

    def body(*refs):
        for k in range(n):
            w_ref, g_ref, m_ref, v_ref = (refs[s * n + k] for s in range(4))
            d_ref, nm_ref, nv_ref = (refs[(4 + s) * n + k] for s in range(3))
            d_ref[...], nm_ref[...], nv_ref[...] = _adamw_update(w_ref[...], g_ref[...], m_ref[...], v_ref[...])

    whole = [pl.BlockSpec(w.shape, lambda i: (0, 0)) for w in ws]
    outs = _pallas_call(
        body, name=name, grid=(1,), in_specs=whole * 4, out_specs=whole * 3,
        out_shape=[jax.ShapeDtypeStruct(w.shape, F32) for _ in range(3) for w in ws],
        compiler_params=_params(("arbitrary",)),
    )(*ws, *gs, *ms, *vs)
    return [[outs[s * n + k] for s in range(3)] for k in range(n)]


ADAMW_STEPS = 4


def _adamw_halves(ws, ms, vs, mine, other, name):
    n = len(ws)
    depth = ws[0].shape[0]
    assert depth == 2
    halves = [(w.shape[1] // 2, w.shape[2]) for w in ws]
    tiles = [hr // ADAMW_STEPS for hr, _ in halves]
    kinds = ((0, True), (0, False), (1, True), (1, False))

    def active(l, h, layer, own):
        mine_half = h == _core()
        return (l == layer) & (mine_half if own else jnp.logical_not(mine_half))

    def body(*refs):
        l, h = pl.program_id(0), pl.program_id(1)
        flags = [active(l, h, layer, own) for layer, own in kinds]
        for k in range(n):
            w_ref, m_ref, v_ref = refs[k], refs[n + k], refs[2 * n + k]
            g_refs = [refs[(3 + s) * n + k] for s in range(4)]
            go_ref, d_ref, nm_ref, nv_ref = (refs[(7 + s) * n + k] for s in range(4))
            for flag, g_ref in zip(flags, g_refs):
                @pl.when(flag)
                def _():
                    gv = g_ref[...]
                    go_ref[...] = gv
                    d_ref[...], nm_ref[...], nv_ref[...] = _adamw_update(w_ref[...], gv, m_ref[...], v_ref[...])

    def blk(k):
        return pl.BlockSpec((None, None, tiles[k], halves[k][1]), lambda l, h, i: (l, h, i, 0))

    def gspec(k, layer, own):
        return pl.BlockSpec((tiles[k], halves[k][1]), lambda l, h, i: (jnp.where(active(l, h, layer, own), i, 0), 0))

    def view(a, k):
        return a.reshape(depth, 2, halves[k][0], halves[k][1])

    blks = [blk(k) for k in range(n)]
    sources = [[(mine if own else other)[layer][k] for k in range(n)] for layer, own in kinds]
    outs = _pallas_call(
        body, name=name, grid=(depth, 2, ADAMW_STEPS),
        in_specs=blks * 3 + [gspec(k, layer, own) for layer, own in kinds for k in range(n)], out_specs=blks * 4,
        out_shape=[jax.ShapeDtypeStruct((depth, 2) + halves[k], F32) for _ in range(4) for k in range(n)],
        compiler_params=_params(("parallel", "parallel", "parallel")),
    )(*[view(a, k) for group in (ws, ms, vs) for k, a in enumerate(group)], *[g for src in sources for g in src])
    return [[outs[s * n + k].reshape(ws[k].shape) for s in range(4)] for k in range(n)]


BIG = ("w_in", "w_pool_up", "w_conv_out", "w_attn_up", "w_o", "w_ff1", "w_ff2")
SMALL = ("norm_mix", "b_gate", "pool_mix", "pool_scale", "conv_w", "q_gain", "k_gain", "norm_mlp")
ORDER = ("norm_mix", "w_in", "b_gate", "pool_mix", "pool_scale", "conv_w", "q_gain", "k_gain", "w_pool_up",
         "w_conv_out", "w_attn_up", "w_o", "norm_mlp", "w_ff1", "w_ff2")
COLUMN_SHARDED = ("w_pool_up", "w_conv_out", "w_attn_up", "w_ff1")


def _matrix_weights(gathered):
    w = {}
    for name, g4 in gathered.items():
        if name in COLUMN_SHARDED:
            w[name] = g4
        else:
            w[name] = g4.reshape(N_CHIPS * g4.shape[1], g4.shape[2])
    return w


def _small_weights(l, small):
    w = {}
    w["norm_mix"] = small["norm_mix"][l][None]
    w["norm_mlp"] = small["norm_mlp"][l][None]
    w["b_gate"] = small["b_gate"][l][None]
    w["pool_mix"] = small["pool_mix"][l].astype(BF)
    w["pool_scale"] = small["pool_scale"][l][None]
    w["conv_w"] = jnp.pad(small["conv_w_full"][l], ((0, 5), (0, 0)))
    w["qk_gain"] = jnp.pad(jnp.stack([jnp.tile(small["q_gain"][l], 2), jnp.tile(small["k_gain"][l], 2)]), ((0, 6), (0, 0)))
    return w


def _to_chip_major(name, g):
    if name == "w_in":
        return g.T.reshape(N_CHIPS, g.shape[1] // N_CHIPS, g.shape[0])
    if name in COLUMN_SHARDED:
        return g
    return g.reshape(N_CHIPS, g.shape[0] // N_CHIPS, g.shape[1])


def _pad8(a):
    a = a.reshape(-1)
    return jnp.pad(a, (0, (-a.size) % (8 * LANES))).reshape(-1, LANES)


def kernel(x, norm_mix, w_in, b_gate, pool_mix, pool_scale, conv_w, q_gain, k_gain, w_pool_up, w_conv_out, w_attn_up, w_o, norm_mlp, w_ff1, w_ff2, loss_target, m_norm_mix, m_w_in, m_b_gate, m_pool_mix, m_pool_scale, m_conv_w, m_q_gain, m_k_gain, m_w_pool_up, m_w_conv_out, m_w_attn_up, m_w_o, m_norm_mlp, m_w_ff1, m_w_ff2, v_norm_mix, v_w_in, v_b_gate, v_pool_mix, v_pool_scale, v_conv_w, v_q_gain, v_k_gain, v_w_pool_up, v_w_conv_out, v_w_attn_up, v_w_o, v_norm_mlp, v_w_ff1, v_w_ff2):
    weights = dict(norm_mix=norm_mix, w_in=w_in, b_gate=b_gate, pool_mix=pool_mix, pool_scale=pool_scale, conv_w=conv_w,
                   q_gain=q_gain, k_gain=k_gain, w_pool_up=w_pool_up, w_conv_out=w_conv_out, w_attn_up=w_attn_up,
                   w_o=w_o, norm_mlp=norm_mlp, w_ff1=w_ff1, w_ff2=w_ff2)
    moms = dict(norm_mix=m_norm_mix, w_in=m_w_in, b_gate=m_b_gate, pool_mix=m_pool_mix, pool_scale=m_pool_scale,
                conv_w=m_conv_w, q_gain=m_q_gain, k_gain=m_k_gain, w_pool_up=m_w_pool_up, w_conv_out=m_w_conv_out,
                w_attn_up=m_w_attn_up, w_o=m_w_o, norm_mlp=m_norm_mlp, w_ff1=m_w_ff1, w_ff2=m_w_ff2)
    vels = dict(norm_mix=v_norm_mix, w_in=v_w_in, b_gate=v_b_gate, pool_mix=v_pool_mix, pool_scale=v_pool_scale,
                conv_w=v_conv_w, q_gain=v_q_gain, k_gain=v_k_gain, w_pool_up=v_w_pool_up, w_conv_out=v_w_conv_out,
                w_attn_up=v_w_attn_up, w_o=v_w_o, norm_mlp=v_norm_mlp, w_ff1=v_w_ff1, w_ff2=v_w_ff2)
    depth = norm_mix.shape[0]
    q = 2 * lax.axis_index("x") + lax.axis_index("y")
    for group in (weights, moms, vels):
        group["w_in"] = jnp.swapaxes(group["w_in"], 1, 2)

    assert depth == 2, "the second layer's gather hides behind the first layer's forward, and likewise backward"
    first, rest = BIG[:1], BIG[1:]
    cw_all = _all_to_all_small(_pad8(conv_w))
    bufs = [{n: _cast_place(weights[n], 0, f"cast_{n}_l0") for n in first}]
    a_ssem, a_rsem, a_views, a_token = _gather_start([bufs[0][n] for n in first], "gather_start_l0_in", cw_all)
    bufs[0].update({n: _cast_place(weights[n], 0, f"cast_{n}_l0") for n in rest})
    bufs += [{n: _cast_place(weights[n], l, f"cast_{n}_l{l}") for n in BIG} for l in range(1, depth)]
    b_ssem, b_rsem, b_views, b_token = _gather_start([bufs[0][n] for n in rest], "gather_start_l0_rest", a_token)
    g_ssem, g_rsem, g_views, g_token = _gather_start([bufs[1][n] for n in BIG], "gather_start_l1", b_token)
    conv_w_full = jnp.concatenate(
        [cw_all[2 * p].reshape(-1)[:conv_w.size].reshape(conv_w.shape) for p in range(N_CHIPS)], axis=-1)
    small = dict(weights)
    small["conv_w_full"] = conv_w_full

    def soon_weights(t):
        got = _gather_finish(a_ssem, a_rsem, a_views, t, "gather_wait_l0_in", "gather_forward_l0_in",
                             [bufs[0][n].shape for n in first])
        return _matrix_weights(dict(zip(first, got)))

    def late_weights(t):
        got = _gather_finish(b_ssem, b_rsem, b_views, t, "gather_wait_l0_rest", "gather_forward_l0_rest",
                             [bufs[0][n].shape for n in rest])
        return _matrix_weights(dict(zip(rest, got)))

    wl, saved = [None] * depth, [None] * depth
    small_1 = _small_weights(1, small)
    (h, hb_1), saved[0], wl[0] = _layer_fwd(x[0], _small_weights(0, small), "l0", after=g_token, soon=soon_weights,
                                            late=late_weights, next_gain=small_1["norm_mix"])
    got = _gather_finish(g_ssem, g_rsem, g_views, h, "gather_wait_l1", "gather_forward_l1",
                         [bufs[1][n].shape for n in BIG])
    (dh, loss_row), saved[1], wl[1] = _layer_fwd(
        h, dict(small_1, **_matrix_weights(dict(zip(BIG, got)))), "l1", target=loss_target[0], hb=hb_1)

    def pair_stage(names, g, tag):
        views = [_halves(_to_chip_major(n, g[n])) for n in names]
        from_sibling = _pair_swap(views, f"grad_pair_swap_{tag}")
        return _pair_sum(views, from_sibling, f"pair_sum_{tag}")

    mine, other = [{}, {}], [{}, {}]

    def finish(names, l, started, after, tag):
        ssem, rsem, parts, landing, _ = started
        parts, arrived = _chip_exchange_wait(ssem, rsem, parts, landing, after, f"grad_chip_exchange_wait_{tag}")
        got = _chip_sum(parts, arrived, f"chip_sum_{tag}")
        mine[l].update(zip(names, got))
        other[l].update(zip(names, _pair_send(got, f"grad_pair_send_{tag}")))

    def small_pieces(g):
        return [_pad8(g[n][:3] if n == "conv_w" else g[n]) for n in SMALL]

    def start_small(l):
        pieces = small_pieces(grads[l]) + ([_pad8(loss_row)] if l == depth - 1 else [])
        return _all_to_all_small_start(jnp.concatenate(pieces, axis=0), f"small_grad_exchange_start_l{l}")

    grads, early, small = [None] * depth, {}, [None] * depth
    dh, grads[1] = _layer_bwd(dh, wl[1], saved[1], "l1")
    second = _chip_exchange_start(pair_stage(BIG, grads[1], "l1"), "grad_chip_exchange_start_l1")
    small[1] = start_small(1)

    def start_rest(g):
        early["rest"] = _chip_exchange_start(pair_stage(rest, g, "l0_rest"), "grad_chip_exchange_start_l0_rest")
        return early["rest"][4]

    def start_last(g):
        early["in"] = _chip_exchange_start(pair_stage(first, g, "l0_in"), "grad_chip_exchange_start_l0_in")
        return early["in"][4]

    dh, grads[0] = _layer_bwd(dh, wl[0], saved[0], "l0", after=[second[4], small[1][4]], mid=start_rest,
                              tail=start_last)
    small[0] = start_small(0)
    started = small[0][4]
    finish(BIG, 1, second, started, "l1")
    finish(rest, 0, early["rest"], started, "l0_rest")
    full = {}

    deltas, new_m, new_v = {}, {}, {}

    def update_matrices(names, tag):
        results = _adamw_halves(
            [weights[n] for n in names], [moms[n] for n in names], [vels[n] for n in names],
            [[mine[l][n] for n in names] for l in range(depth)], [[other[l][n] for n in names] for l in range(depth)],
            f"adamw_{tag}")
        for n, (g_, d_, m_, v_) in zip(names, results):
            full[n], deltas[n], new_m[n], new_v[n] = g_, d_, m_, v_

    update_matrices(rest, "rest")
    finish(first, 0, early["in"], deltas[rest[-1]], "l0_in")
    update_matrices(first, "in")
    summed = []
    for l in range(depth):
        ssem, rsem, part, landing, _ = small[l]
        summed.append(_sum_slices(_all_to_all_small_wait(ssem, rsem, part, landing, deltas[first[-1]],
                                                         f"small_grad_exchange_wait_l{l}"), f"small_sum_l{l}"))
    row = 0
    for n, piece in zip(SMALL, small_pieces(grads[0])):
        size = (weights[n].size if n != "conv_w" else depth * 3 * 512) // depth
        flat = jnp.stack([s[row:row + piece.shape[0]].reshape(-1)[:size] for s in summed])
        row += piece.shape[0]
        if n == "conv_w":
            full[n] = lax.dynamic_slice_in_dim(flat.reshape(depth, 3, 512), q * conv_w.shape[2], conv_w.shape[2], axis=2)
        else:
            full[n] = flat.reshape(weights[n].shape)
    loss = summed[depth - 1][row, 0]
    two_d = {n: (-1, weights[n].shape[-1]) if n not in ("conv_w", "q_gain", "k_gain") else (1, -1) for n in SMALL}
    results = _adamw(*[[group[n].reshape(two_d[n]) for n in SMALL] for group in (weights, full, moms, vels)],
                     "adamw_small")
    for n, (d2, m2, v2) in zip(SMALL, results):
        shape = weights[n].shape
        deltas[n], new_m[n], new_v[n] = d2.reshape(shape), m2.reshape(shape), v2.reshape(shape)
        full[n] = full[n].reshape(shape)
    for group in (full, deltas, new_m, new_v):
        group["w_in"] = jnp.swapaxes(group["w_in"], 1, 2)
    return (loss, dh[None], *[full[n] for n in ORDER], *[deltas[n] for n in ORDER], *[new_m[n] for n in ORDER],
            *[new_v[n] for n in ORDER])
```

```python
import functools

import jax
import jax.numpy as jnp
from jax import lax
from jax.experimental import pallas as pl
from jax.experimental.pallas import tpu as pltpu

F32 = jnp.float32
BF = jnp.bfloat16
MESH_ID = pl.DeviceIdType.MESH
ANY = pl.BlockSpec(memory_space=pl.ANY)

EPS = 1e-6
MASK_VALUE = -1e30
POOL_WINDOWS = (2, 4, 8, 16)
ATTN_DILATIONS = (1, 4, 16)
ATTN_BLOCK = 128
HEAD_DIM = 64
OFF_Q, OFF_K, OFF_V, OFF_GATE = 2048, 2816, 3584, 4352
N_CHIPS = 4
ADAM_LR, ADAM_B1, ADAM_B2, ADAM_EPS, ADAM_WD, ADAM_STEP = 0.001, 0.9, 0.999, 1e-08, 0.01, 10

VMEM_LIMIT = 48 * 1024 * 1024
LANES = 128

_DIMS = {"nn": (((1,), (0,)), ((), ())), "nt": (((1,), (1,)), ((), ())), "tn": (((0,), (0,)), ((), ()))}


def _params(sem):
    return pltpu.CompilerParams(dimension_semantics=sem, vmem_limit_bytes=VMEM_LIMIT)


def _pallas_call(body, **kw):
    def in_hbm(s):
        pin = isinstance(s, jax.ShapeDtypeStruct) and s is not TOKEN and jnp.issubdtype(s.dtype, jnp.floating)
        return pltpu.HBM(s.shape, s.dtype) if pin else s

    out_shape = kw.pop("out_shape")
    kw["out_shape"] = [in_hbm(s) for s in out_shape] if isinstance(out_shape, (list, tuple)) else in_hbm(out_shape)
    call = pl.pallas_call(body, **kw)

    def run(*args):
        pinned = [pltpu.with_memory_space_constraint(a, pltpu.HBM)
                  if hasattr(a, "dtype") and jnp.issubdtype(a.dtype, jnp.floating) else a for a in args]
        return call(*pinned)

    return run


def _dot(a, b, mode="nn"):
    return lax.dot_general(a, b, _DIMS[mode], preferred_element_type=F32)


def _mm(a, b, mode, name, *, tm, tn, tk, out_dtype=F32, res=None, aux=None, epi=None, n_outer=False,
        b_shards=False, out_shards=False, after=None, vec=None):
    if mode == "tn":
        K, M = a.shape
    else:
        M, K = a.shape
    if b_shards:
        if mode == "nn":
            assert b.shape[1] == K
            N = b.shape[2] * N_CHIPS
        else:
            assert mode == "nt"
            N = b.shape[1]
            assert b.shape[2] * N_CHIPS == K
    else:
        N = b.shape[0] if mode == "nt" else b.shape[1]
    tm, tn, tk = min(tm, M), min(tn, N), min(tk, K)
    assert M % tm == 0 and N % tn == 0 and K % tk == 0
    nk = K // tk
    if n_outer:
        grid = (N // tn, M // tm, nk)
        ij = lambda p, q_: (q_, p)
    else:
        grid = (M // tm, N // tn, nk)
        ij = lambda p, q_: (p, q_)

    def amap(p, q_, k):
        i, j = ij(p, q_)
        return (k, i) if mode == "tn" else (i, k)

    a_spec = pl.BlockSpec((tk, tm) if mode == "tn" else (tm, tk), amap)
    if b_shards:
        if mode == "nn":
            per = (N // N_CHIPS) // tn
            assert per >= 1 and (N // N_CHIPS) % tn == 0

            def bmap(p, q_, k):
                i, j = ij(p, q_)
                return (j // per, k, j % per)

            b_spec = pl.BlockSpec((None, tk, tn), bmap)
        else:
            per = (K // N_CHIPS) // tk
            assert per >= 1 and (K // N_CHIPS) % tk == 0

            def bmap(p, q_, k):
                i, j = ij(p, q_)
                return (k // per, j, k % per)

            b_spec = pl.BlockSpec((None, tn, tk), bmap)
    else:
        def bmap(p, q_, k):
            i, j = ij(p, q_)
            return (j, k) if mode == "nt" else (k, j)

        b_spec = pl.BlockSpec((tn, tk) if mode == "nt" else (tk, tn), bmap)

    def omap(p, q_, k):
        return ij(p, q_)

    o_spec = pl.BlockSpec((tm, tn), omap)
    if out_shards:
        per_o = (N // N_CHIPS) // tn
        assert per_o >= 1 and (N // N_CHIPS) % tn == 0

        def osmap(p, q_, k):
            i, j = ij(p, q_)
            return (j // per_o, i, j % per_o)

        out_spec0 = pl.BlockSpec((None, tm, tn), osmap)
        out_shape0 = jax.ShapeDtypeStruct((N_CHIPS, M, N // N_CHIPS), out_dtype)
    else:
        out_spec0 = o_spec
        out_shape0 = jax.ShapeDtypeStruct((M, N), out_dtype)

    in_specs = [a_spec, b_spec]
    args = [a, b]
    if res is not None:
        in_specs.append(o_spec)
        args.append(res)
    if aux is not None:
        in_specs.append(o_spec)
        args.append(aux)
    if vec is not None:
        in_specs.append(pl.BlockSpec((1, tn), lambda p, q_, k: (0, ij(p, q_)[1])))
        args.append(vec)
    after = [] if after is None else list(after) if isinstance(after, (list, tuple)) else [after]
    in_specs += [ANY] * len(after)
    args += after
    out_specs = [out_spec0]
    out_shape = [out_shape0]
    reduces = epi in ("loss", "rms_bwd")
    if reduces:
        assert tn == N and not n_outer and not out_shards
        width = LANES if epi == "loss" else N
        out_specs.append(pl.BlockSpec((1, width), lambda p, q_, k: (0, 0)))
        out_shape.append(jax.ShapeDtypeStruct((1, width), F32))
    if epi == "rms_next":
        assert tn == N and not out_shards
        out_specs.append(o_spec)
        out_shape.append(jax.ShapeDtypeStruct((M, N), BF))
    n_out = len(out_shape)
    has_res, has_aux, has_vec, n_after = res is not None, aux is not None, vec is not None, len(after)

    def body(*refs):
        a_ref, b_ref = refs[0], refs[1]
        pos = 2
        res_ref = aux_ref = vec_ref = None
        if has_res:
            res_ref = refs[pos]
            pos += 1
        if has_aux:
            aux_ref = refs[pos]
            pos += 1
        if has_vec:
            vec_ref = refs[pos]
            pos += 1
        pos += n_after
        outs = refs[pos:pos + n_out]
        part = _dot(a_ref[...].astype(BF), b_ref[...].astype(BF), mode)

        first_row_tile = pl.program_id(0) == 0

        def add_to_sum(row):
            @pl.when(first_row_tile)
            def _():
                outs[1][...] = jnp.zeros_like(outs[1])

            outs[1][...] += row

        def finish(acc):
            if epi == "rms_bwd":
                xv = aux_ref[...]
                r = lax.rsqrt(jnp.mean(xv * xv, axis=-1, keepdims=True) + EPS)
                xhat = xv * r
                dy = acc * vec_ref[...]
                outs[0][...] = res_ref[...] + r * (dy - xhat * jnp.mean(dy * xhat, axis=-1, keepdims=True))
                add_to_sum(jnp.sum(acc * xhat, axis=0, keepdims=True))
                return
            if res_ref is not None:
                acc = res_ref[...] + acc
            if epi == "relu2":
                r = jnp.maximum(acc, 0.0)
                outs[0][...] = (r * r).astype(out_dtype)
            elif epi == "drelu2":
                outs[0][...] = (acc.astype(BF) * (2.0 * jnp.sqrt(aux_ref[...]))).astype(out_dtype)
            elif epi == "rms_next":
                outs[0][...] = acc
                r = lax.rsqrt(jnp.mean(acc * acc, axis=-1, keepdims=True) + EPS)
                outs[1][...] = ((acc * r) * vec_ref[...]).astype(BF)
            elif epi == "loss":
                e = acc - aux_ref[...]
                outs[0][...] = e / float(N)
                add_to_sum(0.5 * jnp.sum(jnp.mean(e * e, axis=-1, keepdims=True)))
            else:
                outs[0][...] = acc.astype(out_dtype)

        if nk == 1:
            finish(part)
        else:
            acc_ref = refs[pos + n_out]
            k = pl.program_id(2)

            @pl.when(k == 0)
            def _():
                acc_ref[...] = part

            @pl.when(k > 0)
            def _():
                acc_ref[...] += part

            @pl.when(k == nk - 1)
            def _():
                finish(acc_ref[...])

    scratch = [pltpu.VMEM((tm, tn), F32)] if nk > 1 else []
    out = _pallas_call(
        body, name=name, grid=grid, in_specs=in_specs, out_specs=out_specs, out_shape=out_shape,
        scratch_shapes=scratch,
        compiler_params=_params(("arbitrary" if reduces else "parallel", "parallel", "arbitrary")),
    )(*args)
    return out if n_out > 1 else out[0]


def _rms_fwd(x, gain, name, after=None):
    T, D = x.shape
    tm = min(512, T)

    def body(x_ref, g_ref, *rest):
        o_ref = rest[-1]
        xv = x_ref[...]
        r = lax.rsqrt(jnp.mean(xv * xv, axis=-1, keepdims=True) + EPS)
        o_ref[...] = ((xv * r) * g_ref[...]).astype(BF)

    extra = [] if after is None else list(after) if isinstance(after, (list, tuple)) else [after]
    return _pallas_call(
        body, name=name, grid=(T // tm,),
        in_specs=[pl.BlockSpec((tm, D), lambda i: (i, 0)), pl.BlockSpec((1, D), lambda i: (0, 0))] + [ANY] * len(extra),
        out_specs=pl.BlockSpec((tm, D), lambda i: (i, 0)), out_shape=jax.ShapeDtypeStruct((T, D), BF),
        compiler_params=_params(("parallel",)),
    )(x, gain, *extra)


def _rms_bwd(dh, x, gain, dres, name):
    T, D = x.shape
    tm = min(512, T)

    def body(dh_ref, x_ref, g_ref, dres_ref, dx_ref, dg_ref):
        xv = x_ref[...]
        r = lax.rsqrt(jnp.mean(xv * xv, axis=-1, keepdims=True) + EPS)
        xhat = xv * r
        dhv = dh_ref[...]
        dy = dhv * g_ref[...]
        dx_ref[...] = dres_ref[...] + r * (dy - xhat * jnp.mean(dy * xhat, axis=-1, keepdims=True))

        @pl.when(pl.program_id(0) == 0)
        def _():
            dg_ref[...] = jnp.zeros_like(dg_ref)

        dg_ref[...] += jnp.sum(dhv * xhat, axis=0, keepdims=True)

    row = pl.BlockSpec((tm, D), lambda i: (i, 0))
    vec = pl.BlockSpec((1, D), lambda i: (0, 0))
    return _pallas_call(
        body, name=name, grid=(T // tm,), in_specs=[row, row, vec, row], out_specs=[row, vec],
        out_shape=[jax.ShapeDtypeStruct((T, D), F32), jax.ShapeDtypeStruct((1, D), F32)],
        compiler_params=_params(("arbitrary",)),
    )(dh, x, gain, dres)


POOL_HALO = 16
CONV_HALO = 8
POOLCONV_ROWS = 512


def _causal_window_sum(v, w):
    s, sh = v, 1
    while sh < w:
        s = s + pltpu.roll(s, sh, 0)
        sh *= 2
    return s


def _anticausal_window_sum(v, w):
    n = v.shape[0]
    s, sh = v, 1
    while sh < w:
        s = s + pltpu.roll(s, n - sh, 0)
        sh *= 2
    return s


def _poolconv_fwd(z, pmix_b, pscale, convw, name):
    T = z.shape[0]
    R = min(POOLCONV_ROWS, T)
    PH, CH = R // POOL_HALO, R // CONV_HALO

    def body(u_ref, uh_ref, b_ref, c_ref, ch_ref, x_ref, xh_ref, mix_ref, sc_ref, cw_ref, yp_ref, yc_ref):
        i = pl.program_id(0)
        keep = (i > 0).astype(F32)
        row = i * R + lax.broadcasted_iota(jnp.int32, (R, 1), 0)
        w_all = jnp.concatenate([uh_ref[...] * keep, u_ref[...]], axis=0)
        for g, w in enumerate(POOL_WINDOWS):
            cols = slice(128 * g, 128 * (g + 1))
            wg = w_all[:, cols]
            s = _causal_window_sum(wg, w)[POOL_HALO:]
            inv_cnt = 1.0 / jnp.minimum(row + 1, w).astype(F32)
            dgrp = s * inv_cnt - wg[POOL_HALO:]
            y = _dot(dgrp.astype(BF), mix_ref[g]) * sc_ref[:, cols]
            yp_ref[:, cols] = y.astype(BF)
        uc = jnp.concatenate([ch_ref[...] * xh_ref[...] * keep, c_ref[...] * x_ref[...]], axis=0)
        yc = cw_ref[2:3, :] * uc + cw_ref[0:1, :] * pltpu.roll(uc, 2, 0) + cw_ref[1:2, :] * pltpu.roll(uc, 1, 0)
        yc_ref[...] = (b_ref[...] * yc[CONV_HALO:]).astype(BF)

    def main(cb):
        return pl.BlockSpec((R, 512), lambda i: (i, cb))

    def prev(cb, halo, per):
        return pl.BlockSpec((halo, 512), lambda i: (jnp.maximum(i * per - 1, 0), cb))

    full = lambda a: pl.BlockSpec(a.shape, lambda i: (0,) * a.ndim)
    return _pallas_call(
        body, name=name, grid=(T // R,),
        in_specs=[main(0), prev(0, POOL_HALO, PH), main(1), main(2), prev(2, CONV_HALO, CH), main(3),
                  prev(3, CONV_HALO, CH), full(pmix_b), full(pscale), full(convw)],
        out_specs=[pl.BlockSpec((R, 512), lambda i: (i, 0))] * 2,
        out_shape=[jax.ShapeDtypeStruct((T, 512), BF)] * 2,
        compiler_params=_params(("parallel",)),
    )(z, z, z, z, z, z, z, pmix_b, pscale, convw)


def _poolconv_bwd(z, dyp, dyc, pmix_b, pscale, convw, dz, name):
    T = z.shape[0]
    R = min(POOLCONV_ROWS, T)
    PH, CH = R // POOL_HALO, R // CONV_HALO
    nsteps = T // R

    def body(u_ref, uh_ref, b_ref, bn_ref, c_ref, ch_ref, x_ref, xh_ref, dyp_ref, dypn_ref, dyc_ref, dycn_ref,
             mix_ref, sc_ref, cw_ref, dz_in_ref, dz_ref, dmix_ref, dsc_ref, dcw_ref):
        i = pl.program_id(0)
        keep_prev = (i > 0).astype(F32)
        keep_next = (i < nsteps - 1).astype(F32)

        @pl.when(i == 0)
        def _():
            dmix_ref[...] = jnp.zeros_like(dmix_ref)
            dsc_ref[...] = jnp.zeros_like(dsc_ref)
            dcw_ref[...] = jnp.zeros_like(dcw_ref)

        row = i * R + lax.broadcasted_iota(jnp.int32, (R, 1), 0)
        row_ext = i * R + lax.broadcasted_iota(jnp.int32, (R + POOL_HALO, 1), 0)
        w_all = jnp.concatenate([uh_ref[...] * keep_prev, u_ref[...]], axis=0)
        dyp_ext = jnp.concatenate([dyp_ref[...], dypn_ref[...] * keep_next], axis=0)
        for g, w in enumerate(POOL_WINDOWS):
            cols = slice(128 * g, 128 * (g + 1))
            wg = w_all[:, cols]
            s = _causal_window_sum(wg, w)[POOL_HALO:]
            inv_cnt = 1.0 / jnp.minimum(row + 1, w).astype(F32)
            dgrp = (s * inv_cnt - wg[POOL_HALO:]).astype(BF)
            y_pre = _dot(dgrp, mix_ref[g])
            dsc_ref[:, cols] += jnp.sum(dyp_ref[:, cols] * y_pre, axis=0, keepdims=True)
            dyb = (dyp_ext[:, cols] * sc_ref[:, cols]).astype(BF)
            dmix_ref[cols, :] += _dot(dgrp, dyb[:R], "tn")
            dd = _dot(dyb, mix_ref[g], "nt")
            inv_cnt_ext = 1.0 / jnp.minimum(row_ext + 1, w).astype(F32)
            e = _anticausal_window_sum(dd * inv_cnt_ext, w)
            dz_ref[:, cols] = (e[:R] - dd[:R]).astype(BF)
        cw0, cw1, cw2 = cw_ref[0:1, :], cw_ref[1:2, :], cw_ref[2:3, :]
        uc = jnp.concatenate([ch_ref[...] * xh_ref[...] * keep_prev, c_ref[...] * x_ref[...]], axis=0)
        uc1 = pltpu.roll(uc, 1, 0)[CONV_HALO:]
        uc2 = pltpu.roll(uc, 2, 0)[CONV_HALO:]
        uc0 = uc[CONV_HALO:]
        yc = cw2 * uc0 + cw0 * uc2 + cw1 * uc1
        dycv = dyc_ref[...]
        dz_ref[:, 512:1024] = (dycv * yc).astype(BF)
        dv_ext = jnp.concatenate([dycv * b_ref[...], dycn_ref[...] * bn_ref[...] * keep_next], axis=0)
        n_ext = R + CONV_HALO
        duc = (cw2 * dv_ext + cw1 * pltpu.roll(dv_ext, n_ext - 1, 0) + cw0 * pltpu.roll(dv_ext, n_ext - 2, 0))[:R]
        dv = dv_ext[:R]
        dcw_ref[0:1, :] += jnp.sum(dv * uc2, axis=0, keepdims=True)
        dcw_ref[1:2, :] += jnp.sum(dv * uc1, axis=0, keepdims=True)
        dcw_ref[2:3, :] += jnp.sum(dv * uc0, axis=0, keepdims=True)
        dz_ref[:, 1024:1536] = (duc * x_ref[...]).astype(BF)
        dz_ref[:, 1536:2048] = (duc * c_ref[...]).astype(BF)

    def main(cb):
        return pl.BlockSpec((R, 512), lambda i: (i, cb))

    def prev(cb, halo, per):
        return pl.BlockSpec((halo, 512), lambda i: (jnp.maximum(i * per - 1, 0), cb))

    def nxt(cb, halo, per):
        return pl.BlockSpec((halo, 512), lambda i: (jnp.minimum((i + 1) * per, T // halo - 1), cb))

    full = lambda a: pl.BlockSpec(a.shape, lambda i: (0,) * a.ndim)
    return _pallas_call(
        body, name=name, grid=(nsteps,),
        in_specs=[main(0), prev(0, POOL_HALO, PH), main(1), nxt(1, CONV_HALO, CH), main(2), prev(2, CONV_HALO, CH),
                  main(3), prev(3, CONV_HALO, CH), main(0), nxt(0, POOL_HALO, PH), main(0), nxt(0, CONV_HALO, CH),
                  full(pmix_b), full(pscale), full(convw), ANY],
        out_specs=[pl.BlockSpec((R, 2048), lambda i: (i, 0)), pl.BlockSpec((512, 128), lambda i: (0, 0)),
                   pl.BlockSpec((1, 512), lambda i: (0, 0)), pl.BlockSpec((8, 512), lambda i: (0, 0))],
        out_shape=[jax.ShapeDtypeStruct(dz.shape, BF), jax.ShapeDtypeStruct((512, 128), F32),
                   jax.ShapeDtypeStruct((1, 512), F32), jax.ShapeDtypeStruct((8, 512), F32)],
        input_output_aliases={15: 0}, compiler_params=_params(("arbitrary",)),
    )(z, z, z, z, z, z, z, z, dyp, dyp, dyc, dyc, pmix_b, pscale, convw, dz)


def _head_sums(v):
    row = lax.broadcasted_iota(jnp.int32, (LANES, LANES), 0) < HEAD_DIM
    col = lax.broadcasted_iota(jnp.int32, (LANES, LANES), 1) < HEAD_DIM
    same_head = jnp.where(jnp.logical_xor(row, col), 0.0, 1.0).astype(BF)
    hi = v.astype(BF)
    lo = (v - hi.astype(F32)).astype(BF)
    return _dot(hi, same_head) + _dot(lo, same_head)


def _head_norm(x, g2, ma):
    r = lax.rsqrt(_head_sums(x * x) / HEAD_DIM + EPS)
    return x * r, r


def _head_norm_bwd(dy, xhat, r, g2, ma):
    dxh = dy * g2
    return r * (dxh - xhat * (_head_sums(dxh * xhat) / HEAD_DIM))


def _attn_masks(other_block_exists):
    lane = lax.broadcasted_iota(jnp.int32, (2 * ATTN_BLOCK, ATTN_BLOCK), 1)
    qi = lax.broadcasted_iota(jnp.int32, (2 * ATTN_BLOCK, ATTN_BLOCK), 0) & (ATTN_BLOCK - 1)
    never = (1 - other_block_exists.astype(jnp.int32)) * (2 * ATTN_BLOCK)
    return lane[:ATTN_BLOCK] < HEAD_DIM, lane <= qi, lane >= qi + never


def _stack_heads(x, ma):
    return jnp.concatenate([jnp.where(ma, x, 0.0), jnp.where(ma, 0.0, x)], axis=0)


def _unstack_heads(y, ma):
    return jnp.where(ma, y[:ATTN_BLOCK], y[ATTN_BLOCK:])


def _stack_cols(tile, ma):
    return jnp.concatenate([tile[:, 0:1], tile[:, HEAD_DIM:HEAD_DIM + 1]], axis=0)


QKV_TILES = (OFF_GATE - OFF_Q) // LANES
KIND_TILES = QKV_TILES // 3


def _qk_norm(z, gains, name):
    T = z.shape[0]
    tm = min(512, T)

    def body(x_ref, g_ref, o_ref):
        ma = lax.broadcasted_iota(jnp.int32, (tm, LANES), 1) < HEAD_DIM
        for tile in range(QKV_TILES):
            v = x_ref[:, LANES * tile:LANES * (tile + 1)]
            if tile < 2 * KIND_TILES:
                g = g_ref[0:1, :] if tile < KIND_TILES else g_ref[1:2, :]
                v = _head_norm(v, g, ma)[0] * g
            o_ref[tile] = v

    return _pallas_call(
        body, name=name, grid=(T // tm,),
        in_specs=[pl.BlockSpec((pl.Element(tm), pl.Element(OFF_GATE - OFF_Q)), lambda i: (i * tm, OFF_Q)),
                  pl.BlockSpec((8, LANES), lambda i: (0, 0))],
        out_specs=pl.BlockSpec((QKV_TILES, tm, LANES), lambda i: (0, i, 0)),
        out_shape=jax.ShapeDtypeStruct((QKV_TILES, T, LANES), F32), compiler_params=_params(("parallel",)),
    )(z, gains)


ATTN_STEP_ROWS = 2048
ATTN_UNROLL = 4


def _attn_steps(T):
    assert ATTN_STEP_ROWS == ATTN_BLOCK * max(ATTN_DILATIONS) and T % ATTN_STEP_ROWS == 0
    return T // ATTN_STEP_ROWS


def _attn_rows(jj, r, sub, d):
    start = jj * sub + r
    if d == 1:
        return pl.ds(pl.multiple_of(start, ATTN_BLOCK), ATTN_BLOCK)
    return pl.ds(start, ATTN_BLOCK, stride=d)


def _pick(flag, a, b):
    return jnp.where(jnp.full(a.shape, flag.astype(jnp.int32)) > 0, a, b)


def _attn_fwd(qkv, name):
    T = qkv.shape[1]
    nbig = _attn_steps(T)
    scale = HEAD_DIM ** -0.5

    def body(q_ref, kc_ref, kp_ref, vc_ref, vp_ref, o_ref, lse_ref):
        jb = pl.program_id(1)
        for gi, d in enumerate(ATTN_DILATIONS):
            pl.when(pl.program_id(0) == gi)(functools.partial(group, d, jb, q_ref, kc_ref, kp_ref, vc_ref, vp_ref,
                                                              o_ref, lse_ref))

    def group(d, jb, q_ref, kc_ref, kp_ref, vc_ref, vp_ref, o_ref, lse_ref):
        sub, m = ATTN_BLOCK * d, ATTN_STEP_ROWS // (ATTN_BLOCK * d)

        def step(s, carry):
            jj, r = s // d, s % d
            here, before = _attn_rows(jj, r, sub, d), _attn_rows(jnp.maximum(jj - 1, 0), r, sub, d)
            edge = _attn_rows(m - 1, r, sub, d)
            first = jj == 0
            ma, mask_c, mask_p = _attn_masks(jb * m + jj > 0)
            qs = _stack_heads(q_ref[here, :], ma).astype(BF)
            kcb = kc_ref[here, :].astype(BF)
            kpb = _pick(first, kp_ref[edge, :], kc_ref[before, :]).astype(BF)
            vcb = vc_ref[here, :].astype(BF)
            vpb = _pick(first, vp_ref[edge, :], vc_ref[before, :]).astype(BF)
            s_c = jnp.where(mask_c, _dot(qs, kcb, "nt") * scale, MASK_VALUE)
            s_p = jnp.where(mask_p, _dot(qs, kpb, "nt") * scale, MASK_VALUE)
            mx = jnp.maximum(jnp.max(s_c, axis=-1, keepdims=True), jnp.max(s_p, axis=-1, keepdims=True))
            p_c = jnp.exp(s_c - mx)
            p_p = jnp.exp(s_p - mx)
            den = jnp.sum(p_c, axis=-1, keepdims=True) + jnp.sum(p_p, axis=-1, keepdims=True)
            o = (_dot(p_c.astype(BF), vcb) + _dot(p_p.astype(BF), vpb)) / den
            o_ref[here, :] = _unstack_heads(o, ma)
            lse_ref[here, :] = _unstack_heads(jnp.broadcast_to(mx + jnp.log(den), o.shape), ma)
            return carry

        lax.fori_loop(0, m * d, step, 0, unroll=ATTN_UNROLL)

    def cur(kind):
        return pl.BlockSpec((None, ATTN_STEP_ROWS, LANES), lambda g, j, t: (KIND_TILES * kind + 2 * g + t, j, 0))

    def prv(kind):
        return pl.BlockSpec((None, ATTN_STEP_ROWS, LANES),
                            lambda g, j, t: (KIND_TILES * kind + 2 * g + t, jnp.maximum(j - 1, 0), 0))

    out = pl.BlockSpec((ATTN_STEP_ROWS, LANES), lambda g, j, t: (j, 2 * g + t))
    width = 2 * LANES * len(ATTN_DILATIONS)
    return _pallas_call(
        body, name=name, grid=(len(ATTN_DILATIONS), nbig, 2), in_specs=[cur(0), cur(1), prv(1), cur(2), prv(2)],
        out_specs=[out, out], out_shape=[jax.ShapeDtypeStruct((T, width), F32)] * 2,
        compiler_params=_params(("parallel", "parallel", "parallel")),
    )(qkv, qkv, qkv, qkv, qkv)


def _attn_bwd(z, qkv, do, c, lse, gains, name, after=None):
    T = z.shape[0]
    nbig = _attn_steps(T)
    scale = HEAD_DIM ** -0.5
    extra = [] if after is None else [after]

    def body(*refs):
        g, jb = pl.program_id(0), pl.program_id(1)
        dgq_ref, dgk_ref = refs[len(refs) - 5], refs[len(refs) - 4]

        @pl.when((g == 0) & (jb == 0) & (pl.program_id(2) == 0))
        def _():
            dgq_ref[...] = jnp.zeros_like(dgq_ref)
            dgk_ref[...] = jnp.zeros_like(dgk_ref)

        for gi, d in enumerate(ATTN_DILATIONS):
            pl.when(g == gi)(functools.partial(group, d, jb, *refs))

    def group(d, jb, qr_ref, kr_ref, vc_ref, vp_ref, qn_ref, qnn_ref, kn_ref, knp_ref, do_ref, don_ref, c_ref, cn_ref,
              lse_ref, lsen_ref, g_ref, *rest):
        dq_ref, dk_ref, dv_ref, dgq_ref, dgk_ref, sq_ref, sk_ref, sv_ref = rest[len(extra):]
        sub, m = ATTN_BLOCK * d, ATTN_STEP_ROWS // (ATTN_BLOCK * d)
        nb = T // sub
        gq, gk = g_ref[0:1, :], g_ref[1:2, :]

        def step(s, carry):
            jj, r = s // d, s % d
            here = _attn_rows(jj, r, sub, d)
            before = _attn_rows(jnp.maximum(jj - 1, 0), r, sub, d)
            behind = _attn_rows(jnp.minimum(jj + 1, m - 1), r, sub, d)
            edge_before, edge_behind = _attn_rows(m - 1, r, sub, d), _attn_rows(0, r, sub, d)
            first, last = jj == 0, jj == m - 1
            block = jb * m + jj
            ma, mask_c, mask_p = _attn_masks(block > 0)
            mask_n = _attn_masks(block < nb - 1)[2]
            qhat, rq = _head_norm(qr_ref[here, :], gq, ma)
            qn = qn_ref[here, :]
            qn_next = _pick(last, qnn_ref[edge_behind, :], qn_ref[behind, :])
            khat, rk = _head_norm(kr_ref[here, :], gk, ma)
            kcb = kn_ref[here, :].astype(BF)
            kpb = _pick(first, knp_ref[edge_before, :], kn_ref[before, :]).astype(BF)
            vcb = vc_ref[here, :].astype(BF)
            vpb = _pick(first, vp_ref[edge_before, :], vc_ref[before, :]).astype(BF)
            do_t, don_t = do_ref[here, :], _pick(last, don_ref[edge_behind, :], do_ref[behind, :])
            c_t, cn_t = c_ref[here, :], _pick(last, cn_ref[edge_behind, :], c_ref[behind, :])
            lse_t, lsen_t = lse_ref[here, :], _pick(last, lsen_ref[edge_behind, :], lse_ref[behind, :])
            qs, dos = _stack_heads(qn, ma).astype(BF), _stack_heads(do_t, ma).astype(BF)
            lse_s, c_s = _stack_cols(lse_t, ma), _stack_cols(c_t, ma)
            s_c = jnp.where(mask_c, _dot(qs, kcb, "nt") * scale, MASK_VALUE)
            s_p = jnp.where(mask_p, _dot(qs, kpb, "nt") * scale, MASK_VALUE)
            p_c = jnp.exp(s_c - lse_s)
            p_p = jnp.exp(s_p - lse_s)
            ds_c = ((p_c * (_dot(dos, vcb, "nt") + c_s)) * scale).astype(BF)
            ds_p = ((p_p * (_dot(dos, vpb, "nt") + c_s)) * scale).astype(BF)
            dq_t = _unstack_heads(_dot(ds_c, kcb) + _dot(ds_p, kpb), ma)
            qs_n, dos_n = _stack_heads(qn_next, ma).astype(BF), _stack_heads(don_t, ma).astype(BF)
            s_n = jnp.where(mask_n, _dot(qs_n, kcb, "nt") * scale, MASK_VALUE)
            p_n = jnp.exp(s_n - _stack_cols(lsen_t, ma))
            ds_n = ((p_n * (_dot(dos_n, vcb, "nt") + _stack_cols(cn_t, ma))) * scale).astype(BF)
            dv_t = _dot(p_c.astype(BF), dos, "tn") + _dot(p_n.astype(BF), dos_n, "tn")
            dk_t = _dot(ds_c, qs, "tn") + _dot(ds_n, qs_n, "tn")
            sq_ref[here, :] = _head_norm_bwd(dq_t, qhat, rq, gq, ma)
            sk_ref[here, :] = _head_norm_bwd(dk_t, khat, rk, gk, ma)
            sv_ref[here, :] = dv_t
            dgq_ref[...] += jnp.sum(dq_t * qhat, axis=0, keepdims=True)
            dgk_ref[...] += jnp.sum(dk_t * khat, axis=0, keepdims=True)
            return carry

        lax.fori_loop(0, m * d, step, 0, unroll=ATTN_UNROLL)
        dq_ref[...] = sq_ref[...].astype(BF)
        dk_ref[...] = sk_ref[...].astype(BF)
        dv_ref[...] = sv_ref[...].astype(BF)

    rows = ATTN_STEP_ROWS

    def raw(col0):
        return pl.BlockSpec((rows, LANES), lambda g, j, t: (j, col0 + 2 * g + t))

    def cur(kind):
        return pl.BlockSpec((None, rows, LANES), lambda g, j, t: (KIND_TILES * kind + 2 * g + t, j, 0))

    def prv(kind):
        return pl.BlockSpec((None, rows, LANES), lambda g, j, t: (KIND_TILES * kind + 2 * g + t, jnp.maximum(j - 1, 0), 0))

    def nxt(kind):
        return pl.BlockSpec((None, rows, LANES),
                            lambda g, j, t: (KIND_TILES * kind + 2 * g + t, jnp.minimum(j + 1, nbig - 1), 0))

    own = pl.BlockSpec((rows, LANES), lambda g, j, t: (j, 2 * g + t))
    own_next = pl.BlockSpec((rows, LANES), lambda g, j, t: (jnp.minimum(j + 1, nbig - 1), 2 * g + t))
    vec = pl.BlockSpec((1, LANES), lambda g, j, t: (0, 0))
    width = 2 * LANES * len(ATTN_DILATIONS)
    return _pallas_call(
        body, name=name, grid=(len(ATTN_DILATIONS), nbig, 2),
        in_specs=[raw(OFF_Q // LANES), raw(OFF_K // LANES), cur(2), prv(2), cur(0), nxt(0), cur(1), prv(1), own, own_next,
                  own, own_next, own, own_next, pl.BlockSpec((8, LANES), lambda g, j, t: (0, 0))] + [ANY] * len(extra),
        out_specs=[own, own, own, vec, vec],
        out_shape=[jax.ShapeDtypeStruct((T, width), BF)] * 3 + [jax.ShapeDtypeStruct((1, LANES), F32)] * 2,
        scratch_shapes=[pltpu.VMEM((rows, LANES), F32)] * 3,
        compiler_params=_params(("arbitrary", "arbitrary", "arbitrary")),
    )(z, z, qkv, qkv, qkv, qkv, qkv, qkv, do, do, c, c, lse, lse, gains, *extra)


MERGE_ROWS = 256
GATE_TILE = 256


def _group_mix(o_refs, lse_refs):
    lses = [r[...] for r in lse_refs]
    m = jnp.maximum(jnp.maximum(lses[0], lses[1]), lses[2])
    es = [jnp.exp(l - m) for l in lses]
    den = es[0] + es[1] + es[2]
    ws = [e / den for e in es]
    y = ws[0] * o_refs[0][...] + ws[1] * o_refs[1][...] + ws[2] * o_refs[2][...]
    return ws, y


def _sigmoid(v):
    return 1.0 / (1.0 + jnp.exp(-v))


def _merge_specs(T, z, bgate, gpu, gco, gau):
    tm = min(MERGE_ROWS, T)
    row = lambda w: pl.BlockSpec((tm, w), lambda i: (i, 0))
    gates = pl.BlockSpec((pl.Element(tm), pl.Element(3 * 1024)), lambda i: (i * tm, OFF_GATE))
    full = lambda a: pl.BlockSpec(a.shape, lambda i: (0,) * a.ndim)
    by_group = [pl.BlockSpec((tm, 256), functools.partial(lambda i, g: (i, g), g=g)) for g in range(3)]
    specs = [row(512), row(512)] + by_group * 2 + [gates, full(bgate), full(gpu), full(gco), full(gau)]
    return tm, row, specs


def _merge_fwd(yp, yc, o3, lse3, z, bgate, gpu, gco, gau, name):
    T = yp.shape[0]
    tm, row, specs = _merge_specs(T, z, bgate, gpu, gco, gau)

    def body(*refs):
        yp_ref, yc_ref = refs[0], refs[1]
        o_refs, lse_refs = refs[2:5], refs[5:8]
        zg_ref = refs[8]
        b_ref, gpu_ref, gco_ref, gau_ref, out_ref = refs[9:14]
        yab = _group_mix(o_refs, lse_refs)[1].astype(BF)
        ys = (yp_ref[...], yc_ref[...], yab)
        ups = (gpu_ref, gco_ref, gau_ref)
        for n in range(N_CHIPS):
            acc = None
            for b in range(3):
                gcol = slice(1024 * b + GATE_TILE * n, 1024 * b + GATE_TILE * (n + 1))
                gate = _sigmoid(zg_ref[:, gcol] + b_ref[:, gcol])
                term = gate * _dot(ys[b], ups[b][n])
                acc = term if acc is None else acc + term
            out_ref[:, GATE_TILE * n:GATE_TILE * (n + 1)] = acc.astype(BF)

    return _pallas_call(
        body, name=name, grid=(T // tm,), in_specs=specs, out_specs=row(1024),
        out_shape=jax.ShapeDtypeStruct((T, 1024), BF), compiler_params=_params(("parallel",)),
    )(yp, yc, *([o3] * 3), *([lse3] * 3), z, bgate, gpu, gco, gau)


def _merge_bwd(dm, yp, yc, o3, lse3, z, bgate, gpu, gco, gau, name):
    T = yp.shape[0]
    tm, row, specs = _merge_specs(T, z, bgate, gpu, gco, gau)
    nsteps = T // tm

    def body(*refs):
        dm_ref, yp_ref, yc_ref = refs[0:3]
        o_refs, lse_refs = refs[3:6], refs[6:9]
        zg_ref = refs[9]
        b_ref, gpu_ref, gco_ref, gau_ref = refs[10:14]
        dzg_ref, dyp_ref, dyc_ref = refs[14:17]
        do_ref, c_ref = refs[17:19]
        dgpu_ref, dgco_ref, dgau_ref, dbg_ref = refs[19:23]
        accs = refs[23:26]
        i = pl.program_id(0)

        @pl.when(i == 0)
        def _():
            for a in accs:
                a[...] = jnp.zeros_like(a)
            dbg_ref[...] = jnp.zeros_like(dbg_ref)

        ws, y = _group_mix(o_refs, lse_refs)
        ys = (yp_ref[...], yc_ref[...], y.astype(BF))
        ups = (gpu_ref, gco_ref, gau_ref)
        dys = [None, None, None]
        for n in range(N_CHIPS):
            dmn = dm_ref[:, GATE_TILE * n:GATE_TILE * (n + 1)]
            for b in range(3):
                gcol = slice(1024 * b + GATE_TILE * n, 1024 * b + GATE_TILE * (n + 1))
                gate = _sigmoid(zg_ref[:, gcol] + b_ref[:, gcol])
                up = _dot(ys[b], ups[b][n])
                dzg = (dmn * up) * (gate * (1.0 - gate))
                dzg_ref[:, gcol] = dzg.astype(BF)
                dbg_ref[:, gcol] += jnp.sum(dzg, axis=0, keepdims=True)
                dup = (dmn * gate).astype(BF)
                accs[b][n] += _dot(ys[b], dup, "tn")
                dyb = _dot(dup, ups[b][n], "nt")
                dys[b] = dyb if dys[b] is None else dys[b] + dyb
        dyp_ref[...] = dys[0]
        dyc_ref[...] = dys[1]
        dya = dys[2]
        lane = lax.broadcasted_iota(jnp.int32, dya.shape, 1) // HEAD_DIM
        pr = dya * y
        rho = jnp.zeros_like(pr)
        for h in range(256 // HEAD_DIM):
            hm = lane == h
            rho = jnp.where(hm, jnp.sum(jnp.where(hm, pr, 0.0), axis=-1, keepdims=True), rho)
        for g in range(3):
            do_ref[:, 256 * g:256 * (g + 1)] = ws[g] * dya
            c_ref[:, 256 * g:256 * (g + 1)] = -(ws[g] * rho)

        @pl.when(i == nsteps - 1)
        def _():
            dgpu_ref[...] = accs[0][...].astype(BF)
            dgco_ref[...] = accs[1][...].astype(BF)
            dgau_ref[...] = accs[2][...].astype(BF)

    full = lambda a: pl.BlockSpec(a.shape, lambda i: (0,) * a.ndim)
    dz_gate = pl.BlockSpec((pl.Element(tm), pl.Element(3072)), lambda i: (i * tm, OFF_GATE))
    out_specs = ([dz_gate, row(512), row(512)] + [row(768)] * 2 + [full(gpu), full(gco), full(gau)]
                 + [pl.BlockSpec((1, 3072), lambda i: (0, 0))])
    out_shape = ([jax.ShapeDtypeStruct(z.shape, BF)] + [jax.ShapeDtypeStruct((T, 512), F32)] * 2
                 + [jax.ShapeDtypeStruct((T, 768), F32)] * 2
                 + [jax.ShapeDtypeStruct(g.shape, BF) for g in (gpu, gco, gau)]
                 + [jax.ShapeDtypeStruct((1, 3072), F32)])
    return _pallas_call(
        body, name=name, grid=(nsteps,), in_specs=[row(1024)] + specs, out_specs=out_specs, out_shape=out_shape,
        scratch_shapes=[pltpu.VMEM(g.shape, F32) for g in (gpu, gco, gau)],
        compiler_params=_params(("arbitrary",)),
    )(dm, yp, yc, *([o3] * 3), *([lse3] * 3), z, bgate, gpu, gco, gau)


def _layer_fwd(x, w, tag, after=None, soon=None, late=None, target=None, hb=None, next_gain=None):
    if hb is None:
        hb = _rms_fwd(x, w["norm_mix"], f"rms_mix_{tag}", after=after)
    if soon is not None:
        w = dict(w, **soon(hb))
    z = _mm(hb, w["w_in"], "nt", f"in_proj_{tag}", tm=512, tn=3712, tk=1024, n_outer=True)
    yp, yc = _poolconv_fwd(z, w["pool_mix"], w["pool_scale"], w["conv_w"], f"poolconv_{tag}")
    qkv = _qk_norm(z, w["qk_gain"], f"qk_norm_{tag}")
    o3, lse3 = _attn_fwd(qkv, f"attn_{tag}")
    if late is not None:
        w = dict(w, **late(lse3))
    merged = _merge_fwd(yp, yc, o3, lse3, z, w["b_gate"], w["w_pool_up"], w["w_conv_out"], w["w_attn_up"],
                        f"merge_{tag}")
    x1, h2b = _mm(merged, w["w_o"], "nn", f"out_proj_{tag}", tm=1024, tn=1024, tk=1024, res=x, vec=w["norm_mlp"],
                  epi="rms_next")
    rb = _mm(h2b, w["w_ff1"], "nn", f"ff1_{tag}", tm=1024, tn=1024, tk=1024, out_dtype=BF, epi="relu2", n_outer=True,
             b_shards=True)
    if target is not None:
        x2 = _mm(rb, w["w_ff2"], "nn", f"ff2_{tag}", tm=512, tn=1024, tk=4096, res=x1, aux=target, epi="loss")
    elif next_gain is not None:
        x2 = _mm(rb, w["w_ff2"], "nn", f"ff2_{tag}", tm=512, tn=1024, tk=4096, res=x1, vec=next_gain, epi="rms_next")
    else:
        x2 = _mm(rb, w["w_ff2"], "nn", f"ff2_{tag}", tm=512, tn=1024, tk=4096, res=x1)
    saved = dict(x=x, hb=hb, z=z, yp=yp, yc=yc, qkv=qkv, o3=o3, lse3=lse3, merged=merged, x1=x1, h2b=h2b, rb=rb)
    return x2, saved, w


def _layer_bwd(dx2, w, s, tag, after=None, mid=None, tail=None):
    g = {}
    dab = _mm(dx2, w["w_ff2"], "nt", f"d_ff2_act_{tag}", tm=1024, tn=1024, tk=1024, out_dtype=BF, aux=s["rb"],
              epi="drelu2", after=after)
    g["w_ff2"] = _mm(s["rb"], dx2, "tn", f"d_ff2_w_{tag}", tm=1024, tn=1024, tk=2048, out_dtype=BF)
    g["w_ff1"] = _mm(s["h2b"], dab, "tn", f"d_ff1_w_{tag}", tm=1024, tn=1024, tk=2048, out_dtype=BF, out_shards=True)
    dx1, g["norm_mlp"] = _mm(dab, w["w_ff1"], "nt", f"d_ff1_act_{tag}", tm=1024, tn=1024, tk=1024, b_shards=True,
                             res=dx2, aux=s["x1"], vec=w["norm_mlp"], epi="rms_bwd")
    dm = _mm(dx1, w["w_o"], "nt", f"d_out_act_{tag}", tm=1024, tn=1024, tk=1024)
    g["w_o"] = _mm(s["merged"], dx1, "tn", f"d_out_w_{tag}", tm=1024, tn=1024, tk=1024, out_dtype=BF)
    (dz, dyp, dyc, do3, c3, g["w_pool_up"], g["w_conv_out"], g["w_attn_up"],
     g["b_gate"]) = _merge_bwd(dm, s["yp"], s["yc"], s["o3"], s["lse3"], s["z"], w["b_gate"], w["w_pool_up"],
                               w["w_conv_out"], w["w_attn_up"], f"d_merge_{tag}")
    behind = mid(g) if mid is not None else None
    dzq, dzk, dzv, dgq, dgk = _attn_bwd(s["z"], s["qkv"], do3, c3, s["lse3"], w["qk_gain"], f"d_attn_{tag}",
                                        after=behind)
    g["q_gain"] = dgq[:, :HEAD_DIM] + dgq[:, HEAD_DIM:]
    g["k_gain"] = dgk[:, :HEAD_DIM] + dgk[:, HEAD_DIM:]
    for off, piece in ((OFF_Q, dzq), (OFF_K, dzk), (OFF_V, dzv)):
        dz = lax.dynamic_update_slice(dz, piece, (0, off))
    dz, g["pool_mix"], g["pool_scale"], g["conv_w"] = _poolconv_bwd(
        s["z"], dyp, dyc, w["pool_mix"], w["pool_scale"], w["conv_w"], dz, f"d_poolconv_{tag}")
    g["w_in"] = _mm(s["hb"], dz, "tn", f"d_in_w_{tag}", tm=512, tn=3712, tk=1024, out_dtype=BF)
    dh = _mm(dz, w["w_in"], "nn", f"d_in_act_{tag}", tm=1024, tn=1024, tk=3712,
             after=tail(g) if tail is not None else None)
    dx, g["norm_mix"] = _rms_bwd(dh, s["x"], w["norm_mix"], dx1, f"d_rms_mix_{tag}")
    return dx, g


def _position():
    x, y, c = lax.axis_index("x"), lax.axis_index("y"), lax.axis_index("c")
    chips = [(1 - x, y), (x, 1 - y), (1 - x, 1 - y)]
    return x, y, c, 2 * x + y, chips, [2 * cx + cy for cx, cy in chips]


def _remote(src, dst, ssem, rsem, dev):
    return pltpu.make_async_remote_copy(src_ref=src, dst_ref=dst, send_sem=ssem, recv_sem=rsem, device_id=dev,
                                        device_id_type=MESH_ID)


def _halves(a):
    return a.reshape(a.shape[0], 2, a.shape[1] // 2, a.shape[2])


SEM = pl.BlockSpec(memory_space=pltpu.SEMAPHORE)
TOKEN = jax.ShapeDtypeStruct((8, LANES), F32)
TOKEN_SPEC = pl.BlockSpec(memory_space=pltpu.VMEM)


def _split_params():
    return pltpu.CompilerParams(has_side_effects=pltpu.SideEffectType.DATAFLOW_SIDE_EFFECTING)


def _gather_start(bufs, name, after):
    n = len(bufs)
    views = [_halves(b) for b in bufs]

    def body(*refs):
        first_sem = n + 1
        ssem, rsem = refs[first_sem:first_sem + ns], refs[first_sem + ns:first_sem + 2 * ns]
        outs, token = refs[first_sem + 2 * ns:first_sem + 2 * ns + n], refs[first_sem + 2 * ns + n]
        x, y, c, q, chips, qs = _position()
        for k in range(n):
            mine = outs[k].at[q, c]
            for j, chip in enumerate(chips):
                _remote(mine, mine, ssem[3 * k + j], rsem[3 * k + j], (chip[0], chip[1], c)).start()
        token[...] = jnp.zeros_like(token)

    ns = 3 * n
    outs = _pallas_call(
        body, name=name, in_specs=[ANY] * (n + 1), out_specs=[SEM] * (2 * ns) + [ANY] * n + [TOKEN_SPEC],
        out_shape=[pltpu.SemaphoreType.DMA(())] * (2 * ns) + [jax.ShapeDtypeStruct(v.shape, v.dtype) for v in views]
        + [TOKEN],
        input_output_aliases={k: k + 2 * ns for k in range(n)}, compiler_params=_split_params(),
    )(*views, after)
    return list(outs[:ns]), list(outs[ns:2 * ns]), list(outs[2 * ns:2 * ns + n]), outs[2 * ns + n]


def _gather_finish(ssem, rsem, views, after, name_wait, name_forward, shapes):
    n = len(views)
    ns = len(ssem)

    def wait_body(*refs):
        ssem_ref, rsem_ref = refs[n:n + ns], refs[n + ns:n + 2 * ns]
        outs = refs[n + 2 * ns + 1:]
        x, y, c, q, chips, qs = _position()
        for k in range(n):
            for j, chip in enumerate(chips):
                cp = _remote(outs[k].at[q, c], outs[k].at[qs[j], c], ssem_ref[3 * k + j], rsem_ref[3 * k + j],
                             (chip[0], chip[1], c))
                cp.wait_send()
                cp.wait_recv()

    landed = _pallas_call(
        wait_body, name=name_wait, in_specs=[ANY] * n + [SEM] * (2 * ns) + [ANY], out_specs=[ANY] * n,
        out_shape=[jax.ShapeDtypeStruct(v.shape, v.dtype) for v in views],
        input_output_aliases={k: k for k in range(n)}, compiler_params=_split_params(),
    )(*views, *ssem, *rsem, after)

    def forward_body(*refs):
        outs = refs[n:2 * n]
        fssem, frsem = refs[2 * n:]
        x, y, c, q, chips, qs = _position()
        sib = (x, y, 1 - c)
        sent = []
        for k in range(n):
            for j in range(3):
                slot = outs[k].at[qs[j], c]
                cp = _remote(slot, slot, fssem.at[k, j], frsem.at[k, j], sib)
                cp.start()
                sent.append(cp)
        for k in range(n):
            for j in range(3):
                slot = outs[k].at[qs[j], 1 - c]
                _remote(slot, slot, fssem.at[k, j], frsem.at[k, j], sib).wait_recv()
        for cp in sent:
            cp.wait_send()

    outs = _pallas_call(
        forward_body, name=name_forward, in_specs=[ANY] * n, out_specs=[ANY] * n,
        out_shape=[jax.ShapeDtypeStruct(v.shape, v.dtype) for v in views],
        input_output_aliases={k: k for k in range(n)}, scratch_shapes=[pltpu.SemaphoreType.DMA((n, 3))] * 2,
    )(*landed)
    return [o.reshape(s) for o, s in zip(outs, shapes)]


def _chip_exchange_start(parts, name):
    n = len(parts)

    def body(*refs):
        ssem, rsem = refs[n:n + ns], refs[n + ns:n + 2 * ns]
        base = n + 2 * ns
        srcs, outs, token = refs[base:base + n], refs[base + n:base + 2 * n], refs[base + 2 * n]
        x, y, c, q, chips, qs = _position()
        for k in range(n):
            for j, chip in enumerate(chips):
                _remote(srcs[k].at[qs[j]], outs[k].at[j], ssem[3 * k + j], rsem[3 * k + j],
                        (chip[0], chip[1], c)).start()
        token[...] = jnp.zeros_like(token)

    ns = 3 * n
    outs = _pallas_call(
        body, name=name, in_specs=[ANY] * n, out_specs=[SEM] * (2 * ns) + [ANY] * (2 * n) + [TOKEN_SPEC],
        out_shape=[pltpu.SemaphoreType.DMA(())] * (2 * ns) + [jax.ShapeDtypeStruct(a.shape, a.dtype) for a in parts]
        + [jax.ShapeDtypeStruct((3,) + a.shape[1:], a.dtype) for a in parts] + [TOKEN],
        input_output_aliases={k: k + 2 * ns for k in range(n)}, compiler_params=_split_params(),
    )(*parts)
    b = 2 * ns
    return list(outs[:ns]), list(outs[ns:b]), list(outs[b:b + n]), list(outs[b + n:b + 2 * n]), outs[b + 2 * n]


def _chip_exchange_wait(ssem, rsem, parts, landing, after, name):
    n = len(parts)
    ns = len(ssem)

    def body(*refs):
        ssem_ref, rsem_ref = refs[2 * n:2 * n + ns], refs[2 * n + ns:2 * n + 2 * ns]
        base = 2 * n + 2 * ns + 1
        srcs, outs = refs[base:base + n], refs[base + n:]
        x, y, c, q, chips, qs = _position()
        for k in range(n):
            for j, chip in enumerate(chips):
                cp = _remote(srcs[k].at[qs[j]], outs[k].at[j], ssem_ref[3 * k + j], rsem_ref[3 * k + j],
                             (chip[0], chip[1], c))
                cp.wait_send()
                cp.wait_recv()

    outs = _pallas_call(
        body, name=name, in_specs=[ANY] * (2 * n) + [SEM] * (2 * ns) + [ANY], out_specs=[ANY] * (2 * n),
        out_shape=[jax.ShapeDtypeStruct(a.shape, a.dtype) for a in list(parts) + list(landing)],
        input_output_aliases={k: k for k in range(2 * n)}, compiler_params=_split_params(),
    )(*parts, *landing, *ssem, *rsem, after)
    return list(outs[:n]), list(outs[n:])


def _pair_swap(views, name):
    n = len(views)

    def body(*refs):
        ins, outs = refs[:n], refs[n:2 * n]
        ssem, rsem = refs[2 * n:]
        x, y, c, _, _, _ = _position()
        cps = [_remote(ins[k].at[pl.ds(0, N_CHIPS), 1 - c], outs[k], ssem.at[k], rsem.at[k], (x, y, 1 - c))
               for k in range(n)]
        for cp in cps:
            cp.start()
        for cp in cps:
            cp.wait()

    return _pallas_call(
        body, name=name, in_specs=[ANY] * n, out_specs=[ANY] * n,
        out_shape=[jax.ShapeDtypeStruct((v.shape[0],) + v.shape[2:], v.dtype) for v in views],
        scratch_shapes=[pltpu.SemaphoreType.DMA((n,))] * 2,
    )(*views)


def _pair_send(arrays, name):
    n = len(arrays)

    def body(*refs):
        ins, outs = refs[:n], refs[n:2 * n]
        ssem, rsem = refs[2 * n:]
        x, y, c, _, _, _ = _position()
        cps = [_remote(ins[k], outs[k], ssem.at[k], rsem.at[k], (x, y, 1 - c)) for k in range(n)]
        for cp in cps:
            cp.start()
        for cp in cps:
            cp.wait()

    return _pallas_call(
        body, name=name, in_specs=[ANY] * n, out_specs=[ANY] * n,
        out_shape=[jax.ShapeDtypeStruct(a.shape, a.dtype) for a in arrays],
        scratch_shapes=[pltpu.SemaphoreType.DMA((n,))] * 2,
    )(*arrays)


def _all_to_all_small(part):
    P = part.shape[0]

    def body(in_ref, out_ref, lsem, ssem, rsem):
        x, y, c = lax.axis_index("x"), lax.axis_index("y"), lax.axis_index("c")
        me = 4 * x + 2 * y + c
        flips = [(fx, fy, fc) for fx in (0, 1) for fy in (0, 1) for fc in (0, 1)][1:]
        peers = [((x + fx) % 2, (y + fy) % 2, (c + fc) % 2) for fx, fy, fc in flips]
        loc = pltpu.make_async_copy(in_ref, out_ref.at[me], lsem)
        loc.start()
        cps = [_remote(in_ref, out_ref.at[me], ssem.at[j], rsem.at[j], peer) for j, peer in enumerate(peers)]
        for cp in cps:
            cp.start()
        for j, (px, py, pc) in enumerate(peers):
            _remote(in_ref, out_ref.at[4 * px + 2 * py + pc], ssem.at[j], rsem.at[j], peers[j]).wait_recv()
        for cp in cps:
            cp.wait_send()
        loc.wait()

    return _pallas_call(
        body, name="small_exchange", in_specs=[ANY], out_specs=ANY,
        out_shape=jax.ShapeDtypeStruct((8, P, LANES), F32),
        scratch_shapes=[pltpu.SemaphoreType.DMA(())] + [pltpu.SemaphoreType.DMA((7,))] * 2,
    )(part)


def _small_peers():
    x, y, c = lax.axis_index("x"), lax.axis_index("y"), lax.axis_index("c")
    flips = [(fx, fy, fc) for fx in (0, 1) for fy in (0, 1) for fc in (0, 1)][1:]
    peers = [((x + fx) % 2, (y + fy) % 2, (c + fc) % 2) for fx, fy, fc in flips]
    return 4 * x + 2 * y + c, peers


def _all_to_all_small_start(part, name):
    P = part.shape[0]
    me = 4 * lax.axis_index("x") + 2 * lax.axis_index("y") + lax.axis_index("c")
    landing = lax.dynamic_update_slice(jnp.zeros((8, P, LANES), F32), part[None], (me, 0, 0))

    def body(*refs):
        sems, src, land, token = refs[2:16], refs[16], refs[17], refs[18]
        me_, peers = _small_peers()
        for j, peer in enumerate(peers):
            _remote(src, land.at[me_], sems[j], sems[7 + j], peer).start()
        token[...] = jnp.zeros_like(token)

    outs = _pallas_call(
        body, name=name, in_specs=[ANY, ANY], out_specs=[SEM] * 14 + [ANY, ANY, TOKEN_SPEC],
        out_shape=[pltpu.SemaphoreType.DMA(())] * 14 + [jax.ShapeDtypeStruct(part.shape, F32),
                                                       jax.ShapeDtypeStruct((8, P, LANES), F32), TOKEN],
        input_output_aliases={0: 14, 1: 15}, compiler_params=_split_params(),
    )(part, landing)
    return list(outs[:7]), list(outs[7:14]), outs[14], outs[15], outs[16]


def _all_to_all_small_wait(ssem, rsem, part, landing, after, name):
    def body(*refs):
        sems, src, land = refs[2:16], refs[17], refs[18]
        _, peers = _small_peers()
        for j, (px, py, pc) in enumerate(peers):
            cp = _remote(src, land.at[4 * px + 2 * py + pc], sems[j], sems[7 + j], peers[j])
            cp.wait_send()
            cp.wait_recv()

    return _pallas_call(
        body, name=name, in_specs=[ANY, ANY] + [SEM] * 14 + [ANY], out_specs=[ANY, ANY],
        out_shape=[jax.ShapeDtypeStruct(part.shape, F32), jax.ShapeDtypeStruct(landing.shape, F32)],
        input_output_aliases={0: 0, 1: 1}, compiler_params=_split_params(),
    )(part, landing, *ssem, *rsem, after)[1]


def _row_tile(rows, width, n_arrays):
    t = rows
    while t % 2 == 0 and t > 8 and 2 * n_arrays * t * width * 4 > VMEM_LIMIT // 2:
        t //= 2
    return t


def _chip():
    return 2 * lax.axis_index("x") + lax.axis_index("y")


def _core():
    return lax.axis_index("c")


def _cast_place(w3, layer, name):
    _, r, c = w3.shape
    tr = _row_tile(r, c, 2)

    def body(w_ref, o_ref):
        o_ref[...] = w_ref[...].astype(BF)

    return _pallas_call(
        body, name=name, grid=(r // tr,), in_specs=[pl.BlockSpec((None, tr, c), lambda i: (layer, i, 0))],
        out_specs=pl.BlockSpec((None, tr, c), lambda i: (_chip(), i, 0)),
        out_shape=jax.ShapeDtypeStruct((N_CHIPS, r, c), BF), compiler_params=_params(("parallel",)),
    )(w3)


def _pair_sum(views, recvs, name):
    n = len(views)

    def body(*refs):
        for g_ref, r_ref, o_ref in zip(refs[:n], refs[n:2 * n], refs[2 * n:]):
            o_ref[...] = (g_ref[...].astype(F32) + r_ref[...].astype(F32)).astype(BF)

    own = [pl.BlockSpec((None, None) + v.shape[2:], lambda p: (p, _core(), 0, 0)) for v in views]
    blk = [pl.BlockSpec((None,) + r.shape[1:], lambda p: (p, 0, 0)) for r in recvs]
    return _pallas_call(
        body, name=name, grid=(N_CHIPS,), in_specs=own + blk, out_specs=blk,
        out_shape=[jax.ShapeDtypeStruct(r.shape, BF) for r in recvs], compiler_params=_params(("parallel",)),
    )(*views, *recvs)


CHIP_SUM_STEPS = 2


def _chip_sum(parts, recvs, name):
    n = len(parts)

    def body(*refs):
        for p_ref, r_ref, o_ref in zip(refs[:n], refs[n:2 * n], refs[2 * n:]):
            acc = p_ref[...].astype(F32)
            for j in range(3):
                acc = acc + r_ref[j].astype(F32)
            o_ref[...] = acc

    rows = [p.shape[1] // CHIP_SUM_STEPS for p in parts]
    return _pallas_call(
        body, name=name, grid=(CHIP_SUM_STEPS,),
        in_specs=[pl.BlockSpec((None, t, p.shape[2]), lambda i: (_chip(), i, 0)) for p, t in zip(parts, rows)]
        + [pl.BlockSpec((3, t, p.shape[2]), lambda i: (0, i, 0)) for p, t in zip(parts, rows)],
        out_specs=[pl.BlockSpec((t, p.shape[2]), lambda i: (i, 0)) for p, t in zip(parts, rows)],
        out_shape=[jax.ShapeDtypeStruct(p.shape[1:], F32) for p in parts], compiler_params=_params(("parallel",)),
    )(*parts, *recvs)


def _sum_slices(a, name):
    n, rows, width = a.shape
    tr = _row_tile(rows, width, n + 1)

    def body(a_ref, o_ref):
        acc = a_ref[0].astype(F32)
        for i in range(1, n):
            acc = acc + a_ref[i].astype(F32)
        o_ref[...] = acc

    return _pallas_call(
        body, name=name, grid=(rows // tr,), in_specs=[pl.BlockSpec((n, tr, width), lambda i: (0, i, 0))],
        out_specs=pl.BlockSpec((tr, width), lambda i: (i, 0)), out_shape=jax.ShapeDtypeStruct((rows, width), F32),
        compiler_params=_params(("parallel",)),
    )(a)


def _adamw_update(w, g, m, v):
    nm = ADAM_B1 * m + (1.0 - ADAM_B1) * g
    nv = ADAM_B2 * v + (1.0 - ADAM_B2) * (g * g)
    m_hat = nm / (1.0 - ADAM_B1 ** ADAM_STEP)
    v_hat = nv / (1.0 - ADAM_B2 ** ADAM_STEP)
    return -ADAM_LR * (m_hat / (jnp.sqrt(v_hat) + ADAM_EPS) + ADAM_WD * w), nm, nv


def _adamw(ws, gs, ms, vs, name):
    n = len(ws)

    def body(*refs):
        for k in range(n):
            w_ref, g_ref, m_ref, v_ref = (refs[s * n + k] for s in range(4))
            d_ref, nm_ref, nv_ref = (refs[(4 + s) * n + k] for s in range(3))
            d_ref[...], nm_ref[...], nv_ref[...] = _adamw_update(w_ref[...], g_ref[...], m_ref[...], v_ref[...])

    whole = [pl.BlockSpec(w.shape, lambda i: (0, 0)) for w in ws]
    outs = _pallas_call(
        body, name=name, grid=(1,), in_specs=whole * 4, out_specs=whole * 3,
        out_shape=[jax.ShapeDtypeStruct(w.shape, F32) for _ in range(3) for w in ws],
        compiler_params=_params(("arbitrary",)),
    )(*ws, *gs, *ms, *vs)
    return [[outs[s * n + k] for s in range(3)] for k in range(n)]


ADAMW_STEPS = 4


def _adamw_halves(ws, ms, vs, mine, other, name):
    n = len(ws)
    depth = ws[0].shape[0]
    assert depth == 2
    halves = [(w.shape[1] // 2, w.shape[2]) for w in ws]
    tiles = [hr // ADAMW_STEPS for hr, _ in halves]
    kinds = ((0, True), (0, False), (1, True), (1, False))

    def active(l, h, layer, own):
        mine_half = h == _core()
        return (l == layer) & (mine_half if own else jnp.logical_not(mine_half))

    def body(*refs):
        l, h = pl.program_id(0), pl.program_id(1)
        flags = [active(l, h, layer, own) for layer, own in kinds]
        for k in range(n):
            w_ref, m_ref, v_ref = refs[k], refs[n + k], refs[2 * n + k]
            g_refs = [refs[(3 + s) * n + k] for s in range(4)]
            go_ref, d_ref, nm_ref, nv_ref = (refs[(7 + s) * n + k] for s in range(4))
            for flag, g_ref in zip(flags, g_refs):
                @pl.when(flag)
                def _():
                    gv = g_ref[...]
                    go_ref[...] = gv
                    d_ref[...], nm_ref[...], nv_ref[...] = _adamw_update(w_ref[...], gv, m_ref[...], v_ref[...])

    def blk(k):
        return pl.BlockSpec((None, None, tiles[k], halves[k][1]), lambda l, h, i: (l, h, i, 0))

    def gspec(k, layer, own):
        return pl.BlockSpec((tiles[k], halves[k][1]), lambda l, h, i: (jnp.where(active(l, h, layer, own), i, 0), 0))

    def view(a, k):
        return a.reshape(depth, 2, halves[k][0], halves[k][1])

    blks = [blk(k) for k in range(n)]
    sources = [[(mine if own else other)[layer][k] for k in range(n)] for layer, own in kinds]
    outs = _pallas_call(
        body, name=name, grid=(depth, 2, ADAMW_STEPS),
        in_specs=blks * 3 + [gspec(k, layer, own) for layer, own in kinds for k in range(n)], out_specs=blks * 4,
        out_shape=[jax.ShapeDtypeStruct((depth, 2) + halves[k], F32) for _ in range(4) for k in range(n)],
        compiler_params=_params(("parallel", "parallel", "parallel")),
    )(*[view(a, k) for group in (ws, ms, vs) for k, a in enumerate(group)], *[g for src in sources for g in src])
    return [[outs[s * n + k].reshape(ws[k].shape) for s in range(4)] for k in range(n)]


BIG = ("w_in", "w_pool_up", "w_conv_out", "w_attn_up", "w_o", "w_ff1", "w_ff2")
SMALL = ("norm_mix", "b_gate", "pool_mix", "pool_scale", "conv_w", "q_gain", "k_gain", "norm_mlp")
ORDER = ("norm_mix", "w_in", "b_gate", "pool_mix", "pool_scale", "conv_w", "q_gain", "k_gain", "w_pool_up",
         "w_conv_out", "w_attn_up", "w_o", "norm_mlp", "w_ff1", "w_ff2")
COLUMN_SHARDED = ("w_pool_up", "w_conv_out", "w_attn_up", "w_ff1")


def _matrix_weights(gathered):
    w = {}
    for name, g4 in gathered.items():
        if name in COLUMN_SHARDED:
            w[name] = g4
        else:
            w[name] = g4.reshape(N_CHIPS * g4.shape[1], g4.shape[2])
    return w


def _small_weights(l, small):
    w = {}
    w["norm_mix"] = small["norm_mix"][l][None]
    w["norm_mlp"] = small["norm_mlp"][l][None]
    w["b_gate"] = small["b_gate"][l][None]
    w["pool_mix"] = small["pool_mix"][l].astype(BF)
    w["pool_scale"] = small["pool_scale"][l][None]
    w["conv_w"] = jnp.pad(small["conv_w_full"][l], ((0, 5), (0, 0)))
    w["qk_gain"] = jnp.pad(jnp.stack([jnp.tile(small["q_gain"][l], 2), jnp.tile(small["k_gain"][l], 2)]), ((0, 6), (0, 0)))
    return w


def _to_chip_major(name, g):
    if name == "w_in":
        return g.T.reshape(N_CHIPS, g.shape[1] // N_CHIPS, g.shape[0])
    if name in COLUMN_SHARDED:
        return g
    return g.reshape(N_CHIPS, g.shape[0] // N_CHIPS, g.shape[1])


def _pad8(a):
    a = a.reshape(-1)
    return jnp.pad(a, (0, (-a.size) % (8 * LANES))).reshape(-1, LANES)


def kernel(x, norm_mix, w_in, b_gate, pool_mix, pool_scale, conv_w, q_gain, k_gain, w_pool_up, w_conv_out, w_attn_up, w_o, norm_mlp, w_ff1, w_ff2, loss_target, m_norm_mix, m_w_in, m_b_gate, m_pool_mix, m_pool_scale, m_conv_w, m_q_gain, m_k_gain, m_w_pool_up, m_w_conv_out, m_w_attn_up, m_w_o, m_norm_mlp, m_w_ff1, m_w_ff2, v_norm_mix, v_w_in, v_b_gate, v_pool_mix, v_pool_scale, v_conv_w, v_q_gain, v_k_gain, v_w_pool_up, v_w_conv_out, v_w_attn_up, v_w_o, v_norm_mlp, v_w_ff1, v_w_ff2):
    weights = dict(norm_mix=norm_mix, w_in=w_in, b_gate=b_gate, pool_mix=pool_mix, pool_scale=pool_scale, conv_w=conv_w,
                   q_gain=q_gain, k_gain=k_gain, w_pool_up=w_pool_up, w_conv_out=w_conv_out, w_attn_up=w_attn_up,
                   w_o=w_o, norm_mlp=norm_mlp, w_ff1=w_ff1, w_ff2=w_ff2)
    moms = dict(norm_mix=m_norm_mix, w_in=m_w_in, b_gate=m_b_gate, pool_mix=m_pool_mix, pool_scale=m_pool_scale,
                conv_w=m_conv_w, q_gain=m_q_gain, k_gain=m_k_gain, w_pool_up=m_w_pool_up, w_conv_out=m_w_conv_out,
                w_attn_up=m_w_attn_up, w_o=m_w_o, norm_mlp=m_norm_mlp, w_ff1=m_w_ff1, w_ff2=m_w_ff2)
    vels = dict(norm_mix=v_norm_mix, w_in=v_w_in, b_gate=v_b_gate, pool_mix=v_pool_mix, pool_scale=v_pool_scale,
                conv_w=v_conv_w, q_gain=v_q_gain, k_gain=v_k_gain, w_pool_up=v_w_pool_up, w_conv_out=v_w_conv_out,
                w_attn_up=v_w_attn_up, w_o=v_w_o, norm_mlp=v_norm_mlp, w_ff1=v_w_ff1, w_ff2=v_w_ff2)
    depth = norm_mix.shape[0]
    q = 2 * lax.axis_index("x") + lax.axis_index("y")
    for group in (weights, moms, vels):
        group["w_in"] = jnp.swapaxes(group["w_in"], 1, 2)

    assert depth == 2, "the second layer's gather hides behind the first layer's forward, and likewise backward"
    first, rest = BIG[:1], BIG[1:]
    cw_all = _all_to_all_small(_pad8(conv_w))
    bufs = [{n: _cast_place(weights[n], 0, f"cast_{n}_l0") for n in first}]
    a_ssem, a_rsem, a_views, a_token = _gather_start([bufs[0][n] for n in first], "gather_start_l0_in", cw_all)
    bufs[0].update({n: _cast_place(weights[n], 0, f"cast_{n}_l0") for n in rest})
    bufs += [{n: _cast_place(weights[n], l, f"cast_{n}_l{l}") for n in BIG} for l in range(1, depth)]
    b_ssem, b_rsem, b_views, b_token = _gather_start([bufs[0][n] for n in rest], "gather_start_l0_rest", a_token)
    g_ssem, g_rsem, g_views, g_token = _gather_start([bufs[1][n] for n in BIG], "gather_start_l1", b_token)
    conv_w_full = jnp.concatenate(
        [cw_all[2 * p].reshape(-1)[:conv_w.size].reshape(conv_w.shape) for p in range(N_CHIPS)], axis=-1)
    small = dict(weights)
    small["conv_w_full"] = conv_w_full

    def soon_weights(t):
        got = _gather_finish(a_ssem, a_rsem, a_views, t, "gather_wait_l0_in", "gather_forward_l0_in",
                             [bufs[0][n].shape for n in first])
        return _matrix_weights(dict(zip(first, got)))

    def late_weights(t):
        got = _gather_finish(b_ssem, b_rsem, b_views, t, "gather_wait_l0_rest", "gather_forward_l0_rest",
                             [bufs[0][n].shape for n in rest])
        return _matrix_weights(dict(zip(rest, got)))

    wl, saved = [None] * depth, [None] * depth
    small_1 = _small_weights(1, small)
    (h, hb_1), saved[0], wl[0] = _layer_fwd(x[0], _small_weights(0, small), "l0", after=g_token, soon=soon_weights,
                                            late=late_weights, next_gain=small_1["norm_mix"])
    got = _gather_finish(g_ssem, g_rsem, g_views, h, "gather_wait_l1", "gather_forward_l1",
                         [bufs[1][n].shape for n in BIG])
    (dh, loss_row), saved[1], wl[1] = _layer_fwd(
        h, dict(small_1, **_matrix_weights(dict(zip(BIG, got)))), "l1", target=loss_target[0], hb=hb_1)

    def pair_stage(names, g, tag):
        views = [_halves(_to_chip_major(n, g[n])) for n in names]
        from_sibling = _pair_swap(views, f"grad_pair_swap_{tag}")
        return _pair_sum(views, from_sibling, f"pair_sum_{tag}")

    mine, other = [{}, {}], [{}, {}]

    def finish(names, l, started, after, tag):
        ssem, rsem, parts, landing, _ = started
        parts, arrived = _chip_exchange_wait(ssem, rsem, parts, landing, after, f"grad_chip_exchange_wait_{tag}")
        got = _chip_sum(parts, arrived, f"chip_sum_{tag}")
        mine[l].update(zip(names, got))
        other[l].update(zip(names, _pair_send(got, f"grad_pair_send_{tag}")))

    def small_pieces(g):
        return [_pad8(g[n][:3] if n == "conv_w" else g[n]) for n in SMALL]

    def start_small(l):
        pieces = small_pieces(grads[l]) + ([_pad8(loss_row)] if l == depth - 1 else [])
        return _all_to_all_small_start(jnp.concatenate(pieces, axis=0), f"small_grad_exchange_start_l{l}")

    grads, early, small = [None] * depth, {}, [None] * depth
    dh, grads[1] = _layer_bwd(dh, wl[1], saved[1], "l1")
    second = _chip_exchange_start(pair_stage(BIG, grads[1], "l1"), "grad_chip_exchange_start_l1")
    small[1] = start_small(1)

    def start_rest(g):
        early["rest"] = _chip_exchange_start(pair_stage(rest, g, "l0_rest"), "grad_chip_exchange_start_l0_rest")
        return early["rest"][4]

    def start_last(g):
        early["in"] = _chip_exchange_start(pair_stage(first, g, "l0_in"), "grad_chip_exchange_start_l0_in")
        return early["in"][4]

    dh, grads[0] = _layer_bwd(dh, wl[0], saved[0], "l0", after=[second[4], small[1][4]], mid=start_rest,
                              tail=start_last)
    small[0] = start_small(0)
    started = small[0][4]
    finish(BIG, 1, second, started, "l1")
    finish(rest, 0, early["rest"], started, "l0_rest")
    full = {}

    deltas, new_m, new_v = {}, {}, {}

    def update_matrices(names, tag):
        results = _adamw_halves(
            [weights[n] for n in names], [moms[n] for n in names], [vels[n] for n in names],
            [[mine[l][n] for n in names] for l in range(depth)], [[other[l][n] for n in names] for l in range(depth)],
            f"adamw_{tag}")
        for n, (g_, d_, m_, v_) in zip(names, results):
            full[n], deltas[n], new_m[n], new_v[n] = g_, d_, m_, v_

    update_matrices(rest, "rest")
    finish(first, 0, early["in"], deltas[rest[-1]], "l0_in")
    update_matrices(first, "in")
    summed = []
    for l in range(depth):
        ssem, rsem, part, landing, _ = small[l]
        summed.append(_sum_slices(_all_to_all_small_wait(ssem, rsem, part, landing, deltas[first[-1]],
                                                         f"small_grad_exchange_wait_l{l}"), f"small_sum_l{l}"))
    row = 0
    for n, piece in zip(SMALL, small_pieces(grads[0])):
        size = (weights[n].size if n != "conv_w" else depth * 3 * 512) // depth
        flat = jnp.stack([s[row:row + piece.shape[0]].reshape(-1)[:size] for s in summed])
        row += piece.shape[0]
        if n == "conv_w":
            full[n] = lax.dynamic_slice_in_dim(flat.reshape(depth, 3, 512), q * conv_w.shape[2], conv_w.shape[2], axis=2)
        else:
            full[n] = flat.reshape(weights[n].shape)
    loss = summed[depth - 1][row, 0]
    two_d = {n: (-1, weights[n].shape[-1]) if n not in ("conv_w", "q_gain", "k_gain") else (1, -1) for n in SMALL}
    results = _adamw(*[[group[n].reshape(two_d[n]) for n in SMALL] for group in (weights, full, moms, vels)],
                     "adamw_small")
    for n, (d2, m2, v2) in zip(SMALL, results):
        shape = weights[n].shape
        deltas[n], new_m[n], new_v[n] = d2.reshape(shape), m2.reshape(shape), v2.reshape(shape)
        full[n] = full[n].reshape(shape)
    for group in (full, deltas, new_m, new_v):
        group["w_in"] = jnp.swapaxes(group["w_in"], 1, 2)
    return (loss, dh[None], *[full[n] for n in ORDER], *[deltas[n] for n in ORDER], *[new_m[n] for n in ORDER],
            *[new_v[n] for n in ORDER])
```

```python
import functools

import jax
import jax.numpy as jnp
from jax import lax
from jax.experimental import pallas as pl
from jax.experimental.pallas import tpu as pltpu

F32 = jnp.float32
BF = jnp.bfloat16
MESH_ID = pl.DeviceIdType.MESH
ANY = pl.BlockSpec(memory_space=pl.ANY)

EPS = 1e-6
MASK_VALUE = -1e30
POOL_WINDOWS = (2, 4, 8, 16)
ATTN_DILATIONS = (1, 4, 16)
ATTN_BLOCK = 128
HEAD_DIM = 64
OFF_Q, OFF_K, OFF_V, OFF_GATE = 2048, 2816, 3584, 4352
N_CHIPS = 4
ADAM_LR, ADAM_B1, ADAM_B2, ADAM_EPS, ADAM_WD, ADAM_STEP = 0.001, 0.9, 0.999, 1e-08, 0.01, 10

VMEM_LIMIT = 48 * 1024 * 1024
LANES = 128

_DIMS = {"nn": (((1,), (0,)), ((), ())), "nt": (((1,), (1,)), ((), ())), "tn": (((0,), (0,)), ((), ()))}


def _params(sem):
    return pltpu.CompilerParams(dimension_semantics=sem, vmem_limit_bytes=VMEM_LIMIT)


def _pallas_call(body, **kw):
    def in_hbm(s):
        pin = isinstance(s, jax.ShapeDtypeStruct) and s is not TOKEN and jnp.issubdtype(s.dtype, jnp.floating)
        return pltpu.HBM(s.shape, s.dtype) if pin else s

    out_shape = kw.pop("out_shape")
    kw["out_shape"] = [in_hbm(s) for s in out_shape] if isinstance(out_shape, (list, tuple)) else in_hbm(out_shape)
    call = pl.pallas_call(body, **kw)

    def run(*args):
        pinned = [pltpu.with_memory_space_constraint(a, pltpu.HBM)
                  if hasattr(a, "dtype") and jnp.issubdtype(a.dtype, jnp.floating) else a for a in args]
        return call(*pinned)

    return run


def _dot(a, b, mode="nn"):
    return lax.dot_general(a, b, _DIMS[mode], preferred_element_type=F32)


def _mm(a, b, mode, name, *, tm, tn, tk, out_dtype=F32, res=None, aux=None, epi=None, n_outer=False,
        b_shards=False, out_shards=False, after=None, vec=None):
    if mode == "tn":
        K, M = a.shape
    else:
        M, K = a.shape
    if b_shards:
        if mode == "nn":
            assert b.shape[1] == K
            N = b.shape[2] * N_CHIPS
        else:
            assert mode == "nt"
            N = b.shape[1]
            assert b.shape[2] * N_CHIPS == K
    else:
        N = b.shape[0] if mode == "nt" else b.shape[1]
    tm, tn, tk = min(tm, M), min(tn, N), min(tk, K)
    assert M % tm == 0 and N % tn == 0 and K % tk == 0
    nk = K // tk
    if n_outer:
        grid = (N // tn, M // tm, nk)
        ij = lambda p, q_: (q_, p)
    else:
        grid = (M // tm, N // tn, nk)
        ij = lambda p, q_: (p, q_)

    def amap(p, q_, k):
        i, j = ij(p, q_)
        return (k, i) if mode == "tn" else (i, k)

    a_spec = pl.BlockSpec((tk, tm) if mode == "tn" else (tm, tk), amap)
    if b_shards:
        if mode == "nn":
            per = (N // N_CHIPS) // tn
            assert per >= 1 and (N // N_CHIPS) % tn == 0

            def bmap(p, q_, k):
                i, j = ij(p, q_)
                return (j // per, k, j % per)

            b_spec = pl.BlockSpec((None, tk, tn), bmap)
        else:
            per = (K // N_CHIPS) // tk
            assert per >= 1 and (K // N_CHIPS) % tk == 0

            def bmap(p, q_, k):
                i, j = ij(p, q_)
                return (k // per, j, k % per)

            b_spec = pl.BlockSpec((None, tn, tk), bmap)
    else:
        def bmap(p, q_, k):
            i, j = ij(p, q_)
            return (j, k) if mode == "nt" else (k, j)

        b_spec = pl.BlockSpec((tn, tk) if mode == "nt" else (tk, tn), bmap)

    def omap(p, q_, k):
        return ij(p, q_)

    o_spec = pl.BlockSpec((tm, tn), omap)
    if out_shards:
        per_o = (N // N_CHIPS) // tn
        assert per_o >= 1 and (N // N_CHIPS) % tn == 0

        def osmap(p, q_, k):
            i, j = ij(p, q_)
            return (j // per_o, i, j % per_o)

        out_spec0 = pl.BlockSpec((None, tm, tn), osmap)
        out_shape0 = jax.ShapeDtypeStruct((N_CHIPS, M, N // N_CHIPS), out_dtype)
    else:
        out_spec0 = o_spec
        out_shape0 = jax.ShapeDtypeStruct((M, N), out_dtype)

    in_specs = [a_spec, b_spec]
    args = [a, b]
    if res is not None:
        in_specs.append(o_spec)
        args.append(res)
    if aux is not None:
        in_specs.append(o_spec)
        args.append(aux)
    if vec is not None:
        in_specs.append(pl.BlockSpec((1, tn), lambda p, q_, k: (0, ij(p, q_)[1])))
        args.append(vec)
    after = [] if after is None else list(after) if isinstance(after, (list, tuple)) else [after]
    in_specs += [ANY] * len(after)
    args += after
    out_specs = [out_spec0]
    out_shape = [out_shape0]
    reduces = epi in ("loss", "rms_bwd")
    if reduces:
        assert tn == N and not n_outer and not out_shards
        width = LANES if epi == "loss" else N
        out_specs.append(pl.BlockSpec((1, width), lambda p, q_, k: (0, 0)))
        out_shape.append(jax.ShapeDtypeStruct((1, width), F32))
    if epi == "rms_next":
        assert tn == N and not out_shards
        out_specs.append(o_spec)
        out_shape.append(jax.ShapeDtypeStruct((M, N), BF))
    n_out = len(out_shape)
    has_res, has_aux, has_vec, n_after = res is not None, aux is not None, vec is not None, len(after)

    def body(*refs):
        a_ref, b_ref = refs[0], refs[1]
        pos = 2
        res_ref = aux_ref = vec_ref = None
        if has_res:
            res_ref = refs[pos]
            pos += 1
        if has_aux:
            aux_ref = refs[pos]
            pos += 1
        if has_vec:
            vec_ref = refs[pos]
            pos += 1
        pos += n_after
        outs = refs[pos:pos + n_out]
        part = _dot(a_ref[...].astype(BF), b_ref[...].astype(BF), mode)

        first_row_tile = pl.program_id(0) == 0

        def add_to_sum(row):
            @pl.when(first_row_tile)
            def _():
                outs[1][...] = jnp.zeros_like(outs[1])

            outs[1][...] += row

        def finish(acc):
            if epi == "rms_bwd":
                xv = aux_ref[...]
                r = lax.rsqrt(jnp.mean(xv * xv, axis=-1, keepdims=True) + EPS)
                xhat = xv * r
                dy = acc * vec_ref[...]
                outs[0][...] = res_ref[...] + r * (dy - xhat * jnp.mean(dy * xhat, axis=-1, keepdims=True))
                add_to_sum(jnp.sum(acc * xhat, axis=0, keepdims=True))
                return
            if res_ref is not None:
                acc = res_ref[...] + acc
            if epi == "relu2":
                r = jnp.maximum(acc, 0.0)
                outs[0][...] = (r * r).astype(out_dtype)
            elif epi == "drelu2":
                outs[0][...] = (acc.astype(BF) * (2.0 * jnp.sqrt(aux_ref[...]))).astype(out_dtype)
            elif epi == "rms_next":
                outs[0][...] = acc
                r = lax.rsqrt(jnp.mean(acc * acc, axis=-1, keepdims=True) + EPS)
                outs[1][...] = ((acc * r) * vec_ref[...]).astype(BF)
            elif epi == "loss":
                e = acc - aux_ref[...]
                outs[0][...] = e / float(N)
                add_to_sum(0.5 * jnp.sum(jnp.mean(e * e, axis=-1, keepdims=True)))
            else:
                outs[0][...] = acc.astype(out_dtype)

        if nk == 1:
            finish(part)
        else:
            acc_ref = refs[pos + n_out]
            k = pl.program_id(2)

            @pl.when(k == 0)
            def _():
                acc_ref[...] = part

            @pl.when(k > 0)
            def _():
                acc_ref[...] += part

            @pl.when(k == nk - 1)
            def _():
                finish(acc_ref[...])

    scratch = [pltpu.VMEM((tm, tn), F32)] if nk > 1 else []
    out = _pallas_call(
        body, name=name, grid=grid, in_specs=in_specs, out_specs=out_specs, out_shape=out_shape,
        scratch_shapes=scratch,
        compiler_params=_params(("arbitrary" if reduces else "parallel", "parallel", "arbitrary")),
    )(*args)
    return out if n_out > 1 else out[0]


def _rms_fwd(x, gain, name, after=None):
    T, D = x.shape
    tm = min(512, T)

    def body(x_ref, g_ref, *rest):
        o_ref = rest[-1]
        xv = x_ref[...]
        r = lax.rsqrt(jnp.mean(xv * xv, axis=-1, keepdims=True) + EPS)
        o_ref[...] = ((xv * r) * g_ref[...]).astype(BF)

    extra = [] if after is None else list(after) if isinstance(after, (list, tuple)) else [after]
    return _pallas_call(
        body, name=name, grid=(T // tm,),
        in_specs=[pl.BlockSpec((tm, D), lambda i: (i, 0)), pl.BlockSpec((1, D), lambda i: (0, 0))] + [ANY] * len(extra),
        out_specs=pl.BlockSpec((tm, D), lambda i: (i, 0)), out_shape=jax.ShapeDtypeStruct((T, D), BF),
        compiler_params=_params(("parallel",)),
    )(x, gain, *extra)


def _rms_bwd(dh, x, gain, dres, name):
    T, D = x.shape
    tm = min(512, T)

    def body(dh_ref, x_ref, g_ref, dres_ref, dx_ref, dg_ref):
        xv = x_ref[...]
        r = lax.rsqrt(jnp.mean(xv * xv, axis=-1, keepdims=True) + EPS)
        xhat = xv * r
        dhv = dh_ref[...]
        dy = dhv * g_ref[...]
        dx_ref[...] = dres_ref[...] + r * (dy - xhat * jnp.mean(dy * xhat, axis=-1, keepdims=True))

        @pl.when(pl.program_id(0) == 0)
        def _():
            dg_ref[...] = jnp.zeros_like(dg_ref)

        dg_ref[...] += jnp.sum(dhv * xhat, axis=0, keepdims=True)

    row = pl.BlockSpec((tm, D), lambda i: (i, 0))
    vec = pl.BlockSpec((1, D), lambda i: (0, 0))
    return _pallas_call(
        body, name=name, grid=(T // tm,), in_specs=[row, row, vec, row], out_specs=[row, vec],
        out_shape=[jax.ShapeDtypeStruct((T, D), F32), jax.ShapeDtypeStruct((1, D), F32)],
        compiler_params=_params(("arbitrary",)),
    )(dh, x, gain, dres)


POOL_HALO = 16
CONV_HALO = 8
POOLCONV_ROWS = 512


def _causal_window_sum(v, w):
    s, sh = v, 1
    while sh < w:
        s = s + pltpu.roll(s, sh, 0)
        sh *= 2
    return s


def _anticausal_window_sum(v, w):
    n = v.shape[0]
    s, sh = v, 1
    while sh < w:
        s = s + pltpu.roll(s, n - sh, 0)
        sh *= 2
    return s


def _poolconv_fwd(z, pmix_b, pscale, convw, name):
    T = z.shape[0]
    R = min(POOLCONV_ROWS, T)
    PH, CH = R // POOL_HALO, R // CONV_HALO

    def body(u_ref, uh_ref, b_ref, c_ref, ch_ref, x_ref, xh_ref, mix_ref, sc_ref, cw_ref, yp_ref, yc_ref):
        i = pl.program_id(0)
        keep = (i > 0).astype(F32)
        row = i * R + lax.broadcasted_iota(jnp.int32, (R, 1), 0)
        w_all = jnp.concatenate([uh_ref[...] * keep, u_ref[...]], axis=0)
        for g, w in enumerate(POOL_WINDOWS):
            cols = slice(128 * g, 128 * (g + 1))
            wg = w_all[:, cols]
            s = _causal_window_sum(wg, w)[POOL_HALO:]
            inv_cnt = 1.0 / jnp.minimum(row + 1, w).astype(F32)
            dgrp = s * inv_cnt - wg[POOL_HALO:]
            y = _dot(dgrp.astype(BF), mix_ref[g]) * sc_ref[:, cols]
            yp_ref[:, cols] = y.astype(BF)
        uc = jnp.concatenate([ch_ref[...] * xh_ref[...] * keep, c_ref[...] * x_ref[...]], axis=0)
        yc = cw_ref[2:3, :] * uc + cw_ref[0:1, :] * pltpu.roll(uc, 2, 0) + cw_ref[1:2, :] * pltpu.roll(uc, 1, 0)
        yc_ref[...] = (b_ref[...] * yc[CONV_HALO:]).astype(BF)

    def main(cb):
        return pl.BlockSpec((R, 512), lambda i: (i, cb))

    def prev(cb, halo, per):
        return pl.BlockSpec((halo, 512), lambda i: (jnp.maximum(i * per - 1, 0), cb))

    full = lambda a: pl.BlockSpec(a.shape, lambda i: (0,) * a.ndim)
    return _pallas_call(
        body, name=name, grid=(T // R,),
        in_specs=[main(0), prev(0, POOL_HALO, PH), main(1), main(2), prev(2, CONV_HALO, CH), main(3),
                  prev(3, CONV_HALO, CH), full(pmix_b), full(pscale), full(convw)],
        out_specs=[pl.BlockSpec((R, 512), lambda i: (i, 0))] * 2,
        out_shape=[jax.ShapeDtypeStruct((T, 512), BF)] * 2,
        compiler_params=_params(("parallel",)),
    )(z, z, z, z, z, z, z, pmix_b, pscale, convw)


def _poolconv_bwd(z, dyp, dyc, pmix_b, pscale, convw, dz, name):
    T = z.shape[0]
    R = min(POOLCONV_ROWS, T)
    PH, CH = R // POOL_HALO, R // CONV_HALO
    nsteps = T // R

    def body(u_ref, uh_ref, b_ref, bn_ref, c_ref, ch_ref, x_ref, xh_ref, dyp_ref, dypn_ref, dyc_ref, dycn_ref,
             mix_ref, sc_ref, cw_ref, dz_in_ref, dz_ref, dmix_ref, dsc_ref, dcw_ref):
        i = pl.program_id(0)
        keep_prev = (i > 0).astype(F32)
        keep_next = (i < nsteps - 1).astype(F32)

        @pl.when(i == 0)
        def _():
            dmix_ref[...] = jnp.zeros_like(dmix_ref)
            dsc_ref[...] = jnp.zeros_like(dsc_ref)
            dcw_ref[...] = jnp.zeros_like(dcw_ref)

        row = i * R + lax.broadcasted_iota(jnp.int32, (R, 1), 0)
        row_ext = i * R + lax.broadcasted_iota(jnp.int32, (R + POOL_HALO, 1), 0)
        w_all = jnp.concatenate([uh_ref[...] * keep_prev, u_ref[...]], axis=0)
        dyp_ext = jnp.concatenate([dyp_ref[...], dypn_ref[...] * keep_next], axis=0)
        for g, w in enumerate(POOL_WINDOWS):
            cols = slice(128 * g, 128 * (g + 1))
            wg = w_all[:, cols]
            s = _causal_window_sum(wg, w)[POOL_HALO:]
            inv_cnt = 1.0 / jnp.minimum(row + 1, w).astype(F32)
            dgrp = (s * inv_cnt - wg[POOL_HALO:]).astype(BF)
            y_pre = _dot(dgrp, mix_ref[g])
            dsc_ref[:, cols] += jnp.sum(dyp_ref[:, cols] * y_pre, axis=0, keepdims=True)
            dyb = (dyp_ext[:, cols] * sc_ref[:, cols]).astype(BF)
            dmix_ref[cols, :] += _dot(dgrp, dyb[:R], "tn")
            dd = _dot(dyb, mix_ref[g], "nt")
            inv_cnt_ext = 1.0 / jnp.minimum(row_ext + 1, w).astype(F32)
            e = _anticausal_window_sum(dd * inv_cnt_ext, w)
            dz_ref[:, cols] = (e[:R] - dd[:R]).astype(BF)
        cw0, cw1, cw2 = cw_ref[0:1, :], cw_ref[1:2, :], cw_ref[2:3, :]
        uc = jnp.concatenate([ch_ref[...] * xh_ref[...] * keep_prev, c_ref[...] * x_ref[...]], axis=0)
        uc1 = pltpu.roll(uc, 1, 0)[CONV_HALO:]
        uc2 = pltpu.roll(uc, 2, 0)[CONV_HALO:]
        uc0 = uc[CONV_HALO:]
        yc = cw2 * uc0 + cw0 * uc2 + cw1 * uc1
        dycv = dyc_ref[...]
        dz_ref[:, 512:1024] = (dycv * yc).astype(BF)
        dv_ext = jnp.concatenate([dycv * b_ref[...], dycn_ref[...] * bn_ref[...] * keep_next], axis=0)
        n_ext = R + CONV_HALO
        duc = (cw2 * dv_ext + cw1 * pltpu.roll(dv_ext, n_ext - 1, 0) + cw0 * pltpu.roll(dv_ext, n_ext - 2, 0))[:R]
        dv = dv_ext[:R]
        dcw_ref[0:1, :] += jnp.sum(dv * uc2, axis=0, keepdims=True)
        dcw_ref[1:2, :] += jnp.sum(dv * uc1, axis=0, keepdims=True)
        dcw_ref[2:3, :] += jnp.sum(dv * uc0, axis=0, keepdims=True)
        dz_ref[:, 1024:1536] = (duc * x_ref[...]).astype(BF)
        dz_ref[:, 1536:2048] = (duc * c_ref[...]).astype(BF)

    def main(cb):
        return pl.BlockSpec((R, 512), lambda i: (i, cb))

    def prev(cb, halo, per):
        return pl.BlockSpec((halo, 512), lambda i: (jnp.maximum(i * per - 1, 0), cb))

    def nxt(cb, halo, per):
        return pl.BlockSpec((halo, 512), lambda i: (jnp.minimum((i + 1) * per, T // halo - 1), cb))

    full = lambda a: pl.BlockSpec(a.shape, lambda i: (0,) * a.ndim)
    return _pallas_call(
        body, name=name, grid=(nsteps,),
        in_specs=[main(0), prev(0, POOL_HALO, PH), main(1), nxt(1, CONV_HALO, CH), main(2), prev(2, CONV_HALO, CH),
                  main(3), prev(3, CONV_HALO, CH), main(0), nxt(0, POOL_HALO, PH), main(0), nxt(0, CONV_HALO, CH),
                  full(pmix_b), full(pscale), full(convw), ANY],
        out_specs=[pl.BlockSpec((R, 2048), lambda i: (i, 0)), pl.BlockSpec((512, 128), lambda i: (0, 0)),
                   pl.BlockSpec((1, 512), lambda i: (0, 0)), pl.BlockSpec((8, 512), lambda i: (0, 0))],
        out_shape=[jax.ShapeDtypeStruct(dz.shape, BF), jax.ShapeDtypeStruct((512, 128), F32),
                   jax.ShapeDtypeStruct((1, 512), F32), jax.ShapeDtypeStruct((8, 512), F32)],
        input_output_aliases={15: 0}, compiler_params=_params(("arbitrary",)),
    )(z, z, z, z, z, z, z, z, dyp, dyp, dyc, dyc, pmix_b, pscale, convw, dz)


def _head_sums(v):
    row = lax.broadcasted_iota(jnp.int32, (LANES, LANES), 0) < HEAD_DIM
    col = lax.broadcasted_iota(jnp.int32, (LANES, LANES), 1) < HEAD_DIM
    same_head = jnp.where(jnp.logical_xor(row, col), 0.0, 1.0).astype(BF)
    hi = v.astype(BF)
    lo = (v - hi.astype(F32)).astype(BF)
    return _dot(hi, same_head) + _dot(lo, same_head)


def _head_norm(x, g2, ma):
    r = lax.rsqrt(_head_sums(x * x) / HEAD_DIM + EPS)
    return x * r, r


def _head_norm_bwd(dy, xhat, r, g2, ma):
    dxh = dy * g2
    return r * (dxh - xhat * (_head_sums(dxh * xhat) / HEAD_DIM))


def _attn_masks(other_block_exists):
    lane = lax.broadcasted_iota(jnp.int32, (2 * ATTN_BLOCK, ATTN_BLOCK), 1)
    qi = lax.broadcasted_iota(jnp.int32, (2 * ATTN_BLOCK, ATTN_BLOCK), 0) & (ATTN_BLOCK - 1)
    never = (1 - other_block_exists.astype(jnp.int32)) * (2 * ATTN_BLOCK)
    return lane[:ATTN_BLOCK] < HEAD_DIM, lane <= qi, lane >= qi + never


def _stack_heads(x, ma):
    return jnp.concatenate([jnp.where(ma, x, 0.0), jnp.where(ma, 0.0, x)], axis=0)


def _unstack_heads(y, ma):
    return jnp.where(ma, y[:ATTN_BLOCK], y[ATTN_BLOCK:])


def _stack_cols(tile, ma):
    return jnp.concatenate([tile[:, 0:1], tile[:, HEAD_DIM:HEAD_DIM + 1]], axis=0)


QKV_TILES = (OFF_GATE - OFF_Q) // LANES
KIND_TILES = QKV_TILES // 3


def _qk_norm(z, gains, name):
    T = z.shape[0]
    tm = min(512, T)

    def body(x_ref, g_ref, o_ref):
        ma = lax.broadcasted_iota(jnp.int32, (tm, LANES), 1) < HEAD_DIM
        for tile in range(QKV_TILES):
            v = x_ref[:, LANES * tile:LANES * (tile + 1)]
            if tile < 2 * KIND_TILES:
                g = g_ref[0:1, :] if tile < KIND_TILES else g_ref[1:2, :]
                v = _head_norm(v, g, ma)[0] * g
            o_ref[tile] = v

    return _pallas_call(
        body, name=name, grid=(T // tm,),
        in_specs=[pl.BlockSpec((pl.Element(tm), pl.Element(OFF_GATE - OFF_Q)), lambda i: (i * tm, OFF_Q)),
                  pl.BlockSpec((8, LANES), lambda i: (0, 0))],
        out_specs=pl.BlockSpec((QKV_TILES, tm, LANES), lambda i: (0, i, 0)),
        out_shape=jax.ShapeDtypeStruct((QKV_TILES, T, LANES), F32), compiler_params=_params(("parallel",)),
    )(z, gains)


ATTN_STEP_ROWS = 2048
ATTN_UNROLL = 4


def _attn_steps(T):
    assert ATTN_STEP_ROWS == ATTN_BLOCK * max(ATTN_DILATIONS) and T % ATTN_STEP_ROWS == 0
    return T // ATTN_STEP_ROWS


def _attn_rows(jj, r, sub, d):
    start = jj * sub + r
    if d == 1:
        return pl.ds(pl.multiple_of(start, ATTN_BLOCK), ATTN_BLOCK)
    return pl.ds(start, ATTN_BLOCK, stride=d)


def _pick(flag, a, b):
    return jnp.where(jnp.full(a.shape, flag.astype(jnp.int32)) > 0, a, b)


def _attn_fwd(qkv, name):
    T = qkv.shape[1]
    nbig = _attn_steps(T)
    scale = HEAD_DIM ** -0.5

    def body(q_ref, kc_ref, kp_ref, vc_ref, vp_ref, o_ref, lse_ref):
        jb = pl.program_id(1)
        for gi, d in enumerate(ATTN_DILATIONS):
            pl.when(pl.program_id(0) == gi)(functools.partial(group, d, jb, q_ref, kc_ref, kp_ref, vc_ref, vp_ref,
                                                              o_ref, lse_ref))

    def group(d, jb, q_ref, kc_ref, kp_ref, vc_ref, vp_ref, o_ref, lse_ref):
        sub, m = ATTN_BLOCK * d, ATTN_STEP_ROWS // (ATTN_BLOCK * d)

        def step(s, carry):
            jj, r = s // d, s % d
            here, before = _attn_rows(jj, r, sub, d), _attn_rows(jnp.maximum(jj - 1, 0), r, sub, d)
            edge = _attn_rows(m - 1, r, sub, d)
            first = jj == 0
            ma, mask_c, mask_p = _attn_masks(jb * m + jj > 0)
            qs = _stack_heads(q_ref[here, :], ma).astype(BF)
            kcb = kc_ref[here, :].astype(BF)
            kpb = _pick(first, kp_ref[edge, :], kc_ref[before, :]).astype(BF)
            vcb = vc_ref[here, :].astype(BF)
            vpb = _pick(first, vp_ref[edge, :], vc_ref[before, :]).astype(BF)
            s_c = jnp.where(mask_c, _dot(qs, kcb, "nt") * scale, MASK_VALUE)
            s_p = jnp.where(mask_p, _dot(qs, kpb, "nt") * scale, MASK_VALUE)
            mx = jnp.maximum(jnp.max(s_c, axis=-1, keepdims=True), jnp.max(s_p, axis=-1, keepdims=True))
            p_c = jnp.exp(s_c - mx)
            p_p = jnp.exp(s_p - mx)
            den = jnp.sum(p_c, axis=-1, keepdims=True) + jnp.sum(p_p, axis=-1, keepdims=True)
            o = (_dot(p_c.astype(BF), vcb) + _dot(p_p.astype(BF), vpb)) / den
            o_ref[here, :] = _unstack_heads(o, ma)
            lse_ref[here, :] = _unstack_heads(jnp.broadcast_to(mx + jnp.log(den), o.shape), ma)
            return carry

        lax.fori_loop(0, m * d, step, 0, unroll=ATTN_UNROLL)

    def cur(kind):
        return pl.BlockSpec((None, ATTN_STEP_ROWS, LANES), lambda g, j, t: (KIND_TILES * kind + 2 * g + t, j, 0))

    def prv(kind):
        return pl.BlockSpec((None, ATTN_STEP_ROWS, LANES),
                            lambda g, j, t: (KIND_TILES * kind + 2 * g + t, jnp.maximum(j - 1, 0), 0))

    out = pl.BlockSpec((ATTN_STEP_ROWS, LANES), lambda g, j, t: (j, 2 * g + t))
    width = 2 * LANES * len(ATTN_DILATIONS)
    return _pallas_call(
        body, name=name, grid=(len(ATTN_DILATIONS), nbig, 2), in_specs=[cur(0), cur(1), prv(1), cur(2), prv(2)],
        out_specs=[out, out], out_shape=[jax.ShapeDtypeStruct((T, width), F32)] * 2,
        compiler_params=_params(("parallel", "parallel", "parallel")),
    )(qkv, qkv, qkv, qkv, qkv)


def _attn_bwd(z, qkv, do, c, lse, gains, name, after=None):
    T = z.shape[0]
    nbig = _attn_steps(T)
    scale = HEAD_DIM ** -0.5
    extra = [] if after is None else [after]

    def body(*refs):
        g, jb = pl.program_id(0), pl.program_id(1)
        dgq_ref, dgk_ref = refs[len(refs) - 5], refs[len(refs) - 4]

        @pl.when((g == 0) & (jb == 0) & (pl.program_id(2) == 0))
        def _():
            dgq_ref[...] = jnp.zeros_like(dgq_ref)
            dgk_ref[...] = jnp.zeros_like(dgk_ref)

        for gi, d in enumerate(ATTN_DILATIONS):
            pl.when(g == gi)(functools.partial(group, d, jb, *refs))

    def group(d, jb, qr_ref, kr_ref, vc_ref, vp_ref, qn_ref, qnn_ref, kn_ref, knp_ref, do_ref, don_ref, c_ref, cn_ref,
              lse_ref, lsen_ref, g_ref, *rest):
        dq_ref, dk_ref, dv_ref, dgq_ref, dgk_ref, sq_ref, sk_ref, sv_ref = rest[len(extra):]
        sub, m = ATTN_BLOCK * d, ATTN_STEP_ROWS // (ATTN_BLOCK * d)
        nb = T // sub
        gq, gk = g_ref[0:1, :], g_ref[1:2, :]

        def step(s, carry):
            jj, r = s // d, s % d
            here = _attn_rows(jj, r, sub, d)
            before = _attn_rows(jnp.maximum(jj - 1, 0), r, sub, d)
            behind = _attn_rows(jnp.minimum(jj + 1, m - 1), r, sub, d)
            edge_before, edge_behind = _attn_rows(m - 1, r, sub, d), _attn_rows(0, r, sub, d)
            first, last = jj == 0, jj == m - 1
            block = jb * m + jj
            ma, mask_c, mask_p = _attn_masks(block > 0)
            mask_n = _attn_masks(block < nb - 1)[2]
            qhat, rq = _head_norm(qr_ref[here, :], gq, ma)
            qn = qn_ref[here, :]
            qn_next = _pick(last, qnn_ref[edge_behind, :], qn_ref[behind, :])
            khat, rk = _head_norm(kr_ref[here, :], gk, ma)
            kcb = kn_ref[here, :].astype(BF)
            kpb = _pick(first, knp_ref[edge_before, :], kn_ref[before, :]).astype(BF)
            vcb = vc_ref[here, :].astype(BF)
            vpb = _pick(first, vp_ref[edge_before, :], vc_ref[before, :]).astype(BF)
            do_t, don_t = do_ref[here, :], _pick(last, don_ref[edge_behind, :], do_ref[behind, :])
            c_t, cn_t = c_ref[here, :], _pick(last, cn_ref[edge_behind, :], c_ref[behind, :])
            lse_t, lsen_t = lse_ref[here, :], _pick(last, lsen_ref[edge_behind, :], lse_ref[behind, :])
            qs, dos = _stack_heads(qn, ma).astype(BF), _stack_heads(do_t, ma).astype(BF)
            lse_s, c_s = _stack_cols(lse_t, ma), _stack_cols(c_t, ma)
            s_c = jnp.where(mask_c, _dot(qs, kcb, "nt") * scale, MASK_VALUE)
            s_p = jnp.where(mask_p, _dot(qs, kpb, "nt") * scale, MASK_VALUE)
            p_c = jnp.exp(s_c - lse_s)
            p_p = jnp.exp(s_p - lse_s)
            ds_c = ((p_c * (_dot(dos, vcb, "nt") + c_s)) * scale).astype(BF)
            ds_p = ((p_p * (_dot(dos, vpb, "nt") + c_s)) * scale).astype(BF)
            dq_t = _unstack_heads(_dot(ds_c, kcb) + _dot(ds_p, kpb), ma)
            qs_n, dos_n = _stack_heads(qn_next, ma).astype(BF), _stack_heads(don_t, ma).astype(BF)
            s_n = jnp.where(mask_n, _dot(qs_n, kcb, "nt") * scale, MASK_VALUE)
            p_n = jnp.exp(s_n - _stack_cols(lsen_t, ma))
            ds_n = ((p_n * (_dot(dos_n, vcb, "nt") + _stack_cols(cn_t, ma))) * scale).astype(BF)
            dv_t = _dot(p_c.astype(BF), dos, "tn") + _dot(p_n.astype(BF), dos_n, "tn")
            dk_t = _dot(ds_c, qs, "tn") + _dot(ds_n, qs_n, "tn")
            sq_ref[here, :] = _head_norm_bwd(dq_t, qhat, rq, gq, ma)
            sk_ref[here, :] = _head_norm_bwd(dk_t, khat, rk, gk, ma)
            sv_ref[here, :] = dv_t
            dgq_ref[...] += jnp.sum(dq_t * qhat, axis=0, keepdims=True)
            dgk_ref[...] += jnp.sum(dk_t * khat, axis=0, keepdims=True)
            return carry

        lax.fori_loop(0, m * d, step, 0, unroll=ATTN_UNROLL)
        dq_ref[...] = sq_ref[...].astype(BF)
        dk_ref[...] = sk_ref[...].astype(BF)
        dv_ref[...] = sv_ref[...].astype(BF)

    rows = ATTN_STEP_ROWS

    def raw(col0):
        return pl.BlockSpec((rows, LANES), lambda g, j, t: (j, col0 + 2 * g + t))

    def cur(kind):
        return pl.BlockSpec((None, rows, LANES), lambda g, j, t: (KIND_TILES * kind + 2 * g + t, j, 0))

    def prv(kind):
        return pl.BlockSpec((None, rows, LANES), lambda g, j, t: (KIND_TILES * kind + 2 * g + t, jnp.maximum(j - 1, 0), 0))

    def nxt(kind):
        return pl.BlockSpec((None, rows, LANES),
                            lambda g, j, t: (KIND_TILES * kind + 2 * g + t, jnp.minimum(j + 1, nbig - 1), 0))

    own = pl.BlockSpec((rows, LANES), lambda g, j, t: (j, 2 * g + t))
    own_next = pl.BlockSpec((rows, LANES), lambda g, j, t: (jnp.minimum(j + 1, nbig - 1), 2 * g + t))
    vec = pl.BlockSpec((1, LANES), lambda g, j, t: (0, 0))
    width = 2 * LANES * len(ATTN_DILATIONS)
    return _pallas_call(
        body, name=name, grid=(len(ATTN_DILATIONS), nbig, 2),
        in_specs=[raw(OFF_Q // LANES), raw(OFF_K // LANES), cur(2), prv(2), cur(0), nxt(0), cur(1), prv(1), own, own_next,
                  own, own_next, own, own_next, pl.BlockSpec((8, LANES), lambda g, j, t: (0, 0))] + [ANY] * len(extra),
        out_specs=[own, own, own, vec, vec],
        out_shape=[jax.ShapeDtypeStruct((T, width), BF)] * 3 + [jax.ShapeDtypeStruct((1, LANES), F32)] * 2,
        scratch_shapes=[pltpu.VMEM((rows, LANES), F32)] * 3,
        compiler_params=_params(("arbitrary", "arbitrary", "arbitrary")),
    )(z, z, qkv, qkv, qkv, qkv, qkv, qkv, do, do, c, c, lse, lse, gains, *extra)


MERGE_ROWS = 256
GATE_TILE = 256


def _group_mix(o_refs, lse_refs):
    lses = [r[...] for r in lse_refs]
    m = jnp.maximum(jnp.maximum(lses[0], lses[1]), lses[2])
    es = [jnp.exp(l - m) for l in lses]
    den = es[0] + es[1] + es[2]
    ws = [e / den for e in es]
    y = ws[0] * o_refs[0][...] + ws[1] * o_refs[1][...] + ws[2] * o_refs[2][...]
    return ws, y


def _sigmoid(v):
    return 1.0 / (1.0 + jnp.exp(-v))


def _merge_specs(T, z, bgate, gpu, gco, gau):
    tm = min(MERGE_ROWS, T)
    row = lambda w: pl.BlockSpec((tm, w), lambda i: (i, 0))
    gate0 = OFF_GATE // GATE_TILE
    gates = [pl.BlockSpec((tm, GATE_TILE), functools.partial(lambda i, cb: (i, cb), cb=gate0 + n))
             for n in range(3 * N_CHIPS)]
    full = lambda a: pl.BlockSpec(a.shape, lambda i: (0,) * a.ndim)
    by_group = [pl.BlockSpec((tm, 256), functools.partial(lambda i, g: (i, g), g=g)) for g in range(3)]
    specs = [row(512), row(512)] + by_group * 2 + gates + [full(bgate), full(gpu), full(gco), full(gau)]
    return tm, row, specs


def _merge_fwd(yp, yc, o3, lse3, z, bgate, gpu, gco, gau, name):
    T = yp.shape[0]
    tm, row, specs = _merge_specs(T, z, bgate, gpu, gco, gau)

    def body(*refs):
        yp_ref, yc_ref = refs[0], refs[1]
        o_refs, lse_refs = refs[2:5], refs[5:8]
        zg = refs[8:20]
        b_ref, gpu_ref, gco_ref, gau_ref, out_ref = refs[20:25]
        yab = _group_mix(o_refs, lse_refs)[1].astype(BF)
        ys = (yp_ref[...], yc_ref[...], yab)
        ups = (gpu_ref, gco_ref, gau_ref)
        for n in range(N_CHIPS):
            acc = None
            for b in range(3):
                gcol = slice(1024 * b + GATE_TILE * n, 1024 * b + GATE_TILE * (n + 1))
                gate = _sigmoid(zg[N_CHIPS * b + n][...] + b_ref[:, gcol])
                term = gate * _dot(ys[b], ups[b][n])
                acc = term if acc is None else acc + term
            out_ref[:, GATE_TILE * n:GATE_TILE * (n + 1)] = acc.astype(BF)

    return _pallas_call(
        body, name=name, grid=(T // tm,), in_specs=specs, out_specs=row(1024),
        out_shape=jax.ShapeDtypeStruct((T, 1024), BF), compiler_params=_params(("parallel",)),
    )(yp, yc, *([o3] * 3), *([lse3] * 3), *([z] * 12), bgate, gpu, gco, gau)


def _merge_bwd(dm, yp, yc, o3, lse3, z, bgate, gpu, gco, gau, name):
    T = yp.shape[0]
    tm, row, specs = _merge_specs(T, z, bgate, gpu, gco, gau)
    nsteps = T // tm

    def body(*refs):
        dm_ref, yp_ref, yc_ref = refs[0:3]
        o_refs, lse_refs = refs[3:6], refs[6:9]
        zg = refs[9:21]
        b_ref, gpu_ref, gco_ref, gau_ref = refs[21:25]
        dzg_ref, dyp_ref, dyc_ref = refs[25:28]
        do_ref, c_ref = refs[28:30]
        dgpu_ref, dgco_ref, dgau_ref, dbg_ref = refs[30:34]
        accs = refs[34:37]
        i = pl.program_id(0)

        @pl.when(i == 0)
        def _():
            for a in accs:
                a[...] = jnp.zeros_like(a)
            dbg_ref[...] = jnp.zeros_like(dbg_ref)

        ws, y = _group_mix(o_refs, lse_refs)
        ys = (yp_ref[...], yc_ref[...], y.astype(BF))
        ups = (gpu_ref, gco_ref, gau_ref)
        dys = [None, None, None]
        for n in range(N_CHIPS):
            dmn = dm_ref[:, GATE_TILE * n:GATE_TILE * (n + 1)]
            for b in range(3):
                gcol = slice(1024 * b + GATE_TILE * n, 1024 * b + GATE_TILE * (n + 1))
                gate = _sigmoid(zg[N_CHIPS * b + n][...] + b_ref[:, gcol])
                up = _dot(ys[b], ups[b][n])
                dzg = (dmn * up) * (gate * (1.0 - gate))
                dzg_ref[:, gcol] = dzg.astype(BF)
                dbg_ref[:, gcol] += jnp.sum(dzg, axis=0, keepdims=True)
                dup = (dmn * gate).astype(BF)
                accs[b][n] += _dot(ys[b], dup, "tn")
                dyb = _dot(dup, ups[b][n], "nt")
                dys[b] = dyb if dys[b] is None else dys[b] + dyb
        dyp_ref[...] = dys[0]
        dyc_ref[...] = dys[1]
        dya = dys[2]
        lane = lax.broadcasted_iota(jnp.int32, dya.shape, 1) // HEAD_DIM
        pr = dya * y
        rho = jnp.zeros_like(pr)
        for h in range(256 // HEAD_DIM):
            hm = lane == h
            rho = jnp.where(hm, jnp.sum(jnp.where(hm, pr, 0.0), axis=-1, keepdims=True), rho)
        for g in range(3):
            do_ref[:, 256 * g:256 * (g + 1)] = ws[g] * dya
            c_ref[:, 256 * g:256 * (g + 1)] = -(ws[g] * rho)

        @pl.when(i == nsteps - 1)
        def _():
            dgpu_ref[...] = accs[0][...].astype(BF)
            dgco_ref[...] = accs[1][...].astype(BF)
            dgau_ref[...] = accs[2][...].astype(BF)

    full = lambda a: pl.BlockSpec(a.shape, lambda i: (0,) * a.ndim)
    dz_gate = pl.BlockSpec((pl.Element(tm), pl.Element(3072)), lambda i: (i * tm, OFF_GATE))
    out_specs = ([dz_gate, row(512), row(512)] + [row(768)] * 2 + [full(gpu), full(gco), full(gau)]
                 + [pl.BlockSpec((1, 3072), lambda i: (0, 0))])
    out_shape = ([jax.ShapeDtypeStruct(z.shape, BF)] + [jax.ShapeDtypeStruct((T, 512), F32)] * 2
                 + [jax.ShapeDtypeStruct((T, 768), F32)] * 2
                 + [jax.ShapeDtypeStruct(g.shape, BF) for g in (gpu, gco, gau)]
                 + [jax.ShapeDtypeStruct((1, 3072), F32)])
    return _pallas_call(
        body, name=name, grid=(nsteps,), in_specs=[row(1024)] + specs, out_specs=out_specs, out_shape=out_shape,
        scratch_shapes=[pltpu.VMEM(g.shape, F32) for g in (gpu, gco, gau)],
        compiler_params=_params(("arbitrary",)),
    )(dm, yp, yc, *([o3] * 3), *([lse3] * 3), *([z] * 12), bgate, gpu, gco, gau)


def _layer_fwd(x, w, tag, after=None, soon=None, late=None, target=None, hb=None, next_gain=None):
    if hb is None:
        hb = _rms_fwd(x, w["norm_mix"], f"rms_mix_{tag}", after=after)
    if soon is not None:
        w = dict(w, **soon(hb))
    z = _mm(hb, w["w_in"], "nt", f"in_proj_{tag}", tm=512, tn=3712, tk=1024, n_outer=True, after=w.get("started"))
    yp, yc = _poolconv_fwd(z, w["pool_mix"], w["pool_scale"], w["conv_w"], f"poolconv_{tag}")
    qkv = _qk_norm(z, w["qk_gain"], f"qk_norm_{tag}")
    o3, lse3 = _attn_fwd(qkv, f"attn_{tag}")
    if late is not None:
        w = dict(w, **late(lse3))
    merged = _merge_fwd(yp, yc, o3, lse3, z, w["b_gate"], w["w_pool_up"], w["w_conv_out"], w["w_attn_up"],
                        f"merge_{tag}")
    x1, h2b = _mm(merged, w["w_o"], "nn", f"out_proj_{tag}", tm=1024, tn=1024, tk=1024, res=x, vec=w["norm_mlp"],
                  epi="rms_next")
    rb = _mm(h2b, w["w_ff1"], "nn", f"ff1_{tag}", tm=1024, tn=1024, tk=1024, out_dtype=BF, epi="relu2", n_outer=True,
             b_shards=True)
    if target is not None:
        x2 = _mm(rb, w["w_ff2"], "nn", f"ff2_{tag}", tm=512, tn=1024, tk=4096, res=x1, aux=target, epi="loss")
    elif next_gain is not None:
        x2 = _mm(rb, w["w_ff2"], "nn", f"ff2_{tag}", tm=512, tn=1024, tk=4096, res=x1, vec=next_gain, epi="rms_next")
    else:
        x2 = _mm(rb, w["w_ff2"], "nn", f"ff2_{tag}", tm=512, tn=1024, tk=4096, res=x1)
    saved = dict(x=x, hb=hb, z=z, yp=yp, yc=yc, qkv=qkv, o3=o3, lse3=lse3, merged=merged, x1=x1, h2b=h2b, rb=rb)
    return x2, saved, w


def _layer_bwd(dx2, w, s, tag, after=None, mid=None, tail=None):
    g = {}
    dab = _mm(dx2, w["w_ff2"], "nt", f"d_ff2_act_{tag}", tm=1024, tn=1024, tk=1024, out_dtype=BF, aux=s["rb"],
              epi="drelu2", after=after)
    g["w_ff2"] = _mm(s["rb"], dx2, "tn", f"d_ff2_w_{tag}", tm=1024, tn=1024, tk=2048, out_dtype=BF)
    g["w_ff1"] = _mm(s["h2b"], dab, "tn", f"d_ff1_w_{tag}", tm=1024, tn=1024, tk=2048, out_dtype=BF, out_shards=True)
    dx1, g["norm_mlp"] = _mm(dab, w["w_ff1"], "nt", f"d_ff1_act_{tag}", tm=1024, tn=1024, tk=1024, b_shards=True,
                             res=dx2, aux=s["x1"], vec=w["norm_mlp"], epi="rms_bwd")
    dm = _mm(dx1, w["w_o"], "nt", f"d_out_act_{tag}", tm=1024, tn=1024, tk=1024)
    g["w_o"] = _mm(s["merged"], dx1, "tn", f"d_out_w_{tag}", tm=1024, tn=1024, tk=1024, out_dtype=BF)
    (dz, dyp, dyc, do3, c3, g["w_pool_up"], g["w_conv_out"], g["w_attn_up"],
     g["b_gate"]) = _merge_bwd(dm, s["yp"], s["yc"], s["o3"], s["lse3"], s["z"], w["b_gate"], w["w_pool_up"],
                               w["w_conv_out"], w["w_attn_up"], f"d_merge_{tag}")
    behind = mid(g) if mid is not None else None
    dzq, dzk, dzv, dgq, dgk = _attn_bwd(s["z"], s["qkv"], do3, c3, s["lse3"], w["qk_gain"], f"d_attn_{tag}",
                                        after=behind)
    g["q_gain"] = dgq[:, :HEAD_DIM] + dgq[:, HEAD_DIM:]
    g["k_gain"] = dgk[:, :HEAD_DIM] + dgk[:, HEAD_DIM:]
    for off, piece in ((OFF_Q, dzq), (OFF_K, dzk), (OFF_V, dzv)):
        dz = lax.dynamic_update_slice(dz, piece, (0, off))
    dz, g["pool_mix"], g["pool_scale"], g["conv_w"] = _poolconv_bwd(
        s["z"], dyp, dyc, w["pool_mix"], w["pool_scale"], w["conv_w"], dz, f"d_poolconv_{tag}")
    g["w_in"] = _mm(s["hb"], dz, "tn", f"d_in_w_{tag}", tm=512, tn=3712, tk=1024, out_dtype=BF)
    dh = _mm(dz, w["w_in"], "nn", f"d_in_act_{tag}", tm=1024, tn=1024, tk=3712,
             after=tail(g) if tail is not None else None)
    dx, g["norm_mix"] = _rms_bwd(dh, s["x"], w["norm_mix"], dx1, f"d_rms_mix_{tag}")
    return dx, g


def _position():
    x, y, c = lax.axis_index("x"), lax.axis_index("y"), lax.axis_index("c")
    chips = [(1 - x, y), (x, 1 - y), (1 - x, 1 - y)]
    return x, y, c, 2 * x + y, chips, [2 * cx + cy for cx, cy in chips]


def _remote(src, dst, ssem, rsem, dev):
    return pltpu.make_async_remote_copy(src_ref=src, dst_ref=dst, send_sem=ssem, recv_sem=rsem, device_id=dev,
                                        device_id_type=MESH_ID)


def _halves(a):
    return a.reshape(a.shape[0], 2, a.shape[1] // 2, a.shape[2])


SEM = pl.BlockSpec(memory_space=pltpu.SEMAPHORE)
TOKEN = jax.ShapeDtypeStruct((8, LANES), F32)
TOKEN_SPEC = pl.BlockSpec(memory_space=pltpu.VMEM)


def _split_params():
    return pltpu.CompilerParams(has_side_effects=pltpu.SideEffectType.DATAFLOW_SIDE_EFFECTING)


def _gather_start(bufs, name, after):
    n = len(bufs)
    views = [_halves(b) for b in bufs]

    def body(*refs):
        first_sem = n + 1
        ssem, rsem = refs[first_sem:first_sem + ns], refs[first_sem + ns:first_sem + 2 * ns]
        outs, token = refs[first_sem + 2 * ns:first_sem + 2 * ns + n], refs[first_sem + 2 * ns + n]
        x, y, c, q, chips, qs = _position()
        for k in range(n):
            mine = outs[k].at[q, c]
            for j, chip in enumerate(chips):
                _remote(mine, mine, ssem[3 * k + j], rsem[3 * k + j], (chip[0], chip[1], c)).start()
        token[...] = jnp.zeros_like(token)

    ns = 3 * n
    outs = _pallas_call(
        body, name=name, in_specs=[ANY] * (n + 1), out_specs=[SEM] * (2 * ns) + [ANY] * n + [TOKEN_SPEC],
        out_shape=[pltpu.SemaphoreType.DMA(())] * (2 * ns) + [jax.ShapeDtypeStruct(v.shape, v.dtype) for v in views]
        + [TOKEN],
        input_output_aliases={k: k + 2 * ns for k in range(n)}, compiler_params=_split_params(),
    )(*views, after)
    return list(outs[:ns]), list(outs[ns:2 * ns]), list(outs[2 * ns:2 * ns + n]), outs[2 * ns + n]


def _gather_finish(ssem, rsem, views, after, name_wait, name_forward, shapes):
    n = len(views)
    ns = len(ssem)

    def wait_body(*refs):
        ssem_ref, rsem_ref = refs[n:n + ns], refs[n + ns:n + 2 * ns]
        outs = refs[n + 2 * ns + 1:]
        x, y, c, q, chips, qs = _position()
        for k in range(n):
            for j, chip in enumerate(chips):
                cp = _remote(outs[k].at[q, c], outs[k].at[qs[j], c], ssem_ref[3 * k + j], rsem_ref[3 * k + j],
                             (chip[0], chip[1], c))
                cp.wait_send()
                cp.wait_recv()

    landed = _pallas_call(
        wait_body, name=name_wait, in_specs=[ANY] * n + [SEM] * (2 * ns) + [ANY], out_specs=[ANY] * n,
        out_shape=[jax.ShapeDtypeStruct(v.shape, v.dtype) for v in views],
        input_output_aliases={k: k for k in range(n)}, compiler_params=_split_params(),
    )(*views, *ssem, *rsem, after)

    def forward_body(*refs):
        outs = refs[n:2 * n]
        fssem, frsem = refs[2 * n:]
        x, y, c, q, chips, qs = _position()
        sib = (x, y, 1 - c)
        sent = []
        for k in range(n):
            for j in range(3):
                slot = outs[k].at[qs[j], c]
                cp = _remote(slot, slot, fssem.at[k, j], frsem.at[k, j], sib)
                cp.start()
                sent.append(cp)
        for k in range(n):
            for j in range(3):
                slot = outs[k].at[qs[j], 1 - c]
                _remote(slot, slot, fssem.at[k, j], frsem.at[k, j], sib).wait_recv()
        for cp in sent:
            cp.wait_send()

    outs = _pallas_call(
        forward_body, name=name_forward, in_specs=[ANY] * n, out_specs=[ANY] * n,
        out_shape=[jax.ShapeDtypeStruct(v.shape, v.dtype) for v in views],
        input_output_aliases={k: k for k in range(n)}, scratch_shapes=[pltpu.SemaphoreType.DMA((n, 3))] * 2,
    )(*landed)
    return [o.reshape(s) for o, s in zip(outs, shapes)]


def _chip_exchange_start(parts, name):
    n = len(parts)

    def body(*refs):
        ssem, rsem = refs[n:n + ns], refs[n + ns:n + 2 * ns]
        base = n + 2 * ns
        srcs, outs, token = refs[base:base + n], refs[base + n:base + 2 * n], refs[base + 2 * n]
        x, y, c, q, chips, qs = _position()
        for k in range(n):
            for j, chip in enumerate(chips):
                _remote(srcs[k].at[qs[j]], outs[k].at[j], ssem[3 * k + j], rsem[3 * k + j],
                        (chip[0], chip[1], c)).start()
        token[...] = jnp.zeros_like(token)

    ns = 3 * n
    outs = _pallas_call(
        body, name=name, in_specs=[ANY] * n, out_specs=[SEM] * (2 * ns) + [ANY] * (2 * n) + [TOKEN_SPEC],
        out_shape=[pltpu.SemaphoreType.DMA(())] * (2 * ns) + [jax.ShapeDtypeStruct(a.shape, a.dtype) for a in parts]
        + [jax.ShapeDtypeStruct((3,) + a.shape[1:], a.dtype) for a in parts] + [TOKEN],
        input_output_aliases={k: k + 2 * ns for k in range(n)}, compiler_params=_split_params(),
    )(*parts)
    b = 2 * ns
    return list(outs[:ns]), list(outs[ns:b]), list(outs[b:b + n]), list(outs[b + n:b + 2 * n]), outs[b + 2 * n]


def _chip_exchange_wait(ssem, rsem, parts, landing, after, name):
    n = len(parts)
    ns = len(ssem)

    def body(*refs):
        ssem_ref, rsem_ref = refs[2 * n:2 * n + ns], refs[2 * n + ns:2 * n + 2 * ns]
        base = 2 * n + 2 * ns + 1
        srcs, outs = refs[base:base + n], refs[base + n:]
        x, y, c, q, chips, qs = _position()
        for k in range(n):
            for j, chip in enumerate(chips):
                cp = _remote(srcs[k].at[qs[j]], outs[k].at[j], ssem_ref[3 * k + j], rsem_ref[3 * k + j],
                             (chip[0], chip[1], c))
                cp.wait_send()
                cp.wait_recv()

    outs = _pallas_call(
        body, name=name, in_specs=[ANY] * (2 * n) + [SEM] * (2 * ns) + [ANY], out_specs=[ANY] * (2 * n),
        out_shape=[jax.ShapeDtypeStruct(a.shape, a.dtype) for a in list(parts) + list(landing)],
        input_output_aliases={k: k for k in range(2 * n)}, compiler_params=_split_params(),
    )(*parts, *landing, *ssem, *rsem, after)
    return list(outs[:n]), list(outs[n:])


def _pair_swap(views, name):
    n = len(views)

    def body(*refs):
        ins, outs = refs[:n], refs[n:2 * n]
        ssem, rsem = refs[2 * n:]
        x, y, c, _, _, _ = _position()
        cps = [_remote(ins[k].at[pl.ds(0, N_CHIPS), 1 - c], outs[k], ssem.at[k], rsem.at[k], (x, y, 1 - c))
               for k in range(n)]
        for cp in cps:
            cp.start()
        for cp in cps:
            cp.wait()

    return _pallas_call(
        body, name=name, in_specs=[ANY] * n, out_specs=[ANY] * n,
        out_shape=[jax.ShapeDtypeStruct((v.shape[0],) + v.shape[2:], v.dtype) for v in views],
        scratch_shapes=[pltpu.SemaphoreType.DMA((n,))] * 2,
    )(*views)


def _pair_send(arrays, name):
    n = len(arrays)

    def body(*refs):
        ins, outs = refs[:n], refs[n:2 * n]
        ssem, rsem = refs[2 * n:]
        x, y, c, _, _, _ = _position()
        cps = [_remote(ins[k], outs[k], ssem.at[k], rsem.at[k], (x, y, 1 - c)) for k in range(n)]
        for cp in cps:
            cp.start()
        for cp in cps:
            cp.wait()

    return _pallas_call(
        body, name=name, in_specs=[ANY] * n, out_specs=[ANY] * n,
        out_shape=[jax.ShapeDtypeStruct(a.shape, a.dtype) for a in arrays],
        scratch_shapes=[pltpu.SemaphoreType.DMA((n,))] * 2,
    )(*arrays)


def _all_to_all_small(part):
    P = part.shape[0]

    def body(in_ref, out_ref, lsem, ssem, rsem):
        x, y, c = lax.axis_index("x"), lax.axis_index("y"), lax.axis_index("c")
        me = 4 * x + 2 * y + c
        flips = [(fx, fy, fc) for fx in (0, 1) for fy in (0, 1) for fc in (0, 1)][1:]
        peers = [((x + fx) % 2, (y + fy) % 2, (c + fc) % 2) for fx, fy, fc in flips]
        loc = pltpu.make_async_copy(in_ref, out_ref.at[me], lsem)
        loc.start()
        cps = [_remote(in_ref, out_ref.at[me], ssem.at[j], rsem.at[j], peer) for j, peer in enumerate(peers)]
        for cp in cps:
            cp.start()
        for j, (px, py, pc) in enumerate(peers):
            _remote(in_ref, out_ref.at[4 * px + 2 * py + pc], ssem.at[j], rsem.at[j], peers[j]).wait_recv()
        for cp in cps:
            cp.wait_send()
        loc.wait()

    return _pallas_call(
        body, name="small_exchange", in_specs=[ANY], out_specs=ANY,
        out_shape=jax.ShapeDtypeStruct((8, P, LANES), F32),
        scratch_shapes=[pltpu.SemaphoreType.DMA(())] + [pltpu.SemaphoreType.DMA((7,))] * 2,
    )(part)


def _small_peers():
    x, y, c = lax.axis_index("x"), lax.axis_index("y"), lax.axis_index("c")
    flips = [(fx, fy, fc) for fx in (0, 1) for fy in (0, 1) for fc in (0, 1)][1:]
    peers = [((x + fx) % 2, (y + fy) % 2, (c + fc) % 2) for fx, fy, fc in flips]
    return 4 * x + 2 * y + c, peers


def _all_to_all_small_start(part, name):
    P = part.shape[0]
    me = 4 * lax.axis_index("x") + 2 * lax.axis_index("y") + lax.axis_index("c")
    landing = lax.dynamic_update_slice(jnp.zeros((8, P, LANES), F32), part[None], (me, 0, 0))

    def body(*refs):
        sems, src, land, token = refs[2:16], refs[16], refs[17], refs[18]
        me_, peers = _small_peers()
        for j, peer in enumerate(peers):
            _remote(src, land.at[me_], sems[j], sems[7 + j], peer).start()
        token[...] = jnp.zeros_like(token)

    outs = _pallas_call(
        body, name=name, in_specs=[ANY, ANY], out_specs=[SEM] * 14 + [ANY, ANY, TOKEN_SPEC],
        out_shape=[pltpu.SemaphoreType.DMA(())] * 14 + [jax.ShapeDtypeStruct(part.shape, F32),
                                                       jax.ShapeDtypeStruct((8, P, LANES), F32), TOKEN],
        input_output_aliases={0: 14, 1: 15}, compiler_params=_split_params(),
    )(part, landing)
    return list(outs[:7]), list(outs[7:14]), outs[14], outs[15], outs[16]


def _all_to_all_small_wait(ssem, rsem, part, landing, after, name):
    def body(*refs):
        sems, src, land = refs[2:16], refs[17], refs[18]
        _, peers = _small_peers()
        for j, (px, py, pc) in enumerate(peers):
            cp = _remote(src, land.at[4 * px + 2 * py + pc], sems[j], sems[7 + j], peers[j])
            cp.wait_send()
            cp.wait_recv()

    return _pallas_call(
        body, name=name, in_specs=[ANY, ANY] + [SEM] * 14 + [ANY], out_specs=[ANY, ANY],
        out_shape=[jax.ShapeDtypeStruct(part.shape, F32), jax.ShapeDtypeStruct(landing.shape, F32)],
        input_output_aliases={0: 0, 1: 1}, compiler_params=_split_params(),
    )(part, landing, *ssem, *rsem, after)[1]


def _row_tile(rows, width, n_arrays):
    t = rows
    while t % 2 == 0 and t > 8 and 2 * n_arrays * t * width * 4 > VMEM_LIMIT // 2:
        t //= 2
    return t


def _chip():
    return 2 * lax.axis_index("x") + lax.axis_index("y")


def _core():
    return lax.axis_index("c")


def _cast_place(w3, layer, name):
    _, r, c = w3.shape
    tr = _row_tile(r, c, 2)

    def body(w_ref, o_ref):
        o_ref[...] = w_ref[...].astype(BF)

    return _pallas_call(
        body, name=name, grid=(r // tr,), in_specs=[pl.BlockSpec((None, tr, c), lambda i: (layer, i, 0))],
        out_specs=pl.BlockSpec((None, tr, c), lambda i: (_chip(), i, 0)),
        out_shape=jax.ShapeDtypeStruct((N_CHIPS, r, c), BF), compiler_params=_params(("parallel",)),
    )(w3)


def _pair_sum(views, recvs, name):
    n = len(views)

    def body(*refs):
        for g_ref, r_ref, o_ref in zip(refs[:n], refs[n:2 * n], refs[2 * n:]):
            o_ref[...] = (g_ref[...].astype(F32) + r_ref[...].astype(F32)).astype(BF)

    own = [pl.BlockSpec((None, None) + v.shape[2:], lambda p: (p, _core(), 0, 0)) for v in views]
    blk = [pl.BlockSpec((None,) + r.shape[1:], lambda p: (p, 0, 0)) for r in recvs]
    return _pallas_call(
        body, name=name, grid=(N_CHIPS,), in_specs=own + blk, out_specs=blk,
        out_shape=[jax.ShapeDtypeStruct(r.shape, BF) for r in recvs], compiler_params=_params(("parallel",)),
    )(*views, *recvs)


CHIP_SUM_STEPS = 2


def _chip_sum(parts, recvs, name):
    n = len(parts)

    def body(*refs):
        for p_ref, r_ref, o_ref in zip(refs[:n], refs[n:2 * n], refs[2 * n:]):
            acc = p_ref[...].astype(F32)
            for j in range(3):
                acc = acc + r_ref[j].astype(F32)
            o_ref[...] = acc

    rows = [p.shape[1] // CHIP_SUM_STEPS for p in parts]
    return _pallas_call(
        body, name=name, grid=(CHIP_SUM_STEPS,),
        in_specs=[pl.BlockSpec((None, t, p.shape[2]), lambda i: (_chip(), i, 0)) for p, t in zip(parts, rows)]
        + [pl.BlockSpec((3, t, p.shape[2]), lambda i: (0, i, 0)) for p, t in zip(parts, rows)],
        out_specs=[pl.BlockSpec((t, p.shape[2]), lambda i: (i, 0)) for p, t in zip(parts, rows)],
        out_shape=[jax.ShapeDtypeStruct(p.shape[1:], F32) for p in parts], compiler_params=_params(("parallel",)),
    )(*parts, *recvs)


def _sum_slices(a, name):
    n, rows, width = a.shape
    tr = _row_tile(rows, width, n + 1)

    def body(a_ref, o_ref):
        acc = a_ref[0].astype(F32)
        for i in range(1, n):
            acc = acc + a_ref[i].astype(F32)
        o_ref[...] = acc

    return _pallas_call(
        body, name=name, grid=(rows // tr,), in_specs=[pl.BlockSpec((n, tr, width), lambda i: (0, i, 0))],
        out_specs=pl.BlockSpec((tr, width), lambda i: (i, 0)), out_shape=jax.ShapeDtypeStruct((rows, width), F32),
        compiler_params=_params(("parallel",)),
    )(a)


def _adamw_update(w, g, m, v):
    nm = ADAM_B1 * m + (1.0 - ADAM_B1) * g
    nv = ADAM_B2 * v + (1.0 - ADAM_B2) * (g * g)
    m_hat = nm / (1.0 - ADAM_B1 ** ADAM_STEP)
    v_hat = nv / (1.0 - ADAM_B2 ** ADAM_STEP)
    return -ADAM_LR * (m_hat / (jnp.sqrt(v_hat) + ADAM_EPS) + ADAM_WD * w), nm, nv


def _adamw(ws, gs, ms, vs, name):
    n = len(ws)

    def body(*refs):
        for k in range(n):
            w_ref, g_ref, m_ref, v_ref = (refs[s * n + k] for s in range(4))
            d_ref, nm_ref, nv_ref = (refs[(4 + s) * n + k] for s in range(3))
            d_ref[...], nm_ref[...], nv_ref[...] = _adamw_update(w_ref[...], g_ref[...], m_ref[...], v_ref[...])

    whole = [pl.BlockSpec(w.shape, lambda i: (0, 0)) for w in ws]
    outs = _pallas_call(
        body, name=name, grid=(1,), in_specs=whole * 4, out_specs=whole * 3,
        out_shape=[jax.ShapeDtypeStruct(w.shape, F32) for _ in range(3) for w in ws],
        compiler_params=_params(("arbitrary",)),
    )(*ws, *gs, *ms, *vs)
    return [[outs[s * n + k] for s in range(3)] for k in range(n)]


ADAMW_STEPS = 4


def _adamw_halves(ws, ms, vs, mine, other, name):
    n = len(ws)
    depth = ws[0].shape[0]
    assert depth == 2
    halves = [(w.shape[1] // 2, w.shape[2]) for w in ws]
    tiles = [hr // ADAMW_STEPS for hr, _ in halves]
    kinds = ((0, True), (0, False), (1, True), (1, False))

    def active(l, h, layer, own):
        mine_half = h == _core()
        return (l == layer) & (mine_half if own else jnp.logical_not(mine_half))

    def body(*refs):
        l, h = pl.program_id(0), pl.program_id(1)
        flags = [active(l, h, layer, own) for layer, own in kinds]
        for k in range(n):
            w_ref, m_ref, v_ref = refs[k], refs[n + k], refs[2 * n + k]
            g_refs = [refs[(3 + s) * n + k] for s in range(4)]
            go_ref, d_ref, nm_ref, nv_ref = (refs[(7 + s) * n + k] for s in range(4))
            for flag, g_ref in zip(flags, g_refs):
                @pl.when(flag)
                def _():
                    gv = g_ref[...]
                    go_ref[...] = gv
                    d_ref[...], nm_ref[...], nv_ref[...] = _adamw_update(w_ref[...], gv, m_ref[...], v_ref[...])

    def blk(k):
        return pl.BlockSpec((None, None, tiles[k], halves[k][1]), lambda l, h, i: (l, h, i, 0))

    def gspec(k, layer, own):
        return pl.BlockSpec((tiles[k], halves[k][1]), lambda l, h, i: (jnp.where(active(l, h, layer, own), i, 0), 0))

    def view(a, k):
        return a.reshape(depth, 2, halves[k][0], halves[k][1])

    blks = [blk(k) for k in range(n)]
    sources = [[(mine if own else other)[layer][k] for k in range(n)] for layer, own in kinds]
    outs = _pallas_call(
        body, name=name, grid=(depth, 2, ADAMW_STEPS),
        in_specs=blks * 3 + [gspec(k, layer, own) for layer, own in kinds for k in range(n)], out_specs=blks * 4,
        out_shape=[jax.ShapeDtypeStruct((depth, 2) + halves[k], F32) for _ in range(4) for k in range(n)],
        compiler_params=_params(("parallel", "parallel", "parallel")),
    )(*[view(a, k) for group in (ws, ms, vs) for k, a in enumerate(group)], *[g for src in sources for g in src])
    return [[outs[s * n + k].reshape(ws[k].shape) for s in range(4)] for k in range(n)]


BIG = ("w_in", "w_pool_up", "w_conv_out", "w_attn_up", "w_o", "w_ff1", "w_ff2")
SMALL = ("norm_mix", "b_gate", "pool_mix", "pool_scale", "conv_w", "q_gain", "k_gain", "norm_mlp")
ORDER = ("norm_mix", "w_in", "b_gate", "pool_mix", "pool_scale", "conv_w", "q_gain", "k_gain", "w_pool_up",
         "w_conv_out", "w_attn_up", "w_o", "norm_mlp", "w_ff1", "w_ff2")
COLUMN_SHARDED = ("w_pool_up", "w_conv_out", "w_attn_up", "w_ff1")


def _matrix_weights(gathered):
    w = {}
    for name, g4 in gathered.items():
        if name in COLUMN_SHARDED:
            w[name] = g4
        else:
            w[name] = g4.reshape(N_CHIPS * g4.shape[1], g4.shape[2])
    return w


def _small_weights(l, small):
    w = {}
    w["norm_mix"] = small["norm_mix"][l][None]
    w["norm_mlp"] = small["norm_mlp"][l][None]
    w["b_gate"] = small["b_gate"][l][None]
    w["pool_mix"] = small["pool_mix"][l].astype(BF)
    w["pool_scale"] = small["pool_scale"][l][None]
    w["conv_w"] = jnp.pad(small["conv_w_full"][l], ((0, 5), (0, 0)))
    w["qk_gain"] = jnp.pad(jnp.stack([jnp.tile(small["q_gain"][l], 2), jnp.tile(small["k_gain"][l], 2)]), ((0, 6), (0, 0)))
    return w


def _to_chip_major(name, g):
    if name == "w_in":
        return g.T.reshape(N_CHIPS, g.shape[1] // N_CHIPS, g.shape[0])
    if name in COLUMN_SHARDED:
        return g
    return g.reshape(N_CHIPS, g.shape[0] // N_CHIPS, g.shape[1])


def _pad8(a):
    a = a.reshape(-1)
    return jnp.pad(a, (0, (-a.size) % (8 * LANES))).reshape(-1, LANES)


def kernel(x, norm_mix, w_in, b_gate, pool_mix, pool_scale, conv_w, q_gain, k_gain, w_pool_up, w_conv_out, w_attn_up, w_o, norm_mlp, w_ff1, w_ff2, loss_target, m_norm_mix, m_w_in, m_b_gate, m_pool_mix, m_pool_scale, m_conv_w, m_q_gain, m_k_gain, m_w_pool_up, m_w_conv_out, m_w_attn_up, m_w_o, m_norm_mlp, m_w_ff1, m_w_ff2, v_norm_mix, v_w_in, v_b_gate, v_pool_mix, v_pool_scale, v_conv_w, v_q_gain, v_k_gain, v_w_pool_up, v_w_conv_out, v_w_attn_up, v_w_o, v_norm_mlp, v_w_ff1, v_w_ff2):
    weights = dict(norm_mix=norm_mix, w_in=w_in, b_gate=b_gate, pool_mix=pool_mix, pool_scale=pool_scale, conv_w=conv_w,
                   q_gain=q_gain, k_gain=k_gain, w_pool_up=w_pool_up, w_conv_out=w_conv_out, w_attn_up=w_attn_up,
                   w_o=w_o, norm_mlp=norm_mlp, w_ff1=w_ff1, w_ff2=w_ff2)
    moms = dict(norm_mix=m_norm_mix, w_in=m_w_in, b_gate=m_b_gate, pool_mix=m_pool_mix, pool_scale=m_pool_scale,
                conv_w=m_conv_w, q_gain=m_q_gain, k_gain=m_k_gain, w_pool_up=m_w_pool_up, w_conv_out=m_w_conv_out,
                w_attn_up=m_w_attn_up, w_o=m_w_o, norm_mlp=m_norm_mlp, w_ff1=m_w_ff1, w_ff2=m_w_ff2)
    vels = dict(norm_mix=v_norm_mix, w_in=v_w_in, b_gate=v_b_gate, pool_mix=v_pool_mix, pool_scale=v_pool_scale,
                conv_w=v_conv_w, q_gain=v_q_gain, k_gain=v_k_gain, w_pool_up=v_w_pool_up, w_conv_out=v_w_conv_out,
                w_attn_up=v_w_attn_up, w_o=v_w_o, norm_mlp=v_norm_mlp, w_ff1=v_w_ff1, w_ff2=v_w_ff2)
    depth = norm_mix.shape[0]
    q = 2 * lax.axis_index("x") + lax.axis_index("y")
    for group in (weights, moms, vels):
        group["w_in"] = jnp.swapaxes(group["w_in"], 1, 2)

    assert depth == 2, "the second layer's gather hides behind the first layer's forward, and likewise backward"
    first, rest = BIG[:1], BIG[1:]
    cw_all = _all_to_all_small(_pad8(conv_w))
    bufs = [{n: _cast_place(weights[n], 0, f"cast_{n}_l0") for n in first}]
    a_ssem, a_rsem, a_views, a_token = _gather_start([bufs[0][n] for n in first], "gather_start_l0_in", cw_all)
    bufs[0].update({n: _cast_place(weights[n], 0, f"cast_{n}_l0") for n in rest})
    bufs += [{n: _cast_place(weights[n], l, f"cast_{n}_l{l}") for n in BIG} for l in range(1, depth)]
    conv_w_full = jnp.concatenate(
        [cw_all[2 * p].reshape(-1)[:conv_w.size].reshape(conv_w.shape) for p in range(N_CHIPS)], axis=-1)
    small = dict(weights)
    small["conv_w_full"] = conv_w_full

    later = {}

    def soon_weights(t):
        got = _gather_finish(a_ssem, a_rsem, a_views, t, "gather_wait_l0_in", "gather_forward_l0_in",
                             [bufs[0][n].shape for n in first])
        later["rest"] = _gather_start([bufs[0][n] for n in rest], "gather_start_l0_rest", got[0])
        later["l1"] = _gather_start([bufs[1][n] for n in BIG], "gather_start_l1", later["rest"][3])
        return dict(_matrix_weights(dict(zip(first, got))), started=[later["rest"][3], later["l1"][3]])

    def late_weights(t):
        b_ssem, b_rsem, b_views, _ = later["rest"]
        got = _gather_finish(b_ssem, b_rsem, b_views, t, "gather_wait_l0_rest", "gather_forward_l0_rest",
                             [bufs[0][n].shape for n in rest])
        return _matrix_weights(dict(zip(rest, got)))

    wl, saved = [None] * depth, [None] * depth
    small_1 = _small_weights(1, small)
    (h, hb_1), saved[0], wl[0] = _layer_fwd(x[0], _small_weights(0, small), "l0", after=a_token, soon=soon_weights,
                                            late=late_weights, next_gain=small_1["norm_mix"])
    g_ssem, g_rsem, g_views, _ = later["l1"]
    got = _gather_finish(g_ssem, g_rsem, g_views, h, "gather_wait_l1", "gather_forward_l1",
                         [bufs[1][n].shape for n in BIG])
    (dh, loss_row), saved[1], wl[1] = _layer_fwd(
        h, dict(small_1, **_matrix_weights(dict(zip(BIG, got)))), "l1", target=loss_target[0], hb=hb_1)

    def pair_stage(names, g, tag):
        views = [_halves(_to_chip_major(n, g[n])) for n in names]
        from_sibling = _pair_swap(views, f"grad_pair_swap_{tag}")
        return _pair_sum(views, from_sibling, f"pair_sum_{tag}")

    mine, other = [{}, {}], [{}, {}]

    def finish(names, l, started, after, tag):
        ssem, rsem, parts, landing, _ = started
        parts, arrived = _chip_exchange_wait(ssem, rsem, parts, landing, after, f"grad_chip_exchange_wait_{tag}")
        got = _chip_sum(parts, arrived, f"chip_sum_{tag}")
        mine[l].update(zip(names, got))
        other[l].update(zip(names, _pair_send(got, f"grad_pair_send_{tag}")))

    def small_pieces(g):
        return [_pad8(g[n][:3] if n == "conv_w" else g[n]) for n in SMALL]

    def start_small(l):
        pieces = small_pieces(grads[l]) + ([_pad8(loss_row)] if l == depth - 1 else [])
        return _all_to_all_small_start(jnp.concatenate(pieces, axis=0), f"small_grad_exchange_start_l{l}")

    grads, early, small = [None] * depth, {}, [None] * depth
    dh, grads[1] = _layer_bwd(dh, wl[1], saved[1], "l1")
    second = _chip_exchange_start(pair_stage(BIG, grads[1], "l1"), "grad_chip_exchange_start_l1")
    small[1] = start_small(1)

    def start_rest(g):
        early["rest"] = _chip_exchange_start(pair_stage(rest, g, "l0_rest"), "grad_chip_exchange_start_l0_rest")
        return early["rest"][4]

    def start_last(g):
        early["in"] = _chip_exchange_start(pair_stage(first, g, "l0_in"), "grad_chip_exchange_start_l0_in")
        return early["in"][4]

    dh, grads[0] = _layer_bwd(dh, wl[0], saved[0], "l0", after=[second[4], small[1][4]], mid=start_rest,
                              tail=start_last)
    small[0] = start_small(0)
    started = small[0][4]
    finish(BIG, 1, second, started, "l1")
    finish(rest, 0, early["rest"], started, "l0_rest")
    full = {}

    deltas, new_m, new_v = {}, {}, {}

    def update_matrices(names, tag):
        results = _adamw_halves(
            [weights[n] for n in names], [moms[n] for n in names], [vels[n] for n in names],
            [[mine[l][n] for n in names] for l in range(depth)], [[other[l][n] for n in names] for l in range(depth)],
            f"adamw_{tag}")
        for n, (g_, d_, m_, v_) in zip(names, results):
            full[n], deltas[n], new_m[n], new_v[n] = g_, d_, m_, v_

    update_matrices(rest, "rest")
    finish(first, 0, early["in"], deltas[rest[-1]], "l0_in")
    update_matrices(first, "in")
    summed = []
    for l in range(depth):
        ssem, rsem, part, landing, _ = small[l]
        summed.append(_sum_slices(_all_to_all_small_wait(ssem, rsem, part, landing, deltas[first[-1]],
                                                         f"small_grad_exchange_wait_l{l}"), f"small_sum_l{l}"))
    row = 0
    for n, piece in zip(SMALL, small_pieces(grads[0])):
        size = (weights[n].size if n != "conv_w" else depth * 3 * 512) // depth
        flat = jnp.stack([s[row:row + piece.shape[0]].reshape(-1)[:size] for s in summed])
        row += piece.shape[0]
        if n == "conv_w":
            full[n] = lax.dynamic_slice_in_dim(flat.reshape(depth, 3, 512), q * conv_w.shape[2], conv_w.shape[2], axis=2)
        else:
            full[n] = flat.reshape(weights[n].shape)
    loss = summed[depth - 1][row, 0]
    two_d = {n: (-1, weights[n].shape[-1]) if n not in ("conv_w", "q_gain", "k_gain") else (1, -1) for n in SMALL}
    results = _adamw(*[[group[n].reshape(two_d[n]) for n in SMALL] for group in (weights, full, moms, vels)],
                     "adamw_small")
    for n, (d2, m2, v2) in zip(SMALL, results):
        shape = weights[n].shape
        deltas[n], new_m[n], new_v[n] = d2.reshape(shape), m2.reshape(shape), v2.reshape(shape)
        full[n] = full[n].reshape(shape)
    for group in (full, deltas, new_m, new_v):
        group["w_in"] = jnp.swapaxes(group["w_in"], 1, 2)
    return (loss, dh[None], *[full[n] for n in ORDER], *[deltas[n] for n in ORDER], *[new_m[n] for n in ORDER],
            *[new_v[n] for n in ORDER])
```

```python
import functools

import jax
import jax.numpy as jnp
from jax import lax
from jax.experimental import pallas as pl
from jax.experimental.pallas import tpu as pltpu

F32 = jnp.float32
BF = jnp.bfloat16
MESH_ID = pl.DeviceIdType.MESH
ANY = pl.BlockSpec(memory_space=pl.ANY)

EPS = 1e-6
MASK_VALUE = -1e30
POOL_WINDOWS = (2, 4, 8, 16)
ATTN_DILATIONS = (1, 4, 16)
ATTN_BLOCK = 128
HEAD_DIM = 64
OFF_Q, OFF_K, OFF_V, OFF_GATE = 2048, 2816, 3584, 4352
N_CHIPS = 4
ADAM_LR, ADAM_B1, ADAM_B2, ADAM_EPS, ADAM_WD, ADAM_STEP = 0.001, 0.9, 0.999, 1e-08, 0.01, 10

VMEM_LIMIT = 48 * 1024 * 1024
LANES = 128

_DIMS = {"nn": (((1,), (0,)), ((), ())), "nt": (((1,), (1,)), ((), ())), "tn": (((0,), (0,)), ((), ()))}


def _params(sem):
    return pltpu.CompilerParams(dimension_semantics=sem, vmem_limit_bytes=VMEM_LIMIT)


def _pallas_call(body, **kw):
    def in_hbm(s):
        pin = isinstance(s, jax.ShapeDtypeStruct) and s is not TOKEN and jnp.issubdtype(s.dtype, jnp.floating)
        return pltpu.HBM(s.shape, s.dtype) if pin else s

    out_shape = kw.pop("out_shape")
    kw["out_shape"] = [in_hbm(s) for s in out_shape] if isinstance(out_shape, (list, tuple)) else in_hbm(out_shape)
    call = pl.pallas_call(body, **kw)

    def run(*args):
        pinned = [pltpu.with_memory_space_constraint(a, pltpu.HBM)
                  if hasattr(a, "dtype") and jnp.issubdtype(a.dtype, jnp.floating) else a for a in args]
        return call(*pinned)

    return run


def _dot(a, b, mode="nn"):
    return lax.dot_general(a, b, _DIMS[mode], preferred_element_type=F32)


def _mm(a, b, mode, name, *, tm, tn, tk, out_dtype=F32, res=None, aux=None, epi=None, n_outer=False,
        b_shards=False, out_shards=False, after=None, vec=None):
    if mode == "tn":
        K, M = a.shape
    else:
        M, K = a.shape
    if b_shards:
        if mode == "nn":
            assert b.shape[1] == K
            N = b.shape[2] * N_CHIPS
        else:
            assert mode == "nt"
            N = b.shape[1]
            assert b.shape[2] * N_CHIPS == K
    else:
        N = b.shape[0] if mode == "nt" else b.shape[1]
    tm, tn, tk = min(tm, M), min(tn, N), min(tk, K)
    assert M % tm == 0 and N % tn == 0 and K % tk == 0
    nk = K // tk
    if n_outer:
        grid = (N // tn, M // tm, nk)
        ij = lambda p, q_: (q_, p)
    else:
        grid = (M // tm, N // tn, nk)
        ij = lambda p, q_: (p, q_)

    def amap(p, q_, k):
        i, j = ij(p, q_)
        return (k, i) if mode == "tn" else (i, k)

    a_spec = pl.BlockSpec((tk, tm) if mode == "tn" else (tm, tk), amap)
    if b_shards:
        if mode == "nn":
            per = (N // N_CHIPS) // tn
            assert per >= 1 and (N // N_CHIPS) % tn == 0

            def bmap(p, q_, k):
                i, j = ij(p, q_)
                return (j // per, k, j % per)

            b_spec = pl.BlockSpec((None, tk, tn), bmap)
        else:
            per = (K // N_CHIPS) // tk
            assert per >= 1 and (K // N_CHIPS) % tk == 0

            def bmap(p, q_, k):
                i, j = ij(p, q_)
                return (k // per, j, k % per)

            b_spec = pl.BlockSpec((None, tn, tk), bmap)
    else:
        def bmap(p, q_, k):
            i, j = ij(p, q_)
            return (j, k) if mode == "nt" else (k, j)

        b_spec = pl.BlockSpec((tn, tk) if mode == "nt" else (tk, tn), bmap)

    def omap(p, q_, k):
        return ij(p, q_)

    o_spec = pl.BlockSpec((tm, tn), omap)
    if out_shards:
        per_o = (N // N_CHIPS) // tn
        assert per_o >= 1 and (N // N_CHIPS) % tn == 0

        def osmap(p, q_, k):
            i, j = ij(p, q_)
            return (j // per_o, i, j % per_o)

        out_spec0 = pl.BlockSpec((None, tm, tn), osmap)
        out_shape0 = jax.ShapeDtypeStruct((N_CHIPS, M, N // N_CHIPS), out_dtype)
    else:
        out_spec0 = o_spec
        out_shape0 = jax.ShapeDtypeStruct((M, N), out_dtype)

    in_specs = [a_spec, b_spec]
    args = [a, b]
    if res is not None:
        in_specs.append(o_spec)
        args.append(res)
    if aux is not None:
        in_specs.append(o_spec)
        args.append(aux)
    if vec is not None:
        in_specs.append(pl.BlockSpec((1, tn), lambda p, q_, k: (0, ij(p, q_)[1])))
        args.append(vec)
    after = [] if after is None else list(after) if isinstance(after, (list, tuple)) else [after]
    in_specs += [ANY] * len(after)
    args += after
    out_specs = [out_spec0]
    out_shape = [out_shape0]
    reduces = epi in ("loss", "rms_bwd")
    if reduces:
        assert tn == N and not n_outer and not out_shards
        width = LANES if epi == "loss" else N
        out_specs.append(pl.BlockSpec((1, width), lambda p, q_, k: (0, 0)))
        out_shape.append(jax.ShapeDtypeStruct((1, width), F32))
    if epi == "rms_next":
        assert tn == N and not out_shards
        out_specs.append(o_spec)
        out_shape.append(jax.ShapeDtypeStruct((M, N), BF))
    n_out = len(out_shape)
    has_res, has_aux, has_vec, n_after = res is not None, aux is not None, vec is not None, len(after)

    def body(*refs):
        a_ref, b_ref = refs[0], refs[1]
        pos = 2
        res_ref = aux_ref = vec_ref = None
        if has_res:
            res_ref = refs[pos]
            pos += 1
        if has_aux:
            aux_ref = refs[pos]
            pos += 1
        if has_vec:
            vec_ref = refs[pos]
            pos += 1
        pos += n_after
        outs = refs[pos:pos + n_out]
        part = _dot(a_ref[...].astype(BF), b_ref[...].astype(BF), mode)

        first_row_tile = pl.program_id(0) == 0

        def add_to_sum(row):
            @pl.when(first_row_tile)
            def _():
                outs[1][...] = jnp.zeros_like(outs[1])

            outs[1][...] += row

        def finish(acc):
            if epi == "rms_bwd":
                xv = aux_ref[...]
                r = lax.rsqrt(jnp.mean(xv * xv, axis=-1, keepdims=True) + EPS)
                xhat = xv * r
                dy = acc * vec_ref[...]
                outs[0][...] = res_ref[...] + r * (dy - xhat * jnp.mean(dy * xhat, axis=-1, keepdims=True))
                add_to_sum(jnp.sum(acc * xhat, axis=0, keepdims=True))
                return
            if res_ref is not None:
                acc = res_ref[...] + acc
            if epi == "relu2":
                r = jnp.maximum(acc, 0.0)
                outs[0][...] = (r * r).astype(out_dtype)
            elif epi == "drelu2":
                outs[0][...] = (acc.astype(BF) * (2.0 * jnp.sqrt(aux_ref[...]))).astype(out_dtype)
            elif epi == "rms_next":
                outs[0][...] = acc
                r = lax.rsqrt(jnp.mean(acc * acc, axis=-1, keepdims=True) + EPS)
                outs[1][...] = ((acc * r) * vec_ref[...]).astype(BF)
            elif epi == "loss":
                e = acc - aux_ref[...]
                outs[0][...] = e / float(N)
                add_to_sum(0.5 * jnp.sum(jnp.mean(e * e, axis=-1, keepdims=True)))
            else:
                outs[0][...] = acc.astype(out_dtype)

        if nk == 1:
            finish(part)
        else:
            acc_ref = refs[pos + n_out]
            k = pl.program_id(2)

            @pl.when(k == 0)
            def _():
                acc_ref[...] = part

            @pl.when(k > 0)
            def _():
                acc_ref[...] += part

            @pl.when(k == nk - 1)
            def _():
                finish(acc_ref[...])

    scratch = [pltpu.VMEM((tm, tn), F32)] if nk > 1 else []
    out = _pallas_call(
        body, name=name, grid=grid, in_specs=in_specs, out_specs=out_specs, out_shape=out_shape,
        scratch_shapes=scratch,
        compiler_params=_params(("arbitrary" if reduces else "parallel", "parallel", "arbitrary")),
    )(*args)
    return out if n_out > 1 else out[0]


def _rms_fwd(x, gain, name, after=None):
    T, D = x.shape
    tm = min(512, T)

    def body(x_ref, g_ref, *rest):
        o_ref = rest[-1]
        xv = x_ref[...]
        r = lax.rsqrt(jnp.mean(xv * xv, axis=-1, keepdims=True) + EPS)
        o_ref[...] = ((xv * r) * g_ref[...]).astype(BF)

    extra = [] if after is None else list(after) if isinstance(after, (list, tuple)) else [after]
    return _pallas_call(
        body, name=name, grid=(T // tm,),
        in_specs=[pl.BlockSpec((tm, D), lambda i: (i, 0)), pl.BlockSpec((1, D), lambda i: (0, 0))] + [ANY] * len(extra),
        out_specs=pl.BlockSpec((tm, D), lambda i: (i, 0)), out_shape=jax.ShapeDtypeStruct((T, D), BF),
        compiler_params=_params(("parallel",)),
    )(x, gain, *extra)


def _rms_bwd(dh, x, gain, dres, name):
    T, D = x.shape
    tm = min(512, T)

    def body(dh_ref, x_ref, g_ref, dres_ref, dx_ref, dg_ref):
        xv = x_ref[...]
        r = lax.rsqrt(jnp.mean(xv * xv, axis=-1, keepdims=True) + EPS)
        xhat = xv * r
        dhv = dh_ref[...]
        dy = dhv * g_ref[...]
        dx_ref[...] = dres_ref[...] + r * (dy - xhat * jnp.mean(dy * xhat, axis=-1, keepdims=True))

        @pl.when(pl.program_id(0) == 0)
        def _():
            dg_ref[...] = jnp.zeros_like(dg_ref)

        dg_ref[...] += jnp.sum(dhv * xhat, axis=0, keepdims=True)

    row = pl.BlockSpec((tm, D), lambda i: (i, 0))
    vec = pl.BlockSpec((1, D), lambda i: (0, 0))
    return _pallas_call(
        body, name=name, grid=(T // tm,), in_specs=[row, row, vec, row], out_specs=[row, vec],
        out_shape=[jax.ShapeDtypeStruct((T, D), F32), jax.ShapeDtypeStruct((1, D), F32)],
        compiler_params=_params(("arbitrary",)),
    )(dh, x, gain, dres)


POOL_HALO = 16
CONV_HALO = 8
POOLCONV_ROWS = 512


def _causal_window_sum(v, w):
    s, sh = v, 1
    while sh < w:
        s = s + pltpu.roll(s, sh, 0)
        sh *= 2
    return s


def _anticausal_window_sum(v, w):
    n = v.shape[0]
    s, sh = v, 1
    while sh < w:
        s = s + pltpu.roll(s, n - sh, 0)
        sh *= 2
    return s


def _poolconv_fwd(z, pmix_b, pscale, convw, name):
    T = z.shape[0]
    R = min(POOLCONV_ROWS, T)
    PH, CH = R // POOL_HALO, R // CONV_HALO

    def body(u_ref, uh_ref, b_ref, c_ref, ch_ref, x_ref, xh_ref, mix_ref, sc_ref, cw_ref, yp_ref, yc_ref):
        i = pl.program_id(0)
        keep = (i > 0).astype(F32)
        row = i * R + lax.broadcasted_iota(jnp.int32, (R, 1), 0)
        w_all = jnp.concatenate([uh_ref[...] * keep, u_ref[...]], axis=0)
        for g, w in enumerate(POOL_WINDOWS):
            cols = slice(128 * g, 128 * (g + 1))
            wg = w_all[:, cols]
            s = _causal_window_sum(wg, w)[POOL_HALO:]
            inv_cnt = 1.0 / jnp.minimum(row + 1, w).astype(F32)
            dgrp = s * inv_cnt - wg[POOL_HALO:]
            y = _dot(dgrp.astype(BF), mix_ref[g]) * sc_ref[:, cols]
            yp_ref[:, cols] = y.astype(BF)
        uc = jnp.concatenate([ch_ref[...] * xh_ref[...] * keep, c_ref[...] * x_ref[...]], axis=0)
        yc = cw_ref[2:3, :] * uc + cw_ref[0:1, :] * pltpu.roll(uc, 2, 0) + cw_ref[1:2, :] * pltpu.roll(uc, 1, 0)
        yc_ref[...] = (b_ref[...] * yc[CONV_HALO:]).astype(BF)

    def main(cb):
        return pl.BlockSpec((R, 512), lambda i: (i, cb))

    def prev(cb, halo, per):
        return pl.BlockSpec((halo, 512), lambda i: (jnp.maximum(i * per - 1, 0), cb))

    full = lambda a: pl.BlockSpec(a.shape, lambda i: (0,) * a.ndim)
    return _pallas_call(
        body, name=name, grid=(T // R,),
        in_specs=[main(0), prev(0, POOL_HALO, PH), main(1), main(2), prev(2, CONV_HALO, CH), main(3),
                  prev(3, CONV_HALO, CH), full(pmix_b), full(pscale), full(convw)],
        out_specs=[pl.BlockSpec((R, 512), lambda i: (i, 0))] * 2,
        out_shape=[jax.ShapeDtypeStruct((T, 512), BF)] * 2,
        compiler_params=_params(("parallel",)),
    )(z, z, z, z, z, z, z, pmix_b, pscale, convw)


def _poolconv_bwd(z, dyp, dyc, pmix_b, pscale, convw, dz, name):
    T = z.shape[0]
    R = min(POOLCONV_ROWS, T)
    PH, CH = R // POOL_HALO, R // CONV_HALO
    nsteps = T // R

    def body(u_ref, uh_ref, b_ref, bn_ref, c_ref, ch_ref, x_ref, xh_ref, dyp_ref, dypn_ref, dyc_ref, dycn_ref,
             mix_ref, sc_ref, cw_ref, dz_in_ref, dz_ref, dmix_ref, dsc_ref, dcw_ref):
        i = pl.program_id(0)
        keep_prev = (i > 0).astype(F32)
        keep_next = (i < nsteps - 1).astype(F32)

        @pl.when(i == 0)
        def _():
            dmix_ref[...] = jnp.zeros_like(dmix_ref)
            dsc_ref[...] = jnp.zeros_like(dsc_ref)
            dcw_ref[...] = jnp.zeros_like(dcw_ref)

        row = i * R + lax.broadcasted_iota(jnp.int32, (R, 1), 0)
        row_ext = i * R + lax.broadcasted_iota(jnp.int32, (R + POOL_HALO, 1), 0)
        w_all = jnp.concatenate([uh_ref[...] * keep_prev, u_ref[...]], axis=0)
        dyp_ext = jnp.concatenate([dyp_ref[...], dypn_ref[...] * keep_next], axis=0)
        for g, w in enumerate(POOL_WINDOWS):
            cols = slice(128 * g, 128 * (g + 1))
            wg = w_all[:, cols]
            s = _causal_window_sum(wg, w)[POOL_HALO:]
            inv_cnt = 1.0 / jnp.minimum(row + 1, w).astype(F32)
            dgrp = (s * inv_cnt - wg[POOL_HALO:]).astype(BF)
            y_pre = _dot(dgrp, mix_ref[g])
            dsc_ref[:, cols] += jnp.sum(dyp_ref[:, cols] * y_pre, axis=0, keepdims=True)
            dyb = (dyp_ext[:, cols] * sc_ref[:, cols]).astype(BF)
            dmix_ref[cols, :] += _dot(dgrp, dyb[:R], "tn")
            dd = _dot(dyb, mix_ref[g], "nt")
            inv_cnt_ext = 1.0 / jnp.minimum(row_ext + 1, w).astype(F32)
            e = _anticausal_window_sum(dd * inv_cnt_ext, w)
            dz_ref[:, cols] = (e[:R] - dd[:R]).astype(BF)
        cw0, cw1, cw2 = cw_ref[0:1, :], cw_ref[1:2, :], cw_ref[2:3, :]
        uc = jnp.concatenate([ch_ref[...] * xh_ref[...] * keep_prev, c_ref[...] * x_ref[...]], axis=0)
        uc1 = pltpu.roll(uc, 1, 0)[CONV_HALO:]
        uc2 = pltpu.roll(uc, 2, 0)[CONV_HALO:]
        uc0 = uc[CONV_HALO:]
        yc = cw2 * uc0 + cw0 * uc2 + cw1 * uc1
        dycv = dyc_ref[...]
        dz_ref[:, 512:1024] = (dycv * yc).astype(BF)
        dv_ext = jnp.concatenate([dycv * b_ref[...], dycn_ref[...] * bn_ref[...] * keep_next], axis=0)
        n_ext = R + CONV_HALO
        duc = (cw2 * dv_ext + cw1 * pltpu.roll(dv_ext, n_ext - 1, 0) + cw0 * pltpu.roll(dv_ext, n_ext - 2, 0))[:R]
        dv = dv_ext[:R]
        dcw_ref[0:1, :] += jnp.sum(dv * uc2, axis=0, keepdims=True)
        dcw_ref[1:2, :] += jnp.sum(dv * uc1, axis=0, keepdims=True)
        dcw_ref[2:3, :] += jnp.sum(dv * uc0, axis=0, keepdims=True)
        dz_ref[:, 1024:1536] = (duc * x_ref[...]).astype(BF)
        dz_ref[:, 1536:2048] = (duc * c_ref[...]).astype(BF)

    def main(cb):
        return pl.BlockSpec((R, 512), lambda i: (i, cb))

    def prev(cb, halo, per):
        return pl.BlockSpec((halo, 512), lambda i: (jnp.maximum(i * per - 1, 0), cb))

    def nxt(cb, halo, per):
        return pl.BlockSpec((halo, 512), lambda i: (jnp.minimum((i + 1) * per, T // halo - 1), cb))

    full = lambda a: pl.BlockSpec(a.shape, lambda i: (0,) * a.ndim)
    return _pallas_call(
        body, name=name, grid=(nsteps,),
        in_specs=[main(0), prev(0, POOL_HALO, PH), main(1), nxt(1, CONV_HALO, CH), main(2), prev(2, CONV_HALO, CH),
                  main(3), prev(3, CONV_HALO, CH), main(0), nxt(0, POOL_HALO, PH), main(0), nxt(0, CONV_HALO, CH),
                  full(pmix_b), full(pscale), full(convw), ANY],
        out_specs=[pl.BlockSpec((R, 2048), lambda i: (i, 0)), pl.BlockSpec((512, 128), lambda i: (0, 0)),
                   pl.BlockSpec((1, 512), lambda i: (0, 0)), pl.BlockSpec((8, 512), lambda i: (0, 0))],
        out_shape=[jax.ShapeDtypeStruct(dz.shape, BF), jax.ShapeDtypeStruct((512, 128), F32),
                   jax.ShapeDtypeStruct((1, 512), F32), jax.ShapeDtypeStruct((8, 512), F32)],
        input_output_aliases={15: 0}, compiler_params=_params(("arbitrary",)),
    )(z, z, z, z, z, z, z, z, dyp, dyp, dyc, dyc, pmix_b, pscale, convw, dz)


def _head_sums(v):
    row = lax.broadcasted_iota(jnp.int32, (LANES, LANES), 0) < HEAD_DIM
    col = lax.broadcasted_iota(jnp.int32, (LANES, LANES), 1) < HEAD_DIM
    same_head = jnp.where(jnp.logical_xor(row, col), 0.0, 1.0).astype(BF)
    hi = v.astype(BF)
    lo = (v - hi.astype(F32)).astype(BF)
    return _dot(hi, same_head) + _dot(lo, same_head)


def _head_norm(x, g2, ma):
    r = lax.rsqrt(_head_sums(x * x) / HEAD_DIM + EPS)
    return x * r, r


def _head_norm_bwd(dy, xhat, r, g2, ma):
    dxh = dy * g2
    return r * (dxh - xhat * (_head_sums(dxh * xhat) / HEAD_DIM))


def _attn_masks(other_block_exists):
    lane = lax.broadcasted_iota(jnp.int32, (2 * ATTN_BLOCK, ATTN_BLOCK), 1)
    qi = lax.broadcasted_iota(jnp.int32, (2 * ATTN_BLOCK, ATTN_BLOCK), 0) & (ATTN_BLOCK - 1)
    never = (1 - other_block_exists.astype(jnp.int32)) * (2 * ATTN_BLOCK)
    return lane[:ATTN_BLOCK] < HEAD_DIM, lane <= qi, lane >= qi + never


def _stack_heads(x, ma):
    return jnp.concatenate([jnp.where(ma, x, 0.0), jnp.where(ma, 0.0, x)], axis=0)


def _unstack_heads(y, ma):
    return jnp.where(ma, y[:ATTN_BLOCK], y[ATTN_BLOCK:])


def _stack_cols(tile, ma):
    return jnp.concatenate([tile[:, 0:1], tile[:, HEAD_DIM:HEAD_DIM + 1]], axis=0)


QKV_TILES = (OFF_GATE - OFF_Q) // LANES
KIND_TILES = QKV_TILES // 3


def _qk_norm(z, gains, name):
    T = z.shape[0]
    tm = min(512, T)

    def body(x_ref, g_ref, o_ref):
        ma = lax.broadcasted_iota(jnp.int32, (tm, LANES), 1) < HEAD_DIM
        for tile in range(QKV_TILES):
            v = x_ref[:, LANES * tile:LANES * (tile + 1)]
            if tile < 2 * KIND_TILES:
                g = g_ref[0:1, :] if tile < KIND_TILES else g_ref[1:2, :]
                v = _head_norm(v, g, ma)[0] * g
            o_ref[tile] = v

    return _pallas_call(
        body, name=name, grid=(T // tm,),
        in_specs=[pl.BlockSpec((pl.Element(tm), pl.Element(OFF_GATE - OFF_Q)), lambda i: (i * tm, OFF_Q)),
                  pl.BlockSpec((8, LANES), lambda i: (0, 0))],
        out_specs=pl.BlockSpec((QKV_TILES, tm, LANES), lambda i: (0, i, 0)),
        out_shape=jax.ShapeDtypeStruct((QKV_TILES, T, LANES), F32), compiler_params=_params(("parallel",)),
    )(z, gains)


ATTN_STEP_ROWS = 2048
ATTN_UNROLL = 4


def _attn_steps(T):
    assert ATTN_STEP_ROWS == ATTN_BLOCK * max(ATTN_DILATIONS) and T % ATTN_STEP_ROWS == 0
    return T // ATTN_STEP_ROWS


def _attn_rows(jj, r, sub, d):
    start = jj * sub + r
    if d == 1:
        return pl.ds(pl.multiple_of(start, ATTN_BLOCK), ATTN_BLOCK)
    return pl.ds(start, ATTN_BLOCK, stride=d)


def _pick(flag, a, b):
    return jnp.where(jnp.full(a.shape, flag.astype(jnp.int32)) > 0, a, b)


def _attn_fwd(qkv, name):
    T = qkv.shape[1]
    nbig = _attn_steps(T)
    scale = HEAD_DIM ** -0.5

    def body(q_ref, kc_ref, kp_ref, vc_ref, vp_ref, o_ref, lse_ref):
        jb = pl.program_id(1)
        for gi, d in enumerate(ATTN_DILATIONS):
            pl.when(pl.program_id(0) == gi)(functools.partial(group, d, jb, q_ref, kc_ref, kp_ref, vc_ref, vp_ref,
                                                              o_ref, lse_ref))

    def group(d, jb, q_ref, kc_ref, kp_ref, vc_ref, vp_ref, o_ref, lse_ref):
        sub, m = ATTN_BLOCK * d, ATTN_STEP_ROWS // (ATTN_BLOCK * d)

        def step(s, carry):
            jj, r = s // d, s % d
            here, before = _attn_rows(jj, r, sub, d), _attn_rows(jnp.maximum(jj - 1, 0), r, sub, d)
            edge = _attn_rows(m - 1, r, sub, d)
            first = jj == 0
            ma, mask_c, mask_p = _attn_masks(jb * m + jj > 0)
            qs = _stack_heads(q_ref[here, :], ma).astype(BF)
            kcb = kc_ref[here, :].astype(BF)
            kpb = _pick(first, kp_ref[edge, :], kc_ref[before, :]).astype(BF)
            vcb = vc_ref[here, :].astype(BF)
            vpb = _pick(first, vp_ref[edge, :], vc_ref[before, :]).astype(BF)
            s_c = jnp.where(mask_c, _dot(qs, kcb, "nt") * scale, MASK_VALUE)
            s_p = jnp.where(mask_p, _dot(qs, kpb, "nt") * scale, MASK_VALUE)
            mx = jnp.maximum(jnp.max(s_c, axis=-1, keepdims=True), jnp.max(s_p, axis=-1, keepdims=True))
            p_c = jnp.exp(s_c - mx)
            p_p = jnp.exp(s_p - mx)
            den = jnp.sum(p_c, axis=-1, keepdims=True) + jnp.sum(p_p, axis=-1, keepdims=True)
            o = (_dot(p_c.astype(BF), vcb) + _dot(p_p.astype(BF), vpb)) / den
            o_ref[here, :] = _unstack_heads(o, ma)
            lse_ref[here, :] = _unstack_heads(jnp.broadcast_to(mx + jnp.log(den), o.shape), ma)
            return carry

        lax.fori_loop(0, m * d, step, 0, unroll=ATTN_UNROLL)

    def cur(kind):
        return pl.BlockSpec((None, ATTN_STEP_ROWS, LANES), lambda g, j, t: (KIND_TILES * kind + 2 * g + t, j, 0))

    def prv(kind):
        return pl.BlockSpec((None, ATTN_STEP_ROWS, LANES),
                            lambda g, j, t: (KIND_TILES * kind + 2 * g + t, jnp.maximum(j - 1, 0), 0))

    out = pl.BlockSpec((ATTN_STEP_ROWS, LANES), lambda g, j, t: (j, 2 * g + t))
    width = 2 * LANES * len(ATTN_DILATIONS)
    return _pallas_call(
        body, name=name, grid=(len(ATTN_DILATIONS), nbig, 2), in_specs=[cur(0), cur(1), prv(1), cur(2), prv(2)],
        out_specs=[out, out], out_shape=[jax.ShapeDtypeStruct((T, width), F32)] * 2,
        compiler_params=_params(("parallel", "parallel", "parallel")),
    )(qkv, qkv, qkv, qkv, qkv)


def _attn_bwd(z, qkv, do, c, lse, gains, name, after=None):
    T = z.shape[0]
    nbig = _attn_steps(T)
    scale = HEAD_DIM ** -0.5
    extra = [] if after is None else [after]

    def body(*refs):
        g, jb = pl.program_id(0), pl.program_id(1)
        dgq_ref, dgk_ref = refs[len(refs) - 5], refs[len(refs) - 4]

        @pl.when((g == 0) & (jb == 0) & (pl.program_id(2) == 0))
        def _():
            dgq_ref[...] = jnp.zeros_like(dgq_ref)
            dgk_ref[...] = jnp.zeros_like(dgk_ref)

        for gi, d in enumerate(ATTN_DILATIONS):
            pl.when(g == gi)(functools.partial(group, d, jb, *refs))

    def group(d, jb, qr_ref, kr_ref, vc_ref, vp_ref, qn_ref, qnn_ref, kn_ref, knp_ref, do_ref, don_ref, c_ref, cn_ref,
              lse_ref, lsen_ref, g_ref, *rest):
        dq_ref, dk_ref, dv_ref, dgq_ref, dgk_ref, sq_ref, sk_ref, sv_ref = rest[len(extra):]
        sub, m = ATTN_BLOCK * d, ATTN_STEP_ROWS // (ATTN_BLOCK * d)
        nb = T // sub
        gq, gk = g_ref[0:1, :], g_ref[1:2, :]

        def step(s, carry):
            jj, r = s // d, s % d
            here = _attn_rows(jj, r, sub, d)
            before = _attn_rows(jnp.maximum(jj - 1, 0), r, sub, d)
            behind = _attn_rows(jnp.minimum(jj + 1, m - 1), r, sub, d)
            edge_before, edge_behind = _attn_rows(m - 1, r, sub, d), _attn_rows(0, r, sub, d)
            first, last = jj == 0, jj == m - 1
            block = jb * m + jj
            ma, mask_c, mask_p = _attn_masks(block > 0)
            mask_n = _attn_masks(block < nb - 1)[2]
            qhat, rq = _head_norm(qr_ref[here, :], gq, ma)
            qn = qn_ref[here, :]
            qn_next = _pick(last, qnn_ref[edge_behind, :], qn_ref[behind, :])
            khat, rk = _head_norm(kr_ref[here, :], gk, ma)
            kcb = kn_ref[here, :].astype(BF)
            kpb = _pick(first, knp_ref[edge_before, :], kn_ref[before, :]).astype(BF)
            vcb = vc_ref[here, :].astype(BF)
            vpb = _pick(first, vp_ref[edge_before, :], vc_ref[before, :]).astype(BF)
            do_t, don_t = do_ref[here, :], _pick(last, don_ref[edge_behind, :], do_ref[behind, :])
            c_t, cn_t = c_ref[here, :], _pick(last, cn_ref[edge_behind, :], c_ref[behind, :])
            lse_t, lsen_t = lse_ref[here, :], _pick(last, lsen_ref[edge_behind, :], lse_ref[behind, :])
            qs, dos = _stack_heads(qn, ma).astype(BF), _stack_heads(do_t, ma).astype(BF)
            lse_s, c_s = _stack_cols(lse_t, ma), _stack_cols(c_t, ma)
            s_c = jnp.where(mask_c, _dot(qs, kcb, "nt") * scale, MASK_VALUE)
            s_p = jnp.where(mask_p, _dot(qs, kpb, "nt") * scale, MASK_VALUE)
            p_c = jnp.exp(s_c - lse_s)
            p_p = jnp.exp(s_p - lse_s)
            ds_c = ((p_c * (_dot(dos, vcb, "nt") + c_s)) * scale).astype(BF)
            ds_p = ((p_p * (_dot(dos, vpb, "nt") + c_s)) * scale).astype(BF)
            dq_t = _unstack_heads(_dot(ds_c, kcb) + _dot(ds_p, kpb), ma)
            qs_n, dos_n = _stack_heads(qn_next, ma).astype(BF), _stack_heads(don_t, ma).astype(BF)
            s_n = jnp.where(mask_n, _dot(qs_n, kcb, "nt") * scale, MASK_VALUE)
            p_n = jnp.exp(s_n - _stack_cols(lsen_t, ma))
            ds_n = ((p_n * (_dot(dos_n, vcb, "nt") + _stack_cols(cn_t, ma))) * scale).astype(BF)
            dv_t = _dot(p_c.astype(BF), dos, "tn") + _dot(p_n.astype(BF), dos_n, "tn")
            dk_t = _dot(ds_c, qs, "tn") + _dot(ds_n, qs_n, "tn")
            sq_ref[here, :] = _head_norm_bwd(dq_t, qhat, rq, gq, ma)
            sk_ref[here, :] = _head_norm_bwd(dk_t, khat, rk, gk, ma)
            sv_ref[here, :] = dv_t
            dgq_ref[...] += jnp.sum(dq_t * qhat, axis=0, keepdims=True)
            dgk_ref[...] += jnp.sum(dk_t * khat, axis=0, keepdims=True)
            return carry

        lax.fori_loop(0, m * d, step, 0, unroll=ATTN_UNROLL)
        dq_ref[...] = sq_ref[...].astype(BF)
        dk_ref[...] = sk_ref[...].astype(BF)
        dv_ref[...] = sv_ref[...].astype(BF)

    rows = ATTN_STEP_ROWS

    def raw(col0):
        return pl.BlockSpec((rows, LANES), lambda g, j, t: (j, col0 + 2 * g + t))

    def cur(kind):
        return pl.BlockSpec((None, rows, LANES), lambda g, j, t: (KIND_TILES * kind + 2 * g + t, j, 0))

    def prv(kind):
        return pl.BlockSpec((None, rows, LANES), lambda g, j, t: (KIND_TILES * kind + 2 * g + t, jnp.maximum(j - 1, 0), 0))

    def nxt(kind):
        return pl.BlockSpec((None, rows, LANES),
                            lambda g, j, t: (KIND_TILES * kind + 2 * g + t, jnp.minimum(j + 1, nbig - 1), 0))

    own = pl.BlockSpec((rows, LANES), lambda g, j, t: (j, 2 * g + t))
    own_next = pl.BlockSpec((rows, LANES), lambda g, j, t: (jnp.minimum(j + 1, nbig - 1), 2 * g + t))
    vec = pl.BlockSpec((1, LANES), lambda g, j, t: (0, 0))
    width = 2 * LANES * len(ATTN_DILATIONS)
    return _pallas_call(
        body, name=name, grid=(len(ATTN_DILATIONS), nbig, 2),
        in_specs=[raw(OFF_Q // LANES), raw(OFF_K // LANES), cur(2), prv(2), cur(0), nxt(0), cur(1), prv(1), own, own_next,
                  own, own_next, own, own_next, pl.BlockSpec((8, LANES), lambda g, j, t: (0, 0))] + [ANY] * len(extra),
        out_specs=[own, own, own, vec, vec],
        out_shape=[jax.ShapeDtypeStruct((T, width), BF)] * 3 + [jax.ShapeDtypeStruct((1, LANES), F32)] * 2,
        scratch_shapes=[pltpu.VMEM((rows, LANES), F32)] * 3,
        compiler_params=_params(("arbitrary", "arbitrary", "arbitrary")),
    )(z, z, qkv, qkv, qkv, qkv, qkv, qkv, do, do, c, c, lse, lse, gains, *extra)


MERGE_ROWS = 256
GATE_TILE = 256


def _group_mix(o_refs, lse_refs):
    lses = [r[...] for r in lse_refs]
    m = jnp.maximum(jnp.maximum(lses[0], lses[1]), lses[2])
    es = [jnp.exp(l - m) for l in lses]
    den = es[0] + es[1] + es[2]
    ws = [e / den for e in es]
    y = ws[0] * o_refs[0][...] + ws[1] * o_refs[1][...] + ws[2] * o_refs[2][...]
    return ws, y


def _sigmoid(v):
    return 1.0 / (1.0 + jnp.exp(-v))


def _merge_specs(T, z, bgate, gpu, gco, gau):
    tm = min(MERGE_ROWS, T)
    row = lambda w: pl.BlockSpec((tm, w), lambda i: (i, 0))
    gate0 = OFF_GATE // GATE_TILE
    gates = [pl.BlockSpec((tm, GATE_TILE), functools.partial(lambda i, cb: (i, cb), cb=gate0 + n))
             for n in range(3 * N_CHIPS)]
    full = lambda a: pl.BlockSpec(a.shape, lambda i: (0,) * a.ndim)
    by_group = [pl.BlockSpec((tm, 256), functools.partial(lambda i, g: (i, g), g=g)) for g in range(3)]
    specs = [row(512), row(512)] + by_group * 2 + gates + [full(bgate), full(gpu), full(gco), full(gau)]
    return tm, row, specs


def _merge_fwd(yp, yc, o3, lse3, z, bgate, gpu, gco, gau, name):
    T = yp.shape[0]
    tm, row, specs = _merge_specs(T, z, bgate, gpu, gco, gau)

    def body(*refs):
        yp_ref, yc_ref = refs[0], refs[1]
        o_refs, lse_refs = refs[2:5], refs[5:8]
        zg = refs[8:20]
        b_ref, gpu_ref, gco_ref, gau_ref, out_ref = refs[20:25]
        yab = _group_mix(o_refs, lse_refs)[1].astype(BF)
        ys = (yp_ref[...], yc_ref[...], yab)
        ups = (gpu_ref, gco_ref, gau_ref)
        for n in range(N_CHIPS):
            acc = None
            for b in range(3):
                gcol = slice(1024 * b + GATE_TILE * n, 1024 * b + GATE_TILE * (n + 1))
                gate = _sigmoid(zg[N_CHIPS * b + n][...] + b_ref[:, gcol])
                term = gate * _dot(ys[b], ups[b][n])
                acc = term if acc is None else acc + term
            out_ref[:, GATE_TILE * n:GATE_TILE * (n + 1)] = acc.astype(BF)

    return _pallas_call(
        body, name=name, grid=(T // tm,), in_specs=specs, out_specs=row(1024),
        out_shape=jax.ShapeDtypeStruct((T, 1024), BF), compiler_params=_params(("parallel",)),
    )(yp, yc, *([o3] * 3), *([lse3] * 3), *([z] * 12), bgate, gpu, gco, gau)


def _merge_bwd(dm, yp, yc, o3, lse3, z, bgate, gpu, gco, gau, name):
    T = yp.shape[0]
    tm, row, specs = _merge_specs(T, z, bgate, gpu, gco, gau)
    nsteps = T // tm

    def body(*refs):
        dm_ref, yp_ref, yc_ref = refs[0:3]
        o_refs, lse_refs = refs[3:6], refs[6:9]
        zg = refs[9:21]
        b_ref, gpu_ref, gco_ref, gau_ref = refs[21:25]
        dzg_ref, dyp_ref, dyc_ref = refs[25:28]
        do_ref, c_ref = refs[28:30]
        dgpu_ref, dgco_ref, dgau_ref, dbg_ref = refs[30:34]
        accs = refs[34:37]
        i = pl.program_id(0)

        @pl.when(i == 0)
        def _():
            for a in accs:
                a[...] = jnp.zeros_like(a)
            dbg_ref[...] = jnp.zeros_like(dbg_ref)

        ws, y = _group_mix(o_refs, lse_refs)
        ys = (yp_ref[...], yc_ref[...], y.astype(BF))
        ups = (gpu_ref, gco_ref, gau_ref)
        dys = [None, None, None]
        for n in range(N_CHIPS):
            dmn = dm_ref[:, GATE_TILE * n:GATE_TILE * (n + 1)]
            for b in range(3):
                gcol = slice(1024 * b + GATE_TILE * n, 1024 * b + GATE_TILE * (n + 1))
                gate = _sigmoid(zg[N_CHIPS * b + n][...] + b_ref[:, gcol])
                up = _dot(ys[b], ups[b][n])
                dzg = (dmn * up) * (gate * (1.0 - gate))
                dzg_ref[:, gcol] = dzg.astype(BF)
                dbg_ref[:, gcol] += jnp.sum(dzg, axis=0, keepdims=True)
                dup = (dmn * gate).astype(BF)
                accs[b][n] += _dot(ys[b], dup, "tn")
                dyb = _dot(dup, ups[b][n], "nt")
                dys[b] = dyb if dys[b] is None else dys[b] + dyb
        dyp_ref[...] = dys[0]
        dyc_ref[...] = dys[1]
        dya = dys[2]
        lane = lax.broadcasted_iota(jnp.int32, dya.shape, 1) // HEAD_DIM
        pr = dya * y
        rho = jnp.zeros_like(pr)
        for h in range(256 // HEAD_DIM):
            hm = lane == h
            rho = jnp.where(hm, jnp.sum(jnp.where(hm, pr, 0.0), axis=-1, keepdims=True), rho)
        for g in range(3):
            do_ref[:, 256 * g:256 * (g + 1)] = ws[g] * dya
            c_ref[:, 256 * g:256 * (g + 1)] = -(ws[g] * rho)

        @pl.when(i == nsteps - 1)
        def _():
            dgpu_ref[...] = accs[0][...].astype(BF)
            dgco_ref[...] = accs[1][...].astype(BF)
            dgau_ref[...] = accs[2][...].astype(BF)

    full = lambda a: pl.BlockSpec(a.shape, lambda i: (0,) * a.ndim)
    dz_gate = pl.BlockSpec((pl.Element(tm), pl.Element(3072)), lambda i: (i * tm, OFF_GATE))
    out_specs = ([dz_gate, row(512), row(512)] + [row(768)] * 2 + [full(gpu), full(gco), full(gau)]
                 + [pl.BlockSpec((1, 3072), lambda i: (0, 0))])
    out_shape = ([jax.ShapeDtypeStruct(z.shape, BF)] + [jax.ShapeDtypeStruct((T, 512), F32)] * 2
                 + [jax.ShapeDtypeStruct((T, 768), F32)] * 2
                 + [jax.ShapeDtypeStruct(g.shape, BF) for g in (gpu, gco, gau)]
                 + [jax.ShapeDtypeStruct((1, 3072), F32)])
    return _pallas_call(
        body, name=name, grid=(nsteps,), in_specs=[row(1024)] + specs, out_specs=out_specs, out_shape=out_shape,
        scratch_shapes=[pltpu.VMEM(g.shape, F32) for g in (gpu, gco, gau)],
        compiler_params=_params(("arbitrary",)),
    )(dm, yp, yc, *([o3] * 3), *([lse3] * 3), *([z] * 12), bgate, gpu, gco, gau)


def _layer_fwd(x, w, tag, after=None, soon=None, late=None, target=None, hb=None, next_gain=None):
    if hb is None:
        hb = _rms_fwd(x, w["norm_mix"], f"rms_mix_{tag}", after=after)
    if soon is not None:
        w = dict(w, **soon(hb))
    z = _mm(hb, w["w_in"], "nt", f"in_proj_{tag}", tm=512, tn=3712, tk=1024, n_outer=True, after=w.get("started"))
    yp, yc = _poolconv_fwd(z, w["pool_mix"], w["pool_scale"], w["conv_w"], f"poolconv_{tag}")
    qkv = _qk_norm(z, w["qk_gain"], f"qk_norm_{tag}")
    o3, lse3 = _attn_fwd(qkv, f"attn_{tag}")
    if late is not None:
        w = dict(w, **late(lse3))
    merged = _merge_fwd(yp, yc, o3, lse3, z, w["b_gate"], w["w_pool_up"], w["w_conv_out"], w["w_attn_up"],
                        f"merge_{tag}")
    x1, h2b = _mm(merged, w["w_o"], "nn", f"out_proj_{tag}", tm=1024, tn=1024, tk=1024, res=x, vec=w["norm_mlp"],
                  epi="rms_next")
    rb = _mm(h2b, w["w_ff1"], "nn", f"ff1_{tag}", tm=1024, tn=1024, tk=1024, out_dtype=BF, epi="relu2", n_outer=True,
             b_shards=True)
    if target is not None:
        x2 = _mm(rb, w["w_ff2"], "nn", f"ff2_{tag}", tm=512, tn=1024, tk=4096, res=x1, aux=target, epi="loss")
    elif next_gain is not None:
        x2 = _mm(rb, w["w_ff2"], "nn", f"ff2_{tag}", tm=512, tn=1024, tk=4096, res=x1, vec=next_gain, epi="rms_next")
    else:
        x2 = _mm(rb, w["w_ff2"], "nn", f"ff2_{tag}", tm=512, tn=1024, tk=4096, res=x1)
    saved = dict(x=x, hb=hb, z=z, yp=yp, yc=yc, qkv=qkv, o3=o3, lse3=lse3, merged=merged, x1=x1, h2b=h2b, rb=rb)
    return x2, saved, w


def _layer_bwd(dx2, w, s, tag, after=None, mid=None, tail=None):
    g = {}
    dab = _mm(dx2, w["w_ff2"], "nt", f"d_ff2_act_{tag}", tm=1024, tn=1024, tk=1024, out_dtype=BF, aux=s["rb"],
              epi="drelu2", after=after)
    g["w_ff2"] = _mm(s["rb"], dx2, "tn", f"d_ff2_w_{tag}", tm=1024, tn=1024, tk=2048, out_dtype=BF)
    g["w_ff1"] = _mm(s["h2b"], dab, "tn", f"d_ff1_w_{tag}", tm=1024, tn=1024, tk=2048, out_dtype=BF, out_shards=True)
    dx1, g["norm_mlp"] = _mm(dab, w["w_ff1"], "nt", f"d_ff1_act_{tag}", tm=1024, tn=1024, tk=1024, b_shards=True,
                             res=dx2, aux=s["x1"], vec=w["norm_mlp"], epi="rms_bwd")
    dm = _mm(dx1, w["w_o"], "nt", f"d_out_act_{tag}", tm=1024, tn=1024, tk=1024)
    g["w_o"] = _mm(s["merged"], dx1, "tn", f"d_out_w_{tag}", tm=1024, tn=1024, tk=1024, out_dtype=BF)
    (dz, dyp, dyc, do3, c3, g["w_pool_up"], g["w_conv_out"], g["w_attn_up"],
     g["b_gate"]) = _merge_bwd(dm, s["yp"], s["yc"], s["o3"], s["lse3"], s["z"], w["b_gate"], w["w_pool_up"],
                               w["w_conv_out"], w["w_attn_up"], f"d_merge_{tag}")
    behind = mid(g) if mid is not None else None
    dzq, dzk, dzv, dgq, dgk = _attn_bwd(s["z"], s["qkv"], do3, c3, s["lse3"], w["qk_gain"], f"d_attn_{tag}",
                                        after=behind)
    g["q_gain"] = dgq[:, :HEAD_DIM] + dgq[:, HEAD_DIM:]
    g["k_gain"] = dgk[:, :HEAD_DIM] + dgk[:, HEAD_DIM:]
    for off, piece in ((OFF_Q, dzq), (OFF_K, dzk), (OFF_V, dzv)):
        dz = lax.dynamic_update_slice(dz, piece, (0, off))
    dz, g["pool_mix"], g["pool_scale"], g["conv_w"] = _poolconv_bwd(
        s["z"], dyp, dyc, w["pool_mix"], w["pool_scale"], w["conv_w"], dz, f"d_poolconv_{tag}")
    g["w_in"] = _mm(s["hb"], dz, "tn", f"d_in_w_{tag}", tm=512, tn=3712, tk=1024, out_dtype=BF)
    dh = _mm(dz, w["w_in"], "nn", f"d_in_act_{tag}", tm=1024, tn=1024, tk=3712,
             after=tail(g) if tail is not None else None)
    dx, g["norm_mix"] = _rms_bwd(dh, s["x"], w["norm_mix"], dx1, f"d_rms_mix_{tag}")
    return dx, g


def _position():
    x, y, c = lax.axis_index("x"), lax.axis_index("y"), lax.axis_index("c")
    chips = [(1 - x, y), (x, 1 - y), (1 - x, 1 - y)]
    return x, y, c, 2 * x + y, chips, [2 * cx + cy for cx, cy in chips]


def _remote(src, dst, ssem, rsem, dev):
    return pltpu.make_async_remote_copy(src_ref=src, dst_ref=dst, send_sem=ssem, recv_sem=rsem, device_id=dev,
                                        device_id_type=MESH_ID)


def _halves(a):
    return a.reshape(a.shape[0], 2, a.shape[1] // 2, a.shape[2])


SEM = pl.BlockSpec(memory_space=pltpu.SEMAPHORE)
TOKEN = jax.ShapeDtypeStruct((8, LANES), F32)
TOKEN_SPEC = pl.BlockSpec(memory_space=pltpu.VMEM)


def _split_params():
    return pltpu.CompilerParams(has_side_effects=pltpu.SideEffectType.DATAFLOW_SIDE_EFFECTING)


def _ici_plan(x, y, c, q, chips, qs):
    return [((q, c), (qs[j], c), (chips[j][0], chips[j][1], c)) for j in range(3)]


def _sibling_plan(x, y, c, q, chips, qs):
    return [((qs[j], c), (qs[j], 1 - c), (x, y, 1 - c)) for j in range(3)]


def _gather_start(bufs, name, after):
    return _copies_start([_halves(b) for b in bufs], name, after, _ici_plan)


def _copies_start(views, name, after, plan):
    n = len(views)

    def body(*refs):
        first_sem = n + 1
        ssem, rsem = refs[first_sem:first_sem + ns], refs[first_sem + ns:first_sem + 2 * ns]
        outs, token = refs[first_sem + 2 * ns:first_sem + 2 * ns + n], refs[first_sem + 2 * ns + n]
        for k in range(n):
            for j, (sent, _, peer) in enumerate(plan(*_position())):
                mine = outs[k].at[sent]
                _remote(mine, mine, ssem[3 * k + j], rsem[3 * k + j], peer).start()
        token[...] = jnp.zeros_like(token)

    ns = 3 * n
    outs = _pallas_call(
        body, name=name, in_specs=[ANY] * (n + 1), out_specs=[SEM] * (2 * ns) + [ANY] * n + [TOKEN_SPEC],
        out_shape=[pltpu.SemaphoreType.DMA(())] * (2 * ns) + [jax.ShapeDtypeStruct(v.shape, v.dtype) for v in views]
        + [TOKEN],
        input_output_aliases={k: k + 2 * ns for k in range(n)}, compiler_params=_split_params(),
    )(*views, after)
    return list(outs[:ns]), list(outs[ns:2 * ns]), list(outs[2 * ns:2 * ns + n]), outs[2 * ns + n]


def _copies_wait(ssem, rsem, views, after, name, plan):
    n = len(views)
    ns = len(ssem)

    def wait_body(*refs):
        ssem_ref, rsem_ref = refs[n:n + ns], refs[n + ns:n + 2 * ns]
        outs = refs[n + 2 * ns + 1:]
        for k in range(n):
            for j, (sent, landing, peer) in enumerate(plan(*_position())):
                cp = _remote(outs[k].at[sent], outs[k].at[landing], ssem_ref[3 * k + j], rsem_ref[3 * k + j], peer)
                cp.wait_send()
                cp.wait_recv()

    return _pallas_call(
        wait_body, name=name, in_specs=[ANY] * n + [SEM] * (2 * ns) + [ANY], out_specs=[ANY] * n,
        out_shape=[jax.ShapeDtypeStruct(v.shape, v.dtype) for v in views],
        input_output_aliases={k: k for k in range(n)}, compiler_params=_split_params(),
    )(*views, *ssem, *rsem, after)


def _gather_finish(ssem, rsem, views, after, name_wait, name_forward, shapes):
    return _gather_forward(_copies_wait(ssem, rsem, views, after, name_wait, _ici_plan), name_forward, shapes)


def _gather_forward(landed, name_forward, shapes):
    n = len(landed)
    views = landed

    def forward_body(*refs):
        outs = refs[n:2 * n]
        fssem, frsem = refs[2 * n:]
        x, y, c, q, chips, qs = _position()
        sib = (x, y, 1 - c)
        sent = []
        for k in range(n):
            for j in range(3):
                slot = outs[k].at[qs[j], c]
                cp = _remote(slot, slot, fssem.at[k, j], frsem.at[k, j], sib)
                cp.start()
                sent.append(cp)
        for k in range(n):
            for j in range(3):
                slot = outs[k].at[qs[j], 1 - c]
                _remote(slot, slot, fssem.at[k, j], frsem.at[k, j], sib).wait_recv()
        for cp in sent:
            cp.wait_send()

    outs = _pallas_call(
        forward_body, name=name_forward, in_specs=[ANY] * n, out_specs=[ANY] * n,
        out_shape=[jax.ShapeDtypeStruct(v.shape, v.dtype) for v in views],
        input_output_aliases={k: k for k in range(n)}, scratch_shapes=[pltpu.SemaphoreType.DMA((n, 3))] * 2,
    )(*landed)
    return [o.reshape(s) for o, s in zip(outs, shapes)]


def _chip_exchange_start(parts, name):
    n = len(parts)

    def body(*refs):
        ssem, rsem = refs[n:n + ns], refs[n + ns:n + 2 * ns]
        base = n + 2 * ns
        srcs, outs, token = refs[base:base + n], refs[base + n:base + 2 * n], refs[base + 2 * n]
        x, y, c, q, chips, qs = _position()
        for k in range(n):
            for j, chip in enumerate(chips):
                _remote(srcs[k].at[qs[j]], outs[k].at[j], ssem[3 * k + j], rsem[3 * k + j],
                        (chip[0], chip[1], c)).start()
        token[...] = jnp.zeros_like(token)

    ns = 3 * n
    outs = _pallas_call(
        body, name=name, in_specs=[ANY] * n, out_specs=[SEM] * (2 * ns) + [ANY] * (2 * n) + [TOKEN_SPEC],
        out_shape=[pltpu.SemaphoreType.DMA(())] * (2 * ns) + [jax.ShapeDtypeStruct(a.shape, a.dtype) for a in parts]
        + [jax.ShapeDtypeStruct((3,) + a.shape[1:], a.dtype) for a in parts] + [TOKEN],
        input_output_aliases={k: k + 2 * ns for k in range(n)}, compiler_params=_split_params(),
    )(*parts)
    b = 2 * ns
    return list(outs[:ns]), list(outs[ns:b]), list(outs[b:b + n]), list(outs[b + n:b + 2 * n]), outs[b + 2 * n]


def _chip_exchange_wait(ssem, rsem, parts, landing, after, name):
    n = len(parts)
    ns = len(ssem)

    def body(*refs):
        ssem_ref, rsem_ref = refs[2 * n:2 * n + ns], refs[2 * n + ns:2 * n + 2 * ns]
        base = 2 * n + 2 * ns + 1
        srcs, outs = refs[base:base + n], refs[base + n:]
        x, y, c, q, chips, qs = _position()
        for k in range(n):
            for j, chip in enumerate(chips):
                cp = _remote(srcs[k].at[qs[j]], outs[k].at[j], ssem_ref[3 * k + j], rsem_ref[3 * k + j],
                             (chip[0], chip[1], c))
                cp.wait_send()
                cp.wait_recv()

    outs = _pallas_call(
        body, name=name, in_specs=[ANY] * (2 * n) + [SEM] * (2 * ns) + [ANY], out_specs=[ANY] * (2 * n),
        out_shape=[jax.ShapeDtypeStruct(a.shape, a.dtype) for a in list(parts) + list(landing)],
        input_output_aliases={k: k for k in range(2 * n)}, compiler_params=_split_params(),
    )(*parts, *landing, *ssem, *rsem, after)
    return list(outs[:n]), list(outs[n:])


def _pair_swap(views, name):
    n = len(views)

    def body(*refs):
        ins, outs = refs[:n], refs[n:2 * n]
        ssem, rsem = refs[2 * n:]
        x, y, c, _, _, _ = _position()
        cps = [_remote(ins[k].at[pl.ds(0, N_CHIPS), 1 - c], outs[k], ssem.at[k], rsem.at[k], (x, y, 1 - c))
               for k in range(n)]
        for cp in cps:
            cp.start()
        for cp in cps:
            cp.wait()

    return _pallas_call(
        body, name=name, in_specs=[ANY] * n, out_specs=[ANY] * n,
        out_shape=[jax.ShapeDtypeStruct((v.shape[0],) + v.shape[2:], v.dtype) for v in views],
        scratch_shapes=[pltpu.SemaphoreType.DMA((n,))] * 2,
    )(*views)


def _pair_send(arrays, name):
    n = len(arrays)

    def body(*refs):
        ins, outs = refs[:n], refs[n:2 * n]
        ssem, rsem = refs[2 * n:]
        x, y, c, _, _, _ = _position()
        cps = [_remote(ins[k], outs[k], ssem.at[k], rsem.at[k], (x, y, 1 - c)) for k in range(n)]
        for cp in cps:
            cp.start()
        for cp in cps:
            cp.wait()

    return _pallas_call(
        body, name=name, in_specs=[ANY] * n, out_specs=[ANY] * n,
        out_shape=[jax.ShapeDtypeStruct(a.shape, a.dtype) for a in arrays],
        scratch_shapes=[pltpu.SemaphoreType.DMA((n,))] * 2,
    )(*arrays)


def _all_to_all_small(part):
    P = part.shape[0]

    def body(in_ref, out_ref, lsem, ssem, rsem):
        x, y, c = lax.axis_index("x"), lax.axis_index("y"), lax.axis_index("c")
        me = 4 * x + 2 * y + c
        flips = [(fx, fy, fc) for fx in (0, 1) for fy in (0, 1) for fc in (0, 1)][1:]
        peers = [((x + fx) % 2, (y + fy) % 2, (c + fc) % 2) for fx, fy, fc in flips]
        loc = pltpu.make_async_copy(in_ref, out_ref.at[me], lsem)
        loc.start()
        cps = [_remote(in_ref, out_ref.at[me], ssem.at[j], rsem.at[j], peer) for j, peer in enumerate(peers)]
        for cp in cps:
            cp.start()
        for j, (px, py, pc) in enumerate(peers):
            _remote(in_ref, out_ref.at[4 * px + 2 * py + pc], ssem.at[j], rsem.at[j], peers[j]).wait_recv()
        for cp in cps:
            cp.wait_send()
        loc.wait()

    return _pallas_call(
        body, name="small_exchange", in_specs=[ANY], out_specs=ANY,
        out_shape=jax.ShapeDtypeStruct((8, P, LANES), F32),
        scratch_shapes=[pltpu.SemaphoreType.DMA(())] + [pltpu.SemaphoreType.DMA((7,))] * 2,
    )(part)


def _small_peers():
    x, y, c = lax.axis_index("x"), lax.axis_index("y"), lax.axis_index("c")
    flips = [(fx, fy, fc) for fx in (0, 1) for fy in (0, 1) for fc in (0, 1)][1:]
    peers = [((x + fx) % 2, (y + fy) % 2, (c + fc) % 2) for fx, fy, fc in flips]
    return 4 * x + 2 * y + c, peers


def _all_to_all_small_start(part, name):
    P = part.shape[0]
    me = 4 * lax.axis_index("x") + 2 * lax.axis_index("y") + lax.axis_index("c")
    landing = lax.dynamic_update_slice(jnp.zeros((8, P, LANES), F32), part[None], (me, 0, 0))

    def body(*refs):
        sems, src, land, token = refs[2:16], refs[16], refs[17], refs[18]
        me_, peers = _small_peers()
        for j, peer in enumerate(peers):
            _remote(src, land.at[me_], sems[j], sems[7 + j], peer).start()
        token[...] = jnp.zeros_like(token)

    outs = _pallas_call(
        body, name=name, in_specs=[ANY, ANY], out_specs=[SEM] * 14 + [ANY, ANY, TOKEN_SPEC],
        out_shape=[pltpu.SemaphoreType.DMA(())] * 14 + [jax.ShapeDtypeStruct(part.shape, F32),
                                                       jax.ShapeDtypeStruct((8, P, LANES), F32), TOKEN],
        input_output_aliases={0: 14, 1: 15}, compiler_params=_split_params(),
    )(part, landing)
    return list(outs[:7]), list(outs[7:14]), outs[14], outs[15], outs[16]


def _all_to_all_small_wait(ssem, rsem, part, landing, after, name):
    def body(*refs):
        sems, src, land = refs[2:16], refs[17], refs[18]
        _, peers = _small_peers()
        for j, (px, py, pc) in enumerate(peers):
            cp = _remote(src, land.at[4 * px + 2 * py + pc], sems[j], sems[7 + j], peers[j])
            cp.wait_send()
            cp.wait_recv()

    return _pallas_call(
        body, name=name, in_specs=[ANY, ANY] + [SEM] * 14 + [ANY], out_specs=[ANY, ANY],
        out_shape=[jax.ShapeDtypeStruct(part.shape, F32), jax.ShapeDtypeStruct(landing.shape, F32)],
        input_output_aliases={0: 0, 1: 1}, compiler_params=_split_params(),
    )(part, landing, *ssem, *rsem, after)[1]


def _row_tile(rows, width, n_arrays):
    t = rows
    while t % 2 == 0 and t > 8 and 2 * n_arrays * t * width * 4 > VMEM_LIMIT // 2:
        t //= 2
    return t


def _chip():
    return 2 * lax.axis_index("x") + lax.axis_index("y")


def _core():
    return lax.axis_index("c")


def _cast_place(w3, layer, name):
    _, r, c = w3.shape
    tr = _row_tile(r, c, 2)

    def body(w_ref, o_ref):
        o_ref[...] = w_ref[...].astype(BF)

    return _pallas_call(
        body, name=name, grid=(r // tr,), in_specs=[pl.BlockSpec((None, tr, c), lambda i: (layer, i, 0))],
        out_specs=pl.BlockSpec((None, tr, c), lambda i: (_chip(), i, 0)),
        out_shape=jax.ShapeDtypeStruct((N_CHIPS, r, c), BF), compiler_params=_params(("parallel",)),
    )(w3)


def _pair_sum(views, recvs, name):
    n = len(views)

    def body(*refs):
        for g_ref, r_ref, o_ref in zip(refs[:n], refs[n:2 * n], refs[2 * n:]):
            o_ref[...] = (g_ref[...].astype(F32) + r_ref[...].astype(F32)).astype(BF)

    own = [pl.BlockSpec((None, None) + v.shape[2:], lambda p: (p, _core(), 0, 0)) for v in views]
    blk = [pl.BlockSpec((None,) + r.shape[1:], lambda p: (p, 0, 0)) for r in recvs]
    return _pallas_call(
        body, name=name, grid=(N_CHIPS,), in_specs=own + blk, out_specs=blk,
        out_shape=[jax.ShapeDtypeStruct(r.shape, BF) for r in recvs], compiler_params=_params(("parallel",)),
    )(*views, *recvs)


CHIP_SUM_STEPS = 2


def _chip_sum(parts, recvs, name):
    n = len(parts)

    def body(*refs):
        for p_ref, r_ref, o_ref in zip(refs[:n], refs[n:2 * n], refs[2 * n:]):
            acc = p_ref[...].astype(F32)
            for j in range(3):
                acc = acc + r_ref[j].astype(F32)
            o_ref[...] = acc

    rows = [p.shape[1] // CHIP_SUM_STEPS for p in parts]
    return _pallas_call(
        body, name=name, grid=(CHIP_SUM_STEPS,),
        in_specs=[pl.BlockSpec((None, t, p.shape[2]), lambda i: (_chip(), i, 0)) for p, t in zip(parts, rows)]
        + [pl.BlockSpec((3, t, p.shape[2]), lambda i: (0, i, 0)) for p, t in zip(parts, rows)],
        out_specs=[pl.BlockSpec((t, p.shape[2]), lambda i: (i, 0)) for p, t in zip(parts, rows)],
        out_shape=[jax.ShapeDtypeStruct(p.shape[1:], F32) for p in parts], compiler_params=_params(("parallel",)),
    )(*parts, *recvs)


def _sum_slices(a, name):
    n, rows, width = a.shape
    tr = _row_tile(rows, width, n + 1)

    def body(a_ref, o_ref):
        acc = a_ref[0].astype(F32)
        for i in range(1, n):
            acc = acc + a_ref[i].astype(F32)
        o_ref[...] = acc

    return _pallas_call(
        body, name=name, grid=(rows // tr,), in_specs=[pl.BlockSpec((n, tr, width), lambda i: (0, i, 0))],
        out_specs=pl.BlockSpec((tr, width), lambda i: (i, 0)), out_shape=jax.ShapeDtypeStruct((rows, width), F32),
        compiler_params=_params(("parallel",)),
    )(a)


def _adamw_update(w, g, m, v):
    nm = ADAM_B1 * m + (1.0 - ADAM_B1) * g
    nv = ADAM_B2 * v + (1.0 - ADAM_B2) * (g * g)
    m_hat = nm / (1.0 - ADAM_B1 ** ADAM_STEP)
    v_hat = nv / (1.0 - ADAM_B2 ** ADAM_STEP)
    return -ADAM_LR * (m_hat / (jnp.sqrt(v_hat) + ADAM_EPS) + ADAM_WD * w), nm, nv


def _adamw(ws, gs, ms, vs, name):
    n = len(ws)

    def body(*refs):
        for k in range(n):
            w_ref, g_ref, m_ref, v_ref = (refs[s * n + k] for s in range(4))
            d_ref, nm_ref, nv_ref = (refs[(4 + s) * n + k] for s in range(3))
            d_ref[...], nm_ref[...], nv_ref[...] = _adamw_update(w_ref[...], g_ref[...], m_ref[...], v_ref[...])

    whole = [pl.BlockSpec(w.shape, lambda i: (0, 0)) for w in ws]
    outs = _pallas_call(
        body, name=name, grid=(1,), in_specs=whole * 4, out_specs=whole * 3,
        out_shape=[jax.ShapeDtypeStruct(w.shape, F32) for _ in range(3) for w in ws],
        compiler_params=_params(("arbitrary",)),
    )(*ws, *gs, *ms, *vs)
    return [[outs[s * n + k] for s in range(3)] for k in range(n)]


ADAMW_STEPS = 4


def _adamw_halves(ws, ms, vs, mine, other, name):
    n = len(ws)
    depth = ws[0].shape[0]
    assert depth == 2
    halves = [(w.shape[1] // 2, w.shape[2]) for w in ws]
    tiles = [hr // ADAMW_STEPS for hr, _ in halves]
    kinds = ((0, True), (0, False), (1, True), (1, False))

    def active(l, h, layer, own):
        mine_half = h == _core()
        return (l == layer) & (mine_half if own else jnp.logical_not(mine_half))

    def body(*refs):
        l, h = pl.program_id(0), pl.program_id(1)
        flags = [active(l, h, layer, own) for layer, own in kinds]
        for k in range(n):
            w_ref, m_ref, v_ref = refs[k], refs[n + k], refs[2 * n + k]
            g_refs = [refs[(3 + s) * n + k] for s in range(4)]
            go_ref, d_ref, nm_ref, nv_ref = (refs[(7 + s) * n + k] for s in range(4))
            for flag, g_ref in zip(flags, g_refs):
                @pl.when(flag)
                def _():
                    gv = g_ref[...]
                    go_ref[...] = gv
                    d_ref[...], nm_ref[...], nv_ref[...] = _adamw_update(w_ref[...], gv, m_ref[...], v_ref[...])

    def blk(k):
        return pl.BlockSpec((None, None, tiles[k], halves[k][1]), lambda l, h, i: (l, h, i, 0))

    def gspec(k, layer, own):
        return pl.BlockSpec((tiles[k], halves[k][1]), lambda l, h, i: (jnp.where(active(l, h, layer, own), i, 0), 0))

    def view(a, k):
        return a.reshape(depth, 2, halves[k][0], halves[k][1])

    blks = [blk(k) for k in range(n)]
    sources = [[(mine if own else other)[layer][k] for k in range(n)] for layer, own in kinds]
    outs = _pallas_call(
        body, name=name, grid=(depth, 2, ADAMW_STEPS),
        in_specs=blks * 3 + [gspec(k, layer, own) for layer, own in kinds for k in range(n)], out_specs=blks * 4,
        out_shape=[jax.ShapeDtypeStruct((depth, 2) + halves[k], F32) for _ in range(4) for k in range(n)],
        compiler_params=_params(("parallel", "parallel", "parallel")),
    )(*[view(a, k) for group in (ws, ms, vs) for k, a in enumerate(group)], *[g for src in sources for g in src])
    return [[outs[s * n + k].reshape(ws[k].shape) for s in range(4)] for k in range(n)]


BIG = ("w_in", "w_pool_up", "w_conv_out", "w_attn_up", "w_o", "w_ff1", "w_ff2")
SMALL = ("norm_mix", "b_gate", "pool_mix", "pool_scale", "conv_w", "q_gain", "k_gain", "norm_mlp")
ORDER = ("norm_mix", "w_in", "b_gate", "pool_mix", "pool_scale", "conv_w", "q_gain", "k_gain", "w_pool_up",
         "w_conv_out", "w_attn_up", "w_o", "norm_mlp", "w_ff1", "w_ff2")
COLUMN_SHARDED = ("w_pool_up", "w_conv_out", "w_attn_up", "w_ff1")


def _matrix_weights(gathered):
    w = {}
    for name, g4 in gathered.items():
        if name in COLUMN_SHARDED:
            w[name] = g4
        else:
            w[name] = g4.reshape(N_CHIPS * g4.shape[1], g4.shape[2])
    return w


def _small_weights(l, small):
    w = {}
    w["norm_mix"] = small["norm_mix"][l][None]
    w["norm_mlp"] = small["norm_mlp"][l][None]
    w["b_gate"] = small["b_gate"][l][None]
    w["pool_mix"] = small["pool_mix"][l].astype(BF)
    w["pool_scale"] = small["pool_scale"][l][None]
    w["conv_w"] = jnp.pad(small["conv_w_full"][l], ((0, 5), (0, 0)))
    w["qk_gain"] = jnp.pad(jnp.stack([jnp.tile(small["q_gain"][l], 2), jnp.tile(small["k_gain"][l], 2)]), ((0, 6), (0, 0)))
    return w


def _to_chip_major(name, g):
    if name == "w_in":
        return g.T.reshape(N_CHIPS, g.shape[1] // N_CHIPS, g.shape[0])
    if name in COLUMN_SHARDED:
        return g
    return g.reshape(N_CHIPS, g.shape[0] // N_CHIPS, g.shape[1])


def _pad8(a):
    a = a.reshape(-1)
    return jnp.pad(a, (0, (-a.size) % (8 * LANES))).reshape(-1, LANES)


def kernel(x, norm_mix, w_in, b_gate, pool_mix, pool_scale, conv_w, q_gain, k_gain, w_pool_up, w_conv_out, w_attn_up, w_o, norm_mlp, w_ff1, w_ff2, loss_target, m_norm_mix, m_w_in, m_b_gate, m_pool_mix, m_pool_scale, m_conv_w, m_q_gain, m_k_gain, m_w_pool_up, m_w_conv_out, m_w_attn_up, m_w_o, m_norm_mlp, m_w_ff1, m_w_ff2, v_norm_mix, v_w_in, v_b_gate, v_pool_mix, v_pool_scale, v_conv_w, v_q_gain, v_k_gain, v_w_pool_up, v_w_conv_out, v_w_attn_up, v_w_o, v_norm_mlp, v_w_ff1, v_w_ff2):
    weights = dict(norm_mix=norm_mix, w_in=w_in, b_gate=b_gate, pool_mix=pool_mix, pool_scale=pool_scale, conv_w=conv_w,
                   q_gain=q_gain, k_gain=k_gain, w_pool_up=w_pool_up, w_conv_out=w_conv_out, w_attn_up=w_attn_up,
                   w_o=w_o, norm_mlp=norm_mlp, w_ff1=w_ff1, w_ff2=w_ff2)
    moms = dict(norm_mix=m_norm_mix, w_in=m_w_in, b_gate=m_b_gate, pool_mix=m_pool_mix, pool_scale=m_pool_scale,
                conv_w=m_conv_w, q_gain=m_q_gain, k_gain=m_k_gain, w_pool_up=m_w_pool_up, w_conv_out=m_w_conv_out,
                w_attn_up=m_w_attn_up, w_o=m_w_o, norm_mlp=m_norm_mlp, w_ff1=m_w_ff1, w_ff2=m_w_ff2)
    vels = dict(norm_mix=v_norm_mix, w_in=v_w_in, b_gate=v_b_gate, pool_mix=v_pool_mix, pool_scale=v_pool_scale,
                conv_w=v_conv_w, q_gain=v_q_gain, k_gain=v_k_gain, w_pool_up=v_w_pool_up, w_conv_out=v_w_conv_out,
                w_attn_up=v_w_attn_up, w_o=v_w_o, norm_mlp=v_norm_mlp, w_ff1=v_w_ff1, w_ff2=v_w_ff2)
    depth = norm_mix.shape[0]
    q = 2 * lax.axis_index("x") + lax.axis_index("y")
    for group in (weights, moms, vels):
        group["w_in"] = jnp.swapaxes(group["w_in"], 1, 2)

    assert depth == 2, "the second layer's gather hides behind the first layer's forward, and likewise backward"
    first, rest = BIG[:1], BIG[1:]
    cw_all = _all_to_all_small(_pad8(conv_w))
    bufs = [{n: _cast_place(weights[n], 0, f"cast_{n}_l0") for n in first}]
    a_ssem, a_rsem, a_views, a_token = _gather_start([bufs[0][n] for n in first], "gather_start_l0_in", cw_all)
    bufs[0].update({n: _cast_place(weights[n], 0, f"cast_{n}_l0") for n in rest})
    bufs += [{n: _cast_place(weights[n], l, f"cast_{n}_l{l}") for n in BIG} for l in range(1, depth)]
    b_ssem, b_rsem, b_views, b_token = _gather_start([bufs[0][n] for n in rest], "gather_start_l0_rest", a_token)
    g_ssem, g_rsem, g_views, g_token = _gather_start([bufs[1][n] for n in BIG], "gather_start_l1", b_token)
    conv_w_full = jnp.concatenate(
        [cw_all[2 * p].reshape(-1)[:conv_w.size].reshape(conv_w.shape) for p in range(N_CHIPS)], axis=-1)
    small = dict(weights)
    small["conv_w_full"] = conv_w_full

    def soon_weights(t):
        got = _gather_finish(a_ssem, a_rsem, a_views, t, "gather_wait_l0_in", "gather_forward_l0_in",
                             [bufs[0][n].shape for n in first])
        return _matrix_weights(dict(zip(first, got)))

    def late_weights(t):
        got = _gather_finish(b_ssem, b_rsem, b_views, t, "gather_wait_l0_rest", "gather_forward_l0_rest",
                             [bufs[0][n].shape for n in rest])
        return _matrix_weights(dict(zip(rest, got)))

    wl, saved = [None] * depth, [None] * depth
    small_1 = _small_weights(1, small)
    (h, hb_1), saved[0], wl[0] = _layer_fwd(x[0], _small_weights(0, small), "l0", after=g_token, soon=soon_weights,
                                            late=late_weights, next_gain=small_1["norm_mix"])
    shapes_1 = [bufs[1][n].shape for n in BIG]
    sibling = {}

    def soon_weights_1(t):
        landed = _copies_wait(g_ssem, g_rsem, g_views, t, "gather_wait_l1", _ici_plan)
        got = _gather_forward(landed[:1], "gather_forward_l1_in", shapes_1[:1])
        sibling["rest"] = _copies_start(landed[1:], "gather_forward_start_l1_rest", got[0], _sibling_plan)
        return dict(_matrix_weights(dict(zip(first, got))), started=sibling["rest"][3])

    def late_weights_1(t):
        f_ssem, f_rsem, f_views, _ = sibling["rest"]
        got = _copies_wait(f_ssem, f_rsem, f_views, t, "gather_forward_wait_l1_rest", _sibling_plan)
        return _matrix_weights({n: o.reshape(s) for n, o, s in zip(rest, got, shapes_1[1:])})

    (dh, loss_row), saved[1], wl[1] = _layer_fwd(h, small_1, "l1", soon=soon_weights_1, late=late_weights_1,
                                                 target=loss_target[0], hb=hb_1)

    def pair_stage(names, g, tag):
        views = [_halves(_to_chip_major(n, g[n])) for n in names]
        from_sibling = _pair_swap(views, f"grad_pair_swap_{tag}")
        return _pair_sum(views, from_sibling, f"pair_sum_{tag}")

    mine, other = [{}, {}], [{}, {}]

    def finish(names, l, started, after, tag):
        ssem, rsem, parts, landing, _ = started
        parts, arrived = _chip_exchange_wait(ssem, rsem, parts, landing, after, f"grad_chip_exchange_wait_{tag}")
        got = _chip_sum(parts, arrived, f"chip_sum_{tag}")
        mine[l].update(zip(names, got))
        other[l].update(zip(names, _pair_send(got, f"grad_pair_send_{tag}")))

    def small_pieces(g):
        return [_pad8(g[n][:3] if n == "conv_w" else g[n]) for n in SMALL]

    def start_small(l):
        pieces = small_pieces(grads[l]) + ([_pad8(loss_row)] if l == depth - 1 else [])
        return _all_to_all_small_start(jnp.concatenate(pieces, axis=0), f"small_grad_exchange_start_l{l}")

    grads, early, small = [None] * depth, {}, [None] * depth
    dh, grads[1] = _layer_bwd(dh, wl[1], saved[1], "l1")
    second = _chip_exchange_start(pair_stage(BIG, grads[1], "l1"), "grad_chip_exchange_start_l1")
    small[1] = start_small(1)

    def start_rest(g):
        early["rest"] = _chip_exchange_start(pair_stage(rest, g, "l0_rest"), "grad_chip_exchange_start_l0_rest")
        return early["rest"][4]

    def start_last(g):
        early["in"] = _chip_exchange_start(pair_stage(first, g, "l0_in"), "grad_chip_exchange_start_l0_in")
        return early["in"][4]

    dh, grads[0] = _layer_bwd(dh, wl[0], saved[0], "l0", after=[second[4], small[1][4]], mid=start_rest,
                              tail=start_last)
    small[0] = start_small(0)
    started = small[0][4]
    finish(BIG, 1, second, started, "l1")
    finish(rest, 0, early["rest"], started, "l0_rest")
    full = {}

    deltas, new_m, new_v = {}, {}, {}

    def update_matrices(names, tag):
        results = _adamw_halves(
            [weights[n] for n in names], [moms[n] for n in names], [vels[n] for n in names],
            [[mine[l][n] for n in names] for l in range(depth)], [[other[l][n] for n in names] for l in range(depth)],
            f"adamw_{tag}")
        for n, (g_, d_, m_, v_) in zip(names, results):
            full[n], deltas[n], new_m[n], new_v[n] = g_, d_, m_, v_

    update_matrices(rest, "rest")
    finish(first, 0, early["in"], deltas[rest[-1]], "l0_in")
    update_matrices(first, "in")
    summed = []
    for l in range(depth):
        ssem, rsem, part, landing, _ = small[l]
        summed.append(_sum_slices(_all_to_all_small_wait(ssem, rsem, part, landing, deltas[first[-1]],
                                                         f"small_grad_exchange_wait_l{l}"), f"small_sum_l{l}"))
    row = 0
    for n, piece in zip(SMALL, small_pieces(grads[0])):
        size = (weights[n].size if n != "conv_w" else depth * 3 * 512) // depth
        flat = jnp.stack([s[row:row + piece.shape[0]].reshape(-1)[:size] for s in summed])
        row += piece.shape[0]
        if n == "conv_w":
            full[n] = lax.dynamic_slice_in_dim(flat.reshape(depth, 3, 512), q * conv_w.shape[2], conv_w.shape[2], axis=2)
        else:
            full[n] = flat.reshape(weights[n].shape)
    loss = summed[depth - 1][row, 0]
    two_d = {n: (-1, weights[n].shape[-1]) if n not in ("conv_w", "q_gain", "k_gain") else (1, -1) for n in SMALL}
    results = _adamw(*[[group[n].reshape(two_d[n]) for n in SMALL] for group in (weights, full, moms, vels)],
                     "adamw_small")
    for n, (d2, m2, v2) in zip(SMALL, results):
        shape = weights[n].shape
        deltas[n], new_m[n], new_v[n] = d2.reshape(shape), m2.reshape(shape), v2.reshape(shape)
        full[n] = full[n].reshape(shape)
    for group in (full, deltas, new_m, new_v):
        group["w_in"] = jnp.swapaxes(group["w_in"], 1, 2)
    return (loss, dh[None], *[full[n] for n in ORDER], *[deltas[n] for n in ORDER], *[new_m[n] for n in ORDER],
            *[new_v[n] for n in ORDER])
```

```python
import functools

import jax
import jax.numpy as jnp
from jax import lax
from jax.experimental import pallas as pl
from jax.experimental.pallas import tpu as pltpu

F32 = jnp.float32
BF = jnp.bfloat16
MESH_ID = pl.DeviceIdType.MESH
ANY = pl.BlockSpec(memory_space=pl.ANY)

EPS = 1e-6
MASK_VALUE = -1e30
POOL_WINDOWS = (2, 4, 8, 16)
ATTN_DILATIONS = (1, 4, 16)
ATTN_BLOCK = 128
HEAD_DIM = 64
OFF_Q, OFF_K, OFF_V, OFF_GATE = 2048, 2816, 3584, 4352
N_CHIPS = 4
ADAM_LR, ADAM_B1, ADAM_B2, ADAM_EPS, ADAM_WD, ADAM_STEP = 0.001, 0.9, 0.999, 1e-08, 0.01, 10

VMEM_LIMIT = 48 * 1024 * 1024
LANES = 128

_DIMS = {"nn": (((1,), (0,)), ((), ())), "nt": (((1,), (1,)), ((), ())), "tn": (((0,), (0,)), ((), ()))}


def _params(sem):
    return pltpu.CompilerParams(dimension_semantics=sem, vmem_limit_bytes=VMEM_LIMIT)


def _pallas_call(body, **kw):
    def in_hbm(s):
        pin = isinstance(s, jax.ShapeDtypeStruct) and s is not TOKEN and jnp.issubdtype(s.dtype, jnp.floating)
        return pltpu.HBM(s.shape, s.dtype) if pin else s

    out_shape = kw.pop("out_shape")
    kw["out_shape"] = [in_hbm(s) for s in out_shape] if isinstance(out_shape, (list, tuple)) else in_hbm(out_shape)
    call = pl.pallas_call(body, **kw)

    def run(*args):
        pinned = [pltpu.with_memory_space_constraint(a, pltpu.HBM)
                  if hasattr(a, "dtype") and jnp.issubdtype(a.dtype, jnp.floating) else a for a in args]
        return call(*pinned)

    return run


def _dot(a, b, mode="nn"):
    return lax.dot_general(a, b, _DIMS[mode], preferred_element_type=F32)


def _mm(a, b, mode, name, *, tm, tn, tk, out_dtype=F32, res=None, aux=None, epi=None, n_outer=False,
        b_shards=False, out_shards=False, after=None, vec=None):
    if mode == "tn":
        K, M = a.shape
    else:
        M, K = a.shape
    if b_shards:
        if mode == "nn":
            assert b.shape[1] == K
            N = b.shape[2] * N_CHIPS
        else:
            assert mode == "nt"
            N = b.shape[1]
            assert b.shape[2] * N_CHIPS == K
    else:
        N = b.shape[0] if mode == "nt" else b.shape[1]
    tm, tn, tk = min(tm, M), min(tn, N), min(tk, K)
    assert M % tm == 0 and N % tn == 0 and K % tk == 0
    nk = K // tk
    if n_outer:
        grid = (N // tn, M // tm, nk)
        ij = lambda p, q_: (q_, p)
    else:
        grid = (M // tm, N // tn, nk)
        ij = lambda p, q_: (p, q_)

    def amap(p, q_, k):
        i, j = ij(p, q_)
        return (k, i) if mode == "tn" else (i, k)

    a_spec = pl.BlockSpec((tk, tm) if mode == "tn" else (tm, tk), amap)
    if b_shards:
        if mode == "nn":
            per = (N // N_CHIPS) // tn
            assert per >= 1 and (N // N_CHIPS) % tn == 0

            def bmap(p, q_, k):
                i, j = ij(p, q_)
                return (j // per, k, j % per)

            b_spec = pl.BlockSpec((None, tk, tn), bmap)
        else:
            per = (K // N_CHIPS) // tk
            assert per >= 1 and (K // N_CHIPS) % tk == 0

            def bmap(p, q_, k):
                i, j = ij(p, q_)
                return (k // per, j, k % per)

            b_spec = pl.BlockSpec((None, tn, tk), bmap)
    else:
        def bmap(p, q_, k):
            i, j = ij(p, q_)
            return (j, k) if mode == "nt" else (k, j)

        b_spec = pl.BlockSpec((tn, tk) if mode == "nt" else (tk, tn), bmap)

    def omap(p, q_, k):
        return ij(p, q_)

    o_spec = pl.BlockSpec((tm, tn), omap)
    if out_shards:
        per_o = (N // N_CHIPS) // tn
        assert per_o >= 1 and (N // N_CHIPS) % tn == 0

        def osmap(p, q_, k):
            i, j = ij(p, q_)
            return (j // per_o, i, j % per_o)

        out_spec0 = pl.BlockSpec((None, tm, tn), osmap)
        out_shape0 = jax.ShapeDtypeStruct((N_CHIPS, M, N // N_CHIPS), out_dtype)
    else:
        out_spec0 = o_spec
        out_shape0 = jax.ShapeDtypeStruct((M, N), out_dtype)

    in_specs = [a_spec, b_spec]
    args = [a, b]
    if res is not None:
        in_specs.append(o_spec)
        args.append(res)
    if aux is not None:
        in_specs.append(o_spec)
        args.append(aux)
    if vec is not None:
        in_specs.append(pl.BlockSpec((1, tn), lambda p, q_, k: (0, ij(p, q_)[1])))
        args.append(vec)
    after = [] if after is None else list(after) if isinstance(after, (list, tuple)) else [after]
    in_specs += [ANY] * len(after)
    args += after
    out_specs = [out_spec0]
    out_shape = [out_shape0]
    reduces = epi in ("loss", "rms_bwd")
    if reduces:
        assert tn == N and not n_outer and not out_shards
        width = LANES if epi == "loss" else N
        out_specs.append(pl.BlockSpec((1, width), lambda p, q_, k: (0, 0)))
        out_shape.append(jax.ShapeDtypeStruct((1, width), F32))
    if epi == "rms_next":
        assert tn == N and not out_shards
        out_specs.append(o_spec)
        out_shape.append(jax.ShapeDtypeStruct((M, N), BF))
    n_out = len(out_shape)
    has_res, has_aux, has_vec, n_after = res is not None, aux is not None, vec is not None, len(after)

    def body(*refs):
        a_ref, b_ref = refs[0], refs[1]
        pos = 2
        res_ref = aux_ref = vec_ref = None
        if has_res:
            res_ref = refs[pos]
            pos += 1
        if has_aux:
            aux_ref = refs[pos]
            pos += 1
        if has_vec:
            vec_ref = refs[pos]
            pos += 1
        pos += n_after
        outs = refs[pos:pos + n_out]
        part = _dot(a_ref[...].astype(BF), b_ref[...].astype(BF), mode)

        first_row_tile = pl.program_id(0) == 0

        def add_to_sum(row):
            @pl.when(first_row_tile)
            def _():
                outs[1][...] = jnp.zeros_like(outs[1])

            outs[1][...] += row

        def finish(acc):
            if epi == "rms_bwd":
                xv = aux_ref[...]
                r = lax.rsqrt(jnp.mean(xv * xv, axis=-1, keepdims=True) + EPS)
                xhat = xv * r
                dy = acc * vec_ref[...]
                outs[0][...] = res_ref[...] + r * (dy - xhat * jnp.mean(dy * xhat, axis=-1, keepdims=True))
                add_to_sum(jnp.sum(acc * xhat, axis=0, keepdims=True))
                return
            if res_ref is not None:
                acc = res_ref[...] + acc
            if epi == "relu2":
                r = jnp.maximum(acc, 0.0)
                outs[0][...] = (r * r).astype(out_dtype)
            elif epi == "drelu2":
                outs[0][...] = (acc.astype(BF) * (2.0 * jnp.sqrt(aux_ref[...]))).astype(out_dtype)
            elif epi == "rms_next":
                outs[0][...] = acc
                r = lax.rsqrt(jnp.mean(acc * acc, axis=-1, keepdims=True) + EPS)
                outs[1][...] = ((acc * r) * vec_ref[...]).astype(BF)
            elif epi == "loss":
                e = acc - aux_ref[...]
                outs[0][...] = e / float(N)
                add_to_sum(0.5 * jnp.sum(jnp.mean(e * e, axis=-1, keepdims=True)))
            else:
                outs[0][...] = acc.astype(out_dtype)

        if nk == 1:
            finish(part)
        else:
            acc_ref = refs[pos + n_out]
            k = pl.program_id(2)

            @pl.when(k == 0)
            def _():
                acc_ref[...] = part

            @pl.when(k > 0)
            def _():
                acc_ref[...] += part

            @pl.when(k == nk - 1)
            def _():
                finish(acc_ref[...])

    scratch = [pltpu.VMEM((tm, tn), F32)] if nk > 1 else []
    out = _pallas_call(
        body, name=name, grid=grid, in_specs=in_specs, out_specs=out_specs, out_shape=out_shape,
        scratch_shapes=scratch,
        compiler_params=_params(("arbitrary" if reduces else "parallel", "parallel", "arbitrary")),
    )(*args)
    return out if n_out > 1 else out[0]


def _rms_fwd(x, gain, name, after=None):
    T, D = x.shape
    tm = min(512, T)

    def body(x_ref, g_ref, *rest):
        o_ref = rest[-1]
        xv = x_ref[...]
        r = lax.rsqrt(jnp.mean(xv * xv, axis=-1, keepdims=True) + EPS)
        o_ref[...] = ((xv * r) * g_ref[...]).astype(BF)

    extra = [] if after is None else list(after) if isinstance(after, (list, tuple)) else [after]
    return _pallas_call(
        body, name=name, grid=(T // tm,),
        in_specs=[pl.BlockSpec((tm, D), lambda i: (i, 0)), pl.BlockSpec((1, D), lambda i: (0, 0))] + [ANY] * len(extra),
        out_specs=pl.BlockSpec((tm, D), lambda i: (i, 0)), out_shape=jax.ShapeDtypeStruct((T, D), BF),
        compiler_params=_params(("parallel",)),
    )(x, gain, *extra)


def _rms_bwd(dh, x, gain, dres, name):
    T, D = x.shape
    tm = min(512, T)

    def body(dh_ref, x_ref, g_ref, dres_ref, dx_ref, dg_ref):
        xv = x_ref[...]
        r = lax.rsqrt(jnp.mean(xv * xv, axis=-1, keepdims=True) + EPS)
        xhat = xv * r
        dhv = dh_ref[...]
        dy = dhv * g_ref[...]
        dx_ref[...] = dres_ref[...] + r * (dy - xhat * jnp.mean(dy * xhat, axis=-1, keepdims=True))

        @pl.when(pl.program_id(0) == 0)
        def _():
            dg_ref[...] = jnp.zeros_like(dg_ref)

        dg_ref[...] += jnp.sum(dhv * xhat, axis=0, keepdims=True)

    row = pl.BlockSpec((tm, D), lambda i: (i, 0))
    vec = pl.BlockSpec((1, D), lambda i: (0, 0))
    return _pallas_call(
        body, name=name, grid=(T // tm,), in_specs=[row, row, vec, row], out_specs=[row, vec],
        out_shape=[jax.ShapeDtypeStruct((T, D), F32), jax.ShapeDtypeStruct((1, D), F32)],
        compiler_params=_params(("arbitrary",)),
    )(dh, x, gain, dres)


POOL_HALO = 16
CONV_HALO = 8
POOLCONV_ROWS = 512


def _causal_window_sum(v, w):
    s, sh = v, 1
    while sh < w:
        s = s + pltpu.roll(s, sh, 0)
        sh *= 2
    return s


def _anticausal_window_sum(v, w):
    n = v.shape[0]
    s, sh = v, 1
    while sh < w:
        s = s + pltpu.roll(s, n - sh, 0)
        sh *= 2
    return s


def _poolconv_fwd(z, pmix_b, pscale, convw, name):
    T = z.shape[0]
    R = min(POOLCONV_ROWS, T)
    PH, CH = R // POOL_HALO, R // CONV_HALO

    def body(u_ref, uh_ref, b_ref, c_ref, ch_ref, x_ref, xh_ref, mix_ref, sc_ref, cw_ref, yp_ref, yc_ref):
        i = pl.program_id(0)
        keep = (i > 0).astype(F32)
        row = i * R + lax.broadcasted_iota(jnp.int32, (R, 1), 0)
        w_all = jnp.concatenate([uh_ref[...] * keep, u_ref[...]], axis=0)
        for g, w in enumerate(POOL_WINDOWS):
            cols = slice(128 * g, 128 * (g + 1))
            wg = w_all[:, cols]
            s = _causal_window_sum(wg, w)[POOL_HALO:]
            inv_cnt = 1.0 / jnp.minimum(row + 1, w).astype(F32)
            dgrp = s * inv_cnt - wg[POOL_HALO:]
            y = _dot(dgrp.astype(BF), mix_ref[g]) * sc_ref[:, cols]
            yp_ref[:, cols] = y.astype(BF)
        uc = jnp.concatenate([ch_ref[...] * xh_ref[...] * keep, c_ref[...] * x_ref[...]], axis=0)
        yc = cw_ref[2:3, :] * uc + cw_ref[0:1, :] * pltpu.roll(uc, 2, 0) + cw_ref[1:2, :] * pltpu.roll(uc, 1, 0)
        yc_ref[...] = (b_ref[...] * yc[CONV_HALO:]).astype(BF)

    def main(cb):
        return pl.BlockSpec((R, 512), lambda i: (i, cb))

    def prev(cb, halo, per):
        return pl.BlockSpec((halo, 512), lambda i: (jnp.maximum(i * per - 1, 0), cb))

    full = lambda a: pl.BlockSpec(a.shape, lambda i: (0,) * a.ndim)
    return _pallas_call(
        body, name=name, grid=(T // R,),
        in_specs=[main(0), prev(0, POOL_HALO, PH), main(1), main(2), prev(2, CONV_HALO, CH), main(3),
                  prev(3, CONV_HALO, CH), full(pmix_b), full(pscale), full(convw)],
        out_specs=[pl.BlockSpec((R, 512), lambda i: (i, 0))] * 2,
        out_shape=[jax.ShapeDtypeStruct((T, 512), BF)] * 2,
        compiler_params=_params(("parallel",)),
    )(z, z, z, z, z, z, z, pmix_b, pscale, convw)


def _poolconv_bwd(z, dyp, dyc, pmix_b, pscale, convw, dz, name):
    T = z.shape[0]
    R = min(POOLCONV_ROWS, T)
    PH, CH = R // POOL_HALO, R // CONV_HALO
    nsteps = T // R

    def body(u_ref, uh_ref, b_ref, bn_ref, c_ref, ch_ref, x_ref, xh_ref, dyp_ref, dypn_ref, dyc_ref, dycn_ref,
             mix_ref, sc_ref, cw_ref, dz_in_ref, dz_ref, dmix_ref, dsc_ref, dcw_ref):
        i = pl.program_id(0)
        keep_prev = (i > 0).astype(F32)
        keep_next = (i < nsteps - 1).astype(F32)

        @pl.when(i == 0)
        def _():
            dmix_ref[...] = jnp.zeros_like(dmix_ref)
            dsc_ref[...] = jnp.zeros_like(dsc_ref)
            dcw_ref[...] = jnp.zeros_like(dcw_ref)

        row = i * R + lax.broadcasted_iota(jnp.int32, (R, 1), 0)
        row_ext = i * R + lax.broadcasted_iota(jnp.int32, (R + POOL_HALO, 1), 0)
        w_all = jnp.concatenate([uh_ref[...] * keep_prev, u_ref[...]], axis=0)
        dyp_ext = jnp.concatenate([dyp_ref[...], dypn_ref[...] * keep_next], axis=0)
        for g, w in enumerate(POOL_WINDOWS):
            cols = slice(128 * g, 128 * (g + 1))
            wg = w_all[:, cols]
            s = _causal_window_sum(wg, w)[POOL_HALO:]
            inv_cnt = 1.0 / jnp.minimum(row + 1, w).astype(F32)
            dgrp = (s * inv_cnt - wg[POOL_HALO:]).astype(BF)
            y_pre = _dot(dgrp, mix_ref[g])
            dsc_ref[:, cols] += jnp.sum(dyp_ref[:, cols] * y_pre, axis=0, keepdims=True)
            dyb = (dyp_ext[:, cols] * sc_ref[:, cols]).astype(BF)
            dmix_ref[cols, :] += _dot(dgrp, dyb[:R], "tn")
            dd = _dot(dyb, mix_ref[g], "nt")
            inv_cnt_ext = 1.0 / jnp.minimum(row_ext + 1, w).astype(F32)
            e = _anticausal_window_sum(dd * inv_cnt_ext, w)
            dz_ref[:, cols] = (e[:R] - dd[:R]).astype(BF)
        cw0, cw1, cw2 = cw_ref[0:1, :], cw_ref[1:2, :], cw_ref[2:3, :]
        uc = jnp.concatenate([ch_ref[...] * xh_ref[...] * keep_prev, c_ref[...] * x_ref[...]], axis=0)
        uc1 = pltpu.roll(uc, 1, 0)[CONV_HALO:]
        uc2 = pltpu.roll(uc, 2, 0)[CONV_HALO:]
        uc0 = uc[CONV_HALO:]
        yc = cw2 * uc0 + cw0 * uc2 + cw1 * uc1
        dycv = dyc_ref[...]
        dz_ref[:, 512:1024] = (dycv * yc).astype(BF)
        dv_ext = jnp.concatenate([dycv * b_ref[...], dycn_ref[...] * bn_ref[...] * keep_next], axis=0)
        n_ext = R + CONV_HALO
        duc = (cw2 * dv_ext + cw1 * pltpu.roll(dv_ext, n_ext - 1, 0) + cw0 * pltpu.roll(dv_ext, n_ext - 2, 0))[:R]
        dv = dv_ext[:R]
        dcw_ref[0:1, :] += jnp.sum(dv * uc2, axis=0, keepdims=True)
        dcw_ref[1:2, :] += jnp.sum(dv * uc1, axis=0, keepdims=True)
        dcw_ref[2:3, :] += jnp.sum(dv * uc0, axis=0, keepdims=True)
        dz_ref[:, 1024:1536] = (duc * x_ref[...]).astype(BF)
        dz_ref[:, 1536:2048] = (duc * c_ref[...]).astype(BF)

    def main(cb):
        return pl.BlockSpec((R, 512), lambda i: (i, cb))

    def prev(cb, halo, per):
        return pl.BlockSpec((halo, 512), lambda i: (jnp.maximum(i * per - 1, 0), cb))

    def nxt(cb, halo, per):
        return pl.BlockSpec((halo, 512), lambda i: (jnp.minimum((i + 1) * per, T // halo - 1), cb))

    full = lambda a: pl.BlockSpec(a.shape, lambda i: (0,) * a.ndim)
    return _pallas_call(
        body, name=name, grid=(nsteps,),
        in_specs=[main(0), prev(0, POOL_HALO, PH), main(1), nxt(1, CONV_HALO, CH), main(2), prev(2, CONV_HALO, CH),
                  main(3), prev(3, CONV_HALO, CH), main(0), nxt(0, POOL_HALO, PH), main(0), nxt(0, CONV_HALO, CH),
                  full(pmix_b), full(pscale), full(convw), ANY],
        out_specs=[pl.BlockSpec((R, 2048), lambda i: (i, 0)), pl.BlockSpec((512, 128), lambda i: (0, 0)),
                   pl.BlockSpec((1, 512), lambda i: (0, 0)), pl.BlockSpec((8, 512), lambda i: (0, 0))],
        out_shape=[jax.ShapeDtypeStruct(dz.shape, BF), jax.ShapeDtypeStruct((512, 128), F32),
                   jax.ShapeDtypeStruct((1, 512), F32), jax.ShapeDtypeStruct((8, 512), F32)],
        input_output_aliases={15: 0}, compiler_params=_params(("arbitrary",)),
    )(z, z, z, z, z, z, z, z, dyp, dyp, dyc, dyc, pmix_b, pscale, convw, dz)


def _head_sums(v):
    row = lax.broadcasted_iota(jnp.int32, (LANES, LANES), 0) < HEAD_DIM
    col = lax.broadcasted_iota(jnp.int32, (LANES, LANES), 1) < HEAD_DIM
    same_head = jnp.where(jnp.logical_xor(row, col), 0.0, 1.0).astype(BF)
    hi = v.astype(BF)
    lo = (v - hi.astype(F32)).astype(BF)
    return _dot(hi, same_head) + _dot(lo, same_head)


def _head_norm(x, g2, ma):
    r = lax.rsqrt(_head_sums(x * x) / HEAD_DIM + EPS)
    return x * r, r


def _head_norm_bwd(dy, xhat, r, g2, ma):
    dxh = dy * g2
    return r * (dxh - xhat * (_head_sums(dxh * xhat) / HEAD_DIM))


def _attn_masks(other_block_exists):
    lane = lax.broadcasted_iota(jnp.int32, (2 * ATTN_BLOCK, ATTN_BLOCK), 1)
    qi = lax.broadcasted_iota(jnp.int32, (2 * ATTN_BLOCK, ATTN_BLOCK), 0) & (ATTN_BLOCK - 1)
    never = (1 - other_block_exists.astype(jnp.int32)) * (2 * ATTN_BLOCK)
    return lane[:ATTN_BLOCK] < HEAD_DIM, lane <= qi, lane >= qi + never


def _stack_heads(x, ma):
    return jnp.concatenate([jnp.where(ma, x, 0.0), jnp.where(ma, 0.0, x)], axis=0)


def _unstack_heads(y, ma):
    return jnp.where(ma, y[:ATTN_BLOCK], y[ATTN_BLOCK:])


def _stack_cols(tile, ma):
    return jnp.concatenate([tile[:, 0:1], tile[:, HEAD_DIM:HEAD_DIM + 1]], axis=0)


QKV_TILES = (OFF_GATE - OFF_Q) // LANES
KIND_TILES = QKV_TILES // 3


def _qk_norm(z, gains, name):
    T = z.shape[0]
    tm = min(512, T)

    def body(x_ref, g_ref, o_ref):
        ma = lax.broadcasted_iota(jnp.int32, (tm, LANES), 1) < HEAD_DIM
        for tile in range(QKV_TILES):
            v = x_ref[:, LANES * tile:LANES * (tile + 1)]
            if tile < 2 * KIND_TILES:
                g = g_ref[0:1, :] if tile < KIND_TILES else g_ref[1:2, :]
                v = _head_norm(v, g, ma)[0] * g
            o_ref[tile] = v

    return _pallas_call(
        body, name=name, grid=(T // tm,),
        in_specs=[pl.BlockSpec((pl.Element(tm), pl.Element(OFF_GATE - OFF_Q)), lambda i: (i * tm, OFF_Q)),
                  pl.BlockSpec((8, LANES), lambda i: (0, 0))],
        out_specs=pl.BlockSpec((QKV_TILES, tm, LANES), lambda i: (0, i, 0)),
        out_shape=jax.ShapeDtypeStruct((QKV_TILES, T, LANES), F32), compiler_params=_params(("parallel",)),
    )(z, gains)


ATTN_STEP_ROWS = 2048
ATTN_UNROLL = 4


def _attn_steps(T):
    assert ATTN_STEP_ROWS == ATTN_BLOCK * max(ATTN_DILATIONS) and T % ATTN_STEP_ROWS == 0
    return T // ATTN_STEP_ROWS


def _attn_rows(jj, r, sub, d):
    start = jj * sub + r
    if d == 1:
        return pl.ds(pl.multiple_of(start, ATTN_BLOCK), ATTN_BLOCK)
    return pl.ds(start, ATTN_BLOCK, stride=d)


def _pick(flag, a, b):
    return jnp.where(jnp.full(a.shape, flag.astype(jnp.int32)) > 0, a, b)


def _attn_fwd(qkv, name):
    T = qkv.shape[1]
    nbig = _attn_steps(T)
    scale = HEAD_DIM ** -0.5

    def body(q_ref, kc_ref, kp_ref, vc_ref, vp_ref, o_ref, lse_ref):
        jb = pl.program_id(1)
        for gi, d in enumerate(ATTN_DILATIONS):
            pl.when(pl.program_id(0) == gi)(functools.partial(group, d, jb, q_ref, kc_ref, kp_ref, vc_ref, vp_ref,
                                                              o_ref, lse_ref))

    def group(d, jb, q_ref, kc_ref, kp_ref, vc_ref, vp_ref, o_ref, lse_ref):
        sub, m = ATTN_BLOCK * d, ATTN_STEP_ROWS // (ATTN_BLOCK * d)

        def step(s, carry):
            jj, r = s // d, s % d
            here, before = _attn_rows(jj, r, sub, d), _attn_rows(jnp.maximum(jj - 1, 0), r, sub, d)
            edge = _attn_rows(m - 1, r, sub, d)
            first = jj == 0
            ma, mask_c, mask_p = _attn_masks(jb * m + jj > 0)
            qs = _stack_heads(q_ref[here, :], ma).astype(BF)
            kcb = kc_ref[here, :].astype(BF)
            kpb = _pick(first, kp_ref[edge, :], kc_ref[before, :]).astype(BF)
            vcb = vc_ref[here, :].astype(BF)
            vpb = _pick(first, vp_ref[edge, :], vc_ref[before, :]).astype(BF)
            s_c = jnp.where(mask_c, _dot(qs, kcb, "nt") * scale, MASK_VALUE)
            s_p = jnp.where(mask_p, _dot(qs, kpb, "nt") * scale, MASK_VALUE)
            mx = jnp.maximum(jnp.max(s_c, axis=-1, keepdims=True), jnp.max(s_p, axis=-1, keepdims=True))
            p_c = jnp.exp(s_c - mx)
            p_p = jnp.exp(s_p - mx)
            den = jnp.sum(p_c, axis=-1, keepdims=True) + jnp.sum(p_p, axis=-1, keepdims=True)
            o = (_dot(p_c.astype(BF), vcb) + _dot(p_p.astype(BF), vpb)) / den
            o_ref[here, :] = _unstack_heads(o, ma)
            lse_ref[here, :] = _unstack_heads(jnp.broadcast_to(mx + jnp.log(den), o.shape), ma)
            return carry

        lax.fori_loop(0, m * d, step, 0, unroll=ATTN_UNROLL)

    def cur(kind):
        return pl.BlockSpec((None, ATTN_STEP_ROWS, LANES), lambda g, j, t: (KIND_TILES * kind + 2 * g + t, j, 0))

    def prv(kind):
        return pl.BlockSpec((None, ATTN_STEP_ROWS, LANES),
                            lambda g, j, t: (KIND_TILES * kind + 2 * g + t, jnp.maximum(j - 1, 0), 0))

    out = pl.BlockSpec((ATTN_STEP_ROWS, LANES), lambda g, j, t: (j, 2 * g + t))
    width = 2 * LANES * len(ATTN_DILATIONS)
    return _pallas_call(
        body, name=name, grid=(len(ATTN_DILATIONS), nbig, 2), in_specs=[cur(0), cur(1), prv(1), cur(2), prv(2)],
        out_specs=[out, out], out_shape=[jax.ShapeDtypeStruct((T, width), F32)] * 2,
        compiler_params=_params(("parallel", "parallel", "parallel")),
    )(qkv, qkv, qkv, qkv, qkv)


def _attn_bwd(z, qkv, do, c, lse, gains, name, after=None):
    T = z.shape[0]
    nbig = _attn_steps(T)
    scale = HEAD_DIM ** -0.5
    extra = [] if after is None else [after]

    def body(*refs):
        g, jb = pl.program_id(0), pl.program_id(1)
        dgq_ref, dgk_ref = refs[len(refs) - 5], refs[len(refs) - 4]

        @pl.when((g == 0) & (jb == 0) & (pl.program_id(2) == 0))
        def _():
            dgq_ref[...] = jnp.zeros_like(dgq_ref)
            dgk_ref[...] = jnp.zeros_like(dgk_ref)

        for gi, d in enumerate(ATTN_DILATIONS):
            pl.when(g == gi)(functools.partial(group, d, jb, *refs))

    def group(d, jb, qr_ref, kr_ref, vc_ref, vp_ref, qn_ref, qnn_ref, kn_ref, knp_ref, do_ref, don_ref, c_ref, cn_ref,
              lse_ref, lsen_ref, g_ref, *rest):
        dq_ref, dk_ref, dv_ref, dgq_ref, dgk_ref, sq_ref, sk_ref, sv_ref = rest[len(extra):]
        sub, m = ATTN_BLOCK * d, ATTN_STEP_ROWS // (ATTN_BLOCK * d)
        nb = T // sub
        gq, gk = g_ref[0:1, :], g_ref[1:2, :]

        def step(s, carry):
            jj, r = s // d, s % d
            here = _attn_rows(jj, r, sub, d)
            before = _attn_rows(jnp.maximum(jj - 1, 0), r, sub, d)
            behind = _attn_rows(jnp.minimum(jj + 1, m - 1), r, sub, d)
            edge_before, edge_behind = _attn_rows(m - 1, r, sub, d), _attn_rows(0, r, sub, d)
            first, last = jj == 0, jj == m - 1
            block = jb * m + jj
            ma, mask_c, mask_p = _attn_masks(block > 0)
            mask_n = _attn_masks(block < nb - 1)[2]
            qhat, rq = _head_norm(qr_ref[here, :], gq, ma)
            qn = qn_ref[here, :]
            qn_next = _pick(last, qnn_ref[edge_behind, :], qn_ref[behind, :])
            khat, rk = _head_norm(kr_ref[here, :], gk, ma)
            kcb = kn_ref[here, :].astype(BF)
            kpb = _pick(first, knp_ref[edge_before, :], kn_ref[before, :]).astype(BF)
            vcb = vc_ref[here, :].astype(BF)
            vpb = _pick(first, vp_ref[edge_before, :], vc_ref[before, :]).astype(BF)
            do_t, don_t = do_ref[here, :], _pick(last, don_ref[edge_behind, :], do_ref[behind, :])
            c_t, cn_t = c_ref[here, :], _pick(last, cn_ref[edge_behind, :], c_ref[behind, :])
            lse_t, lsen_t = lse_ref[here, :], _pick(last, lsen_ref[edge_behind, :], lse_ref[behind, :])
            qs, dos = _stack_heads(qn, ma).astype(BF), _stack_heads(do_t, ma).astype(BF)
            lse_s, c_s = _stack_cols(lse_t, ma), _stack_cols(c_t, ma)
            s_c = jnp.where(mask_c, _dot(qs, kcb, "nt") * scale, MASK_VALUE)
            s_p = jnp.where(mask_p, _dot(qs, kpb, "nt") * scale, MASK_VALUE)
            p_c = jnp.exp(s_c - lse_s)
            p_p = jnp.exp(s_p - lse_s)
            ds_c = ((p_c * (_dot(dos, vcb, "nt") + c_s)) * scale).astype(BF)
            ds_p = ((p_p * (_dot(dos, vpb, "nt") + c_s)) * scale).astype(BF)
            dq_t = _unstack_heads(_dot(ds_c, kcb) + _dot(ds_p, kpb), ma)
            qs_n, dos_n = _stack_heads(qn_next, ma).astype(BF), _stack_heads(don_t, ma).astype(BF)
            s_n = jnp.where(mask_n, _dot(qs_n, kcb, "nt") * scale, MASK_VALUE)
            p_n = jnp.exp(s_n - _stack_cols(lsen_t, ma))
            ds_n = ((p_n * (_dot(dos_n, vcb, "nt") + _stack_cols(cn_t, ma))) * scale).astype(BF)
            dv_t = _dot(p_c.astype(BF), dos, "tn") + _dot(p_n.astype(BF), dos_n, "tn")
            dk_t = _dot(ds_c, qs, "tn") + _dot(ds_n, qs_n, "tn")
            sq_ref[here, :] = _head_norm_bwd(dq_t, qhat, rq, gq, ma)
            sk_ref[here, :] = _head_norm_bwd(dk_t, khat, rk, gk, ma)
            sv_ref[here, :] = dv_t
            dgq_ref[...] += jnp.sum(dq_t * qhat, axis=0, keepdims=True)
            dgk_ref[...] += jnp.sum(dk_t * khat, axis=0, keepdims=True)
            return carry

        lax.fori_loop(0, m * d, step, 0, unroll=ATTN_UNROLL)
        dq_ref[...] = sq_ref[...].astype(BF)
        dk_ref[...] = sk_ref[...].astype(BF)
        dv_ref[...] = sv_ref[...].astype(BF)

    rows = ATTN_STEP_ROWS

    def raw(col0):
        return pl.BlockSpec((rows, LANES), lambda g, j, t: (j, col0 + 2 * g + t))

    def cur(kind):
        return pl.BlockSpec((None, rows, LANES), lambda g, j, t: (KIND_TILES * kind + 2 * g + t, j, 0))

    def prv(kind):
        return pl.BlockSpec((None, rows, LANES), lambda g, j, t: (KIND_TILES * kind + 2 * g + t, jnp.maximum(j - 1, 0), 0))

    def nxt(kind):
        return pl.BlockSpec((None, rows, LANES),
                            lambda g, j, t: (KIND_TILES * kind + 2 * g + t, jnp.minimum(j + 1, nbig - 1), 0))

    own = pl.BlockSpec((rows, LANES), lambda g, j, t: (j, 2 * g + t))
    own_next = pl.BlockSpec((rows, LANES), lambda g, j, t: (jnp.minimum(j + 1, nbig - 1), 2 * g + t))
    vec = pl.BlockSpec((1, LANES), lambda g, j, t: (0, 0))
    width = 2 * LANES * len(ATTN_DILATIONS)
    return _pallas_call(
        body, name=name, grid=(len(ATTN_DILATIONS), nbig, 2),
        in_specs=[raw(OFF_Q // LANES), raw(OFF_K // LANES), cur(2), prv(2), cur(0), nxt(0), cur(1), prv(1), own, own_next,
                  own, own_next, own, own_next, pl.BlockSpec((8, LANES), lambda g, j, t: (0, 0))] + [ANY] * len(extra),
        out_specs=[own, own, own, vec, vec],
        out_shape=[jax.ShapeDtypeStruct((T, width), BF)] * 3 + [jax.ShapeDtypeStruct((1, LANES), F32)] * 2,
        scratch_shapes=[pltpu.VMEM((rows, LANES), F32)] * 3,
        compiler_params=_params(("arbitrary", "arbitrary", "arbitrary")),
    )(z, z, qkv, qkv, qkv, qkv, qkv, qkv, do, do, c, c, lse, lse, gains, *extra)


MERGE_ROWS = 256
GATE_TILE = 256


def _group_mix(o_refs, lse_refs):
    lses = [r[...] for r in lse_refs]
    m = jnp.maximum(jnp.maximum(lses[0], lses[1]), lses[2])
    es = [jnp.exp(l - m) for l in lses]
    den = es[0] + es[1] + es[2]
    ws = [e / den for e in es]
    y = ws[0] * o_refs[0][...] + ws[1] * o_refs[1][...] + ws[2] * o_refs[2][...]
    return ws, y


def _sigmoid(v):
    return 1.0 / (1.0 + jnp.exp(-v))


def _merge_specs(T, z, bgate, gpu, gco, gau):
    tm = min(MERGE_ROWS, T)
    row = lambda w: pl.BlockSpec((tm, w), lambda i: (i, 0))
    gate0 = OFF_GATE // GATE_TILE
    gates = [pl.BlockSpec((tm, GATE_TILE), functools.partial(lambda i, cb: (i, cb), cb=gate0 + n))
             for n in range(3 * N_CHIPS)]
    full = lambda a: pl.BlockSpec(a.shape, lambda i: (0,) * a.ndim)
    by_group = [pl.BlockSpec((tm, 256), functools.partial(lambda i, g: (i, g), g=g)) for g in range(3)]
    specs = [row(512), row(512)] + by_group * 2 + gates + [full(bgate), full(gpu), full(gco), full(gau)]
    return tm, row, specs


def _merge_fwd(yp, yc, o3, lse3, z, bgate, gpu, gco, gau, name):
    T = yp.shape[0]
    tm, row, specs = _merge_specs(T, z, bgate, gpu, gco, gau)

    def body(*refs):
        yp_ref, yc_ref = refs[0], refs[1]
        o_refs, lse_refs = refs[2:5], refs[5:8]
        zg = refs[8:20]
        b_ref, gpu_ref, gco_ref, gau_ref, out_ref = refs[20:25]
        yab = _group_mix(o_refs, lse_refs)[1].astype(BF)
        ys = (yp_ref[...], yc_ref[...], yab)
        ups = (gpu_ref, gco_ref, gau_ref)
        for n in range(N_CHIPS):
            acc = None
            for b in range(3):
                gcol = slice(1024 * b + GATE_TILE * n, 1024 * b + GATE_TILE * (n + 1))
                gate = _sigmoid(zg[N_CHIPS * b + n][...] + b_ref[:, gcol])
                term = gate * _dot(ys[b], ups[b][n])
                acc = term if acc is None else acc + term
            out_ref[:, GATE_TILE * n:GATE_TILE * (n + 1)] = acc.astype(BF)

    return _pallas_call(
        body, name=name, grid=(T // tm,), in_specs=specs, out_specs=row(1024),
        out_shape=jax.ShapeDtypeStruct((T, 1024), BF), compiler_params=_params(("parallel",)),
    )(yp, yc, *([o3] * 3), *([lse3] * 3), *([z] * 12), bgate, gpu, gco, gau)


def _merge_bwd(dm, yp, yc, o3, lse3, z, bgate, gpu, gco, gau, name):
    T = yp.shape[0]
    tm, row, specs = _merge_specs(T, z, bgate, gpu, gco, gau)
    nsteps = T // tm

    def body(*refs):
        dm_ref, yp_ref, yc_ref = refs[0:3]
        o_refs, lse_refs = refs[3:6], refs[6:9]
        zg = refs[9:21]
        b_ref, gpu_ref, gco_ref, gau_ref = refs[21:25]
        dzg_ref, dyp_ref, dyc_ref = refs[25:28]
        do_ref, c_ref = refs[28:30]
        dgpu_ref, dgco_ref, dgau_ref, dbg_ref = refs[30:34]
        accs = refs[34:37]
        i = pl.program_id(0)

        @pl.when(i == 0)
        def _():
            for a in accs:
                a[...] = jnp.zeros_like(a)
            dbg_ref[...] = jnp.zeros_like(dbg_ref)

        ws, y = _group_mix(o_refs, lse_refs)
        ys = (yp_ref[...], yc_ref[...], y.astype(BF))
        ups = (gpu_ref, gco_ref, gau_ref)
        dys = [None, None, None]
        for n in range(N_CHIPS):
            dmn = dm_ref[:, GATE_TILE * n:GATE_TILE * (n + 1)]
            for b in range(3):
                gcol = slice(1024 * b + GATE_TILE * n, 1024 * b + GATE_TILE * (n + 1))
                gate = _sigmoid(zg[N_CHIPS * b + n][...] + b_ref[:, gcol])
                up = _dot(ys[b], ups[b][n])
                dzg = (dmn * up) * (gate * (1.0 - gate))
                dzg_ref[:, gcol] = dzg.astype(BF)
                dbg_ref[:, gcol] += jnp.sum(dzg, axis=0, keepdims=True)
                dup = (dmn * gate).astype(BF)
                accs[b][n] += _dot(ys[b], dup, "tn")
                dyb = _dot(dup, ups[b][n], "nt")
                dys[b] = dyb if dys[b] is None else dys[b] + dyb
        dyp_ref[...] = dys[0]
        dyc_ref[...] = dys[1]
        dya = dys[2]
        lane = lax.broadcasted_iota(jnp.int32, dya.shape, 1) // HEAD_DIM
        pr = dya * y
        rho = jnp.zeros_like(pr)
        for h in range(256 // HEAD_DIM):
            hm = lane == h
            rho = jnp.where(hm, jnp.sum(jnp.where(hm, pr, 0.0), axis=-1, keepdims=True), rho)
        for g in range(3):
            do_ref[:, 256 * g:256 * (g + 1)] = ws[g] * dya
            c_ref[:, 256 * g:256 * (g + 1)] = -(ws[g] * rho)

        @pl.when(i == nsteps - 1)
        def _():
            dgpu_ref[...] = accs[0][...].astype(BF)
            dgco_ref[...] = accs[1][...].astype(BF)
            dgau_ref[...] = accs[2][...].astype(BF)

    full = lambda a: pl.BlockSpec(a.shape, lambda i: (0,) * a.ndim)
    dz_gate = pl.BlockSpec((pl.Element(tm), pl.Element(3072)), lambda i: (i * tm, OFF_GATE))
    out_specs = ([dz_gate, row(512), row(512)] + [row(768)] * 2 + [full(gpu), full(gco), full(gau)]
                 + [pl.BlockSpec((1, 3072), lambda i: (0, 0))])
    out_shape = ([jax.ShapeDtypeStruct(z.shape, BF)] + [jax.ShapeDtypeStruct((T, 512), F32)] * 2
                 + [jax.ShapeDtypeStruct((T, 768), F32)] * 2
                 + [jax.ShapeDtypeStruct(g.shape, BF) for g in (gpu, gco, gau)]
                 + [jax.ShapeDtypeStruct((1, 3072), F32)])
    return _pallas_call(
        body, name=name, grid=(nsteps,), in_specs=[row(1024)] + specs, out_specs=out_specs, out_shape=out_shape,
        scratch_shapes=[pltpu.VMEM(g.shape, F32) for g in (gpu, gco, gau)],
        compiler_params=_params(("arbitrary",)),
    )(dm, yp, yc, *([o3] * 3), *([lse3] * 3), *([z] * 12), bgate, gpu, gco, gau)


def _layer_fwd(x, w, tag, after=None, soon=None, late=None, last=None, target=None, hb=None, next_gain=None):
    if hb is None:
        hb = _rms_fwd(x, w["norm_mix"], f"rms_mix_{tag}", after=after)
    if soon is not None:
        w = dict(w, **soon(hb))
    z = _mm(hb, w["w_in"], "nt", f"in_proj_{tag}", tm=512, tn=3712, tk=1024, n_outer=True, after=w.get("started"))
    yp, yc = _poolconv_fwd(z, w["pool_mix"], w["pool_scale"], w["conv_w"], f"poolconv_{tag}")
    qkv = _qk_norm(z, w["qk_gain"], f"qk_norm_{tag}")
    o3, lse3 = _attn_fwd(qkv, f"attn_{tag}")
    if late is not None:
        w = dict(w, **late(lse3))
    merged = _merge_fwd(yp, yc, o3, lse3, z, w["b_gate"], w["w_pool_up"], w["w_conv_out"], w["w_attn_up"],
                        f"merge_{tag}")
    x1, h2b = _mm(merged, w["w_o"], "nn", f"out_proj_{tag}", tm=1024, tn=1024, tk=1024, res=x, vec=w["norm_mlp"],
                  epi="rms_next", after=w.get("crossing"))
    if last is not None:
        w = dict(w, **last(x1))
    rb = _mm(h2b, w["w_ff1"], "nn", f"ff1_{tag}", tm=1024, tn=1024, tk=1024, out_dtype=BF, epi="relu2", n_outer=True,
             b_shards=True)
    if target is not None:
        x2 = _mm(rb, w["w_ff2"], "nn", f"ff2_{tag}", tm=512, tn=1024, tk=4096, res=x1, aux=target, epi="loss")
    elif next_gain is not None:
        x2 = _mm(rb, w["w_ff2"], "nn", f"ff2_{tag}", tm=512, tn=1024, tk=4096, res=x1, vec=next_gain, epi="rms_next")
    else:
        x2 = _mm(rb, w["w_ff2"], "nn", f"ff2_{tag}", tm=512, tn=1024, tk=4096, res=x1)
    saved = dict(x=x, hb=hb, z=z, yp=yp, yc=yc, qkv=qkv, o3=o3, lse3=lse3, merged=merged, x1=x1, h2b=h2b, rb=rb)
    return x2, saved, w


def _layer_bwd(dx2, w, s, tag, after=None, mid=None, tail=None):
    g = {}
    dab = _mm(dx2, w["w_ff2"], "nt", f"d_ff2_act_{tag}", tm=1024, tn=1024, tk=1024, out_dtype=BF, aux=s["rb"],
              epi="drelu2", after=after)
    g["w_ff2"] = _mm(s["rb"], dx2, "tn", f"d_ff2_w_{tag}", tm=1024, tn=1024, tk=2048, out_dtype=BF)
    g["w_ff1"] = _mm(s["h2b"], dab, "tn", f"d_ff1_w_{tag}", tm=1024, tn=1024, tk=2048, out_dtype=BF, out_shards=True)
    dx1, g["norm_mlp"] = _mm(dab, w["w_ff1"], "nt", f"d_ff1_act_{tag}", tm=1024, tn=1024, tk=1024, b_shards=True,
                             res=dx2, aux=s["x1"], vec=w["norm_mlp"], epi="rms_bwd")
    dm = _mm(dx1, w["w_o"], "nt", f"d_out_act_{tag}", tm=1024, tn=1024, tk=1024)
    g["w_o"] = _mm(s["merged"], dx1, "tn", f"d_out_w_{tag}", tm=1024, tn=1024, tk=1024, out_dtype=BF)
    (dz, dyp, dyc, do3, c3, g["w_pool_up"], g["w_conv_out"], g["w_attn_up"],
     g["b_gate"]) = _merge_bwd(dm, s["yp"], s["yc"], s["o3"], s["lse3"], s["z"], w["b_gate"], w["w_pool_up"],
                               w["w_conv_out"], w["w_attn_up"], f"d_merge_{tag}")
    behind = mid(g) if mid is not None else None
    dzq, dzk, dzv, dgq, dgk = _attn_bwd(s["z"], s["qkv"], do3, c3, s["lse3"], w["qk_gain"], f"d_attn_{tag}",
                                        after=behind)
    g["q_gain"] = dgq[:, :HEAD_DIM] + dgq[:, HEAD_DIM:]
    g["k_gain"] = dgk[:, :HEAD_DIM] + dgk[:, HEAD_DIM:]
    for off, piece in ((OFF_Q, dzq), (OFF_K, dzk), (OFF_V, dzv)):
        dz = lax.dynamic_update_slice(dz, piece, (0, off))
    dz, g["pool_mix"], g["pool_scale"], g["conv_w"] = _poolconv_bwd(
        s["z"], dyp, dyc, w["pool_mix"], w["pool_scale"], w["conv_w"], dz, f"d_poolconv_{tag}")
    g["w_in"] = _mm(s["hb"], dz, "tn", f"d_in_w_{tag}", tm=512, tn=3712, tk=1024, out_dtype=BF)
    dh = _mm(dz, w["w_in"], "nn", f"d_in_act_{tag}", tm=1024, tn=1024, tk=3712,
             after=tail(g) if tail is not None else None)
    dx, g["norm_mix"] = _rms_bwd(dh, s["x"], w["norm_mix"], dx1, f"d_rms_mix_{tag}")
    return dx, g


def _position():
    x, y, c = lax.axis_index("x"), lax.axis_index("y"), lax.axis_index("c")
    chips = [(1 - x, y), (x, 1 - y), (1 - x, 1 - y)]
    return x, y, c, 2 * x + y, chips, [2 * cx + cy for cx, cy in chips]


def _remote(src, dst, ssem, rsem, dev):
    return pltpu.make_async_remote_copy(src_ref=src, dst_ref=dst, send_sem=ssem, recv_sem=rsem, device_id=dev,
                                        device_id_type=MESH_ID)


def _halves(a):
    return a.reshape(a.shape[0], 2, a.shape[1] // 2, a.shape[2])


SEM = pl.BlockSpec(memory_space=pltpu.SEMAPHORE)
TOKEN = jax.ShapeDtypeStruct((8, LANES), F32)
TOKEN_SPEC = pl.BlockSpec(memory_space=pltpu.VMEM)


def _split_params():
    return pltpu.CompilerParams(has_side_effects=pltpu.SideEffectType.DATAFLOW_SIDE_EFFECTING)


def _ici_plan(x, y, c, q, chips, qs):
    return [((q, c), (qs[j], c), (chips[j][0], chips[j][1], c)) for j in range(3)]


def _sibling_plan(x, y, c, q, chips, qs):
    return [((qs[j], c), (qs[j], 1 - c), (x, y, 1 - c)) for j in range(3)]


def _gather_start(bufs, name, after):
    return _copies_start([_halves(b) for b in bufs], name, after, _ici_plan)


def _copies_start(views, name, after, plan):
    n = len(views)

    def body(*refs):
        first_sem = n + 1
        ssem, rsem = refs[first_sem:first_sem + ns], refs[first_sem + ns:first_sem + 2 * ns]
        outs, token = refs[first_sem + 2 * ns:first_sem + 2 * ns + n], refs[first_sem + 2 * ns + n]
        for k in range(n):
            for j, (sent, _, peer) in enumerate(plan(*_position())):
                mine = outs[k].at[sent]
                _remote(mine, mine, ssem[3 * k + j], rsem[3 * k + j], peer).start()
        token[...] = jnp.zeros_like(token)

    ns = 3 * n
    outs = _pallas_call(
        body, name=name, in_specs=[ANY] * (n + 1), out_specs=[SEM] * (2 * ns) + [ANY] * n + [TOKEN_SPEC],
        out_shape=[pltpu.SemaphoreType.DMA(())] * (2 * ns) + [jax.ShapeDtypeStruct(v.shape, v.dtype) for v in views]
        + [TOKEN],
        input_output_aliases={k: k + 2 * ns for k in range(n)}, compiler_params=_split_params(),
    )(*views, after)
    return list(outs[:ns]), list(outs[ns:2 * ns]), list(outs[2 * ns:2 * ns + n]), outs[2 * ns + n]


def _copies_wait(ssem, rsem, views, after, name, plan):
    n = len(views)
    ns = len(ssem)

    def wait_body(*refs):
        ssem_ref, rsem_ref = refs[n:n + ns], refs[n + ns:n + 2 * ns]
        outs = refs[n + 2 * ns + 1:]
        for k in range(n):
            for j, (sent, landing, peer) in enumerate(plan(*_position())):
                cp = _remote(outs[k].at[sent], outs[k].at[landing], ssem_ref[3 * k + j], rsem_ref[3 * k + j], peer)
                cp.wait_send()
                cp.wait_recv()

    return _pallas_call(
        wait_body, name=name, in_specs=[ANY] * n + [SEM] * (2 * ns) + [ANY], out_specs=[ANY] * n,
        out_shape=[jax.ShapeDtypeStruct(v.shape, v.dtype) for v in views],
        input_output_aliases={k: k for k in range(n)}, compiler_params=_split_params(),
    )(*views, *ssem, *rsem, after)


def _gather_finish(ssem, rsem, views, after, name_wait, name_forward, shapes):
    return _gather_forward(_copies_wait(ssem, rsem, views, after, name_wait, _ici_plan), name_forward, shapes)


def _gather_forward(landed, name_forward, shapes):
    n = len(landed)
    views = landed

    def forward_body(*refs):
        outs = refs[n:2 * n]
        fssem, frsem = refs[2 * n:]
        x, y, c, q, chips, qs = _position()
        sib = (x, y, 1 - c)
        sent = []
        for k in range(n):
            for j in range(3):
                slot = outs[k].at[qs[j], c]
                cp = _remote(slot, slot, fssem.at[k, j], frsem.at[k, j], sib)
                cp.start()
                sent.append(cp)
        for k in range(n):
            for j in range(3):
                slot = outs[k].at[qs[j], 1 - c]
                _remote(slot, slot, fssem.at[k, j], frsem.at[k, j], sib).wait_recv()
        for cp in sent:
            cp.wait_send()

    outs = _pallas_call(
        forward_body, name=name_forward, in_specs=[ANY] * n, out_specs=[ANY] * n,
        out_shape=[jax.ShapeDtypeStruct(v.shape, v.dtype) for v in views],
        input_output_aliases={k: k for k in range(n)}, scratch_shapes=[pltpu.SemaphoreType.DMA((n, 3))] * 2,
    )(*landed)
    return [o.reshape(s) for o, s in zip(outs, shapes)]


def _chip_exchange_start(parts, name):
    n = len(parts)

    def body(*refs):
        ssem, rsem = refs[n:n + ns], refs[n + ns:n + 2 * ns]
        base = n + 2 * ns
        srcs, outs, token = refs[base:base + n], refs[base + n:base + 2 * n], refs[base + 2 * n]
        x, y, c, q, chips, qs = _position()
        for k in range(n):
            for j, chip in enumerate(chips):
                _remote(srcs[k].at[qs[j]], outs[k].at[j], ssem[3 * k + j], rsem[3 * k + j],
                        (chip[0], chip[1], c)).start()
        token[...] = jnp.zeros_like(token)

    ns = 3 * n
    outs = _pallas_call(
        body, name=name, in_specs=[ANY] * n, out_specs=[SEM] * (2 * ns) + [ANY] * (2 * n) + [TOKEN_SPEC],
        out_shape=[pltpu.SemaphoreType.DMA(())] * (2 * ns) + [jax.ShapeDtypeStruct(a.shape, a.dtype) for a in parts]
        + [jax.ShapeDtypeStruct((3,) + a.shape[1:], a.dtype) for a in parts] + [TOKEN],
        input_output_aliases={k: k + 2 * ns for k in range(n)}, compiler_params=_split_params(),
    )(*parts)
    b = 2 * ns
    return list(outs[:ns]), list(outs[ns:b]), list(outs[b:b + n]), list(outs[b + n:b + 2 * n]), outs[b + 2 * n]


def _chip_exchange_wait(ssem, rsem, parts, landing, after, name):
    n = len(parts)
    ns = len(ssem)

    def body(*refs):
        ssem_ref, rsem_ref = refs[2 * n:2 * n + ns], refs[2 * n + ns:2 * n + 2 * ns]
        base = 2 * n + 2 * ns + 1
        srcs, outs = refs[base:base + n], refs[base + n:]
        x, y, c, q, chips, qs = _position()
        for k in range(n):
            for j, chip in enumerate(chips):
                cp = _remote(srcs[k].at[qs[j]], outs[k].at[j], ssem_ref[3 * k + j], rsem_ref[3 * k + j],
                             (chip[0], chip[1], c))
                cp.wait_send()
                cp.wait_recv()

    outs = _pallas_call(
        body, name=name, in_specs=[ANY] * (2 * n) + [SEM] * (2 * ns) + [ANY], out_specs=[ANY] * (2 * n),
        out_shape=[jax.ShapeDtypeStruct(a.shape, a.dtype) for a in list(parts) + list(landing)],
        input_output_aliases={k: k for k in range(2 * n)}, compiler_params=_split_params(),
    )(*parts, *landing, *ssem, *rsem, after)
    return list(outs[:n]), list(outs[n:])


def _pair_swap(views, name):
    n = len(views)

    def body(*refs):
        ins, outs = refs[:n], refs[n:2 * n]
        ssem, rsem = refs[2 * n:]
        x, y, c, _, _, _ = _position()
        cps = [_remote(ins[k].at[pl.ds(0, N_CHIPS), 1 - c], outs[k], ssem.at[k], rsem.at[k], (x, y, 1 - c))
               for k in range(n)]
        for cp in cps:
            cp.start()
        for cp in cps:
            cp.wait()

    return _pallas_call(
        body, name=name, in_specs=[ANY] * n, out_specs=[ANY] * n,
        out_shape=[jax.ShapeDtypeStruct((v.shape[0],) + v.shape[2:], v.dtype) for v in views],
        scratch_shapes=[pltpu.SemaphoreType.DMA((n,))] * 2,
    )(*views)


def _pair_send(arrays, name):
    n = len(arrays)

    def body(*refs):
        ins, outs = refs[:n], refs[n:2 * n]
        ssem, rsem = refs[2 * n:]
        x, y, c, _, _, _ = _position()
        cps = [_remote(ins[k], outs[k], ssem.at[k], rsem.at[k], (x, y, 1 - c)) for k in range(n)]
        for cp in cps:
            cp.start()
        for cp in cps:
            cp.wait()

    return _pallas_call(
        body, name=name, in_specs=[ANY] * n, out_specs=[ANY] * n,
        out_shape=[jax.ShapeDtypeStruct(a.shape, a.dtype) for a in arrays],
        scratch_shapes=[pltpu.SemaphoreType.DMA((n,))] * 2,
    )(*arrays)


def _all_to_all_small(part):
    P = part.shape[0]

    def body(in_ref, out_ref, lsem, ssem, rsem):
        x, y, c = lax.axis_index("x"), lax.axis_index("y"), lax.axis_index("c")
        me = 4 * x + 2 * y + c
        flips = [(fx, fy, fc) for fx in (0, 1) for fy in (0, 1) for fc in (0, 1)][1:]
        peers = [((x + fx) % 2, (y + fy) % 2, (c + fc) % 2) for fx, fy, fc in flips]
        loc = pltpu.make_async_copy(in_ref, out_ref.at[me], lsem)
        loc.start()
        cps = [_remote(in_ref, out_ref.at[me], ssem.at[j], rsem.at[j], peer) for j, peer in enumerate(peers)]
        for cp in cps:
            cp.start()
        for j, (px, py, pc) in enumerate(peers):
            _remote(in_ref, out_ref.at[4 * px + 2 * py + pc], ssem.at[j], rsem.at[j], peers[j]).wait_recv()
        for cp in cps:
            cp.wait_send()
        loc.wait()

    return _pallas_call(
        body, name="small_exchange", in_specs=[ANY], out_specs=ANY,
        out_shape=jax.ShapeDtypeStruct((8, P, LANES), F32),
        scratch_shapes=[pltpu.SemaphoreType.DMA(())] + [pltpu.SemaphoreType.DMA((7,))] * 2,
    )(part)


def _small_peers():
    x, y, c = lax.axis_index("x"), lax.axis_index("y"), lax.axis_index("c")
    flips = [(fx, fy, fc) for fx in (0, 1) for fy in (0, 1) for fc in (0, 1)][1:]
    peers = [((x + fx) % 2, (y + fy) % 2, (c + fc) % 2) for fx, fy, fc in flips]
    return 4 * x + 2 * y + c, peers


def _all_to_all_small_start(part, name):
    P = part.shape[0]
    me = 4 * lax.axis_index("x") + 2 * lax.axis_index("y") + lax.axis_index("c")
    landing = lax.dynamic_update_slice(jnp.zeros((8, P, LANES), F32), part[None], (me, 0, 0))

    def body(*refs):
        sems, src, land, token = refs[2:16], refs[16], refs[17], refs[18]
        me_, peers = _small_peers()
        for j, peer in enumerate(peers):
            _remote(src, land.at[me_], sems[j], sems[7 + j], peer).start()
        token[...] = jnp.zeros_like(token)

    outs = _pallas_call(
        body, name=name, in_specs=[ANY, ANY], out_specs=[SEM] * 14 + [ANY, ANY, TOKEN_SPEC],
        out_shape=[pltpu.SemaphoreType.DMA(())] * 14 + [jax.ShapeDtypeStruct(part.shape, F32),
                                                       jax.ShapeDtypeStruct((8, P, LANES), F32), TOKEN],
        input_output_aliases={0: 14, 1: 15}, compiler_params=_split_params(),
    )(part, landing)
    return list(outs[:7]), list(outs[7:14]), outs[14], outs[15], outs[16]


def _all_to_all_small_wait(ssem, rsem, part, landing, after, name):
    def body(*refs):
        sems, src, land = refs[2:16], refs[17], refs[18]
        _, peers = _small_peers()
        for j, (px, py, pc) in enumerate(peers):
            cp = _remote(src, land.at[4 * px + 2 * py + pc], sems[j], sems[7 + j], peers[j])
            cp.wait_send()
            cp.wait_recv()

    return _pallas_call(
        body, name=name, in_specs=[ANY, ANY] + [SEM] * 14 + [ANY], out_specs=[ANY, ANY],
        out_shape=[jax.ShapeDtypeStruct(part.shape, F32), jax.ShapeDtypeStruct(landing.shape, F32)],
        input_output_aliases={0: 0, 1: 1}, compiler_params=_split_params(),
    )(part, landing, *ssem, *rsem, after)[1]


def _row_tile(rows, width, n_arrays):
    t = rows
    while t % 2 == 0 and t > 8 and 2 * n_arrays * t * width * 4 > VMEM_LIMIT // 2:
        t //= 2
    return t


def _chip():
    return 2 * lax.axis_index("x") + lax.axis_index("y")


def _core():
    return lax.axis_index("c")


def _cast_place(w3, layer, name):
    _, r, c = w3.shape
    tr = _row_tile(r, c, 2)

    def body(w_ref, o_ref):
        o_ref[...] = w_ref[...].astype(BF)

    return _pallas_call(
        body, name=name, grid=(r // tr,), in_specs=[pl.BlockSpec((None, tr, c), lambda i: (layer, i, 0))],
        out_specs=pl.BlockSpec((None, tr, c), lambda i: (_chip(), i, 0)),
        out_shape=jax.ShapeDtypeStruct((N_CHIPS, r, c), BF), compiler_params=_params(("parallel",)),
    )(w3)


def _pair_sum(views, recvs, name):
    n = len(views)

    def body(*refs):
        for g_ref, r_ref, o_ref in zip(refs[:n], refs[n:2 * n], refs[2 * n:]):
            o_ref[...] = (g_ref[...].astype(F32) + r_ref[...].astype(F32)).astype(BF)

    own = [pl.BlockSpec((None, None) + v.shape[2:], lambda p: (p, _core(), 0, 0)) for v in views]
    blk = [pl.BlockSpec((None,) + r.shape[1:], lambda p: (p, 0, 0)) for r in recvs]
    return _pallas_call(
        body, name=name, grid=(N_CHIPS,), in_specs=own + blk, out_specs=blk,
        out_shape=[jax.ShapeDtypeStruct(r.shape, BF) for r in recvs], compiler_params=_params(("parallel",)),
    )(*views, *recvs)


CHIP_SUM_STEPS = 2


def _chip_sum(parts, recvs, name):
    n = len(parts)

    def body(*refs):
        for p_ref, r_ref, o_ref in zip(refs[:n], refs[n:2 * n], refs[2 * n:]):
            acc = p_ref[...].astype(F32)
            for j in range(3):
                acc = acc + r_ref[j].astype(F32)
            o_ref[...] = acc

    rows = [p.shape[1] // CHIP_SUM_STEPS for p in parts]
    return _pallas_call(
        body, name=name, grid=(CHIP_SUM_STEPS,),
        in_specs=[pl.BlockSpec((None, t, p.shape[2]), lambda i: (_chip(), i, 0)) for p, t in zip(parts, rows)]
        + [pl.BlockSpec((3, t, p.shape[2]), lambda i: (0, i, 0)) for p, t in zip(parts, rows)],
        out_specs=[pl.BlockSpec((t, p.shape[2]), lambda i: (i, 0)) for p, t in zip(parts, rows)],
        out_shape=[jax.ShapeDtypeStruct(p.shape[1:], F32) for p in parts], compiler_params=_params(("parallel",)),
    )(*parts, *recvs)


def _sum_slices(a, name):
    n, rows, width = a.shape
    tr = _row_tile(rows, width, n + 1)

    def body(a_ref, o_ref):
        acc = a_ref[0].astype(F32)
        for i in range(1, n):
            acc = acc + a_ref[i].astype(F32)
        o_ref[...] = acc

    return _pallas_call(
        body, name=name, grid=(rows // tr,), in_specs=[pl.BlockSpec((n, tr, width), lambda i: (0, i, 0))],
        out_specs=pl.BlockSpec((tr, width), lambda i: (i, 0)), out_shape=jax.ShapeDtypeStruct((rows, width), F32),
        compiler_params=_params(("parallel",)),
    )(a)


def _adamw_update(w, g, m, v):
    nm = ADAM_B1 * m + (1.0 - ADAM_B1) * g
    nv = ADAM_B2 * v + (1.0 - ADAM_B2) * (g * g)
    m_hat = nm / (1.0 - ADAM_B1 ** ADAM_STEP)
    v_hat = nv / (1.0 - ADAM_B2 ** ADAM_STEP)
    return -ADAM_LR * (m_hat / (jnp.sqrt(v_hat) + ADAM_EPS) + ADAM_WD * w), nm, nv


def _adamw(ws, gs, ms, vs, name):
    n = len(ws)

    def body(*refs):
        for k in range(n):
            w_ref, g_ref, m_ref, v_ref = (refs[s * n + k] for s in range(4))
            d_ref, nm_ref, nv_ref = (refs[(4 + s) * n + k] for s in range(3))
            d_ref[...], nm_ref[...], nv_ref[...] = _adamw_update(w_ref[...], g_ref[...], m_ref[...], v_ref[...])

    whole = [pl.BlockSpec(w.shape, lambda i: (0, 0)) for w in ws]
    outs = _pallas_call(
        body, name=name, grid=(1,), in_specs=whole * 4, out_specs=whole * 3,
        out_shape=[jax.ShapeDtypeStruct(w.shape, F32) for _ in range(3) for w in ws],
        compiler_params=_params(("arbitrary",)),
    )(*ws, *gs, *ms, *vs)
    return [[outs[s * n + k] for s in range(3)] for k in range(n)]


ADAMW_STEPS = 4


def _adamw_halves(ws, ms, vs, mine, other, name):
    n = len(ws)
    depth = ws[0].shape[0]
    assert depth == 2
    halves = [(w.shape[1] // 2, w.shape[2]) for w in ws]
    tiles = [hr // ADAMW_STEPS for hr, _ in halves]
    kinds = ((0, True), (0, False), (1, True), (1, False))

    def active(l, h, layer, own):
        mine_half = h == _core()
        return (l == layer) & (mine_half if own else jnp.logical_not(mine_half))

    def body(*refs):
        l, h = pl.program_id(0), pl.program_id(1)
        flags = [active(l, h, layer, own) for layer, own in kinds]
        for k in range(n):
            w_ref, m_ref, v_ref = refs[k], refs[n + k], refs[2 * n + k]
            g_refs = [refs[(3 + s) * n + k] for s in range(4)]
            go_ref, d_ref, nm_ref, nv_ref = (refs[(7 + s) * n + k] for s in range(4))
            for flag, g_ref in zip(flags, g_refs):
                @pl.when(flag)
                def _():
                    gv = g_ref[...]
                    go_ref[...] = gv
                    d_ref[...], nm_ref[...], nv_ref[...] = _adamw_update(w_ref[...], gv, m_ref[...], v_ref[...])

    def blk(k):
        return pl.BlockSpec((None, None, tiles[k], halves[k][1]), lambda l, h, i: (l, h, i, 0))

    def gspec(k, layer, own):
        return pl.BlockSpec((tiles[k], halves[k][1]), lambda l, h, i: (jnp.where(active(l, h, layer, own), i, 0), 0))

    def view(a, k):
        return a.reshape(depth, 2, halves[k][0], halves[k][1])

    blks = [blk(k) for k in range(n)]
    sources = [[(mine if own else other)[layer][k] for k in range(n)] for layer, own in kinds]
    outs = _pallas_call(
        body, name=name, grid=(depth, 2, ADAMW_STEPS),
        in_specs=blks * 3 + [gspec(k, layer, own) for layer, own in kinds for k in range(n)], out_specs=blks * 4,
        out_shape=[jax.ShapeDtypeStruct((depth, 2) + halves[k], F32) for _ in range(4) for k in range(n)],
        compiler_params=_params(("parallel", "parallel", "parallel")),
    )(*[view(a, k) for group in (ws, ms, vs) for k, a in enumerate(group)], *[g for src in sources for g in src])
    return [[outs[s * n + k].reshape(ws[k].shape) for s in range(4)] for k in range(n)]


BIG = ("w_in", "w_pool_up", "w_conv_out", "w_attn_up", "w_o", "w_ff1", "w_ff2")
SMALL = ("norm_mix", "b_gate", "pool_mix", "pool_scale", "conv_w", "q_gain", "k_gain", "norm_mlp")
ORDER = ("norm_mix", "w_in", "b_gate", "pool_mix", "pool_scale", "conv_w", "q_gain", "k_gain", "w_pool_up",
         "w_conv_out", "w_attn_up", "w_o", "norm_mlp", "w_ff1", "w_ff2")
COLUMN_SHARDED = ("w_pool_up", "w_conv_out", "w_attn_up", "w_ff1")


def _matrix_weights(gathered):
    w = {}
    for name, g4 in gathered.items():
        if name in COLUMN_SHARDED:
            w[name] = g4
        else:
            w[name] = g4.reshape(N_CHIPS * g4.shape[1], g4.shape[2])
    return w


def _small_weights(l, small):
    w = {}
    w["norm_mix"] = small["norm_mix"][l][None]
    w["norm_mlp"] = small["norm_mlp"][l][None]
    w["b_gate"] = small["b_gate"][l][None]
    w["pool_mix"] = small["pool_mix"][l].astype(BF)
    w["pool_scale"] = small["pool_scale"][l][None]
    w["conv_w"] = jnp.pad(small["conv_w_full"][l], ((0, 5), (0, 0)))
    w["qk_gain"] = jnp.pad(jnp.stack([jnp.tile(small["q_gain"][l], 2), jnp.tile(small["k_gain"][l], 2)]), ((0, 6), (0, 0)))
    return w


def _to_chip_major(name, g):
    if name == "w_in":
        return g.T.reshape(N_CHIPS, g.shape[1] // N_CHIPS, g.shape[0])
    if name in COLUMN_SHARDED:
        return g
    return g.reshape(N_CHIPS, g.shape[0] // N_CHIPS, g.shape[1])


def _pad8(a):
    a = a.reshape(-1)
    return jnp.pad(a, (0, (-a.size) % (8 * LANES))).reshape(-1, LANES)


def kernel(x, norm_mix, w_in, b_gate, pool_mix, pool_scale, conv_w, q_gain, k_gain, w_pool_up, w_conv_out, w_attn_up, w_o, norm_mlp, w_ff1, w_ff2, loss_target, m_norm_mix, m_w_in, m_b_gate, m_pool_mix, m_pool_scale, m_conv_w, m_q_gain, m_k_gain, m_w_pool_up, m_w_conv_out, m_w_attn_up, m_w_o, m_norm_mlp, m_w_ff1, m_w_ff2, v_norm_mix, v_w_in, v_b_gate, v_pool_mix, v_pool_scale, v_conv_w, v_q_gain, v_k_gain, v_w_pool_up, v_w_conv_out, v_w_attn_up, v_w_o, v_norm_mlp, v_w_ff1, v_w_ff2):
    weights = dict(norm_mix=norm_mix, w_in=w_in, b_gate=b_gate, pool_mix=pool_mix, pool_scale=pool_scale, conv_w=conv_w,
                   q_gain=q_gain, k_gain=k_gain, w_pool_up=w_pool_up, w_conv_out=w_conv_out, w_attn_up=w_attn_up,
                   w_o=w_o, norm_mlp=norm_mlp, w_ff1=w_ff1, w_ff2=w_ff2)
    moms = dict(norm_mix=m_norm_mix, w_in=m_w_in, b_gate=m_b_gate, pool_mix=m_pool_mix, pool_scale=m_pool_scale,
                conv_w=m_conv_w, q_gain=m_q_gain, k_gain=m_k_gain, w_pool_up=m_w_pool_up, w_conv_out=m_w_conv_out,
                w_attn_up=m_w_attn_up, w_o=m_w_o, norm_mlp=m_norm_mlp, w_ff1=m_w_ff1, w_ff2=m_w_ff2)
    vels = dict(norm_mix=v_norm_mix, w_in=v_w_in, b_gate=v_b_gate, pool_mix=v_pool_mix, pool_scale=v_pool_scale,
                conv_w=v_conv_w, q_gain=v_q_gain, k_gain=v_k_gain, w_pool_up=v_w_pool_up, w_conv_out=v_w_conv_out,
                w_attn_up=v_w_attn_up, w_o=v_w_o, norm_mlp=v_norm_mlp, w_ff1=v_w_ff1, w_ff2=v_w_ff2)
    depth = norm_mix.shape[0]
    q = 2 * lax.axis_index("x") + lax.axis_index("y")
    for group in (weights, moms, vels):
        group["w_in"] = jnp.swapaxes(group["w_in"], 1, 2)

    assert depth == 2, "the second layer's gather hides behind the first layer's forward, and likewise backward"
    first, rest = BIG[:1], BIG[1:]
    cw_all = _all_to_all_small(_pad8(conv_w))
    bufs = [{n: _cast_place(weights[n], 0, f"cast_{n}_l0") for n in first}]
    a_ssem, a_rsem, a_views, a_token = _gather_start([bufs[0][n] for n in first], "gather_start_l0_in", cw_all)
    bufs[0].update({n: _cast_place(weights[n], 0, f"cast_{n}_l0") for n in rest})
    bufs += [{n: _cast_place(weights[n], l, f"cast_{n}_l{l}") for n in BIG} for l in range(1, depth)]
    b_ssem, b_rsem, b_views, b_token = _gather_start([bufs[0][n] for n in rest], "gather_start_l0_rest", a_token)
    g_ssem, g_rsem, g_views, g_token = _gather_start([bufs[1][n] for n in BIG], "gather_start_l1", b_token)
    conv_w_full = jnp.concatenate(
        [cw_all[2 * p].reshape(-1)[:conv_w.size].reshape(conv_w.shape) for p in range(N_CHIPS)], axis=-1)
    small = dict(weights)
    small["conv_w_full"] = conv_w_full

    def soon_weights(t):
        got = _gather_finish(a_ssem, a_rsem, a_views, t, "gather_wait_l0_in", "gather_forward_l0_in",
                             [bufs[0][n].shape for n in first])
        return _matrix_weights(dict(zip(first, got)))

    mlp = len(rest) - 2
    shapes_0 = [bufs[0][n].shape for n in rest]
    sibling = {}

    def late_weights(t):
        landed = _copies_wait(b_ssem, b_rsem, b_views, t, "gather_wait_l0_rest", _ici_plan)
        got = _gather_forward(landed[:mlp], "gather_forward_l0_rest", shapes_0[:mlp])
        sibling["mlp"] = _copies_start(landed[mlp:], "gather_forward_start_l0_mlp", got[0], _sibling_plan)
        return dict(_matrix_weights(dict(zip(rest[:mlp], got))), crossing=sibling["mlp"][3])

    def last_weights(t):
        f_ssem, f_rsem, f_views, _ = sibling["mlp"]
        got = _copies_wait(f_ssem, f_rsem, f_views, t, "gather_forward_wait_l0_mlp", _sibling_plan)
        return _matrix_weights({n: o.reshape(s) for n, o, s in zip(rest[mlp:], got, shapes_0[mlp:])})

    wl, saved = [None] * depth, [None] * depth
    small_1 = _small_weights(1, small)
    (h, hb_1), saved[0], wl[0] = _layer_fwd(x[0], _small_weights(0, small), "l0", after=g_token, soon=soon_weights,
                                            late=late_weights, last=last_weights, next_gain=small_1["norm_mix"])
    shapes_1 = [bufs[1][n].shape for n in BIG]

    def soon_weights_1(t):
        landed = _copies_wait(g_ssem, g_rsem, g_views, t, "gather_wait_l1", _ici_plan)
        got = _gather_forward(landed[:1], "gather_forward_l1_in", shapes_1[:1])
        sibling["rest"] = _copies_start(landed[1:], "gather_forward_start_l1_rest", got[0], _sibling_plan)
        return dict(_matrix_weights(dict(zip(first, got))), started=sibling["rest"][3])

    def late_weights_1(t):
        f_ssem, f_rsem, f_views, _ = sibling["rest"]
        got = _copies_wait(f_ssem, f_rsem, f_views, t, "gather_forward_wait_l1_rest", _sibling_plan)
        return _matrix_weights({n: o.reshape(s) for n, o, s in zip(rest, got, shapes_1[1:])})

    (dh, loss_row), saved[1], wl[1] = _layer_fwd(h, small_1, "l1", soon=soon_weights_1, late=late_weights_1,
                                                 target=loss_target[0], hb=hb_1)

    def pair_stage(names, g, tag):
        views = [_halves(_to_chip_major(n, g[n])) for n in names]
        from_sibling = _pair_swap(views, f"grad_pair_swap_{tag}")
        return _pair_sum(views, from_sibling, f"pair_sum_{tag}")

    mine, other = [{}, {}], [{}, {}]

    def finish(names, l, started, after, tag):
        ssem, rsem, parts, landing, _ = started
        parts, arrived = _chip_exchange_wait(ssem, rsem, parts, landing, after, f"grad_chip_exchange_wait_{tag}")
        got = _chip_sum(parts, arrived, f"chip_sum_{tag}")
        mine[l].update(zip(names, got))
        other[l].update(zip(names, _pair_send(got, f"grad_pair_send_{tag}")))

    def small_pieces(g):
        return [_pad8(g[n][:3] if n == "conv_w" else g[n]) for n in SMALL]

    def start_small(l):
        pieces = small_pieces(grads[l]) + ([_pad8(loss_row)] if l == depth - 1 else [])
        return _all_to_all_small_start(jnp.concatenate(pieces, axis=0), f"small_grad_exchange_start_l{l}")

    grads, early, small = [None] * depth, {}, [None] * depth
    dh, grads[1] = _layer_bwd(dh, wl[1], saved[1], "l1")
    second = _chip_exchange_start(pair_stage(BIG, grads[1], "l1"), "grad_chip_exchange_start_l1")
    small[1] = start_small(1)

    def start_rest(g):
        early["rest"] = _chip_exchange_start(pair_stage(rest, g, "l0_rest"), "grad_chip_exchange_start_l0_rest")
        return early["rest"][4]

    def start_last(g):
        early["in"] = _chip_exchange_start(pair_stage(first, g, "l0_in"), "grad_chip_exchange_start_l0_in")
        return early["in"][4]

    dh, grads[0] = _layer_bwd(dh, wl[0], saved[0], "l0", after=[second[4], small[1][4]], mid=start_rest,
                              tail=start_last)
    small[0] = start_small(0)
    started = small[0][4]
    finish(BIG, 1, second, started, "l1")
    finish(rest, 0, early["rest"], started, "l0_rest")
    full = {}

    deltas, new_m, new_v = {}, {}, {}

    def update_matrices(names, tag):
        results = _adamw_halves(
            [weights[n] for n in names], [moms[n] for n in names], [vels[n] for n in names],
            [[mine[l][n] for n in names] for l in range(depth)], [[other[l][n] for n in names] for l in range(depth)],
            f"adamw_{tag}")
        for n, (g_, d_, m_, v_) in zip(names, results):
            full[n], deltas[n], new_m[n], new_v[n] = g_, d_, m_, v_

    update_matrices(rest, "rest")
    finish(first, 0, early["in"], deltas[rest[-1]], "l0_in")
    update_matrices(first, "in")
    summed = []
    for l in range(depth):
        ssem, rsem, part, landing, _ = small[l]
        summed.append(_sum_slices(_all_to_all_small_wait(ssem, rsem, part, landing, deltas[first[-1]],
                                                         f"small_grad_exchange_wait_l{l}"), f"small_sum_l{l}"))
    row = 0
    for n, piece in zip(SMALL, small_pieces(grads[0])):
        size = (weights[n].size if n != "conv_w" else depth * 3 * 512) // depth
        flat = jnp.stack([s[row:row + piece.shape[0]].reshape(-1)[:size] for s in summed])
        row += piece.shape[0]
        if n == "conv_w":
            full[n] = lax.dynamic_slice_in_dim(flat.reshape(depth, 3, 512), q * conv_w.shape[2], conv_w.shape[2], axis=2)
        else:
            full[n] = flat.reshape(weights[n].shape)
    loss = summed[depth - 1][row, 0]
    two_d = {n: (-1, weights[n].shape[-1]) if n not in ("conv_w", "q_gain", "k_gain") else (1, -1) for n in SMALL}
    results = _adamw(*[[group[n].reshape(two_d[n]) for n in SMALL] for group in (weights, full, moms, vels)],
                     "adamw_small")
    for n, (d2, m2, v2) in zip(SMALL, results):
        shape = weights[n].shape
        deltas[n], new_m[n], new_v[n] = d2.reshape(shape), m2.reshape(shape), v2.reshape(shape)
        full[n] = full[n].reshape(shape)
    for group in (full, deltas, new_m, new_v):
        group["w_in"] = jnp.swapaxes(group["w_in"], 1, 2)
    return (loss, dh[None], *[full[n] for n in ORDER], *[deltas[n] for n in ORDER], *[new_m[n] for n in ORDER],
            *[new_v[n] for n in ORDER])
```

```python
import functools

import jax
import jax.numpy as jnp
from jax import lax
from jax.experimental import pallas as pl
from jax.experimental.pallas import tpu as pltpu

F32 = jnp.float32
BF = jnp.bfloat16
MESH_ID = pl.DeviceIdType.MESH
ANY = pl.BlockSpec(memory_space=pl.ANY)

EPS = 1e-6
MASK_VALUE = -1e30
POOL_WINDOWS = (2, 4, 8, 16)
ATTN_DILATIONS = (1, 4, 16)
ATTN_BLOCK = 128
HEAD_DIM = 64
OFF_Q, OFF_K, OFF_V, OFF_GATE = 2048, 2816, 3584, 4352
N_CHIPS = 4
ADAM_LR, ADAM_B1, ADAM_B2, ADAM_EPS, ADAM_WD, ADAM_STEP = 0.001, 0.9, 0.999, 1e-08, 0.01, 10

VMEM_LIMIT = 48 * 1024 * 1024
LANES = 128

_DIMS = {"nn": (((1,), (0,)), ((), ())), "nt": (((1,), (1,)), ((), ())), "tn": (((0,), (0,)), ((), ()))}


def _params(sem):
    return pltpu.CompilerParams(dimension_semantics=sem, vmem_limit_bytes=VMEM_LIMIT)


def _pallas_call(body, **kw):
    def in_hbm(s):
        pin = isinstance(s, jax.ShapeDtypeStruct) and s is not TOKEN and jnp.issubdtype(s.dtype, jnp.floating)
        return pltpu.HBM(s.shape, s.dtype) if pin else s

    out_shape = kw.pop("out_shape")
    kw["out_shape"] = [in_hbm(s) for s in out_shape] if isinstance(out_shape, (list, tuple)) else in_hbm(out_shape)
    call = pl.pallas_call(body, **kw)

    def run(*args):
        pinned = [pltpu.with_memory_space_constraint(a, pltpu.HBM)
                  if hasattr(a, "dtype") and jnp.issubdtype(a.dtype, jnp.floating) else a for a in args]
        return call(*pinned)

    return run


def _dot(a, b, mode="nn"):
    return lax.dot_general(a, b, _DIMS[mode], preferred_element_type=F32)


def _mm(a, b, mode, name, *, tm, tn, tk, out_dtype=F32, res=None, aux=None, epi=None, n_outer=False,
        b_shards=False, out_shards=False, after=None, vec=None):
    if mode == "tn":
        K, M = a.shape
    else:
        M, K = a.shape
    if b_shards:
        if mode == "nn":
            assert b.shape[1] == K
            N = b.shape[2] * N_CHIPS
        else:
            assert mode == "nt"
            N = b.shape[1]
            assert b.shape[2] * N_CHIPS == K
    else:
        N = b.shape[0] if mode == "nt" else b.shape[1]
    tm, tn, tk = min(tm, M), min(tn, N), min(tk, K)
    assert M % tm == 0 and N % tn == 0 and K % tk == 0
    nk = K // tk
    if n_outer:
        grid = (N // tn, M // tm, nk)
        ij = lambda p, q_: (q_, p)
    else:
        grid = (M // tm, N // tn, nk)
        ij = lambda p, q_: (p, q_)

    def amap(p, q_, k):
        i, j = ij(p, q_)
        return (k, i) if mode == "tn" else (i, k)

    a_spec = pl.BlockSpec((tk, tm) if mode == "tn" else (tm, tk), amap)
    if b_shards:
        if mode == "nn":
            per = (N // N_CHIPS) // tn
            assert per >= 1 and (N // N_CHIPS) % tn == 0

            def bmap(p, q_, k):
                i, j = ij(p, q_)
                return (j // per, k, j % per)

            b_spec = pl.BlockSpec((None, tk, tn), bmap)
        else:
            per = (K // N_CHIPS) // tk
            assert per >= 1 and (K // N_CHIPS) % tk == 0

            def bmap(p, q_, k):
                i, j = ij(p, q_)
                return (k // per, j, k % per)

            b_spec = pl.BlockSpec((None, tn, tk), bmap)
    else:
        def bmap(p, q_, k):
            i, j = ij(p, q_)
            return (j, k) if mode == "nt" else (k, j)

        b_spec = pl.BlockSpec((tn, tk) if mode == "nt" else (tk, tn), bmap)

    def omap(p, q_, k):
        return ij(p, q_)

    o_spec = pl.BlockSpec((tm, tn), omap)
    if out_shards:
        per_o = (N // N_CHIPS) // tn
        assert per_o >= 1 and (N // N_CHIPS) % tn == 0

        def osmap(p, q_, k):
            i, j = ij(p, q_)
            return (j // per_o, i, j % per_o)

        out_spec0 = pl.BlockSpec((None, tm, tn), osmap)
        out_shape0 = jax.ShapeDtypeStruct((N_CHIPS, M, N // N_CHIPS), out_dtype)
    else:
        out_spec0 = o_spec
        out_shape0 = jax.ShapeDtypeStruct((M, N), out_dtype)

    in_specs = [a_spec, b_spec]
    args = [a, b]
    if res is not None:
        in_specs.append(o_spec)
        args.append(res)
    if aux is not None:
        in_specs.append(o_spec)
        args.append(aux)
    if vec is not None:
        in_specs.append(pl.BlockSpec((1, tn), lambda p, q_, k: (0, ij(p, q_)[1])))
        args.append(vec)
    after = [] if after is None else list(after) if isinstance(after, (list, tuple)) else [after]
    in_specs += [ANY] * len(after)
    args += after
    out_specs = [out_spec0]
    out_shape = [out_shape0]
    reduces = epi in ("loss", "rms_bwd")
    if reduces:
        assert tn == N and not n_outer and not out_shards
        width = LANES if epi == "loss" else N
        out_specs.append(pl.BlockSpec((1, width), lambda p, q_, k: (0, 0)))
        out_shape.append(jax.ShapeDtypeStruct((1, width), F32))
    if epi == "rms_next":
        assert tn == N and not out_shards
        out_specs.append(o_spec)
        out_shape.append(jax.ShapeDtypeStruct((M, N), BF))
    n_out = len(out_shape)
    has_res, has_aux, has_vec, n_after = res is not None, aux is not None, vec is not None, len(after)

    def body(*refs):
        a_ref, b_ref = refs[0], refs[1]
        pos = 2
        res_ref = aux_ref = vec_ref = None
        if has_res:
            res_ref = refs[pos]
            pos += 1
        if has_aux:
            aux_ref = refs[pos]
            pos += 1
        if has_vec:
            vec_ref = refs[pos]
            pos += 1
        pos += n_after
        outs = refs[pos:pos + n_out]
        part = _dot(a_ref[...].astype(BF), b_ref[...].astype(BF), mode)

        first_row_tile = pl.program_id(0) == 0

        def add_to_sum(row):
            @pl.when(first_row_tile)
            def _():
                outs[1][...] = jnp.zeros_like(outs[1])

            outs[1][...] += row

        def finish(acc):
            if epi == "rms_bwd":
                xv = aux_ref[...]
                r = lax.rsqrt(jnp.mean(xv * xv, axis=-1, keepdims=True) + EPS)
                xhat = xv * r
                dy = acc * vec_ref[...]
                outs[0][...] = res_ref[...] + r * (dy - xhat * jnp.mean(dy * xhat, axis=-1, keepdims=True))
                add_to_sum(jnp.sum(acc * xhat, axis=0, keepdims=True))
                return
            if res_ref is not None:
                acc = res_ref[...] + acc
            if epi == "relu2":
                r = jnp.maximum(acc, 0.0)
                outs[0][...] = (r * r).astype(out_dtype)
            elif epi == "drelu2":
                outs[0][...] = (acc.astype(BF) * (2.0 * jnp.sqrt(aux_ref[...]))).astype(out_dtype)
            elif epi == "rms_next":
                outs[0][...] = acc
                r = lax.rsqrt(jnp.mean(acc * acc, axis=-1, keepdims=True) + EPS)
                outs[1][...] = ((acc * r) * vec_ref[...]).astype(BF)
            elif epi == "loss":
                e = acc - aux_ref[...]
                outs[0][...] = e / float(N)
                add_to_sum(0.5 * jnp.sum(jnp.mean(e * e, axis=-1, keepdims=True)))
            else:
                outs[0][...] = acc.astype(out_dtype)

        if nk == 1:
            finish(part)
        else:
            acc_ref = refs[pos + n_out]
            k = pl.program_id(2)

            @pl.when(k == 0)
            def _():
                acc_ref[...] = part

            @pl.when(k > 0)
            def _():
                acc_ref[...] += part

            @pl.when(k == nk - 1)
            def _():
                finish(acc_ref[...])

    scratch = [pltpu.VMEM((tm, tn), F32)] if nk > 1 else []
    out = _pallas_call(
        body, name=name, grid=grid, in_specs=in_specs, out_specs=out_specs, out_shape=out_shape,
        scratch_shapes=scratch,
        compiler_params=_params(("arbitrary" if reduces else "parallel", "parallel", "arbitrary")),
    )(*args)
    return out if n_out > 1 else out[0]


def _rms_fwd(x, gain, name, after=None):
    T, D = x.shape
    tm = min(512, T)

    def body(x_ref, g_ref, *rest):
        o_ref = rest[-1]
        xv = x_ref[...]
        r = lax.rsqrt(jnp.mean(xv * xv, axis=-1, keepdims=True) + EPS)
        o_ref[...] = ((xv * r) * g_ref[...]).astype(BF)

    extra = [] if after is None else list(after) if isinstance(after, (list, tuple)) else [after]
    return _pallas_call(
        body, name=name, grid=(T // tm,),
        in_specs=[pl.BlockSpec((tm, D), lambda i: (i, 0)), pl.BlockSpec((1, D), lambda i: (0, 0))] + [ANY] * len(extra),
        out_specs=pl.BlockSpec((tm, D), lambda i: (i, 0)), out_shape=jax.ShapeDtypeStruct((T, D), BF),
        compiler_params=_params(("parallel",)),
    )(x, gain, *extra)


def _rms_bwd(dh, x, gain, dres, name):
    T, D = x.shape
    tm = min(512, T)

    def body(dh_ref, x_ref, g_ref, dres_ref, dx_ref, dg_ref):
        xv = x_ref[...]
        r = lax.rsqrt(jnp.mean(xv * xv, axis=-1, keepdims=True) + EPS)
        xhat = xv * r
        dhv = dh_ref[...]
        dy = dhv * g_ref[...]
        dx_ref[...] = dres_ref[...] + r * (dy - xhat * jnp.mean(dy * xhat, axis=-1, keepdims=True))

        @pl.when(pl.program_id(0) == 0)
        def _():
            dg_ref[...] = jnp.zeros_like(dg_ref)

        dg_ref[...] += jnp.sum(dhv * xhat, axis=0, keepdims=True)

    row = pl.BlockSpec((tm, D), lambda i: (i, 0))
    vec = pl.BlockSpec((1, D), lambda i: (0, 0))
    return _pallas_call(
        body, name=name, grid=(T // tm,), in_specs=[row, row, vec, row], out_specs=[row, vec],
        out_shape=[jax.ShapeDtypeStruct((T, D), F32), jax.ShapeDtypeStruct((1, D), F32)],
        compiler_params=_params(("arbitrary",)),
    )(dh, x, gain, dres)


POOL_HALO = 16
CONV_HALO = 8
POOLCONV_ROWS = 512


def _causal_window_sum(v, w):
    s, sh = v, 1
    while sh < w:
        s = s + pltpu.roll(s, sh, 0)
        sh *= 2
    return s


def _anticausal_window_sum(v, w):
    n = v.shape[0]
    s, sh = v, 1
    while sh < w:
        s = s + pltpu.roll(s, n - sh, 0)
        sh *= 2
    return s


def _poolconv_fwd(z, pmix_b, pscale, convw, name):
    T = z.shape[0]
    R = min(POOLCONV_ROWS, T)
    PH, CH = R // POOL_HALO, R // CONV_HALO

    def body(u_ref, uh_ref, b_ref, c_ref, ch_ref, x_ref, xh_ref, mix_ref, sc_ref, cw_ref, yp_ref, yc_ref):
        i = pl.program_id(0)
        keep = (i > 0).astype(F32)
        row = i * R + lax.broadcasted_iota(jnp.int32, (R, 1), 0)
        w_all = jnp.concatenate([uh_ref[...] * keep, u_ref[...]], axis=0)
        for g, w in enumerate(POOL_WINDOWS):
            cols = slice(128 * g, 128 * (g + 1))
            wg = w_all[:, cols]
            s = _causal_window_sum(wg, w)[POOL_HALO:]
            inv_cnt = 1.0 / jnp.minimum(row + 1, w).astype(F32)
            dgrp = s * inv_cnt - wg[POOL_HALO:]
            y = _dot(dgrp.astype(BF), mix_ref[g]) * sc_ref[:, cols]
            yp_ref[:, cols] = y.astype(BF)
        uc = jnp.concatenate([ch_ref[...] * xh_ref[...] * keep, c_ref[...] * x_ref[...]], axis=0)
        yc = cw_ref[2:3, :] * uc + cw_ref[0:1, :] * pltpu.roll(uc, 2, 0) + cw_ref[1:2, :] * pltpu.roll(uc, 1, 0)
        yc_ref[...] = (b_ref[...] * yc[CONV_HALO:]).astype(BF)

    def main(cb):
        return pl.BlockSpec((R, 512), lambda i: (i, cb))

    def prev(cb, halo, per):
        return pl.BlockSpec((halo, 512), lambda i: (jnp.maximum(i * per - 1, 0), cb))

    full = lambda a: pl.BlockSpec(a.shape, lambda i: (0,) * a.ndim)
    return _pallas_call(
        body, name=name, grid=(T // R,),
        in_specs=[main(0), prev(0, POOL_HALO, PH), main(1), main(2), prev(2, CONV_HALO, CH), main(3),
                  prev(3, CONV_HALO, CH), full(pmix_b), full(pscale), full(convw)],
        out_specs=[pl.BlockSpec((R, 512), lambda i: (i, 0))] * 2,
        out_shape=[jax.ShapeDtypeStruct((T, 512), BF)] * 2,
        compiler_params=_params(("parallel",)),
    )(z, z, z, z, z, z, z, pmix_b, pscale, convw)


def _poolconv_bwd(z, dyp, dyc, pmix_b, pscale, convw, dz, name):
    T = z.shape[0]
    R = min(POOLCONV_ROWS, T)
    PH, CH = R // POOL_HALO, R // CONV_HALO
    nsteps = T // R

    def body(u_ref, uh_ref, b_ref, bn_ref, c_ref, ch_ref, x_ref, xh_ref, dyp_ref, dypn_ref, dyc_ref, dycn_ref,
             mix_ref, sc_ref, cw_ref, dz_in_ref, dz_ref, dmix_ref, dsc_ref, dcw_ref):
        i = pl.program_id(0)
        keep_prev = (i > 0).astype(F32)
        keep_next = (i < nsteps - 1).astype(F32)

        @pl.when(i == 0)
        def _():
            dmix_ref[...] = jnp.zeros_like(dmix_ref)
            dsc_ref[...] = jnp.zeros_like(dsc_ref)
            dcw_ref[...] = jnp.zeros_like(dcw_ref)

        row = i * R + lax.broadcasted_iota(jnp.int32, (R, 1), 0)
        row_ext = i * R + lax.broadcasted_iota(jnp.int32, (R + POOL_HALO, 1), 0)
        w_all = jnp.concatenate([uh_ref[...] * keep_prev, u_ref[...]], axis=0)
        dyp_ext = jnp.concatenate([dyp_ref[...], dypn_ref[...] * keep_next], axis=0)
        for g, w in enumerate(POOL_WINDOWS):
            cols = slice(128 * g, 128 * (g + 1))
            wg = w_all[:, cols]
            s = _causal_window_sum(wg, w)[POOL_HALO:]
            inv_cnt = 1.0 / jnp.minimum(row + 1, w).astype(F32)
            dgrp = (s * inv_cnt - wg[POOL_HALO:]).astype(BF)
            y_pre = _dot(dgrp, mix_ref[g])
            dsc_ref[:, cols] += jnp.sum(dyp_ref[:, cols] * y_pre, axis=0, keepdims=True)
            dyb = (dyp_ext[:, cols] * sc_ref[:, cols]).astype(BF)
            dmix_ref[cols, :] += _dot(dgrp, dyb[:R], "tn")
            dd = _dot(dyb, mix_ref[g], "nt")
            inv_cnt_ext = 1.0 / jnp.minimum(row_ext + 1, w).astype(F32)
            e = _anticausal_window_sum(dd * inv_cnt_ext, w)
            dz_ref[:, cols] = (e[:R] - dd[:R]).astype(BF)
        cw0, cw1, cw2 = cw_ref[0:1, :], cw_ref[1:2, :], cw_ref[2:3, :]
        uc = jnp.concatenate([ch_ref[...] * xh_ref[...] * keep_prev, c_ref[...] * x_ref[...]], axis=0)
        uc1 = pltpu.roll(uc, 1, 0)[CONV_HALO:]
        uc2 = pltpu.roll(uc, 2, 0)[CONV_HALO:]
        uc0 = uc[CONV_HALO:]
        yc = cw2 * uc0 + cw0 * uc2 + cw1 * uc1
        dycv = dyc_ref[...]
        dz_ref[:, 512:1024] = (dycv * yc).astype(BF)
        dv_ext = jnp.concatenate([dycv * b_ref[...], dycn_ref[...] * bn_ref[...] * keep_next], axis=0)
        n_ext = R + CONV_HALO
        duc = (cw2 * dv_ext + cw1 * pltpu.roll(dv_ext, n_ext - 1, 0) + cw0 * pltpu.roll(dv_ext, n_ext - 2, 0))[:R]
        dv = dv_ext[:R]
        dcw_ref[0:1, :] += jnp.sum(dv * uc2, axis=0, keepdims=True)
        dcw_ref[1:2, :] += jnp.sum(dv * uc1, axis=0, keepdims=True)
        dcw_ref[2:3, :] += jnp.sum(dv * uc0, axis=0, keepdims=True)
        dz_ref[:, 1024:1536] = (duc * x_ref[...]).astype(BF)
        dz_ref[:, 1536:2048] = (duc * c_ref[...]).astype(BF)

    def main(cb):
        return pl.BlockSpec((R, 512), lambda i: (i, cb))

    def prev(cb, halo, per):
        return pl.BlockSpec((halo, 512), lambda i: (jnp.maximum(i * per - 1, 0), cb))

    def nxt(cb, halo, per):
        return pl.BlockSpec((halo, 512), lambda i: (jnp.minimum((i + 1) * per, T // halo - 1), cb))

    full = lambda a: pl.BlockSpec(a.shape, lambda i: (0,) * a.ndim)
    return _pallas_call(
        body, name=name, grid=(nsteps,),
        in_specs=[main(0), prev(0, POOL_HALO, PH), main(1), nxt(1, CONV_HALO, CH), main(2), prev(2, CONV_HALO, CH),
                  main(3), prev(3, CONV_HALO, CH), main(0), nxt(0, POOL_HALO, PH), main(0), nxt(0, CONV_HALO, CH),
                  full(pmix_b), full(pscale), full(convw), ANY],
        out_specs=[pl.BlockSpec((R, 2048), lambda i: (i, 0)), pl.BlockSpec((512, 128), lambda i: (0, 0)),
                   pl.BlockSpec((1, 512), lambda i: (0, 0)), pl.BlockSpec((8, 512), lambda i: (0, 0))],
        out_shape=[jax.ShapeDtypeStruct(dz.shape, BF), jax.ShapeDtypeStruct((512, 128), F32),
                   jax.ShapeDtypeStruct((1, 512), F32), jax.ShapeDtypeStruct((8, 512), F32)],
        input_output_aliases={15: 0}, compiler_params=_params(("arbitrary",)),
    )(z, z, z, z, z, z, z, z, dyp, dyp, dyc, dyc, pmix_b, pscale, convw, dz)


def _head_sums(v):
    row = lax.broadcasted_iota(jnp.int32, (LANES, LANES), 0) < HEAD_DIM
    col = lax.broadcasted_iota(jnp.int32, (LANES, LANES), 1) < HEAD_DIM
    same_head = jnp.where(jnp.logical_xor(row, col), 0.0, 1.0).astype(BF)
    hi = v.astype(BF)
    lo = (v - hi.astype(F32)).astype(BF)
    return _dot(hi, same_head) + _dot(lo, same_head)


def _head_norm(x, g2, ma):
    r = lax.rsqrt(_head_sums(x * x) / HEAD_DIM + EPS)
    return x * r, r


def _head_norm_bwd(dy, xhat, r, g2, ma):
    dxh = dy * g2
    return r * (dxh - xhat * (_head_sums(dxh * xhat) / HEAD_DIM))


def _attn_masks(other_block_exists):
    lane = lax.broadcasted_iota(jnp.int32, (2 * ATTN_BLOCK, ATTN_BLOCK), 1)
    qi = lax.broadcasted_iota(jnp.int32, (2 * ATTN_BLOCK, ATTN_BLOCK), 0) & (ATTN_BLOCK - 1)
    never = (1 - other_block_exists.astype(jnp.int32)) * (2 * ATTN_BLOCK)
    return lane[:ATTN_BLOCK] < HEAD_DIM, lane <= qi, lane >= qi + never


def _stack_heads(x, ma):
    return jnp.concatenate([jnp.where(ma, x, 0.0), jnp.where(ma, 0.0, x)], axis=0)


def _unstack_heads(y, ma):
    return jnp.where(ma, y[:ATTN_BLOCK], y[ATTN_BLOCK:])


def _stack_cols(tile, ma):
    return jnp.concatenate([tile[:, 0:1], tile[:, HEAD_DIM:HEAD_DIM + 1]], axis=0)


QKV_TILES = (OFF_GATE - OFF_Q) // LANES
KIND_TILES = QKV_TILES // 3


def _qk_norm(z, gains, name):
    T = z.shape[0]
    tm = min(512, T)

    def body(x_ref, g_ref, o_ref):
        ma = lax.broadcasted_iota(jnp.int32, (tm, LANES), 1) < HEAD_DIM
        for tile in range(QKV_TILES):
            v = x_ref[:, LANES * tile:LANES * (tile + 1)]
            if tile < 2 * KIND_TILES:
                g = g_ref[0:1, :] if tile < KIND_TILES else g_ref[1:2, :]
                v = _head_norm(v, g, ma)[0] * g
            o_ref[tile] = v

    return _pallas_call(
        body, name=name, grid=(T // tm,),
        in_specs=[pl.BlockSpec((pl.Element(tm), pl.Element(OFF_GATE - OFF_Q)), lambda i: (i * tm, OFF_Q)),
                  pl.BlockSpec((8, LANES), lambda i: (0, 0))],
        out_specs=pl.BlockSpec((QKV_TILES, tm, LANES), lambda i: (0, i, 0)),
        out_shape=jax.ShapeDtypeStruct((QKV_TILES, T, LANES), F32), compiler_params=_params(("parallel",)),
    )(z, gains)


ATTN_STEP_ROWS = 2048
ATTN_UNROLL = 4


def _attn_steps(T):
    assert ATTN_STEP_ROWS == ATTN_BLOCK * max(ATTN_DILATIONS) and T % ATTN_STEP_ROWS == 0
    return T // ATTN_STEP_ROWS


def _attn_rows(jj, r, sub, d):
    start = jj * sub + r
    if d == 1:
        return pl.ds(pl.multiple_of(start, ATTN_BLOCK), ATTN_BLOCK)
    return pl.ds(start, ATTN_BLOCK, stride=d)


def _pick(flag, a, b):
    return jnp.where(jnp.full(a.shape, flag.astype(jnp.int32)) > 0, a, b)


def _attn_fwd(qkv, name):
    T = qkv.shape[1]
    nbig = _attn_steps(T)
    scale = HEAD_DIM ** -0.5

    def body(q_ref, kc_ref, kp_ref, vc_ref, vp_ref, o_ref, lse_ref):
        jb = pl.program_id(1)
        for gi, d in enumerate(ATTN_DILATIONS):
            pl.when(pl.program_id(0) == gi)(functools.partial(group, d, jb, q_ref, kc_ref, kp_ref, vc_ref, vp_ref,
                                                              o_ref, lse_ref))

    def group(d, jb, q_ref, kc_ref, kp_ref, vc_ref, vp_ref, o_ref, lse_ref):
        sub, m = ATTN_BLOCK * d, ATTN_STEP_ROWS // (ATTN_BLOCK * d)

        def step(s, carry):
            jj, r = s // d, s % d
            here, before = _attn_rows(jj, r, sub, d), _attn_rows(jnp.maximum(jj - 1, 0), r, sub, d)
            edge = _attn_rows(m - 1, r, sub, d)
            first = jj == 0
            ma, mask_c, mask_p = _attn_masks(jb * m + jj > 0)
            qs = _stack_heads(q_ref[here, :], ma).astype(BF)
            kcb = kc_ref[here, :].astype(BF)
            kpb = _pick(first, kp_ref[edge, :], kc_ref[before, :]).astype(BF)
            vcb = vc_ref[here, :].astype(BF)
            vpb = _pick(first, vp_ref[edge, :], vc_ref[before, :]).astype(BF)
            s_c = jnp.where(mask_c, _dot(qs, kcb, "nt") * scale, MASK_VALUE)
            s_p = jnp.where(mask_p, _dot(qs, kpb, "nt") * scale, MASK_VALUE)
            mx = jnp.maximum(jnp.max(s_c, axis=-1, keepdims=True), jnp.max(s_p, axis=-1, keepdims=True))
            p_c = jnp.exp(s_c - mx)
            p_p = jnp.exp(s_p - mx)
            den = jnp.sum(p_c, axis=-1, keepdims=True) + jnp.sum(p_p, axis=-1, keepdims=True)
            o = (_dot(p_c.astype(BF), vcb) + _dot(p_p.astype(BF), vpb)) / den
            o_ref[here, :] = _unstack_heads(o, ma)
            lse_ref[here, :] = _unstack_heads(jnp.broadcast_to(mx + jnp.log(den), o.shape), ma)
            return carry

        lax.fori_loop(0, m * d, step, 0, unroll=ATTN_UNROLL)

    def cur(kind):
        return pl.BlockSpec((None, ATTN_STEP_ROWS, LANES), lambda g, j, t: (KIND_TILES * kind + 2 * g + t, j, 0))

    def prv(kind):
        return pl.BlockSpec((None, ATTN_STEP_ROWS, LANES),
                            lambda g, j, t: (KIND_TILES * kind + 2 * g + t, jnp.maximum(j - 1, 0), 0))

    out = pl.BlockSpec((ATTN_STEP_ROWS, LANES), lambda g, j, t: (j, 2 * g + t))
    width = 2 * LANES * len(ATTN_DILATIONS)
    return _pallas_call(
        body, name=name, grid=(len(ATTN_DILATIONS), nbig, 2), in_specs=[cur(0), cur(1), prv(1), cur(2), prv(2)],
        out_specs=[out, out], out_shape=[jax.ShapeDtypeStruct((T, width), F32)] * 2,
        compiler_params=_params(("parallel", "parallel", "parallel")),
    )(qkv, qkv, qkv, qkv, qkv)


def _attn_bwd(z, qkv, do, c, lse, gains, name, after=None):
    T = z.shape[0]
    nbig = _attn_steps(T)
    scale = HEAD_DIM ** -0.5
    extra = [] if after is None else [after]

    def body(*refs):
        g, jb = pl.program_id(0), pl.program_id(1)
        dgq_ref, dgk_ref = refs[len(refs) - 5], refs[len(refs) - 4]

        @pl.when((g == 0) & (jb == 0) & (pl.program_id(2) == 0))
        def _():
            dgq_ref[...] = jnp.zeros_like(dgq_ref)
            dgk_ref[...] = jnp.zeros_like(dgk_ref)

        for gi, d in enumerate(ATTN_DILATIONS):
            pl.when(g == gi)(functools.partial(group, d, jb, *refs))

    def group(d, jb, qr_ref, kr_ref, vc_ref, vp_ref, qn_ref, qnn_ref, kn_ref, knp_ref, do_ref, don_ref, c_ref, cn_ref,
              lse_ref, lsen_ref, g_ref, *rest):
        dq_ref, dk_ref, dv_ref, dgq_ref, dgk_ref, sq_ref, sk_ref, sv_ref = rest[len(extra):]
        sub, m = ATTN_BLOCK * d, ATTN_STEP_ROWS // (ATTN_BLOCK * d)
        nb = T // sub
        gq, gk = g_ref[0:1, :], g_ref[1:2, :]

        def step(s, carry):
            jj, r = s // d, s % d
            here = _attn_rows(jj, r, sub, d)
            before = _attn_rows(jnp.maximum(jj - 1, 0), r, sub, d)
            behind = _attn_rows(jnp.minimum(jj + 1, m - 1), r, sub, d)
            edge_before, edge_behind = _attn_rows(m - 1, r, sub, d), _attn_rows(0, r, sub, d)
            first, last = jj == 0, jj == m - 1
            block = jb * m + jj
            ma, mask_c, mask_p = _attn_masks(block > 0)
            mask_n = _attn_masks(block < nb - 1)[2]
            qhat, rq = _head_norm(qr_ref[here, :], gq, ma)
            qn = qn_ref[here, :]
            qn_next = _pick(last, qnn_ref[edge_behind, :], qn_ref[behind, :])
            khat, rk = _head_norm(kr_ref[here, :], gk, ma)
            kcb = kn_ref[here, :].astype(BF)
            kpb = _pick(first, knp_ref[edge_before, :], kn_ref[before, :]).astype(BF)
            vcb = vc_ref[here, :].astype(BF)
            vpb = _pick(first, vp_ref[edge_before, :], vc_ref[before, :]).astype(BF)
            do_t, don_t = do_ref[here, :], _pick(last, don_ref[edge_behind, :], do_ref[behind, :])
            c_t, cn_t = c_ref[here, :], _pick(last, cn_ref[edge_behind, :], c_ref[behind, :])
            lse_t, lsen_t = lse_ref[here, :], _pick(last, lsen_ref[edge_behind, :], lse_ref[behind, :])
            qs, dos = _stack_heads(qn, ma).astype(BF), _stack_heads(do_t, ma).astype(BF)
            lse_s, c_s = _stack_cols(lse_t, ma), _stack_cols(c_t, ma)
            s_c = jnp.where(mask_c, _dot(qs, kcb, "nt") * scale, MASK_VALUE)
            s_p = jnp.where(mask_p, _dot(qs, kpb, "nt") * scale, MASK_VALUE)
            p_c = jnp.exp(s_c - lse_s)
            p_p = jnp.exp(s_p - lse_s)
            ds_c = ((p_c * (_dot(dos, vcb, "nt") + c_s)) * scale).astype(BF)
            ds_p = ((p_p * (_dot(dos, vpb, "nt") + c_s)) * scale).astype(BF)
            dq_t = _unstack_heads(_dot(ds_c, kcb) + _dot(ds_p, kpb), ma)
            qs_n, dos_n = _stack_heads(qn_next, ma).astype(BF), _stack_heads(don_t, ma).astype(BF)
            s_n = jnp.where(mask_n, _dot(qs_n, kcb, "nt") * scale, MASK_VALUE)
            p_n = jnp.exp(s_n - _stack_cols(lsen_t, ma))
            ds_n = ((p_n * (_dot(dos_n, vcb, "nt") + _stack_cols(cn_t, ma))) * scale).astype(BF)
            dv_t = _dot(p_c.astype(BF), dos, "tn") + _dot(p_n.astype(BF), dos_n, "tn")
            dk_t = _dot(ds_c, qs, "tn") + _dot(ds_n, qs_n, "tn")
            sq_ref[here, :] = _head_norm_bwd(dq_t, qhat, rq, gq, ma)
            sk_ref[here, :] = _head_norm_bwd(dk_t, khat, rk, gk, ma)
            sv_ref[here, :] = dv_t
            dgq_ref[...] += jnp.sum(dq_t * qhat, axis=0, keepdims=True)
            dgk_ref[...] += jnp.sum(dk_t * khat, axis=0, keepdims=True)
            return carry

        lax.fori_loop(0, m * d, step, 0, unroll=ATTN_UNROLL)
        dq_ref[...] = sq_ref[...].astype(BF)
        dk_ref[...] = sk_ref[...].astype(BF)
        dv_ref[...] = sv_ref[...].astype(BF)

    rows = ATTN_STEP_ROWS

    def raw(col0):
        return pl.BlockSpec((rows, LANES), lambda g, j, t: (j, col0 + 2 * g + t))

    def cur(kind):
        return pl.BlockSpec((None, rows, LANES), lambda g, j, t: (KIND_TILES * kind + 2 * g + t, j, 0))

    def prv(kind):
        return pl.BlockSpec((None, rows, LANES), lambda g, j, t: (KIND_TILES * kind + 2 * g + t, jnp.maximum(j - 1, 0), 0))

    def nxt(kind):
        return pl.BlockSpec((None, rows, LANES),
                            lambda g, j, t: (KIND_TILES * kind + 2 * g + t, jnp.minimum(j + 1, nbig - 1), 0))

    own = pl.BlockSpec((rows, LANES), lambda g, j, t: (j, 2 * g + t))
    own_next = pl.BlockSpec((rows, LANES), lambda g, j, t: (jnp.minimum(j + 1, nbig - 1), 2 * g + t))
    vec = pl.BlockSpec((1, LANES), lambda g, j, t: (0, 0))
    width = 2 * LANES * len(ATTN_DILATIONS)
    return _pallas_call(
        body, name=name, grid=(len(ATTN_DILATIONS), nbig, 2),
        in_specs=[raw(OFF_Q // LANES), raw(OFF_K // LANES), cur(2), prv(2), cur(0), nxt(0), cur(1), prv(1), own, own_next,
                  own, own_next, own, own_next, pl.BlockSpec((8, LANES), lambda g, j, t: (0, 0))] + [ANY] * len(extra),
        out_specs=[own, own, own, vec, vec],
        out_shape=[jax.ShapeDtypeStruct((T, width), BF)] * 3 + [jax.ShapeDtypeStruct((1, LANES), F32)] * 2,
        scratch_shapes=[pltpu.VMEM((rows, LANES), F32)] * 3,
        compiler_params=_params(("arbitrary", "arbitrary", "arbitrary")),
    )(z, z, qkv, qkv, qkv, qkv, qkv, qkv, do, do, c, c, lse, lse, gains, *extra)


MERGE_ROWS = 256
GATE_TILE = 256


def _group_mix(o_refs, lse_refs):
    lses = [r[...] for r in lse_refs]
    m = jnp.maximum(jnp.maximum(lses[0], lses[1]), lses[2])
    es = [jnp.exp(l - m) for l in lses]
    den = es[0] + es[1] + es[2]
    ws = [e / den for e in es]
    y = ws[0] * o_refs[0][...] + ws[1] * o_refs[1][...] + ws[2] * o_refs[2][...]
    return ws, y


def _sigmoid(v):
    return 1.0 / (1.0 + jnp.exp(-v))


def _merge_specs(T, z, bgate, gpu, gco, gau):
    tm = min(MERGE_ROWS, T)
    row = lambda w: pl.BlockSpec((tm, w), lambda i: (i, 0))
    gate0 = OFF_GATE // GATE_TILE
    gates = [pl.BlockSpec((tm, GATE_TILE), functools.partial(lambda i, cb: (i, cb), cb=gate0 + n))
             for n in range(3 * N_CHIPS)]
    full = lambda a: pl.BlockSpec(a.shape, lambda i: (0,) * a.ndim)
    by_group = [pl.BlockSpec((tm, 256), functools.partial(lambda i, g: (i, g), g=g)) for g in range(3)]
    specs = [row(512), row(512)] + by_group * 2 + gates + [full(bgate), full(gpu), full(gco), full(gau)]
    return tm, row, specs


def _merge_fwd(yp, yc, o3, lse3, z, bgate, gpu, gco, gau, name):
    T = yp.shape[0]
    tm, row, specs = _merge_specs(T, z, bgate, gpu, gco, gau)

    def body(*refs):
        yp_ref, yc_ref = refs[0], refs[1]
        o_refs, lse_refs = refs[2:5], refs[5:8]
        zg = refs[8:20]
        b_ref, gpu_ref, gco_ref, gau_ref, out_ref = refs[20:25]
        yab = _group_mix(o_refs, lse_refs)[1].astype(BF)
        ys = (yp_ref[...], yc_ref[...], yab)
        ups = (gpu_ref, gco_ref, gau_ref)
        for n in range(N_CHIPS):
            acc = None
            for b in range(3):
                gcol = slice(1024 * b + GATE_TILE * n, 1024 * b + GATE_TILE * (n + 1))
                gate = _sigmoid(zg[N_CHIPS * b + n][...] + b_ref[:, gcol])
                term = gate * _dot(ys[b], ups[b][n])
                acc = term if acc is None else acc + term
            out_ref[:, GATE_TILE * n:GATE_TILE * (n + 1)] = acc.astype(BF)

    return _pallas_call(
        body, name=name, grid=(T // tm,), in_specs=specs, out_specs=row(1024),
        out_shape=jax.ShapeDtypeStruct((T, 1024), BF), compiler_params=_params(("parallel",)),
    )(yp, yc, *([o3] * 3), *([lse3] * 3), *([z] * 12), bgate, gpu, gco, gau)


def _merge_bwd(dm, yp, yc, o3, lse3, z, bgate, gpu, gco, gau, name):
    T = yp.shape[0]
    tm, row, specs = _merge_specs(T, z, bgate, gpu, gco, gau)
    nsteps = T // tm

    def body(*refs):
        dm_ref, yp_ref, yc_ref = refs[0:3]
        o_refs, lse_refs = refs[3:6], refs[6:9]
        zg = refs[9:21]
        b_ref, gpu_ref, gco_ref, gau_ref = refs[21:25]
        dzg_ref, dyp_ref, dyc_ref = refs[25:28]
        do_ref, c_ref = refs[28:30]
        dgpu_ref, dgco_ref, dgau_ref, dbg_ref = refs[30:34]
        accs = refs[34:37]
        i = pl.program_id(0)

        @pl.when(i == 0)
        def _():
            for a in accs:
                a[...] = jnp.zeros_like(a)
            dbg_ref[...] = jnp.zeros_like(dbg_ref)

        ws, y = _group_mix(o_refs, lse_refs)
        ys = (yp_ref[...], yc_ref[...], y.astype(BF))
        ups = (gpu_ref, gco_ref, gau_ref)
        dys = [None, None, None]
        for n in range(N_CHIPS):
            dmn = dm_ref[:, GATE_TILE * n:GATE_TILE * (n + 1)]
            for b in range(3):
                gcol = slice(1024 * b + GATE_TILE * n, 1024 * b + GATE_TILE * (n + 1))
                gate = _sigmoid(zg[N_CHIPS * b + n][...] + b_ref[:, gcol])
                up = _dot(ys[b], ups[b][n])
                dzg = (dmn * up) * (gate * (1.0 - gate))
                dzg_ref[:, gcol] = dzg.astype(BF)
                dbg_ref[:, gcol] += jnp.sum(dzg, axis=0, keepdims=True)
                dup = (dmn * gate).astype(BF)
                accs[b][n] += _dot(ys[b], dup, "tn")
                dyb = _dot(dup, ups[b][n], "nt")
                dys[b] = dyb if dys[b] is None else dys[b] + dyb
        dyp_ref[...] = dys[0]
        dyc_ref[...] = dys[1]
        dya = dys[2]
        lane = lax.broadcasted_iota(jnp.int32, dya.shape, 1) // HEAD_DIM
        pr = dya * y
        rho = jnp.zeros_like(pr)
        for h in range(256 // HEAD_DIM):
            hm = lane == h
            rho = jnp.where(hm, jnp.sum(jnp.where(hm, pr, 0.0), axis=-1, keepdims=True), rho)
        for g in range(3):
            do_ref[:, 256 * g:256 * (g + 1)] = ws[g] * dya
            c_ref[:, 256 * g:256 * (g + 1)] = -(ws[g] * rho)

        @pl.when(i == nsteps - 1)
        def _():
            dgpu_ref[...] = accs[0][...].astype(BF)
            dgco_ref[...] = accs[1][...].astype(BF)
            dgau_ref[...] = accs[2][...].astype(BF)

    full = lambda a: pl.BlockSpec(a.shape, lambda i: (0,) * a.ndim)
    dz_gate = pl.BlockSpec((pl.Element(tm), pl.Element(3072)), lambda i: (i * tm, OFF_GATE))
    out_specs = ([dz_gate, row(512), row(512)] + [row(768)] * 2 + [full(gpu), full(gco), full(gau)]
                 + [pl.BlockSpec((1, 3072), lambda i: (0, 0))])
    out_shape = ([jax.ShapeDtypeStruct(z.shape, BF)] + [jax.ShapeDtypeStruct((T, 512), F32)] * 2
                 + [jax.ShapeDtypeStruct((T, 768), F32)] * 2
                 + [jax.ShapeDtypeStruct(g.shape, BF) for g in (gpu, gco, gau)]
                 + [jax.ShapeDtypeStruct((1, 3072), F32)])
    return _pallas_call(
        body, name=name, grid=(nsteps,), in_specs=[row(1024)] + specs, out_specs=out_specs, out_shape=out_shape,
        scratch_shapes=[pltpu.VMEM(g.shape, F32) for g in (gpu, gco, gau)],
        compiler_params=_params(("arbitrary",)),
    )(dm, yp, yc, *([o3] * 3), *([lse3] * 3), *([z] * 12), bgate, gpu, gco, gau)


def _layer_fwd(x, w, tag, after=None, soon=None, late=None, last=None, target=None, hb=None, next_gain=None):
    if hb is None:
        hb = _rms_fwd(x, w["norm_mix"], f"rms_mix_{tag}", after=after)
    if soon is not None:
        w = dict(w, **soon(hb))
    z = _mm(hb, w["w_in"], "nt", f"in_proj_{tag}", tm=512, tn=3712, tk=1024, n_outer=True, after=w.get("started"))
    yp, yc = _poolconv_fwd(z, w["pool_mix"], w["pool_scale"], w["conv_w"], f"poolconv_{tag}")
    qkv = _qk_norm(z, w["qk_gain"], f"qk_norm_{tag}")
    o3, lse3 = _attn_fwd(qkv, f"attn_{tag}")
    if late is not None:
        w = dict(w, **late(lse3))
    merged = _merge_fwd(yp, yc, o3, lse3, z, w["b_gate"], w["w_pool_up"], w["w_conv_out"], w["w_attn_up"],
                        f"merge_{tag}")
    x1, h2b = _mm(merged, w["w_o"], "nn", f"out_proj_{tag}", tm=1024, tn=1024, tk=1024, res=x, vec=w["norm_mlp"],
                  epi="rms_next", after=w.get("crossing"))
    if last is not None:
        w = dict(w, **last(x1))
    rb = _mm(h2b, w["w_ff1"], "nn", f"ff1_{tag}", tm=1024, tn=1024, tk=1024, out_dtype=BF, epi="relu2", n_outer=True,
             b_shards=True)
    if target is not None:
        x2 = _mm(rb, w["w_ff2"], "nn", f"ff2_{tag}", tm=512, tn=1024, tk=4096, res=x1, aux=target, epi="loss")
    elif next_gain is not None:
        x2 = _mm(rb, w["w_ff2"], "nn", f"ff2_{tag}", tm=512, tn=1024, tk=4096, res=x1, vec=next_gain, epi="rms_next")
    else:
        x2 = _mm(rb, w["w_ff2"], "nn", f"ff2_{tag}", tm=512, tn=1024, tk=4096, res=x1)
    saved = dict(x=x, hb=hb, z=z, yp=yp, yc=yc, qkv=qkv, o3=o3, lse3=lse3, merged=merged, x1=x1, h2b=h2b, rb=rb)
    return x2, saved, w


def _layer_bwd(dx2, w, s, tag, after=None, head=None, mid=None, tail=None):
    g = {}
    dab = _mm(dx2, w["w_ff2"], "nt", f"d_ff2_act_{tag}", tm=1024, tn=1024, tk=1024, out_dtype=BF, aux=s["rb"],
              epi="drelu2", after=after)
    front = head(dab) if head is not None else None
    g["w_ff2"] = _mm(s["rb"], dx2, "tn", f"d_ff2_w_{tag}", tm=1024, tn=1024, tk=2048, out_dtype=BF, after=front)
    g["w_ff1"] = _mm(s["h2b"], dab, "tn", f"d_ff1_w_{tag}", tm=1024, tn=1024, tk=2048, out_dtype=BF, out_shards=True)
    dx1, g["norm_mlp"] = _mm(dab, w["w_ff1"], "nt", f"d_ff1_act_{tag}", tm=1024, tn=1024, tk=1024, b_shards=True,
                             res=dx2, aux=s["x1"], vec=w["norm_mlp"], epi="rms_bwd")
    dm = _mm(dx1, w["w_o"], "nt", f"d_out_act_{tag}", tm=1024, tn=1024, tk=1024)
    g["w_o"] = _mm(s["merged"], dx1, "tn", f"d_out_w_{tag}", tm=1024, tn=1024, tk=1024, out_dtype=BF)
    (dz, dyp, dyc, do3, c3, g["w_pool_up"], g["w_conv_out"], g["w_attn_up"],
     g["b_gate"]) = _merge_bwd(dm, s["yp"], s["yc"], s["o3"], s["lse3"], s["z"], w["b_gate"], w["w_pool_up"],
                               w["w_conv_out"], w["w_attn_up"], f"d_merge_{tag}")
    behind = mid(g) if mid is not None else None
    dzq, dzk, dzv, dgq, dgk = _attn_bwd(s["z"], s["qkv"], do3, c3, s["lse3"], w["qk_gain"], f"d_attn_{tag}",
                                        after=behind)
    g["q_gain"] = dgq[:, :HEAD_DIM] + dgq[:, HEAD_DIM:]
    g["k_gain"] = dgk[:, :HEAD_DIM] + dgk[:, HEAD_DIM:]
    for off, piece in ((OFF_Q, dzq), (OFF_K, dzk), (OFF_V, dzv)):
        dz = lax.dynamic_update_slice(dz, piece, (0, off))
    dz, g["pool_mix"], g["pool_scale"], g["conv_w"] = _poolconv_bwd(
        s["z"], dyp, dyc, w["pool_mix"], w["pool_scale"], w["conv_w"], dz, f"d_poolconv_{tag}")
    g["w_in"] = _mm(s["hb"], dz, "tn", f"d_in_w_{tag}", tm=512, tn=3712, tk=1024, out_dtype=BF)
    dh = _mm(dz, w["w_in"], "nn", f"d_in_act_{tag}", tm=1024, tn=1024, tk=3712,
             after=tail(g) if tail is not None else None)
    dx, g["norm_mix"] = _rms_bwd(dh, s["x"], w["norm_mix"], dx1, f"d_rms_mix_{tag}")
    return dx, g


def _position():
    x, y, c = lax.axis_index("x"), lax.axis_index("y"), lax.axis_index("c")
    chips = [(1 - x, y), (x, 1 - y), (1 - x, 1 - y)]
    return x, y, c, 2 * x + y, chips, [2 * cx + cy for cx, cy in chips]


def _remote(src, dst, ssem, rsem, dev):
    return pltpu.make_async_remote_copy(src_ref=src, dst_ref=dst, send_sem=ssem, recv_sem=rsem, device_id=dev,
                                        device_id_type=MESH_ID)


def _halves(a):
    return a.reshape(a.shape[0], 2, a.shape[1] // 2, a.shape[2])


SEM = pl.BlockSpec(memory_space=pltpu.SEMAPHORE)
TOKEN = jax.ShapeDtypeStruct((8, LANES), F32)
TOKEN_SPEC = pl.BlockSpec(memory_space=pltpu.VMEM)


def _split_params():
    return pltpu.CompilerParams(has_side_effects=pltpu.SideEffectType.DATAFLOW_SIDE_EFFECTING)


def _ici_plan(x, y, c, q, chips, qs):
    return [((q, c), (qs[j], c), (chips[j][0], chips[j][1], c)) for j in range(3)]


def _sibling_plan(x, y, c, q, chips, qs):
    return [((qs[j], c), (qs[j], 1 - c), (x, y, 1 - c)) for j in range(3)]


def _gather_start(bufs, name, after):
    return _copies_start([_halves(b) for b in bufs], name, after, _ici_plan)


def _copies_start(views, name, after, plan):
    n = len(views)

    def body(*refs):
        first_sem = n + 1
        ssem, rsem = refs[first_sem:first_sem + ns], refs[first_sem + ns:first_sem + 2 * ns]
        outs, token = refs[first_sem + 2 * ns:first_sem + 2 * ns + n], refs[first_sem + 2 * ns + n]
        for k in range(n):
            for j, (sent, _, peer) in enumerate(plan(*_position())):
                mine = outs[k].at[sent]
                _remote(mine, mine, ssem[3 * k + j], rsem[3 * k + j], peer).start()
        token[...] = jnp.zeros_like(token)

    ns = 3 * n
    outs = _pallas_call(
        body, name=name, in_specs=[ANY] * (n + 1), out_specs=[SEM] * (2 * ns) + [ANY] * n + [TOKEN_SPEC],
        out_shape=[pltpu.SemaphoreType.DMA(())] * (2 * ns) + [jax.ShapeDtypeStruct(v.shape, v.dtype) for v in views]
        + [TOKEN],
        input_output_aliases={k: k + 2 * ns for k in range(n)}, compiler_params=_split_params(),
    )(*views, after)
    return list(outs[:ns]), list(outs[ns:2 * ns]), list(outs[2 * ns:2 * ns + n]), outs[2 * ns + n]


def _copies_wait(ssem, rsem, views, after, name, plan):
    n = len(views)
    ns = len(ssem)

    def wait_body(*refs):
        ssem_ref, rsem_ref = refs[n:n + ns], refs[n + ns:n + 2 * ns]
        outs = refs[n + 2 * ns + 1:]
        for k in range(n):
            for j, (sent, landing, peer) in enumerate(plan(*_position())):
                cp = _remote(outs[k].at[sent], outs[k].at[landing], ssem_ref[3 * k + j], rsem_ref[3 * k + j], peer)
                cp.wait_send()
                cp.wait_recv()

    return _pallas_call(
        wait_body, name=name, in_specs=[ANY] * n + [SEM] * (2 * ns) + [ANY], out_specs=[ANY] * n,
        out_shape=[jax.ShapeDtypeStruct(v.shape, v.dtype) for v in views],
        input_output_aliases={k: k for k in range(n)}, compiler_params=_split_params(),
    )(*views, *ssem, *rsem, after)


def _gather_finish(ssem, rsem, views, after, name_wait, name_forward, shapes):
    return _gather_forward(_copies_wait(ssem, rsem, views, after, name_wait, _ici_plan), name_forward, shapes)


def _gather_forward(landed, name_forward, shapes):
    n = len(landed)
    views = landed

    def forward_body(*refs):
        outs = refs[n:2 * n]
        fssem, frsem = refs[2 * n:]
        x, y, c, q, chips, qs = _position()
        sib = (x, y, 1 - c)
        sent = []
        for k in range(n):
            for j in range(3):
                slot = outs[k].at[qs[j], c]
                cp = _remote(slot, slot, fssem.at[k, j], frsem.at[k, j], sib)
                cp.start()
                sent.append(cp)
        for k in range(n):
            for j in range(3):
                slot = outs[k].at[qs[j], 1 - c]
                _remote(slot, slot, fssem.at[k, j], frsem.at[k, j], sib).wait_recv()
        for cp in sent:
            cp.wait_send()

    outs = _pallas_call(
        forward_body, name=name_forward, in_specs=[ANY] * n, out_specs=[ANY] * n,
        out_shape=[jax.ShapeDtypeStruct(v.shape, v.dtype) for v in views],
        input_output_aliases={k: k for k in range(n)}, scratch_shapes=[pltpu.SemaphoreType.DMA((n, 3))] * 2,
    )(*landed)
    return [o.reshape(s) for o, s in zip(outs, shapes)]


def _chip_exchange_start(parts, name):
    n = len(parts)

    def body(*refs):
        ssem, rsem = refs[n:n + ns], refs[n + ns:n + 2 * ns]
        base = n + 2 * ns
        srcs, outs, token = refs[base:base + n], refs[base + n:base + 2 * n], refs[base + 2 * n]
        x, y, c, q, chips, qs = _position()
        for k in range(n):
            for j, chip in enumerate(chips):
                _remote(srcs[k].at[qs[j]], outs[k].at[j], ssem[3 * k + j], rsem[3 * k + j],
                        (chip[0], chip[1], c)).start()
        token[...] = jnp.zeros_like(token)

    ns = 3 * n
    outs = _pallas_call(
        body, name=name, in_specs=[ANY] * n, out_specs=[SEM] * (2 * ns) + [ANY] * (2 * n) + [TOKEN_SPEC],
        out_shape=[pltpu.SemaphoreType.DMA(())] * (2 * ns) + [jax.ShapeDtypeStruct(a.shape, a.dtype) for a in parts]
        + [jax.ShapeDtypeStruct((3,) + a.shape[1:], a.dtype) for a in parts] + [TOKEN],
        input_output_aliases={k: k + 2 * ns for k in range(n)}, compiler_params=_split_params(),
    )(*parts)
    b = 2 * ns
    return list(outs[:ns]), list(outs[ns:b]), list(outs[b:b + n]), list(outs[b + n:b + 2 * n]), outs[b + 2 * n]


def _chip_exchange_wait(ssem, rsem, parts, landing, after, name):
    n = len(parts)
    ns = len(ssem)

    def body(*refs):
        ssem_ref, rsem_ref = refs[2 * n:2 * n + ns], refs[2 * n + ns:2 * n + 2 * ns]
        base = 2 * n + 2 * ns + 1
        srcs, outs = refs[base:base + n], refs[base + n:]
        x, y, c, q, chips, qs = _position()
        for k in range(n):
            for j, chip in enumerate(chips):
                cp = _remote(srcs[k].at[qs[j]], outs[k].at[j], ssem_ref[3 * k + j], rsem_ref[3 * k + j],
                             (chip[0], chip[1], c))
                cp.wait_send()
                cp.wait_recv()

    outs = _pallas_call(
        body, name=name, in_specs=[ANY] * (2 * n) + [SEM] * (2 * ns) + [ANY], out_specs=[ANY] * (2 * n),
        out_shape=[jax.ShapeDtypeStruct(a.shape, a.dtype) for a in list(parts) + list(landing)],
        input_output_aliases={k: k for k in range(2 * n)}, compiler_params=_split_params(),
    )(*parts, *landing, *ssem, *rsem, after)
    return list(outs[:n]), list(outs[n:])


def _pair_swap(views, name):
    n = len(views)

    def body(*refs):
        ins, outs = refs[:n], refs[n:2 * n]
        ssem, rsem = refs[2 * n:]
        x, y, c, _, _, _ = _position()
        cps = [_remote(ins[k].at[pl.ds(0, N_CHIPS), 1 - c], outs[k], ssem.at[k], rsem.at[k], (x, y, 1 - c))
               for k in range(n)]
        for cp in cps:
            cp.start()
        for cp in cps:
            cp.wait()

    return _pallas_call(
        body, name=name, in_specs=[ANY] * n, out_specs=[ANY] * n,
        out_shape=[jax.ShapeDtypeStruct((v.shape[0],) + v.shape[2:], v.dtype) for v in views],
        scratch_shapes=[pltpu.SemaphoreType.DMA((n,))] * 2,
    )(*views)


def _pair_swap_start(views, name):
    n = len(views)

    def body(*refs):
        ins, ssem, rsem, outs, token = refs[:n], refs[n:2 * n], refs[2 * n:3 * n], refs[3 * n:4 * n], refs[4 * n]
        x, y, c, _, _, _ = _position()
        for k in range(n):
            _remote(ins[k].at[pl.ds(0, N_CHIPS), 1 - c], outs[k], ssem[k], rsem[k], (x, y, 1 - c)).start()
        token[...] = jnp.zeros_like(token)

    outs = _pallas_call(
        body, name=name, in_specs=[ANY] * n, out_specs=[SEM] * (2 * n) + [ANY] * n + [TOKEN_SPEC],
        out_shape=[pltpu.SemaphoreType.DMA(())] * (2 * n)
        + [jax.ShapeDtypeStruct((v.shape[0],) + v.shape[2:], v.dtype) for v in views] + [TOKEN],
        compiler_params=_split_params(),
    )(*views)
    return list(outs[:n]), list(outs[n:2 * n]), list(outs[2 * n:3 * n]), outs[3 * n]


def _pair_swap_wait(views, ssem, rsem, landing, after, name):
    n = len(views)

    def body(*refs):
        ins, ssem_ref, rsem_ref = refs[:n], refs[n:2 * n], refs[2 * n:3 * n]
        outs = refs[4 * n + 1:]
        x, y, c, _, _, _ = _position()
        for k in range(n):
            cp = _remote(ins[k].at[pl.ds(0, N_CHIPS), 1 - c], outs[k], ssem_ref[k], rsem_ref[k], (x, y, 1 - c))
            cp.wait_send()
            cp.wait_recv()

    return _pallas_call(
        body, name=name, in_specs=[ANY] * n + [SEM] * (2 * n) + [ANY] * (n + 1), out_specs=[ANY] * n,
        out_shape=[jax.ShapeDtypeStruct(v.shape, v.dtype) for v in landing],
        input_output_aliases={3 * n + k: k for k in range(n)}, compiler_params=_split_params(),
    )(*views, *ssem, *rsem, *landing, after)


def _pair_send(arrays, name):
    n = len(arrays)

    def body(*refs):
        ins, outs = refs[:n], refs[n:2 * n]
        ssem, rsem = refs[2 * n:]
        x, y, c, _, _, _ = _position()
        cps = [_remote(ins[k], outs[k], ssem.at[k], rsem.at[k], (x, y, 1 - c)) for k in range(n)]
        for cp in cps:
            cp.start()
        for cp in cps:
            cp.wait()

    return _pallas_call(
        body, name=name, in_specs=[ANY] * n, out_specs=[ANY] * n,
        out_shape=[jax.ShapeDtypeStruct(a.shape, a.dtype) for a in arrays],
        scratch_shapes=[pltpu.SemaphoreType.DMA((n,))] * 2,
    )(*arrays)


def _all_to_all_small(part):
    P = part.shape[0]

    def body(in_ref, out_ref, lsem, ssem, rsem):
        x, y, c = lax.axis_index("x"), lax.axis_index("y"), lax.axis_index("c")
        me = 4 * x + 2 * y + c
        flips = [(fx, fy, fc) for fx in (0, 1) for fy in (0, 1) for fc in (0, 1)][1:]
        peers = [((x + fx) % 2, (y + fy) % 2, (c + fc) % 2) for fx, fy, fc in flips]
        loc = pltpu.make_async_copy(in_ref, out_ref.at[me], lsem)
        loc.start()
        cps = [_remote(in_ref, out_ref.at[me], ssem.at[j], rsem.at[j], peer) for j, peer in enumerate(peers)]
        for cp in cps:
            cp.start()
        for j, (px, py, pc) in enumerate(peers):
            _remote(in_ref, out_ref.at[4 * px + 2 * py + pc], ssem.at[j], rsem.at[j], peers[j]).wait_recv()
        for cp in cps:
            cp.wait_send()
        loc.wait()

    return _pallas_call(
        body, name="small_exchange", in_specs=[ANY], out_specs=ANY,
        out_shape=jax.ShapeDtypeStruct((8, P, LANES), F32),
        scratch_shapes=[pltpu.SemaphoreType.DMA(())] + [pltpu.SemaphoreType.DMA((7,))] * 2,
    )(part)


def _small_peers():
    x, y, c = lax.axis_index("x"), lax.axis_index("y"), lax.axis_index("c")
    flips = [(fx, fy, fc) for fx in (0, 1) for fy in (0, 1) for fc in (0, 1)][1:]
    peers = [((x + fx) % 2, (y + fy) % 2, (c + fc) % 2) for fx, fy, fc in flips]
    return 4 * x + 2 * y + c, peers


def _all_to_all_small_start(part, name):
    P = part.shape[0]
    me = 4 * lax.axis_index("x") + 2 * lax.axis_index("y") + lax.axis_index("c")
    landing = lax.dynamic_update_slice(jnp.zeros((8, P, LANES), F32), part[None], (me, 0, 0))

    def body(*refs):
        sems, src, land, token = refs[2:16], refs[16], refs[17], refs[18]
        me_, peers = _small_peers()
        for j, peer in enumerate(peers):
            _remote(src, land.at[me_], sems[j], sems[7 + j], peer).start()
        token[...] = jnp.zeros_like(token)

    outs = _pallas_call(
        body, name=name, in_specs=[ANY, ANY], out_specs=[SEM] * 14 + [ANY, ANY, TOKEN_SPEC],
        out_shape=[pltpu.SemaphoreType.DMA(())] * 14 + [jax.ShapeDtypeStruct(part.shape, F32),
                                                       jax.ShapeDtypeStruct((8, P, LANES), F32), TOKEN],
        input_output_aliases={0: 14, 1: 15}, compiler_params=_split_params(),
    )(part, landing)
    return list(outs[:7]), list(outs[7:14]), outs[14], outs[15], outs[16]


def _all_to_all_small_wait(ssem, rsem, part, landing, after, name):
    def body(*refs):
        sems, src, land = refs[2:16], refs[17], refs[18]
        _, peers = _small_peers()
        for j, (px, py, pc) in enumerate(peers):
            cp = _remote(src, land.at[4 * px + 2 * py + pc], sems[j], sems[7 + j], peers[j])
            cp.wait_send()
            cp.wait_recv()

    return _pallas_call(
        body, name=name, in_specs=[ANY, ANY] + [SEM] * 14 + [ANY], out_specs=[ANY, ANY],
        out_shape=[jax.ShapeDtypeStruct(part.shape, F32), jax.ShapeDtypeStruct(landing.shape, F32)],
        input_output_aliases={0: 0, 1: 1}, compiler_params=_split_params(),
    )(part, landing, *ssem, *rsem, after)[1]


def _row_tile(rows, width, n_arrays):
    t = rows
    while t % 2 == 0 and t > 8 and 2 * n_arrays * t * width * 4 > VMEM_LIMIT // 2:
        t //= 2
    return t


def _chip():
    return 2 * lax.axis_index("x") + lax.axis_index("y")


def _core():
    return lax.axis_index("c")


def _cast_place(w3, layer, name):
    _, r, c = w3.shape
    tr = _row_tile(r, c, 2)

    def body(w_ref, o_ref):
        o_ref[...] = w_ref[...].astype(BF)

    return _pallas_call(
        body, name=name, grid=(r // tr,), in_specs=[pl.BlockSpec((None, tr, c), lambda i: (layer, i, 0))],
        out_specs=pl.BlockSpec((None, tr, c), lambda i: (_chip(), i, 0)),
        out_shape=jax.ShapeDtypeStruct((N_CHIPS, r, c), BF), compiler_params=_params(("parallel",)),
    )(w3)


def _pair_sum(views, recvs, name):
    n = len(views)

    def body(*refs):
        for g_ref, r_ref, o_ref in zip(refs[:n], refs[n:2 * n], refs[2 * n:]):
            o_ref[...] = (g_ref[...].astype(F32) + r_ref[...].astype(F32)).astype(BF)

    own = [pl.BlockSpec((None, None) + v.shape[2:], lambda p: (p, _core(), 0, 0)) for v in views]
    blk = [pl.BlockSpec((None,) + r.shape[1:], lambda p: (p, 0, 0)) for r in recvs]
    return _pallas_call(
        body, name=name, grid=(N_CHIPS,), in_specs=own + blk, out_specs=blk,
        out_shape=[jax.ShapeDtypeStruct(r.shape, BF) for r in recvs], compiler_params=_params(("parallel",)),
    )(*views, *recvs)


CHIP_SUM_STEPS = 2


def _chip_sum(parts, recvs, name):
    n = len(parts)

    def body(*refs):
        for p_ref, r_ref, o_ref in zip(refs[:n], refs[n:2 * n], refs[2 * n:]):
            acc = p_ref[...].astype(F32)
            for j in range(3):
                acc = acc + r_ref[j].astype(F32)
            o_ref[...] = acc

    rows = [p.shape[1] // CHIP_SUM_STEPS for p in parts]
    return _pallas_call(
        body, name=name, grid=(CHIP_SUM_STEPS,),
        in_specs=[pl.BlockSpec((None, t, p.shape[2]), lambda i: (_chip(), i, 0)) for p, t in zip(parts, rows)]
        + [pl.BlockSpec((3, t, p.shape[2]), lambda i: (0, i, 0)) for p, t in zip(parts, rows)],
        out_specs=[pl.BlockSpec((t, p.shape[2]), lambda i: (i, 0)) for p, t in zip(parts, rows)],
        out_shape=[jax.ShapeDtypeStruct(p.shape[1:], F32) for p in parts], compiler_params=_params(("parallel",)),
    )(*parts, *recvs)


def _sum_slices(a, name):
    n, rows, width = a.shape
    tr = _row_tile(rows, width, n + 1)

    def body(a_ref, o_ref):
        acc = a_ref[0].astype(F32)
        for i in range(1, n):
            acc = acc + a_ref[i].astype(F32)
        o_ref[...] = acc

    return _pallas_call(
        body, name=name, grid=(rows // tr,), in_specs=[pl.BlockSpec((n, tr, width), lambda i: (0, i, 0))],
        out_specs=pl.BlockSpec((tr, width), lambda i: (i, 0)), out_shape=jax.ShapeDtypeStruct((rows, width), F32),
        compiler_params=_params(("parallel",)),
    )(a)


def _adamw_update(w, g, m, v):
    nm = ADAM_B1 * m + (1.0 - ADAM_B1) * g
    nv = ADAM_B2 * v + (1.0 - ADAM_B2) * (g * g)
    m_hat = nm / (1.0 - ADAM_B1 ** ADAM_STEP)
    v_hat = nv / (1.0 - ADAM_B2 ** ADAM_STEP)
    return -ADAM_LR * (m_hat / (jnp.sqrt(v_hat) + ADAM_EPS) + ADAM_WD * w), nm, nv


def _adamw(ws, gs, ms, vs, name):
    n = len(ws)

    def body(*refs):
        for k in range(n):
            w_ref, g_ref, m_ref, v_ref = (refs[s * n + k] for s in range(4))
            d_ref, nm_ref, nv_ref = (refs[(4 + s) * n + k] for s in range(3))
            d_ref[...], nm_ref[...], nv_ref[...] = _adamw_update(w_ref[...], g_ref[...], m_ref[...], v_ref[...])

    whole = [pl.BlockSpec(w.shape, lambda i: (0, 0)) for w in ws]
    outs = _pallas_call(
        body, name=name, grid=(1,), in_specs=whole * 4, out_specs=whole * 3,
        out_shape=[jax.ShapeDtypeStruct(w.shape, F32) for _ in range(3) for w in ws],
        compiler_params=_params(("arbitrary",)),
    )(*ws, *gs, *ms, *vs)
    return [[outs[s * n + k] for s in range(3)] for k in range(n)]


ADAMW_STEPS = 4


def _adamw_halves(ws, ms, vs, mine, other, name):
    n = len(ws)
    depth = ws[0].shape[0]
    assert depth == 2
    halves = [(w.shape[1] // 2, w.shape[2]) for w in ws]
    tiles = [hr // ADAMW_STEPS for hr, _ in halves]
    kinds = ((0, True), (0, False), (1, True), (1, False))

    def active(l, h, layer, own):
        mine_half = h == _core()
        return (l == layer) & (mine_half if own else jnp.logical_not(mine_half))

    def body(*refs):
        l, h = pl.program_id(0), pl.program_id(1)
        flags = [active(l, h, layer, own) for layer, own in kinds]
        for k in range(n):
            w_ref, m_ref, v_ref = refs[k], refs[n + k], refs[2 * n + k]
            g_refs = [refs[(3 + s) * n + k] for s in range(4)]
            go_ref, d_ref, nm_ref, nv_ref = (refs[(7 + s) * n + k] for s in range(4))
            for flag, g_ref in zip(flags, g_refs):
                @pl.when(flag)
                def _():
                    gv = g_ref[...]
                    go_ref[...] = gv
                    d_ref[...], nm_ref[...], nv_ref[...] = _adamw_update(w_ref[...], gv, m_ref[...], v_ref[...])

    def blk(k):
        return pl.BlockSpec((None, None, tiles[k], halves[k][1]), lambda l, h, i: (l, h, i, 0))

    def gspec(k, layer, own):
        return pl.BlockSpec((tiles[k], halves[k][1]), lambda l, h, i: (jnp.where(active(l, h, layer, own), i, 0), 0))

    def view(a, k):
        return a.reshape(depth, 2, halves[k][0], halves[k][1])

    blks = [blk(k) for k in range(n)]
    sources = [[(mine if own else other)[layer][k] for k in range(n)] for layer, own in kinds]
    outs = _pallas_call(
        body, name=name, grid=(depth, 2, ADAMW_STEPS),
        in_specs=blks * 3 + [gspec(k, layer, own) for layer, own in kinds for k in range(n)], out_specs=blks * 4,
        out_shape=[jax.ShapeDtypeStruct((depth, 2) + halves[k], F32) for _ in range(4) for k in range(n)],
        compiler_params=_params(("parallel", "parallel", "parallel")),
    )(*[view(a, k) for group in (ws, ms, vs) for k, a in enumerate(group)], *[g for src in sources for g in src])
    return [[outs[s * n + k].reshape(ws[k].shape) for s in range(4)] for k in range(n)]


BIG = ("w_in", "w_pool_up", "w_conv_out", "w_attn_up", "w_o", "w_ff1", "w_ff2")
SMALL = ("norm_mix", "b_gate", "pool_mix", "pool_scale", "conv_w", "q_gain", "k_gain", "norm_mlp")
ORDER = ("norm_mix", "w_in", "b_gate", "pool_mix", "pool_scale", "conv_w", "q_gain", "k_gain", "w_pool_up",
         "w_conv_out", "w_attn_up", "w_o", "norm_mlp", "w_ff1", "w_ff2")
COLUMN_SHARDED = ("w_pool_up", "w_conv_out", "w_attn_up", "w_ff1")


def _matrix_weights(gathered):
    w = {}
    for name, g4 in gathered.items():
        if name in COLUMN_SHARDED:
            w[name] = g4
        else:
            w[name] = g4.reshape(N_CHIPS * g4.shape[1], g4.shape[2])
    return w


def _small_weights(l, small):
    w = {}
    w["norm_mix"] = small["norm_mix"][l][None]
    w["norm_mlp"] = small["norm_mlp"][l][None]
    w["b_gate"] = small["b_gate"][l][None]
    w["pool_mix"] = small["pool_mix"][l].astype(BF)
    w["pool_scale"] = small["pool_scale"][l][None]
    w["conv_w"] = jnp.pad(small["conv_w_full"][l], ((0, 5), (0, 0)))
    w["qk_gain"] = jnp.pad(jnp.stack([jnp.tile(small["q_gain"][l], 2), jnp.tile(small["k_gain"][l], 2)]), ((0, 6), (0, 0)))
    return w


def _to_chip_major(name, g):
    if name == "w_in":
        return g.T.reshape(N_CHIPS, g.shape[1] // N_CHIPS, g.shape[0])
    if name in COLUMN_SHARDED:
        return g
    return g.reshape(N_CHIPS, g.shape[0] // N_CHIPS, g.shape[1])


def _pad8(a):
    a = a.reshape(-1)
    return jnp.pad(a, (0, (-a.size) % (8 * LANES))).reshape(-1, LANES)


def kernel(x, norm_mix, w_in, b_gate, pool_mix, pool_scale, conv_w, q_gain, k_gain, w_pool_up, w_conv_out, w_attn_up, w_o, norm_mlp, w_ff1, w_ff2, loss_target, m_norm_mix, m_w_in, m_b_gate, m_pool_mix, m_pool_scale, m_conv_w, m_q_gain, m_k_gain, m_w_pool_up, m_w_conv_out, m_w_attn_up, m_w_o, m_norm_mlp, m_w_ff1, m_w_ff2, v_norm_mix, v_w_in, v_b_gate, v_pool_mix, v_pool_scale, v_conv_w, v_q_gain, v_k_gain, v_w_pool_up, v_w_conv_out, v_w_attn_up, v_w_o, v_norm_mlp, v_w_ff1, v_w_ff2):
    weights = dict(norm_mix=norm_mix, w_in=w_in, b_gate=b_gate, pool_mix=pool_mix, pool_scale=pool_scale, conv_w=conv_w,
                   q_gain=q_gain, k_gain=k_gain, w_pool_up=w_pool_up, w_conv_out=w_conv_out, w_attn_up=w_attn_up,
                   w_o=w_o, norm_mlp=norm_mlp, w_ff1=w_ff1, w_ff2=w_ff2)
    moms = dict(norm_mix=m_norm_mix, w_in=m_w_in, b_gate=m_b_gate, pool_mix=m_pool_mix, pool_scale=m_pool_scale,
                conv_w=m_conv_w, q_gain=m_q_gain, k_gain=m_k_gain, w_pool_up=m_w_pool_up, w_conv_out=m_w_conv_out,
                w_attn_up=m_w_attn_up, w_o=m_w_o, norm_mlp=m_norm_mlp, w_ff1=m_w_ff1, w_ff2=m_w_ff2)
    vels = dict(norm_mix=v_norm_mix, w_in=v_w_in, b_gate=v_b_gate, pool_mix=v_pool_mix, pool_scale=v_pool_scale,
                conv_w=v_conv_w, q_gain=v_q_gain, k_gain=v_k_gain, w_pool_up=v_w_pool_up, w_conv_out=v_w_conv_out,
                w_attn_up=v_w_attn_up, w_o=v_w_o, norm_mlp=v_norm_mlp, w_ff1=v_w_ff1, w_ff2=v_w_ff2)
    depth = norm_mix.shape[0]
    q = 2 * lax.axis_index("x") + lax.axis_index("y")
    for group in (weights, moms, vels):
        group["w_in"] = jnp.swapaxes(group["w_in"], 1, 2)

    assert depth == 2, "the second layer's gather hides behind the first layer's forward, and likewise backward"
    first, rest = BIG[:1], BIG[1:]
    cw_all = _all_to_all_small(_pad8(conv_w))
    bufs = [{n: _cast_place(weights[n], 0, f"cast_{n}_l0") for n in first}]
    a_ssem, a_rsem, a_views, a_token = _gather_start([bufs[0][n] for n in first], "gather_start_l0_in", cw_all)
    bufs[0].update({n: _cast_place(weights[n], 0, f"cast_{n}_l0") for n in rest})
    bufs += [{n: _cast_place(weights[n], l, f"cast_{n}_l{l}") for n in BIG} for l in range(1, depth)]
    b_ssem, b_rsem, b_views, b_token = _gather_start([bufs[0][n] for n in rest], "gather_start_l0_rest", a_token)
    g_ssem, g_rsem, g_views, g_token = _gather_start([bufs[1][n] for n in BIG], "gather_start_l1", b_token)
    conv_w_full = jnp.concatenate(
        [cw_all[2 * p].reshape(-1)[:conv_w.size].reshape(conv_w.shape) for p in range(N_CHIPS)], axis=-1)
    small = dict(weights)
    small["conv_w_full"] = conv_w_full

    def soon_weights(t):
        got = _gather_finish(a_ssem, a_rsem, a_views, t, "gather_wait_l0_in", "gather_forward_l0_in",
                             [bufs[0][n].shape for n in first])
        return _matrix_weights(dict(zip(first, got)))

    mlp = len(rest) - 2
    shapes_0 = [bufs[0][n].shape for n in rest]
    sibling = {}

    def late_weights(t):
        landed = _copies_wait(b_ssem, b_rsem, b_views, t, "gather_wait_l0_rest", _ici_plan)
        got = _gather_forward(landed[:mlp], "gather_forward_l0_rest", shapes_0[:mlp])
        sibling["mlp"] = _copies_start(landed[mlp:], "gather_forward_start_l0_mlp", got[0], _sibling_plan)
        return dict(_matrix_weights(dict(zip(rest[:mlp], got))), crossing=sibling["mlp"][3])

    def last_weights(t):
        f_ssem, f_rsem, f_views, _ = sibling["mlp"]
        got = _copies_wait(f_ssem, f_rsem, f_views, t, "gather_forward_wait_l0_mlp", _sibling_plan)
        return _matrix_weights({n: o.reshape(s) for n, o, s in zip(rest[mlp:], got, shapes_0[mlp:])})

    wl, saved = [None] * depth, [None] * depth
    small_1 = _small_weights(1, small)
    (h, hb_1), saved[0], wl[0] = _layer_fwd(x[0], _small_weights(0, small), "l0", after=g_token, soon=soon_weights,
                                            late=late_weights, last=last_weights, next_gain=small_1["norm_mix"])
    shapes_1 = [bufs[1][n].shape for n in BIG]

    def soon_weights_1(t):
        landed = _copies_wait(g_ssem, g_rsem, g_views, t, "gather_wait_l1", _ici_plan)
        got = _gather_forward(landed[:1], "gather_forward_l1_in", shapes_1[:1])
        sibling["rest"] = _copies_start(landed[1:], "gather_forward_start_l1_rest", got[0], _sibling_plan)
        return dict(_matrix_weights(dict(zip(first, got))), started=sibling["rest"][3])

    def late_weights_1(t):
        f_ssem, f_rsem, f_views, _ = sibling["rest"]
        got = _copies_wait(f_ssem, f_rsem, f_views, t, "gather_forward_wait_l1_rest", _sibling_plan)
        return _matrix_weights({n: o.reshape(s) for n, o, s in zip(rest, got, shapes_1[1:])})

    (dh, loss_row), saved[1], wl[1] = _layer_fwd(h, small_1, "l1", soon=soon_weights_1, late=late_weights_1,
                                                 target=loss_target[0], hb=hb_1)

    def pair_stage(names, g, tag):
        views = [_halves(_to_chip_major(n, g[n])) for n in names]
        from_sibling = _pair_swap(views, f"grad_pair_swap_{tag}")
        return _pair_sum(views, from_sibling, f"pair_sum_{tag}")

    mine, other = [{}, {}], [{}, {}]

    def finish(names, l, started, after, tag):
        ssem, rsem, parts, landing, _ = started
        parts, arrived = _chip_exchange_wait(ssem, rsem, parts, landing, after, f"grad_chip_exchange_wait_{tag}")
        got = _chip_sum(parts, arrived, f"chip_sum_{tag}")
        mine[l].update(zip(names, got))
        other[l].update(zip(names, _pair_send(got, f"grad_pair_send_{tag}")))

    def small_pieces(g):
        return [_pad8(g[n][:3] if n == "conv_w" else g[n]) for n in SMALL]

    def start_small(l):
        pieces = small_pieces(grads[l]) + ([_pad8(loss_row)] if l == depth - 1 else [])
        return _all_to_all_small_start(jnp.concatenate(pieces, axis=0), f"small_grad_exchange_start_l{l}")

    grads, early, small = [None] * depth, {}, [None] * depth
    dh, grads[1] = _layer_bwd(dh, wl[1], saved[1], "l1")
    views_1 = [_halves(_to_chip_major(n, grads[1][n])) for n in BIG]
    s_ssem, s_rsem, s_landing, s_token = _pair_swap_start(views_1, "grad_pair_swap_start_l1")
    small[1] = start_small(1)
    later = {}

    def start_second(t):
        from_sibling = _pair_swap_wait(views_1, s_ssem, s_rsem, s_landing, t, "grad_pair_swap_wait_l1")
        later["l1"] = _chip_exchange_start(_pair_sum(views_1, from_sibling, "pair_sum_l1"), "grad_chip_exchange_start_l1")
        return later["l1"][4]

    def start_rest(g):
        early["rest"] = _chip_exchange_start(pair_stage(rest, g, "l0_rest"), "grad_chip_exchange_start_l0_rest")
        return early["rest"][4]

    def start_last(g):
        early["in"] = _chip_exchange_start(pair_stage(first, g, "l0_in"), "grad_chip_exchange_start_l0_in")
        return early["in"][4]

    dh, grads[0] = _layer_bwd(dh, wl[0], saved[0], "l0", after=[s_token, small[1][4]], head=start_second,
                              mid=start_rest, tail=start_last)
    second = later["l1"]
    small[0] = start_small(0)
    started = small[0][4]
    finish(BIG, 1, second, started, "l1")
    finish(rest, 0, early["rest"], started, "l0_rest")
    full = {}

    deltas, new_m, new_v = {}, {}, {}

    def update_matrices(names, tag):
        results = _adamw_halves(
            [weights[n] for n in names], [moms[n] for n in names], [vels[n] for n in names],
            [[mine[l][n] for n in names] for l in range(depth)], [[other[l][n] for n in names] for l in range(depth)],
            f"adamw_{tag}")
        for n, (g_, d_, m_, v_) in zip(names, results):
            full[n], deltas[n], new_m[n], new_v[n] = g_, d_, m_, v_

    update_matrices(rest, "rest")
    finish(first, 0, early["in"], deltas[rest[-1]], "l0_in")
    update_matrices(first, "in")
    summed = []
    for l in range(depth):
        ssem, rsem, part, landing, _ = small[l]
        summed.append(_sum_slices(_all_to_all_small_wait(ssem, rsem, part, landing, deltas[first[-1]],
                                                         f"small_grad_exchange_wait_l{l}"), f"small_sum_l{l}"))
    row = 0
    for n, piece in zip(SMALL, small_pieces(grads[0])):
        size = (weights[n].size if n != "conv_w" else depth * 3 * 512) // depth
        flat = jnp.stack([s[row:row + piece.shape[0]].reshape(-1)[:size] for s in summed])
        row += piece.shape[0]
        if n == "conv_w":
            full[n] = lax.dynamic_slice_in_dim(flat.reshape(depth, 3, 512), q * conv_w.shape[2], conv_w.shape[2], axis=2)
        else:
            full[n] = flat.reshape(weights[n].shape)
    loss = summed[depth - 1][row, 0]
    two_d = {n: (-1, weights[n].shape[-1]) if n not in ("conv_w", "q_gain", "k_gain") else (1, -1) for n in SMALL}
    results = _adamw(*[[group[n].reshape(two_d[n]) for n in SMALL] for group in (weights, full, moms, vels)],
                     "adamw_small")
    for n, (d2, m2, v2) in zip(SMALL, results):
        shape = weights[n].shape
        deltas[n], new_m[n], new_v[n] = d2.reshape(shape), m2.reshape(shape), v2.reshape(shape)
        full[n] = full[n].reshape(shape)
    for group in (full, deltas, new_m, new_v):
        group["w_in"] = jnp.swapaxes(group["w_in"], 1, 2)
    return (loss, dh[None], *[full[n] for n in ORDER], *[deltas[n] for n in ORDER], *[new_m[n] for n in ORDER],
            *[new_v[n] for n in ORDER])
```

```python
import functools

import jax
import jax.numpy as jnp
from jax import lax
from jax.experimental import pallas as pl
from jax.experimental.pallas import tpu as pltpu

F32 = jnp.float32
BF = jnp.bfloat16
MESH_ID = pl.DeviceIdType.MESH
ANY = pl.BlockSpec(memory_space=pl.ANY)

EPS = 1e-6
MASK_VALUE = -1e30
POOL_WINDOWS = (2, 4, 8, 16)
ATTN_DILATIONS = (1, 4, 16)
ATTN_BLOCK = 128
HEAD_DIM = 64
OFF_Q, OFF_K, OFF_V, OFF_GATE = 2048, 2816, 3584, 4352
N_CHIPS = 4
ADAM_LR, ADAM_B1, ADAM_B2, ADAM_EPS, ADAM_WD, ADAM_STEP = 0.001, 0.9, 0.999, 1e-08, 0.01, 10

VMEM_LIMIT = 48 * 1024 * 1024
LANES = 128

_DIMS = {"nn": (((1,), (0,)), ((), ())), "nt": (((1,), (1,)), ((), ())), "tn": (((0,), (0,)), ((), ()))}


def _params(sem):
    return pltpu.CompilerParams(dimension_semantics=sem, vmem_limit_bytes=VMEM_LIMIT)


def _pallas_call(body, **kw):
    def in_hbm(s):
        pin = isinstance(s, jax.ShapeDtypeStruct) and s is not TOKEN and jnp.issubdtype(s.dtype, jnp.floating)
        return pltpu.HBM(s.shape, s.dtype) if pin else s

    out_shape = kw.pop("out_shape")
    kw["out_shape"] = [in_hbm(s) for s in out_shape] if isinstance(out_shape, (list, tuple)) else in_hbm(out_shape)
    call = pl.pallas_call(body, **kw)

    def run(*args):
        pinned = [pltpu.with_memory_space_constraint(a, pltpu.HBM)
                  if hasattr(a, "dtype") and jnp.issubdtype(a.dtype, jnp.floating) else a for a in args]
        return call(*pinned)

    return run


def _dot(a, b, mode="nn"):
    return lax.dot_general(a, b, _DIMS[mode], preferred_element_type=F32)


def _mm(a, b, mode, name, *, tm, tn, tk, out_dtype=F32, res=None, aux=None, epi=None, n_outer=False,
        b_shards=False, out_shards=False, after=None, vec=None):
    if mode == "tn":
        K, M = a.shape
    else:
        M, K = a.shape
    if b_shards:
        if mode == "nn":
            assert b.shape[1] == K
            N = b.shape[2] * N_CHIPS
        else:
            assert mode == "nt"
            N = b.shape[1]
            assert b.shape[2] * N_CHIPS == K
    else:
        N = b.shape[0] if mode == "nt" else b.shape[1]
    tm, tn, tk = min(tm, M), min(tn, N), min(tk, K)
    assert M % tm == 0 and N % tn == 0 and K % tk == 0
    nk = K // tk
    if n_outer:
        grid = (N // tn, M // tm, nk)
        ij = lambda p, q_: (q_, p)
    else:
        grid = (M // tm, N // tn, nk)
        ij = lambda p, q_: (p, q_)

    def amap(p, q_, k):
        i, j = ij(p, q_)
        return (k, i) if mode == "tn" else (i, k)

    a_spec = pl.BlockSpec((tk, tm) if mode == "tn" else (tm, tk), amap)
    if b_shards:
        if mode == "nn":
            per = (N // N_CHIPS) // tn
            assert per >= 1 and (N // N_CHIPS) % tn == 0

            def bmap(p, q_, k):
                i, j = ij(p, q_)
                return (j // per, k, j % per)

            b_spec = pl.BlockSpec((None, tk, tn), bmap)
        else:
            per = (K // N_CHIPS) // tk
            assert per >= 1 and (K // N_CHIPS) % tk == 0

            def bmap(p, q_, k):
                i, j = ij(p, q_)
                return (k // per, j, k % per)

            b_spec = pl.BlockSpec((None, tn, tk), bmap)
    else:
        def bmap(p, q_, k):
            i, j = ij(p, q_)
            return (j, k) if mode == "nt" else (k, j)

        b_spec = pl.BlockSpec((tn, tk) if mode == "nt" else (tk, tn), bmap)

    def omap(p, q_, k):
        return ij(p, q_)

    o_spec = pl.BlockSpec((tm, tn), omap)
    if out_shards:
        per_o = (N // N_CHIPS) // tn
        assert per_o >= 1 and (N // N_CHIPS) % tn == 0

        def osmap(p, q_, k):
            i, j = ij(p, q_)
            return (j // per_o, i, j % per_o)

        out_spec0 = pl.BlockSpec((None, tm, tn), osmap)
        out_shape0 = jax.ShapeDtypeStruct((N_CHIPS, M, N // N_CHIPS), out_dtype)
    else:
        out_spec0 = o_spec
        out_shape0 = jax.ShapeDtypeStruct((M, N), out_dtype)

    in_specs = [a_spec, b_spec]
    args = [a, b]
    if res is not None:
        in_specs.append(o_spec)
        args.append(res)
    if aux is not None:
        in_specs.append(o_spec)
        args.append(aux)
    if vec is not None:
        in_specs.append(pl.BlockSpec((1, tn), lambda p, q_, k: (0, ij(p, q_)[1])))
        args.append(vec)
    after = [] if after is None else list(after) if isinstance(after, (list, tuple)) else [after]
    in_specs += [ANY] * len(after)
    args += after
    out_specs = [out_spec0]
    out_shape = [out_shape0]
    reduces = epi in ("loss", "rms_bwd")
    if reduces:
        assert tn == N and not n_outer and not out_shards
        width = LANES if epi == "loss" else N
        out_specs.append(pl.BlockSpec((1, width), lambda p, q_, k: (0, 0)))
        out_shape.append(jax.ShapeDtypeStruct((1, width), F32))
    if epi == "rms_next":
        assert tn == N and not out_shards
        out_specs.append(o_spec)
        out_shape.append(jax.ShapeDtypeStruct((M, N), BF))
    n_out = len(out_shape)
    has_res, has_aux, has_vec, n_after = res is not None, aux is not None, vec is not None, len(after)

    def body(*refs):
        a_ref, b_ref = refs[0], refs[1]
        pos = 2
        res_ref = aux_ref = vec_ref = None
        if has_res:
            res_ref = refs[pos]
            pos += 1
        if has_aux:
            aux_ref = refs[pos]
            pos += 1
        if has_vec:
            vec_ref = refs[pos]
            pos += 1
        pos += n_after
        outs = refs[pos:pos + n_out]
        part = _dot(a_ref[...].astype(BF), b_ref[...].astype(BF), mode)

        first_row_tile = pl.program_id(0) == 0

        def add_to_sum(row):
            @pl.when(first_row_tile)
            def _():
                outs[1][...] = jnp.zeros_like(outs[1])

            outs[1][...] += row

        def finish(acc):
            if epi == "rms_bwd":
                xv = aux_ref[...]
                r = lax.rsqrt(jnp.mean(xv * xv, axis=-1, keepdims=True) + EPS)
                xhat = xv * r
                dy = acc * vec_ref[...]
                outs[0][...] = res_ref[...] + r * (dy - xhat * jnp.mean(dy * xhat, axis=-1, keepdims=True))
                add_to_sum(jnp.sum(acc * xhat, axis=0, keepdims=True))
                return
            if res_ref is not None:
                acc = res_ref[...] + acc
            if epi == "relu2":
                r = jnp.maximum(acc, 0.0)
                outs[0][...] = (r * r).astype(out_dtype)
            elif epi == "drelu2":
                outs[0][...] = (acc.astype(BF) * (2.0 * jnp.sqrt(aux_ref[...]))).astype(out_dtype)
            elif epi == "rms_next":
                outs[0][...] = acc
                r = lax.rsqrt(jnp.mean(acc * acc, axis=-1, keepdims=True) + EPS)
                outs[1][...] = ((acc * r) * vec_ref[...]).astype(BF)
            elif epi == "loss":
                e = acc - aux_ref[...]
                outs[0][...] = e / float(N)
                add_to_sum(0.5 * jnp.sum(jnp.mean(e * e, axis=-1, keepdims=True)))
            else:
                outs[0][...] = acc.astype(out_dtype)

        if nk == 1:
            finish(part)
        else:
            acc_ref = refs[pos + n_out]
            k = pl.program_id(2)

            @pl.when(k == 0)
            def _():
                acc_ref[...] = part

            @pl.when(k > 0)
            def _():
                acc_ref[...] += part

            @pl.when(k == nk - 1)
            def _():
                finish(acc_ref[...])

    scratch = [pltpu.VMEM((tm, tn), F32)] if nk > 1 else []
    out = _pallas_call(
        body, name=name, grid=grid, in_specs=in_specs, out_specs=out_specs, out_shape=out_shape,
        scratch_shapes=scratch,
        compiler_params=_params(("arbitrary" if reduces else "parallel", "parallel", "arbitrary")),
    )(*args)
    return out if n_out > 1 else out[0]


def _rms_fwd(x, gain, name, after=None):
    T, D = x.shape
    tm = min(512, T)

    def body(x_ref, g_ref, *rest):
        o_ref = rest[-1]
        xv = x_ref[...]
        r = lax.rsqrt(jnp.mean(xv * xv, axis=-1, keepdims=True) + EPS)
        o_ref[...] = ((xv * r) * g_ref[...]).astype(BF)

    extra = [] if after is None else list(after) if isinstance(after, (list, tuple)) else [after]
    return _pallas_call(
        body, name=name, grid=(T // tm,),
        in_specs=[pl.BlockSpec((tm, D), lambda i: (i, 0)), pl.BlockSpec((1, D), lambda i: (0, 0))] + [ANY] * len(extra),
        out_specs=pl.BlockSpec((tm, D), lambda i: (i, 0)), out_shape=jax.ShapeDtypeStruct((T, D), BF),
        compiler_params=_params(("parallel",)),
    )(x, gain, *extra)


def _rms_bwd(dh, x, gain, dres, name):
    T, D = x.shape
    tm = min(512, T)

    def body(dh_ref, x_ref, g_ref, dres_ref, dx_ref, dg_ref):
        xv = x_ref[...]
        r = lax.rsqrt(jnp.mean(xv * xv, axis=-1, keepdims=True) + EPS)
        xhat = xv * r
        dhv = dh_ref[...]
        dy = dhv * g_ref[...]
        dx_ref[...] = dres_ref[...] + r * (dy - xhat * jnp.mean(dy * xhat, axis=-1, keepdims=True))

        @pl.when(pl.program_id(0) == 0)
        def _():
            dg_ref[...] = jnp.zeros_like(dg_ref)

        dg_ref[...] += jnp.sum(dhv * xhat, axis=0, keepdims=True)

    row = pl.BlockSpec((tm, D), lambda i: (i, 0))
    vec = pl.BlockSpec((1, D), lambda i: (0, 0))
    return _pallas_call(
        body, name=name, grid=(T // tm,), in_specs=[row, row, vec, row], out_specs=[row, vec],
        out_shape=[jax.ShapeDtypeStruct((T, D), F32), jax.ShapeDtypeStruct((1, D), F32)],
        compiler_params=_params(("arbitrary",)),
    )(dh, x, gain, dres)


POOL_HALO = 16
CONV_HALO = 8
POOLCONV_ROWS = 512


def _causal_window_sum(v, w):
    s, sh = v, 1
    while sh < w:
        s = s + pltpu.roll(s, sh, 0)
        sh *= 2
    return s


def _anticausal_window_sum(v, w):
    n = v.shape[0]
    s, sh = v, 1
    while sh < w:
        s = s + pltpu.roll(s, n - sh, 0)
        sh *= 2
    return s


def _poolconv_fwd(z, pmix_b, pscale, convw, name):
    T = z.shape[0]
    R = min(POOLCONV_ROWS, T)
    PH, CH = R // POOL_HALO, R // CONV_HALO

    def body(u_ref, uh_ref, b_ref, c_ref, ch_ref, x_ref, xh_ref, mix_ref, sc_ref, cw_ref, yp_ref, yc_ref):
        i = pl.program_id(0)
        keep = (i > 0).astype(F32)
        row = i * R + lax.broadcasted_iota(jnp.int32, (R, 1), 0)
        w_all = jnp.concatenate([uh_ref[...] * keep, u_ref[...]], axis=0)
        for g, w in enumerate(POOL_WINDOWS):
            cols = slice(128 * g, 128 * (g + 1))
            wg = w_all[:, cols]
            s = _causal_window_sum(wg, w)[POOL_HALO:]
            inv_cnt = 1.0 / jnp.minimum(row + 1, w).astype(F32)
            dgrp = s * inv_cnt - wg[POOL_HALO:]
            y = _dot(dgrp.astype(BF), mix_ref[g]) * sc_ref[:, cols]
            yp_ref[:, cols] = y.astype(BF)
        uc = jnp.concatenate([ch_ref[...] * xh_ref[...] * keep, c_ref[...] * x_ref[...]], axis=0)
        yc = cw_ref[2:3, :] * uc + cw_ref[0:1, :] * pltpu.roll(uc, 2, 0) + cw_ref[1:2, :] * pltpu.roll(uc, 1, 0)
        yc_ref[...] = (b_ref[...] * yc[CONV_HALO:]).astype(BF)

    def main(cb):
        return pl.BlockSpec((R, 512), lambda i: (i, cb))

    def prev(cb, halo, per):
        return pl.BlockSpec((halo, 512), lambda i: (jnp.maximum(i * per - 1, 0), cb))

    full = lambda a: pl.BlockSpec(a.shape, lambda i: (0,) * a.ndim)
    return _pallas_call(
        body, name=name, grid=(T // R,),
        in_specs=[main(0), prev(0, POOL_HALO, PH), main(1), main(2), prev(2, CONV_HALO, CH), main(3),
                  prev(3, CONV_HALO, CH), full(pmix_b), full(pscale), full(convw)],
        out_specs=[pl.BlockSpec((R, 512), lambda i: (i, 0))] * 2,
        out_shape=[jax.ShapeDtypeStruct((T, 512), BF)] * 2,
        compiler_params=_params(("parallel",)),
    )(z, z, z, z, z, z, z, pmix_b, pscale, convw)


def _poolconv_bwd(z, dyp, dyc, pmix_b, pscale, convw, dz, name):
    T = z.shape[0]
    R = min(POOLCONV_ROWS, T)
    PH, CH = R // POOL_HALO, R // CONV_HALO
    nsteps = T // R

    def body(u_ref, uh_ref, b_ref, bn_ref, c_ref, ch_ref, x_ref, xh_ref, dyp_ref, dypn_ref, dyc_ref, dycn_ref,
             mix_ref, sc_ref, cw_ref, dz_in_ref, dz_ref, dmix_ref, dsc_ref, dcw_ref):
        i = pl.program_id(0)
        keep_prev = (i > 0).astype(F32)
        keep_next = (i < nsteps - 1).astype(F32)

        @pl.when(i == 0)
        def _():
            dmix_ref[...] = jnp.zeros_like(dmix_ref)
            dsc_ref[...] = jnp.zeros_like(dsc_ref)
            dcw_ref[...] = jnp.zeros_like(dcw_ref)

        row = i * R + lax.broadcasted_iota(jnp.int32, (R, 1), 0)
        row_ext = i * R + lax.broadcasted_iota(jnp.int32, (R + POOL_HALO, 1), 0)
        w_all = jnp.concatenate([uh_ref[...] * keep_prev, u_ref[...]], axis=0)
        dyp_ext = jnp.concatenate([dyp_ref[...], dypn_ref[...] * keep_next], axis=0)
        for g, w in enumerate(POOL_WINDOWS):
            cols = slice(128 * g, 128 * (g + 1))
            wg = w_all[:, cols]
            s = _causal_window_sum(wg, w)[POOL_HALO:]
            inv_cnt = 1.0 / jnp.minimum(row + 1, w).astype(F32)
            dgrp = (s * inv_cnt - wg[POOL_HALO:]).astype(BF)
            y_pre = _dot(dgrp, mix_ref[g])
            dsc_ref[:, cols] += jnp.sum(dyp_ref[:, cols] * y_pre, axis=0, keepdims=True)
            dyb = (dyp_ext[:, cols] * sc_ref[:, cols]).astype(BF)
            dmix_ref[cols, :] += _dot(dgrp, dyb[:R], "tn")
            dd = _dot(dyb, mix_ref[g], "nt")
            inv_cnt_ext = 1.0 / jnp.minimum(row_ext + 1, w).astype(F32)
            e = _anticausal_window_sum(dd * inv_cnt_ext, w)
            dz_ref[:, cols] = (e[:R] - dd[:R]).astype(BF)
        cw0, cw1, cw2 = cw_ref[0:1, :], cw_ref[1:2, :], cw_ref[2:3, :]
        uc = jnp.concatenate([ch_ref[...] * xh_ref[...] * keep_prev, c_ref[...] * x_ref[...]], axis=0)
        uc1 = pltpu.roll(uc, 1, 0)[CONV_HALO:]
        uc2 = pltpu.roll(uc, 2, 0)[CONV_HALO:]
        uc0 = uc[CONV_HALO:]
        yc = cw2 * uc0 + cw0 * uc2 + cw1 * uc1
        dycv = dyc_ref[...]
        dz_ref[:, 512:1024] = (dycv * yc).astype(BF)
        dv_ext = jnp.concatenate([dycv * b_ref[...], dycn_ref[...] * bn_ref[...] * keep_next], axis=0)
        n_ext = R + CONV_HALO
        duc = (cw2 * dv_ext + cw1 * pltpu.roll(dv_ext, n_ext - 1, 0) + cw0 * pltpu.roll(dv_ext, n_ext - 2, 0))[:R]
        dv = dv_ext[:R]
        dcw_ref[0:1, :] += jnp.sum(dv * uc2, axis=0, keepdims=True)
        dcw_ref[1:2, :] += jnp.sum(dv * uc1, axis=0, keepdims=True)
        dcw_ref[2:3, :] += jnp.sum(dv * uc0, axis=0, keepdims=True)
        dz_ref[:, 1024:1536] = (duc * x_ref[...]).astype(BF)
        dz_ref[:, 1536:2048] = (duc * c_ref[...]).astype(BF)

    def main(cb):
        return pl.BlockSpec((R, 512), lambda i: (i, cb))

    def prev(cb, halo, per):
        return pl.BlockSpec((halo, 512), lambda i: (jnp.maximum(i * per - 1, 0), cb))

    def nxt(cb, halo, per):
        return pl.BlockSpec((halo, 512), lambda i: (jnp.minimum((i + 1) * per, T // halo - 1), cb))

    full = lambda a: pl.BlockSpec(a.shape, lambda i: (0,) * a.ndim)
    return _pallas_call(
        body, name=name, grid=(nsteps,),
        in_specs=[main(0), prev(0, POOL_HALO, PH), main(1), nxt(1, CONV_HALO, CH), main(2), prev(2, CONV_HALO, CH),
                  main(3), prev(3, CONV_HALO, CH), main(0), nxt(0, POOL_HALO, PH), main(0), nxt(0, CONV_HALO, CH),
                  full(pmix_b), full(pscale), full(convw), ANY],
        out_specs=[pl.BlockSpec((R, 2048), lambda i: (i, 0)), pl.BlockSpec((512, 128), lambda i: (0, 0)),
                   pl.BlockSpec((1, 512), lambda i: (0, 0)), pl.BlockSpec((8, 512), lambda i: (0, 0))],
        out_shape=[jax.ShapeDtypeStruct(dz.shape, BF), jax.ShapeDtypeStruct((512, 128), F32),
                   jax.ShapeDtypeStruct((1, 512), F32), jax.ShapeDtypeStruct((8, 512), F32)],
        input_output_aliases={15: 0}, compiler_params=_params(("arbitrary",)),
    )(z, z, z, z, z, z, z, z, dyp, dyp, dyc, dyc, pmix_b, pscale, convw, dz)


def _head_sums(v):
    row = lax.broadcasted_iota(jnp.int32, (LANES, LANES), 0) < HEAD_DIM
    col = lax.broadcasted_iota(jnp.int32, (LANES, LANES), 1) < HEAD_DIM
    same_head = jnp.where(jnp.logical_xor(row, col), 0.0, 1.0).astype(BF)
    hi = v.astype(BF)
    lo = (v - hi.astype(F32)).astype(BF)
    return _dot(hi, same_head) + _dot(lo, same_head)


def _head_norm(x, g2, ma):
    r = lax.rsqrt(_head_sums(x * x) / HEAD_DIM + EPS)
    return x * r, r


def _head_norm_bwd(dy, xhat, r, g2, ma):
    dxh = dy * g2
    return r * (dxh - xhat * (_head_sums(dxh * xhat) / HEAD_DIM))


def _attn_masks(other_block_exists):
    lane = lax.broadcasted_iota(jnp.int32, (2 * ATTN_BLOCK, ATTN_BLOCK), 1)
    qi = lax.broadcasted_iota(jnp.int32, (2 * ATTN_BLOCK, ATTN_BLOCK), 0) & (ATTN_BLOCK - 1)
    never = (1 - other_block_exists.astype(jnp.int32)) * (2 * ATTN_BLOCK)
    return lane[:ATTN_BLOCK] < HEAD_DIM, lane <= qi, lane >= qi + never


def _stack_heads(x, ma):
    return jnp.concatenate([jnp.where(ma, x, 0.0), jnp.where(ma, 0.0, x)], axis=0)


def _unstack_heads(y, ma):
    return jnp.where(ma, y[:ATTN_BLOCK], y[ATTN_BLOCK:])


def _stack_cols(tile, ma):
    return jnp.concatenate([tile[:, 0:1], tile[:, HEAD_DIM:HEAD_DIM + 1]], axis=0)


QKV_TILES = (OFF_GATE - OFF_Q) // LANES
KIND_TILES = QKV_TILES // 3


def _qk_norm(z, gains, name):
    T = z.shape[0]
    tm = min(512, T)

    def body(x_ref, g_ref, o_ref):
        ma = lax.broadcasted_iota(jnp.int32, (tm, LANES), 1) < HEAD_DIM
        for tile in range(QKV_TILES):
            v = x_ref[:, LANES * tile:LANES * (tile + 1)]
            if tile < 2 * KIND_TILES:
                g = g_ref[0:1, :] if tile < KIND_TILES else g_ref[1:2, :]
                v = _head_norm(v, g, ma)[0] * g
            o_ref[tile] = v

    return _pallas_call(
        body, name=name, grid=(T // tm,),
        in_specs=[pl.BlockSpec((pl.Element(tm), pl.Element(OFF_GATE - OFF_Q)), lambda i: (i * tm, OFF_Q)),
                  pl.BlockSpec((8, LANES), lambda i: (0, 0))],
        out_specs=pl.BlockSpec((QKV_TILES, tm, LANES), lambda i: (0, i, 0)),
        out_shape=jax.ShapeDtypeStruct((QKV_TILES, T, LANES), F32), compiler_params=_params(("parallel",)),
    )(z, gains)


ATTN_STEP_ROWS = 2048
ATTN_UNROLL = 4


def _attn_steps(T):
    assert ATTN_STEP_ROWS == ATTN_BLOCK * max(ATTN_DILATIONS) and T % ATTN_STEP_ROWS == 0
    return T // ATTN_STEP_ROWS


def _attn_rows(jj, r, sub, d):
    start = jj * sub + r
    if d == 1:
        return pl.ds(pl.multiple_of(start, ATTN_BLOCK), ATTN_BLOCK)
    return pl.ds(start, ATTN_BLOCK, stride=d)


def _pick(flag, a, b):
    return jnp.where(jnp.full(a.shape, flag.astype(jnp.int32)) > 0, a, b)


def _attn_fwd(qkv, name):
    T = qkv.shape[1]
    nbig = _attn_steps(T)
    scale = HEAD_DIM ** -0.5

    def body(q_ref, kc_ref, kp_ref, vc_ref, vp_ref, o_ref, lse_ref):
        jb = pl.program_id(1)
        for gi, d in enumerate(ATTN_DILATIONS):
            pl.when(pl.program_id(0) == gi)(functools.partial(group, d, jb, q_ref, kc_ref, kp_ref, vc_ref, vp_ref,
                                                              o_ref, lse_ref))

    def group(d, jb, q_ref, kc_ref, kp_ref, vc_ref, vp_ref, o_ref, lse_ref):
        sub, m = ATTN_BLOCK * d, ATTN_STEP_ROWS // (ATTN_BLOCK * d)

        def step(s, carry):
            jj, r = s // d, s % d
            here, before = _attn_rows(jj, r, sub, d), _attn_rows(jnp.maximum(jj - 1, 0), r, sub, d)
            edge = _attn_rows(m - 1, r, sub, d)
            first = jj == 0
            ma, mask_c, mask_p = _attn_masks(jb * m + jj > 0)
            qs = _stack_heads(q_ref[here, :], ma).astype(BF)
            kcb = kc_ref[here, :].astype(BF)
            kpb = _pick(first, kp_ref[edge, :], kc_ref[before, :]).astype(BF)
            vcb = vc_ref[here, :].astype(BF)
            vpb = _pick(first, vp_ref[edge, :], vc_ref[before, :]).astype(BF)
            s_c = jnp.where(mask_c, _dot(qs, kcb, "nt") * scale, MASK_VALUE)
            s_p = jnp.where(mask_p, _dot(qs, kpb, "nt") * scale, MASK_VALUE)
            mx = jnp.maximum(jnp.max(s_c, axis=-1, keepdims=True), jnp.max(s_p, axis=-1, keepdims=True))
            p_c = jnp.exp(s_c - mx)
            p_p = jnp.exp(s_p - mx)
            den = jnp.sum(p_c, axis=-1, keepdims=True) + jnp.sum(p_p, axis=-1, keepdims=True)
            o = (_dot(p_c.astype(BF), vcb) + _dot(p_p.astype(BF), vpb)) / den
            o_ref[here, :] = _unstack_heads(o, ma)
            lse_ref[here, :] = _unstack_heads(jnp.broadcast_to(mx + jnp.log(den), o.shape), ma)
            return carry

        lax.fori_loop(0, m * d, step, 0, unroll=ATTN_UNROLL)

    def cur(kind):
        return pl.BlockSpec((None, ATTN_STEP_ROWS, LANES), lambda g, j, t: (KIND_TILES * kind + 2 * g + t, j, 0))

    def prv(kind):
        return pl.BlockSpec((None, ATTN_STEP_ROWS, LANES),
                            lambda g, j, t: (KIND_TILES * kind + 2 * g + t, jnp.maximum(j - 1, 0), 0))

    out = pl.BlockSpec((ATTN_STEP_ROWS, LANES), lambda g, j, t: (j, 2 * g + t))
    width = 2 * LANES * len(ATTN_DILATIONS)
    return _pallas_call(
        body, name=name, grid=(len(ATTN_DILATIONS), nbig, 2), in_specs=[cur(0), cur(1), prv(1), cur(2), prv(2)],
        out_specs=[out, out], out_shape=[jax.ShapeDtypeStruct((T, width), F32)] * 2,
        compiler_params=_params(("parallel", "parallel", "parallel")),
    )(qkv, qkv, qkv, qkv, qkv)


def _attn_bwd(z, qkv, do, c, lse, gains, name, after=None):
    T = z.shape[0]
    nbig = _attn_steps(T)
    scale = HEAD_DIM ** -0.5
    extra = [] if after is None else [after]

    def body(*refs):
        g, jb = pl.program_id(0), pl.program_id(1)
        dgq_ref, dgk_ref = refs[len(refs) - 5], refs[len(refs) - 4]

        @pl.when((g == 0) & (jb == 0) & (pl.program_id(2) == 0))
        def _():
            dgq_ref[...] = jnp.zeros_like(dgq_ref)
            dgk_ref[...] = jnp.zeros_like(dgk_ref)

        for gi, d in enumerate(ATTN_DILATIONS):
            pl.when(g == gi)(functools.partial(group, d, jb, *refs))

    def group(d, jb, qr_ref, kr_ref, vc_ref, vp_ref, qn_ref, qnn_ref, kn_ref, knp_ref, do_ref, don_ref, c_ref, cn_ref,
              lse_ref, lsen_ref, g_ref, *rest):
        dq_ref, dk_ref, dv_ref, dgq_ref, dgk_ref, sq_ref, sk_ref, sv_ref = rest[len(extra):]
        sub, m = ATTN_BLOCK * d, ATTN_STEP_ROWS // (ATTN_BLOCK * d)
        nb = T // sub
        gq, gk = g_ref[0:1, :], g_ref[1:2, :]

        def step(s, carry):
            jj, r = s // d, s % d
            here = _attn_rows(jj, r, sub, d)
            before = _attn_rows(jnp.maximum(jj - 1, 0), r, sub, d)
            behind = _attn_rows(jnp.minimum(jj + 1, m - 1), r, sub, d)
            edge_before, edge_behind = _attn_rows(m - 1, r, sub, d), _attn_rows(0, r, sub, d)
            first, last = jj == 0, jj == m - 1
            block = jb * m + jj
            ma, mask_c, mask_p = _attn_masks(block > 0)
            mask_n = _attn_masks(block < nb - 1)[2]
            qhat, rq = _head_norm(qr_ref[here, :], gq, ma)
            qn = qn_ref[here, :]
            qn_next = _pick(last, qnn_ref[edge_behind, :], qn_ref[behind, :])
            khat, rk = _head_norm(kr_ref[here, :], gk, ma)
            kcb = kn_ref[here, :].astype(BF)
            kpb = _pick(first, knp_ref[edge_before, :], kn_ref[before, :]).astype(BF)
            vcb = vc_ref[here, :].astype(BF)
            vpb = _pick(first, vp_ref[edge_before, :], vc_ref[before, :]).astype(BF)
            do_t, don_t = do_ref[here, :], _pick(last, don_ref[edge_behind, :], do_ref[behind, :])
            c_t, cn_t = c_ref[here, :], _pick(last, cn_ref[edge_behind, :], c_ref[behind, :])
            lse_t, lsen_t = lse_ref[here, :], _pick(last, lsen_ref[edge_behind, :], lse_ref[behind, :])
            qs, dos = _stack_heads(qn, ma).astype(BF), _stack_heads(do_t, ma).astype(BF)
            lse_s, c_s = _stack_cols(lse_t, ma), _stack_cols(c_t, ma)
            s_c = jnp.where(mask_c, _dot(qs, kcb, "nt") * scale, MASK_VALUE)
            s_p = jnp.where(mask_p, _dot(qs, kpb, "nt") * scale, MASK_VALUE)
            p_c = jnp.exp(s_c - lse_s)
            p_p = jnp.exp(s_p - lse_s)
            ds_c = ((p_c * (_dot(dos, vcb, "nt") + c_s)) * scale).astype(BF)
            ds_p = ((p_p * (_dot(dos, vpb, "nt") + c_s)) * scale).astype(BF)
            dq_t = _unstack_heads(_dot(ds_c, kcb) + _dot(ds_p, kpb), ma)
            qs_n, dos_n = _stack_heads(qn_next, ma).astype(BF), _stack_heads(don_t, ma).astype(BF)
            s_n = jnp.where(mask_n, _dot(qs_n, kcb, "nt") * scale, MASK_VALUE)
            p_n = jnp.exp(s_n - _stack_cols(lsen_t, ma))
            ds_n = ((p_n * (_dot(dos_n, vcb, "nt") + _stack_cols(cn_t, ma))) * scale).astype(BF)
            dv_t = _dot(p_c.astype(BF), dos, "tn") + _dot(p_n.astype(BF), dos_n, "tn")
            dk_t = _dot(ds_c, qs, "tn") + _dot(ds_n, qs_n, "tn")
            sq_ref[here, :] = _head_norm_bwd(dq_t, qhat, rq, gq, ma)
            sk_ref[here, :] = _head_norm_bwd(dk_t, khat, rk, gk, ma)
            sv_ref[here, :] = dv_t
            dgq_ref[...] += jnp.sum(dq_t * qhat, axis=0, keepdims=True)
            dgk_ref[...] += jnp.sum(dk_t * khat, axis=0, keepdims=True)
            return carry

        lax.fori_loop(0, m * d, step, 0, unroll=ATTN_UNROLL)
        dq_ref[...] = sq_ref[...].astype(BF)
        dk_ref[...] = sk_ref[...].astype(BF)
        dv_ref[...] = sv_ref[...].astype(BF)

    rows = ATTN_STEP_ROWS

    def raw(col0):
        return pl.BlockSpec((rows, LANES), lambda g, j, t: (j, col0 + 2 * g + t))

    def cur(kind):
        return pl.BlockSpec((None, rows, LANES), lambda g, j, t: (KIND_TILES * kind + 2 * g + t, j, 0))

    def prv(kind):
        return pl.BlockSpec((None, rows, LANES), lambda g, j, t: (KIND_TILES * kind + 2 * g + t, jnp.maximum(j - 1, 0), 0))

    def nxt(kind):
        return pl.BlockSpec((None, rows, LANES),
                            lambda g, j, t: (KIND_TILES * kind + 2 * g + t, jnp.minimum(j + 1, nbig - 1), 0))

    own = pl.BlockSpec((rows, LANES), lambda g, j, t: (j, 2 * g + t))
    own_next = pl.BlockSpec((rows, LANES), lambda g, j, t: (jnp.minimum(j + 1, nbig - 1), 2 * g + t))
    vec = pl.BlockSpec((1, LANES), lambda g, j, t: (0, 0))
    width = 2 * LANES * len(ATTN_DILATIONS)
    return _pallas_call(
        body, name=name, grid=(len(ATTN_DILATIONS), nbig, 2),
        in_specs=[raw(OFF_Q // LANES), raw(OFF_K // LANES), cur(2), prv(2), cur(0), nxt(0), cur(1), prv(1), own, own_next,
                  own, own_next, own, own_next, pl.BlockSpec((8, LANES), lambda g, j, t: (0, 0))] + [ANY] * len(extra),
        out_specs=[own, own, own, vec, vec],
        out_shape=[jax.ShapeDtypeStruct((T, width), BF)] * 3 + [jax.ShapeDtypeStruct((1, LANES), F32)] * 2,
        scratch_shapes=[pltpu.VMEM((rows, LANES), F32)] * 3,
        compiler_params=_params(("arbitrary", "arbitrary", "arbitrary")),
    )(z, z, qkv, qkv, qkv, qkv, qkv, qkv, do, do, c, c, lse, lse, gains, *extra)


MERGE_ROWS = 256
GATE_TILE = 256


def _group_mix(o_refs, lse_refs):
    lses = [r[...] for r in lse_refs]
    m = jnp.maximum(jnp.maximum(lses[0], lses[1]), lses[2])
    es = [jnp.exp(l - m) for l in lses]
    den = es[0] + es[1] + es[2]
    ws = [e / den for e in es]
    y = ws[0] * o_refs[0][...] + ws[1] * o_refs[1][...] + ws[2] * o_refs[2][...]
    return ws, y


def _sigmoid(v):
    return 1.0 / (1.0 + jnp.exp(-v))


def _merge_specs(T, z, bgate, gpu, gco, gau):
    tm = min(MERGE_ROWS, T)
    row = lambda w: pl.BlockSpec((tm, w), lambda i: (i, 0))
    gate0 = OFF_GATE // GATE_TILE
    gates = [pl.BlockSpec((tm, GATE_TILE), functools.partial(lambda i, cb: (i, cb), cb=gate0 + n))
             for n in range(3 * N_CHIPS)]
    full = lambda a: pl.BlockSpec(a.shape, lambda i: (0,) * a.ndim)
    by_group = [pl.BlockSpec((tm, 256), functools.partial(lambda i, g: (i, g), g=g)) for g in range(3)]
    specs = [row(512), row(512)] + by_group * 2 + gates + [full(bgate), full(gpu), full(gco), full(gau)]
    return tm, row, specs


def _merge_fwd(yp, yc, o3, lse3, z, bgate, gpu, gco, gau, name):
    T = yp.shape[0]
    tm, row, specs = _merge_specs(T, z, bgate, gpu, gco, gau)

    def body(*refs):
        yp_ref, yc_ref = refs[0], refs[1]
        o_refs, lse_refs = refs[2:5], refs[5:8]
        zg = refs[8:20]
        b_ref, gpu_ref, gco_ref, gau_ref, out_ref = refs[20:25]
        yab = _group_mix(o_refs, lse_refs)[1].astype(BF)
        ys = (yp_ref[...], yc_ref[...], yab)
        ups = (gpu_ref, gco_ref, gau_ref)
        for n in range(N_CHIPS):
            acc = None
            for b in range(3):
                gcol = slice(1024 * b + GATE_TILE * n, 1024 * b + GATE_TILE * (n + 1))
                gate = _sigmoid(zg[N_CHIPS * b + n][...] + b_ref[:, gcol])
                term = gate * _dot(ys[b], ups[b][n])
                acc = term if acc is None else acc + term
            out_ref[:, GATE_TILE * n:GATE_TILE * (n + 1)] = acc.astype(BF)

    return _pallas_call(
        body, name=name, grid=(T // tm,), in_specs=specs, out_specs=row(1024),
        out_shape=jax.ShapeDtypeStruct((T, 1024), BF), compiler_params=_params(("parallel",)),
    )(yp, yc, *([o3] * 3), *([lse3] * 3), *([z] * 12), bgate, gpu, gco, gau)


def _merge_bwd(dm, yp, yc, o3, lse3, z, bgate, gpu, gco, gau, name):
    T = yp.shape[0]
    tm, row, specs = _merge_specs(T, z, bgate, gpu, gco, gau)
    nsteps = T // tm

    def body(*refs):
        dm_ref, yp_ref, yc_ref = refs[0:3]
        o_refs, lse_refs = refs[3:6], refs[6:9]
        zg = refs[9:21]
        b_ref, gpu_ref, gco_ref, gau_ref = refs[21:25]
        dzg_ref, dyp_ref, dyc_ref = refs[25:28]
        do_ref, c_ref = refs[28:30]
        dgpu_ref, dgco_ref, dgau_ref, dbg_ref = refs[30:34]
        accs = refs[34:37]
        i = pl.program_id(0)

        @pl.when(i == 0)
        def _():
            for a in accs:
                a[...] = jnp.zeros_like(a)
            dbg_ref[...] = jnp.zeros_like(dbg_ref)

        ws, y = _group_mix(o_refs, lse_refs)
        ys = (yp_ref[...], yc_ref[...], y.astype(BF))
        ups = (gpu_ref, gco_ref, gau_ref)
        dys = [None, None, None]
        for n in range(N_CHIPS):
            dmn = dm_ref[:, GATE_TILE * n:GATE_TILE * (n + 1)]
            for b in range(3):
                gcol = slice(1024 * b + GATE_TILE * n, 1024 * b + GATE_TILE * (n + 1))
                gate = _sigmoid(zg[N_CHIPS * b + n][...] + b_ref[:, gcol])
                up = _dot(ys[b], ups[b][n])
                dzg = (dmn * up) * (gate * (1.0 - gate))
                dzg_ref[:, gcol] = dzg.astype(BF)
                dbg_ref[:, gcol] += jnp.sum(dzg, axis=0, keepdims=True)
                dup = (dmn * gate).astype(BF)
                accs[b][n] += _dot(ys[b], dup, "tn")
                dyb = _dot(dup, ups[b][n], "nt")
                dys[b] = dyb if dys[b] is None else dys[b] + dyb
        dyp_ref[...] = dys[0]
        dyc_ref[...] = dys[1]
        dya = dys[2]
        lane = lax.broadcasted_iota(jnp.int32, dya.shape, 1) // HEAD_DIM
        pr = dya * y
        rho = jnp.zeros_like(pr)
        for h in range(256 // HEAD_DIM):
            hm = lane == h
            rho = jnp.where(hm, jnp.sum(jnp.where(hm, pr, 0.0), axis=-1, keepdims=True), rho)
        for g in range(3):
            do_ref[:, 256 * g:256 * (g + 1)] = ws[g] * dya
            c_ref[:, 256 * g:256 * (g + 1)] = -(ws[g] * rho)

        @pl.when(i == nsteps - 1)
        def _():
            dgpu_ref[...] = accs[0][...].astype(BF)
            dgco_ref[...] = accs[1][...].astype(BF)
            dgau_ref[...] = accs[2][...].astype(BF)

    full = lambda a: pl.BlockSpec(a.shape, lambda i: (0,) * a.ndim)
    dz_gate = pl.BlockSpec((pl.Element(tm), pl.Element(3072)), lambda i: (i * tm, OFF_GATE))
    out_specs = ([dz_gate, row(512), row(512)] + [row(768)] * 2 + [full(gpu), full(gco), full(gau)]
                 + [pl.BlockSpec((1, 3072), lambda i: (0, 0))])
    out_shape = ([jax.ShapeDtypeStruct(z.shape, BF)] + [jax.ShapeDtypeStruct((T, 512), F32)] * 2
                 + [jax.ShapeDtypeStruct((T, 768), F32)] * 2
                 + [jax.ShapeDtypeStruct(g.shape, BF) for g in (gpu, gco, gau)]
                 + [jax.ShapeDtypeStruct((1, 3072), F32)])
    return _pallas_call(
        body, name=name, grid=(nsteps,), in_specs=[row(1024)] + specs, out_specs=out_specs, out_shape=out_shape,
        scratch_shapes=[pltpu.VMEM(g.shape, F32) for g in (gpu, gco, gau)],
        compiler_params=_params(("arbitrary",)),
    )(dm, yp, yc, *([o3] * 3), *([lse3] * 3), *([z] * 12), bgate, gpu, gco, gau)


def _layer_fwd(x, w, tag, after=None, soon=None, late=None, last=None, target=None, hb=None, next_gain=None):
    if hb is None:
        hb = _rms_fwd(x, w["norm_mix"], f"rms_mix_{tag}", after=after)
    if soon is not None:
        w = dict(w, **soon(hb))
    z = _mm(hb, w["w_in"], "nt", f"in_proj_{tag}", tm=512, tn=3712, tk=1024, n_outer=True, after=w.get("started"))
    yp, yc = _poolconv_fwd(z, w["pool_mix"], w["pool_scale"], w["conv_w"], f"poolconv_{tag}")
    qkv = _qk_norm(z, w["qk_gain"], f"qk_norm_{tag}")
    o3, lse3 = _attn_fwd(qkv, f"attn_{tag}")
    if late is not None:
        w = dict(w, **late(lse3))
    merged = _merge_fwd(yp, yc, o3, lse3, z, w["b_gate"], w["w_pool_up"], w["w_conv_out"], w["w_attn_up"],
                        f"merge_{tag}")
    x1, h2b = _mm(merged, w["w_o"], "nn", f"out_proj_{tag}", tm=1024, tn=1024, tk=1024, res=x, vec=w["norm_mlp"],
                  epi="rms_next", after=w.get("crossing"))
    if last is not None:
        w = dict(w, **last(x1))
    rb = _mm(h2b, w["w_ff1"], "nn", f"ff1_{tag}", tm=1024, tn=1024, tk=1024, out_dtype=BF, epi="relu2", n_outer=True,
             b_shards=True)
    if target is not None:
        x2 = _mm(rb, w["w_ff2"], "nn", f"ff2_{tag}", tm=512, tn=1024, tk=4096, res=x1, aux=target, epi="loss")
    elif next_gain is not None:
        x2 = _mm(rb, w["w_ff2"], "nn", f"ff2_{tag}", tm=512, tn=1024, tk=4096, res=x1, vec=next_gain, epi="rms_next")
    else:
        x2 = _mm(rb, w["w_ff2"], "nn", f"ff2_{tag}", tm=512, tn=1024, tk=4096, res=x1)
    saved = dict(x=x, hb=hb, z=z, yp=yp, yc=yc, qkv=qkv, o3=o3, lse3=lse3, merged=merged, x1=x1, h2b=h2b, rb=rb)
    return x2, saved, w


def _layer_bwd(dx2, w, s, tag, after=None, head=None, mid=None, past=None, tail=None):
    g = {}
    dab = _mm(dx2, w["w_ff2"], "nt", f"d_ff2_act_{tag}", tm=1024, tn=1024, tk=1024, out_dtype=BF, aux=s["rb"],
              epi="drelu2", after=after)
    front = head(dab) if head is not None else None
    g["w_ff2"] = _mm(s["rb"], dx2, "tn", f"d_ff2_w_{tag}", tm=1024, tn=1024, tk=2048, out_dtype=BF, after=front)
    g["w_ff1"] = _mm(s["h2b"], dab, "tn", f"d_ff1_w_{tag}", tm=1024, tn=1024, tk=2048, out_dtype=BF, out_shards=True)
    dx1, g["norm_mlp"] = _mm(dab, w["w_ff1"], "nt", f"d_ff1_act_{tag}", tm=1024, tn=1024, tk=1024, b_shards=True,
                             res=dx2, aux=s["x1"], vec=w["norm_mlp"], epi="rms_bwd")
    dm = _mm(dx1, w["w_o"], "nt", f"d_out_act_{tag}", tm=1024, tn=1024, tk=1024)
    g["w_o"] = _mm(s["merged"], dx1, "tn", f"d_out_w_{tag}", tm=1024, tn=1024, tk=1024, out_dtype=BF)
    (dz, dyp, dyc, do3, c3, g["w_pool_up"], g["w_conv_out"], g["w_attn_up"],
     g["b_gate"]) = _merge_bwd(dm, s["yp"], s["yc"], s["o3"], s["lse3"], s["z"], w["b_gate"], w["w_pool_up"],
                               w["w_conv_out"], w["w_attn_up"], f"d_merge_{tag}")
    behind = mid(g) if mid is not None else None
    dzq, dzk, dzv, dgq, dgk = _attn_bwd(s["z"], s["qkv"], do3, c3, s["lse3"], w["qk_gain"], f"d_attn_{tag}",
                                        after=behind)
    g["q_gain"] = dgq[:, :HEAD_DIM] + dgq[:, HEAD_DIM:]
    g["k_gain"] = dgk[:, :HEAD_DIM] + dgk[:, HEAD_DIM:]
    for off, piece in ((OFF_Q, dzq), (OFF_K, dzk), (OFF_V, dzv)):
        dz = lax.dynamic_update_slice(dz, piece, (0, off))
    dz, g["pool_mix"], g["pool_scale"], g["conv_w"] = _poolconv_bwd(
        s["z"], dyp, dyc, w["pool_mix"], w["pool_scale"], w["conv_w"], dz, f"d_poolconv_{tag}")
    g["w_in"] = _mm(s["hb"], dz, "tn", f"d_in_w_{tag}", tm=512, tn=3712, tk=1024, out_dtype=BF,
                    after=past(dzq) if past is not None else None)
    dh = _mm(dz, w["w_in"], "nn", f"d_in_act_{tag}", tm=1024, tn=1024, tk=3712,
             after=tail(g) if tail is not None else None)
    dx, g["norm_mix"] = _rms_bwd(dh, s["x"], w["norm_mix"], dx1, f"d_rms_mix_{tag}")
    return dx, g


def _position():
    x, y, c = lax.axis_index("x"), lax.axis_index("y"), lax.axis_index("c")
    chips = [(1 - x, y), (x, 1 - y), (1 - x, 1 - y)]
    return x, y, c, 2 * x + y, chips, [2 * cx + cy for cx, cy in chips]


def _remote(src, dst, ssem, rsem, dev):
    return pltpu.make_async_remote_copy(src_ref=src, dst_ref=dst, send_sem=ssem, recv_sem=rsem, device_id=dev,
                                        device_id_type=MESH_ID)


def _halves(a):
    return a.reshape(a.shape[0], 2, a.shape[1] // 2, a.shape[2])


SEM = pl.BlockSpec(memory_space=pltpu.SEMAPHORE)
TOKEN = jax.ShapeDtypeStruct((8, LANES), F32)
TOKEN_SPEC = pl.BlockSpec(memory_space=pltpu.VMEM)


def _split_params():
    return pltpu.CompilerParams(has_side_effects=pltpu.SideEffectType.DATAFLOW_SIDE_EFFECTING)


def _ici_plan(x, y, c, q, chips, qs):
    return [((q, c), (qs[j], c), (chips[j][0], chips[j][1], c)) for j in range(3)]


def _sibling_plan(x, y, c, q, chips, qs):
    return [((qs[j], c), (qs[j], 1 - c), (x, y, 1 - c)) for j in range(3)]


def _gather_start(bufs, name, after):
    return _copies_start([_halves(b) for b in bufs], name, after, _ici_plan)


def _copies_start(views, name, after, plan):
    n = len(views)

    def body(*refs):
        first_sem = n + 1
        ssem, rsem = refs[first_sem:first_sem + ns], refs[first_sem + ns:first_sem + 2 * ns]
        outs, token = refs[first_sem + 2 * ns:first_sem + 2 * ns + n], refs[first_sem + 2 * ns + n]
        for k in range(n):
            for j, (sent, _, peer) in enumerate(plan(*_position())):
                mine = outs[k].at[sent]
                _remote(mine, mine, ssem[3 * k + j], rsem[3 * k + j], peer).start()
        token[...] = jnp.zeros_like(token)

    ns = 3 * n
    outs = _pallas_call(
        body, name=name, in_specs=[ANY] * (n + 1), out_specs=[SEM] * (2 * ns) + [ANY] * n + [TOKEN_SPEC],
        out_shape=[pltpu.SemaphoreType.DMA(())] * (2 * ns) + [jax.ShapeDtypeStruct(v.shape, v.dtype) for v in views]
        + [TOKEN],
        input_output_aliases={k: k + 2 * ns for k in range(n)}, compiler_params=_split_params(),
    )(*views, after)
    return list(outs[:ns]), list(outs[ns:2 * ns]), list(outs[2 * ns:2 * ns + n]), outs[2 * ns + n]


def _copies_wait(ssem, rsem, views, after, name, plan):
    n = len(views)
    ns = len(ssem)

    def wait_body(*refs):
        ssem_ref, rsem_ref = refs[n:n + ns], refs[n + ns:n + 2 * ns]
        outs = refs[n + 2 * ns + 1:]
        for k in range(n):
            for j, (sent, landing, peer) in enumerate(plan(*_position())):
                cp = _remote(outs[k].at[sent], outs[k].at[landing], ssem_ref[3 * k + j], rsem_ref[3 * k + j], peer)
                cp.wait_send()
                cp.wait_recv()

    return _pallas_call(
        wait_body, name=name, in_specs=[ANY] * n + [SEM] * (2 * ns) + [ANY], out_specs=[ANY] * n,
        out_shape=[jax.ShapeDtypeStruct(v.shape, v.dtype) for v in views],
        input_output_aliases={k: k for k in range(n)}, compiler_params=_split_params(),
    )(*views, *ssem, *rsem, after)


def _gather_finish(ssem, rsem, views, after, name_wait, name_forward, shapes):
    return _gather_forward(_copies_wait(ssem, rsem, views, after, name_wait, _ici_plan), name_forward, shapes)


def _gather_forward(landed, name_forward, shapes):
    n = len(landed)
    views = landed

    def forward_body(*refs):
        outs = refs[n:2 * n]
        fssem, frsem = refs[2 * n:]
        x, y, c, q, chips, qs = _position()
        sib = (x, y, 1 - c)
        sent = []
        for k in range(n):
            for j in range(3):
                slot = outs[k].at[qs[j], c]
                cp = _remote(slot, slot, fssem.at[k, j], frsem.at[k, j], sib)
                cp.start()
                sent.append(cp)
        for k in range(n):
            for j in range(3):
                slot = outs[k].at[qs[j], 1 - c]
                _remote(slot, slot, fssem.at[k, j], frsem.at[k, j], sib).wait_recv()
        for cp in sent:
            cp.wait_send()

    outs = _pallas_call(
        forward_body, name=name_forward, in_specs=[ANY] * n, out_specs=[ANY] * n,
        out_shape=[jax.ShapeDtypeStruct(v.shape, v.dtype) for v in views],
        input_output_aliases={k: k for k in range(n)}, scratch_shapes=[pltpu.SemaphoreType.DMA((n, 3))] * 2,
    )(*landed)
    return [o.reshape(s) for o, s in zip(outs, shapes)]


def _chip_exchange_start(parts, name):
    n = len(parts)

    def body(*refs):
        ssem, rsem = refs[n:n + ns], refs[n + ns:n + 2 * ns]
        base = n + 2 * ns
        srcs, outs, token = refs[base:base + n], refs[base + n:base + 2 * n], refs[base + 2 * n]
        x, y, c, q, chips, qs = _position()
        for k in range(n):
            for j, chip in enumerate(chips):
                _remote(srcs[k].at[qs[j]], outs[k].at[j], ssem[3 * k + j], rsem[3 * k + j],
                        (chip[0], chip[1], c)).start()
        token[...] = jnp.zeros_like(token)

    ns = 3 * n
    outs = _pallas_call(
        body, name=name, in_specs=[ANY] * n, out_specs=[SEM] * (2 * ns) + [ANY] * (2 * n) + [TOKEN_SPEC],
        out_shape=[pltpu.SemaphoreType.DMA(())] * (2 * ns) + [jax.ShapeDtypeStruct(a.shape, a.dtype) for a in parts]
        + [jax.ShapeDtypeStruct((3,) + a.shape[1:], a.dtype) for a in parts] + [TOKEN],
        input_output_aliases={k: k + 2 * ns for k in range(n)}, compiler_params=_split_params(),
    )(*parts)
    b = 2 * ns
    return list(outs[:ns]), list(outs[ns:b]), list(outs[b:b + n]), list(outs[b + n:b + 2 * n]), outs[b + 2 * n]


def _chip_exchange_wait(ssem, rsem, parts, landing, after, name):
    n = len(parts)
    ns = len(ssem)

    def body(*refs):
        ssem_ref, rsem_ref = refs[2 * n:2 * n + ns], refs[2 * n + ns:2 * n + 2 * ns]
        base = 2 * n + 2 * ns + 1
        srcs, outs = refs[base:base + n], refs[base + n:]
        x, y, c, q, chips, qs = _position()
        for k in range(n):
            for j, chip in enumerate(chips):
                cp = _remote(srcs[k].at[qs[j]], outs[k].at[j], ssem_ref[3 * k + j], rsem_ref[3 * k + j],
                             (chip[0], chip[1], c))
                cp.wait_send()
                cp.wait_recv()

    outs = _pallas_call(
        body, name=name, in_specs=[ANY] * (2 * n) + [SEM] * (2 * ns) + [ANY], out_specs=[ANY] * (2 * n),
        out_shape=[jax.ShapeDtypeStruct(a.shape, a.dtype) for a in list(parts) + list(landing)],
        input_output_aliases={k: k for k in range(2 * n)}, compiler_params=_split_params(),
    )(*parts, *landing, *ssem, *rsem, after)
    return list(outs[:n]), list(outs[n:])


def _pair_swap(views, name):
    n = len(views)

    def body(*refs):
        ins, outs = refs[:n], refs[n:2 * n]
        ssem, rsem = refs[2 * n:]
        x, y, c, _, _, _ = _position()
        cps = [_remote(ins[k].at[pl.ds(0, N_CHIPS), 1 - c], outs[k], ssem.at[k], rsem.at[k], (x, y, 1 - c))
               for k in range(n)]
        for cp in cps:
            cp.start()
        for cp in cps:
            cp.wait()

    return _pallas_call(
        body, name=name, in_specs=[ANY] * n, out_specs=[ANY] * n,
        out_shape=[jax.ShapeDtypeStruct((v.shape[0],) + v.shape[2:], v.dtype) for v in views],
        scratch_shapes=[pltpu.SemaphoreType.DMA((n,))] * 2,
    )(*views)


def _pair_swap_start(views, name):
    n = len(views)

    def body(*refs):
        ins, ssem, rsem, outs, token = refs[:n], refs[n:2 * n], refs[2 * n:3 * n], refs[3 * n:4 * n], refs[4 * n]
        x, y, c, _, _, _ = _position()
        for k in range(n):
            _remote(ins[k].at[pl.ds(0, N_CHIPS), 1 - c], outs[k], ssem[k], rsem[k], (x, y, 1 - c)).start()
        token[...] = jnp.zeros_like(token)

    outs = _pallas_call(
        body, name=name, in_specs=[ANY] * n, out_specs=[SEM] * (2 * n) + [ANY] * n + [TOKEN_SPEC],
        out_shape=[pltpu.SemaphoreType.DMA(())] * (2 * n)
        + [jax.ShapeDtypeStruct((v.shape[0],) + v.shape[2:], v.dtype) for v in views] + [TOKEN],
        compiler_params=_split_params(),
    )(*views)
    return list(outs[:n]), list(outs[n:2 * n]), list(outs[2 * n:3 * n]), outs[3 * n]


def _pair_swap_wait(views, ssem, rsem, landing, after, name):
    n = len(views)

    def body(*refs):
        ins, ssem_ref, rsem_ref = refs[:n], refs[n:2 * n], refs[2 * n:3 * n]
        outs = refs[4 * n + 1:]
        x, y, c, _, _, _ = _position()
        for k in range(n):
            cp = _remote(ins[k].at[pl.ds(0, N_CHIPS), 1 - c], outs[k], ssem_ref[k], rsem_ref[k], (x, y, 1 - c))
            cp.wait_send()
            cp.wait_recv()

    return _pallas_call(
        body, name=name, in_specs=[ANY] * n + [SEM] * (2 * n) + [ANY] * (n + 1), out_specs=[ANY] * n,
        out_shape=[jax.ShapeDtypeStruct(v.shape, v.dtype) for v in landing],
        input_output_aliases={3 * n + k: k for k in range(n)}, compiler_params=_split_params(),
    )(*views, *ssem, *rsem, *landing, after)


def _pair_send(arrays, name):
    n = len(arrays)

    def body(*refs):
        ins, outs = refs[:n], refs[n:2 * n]
        ssem, rsem = refs[2 * n:]
        x, y, c, _, _, _ = _position()
        cps = [_remote(ins[k], outs[k], ssem.at[k], rsem.at[k], (x, y, 1 - c)) for k in range(n)]
        for cp in cps:
            cp.start()
        for cp in cps:
            cp.wait()

    return _pallas_call(
        body, name=name, in_specs=[ANY] * n, out_specs=[ANY] * n,
        out_shape=[jax.ShapeDtypeStruct(a.shape, a.dtype) for a in arrays],
        scratch_shapes=[pltpu.SemaphoreType.DMA((n,))] * 2,
    )(*arrays)


def _all_to_all_small(part):
    P = part.shape[0]

    def body(in_ref, out_ref, lsem, ssem, rsem):
        x, y, c = lax.axis_index("x"), lax.axis_index("y"), lax.axis_index("c")
        me = 4 * x + 2 * y + c
        flips = [(fx, fy, fc) for fx in (0, 1) for fy in (0, 1) for fc in (0, 1)][1:]
        peers = [((x + fx) % 2, (y + fy) % 2, (c + fc) % 2) for fx, fy, fc in flips]
        loc = pltpu.make_async_copy(in_ref, out_ref.at[me], lsem)
        loc.start()
        cps = [_remote(in_ref, out_ref.at[me], ssem.at[j], rsem.at[j], peer) for j, peer in enumerate(peers)]
        for cp in cps:
            cp.start()
        for j, (px, py, pc) in enumerate(peers):
            _remote(in_ref, out_ref.at[4 * px + 2 * py + pc], ssem.at[j], rsem.at[j], peers[j]).wait_recv()
        for cp in cps:
            cp.wait_send()
        loc.wait()

    return _pallas_call(
        body, name="small_exchange", in_specs=[ANY], out_specs=ANY,
        out_shape=jax.ShapeDtypeStruct((8, P, LANES), F32),
        scratch_shapes=[pltpu.SemaphoreType.DMA(())] + [pltpu.SemaphoreType.DMA((7,))] * 2,
    )(part)


def _small_peers():
    x, y, c = lax.axis_index("x"), lax.axis_index("y"), lax.axis_index("c")
    flips = [(fx, fy, fc) for fx in (0, 1) for fy in (0, 1) for fc in (0, 1)][1:]
    peers = [((x + fx) % 2, (y + fy) % 2, (c + fc) % 2) for fx, fy, fc in flips]
    return 4 * x + 2 * y + c, peers


def _all_to_all_small_start(part, name):
    P = part.shape[0]
    me = 4 * lax.axis_index("x") + 2 * lax.axis_index("y") + lax.axis_index("c")
    landing = lax.dynamic_update_slice(jnp.zeros((8, P, LANES), F32), part[None], (me, 0, 0))

    def body(*refs):
        sems, src, land, token = refs[2:16], refs[16], refs[17], refs[18]
        me_, peers = _small_peers()
        for j, peer in enumerate(peers):
            _remote(src, land.at[me_], sems[j], sems[7 + j], peer).start()
        token[...] = jnp.zeros_like(token)

    outs = _pallas_call(
        body, name=name, in_specs=[ANY, ANY], out_specs=[SEM] * 14 + [ANY, ANY, TOKEN_SPEC],
        out_shape=[pltpu.SemaphoreType.DMA(())] * 14 + [jax.ShapeDtypeStruct(part.shape, F32),
                                                       jax.ShapeDtypeStruct((8, P, LANES), F32), TOKEN],
        input_output_aliases={0: 14, 1: 15}, compiler_params=_split_params(),
    )(part, landing)
    return list(outs[:7]), list(outs[7:14]), outs[14], outs[15], outs[16]


def _all_to_all_small_wait(ssem, rsem, part, landing, after, name):
    def body(*refs):
        sems, src, land = refs[2:16], refs[17], refs[18]
        _, peers = _small_peers()
        for j, (px, py, pc) in enumerate(peers):
            cp = _remote(src, land.at[4 * px + 2 * py + pc], sems[j], sems[7 + j], peers[j])
            cp.wait_send()
            cp.wait_recv()

    return _pallas_call(
        body, name=name, in_specs=[ANY, ANY] + [SEM] * 14 + [ANY], out_specs=[ANY, ANY],
        out_shape=[jax.ShapeDtypeStruct(part.shape, F32), jax.ShapeDtypeStruct(landing.shape, F32)],
        input_output_aliases={0: 0, 1: 1}, compiler_params=_split_params(),
    )(part, landing, *ssem, *rsem, after)[1]


def _row_tile(rows, width, n_arrays):
    t = rows
    while t % 2 == 0 and t > 8 and 2 * n_arrays * t * width * 4 > VMEM_LIMIT // 2:
        t //= 2
    return t


def _chip():
    return 2 * lax.axis_index("x") + lax.axis_index("y")


def _core():
    return lax.axis_index("c")


def _cast_place(w3, layer, name):
    _, r, c = w3.shape
    tr = _row_tile(r, c, 2)

    def body(w_ref, o_ref):
        o_ref[...] = w_ref[...].astype(BF)

    return _pallas_call(
        body, name=name, grid=(r // tr,), in_specs=[pl.BlockSpec((None, tr, c), lambda i: (layer, i, 0))],
        out_specs=pl.BlockSpec((None, tr, c), lambda i: (_chip(), i, 0)),
        out_shape=jax.ShapeDtypeStruct((N_CHIPS, r, c), BF), compiler_params=_params(("parallel",)),
    )(w3)


def _pair_sum(views, recvs, name):
    n = len(views)

    def body(*refs):
        for g_ref, r_ref, o_ref in zip(refs[:n], refs[n:2 * n], refs[2 * n:]):
            o_ref[...] = (g_ref[...].astype(F32) + r_ref[...].astype(F32)).astype(BF)

    own = [pl.BlockSpec((None, None) + v.shape[2:], lambda p: (p, _core(), 0, 0)) for v in views]
    blk = [pl.BlockSpec((None,) + r.shape[1:], lambda p: (p, 0, 0)) for r in recvs]
    return _pallas_call(
        body, name=name, grid=(N_CHIPS,), in_specs=own + blk, out_specs=blk,
        out_shape=[jax.ShapeDtypeStruct(r.shape, BF) for r in recvs], compiler_params=_params(("parallel",)),
    )(*views, *recvs)


CHIP_SUM_STEPS = 2


def _chip_sum(parts, recvs, name):
    n = len(parts)

    def body(*refs):
        for p_ref, r_ref, o_ref in zip(refs[:n], refs[n:2 * n], refs[2 * n:]):
            acc = p_ref[...].astype(F32)
            for j in range(3):
                acc = acc + r_ref[j].astype(F32)
            o_ref[...] = acc

    rows = [p.shape[1] // CHIP_SUM_STEPS for p in parts]
    return _pallas_call(
        body, name=name, grid=(CHIP_SUM_STEPS,),
        in_specs=[pl.BlockSpec((None, t, p.shape[2]), lambda i: (_chip(), i, 0)) for p, t in zip(parts, rows)]
        + [pl.BlockSpec((3, t, p.shape[2]), lambda i: (0, i, 0)) for p, t in zip(parts, rows)],
        out_specs=[pl.BlockSpec((t, p.shape[2]), lambda i: (i, 0)) for p, t in zip(parts, rows)],
        out_shape=[jax.ShapeDtypeStruct(p.shape[1:], F32) for p in parts], compiler_params=_params(("parallel",)),
    )(*parts, *recvs)


def _sum_slices(a, name):
    n, rows, width = a.shape
    tr = _row_tile(rows, width, n + 1)

    def body(a_ref, o_ref):
        acc = a_ref[0].astype(F32)
        for i in range(1, n):
            acc = acc + a_ref[i].astype(F32)
        o_ref[...] = acc

    return _pallas_call(
        body, name=name, grid=(rows // tr,), in_specs=[pl.BlockSpec((n, tr, width), lambda i: (0, i, 0))],
        out_specs=pl.BlockSpec((tr, width), lambda i: (i, 0)), out_shape=jax.ShapeDtypeStruct((rows, width), F32),
        compiler_params=_params(("parallel",)),
    )(a)


def _adamw_update(w, g, m, v):
    nm = ADAM_B1 * m + (1.0 - ADAM_B1) * g
    nv = ADAM_B2 * v + (1.0 - ADAM_B2) * (g * g)
    m_hat = nm / (1.0 - ADAM_B1 ** ADAM_STEP)
    v_hat = nv / (1.0 - ADAM_B2 ** ADAM_STEP)
    return -ADAM_LR * (m_hat / (jnp.sqrt(v_hat) + ADAM_EPS) + ADAM_WD * w), nm, nv


def _adamw(ws, gs, ms, vs, name):
    n = len(ws)

    def body(*refs):
        for k in range(n):
            w_ref, g_ref, m_ref, v_ref = (refs[s * n + k] for s in range(4))
            d_ref, nm_ref, nv_ref = (refs[(4 + s) * n + k] for s in range(3))
            d_ref[...], nm_ref[...], nv_ref[...] = _adamw_update(w_ref[...], g_ref[...], m_ref[...], v_ref[...])

    whole = [pl.BlockSpec(w.shape, lambda i: (0, 0)) for w in ws]
    outs = _pallas_call(
        body, name=name, grid=(1,), in_specs=whole * 4, out_specs=whole * 3,
        out_shape=[jax.ShapeDtypeStruct(w.shape, F32) for _ in range(3) for w in ws],
        compiler_params=_params(("arbitrary",)),
    )(*ws, *gs, *ms, *vs)
    return [[outs[s * n + k] for s in range(3)] for k in range(n)]


ADAMW_STEPS = 4


def _adamw_halves(ws, ms, vs, mine, other, name):
    n = len(ws)
    depth = ws[0].shape[0]
    assert depth == 2
    halves = [(w.shape[1] // 2, w.shape[2]) for w in ws]
    tiles = [hr // ADAMW_STEPS for hr, _ in halves]
    kinds = ((0, True), (0, False), (1, True), (1, False))

    def active(l, h, layer, own):
        mine_half = h == _core()
        return (l == layer) & (mine_half if own else jnp.logical_not(mine_half))

    def body(*refs):
        l, h = pl.program_id(0), pl.program_id(1)
        flags = [active(l, h, layer, own) for layer, own in kinds]
        for k in range(n):
            w_ref, m_ref, v_ref = refs[k], refs[n + k], refs[2 * n + k]
            g_refs = [refs[(3 + s) * n + k] for s in range(4)]
            go_ref, d_ref, nm_ref, nv_ref = (refs[(7 + s) * n + k] for s in range(4))
            for flag, g_ref in zip(flags, g_refs):
                @pl.when(flag)
                def _():
                    gv = g_ref[...]
                    go_ref[...] = gv
                    d_ref[...], nm_ref[...], nv_ref[...] = _adamw_update(w_ref[...], gv, m_ref[...], v_ref[...])

    def blk(k):
        return pl.BlockSpec((None, None, tiles[k], halves[k][1]), lambda l, h, i: (l, h, i, 0))

    def gspec(k, layer, own):
        return pl.BlockSpec((tiles[k], halves[k][1]), lambda l, h, i: (jnp.where(active(l, h, layer, own), i, 0), 0))

    def view(a, k):
        return a.reshape(depth, 2, halves[k][0], halves[k][1])

    blks = [blk(k) for k in range(n)]
    sources = [[(mine if own else other)[layer][k] for k in range(n)] for layer, own in kinds]
    outs = _pallas_call(
        body, name=name, grid=(depth, 2, ADAMW_STEPS),
        in_specs=blks * 3 + [gspec(k, layer, own) for layer, own in kinds for k in range(n)], out_specs=blks * 4,
        out_shape=[jax.ShapeDtypeStruct((depth, 2) + halves[k], F32) for _ in range(4) for k in range(n)],
        compiler_params=_params(("parallel", "parallel", "parallel")),
    )(*[view(a, k) for group in (ws, ms, vs) for k, a in enumerate(group)], *[g for src in sources for g in src])
    return [[outs[s * n + k].reshape(ws[k].shape) for s in range(4)] for k in range(n)]


BIG = ("w_in", "w_pool_up", "w_conv_out", "w_attn_up", "w_o", "w_ff1", "w_ff2")
SMALL = ("norm_mix", "b_gate", "pool_mix", "pool_scale", "conv_w", "q_gain", "k_gain", "norm_mlp")
ORDER = ("norm_mix", "w_in", "b_gate", "pool_mix", "pool_scale", "conv_w", "q_gain", "k_gain", "w_pool_up",
         "w_conv_out", "w_attn_up", "w_o", "norm_mlp", "w_ff1", "w_ff2")
COLUMN_SHARDED = ("w_pool_up", "w_conv_out", "w_attn_up", "w_ff1")


def _matrix_weights(gathered):
    w = {}
    for name, g4 in gathered.items():
        if name in COLUMN_SHARDED:
            w[name] = g4
        else:
            w[name] = g4.reshape(N_CHIPS * g4.shape[1], g4.shape[2])
    return w


def _small_weights(l, small):
    w = {}
    w["norm_mix"] = small["norm_mix"][l][None]
    w["norm_mlp"] = small["norm_mlp"][l][None]
    w["b_gate"] = small["b_gate"][l][None]
    w["pool_mix"] = small["pool_mix"][l].astype(BF)
    w["pool_scale"] = small["pool_scale"][l][None]
    w["conv_w"] = jnp.pad(small["conv_w_full"][l], ((0, 5), (0, 0)))
    w["qk_gain"] = jnp.pad(jnp.stack([jnp.tile(small["q_gain"][l], 2), jnp.tile(small["k_gain"][l], 2)]), ((0, 6), (0, 0)))
    return w


def _to_chip_major(name, g):
    if name == "w_in":
        return g.T.reshape(N_CHIPS, g.shape[1] // N_CHIPS, g.shape[0])
    if name in COLUMN_SHARDED:
        return g
    return g.reshape(N_CHIPS, g.shape[0] // N_CHIPS, g.shape[1])


def _pad8(a):
    a = a.reshape(-1)
    return jnp.pad(a, (0, (-a.size) % (8 * LANES))).reshape(-1, LANES)


def kernel(x, norm_mix, w_in, b_gate, pool_mix, pool_scale, conv_w, q_gain, k_gain, w_pool_up, w_conv_out, w_attn_up, w_o, norm_mlp, w_ff1, w_ff2, loss_target, m_norm_mix, m_w_in, m_b_gate, m_pool_mix, m_pool_scale, m_conv_w, m_q_gain, m_k_gain, m_w_pool_up, m_w_conv_out, m_w_attn_up, m_w_o, m_norm_mlp, m_w_ff1, m_w_ff2, v_norm_mix, v_w_in, v_b_gate, v_pool_mix, v_pool_scale, v_conv_w, v_q_gain, v_k_gain, v_w_pool_up, v_w_conv_out, v_w_attn_up, v_w_o, v_norm_mlp, v_w_ff1, v_w_ff2):
    weights = dict(norm_mix=norm_mix, w_in=w_in, b_gate=b_gate, pool_mix=pool_mix, pool_scale=pool_scale, conv_w=conv_w,
                   q_gain=q_gain, k_gain=k_gain, w_pool_up=w_pool_up, w_conv_out=w_conv_out, w_attn_up=w_attn_up,
                   w_o=w_o, norm_mlp=norm_mlp, w_ff1=w_ff1, w_ff2=w_ff2)
    moms = dict(norm_mix=m_norm_mix, w_in=m_w_in, b_gate=m_b_gate, pool_mix=m_pool_mix, pool_scale=m_pool_scale,
                conv_w=m_conv_w, q_gain=m_q_gain, k_gain=m_k_gain, w_pool_up=m_w_pool_up, w_conv_out=m_w_conv_out,
                w_attn_up=m_w_attn_up, w_o=m_w_o, norm_mlp=m_norm_mlp, w_ff1=m_w_ff1, w_ff2=m_w_ff2)
    vels = dict(norm_mix=v_norm_mix, w_in=v_w_in, b_gate=v_b_gate, pool_mix=v_pool_mix, pool_scale=v_pool_scale,
                conv_w=v_conv_w, q_gain=v_q_gain, k_gain=v_k_gain, w_pool_up=v_w_pool_up, w_conv_out=v_w_conv_out,
                w_attn_up=v_w_attn_up, w_o=v_w_o, norm_mlp=v_norm_mlp, w_ff1=v_w_ff1, w_ff2=v_w_ff2)
    depth = norm_mix.shape[0]
    q = 2 * lax.axis_index("x") + lax.axis_index("y")
    for group in (weights, moms, vels):
        group["w_in"] = jnp.swapaxes(group["w_in"], 1, 2)

    assert depth == 2, "the second layer's gather hides behind the first layer's forward, and likewise backward"
    first, rest = BIG[:1], BIG[1:]
    cw_all = _all_to_all_small(_pad8(conv_w))
    bufs = [{n: _cast_place(weights[n], 0, f"cast_{n}_l0") for n in first}]
    a_ssem, a_rsem, a_views, a_token = _gather_start([bufs[0][n] for n in first], "gather_start_l0_in", cw_all)
    bufs[0].update({n: _cast_place(weights[n], 0, f"cast_{n}_l0") for n in rest})
    bufs += [{n: _cast_place(weights[n], l, f"cast_{n}_l{l}") for n in BIG} for l in range(1, depth)]
    b_ssem, b_rsem, b_views, b_token = _gather_start([bufs[0][n] for n in rest], "gather_start_l0_rest", a_token)
    g_ssem, g_rsem, g_views, g_token = _gather_start([bufs[1][n] for n in BIG], "gather_start_l1", b_token)
    conv_w_full = jnp.concatenate(
        [cw_all[2 * p].reshape(-1)[:conv_w.size].reshape(conv_w.shape) for p in range(N_CHIPS)], axis=-1)
    small = dict(weights)
    small["conv_w_full"] = conv_w_full

    def soon_weights(t):
        got = _gather_finish(a_ssem, a_rsem, a_views, t, "gather_wait_l0_in", "gather_forward_l0_in",
                             [bufs[0][n].shape for n in first])
        return _matrix_weights(dict(zip(first, got)))

    mlp = len(rest) - 2
    shapes_0 = [bufs[0][n].shape for n in rest]
    sibling = {}

    def late_weights(t):
        landed = _copies_wait(b_ssem, b_rsem, b_views, t, "gather_wait_l0_rest", _ici_plan)
        got = _gather_forward(landed[:mlp], "gather_forward_l0_rest", shapes_0[:mlp])
        sibling["mlp"] = _copies_start(landed[mlp:], "gather_forward_start_l0_mlp", got[0], _sibling_plan)
        return dict(_matrix_weights(dict(zip(rest[:mlp], got))), crossing=sibling["mlp"][3])

    def last_weights(t):
        f_ssem, f_rsem, f_views, _ = sibling["mlp"]
        got = _copies_wait(f_ssem, f_rsem, f_views, t, "gather_forward_wait_l0_mlp", _sibling_plan)
        return _matrix_weights({n: o.reshape(s) for n, o, s in zip(rest[mlp:], got, shapes_0[mlp:])})

    wl, saved = [None] * depth, [None] * depth
    small_1 = _small_weights(1, small)
    (h, hb_1), saved[0], wl[0] = _layer_fwd(x[0], _small_weights(0, small), "l0", after=g_token, soon=soon_weights,
                                            late=late_weights, last=last_weights, next_gain=small_1["norm_mix"])
    shapes_1 = [bufs[1][n].shape for n in BIG]

    def soon_weights_1(t):
        landed = _copies_wait(g_ssem, g_rsem, g_views, t, "gather_wait_l1", _ici_plan)
        got = _gather_forward(landed[:1], "gather_forward_l1_in", shapes_1[:1])
        sibling["rest"] = _copies_start(landed[1:], "gather_forward_start_l1_rest", got[0], _sibling_plan)
        return dict(_matrix_weights(dict(zip(first, got))), started=sibling["rest"][3])

    def late_weights_1(t):
        f_ssem, f_rsem, f_views, _ = sibling["rest"]
        got = _copies_wait(f_ssem, f_rsem, f_views, t, "gather_forward_wait_l1_rest", _sibling_plan)
        return _matrix_weights({n: o.reshape(s) for n, o, s in zip(rest, got, shapes_1[1:])})

    (dh, loss_row), saved[1], wl[1] = _layer_fwd(h, small_1, "l1", soon=soon_weights_1, late=late_weights_1,
                                                 target=loss_target[0], hb=hb_1)

    def pair_stage(names, g, tag):
        views = [_halves(_to_chip_major(n, g[n])) for n in names]
        from_sibling = _pair_swap(views, f"grad_pair_swap_{tag}")
        return _pair_sum(views, from_sibling, f"pair_sum_{tag}")

    mine, other = [{}, {}], [{}, {}]

    def finish(names, l, started, after, tag):
        ssem, rsem, parts, landing, _ = started
        parts, arrived = _chip_exchange_wait(ssem, rsem, parts, landing, after, f"grad_chip_exchange_wait_{tag}")
        got = _chip_sum(parts, arrived, f"chip_sum_{tag}")
        mine[l].update(zip(names, got))
        other[l].update(zip(names, _pair_send(got, f"grad_pair_send_{tag}")))

    def small_pieces(g):
        return [_pad8(g[n][:3] if n == "conv_w" else g[n]) for n in SMALL]

    def start_small(l):
        pieces = small_pieces(grads[l]) + ([_pad8(loss_row)] if l == depth - 1 else [])
        return _all_to_all_small_start(jnp.concatenate(pieces, axis=0), f"small_grad_exchange_start_l{l}")

    grads, early, small = [None] * depth, {}, [None] * depth
    dh, grads[1] = _layer_bwd(dh, wl[1], saved[1], "l1")
    views_1 = [_halves(_to_chip_major(n, grads[1][n])) for n in BIG]
    s_ssem, s_rsem, s_landing, s_token = _pair_swap_start(views_1, "grad_pair_swap_start_l1")
    small[1] = start_small(1)
    later = {}

    def start_second(t):
        from_sibling = _pair_swap_wait(views_1, s_ssem, s_rsem, s_landing, t, "grad_pair_swap_wait_l1")
        later["l1"] = _chip_exchange_start(_pair_sum(views_1, from_sibling, "pair_sum_l1"), "grad_chip_exchange_start_l1")
        return later["l1"][4]

    def start_rest(g):
        views = [_halves(_to_chip_major(n, g[n])) for n in rest]
        early["swap"] = (views,) + tuple(_pair_swap_start(views, "grad_pair_swap_start_l0_rest"))
        return early["swap"][4]

    def sum_rest(t):
        views, ssem, rsem, landing, _ = early["swap"]
        from_sibling = _pair_swap_wait(views, ssem, rsem, landing, t, "grad_pair_swap_wait_l0_rest")
        early["rest"] = _chip_exchange_start(_pair_sum(views, from_sibling, "pair_sum_l0_rest"),
                                             "grad_chip_exchange_start_l0_rest")
        return early["rest"][4]

    def start_last(g):
        early["in"] = _chip_exchange_start(pair_stage(first, g, "l0_in"), "grad_chip_exchange_start_l0_in")
        return early["in"][4]

    dh, grads[0] = _layer_bwd(dh, wl[0], saved[0], "l0", after=[s_token, small[1][4]], head=start_second,
                              mid=start_rest, past=sum_rest, tail=start_last)
    second = later["l1"]
    small[0] = start_small(0)
    started = small[0][4]
    finish(BIG, 1, second, started, "l1")
    finish(rest, 0, early["rest"], started, "l0_rest")
    full = {}

    deltas, new_m, new_v = {}, {}, {}

    def update_matrices(names, tag):
        results = _adamw_halves(
            [weights[n] for n in names], [moms[n] for n in names], [vels[n] for n in names],
            [[mine[l][n] for n in names] for l in range(depth)], [[other[l][n] for n in names] for l in range(depth)],
            f"adamw_{tag}")
        for n, (g_, d_, m_, v_) in zip(names, results):
            full[n], deltas[n], new_m[n], new_v[n] = g_, d_, m_, v_

    update_matrices(rest, "rest")
    finish(first, 0, early["in"], deltas[rest[-1]], "l0_in")
    update_matrices(first, "in")
    summed = []
    for l in range(depth):
        ssem, rsem, part, landing, _ = small[l]
        summed.append(_sum_slices(_all_to_all_small_wait(ssem, rsem, part, landing, deltas[first[-1]],
                                                         f"small_grad_exchange_wait_l{l}"), f"small_sum_l{l}"))
    row = 0
    for n, piece in zip(SMALL, small_pieces(grads[0])):
        size = (weights[n].size if n != "conv_w" else depth * 3 * 512) // depth
        flat = jnp.stack([s[row:row + piece.shape[0]].reshape(-1)[:size] for s in summed])
        row += piece.shape[0]
        if n == "conv_w":
            full[n] = lax.dynamic_slice_in_dim(flat.reshape(depth, 3, 512), q * conv_w.shape[2], conv_w.shape[2], axis=2)
        else:
            full[n] = flat.reshape(weights[n].shape)
    loss = summed[depth - 1][row, 0]
    two_d = {n: (-1, weights[n].shape[-1]) if n not in ("conv_w", "q_gain", "k_gain") else (1, -1) for n in SMALL}
    results = _adamw(*[[group[n].reshape(two_d[n]) for n in SMALL] for group in (weights, full, moms, vels)],
                     "adamw_small")
    for n, (d2, m2, v2) in zip(SMALL, results):
        shape = weights[n].shape
        deltas[n], new_m[n], new_v[n] = d2.reshape(shape), m2.reshape(shape), v2.reshape(shape)
        full[n] = full[n].reshape(shape)
    for group in (full, deltas, new_m, new_v):
        group["w_in"] = jnp.swapaxes(group["w_in"], 1, 2)
    return (loss, dh[None], *[full[n] for n in ORDER], *[deltas[n] for n in ORDER], *[new_m[n] for n in ORDER],
            *[new_v[n] for n in ORDER])
```
